```python
import math
import jax, jax.numpy as jnp
from jax import lax
import numpy as np

D_MODEL = 1024
BATCH = 16
SEQ = 2048
DEPTH = 2

HEAD_DIM = 64
N_HEADS_SB = 4
N_HEADS_DIL = 4
N_HEADS_FOX = 4
W_SB = N_HEADS_SB * HEAD_DIM
W_DIL = N_HEADS_DIL * HEAD_DIM
W_FOX = N_HEADS_FOX * HEAD_DIM
CONV_WIDTH = D_MODEL - W_SB - W_DIL - W_FOX
N_CONV_GROUPS = 4
CONV_K = 3
DIL_PATTERNS = ((128, 1), (512, 4), (2048, 16))
BLOCK_Q = 128
REL_BUCKETS = 32
REL_MAX_DIST = 2048
D_FF = ((8 * D_MODEL + 3 * 256 - 1) // (3 * 256)) * 256
PROJ_WIDTH = 3 * W_SB + 3 * W_DIL + 3 * W_FOX + 3 * CONV_WIDTH + N_HEADS_FOX
DN_ALPHA = (2 * DEPTH) ** 0.25
DN_BETA = (8 * DEPTH) ** -0.25
LN_EPS = 1e-5

kernel_name = "hybrid_parallel_sb_dilated_fox_shortconv"


def split_heads(t, n_heads):
    b, s, _ = t.shape
    return t.reshape(b, s, n_heads, HEAD_DIM).transpose(0, 2, 1, 3)


def merge_heads(t):
    b, h, s, d = t.shape
    return t.transpose(0, 2, 1, 3).reshape(b, s, h * d)


def layer_norm(x, g, b):
    xf = x.astype(jnp.float32)
    mu = jnp.mean(xf, axis=-1, keepdims=True)
    var = jnp.mean(jnp.square(xf - mu), axis=-1, keepdims=True)
    y = (xf - mu) * lax.rsqrt(var + LN_EPS) * g.astype(jnp.float32) + b.astype(jnp.float32)
    return y.astype(x.dtype)


def stick_breaking_attention(q, k, v):
    b, h, s, d = q.shape
    nb = s // BLOCK_Q
    scale = d ** -0.5
    qb = q.reshape(b, h, nb, BLOCK_Q, d).transpose(2, 0, 1, 3, 4)
    k_pos = jnp.arange(s)

    def block(args):
        q_blk, i = args
        q_pos = i * BLOCK_Q + jnp.arange(BLOCK_Q)
        z = jnp.einsum('bhqd,bhkd->bhqk', q_blk, k).astype(jnp.float32) * scale
        strict = k_pos[None, :] < q_pos[:, None]
        log_beta = jax.nn.log_sigmoid(z)
        log_rest = jnp.where(strict, jax.nn.log_sigmoid(-z), 0.0)
        tail = lax.cumsum(log_rest, axis=3, reverse=True) - log_rest
        a = jnp.where(strict, jnp.exp(log_beta + tail), 0.0)
        return jnp.einsum('bhqk,bhkd->bhqd', a.astype(v.dtype), v)

    out = lax.map(block, (qb, jnp.arange(nb)))
    return out.transpose(1, 2, 0, 3, 4).reshape(b, h, s, d)


def forgetting_attention(q, k, v, log_f):
    b, h, s, d = q.shape
    nb = s // BLOCK_Q
    scale = d ** -0.5
    cum = jnp.cumsum(log_f, axis=-1)
    qb = q.reshape(b, h, nb, BLOCK_Q, d).transpose(2, 0, 1, 3, 4)
    cb = cum.reshape(b, h, nb, BLOCK_Q).transpose(2, 0, 1, 3)
    k_pos = jnp.arange(s)

    def block(args):
        q_blk, c_blk, i = args
        q_pos = i * BLOCK_Q + jnp.arange(BLOCK_Q)
        z = (jnp.einsum('bhqd,bhkd->bhqk', q_blk, k).astype(jnp.float32) * scale
             + (c_blk[..., :, None] - cum[..., None, :]))
        z = jnp.where(k_pos[None, :] <= q_pos[:, None], z, -jnp.inf)
        p = jax.nn.softmax(z, axis=-1)
        return jnp.einsum('bhqk,bhkd->bhqd', p.astype(v.dtype), v)

    out = lax.map(block, (qb, cb, jnp.arange(nb)))
    return out.transpose(1, 2, 0, 3, 4).reshape(b, h, s, d)


def t5_bucket(dist):
    max_exact = REL_BUCKETS // 2
    nf = jnp.maximum(dist, 1).astype(jnp.float32)
    large = max_exact + (jnp.log(nf / max_exact) / math.log(REL_MAX_DIST / max_exact)
                         * (REL_BUCKETS - max_exact)).astype(jnp.int32)
    large = jnp.minimum(large, REL_BUCKETS - 1)
    return jnp.where(dist < max_exact, dist, large)


def to_residue_blocks(t, dil, n_blk):
    b, h, s, e = t.shape
    sub_len = s // dil
    t = t.reshape(b, h, sub_len, dil, e).transpose(0, 1, 3, 2, 4)
    t = jnp.pad(t, ((0, 0), (0, 0), (0, 0), (0, n_blk * BLOCK_Q - sub_len), (0, 0)))
    return t.reshape(b, h, dil, n_blk, BLOCK_Q, e)


def from_residue_blocks(t, s):
    b, h, dil, n_blk, bq, e = t.shape
    t = t.reshape(b, h, dil, n_blk * bq, e)[:, :, :, : s // dil]
    return t.transpose(0, 1, 3, 2, 4).reshape(b, h, s, e)


def with_previous_block(t):
    prev = jnp.pad(t[:, :, :, :-1], ((0, 0), (0, 0), (0, 0), (1, 0), (0, 0), (0, 0)))
    return jnp.concatenate([prev, t], axis=4)


def dilated_window_attention(q, k, v, rel_bias):
    b, h, s, d = q.shape
    scale = d ** -0.5
    outs, maxes, dens = [], [], []
    for window, dil in DIL_PATTERNS:
        n_back = window // dil
        n_blk = -(-(s // dil) // BLOCK_Q)
        qs = to_residue_blocks(q, dil, n_blk)
        kb = with_previous_block(to_residue_blocks(k, dil, n_blk))
        vb = with_previous_block(to_residue_blocks(v, dil, n_blk))
        qi = jnp.arange(BLOCK_Q)[:, None]
        kj = jnp.arange(2 * BLOCK_Q)[None, :]
        sub_dist = qi + BLOCK_Q - kj
        in_band = (sub_dist >= 0) & (sub_dist <= n_back)
        key_valid = (jnp.arange(n_blk)[:, None, None] > 0) | (kj[None] >= BLOCK_Q)
        mask = in_band[None] & key_valid
        bias = rel_bias[t5_bucket(jnp.maximum(sub_dist, 0) * dil)]
        bias = bias.transpose(2, 0, 1).astype(jnp.float32)
        z = (jnp.einsum('bhrnqd,bhrnkd->bhrnqk', qs, kb).astype(jnp.float32) * scale
             + bias[None, :, None, None])
        z = jnp.where(mask[None, None, None], z, -jnp.inf)
        m = jnp.max(z, axis=-1, keepdims=True)
        p = jnp.exp(z - m)
        den = jnp.sum(p, axis=-1, keepdims=True)
        o = jnp.einsum('bhrnqk,bhrnkd->bhrnqd', p.astype(v.dtype), vb).astype(jnp.float32) / den
        outs.append(from_residue_blocks(o, s))
        maxes.append(from_residue_blocks(m, s))
        dens.append(from_residue_blocks(den, s))
    m_all = jnp.stack(maxes)
    w = jnp.stack(dens) * jnp.exp(m_all - jnp.max(m_all, axis=0, keepdims=True))
    out = jnp.sum(w * jnp.stack(outs), axis=0) / jnp.sum(w, axis=0)
    return out.astype(q.dtype)


def short_gated_conv(b_gate, c_gate, h, conv_w):
    u = c_gate * h
    y = lax.conv_general_dilated(
        u, conv_w[:, None, :].astype(u.dtype), window_strides=(1,), padding=[(CONV_K - 1, 0)],
        dimension_numbers=('NWC', 'WIO', 'NWC'), feature_group_count=CONV_WIDTH)
    return b_gate * y


def hybrid_mixer(x, w_in, f_bias, conv_w, w_out, rel_bias):
    proj = x @ w_in
    widths = (W_SB,) * 3 + (W_DIL,) * 3 + (W_FOX,) * 3 + (CONV_WIDTH,) * 3 + (N_HEADS_FOX,)
    points, acc = [], 0
    for wdt in widths[:-1]:
        acc += wdt
        points.append(acc)
    (sb_q, sb_k, sb_v, dl_q, dl_k, dl_v, fx_q, fx_k, fx_v,
     cv_b, cv_c, cv_h, fx_f) = jnp.split(proj, points, axis=-1)
    out_a = stick_breaking_attention(split_heads(sb_q, N_HEADS_SB), split_heads(sb_k, N_HEADS_SB),
                                     split_heads(sb_v, N_HEADS_SB))
    out_b = dilated_window_attention(split_heads(dl_q, N_HEADS_DIL), split_heads(dl_k, N_HEADS_DIL),
                                     split_heads(dl_v, N_HEADS_DIL), rel_bias)
    log_f = jax.nn.log_sigmoid((fx_f + f_bias).astype(jnp.float32)).transpose(0, 2, 1)
    out_c = forgetting_attention(split_heads(fx_q, N_HEADS_FOX), split_heads(fx_k, N_HEADS_FOX),
                                 split_heads(fx_v, N_HEADS_FOX), log_f)
    out_d = short_gated_conv(cv_b, cv_c, cv_h, conv_w)
    mixed = jnp.concatenate([merge_heads(out_a), merge_heads(out_b), merge_heads(out_c),
                             out_d.astype(x.dtype)], axis=-1)
    return mixed @ w_out


def swiglu_ffn(x, w_gate, w_up, w_down):
    return (jax.nn.silu(x @ w_gate) * (x @ w_up)) @ w_down


def _fwd_setup_inputs(seed: int = 0) -> dict:
    key = jax.random.key(seed)
    ks = jax.random.split(key, 13)
    f32 = jnp.float32
    nrm = lambda k, shape, s: jax.random.normal(k, shape, f32) * s
    return {
        "x": jax.random.normal(ks[0], (BATCH, SEQ, D_MODEL), f32),
        "w_in": nrm(ks[1], (DEPTH, D_MODEL, PROJ_WIDTH), D_MODEL ** -0.5),
        "f_bias": 1.0 + nrm(ks[2], (DEPTH, N_HEADS_FOX), 0.1),
        "conv_w": nrm(ks[3], (DEPTH, CONV_K, CONV_WIDTH), CONV_K ** -0.5),
        "w_out": nrm(ks[4], (DEPTH, D_MODEL, D_MODEL), D_MODEL ** -0.5 * DN_BETA),
        "rel_bias": nrm(ks[5], (REL_BUCKETS, N_HEADS_DIL), 0.1),
        "ln1_g": 1.0 + nrm(ks[6], (DEPTH, D_MODEL), 0.02),
        "ln1_b": nrm(ks[7], (DEPTH, D_MODEL), 0.02),
        "w_gate": nrm(ks[8], (DEPTH, D_MODEL, D_FF), D_MODEL ** -0.5),
        "w_up": nrm(ks[9], (DEPTH, D_MODEL, D_FF), D_MODEL ** -0.5),
        "w_down": nrm(ks[10], (DEPTH, D_FF, D_MODEL), D_FF ** -0.5 * DN_BETA),
        "ln2_g": 1.0 + nrm(ks[11], (DEPTH, D_MODEL), 0.02),
        "ln2_b": nrm(ks[12], (DEPTH, D_MODEL), 0.02),
    }


def _fwd_reference(x, w_in, f_bias, conv_w, w_out, rel_bias, ln1_g, ln1_b, w_gate, w_up, w_down,
              ln2_g, ln2_b):
    for layer in range(DEPTH):
        mix = hybrid_mixer(x, w_in[layer], f_bias[layer], conv_w[layer], w_out[layer], rel_bias)
        x = layer_norm(DN_ALPHA * x + mix, ln1_g[layer], ln1_b[layer])
        ffn = swiglu_ffn(x, w_gate[layer], w_up[layer], w_down[layer])
        x = layer_norm(DN_ALPHA * x + ffn, ln2_g[layer], ln2_b[layer])
    return x


import jax as _jax
import jax.numpy as _jnp

TWIN_FORMAT = 'train_step'
FWD_PARAMS = ['x', 'w_in', 'f_bias', 'conv_w', 'w_out', 'rel_bias', 'ln1_g', 'ln1_b', 'w_gate', 'w_up', 'w_down', 'ln2_g', 'ln2_b']
TWIN_WEIGHTS = ['w_in', 'f_bias', 'conv_w', 'w_out', 'rel_bias', 'ln1_g', 'ln1_b', 'w_gate', 'w_up', 'w_down', 'ln2_g', 'ln2_b']
TWIN_DIFF_INPUT = 'x'
TWIN_INPUTS = ['x', 'w_in', 'f_bias', 'conv_w', 'w_out', 'rel_bias', 'ln1_g', 'ln1_b', 'w_gate', 'w_up', 'w_down', 'ln2_g', 'ln2_b', 'loss_target', 'm_w_in', 'm_f_bias', 'm_conv_w', 'm_w_out', 'm_rel_bias', 'm_ln1_g', 'm_ln1_b', 'm_w_gate', 'm_w_up', 'm_w_down', 'm_ln2_g', 'm_ln2_b', 'v_w_in', 'v_f_bias', 'v_conv_w', 'v_w_out', 'v_rel_bias', 'v_ln1_g', 'v_ln1_b', 'v_w_gate', 'v_w_up', 'v_w_down', 'v_ln2_g', 'v_ln2_b']
TWIN_OUTPUTS = ['loss', 'grad_x', 'grad_w_in', 'grad_f_bias', 'grad_conv_w', 'grad_w_out', 'grad_rel_bias', 'grad_ln1_g', 'grad_ln1_b', 'grad_w_gate', 'grad_w_up', 'grad_w_down', 'grad_ln2_g', 'grad_ln2_b', 'delta_w_in', 'delta_f_bias', 'delta_conv_w', 'delta_w_out', 'delta_rel_bias', 'delta_ln1_g', 'delta_ln1_b', 'delta_w_gate', 'delta_w_up', 'delta_w_down', 'delta_ln2_g', 'delta_ln2_b', 'new_m_w_in', 'new_m_f_bias', 'new_m_conv_w', 'new_m_w_out', 'new_m_rel_bias', 'new_m_ln1_g', 'new_m_ln1_b', 'new_m_w_gate', 'new_m_w_up', 'new_m_w_down', 'new_m_ln2_g', 'new_m_ln2_b', 'new_v_w_in', 'new_v_f_bias', 'new_v_conv_w', 'new_v_w_out', 'new_v_rel_bias', 'new_v_ln1_g', 'new_v_ln1_b', 'new_v_w_gate', 'new_v_w_up', 'new_v_w_down', 'new_v_ln2_g', 'new_v_ln2_b']
TWIN_LEAF_KINDS = {'loss': 'loss', 'grad_x': 'grad_x', 'grad_w_in': 'grad_w', 'grad_f_bias': 'grad_w', 'grad_conv_w': 'grad_w', 'grad_w_out': 'grad_w', 'grad_rel_bias': 'grad_w', 'grad_ln1_g': 'grad_w', 'grad_ln1_b': 'grad_w', 'grad_w_gate': 'grad_w', 'grad_w_up': 'grad_w', 'grad_w_down': 'grad_w', 'grad_ln2_g': 'grad_w', 'grad_ln2_b': 'grad_w', 'delta_w_in': 'delta_w', 'delta_f_bias': 'delta_w', 'delta_conv_w': 'delta_w', 'delta_w_out': 'delta_w', 'delta_rel_bias': 'delta_w', 'delta_ln1_g': 'delta_w', 'delta_ln1_b': 'delta_w', 'delta_w_gate': 'delta_w', 'delta_w_up': 'delta_w', 'delta_w_down': 'delta_w', 'delta_ln2_g': 'delta_w', 'delta_ln2_b': 'delta_w', 'new_m_w_in': 'new_m', 'new_m_f_bias': 'new_m', 'new_m_conv_w': 'new_m', 'new_m_w_out': 'new_m', 'new_m_rel_bias': 'new_m', 'new_m_ln1_g': 'new_m', 'new_m_ln1_b': 'new_m', 'new_m_w_gate': 'new_m', 'new_m_w_up': 'new_m', 'new_m_w_down': 'new_m', 'new_m_ln2_g': 'new_m', 'new_m_ln2_b': 'new_m', 'new_v_w_in': 'new_v', 'new_v_f_bias': 'new_v', 'new_v_conv_w': 'new_v', 'new_v_w_out': 'new_v', 'new_v_rel_bias': 'new_v', 'new_v_ln1_g': 'new_v', 'new_v_ln1_b': 'new_v', 'new_v_w_gate': 'new_v', 'new_v_w_up': 'new_v', 'new_v_w_down': 'new_v', 'new_v_ln2_g': 'new_v', 'new_v_ln2_b': 'new_v'}


def _forward(args):
    return _fwd_reference(*[args[k] for k in FWD_PARAMS])


def _output_shape():
    out = _jax.eval_shape(lambda: _forward(_fwd_setup_inputs(0)))
    return out.shape, out.dtype

N_MICROBATCH = 1
ADAM_LR = 0.001
ADAM_B1 = 0.9
ADAM_B2 = 0.999
ADAM_EPS = 1e-08
ADAM_WD = 0.01
ADAM_STEP = 10
PER_EXAMPLE_BATCH_AXIS = {'x': 0, 'loss_target': 0}
SHARED_INPUTS = []
_WEIGHT_DTYPES = {'w_in': _jnp.float32, 'f_bias': _jnp.float32, 'conv_w': _jnp.float32, 'w_out': _jnp.float32, 'rel_bias': _jnp.float32, 'ln1_g': _jnp.float32, 'ln1_b': _jnp.float32, 'w_gate': _jnp.float32, 'w_up': _jnp.float32, 'w_down': _jnp.float32, 'ln2_g': _jnp.float32, 'ln2_b': _jnp.float32}
MOMENT_SCALE = {'w_in': 3.713175e-02, 'f_bias': 1.762538e-01, 'conv_w': 6.698955e-02, 'w_out': 8.318473e-02, 'rel_bias': 1.908093e-02, 'ln1_g': 9.010847e-01, 'ln1_b': 4.209502e-01, 'w_gate': 2.344919e-02, 'w_up': 2.266031e-02, 'w_down': 7.510945e-02, 'ln2_g': 2.261383e+01, 'ln2_b': 7.154090e-01}


def _to_microbatches(a, axis):
    t = _jnp.moveaxis(a, axis, 0)
    t = t.reshape((N_MICROBATCH, t.shape[0] // N_MICROBATCH) + t.shape[1:])
    return _jnp.moveaxis(t, 1, axis + 1)


def setup_inputs(seed: int = 0) -> dict:
    inp = _fwd_setup_inputs(seed)
    key = _jax.random.fold_in(_jax.random.key(seed), 7919)
    shape, _ = _output_shape()
    out = dict(inp)
    out["loss_target"] = _jax.random.normal(_jax.random.fold_in(key, 0), shape, _jnp.float32)
    for i, name in enumerate(TWIN_WEIGHTS):
        w = inp[name].astype(_jnp.float32)
        if MOMENT_SCALE is None:
            s = _jnp.sqrt(_jnp.mean(_jnp.square(w)) + 1e-30)
        else:
            s = MOMENT_SCALE[name]
        km, kv = _jax.random.split(_jax.random.fold_in(key, i + 1))
        out[name] = w
        out["m_" + name] = s * _jax.random.normal(km, w.shape, _jnp.float32)
        out["v_" + name] = (s * s) * _jax.random.uniform(kv, w.shape, _jnp.float32, 0.5, 1.5)
    if N_MICROBATCH > 1:
        for name, axis in PER_EXAMPLE_BATCH_AXIS.items():
            out[name] = _to_microbatches(out[name], axis)
    return {'x': out['x'], 'w_in': out['w_in'], 'f_bias': out['f_bias'], 'conv_w': out['conv_w'], 'w_out': out['w_out'], 'rel_bias': out['rel_bias'], 'ln1_g': out['ln1_g'], 'ln1_b': out['ln1_b'], 'w_gate': out['w_gate'], 'w_up': out['w_up'], 'w_down': out['w_down'], 'ln2_g': out['ln2_g'], 'ln2_b': out['ln2_b'], 'loss_target': out['loss_target'], 'm_w_in': out['m_w_in'], 'm_f_bias': out['m_f_bias'], 'm_conv_w': out['m_conv_w'], 'm_w_out': out['m_w_out'], 'm_rel_bias': out['m_rel_bias'], 'm_ln1_g': out['m_ln1_g'], 'm_ln1_b': out['m_ln1_b'], 'm_w_gate': out['m_w_gate'], 'm_w_up': out['m_w_up'], 'm_w_down': out['m_w_down'], 'm_ln2_g': out['m_ln2_g'], 'm_ln2_b': out['m_ln2_b'], 'v_w_in': out['v_w_in'], 'v_f_bias': out['v_f_bias'], 'v_conv_w': out['v_conv_w'], 'v_w_out': out['v_w_out'], 'v_rel_bias': out['v_rel_bias'], 'v_ln1_g': out['v_ln1_g'], 'v_ln1_b': out['v_ln1_b'], 'v_w_gate': out['v_w_gate'], 'v_w_up': out['v_w_up'], 'v_w_down': out['v_w_down'], 'v_ln2_g': out['v_ln2_g'], 'v_ln2_b': out['v_ln2_b']}


def _loss(weights, diff, rest, loss_target):
    with _jax.named_scope("forward"):
        args = {**rest, TWIN_DIFF_INPUT: diff, **{k: w.astype(_WEIGHT_DTYPES[k]) for k, w in weights.items()}}
        y = _forward(args)
    with _jax.named_scope("loss_head"):
        err = _jnp.square(y.astype(_jnp.float32) - loss_target)
        return 0.5 * _jnp.sum(_jnp.mean(err, axis=-1)) if err.ndim else 0.5 * err


def _adamw(w, g, m, v):
    m = ADAM_B1 * m + (1.0 - ADAM_B1) * g
    v = ADAM_B2 * v + (1.0 - ADAM_B2) * _jnp.square(g)
    m_hat = m / (1.0 - ADAM_B1 ** ADAM_STEP)
    v_hat = v / (1.0 - ADAM_B2 ** ADAM_STEP)
    delta = -ADAM_LR * (m_hat / (_jnp.sqrt(v_hat) + ADAM_EPS) + ADAM_WD * w)
    return delta, m, v


def reference(x, w_in, f_bias, conv_w, w_out, rel_bias, ln1_g, ln1_b, w_gate, w_up, w_down, ln2_g, ln2_b, loss_target, m_w_in, m_f_bias, m_conv_w, m_w_out, m_rel_bias, m_ln1_g, m_ln1_b, m_w_gate, m_w_up, m_w_down, m_ln2_g, m_ln2_b, v_w_in, v_f_bias, v_conv_w, v_w_out, v_rel_bias, v_ln1_g, v_ln1_b, v_w_gate, v_w_up, v_w_down, v_ln2_g, v_ln2_b):
    given = dict(x=x, w_in=w_in, f_bias=f_bias, conv_w=conv_w, w_out=w_out, rel_bias=rel_bias, ln1_g=ln1_g, ln1_b=ln1_b, w_gate=w_gate, w_up=w_up, w_down=w_down, ln2_g=ln2_g, ln2_b=ln2_b, loss_target=loss_target, m_w_in=m_w_in, m_f_bias=m_f_bias, m_conv_w=m_conv_w, m_w_out=m_w_out, m_rel_bias=m_rel_bias, m_ln1_g=m_ln1_g, m_ln1_b=m_ln1_b, m_w_gate=m_w_gate, m_w_up=m_w_up, m_w_down=m_w_down, m_ln2_g=m_ln2_g, m_ln2_b=m_ln2_b, v_w_in=v_w_in, v_f_bias=v_f_bias, v_conv_w=v_conv_w, v_w_out=v_w_out, v_rel_bias=v_rel_bias, v_ln1_g=v_ln1_g, v_ln1_b=v_ln1_b, v_w_gate=v_w_gate, v_w_up=v_w_up, v_w_down=v_w_down, v_ln2_g=v_ln2_g, v_ln2_b=v_ln2_b)
    weights = {n: given[n] for n in TWIN_WEIGHTS}
    shared = {n: given[n] for n in SHARED_INPUTS}
    per_example = {n: given[n] for n in ['x']}
    grad_fn = _jax.value_and_grad(_loss, argnums=(0, 1))

    def one_microbatch(ex, loss_target):
        ex = dict(ex)
        diff = ex.pop(TWIN_DIFF_INPUT)
        return grad_fn(weights, diff, {**shared, **ex}, loss_target)

    if N_MICROBATCH == 1:
        loss, (grad_w, grad_x) = one_microbatch(per_example, given["loss_target"])
    else:
        def body(carry, xs):
            loss_sum, grad_sum = carry
            l_k, (gw_k, gx_k) = one_microbatch(xs[0], xs[1])
            with _jax.named_scope("update"):
                return (loss_sum + l_k, _jax.tree.map(_jnp.add, grad_sum, gw_k)), gx_k

        init = (_jnp.zeros((), _jnp.float32), _jax.tree.map(_jnp.zeros_like, weights))
        (loss, grad_w), grad_x = _jax.lax.scan(body, init, (per_example, given["loss_target"]))
    with _jax.named_scope("update"):
        delta_w, new_m, new_v = {}, {}, {}
        for n in TWIN_WEIGHTS:
            delta_w[n], new_m[n], new_v[n] = _adamw(weights[n], grad_w[n], given["m_" + n], given["v_" + n])
    return (loss, grad_x, *[grad_w[n] for n in TWIN_WEIGHTS], *[delta_w[n] for n in TWIN_WEIGHTS],
            *[new_m[n] for n in TWIN_WEIGHTS], *[new_v[n] for n in TWIN_WEIGHTS])
```

```python
import math

import numpy as np
import jax
import jax.numpy as jnp
from jax import lax
from jax.experimental import pallas as pl
from jax.experimental.pallas import tpu as pltpu

F32 = jnp.float32
BF16 = jnp.bfloat16
MESH = pl.DeviceIdType.MESH

D = 1024
S = 2048
BL = 2
T = BL * S
NH = 4
DFF = 2816
NPROJ = 3076
NPAD = 3200
QKVW = 2304
CONVW = 768
GATEW = 128
PAIRW = 384
BQ = 128
NB = S // BQ
NDEV = 8
ROWS_IN = D // NDEV
ROWS_FF = DFF // NDEV
ALPHA = 4.0 ** 0.25
SCALE = 0.125
NEG = -1e30
LN_EPS = 1e-5
ADAM_LR, ADAM_B1, ADAM_B2, ADAM_EPS, ADAM_WD, ADAM_STEP = 0.001, 0.9, 0.999, 1e-08, 0.01, 10
VMEM_LIMIT = 48 * 1024 * 1024
SMALL_ROWS = 16


def _bucket_thresholds():
    d = np.arange(0, S)
    nf = np.maximum(d, 1).astype(np.float32)
    large = 16 + (np.log(nf / np.float32(16)) / np.float32(math.log(128)) * np.float32(16)).astype(np.int32)
    b = np.where(d < 16, d, np.minimum(large, 31))
    return [int(np.argmax(b >= k)) for k in range(32)]


BUCKET_TH = _bucket_thresholds()


def _cp(sem=None):
    return pltpu.CompilerParams(dimension_semantics=sem, vmem_limit_bytes=VMEM_LIMIT)


def _dot(a, b):
    return lax.dot_general(a, b, (((1,), (0,)), ((), ())), preferred_element_type=F32)


def _dot_nt(a, b):
    return lax.dot_general(a, b, (((1,), (1,)), ((), ())), preferred_element_type=F32)


def _dot_tn(a, b):
    return lax.dot_general(a, b, (((0,), (0,)), ((), ())), preferred_element_type=F32)


def _split2(x):
    hi = x.astype(BF16)
    mid = (x - hi.astype(F32)).astype(BF16)
    return jnp.concatenate([hi, mid], axis=1)


def _split3(x):
    hi = x.astype(BF16)
    r = x - hi.astype(F32)
    mid = r.astype(BF16)
    lo = (r - mid.astype(F32)).astype(BF16)
    return jnp.concatenate([hi, mid, lo], axis=1)


def _log_sigmoid(u):
    return jnp.minimum(u, 0.0) - jnp.log1p(jnp.exp(-jnp.abs(u)))


def _iota(shape, dim):
    return lax.broadcasted_iota(jnp.int32, shape, dim)


def _mm(pairs, *, nt, M, N, tm, tn, out_dtype, name, res=None, res_scale=1.0):
    n = len(pairs)

    def body(*refs):
        acc = None
        for p in range(n):
            a = refs[2 * p][...].astype(BF16)
            b = refs[2 * p + 1][...]
            d = _dot_nt(a, b) if nt else _dot(a, b)
            acc = d if acc is None else acc + d
        if res is not None:
            acc = acc + res_scale * refs[2 * n][...]
        refs[-1][...] = acc.astype(out_dtype)

    ops, specs = [], []
    for a, asp, b, bsp in pairs:
        ops += [a, b]
        specs += [asp, bsp]
    if res is not None:
        ops.append(res)
        specs.append(pl.BlockSpec((tm, tn), lambda i, j: (i, j)))
    return pl.pallas_call(
        body, name=name, grid=(M // tm, N // tn), in_specs=specs,
        out_specs=pl.BlockSpec((tm, tn), lambda i, j: (i, j)),
        out_shape=jax.ShapeDtypeStruct((M, N), out_dtype),
        compiler_params=_cp(("parallel", "parallel")))(*ops)


def _mm_tn(a, b, gbuf, *, Ka, N, tm, tn, tk, a_off, b_off, layer, ooff, name):
    def body(a_ref, b_ref, g_in, o_ref):
        k = pl.program_id(2)
        d = _dot_tn(a_ref[...].astype(BF16), b_ref[...].astype(BF16))

        @pl.when(k == 0)
        def _():
            o_ref[...] = d

        @pl.when(k > 0)
        def _():
            o_ref[...] += d

    return pl.pallas_call(
        body, name=name, grid=(Ka // tm, N // tn, T // tk),
        in_specs=[pl.BlockSpec((tk, tm), lambda i, j, k: (k, a_off + i)),
                  pl.BlockSpec((tk, tn), lambda i, j, k: (k, b_off + j)),
                  pl.BlockSpec(memory_space=pl.ANY)],
        out_specs=pl.BlockSpec((None, tm, tn), lambda i, j, k: (layer, i, ooff + j)),
        out_shape=jax.ShapeDtypeStruct(gbuf.shape, F32),
        input_output_aliases={2: 0},
        compiler_params=_cp(("parallel", "parallel", "arbitrary")))(a, b, gbuf)


def _ffn_up(x1, wgt, wut, layer):
    tm, tn = 512, 256

    def body(x_ref, wg_ref, wu_ref, g_ref, u_ref, a_ref):
        xb = x_ref[...].astype(BF16)
        g = _dot_nt(xb, wg_ref[...])
        u = _dot_nt(xb, wu_ref[...])
        g_ref[...] = g
        u_ref[...] = u
        a_ref[...] = (g * jax.nn.sigmoid(g) * u).astype(BF16)

    wspec = pl.BlockSpec((None, tn, D), lambda i, j: (layer, j, 0))
    ospec = pl.BlockSpec((tm, tn), lambda i, j: (i, j))
    return pl.pallas_call(
        body, name="ffn_up", grid=(T // tm, DFF // tn),
        in_specs=[pl.BlockSpec((tm, D), lambda i, j: (i, 0)), wspec, wspec],
        out_specs=[ospec, ospec, ospec],
        out_shape=[jax.ShapeDtypeStruct((T, DFF), F32), jax.ShapeDtypeStruct((T, DFF), F32),
                   jax.ShapeDtypeStruct((T, DFF), BF16)],
        compiler_params=_cp(("parallel", "parallel")))(x1, wgt, wut)


def _ffn_da(dffn, wd, g, u, layer):
    tm, tn = 512, 256

    def body(d_ref, wd_ref, g_ref, u_ref, dg_ref, du_ref):
        da = _dot_nt(d_ref[...].astype(BF16), wd_ref[...])
        gv = g_ref[...]
        sg = jax.nn.sigmoid(gv)
        dg_ref[...] = (da * u_ref[...] * (sg * (1.0 + gv * (1.0 - sg)))).astype(BF16)
        du_ref[...] = (da * (gv * sg)).astype(BF16)

    ospec = pl.BlockSpec((tm, tn), lambda i, j: (i, j))
    return pl.pallas_call(
        body, name="ffn_da", grid=(T // tm, DFF // tn),
        in_specs=[pl.BlockSpec((tm, D), lambda i, j: (i, 0)),
                  pl.BlockSpec((None, tn, D), lambda i, j: (layer, j, 0)), ospec, ospec],
        out_specs=[ospec, ospec],
        out_shape=[jax.ShapeDtypeStruct((T, DFF), BF16), jax.ShapeDtypeStruct((T, DFF), BF16)],
        compiler_params=_cp(("parallel", "parallel")))(dffn, wd, g, u)


def _ln_fwd(x, f, gam, bet):
    tm = 256

    def body(x_ref, f_ref, g_ref, b_ref, y_ref, xh_ref, r_ref):
        s = ALPHA * x_ref[...] + f_ref[...]
        mu = jnp.mean(s, axis=-1, keepdims=True)
        xc = s - mu
        var = jnp.mean(xc * xc, axis=-1, keepdims=True)
        r = lax.rsqrt(var + LN_EPS)
        xh = xc * r
        xh_ref[...] = xh
        r_ref[...] = r
        y_ref[...] = xh * g_ref[...] + b_ref[...]

    row = pl.BlockSpec((tm, D), lambda i: (i, 0))
    vec = pl.BlockSpec((1, D), lambda i: (0, 0))
    return pl.pallas_call(
        body, name="ln_fwd", grid=(T // tm,), in_specs=[row, row, vec, vec],
        out_specs=[row, row, pl.BlockSpec((tm, 1), lambda i: (i, 0))],
        out_shape=[jax.ShapeDtypeStruct((T, D), F32), jax.ShapeDtypeStruct((T, D), F32),
                   jax.ShapeDtypeStruct((T, 1), F32)],
        compiler_params=_cp(("parallel",)))(x, f, gam, bet)


def _ln_bwd(dy, xh, r, gam):
    tm = 256

    def body(dy_ref, xh_ref, r_ref, g_ref, ds_ref, dg_ref, db_ref):
        i = pl.program_id(0)
        dyv = dy_ref[...]
        xhv = xh_ref[...]
        dxh = dyv * g_ref[...]
        m1 = jnp.mean(dxh, axis=-1, keepdims=True)
        m2 = jnp.mean(dxh * xhv, axis=-1, keepdims=True)
        ds_ref[...] = r_ref[...] * (dxh - m1 - xhv * m2)
        pg = jnp.sum(dyv * xhv, axis=0, keepdims=True)
        pb = jnp.sum(dyv, axis=0, keepdims=True)

        @pl.when(i == 0)
        def _():
            dg_ref[...] = pg
            db_ref[...] = pb

        @pl.when(i > 0)
        def _():
            dg_ref[...] += pg
            db_ref[...] += pb

    row = pl.BlockSpec((tm, D), lambda i: (i, 0))
    vec = pl.BlockSpec((1, D), lambda i: (0, 0))
    return pl.pallas_call(
        body, name="ln_bwd", grid=(T // tm,),
        in_specs=[row, row, pl.BlockSpec((tm, 1), lambda i: (i, 0)), vec],
        out_specs=[row, vec, vec],
        out_shape=[jax.ShapeDtypeStruct((T, D), F32), jax.ShapeDtypeStruct((1, D), F32),
                   jax.ShapeDtypeStruct((1, D), F32)],
        compiler_params=_cp(("arbitrary",)))(dy, xh, r, gam)


def _loss_grad(y, tgt):
    tm = 256

    def body(y_ref, t_ref, l_ref, dy_ref):
        i = pl.program_id(0)
        e = y_ref[...] - t_ref[...]
        dy_ref[...] = e * (1.0 / D)
        p = jnp.sum(jnp.sum(e * e, axis=1, keepdims=True), axis=0, keepdims=True)

        @pl.when(i == 0)
        def _():
            l_ref[...] = p

        @pl.when(i > 0)
        def _():
            l_ref[...] += p

    row = pl.BlockSpec((tm, D), lambda i: (i, 0))
    return pl.pallas_call(
        body, name="loss_grad", grid=(T // tm,), in_specs=[row, row],
        out_specs=[pl.BlockSpec((1, 1), lambda i: (0, 0)), row],
        out_shape=[jax.ShapeDtypeStruct((1, 1), F32), jax.ShapeDtypeStruct((T, D), F32)],
        compiler_params=_cp(("arbitrary",)))(y, tgt)


def _adamw(w, g, m, v, tr):
    R, C = w.shape

    def body(w_ref, g_ref, m_ref, v_ref, d_ref, m2_ref, v2_ref):
        gv = g_ref[...]
        m2 = ADAM_B1 * m_ref[...] + (1.0 - ADAM_B1) * gv
        v2 = ADAM_B2 * v_ref[...] + (1.0 - ADAM_B2) * (gv * gv)
        m_hat = m2 / (1.0 - ADAM_B1 ** ADAM_STEP)
        v_hat = v2 / (1.0 - ADAM_B2 ** ADAM_STEP)
        d_ref[...] = -ADAM_LR * (m_hat / (jnp.sqrt(v_hat) + ADAM_EPS) + ADAM_WD * w_ref[...])
        m2_ref[...] = m2
        v2_ref[...] = v2

    blk = pl.BlockSpec((tr, C), lambda i: (i, 0))
    sh = jax.ShapeDtypeStruct((R, C), F32)
    return pl.pallas_call(
        body, name="adamw", grid=(R // tr,), in_specs=[blk] * 4, out_specs=[blk] * 3,
        out_shape=[sh, sh, sh], compiler_params=_cp(("parallel",)))(w, g, m, v)


def _head_masks(hh):
    lane = _iota((1, BQ), 1)
    maskf = ((lane >> 6) == hh).astype(F32)
    return maskf, maskf.astype(BF16)


def _qkv_spec(g):
    return pl.BlockSpec((S, PAIRW), lambda b, h: (b, 2 * g + h // 2))


def _pair_spec(g):
    return pl.BlockSpec((S, BQ), lambda b, h: (b, 2 * g + h // 2))


def _stat_spec():
    return pl.BlockSpec((None, S, 1), lambda b, h: (b * NH + h, 0, 0))


ANY_SPEC = pl.BlockSpec(memory_space=pl.ANY)


def _sb_fwd(qkv):
    def body(qkv_ref, o_ref, r_ref):
        hh = pl.program_id(1) % 2
        maskf, maskb = _head_masks(hh)
        row = _iota((BQ, BQ), 0)
        col = _iota((BQ, BQ), 1)
        u2 = ((_iota((2 * BQ, BQ), 0) & (BQ - 1)) > _iota((2 * BQ, BQ), 1)).astype(BF16)

        def qblock(i, _):
            q0 = pl.multiple_of(i * BQ, BQ)
            q = qkv_ref[pl.ds(q0, BQ), 0:BQ] * maskb

            def kblock(jj, carry):
                tail_c, acc = carry
                j = i - jj
                k0 = pl.multiple_of(j * BQ, BQ)
                k = qkv_ref[pl.ds(k0, BQ), BQ:2 * BQ]
                v = qkv_ref[pl.ds(k0, BQ), 2 * BQ:3 * BQ]
                z = _dot_nt(q, k) * SCALE
                valid = (k0 + col) < (q0 + row)
                lb = _log_sigmoid(z)
                lr = jnp.where(valid, lb - z, 0.0)
                tail = _dot(_split2(lr), u2)
                a = jnp.where(valid, jnp.exp(lb + tail + tail_c), 0.0)
                acc = acc + _dot(a.astype(BF16), v)
                tail_c = tail_c + jnp.sum(lr, axis=1, keepdims=True)
                return tail_c, acc

            tail_c, acc = lax.fori_loop(0, i + 1, kblock,
                                        (jnp.zeros((BQ, 1), F32), jnp.zeros((BQ, BQ), F32)))
            val = (acc * maskf).astype(BF16)

            @pl.when(hh == 0)
            def _():
                o_ref[pl.ds(q0, BQ), :] = val

            @pl.when(hh == 1)
            def _():
                o_ref[pl.ds(q0, BQ), :] = o_ref[pl.ds(q0, BQ), :] + val

            r_ref[pl.ds(q0, BQ), :] = tail_c
            return 0

        lax.fori_loop(0, NB, qblock, 0)

    return pl.pallas_call(
        body, name="sb_fwd", grid=(BL, NH), in_specs=[_qkv_spec(0)],
        out_specs=[_pair_spec(0), _stat_spec()],
        out_shape=[jax.ShapeDtypeStruct((T, D), BF16), jax.ShapeDtypeStruct((BL * NH, S, 1), F32)],
        compiler_params=_cp(("parallel", "arbitrary")))(qkv)


def _sb_bwd(qkv, dmixed, rtot):
    def body(qkv_ref, do_ref, r_ref, dqkv_ref, dq_s, dk_s, dv_s):
        hh = pl.program_id(1) % 2
        maskf, maskb = _head_masks(hh)
        row = _iota((BQ, BQ), 0)
        col = _iota((BQ, BQ), 1)
        r2 = _iota((2 * BQ, BQ), 0) & (BQ - 1)
        c2 = _iota((2 * BQ, BQ), 1)
        u2 = (r2 > c2).astype(BF16)
        l2 = (r2 < c2).astype(BF16)

        @pl.when(hh == 0)
        def _():
            dq_s[...] = jnp.zeros_like(dq_s)
            dk_s[...] = jnp.zeros_like(dk_s)
            dv_s[...] = jnp.zeros_like(dv_s)

        def qblock(i, _):
            q0 = pl.multiple_of(i * BQ, BQ)
            q = qkv_ref[pl.ds(q0, BQ), 0:BQ] * maskb
            do = do_ref[pl.ds(q0, BQ), :] * maskb
            rt = r_ref[pl.ds(q0, BQ), :]

            def kblock(j, carry):
                pre_l, pre_g, dq = carry
                k0 = pl.multiple_of(j * BQ, BQ)
                k = qkv_ref[pl.ds(k0, BQ), BQ:2 * BQ]
                v = qkv_ref[pl.ds(k0, BQ), 2 * BQ:3 * BQ]
                z = _dot_nt(q, k) * SCALE
                valid = (k0 + col) < (q0 + row)
                lb = _log_sigmoid(z)
                lr = jnp.where(valid, lb - z, 0.0)
                pre_l = pre_l + jnp.sum(lr, axis=1, keepdims=True)
                tail = _dot(_split2(lr), u2) + (rt - pre_l)
                a = jnp.where(valid, jnp.exp(lb + tail), 0.0)
                gm = _dot_nt(do, v) * a
                before = _dot(_split2(gm), l2) + pre_g
                beta = jnp.exp(lb)
                dz = jnp.where(valid, gm * (1.0 - beta) - beta * before, 0.0) * SCALE
                dzb = dz.astype(BF16)
                dq = dq + _dot(dzb, k)
                dk_s[pl.ds(k0, BQ), :] += _dot_tn(dzb, q)
                dv_s[pl.ds(k0, BQ), :] += _dot_tn(a.astype(BF16), do)
                pre_g = pre_g + jnp.sum(gm, axis=1, keepdims=True)
                return pre_l, pre_g, dq

            z1 = jnp.zeros((BQ, 1), F32)
            _, _, dq = lax.fori_loop(0, i + 1, kblock, (z1, z1, jnp.zeros((BQ, BQ), F32)))
            dq_s[pl.ds(q0, BQ), :] += dq * maskf
            return 0

        lax.fori_loop(0, NB, qblock, 0)

        @pl.when(hh == 1)
        def _():
            dqkv_ref[:, 0:BQ] = dq_s[...].astype(BF16)
            dqkv_ref[:, BQ:2 * BQ] = dk_s[...].astype(BF16)
            dqkv_ref[:, 2 * BQ:3 * BQ] = dv_s[...].astype(BF16)

    return pl.pallas_call(
        body, name="sb_bwd", grid=(BL, NH),
        in_specs=[_qkv_spec(0), _pair_spec(0), _stat_spec()],
        out_specs=_qkv_spec(0),
        out_shape=jax.ShapeDtypeStruct((T, QKVW), BF16),
        scratch_shapes=[pltpu.VMEM((S, BQ), F32)] * 3,
        compiler_params=_cp(("parallel", "arbitrary")))(qkv, dmixed, rtot)


def _flash_fwd(qkv, mixed, g, fox, bias):
    def body(*refs):
        if fox:
            qkv_ref, cq_ref, ck_ref, _, o_ref, lse_ref = refs
        else:
            qkv_ref, tbl_ref, _, o_ref, lse_ref = refs
        hh = pl.program_id(1) % 2
        maskf, maskb = _head_masks(hh)
        row = _iota((BQ, BQ), 0)
        col = _iota((BQ, BQ), 1)

        def qblock(i, _):
            q0 = pl.multiple_of(i * BQ, BQ)
            q = qkv_ref[pl.ds(q0, BQ), 0:BQ] * maskb
            if fox:
                cq = cq_ref[pl.ds(q0, BQ), :]

            def kblock(jj, carry):
                m, l, acc = carry
                j = i - jj
                k0 = pl.multiple_of(j * BQ, BQ)
                k = qkv_ref[pl.ds(k0, BQ), BQ:2 * BQ]
                v = qkv_ref[pl.ds(k0, BQ), 2 * BQ:3 * BQ]
                if fox:
                    z = _dot_nt(q, k) * SCALE + (cq - ck_ref[:, pl.ds(k0, BQ)])
                    z = jnp.where((k0 + col) <= (q0 + row), z, NEG)
                else:
                    z = _dot_nt(q, k) * SCALE + tbl_ref[jj]
                m_new = jnp.maximum(m, jnp.max(z, axis=1, keepdims=True))
                alpha = jnp.exp(m - m_new)
                p = jnp.exp(z - m_new)
                l = alpha * l + jnp.sum(p, axis=1, keepdims=True)
                acc = alpha * acc + _dot(p.astype(BF16), v)
                return m_new, l, acc

            m, l, acc = lax.fori_loop(
                0, i + 1, kblock,
                (jnp.full((BQ, 1), NEG, F32), jnp.zeros((BQ, 1), F32), jnp.zeros((BQ, BQ), F32)))
            val = (acc / l * maskf).astype(BF16)

            @pl.when(hh == 0)
            def _():
                o_ref[pl.ds(q0, BQ), :] = val

            @pl.when(hh == 1)
            def _():
                o_ref[pl.ds(q0, BQ), :] = o_ref[pl.ds(q0, BQ), :] + val

            lse_ref[pl.ds(q0, BQ), :] = m + jnp.log(l)
            return 0

        lax.fori_loop(0, NB, qblock, 0)

    if fox:
        bias_specs = [_stat_spec(), pl.BlockSpec((None, 1, S), lambda b, h: (b * NH + h, 0, 0))]
    else:
        bias_specs = [pl.BlockSpec((None, NB, BQ, BQ), lambda b, h: (h, 0, 0, 0))]
    n_in = 2 + len(bias_specs)
    return pl.pallas_call(
        body, name="fox_fwd" if fox else "dil_fwd", grid=(BL, NH),
        in_specs=[_qkv_spec(g)] + bias_specs + [ANY_SPEC],
        out_specs=[_pair_spec(g), _stat_spec()],
        out_shape=[jax.ShapeDtypeStruct((T, D), BF16), jax.ShapeDtypeStruct((BL * NH, S, 1), F32)],
        input_output_aliases={n_in - 1: 0},
        compiler_params=_cp(("parallel", "arbitrary")))(qkv, *bias, mixed)


def _flash_bwd(qkv, mixed, dmixed, lse, dqkv, g, fox, bias):
    def body(*refs):
        if fox:
            qkv_ref, o_ref, do_ref, lse_ref, cq_ref, ck_ref, _, dqkv_ref, db_ref, dq_s, dk_s, dv_s = refs
        else:
            qkv_ref, o_ref, do_ref, lse_ref, tbl_ref, _, dqkv_ref, db_ref, dq_s, dk_s, dv_s = refs
        hh = pl.program_id(1) % 2
        maskf, maskb = _head_masks(hh)
        row = _iota((BQ, BQ), 0)
        col = _iota((BQ, BQ), 1)

        @pl.when(hh == 0)
        def _():
            dq_s[...] = jnp.zeros_like(dq_s)
            dk_s[...] = jnp.zeros_like(dk_s)
            dv_s[...] = jnp.zeros_like(dv_s)

        db_ref[...] = jnp.zeros_like(db_ref)

        def qblock(i, _):
            q0 = pl.multiple_of(i * BQ, BQ)
            q = qkv_ref[pl.ds(q0, BQ), 0:BQ] * maskb
            do = do_ref[pl.ds(q0, BQ), :] * maskb
            lse_q = lse_ref[pl.ds(q0, BQ), :]
            if fox:
                cq = cq_ref[pl.ds(q0, BQ), :]

            def probs(j):
                k0 = pl.multiple_of(j * BQ, BQ)
                k = qkv_ref[pl.ds(k0, BQ), BQ:2 * BQ]
                v = qkv_ref[pl.ds(k0, BQ), 2 * BQ:3 * BQ]
                if fox:
                    z = _dot_nt(q, k) * SCALE + (cq - ck_ref[:, pl.ds(k0, BQ)])
                    z = jnp.where((k0 + col) <= (q0 + row), z, NEG)
                else:
                    z = _dot_nt(q, k) * SCALE + tbl_ref[i - j]
                return k0, k, jnp.exp(z - lse_q), _dot_nt(do, v)

            if fox:
                def dblock(j, acc):
                    _, _, p, dp = probs(j)
                    return acc + jnp.sum(p * dp, axis=1, keepdims=True)

                delta = lax.fori_loop(0, i + 1, dblock, jnp.zeros((BQ, 1), F32))
            else:
                delta = jnp.sum(do.astype(F32) * o_ref[pl.ds(q0, BQ), :].astype(F32), axis=1, keepdims=True)

            def kblock(j, dq):
                k0, k, p, dp = probs(j)
                dz = p * (dp - delta)
                if fox:
                    db_ref[:, pl.ds(k0, BQ)] = db_ref[:, pl.ds(k0, BQ)] - jnp.sum(dz, axis=0, keepdims=True)
                else:
                    db_ref[i - j] = db_ref[i - j] + dz
                dzb = (dz * SCALE).astype(BF16)
                dk_s[pl.ds(k0, BQ), :] += _dot_tn(dzb, q)
                dv_s[pl.ds(k0, BQ), :] += _dot_tn(p.astype(BF16), do)
                return dq + _dot(dzb, k)

            dq = lax.fori_loop(0, i + 1, kblock, jnp.zeros((BQ, BQ), F32))
            dq_s[pl.ds(q0, BQ), :] += dq * maskf
            return 0

        lax.fori_loop(0, NB, qblock, 0)

        @pl.when(hh == 1)
        def _():
            dqkv_ref[:, 0:BQ] = dq_s[...].astype(BF16)
            dqkv_ref[:, BQ:2 * BQ] = dk_s[...].astype(BF16)
            dqkv_ref[:, 2 * BQ:3 * BQ] = dv_s[...].astype(BF16)

    if fox:
        bias_specs = [_stat_spec(), pl.BlockSpec((None, 1, S), lambda b, h: (b * NH + h, 0, 0))]
        db_spec = pl.BlockSpec((None, 1, S), lambda b, h: (b * NH + h, 0, 0))
        db_shape = jax.ShapeDtypeStruct((BL * NH, 1, S), F32)
    else:
        bias_specs = [pl.BlockSpec((None, NB, BQ, BQ), lambda b, h: (h, 0, 0, 0))]
        db_spec = pl.BlockSpec((None, NB, BQ, BQ), lambda b, h: (b * NH + h, 0, 0, 0))
        db_shape = jax.ShapeDtypeStruct((BL * NH, NB, BQ, BQ), F32)
    n_in = 5 + len(bias_specs)
    return pl.pallas_call(
        body, name="fox_bwd" if fox else "dil_bwd", grid=(BL, NH),
        in_specs=[_qkv_spec(g), _pair_spec(g), _pair_spec(g), _stat_spec()] + bias_specs + [ANY_SPEC],
        out_specs=[_qkv_spec(g), db_spec],
        out_shape=[jax.ShapeDtypeStruct((T, QKVW), BF16), db_shape],
        scratch_shapes=[pltpu.VMEM((S, BQ), F32)] * 3,
        input_output_aliases={n_in - 1: 0},
        compiler_params=_cp(("parallel", "arbitrary")))(qkv, mixed, dmixed, lse, *bias, dqkv)


def _delta_tile(d):
    return d * BQ + _iota((BQ, BQ), 0) - _iota((BQ, BQ), 1)


def _bucket_tile(delta):
    b = jnp.zeros((BQ, BQ), jnp.int32)
    for th in BUCKET_TH[1:]:
        b = b + (delta >= th).astype(jnp.int32)
    return b


def _dil_table(rel_bias):
    def body(rb_ref, o_ref):
        for d in range(NB):
            delta = _delta_tile(d)
            bucket = _bucket_tile(delta)
            pos = delta >= 0
            n = ((pos & (delta <= 128)).astype(jnp.int32)
                 + (pos & (delta <= 512) & ((delta & 3) == 0)).astype(jnp.int32)
                 + (pos & ((delta & 15) == 0)).astype(jnp.int32))
            logn = jnp.where(n == 3, math.log(3.0), jnp.where(n == 2, math.log(2.0), jnp.where(n == 1, 0.0, NEG)))
            for h in range(NH):
                val = lax.fori_loop(0, 32, lambda b, acc: jnp.where(bucket == b, rb_ref[b, h], acc),
                                    jnp.zeros((BQ, BQ), F32))
                o_ref[h, d] = val + logn

    return pl.pallas_call(
        body, name="dil_table", in_specs=[pl.BlockSpec(memory_space=pltpu.SMEM)],
        out_specs=pl.BlockSpec(memory_space=pltpu.VMEM),
        out_shape=jax.ShapeDtypeStruct((NH, NB, BQ, BQ), F32), compiler_params=_cp())(rel_bias)


def _dil_table_bwd(dtbl):
    def body(dt_ref, o_ref):
        h = pl.program_id(0)
        rowi = _iota((32, BQ), 0)
        lanei = _iota((32, BQ), 1)

        @pl.when(h == 0)
        def _():
            o_ref[...] = jnp.zeros_like(o_ref)

        acc = jnp.zeros((32, BQ), F32)
        for d in range(NB):
            tile = dt_ref[0, d] + dt_ref[1, d]
            bucket = _bucket_tile(_delta_tile(d))

            def bb(b, acc):
                s = jnp.sum(jnp.sum(jnp.where(bucket == b, tile, 0.0), axis=1, keepdims=True), axis=0, keepdims=True)
                return acc + jnp.where((rowi == b) & (lanei == h), s, 0.0)

            acc = lax.fori_loop(0, 32, bb, acc)
        o_ref[...] += acc

    return pl.pallas_call(
        body, name="dil_table_bwd", grid=(NH,),
        in_specs=[pl.BlockSpec((BL, None, NB, BQ, BQ), lambda h: (0, h, 0, 0, 0))],
        out_specs=pl.BlockSpec((32, BQ), lambda h: (0, 0)),
        out_shape=jax.ShapeDtypeStruct((32, BQ), F32),
        compiler_params=_cp(("arbitrary",)))(dtbl.reshape(BL, NH, NB, BQ, BQ))


def _fox_prep(gate, fb):
    def body(g_ref, fb_ref, c_ref):
        tri = (_iota((BQ, BQ), 0) >= _iota((BQ, BQ), 1)).astype(BF16)

        def blk(i, carry):
            r0 = pl.multiple_of(i * BQ, BQ)
            lf = _log_sigmoid(g_ref[pl.ds(r0, BQ), :] + fb_ref[...])
            c = _dot(tri, _split3(lf))
            c_ref[pl.ds(r0, BQ), :] = c[:, 0:BQ] + c[:, BQ:2 * BQ] + c[:, 2 * BQ:3 * BQ] + carry
            return carry + jnp.sum(lf, axis=0, keepdims=True)

        lax.fori_loop(0, NB, blk, jnp.zeros((1, BQ), F32))

    blk = pl.BlockSpec((S, GATEW), lambda b: (b, 0))
    return pl.pallas_call(
        body, name="fox_prep", grid=(BL,), in_specs=[blk, pl.BlockSpec((1, GATEW), lambda b: (0, 0))],
        out_specs=blk, out_shape=jax.ShapeDtypeStruct((T, GATEW), F32),
        compiler_params=_cp(("parallel",)))(gate, fb)


def _fox_post(dcum, gate, fb):
    def body(dc_ref, g_ref, fb_ref, dg_ref, dfb_ref):
        b = pl.program_id(0)
        tri = (_iota((BQ, BQ), 0) <= _iota((BQ, BQ), 1)).astype(BF16)

        def blk(ii, carry):
            csum, dfb = carry
            r0 = pl.multiple_of((NB - 1 - ii) * BQ, BQ)
            dc = dc_ref[pl.ds(r0, BQ), :]
            c = _dot(tri, _split3(dc))
            dlf = c[:, 0:BQ] + c[:, BQ:2 * BQ] + c[:, 2 * BQ:3 * BQ] + csum
            dg = dlf * jnp.exp(_log_sigmoid(-(g_ref[pl.ds(r0, BQ), :] + fb_ref[...])))
            dg_ref[pl.ds(r0, BQ), :] = dg
            return csum + jnp.sum(dc, axis=0, keepdims=True), dfb + jnp.sum(dg, axis=0, keepdims=True)

        z = jnp.zeros((1, BQ), F32)
        _, dfb = lax.fori_loop(0, NB, blk, (z, z))

        @pl.when(b == 0)
        def _():
            dfb_ref[...] = dfb

        @pl.when(b > 0)
        def _():
            dfb_ref[...] += dfb

    blk = pl.BlockSpec((S, GATEW), lambda b: (b, 0))
    vec = pl.BlockSpec((1, GATEW), lambda b: (0, 0))
    return pl.pallas_call(
        body, name="fox_post", grid=(BL,), in_specs=[blk, blk, vec], out_specs=[blk, vec],
        out_shape=[jax.ShapeDtypeStruct((T, GATEW), F32), jax.ShapeDtypeStruct((1, GATEW), F32)],
        compiler_params=_cp(("arbitrary",)))(dcum, gate, fb)


def _shift_down(x, n):
    return jnp.where(_iota(x.shape, 0) >= n, pltpu.roll(x, n, 0), 0.0)


def _shift_up(x, n):
    return jnp.where(_iota(x.shape, 0) < S - n, pltpu.roll(x, S - n, 0), 0.0)


def _conv_fwd(conv, cw, mixed):
    W = 256

    def body(c_ref, w_ref, _, o_ref):
        u = c_ref[:, W:2 * W] * c_ref[:, 2 * W:3 * W]
        y = w_ref[0:1, :] * _shift_down(u, 2) + w_ref[1:2, :] * _shift_down(u, 1) + w_ref[2:3, :] * u
        o_ref[...] = (c_ref[:, 0:W] * y).astype(BF16)

    return pl.pallas_call(
        body, name="conv_fwd", grid=(BL,),
        in_specs=[pl.BlockSpec((S, CONVW), lambda b: (b, 0)), pl.BlockSpec((8, W), lambda b: (0, 0)), ANY_SPEC],
        out_specs=pl.BlockSpec((S, W), lambda b: (b, 3)),
        out_shape=jax.ShapeDtypeStruct((T, D), BF16), input_output_aliases={2: 0},
        compiler_params=_cp(("parallel",)))(conv, cw, mixed)


def _conv_bwd(conv, cw, dmixed):
    W = 256

    def body(c_ref, w_ref, do_ref, dc_ref, dw_ref):
        b = pl.program_id(0)
        bg = c_ref[:, 0:W]
        cg = c_ref[:, W:2 * W]
        hv = c_ref[:, 2 * W:3 * W]
        do = do_ref[...].astype(F32)
        u = cg * hv
        u1 = _shift_down(u, 1)
        u2 = _shift_down(u, 2)
        y = w_ref[0:1, :] * u2 + w_ref[1:2, :] * u1 + w_ref[2:3, :] * u
        dy = do * bg
        du = w_ref[2:3, :] * dy + w_ref[1:2, :] * _shift_up(dy, 1) + w_ref[0:1, :] * _shift_up(dy, 2)
        dc_ref[:, 0:W] = (do * y).astype(BF16)
        dc_ref[:, W:2 * W] = (du * hv).astype(BF16)
        dc_ref[:, 2 * W:3 * W] = (du * cg).astype(BF16)
        rowi = _iota((8, W), 0)
        dw = (jnp.where(rowi == 0, jnp.sum(dy * u2, axis=0, keepdims=True), 0.0)
              + jnp.where(rowi == 1, jnp.sum(dy * u1, axis=0, keepdims=True), 0.0)
              + jnp.where(rowi == 2, jnp.sum(dy * u, axis=0, keepdims=True), 0.0))

        @pl.when(b == 0)
        def _():
            dw_ref[...] = dw

        @pl.when(b > 0)
        def _():
            dw_ref[...] += dw

    return pl.pallas_call(
        body, name="conv_bwd", grid=(BL,),
        in_specs=[pl.BlockSpec((S, CONVW), lambda b: (b, 0)), pl.BlockSpec((8, W), lambda b: (0, 0)),
                  pl.BlockSpec((S, W), lambda b: (b, 3))],
        out_specs=[pl.BlockSpec((S, CONVW), lambda b: (b, 0)), pl.BlockSpec((8, W), lambda b: (0, 0))],
        out_shape=[jax.ShapeDtypeStruct((T, CONVW), BF16), jax.ShapeDtypeStruct((8, W), F32)],
        compiler_params=_cp(("arbitrary",)))(conv, cw, dmixed)


def _place():
    x, y, c = lax.axis_index("x"), lax.axis_index("y"), lax.axis_index("c")
    return x, y, c


def _allgather_weights(shards):
    n = len(shards)

    def body(*refs):
        ins, outs = refs[:n], refs[n:2 * n]
        send_sems, recv_sems, local_sems = refs[2 * n:]
        x, y, c = _place()
        me, sibling = (x, y, c), (x, y, 1 - c)
        chips = [(1 - x, y), (x, 1 - y), (1 - x, 1 - y)]

        def slot(a, p):
            return outs[a].at[:, 4 * p[0] + 2 * p[1] + p[2]]

        def copy(a, k, block, to, own=False):
            return pltpu.make_async_remote_copy(
                src_ref=ins[a] if own else slot(a, block), dst_ref=slot(a, block),
                send_sem=send_sems.at[a, k], recv_sem=recv_sems.at[a, k], device_id=to, device_id_type=MESH)

        mine = [pltpu.make_async_copy(ins[a], slot(a, me), local_sems.at[a]) for a in range(n)]
        for cp in mine:
            cp.start()
        first = []
        for a in range(n):
            first.append(copy(a, 0, me, sibling, own=True))
            first += [copy(a, 1 + j, me, (*chip, c), own=True) for j, chip in enumerate(chips)]
        for cp in first:
            cp.start()
        passed = []
        for j, chip in enumerate(chips):
            for a in range(n):
                copy(a, 1 + j, (*chip, c), me).wait_recv()
                cp = copy(a, 4 + j, (*chip, c), sibling)
                cp.start()
                passed.append(cp)
        for a in range(n):
            copy(a, 0, sibling, me).wait_recv()
            for j, chip in enumerate(chips):
                copy(a, 4 + j, (*chip, 1 - c), me).wait_recv()
        for cp in first + passed:
            cp.wait_send()
        for cp in mine:
            cp.wait()

    return pl.pallas_call(
        body, name="allgather_weights", in_specs=[ANY_SPEC] * n, out_specs=[ANY_SPEC] * n,
        out_shape=[jax.ShapeDtypeStruct((s.shape[0], NDEV) + s.shape[1:], s.dtype) for s in shards],
        scratch_shapes=[pltpu.SemaphoreType.DMA((n, 7)), pltpu.SemaphoreType.DMA((n, 7)),
                        pltpu.SemaphoreType.DMA((n,))],
        )(*shards)


def _allreduce_small(v):
    def body(v_ref, o_ref, slots, send_sems, recv_sems):
        x, y, c = _place()
        me = 4 * x + 2 * y + c
        slots[me] = v_ref[...]

        def copy(k):
            peer = (x ^ ((k >> 2) & 1), y ^ ((k >> 1) & 1), c ^ (k & 1))
            return pltpu.make_async_remote_copy(
                src_ref=v_ref, dst_ref=slots.at[me], send_sem=send_sems.at[k - 1], recv_sem=recv_sems.at[k - 1],
                device_id=peer, device_id_type=MESH)

        def arrival(k):
            return pltpu.make_async_remote_copy(
                src_ref=v_ref, dst_ref=slots.at[me ^ k], send_sem=send_sems.at[k - 1], recv_sem=recv_sems.at[k - 1],
                device_id=(x, y, c), device_id_type=MESH)

        sends = [copy(k) for k in range(1, NDEV)]
        for cp in sends:
            cp.start()
        for k in range(1, NDEV):
            arrival(k).wait_recv()
        for cp in sends:
            cp.wait_send()
        acc = slots[0]
        for d in range(1, NDEV):
            acc = acc + slots[d]
        o_ref[...] = acc

    return pl.pallas_call(
        body, name="allreduce_small",
        in_specs=[pl.BlockSpec(memory_space=pltpu.VMEM)], out_specs=pl.BlockSpec(memory_space=pltpu.VMEM),
        out_shape=jax.ShapeDtypeStruct(v.shape, F32),
        scratch_shapes=[pltpu.VMEM((NDEV,) + v.shape, F32), pltpu.SemaphoreType.DMA((NDEV - 1,)),
                        pltpu.SemaphoreType.DMA((NDEV - 1,))],
        )(v)


def _sibling_exchange(grads):
    n = len(grads)

    def body(*refs):
        ins, outs = refs[:n], refs[n:2 * n]
        send_sems, recv_sems = refs[2 * n:]
        x, y, c = _place()
        cps = [pltpu.make_async_remote_copy(
            src_ref=ins[a].at[:, :, 1 - c], dst_ref=outs[a], send_sem=send_sems.at[a], recv_sem=recv_sems.at[a],
            device_id=(x, y, 1 - c), device_id_type=MESH) for a in range(n)]
        for cp in cps:
            cp.start()
        for cp in cps:
            cp.wait()

    return pl.pallas_call(
        body, name="sibling_exchange", in_specs=[ANY_SPEC] * n, out_specs=[ANY_SPEC] * n,
        out_shape=[jax.ShapeDtypeStruct(g.shape[:2] + g.shape[3:], F32) for g in grads],
        scratch_shapes=[pltpu.SemaphoreType.DMA((n,)), pltpu.SemaphoreType.DMA((n,))],
        )(*grads)


def _pair_sum(grad, got, core):
    _, _, _, rows, N = grad.shape

    def body(c_ref, g_ref, r_ref, o_ref):
        o_ref[...] = (g_ref[...] + r_ref[...]).astype(BF16)

    return pl.pallas_call(
        body, name="pair_sum",
        grid_spec=pltpu.PrefetchScalarGridSpec(
            num_scalar_prefetch=1, grid=(2, 4),
            in_specs=[pl.BlockSpec((None, None, None, rows, N), lambda l, k, c: (l, k, c[0], 0, 0)),
                      pl.BlockSpec((None, None, rows, N), lambda l, k, c: (l, k, 0, 0))],
            out_specs=pl.BlockSpec((None, None, rows, N), lambda l, k, c: (l, k, 0, 0))),
        out_shape=jax.ShapeDtypeStruct((2, 4, rows, N), BF16),
        compiler_params=_cp(("parallel", "parallel")))(core, grad, got)


def _chip_exchange(psums):
    n = len(psums)

    def body(*refs):
        ins, outs = refs[:n], refs[n:2 * n]
        send_sems, recv_sems, local_sems = refs[2 * n:]
        x, y, c = _place()
        mychip = 2 * x + y
        chips = [(1 - x, y), (x, 1 - y), (1 - x, 1 - y)]
        local = [pltpu.make_async_copy(ins[a].at[:, mychip], outs[a].at[:, mychip], local_sems.at[a]) for a in range(n)]
        for cp in local:
            cp.start()
        sends = []
        for a in range(n):
            for j, chip in enumerate(chips):
                sends.append(pltpu.make_async_remote_copy(
                    src_ref=ins[a].at[:, 2 * chip[0] + chip[1]], dst_ref=outs[a].at[:, mychip],
                    send_sem=send_sems.at[a, j], recv_sem=recv_sems.at[a, j],
                    device_id=(*chip, c), device_id_type=MESH))
        for cp in sends:
            cp.start()
        for a in range(n):
            for j, chip in enumerate(chips):
                pltpu.make_async_remote_copy(
                    src_ref=ins[a].at[:, mychip], dst_ref=outs[a].at[:, 2 * chip[0] + chip[1]],
                    send_sem=send_sems.at[a, j], recv_sem=recv_sems.at[a, j],
                    device_id=(x, y, c), device_id_type=MESH).wait_recv()
        for cp in sends:
            cp.wait_send()
        for cp in local:
            cp.wait()

    return pl.pallas_call(
        body, name="chip_exchange", in_specs=[ANY_SPEC] * n, out_specs=[ANY_SPEC] * n,
        out_shape=[jax.ShapeDtypeStruct(p.shape, BF16) for p in psums],
        scratch_shapes=[pltpu.SemaphoreType.DMA((n, 3)), pltpu.SemaphoreType.DMA((n, 3)),
                        pltpu.SemaphoreType.DMA((n,))],
        )(*psums)


def _chip_sum(parts):
    _, _, rows, N = parts.shape

    def body(p_ref, o_ref):
        acc = p_ref[0].astype(F32)
        for k in range(1, 4):
            acc = acc + p_ref[k].astype(F32)
        o_ref[...] = acc

    return pl.pallas_call(
        body, name="chip_sum", grid=(2,),
        in_specs=[pl.BlockSpec((None, 4, rows, N), lambda l: (l, 0, 0, 0))],
        out_specs=pl.BlockSpec((None, rows, N), lambda l: (l, 0, 0)),
        out_shape=jax.ShapeDtypeStruct((2, rows, N), F32), compiler_params=_cp(("parallel",)))(parts)


def _permute_in(w):
    lead = w.shape[:-1]
    return w.reshape(lead + (3, 3, 2, BQ)).swapaxes(-2, -3).reshape(lead + (QKVW,))


def _unpermute_in(w):
    lead = w.shape[:-1]
    return w.reshape(lead + (3, 2, 3, BQ)).swapaxes(-2, -3).reshape(lead + (QKVW,))


def _row(v):
    v = v.reshape(-1)
    return jnp.pad(v, (0, D - v.shape[0])).reshape(1, D)


def kernel(x, w_in, f_bias, conv_w, w_out, rel_bias, ln1_g, ln1_b, w_gate, w_up, w_down, ln2_g, ln2_b, loss_target, m_w_in, m_f_bias, m_conv_w, m_w_out, m_rel_bias, m_ln1_g, m_ln1_b, m_w_gate, m_w_up, m_w_down, m_ln2_g, m_ln2_b, v_w_in, v_f_bias, v_conv_w, v_w_out, v_rel_bias, v_ln1_g, v_ln1_b, v_w_gate, v_w_up, v_w_down, v_ln2_g, v_ln2_b):
    xi, yi, ci = _place()
    me = 4 * xi + 2 * yi + ci

    win_s = jnp.concatenate([_permute_in(w_in[..., :QKVW]), w_in[..., QKVW:]], axis=-1)
    win_s = jnp.pad(win_s, ((0, 0), (0, 0), (0, NPAD - NPROJ))).astype(BF16)
    shards = [win_s, w_out.astype(BF16), jnp.swapaxes(w_gate, 1, 2).astype(BF16),
              jnp.swapaxes(w_up, 1, 2).astype(BF16), w_down.astype(BF16)]
    full = _allgather_weights(shards)
    Win, Wout, WgT, WuT, Wd = [f.reshape(2, NDEV * f.shape[2], f.shape[3]) for f in full]

    cw_rows = lax.dynamic_update_slice(jnp.zeros((2, 3, 256), F32), conv_w, (0, 0, me * 32))
    small = jnp.concatenate([_row(cw_rows[0]), _row(cw_rows[1]), jnp.zeros((SMALL_ROWS - 2, D), F32)], axis=0)
    small = _allreduce_small(small)
    cw_full = small[0:2, :CONVW].reshape(2, 3, 256)
    cw8 = jnp.pad(cw_full, ((0, 0), (0, 5), (0, 0)))
    fb = jnp.pad(f_bias, ((0, 0), (0, GATEW - NH))).reshape(2, 1, GATEW)
    tbl = _dil_table(rel_bias)

    def wcol(layer, K, tn, off):
        return pl.BlockSpec((None, K, tn), lambda i, j: (layer, 0, off + j))

    def arow(tm, K, blk=0):
        return pl.BlockSpec((tm, K), lambda i, j: (i, blk))

    h = x.reshape(T, D)
    saved = []
    for l in range(2):
        qkv = _mm([(h, arow(512, D), Win, wcol(l, D, 768, 0))], nt=False, M=T, N=QKVW, tm=512, tn=768,
                  out_dtype=BF16, name="proj_qkv")
        conv = _mm([(h, arow(512, D), Win, wcol(l, D, 768, 3))], nt=False, M=T, N=CONVW, tm=512, tn=768,
                   out_dtype=F32, name="proj_conv")
        gate = _mm([(h, arow(512, D), Win, wcol(l, D, 128, 24))], nt=False, M=T, N=GATEW, tm=512, tn=128,
                   out_dtype=F32, name="proj_gate")
        cum = _fox_prep(gate, fb[l])
        cum4 = cum[:, :NH].reshape(BL, S, NH).transpose(0, 2, 1)
        cq = cum4.reshape(BL * NH, S, 1)
        ck = cum4.reshape(BL * NH, 1, S)
        mixed, rtot = _sb_fwd(qkv)
        mixed, lse_d = _flash_fwd(qkv, mixed, 1, False, (tbl,))
        mixed, lse_f = _flash_fwd(qkv, mixed, 2, True, (cq, ck))
        mixed = _conv_fwd(conv, cw8[l], mixed)
        mix = _mm([(mixed, arow(512, D), Wout, wcol(l, D, 512, 0))], nt=False, M=T, N=D, tm=512, tn=512,
                  out_dtype=F32, name="out_proj")
        x1, xh1, r1 = _ln_fwd(h, mix, ln1_g[l:l + 1], ln1_b[l:l + 1])
        g, u, a = _ffn_up(x1, WgT, WuT, l)
        ffn = _mm([(a, arow(512, DFF), Wd, wcol(l, DFF, 512, 0))], nt=False, M=T, N=D, tm=512, tn=512,
                  out_dtype=F32, name="ffn_down")
        x2, xh2, r2 = _ln_fwd(x1, ffn, ln2_g[l:l + 1], ln2_b[l:l + 1])
        saved.append(dict(h=h, qkv=qkv, conv=conv, gate=gate, cq=cq, ck=ck, mixed=mixed, rtot=rtot, lse_d=lse_d,
                          lse_f=lse_f, x1=x1, xh1=xh1, r1=r1, g=g, u=u, a=a, xh2=xh2, r2=r2))
        h = x2

    sq, dy = _loss_grad(h, loss_target.reshape(T, D))
    loss = lax.psum(sq[0, 0], ("x", "y", "c")) * (0.5 / D)

    G_in = jnp.zeros((2, D, NPAD), F32)
    G_out = jnp.zeros((2, D, D), F32)
    G_g = jnp.zeros((2, DFF, D), F32)
    G_u = jnp.zeros((2, DFF, D), F32)
    G_d = jnp.zeros((2, DFF, D), F32)
    small_g = {}

    def wrow(layer, tn, K, blk=0):
        return pl.BlockSpec((None, tn, K), lambda i, j: (layer, j, blk))

    for l in (1, 0):
        sv = saved[l]
        ds2, dg2, db2 = _ln_bwd(dy, sv["xh2"], sv["r2"], ln2_g[l:l + 1])
        dgt, dut = _ffn_da(ds2, Wd, sv["g"], sv["u"], l)
        G_d = _mm_tn(sv["a"], ds2, G_d, Ka=DFF, N=D, tm=256, tn=512, tk=1024, a_off=0, b_off=0, layer=l, ooff=0,
                     name="grad_w_down")
        G_g = _mm_tn(dgt, sv["x1"], G_g, Ka=DFF, N=D, tm=256, tn=512, tk=1024, a_off=0, b_off=0, layer=l, ooff=0,
                     name="grad_w_gate")
        G_u = _mm_tn(dut, sv["x1"], G_u, Ka=DFF, N=D, tm=256, tn=512, tk=1024, a_off=0, b_off=0, layer=l, ooff=0,
                     name="grad_w_up")
        dx1 = _mm([(dgt, arow(512, DFF), WgT, wcol(l, DFF, 512, 0)), (dut, arow(512, DFF), WuT, wcol(l, DFF, 512, 0))],
                  nt=False, M=T, N=D, tm=512, tn=512, out_dtype=F32, name="ffn_dx", res=ds2, res_scale=ALPHA)
        ds1, dg1, db1 = _ln_bwd(dx1, sv["xh1"], sv["r1"], ln1_g[l:l + 1])
        G_out = _mm_tn(sv["mixed"], ds1, G_out, Ka=D, N=D, tm=512, tn=512, tk=1024, a_off=0, b_off=0, layer=l, ooff=0,
                       name="grad_w_out")
        dmixed = _mm([(ds1, arow(512, D), Wout, wrow(l, 512, D))], nt=True, M=T, N=D, tm=512, tn=512,
                     out_dtype=BF16, name="out_proj_dx")
        dqkv = _sb_bwd(sv["qkv"], dmixed, sv["rtot"])
        dqkv, dtbl = _flash_bwd(sv["qkv"], sv["mixed"], dmixed, sv["lse_d"], dqkv, 1, False, (tbl,))
        dqkv, dck = _flash_bwd(sv["qkv"], sv["mixed"], dmixed, sv["lse_f"], dqkv, 2, True, (sv["cq"], sv["ck"]))
        dconv, dcw = _conv_bwd(sv["conv"], cw8[l], dmixed)
        dcum = jnp.pad(dck.reshape(BL, NH, S).transpose(0, 2, 1).reshape(T, NH), ((0, 0), (0, GATEW - NH)))
        dgate, dfb = _fox_post(dcum, sv["gate"], fb[l])
        drb = _dil_table_bwd(dtbl)
        G_in = _mm_tn(sv["h"], dqkv, G_in, Ka=D, N=QKVW, tm=512, tn=768, tk=1024, a_off=0, b_off=0, layer=l, ooff=0,
                      name="grad_w_in_qkv")
        G_in = _mm_tn(sv["h"], dconv, G_in, Ka=D, N=CONVW, tm=512, tn=768, tk=1024, a_off=0, b_off=0, layer=l, ooff=3,
                      name="grad_w_in_conv")
        G_in = _mm_tn(sv["h"], dgate, G_in, Ka=D, N=GATEW, tm=512, tn=128, tk=1024, a_off=0, b_off=0, layer=l, ooff=24,
                      name="grad_w_in_gate")
        dy = _mm([(dqkv, arow(512, QKVW), Win, wrow(l, 512, QKVW, 0)),
                  (dconv, arow(512, CONVW), Win, wrow(l, 512, CONVW, 3)),
                  (dgate, arow(512, GATEW), Win, wrow(l, 512, GATEW, 24))],
                 nt=True, M=T, N=D, tm=512, tn=512, out_dtype=F32, name="proj_dx", res=ds1, res_scale=ALPHA)
        small_g[l] = dict(ln1_g=dg1, ln1_b=db1, ln2_g=dg2, ln2_b=db2, cw=dcw[0:3].reshape(1, CONVW),
                          fb=dfb[:, :NH], rb=drb[:, :NH])
    grad_x = dy.reshape(BL, S, D)

    rows = []
    for name in ("ln1_g", "ln1_b", "ln2_g", "ln2_b"):
        rows += [small_g[0][name], small_g[1][name]]
    rows += [_row(small_g[0]["cw"]), _row(small_g[1]["cw"]),
             _row(jnp.concatenate([small_g[0]["fb"], small_g[1]["fb"]], axis=0)),
             _row(small_g[0]["rb"] + small_g[1]["rb"])]
    rows.append(jnp.zeros((SMALL_ROWS - len(rows), D), F32))
    sg = _allreduce_small(jnp.concatenate(rows, axis=0))
    g_ln1_g, g_ln1_b, g_ln2_g, g_ln2_b = sg[0:2], sg[2:4], sg[4:6], sg[6:8]
    g_conv_full = sg[8:10, :CONVW].reshape(2, 3, 256)
    g_conv = lax.dynamic_slice(g_conv_full, (0, 0, me * 32), (2, 3, 32))
    g_fb = sg[10, :2 * NH].reshape(2, NH)
    g_rb = sg[11, :32 * NH].reshape(32, NH)

    bufs = [G_in, G_out, G_g, G_u, G_d]
    views = [b.reshape(2, 4, 2, b.shape[1] // NDEV, b.shape[2]) for b in bufs]
    got = _sibling_exchange(views)
    core = jnp.reshape(ci, (1,)).astype(jnp.int32)
    psums = [_pair_sum(vw, gt, core) for vw, gt in zip(views, got)]
    parts = _chip_exchange(psums)
    gs = [_chip_sum(p) for p in parts]
    g_in = gs[0]
    g_w_in = jnp.concatenate([_unpermute_in(g_in[..., :QKVW]), g_in[..., QKVW:NPROJ]], axis=-1)
    g_w_out = gs[1]
    g_w_gate = jnp.swapaxes(gs[2], 1, 2)
    g_w_up = jnp.swapaxes(gs[3], 1, 2)
    g_w_down = gs[4]

    def big(w, g, m, v, tr):
        sh = w.shape
        f = lambda t: t.reshape(-1, sh[-1])
        return [t.reshape(sh) for t in _adamw(f(w), f(g), f(m), f(v), tr)]

    up_in = big(w_in, g_w_in, m_w_in, v_w_in, 64)
    up_out = big(w_out, g_w_out, m_w_out, v_w_out, 128)
    up_gate = big(w_gate, g_w_gate, m_w_gate, v_w_gate, 256)
    up_up = big(w_up, g_w_up, m_w_up, v_w_up, 256)
    up_down = big(w_down, g_w_down, m_w_down, v_w_down, 352)

    def pack(fbv, cwv, rbv, l1g, l1b, l2g, l2b):
        r = [l1g, l1b, l2g, l2b, _row(cwv), _row(fbv), _row(rbv)]
        r.append(jnp.zeros((SMALL_ROWS - 11, D), F32))
        return jnp.concatenate(r, axis=0)

    pw = pack(f_bias, conv_w, rel_bias, ln1_g, ln1_b, ln2_g, ln2_b)
    pg = pack(g_fb, g_conv, g_rb, g_ln1_g, g_ln1_b, g_ln2_g, g_ln2_b)
    pm = pack(m_f_bias, m_conv_w, m_rel_bias, m_ln1_g, m_ln1_b, m_ln2_g, m_ln2_b)
    pv = pack(v_f_bias, v_conv_w, v_rel_bias, v_ln1_g, v_ln1_b, v_ln2_g, v_ln2_b)
    ups = _adamw(pw, pg, pm, pv, SMALL_ROWS)

    def unpack(p):
        return dict(ln1_g=p[0:2], ln1_b=p[2:4], ln2_g=p[4:6], ln2_b=p[6:8],
                    conv_w=p[8, :192].reshape(2, 3, 32), f_bias=p[9, :2 * NH].reshape(2, NH),
                    rel_bias=p[10, :32 * NH].reshape(32, NH))

    sm = [unpack(p) for p in ups]

    def group(k):
        return (up_in[k], sm[k]["f_bias"], sm[k]["conv_w"], up_out[k], sm[k]["rel_bias"], sm[k]["ln1_g"],
                sm[k]["ln1_b"], up_gate[k], up_up[k], up_down[k], sm[k]["ln2_g"], sm[k]["ln2_b"])

    grads = (g_w_in, g_fb, g_conv, g_w_out, g_rb, g_ln1_g, g_ln1_b, g_w_gate, g_w_up, g_w_down, g_ln2_g, g_ln2_b)
    return (loss, grad_x) + grads + group(0) + group(1) + group(2)
```

```python
import math

import numpy as np
import jax
import jax.numpy as jnp
from jax import lax
from jax.experimental import pallas as pl
from jax.experimental.pallas import tpu as pltpu

F32 = jnp.float32
BF16 = jnp.bfloat16
MESH = pl.DeviceIdType.MESH

D = 1024
S = 2048
BL = 2
T = BL * S
NH = 4
DFF = 2816
NPROJ = 3076
NPAD = 3200
QKVW = 2304
CONVW = 768
GATEW = 128
PAIRW = 384
BQ = 128
NB = S // BQ
NDEV = 8
ROWS_IN = D // NDEV
ROWS_FF = DFF // NDEV
ALPHA = 4.0 ** 0.25
SCALE = 0.125
NEG = -1e30
LN_EPS = 1e-5
ADAM_LR, ADAM_B1, ADAM_B2, ADAM_EPS, ADAM_WD, ADAM_STEP = 0.001, 0.9, 0.999, 1e-08, 0.01, 10
VMEM_LIMIT = 48 * 1024 * 1024
SMALL_ROWS = 16


def _bucket_thresholds():
    d = np.arange(0, S)
    nf = np.maximum(d, 1).astype(np.float32)
    large = 16 + (np.log(nf / np.float32(16)) / np.float32(math.log(128)) * np.float32(16)).astype(np.int32)
    b = np.where(d < 16, d, np.minimum(large, 31))
    return [int(np.argmax(b >= k)) for k in range(32)]


BUCKET_TH = _bucket_thresholds()


def _cp(sem=None):
    return pltpu.CompilerParams(dimension_semantics=sem, vmem_limit_bytes=VMEM_LIMIT)


def _dot(a, b):
    return lax.dot_general(a, b, (((1,), (0,)), ((), ())), preferred_element_type=F32)


def _dot_nt(a, b):
    return lax.dot_general(a, b, (((1,), (1,)), ((), ())), preferred_element_type=F32)


def _dot_tn(a, b):
    return lax.dot_general(a, b, (((0,), (0,)), ((), ())), preferred_element_type=F32)


def _split2(x):
    hi = x.astype(BF16)
    mid = (x - hi.astype(F32)).astype(BF16)
    return jnp.concatenate([hi, mid], axis=1)


def _split3(x):
    hi = x.astype(BF16)
    r = x - hi.astype(F32)
    mid = r.astype(BF16)
    lo = (r - mid.astype(F32)).astype(BF16)
    return jnp.concatenate([hi, mid, lo], axis=1)


def _log_sigmoid(u):
    return jnp.minimum(u, 0.0) - jnp.log1p(jnp.exp(-jnp.abs(u)))


def _iota(shape, dim):
    return lax.broadcasted_iota(jnp.int32, shape, dim)


def _mm(pairs, *, nt, M, N, tm, tn, out_dtype, name, res=None, res_scale=1.0):
    n = len(pairs)

    def body(*refs):
        acc = None
        for p in range(n):
            a = refs[2 * p][...].astype(BF16)
            b = refs[2 * p + 1][...]
            d = _dot_nt(a, b) if nt else _dot(a, b)
            acc = d if acc is None else acc + d
        if res is not None:
            acc = acc + res_scale * refs[2 * n][...]
        refs[-1][...] = acc.astype(out_dtype)

    ops, specs = [], []
    for a, asp, b, bsp in pairs:
        ops += [a, b]
        specs += [asp, bsp]
    if res is not None:
        ops.append(res)
        specs.append(pl.BlockSpec((tm, tn), lambda i, j: (i, j)))
    return pl.pallas_call(
        body, name=name, grid=(M // tm, N // tn), in_specs=specs,
        out_specs=pl.BlockSpec((tm, tn), lambda i, j: (i, j)),
        out_shape=jax.ShapeDtypeStruct((M, N), out_dtype),
        compiler_params=_cp(("parallel", "parallel")))(*ops)


def _mm_tn(a, b, gbuf, *, Ka, N, tm, tn, tk, a_off, b_off, layer, ooff, name):
    def body(a_ref, b_ref, g_in, o_ref):
        k = pl.program_id(2)
        d = _dot_tn(a_ref[...].astype(BF16), b_ref[...].astype(BF16))

        @pl.when(k == 0)
        def _():
            o_ref[...] = d

        @pl.when(k > 0)
        def _():
            o_ref[...] += d

    return pl.pallas_call(
        body, name=name, grid=(Ka // tm, N // tn, T // tk),
        in_specs=[pl.BlockSpec((tk, tm), lambda i, j, k: (k, a_off + i)),
                  pl.BlockSpec((tk, tn), lambda i, j, k: (k, b_off + j)),
                  pl.BlockSpec(memory_space=pl.ANY)],
        out_specs=pl.BlockSpec((None, tm, tn), lambda i, j, k: (layer, i, ooff + j)),
        out_shape=jax.ShapeDtypeStruct(gbuf.shape, F32),
        input_output_aliases={2: 0},
        compiler_params=_cp(("parallel", "parallel", "arbitrary")))(a, b, gbuf)


def _ffn_up(x1, wgt, wut, layer):
    tm, tn = 512, 256

    def body(x_ref, wg_ref, wu_ref, g_ref, u_ref, a_ref):
        xb = x_ref[...].astype(BF16)
        g = _dot_nt(xb, wg_ref[...])
        u = _dot_nt(xb, wu_ref[...])
        g_ref[...] = g
        u_ref[...] = u
        a_ref[...] = (g * jax.nn.sigmoid(g) * u).astype(BF16)

    wspec = pl.BlockSpec((None, tn, D), lambda i, j: (layer, j, 0))
    ospec = pl.BlockSpec((tm, tn), lambda i, j: (i, j))
    return pl.pallas_call(
        body, name="ffn_up", grid=(T // tm, DFF // tn),
        in_specs=[pl.BlockSpec((tm, D), lambda i, j: (i, 0)), wspec, wspec],
        out_specs=[ospec, ospec, ospec],
        out_shape=[jax.ShapeDtypeStruct((T, DFF), F32), jax.ShapeDtypeStruct((T, DFF), F32),
                   jax.ShapeDtypeStruct((T, DFF), BF16)],
        compiler_params=_cp(("parallel", "parallel")))(x1, wgt, wut)


def _ffn_da(dffn, wd, g, u, layer):
    tm, tn = 512, 256

    def body(d_ref, wd_ref, g_ref, u_ref, dg_ref, du_ref):
        da = _dot_nt(d_ref[...].astype(BF16), wd_ref[...])
        gv = g_ref[...]
        sg = jax.nn.sigmoid(gv)
        dg_ref[...] = (da * u_ref[...] * (sg * (1.0 + gv * (1.0 - sg)))).astype(BF16)
        du_ref[...] = (da * (gv * sg)).astype(BF16)

    ospec = pl.BlockSpec((tm, tn), lambda i, j: (i, j))
    return pl.pallas_call(
        body, name="ffn_da", grid=(T // tm, DFF // tn),
        in_specs=[pl.BlockSpec((tm, D), lambda i, j: (i, 0)),
                  pl.BlockSpec((None, tn, D), lambda i, j: (layer, j, 0)), ospec, ospec],
        out_specs=[ospec, ospec],
        out_shape=[jax.ShapeDtypeStruct((T, DFF), BF16), jax.ShapeDtypeStruct((T, DFF), BF16)],
        compiler_params=_cp(("parallel", "parallel")))(dffn, wd, g, u)


def _ln_fwd(x, f, gam, bet):
    tm = 256

    def body(x_ref, f_ref, g_ref, b_ref, y_ref, xh_ref, r_ref):
        s = ALPHA * x_ref[...] + f_ref[...]
        mu = jnp.mean(s, axis=-1, keepdims=True)
        xc = s - mu
        var = jnp.mean(xc * xc, axis=-1, keepdims=True)
        r = lax.rsqrt(var + LN_EPS)
        xh = xc * r
        xh_ref[...] = xh
        r_ref[...] = r
        y_ref[...] = xh * g_ref[...] + b_ref[...]

    row = pl.BlockSpec((tm, D), lambda i: (i, 0))
    vec = pl.BlockSpec((1, D), lambda i: (0, 0))
    return pl.pallas_call(
        body, name="ln_fwd", grid=(T // tm,), in_specs=[row, row, vec, vec],
        out_specs=[row, row, pl.BlockSpec((tm, 1), lambda i: (i, 0))],
        out_shape=[jax.ShapeDtypeStruct((T, D), F32), jax.ShapeDtypeStruct((T, D), F32),
                   jax.ShapeDtypeStruct((T, 1), F32)],
        compiler_params=_cp(("parallel",)))(x, f, gam, bet)


def _ln_bwd(dy, xh, r, gam):
    tm = 256

    def body(dy_ref, xh_ref, r_ref, g_ref, ds_ref, dg_ref, db_ref):
        i = pl.program_id(0)
        dyv = dy_ref[...]
        xhv = xh_ref[...]
        dxh = dyv * g_ref[...]
        m1 = jnp.mean(dxh, axis=-1, keepdims=True)
        m2 = jnp.mean(dxh * xhv, axis=-1, keepdims=True)
        ds_ref[...] = r_ref[...] * (dxh - m1 - xhv * m2)
        pg = jnp.sum(dyv * xhv, axis=0, keepdims=True)
        pb = jnp.sum(dyv, axis=0, keepdims=True)

        @pl.when(i == 0)
        def _():
            dg_ref[...] = pg
            db_ref[...] = pb

        @pl.when(i > 0)
        def _():
            dg_ref[...] += pg
            db_ref[...] += pb

    row = pl.BlockSpec((tm, D), lambda i: (i, 0))
    vec = pl.BlockSpec((1, D), lambda i: (0, 0))
    return pl.pallas_call(
        body, name="ln_bwd", grid=(T // tm,),
        in_specs=[row, row, pl.BlockSpec((tm, 1), lambda i: (i, 0)), vec],
        out_specs=[row, vec, vec],
        out_shape=[jax.ShapeDtypeStruct((T, D), F32), jax.ShapeDtypeStruct((1, D), F32),
                   jax.ShapeDtypeStruct((1, D), F32)],
        compiler_params=_cp(("arbitrary",)))(dy, xh, r, gam)


def _loss_grad(y, tgt):
    tm = 256

    def body(y_ref, t_ref, l_ref, dy_ref):
        i = pl.program_id(0)
        e = y_ref[...] - t_ref[...]
        dy_ref[...] = e * (1.0 / D)
        p = jnp.sum(jnp.sum(e * e, axis=1, keepdims=True), axis=0, keepdims=True)

        @pl.when(i == 0)
        def _():
            l_ref[...] = p

        @pl.when(i > 0)
        def _():
            l_ref[...] += p

    row = pl.BlockSpec((tm, D), lambda i: (i, 0))
    return pl.pallas_call(
        body, name="loss_grad", grid=(T // tm,), in_specs=[row, row],
        out_specs=[pl.BlockSpec((1, 1), lambda i: (0, 0)), row],
        out_shape=[jax.ShapeDtypeStruct((1, 1), F32), jax.ShapeDtypeStruct((T, D), F32)],
        compiler_params=_cp(("arbitrary",)))(y, tgt)


def _adamw(w, g, m, v, tr):
    R, C = w.shape

    def body(w_ref, g_ref, m_ref, v_ref, d_ref, m2_ref, v2_ref):
        gv = g_ref[...]
        m2 = ADAM_B1 * m_ref[...] + (1.0 - ADAM_B1) * gv
        v2 = ADAM_B2 * v_ref[...] + (1.0 - ADAM_B2) * (gv * gv)
        m_hat = m2 / (1.0 - ADAM_B1 ** ADAM_STEP)
        v_hat = v2 / (1.0 - ADAM_B2 ** ADAM_STEP)
        d_ref[...] = -ADAM_LR * (m_hat / (jnp.sqrt(v_hat) + ADAM_EPS) + ADAM_WD * w_ref[...])
        m2_ref[...] = m2
        v2_ref[...] = v2

    blk = pl.BlockSpec((tr, C), lambda i: (i, 0))
    sh = jax.ShapeDtypeStruct((R, C), F32)
    return pl.pallas_call(
        body, name="adamw", grid=(R // tr,), in_specs=[blk] * 4, out_specs=[blk] * 3,
        out_shape=[sh, sh, sh], compiler_params=_cp(("parallel",)))(w, g, m, v)


def _head_masks(hh):
    lane = _iota((1, BQ), 1)
    maskf = ((lane >> 6) == hh).astype(F32)
    return maskf, maskf.astype(BF16)


def _qkv_spec(g):
    return pl.BlockSpec((S, PAIRW), lambda b, h: (b, 2 * g + h // 2))


def _pair_spec(g):
    return pl.BlockSpec((S, BQ), lambda b, h: (b, 2 * g + h // 2))


def _stat_spec():
    return pl.BlockSpec((None, S, 1), lambda b, h: (b * NH + h, 0, 0))


ANY_SPEC = pl.BlockSpec(memory_space=pl.ANY)


def _sb_fwd(qkv):
    def body(qkv_ref, o_ref, r_ref):
        hh = pl.program_id(1) % 2
        maskf, maskb = _head_masks(hh)
        row = _iota((BQ, BQ), 0)
        col = _iota((BQ, BQ), 1)
        u2 = ((_iota((2 * BQ, BQ), 0) & (BQ - 1)) > _iota((2 * BQ, BQ), 1)).astype(BF16)

        def qblock(i, _):
            q0 = pl.multiple_of(i * BQ, BQ)
            q = qkv_ref[pl.ds(q0, BQ), 0:BQ] * maskb

            def kblock(jj, carry):
                tail_c, acc = carry
                j = i - jj
                k0 = pl.multiple_of(j * BQ, BQ)
                k = qkv_ref[pl.ds(k0, BQ), BQ:2 * BQ]
                v = qkv_ref[pl.ds(k0, BQ), 2 * BQ:3 * BQ]
                z = _dot_nt(q, k) * SCALE
                valid = (k0 + col) < (q0 + row)
                lb = _log_sigmoid(z)
                lr = jnp.where(valid, lb - z, 0.0)
                tail = _dot(_split2(lr), u2)
                a = jnp.where(valid, jnp.exp(lb + tail + tail_c), 0.0)
                acc = acc + _dot(a.astype(BF16), v)
                tail_c = tail_c + jnp.sum(lr, axis=1, keepdims=True)
                return tail_c, acc

            tail_c, acc = lax.fori_loop(0, i + 1, kblock,
                                        (jnp.zeros((BQ, 1), F32), jnp.zeros((BQ, BQ), F32)))
            val = (acc * maskf).astype(BF16)

            @pl.when(hh == 0)
            def _():
                o_ref[pl.ds(q0, BQ), :] = val

            @pl.when(hh == 1)
            def _():
                o_ref[pl.ds(q0, BQ), :] = o_ref[pl.ds(q0, BQ), :] + val

            r_ref[pl.ds(q0, BQ), :] = tail_c
            return 0

        lax.fori_loop(0, NB, qblock, 0)

    return pl.pallas_call(
        body, name="sb_fwd", grid=(BL, NH), in_specs=[_qkv_spec(0)],
        out_specs=[_pair_spec(0), _stat_spec()],
        out_shape=[jax.ShapeDtypeStruct((T, D), BF16), jax.ShapeDtypeStruct((BL * NH, S, 1), F32)],
        compiler_params=_cp(("parallel", "arbitrary")))(qkv)


def _sb_bwd(qkv, dmixed, rtot):
    def body(qkv_ref, do_ref, r_ref, dqkv_ref, dq_s, dk_s, dv_s):
        hh = pl.program_id(1) % 2
        maskf, maskb = _head_masks(hh)
        row = _iota((BQ, BQ), 0)
        col = _iota((BQ, BQ), 1)
        r2 = _iota((2 * BQ, BQ), 0) & (BQ - 1)
        c2 = _iota((2 * BQ, BQ), 1)
        u2 = (r2 > c2).astype(BF16)
        l2 = (r2 < c2).astype(BF16)

        @pl.when(hh == 0)
        def _():
            dq_s[...] = jnp.zeros_like(dq_s)
            dk_s[...] = jnp.zeros_like(dk_s)
            dv_s[...] = jnp.zeros_like(dv_s)

        def qblock(i, _):
            q0 = pl.multiple_of(i * BQ, BQ)
            q = qkv_ref[pl.ds(q0, BQ), 0:BQ] * maskb
            do = do_ref[pl.ds(q0, BQ), :] * maskb
            rt = r_ref[pl.ds(q0, BQ), :]

            def kblock(j, carry):
                pre_l, pre_g, dq = carry
                k0 = pl.multiple_of(j * BQ, BQ)
                k = qkv_ref[pl.ds(k0, BQ), BQ:2 * BQ]
                v = qkv_ref[pl.ds(k0, BQ), 2 * BQ:3 * BQ]
                z = _dot_nt(q, k) * SCALE
                valid = (k0 + col) < (q0 + row)
                lb = _log_sigmoid(z)
                lr = jnp.where(valid, lb - z, 0.0)
                pre_l = pre_l + jnp.sum(lr, axis=1, keepdims=True)
                tail = _dot(_split2(lr), u2) + (rt - pre_l)
                a = jnp.where(valid, jnp.exp(lb + tail), 0.0)
                gm = _dot_nt(do, v) * a
                before = _dot(_split2(gm), l2) + pre_g
                beta = jnp.exp(lb)
                dz = jnp.where(valid, gm * (1.0 - beta) - beta * before, 0.0) * SCALE
                dzb = dz.astype(BF16)
                dq = dq + _dot(dzb, k)
                dk_s[pl.ds(k0, BQ), :] += _dot_tn(dzb, q)
                dv_s[pl.ds(k0, BQ), :] += _dot_tn(a.astype(BF16), do)
                pre_g = pre_g + jnp.sum(gm, axis=1, keepdims=True)
                return pre_l, pre_g, dq

            z1 = jnp.zeros((BQ, 1), F32)
            _, _, dq = lax.fori_loop(0, i + 1, kblock, (z1, z1, jnp.zeros((BQ, BQ), F32)))
            dq_s[pl.ds(q0, BQ), :] += dq * maskf
            return 0

        lax.fori_loop(0, NB, qblock, 0)

        @pl.when(hh == 1)
        def _():
            dqkv_ref[:, 0:BQ] = dq_s[...].astype(BF16)
            dqkv_ref[:, BQ:2 * BQ] = dk_s[...].astype(BF16)
            dqkv_ref[:, 2 * BQ:3 * BQ] = dv_s[...].astype(BF16)

    return pl.pallas_call(
        body, name="sb_bwd", grid=(BL, NH),
        in_specs=[_qkv_spec(0), _pair_spec(0), _stat_spec()],
        out_specs=_qkv_spec(0),
        out_shape=jax.ShapeDtypeStruct((T, QKVW), BF16),
        scratch_shapes=[pltpu.VMEM((S, BQ), F32)] * 3,
        compiler_params=_cp(("parallel", "arbitrary")))(qkv, dmixed, rtot)


def _flash_fwd(qkv, mixed, g, fox, bias):
    def body(*refs):
        if fox:
            qkv_ref, cq_ref, ck_ref, _, o_ref, lse_ref = refs
        else:
            qkv_ref, tbl_ref, _, o_ref, lse_ref = refs
        hh = pl.program_id(1) % 2
        maskf, maskb = _head_masks(hh)
        row = _iota((BQ, BQ), 0)
        col = _iota((BQ, BQ), 1)

        def qblock(i, _):
            q0 = pl.multiple_of(i * BQ, BQ)
            q = qkv_ref[pl.ds(q0, BQ), 0:BQ] * maskb
            if fox:
                cq = cq_ref[pl.ds(q0, BQ), :]

            def kblock(jj, carry):
                m, l, acc = carry
                j = i - jj
                k0 = pl.multiple_of(j * BQ, BQ)
                k = qkv_ref[pl.ds(k0, BQ), BQ:2 * BQ]
                v = qkv_ref[pl.ds(k0, BQ), 2 * BQ:3 * BQ]
                if fox:
                    z = _dot_nt(q, k) * SCALE + (cq - ck_ref[:, pl.ds(k0, BQ)])
                    z = jnp.where((k0 + col) <= (q0 + row), z, NEG)
                else:
                    z = _dot_nt(q, k) * SCALE + tbl_ref[jj]
                m_new = jnp.maximum(m, jnp.max(z, axis=1, keepdims=True))
                alpha = jnp.exp(m - m_new)
                p = jnp.exp(z - m_new)
                l = alpha * l + jnp.sum(p, axis=1, keepdims=True)
                acc = alpha * acc + _dot(p.astype(BF16), v)
                return m_new, l, acc

            m, l, acc = lax.fori_loop(
                0, i + 1, kblock,
                (jnp.full((BQ, 1), NEG, F32), jnp.zeros((BQ, 1), F32), jnp.zeros((BQ, BQ), F32)))
            val = (acc / l * maskf).astype(BF16)

            @pl.when(hh == 0)
            def _():
                o_ref[pl.ds(q0, BQ), :] = val

            @pl.when(hh == 1)
            def _():
                o_ref[pl.ds(q0, BQ), :] = o_ref[pl.ds(q0, BQ), :] + val

            lse_ref[pl.ds(q0, BQ), :] = m + jnp.log(l)
            return 0

        lax.fori_loop(0, NB, qblock, 0)

    if fox:
        bias_specs = [_stat_spec(), pl.BlockSpec((None, 1, S), lambda b, h: (b * NH + h, 0, 0))]
    else:
        bias_specs = [pl.BlockSpec((None, NB, BQ, BQ), lambda b, h: (h, 0, 0, 0))]
    n_in = 2 + len(bias_specs)
    return pl.pallas_call(
        body, name="fox_fwd" if fox else "dil_fwd", grid=(BL, NH),
        in_specs=[_qkv_spec(g)] + bias_specs + [ANY_SPEC],
        out_specs=[_pair_spec(g), _stat_spec()],
        out_shape=[jax.ShapeDtypeStruct((T, D), BF16), jax.ShapeDtypeStruct((BL * NH, S, 1), F32)],
        input_output_aliases={n_in - 1: 0},
        compiler_params=_cp(("parallel", "arbitrary")))(qkv, *bias, mixed)


def _flash_bwd(qkv, mixed, dmixed, lse, dqkv, g, fox, bias):
    def body(*refs):
        if fox:
            qkv_ref, o_ref, do_ref, lse_ref, cq_ref, ck_ref, _, dqkv_ref, db_ref, dq_s, dk_s, dv_s = refs
        else:
            qkv_ref, o_ref, do_ref, lse_ref, tbl_ref, _, dqkv_ref, db_ref, dq_s, dk_s, dv_s = refs
        hh = pl.program_id(1) % 2
        maskf, maskb = _head_masks(hh)
        row = _iota((BQ, BQ), 0)
        col = _iota((BQ, BQ), 1)

        @pl.when(hh == 0)
        def _():
            dq_s[...] = jnp.zeros_like(dq_s)
            dk_s[...] = jnp.zeros_like(dk_s)
            dv_s[...] = jnp.zeros_like(dv_s)

        db_ref[...] = jnp.zeros_like(db_ref)

        def qblock(i, _):
            q0 = pl.multiple_of(i * BQ, BQ)
            q = qkv_ref[pl.ds(q0, BQ), 0:BQ] * maskb
            do = do_ref[pl.ds(q0, BQ), :] * maskb
            lse_q = lse_ref[pl.ds(q0, BQ), :]
            if fox:
                cq = cq_ref[pl.ds(q0, BQ), :]

            def probs(j):
                k0 = pl.multiple_of(j * BQ, BQ)
                k = qkv_ref[pl.ds(k0, BQ), BQ:2 * BQ]
                v = qkv_ref[pl.ds(k0, BQ), 2 * BQ:3 * BQ]
                if fox:
                    z = _dot_nt(q, k) * SCALE + (cq - ck_ref[:, pl.ds(k0, BQ)])
                    z = jnp.where((k0 + col) <= (q0 + row), z, NEG)
                else:
                    z = _dot_nt(q, k) * SCALE + tbl_ref[i - j]
                return k0, k, jnp.exp(z - lse_q), _dot_nt(do, v)

            if fox:
                def dblock(j, acc):
                    _, _, p, dp = probs(j)
                    return acc + jnp.sum(p * dp, axis=1, keepdims=True)

                delta = lax.fori_loop(0, i + 1, dblock, jnp.zeros((BQ, 1), F32))
            else:
                delta = jnp.sum(do.astype(F32) * o_ref[pl.ds(q0, BQ), :].astype(F32), axis=1, keepdims=True)

            def kblock(j, dq):
                k0, k, p, dp = probs(j)
                dz = p * (dp - delta)
                if fox:
                    db_ref[:, pl.ds(k0, BQ)] = db_ref[:, pl.ds(k0, BQ)] - jnp.sum(dz, axis=0, keepdims=True)
                else:
                    db_ref[i - j] = db_ref[i - j] + dz
                dzb = (dz * SCALE).astype(BF16)
                dk_s[pl.ds(k0, BQ), :] += _dot_tn(dzb, q)
                dv_s[pl.ds(k0, BQ), :] += _dot_tn(p.astype(BF16), do)
                return dq + _dot(dzb, k)

            dq = lax.fori_loop(0, i + 1, kblock, jnp.zeros((BQ, BQ), F32))
            dq_s[pl.ds(q0, BQ), :] += dq * maskf
            return 0

        lax.fori_loop(0, NB, qblock, 0)

        @pl.when(hh == 1)
        def _():
            dqkv_ref[:, 0:BQ] = dq_s[...].astype(BF16)
            dqkv_ref[:, BQ:2 * BQ] = dk_s[...].astype(BF16)
            dqkv_ref[:, 2 * BQ:3 * BQ] = dv_s[...].astype(BF16)

    if fox:
        bias_specs = [_stat_spec(), pl.BlockSpec((None, 1, S), lambda b, h: (b * NH + h, 0, 0))]
        db_spec = pl.BlockSpec((None, 1, S), lambda b, h: (b * NH + h, 0, 0))
        db_shape = jax.ShapeDtypeStruct((BL * NH, 1, S), F32)
    else:
        bias_specs = [pl.BlockSpec((None, NB, BQ, BQ), lambda b, h: (h, 0, 0, 0))]
        db_spec = pl.BlockSpec((None, NB, BQ, BQ), lambda b, h: (b * NH + h, 0, 0, 0))
        db_shape = jax.ShapeDtypeStruct((BL * NH, NB, BQ, BQ), F32)
    n_in = 5 + len(bias_specs)
    return pl.pallas_call(
        body, name="fox_bwd" if fox else "dil_bwd", grid=(BL, NH),
        in_specs=[_qkv_spec(g), _pair_spec(g), _pair_spec(g), _stat_spec()] + bias_specs + [ANY_SPEC],
        out_specs=[_qkv_spec(g), db_spec],
        out_shape=[jax.ShapeDtypeStruct((T, QKVW), BF16), db_shape],
        scratch_shapes=[pltpu.VMEM((S, BQ), F32)] * 3,
        input_output_aliases={n_in - 1: 0},
        compiler_params=_cp(("parallel", "arbitrary")))(qkv, mixed, dmixed, lse, *bias, dqkv)


def _delta_tile(d):
    return d * BQ + _iota((BQ, BQ), 0) - _iota((BQ, BQ), 1)


def _bucket_tile(delta):
    b = jnp.zeros((BQ, BQ), jnp.int32)
    for th in BUCKET_TH[1:]:
        b = b + (delta >= th).astype(jnp.int32)
    return b


def _dil_table(rel_bias):
    def body(rb_ref, o_ref):
        for d in range(NB):
            delta = _delta_tile(d)
            bucket = _bucket_tile(delta)
            pos = delta >= 0
            n = ((pos & (delta <= 128)).astype(jnp.int32)
                 + (pos & (delta <= 512) & ((delta & 3) == 0)).astype(jnp.int32)
                 + (pos & ((delta & 15) == 0)).astype(jnp.int32))
            logn = jnp.where(n == 3, math.log(3.0), jnp.where(n == 2, math.log(2.0), jnp.where(n == 1, 0.0, NEG)))
            for h in range(NH):
                val = lax.fori_loop(0, 32, lambda b, acc: jnp.where(bucket == b, rb_ref[b, h], acc),
                                    jnp.zeros((BQ, BQ), F32))
                o_ref[h, d] = val + logn

    return pl.pallas_call(
        body, name="dil_table", in_specs=[pl.BlockSpec(memory_space=pltpu.SMEM)],
        out_specs=pl.BlockSpec(memory_space=pltpu.VMEM),
        out_shape=jax.ShapeDtypeStruct((NH, NB, BQ, BQ), F32), compiler_params=_cp())(rel_bias)


def _dil_table_bwd(dtbl):
    def body(dt_ref, o_ref):
        h = pl.program_id(0)
        rowi = _iota((32, BQ), 0)
        lanei = _iota((32, BQ), 1)

        @pl.when(h == 0)
        def _():
            o_ref[...] = jnp.zeros_like(o_ref)

        acc = jnp.zeros((32, BQ), F32)
        for d in range(NB):
            tile = dt_ref[0, d] + dt_ref[1, d]
            bucket = _bucket_tile(_delta_tile(d))

            def bb(b, acc):
                s = jnp.sum(jnp.sum(jnp.where(bucket == b, tile, 0.0), axis=1, keepdims=True), axis=0, keepdims=True)
                return acc + jnp.where((rowi == b) & (lanei == h), s, 0.0)

            acc = lax.fori_loop(0, 32, bb, acc)
        o_ref[...] += acc

    return pl.pallas_call(
        body, name="dil_table_bwd", grid=(NH,),
        in_specs=[pl.BlockSpec((BL, None, NB, BQ, BQ), lambda h: (0, h, 0, 0, 0))],
        out_specs=pl.BlockSpec((32, BQ), lambda h: (0, 0)),
        out_shape=jax.ShapeDtypeStruct((32, BQ), F32),
        compiler_params=_cp(("arbitrary",)))(dtbl.reshape(BL, NH, NB, BQ, BQ))


def _fox_prep(gate, fb):
    def body(g_ref, fb_ref, c_ref):
        tri = (_iota((BQ, BQ), 0) >= _iota((BQ, BQ), 1)).astype(BF16)

        def blk(i, carry):
            r0 = pl.multiple_of(i * BQ, BQ)
            lf = _log_sigmoid(g_ref[pl.ds(r0, BQ), :] + fb_ref[...])
            c = _dot(tri, _split3(lf))
            c_ref[pl.ds(r0, BQ), :] = c[:, 0:BQ] + c[:, BQ:2 * BQ] + c[:, 2 * BQ:3 * BQ] + carry
            return carry + jnp.sum(lf, axis=0, keepdims=True)

        lax.fori_loop(0, NB, blk, jnp.zeros((1, BQ), F32))

    blk = pl.BlockSpec((S, GATEW), lambda b: (b, 0))
    return pl.pallas_call(
        body, name="fox_prep", grid=(BL,), in_specs=[blk, pl.BlockSpec((1, GATEW), lambda b: (0, 0))],
        out_specs=blk, out_shape=jax.ShapeDtypeStruct((T, GATEW), F32),
        compiler_params=_cp(("parallel",)))(gate, fb)


def _fox_post(dcum, gate, fb):
    def body(dc_ref, g_ref, fb_ref, dg_ref, dfb_ref):
        b = pl.program_id(0)
        tri = (_iota((BQ, BQ), 0) <= _iota((BQ, BQ), 1)).astype(BF16)

        def blk(ii, carry):
            csum, dfb = carry
            r0 = pl.multiple_of((NB - 1 - ii) * BQ, BQ)
            dc = dc_ref[pl.ds(r0, BQ), :]
            c = _dot(tri, _split3(dc))
            dlf = c[:, 0:BQ] + c[:, BQ:2 * BQ] + c[:, 2 * BQ:3 * BQ] + csum
            dg = dlf * jnp.exp(_log_sigmoid(-(g_ref[pl.ds(r0, BQ), :] + fb_ref[...])))
            dg_ref[pl.ds(r0, BQ), :] = dg
            return csum + jnp.sum(dc, axis=0, keepdims=True), dfb + jnp.sum(dg, axis=0, keepdims=True)

        z = jnp.zeros((1, BQ), F32)
        _, dfb = lax.fori_loop(0, NB, blk, (z, z))

        @pl.when(b == 0)
        def _():
            dfb_ref[...] = dfb

        @pl.when(b > 0)
        def _():
            dfb_ref[...] += dfb

    blk = pl.BlockSpec((S, GATEW), lambda b: (b, 0))
    vec = pl.BlockSpec((1, GATEW), lambda b: (0, 0))
    return pl.pallas_call(
        body, name="fox_post", grid=(BL,), in_specs=[blk, blk, vec], out_specs=[blk, vec],
        out_shape=[jax.ShapeDtypeStruct((T, GATEW), F32), jax.ShapeDtypeStruct((1, GATEW), F32)],
        compiler_params=_cp(("arbitrary",)))(dcum, gate, fb)


HB = 2 * BQ


def _lane_masks():
    lane = _iota((1, BQ), 1)
    m0 = (lane < 64).astype(BF16)
    return m0, 1.0 - m0


def _stack(x, m0, m1):
    return jnp.concatenate([x * m0, x * m1], axis=0)


def _lanes(a):
    return jnp.concatenate([a[:BQ], a[BQ:]], axis=1)


def _per_lane(v):
    return jnp.where(_iota((BQ, BQ), 1) < 64, v[:BQ], v[BQ:])


def _pairs_spec(g):
    return pl.BlockSpec((T, PAIRW), lambda p: (0, 2 * g + p))


def _pairo_spec(g):
    return pl.BlockSpec((T, BQ), lambda p: (0, 2 * g + p))


def _stats_spec():
    return pl.BlockSpec((BL, None, 2, S, 1), lambda p: (0, p, 0, 0, 0))


def _keys_spec():
    return pl.BlockSpec((BL, None, 2, 1, S), lambda p: (0, p, 0, 0, 0))


def _tbl_spec():
    return pl.BlockSpec((None, NB, HB, BQ), lambda p: (p, 0, 0, 0))


def _load_kv(qkv_ref, b, j):
    k0 = pl.multiple_of(b * S + j * BQ, BQ)
    return k0, qkv_ref[pl.ds(k0, BQ), BQ:2 * BQ], qkv_ref[pl.ds(k0, BQ), 2 * BQ:3 * BQ]


def _diag_valid(strict):
    r = _iota((HB, BQ), 0) & (BQ - 1)
    c = _iota((HB, BQ), 1)
    return (c < r) if strict else (c <= r)


def _sb_fwd2(qkv):
    def body(qkv_ref, o_ref, r_ref):
        m0, m1 = _lane_masks()
        valid = _diag_valid(True)
        r2 = _iota((HB, BQ), 0) & (BQ - 1)
        u2 = (r2 > _iota((HB, BQ), 1)).astype(BF16)

        def step(b, qs, j, carry, diag):
            tail_c, acc = carry
            _, k, v = _load_kv(qkv_ref, b, j)
            z = _dot_nt(qs, k)
            lb = _log_sigmoid(z)
            lr = lb - z
            if diag:
                lr = jnp.where(valid, lr, 0.0)
            a = jnp.exp(lb + _dot(_split2(lr), u2) + tail_c)
            if diag:
                a = jnp.where(valid, a, 0.0)
            acc = acc + _dot(_lanes(a.astype(BF16)), _stack(v, m0, m1))
            return tail_c + jnp.sum(lr, axis=1, keepdims=True), acc

        def qblock(i, _):
            qs = []
            for b in range(BL):
                q0 = pl.multiple_of(b * S + i * BQ, BQ)
                qs.append(_stack(qkv_ref[pl.ds(q0, BQ), 0:BQ] * SCALE, m0, m1))
            zero = (jnp.zeros((HB, 1), F32), jnp.zeros((BQ, BQ), F32))
            carries = tuple(step(b, qs[b], i, zero, True) for b in range(BL))
            carries = lax.fori_loop(
                1, i + 1, lambda jj, cs: tuple(step(b, qs[b], i - jj, cs[b], False) for b in range(BL)), carries)
            for b in range(BL):
                tail_c, acc = carries[b]
                o_ref[pl.ds(pl.multiple_of(b * S + i * BQ, BQ), BQ), :] = acc.astype(BF16)
                r0 = pl.multiple_of(i * BQ, BQ)
                r_ref[b, 0, pl.ds(r0, BQ), :] = tail_c[:BQ]
                r_ref[b, 1, pl.ds(r0, BQ), :] = tail_c[BQ:]
            return 0

        lax.fori_loop(0, NB, qblock, 0)

    mixed, rtot = pl.pallas_call(
        body, name="sb_fwd", grid=(2,), in_specs=[_pairs_spec(0)],
        out_specs=[_pairo_spec(0), _stats_spec()],
        out_shape=[jax.ShapeDtypeStruct((T, D), BF16), jax.ShapeDtypeStruct((BL, 2, 2, S, 1), F32)],
        compiler_params=_cp(("parallel",)))(qkv)
    return mixed, rtot


def _sb_bwd2(qkv, dmixed, rtot):
    def body(qkv_ref, do_ref, r_ref, dqkv_ref, dk_s, dv_s):
        m0, m1 = _lane_masks()
        valid = _diag_valid(True)
        r2 = _iota((HB, BQ), 0) & (BQ - 1)
        c2 = _iota((HB, BQ), 1)
        u2 = (r2 > c2).astype(BF16)
        l2 = (r2 < c2).astype(BF16)
        dk_s[...] = jnp.zeros_like(dk_s)
        dv_s[...] = jnp.zeros_like(dv_s)

        def step(b, qs, dos, rt, j, carry, diag):
            pre_l, pre_g, dq = carry
            k0, k, v = _load_kv(qkv_ref, b, j)
            z = _dot_nt(qs, k)
            lb = _log_sigmoid(z)
            lr = lb - z
            if diag:
                lr = jnp.where(valid, lr, 0.0)
            pre_l = pre_l + jnp.sum(lr, axis=1, keepdims=True)
            a = jnp.exp(lb + _dot(_split2(lr), u2) + (rt - pre_l))
            if diag:
                a = jnp.where(valid, a, 0.0)
            gm = _dot_nt(dos, v) * a
            before = _dot(_split2(gm), l2) + pre_g
            beta = jnp.exp(lb)
            dz = gm * (1.0 - beta) - beta * before
            if diag:
                dz = jnp.where(valid, dz, 0.0)
            dzb = dz.astype(BF16)
            dq = dq + _dot(_lanes(dzb), _stack(k, m0, m1))
            dk_s[pl.ds(k0, BQ), :] += _dot_tn(dzb, qs)
            dv_s[pl.ds(k0, BQ), :] += _dot_tn(a.astype(BF16), dos)
            return pre_l, pre_g + jnp.sum(gm, axis=1, keepdims=True), dq

        def qblock(i, _):
            qs, dos, rts = [], [], []
            for b in range(BL):
                q0 = pl.multiple_of(b * S + i * BQ, BQ)
                r0 = pl.multiple_of(i * BQ, BQ)
                qs.append(_stack(qkv_ref[pl.ds(q0, BQ), 0:BQ] * SCALE, m0, m1))
                dos.append(_stack(do_ref[pl.ds(q0, BQ), :], m0, m1))
                rts.append(jnp.concatenate([r_ref[b, 0, pl.ds(r0, BQ), :], r_ref[b, 1, pl.ds(r0, BQ), :]], axis=0))
            z1 = jnp.zeros((HB, 1), F32)
            carries = tuple((z1, z1, jnp.zeros((BQ, BQ), F32)) for _ in range(BL))
            carries = lax.fori_loop(
                0, i, lambda j, cs: tuple(step(b, qs[b], dos[b], rts[b], j, cs[b], False) for b in range(BL)), carries)
            for b in range(BL):
                _, _, dq = step(b, qs[b], dos[b], rts[b], i, carries[b], True)
                dqkv_ref[pl.ds(pl.multiple_of(b * S + i * BQ, BQ), BQ), 0:BQ] = (dq * SCALE).astype(BF16)
            return 0

        lax.fori_loop(0, NB, qblock, 0)
        dqkv_ref[:, BQ:2 * BQ] = dk_s[...].astype(BF16)
        dqkv_ref[:, 2 * BQ:3 * BQ] = dv_s[...].astype(BF16)

    return pl.pallas_call(
        body, name="sb_bwd", grid=(2,),
        in_specs=[_pairs_spec(0), _pairo_spec(0), _stats_spec()],
        out_specs=_pairs_spec(0),
        out_shape=jax.ShapeDtypeStruct((T, QKVW), BF16),
        scratch_shapes=[pltpu.VMEM((T, BQ), F32)] * 2,
        compiler_params=_cp(("parallel",)))(qkv, dmixed, rtot)


def _flash_fwd2(qkv, mixed, g, fox, bias):
    def body(*refs):
        if fox:
            qkv_ref, cq_ref, ck_ref, _, o_ref, lse_ref = refs
        else:
            qkv_ref, tbl_ref, _, o_ref, lse_ref = refs
        m0, m1 = _lane_masks()
        valid = _diag_valid(False)

        def step(b, qs, cq, i, j, carry, diag):
            m, l, acc = carry
            k0, k, v = _load_kv(qkv_ref, b, j)
            z = _dot_nt(qs, k)
            if fox:
                kk = pl.multiple_of(j * BQ, BQ)
                ck = jnp.concatenate([jnp.broadcast_to(ck_ref[b, 0, :, pl.ds(kk, BQ)], (BQ, BQ)),
                                      jnp.broadcast_to(ck_ref[b, 1, :, pl.ds(kk, BQ)], (BQ, BQ))], axis=0)
                z = z + (cq - ck)
                if diag:
                    z = jnp.where(valid, z, NEG)
            else:
                z = z + tbl_ref[i - j]
            m_new = jnp.maximum(m, jnp.max(z, axis=1, keepdims=True))
            alpha = jnp.exp(m - m_new)
            p = jnp.exp(z - m_new)
            l = alpha * l + jnp.sum(p, axis=1, keepdims=True)
            acc = _per_lane(alpha) * acc + _dot(_lanes(p.astype(BF16)), _stack(v, m0, m1))
            return m_new, l, acc

        def qblock(i, _):
            qs, cqs = [], []
            r0 = pl.multiple_of(i * BQ, BQ)
            for b in range(BL):
                q0 = pl.multiple_of(b * S + i * BQ, BQ)
                qs.append(_stack(qkv_ref[pl.ds(q0, BQ), 0:BQ] * SCALE, m0, m1))
                cqs.append(jnp.concatenate([cq_ref[b, 0, pl.ds(r0, BQ), :], cq_ref[b, 1, pl.ds(r0, BQ), :]], axis=0)
                           if fox else None)
            zero = (jnp.full((HB, 1), NEG, F32), jnp.zeros((HB, 1), F32), jnp.zeros((BQ, BQ), F32))
            carries = tuple(step(b, qs[b], cqs[b], i, i, zero, True) for b in range(BL))
            carries = lax.fori_loop(
                1, i + 1,
                lambda jj, cs: tuple(step(b, qs[b], cqs[b], i, i - jj, cs[b], False) for b in range(BL)), carries)
            for b in range(BL):
                m, l, acc = carries[b]
                o_ref[pl.ds(pl.multiple_of(b * S + i * BQ, BQ), BQ), :] = (acc / _per_lane(l)).astype(BF16)
                lse = m + jnp.log(l)
                lse_ref[b, 0, pl.ds(r0, BQ), :] = lse[:BQ]
                lse_ref[b, 1, pl.ds(r0, BQ), :] = lse[BQ:]
            return 0

        lax.fori_loop(0, NB, qblock, 0)

    bias_specs = [_stats_spec(), _keys_spec()] if fox else [_tbl_spec()]
    n_in = 2 + len(bias_specs)
    return pl.pallas_call(
        body, name="fox_fwd" if fox else "dil_fwd", grid=(2,),
        in_specs=[_pairs_spec(g)] + bias_specs + [ANY_SPEC],
        out_specs=[_pairo_spec(g), _stats_spec()],
        out_shape=[jax.ShapeDtypeStruct((T, D), BF16), jax.ShapeDtypeStruct((BL, 2, 2, S, 1), F32)],
        input_output_aliases={n_in - 1: 0},
        compiler_params=_cp(("parallel",)))(qkv, *bias, mixed)


def _flash_bwd2(qkv, mixed, dmixed, lse, dqkv, g, fox, bias):
    def body(*refs):
        if fox:
            qkv_ref, o_ref, do_ref, lse_ref, cq_ref, ck_ref, _, dqkv_ref, db_ref, dk_s, dv_s = refs
        else:
            qkv_ref, o_ref, do_ref, lse_ref, tbl_ref, _, dqkv_ref, db_ref, dk_s, dv_s = refs
        m0, m1 = _lane_masks()
        valid = _diag_valid(False)
        dk_s[...] = jnp.zeros_like(dk_s)
        dv_s[...] = jnp.zeros_like(dv_s)
        db_ref[...] = jnp.zeros_like(db_ref)

        def probs(b, qs, dos, cq, lse_q, i, j, diag):
            k0, k, v = _load_kv(qkv_ref, b, j)
            z = _dot_nt(qs, k)
            if fox:
                kk = pl.multiple_of(j * BQ, BQ)
                ck = jnp.concatenate([jnp.broadcast_to(ck_ref[b, 0, :, pl.ds(kk, BQ)], (BQ, BQ)),
                                      jnp.broadcast_to(ck_ref[b, 1, :, pl.ds(kk, BQ)], (BQ, BQ))], axis=0)
                z = z + (cq - ck)
                if diag:
                    z = jnp.where(valid, z, NEG)
            else:
                z = z + tbl_ref[i - j]
            return k0, k, jnp.exp(z - lse_q), _dot_nt(dos, v)

        def step(b, qs, dos, cq, lse_q, delta, i, j, dq, diag):
            k0, k, p, dp = probs(b, qs, dos, cq, lse_q, i, j, diag)
            dz = p * (dp - delta)
            dzb = dz.astype(BF16)
            dk_s[pl.ds(k0, BQ), :] += _dot_tn(dzb, qs)
            dv_s[pl.ds(k0, BQ), :] += _dot_tn(p.astype(BF16), dos)
            return dz, dq + _dot(_lanes(dzb), _stack(k, m0, m1))

        def qblock(i, _):
            r0 = pl.multiple_of(i * BQ, BQ)
            qs, dos, cqs, lses, deltas = [], [], [], [], []
            for b in range(BL):
                q0 = pl.multiple_of(b * S + i * BQ, BQ)
                qs.append(_stack(qkv_ref[pl.ds(q0, BQ), 0:BQ] * SCALE, m0, m1))
                dos.append(_stack(do_ref[pl.ds(q0, BQ), :], m0, m1))
                lses.append(jnp.concatenate([lse_ref[b, 0, pl.ds(r0, BQ), :], lse_ref[b, 1, pl.ds(r0, BQ), :]], axis=0))
                if fox:
                    cqs.append(jnp.concatenate([cq_ref[b, 0, pl.ds(r0, BQ), :], cq_ref[b, 1, pl.ds(r0, BQ), :]], axis=0))
                else:
                    cqs.append(None)
                    ob = o_ref[pl.ds(q0, BQ), :].astype(F32)
                    deltas.append(jnp.sum(dos[b].astype(F32) * jnp.concatenate([ob, ob], axis=0), axis=1, keepdims=True))

            if fox:
                def dstep(b, j, acc, diag):
                    _, _, p, dp = probs(b, qs[b], dos[b], cqs[b], lses[b], i, j, diag)
                    return acc + jnp.sum(p * dp, axis=1, keepdims=True)

                deltas = lax.fori_loop(
                    0, i, lambda j, ds: tuple(dstep(b, j, ds[b], False) for b in range(BL)),
                    tuple(jnp.zeros((HB, 1), F32) for _ in range(BL)))
                deltas = [dstep(b, i, deltas[b], True) for b in range(BL)]

            def inner(j, dqs, diag):
                outs = [step(b, qs[b], dos[b], cqs[b], lses[b], deltas[b], i, j, dqs[b], diag) for b in range(BL)]
                if fox:
                    kk = pl.multiple_of(j * BQ, BQ)
                    for b in range(BL):
                        dz = outs[b][0]
                        db_ref[b, 0, :, pl.ds(kk, BQ)] = db_ref[b, 0, :, pl.ds(kk, BQ)] - jnp.sum(dz[:BQ], axis=0, keepdims=True)
                        db_ref[b, 1, :, pl.ds(kk, BQ)] = db_ref[b, 1, :, pl.ds(kk, BQ)] - jnp.sum(dz[BQ:], axis=0, keepdims=True)
                else:
                    db_ref[i - j] = db_ref[i - j] + (outs[0][0] + outs[1][0])
                return tuple(o[1] for o in outs)

            dqs = tuple(jnp.zeros((BQ, BQ), F32) for _ in range(BL))
            dqs = lax.fori_loop(0, i, lambda j, d: inner(j, d, False), dqs)
            dqs = inner(i, dqs, True)
            for b in range(BL):
                dqkv_ref[pl.ds(pl.multiple_of(b * S + i * BQ, BQ), BQ), 0:BQ] = (dqs[b] * SCALE).astype(BF16)
            return 0

        lax.fori_loop(0, NB, qblock, 0)
        dqkv_ref[:, BQ:2 * BQ] = dk_s[...].astype(BF16)
        dqkv_ref[:, 2 * BQ:3 * BQ] = dv_s[...].astype(BF16)

    if fox:
        bias_specs = [_stats_spec(), _keys_spec()]
        db_spec = _keys_spec()
        db_shape = jax.ShapeDtypeStruct((BL, 2, 2, 1, S), F32)
    else:
        bias_specs = [_tbl_spec()]
        db_spec = _tbl_spec()
        db_shape = jax.ShapeDtypeStruct((2, NB, HB, BQ), F32)
    n_in = 5 + len(bias_specs)
    return pl.pallas_call(
        body, name="fox_bwd" if fox else "dil_bwd", grid=(2,),
        in_specs=[_pairs_spec(g), _pairo_spec(g), _pairo_spec(g), _stats_spec()] + bias_specs + [ANY_SPEC],
        out_specs=[_pairs_spec(g), db_spec],
        out_shape=[jax.ShapeDtypeStruct((T, QKVW), BF16), db_shape],
        scratch_shapes=[pltpu.VMEM((T, BQ), F32)] * 2,
        input_output_aliases={n_in - 1: 0},
        compiler_params=_cp(("parallel",)))(qkv, mixed, dmixed, lse, *bias, dqkv)


def _stacked_delta(d):
    return d * BQ + (_iota((HB, BQ), 0) & (BQ - 1)) - _iota((HB, BQ), 1)


def _buckets_in(d):
    lo, hi = max(d * BQ - (BQ - 1), 0), d * BQ + BQ - 1
    return [b for b in range(32) if BUCKET_TH[b] <= hi and (b == 31 or BUCKET_TH[b + 1] > lo)]


def _in_bucket(delta, b):
    m = delta >= BUCKET_TH[b]
    return m if b == 31 else m & (delta < BUCKET_TH[b + 1])


def _dil_table2(rel_bias):
    def body(rb_ref, o_ref):
        for d in range(NB):
            delta = _stacked_delta(d)
            pos = delta >= 0
            n = ((pos & (delta <= 128)).astype(jnp.int32)
                 + (pos & (delta <= 512) & ((delta & 3) == 0)).astype(jnp.int32)
                 + (pos & ((delta & 15) == 0)).astype(jnp.int32))
            logn = jnp.where(n == 3, math.log(3.0), jnp.where(n == 2, math.log(2.0), jnp.where(n == 1, 0.0, NEG)))
            head1 = _iota((HB, BQ), 0) >= BQ
            for p in range(2):
                val = jnp.zeros((HB, BQ), F32)
                for b in _buckets_in(d):
                    val = jnp.where(_in_bucket(delta, b), jnp.where(head1, rb_ref[b, 2 * p + 1], rb_ref[b, 2 * p]), val)
                o_ref[p, d] = val + logn

    return pl.pallas_call(
        body, name="dil_table", in_specs=[pl.BlockSpec(memory_space=pltpu.SMEM)],
        out_specs=pl.BlockSpec(memory_space=pltpu.VMEM),
        out_shape=jax.ShapeDtypeStruct((2, NB, HB, BQ), F32), compiler_params=_cp())(rel_bias)


def _dil_table_bwd2(dtbl):
    def body(dt_ref, o_ref):
        p = pl.program_id(0)
        rowi = _iota((32, BQ), 0)
        lanei = _iota((32, BQ), 1)

        @pl.when(p == 0)
        def _():
            o_ref[...] = jnp.zeros_like(o_ref)

        out = jnp.zeros((32, BQ), F32)
        for b in range(32):
            acc = None
            for d in range(NB):
                if b in _buckets_in(d):
                    t = jnp.where(_in_bucket(_stacked_delta(d), b), dt_ref[d], 0.0)
                    acc = t if acc is None else acc + t
            rs = jnp.sum(acc, axis=1, keepdims=True)
            s0 = jnp.sum(rs[:BQ], axis=0, keepdims=True)
            s1 = jnp.sum(rs[BQ:], axis=0, keepdims=True)
            out = out + jnp.where((rowi == b) & (lanei == 2 * p), s0, 0.0) + jnp.where((rowi == b) & (lanei == 2 * p + 1), s1, 0.0)
        o_ref[...] += out

    return pl.pallas_call(
        body, name="dil_table_bwd", grid=(2,), in_specs=[_tbl_spec()],
        out_specs=pl.BlockSpec((32, BQ), lambda p: (0, 0)),
        out_shape=jax.ShapeDtypeStruct((32, BQ), F32),
        compiler_params=_cp(("arbitrary",)))(dtbl)


def _shift_down(x, n):
    return jnp.where(_iota(x.shape, 0) >= n, pltpu.roll(x, n, 0), 0.0)


def _shift_up(x, n):
    return jnp.where(_iota(x.shape, 0) < S - n, pltpu.roll(x, S - n, 0), 0.0)


def _conv_fwd(conv, cw, mixed):
    W = 256

    def body(c_ref, w_ref, _, o_ref):
        u = c_ref[:, W:2 * W] * c_ref[:, 2 * W:3 * W]
        y = w_ref[0:1, :] * _shift_down(u, 2) + w_ref[1:2, :] * _shift_down(u, 1) + w_ref[2:3, :] * u
        o_ref[...] = (c_ref[:, 0:W] * y).astype(BF16)

    return pl.pallas_call(
        body, name="conv_fwd", grid=(BL,),
        in_specs=[pl.BlockSpec((S, CONVW), lambda b: (b, 0)), pl.BlockSpec((8, W), lambda b: (0, 0)), ANY_SPEC],
        out_specs=pl.BlockSpec((S, W), lambda b: (b, 3)),
        out_shape=jax.ShapeDtypeStruct((T, D), BF16), input_output_aliases={2: 0},
        compiler_params=_cp(("parallel",)))(conv, cw, mixed)


def _conv_bwd(conv, cw, dmixed):
    W = 256

    def body(c_ref, w_ref, do_ref, dc_ref, dw_ref):
        b = pl.program_id(0)
        bg = c_ref[:, 0:W]
        cg = c_ref[:, W:2 * W]
        hv = c_ref[:, 2 * W:3 * W]
        do = do_ref[...].astype(F32)
        u = cg * hv
        u1 = _shift_down(u, 1)
        u2 = _shift_down(u, 2)
        y = w_ref[0:1, :] * u2 + w_ref[1:2, :] * u1 + w_ref[2:3, :] * u
        dy = do * bg
        du = w_ref[2:3, :] * dy + w_ref[1:2, :] * _shift_up(dy, 1) + w_ref[0:1, :] * _shift_up(dy, 2)
        dc_ref[:, 0:W] = (do * y).astype(BF16)
        dc_ref[:, W:2 * W] = (du * hv).astype(BF16)
        dc_ref[:, 2 * W:3 * W] = (du * cg).astype(BF16)
        rowi = _iota((8, W), 0)
        dw = (jnp.where(rowi == 0, jnp.sum(dy * u2, axis=0, keepdims=True), 0.0)
              + jnp.where(rowi == 1, jnp.sum(dy * u1, axis=0, keepdims=True), 0.0)
              + jnp.where(rowi == 2, jnp.sum(dy * u, axis=0, keepdims=True), 0.0))

        @pl.when(b == 0)
        def _():
            dw_ref[...] = dw

        @pl.when(b > 0)
        def _():
            dw_ref[...] += dw

    return pl.pallas_call(
        body, name="conv_bwd", grid=(BL,),
        in_specs=[pl.BlockSpec((S, CONVW), lambda b: (b, 0)), pl.BlockSpec((8, W), lambda b: (0, 0)),
                  pl.BlockSpec((S, W), lambda b: (b, 3))],
        out_specs=[pl.BlockSpec((S, CONVW), lambda b: (b, 0)), pl.BlockSpec((8, W), lambda b: (0, 0))],
        out_shape=[jax.ShapeDtypeStruct((T, CONVW), BF16), jax.ShapeDtypeStruct((8, W), F32)],
        compiler_params=_cp(("arbitrary",)))(conv, cw, dmixed)


def _place():
    x, y, c = lax.axis_index("x"), lax.axis_index("y"), lax.axis_index("c")
    return x, y, c


def _allgather_weights(shards):
    n = len(shards)

    def body(*refs):
        ins, outs = refs[:n], refs[n:2 * n]
        send_sems, recv_sems, local_sems = refs[2 * n:]
        x, y, c = _place()
        me, sibling = (x, y, c), (x, y, 1 - c)
        chips = [(1 - x, y), (x, 1 - y), (1 - x, 1 - y)]

        def slot(a, p):
            return outs[a].at[:, 4 * p[0] + 2 * p[1] + p[2]]

        def copy(a, k, block, to, own=False):
            return pltpu.make_async_remote_copy(
                src_ref=ins[a] if own else slot(a, block), dst_ref=slot(a, block),
                send_sem=send_sems.at[a, k], recv_sem=recv_sems.at[a, k], device_id=to, device_id_type=MESH)

        mine = [pltpu.make_async_copy(ins[a], slot(a, me), local_sems.at[a]) for a in range(n)]
        for cp in mine:
            cp.start()
        first = []
        for a in range(n):
            first.append(copy(a, 0, me, sibling, own=True))
            first += [copy(a, 1 + j, me, (*chip, c), own=True) for j, chip in enumerate(chips)]
        for cp in first:
            cp.start()
        passed = []
        for j, chip in enumerate(chips):
            for a in range(n):
                copy(a, 1 + j, (*chip, c), me).wait_recv()
                cp = copy(a, 4 + j, (*chip, c), sibling)
                cp.start()
                passed.append(cp)
        for a in range(n):
            copy(a, 0, sibling, me).wait_recv()
            for j, chip in enumerate(chips):
                copy(a, 4 + j, (*chip, 1 - c), me).wait_recv()
        for cp in first + passed:
            cp.wait_send()
        for cp in mine:
            cp.wait()

    return pl.pallas_call(
        body, name="allgather_weights", in_specs=[ANY_SPEC] * n, out_specs=[ANY_SPEC] * n,
        out_shape=[jax.ShapeDtypeStruct((s.shape[0], NDEV) + s.shape[1:], s.dtype) for s in shards],
        scratch_shapes=[pltpu.SemaphoreType.DMA((n, 7)), pltpu.SemaphoreType.DMA((n, 7)),
                        pltpu.SemaphoreType.DMA((n,))],
        )(*shards)


def _allreduce_small(v):
    def body(v_ref, o_ref, slots, send_sems, recv_sems):
        x, y, c = _place()
        me = 4 * x + 2 * y + c
        slots[me] = v_ref[...]

        def copy(k):
            peer = (x ^ ((k >> 2) & 1), y ^ ((k >> 1) & 1), c ^ (k & 1))
            return pltpu.make_async_remote_copy(
                src_ref=v_ref, dst_ref=slots.at[me], send_sem=send_sems.at[k - 1], recv_sem=recv_sems.at[k - 1],
                device_id=peer, device_id_type=MESH)

        def arrival(k):
            return pltpu.make_async_remote_copy(
                src_ref=v_ref, dst_ref=slots.at[me ^ k], send_sem=send_sems.at[k - 1], recv_sem=recv_sems.at[k - 1],
                device_id=(x, y, c), device_id_type=MESH)

        sends = [copy(k) for k in range(1, NDEV)]
        for cp in sends:
            cp.start()
        for k in range(1, NDEV):
            arrival(k).wait_recv()
        for cp in sends:
            cp.wait_send()
        acc = slots[0]
        for d in range(1, NDEV):
            acc = acc + slots[d]
        o_ref[...] = acc

    return pl.pallas_call(
        body, name="allreduce_small",
        in_specs=[pl.BlockSpec(memory_space=pltpu.VMEM)], out_specs=pl.BlockSpec(memory_space=pltpu.VMEM),
        out_shape=jax.ShapeDtypeStruct(v.shape, F32),
        scratch_shapes=[pltpu.VMEM((NDEV,) + v.shape, F32), pltpu.SemaphoreType.DMA((NDEV - 1,)),
                        pltpu.SemaphoreType.DMA((NDEV - 1,))],
        )(v)


def _sibling_exchange(grads):
    n = len(grads)

    def body(*refs):
        ins, outs = refs[:n], refs[n:2 * n]
        send_sems, recv_sems = refs[2 * n:]
        x, y, c = _place()
        cps = [pltpu.make_async_remote_copy(
            src_ref=ins[a].at[:, :, 1 - c], dst_ref=outs[a], send_sem=send_sems.at[a], recv_sem=recv_sems.at[a],
            device_id=(x, y, 1 - c), device_id_type=MESH) for a in range(n)]
        for cp in cps:
            cp.start()
        for cp in cps:
            cp.wait()

    return pl.pallas_call(
        body, name="sibling_exchange", in_specs=[ANY_SPEC] * n, out_specs=[ANY_SPEC] * n,
        out_shape=[jax.ShapeDtypeStruct(g.shape[:2] + g.shape[3:], F32) for g in grads],
        scratch_shapes=[pltpu.SemaphoreType.DMA((n,)), pltpu.SemaphoreType.DMA((n,))],
        )(*grads)


def _pair_sum(grad, got, core):
    _, _, _, rows, N = grad.shape

    def body(c_ref, g_ref, r_ref, o_ref):
        o_ref[...] = (g_ref[...] + r_ref[...]).astype(BF16)

    return pl.pallas_call(
        body, name="pair_sum",
        grid_spec=pltpu.PrefetchScalarGridSpec(
            num_scalar_prefetch=1, grid=(2, 4),
            in_specs=[pl.BlockSpec((None, None, None, rows, N), lambda l, k, c: (l, k, c[0], 0, 0)),
                      pl.BlockSpec((None, None, rows, N), lambda l, k, c: (l, k, 0, 0))],
            out_specs=pl.BlockSpec((None, None, rows, N), lambda l, k, c: (l, k, 0, 0))),
        out_shape=jax.ShapeDtypeStruct((2, 4, rows, N), BF16),
        compiler_params=_cp(("parallel", "parallel")))(core, grad, got)


def _chip_exchange(psums):
    n = len(psums)

    def body(*refs):
        ins, outs = refs[:n], refs[n:2 * n]
        send_sems, recv_sems, local_sems = refs[2 * n:]
        x, y, c = _place()
        mychip = 2 * x + y
        chips = [(1 - x, y), (x, 1 - y), (1 - x, 1 - y)]
        local = [pltpu.make_async_copy(ins[a].at[:, mychip], outs[a].at[:, mychip], local_sems.at[a]) for a in range(n)]
        for cp in local:
            cp.start()
        sends = []
        for a in range(n):
            for j, chip in enumerate(chips):
                sends.append(pltpu.make_async_remote_copy(
                    src_ref=ins[a].at[:, 2 * chip[0] + chip[1]], dst_ref=outs[a].at[:, mychip],
                    send_sem=send_sems.at[a, j], recv_sem=recv_sems.at[a, j],
                    device_id=(*chip, c), device_id_type=MESH))
        for cp in sends:
            cp.start()
        for a in range(n):
            for j, chip in enumerate(chips):
                pltpu.make_async_remote_copy(
                    src_ref=ins[a].at[:, mychip], dst_ref=outs[a].at[:, 2 * chip[0] + chip[1]],
                    send_sem=send_sems.at[a, j], recv_sem=recv_sems.at[a, j],
                    device_id=(x, y, c), device_id_type=MESH).wait_recv()
        for cp in sends:
            cp.wait_send()
        for cp in local:
            cp.wait()

    return pl.pallas_call(
        body, name="chip_exchange", in_specs=[ANY_SPEC] * n, out_specs=[ANY_SPEC] * n,
        out_shape=[jax.ShapeDtypeStruct(p.shape, BF16) for p in psums],
        scratch_shapes=[pltpu.SemaphoreType.DMA((n, 3)), pltpu.SemaphoreType.DMA((n, 3)),
                        pltpu.SemaphoreType.DMA((n,))],
        )(*psums)


def _chip_sum(parts):
    _, _, rows, N = parts.shape

    def body(p_ref, o_ref):
        acc = p_ref[0].astype(F32)
        for k in range(1, 4):
            acc = acc + p_ref[k].astype(F32)
        o_ref[...] = acc

    return pl.pallas_call(
        body, name="chip_sum", grid=(2,),
        in_specs=[pl.BlockSpec((None, 4, rows, N), lambda l: (l, 0, 0, 0))],
        out_specs=pl.BlockSpec((None, rows, N), lambda l: (l, 0, 0)),
        out_shape=jax.ShapeDtypeStruct((2, rows, N), F32), compiler_params=_cp(("parallel",)))(parts)


def _permute_in(w):
    lead = w.shape[:-1]
    return w.reshape(lead + (3, 3, 2, BQ)).swapaxes(-2, -3).reshape(lead + (QKVW,))


def _unpermute_in(w):
    lead = w.shape[:-1]
    return w.reshape(lead + (3, 2, 3, BQ)).swapaxes(-2, -3).reshape(lead + (QKVW,))


def _row(v):
    v = v.reshape(-1)
    return jnp.pad(v, (0, D - v.shape[0])).reshape(1, D)


def kernel(x, w_in, f_bias, conv_w, w_out, rel_bias, ln1_g, ln1_b, w_gate, w_up, w_down, ln2_g, ln2_b, loss_target, m_w_in, m_f_bias, m_conv_w, m_w_out, m_rel_bias, m_ln1_g, m_ln1_b, m_w_gate, m_w_up, m_w_down, m_ln2_g, m_ln2_b, v_w_in, v_f_bias, v_conv_w, v_w_out, v_rel_bias, v_ln1_g, v_ln1_b, v_w_gate, v_w_up, v_w_down, v_ln2_g, v_ln2_b):
    xi, yi, ci = _place()
    me = 4 * xi + 2 * yi + ci

    win_s = jnp.concatenate([_permute_in(w_in[..., :QKVW]), w_in[..., QKVW:]], axis=-1)
    win_s = jnp.pad(win_s, ((0, 0), (0, 0), (0, NPAD - NPROJ))).astype(BF16)
    shards = [win_s, w_out.astype(BF16), jnp.swapaxes(w_gate, 1, 2).astype(BF16),
              jnp.swapaxes(w_up, 1, 2).astype(BF16), w_down.astype(BF16)]
    full = _allgather_weights(shards)
    Win, Wout, WgT, WuT, Wd = [f.reshape(2, NDEV * f.shape[2], f.shape[3]) for f in full]

    cw_rows = lax.dynamic_update_slice(jnp.zeros((2, 3, 256), F32), conv_w, (0, 0, me * 32))
    small = jnp.concatenate([_row(cw_rows[0]), _row(cw_rows[1]), jnp.zeros((SMALL_ROWS - 2, D), F32)], axis=0)
    small = _allreduce_small(small)
    cw_full = small[0:2, :CONVW].reshape(2, 3, 256)
    cw8 = jnp.pad(cw_full, ((0, 0), (0, 5), (0, 0)))
    fb = jnp.pad(f_bias, ((0, 0), (0, GATEW - NH))).reshape(2, 1, GATEW)
    tbl = _dil_table2(rel_bias)

    def wcol(layer, K, tn, off):
        return pl.BlockSpec((None, K, tn), lambda i, j: (layer, 0, off + j))

    def arow(tm, K, blk=0):
        return pl.BlockSpec((tm, K), lambda i, j: (i, blk))

    h = x.reshape(T, D)
    saved = []
    for l in range(2):
        qkv = _mm([(h, arow(512, D), Win, wcol(l, D, 768, 0))], nt=False, M=T, N=QKVW, tm=512, tn=768,
                  out_dtype=BF16, name="proj_qkv")
        conv = _mm([(h, arow(512, D), Win, wcol(l, D, 768, 3))], nt=False, M=T, N=CONVW, tm=512, tn=768,
                   out_dtype=F32, name="proj_conv")
        gate = _mm([(h, arow(512, D), Win, wcol(l, D, 128, 24))], nt=False, M=T, N=GATEW, tm=512, tn=128,
                   out_dtype=F32, name="proj_gate")
        cum = _fox_prep(gate, fb[l])
        cum4 = cum[:, :NH].reshape(BL, S, NH).transpose(0, 2, 1)
        cq = cum4.reshape(BL, 2, 2, S, 1)
        ck = cum4.reshape(BL, 2, 2, 1, S)
        mixed, rtot = _sb_fwd2(qkv)
        mixed, lse_d = _flash_fwd2(qkv, mixed, 1, False, (tbl,))
        mixed, lse_f = _flash_fwd2(qkv, mixed, 2, True, (cq, ck))
        mixed = _conv_fwd(conv, cw8[l], mixed)
        mix = _mm([(mixed, arow(512, D), Wout, wcol(l, D, 512, 0))], nt=False, M=T, N=D, tm=512, tn=512,
                  out_dtype=F32, name="out_proj")
        x1, xh1, r1 = _ln_fwd(h, mix, ln1_g[l:l + 1], ln1_b[l:l + 1])
        g, u, a = _ffn_up(x1, WgT, WuT, l)
        ffn = _mm([(a, arow(512, DFF), Wd, wcol(l, DFF, 512, 0))], nt=False, M=T, N=D, tm=512, tn=512,
                  out_dtype=F32, name="ffn_down")
        x2, xh2, r2 = _ln_fwd(x1, ffn, ln2_g[l:l + 1], ln2_b[l:l + 1])
        saved.append(dict(h=h, qkv=qkv, conv=conv, gate=gate, cq=cq, ck=ck, mixed=mixed, rtot=rtot, lse_d=lse_d,
                          lse_f=lse_f, x1=x1, xh1=xh1, r1=r1, g=g, u=u, a=a, xh2=xh2, r2=r2))
        h = x2

    sq, dy = _loss_grad(h, loss_target.reshape(T, D))
    loss = lax.psum(sq[0, 0], ("x", "y", "c")) * (0.5 / D)

    G_in = jnp.zeros((2, D, NPAD), F32)
    G_out = jnp.zeros((2, D, D), F32)
    G_g = jnp.zeros((2, DFF, D), F32)
    G_u = jnp.zeros((2, DFF, D), F32)
    G_d = jnp.zeros((2, DFF, D), F32)
    small_g = {}

    def wrow(layer, tn, K, blk=0):
        return pl.BlockSpec((None, tn, K), lambda i, j: (layer, j, blk))

    for l in (1, 0):
        sv = saved[l]
        ds2, dg2, db2 = _ln_bwd(dy, sv["xh2"], sv["r2"], ln2_g[l:l + 1])
        dgt, dut = _ffn_da(ds2, Wd, sv["g"], sv["u"], l)
        G_d = _mm_tn(sv["a"], ds2, G_d, Ka=DFF, N=D, tm=256, tn=512, tk=1024, a_off=0, b_off=0, layer=l, ooff=0,
                     name="grad_w_down")
        G_g = _mm_tn(dgt, sv["x1"], G_g, Ka=DFF, N=D, tm=256, tn=512, tk=1024, a_off=0, b_off=0, layer=l, ooff=0,
                     name="grad_w_gate")
        G_u = _mm_tn(dut, sv["x1"], G_u, Ka=DFF, N=D, tm=256, tn=512, tk=1024, a_off=0, b_off=0, layer=l, ooff=0,
                     name="grad_w_up")
        dx1 = _mm([(dgt, arow(512, DFF), WgT, wcol(l, DFF, 512, 0)), (dut, arow(512, DFF), WuT, wcol(l, DFF, 512, 0))],
                  nt=False, M=T, N=D, tm=512, tn=512, out_dtype=F32, name="ffn_dx", res=ds2, res_scale=ALPHA)
        ds1, dg1, db1 = _ln_bwd(dx1, sv["xh1"], sv["r1"], ln1_g[l:l + 1])
        G_out = _mm_tn(sv["mixed"], ds1, G_out, Ka=D, N=D, tm=512, tn=512, tk=1024, a_off=0, b_off=0, layer=l, ooff=0,
                       name="grad_w_out")
        dmixed = _mm([(ds1, arow(512, D), Wout, wrow(l, 512, D))], nt=True, M=T, N=D, tm=512, tn=512,
                     out_dtype=BF16, name="out_proj_dx")
        dqkv = _sb_bwd2(sv["qkv"], dmixed, sv["rtot"])
        dqkv, dtbl = _flash_bwd2(sv["qkv"], sv["mixed"], dmixed, sv["lse_d"], dqkv, 1, False, (tbl,))
        dqkv, dck = _flash_bwd2(sv["qkv"], sv["mixed"], dmixed, sv["lse_f"], dqkv, 2, True, (sv["cq"], sv["ck"]))
        dconv, dcw = _conv_bwd(sv["conv"], cw8[l], dmixed)
        dcum = jnp.pad(dck.reshape(BL, NH, S).transpose(0, 2, 1).reshape(T, NH), ((0, 0), (0, GATEW - NH)))
        dgate, dfb = _fox_post(dcum, sv["gate"], fb[l])
        drb = _dil_table_bwd2(dtbl)
        G_in = _mm_tn(sv["h"], dqkv, G_in, Ka=D, N=QKVW, tm=512, tn=768, tk=1024, a_off=0, b_off=0, layer=l, ooff=0,
                      name="grad_w_in_qkv")
        G_in = _mm_tn(sv["h"], dconv, G_in, Ka=D, N=CONVW, tm=512, tn=768, tk=1024, a_off=0, b_off=0, layer=l, ooff=3,
                      name="grad_w_in_conv")
        G_in = _mm_tn(sv["h"], dgate, G_in, Ka=D, N=GATEW, tm=512, tn=128, tk=1024, a_off=0, b_off=0, layer=l, ooff=24,
                      name="grad_w_in_gate")
        dy = _mm([(dqkv, arow(512, QKVW), Win, wrow(l, 512, QKVW, 0)),
                  (dconv, arow(512, CONVW), Win, wrow(l, 512, CONVW, 3)),
                  (dgate, arow(512, GATEW), Win, wrow(l, 512, GATEW, 24))],
                 nt=True, M=T, N=D, tm=512, tn=512, out_dtype=F32, name="proj_dx", res=ds1, res_scale=ALPHA)
        small_g[l] = dict(ln1_g=dg1, ln1_b=db1, ln2_g=dg2, ln2_b=db2, cw=dcw[0:3].reshape(1, CONVW),
                          fb=dfb[:, :NH], rb=drb[:, :NH])
    grad_x = dy.reshape(BL, S, D)

    rows = []
    for name in ("ln1_g", "ln1_b", "ln2_g", "ln2_b"):
        rows += [small_g[0][name], small_g[1][name]]
    rows += [_row(small_g[0]["cw"]), _row(small_g[1]["cw"]),
             _row(jnp.concatenate([small_g[0]["fb"], small_g[1]["fb"]], axis=0)),
             _row(small_g[0]["rb"] + small_g[1]["rb"])]
    rows.append(jnp.zeros((SMALL_ROWS - len(rows), D), F32))
    sg = _allreduce_small(jnp.concatenate(rows, axis=0))
    g_ln1_g, g_ln1_b, g_ln2_g, g_ln2_b = sg[0:2], sg[2:4], sg[4:6], sg[6:8]
    g_conv_full = sg[8:10, :CONVW].reshape(2, 3, 256)
    g_conv = lax.dynamic_slice(g_conv_full, (0, 0, me * 32), (2, 3, 32))
    g_fb = sg[10, :2 * NH].reshape(2, NH)
    g_rb = sg[11, :32 * NH].reshape(32, NH)

    bufs = [G_in, G_out, G_g, G_u, G_d]
    views = [b.reshape(2, 4, 2, b.shape[1] // NDEV, b.shape[2]) for b in bufs]
    got = _sibling_exchange(views)
    core = jnp.reshape(ci, (1,)).astype(jnp.int32)
    psums = [_pair_sum(vw, gt, core) for vw, gt in zip(views, got)]
    parts = _chip_exchange(psums)
    gs = [_chip_sum(p) for p in parts]
    g_in = gs[0]
    g_w_in = jnp.concatenate([_unpermute_in(g_in[..., :QKVW]), g_in[..., QKVW:NPROJ]], axis=-1)
    g_w_out = gs[1]
    g_w_gate = jnp.swapaxes(gs[2], 1, 2)
    g_w_up = jnp.swapaxes(gs[3], 1, 2)
    g_w_down = gs[4]

    def big(w, g, m, v, tr):
        sh = w.shape
        f = lambda t: t.reshape(-1, sh[-1])
        return [t.reshape(sh) for t in _adamw(f(w), f(g), f(m), f(v), tr)]

    up_in = big(w_in, g_w_in, m_w_in, v_w_in, 64)
    up_out = big(w_out, g_w_out, m_w_out, v_w_out, 128)
    up_gate = big(w_gate, g_w_gate, m_w_gate, v_w_gate, 256)
    up_up = big(w_up, g_w_up, m_w_up, v_w_up, 256)
    up_down = big(w_down, g_w_down, m_w_down, v_w_down, 352)

    def pack(fbv, cwv, rbv, l1g, l1b, l2g, l2b):
        r = [l1g, l1b, l2g, l2b, _row(cwv), _row(fbv), _row(rbv)]
        r.append(jnp.zeros((SMALL_ROWS - 11, D), F32))
        return jnp.concatenate(r, axis=0)

    pw = pack(f_bias, conv_w, rel_bias, ln1_g, ln1_b, ln2_g, ln2_b)
    pg = pack(g_fb, g_conv, g_rb, g_ln1_g, g_ln1_b, g_ln2_g, g_ln2_b)
    pm = pack(m_f_bias, m_conv_w, m_rel_bias, m_ln1_g, m_ln1_b, m_ln2_g, m_ln2_b)
    pv = pack(v_f_bias, v_conv_w, v_rel_bias, v_ln1_g, v_ln1_b, v_ln2_g, v_ln2_b)
    ups = _adamw(pw, pg, pm, pv, SMALL_ROWS)

    def unpack(p):
        return dict(ln1_g=p[0:2], ln1_b=p[2:4], ln2_g=p[4:6], ln2_b=p[6:8],
                    conv_w=p[8, :192].reshape(2, 3, 32), f_bias=p[9, :2 * NH].reshape(2, NH),
                    rel_bias=p[10, :32 * NH].reshape(32, NH))

    sm = [unpack(p) for p in ups]

    def group(k):
        return (up_in[k], sm[k]["f_bias"], sm[k]["conv_w"], up_out[k], sm[k]["rel_bias"], sm[k]["ln1_g"],
                sm[k]["ln1_b"], up_gate[k], up_up[k], up_down[k], sm[k]["ln2_g"], sm[k]["ln2_b"])

    grads = (g_w_in, g_fb, g_conv, g_w_out, g_rb, g_ln1_g, g_ln1_b, g_w_gate, g_w_up, g_w_down, g_ln2_g, g_ln2_b)
    return (loss, grad_x) + grads + group(0) + group(1) + group(2)
```

```python
import math

import numpy as np
import jax
import jax.numpy as jnp
from jax import lax
from jax.experimental import pallas as pl
from jax.experimental.pallas import tpu as pltpu

F32 = jnp.float32
BF16 = jnp.bfloat16
MESH = pl.DeviceIdType.MESH

D = 1024
S = 2048
BL = 2
T = BL * S
NH = 4
DFF = 2816
NPROJ = 3076
NPAD = 3200
QKVW = 2304
CONVW = 768
GATEW = 128
PAIRW = 384
BQ = 128
NB = S // BQ
NDEV = 8
ROWS_IN = D // NDEV
ROWS_FF = DFF // NDEV
ALPHA = 4.0 ** 0.25
SCALE = 0.125
NEG = -1e30
LN_EPS = 1e-5
ADAM_LR, ADAM_B1, ADAM_B2, ADAM_EPS, ADAM_WD, ADAM_STEP = 0.001, 0.9, 0.999, 1e-08, 0.01, 10
VMEM_LIMIT = 48 * 1024 * 1024
SMALL_ROWS = 16


def _bucket_thresholds():
    d = np.arange(0, S)
    nf = np.maximum(d, 1).astype(np.float32)
    large = 16 + (np.log(nf / np.float32(16)) / np.float32(math.log(128)) * np.float32(16)).astype(np.int32)
    b = np.where(d < 16, d, np.minimum(large, 31))
    return [int(np.argmax(b >= k)) for k in range(32)]


BUCKET_TH = _bucket_thresholds()


def _cp(sem=None):
    return pltpu.CompilerParams(dimension_semantics=sem, vmem_limit_bytes=VMEM_LIMIT)


def _dot(a, b):
    return lax.dot_general(a, b, (((1,), (0,)), ((), ())), preferred_element_type=F32)


def _dot_nt(a, b):
    return lax.dot_general(a, b, (((1,), (1,)), ((), ())), preferred_element_type=F32)


def _dot_tn(a, b):
    return lax.dot_general(a, b, (((0,), (0,)), ((), ())), preferred_element_type=F32)


def _split2(x):
    hi = x.astype(BF16)
    mid = (x - hi.astype(F32)).astype(BF16)
    return jnp.concatenate([hi, mid], axis=1)


def _split3(x):
    hi = x.astype(BF16)
    r = x - hi.astype(F32)
    mid = r.astype(BF16)
    lo = (r - mid.astype(F32)).astype(BF16)
    return jnp.concatenate([hi, mid, lo], axis=1)


def _log_sigmoid(u):
    return jnp.minimum(u, 0.0) - jnp.log1p(jnp.exp(-jnp.abs(u)))


def _log_sigmoid_tile(u):
    return jnp.minimum(u, 0.0) - jnp.log(1.0 + jnp.exp(jnp.minimum(u, -u)))


def _iota(shape, dim):
    return lax.broadcasted_iota(jnp.int32, shape, dim)


def _mm(pairs, *, nt, M, N, tm, tn, out_dtype, name, res=None, res_scale=1.0):
    n = len(pairs)

    def body(*refs):
        acc = None
        for p in range(n):
            a = refs[2 * p][...].astype(BF16)
            b = refs[2 * p + 1][...]
            d = _dot_nt(a, b) if nt else _dot(a, b)
            acc = d if acc is None else acc + d
        if res is not None:
            acc = acc + res_scale * refs[2 * n][...]
        refs[-1][...] = acc.astype(out_dtype)

    ops, specs = [], []
    for a, asp, b, bsp in pairs:
        ops += [a, b]
        specs += [asp, bsp]
    if res is not None:
        ops.append(res)
        specs.append(pl.BlockSpec((tm, tn), lambda i, j: (i, j)))
    return pl.pallas_call(
        body, name=name, grid=(M // tm, N // tn), in_specs=specs,
        out_specs=pl.BlockSpec((tm, tn), lambda i, j: (i, j)),
        out_shape=jax.ShapeDtypeStruct((M, N), out_dtype),
        compiler_params=_cp(("parallel", "parallel")))(*ops)


def _mm_tn(a, b, gbuf, *, Ka, N, tm, tn, tk, a_off, b_off, layer, ooff, name):
    def body(a_ref, b_ref, g_in, o_ref):
        k = pl.program_id(2)
        d = _dot_tn(a_ref[...].astype(BF16), b_ref[...].astype(BF16))

        @pl.when(k == 0)
        def _():
            o_ref[...] = d

        @pl.when(k > 0)
        def _():
            o_ref[...] += d

    return pl.pallas_call(
        body, name=name, grid=(Ka // tm, N // tn, T // tk),
        in_specs=[pl.BlockSpec((tk, tm), lambda i, j, k: (k, a_off + i)),
                  pl.BlockSpec((tk, tn), lambda i, j, k: (k, b_off + j)),
                  pl.BlockSpec(memory_space=pl.ANY)],
        out_specs=pl.BlockSpec((None, tm, tn), lambda i, j, k: (layer, i, ooff + j)),
        out_shape=jax.ShapeDtypeStruct(gbuf.shape, F32),
        input_output_aliases={2: 0},
        compiler_params=_cp(("parallel", "parallel", "arbitrary")))(a, b, gbuf)


def _ffn_up(x1, wgt, wut, layer):
    tm, tn = 1024, 256

    def body(x_ref, wg_ref, wu_ref, g_ref, u_ref, a_ref):
        xb = x_ref[...].astype(BF16)
        g = _dot_nt(xb, wg_ref[...])
        u = _dot_nt(xb, wu_ref[...])
        g_ref[...] = g.astype(BF16)
        u_ref[...] = u.astype(BF16)
        a_ref[...] = (g * jax.nn.sigmoid(g) * u).astype(BF16)

    wspec = pl.BlockSpec((None, tn, D), lambda i, j: (layer, j, 0))
    ospec = pl.BlockSpec((tm, tn), lambda i, j: (i, j))
    return pl.pallas_call(
        body, name="ffn_up", grid=(T // tm, DFF // tn),
        in_specs=[pl.BlockSpec((tm, D), lambda i, j: (i, 0)), wspec, wspec],
        out_specs=[ospec, ospec, ospec],
        out_shape=[jax.ShapeDtypeStruct((T, DFF), BF16)] * 3,
        compiler_params=_cp(("parallel", "parallel")))(x1, wgt, wut)


def _ffn_da(dffn, wd, g, u, layer):
    tm, tn = 1024, 256

    def body(d_ref, wd_ref, g_ref, u_ref, dg_ref, du_ref):
        da = _dot_nt(d_ref[...].astype(BF16), wd_ref[...])
        gv = g_ref[...].astype(F32)
        sg = jax.nn.sigmoid(gv)
        dg_ref[...] = (da * u_ref[...].astype(F32) * (sg * (1.0 + gv * (1.0 - sg)))).astype(BF16)
        du_ref[...] = (da * (gv * sg)).astype(BF16)

    ospec = pl.BlockSpec((tm, tn), lambda i, j: (i, j))
    return pl.pallas_call(
        body, name="ffn_da", grid=(T // tm, DFF // tn),
        in_specs=[pl.BlockSpec((tm, D), lambda i, j: (i, 0)),
                  pl.BlockSpec((None, tn, D), lambda i, j: (layer, j, 0)), ospec, ospec],
        out_specs=[ospec, ospec],
        out_shape=[jax.ShapeDtypeStruct((T, DFF), BF16), jax.ShapeDtypeStruct((T, DFF), BF16)],
        compiler_params=_cp(("parallel", "parallel")))(dffn, wd, g, u)


def _ln_fwd(x, f, gam, bet):
    tm = 256

    def body(x_ref, f_ref, g_ref, b_ref, y_ref, xh_ref, r_ref, yb_ref):
        s = ALPHA * x_ref[...] + f_ref[...]
        mu = jnp.mean(s, axis=-1, keepdims=True)
        xc = s - mu
        var = jnp.mean(xc * xc, axis=-1, keepdims=True)
        r = lax.rsqrt(var + LN_EPS)
        xh = xc * r
        xh_ref[...] = xh
        r_ref[...] = r
        y = xh * g_ref[...] + b_ref[...]
        y_ref[...] = y
        yb_ref[...] = y.astype(BF16)

    row = pl.BlockSpec((tm, D), lambda i: (i, 0))
    vec = pl.BlockSpec((1, D), lambda i: (0, 0))
    return pl.pallas_call(
        body, name="ln_fwd", grid=(T // tm,), in_specs=[row, row, vec, vec],
        out_specs=[row, row, pl.BlockSpec((tm, 1), lambda i: (i, 0)), row],
        out_shape=[jax.ShapeDtypeStruct((T, D), F32), jax.ShapeDtypeStruct((T, D), F32),
                   jax.ShapeDtypeStruct((T, 1), F32), jax.ShapeDtypeStruct((T, D), BF16)],
        compiler_params=_cp(("parallel",)))(x, f, gam, bet)


def _ln_bwd(dy, xh, r, gam):
    tm = 256

    def body(dy_ref, xh_ref, r_ref, g_ref, ds_ref, dg_ref, db_ref, dsb_ref):
        i = pl.program_id(0)
        dyv = dy_ref[...]
        xhv = xh_ref[...]
        dxh = dyv * g_ref[...]
        m1 = jnp.mean(dxh, axis=-1, keepdims=True)
        m2 = jnp.mean(dxh * xhv, axis=-1, keepdims=True)
        ds = r_ref[...] * (dxh - m1 - xhv * m2)
        ds_ref[...] = ds
        dsb_ref[...] = ds.astype(BF16)
        pg = jnp.sum(dyv * xhv, axis=0, keepdims=True)
        pb = jnp.sum(dyv, axis=0, keepdims=True)

        @pl.when(i == 0)
        def _():
            dg_ref[...] = pg
            db_ref[...] = pb

        @pl.when(i > 0)
        def _():
            dg_ref[...] += pg
            db_ref[...] += pb

    row = pl.BlockSpec((tm, D), lambda i: (i, 0))
    vec = pl.BlockSpec((1, D), lambda i: (0, 0))
    return pl.pallas_call(
        body, name="ln_bwd", grid=(T // tm,),
        in_specs=[row, row, pl.BlockSpec((tm, 1), lambda i: (i, 0)), vec],
        out_specs=[row, vec, vec, row],
        out_shape=[jax.ShapeDtypeStruct((T, D), F32), jax.ShapeDtypeStruct((1, D), F32),
                   jax.ShapeDtypeStruct((1, D), F32), jax.ShapeDtypeStruct((T, D), BF16)],
        compiler_params=_cp(("arbitrary",)))(dy, xh, r, gam)


def _loss_grad(y, tgt):
    tm = 256

    def body(y_ref, t_ref, l_ref, dy_ref):
        i = pl.program_id(0)
        e = y_ref[...] - t_ref[...]
        dy_ref[...] = e * (1.0 / D)
        p = jnp.sum(jnp.sum(e * e, axis=1, keepdims=True), axis=0, keepdims=True)

        @pl.when(i == 0)
        def _():
            l_ref[...] = p

        @pl.when(i > 0)
        def _():
            l_ref[...] += p

    row = pl.BlockSpec((tm, D), lambda i: (i, 0))
    return pl.pallas_call(
        body, name="loss_grad", grid=(T // tm,), in_specs=[row, row],
        out_specs=[pl.BlockSpec((1, 1), lambda i: (0, 0)), row],
        out_shape=[jax.ShapeDtypeStruct((1, 1), F32), jax.ShapeDtypeStruct((T, D), F32)],
        compiler_params=_cp(("arbitrary",)))(y, tgt)


def _adamw(w, g, m, v, tr):
    R, C = w.shape

    def body(w_ref, g_ref, m_ref, v_ref, d_ref, m2_ref, v2_ref):
        gv = g_ref[...]
        m2 = ADAM_B1 * m_ref[...] + (1.0 - ADAM_B1) * gv
        v2 = ADAM_B2 * v_ref[...] + (1.0 - ADAM_B2) * (gv * gv)
        m_hat = m2 / (1.0 - ADAM_B1 ** ADAM_STEP)
        v_hat = v2 / (1.0 - ADAM_B2 ** ADAM_STEP)
        d_ref[...] = -ADAM_LR * (m_hat / (jnp.sqrt(v_hat) + ADAM_EPS) + ADAM_WD * w_ref[...])
        m2_ref[...] = m2
        v2_ref[...] = v2

    blk = pl.BlockSpec((tr, C), lambda i: (i, 0))
    sh = jax.ShapeDtypeStruct((R, C), F32)
    return pl.pallas_call(
        body, name="adamw", grid=(R // tr,), in_specs=[blk] * 4, out_specs=[blk] * 3,
        out_shape=[sh, sh, sh], compiler_params=_cp(("parallel",)))(w, g, m, v)


def _head_masks(hh):
    lane = _iota((1, BQ), 1)
    maskf = ((lane >> 6) == hh).astype(F32)
    return maskf, maskf.astype(BF16)


def _qkv_spec(g):
    return pl.BlockSpec((S, PAIRW), lambda b, h: (b, 2 * g + h // 2))


def _pair_spec(g):
    return pl.BlockSpec((S, BQ), lambda b, h: (b, 2 * g + h // 2))


def _stat_spec():
    return pl.BlockSpec((None, S, 1), lambda b, h: (b * NH + h, 0, 0))


ANY_SPEC = pl.BlockSpec(memory_space=pl.ANY)


def _sb_fwd(qkv):
    def body(qkv_ref, o_ref, r_ref):
        hh = pl.program_id(1) % 2
        maskf, maskb = _head_masks(hh)
        row = _iota((BQ, BQ), 0)
        col = _iota((BQ, BQ), 1)
        u2 = ((_iota((2 * BQ, BQ), 0) & (BQ - 1)) > _iota((2 * BQ, BQ), 1)).astype(BF16)

        def qblock(i, _):
            q0 = pl.multiple_of(i * BQ, BQ)
            q = qkv_ref[pl.ds(q0, BQ), 0:BQ] * maskb

            def kblock(jj, carry):
                tail_c, acc = carry
                j = i - jj
                k0 = pl.multiple_of(j * BQ, BQ)
                k = qkv_ref[pl.ds(k0, BQ), BQ:2 * BQ]
                v = qkv_ref[pl.ds(k0, BQ), 2 * BQ:3 * BQ]
                z = _dot_nt(q, k) * SCALE
                valid = (k0 + col) < (q0 + row)
                lb = _log_sigmoid(z)
                lr = jnp.where(valid, lb - z, 0.0)
                tail = _dot(_split2(lr), u2)
                a = jnp.where(valid, jnp.exp(lb + tail + tail_c), 0.0)
                acc = acc + _dot(a.astype(BF16), v)
                tail_c = tail_c + jnp.sum(lr, axis=1, keepdims=True)
                return tail_c, acc

            tail_c, acc = lax.fori_loop(0, i + 1, kblock,
                                        (jnp.zeros((BQ, 1), F32), jnp.zeros((BQ, BQ), F32)))
            val = (acc * maskf).astype(BF16)

            @pl.when(hh == 0)
            def _():
                o_ref[pl.ds(q0, BQ), :] = val

            @pl.when(hh == 1)
            def _():
                o_ref[pl.ds(q0, BQ), :] = o_ref[pl.ds(q0, BQ), :] + val

            r_ref[pl.ds(q0, BQ), :] = tail_c
            return 0

        lax.fori_loop(0, NB, qblock, 0)

    return pl.pallas_call(
        body, name="sb_fwd", grid=(BL, NH), in_specs=[_qkv_spec(0)],
        out_specs=[_pair_spec(0), _stat_spec()],
        out_shape=[jax.ShapeDtypeStruct((T, D), BF16), jax.ShapeDtypeStruct((BL * NH, S, 1), F32)],
        compiler_params=_cp(("parallel", "arbitrary")))(qkv)


def _sb_bwd(qkv, dmixed, rtot):
    def body(qkv_ref, do_ref, r_ref, dqkv_ref, dq_s, dk_s, dv_s):
        hh = pl.program_id(1) % 2
        maskf, maskb = _head_masks(hh)
        row = _iota((BQ, BQ), 0)
        col = _iota((BQ, BQ), 1)
        r2 = _iota((2 * BQ, BQ), 0) & (BQ - 1)
        c2 = _iota((2 * BQ, BQ), 1)
        u2 = (r2 > c2).astype(BF16)
        l2 = (r2 < c2).astype(BF16)

        @pl.when(hh == 0)
        def _():
            dq_s[...] = jnp.zeros_like(dq_s)
            dk_s[...] = jnp.zeros_like(dk_s)
            dv_s[...] = jnp.zeros_like(dv_s)

        def qblock(i, _):
            q0 = pl.multiple_of(i * BQ, BQ)
            q = qkv_ref[pl.ds(q0, BQ), 0:BQ] * maskb
            do = do_ref[pl.ds(q0, BQ), :] * maskb
            rt = r_ref[pl.ds(q0, BQ), :]

            def kblock(j, carry):
                pre_l, pre_g, dq = carry
                k0 = pl.multiple_of(j * BQ, BQ)
                k = qkv_ref[pl.ds(k0, BQ), BQ:2 * BQ]
                v = qkv_ref[pl.ds(k0, BQ), 2 * BQ:3 * BQ]
                z = _dot_nt(q, k) * SCALE
                valid = (k0 + col) < (q0 + row)
                lb = _log_sigmoid(z)
                lr = jnp.where(valid, lb - z, 0.0)
                pre_l = pre_l + jnp.sum(lr, axis=1, keepdims=True)
                tail = _dot(_split2(lr), u2) + (rt - pre_l)
                a = jnp.where(valid, jnp.exp(lb + tail), 0.0)
                gm = _dot_nt(do, v) * a
                before = _dot(_split2(gm), l2) + pre_g
                beta = jnp.exp(lb)
                dz = jnp.where(valid, gm * (1.0 - beta) - beta * before, 0.0) * SCALE
                dzb = dz.astype(BF16)
                dq = dq + _dot(dzb, k)
                dk_s[pl.ds(k0, BQ), :] += _dot_tn(dzb, q)
                dv_s[pl.ds(k0, BQ), :] += _dot_tn(a.astype(BF16), do)
                pre_g = pre_g + jnp.sum(gm, axis=1, keepdims=True)
                return pre_l, pre_g, dq

            z1 = jnp.zeros((BQ, 1), F32)
            _, _, dq = lax.fori_loop(0, i + 1, kblock, (z1, z1, jnp.zeros((BQ, BQ), F32)))
            dq_s[pl.ds(q0, BQ), :] += dq * maskf
            return 0

        lax.fori_loop(0, NB, qblock, 0)

        @pl.when(hh == 1)
        def _():
            dqkv_ref[:, 0:BQ] = dq_s[...].astype(BF16)
            dqkv_ref[:, BQ:2 * BQ] = dk_s[...].astype(BF16)
            dqkv_ref[:, 2 * BQ:3 * BQ] = dv_s[...].astype(BF16)

    return pl.pallas_call(
        body, name="sb_bwd", grid=(BL, NH),
        in_specs=[_qkv_spec(0), _pair_spec(0), _stat_spec()],
        out_specs=_qkv_spec(0),
        out_shape=jax.ShapeDtypeStruct((T, QKVW), BF16),
        scratch_shapes=[pltpu.VMEM((S, BQ), F32)] * 3,
        compiler_params=_cp(("parallel", "arbitrary")))(qkv, dmixed, rtot)


def _flash_fwd(qkv, mixed, g, fox, bias):
    def body(*refs):
        if fox:
            qkv_ref, cq_ref, ck_ref, _, o_ref, lse_ref = refs
        else:
            qkv_ref, tbl_ref, _, o_ref, lse_ref = refs
        hh = pl.program_id(1) % 2
        maskf, maskb = _head_masks(hh)
        row = _iota((BQ, BQ), 0)
        col = _iota((BQ, BQ), 1)

        def qblock(i, _):
            q0 = pl.multiple_of(i * BQ, BQ)
            q = qkv_ref[pl.ds(q0, BQ), 0:BQ] * maskb
            if fox:
                cq = cq_ref[pl.ds(q0, BQ), :]

            def kblock(jj, carry):
                m, l, acc = carry
                j = i - jj
                k0 = pl.multiple_of(j * BQ, BQ)
                k = qkv_ref[pl.ds(k0, BQ), BQ:2 * BQ]
                v = qkv_ref[pl.ds(k0, BQ), 2 * BQ:3 * BQ]
                if fox:
                    z = _dot_nt(q, k) * SCALE + (cq - ck_ref[:, pl.ds(k0, BQ)])
                    z = jnp.where((k0 + col) <= (q0 + row), z, NEG)
                else:
                    z = _dot_nt(q, k) * SCALE + tbl_ref[jj]
                m_new = jnp.maximum(m, jnp.max(z, axis=1, keepdims=True))
                alpha = jnp.exp(m - m_new)
                p = jnp.exp(z - m_new)
                l = alpha * l + jnp.sum(p, axis=1, keepdims=True)
                acc = alpha * acc + _dot(p.astype(BF16), v)
                return m_new, l, acc

            m, l, acc = lax.fori_loop(
                0, i + 1, kblock,
                (jnp.full((BQ, 1), NEG, F32), jnp.zeros((BQ, 1), F32), jnp.zeros((BQ, BQ), F32)))
            val = (acc / l * maskf).astype(BF16)

            @pl.when(hh == 0)
            def _():
                o_ref[pl.ds(q0, BQ), :] = val

            @pl.when(hh == 1)
            def _():
                o_ref[pl.ds(q0, BQ), :] = o_ref[pl.ds(q0, BQ), :] + val

            lse_ref[pl.ds(q0, BQ), :] = m + jnp.log(l)
            return 0

        lax.fori_loop(0, NB, qblock, 0)

    if fox:
        bias_specs = [_stat_spec(), pl.BlockSpec((None, 1, S), lambda b, h: (b * NH + h, 0, 0))]
    else:
        bias_specs = [pl.BlockSpec((None, NB, BQ, BQ), lambda b, h: (h, 0, 0, 0))]
    n_in = 2 + len(bias_specs)
    return pl.pallas_call(
        body, name="fox_fwd" if fox else "dil_fwd", grid=(BL, NH),
        in_specs=[_qkv_spec(g)] + bias_specs + [ANY_SPEC],
        out_specs=[_pair_spec(g), _stat_spec()],
        out_shape=[jax.ShapeDtypeStruct((T, D), BF16), jax.ShapeDtypeStruct((BL * NH, S, 1), F32)],
        input_output_aliases={n_in - 1: 0},
        compiler_params=_cp(("parallel", "arbitrary")))(qkv, *bias, mixed)


def _flash_bwd(qkv, mixed, dmixed, lse, dqkv, g, fox, bias):
    def body(*refs):
        if fox:
            qkv_ref, o_ref, do_ref, lse_ref, cq_ref, ck_ref, _, dqkv_ref, db_ref, dq_s, dk_s, dv_s = refs
        else:
            qkv_ref, o_ref, do_ref, lse_ref, tbl_ref, _, dqkv_ref, db_ref, dq_s, dk_s, dv_s = refs
        hh = pl.program_id(1) % 2
        maskf, maskb = _head_masks(hh)
        row = _iota((BQ, BQ), 0)
        col = _iota((BQ, BQ), 1)

        @pl.when(hh == 0)
        def _():
            dq_s[...] = jnp.zeros_like(dq_s)
            dk_s[...] = jnp.zeros_like(dk_s)
            dv_s[...] = jnp.zeros_like(dv_s)

        db_ref[...] = jnp.zeros_like(db_ref)

        def qblock(i, _):
            q0 = pl.multiple_of(i * BQ, BQ)
            q = qkv_ref[pl.ds(q0, BQ), 0:BQ] * maskb
            do = do_ref[pl.ds(q0, BQ), :] * maskb
            lse_q = lse_ref[pl.ds(q0, BQ), :]
            if fox:
                cq = cq_ref[pl.ds(q0, BQ), :]

            def probs(j):
                k0 = pl.multiple_of(j * BQ, BQ)
                k = qkv_ref[pl.ds(k0, BQ), BQ:2 * BQ]
                v = qkv_ref[pl.ds(k0, BQ), 2 * BQ:3 * BQ]
                if fox:
                    z = _dot_nt(q, k) * SCALE + (cq - ck_ref[:, pl.ds(k0, BQ)])
                    z = jnp.where((k0 + col) <= (q0 + row), z, NEG)
                else:
                    z = _dot_nt(q, k) * SCALE + tbl_ref[i - j]
                return k0, k, jnp.exp(z - lse_q), _dot_nt(do, v)

            if fox:
                def dblock(j, acc):
                    _, _, p, dp = probs(j)
                    return acc + jnp.sum(p * dp, axis=1, keepdims=True)

                delta = lax.fori_loop(0, i + 1, dblock, jnp.zeros((BQ, 1), F32))
            else:
                delta = jnp.sum(do.astype(F32) * o_ref[pl.ds(q0, BQ), :].astype(F32), axis=1, keepdims=True)

            def kblock(j, dq):
                k0, k, p, dp = probs(j)
                dz = p * (dp - delta)
                if fox:
                    db_ref[:, pl.ds(k0, BQ)] = db_ref[:, pl.ds(k0, BQ)] - jnp.sum(dz, axis=0, keepdims=True)
                else:
                    db_ref[i - j] = db_ref[i - j] + dz
                dzb = (dz * SCALE).astype(BF16)
                dk_s[pl.ds(k0, BQ), :] += _dot_tn(dzb, q)
                dv_s[pl.ds(k0, BQ), :] += _dot_tn(p.astype(BF16), do)
                return dq + _dot(dzb, k)

            dq = lax.fori_loop(0, i + 1, kblock, jnp.zeros((BQ, BQ), F32))
            dq_s[pl.ds(q0, BQ), :] += dq * maskf
            return 0

        lax.fori_loop(0, NB, qblock, 0)

        @pl.when(hh == 1)
        def _():
            dqkv_ref[:, 0:BQ] = dq_s[...].astype(BF16)
            dqkv_ref[:, BQ:2 * BQ] = dk_s[...].astype(BF16)
            dqkv_ref[:, 2 * BQ:3 * BQ] = dv_s[...].astype(BF16)

    if fox:
        bias_specs = [_stat_spec(), pl.BlockSpec((None, 1, S), lambda b, h: (b * NH + h, 0, 0))]
        db_spec = pl.BlockSpec((None, 1, S), lambda b, h: (b * NH + h, 0, 0))
        db_shape = jax.ShapeDtypeStruct((BL * NH, 1, S), F32)
    else:
        bias_specs = [pl.BlockSpec((None, NB, BQ, BQ), lambda b, h: (h, 0, 0, 0))]
        db_spec = pl.BlockSpec((None, NB, BQ, BQ), lambda b, h: (b * NH + h, 0, 0, 0))
        db_shape = jax.ShapeDtypeStruct((BL * NH, NB, BQ, BQ), F32)
    n_in = 5 + len(bias_specs)
    return pl.pallas_call(
        body, name="fox_bwd" if fox else "dil_bwd", grid=(BL, NH),
        in_specs=[_qkv_spec(g), _pair_spec(g), _pair_spec(g), _stat_spec()] + bias_specs + [ANY_SPEC],
        out_specs=[_qkv_spec(g), db_spec],
        out_shape=[jax.ShapeDtypeStruct((T, QKVW), BF16), db_shape],
        scratch_shapes=[pltpu.VMEM((S, BQ), F32)] * 3,
        input_output_aliases={n_in - 1: 0},
        compiler_params=_cp(("parallel", "arbitrary")))(qkv, mixed, dmixed, lse, *bias, dqkv)


def _delta_tile(d):
    return d * BQ + _iota((BQ, BQ), 0) - _iota((BQ, BQ), 1)


def _bucket_tile(delta):
    b = jnp.zeros((BQ, BQ), jnp.int32)
    for th in BUCKET_TH[1:]:
        b = b + (delta >= th).astype(jnp.int32)
    return b


def _dil_table(rel_bias):
    def body(rb_ref, o_ref):
        for d in range(NB):
            delta = _delta_tile(d)
            bucket = _bucket_tile(delta)
            pos = delta >= 0
            n = ((pos & (delta <= 128)).astype(jnp.int32)
                 + (pos & (delta <= 512) & ((delta & 3) == 0)).astype(jnp.int32)
                 + (pos & ((delta & 15) == 0)).astype(jnp.int32))
            logn = jnp.where(n == 3, math.log(3.0), jnp.where(n == 2, math.log(2.0), jnp.where(n == 1, 0.0, NEG)))
            for h in range(NH):
                val = lax.fori_loop(0, 32, lambda b, acc: jnp.where(bucket == b, rb_ref[b, h], acc),
                                    jnp.zeros((BQ, BQ), F32))
                o_ref[h, d] = val + logn

    return pl.pallas_call(
        body, name="dil_table", in_specs=[pl.BlockSpec(memory_space=pltpu.SMEM)],
        out_specs=pl.BlockSpec(memory_space=pltpu.VMEM),
        out_shape=jax.ShapeDtypeStruct((NH, NB, BQ, BQ), F32), compiler_params=_cp())(rel_bias)


def _dil_table_bwd(dtbl):
    def body(dt_ref, o_ref):
        h = pl.program_id(0)
        rowi = _iota((32, BQ), 0)
        lanei = _iota((32, BQ), 1)

        @pl.when(h == 0)
        def _():
            o_ref[...] = jnp.zeros_like(o_ref)

        acc = jnp.zeros((32, BQ), F32)
        for d in range(NB):
            tile = dt_ref[0, d] + dt_ref[1, d]
            bucket = _bucket_tile(_delta_tile(d))

            def bb(b, acc):
                s = jnp.sum(jnp.sum(jnp.where(bucket == b, tile, 0.0), axis=1, keepdims=True), axis=0, keepdims=True)
                return acc + jnp.where((rowi == b) & (lanei == h), s, 0.0)

            acc = lax.fori_loop(0, 32, bb, acc)
        o_ref[...] += acc

    return pl.pallas_call(
        body, name="dil_table_bwd", grid=(NH,),
        in_specs=[pl.BlockSpec((BL, None, NB, BQ, BQ), lambda h: (0, h, 0, 0, 0))],
        out_specs=pl.BlockSpec((32, BQ), lambda h: (0, 0)),
        out_shape=jax.ShapeDtypeStruct((32, BQ), F32),
        compiler_params=_cp(("arbitrary",)))(dtbl.reshape(BL, NH, NB, BQ, BQ))


def _fox_prep(gate, fb):
    def body(g_ref, fb_ref, c_ref):
        tri = (_iota((BQ, BQ), 0) >= _iota((BQ, BQ), 1)).astype(BF16)

        def blk(i, carry):
            r0 = pl.multiple_of(i * BQ, BQ)
            lf = _log_sigmoid(g_ref[pl.ds(r0, BQ), :] + fb_ref[...])
            c = _dot(tri, _split3(lf))
            c_ref[pl.ds(r0, BQ), :] = c[:, 0:BQ] + c[:, BQ:2 * BQ] + c[:, 2 * BQ:3 * BQ] + carry
            return carry + jnp.sum(lf, axis=0, keepdims=True)

        lax.fori_loop(0, NB, blk, jnp.zeros((1, BQ), F32))

    blk = pl.BlockSpec((S, GATEW), lambda b: (b, 0))
    return pl.pallas_call(
        body, name="fox_prep", grid=(BL,), in_specs=[blk, pl.BlockSpec((1, GATEW), lambda b: (0, 0))],
        out_specs=blk, out_shape=jax.ShapeDtypeStruct((T, GATEW), F32),
        compiler_params=_cp(("parallel",)))(gate, fb)


def _fox_post(dcum, gate, fb):
    def body(dc_ref, g_ref, fb_ref, dg_ref, dfb_ref):
        b = pl.program_id(0)
        tri = (_iota((BQ, BQ), 0) <= _iota((BQ, BQ), 1)).astype(BF16)

        def blk(ii, carry):
            csum, dfb = carry
            r0 = pl.multiple_of((NB - 1 - ii) * BQ, BQ)
            dc = dc_ref[pl.ds(r0, BQ), :]
            c = _dot(tri, _split3(dc))
            dlf = c[:, 0:BQ] + c[:, BQ:2 * BQ] + c[:, 2 * BQ:3 * BQ] + csum
            dg = dlf * jnp.exp(_log_sigmoid(-(g_ref[pl.ds(r0, BQ), :] + fb_ref[...])))
            dg_ref[pl.ds(r0, BQ), :] = dg
            return csum + jnp.sum(dc, axis=0, keepdims=True), dfb + jnp.sum(dg, axis=0, keepdims=True)

        z = jnp.zeros((1, BQ), F32)
        _, dfb = lax.fori_loop(0, NB, blk, (z, z))

        @pl.when(b == 0)
        def _():
            dfb_ref[...] = dfb

        @pl.when(b > 0)
        def _():
            dfb_ref[...] += dfb

    blk = pl.BlockSpec((S, GATEW), lambda b: (b, 0))
    vec = pl.BlockSpec((1, GATEW), lambda b: (0, 0))
    return pl.pallas_call(
        body, name="fox_post", grid=(BL,), in_specs=[blk, blk, vec], out_specs=[blk, vec],
        out_shape=[jax.ShapeDtypeStruct((T, GATEW), F32), jax.ShapeDtypeStruct((1, GATEW), F32)],
        compiler_params=_cp(("arbitrary",)))(dcum, gate, fb)


HB = 2 * BQ


def _lane_masks():
    lane = _iota((1, BQ), 1)
    m0 = (lane < 64).astype(BF16)
    return m0, 1.0 - m0


def _stack(x, m0, m1):
    return jnp.concatenate([x * m0, x * m1], axis=0)


def _lanes(a):
    return jnp.concatenate([a[:BQ], a[BQ:]], axis=1)


def _per_lane(v):
    return jnp.where(_iota((BQ, BQ), 1) < 64, v[:BQ], v[BQ:])


def _pairs_spec(g):
    return pl.BlockSpec((T, PAIRW), lambda p: (0, 2 * g + p))


def _pairo_spec(g):
    return pl.BlockSpec((T, BQ), lambda p: (0, 2 * g + p))


def _stats_spec():
    return pl.BlockSpec((BL, None, 2, S, 1), lambda p: (0, p, 0, 0, 0))


def _keys_spec():
    return pl.BlockSpec((BL, None, 2, 1, S), lambda p: (0, p, 0, 0, 0))


def _tbl_spec():
    return pl.BlockSpec((None, NB, HB, BQ), lambda p: (p, 0, 0, 0))


def _load_kv(qkv_ref, b, j):
    k0 = pl.multiple_of(b * S + j * BQ, BQ)
    return k0, qkv_ref[pl.ds(k0, BQ), BQ:2 * BQ], qkv_ref[pl.ds(k0, BQ), 2 * BQ:3 * BQ]


def _diag_valid(strict):
    r = _iota((HB, BQ), 0) & (BQ - 1)
    c = _iota((HB, BQ), 1)
    return (c < r) if strict else (c <= r)


def _sb_fwd2(qkv):
    def body(qkv_ref, o_ref, r_ref):
        m0, m1 = _lane_masks()
        valid = _diag_valid(True)
        r2 = _iota((HB, BQ), 0) & (BQ - 1)
        u2 = (r2 > _iota((HB, BQ), 1)).astype(BF16)

        def step(b, qs, j, carry, diag):
            tail_c, acc = carry
            _, k, v = _load_kv(qkv_ref, b, j)
            z = _dot_nt(qs, k)
            lb = _log_sigmoid_tile(z)
            lr = lb - z
            if diag:
                lr = jnp.where(valid, lr, 0.0)
            a = jnp.exp(lb + _dot(_split2(lr), u2) + tail_c)
            if diag:
                a = jnp.where(valid, a, 0.0)
            acc = acc + _dot(_lanes(a.astype(BF16)), _stack(v, m0, m1))
            return tail_c + jnp.sum(lr, axis=1, keepdims=True), acc

        def qblock(i, _):
            qs = []
            for b in range(BL):
                q0 = pl.multiple_of(b * S + i * BQ, BQ)
                qs.append(_stack(qkv_ref[pl.ds(q0, BQ), 0:BQ] * SCALE, m0, m1))
            zero = (jnp.zeros((HB, 1), F32), jnp.zeros((BQ, BQ), F32))
            carries = tuple(step(b, qs[b], i, zero, True) for b in range(BL))
            carries = lax.fori_loop(
                1, i + 1, lambda jj, cs: tuple(step(b, qs[b], i - jj, cs[b], False) for b in range(BL)), carries)
            for b in range(BL):
                tail_c, acc = carries[b]
                o_ref[pl.ds(pl.multiple_of(b * S + i * BQ, BQ), BQ), :] = acc.astype(BF16)
                r0 = pl.multiple_of(i * BQ, BQ)
                r_ref[b, 0, pl.ds(r0, BQ), :] = tail_c[:BQ]
                r_ref[b, 1, pl.ds(r0, BQ), :] = tail_c[BQ:]
            return 0

        lax.fori_loop(0, NB, qblock, 0)

    mixed, rtot = pl.pallas_call(
        body, name="sb_fwd", grid=(2,), in_specs=[_pairs_spec(0)],
        out_specs=[_pairo_spec(0), _stats_spec()],
        out_shape=[jax.ShapeDtypeStruct((T, D), BF16), jax.ShapeDtypeStruct((BL, 2, 2, S, 1), F32)],
        compiler_params=_cp(("parallel",)))(qkv)
    return mixed, rtot


def _sb_bwd2(qkv, dmixed, rtot):
    def body(qkv_ref, do_ref, r_ref, dqkv_ref, dk_s, dv_s):
        m0, m1 = _lane_masks()
        valid = _diag_valid(True)
        r2 = _iota((HB, BQ), 0) & (BQ - 1)
        c2 = _iota((HB, BQ), 1)
        u2 = (r2 > c2).astype(BF16)
        l2 = (r2 < c2).astype(BF16)
        dk_s[...] = jnp.zeros_like(dk_s)
        dv_s[...] = jnp.zeros_like(dv_s)

        def step(b, qs, dos, rt, j, carry, diag):
            pre_l, pre_g, dq = carry
            k0, k, v = _load_kv(qkv_ref, b, j)
            z = _dot_nt(qs, k)
            lb = _log_sigmoid_tile(z)
            lr = lb - z
            if diag:
                lr = jnp.where(valid, lr, 0.0)
            pre_l = pre_l + jnp.sum(lr, axis=1, keepdims=True)
            a = jnp.exp(lb + _dot(_split2(lr), u2) + (rt - pre_l))
            if diag:
                a = jnp.where(valid, a, 0.0)
            gm = _dot_nt(dos, v) * a
            before = _dot(_split2(gm), l2) + pre_g
            beta = jnp.exp(lb)
            dz = gm * (1.0 - beta) - beta * before
            if diag:
                dz = jnp.where(valid, dz, 0.0)
            dzb = dz.astype(BF16)
            dq = dq + _dot(_lanes(dzb), _stack(k, m0, m1))
            dk_s[pl.ds(k0, BQ), :] += _dot_tn(dzb, qs)
            dv_s[pl.ds(k0, BQ), :] += _dot_tn(a.astype(BF16), dos)
            return pre_l, pre_g + jnp.sum(gm, axis=1, keepdims=True), dq

        def qblock(i, _):
            qs, dos, rts = [], [], []
            for b in range(BL):
                q0 = pl.multiple_of(b * S + i * BQ, BQ)
                r0 = pl.multiple_of(i * BQ, BQ)
                qs.append(_stack(qkv_ref[pl.ds(q0, BQ), 0:BQ] * SCALE, m0, m1))
                dos.append(_stack(do_ref[pl.ds(q0, BQ), :], m0, m1))
                rts.append(jnp.concatenate([r_ref[b, 0, pl.ds(r0, BQ), :], r_ref[b, 1, pl.ds(r0, BQ), :]], axis=0))
            z1 = jnp.zeros((HB, 1), F32)
            carries = tuple((z1, z1, jnp.zeros((BQ, BQ), F32)) for _ in range(BL))
            carries = lax.fori_loop(
                0, i, lambda j, cs: tuple(step(b, qs[b], dos[b], rts[b], j, cs[b], False) for b in range(BL)), carries)
            for b in range(BL):
                _, _, dq = step(b, qs[b], dos[b], rts[b], i, carries[b], True)
                dqkv_ref[pl.ds(pl.multiple_of(b * S + i * BQ, BQ), BQ), 0:BQ] = (dq * SCALE).astype(BF16)
            return 0

        lax.fori_loop(0, NB, qblock, 0)
        dqkv_ref[:, BQ:2 * BQ] = dk_s[...].astype(BF16)
        dqkv_ref[:, 2 * BQ:3 * BQ] = dv_s[...].astype(BF16)

    return pl.pallas_call(
        body, name="sb_bwd", grid=(2,),
        in_specs=[_pairs_spec(0), _pairo_spec(0), _stats_spec()],
        out_specs=_pairs_spec(0),
        out_shape=jax.ShapeDtypeStruct((T, QKVW), BF16),
        scratch_shapes=[pltpu.VMEM((T, BQ), F32)] * 2,
        compiler_params=_cp(("parallel",)))(qkv, dmixed, rtot)


def _flash_fwd2(qkv, mixed, g, fox, bias):
    def body(*refs):
        if fox:
            qkv_ref, cq_ref, ck_ref, _, o_ref, lse_ref = refs
        else:
            qkv_ref, tbl_ref, _, o_ref, lse_ref = refs
        m0, m1 = _lane_masks()
        valid = _diag_valid(False)

        def step(b, qs, cq, i, j, carry, diag):
            m, l, acc = carry
            k0, k, v = _load_kv(qkv_ref, b, j)
            z = _dot_nt(qs, k)
            if fox:
                kk = pl.multiple_of(j * BQ, BQ)
                ck = jnp.concatenate([jnp.broadcast_to(ck_ref[b, 0, :, pl.ds(kk, BQ)], (BQ, BQ)),
                                      jnp.broadcast_to(ck_ref[b, 1, :, pl.ds(kk, BQ)], (BQ, BQ))], axis=0)
                z = z + (cq - ck)
                if diag:
                    z = jnp.where(valid, z, NEG)
            else:
                z = z + tbl_ref[i - j]
            m_new = jnp.maximum(m, jnp.max(z, axis=1, keepdims=True))
            alpha = jnp.exp(m - m_new)
            p = jnp.exp(z - m_new)
            l = alpha * l + jnp.sum(p, axis=1, keepdims=True)
            acc = _per_lane(alpha) * acc + _dot(_lanes(p.astype(BF16)), _stack(v, m0, m1))
            return m_new, l, acc

        def qblock(i, _):
            qs, cqs = [], []
            r0 = pl.multiple_of(i * BQ, BQ)
            for b in range(BL):
                q0 = pl.multiple_of(b * S + i * BQ, BQ)
                qs.append(_stack(qkv_ref[pl.ds(q0, BQ), 0:BQ] * SCALE, m0, m1))
                cqs.append(jnp.concatenate([cq_ref[b, 0, pl.ds(r0, BQ), :], cq_ref[b, 1, pl.ds(r0, BQ), :]], axis=0)
                           if fox else None)
            zero = (jnp.full((HB, 1), NEG, F32), jnp.zeros((HB, 1), F32), jnp.zeros((BQ, BQ), F32))
            carries = tuple(step(b, qs[b], cqs[b], i, i, zero, True) for b in range(BL))
            carries = lax.fori_loop(
                1, i + 1,
                lambda jj, cs: tuple(step(b, qs[b], cqs[b], i, i - jj, cs[b], False) for b in range(BL)), carries)
            for b in range(BL):
                m, l, acc = carries[b]
                o_ref[pl.ds(pl.multiple_of(b * S + i * BQ, BQ), BQ), :] = (acc / _per_lane(l)).astype(BF16)
                lse = m + jnp.log(l)
                lse_ref[b, 0, pl.ds(r0, BQ), :] = lse[:BQ]
                lse_ref[b, 1, pl.ds(r0, BQ), :] = lse[BQ:]
            return 0

        lax.fori_loop(0, NB, qblock, 0)

    bias_specs = [_stats_spec(), _keys_spec()] if fox else [_tbl_spec()]
    n_in = 2 + len(bias_specs)
    return pl.pallas_call(
        body, name="fox_fwd" if fox else "dil_fwd", grid=(2,),
        in_specs=[_pairs_spec(g)] + bias_specs + [ANY_SPEC],
        out_specs=[_pairo_spec(g), _stats_spec()],
        out_shape=[jax.ShapeDtypeStruct((T, D), BF16), jax.ShapeDtypeStruct((BL, 2, 2, S, 1), F32)],
        input_output_aliases={n_in - 1: 0},
        compiler_params=_cp(("parallel",)))(qkv, *bias, mixed)


def _flash_bwd2(qkv, mixed, dmixed, lse, dqkv, g, fox, bias):
    def body(*refs):
        if fox:
            qkv_ref, o_ref, do_ref, lse_ref, cq_ref, ck_ref, _, dqkv_ref, db_ref, dk_s, dv_s = refs
        else:
            qkv_ref, o_ref, do_ref, lse_ref, tbl_ref, _, dqkv_ref, db_ref, dk_s, dv_s = refs
        m0, m1 = _lane_masks()
        valid = _diag_valid(False)
        dk_s[...] = jnp.zeros_like(dk_s)
        dv_s[...] = jnp.zeros_like(dv_s)
        db_ref[...] = jnp.zeros_like(db_ref)

        def probs(b, qs, dos, cq, lse_q, i, j, diag):
            k0, k, v = _load_kv(qkv_ref, b, j)
            z = _dot_nt(qs, k)
            if fox:
                kk = pl.multiple_of(j * BQ, BQ)
                ck = jnp.concatenate([jnp.broadcast_to(ck_ref[b, 0, :, pl.ds(kk, BQ)], (BQ, BQ)),
                                      jnp.broadcast_to(ck_ref[b, 1, :, pl.ds(kk, BQ)], (BQ, BQ))], axis=0)
                z = z + (cq - ck)
                if diag:
                    z = jnp.where(valid, z, NEG)
            else:
                z = z + tbl_ref[i - j]
            return k0, k, jnp.exp(z - lse_q), _dot_nt(dos, v)

        def step(b, qs, dos, cq, lse_q, delta, i, j, dq, diag):
            k0, k, p, dp = probs(b, qs, dos, cq, lse_q, i, j, diag)
            dz = p * (dp - delta)
            dzb = dz.astype(BF16)
            dk_s[pl.ds(k0, BQ), :] += _dot_tn(dzb, qs)
            dv_s[pl.ds(k0, BQ), :] += _dot_tn(p.astype(BF16), dos)
            return dz, dq + _dot(_lanes(dzb), _stack(k, m0, m1))

        def qblock(i, _):
            r0 = pl.multiple_of(i * BQ, BQ)
            qs, dos, cqs, lses, deltas = [], [], [], [], []
            for b in range(BL):
                q0 = pl.multiple_of(b * S + i * BQ, BQ)
                qs.append(_stack(qkv_ref[pl.ds(q0, BQ), 0:BQ] * SCALE, m0, m1))
                dos.append(_stack(do_ref[pl.ds(q0, BQ), :], m0, m1))
                lses.append(jnp.concatenate([lse_ref[b, 0, pl.ds(r0, BQ), :], lse_ref[b, 1, pl.ds(r0, BQ), :]], axis=0))
                if fox:
                    cqs.append(jnp.concatenate([cq_ref[b, 0, pl.ds(r0, BQ), :], cq_ref[b, 1, pl.ds(r0, BQ), :]], axis=0))
                else:
                    cqs.append(None)
                    ob = o_ref[pl.ds(q0, BQ), :].astype(F32)
                    deltas.append(jnp.sum(dos[b].astype(F32) * jnp.concatenate([ob, ob], axis=0), axis=1, keepdims=True))

            if fox:
                def dstep(b, j, acc, diag):
                    _, _, p, dp = probs(b, qs[b], dos[b], cqs[b], lses[b], i, j, diag)
                    return acc + jnp.sum(p * dp, axis=1, keepdims=True)

                deltas = lax.fori_loop(
                    0, i, lambda j, ds: tuple(dstep(b, j, ds[b], False) for b in range(BL)),
                    tuple(jnp.zeros((HB, 1), F32) for _ in range(BL)))
                deltas = [dstep(b, i, deltas[b], True) for b in range(BL)]

            def inner(j, dqs, diag):
                outs = [step(b, qs[b], dos[b], cqs[b], lses[b], deltas[b], i, j, dqs[b], diag) for b in range(BL)]
                if fox:
                    kk = pl.multiple_of(j * BQ, BQ)
                    for b in range(BL):
                        dz = outs[b][0]
                        db_ref[b, 0, :, pl.ds(kk, BQ)] = db_ref[b, 0, :, pl.ds(kk, BQ)] - jnp.sum(dz[:BQ], axis=0, keepdims=True)
                        db_ref[b, 1, :, pl.ds(kk, BQ)] = db_ref[b, 1, :, pl.ds(kk, BQ)] - jnp.sum(dz[BQ:], axis=0, keepdims=True)
                else:
                    db_ref[i - j] = db_ref[i - j] + (outs[0][0] + outs[1][0])
                return tuple(o[1] for o in outs)

            dqs = tuple(jnp.zeros((BQ, BQ), F32) for _ in range(BL))
            dqs = lax.fori_loop(0, i, lambda j, d: inner(j, d, False), dqs)
            dqs = inner(i, dqs, True)
            for b in range(BL):
                dqkv_ref[pl.ds(pl.multiple_of(b * S + i * BQ, BQ), BQ), 0:BQ] = (dqs[b] * SCALE).astype(BF16)
            return 0

        lax.fori_loop(0, NB, qblock, 0)
        dqkv_ref[:, BQ:2 * BQ] = dk_s[...].astype(BF16)
        dqkv_ref[:, 2 * BQ:3 * BQ] = dv_s[...].astype(BF16)

    if fox:
        bias_specs = [_stats_spec(), _keys_spec()]
        db_spec = _keys_spec()
        db_shape = jax.ShapeDtypeStruct((BL, 2, 2, 1, S), F32)
    else:
        bias_specs = [_tbl_spec()]
        db_spec = _tbl_spec()
        db_shape = jax.ShapeDtypeStruct((2, NB, HB, BQ), F32)
    n_in = 5 + len(bias_specs)
    return pl.pallas_call(
        body, name="fox_bwd" if fox else "dil_bwd", grid=(2,),
        in_specs=[_pairs_spec(g), _pairo_spec(g), _pairo_spec(g), _stats_spec()] + bias_specs + [ANY_SPEC],
        out_specs=[_pairs_spec(g), db_spec],
        out_shape=[jax.ShapeDtypeStruct((T, QKVW), BF16), db_shape],
        scratch_shapes=[pltpu.VMEM((T, BQ), F32)] * 2,
        input_output_aliases={n_in - 1: 0},
        compiler_params=_cp(("parallel",)))(qkv, mixed, dmixed, lse, *bias, dqkv)


def _stacked_delta(d):
    return d * BQ + (_iota((HB, BQ), 0) & (BQ - 1)) - _iota((HB, BQ), 1)


def _buckets_in(d):
    lo, hi = max(d * BQ - (BQ - 1), 0), d * BQ + BQ - 1
    return [b for b in range(32) if BUCKET_TH[b] <= hi and (b == 31 or BUCKET_TH[b + 1] > lo)]


def _in_bucket(delta, b):
    m = delta >= BUCKET_TH[b]
    return m if b == 31 else m & (delta < BUCKET_TH[b + 1])


def _dil_table2(rel_bias):
    def body(rb_ref, o_ref):
        for d in range(NB):
            delta = _stacked_delta(d)
            pos = delta >= 0
            n = ((pos & (delta <= 128)).astype(jnp.int32)
                 + (pos & (delta <= 512) & ((delta & 3) == 0)).astype(jnp.int32)
                 + (pos & ((delta & 15) == 0)).astype(jnp.int32))
            logn = jnp.where(n == 3, math.log(3.0), jnp.where(n == 2, math.log(2.0), jnp.where(n == 1, 0.0, NEG)))
            head1 = _iota((HB, BQ), 0) >= BQ
            for p in range(2):
                val = jnp.zeros((HB, BQ), F32)
                for b in _buckets_in(d):
                    val = jnp.where(_in_bucket(delta, b), jnp.where(head1, rb_ref[b, 2 * p + 1], rb_ref[b, 2 * p]), val)
                o_ref[p, d] = val + logn

    return pl.pallas_call(
        body, name="dil_table", in_specs=[pl.BlockSpec(memory_space=pltpu.SMEM)],
        out_specs=pl.BlockSpec(memory_space=pltpu.VMEM),
        out_shape=jax.ShapeDtypeStruct((2, NB, HB, BQ), F32), compiler_params=_cp())(rel_bias)


def _dil_table_bwd2(dtbl):
    def body(dt_ref, o_ref):
        p = pl.program_id(0)
        rowi = _iota((32, BQ), 0)
        lanei = _iota((32, BQ), 1)

        @pl.when(p == 0)
        def _():
            o_ref[...] = jnp.zeros_like(o_ref)

        out = jnp.zeros((32, BQ), F32)
        for b in range(32):
            acc = None
            for d in range(NB):
                if b in _buckets_in(d):
                    t = jnp.where(_in_bucket(_stacked_delta(d), b), dt_ref[d], 0.0)
                    acc = t if acc is None else acc + t
            rs = jnp.sum(acc, axis=1, keepdims=True)
            s0 = jnp.sum(rs[:BQ], axis=0, keepdims=True)
            s1 = jnp.sum(rs[BQ:], axis=0, keepdims=True)
            out = out + jnp.where((rowi == b) & (lanei == 2 * p), s0, 0.0) + jnp.where((rowi == b) & (lanei == 2 * p + 1), s1, 0.0)
        o_ref[...] += out

    return pl.pallas_call(
        body, name="dil_table_bwd", grid=(2,), in_specs=[_tbl_spec()],
        out_specs=pl.BlockSpec((32, BQ), lambda p: (0, 0)),
        out_shape=jax.ShapeDtypeStruct((32, BQ), F32),
        compiler_params=_cp(("arbitrary",)))(dtbl)


def _shift_down(x, n):
    return jnp.where(_iota(x.shape, 0) >= n, pltpu.roll(x, n, 0), 0.0)


def _shift_up(x, n):
    return jnp.where(_iota(x.shape, 0) < S - n, pltpu.roll(x, S - n, 0), 0.0)


def _conv_fwd(conv, cw, mixed):
    W = 256

    def body(c_ref, w_ref, _, o_ref):
        u = c_ref[:, W:2 * W] * c_ref[:, 2 * W:3 * W]
        y = w_ref[0:1, :] * _shift_down(u, 2) + w_ref[1:2, :] * _shift_down(u, 1) + w_ref[2:3, :] * u
        o_ref[...] = (c_ref[:, 0:W] * y).astype(BF16)

    return pl.pallas_call(
        body, name="conv_fwd", grid=(BL,),
        in_specs=[pl.BlockSpec((S, CONVW), lambda b: (b, 0)), pl.BlockSpec((8, W), lambda b: (0, 0)), ANY_SPEC],
        out_specs=pl.BlockSpec((S, W), lambda b: (b, 3)),
        out_shape=jax.ShapeDtypeStruct((T, D), BF16), input_output_aliases={2: 0},
        compiler_params=_cp(("parallel",)))(conv, cw, mixed)


def _conv_bwd(conv, cw, dmixed):
    W = 256

    def body(c_ref, w_ref, do_ref, dc_ref, dw_ref):
        b = pl.program_id(0)
        bg = c_ref[:, 0:W]
        cg = c_ref[:, W:2 * W]
        hv = c_ref[:, 2 * W:3 * W]
        do = do_ref[...].astype(F32)
        u = cg * hv
        u1 = _shift_down(u, 1)
        u2 = _shift_down(u, 2)
        y = w_ref[0:1, :] * u2 + w_ref[1:2, :] * u1 + w_ref[2:3, :] * u
        dy = do * bg
        du = w_ref[2:3, :] * dy + w_ref[1:2, :] * _shift_up(dy, 1) + w_ref[0:1, :] * _shift_up(dy, 2)
        dc_ref[:, 0:W] = (do * y).astype(BF16)
        dc_ref[:, W:2 * W] = (du * hv).astype(BF16)
        dc_ref[:, 2 * W:3 * W] = (du * cg).astype(BF16)
        rowi = _iota((8, W), 0)
        dw = (jnp.where(rowi == 0, jnp.sum(dy * u2, axis=0, keepdims=True), 0.0)
              + jnp.where(rowi == 1, jnp.sum(dy * u1, axis=0, keepdims=True), 0.0)
              + jnp.where(rowi == 2, jnp.sum(dy * u, axis=0, keepdims=True), 0.0))

        @pl.when(b == 0)
        def _():
            dw_ref[...] = dw

        @pl.when(b > 0)
        def _():
            dw_ref[...] += dw

    return pl.pallas_call(
        body, name="conv_bwd", grid=(BL,),
        in_specs=[pl.BlockSpec((S, CONVW), lambda b: (b, 0)), pl.BlockSpec((8, W), lambda b: (0, 0)),
                  pl.BlockSpec((S, W), lambda b: (b, 3))],
        out_specs=[pl.BlockSpec((S, CONVW), lambda b: (b, 0)), pl.BlockSpec((8, W), lambda b: (0, 0))],
        out_shape=[jax.ShapeDtypeStruct((T, CONVW), BF16), jax.ShapeDtypeStruct((8, W), F32)],
        compiler_params=_cp(("arbitrary",)))(conv, cw, dmixed)


def _place():
    x, y, c = lax.axis_index("x"), lax.axis_index("y"), lax.axis_index("c")
    return x, y, c


def _allgather_weights(shards):
    n = len(shards)

    def body(*refs):
        ins, outs = refs[:n], refs[n:2 * n]
        send_sems, recv_sems, local_sems = refs[2 * n:]
        x, y, c = _place()
        me, sibling = (x, y, c), (x, y, 1 - c)
        chips = [(1 - x, y), (x, 1 - y), (1 - x, 1 - y)]

        def slot(a, p):
            return outs[a].at[:, 4 * p[0] + 2 * p[1] + p[2]]

        def copy(a, k, block, to, own=False):
            return pltpu.make_async_remote_copy(
                src_ref=ins[a] if own else slot(a, block), dst_ref=slot(a, block),
                send_sem=send_sems.at[a, k], recv_sem=recv_sems.at[a, k], device_id=to, device_id_type=MESH)

        mine = [pltpu.make_async_copy(ins[a], slot(a, me), local_sems.at[a]) for a in range(n)]
        for cp in mine:
            cp.start()
        first = []
        for a in range(n):
            first.append(copy(a, 0, me, sibling, own=True))
            first += [copy(a, 1 + j, me, (*chip, c), own=True) for j, chip in enumerate(chips)]
        for cp in first:
            cp.start()
        passed = []
        for j, chip in enumerate(chips):
            for a in range(n):
                copy(a, 1 + j, (*chip, c), me).wait_recv()
                cp = copy(a, 4 + j, (*chip, c), sibling)
                cp.start()
                passed.append(cp)
        for a in range(n):
            copy(a, 0, sibling, me).wait_recv()
            for j, chip in enumerate(chips):
                copy(a, 4 + j, (*chip, 1 - c), me).wait_recv()
        for cp in first + passed:
            cp.wait_send()
        for cp in mine:
            cp.wait()

    return pl.pallas_call(
        body, name="allgather_weights", in_specs=[ANY_SPEC] * n, out_specs=[ANY_SPEC] * n,
        out_shape=[jax.ShapeDtypeStruct((s.shape[0], NDEV) + s.shape[1:], s.dtype) for s in shards],
        scratch_shapes=[pltpu.SemaphoreType.DMA((n, 7)), pltpu.SemaphoreType.DMA((n, 7)),
                        pltpu.SemaphoreType.DMA((n,))],
        )(*shards)


def _allreduce_small(v):
    def body(v_ref, o_ref, slots, send_sems, recv_sems):
        x, y, c = _place()
        me = 4 * x + 2 * y + c
        slots[me] = v_ref[...]

        def copy(k):
            peer = (x ^ ((k >> 2) & 1), y ^ ((k >> 1) & 1), c ^ (k & 1))
            return pltpu.make_async_remote_copy(
                src_ref=v_ref, dst_ref=slots.at[me], send_sem=send_sems.at[k - 1], recv_sem=recv_sems.at[k - 1],
                device_id=peer, device_id_type=MESH)

        def arrival(k):
            return pltpu.make_async_remote_copy(
                src_ref=v_ref, dst_ref=slots.at[me ^ k], send_sem=send_sems.at[k - 1], recv_sem=recv_sems.at[k - 1],
                device_id=(x, y, c), device_id_type=MESH)

        sends = [copy(k) for k in range(1, NDEV)]
        for cp in sends:
            cp.start()
        for k in range(1, NDEV):
            arrival(k).wait_recv()
        for cp in sends:
            cp.wait_send()
        acc = slots[0]
        for d in range(1, NDEV):
            acc = acc + slots[d]
        o_ref[...] = acc

    return pl.pallas_call(
        body, name="allreduce_small",
        in_specs=[pl.BlockSpec(memory_space=pltpu.VMEM)], out_specs=pl.BlockSpec(memory_space=pltpu.VMEM),
        out_shape=jax.ShapeDtypeStruct(v.shape, F32),
        scratch_shapes=[pltpu.VMEM((NDEV,) + v.shape, F32), pltpu.SemaphoreType.DMA((NDEV - 1,)),
                        pltpu.SemaphoreType.DMA((NDEV - 1,))],
        )(v)


def _sibling_exchange(grads):
    n = len(grads)

    def body(*refs):
        ins, outs = refs[:n], refs[n:2 * n]
        send_sems, recv_sems = refs[2 * n:]
        x, y, c = _place()
        cps = [pltpu.make_async_remote_copy(
            src_ref=ins[a].at[:, :, 1 - c], dst_ref=outs[a], send_sem=send_sems.at[a], recv_sem=recv_sems.at[a],
            device_id=(x, y, 1 - c), device_id_type=MESH) for a in range(n)]
        for cp in cps:
            cp.start()
        for cp in cps:
            cp.wait()

    return pl.pallas_call(
        body, name="sibling_exchange", in_specs=[ANY_SPEC] * n, out_specs=[ANY_SPEC] * n,
        out_shape=[jax.ShapeDtypeStruct(g.shape[:2] + g.shape[3:], F32) for g in grads],
        scratch_shapes=[pltpu.SemaphoreType.DMA((n,)), pltpu.SemaphoreType.DMA((n,))],
        )(*grads)


def _pair_sum(grad, got, core):
    _, _, _, rows, N = grad.shape

    def body(c_ref, g_ref, r_ref, o_ref):
        o_ref[...] = (g_ref[...] + r_ref[...]).astype(BF16)

    return pl.pallas_call(
        body, name="pair_sum",
        grid_spec=pltpu.PrefetchScalarGridSpec(
            num_scalar_prefetch=1, grid=(2, 4),
            in_specs=[pl.BlockSpec((None, None, None, rows, N), lambda l, k, c: (l, k, c[0], 0, 0)),
                      pl.BlockSpec((None, None, rows, N), lambda l, k, c: (l, k, 0, 0))],
            out_specs=pl.BlockSpec((None, None, rows, N), lambda l, k, c: (l, k, 0, 0))),
        out_shape=jax.ShapeDtypeStruct((2, 4, rows, N), BF16),
        compiler_params=_cp(("parallel", "parallel")))(core, grad, got)


def _chip_exchange(psums):
    n = len(psums)

    def body(*refs):
        ins, outs = refs[:n], refs[n:2 * n]
        send_sems, recv_sems, local_sems = refs[2 * n:]
        x, y, c = _place()
        mychip = 2 * x + y
        chips = [(1 - x, y), (x, 1 - y), (1 - x, 1 - y)]
        local = [pltpu.make_async_copy(ins[a].at[:, mychip], outs[a].at[:, mychip], local_sems.at[a]) for a in range(n)]
        for cp in local:
            cp.start()
        sends = []
        for a in range(n):
            for j, chip in enumerate(chips):
                sends.append(pltpu.make_async_remote_copy(
                    src_ref=ins[a].at[:, 2 * chip[0] + chip[1]], dst_ref=outs[a].at[:, mychip],
                    send_sem=send_sems.at[a, j], recv_sem=recv_sems.at[a, j],
                    device_id=(*chip, c), device_id_type=MESH))
        for cp in sends:
            cp.start()
        for a in range(n):
            for j, chip in enumerate(chips):
                pltpu.make_async_remote_copy(
                    src_ref=ins[a].at[:, mychip], dst_ref=outs[a].at[:, 2 * chip[0] + chip[1]],
                    send_sem=send_sems.at[a, j], recv_sem=recv_sems.at[a, j],
                    device_id=(x, y, c), device_id_type=MESH).wait_recv()
        for cp in sends:
            cp.wait_send()
        for cp in local:
            cp.wait()

    return pl.pallas_call(
        body, name="chip_exchange", in_specs=[ANY_SPEC] * n, out_specs=[ANY_SPEC] * n,
        out_shape=[jax.ShapeDtypeStruct(p.shape, BF16) for p in psums],
        scratch_shapes=[pltpu.SemaphoreType.DMA((n, 3)), pltpu.SemaphoreType.DMA((n, 3)),
                        pltpu.SemaphoreType.DMA((n,))],
        )(*psums)


def _chip_sum(parts):
    _, _, rows, N = parts.shape

    def body(p_ref, o_ref):
        acc = p_ref[0].astype(F32)
        for k in range(1, 4):
            acc = acc + p_ref[k].astype(F32)
        o_ref[...] = acc

    return pl.pallas_call(
        body, name="chip_sum", grid=(2,),
        in_specs=[pl.BlockSpec((None, 4, rows, N), lambda l: (l, 0, 0, 0))],
        out_specs=pl.BlockSpec((None, rows, N), lambda l: (l, 0, 0)),
        out_shape=jax.ShapeDtypeStruct((2, rows, N), F32), compiler_params=_cp(("parallel",)))(parts)


def _permute_in(w):
    lead = w.shape[:-1]
    return w.reshape(lead + (3, 3, 2, BQ)).swapaxes(-2, -3).reshape(lead + (QKVW,))


def _unpermute_in(w):
    lead = w.shape[:-1]
    return w.reshape(lead + (3, 2, 3, BQ)).swapaxes(-2, -3).reshape(lead + (QKVW,))


def _row(v):
    v = v.reshape(-1)
    return jnp.pad(v, (0, D - v.shape[0])).reshape(1, D)


def kernel(x, w_in, f_bias, conv_w, w_out, rel_bias, ln1_g, ln1_b, w_gate, w_up, w_down, ln2_g, ln2_b, loss_target, m_w_in, m_f_bias, m_conv_w, m_w_out, m_rel_bias, m_ln1_g, m_ln1_b, m_w_gate, m_w_up, m_w_down, m_ln2_g, m_ln2_b, v_w_in, v_f_bias, v_conv_w, v_w_out, v_rel_bias, v_ln1_g, v_ln1_b, v_w_gate, v_w_up, v_w_down, v_ln2_g, v_ln2_b):
    xi, yi, ci = _place()
    me = 4 * xi + 2 * yi + ci

    win_s = jnp.concatenate([_permute_in(w_in[..., :QKVW]), w_in[..., QKVW:]], axis=-1)
    win_s = jnp.pad(win_s, ((0, 0), (0, 0), (0, NPAD - NPROJ))).astype(BF16)
    shards = [win_s, w_out.astype(BF16), jnp.swapaxes(w_gate, 1, 2).astype(BF16),
              jnp.swapaxes(w_up, 1, 2).astype(BF16), w_down.astype(BF16)]
    full = _allgather_weights(shards)
    Win, Wout, WgT, WuT, Wd = [f.reshape(2, NDEV * f.shape[2], f.shape[3]) for f in full]

    cw_rows = lax.dynamic_update_slice(jnp.zeros((2, 3, 256), F32), conv_w, (0, 0, me * 32))
    small = jnp.concatenate([_row(cw_rows[0]), _row(cw_rows[1]), jnp.zeros((SMALL_ROWS - 2, D), F32)], axis=0)
    small = _allreduce_small(small)
    cw_full = small[0:2, :CONVW].reshape(2, 3, 256)
    cw8 = jnp.pad(cw_full, ((0, 0), (0, 5), (0, 0)))
    fb = jnp.pad(f_bias, ((0, 0), (0, GATEW - NH))).reshape(2, 1, GATEW)
    tbl = _dil_table2(rel_bias)

    def wcol(layer, K, tn, off):
        return pl.BlockSpec((None, K, tn), lambda i, j: (layer, 0, off + j))

    def arow(tm, K, blk=0):
        return pl.BlockSpec((tm, K), lambda i, j: (i, blk))

    h = x.reshape(T, D)
    hb = h.astype(BF16)
    saved = []
    for l in range(2):
        qkv = _mm([(hb, arow(512, D), Win, wcol(l, D, 768, 0))], nt=False, M=T, N=QKVW, tm=512, tn=768,
                  out_dtype=BF16, name="proj_qkv")
        conv = _mm([(hb, arow(512, D), Win, wcol(l, D, 768, 3))], nt=False, M=T, N=CONVW, tm=512, tn=768,
                   out_dtype=F32, name="proj_conv")
        gate = _mm([(hb, arow(512, D), Win, wcol(l, D, 128, 24))], nt=False, M=T, N=GATEW, tm=512, tn=128,
                   out_dtype=F32, name="proj_gate")
        cum = _fox_prep(gate, fb[l])
        cum4 = cum[:, :NH].reshape(BL, S, NH).transpose(0, 2, 1)
        cq = cum4.reshape(BL, 2, 2, S, 1)
        ck = cum4.reshape(BL, 2, 2, 1, S)
        mixed, rtot = _sb_fwd2(qkv)
        mixed, lse_d = _flash_fwd2(qkv, mixed, 1, False, (tbl,))
        mixed, lse_f = _flash_fwd2(qkv, mixed, 2, True, (cq, ck))
        mixed = _conv_fwd(conv, cw8[l], mixed)
        mix = _mm([(mixed, arow(512, D), Wout, wcol(l, D, 512, 0))], nt=False, M=T, N=D, tm=512, tn=512,
                  out_dtype=F32, name="out_proj")
        x1, xh1, r1, x1b = _ln_fwd(h, mix, ln1_g[l:l + 1], ln1_b[l:l + 1])
        g, u, a = _ffn_up(x1b, WgT, WuT, l)
        ffn = _mm([(a, arow(512, DFF), Wd, wcol(l, DFF, 512, 0))], nt=False, M=T, N=D, tm=512, tn=512,
                  out_dtype=F32, name="ffn_down")
        x2, xh2, r2, x2b = _ln_fwd(x1, ffn, ln2_g[l:l + 1], ln2_b[l:l + 1])
        saved.append(dict(h=hb, qkv=qkv, conv=conv, gate=gate, cq=cq, ck=ck, mixed=mixed, rtot=rtot, lse_d=lse_d,
                          lse_f=lse_f, x1=x1b, xh1=xh1, r1=r1, g=g, u=u, a=a, xh2=xh2, r2=r2))
        h, hb = x2, x2b

    sq, dy = _loss_grad(h, loss_target.reshape(T, D))
    loss = lax.psum(sq[0, 0], ("x", "y", "c")) * (0.5 / D)

    G_in = jnp.zeros((2, D, NPAD), F32)
    G_out = jnp.zeros((2, D, D), F32)
    G_g = jnp.zeros((2, DFF, D), F32)
    G_u = jnp.zeros((2, DFF, D), F32)
    G_d = jnp.zeros((2, DFF, D), F32)
    small_g = {}

    def wrow(layer, tn, K, blk=0):
        return pl.BlockSpec((None, tn, K), lambda i, j: (layer, j, blk))

    for l in (1, 0):
        sv = saved[l]
        ds2, dg2, db2, ds2b = _ln_bwd(dy, sv["xh2"], sv["r2"], ln2_g[l:l + 1])
        dgt, dut = _ffn_da(ds2b, Wd, sv["g"], sv["u"], l)
        G_d = _mm_tn(sv["a"], ds2b, G_d, Ka=DFF, N=D, tm=1408, tn=1024, tk=512, a_off=0, b_off=0, layer=l, ooff=0,
                     name="grad_w_down")
        G_g = _mm_tn(dgt, sv["x1"], G_g, Ka=DFF, N=D, tm=1408, tn=1024, tk=512, a_off=0, b_off=0, layer=l, ooff=0,
                     name="grad_w_gate")
        G_u = _mm_tn(dut, sv["x1"], G_u, Ka=DFF, N=D, tm=1408, tn=1024, tk=512, a_off=0, b_off=0, layer=l, ooff=0,
                     name="grad_w_up")
        dx1 = _mm([(dgt, arow(512, DFF), WgT, wcol(l, DFF, 512, 0)), (dut, arow(512, DFF), WuT, wcol(l, DFF, 512, 0))],
                  nt=False, M=T, N=D, tm=512, tn=512, out_dtype=F32, name="ffn_dx", res=ds2, res_scale=ALPHA)
        ds1, dg1, db1, ds1b = _ln_bwd(dx1, sv["xh1"], sv["r1"], ln1_g[l:l + 1])
        G_out = _mm_tn(sv["mixed"], ds1b, G_out, Ka=D, N=D, tm=1024, tn=1024, tk=512, a_off=0, b_off=0, layer=l, ooff=0,
                       name="grad_w_out")
        dmixed = _mm([(ds1b, arow(512, D), Wout, wrow(l, 512, D))], nt=True, M=T, N=D, tm=512, tn=512,
                     out_dtype=BF16, name="out_proj_dx")
        dqkv = _sb_bwd2(sv["qkv"], dmixed, sv["rtot"])
        dqkv, dtbl = _flash_bwd2(sv["qkv"], sv["mixed"], dmixed, sv["lse_d"], dqkv, 1, False, (tbl,))
        dqkv, dck = _flash_bwd2(sv["qkv"], sv["mixed"], dmixed, sv["lse_f"], dqkv, 2, True, (sv["cq"], sv["ck"]))
        dconv, dcw = _conv_bwd(sv["conv"], cw8[l], dmixed)
        dcum = jnp.pad(dck.reshape(BL, NH, S).transpose(0, 2, 1).reshape(T, NH), ((0, 0), (0, GATEW - NH)))
        dgate, dfb = _fox_post(dcum, sv["gate"], fb[l])
        drb = _dil_table_bwd2(dtbl)
        G_in = _mm_tn(sv["h"], dqkv, G_in, Ka=D, N=QKVW, tm=1024, tn=768, tk=512, a_off=0, b_off=0, layer=l, ooff=0,
                      name="grad_w_in_qkv")
        G_in = _mm_tn(sv["h"], dconv, G_in, Ka=D, N=CONVW, tm=1024, tn=768, tk=512, a_off=0, b_off=0, layer=l, ooff=3,
                      name="grad_w_in_conv")
        G_in = _mm_tn(sv["h"], dgate, G_in, Ka=D, N=GATEW, tm=1024, tn=128, tk=512, a_off=0, b_off=0, layer=l, ooff=24,
                      name="grad_w_in_gate")
        dy = _mm([(dqkv, arow(512, QKVW), Win, wrow(l, 512, QKVW, 0)),
                  (dconv, arow(512, CONVW), Win, wrow(l, 512, CONVW, 3)),
                  (dgate, arow(512, GATEW), Win, wrow(l, 512, GATEW, 24))],
                 nt=True, M=T, N=D, tm=512, tn=512, out_dtype=F32, name="proj_dx", res=ds1, res_scale=ALPHA)
        small_g[l] = dict(ln1_g=dg1, ln1_b=db1, ln2_g=dg2, ln2_b=db2, cw=dcw[0:3].reshape(1, CONVW),
                          fb=dfb[:, :NH], rb=drb[:, :NH])
    grad_x = dy.reshape(BL, S, D)

    rows = []
    for name in ("ln1_g", "ln1_b", "ln2_g", "ln2_b"):
        rows += [small_g[0][name], small_g[1][name]]
    rows += [_row(small_g[0]["cw"]), _row(small_g[1]["cw"]),
             _row(jnp.concatenate([small_g[0]["fb"], small_g[1]["fb"]], axis=0)),
             _row(small_g[0]["rb"] + small_g[1]["rb"])]
    rows.append(jnp.zeros((SMALL_ROWS - len(rows), D), F32))
    sg = _allreduce_small(jnp.concatenate(rows, axis=0))
    g_ln1_g, g_ln1_b, g_ln2_g, g_ln2_b = sg[0:2], sg[2:4], sg[4:6], sg[6:8]
    g_conv_full = sg[8:10, :CONVW].reshape(2, 3, 256)
    g_conv = lax.dynamic_slice(g_conv_full, (0, 0, me * 32), (2, 3, 32))
    g_fb = sg[10, :2 * NH].reshape(2, NH)
    g_rb = sg[11, :32 * NH].reshape(32, NH)

    bufs = [G_in, G_out, G_g, G_u, G_d]
    views = [b.reshape(2, 4, 2, b.shape[1] // NDEV, b.shape[2]) for b in bufs]
    got = _sibling_exchange(views)
    core = jnp.reshape(ci, (1,)).astype(jnp.int32)
    psums = [_pair_sum(vw, gt, core) for vw, gt in zip(views, got)]
    parts = _chip_exchange(psums)
    gs = [_chip_sum(p) for p in parts]
    g_in = gs[0]
    g_w_in = jnp.concatenate([_unpermute_in(g_in[..., :QKVW]), g_in[..., QKVW:NPROJ]], axis=-1)
    g_w_out = gs[1]
    g_w_gate = jnp.swapaxes(gs[2], 1, 2)
    g_w_up = jnp.swapaxes(gs[3], 1, 2)
    g_w_down = gs[4]

    def big(w, g, m, v, tr):
        sh = w.shape
        f = lambda t: t.reshape(-1, sh[-1])
        return [t.reshape(sh) for t in _adamw(f(w), f(g), f(m), f(v), tr)]

    up_in = big(w_in, g_w_in, m_w_in, v_w_in, 64)
    up_out = big(w_out, g_w_out, m_w_out, v_w_out, 128)
    up_gate = big(w_gate, g_w_gate, m_w_gate, v_w_gate, 256)
    up_up = big(w_up, g_w_up, m_w_up, v_w_up, 256)
    up_down = big(w_down, g_w_down, m_w_down, v_w_down, 352)

    def pack(fbv, cwv, rbv, l1g, l1b, l2g, l2b):
        r = [l1g, l1b, l2g, l2b, _row(cwv), _row(fbv), _row(rbv)]
        r.append(jnp.zeros((SMALL_ROWS - 11, D), F32))
        return jnp.concatenate(r, axis=0)

    pw = pack(f_bias, conv_w, rel_bias, ln1_g, ln1_b, ln2_g, ln2_b)
    pg = pack(g_fb, g_conv, g_rb, g_ln1_g, g_ln1_b, g_ln2_g, g_ln2_b)
    pm = pack(m_f_bias, m_conv_w, m_rel_bias, m_ln1_g, m_ln1_b, m_ln2_g, m_ln2_b)
    pv = pack(v_f_bias, v_conv_w, v_rel_bias, v_ln1_g, v_ln1_b, v_ln2_g, v_ln2_b)
    ups = _adamw(pw, pg, pm, pv, SMALL_ROWS)

    def unpack(p):
        return dict(ln1_g=p[0:2], ln1_b=p[2:4], ln2_g=p[4:6], ln2_b=p[6:8],
                    conv_w=p[8, :192].reshape(2, 3, 32), f_bias=p[9, :2 * NH].reshape(2, NH),
                    rel_bias=p[10, :32 * NH].reshape(32, NH))

    sm = [unpack(p) for p in ups]

    def group(k):
        return (up_in[k], sm[k]["f_bias"], sm[k]["conv_w"], up_out[k], sm[k]["rel_bias"], sm[k]["ln1_g"],
                sm[k]["ln1_b"], up_gate[k], up_up[k], up_down[k], sm[k]["ln2_g"], sm[k]["ln2_b"])

    grads = (g_w_in, g_fb, g_conv, g_w_out, g_rb, g_ln1_g, g_ln1_b, g_w_gate, g_w_up, g_w_down, g_ln2_g, g_ln2_b)
    return (loss, grad_x) + grads + group(0) + group(1) + group(2)
```

```python
import math

import numpy as np
import jax
import jax.numpy as jnp
from jax import lax
from jax.experimental import pallas as pl
from jax.experimental.pallas import tpu as pltpu

F32 = jnp.float32
BF16 = jnp.bfloat16
MESH = pl.DeviceIdType.MESH

D = 1024
S = 2048
BL = 2
T = BL * S
NH = 4
DFF = 2816
NPROJ = 3076
NPAD = 3200
QKVW = 2304
CONVW = 768
GATEW = 128
PAIRW = 384
BQ = 128
HB = 2 * BQ
NB = S // BQ
NDEV = 8
NSTAT = BL * NH
ALPHA = 4.0 ** 0.25
SCALE = 0.125
NEG = -1e30
LN_EPS = 1e-5
ADAM_LR, ADAM_B1, ADAM_B2, ADAM_EPS, ADAM_WD, ADAM_STEP = 0.001, 0.9, 0.999, 1e-08, 0.01, 10
VMEM_LIMIT = 48 * 1024 * 1024
SMALL_ROWS = 16


def _bucket_thresholds():
    d = np.arange(0, S)
    nf = np.maximum(d, 1).astype(np.float32)
    large = 16 + (np.log(nf / np.float32(16)) / np.float32(math.log(128)) * np.float32(16)).astype(np.int32)
    b = np.where(d < 16, d, np.minimum(large, 31))
    return [int(np.argmax(b >= k)) for k in range(32)]


BUCKET_TH = _bucket_thresholds()


def _cp(sem=None):
    return pltpu.CompilerParams(dimension_semantics=sem, vmem_limit_bytes=VMEM_LIMIT)


def _dot(a, b):
    return lax.dot_general(a, b, (((1,), (0,)), ((), ())), preferred_element_type=F32)


def _dot_nt(a, b):
    return lax.dot_general(a, b, (((1,), (1,)), ((), ())), preferred_element_type=F32)


def _dot_tn(a, b):
    return lax.dot_general(a, b, (((0,), (0,)), ((), ())), preferred_element_type=F32)


def _split2(x):
    hi = x.astype(BF16)
    mid = (x - hi.astype(F32)).astype(BF16)
    return jnp.concatenate([hi, mid], axis=1)


def _split3(x):
    hi = x.astype(BF16)
    r = x - hi.astype(F32)
    mid = r.astype(BF16)
    lo = (r - mid.astype(F32)).astype(BF16)
    return jnp.concatenate([hi, mid, lo], axis=1)


def _log_sigmoid(u):
    return jnp.minimum(u, 0.0) - jnp.log1p(jnp.exp(-jnp.abs(u)))


def _log_sigmoid_tile(u):
    return jnp.minimum(u, 0.0) - jnp.log(1.0 + jnp.exp(jnp.minimum(u, -u)))


def _iota(shape, dim):
    return lax.broadcasted_iota(jnp.int32, shape, dim)


ANY_SPEC = pl.BlockSpec(memory_space=pl.ANY)
VMEM_SPEC = pl.BlockSpec(memory_space=pltpu.VMEM)


def _mm(pairs, *, nt, M, N, tm, tn, out_dtype, name, res=None, res_scale=1.0):
    n = len(pairs)

    def body(*refs):
        acc = None
        for p in range(n):
            a = refs[2 * p][...].astype(BF16)
            b = refs[2 * p + 1][...]
            d = _dot_nt(a, b) if nt else _dot(a, b)
            acc = d if acc is None else acc + d
        if res is not None:
            acc = acc + res_scale * refs[2 * n][...]
        refs[-1][...] = acc.astype(out_dtype)

    ops, specs = [], []
    for a, asp, b, bsp in pairs:
        ops += [a, b]
        specs += [asp, bsp]
    if res is not None:
        ops.append(res)
        specs.append(pl.BlockSpec((tm, tn), lambda i, j: (i, j)))
    return pl.pallas_call(
        body, name=name, grid=(M // tm, N // tn), in_specs=specs,
        out_specs=pl.BlockSpec((tm, tn), lambda i, j: (i, j)),
        out_shape=jax.ShapeDtypeStruct((M, N), out_dtype),
        compiler_params=_cp(("parallel", "parallel")))(*ops)


def _mm_tn(a, b, gbuf, *, Ka, N, tm, tn, tk, layer, ooff, name):
    def body(a_ref, b_ref, g_in, o_ref):
        k = pl.program_id(2)
        d = _dot_tn(a_ref[...].astype(BF16), b_ref[...].astype(BF16))

        @pl.when(k == 0)
        def _():
            o_ref[...] = d

        @pl.when(k > 0)
        def _():
            o_ref[...] += d

    return pl.pallas_call(
        body, name=name, grid=(Ka // tm, N // tn, T // tk),
        in_specs=[pl.BlockSpec((tk, tm), lambda i, j, k: (k, i)),
                  pl.BlockSpec((tk, tn), lambda i, j, k: (k, j)),
                  ANY_SPEC],
        out_specs=pl.BlockSpec((None, tm, tn), lambda i, j, k: (layer, i, ooff + j)),
        out_shape=jax.ShapeDtypeStruct(gbuf.shape, F32),
        input_output_aliases={2: 0},
        compiler_params=_cp(("parallel", "parallel", "arbitrary")))(a, b, gbuf)


def _ffn_up(x1, wgt, wut, layer):
    tm, tn = 1024, 256

    def body(x_ref, wg_ref, wu_ref, g_ref, u_ref, a_ref):
        xb = x_ref[...]
        g = _dot_nt(xb, wg_ref[...])
        u = _dot_nt(xb, wu_ref[...])
        g_ref[...] = g.astype(BF16)
        u_ref[...] = u.astype(BF16)
        a_ref[...] = (g * jax.nn.sigmoid(g) * u).astype(BF16)

    wspec = pl.BlockSpec((None, tn, D), lambda i, j: (layer, j, 0))
    ospec = pl.BlockSpec((tm, tn), lambda i, j: (i, j))
    return pl.pallas_call(
        body, name="ffn_up", grid=(T // tm, DFF // tn),
        in_specs=[pl.BlockSpec((tm, D), lambda i, j: (i, 0)), wspec, wspec],
        out_specs=[ospec, ospec, ospec],
        out_shape=[jax.ShapeDtypeStruct((T, DFF), BF16)] * 3,
        compiler_params=_cp(("parallel", "parallel")))(x1, wgt, wut)


def _ffn_da(dffn, wd, g, u, layer):
    tm, tn = 1024, 256

    def body(d_ref, wd_ref, g_ref, u_ref, dg_ref, du_ref):
        da = _dot_nt(d_ref[...], wd_ref[...])
        gv = g_ref[...].astype(F32)
        sg = jax.nn.sigmoid(gv)
        dg_ref[...] = (da * u_ref[...].astype(F32) * (sg * (1.0 + gv * (1.0 - sg)))).astype(BF16)
        du_ref[...] = (da * (gv * sg)).astype(BF16)

    ospec = pl.BlockSpec((tm, tn), lambda i, j: (i, j))
    return pl.pallas_call(
        body, name="ffn_da", grid=(T // tm, DFF // tn),
        in_specs=[pl.BlockSpec((tm, D), lambda i, j: (i, 0)),
                  pl.BlockSpec((None, tn, D), lambda i, j: (layer, j, 0)), ospec, ospec],
        out_specs=[ospec, ospec],
        out_shape=[jax.ShapeDtypeStruct((T, DFF), BF16), jax.ShapeDtypeStruct((T, DFF), BF16)],
        compiler_params=_cp(("parallel", "parallel")))(dffn, wd, g, u)


def _ln_fwd(x, f, gam, bet):
    tm = 256

    def body(x_ref, f_ref, g_ref, b_ref, y_ref, xh_ref, r_ref, yb_ref):
        s = ALPHA * x_ref[...] + f_ref[...]
        mu = jnp.mean(s, axis=-1, keepdims=True)
        xc = s - mu
        var = jnp.mean(xc * xc, axis=-1, keepdims=True)
        r = lax.rsqrt(var + LN_EPS)
        xh = xc * r
        xh_ref[...] = xh
        r_ref[...] = r
        y = xh * g_ref[...] + b_ref[...]
        y_ref[...] = y
        yb_ref[...] = y.astype(BF16)

    row = pl.BlockSpec((tm, D), lambda i: (i, 0))
    vec = pl.BlockSpec((1, D), lambda i: (0, 0))
    return pl.pallas_call(
        body, name="ln_fwd", grid=(T // tm,), in_specs=[row, row, vec, vec],
        out_specs=[row, row, pl.BlockSpec((tm, 1), lambda i: (i, 0)), row],
        out_shape=[jax.ShapeDtypeStruct((T, D), F32), jax.ShapeDtypeStruct((T, D), F32),
                   jax.ShapeDtypeStruct((T, 1), F32), jax.ShapeDtypeStruct((T, D), BF16)],
        compiler_params=_cp(("parallel",)))(x, f, gam, bet)


def _ln_bwd(dy, xh, r, gam):
    tm = 256

    def body(dy_ref, xh_ref, r_ref, g_ref, ds_ref, dg_ref, db_ref, dsb_ref):
        i = pl.program_id(0)
        dyv = dy_ref[...]
        xhv = xh_ref[...]
        dxh = dyv * g_ref[...]
        m1 = jnp.mean(dxh, axis=-1, keepdims=True)
        m2 = jnp.mean(dxh * xhv, axis=-1, keepdims=True)
        ds = r_ref[...] * (dxh - m1 - xhv * m2)
        ds_ref[...] = ds
        dsb_ref[...] = ds.astype(BF16)
        pg = jnp.sum(dyv * xhv, axis=0, keepdims=True)
        pb = jnp.sum(dyv, axis=0, keepdims=True)

        @pl.when(i == 0)
        def _():
            dg_ref[...] = pg
            db_ref[...] = pb

        @pl.when(i > 0)
        def _():
            dg_ref[...] += pg
            db_ref[...] += pb

    row = pl.BlockSpec((tm, D), lambda i: (i, 0))
    vec = pl.BlockSpec((1, D), lambda i: (0, 0))
    return pl.pallas_call(
        body, name="ln_bwd", grid=(T // tm,),
        in_specs=[row, row, pl.BlockSpec((tm, 1), lambda i: (i, 0)), vec],
        out_specs=[row, vec, vec, row],
        out_shape=[jax.ShapeDtypeStruct((T, D), F32), jax.ShapeDtypeStruct((1, D), F32),
                   jax.ShapeDtypeStruct((1, D), F32), jax.ShapeDtypeStruct((T, D), BF16)],
        compiler_params=_cp(("arbitrary",)))(dy, xh, r, gam)


def _loss_grad(y, tgt):
    tm = 256

    def body(y_ref, t_ref, l_ref, dy_ref):
        i = pl.program_id(0)
        e = y_ref[...] - t_ref[...]
        dy_ref[...] = e * (1.0 / D)
        p = jnp.sum(jnp.sum(e * e, axis=1, keepdims=True), axis=0, keepdims=True)

        @pl.when(i == 0)
        def _():
            l_ref[...] = p

        @pl.when(i > 0)
        def _():
            l_ref[...] += p

    row = pl.BlockSpec((tm, D), lambda i: (i, 0))
    return pl.pallas_call(
        body, name="loss_grad", grid=(T // tm,), in_specs=[row, row],
        out_specs=[pl.BlockSpec((1, 1), lambda i: (0, 0)), row],
        out_shape=[jax.ShapeDtypeStruct((1, 1), F32), jax.ShapeDtypeStruct((T, D), F32)],
        compiler_params=_cp(("arbitrary",)))(y, tgt)


def _adamw(w, g, m, v, tr):
    R, C = w.shape

    def body(w_ref, g_ref, m_ref, v_ref, d_ref, m2_ref, v2_ref):
        gv = g_ref[...]
        m2 = ADAM_B1 * m_ref[...] + (1.0 - ADAM_B1) * gv
        v2 = ADAM_B2 * v_ref[...] + (1.0 - ADAM_B2) * (gv * gv)
        m_hat = m2 / (1.0 - ADAM_B1 ** ADAM_STEP)
        v_hat = v2 / (1.0 - ADAM_B2 ** ADAM_STEP)
        d_ref[...] = -ADAM_LR * (m_hat / (jnp.sqrt(v_hat) + ADAM_EPS) + ADAM_WD * w_ref[...])
        m2_ref[...] = m2
        v2_ref[...] = v2

    blk = pl.BlockSpec((tr, C), lambda i: (i, 0))
    sh = jax.ShapeDtypeStruct((R, C), F32)
    return pl.pallas_call(
        body, name="adamw", grid=(R // tr,), in_specs=[blk] * 4, out_specs=[blk] * 3,
        out_shape=[sh, sh, sh], compiler_params=_cp(("parallel",)))(w, g, m, v)


CHAINS = [(p, b) for p in range(2) for b in range(BL)]
NC = len(CHAINS)


def _lane_masks():
    lane = _iota((1, BQ), 1)
    m0 = (lane < 64).astype(BF16)
    return m0, 1.0 - m0


def _stack(x, m0, m1):
    return jnp.concatenate([x * m0, x * m1], axis=0)


def _lanes(a):
    return jnp.concatenate([a[:BQ], a[BQ:]], axis=1)


def _per_lane(v):
    return jnp.where(_iota((BQ, BQ), 1) < 64, v[:BQ], v[BQ:])


def _diag_valid(strict):
    r = _iota((HB, BQ), 0) & (BQ - 1)
    c = _iota((HB, BQ), 1)
    return (c < r) if strict else (c <= r)


def _rows(b, i):
    return pl.ds(pl.multiple_of(b * S + i * BQ, BQ), BQ)


def _load_q(qkv_v, p, b, i, m0, m1):
    return _stack(qkv_v[_rows(b, i), p * PAIRW:p * PAIRW + BQ] * SCALE, m0, m1)


def _load_kv(qkv_v, p, b, j):
    r = _rows(b, j)
    return qkv_v[r, p * PAIRW + BQ:p * PAIRW + 2 * BQ], qkv_v[r, p * PAIRW + 2 * BQ:p * PAIRW + 3 * BQ]


def _stat_cols(tile8, p, b):
    lane = _iota((BQ, NSTAT), 1)
    c = b * NH + 2 * p
    return jnp.concatenate([jnp.sum(jnp.where(lane == c, tile8, 0.0), axis=1, keepdims=True),
                            jnp.sum(jnp.where(lane == c + 1, tile8, 0.0), axis=1, keepdims=True)], axis=0)


def _stat_tile(cols):
    lane = _iota((BQ, NSTAT), 1)
    t = jnp.zeros((BQ, NSTAT), F32)
    for (p, b), v in cols.items():
        c = b * NH + 2 * p
        t = t + jnp.where(lane == c, v[:BQ], 0.0) + jnp.where(lane == c + 1, v[BQ:], 0.0)
    return t


def _key_rows(ck_ref, p, b, j):
    c = b * NH + 2 * p
    kk = pl.ds(pl.multiple_of(j * BQ, BQ), BQ)
    return jnp.concatenate([jnp.broadcast_to(ck_ref[c:c + 1, kk], (BQ, BQ)),
                            jnp.broadcast_to(ck_ref[c + 1:c + 2, kk], (BQ, BQ))], axis=0)


def _copy_in(src, dst, sem):
    cp = pltpu.make_async_copy(src, dst, sem)
    cp.start()
    cp.wait()


STAT_SHAPE = jax.ShapeDtypeStruct((S, NSTAT), F32)
SLAB_QKV = pltpu.VMEM((T, 2 * PAIRW), BF16)
SLAB_OUT = pltpu.VMEM((T, 2 * BQ), BF16)
ACC_KV = pltpu.VMEM((2, T, BQ), F32)


def _sb_fwd(qkv):
    def body(qkv_hbm, o_hbm, r_ref, qkv_v, o_v, sem):
        _copy_in(qkv_hbm.at[:, pl.ds(0, 2 * PAIRW)], qkv_v, sem)
        m0, m1 = _lane_masks()
        valid = _diag_valid(True)
        u2 = ((_iota((HB, BQ), 0) & (BQ - 1)) > _iota((HB, BQ), 1)).astype(BF16)

        def steps(qs, j, cs, diag):
            kv = [_load_kv(qkv_v, p, b, j) for p, b in CHAINS]
            zs = [_dot_nt(qs[c], kv[c][0]) for c in range(NC)]
            lbs, lrs = [], []
            for c in range(NC):
                lb = _log_sigmoid_tile(zs[c])
                lr = lb - zs[c]
                if diag:
                    lr = jnp.where(valid, lr, 0.0)
                lbs.append(lb)
                lrs.append(lr)
            tails = [_dot(_split2(lrs[c]), u2) for c in range(NC)]
            out = []
            for c in range(NC):
                tail_c, acc = cs[c]
                a = jnp.exp(lbs[c] + tails[c] + tail_c)
                if diag:
                    a = jnp.where(valid, a, 0.0)
                acc = acc + _dot(_lanes(a.astype(BF16)), _stack(kv[c][1], m0, m1))
                out.append((tail_c + jnp.sum(lrs[c], axis=1, keepdims=True), acc))
            return tuple(out)

        def qblock(i, _):
            qs = [_load_q(qkv_v, p, b, i, m0, m1) for p, b in CHAINS]
            zero = (jnp.zeros((HB, 1), F32), jnp.zeros((BQ, BQ), F32))
            cs = steps(qs, i, (zero,) * NC, True)
            cs = lax.fori_loop(1, i + 1, lambda jj, cs: steps(qs, i - jj, cs, False), cs)
            for c, (p, b) in enumerate(CHAINS):
                o_v[_rows(b, i), p * BQ:(p + 1) * BQ] = cs[c][1].astype(BF16)
            r_ref[_rows(0, i), :] = _stat_tile({pb: cs[c][0] for c, pb in enumerate(CHAINS)})
            return 0

        lax.fori_loop(0, NB, qblock, 0)
        _copy_in(o_v, o_hbm.at[:, pl.ds(0, 2 * BQ)], sem)

    return pl.pallas_call(
        body, name="sb_fwd", in_specs=[ANY_SPEC], out_specs=[ANY_SPEC, VMEM_SPEC],
        out_shape=[jax.ShapeDtypeStruct((T, D), BF16), STAT_SHAPE],
        scratch_shapes=[SLAB_QKV, SLAB_OUT, pltpu.SemaphoreType.DMA],
        compiler_params=_cp())(qkv)


def _sb_bwd(qkv, dmixed, rtot):
    def body(qkv_hbm, do_hbm, r_ref, dqkv_hbm, qkv_v, do_v, dq_v, dk_s, dv_s, sem):
        _copy_in(qkv_hbm.at[:, pl.ds(0, 2 * PAIRW)], qkv_v, sem)
        _copy_in(do_hbm.at[:, pl.ds(0, 2 * BQ)], do_v, sem)
        m0, m1 = _lane_masks()
        valid = _diag_valid(True)
        r2 = _iota((HB, BQ), 0) & (BQ - 1)
        c2 = _iota((HB, BQ), 1)
        u2 = (r2 > c2).astype(BF16)
        l2 = (r2 < c2).astype(BF16)
        dk_s[...] = jnp.zeros_like(dk_s)
        dv_s[...] = jnp.zeros_like(dv_s)

        def steps(qs, dos, rts, j, cs, diag):
            kv = [_load_kv(qkv_v, p, b, j) for p, b in CHAINS]
            zs = [_dot_nt(qs[c], kv[c][0]) for c in range(NC)]
            das = [_dot_nt(dos[c], kv[c][1]) for c in range(NC)]
            lbs, lrs, pre_ls = [], [], []
            for c in range(NC):
                lb = _log_sigmoid_tile(zs[c])
                lr = lb - zs[c]
                if diag:
                    lr = jnp.where(valid, lr, 0.0)
                lbs.append(lb)
                lrs.append(lr)
                pre_ls.append(cs[c][0] + jnp.sum(lr, axis=1, keepdims=True))
            tails = [_dot(_split2(lrs[c]), u2) for c in range(NC)]
            avs, gms = [], []
            for c in range(NC):
                a = jnp.exp(lbs[c] + tails[c] + (rts[c] - pre_ls[c]))
                if diag:
                    a = jnp.where(valid, a, 0.0)
                avs.append(a)
                gms.append(das[c] * a)
            befores = [_dot(_split2(gms[c]), l2) for c in range(NC)]
            dzbs = []
            for c in range(NC):
                beta = jnp.exp(lbs[c])
                dz = gms[c] * (1.0 - beta) - beta * (befores[c] + cs[c][1])
                if diag:
                    dz = jnp.where(valid, dz, 0.0)
                dzbs.append(dz.astype(BF16))
            out = []
            for c, (p, b) in enumerate(CHAINS):
                dq = cs[c][2] + _dot(_lanes(dzbs[c]), _stack(kv[c][0], m0, m1))
                dk_s[p, _rows(b, j), :] += _dot_tn(dzbs[c], qs[c])
                dv_s[p, _rows(b, j), :] += _dot_tn(avs[c].astype(BF16), dos[c])
                out.append((pre_ls[c], cs[c][1] + jnp.sum(gms[c], axis=1, keepdims=True), dq))
            return tuple(out)

        def qblock(i, _):
            r8 = r_ref[_rows(0, i), :]
            qs = [_load_q(qkv_v, p, b, i, m0, m1) for p, b in CHAINS]
            dos = [_stack(do_v[_rows(b, i), p * BQ:(p + 1) * BQ], m0, m1) for p, b in CHAINS]
            rts = [_stat_cols(r8, p, b) for p, b in CHAINS]
            z1 = jnp.zeros((HB, 1), F32)
            cs = ((z1, z1, jnp.zeros((BQ, BQ), F32)),) * NC
            cs = lax.fori_loop(0, i, lambda j, cs: steps(qs, dos, rts, j, cs, False), cs)
            cs = steps(qs, dos, rts, i, cs, True)
            for c, (p, b) in enumerate(CHAINS):
                dq_v[_rows(b, i), p * PAIRW:p * PAIRW + BQ] = (cs[c][2] * SCALE).astype(BF16)
            return 0

        lax.fori_loop(0, NB, qblock, 0)
        for p in range(2):
            dq_v[:, p * PAIRW + BQ:p * PAIRW + 2 * BQ] = dk_s[p].astype(BF16)
            dq_v[:, p * PAIRW + 2 * BQ:p * PAIRW + 3 * BQ] = dv_s[p].astype(BF16)
        _copy_in(dq_v, dqkv_hbm.at[:, pl.ds(0, 2 * PAIRW)], sem)

    return pl.pallas_call(
        body, name="sb_bwd", in_specs=[ANY_SPEC, ANY_SPEC, VMEM_SPEC], out_specs=ANY_SPEC,
        out_shape=jax.ShapeDtypeStruct((T, QKVW), BF16),
        scratch_shapes=[SLAB_QKV, SLAB_OUT, SLAB_QKV, ACC_KV, ACC_KV, pltpu.SemaphoreType.DMA],
        compiler_params=_cp())(qkv, dmixed, rtot)


def _flash_fwd(qkv, mixed, g, fox, bias):
    def body(*refs):
        if fox:
            qkv_hbm, cq_ref, ck_ref, _, o_hbm, lse_ref, qkv_v, o_v, sem = refs
        else:
            qkv_hbm, tbl_ref, _, o_hbm, lse_ref, qkv_v, o_v, sem = refs
        _copy_in(qkv_hbm.at[:, pl.ds(g * 2 * PAIRW, 2 * PAIRW)], qkv_v, sem)
        m0, m1 = _lane_masks()
        valid = _diag_valid(False)

        def steps(qs, cqs, i, j, cs, diag):
            kv = [_load_kv(qkv_v, p, b, j) for p, b in CHAINS]
            zs = [_dot_nt(qs[c], kv[c][0]) for c in range(NC)]
            prs, alphas, out = [], [], []
            for c, (p, b) in enumerate(CHAINS):
                m, l, _ = cs[c]
                if fox:
                    z = zs[c] + (cqs[c] - _key_rows(ck_ref, p, b, j))
                    if diag:
                        z = jnp.where(valid, z, NEG)
                else:
                    z = zs[c] + tbl_ref[p, i - j]
                m_new = jnp.maximum(m, jnp.max(z, axis=1, keepdims=True))
                alpha = jnp.exp(m - m_new)
                pr = jnp.exp(z - m_new)
                prs.append(pr.astype(BF16))
                alphas.append(alpha)
                out.append((m_new, alpha * l + jnp.sum(pr, axis=1, keepdims=True)))
            pvs = [_dot(_lanes(prs[c]), _stack(kv[c][1], m0, m1)) for c in range(NC)]
            return tuple((out[c][0], out[c][1], _per_lane(alphas[c]) * cs[c][2] + pvs[c]) for c in range(NC))

        def qblock(i, _):
            qs = [_load_q(qkv_v, p, b, i, m0, m1) for p, b in CHAINS]
            if fox:
                c8 = cq_ref[_rows(0, i), :]
                cqs = [_stat_cols(c8, p, b) for p, b in CHAINS]
            else:
                cqs = [None] * NC
            zero = (jnp.full((HB, 1), NEG, F32), jnp.zeros((HB, 1), F32), jnp.zeros((BQ, BQ), F32))
            cs = steps(qs, cqs, i, i, (zero,) * NC, True)
            cs = lax.fori_loop(1, i + 1, lambda jj, cs: steps(qs, cqs, i, i - jj, cs, False), cs)
            for c, (p, b) in enumerate(CHAINS):
                m, l, acc = cs[c]
                o_v[_rows(b, i), p * BQ:(p + 1) * BQ] = (acc / _per_lane(l)).astype(BF16)
            lse_ref[_rows(0, i), :] = _stat_tile({pb: cs[c][0] + jnp.log(cs[c][1]) for c, pb in enumerate(CHAINS)})
            return 0

        lax.fori_loop(0, NB, qblock, 0)
        _copy_in(o_v, o_hbm.at[:, pl.ds(g * 2 * BQ, 2 * BQ)], sem)

    bias_specs = [VMEM_SPEC, VMEM_SPEC] if fox else [VMEM_SPEC]
    n_in = 2 + len(bias_specs)
    return pl.pallas_call(
        body, name="fox_fwd" if fox else "dil_fwd",
        in_specs=[ANY_SPEC] + bias_specs + [ANY_SPEC], out_specs=[ANY_SPEC, VMEM_SPEC],
        out_shape=[jax.ShapeDtypeStruct((T, D), BF16), STAT_SHAPE],
        scratch_shapes=[SLAB_QKV, SLAB_OUT, pltpu.SemaphoreType.DMA],
        input_output_aliases={n_in - 1: 0},
        compiler_params=_cp())(qkv, *bias, mixed)


def _flash_bwd(qkv, mixed, dmixed, lse, dqkv, g, fox, bias):
    def body(*refs):
        if fox:
            (qkv_hbm, o_hbm, do_hbm, lse_ref, cq_ref, ck_ref, _, dqkv_hbm, db_ref,
             qkv_v, o_v, do_v, dq_v, dk_s, dv_s, sem) = refs
        else:
            (qkv_hbm, o_hbm, do_hbm, lse_ref, tbl_ref, _, dqkv_hbm, db_ref,
             qkv_v, o_v, do_v, dq_v, dk_s, dv_s, sem) = refs
        _copy_in(qkv_hbm.at[:, pl.ds(g * 2 * PAIRW, 2 * PAIRW)], qkv_v, sem)
        _copy_in(do_hbm.at[:, pl.ds(g * 2 * BQ, 2 * BQ)], do_v, sem)
        if not fox:
            _copy_in(o_hbm.at[:, pl.ds(g * 2 * BQ, 2 * BQ)], o_v, sem)
        m0, m1 = _lane_masks()
        valid = _diag_valid(False)
        dk_s[...] = jnp.zeros_like(dk_s)
        dv_s[...] = jnp.zeros_like(dv_s)
        db_ref[...] = jnp.zeros_like(db_ref)

        def probs(qs, dos, cqs, lses, i, j, diag):
            kv = [_load_kv(qkv_v, p, b, j) for p, b in CHAINS]
            zs = [_dot_nt(qs[c], kv[c][0]) for c in range(NC)]
            dps = [_dot_nt(dos[c], kv[c][1]) for c in range(NC)]
            prs = []
            for c, (p, b) in enumerate(CHAINS):
                if fox:
                    z = zs[c] + (cqs[c] - _key_rows(ck_ref, p, b, j))
                    if diag:
                        z = jnp.where(valid, z, NEG)
                else:
                    z = zs[c] + tbl_ref[p, i - j]
                prs.append(jnp.exp(z - lses[c]))
            return [kv[c][0] for c in range(NC)], prs, dps

        def qblock(i, _):
            l8 = lse_ref[_rows(0, i), :]
            qs = [_load_q(qkv_v, p, b, i, m0, m1) for p, b in CHAINS]
            dos = [_stack(do_v[_rows(b, i), p * BQ:(p + 1) * BQ], m0, m1) for p, b in CHAINS]
            lses = [_stat_cols(l8, p, b) for p, b in CHAINS]
            if fox:
                c8 = cq_ref[_rows(0, i), :]
                cqs = [_stat_cols(c8, p, b) for p, b in CHAINS]

                def dsteps(j, accs, diag):
                    _, prs, dps = probs(qs, dos, cqs, lses, i, j, diag)
                    return tuple(accs[c] + jnp.sum(prs[c] * dps[c], axis=1, keepdims=True) for c in range(NC))

                deltas = lax.fori_loop(0, i, lambda j, ds: dsteps(j, ds, False), (jnp.zeros((HB, 1), F32),) * NC)
                deltas = dsteps(i, deltas, True)
            else:
                cqs = [None] * NC
                deltas = []
                for c, (p, b) in enumerate(CHAINS):
                    ob = o_v[_rows(b, i), p * BQ:(p + 1) * BQ].astype(F32)
                    deltas.append(jnp.sum(dos[c].astype(F32) * jnp.concatenate([ob, ob], axis=0), axis=1, keepdims=True))

            def inner(j, dqs, diag):
                ks, prs, dps = probs(qs, dos, cqs, lses, i, j, diag)
                new, dzs = [], {}
                dzl = [prs[c] * (dps[c] - deltas[c]) for c in range(NC)]
                dzbs = [dz.astype(BF16) for dz in dzl]
                for c, (p, b) in enumerate(CHAINS):
                    dk_s[p, _rows(b, j), :] += _dot_tn(dzbs[c], qs[c])
                    dv_s[p, _rows(b, j), :] += _dot_tn(prs[c].astype(BF16), dos[c])
                    new.append(dqs[c] + _dot(_lanes(dzbs[c]), _stack(ks[c], m0, m1)))
                    dzs[(p, b)] = dzl[c]
                if fox:
                    kk = pl.ds(pl.multiple_of(j * BQ, BQ), BQ)
                    for (p, b), dz in dzs.items():
                        r = b * NH + 2 * p
                        db_ref[r:r + 1, kk] = db_ref[r:r + 1, kk] - jnp.sum(dz[:BQ], axis=0, keepdims=True)
                        db_ref[r + 1:r + 2, kk] = db_ref[r + 1:r + 2, kk] - jnp.sum(dz[BQ:], axis=0, keepdims=True)
                else:
                    for p in range(2):
                        db_ref[p, i - j] = db_ref[p, i - j] + (dzs[(p, 0)] + dzs[(p, 1)])
                return tuple(new)

            dqs = tuple(jnp.zeros((BQ, BQ), F32) for _ in CHAINS)
            dqs = lax.fori_loop(0, i, lambda j, d: inner(j, d, False), dqs)
            dqs = inner(i, dqs, True)
            for c, (p, b) in enumerate(CHAINS):
                dq_v[_rows(b, i), p * PAIRW:p * PAIRW + BQ] = (dqs[c] * SCALE).astype(BF16)
            return 0

        lax.fori_loop(0, NB, qblock, 0)
        for p in range(2):
            dq_v[:, p * PAIRW + BQ:p * PAIRW + 2 * BQ] = dk_s[p].astype(BF16)
            dq_v[:, p * PAIRW + 2 * BQ:p * PAIRW + 3 * BQ] = dv_s[p].astype(BF16)
        _copy_in(dq_v, dqkv_hbm.at[:, pl.ds(g * 2 * PAIRW, 2 * PAIRW)], sem)

    bias_specs = [VMEM_SPEC, VMEM_SPEC] if fox else [VMEM_SPEC]
    db_shape = jax.ShapeDtypeStruct((NSTAT, S), F32) if fox else jax.ShapeDtypeStruct((2, NB, HB, BQ), F32)
    n_in = 5 + len(bias_specs)
    return pl.pallas_call(
        body, name="fox_bwd" if fox else "dil_bwd",
        in_specs=[ANY_SPEC, ANY_SPEC, ANY_SPEC, VMEM_SPEC] + bias_specs + [ANY_SPEC],
        out_specs=[ANY_SPEC, VMEM_SPEC],
        out_shape=[jax.ShapeDtypeStruct((T, QKVW), BF16), db_shape],
        scratch_shapes=[SLAB_QKV, SLAB_OUT, SLAB_OUT, SLAB_QKV, ACC_KV, ACC_KV, pltpu.SemaphoreType.DMA],
        input_output_aliases={n_in - 1: 0},
        compiler_params=_cp())(qkv, mixed, dmixed, lse, *bias, dqkv)


def _stacked_delta(d):
    return d * BQ + (_iota((HB, BQ), 0) & (BQ - 1)) - _iota((HB, BQ), 1)


def _buckets_in(d):
    lo, hi = max(d * BQ - (BQ - 1), 0), d * BQ + BQ - 1
    return [b for b in range(32) if BUCKET_TH[b] <= hi and (b == 31 or BUCKET_TH[b + 1] > lo)]


def _in_bucket(delta, b):
    m = delta >= BUCKET_TH[b]
    return m if b == 31 else m & (delta < BUCKET_TH[b + 1])


def _dil_table(rel_bias):
    def body(rb_ref, o_ref):
        for d in range(NB):
            delta = _stacked_delta(d)
            pos = delta >= 0
            n = ((pos & (delta <= 128)).astype(jnp.int32)
                 + (pos & (delta <= 512) & ((delta & 3) == 0)).astype(jnp.int32)
                 + (pos & ((delta & 15) == 0)).astype(jnp.int32))
            logn = jnp.where(n == 3, math.log(3.0), jnp.where(n == 2, math.log(2.0), jnp.where(n == 1, 0.0, NEG)))
            head1 = _iota((HB, BQ), 0) >= BQ
            for p in range(2):
                val = jnp.zeros((HB, BQ), F32)
                for b in _buckets_in(d):
                    val = jnp.where(_in_bucket(delta, b), jnp.where(head1, rb_ref[b, 2 * p + 1], rb_ref[b, 2 * p]), val)
                o_ref[p, d] = val + logn

    return pl.pallas_call(
        body, name="dil_table", in_specs=[pl.BlockSpec(memory_space=pltpu.SMEM)], out_specs=VMEM_SPEC,
        out_shape=jax.ShapeDtypeStruct((2, NB, HB, BQ), F32), compiler_params=_cp())(rel_bias)


def _dil_table_bwd(dtbl):
    def body(dt_ref, o_ref):
        p = pl.program_id(0)
        rowi = _iota((32, BQ), 0)
        lanei = _iota((32, BQ), 1)

        @pl.when(p == 0)
        def _():
            o_ref[...] = jnp.zeros_like(o_ref)

        out = jnp.zeros((32, BQ), F32)
        for b in range(32):
            acc = None
            for d in range(NB):
                if b in _buckets_in(d):
                    t = jnp.where(_in_bucket(_stacked_delta(d), b), dt_ref[d], 0.0)
                    acc = t if acc is None else acc + t
            rs = jnp.sum(acc, axis=1, keepdims=True)
            s0 = jnp.sum(rs[:BQ], axis=0, keepdims=True)
            s1 = jnp.sum(rs[BQ:], axis=0, keepdims=True)
            out = (out + jnp.where((rowi == b) & (lanei == 2 * p), s0, 0.0)
                   + jnp.where((rowi == b) & (lanei == 2 * p + 1), s1, 0.0))
        o_ref[...] += out

    return pl.pallas_call(
        body, name="dil_table_bwd", grid=(2,),
        in_specs=[pl.BlockSpec((None, NB, HB, BQ), lambda p: (p, 0, 0, 0))],
        out_specs=pl.BlockSpec((32, BQ), lambda p: (0, 0)),
        out_shape=jax.ShapeDtypeStruct((32, BQ), F32),
        compiler_params=_cp(("arbitrary",)))(dtbl)


def _fox_prep(gate, fb):
    def body(g_ref, fb_ref, c_ref):
        tri = (_iota((BQ, BQ), 0) >= _iota((BQ, BQ), 1)).astype(BF16)

        def blk(i, carry):
            r0 = pl.multiple_of(i * BQ, BQ)
            lf = _log_sigmoid(g_ref[pl.ds(r0, BQ), :] + fb_ref[...])
            c = _dot(tri, _split3(lf))
            c_ref[pl.ds(r0, BQ), :] = c[:, 0:BQ] + c[:, BQ:2 * BQ] + c[:, 2 * BQ:3 * BQ] + carry
            return carry + jnp.sum(lf, axis=0, keepdims=True)

        lax.fori_loop(0, NB, blk, jnp.zeros((1, BQ), F32))

    blk = pl.BlockSpec((S, GATEW), lambda b: (b, 0))
    return pl.pallas_call(
        body, name="fox_prep", grid=(BL,), in_specs=[blk, pl.BlockSpec((1, GATEW), lambda b: (0, 0))],
        out_specs=blk, out_shape=jax.ShapeDtypeStruct((T, GATEW), F32),
        compiler_params=_cp(("parallel",)))(gate, fb)


def _fox_post(dcum, gate, fb):
    def body(dc_ref, g_ref, fb_ref, dg_ref, dfb_ref):
        b = pl.program_id(0)
        tri = (_iota((BQ, BQ), 0) <= _iota((BQ, BQ), 1)).astype(BF16)

        def blk(ii, carry):
            csum, dfb = carry
            r0 = pl.multiple_of((NB - 1 - ii) * BQ, BQ)
            dc = dc_ref[pl.ds(r0, BQ), :]
            c = _dot(tri, _split3(dc))
            dlf = c[:, 0:BQ] + c[:, BQ:2 * BQ] + c[:, 2 * BQ:3 * BQ] + csum
            dg = dlf * jnp.exp(_log_sigmoid(-(g_ref[pl.ds(r0, BQ), :] + fb_ref[...])))
            dg_ref[pl.ds(r0, BQ), :] = dg
            return csum + jnp.sum(dc, axis=0, keepdims=True), dfb + jnp.sum(dg, axis=0, keepdims=True)

        z = jnp.zeros((1, BQ), F32)
        _, dfb = lax.fori_loop(0, NB, blk, (z, z))

        @pl.when(b == 0)
        def _():
            dfb_ref[...] = dfb

        @pl.when(b > 0)
        def _():
            dfb_ref[...] += dfb

    blk = pl.BlockSpec((S, GATEW), lambda b: (b, 0))
    vec = pl.BlockSpec((1, GATEW), lambda b: (0, 0))
    return pl.pallas_call(
        body, name="fox_post", grid=(BL,), in_specs=[blk, blk, vec], out_specs=[blk, vec],
        out_shape=[jax.ShapeDtypeStruct((T, GATEW), F32), jax.ShapeDtypeStruct((1, GATEW), F32)],
        compiler_params=_cp(("arbitrary",)))(dcum, gate, fb)


def _shift_down(x, n):
    return jnp.where(_iota(x.shape, 0) >= n, pltpu.roll(x, n, 0), 0.0)


def _shift_up(x, n):
    return jnp.where(_iota(x.shape, 0) < S - n, pltpu.roll(x, S - n, 0), 0.0)


def _conv_fwd(conv, cw, mixed):
    W = 256

    def body(c_ref, w_ref, _, o_ref):
        u = c_ref[:, W:2 * W] * c_ref[:, 2 * W:3 * W]
        y = w_ref[0:1, :] * _shift_down(u, 2) + w_ref[1:2, :] * _shift_down(u, 1) + w_ref[2:3, :] * u
        o_ref[...] = (c_ref[:, 0:W] * y).astype(BF16)

    return pl.pallas_call(
        body, name="conv_fwd", grid=(BL,),
        in_specs=[pl.BlockSpec((S, CONVW), lambda b: (b, 0)), pl.BlockSpec((8, W), lambda b: (0, 0)), ANY_SPEC],
        out_specs=pl.BlockSpec((S, W), lambda b: (b, 3)),
        out_shape=jax.ShapeDtypeStruct((T, D), BF16), input_output_aliases={2: 0},
        compiler_params=_cp(("parallel",)))(conv, cw, mixed)


def _conv_bwd(conv, cw, dmixed):
    W = 256

    def body(c_ref, w_ref, do_ref, dc_ref, dw_ref):
        b = pl.program_id(0)
        bg = c_ref[:, 0:W]
        cg = c_ref[:, W:2 * W]
        hv = c_ref[:, 2 * W:3 * W]
        do = do_ref[...].astype(F32)
        u = cg * hv
        u1 = _shift_down(u, 1)
        u2 = _shift_down(u, 2)
        y = w_ref[0:1, :] * u2 + w_ref[1:2, :] * u1 + w_ref[2:3, :] * u
        dy = do * bg
        du = w_ref[2:3, :] * dy + w_ref[1:2, :] * _shift_up(dy, 1) + w_ref[0:1, :] * _shift_up(dy, 2)
        dc_ref[:, 0:W] = (do * y).astype(BF16)
        dc_ref[:, W:2 * W] = (du * hv).astype(BF16)
        dc_ref[:, 2 * W:3 * W] = (du * cg).astype(BF16)
        rowi = _iota((8, W), 0)
        dw = (jnp.where(rowi == 0, jnp.sum(dy * u2, axis=0, keepdims=True), 0.0)
              + jnp.where(rowi == 1, jnp.sum(dy * u1, axis=0, keepdims=True), 0.0)
              + jnp.where(rowi == 2, jnp.sum(dy * u, axis=0, keepdims=True), 0.0))

        @pl.when(b == 0)
        def _():
            dw_ref[...] = dw

        @pl.when(b > 0)
        def _():
            dw_ref[...] += dw

    return pl.pallas_call(
        body, name="conv_bwd", grid=(BL,),
        in_specs=[pl.BlockSpec((S, CONVW), lambda b: (b, 0)), pl.BlockSpec((8, W), lambda b: (0, 0)),
                  pl.BlockSpec((S, W), lambda b: (b, 3))],
        out_specs=[pl.BlockSpec((S, CONVW), lambda b: (b, 0)), pl.BlockSpec((8, W), lambda b: (0, 0))],
        out_shape=[jax.ShapeDtypeStruct((T, CONVW), BF16), jax.ShapeDtypeStruct((8, W), F32)],
        compiler_params=_cp(("arbitrary",)))(conv, cw, dmixed)


def _place():
    x, y, c = lax.axis_index("x"), lax.axis_index("y"), lax.axis_index("c")
    return x, y, c


def _allgather_weights(shards):
    n = len(shards)

    def body(*refs):
        ins, outs = refs[:n], refs[n:2 * n]
        send_sems, recv_sems, local_sems = refs[2 * n:]
        x, y, c = _place()
        me, sibling = (x, y, c), (x, y, 1 - c)
        chips = [(1 - x, y), (x, 1 - y), (1 - x, 1 - y)]

        def slot(a, p):
            return outs[a].at[:, 4 * p[0] + 2 * p[1] + p[2]]

        def copy(a, k, block, to, own=False):
            return pltpu.make_async_remote_copy(
                src_ref=ins[a] if own else slot(a, block), dst_ref=slot(a, block),
                send_sem=send_sems.at[a, k], recv_sem=recv_sems.at[a, k], device_id=to, device_id_type=MESH)

        mine = [pltpu.make_async_copy(ins[a], slot(a, me), local_sems.at[a]) for a in range(n)]
        for cp in mine:
            cp.start()
        first = []
        for a in range(n):
            first.append(copy(a, 0, me, sibling, own=True))
            first += [copy(a, 1 + j, me, (*chip, c), own=True) for j, chip in enumerate(chips)]
        for cp in first:
            cp.start()
        passed = []
        for j, chip in enumerate(chips):
            for a in range(n):
                copy(a, 1 + j, (*chip, c), me).wait_recv()
                cp = copy(a, 4 + j, (*chip, c), sibling)
                cp.start()
                passed.append(cp)
        for a in range(n):
            copy(a, 0, sibling, me).wait_recv()
            for j, chip in enumerate(chips):
                copy(a, 4 + j, (*chip, 1 - c), me).wait_recv()
        for cp in first + passed:
            cp.wait_send()
        for cp in mine:
            cp.wait()

    return pl.pallas_call(
        body, name="allgather_weights", in_specs=[ANY_SPEC] * n, out_specs=[ANY_SPEC] * n,
        out_shape=[jax.ShapeDtypeStruct((s.shape[0], NDEV) + s.shape[1:], s.dtype) for s in shards],
        scratch_shapes=[pltpu.SemaphoreType.DMA((n, 7)), pltpu.SemaphoreType.DMA((n, 7)),
                        pltpu.SemaphoreType.DMA((n,))],
        )(*shards)


def _allreduce_small(v):
    def body(v_ref, o_ref, slots, send_sems, recv_sems):
        x, y, c = _place()
        me = 4 * x + 2 * y + c
        slots[me] = v_ref[...]

        def copy(k):
            peer = (x ^ ((k >> 2) & 1), y ^ ((k >> 1) & 1), c ^ (k & 1))
            return pltpu.make_async_remote_copy(
                src_ref=v_ref, dst_ref=slots.at[me], send_sem=send_sems.at[k - 1], recv_sem=recv_sems.at[k - 1],
                device_id=peer, device_id_type=MESH)

        def arrival(k):
            return pltpu.make_async_remote_copy(
                src_ref=v_ref, dst_ref=slots.at[me ^ k], send_sem=send_sems.at[k - 1], recv_sem=recv_sems.at[k - 1],
                device_id=(x, y, c), device_id_type=MESH)

        sends = [copy(k) for k in range(1, NDEV)]
        for cp in sends:
            cp.start()
        for k in range(1, NDEV):
            arrival(k).wait_recv()
        for cp in sends:
            cp.wait_send()
        acc = slots[0]
        for d in range(1, NDEV):
            acc = acc + slots[d]
        o_ref[...] = acc

    return pl.pallas_call(
        body, name="allreduce_small", in_specs=[VMEM_SPEC], out_specs=VMEM_SPEC,
        out_shape=jax.ShapeDtypeStruct(v.shape, F32),
        scratch_shapes=[pltpu.VMEM((NDEV,) + v.shape, F32), pltpu.SemaphoreType.DMA((NDEV - 1,)),
                        pltpu.SemaphoreType.DMA((NDEV - 1,))],
        )(v)


def _sibling_exchange(grads):
    n = len(grads)

    def body(*refs):
        ins, outs = refs[:n], refs[n:2 * n]
        send_sems, recv_sems = refs[2 * n:]
        x, y, c = _place()
        cps = [pltpu.make_async_remote_copy(
            src_ref=ins[a].at[:, :, 1 - c], dst_ref=outs[a], send_sem=send_sems.at[a], recv_sem=recv_sems.at[a],
            device_id=(x, y, 1 - c), device_id_type=MESH) for a in range(n)]
        for cp in cps:
            cp.start()
        for cp in cps:
            cp.wait()

    return pl.pallas_call(
        body, name="sibling_exchange", in_specs=[ANY_SPEC] * n, out_specs=[ANY_SPEC] * n,
        out_shape=[jax.ShapeDtypeStruct(g.shape[:2] + g.shape[3:], F32) for g in grads],
        scratch_shapes=[pltpu.SemaphoreType.DMA((n,)), pltpu.SemaphoreType.DMA((n,))],
        )(*grads)


def _pair_sum(grad, got, core):
    _, _, _, rows, N = grad.shape

    def body(c_ref, g_ref, r_ref, o_ref):
        o_ref[...] = (g_ref[...] + r_ref[...]).astype(BF16)

    return pl.pallas_call(
        body, name="pair_sum",
        grid_spec=pltpu.PrefetchScalarGridSpec(
            num_scalar_prefetch=1, grid=(2, 4),
            in_specs=[pl.BlockSpec((None, None, None, rows, N), lambda l, k, c: (l, k, c[0], 0, 0)),
                      pl.BlockSpec((None, None, rows, N), lambda l, k, c: (l, k, 0, 0))],
            out_specs=pl.BlockSpec((None, None, rows, N), lambda l, k, c: (l, k, 0, 0))),
        out_shape=jax.ShapeDtypeStruct((2, 4, rows, N), BF16),
        compiler_params=_cp(("parallel", "parallel")))(core, grad, got)


def _chip_exchange(psums):
    n = len(psums)

    def body(*refs):
        ins, outs = refs[:n], refs[n:2 * n]
        send_sems, recv_sems, local_sems = refs[2 * n:]
        x, y, c = _place()
        mychip = 2 * x + y
        chips = [(1 - x, y), (x, 1 - y), (1 - x, 1 - y)]
        local = [pltpu.make_async_copy(ins[a].at[:, mychip], outs[a].at[:, mychip], local_sems.at[a]) for a in range(n)]
        for cp in local:
            cp.start()
        sends = []
        for a in range(n):
            for j, chip in enumerate(chips):
                sends.append(pltpu.make_async_remote_copy(
                    src_ref=ins[a].at[:, 2 * chip[0] + chip[1]], dst_ref=outs[a].at[:, mychip],
                    send_sem=send_sems.at[a, j], recv_sem=recv_sems.at[a, j],
                    device_id=(*chip, c), device_id_type=MESH))
        for cp in sends:
            cp.start()
        for a in range(n):
            for j, chip in enumerate(chips):
                pltpu.make_async_remote_copy(
                    src_ref=ins[a].at[:, mychip], dst_ref=outs[a].at[:, 2 * chip[0] + chip[1]],
                    send_sem=send_sems.at[a, j], recv_sem=recv_sems.at[a, j],
                    device_id=(x, y, c), device_id_type=MESH).wait_recv()
        for cp in sends:
            cp.wait_send()
        for cp in local:
            cp.wait()

    return pl.pallas_call(
        body, name="chip_exchange", in_specs=[ANY_SPEC] * n, out_specs=[ANY_SPEC] * n,
        out_shape=[jax.ShapeDtypeStruct(p.shape, BF16) for p in psums],
        scratch_shapes=[pltpu.SemaphoreType.DMA((n, 3)), pltpu.SemaphoreType.DMA((n, 3)),
                        pltpu.SemaphoreType.DMA((n,))],
        )(*psums)


def _chip_sum(parts):
    _, _, rows, N = parts.shape

    def body(p_ref, o_ref):
        acc = p_ref[0].astype(F32)
        for k in range(1, 4):
            acc = acc + p_ref[k].astype(F32)
        o_ref[...] = acc

    return pl.pallas_call(
        body, name="chip_sum", grid=(2,),
        in_specs=[pl.BlockSpec((None, 4, rows, N), lambda l: (l, 0, 0, 0))],
        out_specs=pl.BlockSpec((None, rows, N), lambda l: (l, 0, 0)),
        out_shape=jax.ShapeDtypeStruct((2, rows, N), F32), compiler_params=_cp(("parallel",)))(parts)


def _permute_in(w):
    lead = w.shape[:-1]
    return w.reshape(lead + (3, 3, 2, BQ)).swapaxes(-2, -3).reshape(lead + (QKVW,))


def _unpermute_in(w):
    lead = w.shape[:-1]
    return w.reshape(lead + (3, 2, 3, BQ)).swapaxes(-2, -3).reshape(lead + (QKVW,))


def _row(v):
    v = v.reshape(-1)
    return jnp.pad(v, (0, D - v.shape[0])).reshape(1, D)


def kernel(x, w_in, f_bias, conv_w, w_out, rel_bias, ln1_g, ln1_b, w_gate, w_up, w_down, ln2_g, ln2_b, loss_target, m_w_in, m_f_bias, m_conv_w, m_w_out, m_rel_bias, m_ln1_g, m_ln1_b, m_w_gate, m_w_up, m_w_down, m_ln2_g, m_ln2_b, v_w_in, v_f_bias, v_conv_w, v_w_out, v_rel_bias, v_ln1_g, v_ln1_b, v_w_gate, v_w_up, v_w_down, v_ln2_g, v_ln2_b):
    xi, yi, ci = _place()
    me = 4 * xi + 2 * yi + ci

    win_s = jnp.concatenate([_permute_in(w_in[..., :QKVW]), w_in[..., QKVW:]], axis=-1)
    win_s = jnp.pad(win_s, ((0, 0), (0, 0), (0, NPAD - NPROJ))).astype(BF16)
    shards = [win_s, w_out.astype(BF16), jnp.swapaxes(w_gate, 1, 2).astype(BF16),
              jnp.swapaxes(w_up, 1, 2).astype(BF16), w_down.astype(BF16)]
    full = _allgather_weights(shards)
    Win, Wout, WgT, WuT, Wd = [f.reshape(2, NDEV * f.shape[2], f.shape[3]) for f in full]

    cw_rows = lax.dynamic_update_slice(jnp.zeros((2, 3, 256), F32), conv_w, (0, 0, me * 32))
    small = jnp.concatenate([_row(cw_rows[0]), _row(cw_rows[1]), jnp.zeros((SMALL_ROWS - 2, D), F32)], axis=0)
    small = _allreduce_small(small)
    cw_full = small[0:2, :CONVW].reshape(2, 3, 256)
    cw8 = jnp.pad(cw_full, ((0, 0), (0, 5), (0, 0)))
    fb = jnp.pad(f_bias, ((0, 0), (0, GATEW - NH))).reshape(2, 1, GATEW)
    tbl = _dil_table(rel_bias)

    def wcol(layer, K, tn, off):
        return pl.BlockSpec((None, K, tn), lambda i, j: (layer, 0, off + j))

    def arow(tm, K, blk=0):
        return pl.BlockSpec((tm, K), lambda i, j: (i, blk))

    h = x.reshape(T, D)
    hb = h.astype(BF16)
    saved = []
    for l in range(2):
        qkv = _mm([(hb, arow(512, D), Win, wcol(l, D, 768, 0))], nt=False, M=T, N=QKVW, tm=512, tn=768,
                  out_dtype=BF16, name="proj_qkv")
        conv = _mm([(hb, arow(512, D), Win, wcol(l, D, 768, 3))], nt=False, M=T, N=CONVW, tm=512, tn=768,
                   out_dtype=F32, name="proj_conv")
        gate = _mm([(hb, arow(512, D), Win, wcol(l, D, 128, 24))], nt=False, M=T, N=GATEW, tm=512, tn=128,
                   out_dtype=F32, name="proj_gate")
        cum = _fox_prep(gate, fb[l])
        cq = cum[:, :NH].reshape(BL, S, NH).transpose(1, 0, 2).reshape(S, NSTAT)
        ck = cq.T
        mixed, rtot = _sb_fwd(qkv)
        mixed, lse_d = _flash_fwd(qkv, mixed, 1, False, (tbl,))
        mixed, lse_f = _flash_fwd(qkv, mixed, 2, True, (cq, ck))
        mixed = _conv_fwd(conv, cw8[l], mixed)
        mix = _mm([(mixed, arow(512, D), Wout, wcol(l, D, 512, 0))], nt=False, M=T, N=D, tm=512, tn=512,
                  out_dtype=F32, name="out_proj")
        x1, xh1, r1, x1b = _ln_fwd(h, mix, ln1_g[l:l + 1], ln1_b[l:l + 1])
        g, u, a = _ffn_up(x1b, WgT, WuT, l)
        ffn = _mm([(a, arow(512, DFF), Wd, wcol(l, DFF, 512, 0))], nt=False, M=T, N=D, tm=512, tn=512,
                  out_dtype=F32, name="ffn_down")
        x2, xh2, r2, x2b = _ln_fwd(x1, ffn, ln2_g[l:l + 1], ln2_b[l:l + 1])
        saved.append(dict(h=hb, qkv=qkv, conv=conv, gate=gate, cq=cq, ck=ck, mixed=mixed, rtot=rtot, lse_d=lse_d,
                          lse_f=lse_f, x1=x1b, xh1=xh1, r1=r1, g=g, u=u, a=a, xh2=xh2, r2=r2))
        h, hb = x2, x2b

    sq, dy = _loss_grad(h, loss_target.reshape(T, D))
    loss = lax.psum(sq[0, 0], ("x", "y", "c")) * (0.5 / D)

    G_in = jnp.zeros((2, D, NPAD), F32)
    G_out = jnp.zeros((2, D, D), F32)
    G_g = jnp.zeros((2, DFF, D), F32)
    G_u = jnp.zeros((2, DFF, D), F32)
    G_d = jnp.zeros((2, DFF, D), F32)
    small_g = {}

    def wrow(layer, tn, K, blk=0):
        return pl.BlockSpec((None, tn, K), lambda i, j: (layer, j, blk))

    for l in (1, 0):
        sv = saved[l]
        ds2, dg2, db2, ds2b = _ln_bwd(dy, sv["xh2"], sv["r2"], ln2_g[l:l + 1])
        dgt, dut = _ffn_da(ds2b, Wd, sv["g"], sv["u"], l)
        G_d = _mm_tn(sv["a"], ds2b, G_d, Ka=DFF, N=D, tm=1408, tn=1024, tk=512, layer=l, ooff=0, name="grad_w_down")
        G_g = _mm_tn(dgt, sv["x1"], G_g, Ka=DFF, N=D, tm=1408, tn=1024, tk=512, layer=l, ooff=0, name="grad_w_gate")
        G_u = _mm_tn(dut, sv["x1"], G_u, Ka=DFF, N=D, tm=1408, tn=1024, tk=512, layer=l, ooff=0, name="grad_w_up")
        dx1 = _mm([(dgt, arow(512, DFF), WgT, wcol(l, DFF, 512, 0)), (dut, arow(512, DFF), WuT, wcol(l, DFF, 512, 0))],
                  nt=False, M=T, N=D, tm=512, tn=512, out_dtype=F32, name="ffn_dx", res=ds2, res_scale=ALPHA)
        ds1, dg1, db1, ds1b = _ln_bwd(dx1, sv["xh1"], sv["r1"], ln1_g[l:l + 1])
        G_out = _mm_tn(sv["mixed"], ds1b, G_out, Ka=D, N=D, tm=1024, tn=1024, tk=512, layer=l, ooff=0,
                       name="grad_w_out")
        dmixed = _mm([(ds1b, arow(512, D), Wout, wrow(l, 512, D))], nt=True, M=T, N=D, tm=512, tn=512,
                     out_dtype=BF16, name="out_proj_dx")
        dqkv = _sb_bwd(sv["qkv"], dmixed, sv["rtot"])
        dqkv, dtbl = _flash_bwd(sv["qkv"], sv["mixed"], dmixed, sv["lse_d"], dqkv, 1, False, (tbl,))
        dqkv, dck = _flash_bwd(sv["qkv"], sv["mixed"], dmixed, sv["lse_f"], dqkv, 2, True, (sv["cq"], sv["ck"]))
        dconv, dcw = _conv_bwd(sv["conv"], cw8[l], dmixed)
        dcum = jnp.pad(dck.reshape(BL, NH, S).transpose(0, 2, 1).reshape(T, NH), ((0, 0), (0, GATEW - NH)))
        dgate, dfb = _fox_post(dcum, sv["gate"], fb[l])
        drb = _dil_table_bwd(dtbl)
        G_in = _mm_tn(sv["h"], dqkv, G_in, Ka=D, N=QKVW, tm=1024, tn=768, tk=512, layer=l, ooff=0, name="grad_w_in_qkv")
        G_in = _mm_tn(sv["h"], dconv, G_in, Ka=D, N=CONVW, tm=1024, tn=768, tk=512, layer=l, ooff=3,
                      name="grad_w_in_conv")
        G_in = _mm_tn(sv["h"], dgate, G_in, Ka=D, N=GATEW, tm=1024, tn=128, tk=512, layer=l, ooff=24,
                      name="grad_w_in_gate")
        dy = _mm([(dqkv, arow(512, QKVW), Win, wrow(l, 512, QKVW, 0)),
                  (dconv, arow(512, CONVW), Win, wrow(l, 512, CONVW, 3)),
                  (dgate, arow(512, GATEW), Win, wrow(l, 512, GATEW, 24))],
                 nt=True, M=T, N=D, tm=512, tn=512, out_dtype=F32, name="proj_dx", res=ds1, res_scale=ALPHA)
        small_g[l] = dict(ln1_g=dg1, ln1_b=db1, ln2_g=dg2, ln2_b=db2, cw=dcw[0:3].reshape(1, CONVW),
                          fb=dfb[:, :NH], rb=drb[:, :NH])
    grad_x = dy.reshape(BL, S, D)

    rows = []
    for name in ("ln1_g", "ln1_b", "ln2_g", "ln2_b"):
        rows += [small_g[0][name], small_g[1][name]]
    rows += [_row(small_g[0]["cw"]), _row(small_g[1]["cw"]),
             _row(jnp.concatenate([small_g[0]["fb"], small_g[1]["fb"]], axis=0)),
             _row(small_g[0]["rb"] + small_g[1]["rb"])]
    rows.append(jnp.zeros((SMALL_ROWS - len(rows), D), F32))
    sg = _allreduce_small(jnp.concatenate(rows, axis=0))
    g_ln1_g, g_ln1_b, g_ln2_g, g_ln2_b = sg[0:2], sg[2:4], sg[4:6], sg[6:8]
    g_conv_full = sg[8:10, :CONVW].reshape(2, 3, 256)
    g_conv = lax.dynamic_slice(g_conv_full, (0, 0, me * 32), (2, 3, 32))
    g_fb = sg[10, :2 * NH].reshape(2, NH)
    g_rb = sg[11, :32 * NH].reshape(32, NH)

    bufs = [G_in, G_out, G_g, G_u, G_d]
    views = [b.reshape(2, 4, 2, b.shape[1] // NDEV, b.shape[2]) for b in bufs]
    got = _sibling_exchange(views)
    core = jnp.reshape(ci, (1,)).astype(jnp.int32)
    psums = [_pair_sum(vw, gt, core) for vw, gt in zip(views, got)]
    parts = _chip_exchange(psums)
    gs = [_chip_sum(p) for p in parts]
    g_in = gs[0]
    g_w_in = jnp.concatenate([_unpermute_in(g_in[..., :QKVW]), g_in[..., QKVW:NPROJ]], axis=-1)
    g_w_out = gs[1]
    g_w_gate = jnp.swapaxes(gs[2], 1, 2)
    g_w_up = jnp.swapaxes(gs[3], 1, 2)
    g_w_down = gs[4]

    def big(w, g, m, v, tr):
        sh = w.shape
        f = lambda t: t.reshape(-1, sh[-1])
        return [t.reshape(sh) for t in _adamw(f(w), f(g), f(m), f(v), tr)]

    up_in = big(w_in, g_w_in, m_w_in, v_w_in, 64)
    up_out = big(w_out, g_w_out, m_w_out, v_w_out, 128)
    up_gate = big(w_gate, g_w_gate, m_w_gate, v_w_gate, 256)
    up_up = big(w_up, g_w_up, m_w_up, v_w_up, 256)
    up_down = big(w_down, g_w_down, m_w_down, v_w_down, 352)

    def pack(fbv, cwv, rbv, l1g, l1b, l2g, l2b):
        r = [l1g, l1b, l2g, l2b, _row(cwv), _row(fbv), _row(rbv)]
        r.append(jnp.zeros((SMALL_ROWS - 11, D), F32))
        return jnp.concatenate(r, axis=0)

    pw = pack(f_bias, conv_w, rel_bias, ln1_g, ln1_b, ln2_g, ln2_b)
    pg = pack(g_fb, g_conv, g_rb, g_ln1_g, g_ln1_b, g_ln2_g, g_ln2_b)
    pm = pack(m_f_bias, m_conv_w, m_rel_bias, m_ln1_g, m_ln1_b, m_ln2_g, m_ln2_b)
    pv = pack(v_f_bias, v_conv_w, v_rel_bias, v_ln1_g, v_ln1_b, v_ln2_g, v_ln2_b)
    ups = _adamw(pw, pg, pm, pv, SMALL_ROWS)

    def unpack(p):
        return dict(ln1_g=p[0:2], ln1_b=p[2:4], ln2_g=p[4:6], ln2_b=p[6:8],
                    conv_w=p[8, :192].reshape(2, 3, 32), f_bias=p[9, :2 * NH].reshape(2, NH),
                    rel_bias=p[10, :32 * NH].reshape(32, NH))

    sm = [unpack(p) for p in ups]

    def group(k):
        return (up_in[k], sm[k]["f_bias"], sm[k]["conv_w"], up_out[k], sm[k]["rel_bias"], sm[k]["ln1_g"],
                sm[k]["ln1_b"], up_gate[k], up_up[k], up_down[k], sm[k]["ln2_g"], sm[k]["ln2_b"])

    grads = (g_w_in, g_fb, g_conv, g_w_out, g_rb, g_ln1_g, g_ln1_b, g_w_gate, g_w_up, g_w_down, g_ln2_g, g_ln2_b)
    return (loss, grad_x) + grads + group(0) + group(1) + group(2)
```

```python
import math

import numpy as np
import jax
import jax.numpy as jnp
from jax import lax
from jax.experimental import pallas as pl
from jax.experimental.pallas import tpu as pltpu

F32 = jnp.float32
BF16 = jnp.bfloat16
MESH = pl.DeviceIdType.MESH

D = 1024
S = 2048
BL = 2
T = BL * S
NH = 4
DFF = 2816
NPROJ = 3076
NPAD = 3200
QKVW = 2304
CONVW = 768
GATEW = 128
PAIRW = 384
BQ = 128
HB = 2 * BQ
NB = S // BQ
NDEV = 8
NSTAT = BL * NH
ALPHA = 4.0 ** 0.25
SCALE = 0.125
NEG = -1e30
LN_EPS = 1e-5
ADAM_LR, ADAM_B1, ADAM_B2, ADAM_EPS, ADAM_WD, ADAM_STEP = 0.001, 0.9, 0.999, 1e-08, 0.01, 10
VMEM_LIMIT = 48 * 1024 * 1024
SMALL_ROWS = 16


def _bucket_thresholds():
    d = np.arange(0, S)
    nf = np.maximum(d, 1).astype(np.float32)
    large = 16 + (np.log(nf / np.float32(16)) / np.float32(math.log(128)) * np.float32(16)).astype(np.int32)
    b = np.where(d < 16, d, np.minimum(large, 31))
    return [int(np.argmax(b >= k)) for k in range(32)]


BUCKET_TH = _bucket_thresholds()


def _cp(sem=None):
    return pltpu.CompilerParams(dimension_semantics=sem, vmem_limit_bytes=VMEM_LIMIT)


def _dot(a, b):
    return lax.dot_general(a, b, (((1,), (0,)), ((), ())), preferred_element_type=F32)


def _dot_nt(a, b):
    return lax.dot_general(a, b, (((1,), (1,)), ((), ())), preferred_element_type=F32)


def _dot_tn(a, b):
    return lax.dot_general(a, b, (((0,), (0,)), ((), ())), preferred_element_type=F32)


def _split2(x):
    hi = x.astype(BF16)
    mid = (x - hi.astype(F32)).astype(BF16)
    return jnp.concatenate([hi, mid], axis=1)


def _split3(x):
    hi = x.astype(BF16)
    r = x - hi.astype(F32)
    mid = r.astype(BF16)
    lo = (r - mid.astype(F32)).astype(BF16)
    return jnp.concatenate([hi, mid, lo], axis=1)


def _log_sigmoid(u):
    return jnp.minimum(u, 0.0) - jnp.log1p(jnp.exp(-jnp.abs(u)))


def _log_sigmoid_tile(u):
    return jnp.minimum(u, 0.0) - jnp.log(1.0 + jnp.exp(jnp.minimum(u, -u)))


def _iota(shape, dim):
    return lax.broadcasted_iota(jnp.int32, shape, dim)


ANY_SPEC = pl.BlockSpec(memory_space=pl.ANY)
VMEM_SPEC = pl.BlockSpec(memory_space=pltpu.VMEM)


def _mm(pairs, *, nt, M, N, tm, tn, out_dtype, name, res=None, res_scale=1.0):
    n = len(pairs)

    def body(*refs):
        acc = None
        for p in range(n):
            a = refs[2 * p][...].astype(BF16)
            b = refs[2 * p + 1][...]
            d = _dot_nt(a, b) if nt else _dot(a, b)
            acc = d if acc is None else acc + d
        if res is not None:
            acc = acc + res_scale * refs[2 * n][...]
        refs[-1][...] = acc.astype(out_dtype)

    ops, specs = [], []
    for a, asp, b, bsp in pairs:
        ops += [a, b]
        specs += [asp, bsp]
    if res is not None:
        ops.append(res)
        specs.append(pl.BlockSpec((tm, tn), lambda i, j: (i, j)))
    return pl.pallas_call(
        body, name=name, grid=(M // tm, N // tn), in_specs=specs,
        out_specs=pl.BlockSpec((tm, tn), lambda i, j: (i, j)),
        out_shape=jax.ShapeDtypeStruct((M, N), out_dtype),
        compiler_params=_cp(("parallel", "parallel")))(*ops)


def _mm_tn(a, b, gbuf, *, C, Ka, N, tm, tn, tk, ooff, name):
    def body(*refs):
        a_ref, b_ref, o_ref = refs[0], refs[1], refs[-1]
        k = pl.program_id(2)
        d = _dot_tn(a_ref[...].astype(BF16), b_ref[...].astype(BF16))

        @pl.when(k == 0)
        def _():
            o_ref[...] = d

        @pl.when(k > 0)
        def _():
            o_ref[...] += d

    ops = [a, b] + ([] if gbuf is None else [gbuf])
    return pl.pallas_call(
        body, name=name, grid=(Ka // tm, N // tn, T // tk),
        in_specs=[pl.BlockSpec((tk, tm), lambda i, j, k: (k, i)),
                  pl.BlockSpec((tk, tn), lambda i, j, k: (k, j))] + ([] if gbuf is None else [ANY_SPEC]),
        out_specs=pl.BlockSpec((tm, tn), lambda i, j, k: (i, ooff + j)),
        out_shape=jax.ShapeDtypeStruct((Ka, C), F32),
        input_output_aliases={} if gbuf is None else {2: 0},
        compiler_params=_cp(("parallel", "parallel", "arbitrary")))(*ops)


def _ffn_up(x1, wgt, wut, layer):
    tm, tn = 1024, 256

    def body(x_ref, wg_ref, wu_ref, g_ref, u_ref, a_ref):
        xb = x_ref[...]
        g = _dot_nt(xb, wg_ref[...])
        u = _dot_nt(xb, wu_ref[...])
        g_ref[...] = g.astype(BF16)
        u_ref[...] = u.astype(BF16)
        a_ref[...] = (g * jax.nn.sigmoid(g) * u).astype(BF16)

    wspec = pl.BlockSpec((None, tn, D), lambda i, j: (layer, j, 0))
    ospec = pl.BlockSpec((tm, tn), lambda i, j: (i, j))
    return pl.pallas_call(
        body, name="ffn_up", grid=(T // tm, DFF // tn),
        in_specs=[pl.BlockSpec((tm, D), lambda i, j: (i, 0)), wspec, wspec],
        out_specs=[ospec, ospec, ospec],
        out_shape=[jax.ShapeDtypeStruct((T, DFF), BF16)] * 3,
        compiler_params=_cp(("parallel", "parallel")))(x1, wgt, wut)


def _ffn_da(dffn, wd, g, u, layer):
    tm, tn = 1024, 256

    def body(d_ref, wd_ref, g_ref, u_ref, dg_ref, du_ref):
        da = _dot_nt(d_ref[...], wd_ref[...])
        gv = g_ref[...].astype(F32)
        sg = jax.nn.sigmoid(gv)
        dg_ref[...] = (da * u_ref[...].astype(F32) * (sg * (1.0 + gv * (1.0 - sg)))).astype(BF16)
        du_ref[...] = (da * (gv * sg)).astype(BF16)

    ospec = pl.BlockSpec((tm, tn), lambda i, j: (i, j))
    return pl.pallas_call(
        body, name="ffn_da", grid=(T // tm, DFF // tn),
        in_specs=[pl.BlockSpec((tm, D), lambda i, j: (i, 0)),
                  pl.BlockSpec((None, tn, D), lambda i, j: (layer, j, 0)), ospec, ospec],
        out_specs=[ospec, ospec],
        out_shape=[jax.ShapeDtypeStruct((T, DFF), BF16), jax.ShapeDtypeStruct((T, DFF), BF16)],
        compiler_params=_cp(("parallel", "parallel")))(dffn, wd, g, u)


def _ln_fwd(x, f, gam, bet):
    tm = 256

    def body(x_ref, f_ref, g_ref, b_ref, y_ref, xh_ref, r_ref, yb_ref):
        s = ALPHA * x_ref[...] + f_ref[...]
        mu = jnp.mean(s, axis=-1, keepdims=True)
        xc = s - mu
        var = jnp.mean(xc * xc, axis=-1, keepdims=True)
        r = lax.rsqrt(var + LN_EPS)
        xh = xc * r
        xh_ref[...] = xh
        r_ref[...] = r
        y = xh * g_ref[...] + b_ref[...]
        y_ref[...] = y
        yb_ref[...] = y.astype(BF16)

    row = pl.BlockSpec((tm, D), lambda i: (i, 0))
    vec = pl.BlockSpec((1, D), lambda i: (0, 0))
    return pl.pallas_call(
        body, name="ln_fwd", grid=(T // tm,), in_specs=[row, row, vec, vec],
        out_specs=[row, row, pl.BlockSpec((tm, 1), lambda i: (i, 0)), row],
        out_shape=[jax.ShapeDtypeStruct((T, D), F32), jax.ShapeDtypeStruct((T, D), F32),
                   jax.ShapeDtypeStruct((T, 1), F32), jax.ShapeDtypeStruct((T, D), BF16)],
        compiler_params=_cp(("parallel",)))(x, f, gam, bet)


def _ln_bwd(dy, xh, r, gam):
    tm = 256

    def body(dy_ref, xh_ref, r_ref, g_ref, ds_ref, dg_ref, db_ref, dsb_ref):
        i = pl.program_id(0)
        dyv = dy_ref[...]
        xhv = xh_ref[...]
        dxh = dyv * g_ref[...]
        m1 = jnp.mean(dxh, axis=-1, keepdims=True)
        m2 = jnp.mean(dxh * xhv, axis=-1, keepdims=True)
        ds = r_ref[...] * (dxh - m1 - xhv * m2)
        ds_ref[...] = ds
        dsb_ref[...] = ds.astype(BF16)
        pg = jnp.sum(dyv * xhv, axis=0, keepdims=True)
        pb = jnp.sum(dyv, axis=0, keepdims=True)

        @pl.when(i == 0)
        def _():
            dg_ref[...] = pg
            db_ref[...] = pb

        @pl.when(i > 0)
        def _():
            dg_ref[...] += pg
            db_ref[...] += pb

    row = pl.BlockSpec((tm, D), lambda i: (i, 0))
    vec = pl.BlockSpec((1, D), lambda i: (0, 0))
    return pl.pallas_call(
        body, name="ln_bwd", grid=(T // tm,),
        in_specs=[row, row, pl.BlockSpec((tm, 1), lambda i: (i, 0)), vec],
        out_specs=[row, vec, vec, row],
        out_shape=[jax.ShapeDtypeStruct((T, D), F32), jax.ShapeDtypeStruct((1, D), F32),
                   jax.ShapeDtypeStruct((1, D), F32), jax.ShapeDtypeStruct((T, D), BF16)],
        compiler_params=_cp(("arbitrary",)))(dy, xh, r, gam)


def _loss_grad(y, tgt):
    tm = 256

    def body(y_ref, t_ref, l_ref, dy_ref):
        i = pl.program_id(0)
        e = y_ref[...] - t_ref[...]
        dy_ref[...] = e * (1.0 / D)
        p = jnp.sum(jnp.sum(e * e, axis=1, keepdims=True), axis=0, keepdims=True)

        @pl.when(i == 0)
        def _():
            l_ref[...] = p

        @pl.when(i > 0)
        def _():
            l_ref[...] += p

    row = pl.BlockSpec((tm, D), lambda i: (i, 0))
    return pl.pallas_call(
        body, name="loss_grad", grid=(T // tm,), in_specs=[row, row],
        out_specs=[pl.BlockSpec((1, 1), lambda i: (0, 0)), row],
        out_shape=[jax.ShapeDtypeStruct((1, 1), F32), jax.ShapeDtypeStruct((T, D), F32)],
        compiler_params=_cp(("arbitrary",)))(y, tgt)


def _adamw(w, g, m, v, tr):
    R, C = w.shape

    def body(w_ref, g_ref, m_ref, v_ref, d_ref, m2_ref, v2_ref):
        gv = g_ref[...]
        m2 = ADAM_B1 * m_ref[...] + (1.0 - ADAM_B1) * gv
        v2 = ADAM_B2 * v_ref[...] + (1.0 - ADAM_B2) * (gv * gv)
        m_hat = m2 / (1.0 - ADAM_B1 ** ADAM_STEP)
        v_hat = v2 / (1.0 - ADAM_B2 ** ADAM_STEP)
        d_ref[...] = -ADAM_LR * (m_hat / (jnp.sqrt(v_hat) + ADAM_EPS) + ADAM_WD * w_ref[...])
        m2_ref[...] = m2
        v2_ref[...] = v2

    blk = pl.BlockSpec((tr, C), lambda i: (i, 0))
    sh = jax.ShapeDtypeStruct((R, C), F32)
    return pl.pallas_call(
        body, name="adamw", grid=(R // tr,), in_specs=[blk] * 4, out_specs=[blk] * 3,
        out_shape=[sh, sh, sh], compiler_params=_cp(("parallel",)))(w, g, m, v)


CHAINS = [(p, b) for p in range(2) for b in range(BL)]
NC = len(CHAINS)


def _lane_masks():
    lane = _iota((1, BQ), 1)
    m0 = (lane < 64).astype(BF16)
    return m0, 1.0 - m0


def _stack(x, m0, m1):
    return jnp.concatenate([x * m0, x * m1], axis=0)


def _lanes(a):
    return jnp.concatenate([a[:BQ], a[BQ:]], axis=1)


def _per_lane(v):
    return jnp.where(_iota((BQ, BQ), 1) < 64, v[:BQ], v[BQ:])


def _diag_valid(strict):
    r = _iota((HB, BQ), 0) & (BQ - 1)
    c = _iota((HB, BQ), 1)
    return (c < r) if strict else (c <= r)


def _rows(b, i):
    return pl.ds(pl.multiple_of(b * S + i * BQ, BQ), BQ)


def _load_q(qkv_v, p, b, i, m0, m1):
    return _stack(qkv_v[_rows(b, i), p * PAIRW:p * PAIRW + BQ] * SCALE, m0, m1)


def _load_kv(qkv_v, p, b, j):
    r = _rows(b, j)
    return qkv_v[r, p * PAIRW + BQ:p * PAIRW + 2 * BQ], qkv_v[r, p * PAIRW + 2 * BQ:p * PAIRW + 3 * BQ]


def _stat_cols(tile8, p, b):
    lane = _iota((BQ, NSTAT), 1)
    c = b * NH + 2 * p
    return jnp.concatenate([jnp.sum(jnp.where(lane == c, tile8, 0.0), axis=1, keepdims=True),
                            jnp.sum(jnp.where(lane == c + 1, tile8, 0.0), axis=1, keepdims=True)], axis=0)


def _stat_tile(cols):
    lane = _iota((BQ, NSTAT), 1)
    t = jnp.zeros((BQ, NSTAT), F32)
    for (p, b), v in cols.items():
        c = b * NH + 2 * p
        t = t + jnp.where(lane == c, v[:BQ], 0.0) + jnp.where(lane == c + 1, v[BQ:], 0.0)
    return t


def _key_rows(ck_ref, p, b, j):
    c = b * NH + 2 * p
    kk = pl.ds(pl.multiple_of(j * BQ, BQ), BQ)
    return jnp.concatenate([jnp.broadcast_to(ck_ref[c:c + 1, kk], (BQ, BQ)),
                            jnp.broadcast_to(ck_ref[c + 1:c + 2, kk], (BQ, BQ))], axis=0)


def _copy_in(src, dst, sem):
    cp = pltpu.make_async_copy(src, dst, sem)
    cp.start()
    cp.wait()


STAT_SHAPE = jax.ShapeDtypeStruct((S, NSTAT), F32)
SLAB_QKV = pltpu.VMEM((T, 2 * PAIRW), BF16)
SLAB_OUT = pltpu.VMEM((T, 2 * BQ), BF16)
ACC_KV = pltpu.VMEM((2, T, BQ), F32)


class _Job:
    def __init__(self, ins, out_shapes, aliases, sems, start, finish):
        self.ins, self.out_shapes, self.aliases, self.sems = list(ins), list(out_shapes), dict(aliases), list(sems)
        self.start, self.finish = start, finish


def _host_call(body, name, ins, in_specs, out_shapes, out_specs, scratch, aliases, job):
    n_in, n_out, n_scr = len(ins), len(out_shapes), len(scratch)
    jins = job.ins if job else []
    jouts = job.out_shapes if job else []
    jsems = job.sems if job else []

    def wrapped(*refs):
        a = n_in
        b = a + len(jins)
        c = b + n_out
        d = c + len(jouts)
        e = d + n_scr
        comm = None
        if job:
            jrefs = (refs[a:b], refs[c:d], refs[e:])
            comm = (lambda: job.start(*jrefs), lambda st: job.finish(st, *jrefs))
        body(refs[:a], refs[b:c], refs[d:e], comm)

    al = dict(aliases)
    if job:
        for ji, jo in job.aliases.items():
            al[n_in + ji] = n_out + jo
    res = pl.pallas_call(
        wrapped, name=name, in_specs=list(in_specs) + [ANY_SPEC] * len(jins),
        out_specs=list(out_specs) + [ANY_SPEC] * len(jouts), out_shape=list(out_shapes) + list(jouts),
        scratch_shapes=list(scratch) + list(jsems), input_output_aliases=al,
        compiler_params=_cp())(*ins, *jins)
    return res[:n_out], res[n_out:]


def _sb_fwd(qkv, job=None):
    def body(ins, outs, scr, comm):
        (qkv_hbm,), (o_hbm, r_ref), (qkv_v, o_v, sem) = ins, outs, scr
        _copy_in(qkv_hbm.at[:, pl.ds(0, 2 * PAIRW)], qkv_v, sem)
        st = comm[0]() if comm else None
        m0, m1 = _lane_masks()
        valid = _diag_valid(True)
        u2 = ((_iota((HB, BQ), 0) & (BQ - 1)) > _iota((HB, BQ), 1)).astype(BF16)

        def steps(qs, j, cs, diag):
            kv = [_load_kv(qkv_v, p, b, j) for p, b in CHAINS]
            zs = [_dot_nt(qs[c], kv[c][0]) for c in range(NC)]
            lbs, lrs = [], []
            for c in range(NC):
                lb = _log_sigmoid_tile(zs[c])
                lr = lb - zs[c]
                if diag:
                    lr = jnp.where(valid, lr, 0.0)
                lbs.append(lb)
                lrs.append(lr)
            tails = [_dot(_split2(lrs[c]), u2) for c in range(NC)]
            out = []
            for c in range(NC):
                tail_c, acc = cs[c]
                a = jnp.exp(lbs[c] + tails[c] + tail_c)
                if diag:
                    a = jnp.where(valid, a, 0.0)
                acc = acc + _dot(_lanes(a.astype(BF16)), _stack(kv[c][1], m0, m1))
                out.append((tail_c + jnp.sum(lrs[c], axis=1, keepdims=True), acc))
            return tuple(out)

        def qblock(i, _):
            qs = [_load_q(qkv_v, p, b, i, m0, m1) for p, b in CHAINS]
            zero = (jnp.zeros((HB, 1), F32), jnp.zeros((BQ, BQ), F32))
            cs = steps(qs, i, (zero,) * NC, True)
            cs = lax.fori_loop(1, i + 1, lambda jj, cs: steps(qs, i - jj, cs, False), cs)
            for c, (p, b) in enumerate(CHAINS):
                o_v[_rows(b, i), p * BQ:(p + 1) * BQ] = cs[c][1].astype(BF16)
            r_ref[_rows(0, i), :] = _stat_tile({pb: cs[c][0] for c, pb in enumerate(CHAINS)})
            return 0

        lax.fori_loop(0, NB, qblock, 0)
        _copy_in(o_v, o_hbm.at[:, pl.ds(0, 2 * BQ)], sem)
        if comm:
            comm[1](st)

    (mixed, rtot), extra = _host_call(
        body, "sb_fwd", [qkv], [ANY_SPEC], [jax.ShapeDtypeStruct((T, D), BF16), STAT_SHAPE], [ANY_SPEC, VMEM_SPEC],
        [SLAB_QKV, SLAB_OUT, pltpu.SemaphoreType.DMA], {}, job)
    return mixed, rtot, extra


def _sb_bwd(qkv, dmixed, rtot, job=None):
    def body(ins, outs, scr, comm):
        (qkv_hbm, do_hbm, r_ref), (dqkv_hbm,), (qkv_v, do_v, dq_v, dk_s, dv_s, sem) = ins, outs, scr
        _copy_in(qkv_hbm.at[:, pl.ds(0, 2 * PAIRW)], qkv_v, sem)
        _copy_in(do_hbm.at[:, pl.ds(0, 2 * BQ)], do_v, sem)
        st = comm[0]() if comm else None
        m0, m1 = _lane_masks()
        valid = _diag_valid(True)
        r2 = _iota((HB, BQ), 0) & (BQ - 1)
        c2 = _iota((HB, BQ), 1)
        u2 = (r2 > c2).astype(BF16)
        l2 = (r2 < c2).astype(BF16)
        dk_s[...] = jnp.zeros_like(dk_s)
        dv_s[...] = jnp.zeros_like(dv_s)

        def steps(qs, dos, rts, j, cs, diag):
            kv = [_load_kv(qkv_v, p, b, j) for p, b in CHAINS]
            zs = [_dot_nt(qs[c], kv[c][0]) for c in range(NC)]
            das = [_dot_nt(dos[c], kv[c][1]) for c in range(NC)]
            lbs, lrs, pre_ls = [], [], []
            for c in range(NC):
                lb = _log_sigmoid_tile(zs[c])
                lr = lb - zs[c]
                if diag:
                    lr = jnp.where(valid, lr, 0.0)
                lbs.append(lb)
                lrs.append(lr)
                pre_ls.append(cs[c][0] + jnp.sum(lr, axis=1, keepdims=True))
            tails = [_dot(_split2(lrs[c]), u2) for c in range(NC)]
            avs, gms = [], []
            for c in range(NC):
                a = jnp.exp(lbs[c] + tails[c] + (rts[c] - pre_ls[c]))
                if diag:
                    a = jnp.where(valid, a, 0.0)
                avs.append(a)
                gms.append(das[c] * a)
            befores = [_dot(_split2(gms[c]), l2) for c in range(NC)]
            dzbs = []
            for c in range(NC):
                beta = jnp.exp(lbs[c])
                dz = gms[c] * (1.0 - beta) - beta * (befores[c] + cs[c][1])
                if diag:
                    dz = jnp.where(valid, dz, 0.0)
                dzbs.append(dz.astype(BF16))
            out = []
            for c, (p, b) in enumerate(CHAINS):
                dq = cs[c][2] + _dot(_lanes(dzbs[c]), _stack(kv[c][0], m0, m1))
                dk_s[p, _rows(b, j), :] += _dot_tn(dzbs[c], qs[c])
                dv_s[p, _rows(b, j), :] += _dot_tn(avs[c].astype(BF16), dos[c])
                out.append((pre_ls[c], cs[c][1] + jnp.sum(gms[c], axis=1, keepdims=True), dq))
            return tuple(out)

        def qblock(i, _):
            r8 = r_ref[_rows(0, i), :]
            qs = [_load_q(qkv_v, p, b, i, m0, m1) for p, b in CHAINS]
            dos = [_stack(do_v[_rows(b, i), p * BQ:(p + 1) * BQ], m0, m1) for p, b in CHAINS]
            rts = [_stat_cols(r8, p, b) for p, b in CHAINS]
            z1 = jnp.zeros((HB, 1), F32)
            cs = ((z1, z1, jnp.zeros((BQ, BQ), F32)),) * NC
            cs = lax.fori_loop(0, i, lambda j, cs: steps(qs, dos, rts, j, cs, False), cs)
            cs = steps(qs, dos, rts, i, cs, True)
            for c, (p, b) in enumerate(CHAINS):
                dq_v[_rows(b, i), p * PAIRW:p * PAIRW + BQ] = (cs[c][2] * SCALE).astype(BF16)
            return 0

        lax.fori_loop(0, NB, qblock, 0)
        for p in range(2):
            dq_v[:, p * PAIRW + BQ:p * PAIRW + 2 * BQ] = dk_s[p].astype(BF16)
            dq_v[:, p * PAIRW + 2 * BQ:p * PAIRW + 3 * BQ] = dv_s[p].astype(BF16)
        _copy_in(dq_v, dqkv_hbm.at[:, pl.ds(0, 2 * PAIRW)], sem)
        if comm:
            comm[1](st)

    (dqkv,), extra = _host_call(
        body, "sb_bwd", [qkv, dmixed, rtot], [ANY_SPEC, ANY_SPEC, VMEM_SPEC],
        [jax.ShapeDtypeStruct((T, QKVW), BF16)], [ANY_SPEC],
        [SLAB_QKV, SLAB_OUT, SLAB_QKV, ACC_KV, ACC_KV, pltpu.SemaphoreType.DMA], {}, job)
    return dqkv, extra


def _flash_fwd(qkv, mixed, g, fox, bias, job=None):
    def body(ins, outs, scr, comm):
        if fox:
            qkv_hbm, cq_ref, ck_ref, _ = ins
        else:
            qkv_hbm, tbl_ref, _ = ins
        (o_hbm, lse_ref), (qkv_v, o_v, sem) = outs, scr
        _copy_in(qkv_hbm.at[:, pl.ds(g * 2 * PAIRW, 2 * PAIRW)], qkv_v, sem)
        st = comm[0]() if comm else None
        m0, m1 = _lane_masks()
        valid = _diag_valid(False)

        def steps(qs, cqs, i, j, cs, diag):
            kv = [_load_kv(qkv_v, p, b, j) for p, b in CHAINS]
            zs = [_dot_nt(qs[c], kv[c][0]) for c in range(NC)]
            prs, alphas, out = [], [], []
            for c, (p, b) in enumerate(CHAINS):
                m, l, _ = cs[c]
                if fox:
                    z = zs[c] + (cqs[c] - _key_rows(ck_ref, p, b, j))
                    if diag:
                        z = jnp.where(valid, z, NEG)
                else:
                    z = zs[c] + tbl_ref[p, i - j]
                m_new = jnp.maximum(m, jnp.max(z, axis=1, keepdims=True))
                alpha = jnp.exp(m - m_new)
                pr = jnp.exp(z - m_new)
                prs.append(pr.astype(BF16))
                alphas.append(alpha)
                out.append((m_new, alpha * l + jnp.sum(pr, axis=1, keepdims=True)))
            pvs = [_dot(_lanes(prs[c]), _stack(kv[c][1], m0, m1)) for c in range(NC)]
            return tuple((out[c][0], out[c][1], _per_lane(alphas[c]) * cs[c][2] + pvs[c]) for c in range(NC))

        def qblock(i, _):
            qs = [_load_q(qkv_v, p, b, i, m0, m1) for p, b in CHAINS]
            if fox:
                c8 = cq_ref[_rows(0, i), :]
                cqs = [_stat_cols(c8, p, b) for p, b in CHAINS]
            else:
                cqs = [None] * NC
            zero = (jnp.full((HB, 1), NEG, F32), jnp.zeros((HB, 1), F32), jnp.zeros((BQ, BQ), F32))
            cs = steps(qs, cqs, i, i, (zero,) * NC, True)
            cs = lax.fori_loop(1, i + 1, lambda jj, cs: steps(qs, cqs, i, i - jj, cs, False), cs)
            for c, (p, b) in enumerate(CHAINS):
                m, l, acc = cs[c]
                o_v[_rows(b, i), p * BQ:(p + 1) * BQ] = (acc / _per_lane(l)).astype(BF16)
            lse_ref[_rows(0, i), :] = _stat_tile({pb: cs[c][0] + jnp.log(cs[c][1]) for c, pb in enumerate(CHAINS)})
            return 0

        lax.fori_loop(0, NB, qblock, 0)
        _copy_in(o_v, o_hbm.at[:, pl.ds(g * 2 * BQ, 2 * BQ)], sem)
        if comm:
            comm[1](st)

    bias_specs = [VMEM_SPEC, VMEM_SPEC] if fox else [VMEM_SPEC]
    n_in = 2 + len(bias_specs)
    (mixed, lse), extra = _host_call(
        body, "fox_fwd" if fox else "dil_fwd", [qkv, *bias, mixed], [ANY_SPEC] + bias_specs + [ANY_SPEC],
        [jax.ShapeDtypeStruct((T, D), BF16), STAT_SHAPE], [ANY_SPEC, VMEM_SPEC],
        [SLAB_QKV, SLAB_OUT, pltpu.SemaphoreType.DMA], {n_in - 1: 0}, job)
    return mixed, lse, extra


def _flash_bwd(qkv, mixed, dmixed, lse, dqkv, g, fox, bias, job=None):
    def body(ins, outs, scr, comm):
        if fox:
            qkv_hbm, o_hbm, do_hbm, lse_ref, cq_ref, ck_ref, _ = ins
        else:
            qkv_hbm, o_hbm, do_hbm, lse_ref, tbl_ref, _ = ins
        (dqkv_hbm, db_ref), (qkv_v, o_v, do_v, dq_v, dk_s, dv_s, sem) = outs, scr
        _copy_in(qkv_hbm.at[:, pl.ds(g * 2 * PAIRW, 2 * PAIRW)], qkv_v, sem)
        _copy_in(do_hbm.at[:, pl.ds(g * 2 * BQ, 2 * BQ)], do_v, sem)
        if not fox:
            _copy_in(o_hbm.at[:, pl.ds(g * 2 * BQ, 2 * BQ)], o_v, sem)
        st = comm[0]() if comm else None
        m0, m1 = _lane_masks()
        valid = _diag_valid(False)
        dk_s[...] = jnp.zeros_like(dk_s)
        dv_s[...] = jnp.zeros_like(dv_s)
        db_ref[...] = jnp.zeros_like(db_ref)

        def probs(qs, dos, cqs, lses, i, j, diag):
            kv = [_load_kv(qkv_v, p, b, j) for p, b in CHAINS]
            zs = [_dot_nt(qs[c], kv[c][0]) for c in range(NC)]
            dps = [_dot_nt(dos[c], kv[c][1]) for c in range(NC)]
            prs = []
            for c, (p, b) in enumerate(CHAINS):
                if fox:
                    z = zs[c] + (cqs[c] - _key_rows(ck_ref, p, b, j))
                    if diag:
                        z = jnp.where(valid, z, NEG)
                else:
                    z = zs[c] + tbl_ref[p, i - j]
                prs.append(jnp.exp(z - lses[c]))
            return [kv[c][0] for c in range(NC)], prs, dps

        def qblock(i, _):
            l8 = lse_ref[_rows(0, i), :]
            qs = [_load_q(qkv_v, p, b, i, m0, m1) for p, b in CHAINS]
            dos = [_stack(do_v[_rows(b, i), p * BQ:(p + 1) * BQ], m0, m1) for p, b in CHAINS]
            lses = [_stat_cols(l8, p, b) for p, b in CHAINS]
            if fox:
                c8 = cq_ref[_rows(0, i), :]
                cqs = [_stat_cols(c8, p, b) for p, b in CHAINS]

                def dsteps(j, accs, diag):
                    _, prs, dps = probs(qs, dos, cqs, lses, i, j, diag)
                    return tuple(accs[c] + jnp.sum(prs[c] * dps[c], axis=1, keepdims=True) for c in range(NC))

                deltas = lax.fori_loop(0, i, lambda j, ds: dsteps(j, ds, False), (jnp.zeros((HB, 1), F32),) * NC)
                deltas = dsteps(i, deltas, True)
            else:
                cqs = [None] * NC
                deltas = []
                for c, (p, b) in enumerate(CHAINS):
                    ob = o_v[_rows(b, i), p * BQ:(p + 1) * BQ].astype(F32)
                    deltas.append(jnp.sum(dos[c].astype(F32) * jnp.concatenate([ob, ob], axis=0), axis=1, keepdims=True))

            def inner(j, dqs, diag):
                ks, prs, dps = probs(qs, dos, cqs, lses, i, j, diag)
                new, dzs = [], {}
                dzl = [prs[c] * (dps[c] - deltas[c]) for c in range(NC)]
                dzbs = [dz.astype(BF16) for dz in dzl]
                for c, (p, b) in enumerate(CHAINS):
                    dk_s[p, _rows(b, j), :] += _dot_tn(dzbs[c], qs[c])
                    dv_s[p, _rows(b, j), :] += _dot_tn(prs[c].astype(BF16), dos[c])
                    new.append(dqs[c] + _dot(_lanes(dzbs[c]), _stack(ks[c], m0, m1)))
                    dzs[(p, b)] = dzl[c]
                if fox:
                    kk = pl.ds(pl.multiple_of(j * BQ, BQ), BQ)
                    for (p, b), dz in dzs.items():
                        r = b * NH + 2 * p
                        db_ref[r:r + 1, kk] = db_ref[r:r + 1, kk] - jnp.sum(dz[:BQ], axis=0, keepdims=True)
                        db_ref[r + 1:r + 2, kk] = db_ref[r + 1:r + 2, kk] - jnp.sum(dz[BQ:], axis=0, keepdims=True)
                else:
                    for p in range(2):
                        db_ref[p, i - j] = db_ref[p, i - j] + (dzs[(p, 0)] + dzs[(p, 1)])
                return tuple(new)

            dqs = tuple(jnp.zeros((BQ, BQ), F32) for _ in CHAINS)
            dqs = lax.fori_loop(0, i, lambda j, d: inner(j, d, False), dqs)
            dqs = inner(i, dqs, True)
            for c, (p, b) in enumerate(CHAINS):
                dq_v[_rows(b, i), p * PAIRW:p * PAIRW + BQ] = (dqs[c] * SCALE).astype(BF16)
            return 0

        lax.fori_loop(0, NB, qblock, 0)
        for p in range(2):
            dq_v[:, p * PAIRW + BQ:p * PAIRW + 2 * BQ] = dk_s[p].astype(BF16)
            dq_v[:, p * PAIRW + 2 * BQ:p * PAIRW + 3 * BQ] = dv_s[p].astype(BF16)
        _copy_in(dq_v, dqkv_hbm.at[:, pl.ds(g * 2 * PAIRW, 2 * PAIRW)], sem)
        if comm:
            comm[1](st)

    bias_specs = [VMEM_SPEC, VMEM_SPEC] if fox else [VMEM_SPEC]
    db_shape = jax.ShapeDtypeStruct((NSTAT, S), F32) if fox else jax.ShapeDtypeStruct((2, NB, HB, BQ), F32)
    n_in = 5 + len(bias_specs)
    (dqkv, db), extra = _host_call(
        body, "fox_bwd" if fox else "dil_bwd", [qkv, mixed, dmixed, lse, *bias, dqkv],
        [ANY_SPEC, ANY_SPEC, ANY_SPEC, VMEM_SPEC] + bias_specs + [ANY_SPEC],
        [jax.ShapeDtypeStruct((T, QKVW), BF16), db_shape], [ANY_SPEC, VMEM_SPEC],
        [SLAB_QKV, SLAB_OUT, SLAB_OUT, SLAB_QKV, ACC_KV, ACC_KV, pltpu.SemaphoreType.DMA], {n_in - 1: 0}, job)
    return dqkv, db, extra


def _stacked_delta(d):
    return d * BQ + (_iota((HB, BQ), 0) & (BQ - 1)) - _iota((HB, BQ), 1)


def _buckets_in(d):
    lo, hi = max(d * BQ - (BQ - 1), 0), d * BQ + BQ - 1
    return [b for b in range(32) if BUCKET_TH[b] <= hi and (b == 31 or BUCKET_TH[b + 1] > lo)]


def _in_bucket(delta, b):
    m = delta >= BUCKET_TH[b]
    return m if b == 31 else m & (delta < BUCKET_TH[b + 1])


def _dil_table(rel_bias):
    def body(rb_ref, o_ref):
        for d in range(NB):
            delta = _stacked_delta(d)
            pos = delta >= 0
            n = ((pos & (delta <= 128)).astype(jnp.int32)
                 + (pos & (delta <= 512) & ((delta & 3) == 0)).astype(jnp.int32)
                 + (pos & ((delta & 15) == 0)).astype(jnp.int32))
            logn = jnp.where(n == 3, math.log(3.0), jnp.where(n == 2, math.log(2.0), jnp.where(n == 1, 0.0, NEG)))
            head1 = _iota((HB, BQ), 0) >= BQ
            for p in range(2):
                val = jnp.zeros((HB, BQ), F32)
                for b in _buckets_in(d):
                    val = jnp.where(_in_bucket(delta, b), jnp.where(head1, rb_ref[b, 2 * p + 1], rb_ref[b, 2 * p]), val)
                o_ref[p, d] = val + logn

    return pl.pallas_call(
        body, name="dil_table", in_specs=[pl.BlockSpec(memory_space=pltpu.SMEM)], out_specs=VMEM_SPEC,
        out_shape=jax.ShapeDtypeStruct((2, NB, HB, BQ), F32), compiler_params=_cp())(rel_bias)


def _dil_table_bwd(dtbl):
    def body(dt_ref, o_ref):
        p = pl.program_id(0)
        rowi = _iota((32, BQ), 0)
        lanei = _iota((32, BQ), 1)

        @pl.when(p == 0)
        def _():
            o_ref[...] = jnp.zeros_like(o_ref)

        out = jnp.zeros((32, BQ), F32)
        for b in range(32):
            acc = None
            for d in range(NB):
                if b in _buckets_in(d):
                    t = jnp.where(_in_bucket(_stacked_delta(d), b), dt_ref[d], 0.0)
                    acc = t if acc is None else acc + t
            rs = jnp.sum(acc, axis=1, keepdims=True)
            s0 = jnp.sum(rs[:BQ], axis=0, keepdims=True)
            s1 = jnp.sum(rs[BQ:], axis=0, keepdims=True)
            out = (out + jnp.where((rowi == b) & (lanei == 2 * p), s0, 0.0)
                   + jnp.where((rowi == b) & (lanei == 2 * p + 1), s1, 0.0))
        o_ref[...] += out

    return pl.pallas_call(
        body, name="dil_table_bwd", grid=(2,),
        in_specs=[pl.BlockSpec((None, NB, HB, BQ), lambda p: (p, 0, 0, 0))],
        out_specs=pl.BlockSpec((32, BQ), lambda p: (0, 0)),
        out_shape=jax.ShapeDtypeStruct((32, BQ), F32),
        compiler_params=_cp(("arbitrary",)))(dtbl)


def _fox_prep(gate, fb):
    def body(g_ref, fb_ref, c_ref):
        tri = (_iota((BQ, BQ), 0) >= _iota((BQ, BQ), 1)).astype(BF16)

        def blk(i, carry):
            r0 = pl.multiple_of(i * BQ, BQ)
            lf = _log_sigmoid(g_ref[pl.ds(r0, BQ), :] + fb_ref[...])
            c = _dot(tri, _split3(lf))
            c_ref[pl.ds(r0, BQ), :] = c[:, 0:BQ] + c[:, BQ:2 * BQ] + c[:, 2 * BQ:3 * BQ] + carry
            return carry + jnp.sum(lf, axis=0, keepdims=True)

        lax.fori_loop(0, NB, blk, jnp.zeros((1, BQ), F32))

    blk = pl.BlockSpec((S, GATEW), lambda b: (b, 0))
    return pl.pallas_call(
        body, name="fox_prep", grid=(BL,), in_specs=[blk, pl.BlockSpec((1, GATEW), lambda b: (0, 0))],
        out_specs=blk, out_shape=jax.ShapeDtypeStruct((T, GATEW), F32),
        compiler_params=_cp(("parallel",)))(gate, fb)


def _fox_post(dcum, gate, fb):
    def body(dc_ref, g_ref, fb_ref, dg_ref, dfb_ref):
        b = pl.program_id(0)
        tri = (_iota((BQ, BQ), 0) <= _iota((BQ, BQ), 1)).astype(BF16)

        def blk(ii, carry):
            csum, dfb = carry
            r0 = pl.multiple_of((NB - 1 - ii) * BQ, BQ)
            dc = dc_ref[pl.ds(r0, BQ), :]
            c = _dot(tri, _split3(dc))
            dlf = c[:, 0:BQ] + c[:, BQ:2 * BQ] + c[:, 2 * BQ:3 * BQ] + csum
            dg = dlf * jnp.exp(_log_sigmoid(-(g_ref[pl.ds(r0, BQ), :] + fb_ref[...])))
            dg_ref[pl.ds(r0, BQ), :] = dg
            return csum + jnp.sum(dc, axis=0, keepdims=True), dfb + jnp.sum(dg, axis=0, keepdims=True)

        z = jnp.zeros((1, BQ), F32)
        _, dfb = lax.fori_loop(0, NB, blk, (z, z))

        @pl.when(b == 0)
        def _():
            dfb_ref[...] = dfb

        @pl.when(b > 0)
        def _():
            dfb_ref[...] += dfb

    blk = pl.BlockSpec((S, GATEW), lambda b: (b, 0))
    vec = pl.BlockSpec((1, GATEW), lambda b: (0, 0))
    return pl.pallas_call(
        body, name="fox_post", grid=(BL,), in_specs=[blk, blk, vec], out_specs=[blk, vec],
        out_shape=[jax.ShapeDtypeStruct((T, GATEW), F32), jax.ShapeDtypeStruct((1, GATEW), F32)],
        compiler_params=_cp(("arbitrary",)))(dcum, gate, fb)


def _shift_down(x, n):
    return jnp.where(_iota(x.shape, 0) >= n, pltpu.roll(x, n, 0), 0.0)


def _shift_up(x, n):
    return jnp.where(_iota(x.shape, 0) < S - n, pltpu.roll(x, S - n, 0), 0.0)


def _conv_fwd(conv, cw, mixed):
    W = 256

    def body(c_ref, w_ref, _, o_ref):
        u = c_ref[:, W:2 * W] * c_ref[:, 2 * W:3 * W]
        y = w_ref[0:1, :] * _shift_down(u, 2) + w_ref[1:2, :] * _shift_down(u, 1) + w_ref[2:3, :] * u
        o_ref[...] = (c_ref[:, 0:W] * y).astype(BF16)

    return pl.pallas_call(
        body, name="conv_fwd", grid=(BL,),
        in_specs=[pl.BlockSpec((S, CONVW), lambda b: (b, 0)), pl.BlockSpec((8, W), lambda b: (0, 0)), ANY_SPEC],
        out_specs=pl.BlockSpec((S, W), lambda b: (b, 3)),
        out_shape=jax.ShapeDtypeStruct((T, D), BF16), input_output_aliases={2: 0},
        compiler_params=_cp(("parallel",)))(conv, cw, mixed)


def _conv_bwd(conv, cw, dmixed):
    W = 256

    def body(c_ref, w_ref, do_ref, dc_ref, dw_ref):
        b = pl.program_id(0)
        bg = c_ref[:, 0:W]
        cg = c_ref[:, W:2 * W]
        hv = c_ref[:, 2 * W:3 * W]
        do = do_ref[...].astype(F32)
        u = cg * hv
        u1 = _shift_down(u, 1)
        u2 = _shift_down(u, 2)
        y = w_ref[0:1, :] * u2 + w_ref[1:2, :] * u1 + w_ref[2:3, :] * u
        dy = do * bg
        du = w_ref[2:3, :] * dy + w_ref[1:2, :] * _shift_up(dy, 1) + w_ref[0:1, :] * _shift_up(dy, 2)
        dc_ref[:, 0:W] = (do * y).astype(BF16)
        dc_ref[:, W:2 * W] = (du * hv).astype(BF16)
        dc_ref[:, 2 * W:3 * W] = (du * cg).astype(BF16)
        rowi = _iota((8, W), 0)
        dw = (jnp.where(rowi == 0, jnp.sum(dy * u2, axis=0, keepdims=True), 0.0)
              + jnp.where(rowi == 1, jnp.sum(dy * u1, axis=0, keepdims=True), 0.0)
              + jnp.where(rowi == 2, jnp.sum(dy * u, axis=0, keepdims=True), 0.0))

        @pl.when(b == 0)
        def _():
            dw_ref[...] = dw

        @pl.when(b > 0)
        def _():
            dw_ref[...] += dw

    return pl.pallas_call(
        body, name="conv_bwd", grid=(BL,),
        in_specs=[pl.BlockSpec((S, CONVW), lambda b: (b, 0)), pl.BlockSpec((8, W), lambda b: (0, 0)),
                  pl.BlockSpec((S, W), lambda b: (b, 3))],
        out_specs=[pl.BlockSpec((S, CONVW), lambda b: (b, 0)), pl.BlockSpec((8, W), lambda b: (0, 0))],
        out_shape=[jax.ShapeDtypeStruct((T, CONVW), BF16), jax.ShapeDtypeStruct((8, W), F32)],
        compiler_params=_cp(("arbitrary",)))(conv, cw, dmixed)


def _place():
    x, y, c = lax.axis_index("x"), lax.axis_index("y"), lax.axis_index("c")
    return x, y, c


def _allgather_weights(shards):
    n = len(shards)

    def body(*refs):
        ins, outs = refs[:n], refs[n:2 * n]
        send_sems, recv_sems, local_sems = refs[2 * n:]
        x, y, c = _place()
        me, sibling = (x, y, c), (x, y, 1 - c)
        chips = [(1 - x, y), (x, 1 - y), (1 - x, 1 - y)]

        def slot(a, p):
            return outs[a].at[:, 4 * p[0] + 2 * p[1] + p[2]]

        def copy(a, k, block, to, own=False):
            return pltpu.make_async_remote_copy(
                src_ref=ins[a] if own else slot(a, block), dst_ref=slot(a, block),
                send_sem=send_sems.at[a, k], recv_sem=recv_sems.at[a, k], device_id=to, device_id_type=MESH)

        mine = [pltpu.make_async_copy(ins[a], slot(a, me), local_sems.at[a]) for a in range(n)]
        for cp in mine:
            cp.start()
        first = []
        for a in range(n):
            first.append(copy(a, 0, me, sibling, own=True))
            first += [copy(a, 1 + j, me, (*chip, c), own=True) for j, chip in enumerate(chips)]
        for cp in first:
            cp.start()
        passed = []
        for j, chip in enumerate(chips):
            for a in range(n):
                copy(a, 1 + j, (*chip, c), me).wait_recv()
                cp = copy(a, 4 + j, (*chip, c), sibling)
                cp.start()
                passed.append(cp)
        for a in range(n):
            copy(a, 0, sibling, me).wait_recv()
            for j, chip in enumerate(chips):
                copy(a, 4 + j, (*chip, 1 - c), me).wait_recv()
        for cp in first + passed:
            cp.wait_send()
        for cp in mine:
            cp.wait()

    return pl.pallas_call(
        body, name="allgather_weights", in_specs=[ANY_SPEC] * n, out_specs=[ANY_SPEC] * n,
        out_shape=[jax.ShapeDtypeStruct((s.shape[0], NDEV) + s.shape[1:], s.dtype) for s in shards],
        scratch_shapes=[pltpu.SemaphoreType.DMA((n, 7)), pltpu.SemaphoreType.DMA((n, 7)),
                        pltpu.SemaphoreType.DMA((n,))],
        )(*shards)


def _allreduce_small(v):
    def body(v_ref, o_ref, slots, send_sems, recv_sems):
        x, y, c = _place()
        me = 4 * x + 2 * y + c
        slots[me] = v_ref[...]

        def copy(k):
            peer = (x ^ ((k >> 2) & 1), y ^ ((k >> 1) & 1), c ^ (k & 1))
            return pltpu.make_async_remote_copy(
                src_ref=v_ref, dst_ref=slots.at[me], send_sem=send_sems.at[k - 1], recv_sem=recv_sems.at[k - 1],
                device_id=peer, device_id_type=MESH)

        def arrival(k):
            return pltpu.make_async_remote_copy(
                src_ref=v_ref, dst_ref=slots.at[me ^ k], send_sem=send_sems.at[k - 1], recv_sem=recv_sems.at[k - 1],
                device_id=(x, y, c), device_id_type=MESH)

        sends = [copy(k) for k in range(1, NDEV)]
        for cp in sends:
            cp.start()
        for k in range(1, NDEV):
            arrival(k).wait_recv()
        for cp in sends:
            cp.wait_send()
        acc = slots[0]
        for d in range(1, NDEV):
            acc = acc + slots[d]
        o_ref[...] = acc

    return pl.pallas_call(
        body, name="allreduce_small", in_specs=[VMEM_SPEC], out_specs=VMEM_SPEC,
        out_shape=jax.ShapeDtypeStruct(v.shape, F32),
        scratch_shapes=[pltpu.VMEM((NDEV,) + v.shape, F32), pltpu.SemaphoreType.DMA((NDEV - 1,)),
                        pltpu.SemaphoreType.DMA((NDEV - 1,))],
        )(v)


def _sibling_exchange(grads):
    n = len(grads)

    def body(*refs):
        ins, outs = refs[:n], refs[n:2 * n]
        send_sems, recv_sems = refs[2 * n:]
        x, y, c = _place()
        cps = [pltpu.make_async_remote_copy(
            src_ref=ins[a].at[:, :, 1 - c], dst_ref=outs[a], send_sem=send_sems.at[a], recv_sem=recv_sems.at[a],
            device_id=(x, y, 1 - c), device_id_type=MESH) for a in range(n)]
        for cp in cps:
            cp.start()
        for cp in cps:
            cp.wait()

    return pl.pallas_call(
        body, name="sibling_exchange", in_specs=[ANY_SPEC] * n, out_specs=[ANY_SPEC] * n,
        out_shape=[jax.ShapeDtypeStruct(g.shape[:2] + g.shape[3:], F32) for g in grads],
        scratch_shapes=[pltpu.SemaphoreType.DMA((n,)), pltpu.SemaphoreType.DMA((n,))],
        )(*grads)


def _pair_sum(grad, got, core):
    _, _, _, rows, N = grad.shape

    def body(c_ref, g_ref, r_ref, o_ref):
        o_ref[...] = (g_ref[...] + r_ref[...]).astype(BF16)

    return pl.pallas_call(
        body, name="pair_sum",
        grid_spec=pltpu.PrefetchScalarGridSpec(
            num_scalar_prefetch=1, grid=(2, 4),
            in_specs=[pl.BlockSpec((None, None, None, rows, N), lambda l, k, c: (l, k, c[0], 0, 0)),
                      pl.BlockSpec((None, None, rows, N), lambda l, k, c: (l, k, 0, 0))],
            out_specs=pl.BlockSpec((None, None, rows, N), lambda l, k, c: (l, k, 0, 0))),
        out_shape=jax.ShapeDtypeStruct((2, 4, rows, N), BF16),
        compiler_params=_cp(("parallel", "parallel")))(core, grad, got)


def _chip_exchange(psums):
    n = len(psums)

    def body(*refs):
        ins, outs = refs[:n], refs[n:2 * n]
        send_sems, recv_sems, local_sems = refs[2 * n:]
        x, y, c = _place()
        mychip = 2 * x + y
        chips = [(1 - x, y), (x, 1 - y), (1 - x, 1 - y)]
        local = [pltpu.make_async_copy(ins[a].at[:, mychip], outs[a].at[:, mychip], local_sems.at[a]) for a in range(n)]
        for cp in local:
            cp.start()
        sends = []
        for a in range(n):
            for j, chip in enumerate(chips):
                sends.append(pltpu.make_async_remote_copy(
                    src_ref=ins[a].at[:, 2 * chip[0] + chip[1]], dst_ref=outs[a].at[:, mychip],
                    send_sem=send_sems.at[a, j], recv_sem=recv_sems.at[a, j],
                    device_id=(*chip, c), device_id_type=MESH))
        for cp in sends:
            cp.start()
        for a in range(n):
            for j, chip in enumerate(chips):
                pltpu.make_async_remote_copy(
                    src_ref=ins[a].at[:, mychip], dst_ref=outs[a].at[:, 2 * chip[0] + chip[1]],
                    send_sem=send_sems.at[a, j], recv_sem=recv_sems.at[a, j],
                    device_id=(x, y, c), device_id_type=MESH).wait_recv()
        for cp in sends:
            cp.wait_send()
        for cp in local:
            cp.wait()

    return pl.pallas_call(
        body, name="chip_exchange", in_specs=[ANY_SPEC] * n, out_specs=[ANY_SPEC] * n,
        out_shape=[jax.ShapeDtypeStruct(p.shape, BF16) for p in psums],
        scratch_shapes=[pltpu.SemaphoreType.DMA((n, 3)), pltpu.SemaphoreType.DMA((n, 3)),
                        pltpu.SemaphoreType.DMA((n,))],
        )(*psums)


def _chip_sum(parts):
    _, _, rows, N = parts.shape

    def body(p_ref, o_ref):
        acc = p_ref[0].astype(F32)
        for k in range(1, 4):
            acc = acc + p_ref[k].astype(F32)
        o_ref[...] = acc

    return pl.pallas_call(
        body, name="chip_sum", grid=(2,),
        in_specs=[pl.BlockSpec((None, 4, rows, N), lambda l: (l, 0, 0, 0))],
        out_specs=pl.BlockSpec((None, rows, N), lambda l: (l, 0, 0)),
        out_shape=jax.ShapeDtypeStruct((2, rows, N), F32), compiler_params=_cp(("parallel",)))(parts)


def _chips_of(x, y):
    return [(1 - x, y), (x, 1 - y), (1 - x, 1 - y)]


def _dev(p):
    return 4 * p[0] + 2 * p[1] + p[2]


def _gather_job_a(shards):
    n = len(shards)

    def peers(x, y, c):
        return [(x, y, 1 - c)] + [(*chip, c) for chip in _chips_of(x, y)]

    def start(ins, outs, sems):
        send, recv, loc = sems
        x, y, c = _place()
        me = (x, y, c)
        cps = []
        for a in range(n):
            cps.append(pltpu.make_async_copy(ins[a], outs[a].at[_dev(me)], loc.at[a]))
            for k, peer in enumerate(peers(x, y, c)):
                cps.append(pltpu.make_async_remote_copy(
                    src_ref=ins[a], dst_ref=outs[a].at[_dev(me)], send_sem=send.at[a, k], recv_sem=recv.at[a, k],
                    device_id=peer, device_id_type=MESH))
        for cp in cps:
            cp.start()
        return cps

    def finish(cps, ins, outs, sems):
        send, recv, loc = sems
        x, y, c = _place()
        for a in range(n):
            for k, peer in enumerate(peers(x, y, c)):
                pltpu.make_async_remote_copy(
                    src_ref=ins[a], dst_ref=outs[a].at[_dev(peer)], send_sem=send.at[a, k], recv_sem=recv.at[a, k],
                    device_id=(x, y, c), device_id_type=MESH).wait_recv()
        for a in range(n):
            cps[5 * a].wait()
            for k in range(4):
                cps[5 * a + 1 + k].wait_send()

    return _Job(shards, [jax.ShapeDtypeStruct((NDEV,) + s.shape, s.dtype) for s in shards], {},
                [pltpu.SemaphoreType.DMA((n, 4)), pltpu.SemaphoreType.DMA((n, 4)), pltpu.SemaphoreType.DMA((n,))],
                start, finish)


def _gather_job_b(gathered):
    n = len(gathered)

    def start(ins, outs, sems):
        send, recv = sems
        x, y, c = _place()
        cps = []
        for a in range(n):
            for j, chip in enumerate(_chips_of(x, y)):
                blk = outs[a].at[_dev((*chip, c))]
                cps.append(pltpu.make_async_remote_copy(
                    src_ref=blk, dst_ref=blk, send_sem=send.at[a, j], recv_sem=recv.at[a, j],
                    device_id=(x, y, 1 - c), device_id_type=MESH))
        for cp in cps:
            cp.start()
        return cps

    def finish(cps, ins, outs, sems):
        send, recv = sems
        x, y, c = _place()
        for a in range(n):
            for j, chip in enumerate(_chips_of(x, y)):
                blk = outs[a].at[_dev((*chip, 1 - c))]
                pltpu.make_async_remote_copy(
                    src_ref=blk, dst_ref=blk, send_sem=send.at[a, j], recv_sem=recv.at[a, j],
                    device_id=(x, y, c), device_id_type=MESH).wait_recv()
        for cp in cps:
            cp.wait_send()

    return _Job(gathered, [jax.ShapeDtypeStruct(g.shape, g.dtype) for g in gathered], {a: a for a in range(n)},
                [pltpu.SemaphoreType.DMA((n, 3)), pltpu.SemaphoreType.DMA((n, 3))], start, finish)


def _sibling_job(grads):
    n = len(grads)

    def start(ins, outs, sems):
        send, recv = sems
        x, y, c = _place()
        cps = [pltpu.make_async_remote_copy(
            src_ref=ins[a].at[:, 1 - c], dst_ref=outs[a], send_sem=send.at[a], recv_sem=recv.at[a],
            device_id=(x, y, 1 - c), device_id_type=MESH) for a in range(n)]
        for cp in cps:
            cp.start()
        return cps

    def finish(cps, ins, outs, sems):
        for cp in cps:
            cp.wait()

    return _Job(grads, [jax.ShapeDtypeStruct(g.shape[:1] + g.shape[2:], F32) for g in grads], {},
                [pltpu.SemaphoreType.DMA((n,)), pltpu.SemaphoreType.DMA((n,))], start, finish)


def _chip_job(psums):
    n = len(psums)

    def start(ins, outs, sems):
        send, recv, loc = sems
        x, y, c = _place()
        mychip = 2 * x + y
        cps = []
        for a in range(n):
            cps.append(pltpu.make_async_copy(ins[a].at[mychip], outs[a].at[mychip], loc.at[a]))
            for j, chip in enumerate(_chips_of(x, y)):
                cps.append(pltpu.make_async_remote_copy(
                    src_ref=ins[a].at[2 * chip[0] + chip[1]], dst_ref=outs[a].at[mychip],
                    send_sem=send.at[a, j], recv_sem=recv.at[a, j], device_id=(*chip, c), device_id_type=MESH))
        for cp in cps:
            cp.start()
        return cps

    def finish(cps, ins, outs, sems):
        send, recv, loc = sems
        x, y, c = _place()
        mychip = 2 * x + y
        for a in range(n):
            for j, chip in enumerate(_chips_of(x, y)):
                pltpu.make_async_remote_copy(
                    src_ref=ins[a].at[mychip], dst_ref=outs[a].at[2 * chip[0] + chip[1]],
                    send_sem=send.at[a, j], recv_sem=recv.at[a, j], device_id=(x, y, c), device_id_type=MESH).wait_recv()
        for a in range(n):
            cps[4 * a].wait()
            for j in range(3):
                cps[4 * a + 1 + j].wait_send()

    return _Job(psums, [jax.ShapeDtypeStruct(p.shape, BF16) for p in psums], {},
                [pltpu.SemaphoreType.DMA((n, 3)), pltpu.SemaphoreType.DMA((n, 3)), pltpu.SemaphoreType.DMA((n,))],
                start, finish)


def _run_job(job, name):
    def body(ins, outs, scr, comm):
        comm[1](comm[0]())

    return _host_call(body, name, [], [], [], [], [], {}, job)[1]


def _pair_sum1(grad, got, core):
    _, _, rows, N = grad.shape

    def body(c_ref, g_ref, r_ref, o_ref):
        o_ref[...] = (g_ref[...] + r_ref[...]).astype(BF16)

    return pl.pallas_call(
        body, name="pair_sum",
        grid_spec=pltpu.PrefetchScalarGridSpec(
            num_scalar_prefetch=1, grid=(4,),
            in_specs=[pl.BlockSpec((None, None, rows, N), lambda k, c: (k, c[0], 0, 0)),
                      pl.BlockSpec((None, rows, N), lambda k, c: (k, 0, 0))],
            out_specs=pl.BlockSpec((None, rows, N), lambda k, c: (k, 0, 0))),
        out_shape=jax.ShapeDtypeStruct((4, rows, N), BF16),
        compiler_params=_cp(("parallel",)))(core, grad, got)


def _chip_sum1(parts):
    _, rows, N = parts.shape

    def body(p_ref, o_ref):
        acc = p_ref[0].astype(F32)
        for k in range(1, 4):
            acc = acc + p_ref[k].astype(F32)
        o_ref[...] = acc

    return pl.pallas_call(
        body, name="chip_sum", in_specs=[VMEM_SPEC], out_specs=VMEM_SPEC,
        out_shape=jax.ShapeDtypeStruct((rows, N), F32), compiler_params=_cp())(parts)


def _permute_in(w):
    lead = w.shape[:-1]
    return w.reshape(lead + (3, 3, 2, BQ)).swapaxes(-2, -3).reshape(lead + (QKVW,))


def _unpermute_in(w):
    lead = w.shape[:-1]
    return w.reshape(lead + (3, 2, 3, BQ)).swapaxes(-2, -3).reshape(lead + (QKVW,))


def _row(v):
    v = v.reshape(-1)
    return jnp.pad(v, (0, D - v.shape[0])).reshape(1, D)


def _step_without_overlap(x, w_in, f_bias, conv_w, w_out, rel_bias, ln1_g, ln1_b, w_gate, w_up, w_down, ln2_g, ln2_b, loss_target, m_w_in, m_f_bias, m_conv_w, m_w_out, m_rel_bias, m_ln1_g, m_ln1_b, m_w_gate, m_w_up, m_w_down, m_ln2_g, m_ln2_b, v_w_in, v_f_bias, v_conv_w, v_w_out, v_rel_bias, v_ln1_g, v_ln1_b, v_w_gate, v_w_up, v_w_down, v_ln2_g, v_ln2_b):
    xi, yi, ci = _place()
    me = 4 * xi + 2 * yi + ci

    win_s = jnp.concatenate([_permute_in(w_in[..., :QKVW]), w_in[..., QKVW:]], axis=-1)
    win_s = jnp.pad(win_s, ((0, 0), (0, 0), (0, NPAD - NPROJ))).astype(BF16)
    shards = [win_s, w_out.astype(BF16), jnp.swapaxes(w_gate, 1, 2).astype(BF16),
              jnp.swapaxes(w_up, 1, 2).astype(BF16), w_down.astype(BF16)]
    full = _allgather_weights(shards)
    Win, Wout, WgT, WuT, Wd = [f.reshape(2, NDEV * f.shape[2], f.shape[3]) for f in full]

    cw_rows = lax.dynamic_update_slice(jnp.zeros((2, 3, 256), F32), conv_w, (0, 0, me * 32))
    small = jnp.concatenate([_row(cw_rows[0]), _row(cw_rows[1]), jnp.zeros((SMALL_ROWS - 2, D), F32)], axis=0)
    small = _allreduce_small(small)
    cw_full = small[0:2, :CONVW].reshape(2, 3, 256)
    cw8 = jnp.pad(cw_full, ((0, 0), (0, 5), (0, 0)))
    fb = jnp.pad(f_bias, ((0, 0), (0, GATEW - NH))).reshape(2, 1, GATEW)
    tbl = _dil_table(rel_bias)

    def wcol(layer, K, tn, off):
        return pl.BlockSpec((None, K, tn), lambda i, j: (layer, 0, off + j))

    def arow(tm, K, blk=0):
        return pl.BlockSpec((tm, K), lambda i, j: (i, blk))

    h = x.reshape(T, D)
    hb = h.astype(BF16)
    saved = []
    for l in range(2):
        qkv = _mm([(hb, arow(512, D), Win, wcol(l, D, 768, 0))], nt=False, M=T, N=QKVW, tm=512, tn=768,
                  out_dtype=BF16, name="proj_qkv")
        conv = _mm([(hb, arow(512, D), Win, wcol(l, D, 768, 3))], nt=False, M=T, N=CONVW, tm=512, tn=768,
                   out_dtype=F32, name="proj_conv")
        gate = _mm([(hb, arow(512, D), Win, wcol(l, D, 128, 24))], nt=False, M=T, N=GATEW, tm=512, tn=128,
                   out_dtype=F32, name="proj_gate")
        cum = _fox_prep(gate, fb[l])
        cq = cum[:, :NH].reshape(BL, S, NH).transpose(1, 0, 2).reshape(S, NSTAT)
        ck = cq.T
        mixed, rtot = _sb_fwd(qkv)
        mixed, lse_d = _flash_fwd(qkv, mixed, 1, False, (tbl,))
        mixed, lse_f = _flash_fwd(qkv, mixed, 2, True, (cq, ck))
        mixed = _conv_fwd(conv, cw8[l], mixed)
        mix = _mm([(mixed, arow(512, D), Wout, wcol(l, D, 512, 0))], nt=False, M=T, N=D, tm=512, tn=512,
                  out_dtype=F32, name="out_proj")
        x1, xh1, r1, x1b = _ln_fwd(h, mix, ln1_g[l:l + 1], ln1_b[l:l + 1])
        g, u, a = _ffn_up(x1b, WgT, WuT, l)
        ffn = _mm([(a, arow(512, DFF), Wd, wcol(l, DFF, 512, 0))], nt=False, M=T, N=D, tm=512, tn=512,
                  out_dtype=F32, name="ffn_down")
        x2, xh2, r2, x2b = _ln_fwd(x1, ffn, ln2_g[l:l + 1], ln2_b[l:l + 1])
        saved.append(dict(h=hb, qkv=qkv, conv=conv, gate=gate, cq=cq, ck=ck, mixed=mixed, rtot=rtot, lse_d=lse_d,
                          lse_f=lse_f, x1=x1b, xh1=xh1, r1=r1, g=g, u=u, a=a, xh2=xh2, r2=r2))
        h, hb = x2, x2b

    sq, dy = _loss_grad(h, loss_target.reshape(T, D))
    loss = lax.psum(sq[0, 0], ("x", "y", "c")) * (0.5 / D)

    G_in = jnp.zeros((2, D, NPAD), F32)
    G_out = jnp.zeros((2, D, D), F32)
    G_g = jnp.zeros((2, DFF, D), F32)
    G_u = jnp.zeros((2, DFF, D), F32)
    G_d = jnp.zeros((2, DFF, D), F32)
    small_g = {}

    def wrow(layer, tn, K, blk=0):
        return pl.BlockSpec((None, tn, K), lambda i, j: (layer, j, blk))

    for l in (1, 0):
        sv = saved[l]
        ds2, dg2, db2, ds2b = _ln_bwd(dy, sv["xh2"], sv["r2"], ln2_g[l:l + 1])
        dgt, dut = _ffn_da(ds2b, Wd, sv["g"], sv["u"], l)
        G_d = _mm_tn(sv["a"], ds2b, G_d, Ka=DFF, N=D, tm=1408, tn=1024, tk=512, layer=l, ooff=0, name="grad_w_down")
        G_g = _mm_tn(dgt, sv["x1"], G_g, Ka=DFF, N=D, tm=1408, tn=1024, tk=512, layer=l, ooff=0, name="grad_w_gate")
        G_u = _mm_tn(dut, sv["x1"], G_u, Ka=DFF, N=D, tm=1408, tn=1024, tk=512, layer=l, ooff=0, name="grad_w_up")
        dx1 = _mm([(dgt, arow(512, DFF), WgT, wcol(l, DFF, 512, 0)), (dut, arow(512, DFF), WuT, wcol(l, DFF, 512, 0))],
                  nt=False, M=T, N=D, tm=512, tn=512, out_dtype=F32, name="ffn_dx", res=ds2, res_scale=ALPHA)
        ds1, dg1, db1, ds1b = _ln_bwd(dx1, sv["xh1"], sv["r1"], ln1_g[l:l + 1])
        G_out = _mm_tn(sv["mixed"], ds1b, G_out, Ka=D, N=D, tm=1024, tn=1024, tk=512, layer=l, ooff=0,
                       name="grad_w_out")
        dmixed = _mm([(ds1b, arow(512, D), Wout, wrow(l, 512, D))], nt=True, M=T, N=D, tm=512, tn=512,
                     out_dtype=BF16, name="out_proj_dx")
        dqkv = _sb_bwd(sv["qkv"], dmixed, sv["rtot"])
        dqkv, dtbl = _flash_bwd(sv["qkv"], sv["mixed"], dmixed, sv["lse_d"], dqkv, 1, False, (tbl,))
        dqkv, dck = _flash_bwd(sv["qkv"], sv["mixed"], dmixed, sv["lse_f"], dqkv, 2, True, (sv["cq"], sv["ck"]))
        dconv, dcw = _conv_bwd(sv["conv"], cw8[l], dmixed)
        dcum = jnp.pad(dck.reshape(BL, NH, S).transpose(0, 2, 1).reshape(T, NH), ((0, 0), (0, GATEW - NH)))
        dgate, dfb = _fox_post(dcum, sv["gate"], fb[l])
        drb = _dil_table_bwd(dtbl)
        G_in = _mm_tn(sv["h"], dqkv, G_in, Ka=D, N=QKVW, tm=1024, tn=768, tk=512, layer=l, ooff=0, name="grad_w_in_qkv")
        G_in = _mm_tn(sv["h"], dconv, G_in, Ka=D, N=CONVW, tm=1024, tn=768, tk=512, layer=l, ooff=3,
                      name="grad_w_in_conv")
        G_in = _mm_tn(sv["h"], dgate, G_in, Ka=D, N=GATEW, tm=1024, tn=128, tk=512, layer=l, ooff=24,
                      name="grad_w_in_gate")
        dy = _mm([(dqkv, arow(512, QKVW), Win, wrow(l, 512, QKVW, 0)),
                  (dconv, arow(512, CONVW), Win, wrow(l, 512, CONVW, 3)),
                  (dgate, arow(512, GATEW), Win, wrow(l, 512, GATEW, 24))],
                 nt=True, M=T, N=D, tm=512, tn=512, out_dtype=F32, name="proj_dx", res=ds1, res_scale=ALPHA)
        small_g[l] = dict(ln1_g=dg1, ln1_b=db1, ln2_g=dg2, ln2_b=db2, cw=dcw[0:3].reshape(1, CONVW),
                          fb=dfb[:, :NH], rb=drb[:, :NH])
    grad_x = dy.reshape(BL, S, D)

    rows = []
    for name in ("ln1_g", "ln1_b", "ln2_g", "ln2_b"):
        rows += [small_g[0][name], small_g[1][name]]
    rows += [_row(small_g[0]["cw"]), _row(small_g[1]["cw"]),
             _row(jnp.concatenate([small_g[0]["fb"], small_g[1]["fb"]], axis=0)),
             _row(small_g[0]["rb"] + small_g[1]["rb"])]
    rows.append(jnp.zeros((SMALL_ROWS - len(rows), D), F32))
    sg = _allreduce_small(jnp.concatenate(rows, axis=0))
    g_ln1_g, g_ln1_b, g_ln2_g, g_ln2_b = sg[0:2], sg[2:4], sg[4:6], sg[6:8]
    g_conv_full = sg[8:10, :CONVW].reshape(2, 3, 256)
    g_conv = lax.dynamic_slice(g_conv_full, (0, 0, me * 32), (2, 3, 32))
    g_fb = sg[10, :2 * NH].reshape(2, NH)
    g_rb = sg[11, :32 * NH].reshape(32, NH)

    bufs = [G_in, G_out, G_g, G_u, G_d]
    views = [b.reshape(2, 4, 2, b.shape[1] // NDEV, b.shape[2]) for b in bufs]
    got = _sibling_exchange(views)
    core = jnp.reshape(ci, (1,)).astype(jnp.int32)
    psums = [_pair_sum(vw, gt, core) for vw, gt in zip(views, got)]
    parts = _chip_exchange(psums)
    gs = [_chip_sum(p) for p in parts]
    g_in = gs[0]
    g_w_in = jnp.concatenate([_unpermute_in(g_in[..., :QKVW]), g_in[..., QKVW:NPROJ]], axis=-1)
    g_w_out = gs[1]
    g_w_gate = jnp.swapaxes(gs[2], 1, 2)
    g_w_up = jnp.swapaxes(gs[3], 1, 2)
    g_w_down = gs[4]

    def big(w, g, m, v, tr):
        sh = w.shape
        f = lambda t: t.reshape(-1, sh[-1])
        return [t.reshape(sh) for t in _adamw(f(w), f(g), f(m), f(v), tr)]

    up_in = big(w_in, g_w_in, m_w_in, v_w_in, 64)
    up_out = big(w_out, g_w_out, m_w_out, v_w_out, 128)
    up_gate = big(w_gate, g_w_gate, m_w_gate, v_w_gate, 256)
    up_up = big(w_up, g_w_up, m_w_up, v_w_up, 256)
    up_down = big(w_down, g_w_down, m_w_down, v_w_down, 352)

    def pack(fbv, cwv, rbv, l1g, l1b, l2g, l2b):
        r = [l1g, l1b, l2g, l2b, _row(cwv), _row(fbv), _row(rbv)]
        r.append(jnp.zeros((SMALL_ROWS - 11, D), F32))
        return jnp.concatenate(r, axis=0)

    pw = pack(f_bias, conv_w, rel_bias, ln1_g, ln1_b, ln2_g, ln2_b)
    pg = pack(g_fb, g_conv, g_rb, g_ln1_g, g_ln1_b, g_ln2_g, g_ln2_b)
    pm = pack(m_f_bias, m_conv_w, m_rel_bias, m_ln1_g, m_ln1_b, m_ln2_g, m_ln2_b)
    pv = pack(v_f_bias, v_conv_w, v_rel_bias, v_ln1_g, v_ln1_b, v_ln2_g, v_ln2_b)
    ups = _adamw(pw, pg, pm, pv, SMALL_ROWS)

    def unpack(p):
        return dict(ln1_g=p[0:2], ln1_b=p[2:4], ln2_g=p[4:6], ln2_b=p[6:8],
                    conv_w=p[8, :192].reshape(2, 3, 32), f_bias=p[9, :2 * NH].reshape(2, NH),
                    rel_bias=p[10, :32 * NH].reshape(32, NH))

    sm = [unpack(p) for p in ups]

    def group(k):
        return (up_in[k], sm[k]["f_bias"], sm[k]["conv_w"], up_out[k], sm[k]["rel_bias"], sm[k]["ln1_g"],
                sm[k]["ln1_b"], up_gate[k], up_up[k], up_down[k], sm[k]["ln2_g"], sm[k]["ln2_b"])

    grads = (g_w_in, g_fb, g_conv, g_w_out, g_rb, g_ln1_g, g_ln1_b, g_w_gate, g_w_up, g_w_down, g_ln2_g, g_ln2_b)
    return (loss, grad_x) + grads + group(0) + group(1) + group(2)


def _pair_sums(views, gots, core):
    n = len(views)

    def body(c_ref, *refs):
        for a in range(n):
            refs[2 * n + a][...] = (refs[a][...] + refs[n + a][...]).astype(BF16)

    def vspec(v):
        return pl.BlockSpec((None, None) + v.shape[2:], lambda k, c: (k, c[0], 0, 0))

    def gspec(g):
        return pl.BlockSpec((None,) + g.shape[1:], lambda k, c: (k, 0, 0))

    return pl.pallas_call(
        body, name="pair_sums",
        grid_spec=pltpu.PrefetchScalarGridSpec(
            num_scalar_prefetch=1, grid=(4,),
            in_specs=[vspec(v) for v in views] + [gspec(g) for g in gots],
            out_specs=[gspec(g) for g in gots]),
        out_shape=[jax.ShapeDtypeStruct(g.shape, BF16) for g in gots],
        compiler_params=_cp(("parallel",)))(core, *views, *gots)


def _chip_sums(parts):
    n = len(parts)

    def body(*refs):
        for a in range(n):
            acc = refs[a][0].astype(F32)
            for k in range(1, 4):
                acc = acc + refs[a][k].astype(F32)
            refs[n + a][...] = acc

    return pl.pallas_call(
        body, name="chip_sums", in_specs=[VMEM_SPEC] * n, out_specs=[VMEM_SPEC] * n,
        out_shape=[jax.ShapeDtypeStruct(p.shape[1:], F32) for p in parts], compiler_params=_cp())(*parts)


def kernel(x, w_in, f_bias, conv_w, w_out, rel_bias, ln1_g, ln1_b, w_gate, w_up, w_down, ln2_g, ln2_b, loss_target, m_w_in, m_f_bias, m_conv_w, m_w_out, m_rel_bias, m_ln1_g, m_ln1_b, m_w_gate, m_w_up, m_w_down, m_ln2_g, m_ln2_b, v_w_in, v_f_bias, v_conv_w, v_w_out, v_rel_bias, v_ln1_g, v_ln1_b, v_w_gate, v_w_up, v_w_down, v_ln2_g, v_ln2_b):
    xi, yi, ci = _place()
    me = 4 * xi + 2 * yi + ci
    core = jnp.reshape(ci, (1,)).astype(jnp.int32)

    win_s = jnp.concatenate([_permute_in(w_in[..., :QKVW]), w_in[..., QKVW:]], axis=-1)
    win_s = jnp.pad(win_s, ((0, 0), (0, 0), (0, NPAD - NPROJ))).astype(BF16)
    per_layer = [win_s, w_out.astype(BF16), jnp.swapaxes(w_gate, 1, 2).astype(BF16),
                 jnp.swapaxes(w_up, 1, 2).astype(BF16), w_down.astype(BF16)]

    def shards(l):
        return [s[l] for s in per_layer]

    def whole(gathered):
        return [g.reshape(NDEV * g.shape[1], g.shape[2]) for g in gathered]

    weights = [whole(_run_job(_gather_job_b(_run_job(_gather_job_a(shards(0)), "gather_a")), "gather_b"))]

    cw_rows = lax.dynamic_update_slice(jnp.zeros((2, 3, 256), F32), conv_w, (0, 0, me * 32))
    small = jnp.concatenate([_row(cw_rows[0]), _row(cw_rows[1]), jnp.zeros((SMALL_ROWS - 2, D), F32)], axis=0)
    small = _allreduce_small(small)
    cw_full = small[0:2, :CONVW].reshape(2, 3, 256)
    cw8 = jnp.pad(cw_full, ((0, 0), (0, 5), (0, 0)))
    fb = jnp.pad(f_bias, ((0, 0), (0, GATEW - NH))).reshape(2, 1, GATEW)
    tbl = _dil_table(rel_bias)

    def wcol(K, tn, off):
        return pl.BlockSpec((K, tn), lambda i, j: (0, off + j))

    def wrow(tn, K, blk=0):
        return pl.BlockSpec((tn, K), lambda i, j: (j, blk))

    def arow(tm, K, blk=0):
        return pl.BlockSpec((tm, K), lambda i, j: (i, blk))

    h = x.reshape(T, D)
    hb = h.astype(BF16)
    saved = []
    for l in range(2):
        Win, Wout, WgT, WuT, Wd = weights[l]
        qkv = _mm([(hb, arow(512, D), Win, wcol(D, 768, 0))], nt=False, M=T, N=QKVW, tm=512, tn=768,
                  out_dtype=BF16, name="proj_qkv")
        conv = _mm([(hb, arow(512, D), Win, wcol(D, 768, 3))], nt=False, M=T, N=CONVW, tm=512, tn=768,
                   out_dtype=F32, name="proj_conv")
        gate = _mm([(hb, arow(512, D), Win, wcol(D, 128, 24))], nt=False, M=T, N=GATEW, tm=512, tn=128,
                   out_dtype=F32, name="proj_gate")
        cum = _fox_prep(gate, fb[l])
        cq = cum[:, :NH].reshape(BL, S, NH).transpose(1, 0, 2).reshape(S, NSTAT)
        ck = cq.T
        if l == 0:
            mixed, rtot, leg_a = _sb_fwd(qkv, job=_gather_job_a(shards(1)))
            mixed, lse_d, leg_b = _flash_fwd(qkv, mixed, 1, False, (tbl,), job=_gather_job_b(list(leg_a)))
            weights.append(whole(leg_b))
        else:
            mixed, rtot, _ = _sb_fwd(qkv)
            mixed, lse_d, _ = _flash_fwd(qkv, mixed, 1, False, (tbl,))
        mixed, lse_f, _ = _flash_fwd(qkv, mixed, 2, True, (cq, ck))
        mixed = _conv_fwd(conv, cw8[l], mixed)
        mix = _mm([(mixed, arow(512, D), Wout, wcol(D, 512, 0))], nt=False, M=T, N=D, tm=512, tn=512,
                  out_dtype=F32, name="out_proj")
        x1, xh1, r1, x1b = _ln_fwd(h, mix, ln1_g[l:l + 1], ln1_b[l:l + 1])
        g, u, a = _ffn_up(x1b, WgT[None], WuT[None], 0)
        ffn = _mm([(a, arow(512, DFF), Wd, wcol(DFF, 512, 0))], nt=False, M=T, N=D, tm=512, tn=512,
                  out_dtype=F32, name="ffn_down")
        x2, xh2, r2, x2b = _ln_fwd(x1, ffn, ln2_g[l:l + 1], ln2_b[l:l + 1])
        saved.append(dict(h=hb, qkv=qkv, conv=conv, gate=gate, cq=cq, ck=ck, mixed=mixed, rtot=rtot, lse_d=lse_d,
                          lse_f=lse_f, x1=x1b, xh1=xh1, r1=r1, g=g, u=u, a=a, xh2=xh2, r2=r2))
        h, hb = x2, x2b

    sq, dy = _loss_grad(h, loss_target.reshape(T, D))
    loss = lax.psum(sq[0, 0], ("x", "y", "c")) * (0.5 / D)

    def view(gr):
        return gr.reshape(4, 2, gr.shape[0] // NDEV, gr.shape[1])

    def reduce_tail(views, gots):
        return _chip_sums(_run_job(_chip_job(_pair_sums(views, gots, core)), "chip_exchange"))

    G = [None, None]
    small_g = {}
    shard_g = {}
    for l in (1, 0):
        sv = saved[l]
        Win, Wout, WgT, WuT, Wd = weights[l]
        ds2, dg2, db2, ds2b = _ln_bwd(dy, sv["xh2"], sv["r2"], ln2_g[l:l + 1])
        dgt, dut = _ffn_da(ds2b, Wd[None], sv["g"], sv["u"], 0)
        G_d = _mm_tn(sv["a"], ds2b, None, C=D, Ka=DFF, N=D, tm=1408, tn=1024, tk=512, ooff=0, name="grad_w_down")
        G_g = _mm_tn(dgt, sv["x1"], None, C=D, Ka=DFF, N=D, tm=1408, tn=1024, tk=512, ooff=0, name="grad_w_gate")
        G_u = _mm_tn(dut, sv["x1"], None, C=D, Ka=DFF, N=D, tm=1408, tn=1024, tk=512, ooff=0, name="grad_w_up")
        dx1 = _mm([(dgt, arow(512, DFF), WgT, wcol(DFF, 512, 0)), (dut, arow(512, DFF), WuT, wcol(DFF, 512, 0))],
                  nt=False, M=T, N=D, tm=512, tn=512, out_dtype=F32, name="ffn_dx", res=ds2, res_scale=ALPHA)
        ds1, dg1, db1, ds1b = _ln_bwd(dx1, sv["xh1"], sv["r1"], ln1_g[l:l + 1])
        G_out = _mm_tn(sv["mixed"], ds1b, None, C=D, Ka=D, N=D, tm=1024, tn=1024, tk=512, ooff=0, name="grad_w_out")
        dmixed = _mm([(ds1b, arow(512, D), Wout, wrow(512, D))], nt=True, M=T, N=D, tm=512, tn=512,
                     out_dtype=BF16, name="out_proj_dx")
        if l == 0:
            early = [view(t) for t in (G[1]["in"], G[1]["out"], G[1]["g"], G[1]["u"], G[1]["d"], G_g, G_u, G_d)]
            dqkv, gots = _sb_bwd(sv["qkv"], dmixed, sv["rtot"], job=_sibling_job(early))
            ps1 = _pair_sums(early[:5], list(gots[:5]), core)
            ps0 = _pair_sums(early[5:], list(gots[5:]), core)
            dqkv, dtbl, parts0 = _flash_bwd(sv["qkv"], sv["mixed"], dmixed, sv["lse_d"], dqkv, 1, False, (tbl,),
                                            job=_chip_job(ps0))
            dqkv, dck, parts1 = _flash_bwd(sv["qkv"], sv["mixed"], dmixed, sv["lse_f"], dqkv, 2, True,
                                           (sv["cq"], sv["ck"]), job=_chip_job(ps1))
            s1 = _chip_sums(list(parts1))
            s0 = _chip_sums(list(parts0))
            shard_g[1] = dict(zip(("in", "out", "g", "u", "d"), s1))
            shard_g[0] = dict(zip(("g", "u", "d"), s0))
        else:
            dqkv, _ = _sb_bwd(sv["qkv"], dmixed, sv["rtot"])
            dqkv, dtbl, _ = _flash_bwd(sv["qkv"], sv["mixed"], dmixed, sv["lse_d"], dqkv, 1, False, (tbl,))
            dqkv, dck, _ = _flash_bwd(sv["qkv"], sv["mixed"], dmixed, sv["lse_f"], dqkv, 2, True, (sv["cq"], sv["ck"]))
        dconv, dcw = _conv_bwd(sv["conv"], cw8[l], dmixed)
        dcum = jnp.pad(dck.reshape(BL, NH, S).transpose(0, 2, 1).reshape(T, NH), ((0, 0), (0, GATEW - NH)))
        dgate, dfb = _fox_post(dcum, sv["gate"], fb[l])
        drb = _dil_table_bwd(dtbl)
        G_in = _mm_tn(sv["h"], dqkv, None, C=NPAD, Ka=D, N=QKVW, tm=1024, tn=768, tk=512, ooff=0, name="grad_w_in_qkv")
        G_in = _mm_tn(sv["h"], dconv, G_in, C=NPAD, Ka=D, N=CONVW, tm=1024, tn=768, tk=512, ooff=3,
                      name="grad_w_in_conv")
        G_in = _mm_tn(sv["h"], dgate, G_in, C=NPAD, Ka=D, N=GATEW, tm=1024, tn=128, tk=512, ooff=24,
                      name="grad_w_in_gate")
        G[l] = {"in": G_in, "out": G_out, "g": G_g, "u": G_u, "d": G_d}
        dy = _mm([(dqkv, arow(512, QKVW), Win, wrow(512, QKVW, 0)),
                  (dconv, arow(512, CONVW), Win, wrow(512, CONVW, 3)),
                  (dgate, arow(512, GATEW), Win, wrow(512, GATEW, 24))],
                 nt=True, M=T, N=D, tm=512, tn=512, out_dtype=F32, name="proj_dx", res=ds1, res_scale=ALPHA)
        small_g[l] = dict(ln1_g=dg1, ln1_b=db1, ln2_g=dg2, ln2_b=db2, cw=dcw[0:3].reshape(1, CONVW),
                          fb=dfb[:, :NH], rb=drb[:, :NH])
    grad_x = dy.reshape(BL, S, D)

    late = [view(G[0]["in"]), view(G[0]["out"])]
    s_late = reduce_tail(late, list(_run_job(_sibling_job(late), "sibling_exchange")))
    shard_g[0]["in"], shard_g[0]["out"] = s_late

    rows = []
    for name in ("ln1_g", "ln1_b", "ln2_g", "ln2_b"):
        rows += [small_g[0][name], small_g[1][name]]
    rows += [_row(small_g[0]["cw"]), _row(small_g[1]["cw"]),
             _row(jnp.concatenate([small_g[0]["fb"], small_g[1]["fb"]], axis=0)),
             _row(small_g[0]["rb"] + small_g[1]["rb"])]
    rows.append(jnp.zeros((SMALL_ROWS - len(rows), D), F32))
    sg = _allreduce_small(jnp.concatenate(rows, axis=0))
    g_ln1_g, g_ln1_b, g_ln2_g, g_ln2_b = sg[0:2], sg[2:4], sg[4:6], sg[6:8]
    g_conv_full = sg[8:10, :CONVW].reshape(2, 3, 256)
    g_conv = lax.dynamic_slice(g_conv_full, (0, 0, me * 32), (2, 3, 32))
    g_fb = sg[10, :2 * NH].reshape(2, NH)
    g_rb = sg[11, :32 * NH].reshape(32, NH)

    def both(name):
        return jnp.stack([shard_g[0][name], shard_g[1][name]])

    g_in = both("in")
    g_w_in = jnp.concatenate([_unpermute_in(g_in[..., :QKVW]), g_in[..., QKVW:NPROJ]], axis=-1)
    g_w_out = both("out")
    g_w_gate = jnp.swapaxes(both("g"), 1, 2)
    g_w_up = jnp.swapaxes(both("u"), 1, 2)
    g_w_down = both("d")

    def big(w, g, m, v, tr):
        sh = w.shape
        f = lambda t: t.reshape(-1, sh[-1])
        return [t.reshape(sh) for t in _adamw(f(w), f(g), f(m), f(v), tr)]

    up_in = big(w_in, g_w_in, m_w_in, v_w_in, 64)
    up_out = big(w_out, g_w_out, m_w_out, v_w_out, 128)
    up_gate = big(w_gate, g_w_gate, m_w_gate, v_w_gate, 256)
    up_up = big(w_up, g_w_up, m_w_up, v_w_up, 256)
    up_down = big(w_down, g_w_down, m_w_down, v_w_down, 352)

    def pack(fbv, cwv, rbv, l1g, l1b, l2g, l2b):
        r = [l1g, l1b, l2g, l2b, _row(cwv), _row(fbv), _row(rbv)]
        r.append(jnp.zeros((SMALL_ROWS - 11, D), F32))
        return jnp.concatenate(r, axis=0)

    pw = pack(f_bias, conv_w, rel_bias, ln1_g, ln1_b, ln2_g, ln2_b)
    pg = pack(g_fb, g_conv, g_rb, g_ln1_g, g_ln1_b, g_ln2_g, g_ln2_b)
    pm = pack(m_f_bias, m_conv_w, m_rel_bias, m_ln1_g, m_ln1_b, m_ln2_g, m_ln2_b)
    pv = pack(v_f_bias, v_conv_w, v_rel_bias, v_ln1_g, v_ln1_b, v_ln2_g, v_ln2_b)
    ups = _adamw(pw, pg, pm, pv, SMALL_ROWS)

    def unpack(p):
        return dict(ln1_g=p[0:2], ln1_b=p[2:4], ln2_g=p[4:6], ln2_b=p[6:8],
                    conv_w=p[8, :192].reshape(2, 3, 32), f_bias=p[9, :2 * NH].reshape(2, NH),
                    rel_bias=p[10, :32 * NH].reshape(32, NH))

    sm = [unpack(p) for p in ups]

    def group(k):
        return (up_in[k], sm[k]["f_bias"], sm[k]["conv_w"], up_out[k], sm[k]["rel_bias"], sm[k]["ln1_g"],
                sm[k]["ln1_b"], up_gate[k], up_up[k], up_down[k], sm[k]["ln2_g"], sm[k]["ln2_b"])

    grads = (g_w_in, g_fb, g_conv, g_w_out, g_rb, g_ln1_g, g_ln1_b, g_w_gate, g_w_up, g_w_down, g_ln2_g, g_ln2_b)
    return (loss, grad_x) + grads + group(0) + group(1) + group(2)
```

```python
import math

import numpy as np
import jax
import jax.numpy as jnp
from jax import lax
from jax.experimental import pallas as pl
from jax.experimental.pallas import tpu as pltpu

F32 = jnp.float32
BF16 = jnp.bfloat16
MESH = pl.DeviceIdType.MESH

D = 1024
S = 2048
BL = 2
T = BL * S
NH = 4
DFF = 2816
NPROJ = 3076
NPAD = 3200
QKVW = 2304
CONVW = 768
GATEW = 128
PAIRW = 384
BQ = 128
HB = 2 * BQ
NB = S // BQ
NDEV = 8
NSTAT = BL * NH
ALPHA = 4.0 ** 0.25
SCALE = 0.125
NEG = -1e30
LN_EPS = 1e-5
ADAM_LR, ADAM_B1, ADAM_B2, ADAM_EPS, ADAM_WD, ADAM_STEP = 0.001, 0.9, 0.999, 1e-08, 0.01, 10
VMEM_LIMIT = 48 * 1024 * 1024
SMALL_ROWS = 16


def _bucket_thresholds():
    d = np.arange(0, S)
    nf = np.maximum(d, 1).astype(np.float32)
    large = 16 + (np.log(nf / np.float32(16)) / np.float32(math.log(128)) * np.float32(16)).astype(np.int32)
    b = np.where(d < 16, d, np.minimum(large, 31))
    return [int(np.argmax(b >= k)) for k in range(32)]


BUCKET_TH = _bucket_thresholds()


def _cp(sem=None):
    return pltpu.CompilerParams(dimension_semantics=sem, vmem_limit_bytes=VMEM_LIMIT)


def _dot(a, b):
    return lax.dot_general(a, b, (((1,), (0,)), ((), ())), preferred_element_type=F32)


def _dot_nt(a, b):
    return lax.dot_general(a, b, (((1,), (1,)), ((), ())), preferred_element_type=F32)


def _dot_tn(a, b):
    return lax.dot_general(a, b, (((0,), (0,)), ((), ())), preferred_element_type=F32)


def _split2(x):
    hi = x.astype(BF16)
    mid = (x - hi.astype(F32)).astype(BF16)
    return jnp.concatenate([hi, mid], axis=1)


def _split3(x):
    hi = x.astype(BF16)
    r = x - hi.astype(F32)
    mid = r.astype(BF16)
    lo = (r - mid.astype(F32)).astype(BF16)
    return jnp.concatenate([hi, mid, lo], axis=1)


def _log_sigmoid(u):
    return jnp.minimum(u, 0.0) - jnp.log1p(jnp.exp(-jnp.abs(u)))


def _log_sigmoid_tile(u):
    return jnp.minimum(u, 0.0) - jnp.log(1.0 + jnp.exp(jnp.minimum(u, -u)))


def _iota(shape, dim):
    return lax.broadcasted_iota(jnp.int32, shape, dim)


ANY_SPEC = pl.BlockSpec(memory_space=pl.ANY)
VMEM_SPEC = pl.BlockSpec(memory_space=pltpu.VMEM)


def _mm(pairs, *, nt, M, N, tm, tn, out_dtype, name, res=None, res_scale=1.0):
    n = len(pairs)

    def body(*refs):
        acc = None
        for p in range(n):
            a = refs[2 * p][...].astype(BF16)
            b = refs[2 * p + 1][...]
            d = _dot_nt(a, b) if nt else _dot(a, b)
            acc = d if acc is None else acc + d
        if res is not None:
            acc = acc + res_scale * refs[2 * n][...]
        refs[-1][...] = acc.astype(out_dtype)

    ops, specs = [], []
    for a, asp, b, bsp in pairs:
        ops += [a, b]
        specs += [asp, bsp]
    if res is not None:
        ops.append(res)
        specs.append(pl.BlockSpec((tm, tn), lambda i, j: (i, j)))
    return pl.pallas_call(
        body, name=name, grid=(M // tm, N // tn), in_specs=specs,
        out_specs=pl.BlockSpec((tm, tn), lambda i, j: (i, j)),
        out_shape=jax.ShapeDtypeStruct((M, N), out_dtype),
        compiler_params=_cp(("parallel", "parallel")))(*ops)


def _mm_tn(a, b, gbuf, *, C, Ka, N, tm, tn, tk, ooff, name):
    def body(*refs):
        a_ref, b_ref, o_ref = refs[0], refs[1], refs[-1]
        k = pl.program_id(2)
        d = _dot_tn(a_ref[...].astype(BF16), b_ref[...].astype(BF16))

        @pl.when(k == 0)
        def _():
            o_ref[...] = d

        @pl.when(k > 0)
        def _():
            o_ref[...] += d

    ops = [a, b] + ([] if gbuf is None else [gbuf])
    return pl.pallas_call(
        body, name=name, grid=(Ka // tm, N // tn, T // tk),
        in_specs=[pl.BlockSpec((tk, tm), lambda i, j, k: (k, i)),
                  pl.BlockSpec((tk, tn), lambda i, j, k: (k, j))] + ([] if gbuf is None else [ANY_SPEC]),
        out_specs=pl.BlockSpec((tm, tn), lambda i, j, k: (i, ooff + j)),
        out_shape=jax.ShapeDtypeStruct((Ka, C), F32),
        input_output_aliases={} if gbuf is None else {2: 0},
        compiler_params=_cp(("parallel", "parallel", "arbitrary")))(*ops)


def _ffn_up(x1, wgt, wut, layer):
    tm, tn = 1024, 256

    def body(x_ref, wg_ref, wu_ref, g_ref, u_ref, a_ref):
        xb = x_ref[...]
        g = _dot_nt(xb, wg_ref[...])
        u = _dot_nt(xb, wu_ref[...])
        g_ref[...] = g.astype(BF16)
        u_ref[...] = u.astype(BF16)
        a_ref[...] = (g * jax.nn.sigmoid(g) * u).astype(BF16)

    wspec = pl.BlockSpec((None, tn, D), lambda i, j: (layer, j, 0))
    ospec = pl.BlockSpec((tm, tn), lambda i, j: (i, j))
    return pl.pallas_call(
        body, name="ffn_up", grid=(T // tm, DFF // tn),
        in_specs=[pl.BlockSpec((tm, D), lambda i, j: (i, 0)), wspec, wspec],
        out_specs=[ospec, ospec, ospec],
        out_shape=[jax.ShapeDtypeStruct((T, DFF), BF16)] * 3,
        compiler_params=_cp(("parallel", "parallel")))(x1, wgt, wut)


def _ffn_da(dffn, wd, g, u, layer):
    tm, tn = 1024, 256

    def body(d_ref, wd_ref, g_ref, u_ref, dg_ref, du_ref):
        da = _dot_nt(d_ref[...], wd_ref[...])
        gv = g_ref[...].astype(F32)
        sg = jax.nn.sigmoid(gv)
        dg_ref[...] = (da * u_ref[...].astype(F32) * (sg * (1.0 + gv * (1.0 - sg)))).astype(BF16)
        du_ref[...] = (da * (gv * sg)).astype(BF16)

    ospec = pl.BlockSpec((tm, tn), lambda i, j: (i, j))
    return pl.pallas_call(
        body, name="ffn_da", grid=(T // tm, DFF // tn),
        in_specs=[pl.BlockSpec((tm, D), lambda i, j: (i, 0)),
                  pl.BlockSpec((None, tn, D), lambda i, j: (layer, j, 0)), ospec, ospec],
        out_specs=[ospec, ospec],
        out_shape=[jax.ShapeDtypeStruct((T, DFF), BF16), jax.ShapeDtypeStruct((T, DFF), BF16)],
        compiler_params=_cp(("parallel", "parallel")))(dffn, wd, g, u)


def _ln_fwd(x, f, gam, bet):
    tm = 256

    def body(x_ref, f_ref, g_ref, b_ref, y_ref, xh_ref, r_ref, yb_ref):
        s = ALPHA * x_ref[...] + f_ref[...]
        mu = jnp.mean(s, axis=-1, keepdims=True)
        xc = s - mu
        var = jnp.mean(xc * xc, axis=-1, keepdims=True)
        r = lax.rsqrt(var + LN_EPS)
        xh = xc * r
        xh_ref[...] = xh
        r_ref[...] = r
        y = xh * g_ref[...] + b_ref[...]
        y_ref[...] = y
        yb_ref[...] = y.astype(BF16)

    row = pl.BlockSpec((tm, D), lambda i: (i, 0))
    vec = pl.BlockSpec((1, D), lambda i: (0, 0))
    return pl.pallas_call(
        body, name="ln_fwd", grid=(T // tm,), in_specs=[row, row, vec, vec],
        out_specs=[row, row, pl.BlockSpec((tm, 1), lambda i: (i, 0)), row],
        out_shape=[jax.ShapeDtypeStruct((T, D), F32), jax.ShapeDtypeStruct((T, D), F32),
                   jax.ShapeDtypeStruct((T, 1), F32), jax.ShapeDtypeStruct((T, D), BF16)],
        compiler_params=_cp(("parallel",)))(x, f, gam, bet)


def _ln_bwd(dy, xh, r, gam):
    tm = 256

    def body(dy_ref, xh_ref, r_ref, g_ref, ds_ref, dg_ref, db_ref, dsb_ref):
        i = pl.program_id(0)
        dyv = dy_ref[...]
        xhv = xh_ref[...]
        dxh = dyv * g_ref[...]
        m1 = jnp.mean(dxh, axis=-1, keepdims=True)
        m2 = jnp.mean(dxh * xhv, axis=-1, keepdims=True)
        ds = r_ref[...] * (dxh - m1 - xhv * m2)
        ds_ref[...] = ds
        dsb_ref[...] = ds.astype(BF16)
        pg = jnp.sum(dyv * xhv, axis=0, keepdims=True)
        pb = jnp.sum(dyv, axis=0, keepdims=True)

        @pl.when(i == 0)
        def _():
            dg_ref[...] = pg
            db_ref[...] = pb

        @pl.when(i > 0)
        def _():
            dg_ref[...] += pg
            db_ref[...] += pb

    row = pl.BlockSpec((tm, D), lambda i: (i, 0))
    vec = pl.BlockSpec((1, D), lambda i: (0, 0))
    return pl.pallas_call(
        body, name="ln_bwd", grid=(T // tm,),
        in_specs=[row, row, pl.BlockSpec((tm, 1), lambda i: (i, 0)), vec],
        out_specs=[row, vec, vec, row],
        out_shape=[jax.ShapeDtypeStruct((T, D), F32), jax.ShapeDtypeStruct((1, D), F32),
                   jax.ShapeDtypeStruct((1, D), F32), jax.ShapeDtypeStruct((T, D), BF16)],
        compiler_params=_cp(("arbitrary",)))(dy, xh, r, gam)


def _loss_grad(y, tgt):
    tm = 256

    def body(y_ref, t_ref, l_ref, dy_ref):
        i = pl.program_id(0)
        e = y_ref[...] - t_ref[...]
        dy_ref[...] = e * (1.0 / D)
        p = jnp.sum(jnp.sum(e * e, axis=1, keepdims=True), axis=0, keepdims=True)

        @pl.when(i == 0)
        def _():
            l_ref[...] = p

        @pl.when(i > 0)
        def _():
            l_ref[...] += p

    row = pl.BlockSpec((tm, D), lambda i: (i, 0))
    return pl.pallas_call(
        body, name="loss_grad", grid=(T // tm,), in_specs=[row, row],
        out_specs=[pl.BlockSpec((1, 1), lambda i: (0, 0)), row],
        out_shape=[jax.ShapeDtypeStruct((1, 1), F32), jax.ShapeDtypeStruct((T, D), F32)],
        compiler_params=_cp(("arbitrary",)))(y, tgt)


def _adamw(w, g, m, v, tr):
    R, C = w.shape

    def body(w_ref, g_ref, m_ref, v_ref, d_ref, m2_ref, v2_ref):
        gv = g_ref[...]
        m2 = ADAM_B1 * m_ref[...] + (1.0 - ADAM_B1) * gv
        v2 = ADAM_B2 * v_ref[...] + (1.0 - ADAM_B2) * (gv * gv)
        m_hat = m2 / (1.0 - ADAM_B1 ** ADAM_STEP)
        v_hat = v2 / (1.0 - ADAM_B2 ** ADAM_STEP)
        d_ref[...] = -ADAM_LR * (m_hat / (jnp.sqrt(v_hat) + ADAM_EPS) + ADAM_WD * w_ref[...])
        m2_ref[...] = m2
        v2_ref[...] = v2

    blk = pl.BlockSpec((tr, C), lambda i: (i, 0))
    sh = jax.ShapeDtypeStruct((R, C), F32)
    return pl.pallas_call(
        body, name="adamw", grid=(R // tr,), in_specs=[blk] * 4, out_specs=[blk] * 3,
        out_shape=[sh, sh, sh], compiler_params=_cp(("parallel",)))(w, g, m, v)


CHAINS = [(p, b) for p in range(2) for b in range(BL)]
NC = len(CHAINS)


def _lane_masks():
    lane = _iota((1, BQ), 1)
    m0 = (lane < 64).astype(BF16)
    return m0, 1.0 - m0


def _stack(x, m0, m1):
    return jnp.concatenate([x * m0, x * m1], axis=0)


def _lanes(a):
    return jnp.concatenate([a[:BQ], a[BQ:]], axis=1)


def _per_lane(v):
    return jnp.where(_iota((BQ, BQ), 1) < 64, v[:BQ], v[BQ:])


def _diag_valid(strict):
    r = _iota((HB, BQ), 0) & (BQ - 1)
    c = _iota((HB, BQ), 1)
    return (c < r) if strict else (c <= r)


def _rows(b, i):
    return pl.ds(pl.multiple_of(b * S + i * BQ, BQ), BQ)


def _load_q(qkv_v, p, b, i, m0, m1):
    return _stack(qkv_v[_rows(b, i), p * PAIRW:p * PAIRW + BQ] * SCALE, m0, m1)


def _load_kv(qkv_v, p, b, j):
    r = _rows(b, j)
    return qkv_v[r, p * PAIRW + BQ:p * PAIRW + 2 * BQ], qkv_v[r, p * PAIRW + 2 * BQ:p * PAIRW + 3 * BQ]


def _stat_cols(tile8, p, b):
    lane = _iota((BQ, NSTAT), 1)
    c = b * NH + 2 * p
    return jnp.concatenate([jnp.sum(jnp.where(lane == c, tile8, 0.0), axis=1, keepdims=True),
                            jnp.sum(jnp.where(lane == c + 1, tile8, 0.0), axis=1, keepdims=True)], axis=0)


def _stat_tile(cols):
    lane = _iota((BQ, NSTAT), 1)
    t = jnp.zeros((BQ, NSTAT), F32)
    for (p, b), v in cols.items():
        c = b * NH + 2 * p
        t = t + jnp.where(lane == c, v[:BQ], 0.0) + jnp.where(lane == c + 1, v[BQ:], 0.0)
    return t


def _key_rows(ck_ref, p, b, j):
    c = b * NH + 2 * p
    kk = pl.ds(pl.multiple_of(j * BQ, BQ), BQ)
    return jnp.concatenate([jnp.broadcast_to(ck_ref[c:c + 1, kk], (BQ, BQ)),
                            jnp.broadcast_to(ck_ref[c + 1:c + 2, kk], (BQ, BQ))], axis=0)


def _copy_in(src, dst, sem):
    cp = pltpu.make_async_copy(src, dst, sem)
    cp.start()
    cp.wait()


STAT_SHAPE = jax.ShapeDtypeStruct((S, NSTAT), F32)
SLAB_QKV = pltpu.VMEM((T, 2 * PAIRW), BF16)
SLAB_OUT = pltpu.VMEM((T, 2 * BQ), BF16)
ACC_KV = pltpu.VMEM((2, T, BQ), F32)


class _Job:
    def __init__(self, ins, out_shapes, aliases, sems, start, finish):
        self.ins, self.out_shapes, self.aliases, self.sems = list(ins), list(out_shapes), dict(aliases), list(sems)
        self.start, self.finish = start, finish


def _host_call(body, name, ins, in_specs, out_shapes, out_specs, scratch, aliases, job):
    n_in, n_out, n_scr = len(ins), len(out_shapes), len(scratch)
    jins = job.ins if job else []
    jouts = job.out_shapes if job else []
    jsems = job.sems if job else []

    def wrapped(*refs):
        a = n_in
        b = a + len(jins)
        c = b + n_out
        d = c + len(jouts)
        e = d + n_scr
        comm = None
        if job:
            jrefs = (refs[a:b], refs[c:d], refs[e:])
            comm = (lambda: job.start(*jrefs), lambda st: job.finish(st, *jrefs))
        body(refs[:a], refs[b:c], refs[d:e], comm)

    al = dict(aliases)
    if job:
        for ji, jo in job.aliases.items():
            al[n_in + ji] = n_out + jo
    res = pl.pallas_call(
        wrapped, name=name, in_specs=list(in_specs) + [ANY_SPEC] * len(jins),
        out_specs=list(out_specs) + [ANY_SPEC] * len(jouts), out_shape=list(out_shapes) + list(jouts),
        scratch_shapes=list(scratch) + list(jsems), input_output_aliases=al,
        compiler_params=_cp())(*ins, *jins)
    return res[:n_out], res[n_out:]


def _sb_fwd(qkv, job=None):
    def body(ins, outs, scr, comm):
        (qkv_hbm,), (o_hbm, r_ref), (qkv_v, o_v, sem) = ins, outs, scr
        _copy_in(qkv_hbm.at[:, pl.ds(0, 2 * PAIRW)], qkv_v, sem)
        st = comm[0]() if comm else None
        m0, m1 = _lane_masks()
        valid = _diag_valid(True)
        u2 = ((_iota((HB, BQ), 0) & (BQ - 1)) > _iota((HB, BQ), 1)).astype(BF16)

        def steps(qs, j, cs, diag):
            kv = [_load_kv(qkv_v, p, b, j) for p, b in CHAINS]
            zs = [_dot_nt(qs[c], kv[c][0]) for c in range(NC)]
            lbs, lrs = [], []
            for c in range(NC):
                lb = _log_sigmoid_tile(zs[c])
                lr = lb - zs[c]
                if diag:
                    lr = jnp.where(valid, lr, 0.0)
                lbs.append(lb)
                lrs.append(lr)
            tails = [_dot(_split2(lrs[c]), u2) for c in range(NC)]
            out = []
            for c in range(NC):
                tail_c, acc = cs[c]
                a = jnp.exp(lbs[c] + tails[c] + tail_c)
                if diag:
                    a = jnp.where(valid, a, 0.0)
                acc = acc + _dot(_lanes(a.astype(BF16)), _stack(kv[c][1], m0, m1))
                out.append((tail_c + jnp.sum(lrs[c], axis=1, keepdims=True), acc))
            return tuple(out)

        def qblock(i, _):
            qs = [_load_q(qkv_v, p, b, i, m0, m1) for p, b in CHAINS]
            zero = (jnp.zeros((HB, 1), F32), jnp.zeros((BQ, BQ), F32))
            cs = steps(qs, i, (zero,) * NC, True)
            cs = lax.fori_loop(1, i + 1, lambda jj, cs: steps(qs, i - jj, cs, False), cs)
            for c, (p, b) in enumerate(CHAINS):
                o_v[_rows(b, i), p * BQ:(p + 1) * BQ] = cs[c][1].astype(BF16)
            r_ref[_rows(0, i), :] = _stat_tile({pb: cs[c][0] for c, pb in enumerate(CHAINS)})
            return 0

        lax.fori_loop(0, NB, qblock, 0)
        _copy_in(o_v, o_hbm.at[:, pl.ds(0, 2 * BQ)], sem)
        if comm:
            comm[1](st)

    (mixed, rtot), extra = _host_call(
        body, "sb_fwd", [qkv], [ANY_SPEC], [jax.ShapeDtypeStruct((T, D), BF16), STAT_SHAPE], [ANY_SPEC, VMEM_SPEC],
        [SLAB_QKV, SLAB_OUT, pltpu.SemaphoreType.DMA], {}, job)
    return mixed, rtot, extra


def _sb_bwd(qkv, dmixed, rtot, job=None):
    def body(ins, outs, scr, comm):
        (qkv_hbm, do_hbm, r_ref), (dqkv_hbm,), (qkv_v, do_v, dq_v, dk_s, dv_s, sem) = ins, outs, scr
        _copy_in(qkv_hbm.at[:, pl.ds(0, 2 * PAIRW)], qkv_v, sem)
        _copy_in(do_hbm.at[:, pl.ds(0, 2 * BQ)], do_v, sem)
        st = comm[0]() if comm else None
        m0, m1 = _lane_masks()
        valid = _diag_valid(True)
        r2 = _iota((HB, BQ), 0) & (BQ - 1)
        c2 = _iota((HB, BQ), 1)
        u2 = (r2 > c2).astype(BF16)
        l2 = (r2 < c2).astype(BF16)
        dk_s[...] = jnp.zeros_like(dk_s)
        dv_s[...] = jnp.zeros_like(dv_s)

        def steps(qs, dos, rts, j, cs, diag):
            kv = [_load_kv(qkv_v, p, b, j) for p, b in CHAINS]
            zs = [_dot_nt(qs[c], kv[c][0]) for c in range(NC)]
            das = [_dot_nt(dos[c], kv[c][1]) for c in range(NC)]
            lbs, lrs, pre_ls = [], [], []
            for c in range(NC):
                lb = _log_sigmoid_tile(zs[c])
                lr = lb - zs[c]
                if diag:
                    lr = jnp.where(valid, lr, 0.0)
                lbs.append(lb)
                lrs.append(lr)
                pre_ls.append(cs[c][0] + jnp.sum(lr, axis=1, keepdims=True))
            tails = [_dot(_split2(lrs[c]), u2) for c in range(NC)]
            avs, gms = [], []
            for c in range(NC):
                a = jnp.exp(lbs[c] + tails[c] + (rts[c] - pre_ls[c]))
                if diag:
                    a = jnp.where(valid, a, 0.0)
                avs.append(a)
                gms.append(das[c] * a)
            befores = [_dot(_split2(gms[c]), l2) for c in range(NC)]
            dzbs = []
            for c in range(NC):
                beta = jnp.exp(lbs[c])
                dz = gms[c] * (1.0 - beta) - beta * (befores[c] + cs[c][1])
                if diag:
                    dz = jnp.where(valid, dz, 0.0)
                dzbs.append(dz.astype(BF16))
            out = []
            for c, (p, b) in enumerate(CHAINS):
                dq = cs[c][2] + _dot(_lanes(dzbs[c]), _stack(kv[c][0], m0, m1))
                dk_s[p, _rows(b, j), :] += _dot_tn(dzbs[c], qs[c])
                dv_s[p, _rows(b, j), :] += _dot_tn(avs[c].astype(BF16), dos[c])
                out.append((pre_ls[c], cs[c][1] + jnp.sum(gms[c], axis=1, keepdims=True), dq))
            return tuple(out)

        def qblock(i, _):
            r8 = r_ref[_rows(0, i), :]
            qs = [_load_q(qkv_v, p, b, i, m0, m1) for p, b in CHAINS]
            dos = [_stack(do_v[_rows(b, i), p * BQ:(p + 1) * BQ], m0, m1) for p, b in CHAINS]
            rts = [_stat_cols(r8, p, b) for p, b in CHAINS]
            z1 = jnp.zeros((HB, 1), F32)
            cs = ((z1, z1, jnp.zeros((BQ, BQ), F32)),) * NC
            cs = lax.fori_loop(0, i, lambda j, cs: steps(qs, dos, rts, j, cs, False), cs)
            cs = steps(qs, dos, rts, i, cs, True)
            for c, (p, b) in enumerate(CHAINS):
                dq_v[_rows(b, i), p * PAIRW:p * PAIRW + BQ] = (cs[c][2] * SCALE).astype(BF16)
            return 0

        lax.fori_loop(0, NB, qblock, 0)
        for p in range(2):
            dq_v[:, p * PAIRW + BQ:p * PAIRW + 2 * BQ] = dk_s[p].astype(BF16)
            dq_v[:, p * PAIRW + 2 * BQ:p * PAIRW + 3 * BQ] = dv_s[p].astype(BF16)
        _copy_in(dq_v, dqkv_hbm.at[:, pl.ds(0, 2 * PAIRW)], sem)
        if comm:
            comm[1](st)

    (dqkv,), extra = _host_call(
        body, "sb_bwd", [qkv, dmixed, rtot], [ANY_SPEC, ANY_SPEC, VMEM_SPEC],
        [jax.ShapeDtypeStruct((T, QKVW), BF16)], [ANY_SPEC],
        [SLAB_QKV, SLAB_OUT, SLAB_QKV, ACC_KV, ACC_KV, pltpu.SemaphoreType.DMA], {}, job)
    return dqkv, extra


def _flash_fwd(qkv, mixed, g, fox, bias, job=None):
    def body(ins, outs, scr, comm):
        if fox:
            qkv_hbm, cq_ref, ck_ref, _ = ins
        else:
            qkv_hbm, tbl_ref, _ = ins
        (o_hbm, lse_ref), (qkv_v, o_v, sem) = outs, scr
        _copy_in(qkv_hbm.at[:, pl.ds(g * 2 * PAIRW, 2 * PAIRW)], qkv_v, sem)
        st = comm[0]() if comm else None
        m0, m1 = _lane_masks()
        valid = _diag_valid(False)

        def steps(qs, cqs, i, j, cs, diag):
            kv = [_load_kv(qkv_v, p, b, j) for p, b in CHAINS]
            zs = [_dot_nt(qs[c], kv[c][0]) for c in range(NC)]
            prs, alphas, out = [], [], []
            for c, (p, b) in enumerate(CHAINS):
                m, l, _ = cs[c]
                if fox:
                    z = zs[c] + (cqs[c] - _key_rows(ck_ref, p, b, j))
                    if diag:
                        z = jnp.where(valid, z, NEG)
                else:
                    z = zs[c] + tbl_ref[p, i - j]
                m_new = jnp.maximum(m, jnp.max(z, axis=1, keepdims=True))
                alpha = jnp.exp(m - m_new)
                pr = jnp.exp(z - m_new)
                prs.append(pr.astype(BF16))
                alphas.append(alpha)
                out.append((m_new, alpha * l + jnp.sum(pr, axis=1, keepdims=True)))
            pvs = [_dot(_lanes(prs[c]), _stack(kv[c][1], m0, m1)) for c in range(NC)]
            return tuple((out[c][0], out[c][1], _per_lane(alphas[c]) * cs[c][2] + pvs[c]) for c in range(NC))

        def qblock(i, _):
            qs = [_load_q(qkv_v, p, b, i, m0, m1) for p, b in CHAINS]
            if fox:
                c8 = cq_ref[_rows(0, i), :]
                cqs = [_stat_cols(c8, p, b) for p, b in CHAINS]
            else:
                cqs = [None] * NC
            zero = (jnp.full((HB, 1), NEG, F32), jnp.zeros((HB, 1), F32), jnp.zeros((BQ, BQ), F32))
            cs = steps(qs, cqs, i, i, (zero,) * NC, True)
            cs = lax.fori_loop(1, i + 1, lambda jj, cs: steps(qs, cqs, i, i - jj, cs, False), cs)
            for c, (p, b) in enumerate(CHAINS):
                m, l, acc = cs[c]
                o_v[_rows(b, i), p * BQ:(p + 1) * BQ] = (acc / _per_lane(l)).astype(BF16)
            lse_ref[_rows(0, i), :] = _stat_tile({pb: cs[c][0] + jnp.log(cs[c][1]) for c, pb in enumerate(CHAINS)})
            return 0

        lax.fori_loop(0, NB, qblock, 0)
        _copy_in(o_v, o_hbm.at[:, pl.ds(g * 2 * BQ, 2 * BQ)], sem)
        if comm:
            comm[1](st)

    bias_specs = [VMEM_SPEC, VMEM_SPEC] if fox else [VMEM_SPEC]
    n_in = 2 + len(bias_specs)
    (mixed, lse), extra = _host_call(
        body, "fox_fwd" if fox else "dil_fwd", [qkv, *bias, mixed], [ANY_SPEC] + bias_specs + [ANY_SPEC],
        [jax.ShapeDtypeStruct((T, D), BF16), STAT_SHAPE], [ANY_SPEC, VMEM_SPEC],
        [SLAB_QKV, SLAB_OUT, pltpu.SemaphoreType.DMA], {n_in - 1: 0}, job)
    return mixed, lse, extra


def _flash_bwd(qkv, mixed, dmixed, lse, dqkv, g, fox, bias, job=None):
    def body(ins, outs, scr, comm):
        if fox:
            qkv_hbm, o_hbm, do_hbm, lse_ref, cq_ref, ck_ref, _ = ins
        else:
            qkv_hbm, o_hbm, do_hbm, lse_ref, tbl_ref, _ = ins
        (dqkv_hbm, db_ref), (qkv_v, o_v, do_v, dq_v, dk_s, dv_s, sem) = outs, scr
        _copy_in(qkv_hbm.at[:, pl.ds(g * 2 * PAIRW, 2 * PAIRW)], qkv_v, sem)
        _copy_in(do_hbm.at[:, pl.ds(g * 2 * BQ, 2 * BQ)], do_v, sem)
        if not fox:
            _copy_in(o_hbm.at[:, pl.ds(g * 2 * BQ, 2 * BQ)], o_v, sem)
        st = comm[0]() if comm else None
        m0, m1 = _lane_masks()
        valid = _diag_valid(False)
        dk_s[...] = jnp.zeros_like(dk_s)
        dv_s[...] = jnp.zeros_like(dv_s)
        db_ref[...] = jnp.zeros_like(db_ref)

        def probs(qs, dos, cqs, lses, i, j, diag):
            kv = [_load_kv(qkv_v, p, b, j) for p, b in CHAINS]
            zs = [_dot_nt(qs[c], kv[c][0]) for c in range(NC)]
            dps = [_dot_nt(dos[c], kv[c][1]) for c in range(NC)]
            prs = []
            for c, (p, b) in enumerate(CHAINS):
                if fox:
                    z = zs[c] + (cqs[c] - _key_rows(ck_ref, p, b, j))
                    if diag:
                        z = jnp.where(valid, z, NEG)
                else:
                    z = zs[c] + tbl_ref[p, i - j]
                prs.append(jnp.exp(z - lses[c]))
            return [kv[c][0] for c in range(NC)], prs, dps

        def qblock(i, _):
            l8 = lse_ref[_rows(0, i), :]
            qs = [_load_q(qkv_v, p, b, i, m0, m1) for p, b in CHAINS]
            dos = [_stack(do_v[_rows(b, i), p * BQ:(p + 1) * BQ], m0, m1) for p, b in CHAINS]
            lses = [_stat_cols(l8, p, b) for p, b in CHAINS]
            if fox:
                c8 = cq_ref[_rows(0, i), :]
                cqs = [_stat_cols(c8, p, b) for p, b in CHAINS]

                def dsteps(j, accs, diag):
                    _, prs, dps = probs(qs, dos, cqs, lses, i, j, diag)
                    return tuple(accs[c] + jnp.sum(prs[c] * dps[c], axis=1, keepdims=True) for c in range(NC))

                deltas = lax.fori_loop(0, i, lambda j, ds: dsteps(j, ds, False), (jnp.zeros((HB, 1), F32),) * NC)
                deltas = dsteps(i, deltas, True)
            else:
                cqs = [None] * NC
                deltas = []
                for c, (p, b) in enumerate(CHAINS):
                    ob = o_v[_rows(b, i), p * BQ:(p + 1) * BQ].astype(F32)
                    deltas.append(jnp.sum(dos[c].astype(F32) * jnp.concatenate([ob, ob], axis=0), axis=1, keepdims=True))

            def inner(j, dqs, diag):
                ks, prs, dps = probs(qs, dos, cqs, lses, i, j, diag)
                new, dzs = [], {}
                dzl = [prs[c] * (dps[c] - deltas[c]) for c in range(NC)]
                dzbs = [dz.astype(BF16) for dz in dzl]
                for c, (p, b) in enumerate(CHAINS):
                    dk_s[p, _rows(b, j), :] += _dot_tn(dzbs[c], qs[c])
                    dv_s[p, _rows(b, j), :] += _dot_tn(prs[c].astype(BF16), dos[c])
                    new.append(dqs[c] + _dot(_lanes(dzbs[c]), _stack(ks[c], m0, m1)))
                    dzs[(p, b)] = dzl[c]
                if fox:
                    kk = pl.ds(pl.multiple_of(j * BQ, BQ), BQ)
                    for (p, b), dz in dzs.items():
                        r = b * NH + 2 * p
                        db_ref[r:r + 1, kk] = db_ref[r:r + 1, kk] - jnp.sum(dz[:BQ], axis=0, keepdims=True)
                        db_ref[r + 1:r + 2, kk] = db_ref[r + 1:r + 2, kk] - jnp.sum(dz[BQ:], axis=0, keepdims=True)
                else:
                    for p in range(2):
                        db_ref[p, i - j] = db_ref[p, i - j] + (dzs[(p, 0)] + dzs[(p, 1)])
                return tuple(new)

            dqs = tuple(jnp.zeros((BQ, BQ), F32) for _ in CHAINS)
            dqs = lax.fori_loop(0, i, lambda j, d: inner(j, d, False), dqs)
            dqs = inner(i, dqs, True)
            for c, (p, b) in enumerate(CHAINS):
                dq_v[_rows(b, i), p * PAIRW:p * PAIRW + BQ] = (dqs[c] * SCALE).astype(BF16)
            return 0

        lax.fori_loop(0, NB, qblock, 0)
        for p in range(2):
            dq_v[:, p * PAIRW + BQ:p * PAIRW + 2 * BQ] = dk_s[p].astype(BF16)
            dq_v[:, p * PAIRW + 2 * BQ:p * PAIRW + 3 * BQ] = dv_s[p].astype(BF16)
        _copy_in(dq_v, dqkv_hbm.at[:, pl.ds(g * 2 * PAIRW, 2 * PAIRW)], sem)
        if comm:
            comm[1](st)

    bias_specs = [VMEM_SPEC, VMEM_SPEC] if fox else [VMEM_SPEC]
    db_shape = jax.ShapeDtypeStruct((NSTAT, S), F32) if fox else jax.ShapeDtypeStruct((2, NB, HB, BQ), F32)
    n_in = 5 + len(bias_specs)
    (dqkv, db), extra = _host_call(
        body, "fox_bwd" if fox else "dil_bwd", [qkv, mixed, dmixed, lse, *bias, dqkv],
        [ANY_SPEC, ANY_SPEC, ANY_SPEC, VMEM_SPEC] + bias_specs + [ANY_SPEC],
        [jax.ShapeDtypeStruct((T, QKVW), BF16), db_shape], [ANY_SPEC, VMEM_SPEC],
        [SLAB_QKV, SLAB_OUT, SLAB_OUT, SLAB_QKV, ACC_KV, ACC_KV, pltpu.SemaphoreType.DMA], {n_in - 1: 0}, job)
    return dqkv, db, extra


def _stacked_delta(d):
    return d * BQ + (_iota((HB, BQ), 0) & (BQ - 1)) - _iota((HB, BQ), 1)


def _buckets_in(d):
    lo, hi = max(d * BQ - (BQ - 1), 0), d * BQ + BQ - 1
    return [b for b in range(32) if BUCKET_TH[b] <= hi and (b == 31 or BUCKET_TH[b + 1] > lo)]


def _in_bucket(delta, b):
    m = delta >= BUCKET_TH[b]
    return m if b == 31 else m & (delta < BUCKET_TH[b + 1])


def _dil_table(rel_bias):
    def body(rb_ref, o_ref):
        for d in range(NB):
            delta = _stacked_delta(d)
            pos = delta >= 0
            n = ((pos & (delta <= 128)).astype(jnp.int32)
                 + (pos & (delta <= 512) & ((delta & 3) == 0)).astype(jnp.int32)
                 + (pos & ((delta & 15) == 0)).astype(jnp.int32))
            logn = jnp.where(n == 3, math.log(3.0), jnp.where(n == 2, math.log(2.0), jnp.where(n == 1, 0.0, NEG)))
            head1 = _iota((HB, BQ), 0) >= BQ
            for p in range(2):
                val = jnp.zeros((HB, BQ), F32)
                for b in _buckets_in(d):
                    val = jnp.where(_in_bucket(delta, b), jnp.where(head1, rb_ref[b, 2 * p + 1], rb_ref[b, 2 * p]), val)
                o_ref[p, d] = val + logn

    return pl.pallas_call(
        body, name="dil_table", in_specs=[pl.BlockSpec(memory_space=pltpu.SMEM)], out_specs=VMEM_SPEC,
        out_shape=jax.ShapeDtypeStruct((2, NB, HB, BQ), F32), compiler_params=_cp())(rel_bias)


def _dil_table_bwd(dtbl):
    def body(dt_ref, o_ref):
        p = pl.program_id(0)
        rowi = _iota((32, BQ), 0)
        lanei = _iota((32, BQ), 1)

        @pl.when(p == 0)
        def _():
            o_ref[...] = jnp.zeros_like(o_ref)

        out = jnp.zeros((32, BQ), F32)
        for b in range(32):
            acc = None
            for d in range(NB):
                if b in _buckets_in(d):
                    t = jnp.where(_in_bucket(_stacked_delta(d), b), dt_ref[d], 0.0)
                    acc = t if acc is None else acc + t
            rs = jnp.sum(acc, axis=1, keepdims=True)
            s0 = jnp.sum(rs[:BQ], axis=0, keepdims=True)
            s1 = jnp.sum(rs[BQ:], axis=0, keepdims=True)
            out = (out + jnp.where((rowi == b) & (lanei == 2 * p), s0, 0.0)
                   + jnp.where((rowi == b) & (lanei == 2 * p + 1), s1, 0.0))
        o_ref[...] += out

    return pl.pallas_call(
        body, name="dil_table_bwd", grid=(2,),
        in_specs=[pl.BlockSpec((None, NB, HB, BQ), lambda p: (p, 0, 0, 0))],
        out_specs=pl.BlockSpec((32, BQ), lambda p: (0, 0)),
        out_shape=jax.ShapeDtypeStruct((32, BQ), F32),
        compiler_params=_cp(("arbitrary",)))(dtbl)


def _fox_prep(gate, fb):
    def body(g_ref, fb_ref, c_ref):
        tri = (_iota((BQ, BQ), 0) >= _iota((BQ, BQ), 1)).astype(BF16)

        def blk(i, carry):
            r0 = pl.multiple_of(i * BQ, BQ)
            lf = _log_sigmoid(g_ref[pl.ds(r0, BQ), :] + fb_ref[...])
            c = _dot(tri, _split3(lf))
            c_ref[pl.ds(r0, BQ), :] = c[:, 0:BQ] + c[:, BQ:2 * BQ] + c[:, 2 * BQ:3 * BQ] + carry
            return carry + jnp.sum(lf, axis=0, keepdims=True)

        lax.fori_loop(0, NB, blk, jnp.zeros((1, BQ), F32))

    blk = pl.BlockSpec((S, GATEW), lambda b: (b, 0))
    return pl.pallas_call(
        body, name="fox_prep", grid=(BL,), in_specs=[blk, pl.BlockSpec((1, GATEW), lambda b: (0, 0))],
        out_specs=blk, out_shape=jax.ShapeDtypeStruct((T, GATEW), F32),
        compiler_params=_cp(("parallel",)))(gate, fb)


def _fox_post(dcum, gate, fb):
    def body(dc_ref, g_ref, fb_ref, dg_ref, dfb_ref):
        b = pl.program_id(0)
        tri = (_iota((BQ, BQ), 0) <= _iota((BQ, BQ), 1)).astype(BF16)

        def blk(ii, carry):
            csum, dfb = carry
            r0 = pl.multiple_of((NB - 1 - ii) * BQ, BQ)
            dc = dc_ref[pl.ds(r0, BQ), :]
            c = _dot(tri, _split3(dc))
            dlf = c[:, 0:BQ] + c[:, BQ:2 * BQ] + c[:, 2 * BQ:3 * BQ] + csum
            dg = dlf * jnp.exp(_log_sigmoid(-(g_ref[pl.ds(r0, BQ), :] + fb_ref[...])))
            dg_ref[pl.ds(r0, BQ), :] = dg
            return csum + jnp.sum(dc, axis=0, keepdims=True), dfb + jnp.sum(dg, axis=0, keepdims=True)

        z = jnp.zeros((1, BQ), F32)
        _, dfb = lax.fori_loop(0, NB, blk, (z, z))

        @pl.when(b == 0)
        def _():
            dfb_ref[...] = dfb

        @pl.when(b > 0)
        def _():
            dfb_ref[...] += dfb

    blk = pl.BlockSpec((S, GATEW), lambda b: (b, 0))
    vec = pl.BlockSpec((1, GATEW), lambda b: (0, 0))
    return pl.pallas_call(
        body, name="fox_post", grid=(BL,), in_specs=[blk, blk, vec], out_specs=[blk, vec],
        out_shape=[jax.ShapeDtypeStruct((T, GATEW), F32), jax.ShapeDtypeStruct((1, GATEW), F32)],
        compiler_params=_cp(("arbitrary",)))(dcum, gate, fb)


def _shift_down(x, n):
    return jnp.where(_iota(x.shape, 0) >= n, pltpu.roll(x, n, 0), 0.0)


def _shift_up(x, n):
    return jnp.where(_iota(x.shape, 0) < S - n, pltpu.roll(x, S - n, 0), 0.0)


def _conv_fwd(conv, cw, mixed):
    W = 256

    def body(c_ref, w_ref, _, o_ref):
        u = c_ref[:, W:2 * W] * c_ref[:, 2 * W:3 * W]
        y = w_ref[0:1, :] * _shift_down(u, 2) + w_ref[1:2, :] * _shift_down(u, 1) + w_ref[2:3, :] * u
        o_ref[...] = (c_ref[:, 0:W] * y).astype(BF16)

    return pl.pallas_call(
        body, name="conv_fwd", grid=(BL,),
        in_specs=[pl.BlockSpec((S, CONVW), lambda b: (b, 0)), pl.BlockSpec((8, W), lambda b: (0, 0)), ANY_SPEC],
        out_specs=pl.BlockSpec((S, W), lambda b: (b, 3)),
        out_shape=jax.ShapeDtypeStruct((T, D), BF16), input_output_aliases={2: 0},
        compiler_params=_cp(("parallel",)))(conv, cw, mixed)


def _conv_bwd(conv, cw, dmixed):
    W = 256

    def body(c_ref, w_ref, do_ref, dc_ref, dw_ref):
        b = pl.program_id(0)
        bg = c_ref[:, 0:W]
        cg = c_ref[:, W:2 * W]
        hv = c_ref[:, 2 * W:3 * W]
        do = do_ref[...].astype(F32)
        u = cg * hv
        u1 = _shift_down(u, 1)
        u2 = _shift_down(u, 2)
        y = w_ref[0:1, :] * u2 + w_ref[1:2, :] * u1 + w_ref[2:3, :] * u
        dy = do * bg
        du = w_ref[2:3, :] * dy + w_ref[1:2, :] * _shift_up(dy, 1) + w_ref[0:1, :] * _shift_up(dy, 2)
        dc_ref[:, 0:W] = (do * y).astype(BF16)
        dc_ref[:, W:2 * W] = (du * hv).astype(BF16)
        dc_ref[:, 2 * W:3 * W] = (du * cg).astype(BF16)
        rowi = _iota((8, W), 0)
        dw = (jnp.where(rowi == 0, jnp.sum(dy * u2, axis=0, keepdims=True), 0.0)
              + jnp.where(rowi == 1, jnp.sum(dy * u1, axis=0, keepdims=True), 0.0)
              + jnp.where(rowi == 2, jnp.sum(dy * u, axis=0, keepdims=True), 0.0))

        @pl.when(b == 0)
        def _():
            dw_ref[...] = dw

        @pl.when(b > 0)
        def _():
            dw_ref[...] += dw

    return pl.pallas_call(
        body, name="conv_bwd", grid=(BL,),
        in_specs=[pl.BlockSpec((S, CONVW), lambda b: (b, 0)), pl.BlockSpec((8, W), lambda b: (0, 0)),
                  pl.BlockSpec((S, W), lambda b: (b, 3))],
        out_specs=[pl.BlockSpec((S, CONVW), lambda b: (b, 0)), pl.BlockSpec((8, W), lambda b: (0, 0))],
        out_shape=[jax.ShapeDtypeStruct((T, CONVW), BF16), jax.ShapeDtypeStruct((8, W), F32)],
        compiler_params=_cp(("arbitrary",)))(conv, cw, dmixed)


def _place():
    x, y, c = lax.axis_index("x"), lax.axis_index("y"), lax.axis_index("c")
    return x, y, c


def _allgather_weights(shards):
    n = len(shards)

    def body(*refs):
        ins, outs = refs[:n], refs[n:2 * n]
        send_sems, recv_sems, local_sems = refs[2 * n:]
        x, y, c = _place()
        me, sibling = (x, y, c), (x, y, 1 - c)
        chips = [(1 - x, y), (x, 1 - y), (1 - x, 1 - y)]

        def slot(a, p):
            return outs[a].at[:, 4 * p[0] + 2 * p[1] + p[2]]

        def copy(a, k, block, to, own=False):
            return pltpu.make_async_remote_copy(
                src_ref=ins[a] if own else slot(a, block), dst_ref=slot(a, block),
                send_sem=send_sems.at[a, k], recv_sem=recv_sems.at[a, k], device_id=to, device_id_type=MESH)

        mine = [pltpu.make_async_copy(ins[a], slot(a, me), local_sems.at[a]) for a in range(n)]
        for cp in mine:
            cp.start()
        first = []
        for a in range(n):
            first.append(copy(a, 0, me, sibling, own=True))
            first += [copy(a, 1 + j, me, (*chip, c), own=True) for j, chip in enumerate(chips)]
        for cp in first:
            cp.start()
        passed = []
        for j, chip in enumerate(chips):
            for a in range(n):
                copy(a, 1 + j, (*chip, c), me).wait_recv()
                cp = copy(a, 4 + j, (*chip, c), sibling)
                cp.start()
                passed.append(cp)
        for a in range(n):
            copy(a, 0, sibling, me).wait_recv()
            for j, chip in enumerate(chips):
                copy(a, 4 + j, (*chip, 1 - c), me).wait_recv()
        for cp in first + passed:
            cp.wait_send()
        for cp in mine:
            cp.wait()

    return pl.pallas_call(
        body, name="allgather_weights", in_specs=[ANY_SPEC] * n, out_specs=[ANY_SPEC] * n,
        out_shape=[jax.ShapeDtypeStruct((s.shape[0], NDEV) + s.shape[1:], s.dtype) for s in shards],
        scratch_shapes=[pltpu.SemaphoreType.DMA((n, 7)), pltpu.SemaphoreType.DMA((n, 7)),
                        pltpu.SemaphoreType.DMA((n,))],
        )(*shards)


def _allreduce_small(v):
    def body(v_ref, o_ref, slots, send_sems, recv_sems):
        x, y, c = _place()
        me = 4 * x + 2 * y + c
        slots[me] = v_ref[...]

        def copy(k):
            peer = (x ^ ((k >> 2) & 1), y ^ ((k >> 1) & 1), c ^ (k & 1))
            return pltpu.make_async_remote_copy(
                src_ref=v_ref, dst_ref=slots.at[me], send_sem=send_sems.at[k - 1], recv_sem=recv_sems.at[k - 1],
                device_id=peer, device_id_type=MESH)

        def arrival(k):
            return pltpu.make_async_remote_copy(
                src_ref=v_ref, dst_ref=slots.at[me ^ k], send_sem=send_sems.at[k - 1], recv_sem=recv_sems.at[k - 1],
                device_id=(x, y, c), device_id_type=MESH)

        sends = [copy(k) for k in range(1, NDEV)]
        for cp in sends:
            cp.start()
        for k in range(1, NDEV):
            arrival(k).wait_recv()
        for cp in sends:
            cp.wait_send()
        acc = slots[0]
        for d in range(1, NDEV):
            acc = acc + slots[d]
        o_ref[...] = acc

    return pl.pallas_call(
        body, name="allreduce_small", in_specs=[VMEM_SPEC], out_specs=VMEM_SPEC,
        out_shape=jax.ShapeDtypeStruct(v.shape, F32),
        scratch_shapes=[pltpu.VMEM((NDEV,) + v.shape, F32), pltpu.SemaphoreType.DMA((NDEV - 1,)),
                        pltpu.SemaphoreType.DMA((NDEV - 1,))],
        )(v)


def _sibling_exchange(grads):
    n = len(grads)

    def body(*refs):
        ins, outs = refs[:n], refs[n:2 * n]
        send_sems, recv_sems = refs[2 * n:]
        x, y, c = _place()
        cps = [pltpu.make_async_remote_copy(
            src_ref=ins[a].at[:, :, 1 - c], dst_ref=outs[a], send_sem=send_sems.at[a], recv_sem=recv_sems.at[a],
            device_id=(x, y, 1 - c), device_id_type=MESH) for a in range(n)]
        for cp in cps:
            cp.start()
        for cp in cps:
            cp.wait()

    return pl.pallas_call(
        body, name="sibling_exchange", in_specs=[ANY_SPEC] * n, out_specs=[ANY_SPEC] * n,
        out_shape=[jax.ShapeDtypeStruct(g.shape[:2] + g.shape[3:], F32) for g in grads],
        scratch_shapes=[pltpu.SemaphoreType.DMA((n,)), pltpu.SemaphoreType.DMA((n,))],
        )(*grads)


def _pair_sum(grad, got, core):
    _, _, _, rows, N = grad.shape

    def body(c_ref, g_ref, r_ref, o_ref):
        o_ref[...] = (g_ref[...] + r_ref[...]).astype(BF16)

    return pl.pallas_call(
        body, name="pair_sum",
        grid_spec=pltpu.PrefetchScalarGridSpec(
            num_scalar_prefetch=1, grid=(2, 4),
            in_specs=[pl.BlockSpec((None, None, None, rows, N), lambda l, k, c: (l, k, c[0], 0, 0)),
                      pl.BlockSpec((None, None, rows, N), lambda l, k, c: (l, k, 0, 0))],
            out_specs=pl.BlockSpec((None, None, rows, N), lambda l, k, c: (l, k, 0, 0))),
        out_shape=jax.ShapeDtypeStruct((2, 4, rows, N), BF16),
        compiler_params=_cp(("parallel", "parallel")))(core, grad, got)


def _chip_exchange(psums):
    n = len(psums)

    def body(*refs):
        ins, outs = refs[:n], refs[n:2 * n]
        send_sems, recv_sems, local_sems = refs[2 * n:]
        x, y, c = _place()
        mychip = 2 * x + y
        chips = [(1 - x, y), (x, 1 - y), (1 - x, 1 - y)]
        local = [pltpu.make_async_copy(ins[a].at[:, mychip], outs[a].at[:, mychip], local_sems.at[a]) for a in range(n)]
        for cp in local:
            cp.start()
        sends = []
        for a in range(n):
            for j, chip in enumerate(chips):
                sends.append(pltpu.make_async_remote_copy(
                    src_ref=ins[a].at[:, 2 * chip[0] + chip[1]], dst_ref=outs[a].at[:, mychip],
                    send_sem=send_sems.at[a, j], recv_sem=recv_sems.at[a, j],
                    device_id=(*chip, c), device_id_type=MESH))
        for cp in sends:
            cp.start()
        for a in range(n):
            for j, chip in enumerate(chips):
                pltpu.make_async_remote_copy(
                    src_ref=ins[a].at[:, mychip], dst_ref=outs[a].at[:, 2 * chip[0] + chip[1]],
                    send_sem=send_sems.at[a, j], recv_sem=recv_sems.at[a, j],
                    device_id=(x, y, c), device_id_type=MESH).wait_recv()
        for cp in sends:
            cp.wait_send()
        for cp in local:
            cp.wait()

    return pl.pallas_call(
        body, name="chip_exchange", in_specs=[ANY_SPEC] * n, out_specs=[ANY_SPEC] * n,
        out_shape=[jax.ShapeDtypeStruct(p.shape, BF16) for p in psums],
        scratch_shapes=[pltpu.SemaphoreType.DMA((n, 3)), pltpu.SemaphoreType.DMA((n, 3)),
                        pltpu.SemaphoreType.DMA((n,))],
        )(*psums)


def _chip_sum(parts):
    _, _, rows, N = parts.shape

    def body(p_ref, o_ref):
        acc = p_ref[0].astype(F32)
        for k in range(1, 4):
            acc = acc + p_ref[k].astype(F32)
        o_ref[...] = acc

    return pl.pallas_call(
        body, name="chip_sum", grid=(2,),
        in_specs=[pl.BlockSpec((None, 4, rows, N), lambda l: (l, 0, 0, 0))],
        out_specs=pl.BlockSpec((None, rows, N), lambda l: (l, 0, 0)),
        out_shape=jax.ShapeDtypeStruct((2, rows, N), F32), compiler_params=_cp(("parallel",)))(parts)


def _chips_of(x, y):
    return [(1 - x, y), (x, 1 - y), (1 - x, 1 - y)]


def _dev(p):
    return 4 * p[0] + 2 * p[1] + p[2]


def _gather_job_a(shards):
    n = len(shards)

    def peers(x, y, c):
        return [(x, y, 1 - c)] + [(*chip, c) for chip in _chips_of(x, y)]

    def start(ins, outs, sems):
        send, recv, loc = sems
        x, y, c = _place()
        me = (x, y, c)
        cps = []
        for a in range(n):
            cps.append(pltpu.make_async_copy(ins[a], outs[a].at[_dev(me)], loc.at[a]))
            for k, peer in enumerate(peers(x, y, c)):
                cps.append(pltpu.make_async_remote_copy(
                    src_ref=ins[a], dst_ref=outs[a].at[_dev(me)], send_sem=send.at[a, k], recv_sem=recv.at[a, k],
                    device_id=peer, device_id_type=MESH))
        for cp in cps:
            cp.start()
        return cps

    def finish(cps, ins, outs, sems):
        send, recv, loc = sems
        x, y, c = _place()
        for a in range(n):
            for k, peer in enumerate(peers(x, y, c)):
                pltpu.make_async_remote_copy(
                    src_ref=ins[a], dst_ref=outs[a].at[_dev(peer)], send_sem=send.at[a, k], recv_sem=recv.at[a, k],
                    device_id=(x, y, c), device_id_type=MESH).wait_recv()
        for a in range(n):
            cps[5 * a].wait()
            for k in range(4):
                cps[5 * a + 1 + k].wait_send()

    return _Job(shards, [jax.ShapeDtypeStruct((NDEV,) + s.shape, s.dtype) for s in shards], {},
                [pltpu.SemaphoreType.DMA((n, 4)), pltpu.SemaphoreType.DMA((n, 4)), pltpu.SemaphoreType.DMA((n,))],
                start, finish)


def _gather_job_b(gathered):
    n = len(gathered)

    def start(ins, outs, sems):
        send, recv = sems
        x, y, c = _place()
        cps = []
        for a in range(n):
            for j, chip in enumerate(_chips_of(x, y)):
                blk = outs[a].at[_dev((*chip, c))]
                cps.append(pltpu.make_async_remote_copy(
                    src_ref=blk, dst_ref=blk, send_sem=send.at[a, j], recv_sem=recv.at[a, j],
                    device_id=(x, y, 1 - c), device_id_type=MESH))
        for cp in cps:
            cp.start()
        return cps

    def finish(cps, ins, outs, sems):
        send, recv = sems
        x, y, c = _place()
        for a in range(n):
            for j, chip in enumerate(_chips_of(x, y)):
                blk = outs[a].at[_dev((*chip, 1 - c))]
                pltpu.make_async_remote_copy(
                    src_ref=blk, dst_ref=blk, send_sem=send.at[a, j], recv_sem=recv.at[a, j],
                    device_id=(x, y, c), device_id_type=MESH).wait_recv()
        for cp in cps:
            cp.wait_send()

    return _Job(gathered, [jax.ShapeDtypeStruct(g.shape, g.dtype) for g in gathered], {a: a for a in range(n)},
                [pltpu.SemaphoreType.DMA((n, 3)), pltpu.SemaphoreType.DMA((n, 3))], start, finish)


def _sibling_job(grads):
    n = len(grads)

    def start(ins, outs, sems):
        send, recv = sems
        x, y, c = _place()
        cps = [pltpu.make_async_remote_copy(
            src_ref=ins[a].at[:, 1 - c], dst_ref=outs[a], send_sem=send.at[a], recv_sem=recv.at[a],
            device_id=(x, y, 1 - c), device_id_type=MESH) for a in range(n)]
        for cp in cps:
            cp.start()
        return cps

    def finish(cps, ins, outs, sems):
        for cp in cps:
            cp.wait()

    return _Job(grads, [jax.ShapeDtypeStruct(g.shape[:1] + g.shape[2:], F32) for g in grads], {},
                [pltpu.SemaphoreType.DMA((n,)), pltpu.SemaphoreType.DMA((n,))], start, finish)


def _chip_job(psums):
    n = len(psums)

    def start(ins, outs, sems):
        send, recv, loc = sems
        x, y, c = _place()
        mychip = 2 * x + y
        cps = []
        for a in range(n):
            cps.append(pltpu.make_async_copy(ins[a].at[mychip], outs[a].at[mychip], loc.at[a]))
            for j, chip in enumerate(_chips_of(x, y)):
                cps.append(pltpu.make_async_remote_copy(
                    src_ref=ins[a].at[2 * chip[0] + chip[1]], dst_ref=outs[a].at[mychip],
                    send_sem=send.at[a, j], recv_sem=recv.at[a, j], device_id=(*chip, c), device_id_type=MESH))
        for cp in cps:
            cp.start()
        return cps

    def finish(cps, ins, outs, sems):
        send, recv, loc = sems
        x, y, c = _place()
        mychip = 2 * x + y
        for a in range(n):
            for j, chip in enumerate(_chips_of(x, y)):
                pltpu.make_async_remote_copy(
                    src_ref=ins[a].at[mychip], dst_ref=outs[a].at[2 * chip[0] + chip[1]],
                    send_sem=send.at[a, j], recv_sem=recv.at[a, j], device_id=(x, y, c), device_id_type=MESH).wait_recv()
        for a in range(n):
            cps[4 * a].wait()
            for j in range(3):
                cps[4 * a + 1 + j].wait_send()

    return _Job(psums, [jax.ShapeDtypeStruct(p.shape, BF16) for p in psums], {},
                [pltpu.SemaphoreType.DMA((n, 3)), pltpu.SemaphoreType.DMA((n, 3)), pltpu.SemaphoreType.DMA((n,))],
                start, finish)


def _join_jobs(*jobs):
    jobs = [j for j in jobs if j is not None]
    if len(jobs) <= 1:
        return jobs[0] if jobs else None
    cut = lambda seq, sizes: [seq[sum(sizes[:k]):sum(sizes[:k + 1])] for k in range(len(sizes))]
    n_in = [len(j.ins) for j in jobs]
    n_out = [len(j.out_shapes) for j in jobs]
    n_sem = [len(j.sems) for j in jobs]
    aliases = {}
    for k, j in enumerate(jobs):
        for a, b in j.aliases.items():
            aliases[sum(n_in[:k]) + a] = sum(n_out[:k]) + b

    def start(ins, outs, sems):
        return [j.start(i, o, s) for j, i, o, s in zip(jobs, cut(ins, n_in), cut(outs, n_out), cut(sems, n_sem))]

    def finish(sts, ins, outs, sems):
        for j, st, i, o, s in zip(jobs, sts, cut(ins, n_in), cut(outs, n_out), cut(sems, n_sem)):
            j.finish(st, i, o, s)

    return _Job([t for j in jobs for t in j.ins], [t for j in jobs for t in j.out_shapes], aliases,
                [t for j in jobs for t in j.sems], start, finish)


def _run_job(job, name):
    def body(ins, outs, scr, comm):
        comm[1](comm[0]())

    return _host_call(body, name, [], [], [], [], [], {}, job)[1]


def _pair_sum1(grad, got, core):
    _, _, rows, N = grad.shape

    def body(c_ref, g_ref, r_ref, o_ref):
        o_ref[...] = (g_ref[...] + r_ref[...]).astype(BF16)

    return pl.pallas_call(
        body, name="pair_sum",
        grid_spec=pltpu.PrefetchScalarGridSpec(
            num_scalar_prefetch=1, grid=(4,),
            in_specs=[pl.BlockSpec((None, None, rows, N), lambda k, c: (k, c[0], 0, 0)),
                      pl.BlockSpec((None, rows, N), lambda k, c: (k, 0, 0))],
            out_specs=pl.BlockSpec((None, rows, N), lambda k, c: (k, 0, 0))),
        out_shape=jax.ShapeDtypeStruct((4, rows, N), BF16),
        compiler_params=_cp(("parallel",)))(core, grad, got)


def _chip_sum1(parts):
    _, rows, N = parts.shape

    def body(p_ref, o_ref):
        acc = p_ref[0].astype(F32)
        for k in range(1, 4):
            acc = acc + p_ref[k].astype(F32)
        o_ref[...] = acc

    return pl.pallas_call(
        body, name="chip_sum", in_specs=[VMEM_SPEC], out_specs=VMEM_SPEC,
        out_shape=jax.ShapeDtypeStruct((rows, N), F32), compiler_params=_cp())(parts)


def _permute_in(w):
    lead = w.shape[:-1]
    return w.reshape(lead + (3, 3, 2, BQ)).swapaxes(-2, -3).reshape(lead + (QKVW,))


def _unpermute_in(w):
    lead = w.shape[:-1]
    return w.reshape(lead + (3, 2, 3, BQ)).swapaxes(-2, -3).reshape(lead + (QKVW,))


def _row(v):
    v = v.reshape(-1)
    return jnp.pad(v, (0, D - v.shape[0])).reshape(1, D)


def _step_without_overlap(x, w_in, f_bias, conv_w, w_out, rel_bias, ln1_g, ln1_b, w_gate, w_up, w_down, ln2_g, ln2_b, loss_target, m_w_in, m_f_bias, m_conv_w, m_w_out, m_rel_bias, m_ln1_g, m_ln1_b, m_w_gate, m_w_up, m_w_down, m_ln2_g, m_ln2_b, v_w_in, v_f_bias, v_conv_w, v_w_out, v_rel_bias, v_ln1_g, v_ln1_b, v_w_gate, v_w_up, v_w_down, v_ln2_g, v_ln2_b):
    xi, yi, ci = _place()
    me = 4 * xi + 2 * yi + ci

    win_s = jnp.concatenate([_permute_in(w_in[..., :QKVW]), w_in[..., QKVW:]], axis=-1)
    win_s = jnp.pad(win_s, ((0, 0), (0, 0), (0, NPAD - NPROJ))).astype(BF16)
    shards = [win_s, w_out.astype(BF16), jnp.swapaxes(w_gate, 1, 2).astype(BF16),
              jnp.swapaxes(w_up, 1, 2).astype(BF16), w_down.astype(BF16)]
    full = _allgather_weights(shards)
    Win, Wout, WgT, WuT, Wd = [f.reshape(2, NDEV * f.shape[2], f.shape[3]) for f in full]

    cw_rows = lax.dynamic_update_slice(jnp.zeros((2, 3, 256), F32), conv_w, (0, 0, me * 32))
    small = jnp.concatenate([_row(cw_rows[0]), _row(cw_rows[1]), jnp.zeros((SMALL_ROWS - 2, D), F32)], axis=0)
    small = _allreduce_small(small)
    cw_full = small[0:2, :CONVW].reshape(2, 3, 256)
    cw8 = jnp.pad(cw_full, ((0, 0), (0, 5), (0, 0)))
    fb = jnp.pad(f_bias, ((0, 0), (0, GATEW - NH))).reshape(2, 1, GATEW)
    tbl = _dil_table(rel_bias)

    def wcol(layer, K, tn, off):
        return pl.BlockSpec((None, K, tn), lambda i, j: (layer, 0, off + j))

    def arow(tm, K, blk=0):
        return pl.BlockSpec((tm, K), lambda i, j: (i, blk))

    h = x.reshape(T, D)
    hb = h.astype(BF16)
    saved = []
    for l in range(2):
        qkv = _mm([(hb, arow(512, D), Win, wcol(l, D, 768, 0))], nt=False, M=T, N=QKVW, tm=512, tn=768,
                  out_dtype=BF16, name="proj_qkv")
        conv = _mm([(hb, arow(512, D), Win, wcol(l, D, 768, 3))], nt=False, M=T, N=CONVW, tm=512, tn=768,
                   out_dtype=F32, name="proj_conv")
        gate = _mm([(hb, arow(512, D), Win, wcol(l, D, 128, 24))], nt=False, M=T, N=GATEW, tm=512, tn=128,
                   out_dtype=F32, name="proj_gate")
        cum = _fox_prep(gate, fb[l])
        cq = cum[:, :NH].reshape(BL, S, NH).transpose(1, 0, 2).reshape(S, NSTAT)
        ck = cq.T
        mixed, rtot = _sb_fwd(qkv)
        mixed, lse_d = _flash_fwd(qkv, mixed, 1, False, (tbl,))
        mixed, lse_f = _flash_fwd(qkv, mixed, 2, True, (cq, ck))
        mixed = _conv_fwd(conv, cw8[l], mixed)
        mix = _mm([(mixed, arow(512, D), Wout, wcol(l, D, 512, 0))], nt=False, M=T, N=D, tm=512, tn=512,
                  out_dtype=F32, name="out_proj")
        x1, xh1, r1, x1b = _ln_fwd(h, mix, ln1_g[l:l + 1], ln1_b[l:l + 1])
        g, u, a = _ffn_up(x1b, WgT, WuT, l)
        ffn = _mm([(a, arow(512, DFF), Wd, wcol(l, DFF, 512, 0))], nt=False, M=T, N=D, tm=512, tn=512,
                  out_dtype=F32, name="ffn_down")
        x2, xh2, r2, x2b = _ln_fwd(x1, ffn, ln2_g[l:l + 1], ln2_b[l:l + 1])
        saved.append(dict(h=hb, qkv=qkv, conv=conv, gate=gate, cq=cq, ck=ck, mixed=mixed, rtot=rtot, lse_d=lse_d,
                          lse_f=lse_f, x1=x1b, xh1=xh1, r1=r1, g=g, u=u, a=a, xh2=xh2, r2=r2))
        h, hb = x2, x2b

    sq, dy = _loss_grad(h, loss_target.reshape(T, D))
    loss = lax.psum(sq[0, 0], ("x", "y", "c")) * (0.5 / D)

    G_in = jnp.zeros((2, D, NPAD), F32)
    G_out = jnp.zeros((2, D, D), F32)
    G_g = jnp.zeros((2, DFF, D), F32)
    G_u = jnp.zeros((2, DFF, D), F32)
    G_d = jnp.zeros((2, DFF, D), F32)
    small_g = {}

    def wrow(layer, tn, K, blk=0):
        return pl.BlockSpec((None, tn, K), lambda i, j: (layer, j, blk))

    for l in (1, 0):
        sv = saved[l]
        ds2, dg2, db2, ds2b = _ln_bwd(dy, sv["xh2"], sv["r2"], ln2_g[l:l + 1])
        dgt, dut = _ffn_da(ds2b, Wd, sv["g"], sv["u"], l)
        G_d = _mm_tn(sv["a"], ds2b, G_d, Ka=DFF, N=D, tm=1408, tn=1024, tk=512, layer=l, ooff=0, name="grad_w_down")
        G_g = _mm_tn(dgt, sv["x1"], G_g, Ka=DFF, N=D, tm=1408, tn=1024, tk=512, layer=l, ooff=0, name="grad_w_gate")
        G_u = _mm_tn(dut, sv["x1"], G_u, Ka=DFF, N=D, tm=1408, tn=1024, tk=512, layer=l, ooff=0, name="grad_w_up")
        dx1 = _mm([(dgt, arow(512, DFF), WgT, wcol(l, DFF, 512, 0)), (dut, arow(512, DFF), WuT, wcol(l, DFF, 512, 0))],
                  nt=False, M=T, N=D, tm=512, tn=512, out_dtype=F32, name="ffn_dx", res=ds2, res_scale=ALPHA)
        ds1, dg1, db1, ds1b = _ln_bwd(dx1, sv["xh1"], sv["r1"], ln1_g[l:l + 1])
        G_out = _mm_tn(sv["mixed"], ds1b, G_out, Ka=D, N=D, tm=1024, tn=1024, tk=512, layer=l, ooff=0,
                       name="grad_w_out")
        dmixed = _mm([(ds1b, arow(512, D), Wout, wrow(l, 512, D))], nt=True, M=T, N=D, tm=512, tn=512,
                     out_dtype=BF16, name="out_proj_dx")
        dqkv = _sb_bwd(sv["qkv"], dmixed, sv["rtot"])
        dqkv, dtbl = _flash_bwd(sv["qkv"], sv["mixed"], dmixed, sv["lse_d"], dqkv, 1, False, (tbl,))
        dqkv, dck = _flash_bwd(sv["qkv"], sv["mixed"], dmixed, sv["lse_f"], dqkv, 2, True, (sv["cq"], sv["ck"]))
        dconv, dcw = _conv_bwd(sv["conv"], cw8[l], dmixed)
        dcum = jnp.pad(dck.reshape(BL, NH, S).transpose(0, 2, 1).reshape(T, NH), ((0, 0), (0, GATEW - NH)))
        dgate, dfb = _fox_post(dcum, sv["gate"], fb[l])
        drb = _dil_table_bwd(dtbl)
        G_in = _mm_tn(sv["h"], dqkv, G_in, Ka=D, N=QKVW, tm=1024, tn=768, tk=512, layer=l, ooff=0, name="grad_w_in_qkv")
        G_in = _mm_tn(sv["h"], dconv, G_in, Ka=D, N=CONVW, tm=1024, tn=768, tk=512, layer=l, ooff=3,
                      name="grad_w_in_conv")
        G_in = _mm_tn(sv["h"], dgate, G_in, Ka=D, N=GATEW, tm=1024, tn=128, tk=512, layer=l, ooff=24,
                      name="grad_w_in_gate")
        dy = _mm([(dqkv, arow(512, QKVW), Win, wrow(l, 512, QKVW, 0)),
                  (dconv, arow(512, CONVW), Win, wrow(l, 512, CONVW, 3)),
                  (dgate, arow(512, GATEW), Win, wrow(l, 512, GATEW, 24))],
                 nt=True, M=T, N=D, tm=512, tn=512, out_dtype=F32, name="proj_dx", res=ds1, res_scale=ALPHA)
        small_g[l] = dict(ln1_g=dg1, ln1_b=db1, ln2_g=dg2, ln2_b=db2, cw=dcw[0:3].reshape(1, CONVW),
                          fb=dfb[:, :NH], rb=drb[:, :NH])
    grad_x = dy.reshape(BL, S, D)

    rows = []
    for name in ("ln1_g", "ln1_b", "ln2_g", "ln2_b"):
        rows += [small_g[0][name], small_g[1][name]]
    rows += [_row(small_g[0]["cw"]), _row(small_g[1]["cw"]),
             _row(jnp.concatenate([small_g[0]["fb"], small_g[1]["fb"]], axis=0)),
             _row(small_g[0]["rb"] + small_g[1]["rb"])]
    rows.append(jnp.zeros((SMALL_ROWS - len(rows), D), F32))
    sg = _allreduce_small(jnp.concatenate(rows, axis=0))
    g_ln1_g, g_ln1_b, g_ln2_g, g_ln2_b = sg[0:2], sg[2:4], sg[4:6], sg[6:8]
    g_conv_full = sg[8:10, :CONVW].reshape(2, 3, 256)
    g_conv = lax.dynamic_slice(g_conv_full, (0, 0, me * 32), (2, 3, 32))
    g_fb = sg[10, :2 * NH].reshape(2, NH)
    g_rb = sg[11, :32 * NH].reshape(32, NH)

    bufs = [G_in, G_out, G_g, G_u, G_d]
    views = [b.reshape(2, 4, 2, b.shape[1] // NDEV, b.shape[2]) for b in bufs]
    got = _sibling_exchange(views)
    core = jnp.reshape(ci, (1,)).astype(jnp.int32)
    psums = [_pair_sum(vw, gt, core) for vw, gt in zip(views, got)]
    parts = _chip_exchange(psums)
    gs = [_chip_sum(p) for p in parts]
    g_in = gs[0]
    g_w_in = jnp.concatenate([_unpermute_in(g_in[..., :QKVW]), g_in[..., QKVW:NPROJ]], axis=-1)
    g_w_out = gs[1]
    g_w_gate = jnp.swapaxes(gs[2], 1, 2)
    g_w_up = jnp.swapaxes(gs[3], 1, 2)
    g_w_down = gs[4]

    def big(w, g, m, v, tr):
        sh = w.shape
        f = lambda t: t.reshape(-1, sh[-1])
        return [t.reshape(sh) for t in _adamw(f(w), f(g), f(m), f(v), tr)]

    up_in = big(w_in, g_w_in, m_w_in, v_w_in, 64)
    up_out = big(w_out, g_w_out, m_w_out, v_w_out, 128)
    up_gate = big(w_gate, g_w_gate, m_w_gate, v_w_gate, 256)
    up_up = big(w_up, g_w_up, m_w_up, v_w_up, 256)
    up_down = big(w_down, g_w_down, m_w_down, v_w_down, 352)

    def pack(fbv, cwv, rbv, l1g, l1b, l2g, l2b):
        r = [l1g, l1b, l2g, l2b, _row(cwv), _row(fbv), _row(rbv)]
        r.append(jnp.zeros((SMALL_ROWS - 11, D), F32))
        return jnp.concatenate(r, axis=0)

    pw = pack(f_bias, conv_w, rel_bias, ln1_g, ln1_b, ln2_g, ln2_b)
    pg = pack(g_fb, g_conv, g_rb, g_ln1_g, g_ln1_b, g_ln2_g, g_ln2_b)
    pm = pack(m_f_bias, m_conv_w, m_rel_bias, m_ln1_g, m_ln1_b, m_ln2_g, m_ln2_b)
    pv = pack(v_f_bias, v_conv_w, v_rel_bias, v_ln1_g, v_ln1_b, v_ln2_g, v_ln2_b)
    ups = _adamw(pw, pg, pm, pv, SMALL_ROWS)

    def unpack(p):
        return dict(ln1_g=p[0:2], ln1_b=p[2:4], ln2_g=p[4:6], ln2_b=p[6:8],
                    conv_w=p[8, :192].reshape(2, 3, 32), f_bias=p[9, :2 * NH].reshape(2, NH),
                    rel_bias=p[10, :32 * NH].reshape(32, NH))

    sm = [unpack(p) for p in ups]

    def group(k):
        return (up_in[k], sm[k]["f_bias"], sm[k]["conv_w"], up_out[k], sm[k]["rel_bias"], sm[k]["ln1_g"],
                sm[k]["ln1_b"], up_gate[k], up_up[k], up_down[k], sm[k]["ln2_g"], sm[k]["ln2_b"])

    grads = (g_w_in, g_fb, g_conv, g_w_out, g_rb, g_ln1_g, g_ln1_b, g_w_gate, g_w_up, g_w_down, g_ln2_g, g_ln2_b)
    return (loss, grad_x) + grads + group(0) + group(1) + group(2)


def _pair_sums(views, gots, core):
    n = len(views)

    def body(c_ref, *refs):
        for a in range(n):
            refs[2 * n + a][...] = (refs[a][...] + refs[n + a][...]).astype(BF16)

    def vspec(v):
        return pl.BlockSpec((None, None) + v.shape[2:], lambda k, c: (k, c[0], 0, 0))

    def gspec(g):
        return pl.BlockSpec((None,) + g.shape[1:], lambda k, c: (k, 0, 0))

    return pl.pallas_call(
        body, name="pair_sums",
        grid_spec=pltpu.PrefetchScalarGridSpec(
            num_scalar_prefetch=1, grid=(4,),
            in_specs=[vspec(v) for v in views] + [gspec(g) for g in gots],
            out_specs=[gspec(g) for g in gots]),
        out_shape=[jax.ShapeDtypeStruct(g.shape, BF16) for g in gots],
        compiler_params=_cp(("parallel",)))(core, *views, *gots)


def _chip_sums(parts):
    n = len(parts)

    def body(*refs):
        for a in range(n):
            acc = refs[a][0].astype(F32)
            for k in range(1, 4):
                acc = acc + refs[a][k].astype(F32)
            refs[n + a][...] = acc

    return pl.pallas_call(
        body, name="chip_sums", in_specs=[VMEM_SPEC] * n, out_specs=[VMEM_SPEC] * n,
        out_shape=[jax.ShapeDtypeStruct(p.shape[1:], F32) for p in parts], compiler_params=_cp())(*parts)


def kernel(x, w_in, f_bias, conv_w, w_out, rel_bias, ln1_g, ln1_b, w_gate, w_up, w_down, ln2_g, ln2_b, loss_target, m_w_in, m_f_bias, m_conv_w, m_w_out, m_rel_bias, m_ln1_g, m_ln1_b, m_w_gate, m_w_up, m_w_down, m_ln2_g, m_ln2_b, v_w_in, v_f_bias, v_conv_w, v_w_out, v_rel_bias, v_ln1_g, v_ln1_b, v_w_gate, v_w_up, v_w_down, v_ln2_g, v_ln2_b):
    xi, yi, ci = _place()
    me = 4 * xi + 2 * yi + ci
    core = jnp.reshape(ci, (1,)).astype(jnp.int32)

    win_s = jnp.concatenate([_permute_in(w_in[..., :QKVW]), w_in[..., QKVW:]], axis=-1)
    win_s = jnp.pad(win_s, ((0, 0), (0, 0), (0, NPAD - NPROJ))).astype(BF16)
    per_layer = [win_s, w_out.astype(BF16), jnp.swapaxes(w_gate, 1, 2).astype(BF16),
                 jnp.swapaxes(w_up, 1, 2).astype(BF16), w_down.astype(BF16)]

    def shards(l):
        return [s[l] for s in per_layer]

    def whole(g):
        return g.reshape(NDEV * g.shape[1], g.shape[2])

    sh = [shards(0), shards(1)]
    first = _run_job(_gather_job_b(_run_job(_gather_job_a(sh[0][:1]), "gather_a")), "gather_b")
    W = [{"win": whole(first[0])}, {}]

    cw_rows = lax.dynamic_update_slice(jnp.zeros((2, 3, 256), F32), conv_w, (0, 0, me * 32))
    small = jnp.concatenate([_row(cw_rows[0]), _row(cw_rows[1]), jnp.zeros((SMALL_ROWS - 2, D), F32)], axis=0)
    small = _allreduce_small(small)
    cw_full = small[0:2, :CONVW].reshape(2, 3, 256)
    cw8 = jnp.pad(cw_full, ((0, 0), (0, 5), (0, 0)))
    fb = jnp.pad(f_bias, ((0, 0), (0, GATEW - NH))).reshape(2, 1, GATEW)
    tbl = _dil_table(rel_bias)

    def wcol(K, tn, off):
        return pl.BlockSpec((K, tn), lambda i, j: (0, off + j))

    def wrow(tn, K, blk=0):
        return pl.BlockSpec((tn, K), lambda i, j: (j, blk))

    def arow(tm, K, blk=0):
        return pl.BlockSpec((tm, K), lambda i, j: (i, blk))

    h = x.reshape(T, D)
    hb = h.astype(BF16)
    saved = []
    for l in range(2):
        Win = W[l]["win"]
        qkv = _mm([(hb, arow(512, D), Win, wcol(D, 768, 0))], nt=False, M=T, N=QKVW, tm=512, tn=768,
                  out_dtype=BF16, name="proj_qkv")
        conv = _mm([(hb, arow(512, D), Win, wcol(D, 768, 3))], nt=False, M=T, N=CONVW, tm=512, tn=768,
                   out_dtype=F32, name="proj_conv")
        gate = _mm([(hb, arow(512, D), Win, wcol(D, 128, 24))], nt=False, M=T, N=GATEW, tm=512, tn=128,
                   out_dtype=F32, name="proj_gate")
        cum = _fox_prep(gate, fb[l])
        cq = cum[:, :NH].reshape(BL, S, NH).transpose(1, 0, 2).reshape(S, NSTAT)
        ck = cq.T
        if l == 0:
            mixed, rtot, a0 = _sb_fwd(qkv, job=_gather_job_a(sh[0][1:]))
            mixed, lse_d, ex = _flash_fwd(qkv, mixed, 1, False, (tbl,),
                                          job=_join_jobs(_gather_job_b(list(a0)), _gather_job_a(sh[1][:2])))
            W[0].update(zip(("wout", "wgT", "wuT", "wd"), [whole(t) for t in ex[:4]]))
            mixed, lse_f, ex = _flash_fwd(qkv, mixed, 2, True, (cq, ck),
                                          job=_join_jobs(_gather_job_b(list(ex[4:])), _gather_job_a(sh[1][2:])))
            W[1].update(zip(("win", "wout"), [whole(t) for t in ex[:2]]))
            a2 = list(ex[2:])
        else:
            mixed, rtot, ex = _sb_fwd(qkv, job=_gather_job_b(a2))
            W[1].update(zip(("wgT", "wuT", "wd"), [whole(t) for t in ex]))
            mixed, lse_d, _ = _flash_fwd(qkv, mixed, 1, False, (tbl,))
            mixed, lse_f, _ = _flash_fwd(qkv, mixed, 2, True, (cq, ck))
        Wout, WgT, WuT, Wd = W[l]["wout"], W[l]["wgT"], W[l]["wuT"], W[l]["wd"]
        mixed = _conv_fwd(conv, cw8[l], mixed)
        mix = _mm([(mixed, arow(512, D), Wout, wcol(D, 512, 0))], nt=False, M=T, N=D, tm=512, tn=512,
                  out_dtype=F32, name="out_proj")
        x1, xh1, r1, x1b = _ln_fwd(h, mix, ln1_g[l:l + 1], ln1_b[l:l + 1])
        g, u, a = _ffn_up(x1b, WgT[None], WuT[None], 0)
        ffn = _mm([(a, arow(512, DFF), Wd, wcol(DFF, 512, 0))], nt=False, M=T, N=D, tm=512, tn=512,
                  out_dtype=F32, name="ffn_down")
        x2, xh2, r2, x2b = _ln_fwd(x1, ffn, ln2_g[l:l + 1], ln2_b[l:l + 1])
        saved.append(dict(h=hb, qkv=qkv, conv=conv, gate=gate, cq=cq, ck=ck, mixed=mixed, rtot=rtot, lse_d=lse_d,
                          lse_f=lse_f, x1=x1b, xh1=xh1, r1=r1, g=g, u=u, a=a, xh2=xh2, r2=r2))
        h, hb = x2, x2b

    sq, dy = _loss_grad(h, loss_target.reshape(T, D))
    loss = lax.psum(sq[0, 0], ("x", "y", "c")) * (0.5 / D)

    def view(gr):
        return gr.reshape(4, 2, gr.shape[0] // NDEV, gr.shape[1])

    def reduce_tail(views, gots):
        return _chip_sums(_run_job(_chip_job(_pair_sums(views, gots, core)), "chip_exchange"))

    G = [None, None]
    small_g = {}
    shard_g = {}
    for l in (1, 0):
        sv = saved[l]
        Win, Wout, WgT, WuT, Wd = W[l]["win"], W[l]["wout"], W[l]["wgT"], W[l]["wuT"], W[l]["wd"]
        ds2, dg2, db2, ds2b = _ln_bwd(dy, sv["xh2"], sv["r2"], ln2_g[l:l + 1])
        dgt, dut = _ffn_da(ds2b, Wd[None], sv["g"], sv["u"], 0)
        G_d = _mm_tn(sv["a"], ds2b, None, C=D, Ka=DFF, N=D, tm=1408, tn=1024, tk=512, ooff=0, name="grad_w_down")
        G_g = _mm_tn(dgt, sv["x1"], None, C=D, Ka=DFF, N=D, tm=1408, tn=1024, tk=512, ooff=0, name="grad_w_gate")
        G_u = _mm_tn(dut, sv["x1"], None, C=D, Ka=DFF, N=D, tm=1408, tn=1024, tk=512, ooff=0, name="grad_w_up")
        dx1 = _mm([(dgt, arow(512, DFF), WgT, wcol(DFF, 512, 0)), (dut, arow(512, DFF), WuT, wcol(DFF, 512, 0))],
                  nt=False, M=T, N=D, tm=512, tn=512, out_dtype=F32, name="ffn_dx", res=ds2, res_scale=ALPHA)
        ds1, dg1, db1, ds1b = _ln_bwd(dx1, sv["xh1"], sv["r1"], ln1_g[l:l + 1])
        G_out = _mm_tn(sv["mixed"], ds1b, None, C=D, Ka=D, N=D, tm=1024, tn=1024, tk=512, ooff=0, name="grad_w_out")
        dmixed = _mm([(ds1b, arow(512, D), Wout, wrow(512, D))], nt=True, M=T, N=D, tm=512, tn=512,
                     out_dtype=BF16, name="out_proj_dx")
        if l == 0:
            early = [view(t) for t in (G[1]["in"], G[1]["out"], G[1]["g"], G[1]["u"], G[1]["d"], G_g, G_u, G_d, G_out)]
            dqkv, gots = _sb_bwd(sv["qkv"], dmixed, sv["rtot"], job=_sibling_job(early))
            ps1 = _pair_sums(early[:5], list(gots[:5]), core)
            ps0 = _pair_sums(early[5:], list(gots[5:]), core)
            dqkv, dtbl, parts0 = _flash_bwd(sv["qkv"], sv["mixed"], dmixed, sv["lse_d"], dqkv, 1, False, (tbl,),
                                            job=_chip_job(ps0))
            dqkv, dck, parts1 = _flash_bwd(sv["qkv"], sv["mixed"], dmixed, sv["lse_f"], dqkv, 2, True,
                                           (sv["cq"], sv["ck"]), job=_chip_job(ps1))
            s1 = _chip_sums(list(parts1))
            s0 = _chip_sums(list(parts0))
            shard_g[1] = dict(zip(("in", "out", "g", "u", "d"), s1))
            shard_g[0] = dict(zip(("g", "u", "d", "out"), s0))
        else:
            dqkv, _ = _sb_bwd(sv["qkv"], dmixed, sv["rtot"])
            dqkv, dtbl, _ = _flash_bwd(sv["qkv"], sv["mixed"], dmixed, sv["lse_d"], dqkv, 1, False, (tbl,))
            dqkv, dck, _ = _flash_bwd(sv["qkv"], sv["mixed"], dmixed, sv["lse_f"], dqkv, 2, True, (sv["cq"], sv["ck"]))
        dconv, dcw = _conv_bwd(sv["conv"], cw8[l], dmixed)
        dcum = jnp.pad(dck.reshape(BL, NH, S).transpose(0, 2, 1).reshape(T, NH), ((0, 0), (0, GATEW - NH)))
        dgate, dfb = _fox_post(dcum, sv["gate"], fb[l])
        drb = _dil_table_bwd(dtbl)
        G_in = _mm_tn(sv["h"], dqkv, None, C=NPAD, Ka=D, N=QKVW, tm=1024, tn=768, tk=512, ooff=0, name="grad_w_in_qkv")
        G_in = _mm_tn(sv["h"], dconv, G_in, C=NPAD, Ka=D, N=CONVW, tm=1024, tn=768, tk=512, ooff=3,
                      name="grad_w_in_conv")
        G_in = _mm_tn(sv["h"], dgate, G_in, C=NPAD, Ka=D, N=GATEW, tm=1024, tn=128, tk=512, ooff=24,
                      name="grad_w_in_gate")
        G[l] = {"in": G_in, "out": G_out, "g": G_g, "u": G_u, "d": G_d}
        dy = _mm([(dqkv, arow(512, QKVW), Win, wrow(512, QKVW, 0)),
                  (dconv, arow(512, CONVW), Win, wrow(512, CONVW, 3)),
                  (dgate, arow(512, GATEW), Win, wrow(512, GATEW, 24))],
                 nt=True, M=T, N=D, tm=512, tn=512, out_dtype=F32, name="proj_dx", res=ds1, res_scale=ALPHA)
        small_g[l] = dict(ln1_g=dg1, ln1_b=db1, ln2_g=dg2, ln2_b=db2, cw=dcw[0:3].reshape(1, CONVW),
                          fb=dfb[:, :NH], rb=drb[:, :NH])
    grad_x = dy.reshape(BL, S, D)

    late = [view(G[0]["in"])]
    shard_g[0]["in"] = reduce_tail(late, list(_run_job(_sibling_job(late), "sibling_exchange")))[0]

    rows = []
    for name in ("ln1_g", "ln1_b", "ln2_g", "ln2_b"):
        rows += [small_g[0][name], small_g[1][name]]
    rows += [_row(small_g[0]["cw"]), _row(small_g[1]["cw"]),
             _row(jnp.concatenate([small_g[0]["fb"], small_g[1]["fb"]], axis=0)),
             _row(small_g[0]["rb"] + small_g[1]["rb"])]
    rows.append(jnp.zeros((SMALL_ROWS - len(rows), D), F32))
    sg = _allreduce_small(jnp.concatenate(rows, axis=0))
    g_ln1_g, g_ln1_b, g_ln2_g, g_ln2_b = sg[0:2], sg[2:4], sg[4:6], sg[6:8]
    g_conv_full = sg[8:10, :CONVW].reshape(2, 3, 256)
    g_conv = lax.dynamic_slice(g_conv_full, (0, 0, me * 32), (2, 3, 32))
    g_fb = sg[10, :2 * NH].reshape(2, NH)
    g_rb = sg[11, :32 * NH].reshape(32, NH)

    def both(name):
        return jnp.stack([shard_g[0][name], shard_g[1][name]])

    g_in = both("in")
    g_w_in = jnp.concatenate([_unpermute_in(g_in[..., :QKVW]), g_in[..., QKVW:NPROJ]], axis=-1)
    g_w_out = both("out")
    g_w_gate = jnp.swapaxes(both("g"), 1, 2)
    g_w_up = jnp.swapaxes(both("u"), 1, 2)
    g_w_down = both("d")

    def big(w, g, m, v, tr):
        sh = w.shape
        f = lambda t: t.reshape(-1, sh[-1])
        return [t.reshape(sh) for t in _adamw(f(w), f(g), f(m), f(v), tr)]

    up_in = big(w_in, g_w_in, m_w_in, v_w_in, 64)
    up_out = big(w_out, g_w_out, m_w_out, v_w_out, 128)
    up_gate = big(w_gate, g_w_gate, m_w_gate, v_w_gate, 256)
    up_up = big(w_up, g_w_up, m_w_up, v_w_up, 256)
    up_down = big(w_down, g_w_down, m_w_down, v_w_down, 352)

    def pack(fbv, cwv, rbv, l1g, l1b, l2g, l2b):
        r = [l1g, l1b, l2g, l2b, _row(cwv), _row(fbv), _row(rbv)]
        r.append(jnp.zeros((SMALL_ROWS - 11, D), F32))
        return jnp.concatenate(r, axis=0)

    pw = pack(f_bias, conv_w, rel_bias, ln1_g, ln1_b, ln2_g, ln2_b)
    pg = pack(g_fb, g_conv, g_rb, g_ln1_g, g_ln1_b, g_ln2_g, g_ln2_b)
    pm = pack(m_f_bias, m_conv_w, m_rel_bias, m_ln1_g, m_ln1_b, m_ln2_g, m_ln2_b)
    pv = pack(v_f_bias, v_conv_w, v_rel_bias, v_ln1_g, v_ln1_b, v_ln2_g, v_ln2_b)
    ups = _adamw(pw, pg, pm, pv, SMALL_ROWS)

    def unpack(p):
        return dict(ln1_g=p[0:2], ln1_b=p[2:4], ln2_g=p[4:6], ln2_b=p[6:8],
                    conv_w=p[8, :192].reshape(2, 3, 32), f_bias=p[9, :2 * NH].reshape(2, NH),
                    rel_bias=p[10, :32 * NH].reshape(32, NH))

    sm = [unpack(p) for p in ups]

    def group(k):
        return (up_in[k], sm[k]["f_bias"], sm[k]["conv_w"], up_out[k], sm[k]["rel_bias"], sm[k]["ln1_g"],
                sm[k]["ln1_b"], up_gate[k], up_up[k], up_down[k], sm[k]["ln2_g"], sm[k]["ln2_b"])

    grads = (g_w_in, g_fb, g_conv, g_w_out, g_rb, g_ln1_g, g_ln1_b, g_w_gate, g_w_up, g_w_down, g_ln2_g, g_ln2_b)
    return (loss, grad_x) + grads + group(0) + group(1) + group(2)
```

```python
import math

import numpy as np
import jax
import jax.numpy as jnp
from jax import lax
from jax.experimental import pallas as pl
from jax.experimental.pallas import tpu as pltpu

F32 = jnp.float32
BF16 = jnp.bfloat16
MESH = pl.DeviceIdType.MESH

D = 1024
S = 2048
BL = 2
T = BL * S
NH = 4
DFF = 2816
NPROJ = 3076
NPAD = 3200
QKVW = 2304
CONVW = 768
GATEW = 128
PAIRW = 384
BQ = 128
HB = 2 * BQ
NB = S // BQ
NDEV = 8
NSTAT = BL * NH
ALPHA = 4.0 ** 0.25
SCALE = 0.125
NEG = -1e30
LN_EPS = 1e-5
ADAM_LR, ADAM_B1, ADAM_B2, ADAM_EPS, ADAM_WD, ADAM_STEP = 0.001, 0.9, 0.999, 1e-08, 0.01, 10
VMEM_LIMIT = 48 * 1024 * 1024
SMALL_ROWS = 16


def _bucket_thresholds():
    d = np.arange(0, S)
    nf = np.maximum(d, 1).astype(np.float32)
    large = 16 + (np.log(nf / np.float32(16)) / np.float32(math.log(128)) * np.float32(16)).astype(np.int32)
    b = np.where(d < 16, d, np.minimum(large, 31))
    return [int(np.argmax(b >= k)) for k in range(32)]


BUCKET_TH = _bucket_thresholds()


def _cp(sem=None):
    return pltpu.CompilerParams(dimension_semantics=sem, vmem_limit_bytes=VMEM_LIMIT)


def _dot(a, b):
    return lax.dot_general(a, b, (((1,), (0,)), ((), ())), preferred_element_type=F32)


def _dot_nt(a, b):
    return lax.dot_general(a, b, (((1,), (1,)), ((), ())), preferred_element_type=F32)


def _dot_tn(a, b):
    return lax.dot_general(a, b, (((0,), (0,)), ((), ())), preferred_element_type=F32)


def _split2(x):
    hi = x.astype(BF16)
    mid = (x - hi.astype(F32)).astype(BF16)
    return jnp.concatenate([hi, mid], axis=1)


def _split3(x):
    hi = x.astype(BF16)
    r = x - hi.astype(F32)
    mid = r.astype(BF16)
    lo = (r - mid.astype(F32)).astype(BF16)
    return jnp.concatenate([hi, mid, lo], axis=1)


def _log_sigmoid(u):
    return jnp.minimum(u, 0.0) - jnp.log1p(jnp.exp(-jnp.abs(u)))


def _log_sigmoid_tile(u):
    return jnp.minimum(u, 0.0) - jnp.log(1.0 + jnp.exp(jnp.minimum(u, -u)))


def _iota(shape, dim):
    return lax.broadcasted_iota(jnp.int32, shape, dim)


ANY_SPEC = pl.BlockSpec(memory_space=pl.ANY)
VMEM_SPEC = pl.BlockSpec(memory_space=pltpu.VMEM)


def _mm(pairs, *, nt, M, N, tm, tn, out_dtype, name, res=None, res_scale=1.0):
    n = len(pairs)

    def body(*refs):
        acc = None
        for p in range(n):
            a = refs[2 * p][...].astype(BF16)
            b = refs[2 * p + 1][...]
            d = _dot_nt(a, b) if nt else _dot(a, b)
            acc = d if acc is None else acc + d
        if res is not None:
            acc = acc + res_scale * refs[2 * n][...]
        refs[-1][...] = acc.astype(out_dtype)

    ops, specs = [], []
    for a, asp, b, bsp in pairs:
        ops += [a, b]
        specs += [asp, bsp]
    if res is not None:
        ops.append(res)
        specs.append(pl.BlockSpec((tm, tn), lambda i, j: (i, j)))
    return pl.pallas_call(
        body, name=name, grid=(M // tm, N // tn), in_specs=specs,
        out_specs=pl.BlockSpec((tm, tn), lambda i, j: (i, j)),
        out_shape=jax.ShapeDtypeStruct((M, N), out_dtype),
        compiler_params=_cp(("parallel", "parallel")))(*ops)


def _mm_tn(a, b, gbuf, *, C, Ka, N, tm, tn, tk, ooff, name):
    def body(*refs):
        a_ref, b_ref, o_ref = refs[0], refs[1], refs[-1]
        k = pl.program_id(2)
        d = _dot_tn(a_ref[...].astype(BF16), b_ref[...].astype(BF16))

        @pl.when(k == 0)
        def _():
            o_ref[...] = d

        @pl.when(k > 0)
        def _():
            o_ref[...] += d

    ops = [a, b] + ([] if gbuf is None else [gbuf])
    return pl.pallas_call(
        body, name=name, grid=(Ka // tm, N // tn, T // tk),
        in_specs=[pl.BlockSpec((tk, tm), lambda i, j, k: (k, i)),
                  pl.BlockSpec((tk, tn), lambda i, j, k: (k, j))] + ([] if gbuf is None else [ANY_SPEC]),
        out_specs=pl.BlockSpec((tm, tn), lambda i, j, k: (i, ooff + j)),
        out_shape=jax.ShapeDtypeStruct((Ka, C), F32),
        input_output_aliases={} if gbuf is None else {2: 0},
        compiler_params=_cp(("parallel", "parallel", "arbitrary")))(*ops)


def _ffn_up(x1, wgt, wut, layer):
    tm, tn = 1024, 256

    def body(x_ref, wg_ref, wu_ref, g_ref, u_ref, a_ref):
        xb = x_ref[...]
        g = _dot_nt(xb, wg_ref[...])
        u = _dot_nt(xb, wu_ref[...])
        g_ref[...] = g.astype(BF16)
        u_ref[...] = u.astype(BF16)
        a_ref[...] = (g * jax.nn.sigmoid(g) * u).astype(BF16)

    wspec = pl.BlockSpec((None, tn, D), lambda i, j: (layer, j, 0))
    ospec = pl.BlockSpec((tm, tn), lambda i, j: (i, j))
    return pl.pallas_call(
        body, name="ffn_up", grid=(T // tm, DFF // tn),
        in_specs=[pl.BlockSpec((tm, D), lambda i, j: (i, 0)), wspec, wspec],
        out_specs=[ospec, ospec, ospec],
        out_shape=[jax.ShapeDtypeStruct((T, DFF), BF16)] * 3,
        compiler_params=_cp(("parallel", "parallel")))(x1, wgt, wut)


def _ffn_da(dffn, wd, g, u, layer):
    tm, tn = 1024, 256

    def body(d_ref, wd_ref, g_ref, u_ref, dg_ref, du_ref):
        da = _dot_nt(d_ref[...], wd_ref[...])
        gv = g_ref[...].astype(F32)
        sg = jax.nn.sigmoid(gv)
        dg_ref[...] = (da * u_ref[...].astype(F32) * (sg * (1.0 + gv * (1.0 - sg)))).astype(BF16)
        du_ref[...] = (da * (gv * sg)).astype(BF16)

    ospec = pl.BlockSpec((tm, tn), lambda i, j: (i, j))
    return pl.pallas_call(
        body, name="ffn_da", grid=(T // tm, DFF // tn),
        in_specs=[pl.BlockSpec((tm, D), lambda i, j: (i, 0)),
                  pl.BlockSpec((None, tn, D), lambda i, j: (layer, j, 0)), ospec, ospec],
        out_specs=[ospec, ospec],
        out_shape=[jax.ShapeDtypeStruct((T, DFF), BF16), jax.ShapeDtypeStruct((T, DFF), BF16)],
        compiler_params=_cp(("parallel", "parallel")))(dffn, wd, g, u)


def _ln_fwd(x, f, gam, bet):
    tm = 256

    def body(x_ref, f_ref, g_ref, b_ref, y_ref, xh_ref, r_ref, yb_ref):
        s = ALPHA * x_ref[...] + f_ref[...]
        mu = jnp.mean(s, axis=-1, keepdims=True)
        xc = s - mu
        var = jnp.mean(xc * xc, axis=-1, keepdims=True)
        r = lax.rsqrt(var + LN_EPS)
        xh = xc * r
        xh_ref[...] = xh
        r_ref[...] = r
        y = xh * g_ref[...] + b_ref[...]
        y_ref[...] = y
        yb_ref[...] = y.astype(BF16)

    row = pl.BlockSpec((tm, D), lambda i: (i, 0))
    vec = pl.BlockSpec((1, D), lambda i: (0, 0))
    return pl.pallas_call(
        body, name="ln_fwd", grid=(T // tm,), in_specs=[row, row, vec, vec],
        out_specs=[row, row, pl.BlockSpec((tm, 1), lambda i: (i, 0)), row],
        out_shape=[jax.ShapeDtypeStruct((T, D), F32), jax.ShapeDtypeStruct((T, D), F32),
                   jax.ShapeDtypeStruct((T, 1), F32), jax.ShapeDtypeStruct((T, D), BF16)],
        compiler_params=_cp(("parallel",)))(x, f, gam, bet)


def _ln_bwd(dy, xh, r, gam):
    tm = 256

    def body(dy_ref, xh_ref, r_ref, g_ref, ds_ref, dg_ref, db_ref, dsb_ref):
        i = pl.program_id(0)
        dyv = dy_ref[...]
        xhv = xh_ref[...]
        dxh = dyv * g_ref[...]
        m1 = jnp.mean(dxh, axis=-1, keepdims=True)
        m2 = jnp.mean(dxh * xhv, axis=-1, keepdims=True)
        ds = r_ref[...] * (dxh - m1 - xhv * m2)
        ds_ref[...] = ds
        dsb_ref[...] = ds.astype(BF16)
        pg = jnp.sum(dyv * xhv, axis=0, keepdims=True)
        pb = jnp.sum(dyv, axis=0, keepdims=True)

        @pl.when(i == 0)
        def _():
            dg_ref[...] = pg
            db_ref[...] = pb

        @pl.when(i > 0)
        def _():
            dg_ref[...] += pg
            db_ref[...] += pb

    row = pl.BlockSpec((tm, D), lambda i: (i, 0))
    vec = pl.BlockSpec((1, D), lambda i: (0, 0))
    return pl.pallas_call(
        body, name="ln_bwd", grid=(T // tm,),
        in_specs=[row, row, pl.BlockSpec((tm, 1), lambda i: (i, 0)), vec],
        out_specs=[row, vec, vec, row],
        out_shape=[jax.ShapeDtypeStruct((T, D), F32), jax.ShapeDtypeStruct((1, D), F32),
                   jax.ShapeDtypeStruct((1, D), F32), jax.ShapeDtypeStruct((T, D), BF16)],
        compiler_params=_cp(("arbitrary",)))(dy, xh, r, gam)


def _loss_grad(y, tgt):
    tm = 256

    def body(y_ref, t_ref, l_ref, dy_ref):
        i = pl.program_id(0)
        e = y_ref[...] - t_ref[...]
        dy_ref[...] = e * (1.0 / D)
        p = jnp.sum(jnp.sum(e * e, axis=1, keepdims=True), axis=0, keepdims=True)

        @pl.when(i == 0)
        def _():
            l_ref[...] = p

        @pl.when(i > 0)
        def _():
            l_ref[...] += p

    row = pl.BlockSpec((tm, D), lambda i: (i, 0))
    return pl.pallas_call(
        body, name="loss_grad", grid=(T // tm,), in_specs=[row, row],
        out_specs=[pl.BlockSpec((1, 1), lambda i: (0, 0)), row],
        out_shape=[jax.ShapeDtypeStruct((1, 1), F32), jax.ShapeDtypeStruct((T, D), F32)],
        compiler_params=_cp(("arbitrary",)))(y, tgt)


def _adamw(w, g, m, v, tr):
    R, C = w.shape

    def body(w_ref, g_ref, m_ref, v_ref, d_ref, m2_ref, v2_ref):
        gv = g_ref[...]
        m2 = ADAM_B1 * m_ref[...] + (1.0 - ADAM_B1) * gv
        v2 = ADAM_B2 * v_ref[...] + (1.0 - ADAM_B2) * (gv * gv)
        m_hat = m2 / (1.0 - ADAM_B1 ** ADAM_STEP)
        v_hat = v2 / (1.0 - ADAM_B2 ** ADAM_STEP)
        d_ref[...] = -ADAM_LR * (m_hat / (jnp.sqrt(v_hat) + ADAM_EPS) + ADAM_WD * w_ref[...])
        m2_ref[...] = m2
        v2_ref[...] = v2

    blk = pl.BlockSpec((tr, C), lambda i: (i, 0))
    sh = jax.ShapeDtypeStruct((R, C), F32)
    return pl.pallas_call(
        body, name="adamw", grid=(R // tr,), in_specs=[blk] * 4, out_specs=[blk] * 3,
        out_shape=[sh, sh, sh], compiler_params=_cp(("parallel",)))(w, g, m, v)


CHAINS = [(p, b) for p in range(2) for b in range(BL)]
NC = len(CHAINS)


def _lane_masks():
    lane = _iota((1, BQ), 1)
    m0 = (lane < 64).astype(BF16)
    return m0, 1.0 - m0


def _stack(x, m0, m1):
    return jnp.concatenate([x * m0, x * m1], axis=0)


def _lanes(a):
    return jnp.concatenate([a[:BQ], a[BQ:]], axis=1)


def _per_lane(v):
    return jnp.where(_iota((BQ, BQ), 1) < 64, v[:BQ], v[BQ:])


def _diag_valid(strict):
    r = _iota((HB, BQ), 0) & (BQ - 1)
    c = _iota((HB, BQ), 1)
    return (c < r) if strict else (c <= r)


def _rows(b, i):
    return pl.ds(pl.multiple_of(b * S + i * BQ, BQ), BQ)


def _load_q(qkv_v, p, b, i, m0, m1):
    return _stack(qkv_v[_rows(b, i), p * PAIRW:p * PAIRW + BQ] * SCALE, m0, m1)


def _load_kv(qkv_v, p, b, j):
    r = _rows(b, j)
    return qkv_v[r, p * PAIRW + BQ:p * PAIRW + 2 * BQ], qkv_v[r, p * PAIRW + 2 * BQ:p * PAIRW + 3 * BQ]


def _stat_cols(tile8, p, b):
    lane = _iota((BQ, NSTAT), 1)
    c = b * NH + 2 * p
    return jnp.concatenate([jnp.sum(jnp.where(lane == c, tile8, 0.0), axis=1, keepdims=True),
                            jnp.sum(jnp.where(lane == c + 1, tile8, 0.0), axis=1, keepdims=True)], axis=0)


def _stat_tile(cols):
    lane = _iota((BQ, NSTAT), 1)
    t = jnp.zeros((BQ, NSTAT), F32)
    for (p, b), v in cols.items():
        c = b * NH + 2 * p
        t = t + jnp.where(lane == c, v[:BQ], 0.0) + jnp.where(lane == c + 1, v[BQ:], 0.0)
    return t


def _key_rows(ck_ref, p, b, j):
    c = b * NH + 2 * p
    kk = pl.ds(pl.multiple_of(j * BQ, BQ), BQ)
    return jnp.concatenate([jnp.broadcast_to(ck_ref[c:c + 1, kk], (BQ, BQ)),
                            jnp.broadcast_to(ck_ref[c + 1:c + 2, kk], (BQ, BQ))], axis=0)


def _copy_in(src, dst, sem):
    cp = pltpu.make_async_copy(src, dst, sem)
    cp.start()
    cp.wait()


STAT_SHAPE = jax.ShapeDtypeStruct((S, NSTAT), F32)
SLAB_QKV = pltpu.VMEM((T, 2 * PAIRW), BF16)
SLAB_OUT = pltpu.VMEM((T, 2 * BQ), BF16)
ACC_KV = pltpu.VMEM((2, T, BQ), F32)
SLAB_O32 = pltpu.VMEM((T, 2 * BQ), F32)


class _Job:
    def __init__(self, ins, out_shapes, aliases, sems, start, finish):
        self.ins, self.out_shapes, self.aliases, self.sems = list(ins), list(out_shapes), dict(aliases), list(sems)
        self.start, self.finish = start, finish


def _host_call(body, name, ins, in_specs, out_shapes, out_specs, scratch, aliases, job):
    n_in, n_out, n_scr = len(ins), len(out_shapes), len(scratch)
    jins = job.ins if job else []
    jouts = job.out_shapes if job else []
    jsems = job.sems if job else []

    def wrapped(*refs):
        a = n_in
        b = a + len(jins)
        c = b + n_out
        d = c + len(jouts)
        e = d + n_scr
        comm = None
        if job:
            jrefs = (refs[a:b], refs[c:d], refs[e:])
            comm = (lambda: job.start(*jrefs), lambda st: job.finish(st, *jrefs))
        body(refs[:a], refs[b:c], refs[d:e], comm)

    al = dict(aliases)
    if job:
        for ji, jo in job.aliases.items():
            al[n_in + ji] = n_out + jo
    res = pl.pallas_call(
        wrapped, name=name, in_specs=list(in_specs) + [ANY_SPEC] * len(jins),
        out_specs=list(out_specs) + [ANY_SPEC] * len(jouts), out_shape=list(out_shapes) + list(jouts),
        scratch_shapes=list(scratch) + list(jsems), input_output_aliases=al,
        compiler_params=_cp())(*ins, *jins)
    return res[:n_out], res[n_out:]


def _sb_fwd(qkv, job=None):
    def body(ins, outs, scr, comm):
        (qkv_hbm,), (o_hbm, r_ref), (qkv_v, o_v, sem) = ins, outs, scr
        _copy_in(qkv_hbm.at[:, pl.ds(0, 2 * PAIRW)], qkv_v, sem)
        st = comm[0]() if comm else None
        m0, m1 = _lane_masks()
        valid = _diag_valid(True)
        u2 = ((_iota((HB, BQ), 0) & (BQ - 1)) > _iota((HB, BQ), 1)).astype(BF16)

        def steps(qs, j, cs, diag):
            kv = [_load_kv(qkv_v, p, b, j) for p, b in CHAINS]
            zs = [_dot_nt(qs[c], kv[c][0]) for c in range(NC)]
            lbs, lrs = [], []
            for c in range(NC):
                lb = _log_sigmoid_tile(zs[c])
                lr = lb - zs[c]
                if diag:
                    lr = jnp.where(valid, lr, 0.0)
                lbs.append(lb)
                lrs.append(lr)
            tails = [_dot(_split2(lrs[c]), u2) for c in range(NC)]
            out = []
            for c in range(NC):
                tail_c, acc = cs[c]
                a = jnp.exp(lbs[c] + tails[c] + tail_c)
                if diag:
                    a = jnp.where(valid, a, 0.0)
                acc = acc + _dot(_lanes(a.astype(BF16)), _stack(kv[c][1], m0, m1))
                out.append((tail_c + jnp.sum(lrs[c], axis=1, keepdims=True), acc))
            return tuple(out)

        def qblock(i, _):
            qs = [_load_q(qkv_v, p, b, i, m0, m1) for p, b in CHAINS]
            zero = (jnp.zeros((HB, 1), F32), jnp.zeros((BQ, BQ), F32))
            cs = steps(qs, i, (zero,) * NC, True)
            cs = lax.fori_loop(1, i + 1, lambda jj, cs: steps(qs, i - jj, cs, False), cs)
            for c, (p, b) in enumerate(CHAINS):
                o_v[_rows(b, i), p * BQ:(p + 1) * BQ] = cs[c][1].astype(BF16)
            r_ref[_rows(0, i), :] = _stat_tile({pb: cs[c][0] for c, pb in enumerate(CHAINS)})
            return 0

        lax.fori_loop(0, NB, qblock, 0)
        _copy_in(o_v, o_hbm.at[:, pl.ds(0, 2 * BQ)], sem)
        if comm:
            comm[1](st)

    (mixed, rtot), extra = _host_call(
        body, "sb_fwd", [qkv], [ANY_SPEC], [jax.ShapeDtypeStruct((T, D), BF16), STAT_SHAPE], [ANY_SPEC, VMEM_SPEC],
        [SLAB_QKV, SLAB_OUT, pltpu.SemaphoreType.DMA], {}, job)
    return mixed, rtot, extra


def _sb_bwd(qkv, dmixed, rtot, job=None):
    def body(ins, outs, scr, comm):
        (qkv_hbm, do_hbm, r_ref), (dqkv_hbm,), (qkv_v, do_v, dq_v, dk_s, dv_s, sem) = ins, outs, scr
        _copy_in(qkv_hbm.at[:, pl.ds(0, 2 * PAIRW)], qkv_v, sem)
        _copy_in(do_hbm.at[:, pl.ds(0, 2 * BQ)], do_v, sem)
        st = comm[0]() if comm else None
        m0, m1 = _lane_masks()
        valid = _diag_valid(True)
        r2 = _iota((HB, BQ), 0) & (BQ - 1)
        c2 = _iota((HB, BQ), 1)
        u2 = (r2 > c2).astype(BF16)
        l2 = (r2 < c2).astype(BF16)
        dk_s[...] = jnp.zeros_like(dk_s)
        dv_s[...] = jnp.zeros_like(dv_s)

        def steps(qs, dos, rts, j, cs, diag):
            kv = [_load_kv(qkv_v, p, b, j) for p, b in CHAINS]
            zs = [_dot_nt(qs[c], kv[c][0]) for c in range(NC)]
            das = [_dot_nt(dos[c], kv[c][1]) for c in range(NC)]
            lbs, lrs, pre_ls = [], [], []
            for c in range(NC):
                lb = _log_sigmoid_tile(zs[c])
                lr = lb - zs[c]
                if diag:
                    lr = jnp.where(valid, lr, 0.0)
                lbs.append(lb)
                lrs.append(lr)
                pre_ls.append(cs[c][0] + jnp.sum(lr, axis=1, keepdims=True))
            tails = [_dot(_split2(lrs[c]), u2) for c in range(NC)]
            avs, gms = [], []
            for c in range(NC):
                a = jnp.exp(lbs[c] + tails[c] + (rts[c] - pre_ls[c]))
                if diag:
                    a = jnp.where(valid, a, 0.0)
                avs.append(a)
                gms.append(das[c] * a)
            befores = [_dot(_split2(gms[c]), l2) for c in range(NC)]
            dzbs = []
            for c in range(NC):
                beta = jnp.exp(lbs[c])
                dz = gms[c] * (1.0 - beta) - beta * (befores[c] + cs[c][1])
                if diag:
                    dz = jnp.where(valid, dz, 0.0)
                dzbs.append(dz.astype(BF16))
            out = []
            for c, (p, b) in enumerate(CHAINS):
                dq = cs[c][2] + _dot(_lanes(dzbs[c]), _stack(kv[c][0], m0, m1))
                dk_s[p, _rows(b, j), :] += _dot_tn(dzbs[c], qs[c])
                dv_s[p, _rows(b, j), :] += _dot_tn(avs[c].astype(BF16), dos[c])
                out.append((pre_ls[c], cs[c][1] + jnp.sum(gms[c], axis=1, keepdims=True), dq))
            return tuple(out)

        def qblock(i, _):
            r8 = r_ref[_rows(0, i), :]
            qs = [_load_q(qkv_v, p, b, i, m0, m1) for p, b in CHAINS]
            dos = [_stack(do_v[_rows(b, i), p * BQ:(p + 1) * BQ], m0, m1) for p, b in CHAINS]
            rts = [_stat_cols(r8, p, b) for p, b in CHAINS]
            z1 = jnp.zeros((HB, 1), F32)
            cs = ((z1, z1, jnp.zeros((BQ, BQ), F32)),) * NC
            cs = lax.fori_loop(0, i, lambda j, cs: steps(qs, dos, rts, j, cs, False), cs)
            cs = steps(qs, dos, rts, i, cs, True)
            for c, (p, b) in enumerate(CHAINS):
                dq_v[_rows(b, i), p * PAIRW:p * PAIRW + BQ] = (cs[c][2] * SCALE).astype(BF16)
            return 0

        lax.fori_loop(0, NB, qblock, 0)
        for p in range(2):
            dq_v[:, p * PAIRW + BQ:p * PAIRW + 2 * BQ] = dk_s[p].astype(BF16)
            dq_v[:, p * PAIRW + 2 * BQ:p * PAIRW + 3 * BQ] = dv_s[p].astype(BF16)
        _copy_in(dq_v, dqkv_hbm.at[:, pl.ds(0, 2 * PAIRW)], sem)
        if comm:
            comm[1](st)

    (dqkv,), extra = _host_call(
        body, "sb_bwd", [qkv, dmixed, rtot], [ANY_SPEC, ANY_SPEC, VMEM_SPEC],
        [jax.ShapeDtypeStruct((T, QKVW), BF16)], [ANY_SPEC],
        [SLAB_QKV, SLAB_OUT, SLAB_QKV, ACC_KV, ACC_KV, pltpu.SemaphoreType.DMA], {}, job)
    return dqkv, extra


def _flash_fwd(qkv, mixed, g, fox, bias, job=None):
    def body(ins, outs, scr, comm):
        if fox:
            qkv_hbm, cq_ref, ck_ref, _ = ins
        else:
            qkv_hbm, tbl_ref, _ = ins
        if fox:
            (o_hbm, lse_ref, o32_hbm), (qkv_v, o_v, sem, o32_v) = outs, scr
        else:
            (o_hbm, lse_ref), (qkv_v, o_v, sem) = outs, scr
        _copy_in(qkv_hbm.at[:, pl.ds(g * 2 * PAIRW, 2 * PAIRW)], qkv_v, sem)
        st = comm[0]() if comm else None
        m0, m1 = _lane_masks()
        valid = _diag_valid(False)

        def steps(qs, cqs, i, j, cs, diag):
            kv = [_load_kv(qkv_v, p, b, j) for p, b in CHAINS]
            zs = [_dot_nt(qs[c], kv[c][0]) for c in range(NC)]
            prs, alphas, out = [], [], []
            for c, (p, b) in enumerate(CHAINS):
                m, l, _ = cs[c]
                if fox:
                    z = zs[c] + (cqs[c] - _key_rows(ck_ref, p, b, j))
                    if diag:
                        z = jnp.where(valid, z, NEG)
                else:
                    z = zs[c] + tbl_ref[p, i - j]
                m_new = jnp.maximum(m, jnp.max(z, axis=1, keepdims=True))
                alpha = jnp.exp(m - m_new)
                pr = jnp.exp(z - m_new)
                prs.append(_split2(pr) if fox else pr.astype(BF16))
                alphas.append(alpha)
                out.append((m_new, alpha * l + jnp.sum(pr, axis=1, keepdims=True)))
            if fox:
                pvs = []
                for c in range(NC):
                    vs = _stack(kv[c][1], m0, m1)
                    vs = jnp.concatenate([vs[:BQ], vs[:BQ], vs[BQ:], vs[BQ:]], axis=0)
                    pvs.append(_dot(_lanes(prs[c]), vs))
            else:
                pvs = [_dot(_lanes(prs[c]), _stack(kv[c][1], m0, m1)) for c in range(NC)]
            return tuple((out[c][0], out[c][1], _per_lane(alphas[c]) * cs[c][2] + pvs[c]) for c in range(NC))

        def qblock(i, _):
            qs = [_load_q(qkv_v, p, b, i, m0, m1) for p, b in CHAINS]
            if fox:
                c8 = cq_ref[_rows(0, i), :]
                cqs = [_stat_cols(c8, p, b) for p, b in CHAINS]
            else:
                cqs = [None] * NC
            zero = (jnp.full((HB, 1), NEG, F32), jnp.zeros((HB, 1), F32), jnp.zeros((BQ, BQ), F32))
            cs = steps(qs, cqs, i, i, (zero,) * NC, True)
            cs = lax.fori_loop(1, i + 1, lambda jj, cs: steps(qs, cqs, i, i - jj, cs, False), cs)
            for c, (p, b) in enumerate(CHAINS):
                m, l, acc = cs[c]
                o = acc / _per_lane(l)
                o_v[_rows(b, i), p * BQ:(p + 1) * BQ] = o.astype(BF16)
                if fox:
                    o32_v[_rows(b, i), p * BQ:(p + 1) * BQ] = o
            lse_ref[_rows(0, i), :] = _stat_tile({pb: cs[c][0] + jnp.log(cs[c][1]) for c, pb in enumerate(CHAINS)})
            return 0

        lax.fori_loop(0, NB, qblock, 0)
        _copy_in(o_v, o_hbm.at[:, pl.ds(g * 2 * BQ, 2 * BQ)], sem)
        if fox:
            _copy_in(o32_v, o32_hbm, sem)
        if comm:
            comm[1](st)

    bias_specs = [VMEM_SPEC, VMEM_SPEC] if fox else [VMEM_SPEC]
    n_in = 2 + len(bias_specs)
    o32 = [jax.ShapeDtypeStruct((T, 2 * BQ), F32)] if fox else []
    res, extra = _host_call(
        body, "fox_fwd" if fox else "dil_fwd", [qkv, *bias, mixed], [ANY_SPEC] + bias_specs + [ANY_SPEC],
        [jax.ShapeDtypeStruct((T, D), BF16), STAT_SHAPE] + o32, [ANY_SPEC, VMEM_SPEC] + [ANY_SPEC] * len(o32),
        [SLAB_QKV, SLAB_OUT, pltpu.SemaphoreType.DMA] + ([SLAB_O32] if fox else []), {n_in - 1: 0}, job)
    return (*res, extra)


def _flash_bwd(qkv, mixed, dmixed, lse, dqkv, g, fox, bias, job=None):
    def body(ins, outs, scr, comm):
        if fox:
            qkv_hbm, o_hbm, do_hbm, lse_ref, cq_ref, ck_ref, _ = ins
        else:
            qkv_hbm, o_hbm, do_hbm, lse_ref, tbl_ref, _ = ins
        (dqkv_hbm, db_ref), (qkv_v, o_v, do_v, dq_v, dk_s, dv_s, sem) = outs, scr
        _copy_in(qkv_hbm.at[:, pl.ds(g * 2 * PAIRW, 2 * PAIRW)], qkv_v, sem)
        _copy_in(do_hbm.at[:, pl.ds(g * 2 * BQ, 2 * BQ)], do_v, sem)
        if fox:
            _copy_in(o_hbm, o_v, sem)
        else:
            _copy_in(o_hbm.at[:, pl.ds(g * 2 * BQ, 2 * BQ)], o_v, sem)
        st = comm[0]() if comm else None
        m0, m1 = _lane_masks()
        valid = _diag_valid(False)
        dk_s[...] = jnp.zeros_like(dk_s)
        dv_s[...] = jnp.zeros_like(dv_s)
        db_ref[...] = jnp.zeros_like(db_ref)

        def probs(qs, dos, cqs, lses, i, j, diag):
            kv = [_load_kv(qkv_v, p, b, j) for p, b in CHAINS]
            zs = [_dot_nt(qs[c], kv[c][0]) for c in range(NC)]
            dps = [_dot_nt(dos[c], kv[c][1]) for c in range(NC)]
            prs = []
            for c, (p, b) in enumerate(CHAINS):
                if fox:
                    z = zs[c] + (cqs[c] - _key_rows(ck_ref, p, b, j))
                    if diag:
                        z = jnp.where(valid, z, NEG)
                else:
                    z = zs[c] + tbl_ref[p, i - j]
                prs.append(jnp.exp(z - lses[c]))
            return [kv[c][0] for c in range(NC)], prs, dps

        def qblock(i, _):
            l8 = lse_ref[_rows(0, i), :]
            qs = [_load_q(qkv_v, p, b, i, m0, m1) for p, b in CHAINS]
            dos = [_stack(do_v[_rows(b, i), p * BQ:(p + 1) * BQ], m0, m1) for p, b in CHAINS]
            lses = [_stat_cols(l8, p, b) for p, b in CHAINS]
            if fox:
                c8 = cq_ref[_rows(0, i), :]
                cqs = [_stat_cols(c8, p, b) for p, b in CHAINS]
            else:
                cqs = [None] * NC
            deltas = []
            for c, (p, b) in enumerate(CHAINS):
                ob = o_v[_rows(b, i), p * BQ:(p + 1) * BQ].astype(F32)
                deltas.append(jnp.sum(dos[c].astype(F32) * jnp.concatenate([ob, ob], axis=0), axis=1, keepdims=True))

            def inner(j, dqs, diag):
                ks, prs, dps = probs(qs, dos, cqs, lses, i, j, diag)
                new, dzs = [], {}
                dzl = [prs[c] * (dps[c] - deltas[c]) for c in range(NC)]
                dzbs = [dz.astype(BF16) for dz in dzl]
                for c, (p, b) in enumerate(CHAINS):
                    dk_s[p, _rows(b, j), :] += _dot_tn(dzbs[c], qs[c])
                    dv_s[p, _rows(b, j), :] += _dot_tn(prs[c].astype(BF16), dos[c])
                    new.append(dqs[c] + _dot(_lanes(dzbs[c]), _stack(ks[c], m0, m1)))
                    dzs[(p, b)] = dzl[c]
                if fox:
                    kk = pl.ds(pl.multiple_of(j * BQ, BQ), BQ)
                    for (p, b), dz in dzs.items():
                        r = b * NH + 2 * p
                        db_ref[r:r + 1, kk] = db_ref[r:r + 1, kk] - jnp.sum(dz[:BQ], axis=0, keepdims=True)
                        db_ref[r + 1:r + 2, kk] = db_ref[r + 1:r + 2, kk] - jnp.sum(dz[BQ:], axis=0, keepdims=True)
                else:
                    for p in range(2):
                        db_ref[p, i - j] = db_ref[p, i - j] + (dzs[(p, 0)] + dzs[(p, 1)])
                return tuple(new)

            dqs = tuple(jnp.zeros((BQ, BQ), F32) for _ in CHAINS)
            dqs = lax.fori_loop(0, i, lambda j, d: inner(j, d, False), dqs)
            dqs = inner(i, dqs, True)
            for c, (p, b) in enumerate(CHAINS):
                dq_v[_rows(b, i), p * PAIRW:p * PAIRW + BQ] = (dqs[c] * SCALE).astype(BF16)
            return 0

        lax.fori_loop(0, NB, qblock, 0)
        for p in range(2):
            dq_v[:, p * PAIRW + BQ:p * PAIRW + 2 * BQ] = dk_s[p].astype(BF16)
            dq_v[:, p * PAIRW + 2 * BQ:p * PAIRW + 3 * BQ] = dv_s[p].astype(BF16)
        _copy_in(dq_v, dqkv_hbm.at[:, pl.ds(g * 2 * PAIRW, 2 * PAIRW)], sem)
        if comm:
            comm[1](st)

    bias_specs = [VMEM_SPEC, VMEM_SPEC] if fox else [VMEM_SPEC]
    db_shape = jax.ShapeDtypeStruct((NSTAT, S), F32) if fox else jax.ShapeDtypeStruct((2, NB, HB, BQ), F32)
    n_in = 5 + len(bias_specs)
    (dqkv, db), extra = _host_call(
        body, "fox_bwd" if fox else "dil_bwd", [qkv, mixed, dmixed, lse, *bias, dqkv],
        [ANY_SPEC, ANY_SPEC, ANY_SPEC, VMEM_SPEC] + bias_specs + [ANY_SPEC],
        [jax.ShapeDtypeStruct((T, QKVW), BF16), db_shape], [ANY_SPEC, VMEM_SPEC],
        [SLAB_QKV, SLAB_O32 if fox else SLAB_OUT, SLAB_OUT, SLAB_QKV, ACC_KV, ACC_KV, pltpu.SemaphoreType.DMA],
        {n_in - 1: 0}, job)
    return dqkv, db, extra


def _stacked_delta(d):
    return d * BQ + (_iota((HB, BQ), 0) & (BQ - 1)) - _iota((HB, BQ), 1)


def _buckets_in(d):
    lo, hi = max(d * BQ - (BQ - 1), 0), d * BQ + BQ - 1
    return [b for b in range(32) if BUCKET_TH[b] <= hi and (b == 31 or BUCKET_TH[b + 1] > lo)]


def _in_bucket(delta, b):
    m = delta >= BUCKET_TH[b]
    return m if b == 31 else m & (delta < BUCKET_TH[b + 1])


def _dil_table(rel_bias):
    def body(rb_ref, o_ref):
        for d in range(NB):
            delta = _stacked_delta(d)
            pos = delta >= 0
            n = ((pos & (delta <= 128)).astype(jnp.int32)
                 + (pos & (delta <= 512) & ((delta & 3) == 0)).astype(jnp.int32)
                 + (pos & ((delta & 15) == 0)).astype(jnp.int32))
            logn = jnp.where(n == 3, math.log(3.0), jnp.where(n == 2, math.log(2.0), jnp.where(n == 1, 0.0, NEG)))
            head1 = _iota((HB, BQ), 0) >= BQ
            for p in range(2):
                val = jnp.zeros((HB, BQ), F32)
                for b in _buckets_in(d):
                    val = jnp.where(_in_bucket(delta, b), jnp.where(head1, rb_ref[b, 2 * p + 1], rb_ref[b, 2 * p]), val)
                o_ref[p, d] = val + logn

    return pl.pallas_call(
        body, name="dil_table", in_specs=[pl.BlockSpec(memory_space=pltpu.SMEM)], out_specs=VMEM_SPEC,
        out_shape=jax.ShapeDtypeStruct((2, NB, HB, BQ), F32), compiler_params=_cp())(rel_bias)


def _dil_table_bwd(dtbl):
    def body(dt_ref, o_ref):
        p = pl.program_id(0)
        rowi = _iota((32, BQ), 0)
        lanei = _iota((32, BQ), 1)

        @pl.when(p == 0)
        def _():
            o_ref[...] = jnp.zeros_like(o_ref)

        out = jnp.zeros((32, BQ), F32)
        for b in range(32):
            acc = None
            for d in range(NB):
                if b in _buckets_in(d):
                    t = jnp.where(_in_bucket(_stacked_delta(d), b), dt_ref[d], 0.0)
                    acc = t if acc is None else acc + t
            rs = jnp.sum(acc, axis=1, keepdims=True)
            s0 = jnp.sum(rs[:BQ], axis=0, keepdims=True)
            s1 = jnp.sum(rs[BQ:], axis=0, keepdims=True)
            out = (out + jnp.where((rowi == b) & (lanei == 2 * p), s0, 0.0)
                   + jnp.where((rowi == b) & (lanei == 2 * p + 1), s1, 0.0))
        o_ref[...] += out

    return pl.pallas_call(
        body, name="dil_table_bwd", grid=(2,),
        in_specs=[pl.BlockSpec((None, NB, HB, BQ), lambda p: (p, 0, 0, 0))],
        out_specs=pl.BlockSpec((32, BQ), lambda p: (0, 0)),
        out_shape=jax.ShapeDtypeStruct((32, BQ), F32),
        compiler_params=_cp(("arbitrary",)))(dtbl)


def _fox_prep(gate, fb):
    def body(g_ref, fb_ref, c_ref):
        tri = (_iota((BQ, BQ), 0) >= _iota((BQ, BQ), 1)).astype(BF16)

        def blk(i, carry):
            r0 = pl.multiple_of(i * BQ, BQ)
            lf = _log_sigmoid(g_ref[pl.ds(r0, BQ), :] + fb_ref[...])
            c = _dot(tri, _split3(lf))
            c_ref[pl.ds(r0, BQ), :] = c[:, 0:BQ] + c[:, BQ:2 * BQ] + c[:, 2 * BQ:3 * BQ] + carry
            return carry + jnp.sum(lf, axis=0, keepdims=True)

        lax.fori_loop(0, NB, blk, jnp.zeros((1, BQ), F32))

    blk = pl.BlockSpec((S, GATEW), lambda b: (b, 0))
    return pl.pallas_call(
        body, name="fox_prep", grid=(BL,), in_specs=[blk, pl.BlockSpec((1, GATEW), lambda b: (0, 0))],
        out_specs=blk, out_shape=jax.ShapeDtypeStruct((T, GATEW), F32),
        compiler_params=_cp(("parallel",)))(gate, fb)


def _fox_post(dcum, gate, fb):
    def body(dc_ref, g_ref, fb_ref, dg_ref, dfb_ref):
        b = pl.program_id(0)
        tri = (_iota((BQ, BQ), 0) <= _iota((BQ, BQ), 1)).astype(BF16)

        def blk(ii, carry):
            csum, dfb = carry
            r0 = pl.multiple_of((NB - 1 - ii) * BQ, BQ)
            dc = dc_ref[pl.ds(r0, BQ), :]
            c = _dot(tri, _split3(dc))
            dlf = c[:, 0:BQ] + c[:, BQ:2 * BQ] + c[:, 2 * BQ:3 * BQ] + csum
            dg = dlf * jnp.exp(_log_sigmoid(-(g_ref[pl.ds(r0, BQ), :] + fb_ref[...])))
            dg_ref[pl.ds(r0, BQ), :] = dg
            return csum + jnp.sum(dc, axis=0, keepdims=True), dfb + jnp.sum(dg, axis=0, keepdims=True)

        z = jnp.zeros((1, BQ), F32)
        _, dfb = lax.fori_loop(0, NB, blk, (z, z))

        @pl.when(b == 0)
        def _():
            dfb_ref[...] = dfb

        @pl.when(b > 0)
        def _():
            dfb_ref[...] += dfb

    blk = pl.BlockSpec((S, GATEW), lambda b: (b, 0))
    vec = pl.BlockSpec((1, GATEW), lambda b: (0, 0))
    return pl.pallas_call(
        body, name="fox_post", grid=(BL,), in_specs=[blk, blk, vec], out_specs=[blk, vec],
        out_shape=[jax.ShapeDtypeStruct((T, GATEW), F32), jax.ShapeDtypeStruct((1, GATEW), F32)],
        compiler_params=_cp(("arbitrary",)))(dcum, gate, fb)


def _shift_down(x, n):
    return jnp.where(_iota(x.shape, 0) >= n, pltpu.roll(x, n, 0), 0.0)


def _shift_up(x, n):
    return jnp.where(_iota(x.shape, 0) < S - n, pltpu.roll(x, S - n, 0), 0.0)


def _conv_fwd(conv, cw, mixed):
    W = 256

    def body(c_ref, w_ref, _, o_ref):
        u = c_ref[:, W:2 * W] * c_ref[:, 2 * W:3 * W]
        y = w_ref[0:1, :] * _shift_down(u, 2) + w_ref[1:2, :] * _shift_down(u, 1) + w_ref[2:3, :] * u
        o_ref[...] = (c_ref[:, 0:W] * y).astype(BF16)

    return pl.pallas_call(
        body, name="conv_fwd", grid=(BL,),
        in_specs=[pl.BlockSpec((S, CONVW), lambda b: (b, 0)), pl.BlockSpec((8, W), lambda b: (0, 0)), ANY_SPEC],
        out_specs=pl.BlockSpec((S, W), lambda b: (b, 3)),
        out_shape=jax.ShapeDtypeStruct((T, D), BF16), input_output_aliases={2: 0},
        compiler_params=_cp(("parallel",)))(conv, cw, mixed)


def _conv_bwd(conv, cw, dmixed):
    W = 256

    def body(c_ref, w_ref, do_ref, dc_ref, dw_ref):
        b = pl.program_id(0)
        bg = c_ref[:, 0:W]
        cg = c_ref[:, W:2 * W]
        hv = c_ref[:, 2 * W:3 * W]
        do = do_ref[...].astype(F32)
        u = cg * hv
        u1 = _shift_down(u, 1)
        u2 = _shift_down(u, 2)
        y = w_ref[0:1, :] * u2 + w_ref[1:2, :] * u1 + w_ref[2:3, :] * u
        dy = do * bg
        du = w_ref[2:3, :] * dy + w_ref[1:2, :] * _shift_up(dy, 1) + w_ref[0:1, :] * _shift_up(dy, 2)
        dc_ref[:, 0:W] = (do * y).astype(BF16)
        dc_ref[:, W:2 * W] = (du * hv).astype(BF16)
        dc_ref[:, 2 * W:3 * W] = (du * cg).astype(BF16)
        rowi = _iota((8, W), 0)
        dw = (jnp.where(rowi == 0, jnp.sum(dy * u2, axis=0, keepdims=True), 0.0)
              + jnp.where(rowi == 1, jnp.sum(dy * u1, axis=0, keepdims=True), 0.0)
              + jnp.where(rowi == 2, jnp.sum(dy * u, axis=0, keepdims=True), 0.0))

        @pl.when(b == 0)
        def _():
            dw_ref[...] = dw

        @pl.when(b > 0)
        def _():
            dw_ref[...] += dw

    return pl.pallas_call(
        body, name="conv_bwd", grid=(BL,),
        in_specs=[pl.BlockSpec((S, CONVW), lambda b: (b, 0)), pl.BlockSpec((8, W), lambda b: (0, 0)),
                  pl.BlockSpec((S, W), lambda b: (b, 3))],
        out_specs=[pl.BlockSpec((S, CONVW), lambda b: (b, 0)), pl.BlockSpec((8, W), lambda b: (0, 0))],
        out_shape=[jax.ShapeDtypeStruct((T, CONVW), BF16), jax.ShapeDtypeStruct((8, W), F32)],
        compiler_params=_cp(("arbitrary",)))(conv, cw, dmixed)


def _place():
    x, y, c = lax.axis_index("x"), lax.axis_index("y"), lax.axis_index("c")
    return x, y, c


def _allgather_weights(shards):
    n = len(shards)

    def body(*refs):
        ins, outs = refs[:n], refs[n:2 * n]
        send_sems, recv_sems, local_sems = refs[2 * n:]
        x, y, c = _place()
        me, sibling = (x, y, c), (x, y, 1 - c)
        chips = [(1 - x, y), (x, 1 - y), (1 - x, 1 - y)]

        def slot(a, p):
            return outs[a].at[:, 4 * p[0] + 2 * p[1] + p[2]]

        def copy(a, k, block, to, own=False):
            return pltpu.make_async_remote_copy(
                src_ref=ins[a] if own else slot(a, block), dst_ref=slot(a, block),
                send_sem=send_sems.at[a, k], recv_sem=recv_sems.at[a, k], device_id=to, device_id_type=MESH)

        mine = [pltpu.make_async_copy(ins[a], slot(a, me), local_sems.at[a]) for a in range(n)]
        for cp in mine:
            cp.start()
        first = []
        for a in range(n):
            first.append(copy(a, 0, me, sibling, own=True))
            first += [copy(a, 1 + j, me, (*chip, c), own=True) for j, chip in enumerate(chips)]
        for cp in first:
            cp.start()
        passed = []
        for j, chip in enumerate(chips):
            for a in range(n):
                copy(a, 1 + j, (*chip, c), me).wait_recv()
                cp = copy(a, 4 + j, (*chip, c), sibling)
                cp.start()
                passed.append(cp)
        for a in range(n):
            copy(a, 0, sibling, me).wait_recv()
            for j, chip in enumerate(chips):
                copy(a, 4 + j, (*chip, 1 - c), me).wait_recv()
        for cp in first + passed:
            cp.wait_send()
        for cp in mine:
            cp.wait()

    return pl.pallas_call(
        body, name="allgather_weights", in_specs=[ANY_SPEC] * n, out_specs=[ANY_SPEC] * n,
        out_shape=[jax.ShapeDtypeStruct((s.shape[0], NDEV) + s.shape[1:], s.dtype) for s in shards],
        scratch_shapes=[pltpu.SemaphoreType.DMA((n, 7)), pltpu.SemaphoreType.DMA((n, 7)),
                        pltpu.SemaphoreType.DMA((n,))],
        )(*shards)


def _allreduce_small(v):
    def body(v_ref, o_ref, slots, send_sems, recv_sems):
        x, y, c = _place()
        me = 4 * x + 2 * y + c
        slots[me] = v_ref[...]

        def copy(k):
            peer = (x ^ ((k >> 2) & 1), y ^ ((k >> 1) & 1), c ^ (k & 1))
            return pltpu.make_async_remote_copy(
                src_ref=v_ref, dst_ref=slots.at[me], send_sem=send_sems.at[k - 1], recv_sem=recv_sems.at[k - 1],
                device_id=peer, device_id_type=MESH)

        def arrival(k):
            return pltpu.make_async_remote_copy(
                src_ref=v_ref, dst_ref=slots.at[me ^ k], send_sem=send_sems.at[k - 1], recv_sem=recv_sems.at[k - 1],
                device_id=(x, y, c), device_id_type=MESH)

        sends = [copy(k) for k in range(1, NDEV)]
        for cp in sends:
            cp.start()
        for k in range(1, NDEV):
            arrival(k).wait_recv()
        for cp in sends:
            cp.wait_send()
        acc = slots[0]
        for d in range(1, NDEV):
            acc = acc + slots[d]
        o_ref[...] = acc

    return pl.pallas_call(
        body, name="allreduce_small", in_specs=[VMEM_SPEC], out_specs=VMEM_SPEC,
        out_shape=jax.ShapeDtypeStruct(v.shape, F32),
        scratch_shapes=[pltpu.VMEM((NDEV,) + v.shape, F32), pltpu.SemaphoreType.DMA((NDEV - 1,)),
                        pltpu.SemaphoreType.DMA((NDEV - 1,))],
        )(v)


def _sibling_exchange(grads):
    n = len(grads)

    def body(*refs):
        ins, outs = refs[:n], refs[n:2 * n]
        send_sems, recv_sems = refs[2 * n:]
        x, y, c = _place()
        cps = [pltpu.make_async_remote_copy(
            src_ref=ins[a].at[:, :, 1 - c], dst_ref=outs[a], send_sem=send_sems.at[a], recv_sem=recv_sems.at[a],
            device_id=(x, y, 1 - c), device_id_type=MESH) for a in range(n)]
        for cp in cps:
            cp.start()
        for cp in cps:
            cp.wait()

    return pl.pallas_call(
        body, name="sibling_exchange", in_specs=[ANY_SPEC] * n, out_specs=[ANY_SPEC] * n,
        out_shape=[jax.ShapeDtypeStruct(g.shape[:2] + g.shape[3:], F32) for g in grads],
        scratch_shapes=[pltpu.SemaphoreType.DMA((n,)), pltpu.SemaphoreType.DMA((n,))],
        )(*grads)


def _pair_sum(grad, got, core):
    _, _, _, rows, N = grad.shape

    def body(c_ref, g_ref, r_ref, o_ref):
        o_ref[...] = (g_ref[...] + r_ref[...]).astype(BF16)

    return pl.pallas_call(
        body, name="pair_sum",
        grid_spec=pltpu.PrefetchScalarGridSpec(
            num_scalar_prefetch=1, grid=(2, 4),
            in_specs=[pl.BlockSpec((None, None, None, rows, N), lambda l, k, c: (l, k, c[0], 0, 0)),
                      pl.BlockSpec((None, None, rows, N), lambda l, k, c: (l, k, 0, 0))],
            out_specs=pl.BlockSpec((None, None, rows, N), lambda l, k, c: (l, k, 0, 0))),
        out_shape=jax.ShapeDtypeStruct((2, 4, rows, N), BF16),
        compiler_params=_cp(("parallel", "parallel")))(core, grad, got)


def _chip_exchange(psums):
    n = len(psums)

    def body(*refs):
        ins, outs = refs[:n], refs[n:2 * n]
        send_sems, recv_sems, local_sems = refs[2 * n:]
        x, y, c = _place()
        mychip = 2 * x + y
        chips = [(1 - x, y), (x, 1 - y), (1 - x, 1 - y)]
        local = [pltpu.make_async_copy(ins[a].at[:, mychip], outs[a].at[:, mychip], local_sems.at[a]) for a in range(n)]
        for cp in local:
            cp.start()
        sends = []
        for a in range(n):
            for j, chip in enumerate(chips):
                sends.append(pltpu.make_async_remote_copy(
                    src_ref=ins[a].at[:, 2 * chip[0] + chip[1]], dst_ref=outs[a].at[:, mychip],
                    send_sem=send_sems.at[a, j], recv_sem=recv_sems.at[a, j],
                    device_id=(*chip, c), device_id_type=MESH))
        for cp in sends:
            cp.start()
        for a in range(n):
            for j, chip in enumerate(chips):
                pltpu.make_async_remote_copy(
                    src_ref=ins[a].at[:, mychip], dst_ref=outs[a].at[:, 2 * chip[0] + chip[1]],
                    send_sem=send_sems.at[a, j], recv_sem=recv_sems.at[a, j],
                    device_id=(x, y, c), device_id_type=MESH).wait_recv()
        for cp in sends:
            cp.wait_send()
        for cp in local:
            cp.wait()

    return pl.pallas_call(
        body, name="chip_exchange", in_specs=[ANY_SPEC] * n, out_specs=[ANY_SPEC] * n,
        out_shape=[jax.ShapeDtypeStruct(p.shape, BF16) for p in psums],
        scratch_shapes=[pltpu.SemaphoreType.DMA((n, 3)), pltpu.SemaphoreType.DMA((n, 3)),
                        pltpu.SemaphoreType.DMA((n,))],
        )(*psums)


def _chip_sum(parts):
    _, _, rows, N = parts.shape

    def body(p_ref, o_ref):
        acc = p_ref[0].astype(F32)
        for k in range(1, 4):
            acc = acc + p_ref[k].astype(F32)
        o_ref[...] = acc

    return pl.pallas_call(
        body, name="chip_sum", grid=(2,),
        in_specs=[pl.BlockSpec((None, 4, rows, N), lambda l: (l, 0, 0, 0))],
        out_specs=pl.BlockSpec((None, rows, N), lambda l: (l, 0, 0)),
        out_shape=jax.ShapeDtypeStruct((2, rows, N), F32), compiler_params=_cp(("parallel",)))(parts)


def _chips_of(x, y):
    return [(1 - x, y), (x, 1 - y), (1 - x, 1 - y)]


def _dev(p):
    return 4 * p[0] + 2 * p[1] + p[2]


def _gather_job_a(shards):
    n = len(shards)

    def peers(x, y, c):
        return [(x, y, 1 - c)] + [(*chip, c) for chip in _chips_of(x, y)]

    def start(ins, outs, sems):
        send, recv, loc = sems
        x, y, c = _place()
        me = (x, y, c)
        cps = []
        for a in range(n):
            cps.append(pltpu.make_async_copy(ins[a], outs[a].at[_dev(me)], loc.at[a]))
            for k, peer in enumerate(peers(x, y, c)):
                cps.append(pltpu.make_async_remote_copy(
                    src_ref=ins[a], dst_ref=outs[a].at[_dev(me)], send_sem=send.at[a, k], recv_sem=recv.at[a, k],
                    device_id=peer, device_id_type=MESH))
        for cp in cps:
            cp.start()
        return cps

    def finish(cps, ins, outs, sems):
        send, recv, loc = sems
        x, y, c = _place()
        for a in range(n):
            for k, peer in enumerate(peers(x, y, c)):
                pltpu.make_async_remote_copy(
                    src_ref=ins[a], dst_ref=outs[a].at[_dev(peer)], send_sem=send.at[a, k], recv_sem=recv.at[a, k],
                    device_id=(x, y, c), device_id_type=MESH).wait_recv()
        for a in range(n):
            cps[5 * a].wait()
            for k in range(4):
                cps[5 * a + 1 + k].wait_send()

    return _Job(shards, [jax.ShapeDtypeStruct((NDEV,) + s.shape, s.dtype) for s in shards], {},
                [pltpu.SemaphoreType.DMA((n, 4)), pltpu.SemaphoreType.DMA((n, 4)), pltpu.SemaphoreType.DMA((n,))],
                start, finish)


def _gather_job_b(gathered):
    n = len(gathered)

    def start(ins, outs, sems):
        send, recv = sems
        x, y, c = _place()
        cps = []
        for a in range(n):
            for j, chip in enumerate(_chips_of(x, y)):
                blk = outs[a].at[_dev((*chip, c))]
                cps.append(pltpu.make_async_remote_copy(
                    src_ref=blk, dst_ref=blk, send_sem=send.at[a, j], recv_sem=recv.at[a, j],
                    device_id=(x, y, 1 - c), device_id_type=MESH))
        for cp in cps:
            cp.start()
        return cps

    def finish(cps, ins, outs, sems):
        send, recv = sems
        x, y, c = _place()
        for a in range(n):
            for j, chip in enumerate(_chips_of(x, y)):
                blk = outs[a].at[_dev((*chip, 1 - c))]
                pltpu.make_async_remote_copy(
                    src_ref=blk, dst_ref=blk, send_sem=send.at[a, j], recv_sem=recv.at[a, j],
                    device_id=(x, y, c), device_id_type=MESH).wait_recv()
        for cp in cps:
            cp.wait_send()

    return _Job(gathered, [jax.ShapeDtypeStruct(g.shape, g.dtype) for g in gathered], {a: a for a in range(n)},
                [pltpu.SemaphoreType.DMA((n, 3)), pltpu.SemaphoreType.DMA((n, 3))], start, finish)


def _sibling_job(grads):
    n = len(grads)

    def start(ins, outs, sems):
        send, recv = sems
        x, y, c = _place()
        cps = [pltpu.make_async_remote_copy(
            src_ref=ins[a].at[:, 1 - c], dst_ref=outs[a], send_sem=send.at[a], recv_sem=recv.at[a],
            device_id=(x, y, 1 - c), device_id_type=MESH) for a in range(n)]
        for cp in cps:
            cp.start()
        return cps

    def finish(cps, ins, outs, sems):
        for cp in cps:
            cp.wait()

    return _Job(grads, [jax.ShapeDtypeStruct(g.shape[:1] + g.shape[2:], F32) for g in grads], {},
                [pltpu.SemaphoreType.DMA((n,)), pltpu.SemaphoreType.DMA((n,))], start, finish)


def _chip_job(psums):
    n = len(psums)

    def start(ins, outs, sems):
        send, recv, loc = sems
        x, y, c = _place()
        mychip = 2 * x + y
        cps = []
        for a in range(n):
            cps.append(pltpu.make_async_copy(ins[a].at[mychip], outs[a].at[mychip], loc.at[a]))
            for j, chip in enumerate(_chips_of(x, y)):
                cps.append(pltpu.make_async_remote_copy(
                    src_ref=ins[a].at[2 * chip[0] + chip[1]], dst_ref=outs[a].at[mychip],
                    send_sem=send.at[a, j], recv_sem=recv.at[a, j], device_id=(*chip, c), device_id_type=MESH))
        for cp in cps:
            cp.start()
        return cps

    def finish(cps, ins, outs, sems):
        send, recv, loc = sems
        x, y, c = _place()
        mychip = 2 * x + y
        for a in range(n):
            for j, chip in enumerate(_chips_of(x, y)):
                pltpu.make_async_remote_copy(
                    src_ref=ins[a].at[mychip], dst_ref=outs[a].at[2 * chip[0] + chip[1]],
                    send_sem=send.at[a, j], recv_sem=recv.at[a, j], device_id=(x, y, c), device_id_type=MESH).wait_recv()
        for a in range(n):
            cps[4 * a].wait()
            for j in range(3):
                cps[4 * a + 1 + j].wait_send()

    return _Job(psums, [jax.ShapeDtypeStruct(p.shape, BF16) for p in psums], {},
                [pltpu.SemaphoreType.DMA((n, 3)), pltpu.SemaphoreType.DMA((n, 3)), pltpu.SemaphoreType.DMA((n,))],
                start, finish)


def _join_jobs(*jobs):
    jobs = [j for j in jobs if j is not None]
    if len(jobs) <= 1:
        return jobs[0] if jobs else None
    cut = lambda seq, sizes: [seq[sum(sizes[:k]):sum(sizes[:k + 1])] for k in range(len(sizes))]
    n_in = [len(j.ins) for j in jobs]
    n_out = [len(j.out_shapes) for j in jobs]
    n_sem = [len(j.sems) for j in jobs]
    aliases = {}
    for k, j in enumerate(jobs):
        for a, b in j.aliases.items():
            aliases[sum(n_in[:k]) + a] = sum(n_out[:k]) + b

    def start(ins, outs, sems):
        return [j.start(i, o, s) for j, i, o, s in zip(jobs, cut(ins, n_in), cut(outs, n_out), cut(sems, n_sem))]

    def finish(sts, ins, outs, sems):
        for j, st, i, o, s in zip(jobs, sts, cut(ins, n_in), cut(outs, n_out), cut(sems, n_sem)):
            j.finish(st, i, o, s)

    return _Job([t for j in jobs for t in j.ins], [t for j in jobs for t in j.out_shapes], aliases,
                [t for j in jobs for t in j.sems], start, finish)


def _run_job(job, name):
    def body(ins, outs, scr, comm):
        comm[1](comm[0]())

    return _host_call(body, name, [], [], [], [], [], {}, job)[1]


def _pair_sum1(grad, got, core):
    _, _, rows, N = grad.shape

    def body(c_ref, g_ref, r_ref, o_ref):
        o_ref[...] = (g_ref[...] + r_ref[...]).astype(BF16)

    return pl.pallas_call(
        body, name="pair_sum",
        grid_spec=pltpu.PrefetchScalarGridSpec(
            num_scalar_prefetch=1, grid=(4,),
            in_specs=[pl.BlockSpec((None, None, rows, N), lambda k, c: (k, c[0], 0, 0)),
                      pl.BlockSpec((None, rows, N), lambda k, c: (k, 0, 0))],
            out_specs=pl.BlockSpec((None, rows, N), lambda k, c: (k, 0, 0))),
        out_shape=jax.ShapeDtypeStruct((4, rows, N), BF16),
        compiler_params=_cp(("parallel",)))(core, grad, got)


def _chip_sum1(parts):
    _, rows, N = parts.shape

    def body(p_ref, o_ref):
        acc = p_ref[0].astype(F32)
        for k in range(1, 4):
            acc = acc + p_ref[k].astype(F32)
        o_ref[...] = acc

    return pl.pallas_call(
        body, name="chip_sum", in_specs=[VMEM_SPEC], out_specs=VMEM_SPEC,
        out_shape=jax.ShapeDtypeStruct((rows, N), F32), compiler_params=_cp())(parts)


def _permute_in(w):
    lead = w.shape[:-1]
    return w.reshape(lead + (3, 3, 2, BQ)).swapaxes(-2, -3).reshape(lead + (QKVW,))


def _unpermute_in(w):
    lead = w.shape[:-1]
    return w.reshape(lead + (3, 2, 3, BQ)).swapaxes(-2, -3).reshape(lead + (QKVW,))


def _row(v):
    v = v.reshape(-1)
    return jnp.pad(v, (0, D - v.shape[0])).reshape(1, D)


def _step_without_overlap(x, w_in, f_bias, conv_w, w_out, rel_bias, ln1_g, ln1_b, w_gate, w_up, w_down, ln2_g, ln2_b, loss_target, m_w_in, m_f_bias, m_conv_w, m_w_out, m_rel_bias, m_ln1_g, m_ln1_b, m_w_gate, m_w_up, m_w_down, m_ln2_g, m_ln2_b, v_w_in, v_f_bias, v_conv_w, v_w_out, v_rel_bias, v_ln1_g, v_ln1_b, v_w_gate, v_w_up, v_w_down, v_ln2_g, v_ln2_b):
    xi, yi, ci = _place()
    me = 4 * xi + 2 * yi + ci

    win_s = jnp.concatenate([_permute_in(w_in[..., :QKVW]), w_in[..., QKVW:]], axis=-1)
    win_s = jnp.pad(win_s, ((0, 0), (0, 0), (0, NPAD - NPROJ))).astype(BF16)
    shards = [win_s, w_out.astype(BF16), jnp.swapaxes(w_gate, 1, 2).astype(BF16),
              jnp.swapaxes(w_up, 1, 2).astype(BF16), w_down.astype(BF16)]
    full = _allgather_weights(shards)
    Win, Wout, WgT, WuT, Wd = [f.reshape(2, NDEV * f.shape[2], f.shape[3]) for f in full]

    cw_rows = lax.dynamic_update_slice(jnp.zeros((2, 3, 256), F32), conv_w, (0, 0, me * 32))
    small = jnp.concatenate([_row(cw_rows[0]), _row(cw_rows[1]), jnp.zeros((SMALL_ROWS - 2, D), F32)], axis=0)
    small = _allreduce_small(small)
    cw_full = small[0:2, :CONVW].reshape(2, 3, 256)
    cw8 = jnp.pad(cw_full, ((0, 0), (0, 5), (0, 0)))
    fb = jnp.pad(f_bias, ((0, 0), (0, GATEW - NH))).reshape(2, 1, GATEW)
    tbl = _dil_table(rel_bias)

    def wcol(layer, K, tn, off):
        return pl.BlockSpec((None, K, tn), lambda i, j: (layer, 0, off + j))

    def arow(tm, K, blk=0):
        return pl.BlockSpec((tm, K), lambda i, j: (i, blk))

    h = x.reshape(T, D)
    hb = h.astype(BF16)
    saved = []
    for l in range(2):
        qkv = _mm([(hb, arow(512, D), Win, wcol(l, D, 768, 0))], nt=False, M=T, N=QKVW, tm=512, tn=768,
                  out_dtype=BF16, name="proj_qkv")
        conv = _mm([(hb, arow(512, D), Win, wcol(l, D, 768, 3))], nt=False, M=T, N=CONVW, tm=512, tn=768,
                   out_dtype=F32, name="proj_conv")
        gate = _mm([(hb, arow(512, D), Win, wcol(l, D, 128, 24))], nt=False, M=T, N=GATEW, tm=512, tn=128,
                   out_dtype=F32, name="proj_gate")
        cum = _fox_prep(gate, fb[l])
        cq = cum[:, :NH].reshape(BL, S, NH).transpose(1, 0, 2).reshape(S, NSTAT)
        ck = cq.T
        mixed, rtot = _sb_fwd(qkv)
        mixed, lse_d = _flash_fwd(qkv, mixed, 1, False, (tbl,))
        mixed, lse_f = _flash_fwd(qkv, mixed, 2, True, (cq, ck))
        mixed = _conv_fwd(conv, cw8[l], mixed)
        mix = _mm([(mixed, arow(512, D), Wout, wcol(l, D, 512, 0))], nt=False, M=T, N=D, tm=512, tn=512,
                  out_dtype=F32, name="out_proj")
        x1, xh1, r1, x1b = _ln_fwd(h, mix, ln1_g[l:l + 1], ln1_b[l:l + 1])
        g, u, a = _ffn_up(x1b, WgT, WuT, l)
        ffn = _mm([(a, arow(512, DFF), Wd, wcol(l, DFF, 512, 0))], nt=False, M=T, N=D, tm=512, tn=512,
                  out_dtype=F32, name="ffn_down")
        x2, xh2, r2, x2b = _ln_fwd(x1, ffn, ln2_g[l:l + 1], ln2_b[l:l + 1])
        saved.append(dict(h=hb, qkv=qkv, conv=conv, gate=gate, cq=cq, ck=ck, mixed=mixed, rtot=rtot, lse_d=lse_d,
                          lse_f=lse_f, x1=x1b, xh1=xh1, r1=r1, g=g, u=u, a=a, xh2=xh2, r2=r2))
        h, hb = x2, x2b

    sq, dy = _loss_grad(h, loss_target.reshape(T, D))
    loss = lax.psum(sq[0, 0], ("x", "y", "c")) * (0.5 / D)

    G_in = jnp.zeros((2, D, NPAD), F32)
    G_out = jnp.zeros((2, D, D), F32)
    G_g = jnp.zeros((2, DFF, D), F32)
    G_u = jnp.zeros((2, DFF, D), F32)
    G_d = jnp.zeros((2, DFF, D), F32)
    small_g = {}

    def wrow(layer, tn, K, blk=0):
        return pl.BlockSpec((None, tn, K), lambda i, j: (layer, j, blk))

    for l in (1, 0):
        sv = saved[l]
        ds2, dg2, db2, ds2b = _ln_bwd(dy, sv["xh2"], sv["r2"], ln2_g[l:l + 1])
        dgt, dut = _ffn_da(ds2b, Wd, sv["g"], sv["u"], l)
        G_d = _mm_tn(sv["a"], ds2b, G_d, Ka=DFF, N=D, tm=1408, tn=1024, tk=512, layer=l, ooff=0, name="grad_w_down")
        G_g = _mm_tn(dgt, sv["x1"], G_g, Ka=DFF, N=D, tm=1408, tn=1024, tk=512, layer=l, ooff=0, name="grad_w_gate")
        G_u = _mm_tn(dut, sv["x1"], G_u, Ka=DFF, N=D, tm=1408, tn=1024, tk=512, layer=l, ooff=0, name="grad_w_up")
        dx1 = _mm([(dgt, arow(512, DFF), WgT, wcol(l, DFF, 512, 0)), (dut, arow(512, DFF), WuT, wcol(l, DFF, 512, 0))],
                  nt=False, M=T, N=D, tm=512, tn=512, out_dtype=F32, name="ffn_dx", res=ds2, res_scale=ALPHA)
        ds1, dg1, db1, ds1b = _ln_bwd(dx1, sv["xh1"], sv["r1"], ln1_g[l:l + 1])
        G_out = _mm_tn(sv["mixed"], ds1b, G_out, Ka=D, N=D, tm=1024, tn=1024, tk=512, layer=l, ooff=0,
                       name="grad_w_out")
        dmixed = _mm([(ds1b, arow(512, D), Wout, wrow(l, 512, D))], nt=True, M=T, N=D, tm=512, tn=512,
                     out_dtype=BF16, name="out_proj_dx")
        dqkv = _sb_bwd(sv["qkv"], dmixed, sv["rtot"])
        dqkv, dtbl = _flash_bwd(sv["qkv"], sv["mixed"], dmixed, sv["lse_d"], dqkv, 1, False, (tbl,))
        dqkv, dck = _flash_bwd(sv["qkv"], sv["mixed"], dmixed, sv["lse_f"], dqkv, 2, True, (sv["cq"], sv["ck"]))
        dconv, dcw = _conv_bwd(sv["conv"], cw8[l], dmixed)
        dcum = jnp.pad(dck.reshape(BL, NH, S).transpose(0, 2, 1).reshape(T, NH), ((0, 0), (0, GATEW - NH)))
        dgate, dfb = _fox_post(dcum, sv["gate"], fb[l])
        drb = _dil_table_bwd(dtbl)
        G_in = _mm_tn(sv["h"], dqkv, G_in, Ka=D, N=QKVW, tm=1024, tn=768, tk=512, layer=l, ooff=0, name="grad_w_in_qkv")
        G_in = _mm_tn(sv["h"], dconv, G_in, Ka=D, N=CONVW, tm=1024, tn=768, tk=512, layer=l, ooff=3,
                      name="grad_w_in_conv")
        G_in = _mm_tn(sv["h"], dgate, G_in, Ka=D, N=GATEW, tm=1024, tn=128, tk=512, layer=l, ooff=24,
                      name="grad_w_in_gate")
        dy = _mm([(dqkv, arow(512, QKVW), Win, wrow(l, 512, QKVW, 0)),
                  (dconv, arow(512, CONVW), Win, wrow(l, 512, CONVW, 3)),
                  (dgate, arow(512, GATEW), Win, wrow(l, 512, GATEW, 24))],
                 nt=True, M=T, N=D, tm=512, tn=512, out_dtype=F32, name="proj_dx", res=ds1, res_scale=ALPHA)
        small_g[l] = dict(ln1_g=dg1, ln1_b=db1, ln2_g=dg2, ln2_b=db2, cw=dcw[0:3].reshape(1, CONVW),
                          fb=dfb[:, :NH], rb=drb[:, :NH])
    grad_x = dy.reshape(BL, S, D)

    rows = []
    for name in ("ln1_g", "ln1_b", "ln2_g", "ln2_b"):
        rows += [small_g[0][name], small_g[1][name]]
    rows += [_row(small_g[0]["cw"]), _row(small_g[1]["cw"]),
             _row(jnp.concatenate([small_g[0]["fb"], small_g[1]["fb"]], axis=0)),
             _row(small_g[0]["rb"] + small_g[1]["rb"])]
    rows.append(jnp.zeros((SMALL_ROWS - len(rows), D), F32))
    sg = _allreduce_small(jnp.concatenate(rows, axis=0))
    g_ln1_g, g_ln1_b, g_ln2_g, g_ln2_b = sg[0:2], sg[2:4], sg[4:6], sg[6:8]
    g_conv_full = sg[8:10, :CONVW].reshape(2, 3, 256)
    g_conv = lax.dynamic_slice(g_conv_full, (0, 0, me * 32), (2, 3, 32))
    g_fb = sg[10, :2 * NH].reshape(2, NH)
    g_rb = sg[11, :32 * NH].reshape(32, NH)

    bufs = [G_in, G_out, G_g, G_u, G_d]
    views = [b.reshape(2, 4, 2, b.shape[1] // NDEV, b.shape[2]) for b in bufs]
    got = _sibling_exchange(views)
    core = jnp.reshape(ci, (1,)).astype(jnp.int32)
    psums = [_pair_sum(vw, gt, core) for vw, gt in zip(views, got)]
    parts = _chip_exchange(psums)
    gs = [_chip_sum(p) for p in parts]
    g_in = gs[0]
    g_w_in = jnp.concatenate([_unpermute_in(g_in[..., :QKVW]), g_in[..., QKVW:NPROJ]], axis=-1)
    g_w_out = gs[1]
    g_w_gate = jnp.swapaxes(gs[2], 1, 2)
    g_w_up = jnp.swapaxes(gs[3], 1, 2)
    g_w_down = gs[4]

    def big(w, g, m, v, tr):
        sh = w.shape
        f = lambda t: t.reshape(-1, sh[-1])
        return [t.reshape(sh) for t in _adamw(f(w), f(g), f(m), f(v), tr)]

    up_in = big(w_in, g_w_in, m_w_in, v_w_in, 64)
    up_out = big(w_out, g_w_out, m_w_out, v_w_out, 128)
    up_gate = big(w_gate, g_w_gate, m_w_gate, v_w_gate, 256)
    up_up = big(w_up, g_w_up, m_w_up, v_w_up, 256)
    up_down = big(w_down, g_w_down, m_w_down, v_w_down, 352)

    def pack(fbv, cwv, rbv, l1g, l1b, l2g, l2b):
        r = [l1g, l1b, l2g, l2b, _row(cwv), _row(fbv), _row(rbv)]
        r.append(jnp.zeros((SMALL_ROWS - 11, D), F32))
        return jnp.concatenate(r, axis=0)

    pw = pack(f_bias, conv_w, rel_bias, ln1_g, ln1_b, ln2_g, ln2_b)
    pg = pack(g_fb, g_conv, g_rb, g_ln1_g, g_ln1_b, g_ln2_g, g_ln2_b)
    pm = pack(m_f_bias, m_conv_w, m_rel_bias, m_ln1_g, m_ln1_b, m_ln2_g, m_ln2_b)
    pv = pack(v_f_bias, v_conv_w, v_rel_bias, v_ln1_g, v_ln1_b, v_ln2_g, v_ln2_b)
    ups = _adamw(pw, pg, pm, pv, SMALL_ROWS)

    def unpack(p):
        return dict(ln1_g=p[0:2], ln1_b=p[2:4], ln2_g=p[4:6], ln2_b=p[6:8],
                    conv_w=p[8, :192].reshape(2, 3, 32), f_bias=p[9, :2 * NH].reshape(2, NH),
                    rel_bias=p[10, :32 * NH].reshape(32, NH))

    sm = [unpack(p) for p in ups]

    def group(k):
        return (up_in[k], sm[k]["f_bias"], sm[k]["conv_w"], up_out[k], sm[k]["rel_bias"], sm[k]["ln1_g"],
                sm[k]["ln1_b"], up_gate[k], up_up[k], up_down[k], sm[k]["ln2_g"], sm[k]["ln2_b"])

    grads = (g_w_in, g_fb, g_conv, g_w_out, g_rb, g_ln1_g, g_ln1_b, g_w_gate, g_w_up, g_w_down, g_ln2_g, g_ln2_b)
    return (loss, grad_x) + grads + group(0) + group(1) + group(2)


def _pair_sums(views, gots, core):
    n = len(views)

    def body(c_ref, *refs):
        for a in range(n):
            refs[2 * n + a][...] = (refs[a][...] + refs[n + a][...]).astype(BF16)

    def vspec(v):
        return pl.BlockSpec((None, None) + v.shape[2:], lambda k, c: (k, c[0], 0, 0))

    def gspec(g):
        return pl.BlockSpec((None,) + g.shape[1:], lambda k, c: (k, 0, 0))

    return pl.pallas_call(
        body, name="pair_sums",
        grid_spec=pltpu.PrefetchScalarGridSpec(
            num_scalar_prefetch=1, grid=(4,),
            in_specs=[vspec(v) for v in views] + [gspec(g) for g in gots],
            out_specs=[gspec(g) for g in gots]),
        out_shape=[jax.ShapeDtypeStruct(g.shape, BF16) for g in gots],
        compiler_params=_cp(("parallel",)))(core, *views, *gots)


def _chip_sums(parts):
    n = len(parts)

    def body(*refs):
        for a in range(n):
            acc = refs[a][0].astype(F32)
            for k in range(1, 4):
                acc = acc + refs[a][k].astype(F32)
            refs[n + a][...] = acc

    return pl.pallas_call(
        body, name="chip_sums", in_specs=[VMEM_SPEC] * n, out_specs=[VMEM_SPEC] * n,
        out_shape=[jax.ShapeDtypeStruct(p.shape[1:], F32) for p in parts], compiler_params=_cp())(*parts)


def kernel(x, w_in, f_bias, conv_w, w_out, rel_bias, ln1_g, ln1_b, w_gate, w_up, w_down, ln2_g, ln2_b, loss_target, m_w_in, m_f_bias, m_conv_w, m_w_out, m_rel_bias, m_ln1_g, m_ln1_b, m_w_gate, m_w_up, m_w_down, m_ln2_g, m_ln2_b, v_w_in, v_f_bias, v_conv_w, v_w_out, v_rel_bias, v_ln1_g, v_ln1_b, v_w_gate, v_w_up, v_w_down, v_ln2_g, v_ln2_b):
    xi, yi, ci = _place()
    me = 4 * xi + 2 * yi + ci
    core = jnp.reshape(ci, (1,)).astype(jnp.int32)

    win_s = jnp.concatenate([_permute_in(w_in[..., :QKVW]), w_in[..., QKVW:]], axis=-1)
    win_s = jnp.pad(win_s, ((0, 0), (0, 0), (0, NPAD - NPROJ))).astype(BF16)
    per_layer = [win_s, w_out.astype(BF16), jnp.swapaxes(w_gate, 1, 2).astype(BF16),
                 jnp.swapaxes(w_up, 1, 2).astype(BF16), w_down.astype(BF16)]

    def shards(l):
        return [s[l] for s in per_layer]

    def whole(g):
        return g.reshape(NDEV * g.shape[1], g.shape[2])

    sh = [shards(0), shards(1)]
    first = _run_job(_gather_job_b(_run_job(_gather_job_a(sh[0][:1]), "gather_a")), "gather_b")
    W = [{"win": whole(first[0])}, {}]

    cw_rows = lax.dynamic_update_slice(jnp.zeros((2, 3, 256), F32), conv_w, (0, 0, me * 32))
    small = jnp.concatenate([_row(cw_rows[0]), _row(cw_rows[1]), jnp.zeros((SMALL_ROWS - 2, D), F32)], axis=0)
    small = _allreduce_small(small)
    cw_full = small[0:2, :CONVW].reshape(2, 3, 256)
    cw8 = jnp.pad(cw_full, ((0, 0), (0, 5), (0, 0)))
    fb = jnp.pad(f_bias, ((0, 0), (0, GATEW - NH))).reshape(2, 1, GATEW)
    tbl = _dil_table(rel_bias)

    def wcol(K, tn, off):
        return pl.BlockSpec((K, tn), lambda i, j: (0, off + j))

    def wrow(tn, K, blk=0):
        return pl.BlockSpec((tn, K), lambda i, j: (j, blk))

    def arow(tm, K, blk=0):
        return pl.BlockSpec((tm, K), lambda i, j: (i, blk))

    h = x.reshape(T, D)
    hb = h.astype(BF16)
    saved = []
    for l in range(2):
        Win = W[l]["win"]
        qkv = _mm([(hb, arow(512, D), Win, wcol(D, 768, 0))], nt=False, M=T, N=QKVW, tm=512, tn=768,
                  out_dtype=BF16, name="proj_qkv")
        conv = _mm([(hb, arow(512, D), Win, wcol(D, 768, 3))], nt=False, M=T, N=CONVW, tm=512, tn=768,
                   out_dtype=F32, name="proj_conv")
        gate = _mm([(hb, arow(512, D), Win, wcol(D, 128, 24))], nt=False, M=T, N=GATEW, tm=512, tn=128,
                   out_dtype=F32, name="proj_gate")
        cum = _fox_prep(gate, fb[l])
        cq = cum[:, :NH].reshape(BL, S, NH).transpose(1, 0, 2).reshape(S, NSTAT)
        ck = cq.T
        if l == 0:
            mixed, rtot, a0 = _sb_fwd(qkv, job=_gather_job_a(sh[0][1:]))
            mixed, lse_d, ex = _flash_fwd(qkv, mixed, 1, False, (tbl,),
                                          job=_join_jobs(_gather_job_b(list(a0)), _gather_job_a(sh[1][:2])))
            W[0].update(zip(("wout", "wgT", "wuT", "wd"), [whole(t) for t in ex[:4]]))
            mixed, lse_f, o_fox, ex = _flash_fwd(qkv, mixed, 2, True, (cq, ck),
                                                 job=_join_jobs(_gather_job_b(list(ex[4:])), _gather_job_a(sh[1][2:])))
            W[1].update(zip(("win", "wout"), [whole(t) for t in ex[:2]]))
            a2 = list(ex[2:])
        else:
            mixed, rtot, ex = _sb_fwd(qkv, job=_gather_job_b(a2))
            W[1].update(zip(("wgT", "wuT", "wd"), [whole(t) for t in ex]))
            mixed, lse_d, _ = _flash_fwd(qkv, mixed, 1, False, (tbl,))
            mixed, lse_f, o_fox, _ = _flash_fwd(qkv, mixed, 2, True, (cq, ck))
        Wout, WgT, WuT, Wd = W[l]["wout"], W[l]["wgT"], W[l]["wuT"], W[l]["wd"]
        mixed = _conv_fwd(conv, cw8[l], mixed)
        mix = _mm([(mixed, arow(512, D), Wout, wcol(D, 512, 0))], nt=False, M=T, N=D, tm=512, tn=512,
                  out_dtype=F32, name="out_proj")
        x1, xh1, r1, x1b = _ln_fwd(h, mix, ln1_g[l:l + 1], ln1_b[l:l + 1])
        g, u, a = _ffn_up(x1b, WgT[None], WuT[None], 0)
        ffn = _mm([(a, arow(512, DFF), Wd, wcol(DFF, 512, 0))], nt=False, M=T, N=D, tm=512, tn=512,
                  out_dtype=F32, name="ffn_down")
        x2, xh2, r2, x2b = _ln_fwd(x1, ffn, ln2_g[l:l + 1], ln2_b[l:l + 1])
        saved.append(dict(h=hb, qkv=qkv, conv=conv, gate=gate, cq=cq, ck=ck, mixed=mixed, rtot=rtot, lse_d=lse_d,
                          lse_f=lse_f, o_fox=o_fox, x1=x1b, xh1=xh1, r1=r1, g=g, u=u, a=a, xh2=xh2, r2=r2))
        h, hb = x2, x2b

    sq, dy = _loss_grad(h, loss_target.reshape(T, D))
    loss = lax.psum(sq[0, 0], ("x", "y", "c")) * (0.5 / D)

    def view(gr):
        return gr.reshape(4, 2, gr.shape[0] // NDEV, gr.shape[1])

    def reduce_tail(views, gots):
        return _chip_sums(_run_job(_chip_job(_pair_sums(views, gots, core)), "chip_exchange"))

    G = [None, None]
    small_g = {}
    shard_g = {}
    for l in (1, 0):
        sv = saved[l]
        Win, Wout, WgT, WuT, Wd = W[l]["win"], W[l]["wout"], W[l]["wgT"], W[l]["wuT"], W[l]["wd"]
        ds2, dg2, db2, ds2b = _ln_bwd(dy, sv["xh2"], sv["r2"], ln2_g[l:l + 1])
        dgt, dut = _ffn_da(ds2b, Wd[None], sv["g"], sv["u"], 0)
        G_d = _mm_tn(sv["a"], ds2b, None, C=D, Ka=DFF, N=D, tm=1408, tn=1024, tk=512, ooff=0, name="grad_w_down")
        G_g = _mm_tn(dgt, sv["x1"], None, C=D, Ka=DFF, N=D, tm=1408, tn=1024, tk=512, ooff=0, name="grad_w_gate")
        G_u = _mm_tn(dut, sv["x1"], None, C=D, Ka=DFF, N=D, tm=1408, tn=1024, tk=512, ooff=0, name="grad_w_up")
        dx1 = _mm([(dgt, arow(512, DFF), WgT, wcol(DFF, 512, 0)), (dut, arow(512, DFF), WuT, wcol(DFF, 512, 0))],
                  nt=False, M=T, N=D, tm=512, tn=512, out_dtype=F32, name="ffn_dx", res=ds2, res_scale=ALPHA)
        ds1, dg1, db1, ds1b = _ln_bwd(dx1, sv["xh1"], sv["r1"], ln1_g[l:l + 1])
        G_out = _mm_tn(sv["mixed"], ds1b, None, C=D, Ka=D, N=D, tm=1024, tn=1024, tk=512, ooff=0, name="grad_w_out")
        dmixed = _mm([(ds1b, arow(512, D), Wout, wrow(512, D))], nt=True, M=T, N=D, tm=512, tn=512,
                     out_dtype=BF16, name="out_proj_dx")
        if l == 0:
            early = [view(t) for t in (G[1]["in"], G[1]["out"], G[1]["g"], G[1]["u"], G[1]["d"], G_g, G_u, G_d, G_out)]
            dqkv, gots = _sb_bwd(sv["qkv"], dmixed, sv["rtot"], job=_sibling_job(early))
            ps1 = _pair_sums(early[:5], list(gots[:5]), core)
            ps0 = _pair_sums(early[5:], list(gots[5:]), core)
            dqkv, dtbl, parts0 = _flash_bwd(sv["qkv"], sv["mixed"], dmixed, sv["lse_d"], dqkv, 1, False, (tbl,),
                                            job=_chip_job(ps0))
            dqkv, dck, parts1 = _flash_bwd(sv["qkv"], sv["o_fox"], dmixed, sv["lse_f"], dqkv, 2, True,
                                           (sv["cq"], sv["ck"]), job=_chip_job(ps1))
            s1 = _chip_sums(list(parts1))
            s0 = _chip_sums(list(parts0))
            shard_g[1] = dict(zip(("in", "out", "g", "u", "d"), s1))
            shard_g[0] = dict(zip(("g", "u", "d", "out"), s0))
        else:
            dqkv, _ = _sb_bwd(sv["qkv"], dmixed, sv["rtot"])
            dqkv, dtbl, _ = _flash_bwd(sv["qkv"], sv["mixed"], dmixed, sv["lse_d"], dqkv, 1, False, (tbl,))
            dqkv, dck, _ = _flash_bwd(sv["qkv"], sv["o_fox"], dmixed, sv["lse_f"], dqkv, 2, True, (sv["cq"], sv["ck"]))
        dconv, dcw = _conv_bwd(sv["conv"], cw8[l], dmixed)
        dcum = jnp.pad(dck.reshape(BL, NH, S).transpose(0, 2, 1).reshape(T, NH), ((0, 0), (0, GATEW - NH)))
        dgate, dfb = _fox_post(dcum, sv["gate"], fb[l])
        drb = _dil_table_bwd(dtbl)
        G_in = _mm_tn(sv["h"], dqkv, None, C=NPAD, Ka=D, N=QKVW, tm=1024, tn=768, tk=512, ooff=0, name="grad_w_in_qkv")
        G_in = _mm_tn(sv["h"], dconv, G_in, C=NPAD, Ka=D, N=CONVW, tm=1024, tn=768, tk=512, ooff=3,
                      name="grad_w_in_conv")
        G_in = _mm_tn(sv["h"], dgate, G_in, C=NPAD, Ka=D, N=GATEW, tm=1024, tn=128, tk=512, ooff=24,
                      name="grad_w_in_gate")
        G[l] = {"in": G_in, "out": G_out, "g": G_g, "u": G_u, "d": G_d}
        dy = _mm([(dqkv, arow(512, QKVW), Win, wrow(512, QKVW, 0)),
                  (dconv, arow(512, CONVW), Win, wrow(512, CONVW, 3)),
                  (dgate, arow(512, GATEW), Win, wrow(512, GATEW, 24))],
                 nt=True, M=T, N=D, tm=512, tn=512, out_dtype=F32, name="proj_dx", res=ds1, res_scale=ALPHA)
        small_g[l] = dict(ln1_g=dg1, ln1_b=db1, ln2_g=dg2, ln2_b=db2, cw=dcw[0:3].reshape(1, CONVW),
                          fb=dfb[:, :NH], rb=drb[:, :NH])
    grad_x = dy.reshape(BL, S, D)

    late = [view(G[0]["in"])]
    shard_g[0]["in"] = reduce_tail(late, list(_run_job(_sibling_job(late), "sibling_exchange")))[0]

    rows = []
    for name in ("ln1_g", "ln1_b", "ln2_g", "ln2_b"):
        rows += [small_g[0][name], small_g[1][name]]
    rows += [_row(small_g[0]["cw"]), _row(small_g[1]["cw"]),
             _row(jnp.concatenate([small_g[0]["fb"], small_g[1]["fb"]], axis=0)),
             _row(small_g[0]["rb"] + small_g[1]["rb"])]
    rows.append(jnp.zeros((SMALL_ROWS - len(rows), D), F32))
    sg = _allreduce_small(jnp.concatenate(rows, axis=0))
    g_ln1_g, g_ln1_b, g_ln2_g, g_ln2_b = sg[0:2], sg[2:4], sg[4:6], sg[6:8]
    g_conv_full = sg[8:10, :CONVW].reshape(2, 3, 256)
    g_conv = lax.dynamic_slice(g_conv_full, (0, 0, me * 32), (2, 3, 32))
    g_fb = sg[10, :2 * NH].reshape(2, NH)
    g_rb = sg[11, :32 * NH].reshape(32, NH)

    def both(name):
        return jnp.stack([shard_g[0][name], shard_g[1][name]])

    g_in = both("in")
    g_w_in = jnp.concatenate([_unpermute_in(g_in[..., :QKVW]), g_in[..., QKVW:NPROJ]], axis=-1)
    g_w_out = both("out")
    g_w_gate = jnp.swapaxes(both("g"), 1, 2)
    g_w_up = jnp.swapaxes(both("u"), 1, 2)
    g_w_down = both("d")

    def big(w, g, m, v, tr):
        sh = w.shape
        f = lambda t: t.reshape(-1, sh[-1])
        return [t.reshape(sh) for t in _adamw(f(w), f(g), f(m), f(v), tr)]

    up_in = big(w_in, g_w_in, m_w_in, v_w_in, 64)
    up_out = big(w_out, g_w_out, m_w_out, v_w_out, 128)
    up_gate = big(w_gate, g_w_gate, m_w_gate, v_w_gate, 256)
    up_up = big(w_up, g_w_up, m_w_up, v_w_up, 256)
    up_down = big(w_down, g_w_down, m_w_down, v_w_down, 352)

    def pack(fbv, cwv, rbv, l1g, l1b, l2g, l2b):
        r = [l1g, l1b, l2g, l2b, _row(cwv), _row(fbv), _row(rbv)]
        r.append(jnp.zeros((SMALL_ROWS - 11, D), F32))
        return jnp.concatenate(r, axis=0)

    pw = pack(f_bias, conv_w, rel_bias, ln1_g, ln1_b, ln2_g, ln2_b)
    pg = pack(g_fb, g_conv, g_rb, g_ln1_g, g_ln1_b, g_ln2_g, g_ln2_b)
    pm = pack(m_f_bias, m_conv_w, m_rel_bias, m_ln1_g, m_ln1_b, m_ln2_g, m_ln2_b)
    pv = pack(v_f_bias, v_conv_w, v_rel_bias, v_ln1_g, v_ln1_b, v_ln2_g, v_ln2_b)
    ups = _adamw(pw, pg, pm, pv, SMALL_ROWS)

    def unpack(p):
        return dict(ln1_g=p[0:2], ln1_b=p[2:4], ln2_g=p[4:6], ln2_b=p[6:8],
                    conv_w=p[8, :192].reshape(2, 3, 32), f_bias=p[9, :2 * NH].reshape(2, NH),
                    rel_bias=p[10, :32 * NH].reshape(32, NH))

    sm = [unpack(p) for p in ups]

    def group(k):
        return (up_in[k], sm[k]["f_bias"], sm[k]["conv_w"], up_out[k], sm[k]["rel_bias"], sm[k]["ln1_g"],
                sm[k]["ln1_b"], up_gate[k], up_up[k], up_down[k], sm[k]["ln2_g"], sm[k]["ln2_b"])

    grads = (g_w_in, g_fb, g_conv, g_w_out, g_rb, g_ln1_g, g_ln1_b, g_w_gate, g_w_up, g_w_down, g_ln2_g, g_ln2_b)
    return (loss, grad_x) + grads + group(0) + group(1) + group(2)
```

```python
import math

import numpy as np
import jax
import jax.numpy as jnp
from jax import lax
from jax.experimental import pallas as pl
from jax.experimental.pallas import tpu as pltpu

F32 = jnp.float32
BF16 = jnp.bfloat16
MESH = pl.DeviceIdType.MESH

D = 1024
S = 2048
BL = 2
T = BL * S
NH = 4
DFF = 2816
NPROJ = 3076
NPAD = 3200
QKVW = 2304
CONVW = 768
GATEW = 128
PAIRW = 384
BQ = 128
HB = 2 * BQ
NB = S // BQ
NDEV = 8
NSTAT = BL * NH
ALPHA = 4.0 ** 0.25
SCALE = 0.125
NEG = -1e30
LN_EPS = 1e-5
ADAM_LR, ADAM_B1, ADAM_B2, ADAM_EPS, ADAM_WD, ADAM_STEP = 0.001, 0.9, 0.999, 1e-08, 0.01, 10
VMEM_LIMIT = 48 * 1024 * 1024
SMALL_ROWS = 16


def _bucket_thresholds():
    d = np.arange(0, S)
    nf = np.maximum(d, 1).astype(np.float32)
    large = 16 + (np.log(nf / np.float32(16)) / np.float32(math.log(128)) * np.float32(16)).astype(np.int32)
    b = np.where(d < 16, d, np.minimum(large, 31))
    return [int(np.argmax(b >= k)) for k in range(32)]


BUCKET_TH = _bucket_thresholds()


def _cp(sem=None):
    return pltpu.CompilerParams(dimension_semantics=sem, vmem_limit_bytes=VMEM_LIMIT)


def _dot(a, b):
    return lax.dot_general(a, b, (((1,), (0,)), ((), ())), preferred_element_type=F32)


def _dot_nt(a, b):
    return lax.dot_general(a, b, (((1,), (1,)), ((), ())), preferred_element_type=F32)


def _dot_tn(a, b):
    return lax.dot_general(a, b, (((0,), (0,)), ((), ())), preferred_element_type=F32)


def _split2(x):
    hi = x.astype(BF16)
    mid = (x - hi.astype(F32)).astype(BF16)
    return jnp.concatenate([hi, mid], axis=1)


def _split3(x):
    hi = x.astype(BF16)
    r = x - hi.astype(F32)
    mid = r.astype(BF16)
    lo = (r - mid.astype(F32)).astype(BF16)
    return jnp.concatenate([hi, mid, lo], axis=1)


def _log_sigmoid(u):
    return jnp.minimum(u, 0.0) - jnp.log1p(jnp.exp(-jnp.abs(u)))


def _log_sigmoid_tile(u):
    return jnp.minimum(u, 0.0) - jnp.log(1.0 + jnp.exp(jnp.minimum(u, -u)))


def _iota(shape, dim):
    return lax.broadcasted_iota(jnp.int32, shape, dim)


ANY_SPEC = pl.BlockSpec(memory_space=pl.ANY)
VMEM_SPEC = pl.BlockSpec(memory_space=pltpu.VMEM)


def _mm(pairs, *, nt, M, N, tm, tn, out_dtype, name, res=None, res_scale=1.0):
    n = len(pairs)

    def body(*refs):
        acc = None
        for p in range(n):
            a = refs[2 * p][...].astype(BF16)
            b = refs[2 * p + 1][...]
            d = _dot_nt(a, b) if nt else _dot(a, b)
            acc = d if acc is None else acc + d
        if res is not None:
            acc = acc + res_scale * refs[2 * n][...]
        refs[-1][...] = acc.astype(out_dtype)

    ops, specs = [], []
    for a, asp, b, bsp in pairs:
        ops += [a, b]
        specs += [asp, bsp]
    if res is not None:
        ops.append(res)
        specs.append(pl.BlockSpec((tm, tn), lambda i, j: (i, j)))
    return pl.pallas_call(
        body, name=name, grid=(M // tm, N // tn), in_specs=specs,
        out_specs=pl.BlockSpec((tm, tn), lambda i, j: (i, j)),
        out_shape=jax.ShapeDtypeStruct((M, N), out_dtype),
        compiler_params=_cp(("parallel", "parallel")))(*ops)


def _mm_tn(a, b, gbuf, *, C, Ka, N, tm, tn, tk, ooff, name):
    def body(*refs):
        a_ref, b_ref, o_ref = refs[0], refs[1], refs[-1]
        k = pl.program_id(2)
        d = _dot_tn(a_ref[...].astype(BF16), b_ref[...].astype(BF16))

        @pl.when(k == 0)
        def _():
            o_ref[...] = d

        @pl.when(k > 0)
        def _():
            o_ref[...] += d

    ops = [a, b] + ([] if gbuf is None else [gbuf])
    return pl.pallas_call(
        body, name=name, grid=(Ka // tm, N // tn, T // tk),
        in_specs=[pl.BlockSpec((tk, tm), lambda i, j, k: (k, i)),
                  pl.BlockSpec((tk, tn), lambda i, j, k: (k, j))] + ([] if gbuf is None else [ANY_SPEC]),
        out_specs=pl.BlockSpec((tm, tn), lambda i, j, k: (i, ooff + j)),
        out_shape=jax.ShapeDtypeStruct((Ka, C), F32),
        input_output_aliases={} if gbuf is None else {2: 0},
        compiler_params=_cp(("parallel", "parallel", "arbitrary")))(*ops)


def _ffn_up(x1, wgt, wut, layer):
    tm, tn = 1024, 256

    def body(x_ref, wg_ref, wu_ref, s_ref, t_ref, a_ref):
        xb = x_ref[...]
        g = _dot_nt(xb, wg_ref[...])
        u = _dot_nt(xb, wu_ref[...])
        sg = jax.nn.sigmoid(g)
        s = g * sg
        s_ref[...] = s.astype(BF16)
        t_ref[...] = (u * (sg * (1.0 + g * (1.0 - sg)))).astype(BF16)
        a_ref[...] = (s * u).astype(BF16)

    wspec = pl.BlockSpec((None, tn, D), lambda i, j: (layer, j, 0))
    ospec = pl.BlockSpec((tm, tn), lambda i, j: (i, j))
    return pl.pallas_call(
        body, name="ffn_up", grid=(T // tm, DFF // tn),
        in_specs=[pl.BlockSpec((tm, D), lambda i, j: (i, 0)), wspec, wspec],
        out_specs=[ospec, ospec, ospec],
        out_shape=[jax.ShapeDtypeStruct((T, DFF), BF16)] * 3,
        compiler_params=_cp(("parallel", "parallel")))(x1, wgt, wut)


def _ffn_da(dffn, wd, g, u, layer):
    tm, tn = 1024, 256

    def body(d_ref, wd_ref, s_ref, t_ref, dg_ref, du_ref):
        da = _dot_nt(d_ref[...], wd_ref[...])
        dg_ref[...] = (da * t_ref[...].astype(F32)).astype(BF16)
        du_ref[...] = (da * s_ref[...].astype(F32)).astype(BF16)

    ospec = pl.BlockSpec((tm, tn), lambda i, j: (i, j))
    return pl.pallas_call(
        body, name="ffn_da", grid=(T // tm, DFF // tn),
        in_specs=[pl.BlockSpec((tm, D), lambda i, j: (i, 0)),
                  pl.BlockSpec((None, tn, D), lambda i, j: (layer, j, 0)), ospec, ospec],
        out_specs=[ospec, ospec],
        out_shape=[jax.ShapeDtypeStruct((T, DFF), BF16), jax.ShapeDtypeStruct((T, DFF), BF16)],
        compiler_params=_cp(("parallel", "parallel")))(dffn, wd, g, u)


def _ln_fwd(x, f, gam, bet):
    tm = 256

    def body(x_ref, f_ref, g_ref, b_ref, y_ref, xh_ref, r_ref, yb_ref):
        s = ALPHA * x_ref[...] + f_ref[...]
        mu = jnp.mean(s, axis=-1, keepdims=True)
        xc = s - mu
        var = jnp.mean(xc * xc, axis=-1, keepdims=True)
        r = lax.rsqrt(var + LN_EPS)
        xh = xc * r
        xh_ref[...] = xh
        r_ref[...] = r
        y = xh * g_ref[...] + b_ref[...]
        y_ref[...] = y
        yb_ref[...] = y.astype(BF16)

    row = pl.BlockSpec((tm, D), lambda i: (i, 0))
    vec = pl.BlockSpec((1, D), lambda i: (0, 0))
    return pl.pallas_call(
        body, name="ln_fwd", grid=(T // tm,), in_specs=[row, row, vec, vec],
        out_specs=[row, row, pl.BlockSpec((tm, 1), lambda i: (i, 0)), row],
        out_shape=[jax.ShapeDtypeStruct((T, D), F32), jax.ShapeDtypeStruct((T, D), F32),
                   jax.ShapeDtypeStruct((T, 1), F32), jax.ShapeDtypeStruct((T, D), BF16)],
        compiler_params=_cp(("parallel",)))(x, f, gam, bet)


def _mm_ln(a, w, x, gam, bet, name):
    tm = 256
    K = a.shape[1]

    def body(a_ref, w_ref, x_ref, g_ref, b_ref, y_ref, xh_ref, r_ref, yb_ref):
        s = ALPHA * x_ref[...] + _dot(a_ref[...], w_ref[...])
        mu = jnp.mean(s, axis=-1, keepdims=True)
        xc = s - mu
        var = jnp.mean(xc * xc, axis=-1, keepdims=True)
        r = lax.rsqrt(var + LN_EPS)
        xh = xc * r
        xh_ref[...] = xh
        r_ref[...] = r
        y = xh * g_ref[...] + b_ref[...]
        y_ref[...] = y
        yb_ref[...] = y.astype(BF16)

    row = pl.BlockSpec((tm, D), lambda i: (i, 0))
    vec = pl.BlockSpec((1, D), lambda i: (0, 0))
    return pl.pallas_call(
        body, name=name, grid=(T // tm,),
        in_specs=[pl.BlockSpec((tm, K), lambda i: (i, 0)), pl.BlockSpec((K, D), lambda i: (0, 0)), row, vec, vec],
        out_specs=[row, row, pl.BlockSpec((tm, 1), lambda i: (i, 0)), row],
        out_shape=[jax.ShapeDtypeStruct((T, D), F32), jax.ShapeDtypeStruct((T, D), F32),
                   jax.ShapeDtypeStruct((T, 1), F32), jax.ShapeDtypeStruct((T, D), BF16)],
        compiler_params=_cp(("parallel",)))(a, w, x, gam, bet)


def _ln_bwd(dy, xh, r, gam):
    tm = 256

    def body(dy_ref, xh_ref, r_ref, g_ref, ds_ref, dg_ref, db_ref, dsb_ref):
        i = pl.program_id(0)
        dyv = dy_ref[...]
        xhv = xh_ref[...]
        dxh = dyv * g_ref[...]
        m1 = jnp.mean(dxh, axis=-1, keepdims=True)
        m2 = jnp.mean(dxh * xhv, axis=-1, keepdims=True)
        ds = r_ref[...] * (dxh - m1 - xhv * m2)
        ds_ref[...] = ds
        dsb_ref[...] = ds.astype(BF16)
        pg = jnp.sum(dyv * xhv, axis=0, keepdims=True)
        pb = jnp.sum(dyv, axis=0, keepdims=True)

        @pl.when(i == 0)
        def _():
            dg_ref[...] = pg
            db_ref[...] = pb

        @pl.when(i > 0)
        def _():
            dg_ref[...] += pg
            db_ref[...] += pb

    row = pl.BlockSpec((tm, D), lambda i: (i, 0))
    vec = pl.BlockSpec((1, D), lambda i: (0, 0))
    return pl.pallas_call(
        body, name="ln_bwd", grid=(T // tm,),
        in_specs=[row, row, pl.BlockSpec((tm, 1), lambda i: (i, 0)), vec],
        out_specs=[row, vec, vec, row],
        out_shape=[jax.ShapeDtypeStruct((T, D), F32), jax.ShapeDtypeStruct((1, D), F32),
                   jax.ShapeDtypeStruct((1, D), F32), jax.ShapeDtypeStruct((T, D), BF16)],
        compiler_params=_cp(("arbitrary",)))(dy, xh, r, gam)


def _loss_grad(y, tgt):
    tm = 256

    def body(y_ref, t_ref, l_ref, dy_ref):
        i = pl.program_id(0)
        e = y_ref[...] - t_ref[...]
        dy_ref[...] = e * (1.0 / D)
        p = jnp.sum(jnp.sum(e * e, axis=1, keepdims=True), axis=0, keepdims=True)

        @pl.when(i == 0)
        def _():
            l_ref[...] = p

        @pl.when(i > 0)
        def _():
            l_ref[...] += p

    row = pl.BlockSpec((tm, D), lambda i: (i, 0))
    return pl.pallas_call(
        body, name="loss_grad", grid=(T // tm,), in_specs=[row, row],
        out_specs=[pl.BlockSpec((1, 1), lambda i: (0, 0)), row],
        out_shape=[jax.ShapeDtypeStruct((1, 1), F32), jax.ShapeDtypeStruct((T, D), F32)],
        compiler_params=_cp(("arbitrary",)))(y, tgt)


def _adamw(w, g, m, v, tr):
    R, C = w.shape

    def body(w_ref, g_ref, m_ref, v_ref, d_ref, m2_ref, v2_ref):
        gv = g_ref[...]
        m2 = ADAM_B1 * m_ref[...] + (1.0 - ADAM_B1) * gv
        v2 = ADAM_B2 * v_ref[...] + (1.0 - ADAM_B2) * (gv * gv)
        m_hat = m2 / (1.0 - ADAM_B1 ** ADAM_STEP)
        v_hat = v2 / (1.0 - ADAM_B2 ** ADAM_STEP)
        d_ref[...] = -ADAM_LR * (m_hat / (jnp.sqrt(v_hat) + ADAM_EPS) + ADAM_WD * w_ref[...])
        m2_ref[...] = m2
        v2_ref[...] = v2

    blk = pl.BlockSpec((tr, C), lambda i: (i, 0))
    sh = jax.ShapeDtypeStruct((R, C), F32)
    return pl.pallas_call(
        body, name="adamw", grid=(R // tr,), in_specs=[blk] * 4, out_specs=[blk] * 3,
        out_shape=[sh, sh, sh], compiler_params=_cp(("parallel",)))(w, g, m, v)


CHAINS = [(p, b) for p in range(2) for b in range(BL)]
NC = len(CHAINS)


def _lane_masks():
    lane = _iota((1, BQ), 1)
    m0 = (lane < 64).astype(BF16)
    return m0, 1.0 - m0


def _stack(x, m0, m1):
    return jnp.concatenate([x * m0, x * m1], axis=0)


def _lanes(a):
    return jnp.concatenate([a[:BQ], a[BQ:]], axis=1)


def _per_lane(v):
    return jnp.where(_iota((BQ, BQ), 1) < 64, v[:BQ], v[BQ:])


def _diag_valid(strict):
    r = _iota((HB, BQ), 0) & (BQ - 1)
    c = _iota((HB, BQ), 1)
    return (c < r) if strict else (c <= r)


def _rows(b, i):
    return pl.ds(pl.multiple_of(b * S + i * BQ, BQ), BQ)


def _load_q(qkv_v, p, b, i, m0, m1):
    return _stack(qkv_v[_rows(b, i), p * PAIRW:p * PAIRW + BQ] * SCALE, m0, m1)


def _load_kv(qkv_v, p, b, j):
    r = _rows(b, j)
    return qkv_v[r, p * PAIRW + BQ:p * PAIRW + 2 * BQ], qkv_v[r, p * PAIRW + 2 * BQ:p * PAIRW + 3 * BQ]


def _stat_cols(tile8, p, b):
    lane = _iota((BQ, NSTAT), 1)
    c = b * NH + 2 * p
    return jnp.concatenate([jnp.sum(jnp.where(lane == c, tile8, 0.0), axis=1, keepdims=True),
                            jnp.sum(jnp.where(lane == c + 1, tile8, 0.0), axis=1, keepdims=True)], axis=0)


def _stat_tile(cols):
    lane = _iota((BQ, NSTAT), 1)
    t = jnp.zeros((BQ, NSTAT), F32)
    for (p, b), v in cols.items():
        c = b * NH + 2 * p
        t = t + jnp.where(lane == c, v[:BQ], 0.0) + jnp.where(lane == c + 1, v[BQ:], 0.0)
    return t


def _key_rows(ck_ref, p, b, j):
    c = b * NH + 2 * p
    kk = pl.ds(pl.multiple_of(j * BQ, BQ), BQ)
    return jnp.concatenate([jnp.broadcast_to(ck_ref[c:c + 1, kk], (BQ, BQ)),
                            jnp.broadcast_to(ck_ref[c + 1:c + 2, kk], (BQ, BQ))], axis=0)


def _copy_in(src, dst, sem):
    cp = pltpu.make_async_copy(src, dst, sem)
    cp.start()
    cp.wait()


STAT_SHAPE = jax.ShapeDtypeStruct((S, NSTAT), F32)
SLAB_QKV = pltpu.VMEM((T, 2 * PAIRW), BF16)
SLAB_OUT = pltpu.VMEM((T, 2 * BQ), BF16)
ACC_KV = pltpu.VMEM((2, T, BQ), F32)
SLAB_O32 = pltpu.VMEM((T, 2 * BQ), F32)


class _Job:
    def __init__(self, ins, out_shapes, aliases, sems, start, finish):
        self.ins, self.out_shapes, self.aliases, self.sems = list(ins), list(out_shapes), dict(aliases), list(sems)
        self.start, self.finish = start, finish


def _host_call(body, name, ins, in_specs, out_shapes, out_specs, scratch, aliases, job):
    n_in, n_out, n_scr = len(ins), len(out_shapes), len(scratch)
    jins = job.ins if job else []
    jouts = job.out_shapes if job else []
    jsems = job.sems if job else []

    def wrapped(*refs):
        a = n_in
        b = a + len(jins)
        c = b + n_out
        d = c + len(jouts)
        e = d + n_scr
        comm = None
        if job:
            jrefs = (refs[a:b], refs[c:d], refs[e:])
            comm = (lambda: job.start(*jrefs), lambda st: job.finish(st, *jrefs))
        body(refs[:a], refs[b:c], refs[d:e], comm)

    al = dict(aliases)
    if job:
        for ji, jo in job.aliases.items():
            al[n_in + ji] = n_out + jo
    res = pl.pallas_call(
        wrapped, name=name, in_specs=list(in_specs) + [ANY_SPEC] * len(jins),
        out_specs=list(out_specs) + [ANY_SPEC] * len(jouts), out_shape=list(out_shapes) + list(jouts),
        scratch_shapes=list(scratch) + list(jsems), input_output_aliases=al,
        compiler_params=_cp())(*ins, *jins)
    return res[:n_out], res[n_out:]


def _sb_fwd(qkv, job=None):
    def body(ins, outs, scr, comm):
        (qkv_hbm,), (o_hbm, r_ref), (qkv_v, o_v, sem) = ins, outs, scr
        _copy_in(qkv_hbm.at[:, pl.ds(0, 2 * PAIRW)], qkv_v, sem)
        st = comm[0]() if comm else None
        m0, m1 = _lane_masks()
        valid = _diag_valid(True)
        u2 = ((_iota((HB, BQ), 0) & (BQ - 1)) > _iota((HB, BQ), 1)).astype(BF16)

        def steps(qs, j, cs, diag):
            kv = [_load_kv(qkv_v, p, b, j) for p, b in CHAINS]
            zs = [_dot_nt(qs[c], kv[c][0]) for c in range(NC)]
            lbs, lrs = [], []
            for c in range(NC):
                lb = _log_sigmoid_tile(zs[c])
                lr = lb - zs[c]
                if diag:
                    lr = jnp.where(valid, lr, 0.0)
                lbs.append(lb)
                lrs.append(lr)
            tails = [_dot(_split2(lrs[c]), u2) for c in range(NC)]
            out = []
            for c in range(NC):
                tail_c, acc = cs[c]
                a = jnp.exp(lbs[c] + tails[c] + tail_c)
                if diag:
                    a = jnp.where(valid, a, 0.0)
                acc = acc + _dot(_lanes(a.astype(BF16)), _stack(kv[c][1], m0, m1))
                out.append((tail_c + jnp.sum(lrs[c], axis=1, keepdims=True), acc))
            return tuple(out)

        def qblock(i, _):
            qs = [_load_q(qkv_v, p, b, i, m0, m1) for p, b in CHAINS]
            zero = (jnp.zeros((HB, 1), F32), jnp.zeros((BQ, BQ), F32))
            cs = steps(qs, i, (zero,) * NC, True)
            cs = lax.fori_loop(1, i + 1, lambda jj, cs: steps(qs, i - jj, cs, False), cs)
            for c, (p, b) in enumerate(CHAINS):
                o_v[_rows(b, i), p * BQ:(p + 1) * BQ] = cs[c][1].astype(BF16)
            r_ref[_rows(0, i), :] = _stat_tile({pb: cs[c][0] for c, pb in enumerate(CHAINS)})
            return 0

        lax.fori_loop(0, NB, qblock, 0)
        _copy_in(o_v, o_hbm.at[:, pl.ds(0, 2 * BQ)], sem)
        if comm:
            comm[1](st)

    (mixed, rtot), extra = _host_call(
        body, "sb_fwd", [qkv], [ANY_SPEC], [jax.ShapeDtypeStruct((T, D), BF16), STAT_SHAPE], [ANY_SPEC, VMEM_SPEC],
        [SLAB_QKV, SLAB_OUT, pltpu.SemaphoreType.DMA], {}, job)
    return mixed, rtot, extra


def _sb_bwd(qkv, dmixed, rtot, job=None):
    def body(ins, outs, scr, comm):
        (qkv_hbm, do_hbm, r_ref), (dqkv_hbm,), (qkv_v, do_v, dq_v, dk_s, dv_s, sem) = ins, outs, scr
        _copy_in(qkv_hbm.at[:, pl.ds(0, 2 * PAIRW)], qkv_v, sem)
        _copy_in(do_hbm.at[:, pl.ds(0, 2 * BQ)], do_v, sem)
        st = comm[0]() if comm else None
        m0, m1 = _lane_masks()
        valid = _diag_valid(True)
        r2 = _iota((HB, BQ), 0) & (BQ - 1)
        c2 = _iota((HB, BQ), 1)
        u2 = (r2 > c2).astype(BF16)
        l2 = (r2 < c2).astype(BF16)
        dk_s[...] = jnp.zeros_like(dk_s)
        dv_s[...] = jnp.zeros_like(dv_s)

        def steps(qs, dos, rts, j, cs, diag):
            kv = [_load_kv(qkv_v, p, b, j) for p, b in CHAINS]
            zs = [_dot_nt(qs[c], kv[c][0]) for c in range(NC)]
            das = [_dot_nt(dos[c], kv[c][1]) for c in range(NC)]
            lbs, lrs, pre_ls = [], [], []
            for c in range(NC):
                lb = _log_sigmoid_tile(zs[c])
                lr = lb - zs[c]
                if diag:
                    lr = jnp.where(valid, lr, 0.0)
                lbs.append(lb)
                lrs.append(lr)
                pre_ls.append(cs[c][0] + jnp.sum(lr, axis=1, keepdims=True))
            tails = [_dot(_split2(lrs[c]), u2) for c in range(NC)]
            avs, gms = [], []
            for c in range(NC):
                a = jnp.exp(lbs[c] + tails[c] + (rts[c] - pre_ls[c]))
                if diag:
                    a = jnp.where(valid, a, 0.0)
                avs.append(a)
                gms.append(das[c] * a)
            befores = [_dot(_split2(gms[c]), l2) for c in range(NC)]
            dzbs = []
            for c in range(NC):
                beta = jnp.exp(lbs[c])
                dz = gms[c] * (1.0 - beta) - beta * (befores[c] + cs[c][1])
                if diag:
                    dz = jnp.where(valid, dz, 0.0)
                dzbs.append(dz.astype(BF16))
            out = []
            for c, (p, b) in enumerate(CHAINS):
                dq = cs[c][2] + _dot(_lanes(dzbs[c]), _stack(kv[c][0], m0, m1))
                dk_s[p, _rows(b, j), :] += _dot_tn(dzbs[c], qs[c])
                dv_s[p, _rows(b, j), :] += _dot_tn(avs[c].astype(BF16), dos[c])
                out.append((pre_ls[c], cs[c][1] + jnp.sum(gms[c], axis=1, keepdims=True), dq))
            return tuple(out)

        def qblock(i, _):
            r8 = r_ref[_rows(0, i), :]
            qs = [_load_q(qkv_v, p, b, i, m0, m1) for p, b in CHAINS]
            dos = [_stack(do_v[_rows(b, i), p * BQ:(p + 1) * BQ], m0, m1) for p, b in CHAINS]
            rts = [_stat_cols(r8, p, b) for p, b in CHAINS]
            z1 = jnp.zeros((HB, 1), F32)
            cs = ((z1, z1, jnp.zeros((BQ, BQ), F32)),) * NC
            cs = lax.fori_loop(0, i, lambda j, cs: steps(qs, dos, rts, j, cs, False), cs)
            cs = steps(qs, dos, rts, i, cs, True)
            for c, (p, b) in enumerate(CHAINS):
                dq_v[_rows(b, i), p * PAIRW:p * PAIRW + BQ] = (cs[c][2] * SCALE).astype(BF16)
            return 0

        lax.fori_loop(0, NB, qblock, 0)
        for p in range(2):
            dq_v[:, p * PAIRW + BQ:p * PAIRW + 2 * BQ] = dk_s[p].astype(BF16)
            dq_v[:, p * PAIRW + 2 * BQ:p * PAIRW + 3 * BQ] = dv_s[p].astype(BF16)
        _copy_in(dq_v, dqkv_hbm.at[:, pl.ds(0, 2 * PAIRW)], sem)
        if comm:
            comm[1](st)

    (dqkv,), extra = _host_call(
        body, "sb_bwd", [qkv, dmixed, rtot], [ANY_SPEC, ANY_SPEC, VMEM_SPEC],
        [jax.ShapeDtypeStruct((T, QKVW), BF16)], [ANY_SPEC],
        [SLAB_QKV, SLAB_OUT, SLAB_QKV, ACC_KV, ACC_KV, pltpu.SemaphoreType.DMA], {}, job)
    return dqkv, extra


def _flash_fwd(qkv, mixed, g, fox, bias, job=None):
    def body(ins, outs, scr, comm):
        if fox:
            qkv_hbm, cq_ref, ck_ref, _ = ins
        else:
            qkv_hbm, tbl_ref, _ = ins
        if fox:
            (o_hbm, lse_ref, o32_hbm), (qkv_v, o_v, sem, o32_v) = outs, scr
        else:
            (o_hbm, lse_ref), (qkv_v, o_v, sem) = outs, scr
        _copy_in(qkv_hbm.at[:, pl.ds(g * 2 * PAIRW, 2 * PAIRW)], qkv_v, sem)
        st = comm[0]() if comm else None
        m0, m1 = _lane_masks()
        valid = _diag_valid(False)

        def steps(qs, cqs, i, j, cs, diag):
            kv = [_load_kv(qkv_v, p, b, j) for p, b in CHAINS]
            zs = [_dot_nt(qs[c], kv[c][0]) for c in range(NC)]
            prs, alphas, out = [], [], []
            for c, (p, b) in enumerate(CHAINS):
                m, l, _ = cs[c]
                if fox:
                    z = zs[c] + (cqs[c] - _key_rows(ck_ref, p, b, j))
                    if diag:
                        z = jnp.where(valid, z, NEG)
                else:
                    z = zs[c] + tbl_ref[p, i - j]
                m_new = jnp.maximum(m, jnp.max(z, axis=1, keepdims=True))
                alpha = jnp.exp(m - m_new)
                pr = jnp.exp(z - m_new)
                prs.append(_split2(pr) if fox else pr.astype(BF16))
                alphas.append(alpha)
                out.append((m_new, alpha * l + jnp.sum(pr, axis=1, keepdims=True)))
            if fox:
                pvs = []
                for c in range(NC):
                    vs = _stack(kv[c][1], m0, m1)
                    vs = jnp.concatenate([vs[:BQ], vs[:BQ], vs[BQ:], vs[BQ:]], axis=0)
                    pvs.append(_dot(_lanes(prs[c]), vs))
            else:
                pvs = [_dot(_lanes(prs[c]), _stack(kv[c][1], m0, m1)) for c in range(NC)]
            return tuple((out[c][0], out[c][1], _per_lane(alphas[c]) * cs[c][2] + pvs[c]) for c in range(NC))

        def qblock(i, _):
            qs = [_load_q(qkv_v, p, b, i, m0, m1) for p, b in CHAINS]
            if fox:
                c8 = cq_ref[_rows(0, i), :]
                cqs = [_stat_cols(c8, p, b) for p, b in CHAINS]
            else:
                cqs = [None] * NC
            zero = (jnp.full((HB, 1), NEG, F32), jnp.zeros((HB, 1), F32), jnp.zeros((BQ, BQ), F32))
            cs = steps(qs, cqs, i, i, (zero,) * NC, True)
            cs = lax.fori_loop(1, i + 1, lambda jj, cs: steps(qs, cqs, i, i - jj, cs, False), cs)
            for c, (p, b) in enumerate(CHAINS):
                m, l, acc = cs[c]
                o = acc / _per_lane(l)
                o_v[_rows(b, i), p * BQ:(p + 1) * BQ] = o.astype(BF16)
                if fox:
                    o32_v[_rows(b, i), p * BQ:(p + 1) * BQ] = o
            lse_ref[_rows(0, i), :] = _stat_tile({pb: cs[c][0] + jnp.log(cs[c][1]) for c, pb in enumerate(CHAINS)})
            return 0

        lax.fori_loop(0, NB, qblock, 0)
        _copy_in(o_v, o_hbm.at[:, pl.ds(g * 2 * BQ, 2 * BQ)], sem)
        if fox:
            _copy_in(o32_v, o32_hbm, sem)
        if comm:
            comm[1](st)

    bias_specs = [VMEM_SPEC, VMEM_SPEC] if fox else [VMEM_SPEC]
    n_in = 2 + len(bias_specs)
    o32 = [jax.ShapeDtypeStruct((T, 2 * BQ), F32)] if fox else []
    res, extra = _host_call(
        body, "fox_fwd" if fox else "dil_fwd", [qkv, *bias, mixed], [ANY_SPEC] + bias_specs + [ANY_SPEC],
        [jax.ShapeDtypeStruct((T, D), BF16), STAT_SHAPE] + o32, [ANY_SPEC, VMEM_SPEC] + [ANY_SPEC] * len(o32),
        [SLAB_QKV, SLAB_OUT, pltpu.SemaphoreType.DMA] + ([SLAB_O32] if fox else []), {n_in - 1: 0}, job)
    return (*res, extra)


def _flash_bwd(qkv, mixed, dmixed, lse, dqkv, g, fox, bias, job=None):
    def body(ins, outs, scr, comm):
        if fox:
            qkv_hbm, o_hbm, do_hbm, lse_ref, cq_ref, ck_ref, _ = ins
        else:
            qkv_hbm, o_hbm, do_hbm, lse_ref, tbl_ref, _ = ins
        (dqkv_hbm, db_ref), (qkv_v, o_v, do_v, dq_v, dk_s, dv_s, sem) = outs, scr
        _copy_in(qkv_hbm.at[:, pl.ds(g * 2 * PAIRW, 2 * PAIRW)], qkv_v, sem)
        _copy_in(do_hbm.at[:, pl.ds(g * 2 * BQ, 2 * BQ)], do_v, sem)
        if fox:
            _copy_in(o_hbm, o_v, sem)
        else:
            _copy_in(o_hbm.at[:, pl.ds(g * 2 * BQ, 2 * BQ)], o_v, sem)
        st = comm[0]() if comm else None
        m0, m1 = _lane_masks()
        valid = _diag_valid(False)
        dk_s[...] = jnp.zeros_like(dk_s)
        dv_s[...] = jnp.zeros_like(dv_s)
        db_ref[...] = jnp.zeros_like(db_ref)

        def probs(qs, dos, cqs, lses, i, j, diag):
            kv = [_load_kv(qkv_v, p, b, j) for p, b in CHAINS]
            zs = [_dot_nt(qs[c], kv[c][0]) for c in range(NC)]
            dps = [_dot_nt(dos[c], kv[c][1]) for c in range(NC)]
            prs = []
            for c, (p, b) in enumerate(CHAINS):
                if fox:
                    z = zs[c] + (cqs[c] - _key_rows(ck_ref, p, b, j))
                    if diag:
                        z = jnp.where(valid, z, NEG)
                else:
                    z = zs[c] + tbl_ref[p, i - j]
                prs.append(jnp.exp(z - lses[c]))
            return [kv[c][0] for c in range(NC)], prs, dps

        def qblock(i, _):
            l8 = lse_ref[_rows(0, i), :]
            qs = [_load_q(qkv_v, p, b, i, m0, m1) for p, b in CHAINS]
            dos = [_stack(do_v[_rows(b, i), p * BQ:(p + 1) * BQ], m0, m1) for p, b in CHAINS]
            lses = [_stat_cols(l8, p, b) for p, b in CHAINS]
            if fox:
                c8 = cq_ref[_rows(0, i), :]
                cqs = [_stat_cols(c8, p, b) for p, b in CHAINS]
            else:
                cqs = [None] * NC
            deltas = []
            for c, (p, b) in enumerate(CHAINS):
                ob = o_v[_rows(b, i), p * BQ:(p + 1) * BQ].astype(F32)
                deltas.append(jnp.sum(dos[c].astype(F32) * jnp.concatenate([ob, ob], axis=0), axis=1, keepdims=True))

            def inner(j, dqs, diag):
                ks, prs, dps = probs(qs, dos, cqs, lses, i, j, diag)
                new, dzs = [], {}
                dzl = [prs[c] * (dps[c] - deltas[c]) for c in range(NC)]
                dzbs = [dz.astype(BF16) for dz in dzl]
                for c, (p, b) in enumerate(CHAINS):
                    dk_s[p, _rows(b, j), :] += _dot_tn(dzbs[c], qs[c])
                    dv_s[p, _rows(b, j), :] += _dot_tn(prs[c].astype(BF16), dos[c])
                    new.append(dqs[c] + _dot(_lanes(dzbs[c]), _stack(ks[c], m0, m1)))
                    dzs[(p, b)] = dzl[c]
                if fox:
                    kk = pl.ds(pl.multiple_of(j * BQ, BQ), BQ)
                    for (p, b), dz in dzs.items():
                        r = b * NH + 2 * p
                        db_ref[r:r + 1, kk] = db_ref[r:r + 1, kk] - jnp.sum(dz[:BQ], axis=0, keepdims=True)
                        db_ref[r + 1:r + 2, kk] = db_ref[r + 1:r + 2, kk] - jnp.sum(dz[BQ:], axis=0, keepdims=True)
                else:
                    for p in range(2):
                        db_ref[p, i - j] = db_ref[p, i - j] + (dzs[(p, 0)] + dzs[(p, 1)])
                return tuple(new)

            dqs = tuple(jnp.zeros((BQ, BQ), F32) for _ in CHAINS)
            dqs = lax.fori_loop(0, i, lambda j, d: inner(j, d, False), dqs)
            dqs = inner(i, dqs, True)
            for c, (p, b) in enumerate(CHAINS):
                dq_v[_rows(b, i), p * PAIRW:p * PAIRW + BQ] = (dqs[c] * SCALE).astype(BF16)
            return 0

        lax.fori_loop(0, NB, qblock, 0)
        for p in range(2):
            dq_v[:, p * PAIRW + BQ:p * PAIRW + 2 * BQ] = dk_s[p].astype(BF16)
            dq_v[:, p * PAIRW + 2 * BQ:p * PAIRW + 3 * BQ] = dv_s[p].astype(BF16)
        _copy_in(dq_v, dqkv_hbm.at[:, pl.ds(g * 2 * PAIRW, 2 * PAIRW)], sem)
        if comm:
            comm[1](st)

    bias_specs = [VMEM_SPEC, VMEM_SPEC] if fox else [VMEM_SPEC]
    db_shape = jax.ShapeDtypeStruct((NSTAT, S), F32) if fox else jax.ShapeDtypeStruct((2, NB, HB, BQ), F32)
    n_in = 5 + len(bias_specs)
    (dqkv, db), extra = _host_call(
        body, "fox_bwd" if fox else "dil_bwd", [qkv, mixed, dmixed, lse, *bias, dqkv],
        [ANY_SPEC, ANY_SPEC, ANY_SPEC, VMEM_SPEC] + bias_specs + [ANY_SPEC],
        [jax.ShapeDtypeStruct((T, QKVW), BF16), db_shape], [ANY_SPEC, VMEM_SPEC],
        [SLAB_QKV, SLAB_O32 if fox else SLAB_OUT, SLAB_OUT, SLAB_QKV, ACC_KV, ACC_KV, pltpu.SemaphoreType.DMA],
        {n_in - 1: 0}, job)
    return dqkv, db, extra


def _stacked_delta(d):
    return d * BQ + (_iota((HB, BQ), 0) & (BQ - 1)) - _iota((HB, BQ), 1)


def _buckets_in(d):
    lo, hi = max(d * BQ - (BQ - 1), 0), d * BQ + BQ - 1
    return [b for b in range(32) if BUCKET_TH[b] <= hi and (b == 31 or BUCKET_TH[b + 1] > lo)]


def _in_bucket(delta, b):
    m = delta >= BUCKET_TH[b]
    return m if b == 31 else m & (delta < BUCKET_TH[b + 1])


def _dil_table(rel_bias):
    def body(rb_ref, o_ref):
        for d in range(NB):
            delta = _stacked_delta(d)
            pos = delta >= 0
            n = ((pos & (delta <= 128)).astype(jnp.int32)
                 + (pos & (delta <= 512) & ((delta & 3) == 0)).astype(jnp.int32)
                 + (pos & ((delta & 15) == 0)).astype(jnp.int32))
            logn = jnp.where(n == 3, math.log(3.0), jnp.where(n == 2, math.log(2.0), jnp.where(n == 1, 0.0, NEG)))
            head1 = _iota((HB, BQ), 0) >= BQ
            for p in range(2):
                val = jnp.zeros((HB, BQ), F32)
                for b in _buckets_in(d):
                    val = jnp.where(_in_bucket(delta, b), jnp.where(head1, rb_ref[b, 2 * p + 1], rb_ref[b, 2 * p]), val)
                o_ref[p, d] = val + logn

    return pl.pallas_call(
        body, name="dil_table", in_specs=[pl.BlockSpec(memory_space=pltpu.SMEM)], out_specs=VMEM_SPEC,
        out_shape=jax.ShapeDtypeStruct((2, NB, HB, BQ), F32), compiler_params=_cp())(rel_bias)


def _dil_table_bwd(dtbl):
    def body(dt_ref, o_ref):
        p = pl.program_id(0)
        rowi = _iota((32, BQ), 0)
        lanei = _iota((32, BQ), 1)

        @pl.when(p == 0)
        def _():
            o_ref[...] = jnp.zeros_like(o_ref)

        out = jnp.zeros((32, BQ), F32)
        for b in range(32):
            acc = None
            for d in range(NB):
                if b in _buckets_in(d):
                    t = jnp.where(_in_bucket(_stacked_delta(d), b), dt_ref[d], 0.0)
                    acc = t if acc is None else acc + t
            rs = jnp.sum(acc, axis=1, keepdims=True)
            s0 = jnp.sum(rs[:BQ], axis=0, keepdims=True)
            s1 = jnp.sum(rs[BQ:], axis=0, keepdims=True)
            out = (out + jnp.where((rowi == b) & (lanei == 2 * p), s0, 0.0)
                   + jnp.where((rowi == b) & (lanei == 2 * p + 1), s1, 0.0))
        o_ref[...] += out

    return pl.pallas_call(
        body, name="dil_table_bwd", grid=(2,),
        in_specs=[pl.BlockSpec((None, NB, HB, BQ), lambda p: (p, 0, 0, 0))],
        out_specs=pl.BlockSpec((32, BQ), lambda p: (0, 0)),
        out_shape=jax.ShapeDtypeStruct((32, BQ), F32),
        compiler_params=_cp(("arbitrary",)))(dtbl)


def _fox_prep(gate, fb):
    def body(g_ref, fb_ref, c_ref):
        tri = (_iota((BQ, BQ), 0) >= _iota((BQ, BQ), 1)).astype(BF16)

        def blk(i, carry):
            r0 = pl.multiple_of(i * BQ, BQ)
            lf = _log_sigmoid(g_ref[pl.ds(r0, BQ), :] + fb_ref[...])
            c = _dot(tri, _split3(lf))
            c_ref[pl.ds(r0, BQ), :] = c[:, 0:BQ] + c[:, BQ:2 * BQ] + c[:, 2 * BQ:3 * BQ] + carry
            return carry + jnp.sum(lf, axis=0, keepdims=True)

        lax.fori_loop(0, NB, blk, jnp.zeros((1, BQ), F32))

    blk = pl.BlockSpec((S, GATEW), lambda b: (b, 0))
    return pl.pallas_call(
        body, name="fox_prep", grid=(BL,), in_specs=[blk, pl.BlockSpec((1, GATEW), lambda b: (0, 0))],
        out_specs=blk, out_shape=jax.ShapeDtypeStruct((T, GATEW), F32),
        compiler_params=_cp(("parallel",)))(gate, fb)


def _fox_post(dcum, gate, fb):
    def body(dc_ref, g_ref, fb_ref, dg_ref, dfb_ref):
        b = pl.program_id(0)
        tri = (_iota((BQ, BQ), 0) <= _iota((BQ, BQ), 1)).astype(BF16)

        def blk(ii, carry):
            csum, dfb = carry
            r0 = pl.multiple_of((NB - 1 - ii) * BQ, BQ)
            dc = dc_ref[pl.ds(r0, BQ), :]
            c = _dot(tri, _split3(dc))
            dlf = c[:, 0:BQ] + c[:, BQ:2 * BQ] + c[:, 2 * BQ:3 * BQ] + csum
            dg = dlf * jnp.exp(_log_sigmoid(-(g_ref[pl.ds(r0, BQ), :] + fb_ref[...])))
            dg_ref[pl.ds(r0, BQ), :] = dg
            return csum + jnp.sum(dc, axis=0, keepdims=True), dfb + jnp.sum(dg, axis=0, keepdims=True)

        z = jnp.zeros((1, BQ), F32)
        _, dfb = lax.fori_loop(0, NB, blk, (z, z))

        @pl.when(b == 0)
        def _():
            dfb_ref[...] = dfb

        @pl.when(b > 0)
        def _():
            dfb_ref[...] += dfb

    blk = pl.BlockSpec((S, GATEW), lambda b: (b, 0))
    vec = pl.BlockSpec((1, GATEW), lambda b: (0, 0))
    return pl.pallas_call(
        body, name="fox_post", grid=(BL,), in_specs=[blk, blk, vec], out_specs=[blk, vec],
        out_shape=[jax.ShapeDtypeStruct((T, GATEW), F32), jax.ShapeDtypeStruct((1, GATEW), F32)],
        compiler_params=_cp(("arbitrary",)))(dcum, gate, fb)


def _shift_down(x, n):
    return jnp.where(_iota(x.shape, 0) >= n, pltpu.roll(x, n, 0), 0.0)


def _shift_up(x, n):
    return jnp.where(_iota(x.shape, 0) < S - n, pltpu.roll(x, S - n, 0), 0.0)


def _conv_fwd(conv, cw, mixed):
    W = 256

    def body(c_ref, w_ref, _, o_ref):
        u = c_ref[:, W:2 * W] * c_ref[:, 2 * W:3 * W]
        y = w_ref[0:1, :] * _shift_down(u, 2) + w_ref[1:2, :] * _shift_down(u, 1) + w_ref[2:3, :] * u
        o_ref[...] = (c_ref[:, 0:W] * y).astype(BF16)

    return pl.pallas_call(
        body, name="conv_fwd", grid=(BL,),
        in_specs=[pl.BlockSpec((S, CONVW), lambda b: (b, 0)), pl.BlockSpec((8, W), lambda b: (0, 0)), ANY_SPEC],
        out_specs=pl.BlockSpec((S, W), lambda b: (b, 3)),
        out_shape=jax.ShapeDtypeStruct((T, D), BF16), input_output_aliases={2: 0},
        compiler_params=_cp(("parallel",)))(conv, cw, mixed)


def _conv_bwd(conv, cw, dmixed):
    W = 256

    def body(c_ref, w_ref, do_ref, dc_ref, dw_ref):
        b = pl.program_id(0)
        bg = c_ref[:, 0:W]
        cg = c_ref[:, W:2 * W]
        hv = c_ref[:, 2 * W:3 * W]
        do = do_ref[...].astype(F32)
        u = cg * hv
        u1 = _shift_down(u, 1)
        u2 = _shift_down(u, 2)
        y = w_ref[0:1, :] * u2 + w_ref[1:2, :] * u1 + w_ref[2:3, :] * u
        dy = do * bg
        du = w_ref[2:3, :] * dy + w_ref[1:2, :] * _shift_up(dy, 1) + w_ref[0:1, :] * _shift_up(dy, 2)
        dc_ref[:, 0:W] = (do * y).astype(BF16)
        dc_ref[:, W:2 * W] = (du * hv).astype(BF16)
        dc_ref[:, 2 * W:3 * W] = (du * cg).astype(BF16)
        rowi = _iota((8, W), 0)
        dw = (jnp.where(rowi == 0, jnp.sum(dy * u2, axis=0, keepdims=True), 0.0)
              + jnp.where(rowi == 1, jnp.sum(dy * u1, axis=0, keepdims=True), 0.0)
              + jnp.where(rowi == 2, jnp.sum(dy * u, axis=0, keepdims=True), 0.0))

        @pl.when(b == 0)
        def _():
            dw_ref[...] = dw

        @pl.when(b > 0)
        def _():
            dw_ref[...] += dw

    return pl.pallas_call(
        body, name="conv_bwd", grid=(BL,),
        in_specs=[pl.BlockSpec((S, CONVW), lambda b: (b, 0)), pl.BlockSpec((8, W), lambda b: (0, 0)),
                  pl.BlockSpec((S, W), lambda b: (b, 3))],
        out_specs=[pl.BlockSpec((S, CONVW), lambda b: (b, 0)), pl.BlockSpec((8, W), lambda b: (0, 0))],
        out_shape=[jax.ShapeDtypeStruct((T, CONVW), BF16), jax.ShapeDtypeStruct((8, W), F32)],
        compiler_params=_cp(("arbitrary",)))(conv, cw, dmixed)


def _place():
    x, y, c = lax.axis_index("x"), lax.axis_index("y"), lax.axis_index("c")
    return x, y, c


def _allgather_weights(shards):
    n = len(shards)

    def body(*refs):
        ins, outs = refs[:n], refs[n:2 * n]
        send_sems, recv_sems, local_sems = refs[2 * n:]
        x, y, c = _place()
        me, sibling = (x, y, c), (x, y, 1 - c)
        chips = [(1 - x, y), (x, 1 - y), (1 - x, 1 - y)]

        def slot(a, p):
            return outs[a].at[:, 4 * p[0] + 2 * p[1] + p[2]]

        def copy(a, k, block, to, own=False):
            return pltpu.make_async_remote_copy(
                src_ref=ins[a] if own else slot(a, block), dst_ref=slot(a, block),
                send_sem=send_sems.at[a, k], recv_sem=recv_sems.at[a, k], device_id=to, device_id_type=MESH)

        mine = [pltpu.make_async_copy(ins[a], slot(a, me), local_sems.at[a]) for a in range(n)]
        for cp in mine:
            cp.start()
        first = []
        for a in range(n):
            first.append(copy(a, 0, me, sibling, own=True))
            first += [copy(a, 1 + j, me, (*chip, c), own=True) for j, chip in enumerate(chips)]
        for cp in first:
            cp.start()
        passed = []
        for j, chip in enumerate(chips):
            for a in range(n):
                copy(a, 1 + j, (*chip, c), me).wait_recv()
                cp = copy(a, 4 + j, (*chip, c), sibling)
                cp.start()
                passed.append(cp)
        for a in range(n):
            copy(a, 0, sibling, me).wait_recv()
            for j, chip in enumerate(chips):
                copy(a, 4 + j, (*chip, 1 - c), me).wait_recv()
        for cp in first + passed:
            cp.wait_send()
        for cp in mine:
            cp.wait()

    return pl.pallas_call(
        body, name="allgather_weights", in_specs=[ANY_SPEC] * n, out_specs=[ANY_SPEC] * n,
        out_shape=[jax.ShapeDtypeStruct((s.shape[0], NDEV) + s.shape[1:], s.dtype) for s in shards],
        scratch_shapes=[pltpu.SemaphoreType.DMA((n, 7)), pltpu.SemaphoreType.DMA((n, 7)),
                        pltpu.SemaphoreType.DMA((n,))],
        )(*shards)


def _allreduce_small(v):
    def body(v_ref, o_ref, slots, send_sems, recv_sems):
        x, y, c = _place()
        me = 4 * x + 2 * y + c
        slots[me] = v_ref[...]

        def copy(k):
            peer = (x ^ ((k >> 2) & 1), y ^ ((k >> 1) & 1), c ^ (k & 1))
            return pltpu.make_async_remote_copy(
                src_ref=v_ref, dst_ref=slots.at[me], send_sem=send_sems.at[k - 1], recv_sem=recv_sems.at[k - 1],
                device_id=peer, device_id_type=MESH)

        def arrival(k):
            return pltpu.make_async_remote_copy(
                src_ref=v_ref, dst_ref=slots.at[me ^ k], send_sem=send_sems.at[k - 1], recv_sem=recv_sems.at[k - 1],
                device_id=(x, y, c), device_id_type=MESH)

        sends = [copy(k) for k in range(1, NDEV)]
        for cp in sends:
            cp.start()
        for k in range(1, NDEV):
            arrival(k).wait_recv()
        for cp in sends:
            cp.wait_send()
        acc = slots[0]
        for d in range(1, NDEV):
            acc = acc + slots[d]
        o_ref[...] = acc

    return pl.pallas_call(
        body, name="allreduce_small", in_specs=[VMEM_SPEC], out_specs=VMEM_SPEC,
        out_shape=jax.ShapeDtypeStruct(v.shape, F32),
        scratch_shapes=[pltpu.VMEM((NDEV,) + v.shape, F32), pltpu.SemaphoreType.DMA((NDEV - 1,)),
                        pltpu.SemaphoreType.DMA((NDEV - 1,))],
        )(v)


def _sibling_exchange(grads):
    n = len(grads)

    def body(*refs):
        ins, outs = refs[:n], refs[n:2 * n]
        send_sems, recv_sems = refs[2 * n:]
        x, y, c = _place()
        cps = [pltpu.make_async_remote_copy(
            src_ref=ins[a].at[:, :, 1 - c], dst_ref=outs[a], send_sem=send_sems.at[a], recv_sem=recv_sems.at[a],
            device_id=(x, y, 1 - c), device_id_type=MESH) for a in range(n)]
        for cp in cps:
            cp.start()
        for cp in cps:
            cp.wait()

    return pl.pallas_call(
        body, name="sibling_exchange", in_specs=[ANY_SPEC] * n, out_specs=[ANY_SPEC] * n,
        out_shape=[jax.ShapeDtypeStruct(g.shape[:2] + g.shape[3:], F32) for g in grads],
        scratch_shapes=[pltpu.SemaphoreType.DMA((n,)), pltpu.SemaphoreType.DMA((n,))],
        )(*grads)


def _pair_sum(grad, got, core):
    _, _, _, rows, N = grad.shape

    def body(c_ref, g_ref, r_ref, o_ref):
        o_ref[...] = (g_ref[...] + r_ref[...]).astype(BF16)

    return pl.pallas_call(
        body, name="pair_sum",
        grid_spec=pltpu.PrefetchScalarGridSpec(
            num_scalar_prefetch=1, grid=(2, 4),
            in_specs=[pl.BlockSpec((None, None, None, rows, N), lambda l, k, c: (l, k, c[0], 0, 0)),
                      pl.BlockSpec((None, None, rows, N), lambda l, k, c: (l, k, 0, 0))],
            out_specs=pl.BlockSpec((None, None, rows, N), lambda l, k, c: (l, k, 0, 0))),
        out_shape=jax.ShapeDtypeStruct((2, 4, rows, N), BF16),
        compiler_params=_cp(("parallel", "parallel")))(core, grad, got)


def _chip_exchange(psums):
    n = len(psums)

    def body(*refs):
        ins, outs = refs[:n], refs[n:2 * n]
        send_sems, recv_sems, local_sems = refs[2 * n:]
        x, y, c = _place()
        mychip = 2 * x + y
        chips = [(1 - x, y), (x, 1 - y), (1 - x, 1 - y)]
        local = [pltpu.make_async_copy(ins[a].at[:, mychip], outs[a].at[:, mychip], local_sems.at[a]) for a in range(n)]
        for cp in local:
            cp.start()
        sends = []
        for a in range(n):
            for j, chip in enumerate(chips):
                sends.append(pltpu.make_async_remote_copy(
                    src_ref=ins[a].at[:, 2 * chip[0] + chip[1]], dst_ref=outs[a].at[:, mychip],
                    send_sem=send_sems.at[a, j], recv_sem=recv_sems.at[a, j],
                    device_id=(*chip, c), device_id_type=MESH))
        for cp in sends:
            cp.start()
        for a in range(n):
            for j, chip in enumerate(chips):
                pltpu.make_async_remote_copy(
                    src_ref=ins[a].at[:, mychip], dst_ref=outs[a].at[:, 2 * chip[0] + chip[1]],
                    send_sem=send_sems.at[a, j], recv_sem=recv_sems.at[a, j],
                    device_id=(x, y, c), device_id_type=MESH).wait_recv()
        for cp in sends:
            cp.wait_send()
        for cp in local:
            cp.wait()

    return pl.pallas_call(
        body, name="chip_exchange", in_specs=[ANY_SPEC] * n, out_specs=[ANY_SPEC] * n,
        out_shape=[jax.ShapeDtypeStruct(p.shape, BF16) for p in psums],
        scratch_shapes=[pltpu.SemaphoreType.DMA((n, 3)), pltpu.SemaphoreType.DMA((n, 3)),
                        pltpu.SemaphoreType.DMA((n,))],
        )(*psums)


def _chip_sum(parts):
    _, _, rows, N = parts.shape

    def body(p_ref, o_ref):
        acc = p_ref[0].astype(F32)
        for k in range(1, 4):
            acc = acc + p_ref[k].astype(F32)
        o_ref[...] = acc

    return pl.pallas_call(
        body, name="chip_sum", grid=(2,),
        in_specs=[pl.BlockSpec((None, 4, rows, N), lambda l: (l, 0, 0, 0))],
        out_specs=pl.BlockSpec((None, rows, N), lambda l: (l, 0, 0)),
        out_shape=jax.ShapeDtypeStruct((2, rows, N), F32), compiler_params=_cp(("parallel",)))(parts)


def _chips_of(x, y):
    return [(1 - x, y), (x, 1 - y), (1 - x, 1 - y)]


def _dev(p):
    return 4 * p[0] + 2 * p[1] + p[2]


def _gather_job_a(shards):
    n = len(shards)

    def peers(x, y, c):
        return [(x, y, 1 - c)] + [(*chip, c) for chip in _chips_of(x, y)]

    def start(ins, outs, sems):
        send, recv, loc = sems
        x, y, c = _place()
        me = (x, y, c)
        cps = []
        for a in range(n):
            cps.append(pltpu.make_async_copy(ins[a], outs[a].at[_dev(me)], loc.at[a]))
            for k, peer in enumerate(peers(x, y, c)):
                cps.append(pltpu.make_async_remote_copy(
                    src_ref=ins[a], dst_ref=outs[a].at[_dev(me)], send_sem=send.at[a, k], recv_sem=recv.at[a, k],
                    device_id=peer, device_id_type=MESH))
        for cp in cps:
            cp.start()
        return cps

    def finish(cps, ins, outs, sems):
        send, recv, loc = sems
        x, y, c = _place()
        for a in range(n):
            for k, peer in enumerate(peers(x, y, c)):
                pltpu.make_async_remote_copy(
                    src_ref=ins[a], dst_ref=outs[a].at[_dev(peer)], send_sem=send.at[a, k], recv_sem=recv.at[a, k],
                    device_id=(x, y, c), device_id_type=MESH).wait_recv()
        for a in range(n):
            cps[5 * a].wait()
            for k in range(4):
                cps[5 * a + 1 + k].wait_send()

    return _Job(shards, [jax.ShapeDtypeStruct((NDEV,) + s.shape, s.dtype) for s in shards], {},
                [pltpu.SemaphoreType.DMA((n, 4)), pltpu.SemaphoreType.DMA((n, 4)), pltpu.SemaphoreType.DMA((n,))],
                start, finish)


def _gather_job_b(gathered):
    n = len(gathered)

    def start(ins, outs, sems):
        send, recv = sems
        x, y, c = _place()
        cps = []
        for a in range(n):
            for j, chip in enumerate(_chips_of(x, y)):
                blk = outs[a].at[_dev((*chip, c))]
                cps.append(pltpu.make_async_remote_copy(
                    src_ref=blk, dst_ref=blk, send_sem=send.at[a, j], recv_sem=recv.at[a, j],
                    device_id=(x, y, 1 - c), device_id_type=MESH))
        for cp in cps:
            cp.start()
        return cps

    def finish(cps, ins, outs, sems):
        send, recv = sems
        x, y, c = _place()
        for a in range(n):
            for j, chip in enumerate(_chips_of(x, y)):
                blk = outs[a].at[_dev((*chip, 1 - c))]
                pltpu.make_async_remote_copy(
                    src_ref=blk, dst_ref=blk, send_sem=send.at[a, j], recv_sem=recv.at[a, j],
                    device_id=(x, y, c), device_id_type=MESH).wait_recv()
        for cp in cps:
            cp.wait_send()

    return _Job(gathered, [jax.ShapeDtypeStruct(g.shape, g.dtype) for g in gathered], {a: a for a in range(n)},
                [pltpu.SemaphoreType.DMA((n, 3)), pltpu.SemaphoreType.DMA((n, 3))], start, finish)


def _sibling_job(grads):
    n = len(grads)

    def start(ins, outs, sems):
        send, recv = sems
        x, y, c = _place()
        cps = [pltpu.make_async_remote_copy(
            src_ref=ins[a].at[:, 1 - c], dst_ref=outs[a], send_sem=send.at[a], recv_sem=recv.at[a],
            device_id=(x, y, 1 - c), device_id_type=MESH) for a in range(n)]
        for cp in cps:
            cp.start()
        return cps

    def finish(cps, ins, outs, sems):
        for cp in cps:
            cp.wait()

    return _Job(grads, [jax.ShapeDtypeStruct(g.shape[:1] + g.shape[2:], F32) for g in grads], {},
                [pltpu.SemaphoreType.DMA((n,)), pltpu.SemaphoreType.DMA((n,))], start, finish)


def _chip_job(psums):
    n = len(psums)

    def start(ins, outs, sems):
        send, recv, loc = sems
        x, y, c = _place()
        mychip = 2 * x + y
        cps = []
        for a in range(n):
            cps.append(pltpu.make_async_copy(ins[a].at[mychip], outs[a].at[mychip], loc.at[a]))
            for j, chip in enumerate(_chips_of(x, y)):
                cps.append(pltpu.make_async_remote_copy(
                    src_ref=ins[a].at[2 * chip[0] + chip[1]], dst_ref=outs[a].at[mychip],
                    send_sem=send.at[a, j], recv_sem=recv.at[a, j], device_id=(*chip, c), device_id_type=MESH))
        for cp in cps:
            cp.start()
        return cps

    def finish(cps, ins, outs, sems):
        send, recv, loc = sems
        x, y, c = _place()
        mychip = 2 * x + y
        for a in range(n):
            for j, chip in enumerate(_chips_of(x, y)):
                pltpu.make_async_remote_copy(
                    src_ref=ins[a].at[mychip], dst_ref=outs[a].at[2 * chip[0] + chip[1]],
                    send_sem=send.at[a, j], recv_sem=recv.at[a, j], device_id=(x, y, c), device_id_type=MESH).wait_recv()
        for a in range(n):
            cps[4 * a].wait()
            for j in range(3):
                cps[4 * a + 1 + j].wait_send()

    return _Job(psums, [jax.ShapeDtypeStruct(p.shape, BF16) for p in psums], {},
                [pltpu.SemaphoreType.DMA((n, 3)), pltpu.SemaphoreType.DMA((n, 3)), pltpu.SemaphoreType.DMA((n,))],
                start, finish)


def _join_jobs(*jobs):
    jobs = [j for j in jobs if j is not None]
    if len(jobs) <= 1:
        return jobs[0] if jobs else None
    cut = lambda seq, sizes: [seq[sum(sizes[:k]):sum(sizes[:k + 1])] for k in range(len(sizes))]
    n_in = [len(j.ins) for j in jobs]
    n_out = [len(j.out_shapes) for j in jobs]
    n_sem = [len(j.sems) for j in jobs]
    aliases = {}
    for k, j in enumerate(jobs):
        for a, b in j.aliases.items():
            aliases[sum(n_in[:k]) + a] = sum(n_out[:k]) + b

    def start(ins, outs, sems):
        return [j.start(i, o, s) for j, i, o, s in zip(jobs, cut(ins, n_in), cut(outs, n_out), cut(sems, n_sem))]

    def finish(sts, ins, outs, sems):
        for j, st, i, o, s in zip(jobs, sts, cut(ins, n_in), cut(outs, n_out), cut(sems, n_sem)):
            j.finish(st, i, o, s)

    return _Job([t for j in jobs for t in j.ins], [t for j in jobs for t in j.out_shapes], aliases,
                [t for j in jobs for t in j.sems], start, finish)


def _run_job(job, name):
    def body(ins, outs, scr, comm):
        comm[1](comm[0]())

    return _host_call(body, name, [], [], [], [], [], {}, job)[1]


def _pair_sum1(grad, got, core):
    _, _, rows, N = grad.shape

    def body(c_ref, g_ref, r_ref, o_ref):
        o_ref[...] = (g_ref[...] + r_ref[...]).astype(BF16)

    return pl.pallas_call(
        body, name="pair_sum",
        grid_spec=pltpu.PrefetchScalarGridSpec(
            num_scalar_prefetch=1, grid=(4,),
            in_specs=[pl.BlockSpec((None, None, rows, N), lambda k, c: (k, c[0], 0, 0)),
                      pl.BlockSpec((None, rows, N), lambda k, c: (k, 0, 0))],
            out_specs=pl.BlockSpec((None, rows, N), lambda k, c: (k, 0, 0))),
        out_shape=jax.ShapeDtypeStruct((4, rows, N), BF16),
        compiler_params=_cp(("parallel",)))(core, grad, got)


def _chip_sum1(parts):
    _, rows, N = parts.shape

    def body(p_ref, o_ref):
        acc = p_ref[0].astype(F32)
        for k in range(1, 4):
            acc = acc + p_ref[k].astype(F32)
        o_ref[...] = acc

    return pl.pallas_call(
        body, name="chip_sum", in_specs=[VMEM_SPEC], out_specs=VMEM_SPEC,
        out_shape=jax.ShapeDtypeStruct((rows, N), F32), compiler_params=_cp())(parts)


def _permute_in(w):
    lead = w.shape[:-1]
    return w.reshape(lead + (3, 3, 2, BQ)).swapaxes(-2, -3).reshape(lead + (QKVW,))


def _unpermute_in(w):
    lead = w.shape[:-1]
    return w.reshape(lead + (3, 2, 3, BQ)).swapaxes(-2, -3).reshape(lead + (QKVW,))


def _row(v):
    v = v.reshape(-1)
    return jnp.pad(v, (0, D - v.shape[0])).reshape(1, D)


def _step_without_overlap(x, w_in, f_bias, conv_w, w_out, rel_bias, ln1_g, ln1_b, w_gate, w_up, w_down, ln2_g, ln2_b, loss_target, m_w_in, m_f_bias, m_conv_w, m_w_out, m_rel_bias, m_ln1_g, m_ln1_b, m_w_gate, m_w_up, m_w_down, m_ln2_g, m_ln2_b, v_w_in, v_f_bias, v_conv_w, v_w_out, v_rel_bias, v_ln1_g, v_ln1_b, v_w_gate, v_w_up, v_w_down, v_ln2_g, v_ln2_b):
    xi, yi, ci = _place()
    me = 4 * xi + 2 * yi + ci

    win_s = jnp.concatenate([_permute_in(w_in[..., :QKVW]), w_in[..., QKVW:]], axis=-1)
    win_s = jnp.pad(win_s, ((0, 0), (0, 0), (0, NPAD - NPROJ))).astype(BF16)
    shards = [win_s, w_out.astype(BF16), jnp.swapaxes(w_gate, 1, 2).astype(BF16),
              jnp.swapaxes(w_up, 1, 2).astype(BF16), w_down.astype(BF16)]
    full = _allgather_weights(shards)
    Win, Wout, WgT, WuT, Wd = [f.reshape(2, NDEV * f.shape[2], f.shape[3]) for f in full]

    cw_rows = lax.dynamic_update_slice(jnp.zeros((2, 3, 256), F32), conv_w, (0, 0, me * 32))
    small = jnp.concatenate([_row(cw_rows[0]), _row(cw_rows[1]), jnp.zeros((SMALL_ROWS - 2, D), F32)], axis=0)
    small = _allreduce_small(small)
    cw_full = small[0:2, :CONVW].reshape(2, 3, 256)
    cw8 = jnp.pad(cw_full, ((0, 0), (0, 5), (0, 0)))
    fb = jnp.pad(f_bias, ((0, 0), (0, GATEW - NH))).reshape(2, 1, GATEW)
    tbl = _dil_table(rel_bias)

    def wcol(layer, K, tn, off):
        return pl.BlockSpec((None, K, tn), lambda i, j: (layer, 0, off + j))

    def arow(tm, K, blk=0):
        return pl.BlockSpec((tm, K), lambda i, j: (i, blk))

    h = x.reshape(T, D)
    hb = h.astype(BF16)
    saved = []
    for l in range(2):
        qkv = _mm([(hb, arow(512, D), Win, wcol(l, D, 768, 0))], nt=False, M=T, N=QKVW, tm=512, tn=768,
                  out_dtype=BF16, name="proj_qkv")
        conv = _mm([(hb, arow(512, D), Win, wcol(l, D, 768, 3))], nt=False, M=T, N=CONVW, tm=512, tn=768,
                   out_dtype=F32, name="proj_conv")
        gate = _mm([(hb, arow(512, D), Win, wcol(l, D, 128, 24))], nt=False, M=T, N=GATEW, tm=512, tn=128,
                   out_dtype=F32, name="proj_gate")
        cum = _fox_prep(gate, fb[l])
        cq = cum[:, :NH].reshape(BL, S, NH).transpose(1, 0, 2).reshape(S, NSTAT)
        ck = cq.T
        mixed, rtot = _sb_fwd(qkv)
        mixed, lse_d = _flash_fwd(qkv, mixed, 1, False, (tbl,))
        mixed, lse_f = _flash_fwd(qkv, mixed, 2, True, (cq, ck))
        mixed = _conv_fwd(conv, cw8[l], mixed)
        mix = _mm([(mixed, arow(512, D), Wout, wcol(l, D, 512, 0))], nt=False, M=T, N=D, tm=512, tn=512,
                  out_dtype=F32, name="out_proj")
        x1, xh1, r1, x1b = _ln_fwd(h, mix, ln1_g[l:l + 1], ln1_b[l:l + 1])
        g, u, a = _ffn_up(x1b, WgT, WuT, l)
        ffn = _mm([(a, arow(512, DFF), Wd, wcol(l, DFF, 512, 0))], nt=False, M=T, N=D, tm=512, tn=512,
                  out_dtype=F32, name="ffn_down")
        x2, xh2, r2, x2b = _ln_fwd(x1, ffn, ln2_g[l:l + 1], ln2_b[l:l + 1])
        saved.append(dict(h=hb, qkv=qkv, conv=conv, gate=gate, cq=cq, ck=ck, mixed=mixed, rtot=rtot, lse_d=lse_d,
                          lse_f=lse_f, x1=x1b, xh1=xh1, r1=r1, g=g, u=u, a=a, xh2=xh2, r2=r2))
        h, hb = x2, x2b

    sq, dy = _loss_grad(h, loss_target.reshape(T, D))
    loss = lax.psum(sq[0, 0], ("x", "y", "c")) * (0.5 / D)

    G_in = jnp.zeros((2, D, NPAD), F32)
    G_out = jnp.zeros((2, D, D), F32)
    G_g = jnp.zeros((2, DFF, D), F32)
    G_u = jnp.zeros((2, DFF, D), F32)
    G_d = jnp.zeros((2, DFF, D), F32)
    small_g = {}

    def wrow(layer, tn, K, blk=0):
        return pl.BlockSpec((None, tn, K), lambda i, j: (layer, j, blk))

    for l in (1, 0):
        sv = saved[l]
        ds2, dg2, db2, ds2b = _ln_bwd(dy, sv["xh2"], sv["r2"], ln2_g[l:l + 1])
        dgt, dut = _ffn_da(ds2b, Wd, sv["g"], sv["u"], l)
        G_d = _mm_tn(sv["a"], ds2b, G_d, Ka=DFF, N=D, tm=1408, tn=1024, tk=1024, layer=l, ooff=0, name="grad_w_down")
        G_g = _mm_tn(dgt, sv["x1"], G_g, Ka=DFF, N=D, tm=1408, tn=1024, tk=1024, layer=l, ooff=0, name="grad_w_gate")
        G_u = _mm_tn(dut, sv["x1"], G_u, Ka=DFF, N=D, tm=1408, tn=1024, tk=1024, layer=l, ooff=0, name="grad_w_up")
        dx1 = _mm([(dgt, arow(512, DFF), WgT, wcol(l, DFF, 512, 0)), (dut, arow(512, DFF), WuT, wcol(l, DFF, 512, 0))],
                  nt=False, M=T, N=D, tm=512, tn=512, out_dtype=F32, name="ffn_dx", res=ds2, res_scale=ALPHA)
        ds1, dg1, db1, ds1b = _ln_bwd(dx1, sv["xh1"], sv["r1"], ln1_g[l:l + 1])
        G_out = _mm_tn(sv["mixed"], ds1b, G_out, Ka=D, N=D, tm=1024, tn=1024, tk=1024, layer=l, ooff=0,
                       name="grad_w_out")
        dmixed = _mm([(ds1b, arow(512, D), Wout, wrow(l, 512, D))], nt=True, M=T, N=D, tm=512, tn=512,
                     out_dtype=BF16, name="out_proj_dx")
        dqkv = _sb_bwd(sv["qkv"], dmixed, sv["rtot"])
        dqkv, dtbl = _flash_bwd(sv["qkv"], sv["mixed"], dmixed, sv["lse_d"], dqkv, 1, False, (tbl,))
        dqkv, dck = _flash_bwd(sv["qkv"], sv["mixed"], dmixed, sv["lse_f"], dqkv, 2, True, (sv["cq"], sv["ck"]))
        dconv, dcw = _conv_bwd(sv["conv"], cw8[l], dmixed)
        dcum = jnp.pad(dck.reshape(BL, NH, S).transpose(0, 2, 1).reshape(T, NH), ((0, 0), (0, GATEW - NH)))
        dgate, dfb = _fox_post(dcum, sv["gate"], fb[l])
        drb = _dil_table_bwd(dtbl)
        G_in = _mm_tn(sv["h"], dqkv, G_in, Ka=D, N=QKVW, tm=1024, tn=768, tk=1024, layer=l, ooff=0, name="grad_w_in_qkv")
        G_in = _mm_tn(sv["h"], dconv, G_in, Ka=D, N=CONVW, tm=1024, tn=768, tk=1024, layer=l, ooff=3,
                      name="grad_w_in_conv")
        G_in = _mm_tn(sv["h"], dgate, G_in, Ka=D, N=GATEW, tm=1024, tn=128, tk=1024, layer=l, ooff=24,
                      name="grad_w_in_gate")
        dy = _mm([(dqkv, arow(512, QKVW), Win, wrow(l, 512, QKVW, 0)),
                  (dconv, arow(512, CONVW), Win, wrow(l, 512, CONVW, 3)),
                  (dgate, arow(512, GATEW), Win, wrow(l, 512, GATEW, 24))],
                 nt=True, M=T, N=D, tm=512, tn=512, out_dtype=F32, name="proj_dx", res=ds1, res_scale=ALPHA)
        small_g[l] = dict(ln1_g=dg1, ln1_b=db1, ln2_g=dg2, ln2_b=db2, cw=dcw[0:3].reshape(1, CONVW),
                          fb=dfb[:, :NH], rb=drb[:, :NH])
    grad_x = dy.reshape(BL, S, D)

    rows = []
    for name in ("ln1_g", "ln1_b", "ln2_g", "ln2_b"):
        rows += [small_g[0][name], small_g[1][name]]
    rows += [_row(small_g[0]["cw"]), _row(small_g[1]["cw"]),
             _row(jnp.concatenate([small_g[0]["fb"], small_g[1]["fb"]], axis=0)),
             _row(small_g[0]["rb"] + small_g[1]["rb"])]
    rows.append(jnp.zeros((SMALL_ROWS - len(rows), D), F32))
    sg = _allreduce_small(jnp.concatenate(rows, axis=0))
    g_ln1_g, g_ln1_b, g_ln2_g, g_ln2_b = sg[0:2], sg[2:4], sg[4:6], sg[6:8]
    g_conv_full = sg[8:10, :CONVW].reshape(2, 3, 256)
    g_conv = lax.dynamic_slice(g_conv_full, (0, 0, me * 32), (2, 3, 32))
    g_fb = sg[10, :2 * NH].reshape(2, NH)
    g_rb = sg[11, :32 * NH].reshape(32, NH)

    bufs = [G_in, G_out, G_g, G_u, G_d]
    views = [b.reshape(2, 4, 2, b.shape[1] // NDEV, b.shape[2]) for b in bufs]
    got = _sibling_exchange(views)
    core = jnp.reshape(ci, (1,)).astype(jnp.int32)
    psums = [_pair_sum(vw, gt, core) for vw, gt in zip(views, got)]
    parts = _chip_exchange(psums)
    gs = [_chip_sum(p) for p in parts]
    g_in = gs[0]
    g_w_in = jnp.concatenate([_unpermute_in(g_in[..., :QKVW]), g_in[..., QKVW:NPROJ]], axis=-1)
    g_w_out = gs[1]
    g_w_gate = jnp.swapaxes(gs[2], 1, 2)
    g_w_up = jnp.swapaxes(gs[3], 1, 2)
    g_w_down = gs[4]

    def big(w, g, m, v, tr):
        sh = w.shape
        f = lambda t: t.reshape(-1, sh[-1])
        return [t.reshape(sh) for t in _adamw(f(w), f(g), f(m), f(v), tr)]

    up_in = big(w_in, g_w_in, m_w_in, v_w_in, 64)
    up_out = big(w_out, g_w_out, m_w_out, v_w_out, 128)
    up_gate = big(w_gate, g_w_gate, m_w_gate, v_w_gate, 256)
    up_up = big(w_up, g_w_up, m_w_up, v_w_up, 256)
    up_down = big(w_down, g_w_down, m_w_down, v_w_down, 352)

    def pack(fbv, cwv, rbv, l1g, l1b, l2g, l2b):
        r = [l1g, l1b, l2g, l2b, _row(cwv), _row(fbv), _row(rbv)]
        r.append(jnp.zeros((SMALL_ROWS - 11, D), F32))
        return jnp.concatenate(r, axis=0)

    pw = pack(f_bias, conv_w, rel_bias, ln1_g, ln1_b, ln2_g, ln2_b)
    pg = pack(g_fb, g_conv, g_rb, g_ln1_g, g_ln1_b, g_ln2_g, g_ln2_b)
    pm = pack(m_f_bias, m_conv_w, m_rel_bias, m_ln1_g, m_ln1_b, m_ln2_g, m_ln2_b)
    pv = pack(v_f_bias, v_conv_w, v_rel_bias, v_ln1_g, v_ln1_b, v_ln2_g, v_ln2_b)
    ups = _adamw(pw, pg, pm, pv, SMALL_ROWS)

    def unpack(p):
        return dict(ln1_g=p[0:2], ln1_b=p[2:4], ln2_g=p[4:6], ln2_b=p[6:8],
                    conv_w=p[8, :192].reshape(2, 3, 32), f_bias=p[9, :2 * NH].reshape(2, NH),
                    rel_bias=p[10, :32 * NH].reshape(32, NH))

    sm = [unpack(p) for p in ups]

    def group(k):
        return (up_in[k], sm[k]["f_bias"], sm[k]["conv_w"], up_out[k], sm[k]["rel_bias"], sm[k]["ln1_g"],
                sm[k]["ln1_b"], up_gate[k], up_up[k], up_down[k], sm[k]["ln2_g"], sm[k]["ln2_b"])

    grads = (g_w_in, g_fb, g_conv, g_w_out, g_rb, g_ln1_g, g_ln1_b, g_w_gate, g_w_up, g_w_down, g_ln2_g, g_ln2_b)
    return (loss, grad_x) + grads + group(0) + group(1) + group(2)


def _pair_sums(views, gots, core):
    n = len(views)

    def body(c_ref, *refs):
        for a in range(n):
            refs[2 * n + a][...] = (refs[a][...] + refs[n + a][...]).astype(BF16)

    def vspec(v):
        return pl.BlockSpec((None, None) + v.shape[2:], lambda k, c: (k, c[0], 0, 0))

    def gspec(g):
        return pl.BlockSpec((None,) + g.shape[1:], lambda k, c: (k, 0, 0))

    return pl.pallas_call(
        body, name="pair_sums",
        grid_spec=pltpu.PrefetchScalarGridSpec(
            num_scalar_prefetch=1, grid=(4,),
            in_specs=[vspec(v) for v in views] + [gspec(g) for g in gots],
            out_specs=[gspec(g) for g in gots]),
        out_shape=[jax.ShapeDtypeStruct(g.shape, BF16) for g in gots],
        compiler_params=_cp(("parallel",)))(core, *views, *gots)


def _chip_sums(parts):
    n = len(parts)

    def body(*refs):
        for a in range(n):
            acc = refs[a][0].astype(F32)
            for k in range(1, 4):
                acc = acc + refs[a][k].astype(F32)
            refs[n + a][...] = acc

    return pl.pallas_call(
        body, name="chip_sums", in_specs=[VMEM_SPEC] * n, out_specs=[VMEM_SPEC] * n,
        out_shape=[jax.ShapeDtypeStruct(p.shape[1:], F32) for p in parts], compiler_params=_cp())(*parts)


def kernel(x, w_in, f_bias, conv_w, w_out, rel_bias, ln1_g, ln1_b, w_gate, w_up, w_down, ln2_g, ln2_b, loss_target, m_w_in, m_f_bias, m_conv_w, m_w_out, m_rel_bias, m_ln1_g, m_ln1_b, m_w_gate, m_w_up, m_w_down, m_ln2_g, m_ln2_b, v_w_in, v_f_bias, v_conv_w, v_w_out, v_rel_bias, v_ln1_g, v_ln1_b, v_w_gate, v_w_up, v_w_down, v_ln2_g, v_ln2_b):
    xi, yi, ci = _place()
    me = 4 * xi + 2 * yi + ci
    core = jnp.reshape(ci, (1,)).astype(jnp.int32)

    win_s = jnp.concatenate([_permute_in(w_in[..., :QKVW]), w_in[..., QKVW:]], axis=-1)
    win_s = jnp.pad(win_s, ((0, 0), (0, 0), (0, NPAD - NPROJ))).astype(BF16)
    per_layer = [win_s, w_out.astype(BF16), jnp.swapaxes(w_gate, 1, 2).astype(BF16),
                 jnp.swapaxes(w_up, 1, 2).astype(BF16), w_down.astype(BF16)]

    def shards(l):
        return [s[l] for s in per_layer]

    def whole(g):
        return g.reshape(NDEV * g.shape[1], g.shape[2])

    sh = [shards(0), shards(1)]
    first = _run_job(_gather_job_b(_run_job(_gather_job_a(sh[0][:1]), "gather_a")), "gather_b")
    W = [{"win": whole(first[0])}, {}]

    cw_rows = lax.dynamic_update_slice(jnp.zeros((2, 3, 256), F32), conv_w, (0, 0, me * 32))
    small = jnp.concatenate([_row(cw_rows[0]), _row(cw_rows[1]), jnp.zeros((SMALL_ROWS - 2, D), F32)], axis=0)
    small = _allreduce_small(small)
    cw_full = small[0:2, :CONVW].reshape(2, 3, 256)
    cw8 = jnp.pad(cw_full, ((0, 0), (0, 5), (0, 0)))
    fb = jnp.pad(f_bias, ((0, 0), (0, GATEW - NH))).reshape(2, 1, GATEW)
    tbl = _dil_table(rel_bias)

    def wcol(K, tn, off):
        return pl.BlockSpec((K, tn), lambda i, j: (0, off + j))

    def wrow(tn, K, blk=0):
        return pl.BlockSpec((tn, K), lambda i, j: (j, blk))

    def arow(tm, K, blk=0):
        return pl.BlockSpec((tm, K), lambda i, j: (i, blk))

    h = x.reshape(T, D)
    hb = h.astype(BF16)
    saved = []
    for l in range(2):
        Win = W[l]["win"]
        qkv = _mm([(hb, arow(512, D), Win, wcol(D, 768, 0))], nt=False, M=T, N=QKVW, tm=512, tn=768,
                  out_dtype=BF16, name="proj_qkv")
        conv = _mm([(hb, arow(512, D), Win, wcol(D, 768, 3))], nt=False, M=T, N=CONVW, tm=512, tn=768,
                   out_dtype=F32, name="proj_conv")
        gate = _mm([(hb, arow(512, D), Win, wcol(D, 128, 24))], nt=False, M=T, N=GATEW, tm=512, tn=128,
                   out_dtype=F32, name="proj_gate")
        cum = _fox_prep(gate, fb[l])
        cq = cum[:, :NH].reshape(BL, S, NH).transpose(1, 0, 2).reshape(S, NSTAT)
        ck = cq.T
        if l == 0:
            mixed, rtot, a0 = _sb_fwd(qkv, job=_gather_job_a(sh[0][1:]))
            mixed, lse_d, ex = _flash_fwd(qkv, mixed, 1, False, (tbl,),
                                          job=_join_jobs(_gather_job_b(list(a0)), _gather_job_a(sh[1][:2])))
            W[0].update(zip(("wout", "wgT", "wuT", "wd"), [whole(t) for t in ex[:4]]))
            mixed, lse_f, o_fox, ex = _flash_fwd(qkv, mixed, 2, True, (cq, ck),
                                                 job=_join_jobs(_gather_job_b(list(ex[4:])), _gather_job_a(sh[1][2:])))
            W[1].update(zip(("win", "wout"), [whole(t) for t in ex[:2]]))
            a2 = list(ex[2:])
        else:
            mixed, rtot, ex = _sb_fwd(qkv, job=_gather_job_b(a2))
            W[1].update(zip(("wgT", "wuT", "wd"), [whole(t) for t in ex]))
            mixed, lse_d, _ = _flash_fwd(qkv, mixed, 1, False, (tbl,))
            mixed, lse_f, o_fox, _ = _flash_fwd(qkv, mixed, 2, True, (cq, ck))
        Wout, WgT, WuT, Wd = W[l]["wout"], W[l]["wgT"], W[l]["wuT"], W[l]["wd"]
        mixed = _conv_fwd(conv, cw8[l], mixed)
        x1, xh1, r1, x1b = _mm_ln(mixed, Wout, h, ln1_g[l:l + 1], ln1_b[l:l + 1], "out_proj_ln")
        g, u, a = _ffn_up(x1b, WgT[None], WuT[None], 0)
        x2, xh2, r2, x2b = _mm_ln(a, Wd, x1, ln2_g[l:l + 1], ln2_b[l:l + 1], "ffn_down_ln")
        saved.append(dict(h=hb, qkv=qkv, conv=conv, gate=gate, cq=cq, ck=ck, mixed=mixed, rtot=rtot, lse_d=lse_d,
                          lse_f=lse_f, o_fox=o_fox, x1=x1b, xh1=xh1, r1=r1, g=g, u=u, a=a, xh2=xh2, r2=r2))
        h, hb = x2, x2b

    sq, dy = _loss_grad(h, loss_target.reshape(T, D))
    loss = lax.psum(sq[0, 0], ("x", "y", "c")) * (0.5 / D)

    def view(gr):
        return gr.reshape(4, 2, gr.shape[0] // NDEV, gr.shape[1])

    def reduce_tail(views, gots):
        return _chip_sums(_run_job(_chip_job(_pair_sums(views, gots, core)), "chip_exchange"))

    G = [None, None]
    small_g = {}
    shard_g = {}
    for l in (1, 0):
        sv = saved[l]
        Win, Wout, WgT, WuT, Wd = W[l]["win"], W[l]["wout"], W[l]["wgT"], W[l]["wuT"], W[l]["wd"]
        ds2, dg2, db2, ds2b = _ln_bwd(dy, sv["xh2"], sv["r2"], ln2_g[l:l + 1])
        dgt, dut = _ffn_da(ds2b, Wd[None], sv["g"], sv["u"], 0)
        G_d = _mm_tn(sv["a"], ds2b, None, C=D, Ka=DFF, N=D, tm=1408, tn=1024, tk=1024, ooff=0, name="grad_w_down")
        G_g = _mm_tn(dgt, sv["x1"], None, C=D, Ka=DFF, N=D, tm=1408, tn=1024, tk=1024, ooff=0, name="grad_w_gate")
        G_u = _mm_tn(dut, sv["x1"], None, C=D, Ka=DFF, N=D, tm=1408, tn=1024, tk=1024, ooff=0, name="grad_w_up")
        dx1 = _mm([(dgt, arow(512, DFF), WgT, wcol(DFF, 512, 0)), (dut, arow(512, DFF), WuT, wcol(DFF, 512, 0))],
                  nt=False, M=T, N=D, tm=512, tn=512, out_dtype=F32, name="ffn_dx", res=ds2, res_scale=ALPHA)
        ds1, dg1, db1, ds1b = _ln_bwd(dx1, sv["xh1"], sv["r1"], ln1_g[l:l + 1])
        G_out = _mm_tn(sv["mixed"], ds1b, None, C=D, Ka=D, N=D, tm=1024, tn=1024, tk=1024, ooff=0, name="grad_w_out")
        dmixed = _mm([(ds1b, arow(512, D), Wout, wrow(512, D))], nt=True, M=T, N=D, tm=512, tn=512,
                     out_dtype=BF16, name="out_proj_dx")
        if l == 0:
            early = [view(t) for t in (G[1]["in"], G[1]["out"], G[1]["g"], G[1]["u"], G[1]["d"], G_g, G_u, G_d, G_out)]
            dqkv, gots = _sb_bwd(sv["qkv"], dmixed, sv["rtot"], job=_sibling_job(early))
            ps1 = _pair_sums(early[:5], list(gots[:5]), core)
            ps0 = _pair_sums(early[5:], list(gots[5:]), core)
            dqkv, dtbl, parts0 = _flash_bwd(sv["qkv"], sv["mixed"], dmixed, sv["lse_d"], dqkv, 1, False, (tbl,),
                                            job=_chip_job(ps0))
            dqkv, dck, parts1 = _flash_bwd(sv["qkv"], sv["o_fox"], dmixed, sv["lse_f"], dqkv, 2, True,
                                           (sv["cq"], sv["ck"]), job=_chip_job(ps1))
            s1 = _chip_sums(list(parts1))
            s0 = _chip_sums(list(parts0))
            shard_g[1] = dict(zip(("in", "out", "g", "u", "d"), s1))
            shard_g[0] = dict(zip(("g", "u", "d", "out"), s0))
        else:
            dqkv, _ = _sb_bwd(sv["qkv"], dmixed, sv["rtot"])
            dqkv, dtbl, _ = _flash_bwd(sv["qkv"], sv["mixed"], dmixed, sv["lse_d"], dqkv, 1, False, (tbl,))
            dqkv, dck, _ = _flash_bwd(sv["qkv"], sv["o_fox"], dmixed, sv["lse_f"], dqkv, 2, True, (sv["cq"], sv["ck"]))
        dconv, dcw = _conv_bwd(sv["conv"], cw8[l], dmixed)
        dcum = jnp.pad(dck.reshape(BL, NH, S).transpose(0, 2, 1).reshape(T, NH), ((0, 0), (0, GATEW - NH)))
        dgate, dfb = _fox_post(dcum, sv["gate"], fb[l])
        drb = _dil_table_bwd(dtbl)
        G_in = _mm_tn(sv["h"], dqkv, None, C=NPAD, Ka=D, N=QKVW, tm=1024, tn=768, tk=1024, ooff=0, name="grad_w_in_qkv")
        G_in = _mm_tn(sv["h"], dconv, G_in, C=NPAD, Ka=D, N=CONVW, tm=1024, tn=768, tk=1024, ooff=3,
                      name="grad_w_in_conv")
        G_in = _mm_tn(sv["h"], dgate, G_in, C=NPAD, Ka=D, N=GATEW, tm=1024, tn=128, tk=1024, ooff=24,
                      name="grad_w_in_gate")
        G[l] = {"in": G_in, "out": G_out, "g": G_g, "u": G_u, "d": G_d}
        dy = _mm([(dqkv, arow(512, QKVW), Win, wrow(512, QKVW, 0)),
                  (dconv, arow(512, CONVW), Win, wrow(512, CONVW, 3)),
                  (dgate, arow(512, GATEW), Win, wrow(512, GATEW, 24))],
                 nt=True, M=T, N=D, tm=512, tn=512, out_dtype=F32, name="proj_dx", res=ds1, res_scale=ALPHA)
        small_g[l] = dict(ln1_g=dg1, ln1_b=db1, ln2_g=dg2, ln2_b=db2, cw=dcw[0:3].reshape(1, CONVW),
                          fb=dfb[:, :NH], rb=drb[:, :NH])
    grad_x = dy.reshape(BL, S, D)

    late = [view(G[0]["in"])]
    shard_g[0]["in"] = reduce_tail(late, list(_run_job(_sibling_job(late), "sibling_exchange")))[0]

    rows = []
    for name in ("ln1_g", "ln1_b", "ln2_g", "ln2_b"):
        rows += [small_g[0][name], small_g[1][name]]
    rows += [_row(small_g[0]["cw"]), _row(small_g[1]["cw"]),
             _row(jnp.concatenate([small_g[0]["fb"], small_g[1]["fb"]], axis=0)),
             _row(small_g[0]["rb"] + small_g[1]["rb"])]
    rows.append(jnp.zeros((SMALL_ROWS - len(rows), D), F32))
    sg = _allreduce_small(jnp.concatenate(rows, axis=0))
    g_ln1_g, g_ln1_b, g_ln2_g, g_ln2_b = sg[0:2], sg[2:4], sg[4:6], sg[6:8]
    g_conv_full = sg[8:10, :CONVW].reshape(2, 3, 256)
    g_conv = lax.dynamic_slice(g_conv_full, (0, 0, me * 32), (2, 3, 32))
    g_fb = sg[10, :2 * NH].reshape(2, NH)
    g_rb = sg[11, :32 * NH].reshape(32, NH)

    def both(name):
        return jnp.stack([shard_g[0][name], shard_g[1][name]])

    g_in = both("in")
    g_w_in = jnp.concatenate([_unpermute_in(g_in[..., :QKVW]), g_in[..., QKVW:NPROJ]], axis=-1)
    g_w_out = both("out")
    g_w_gate = jnp.swapaxes(both("g"), 1, 2)
    g_w_up = jnp.swapaxes(both("u"), 1, 2)
    g_w_down = both("d")

    def big(w, g, m, v, tr):
        sh = w.shape
        f = lambda t: t.reshape(-1, sh[-1])
        return [t.reshape(sh) for t in _adamw(f(w), f(g), f(m), f(v), tr)]

    up_in = big(w_in, g_w_in, m_w_in, v_w_in, 64)
    up_out = big(w_out, g_w_out, m_w_out, v_w_out, 128)
    up_gate = big(w_gate, g_w_gate, m_w_gate, v_w_gate, 256)
    up_up = big(w_up, g_w_up, m_w_up, v_w_up, 256)
    up_down = big(w_down, g_w_down, m_w_down, v_w_down, 352)

    def pack(fbv, cwv, rbv, l1g, l1b, l2g, l2b):
        r = [l1g, l1b, l2g, l2b, _row(cwv), _row(fbv), _row(rbv)]
        r.append(jnp.zeros((SMALL_ROWS - 11, D), F32))
        return jnp.concatenate(r, axis=0)

    pw = pack(f_bias, conv_w, rel_bias, ln1_g, ln1_b, ln2_g, ln2_b)
    pg = pack(g_fb, g_conv, g_rb, g_ln1_g, g_ln1_b, g_ln2_g, g_ln2_b)
    pm = pack(m_f_bias, m_conv_w, m_rel_bias, m_ln1_g, m_ln1_b, m_ln2_g, m_ln2_b)
    pv = pack(v_f_bias, v_conv_w, v_rel_bias, v_ln1_g, v_ln1_b, v_ln2_g, v_ln2_b)
    ups = _adamw(pw, pg, pm, pv, SMALL_ROWS)

    def unpack(p):
        return dict(ln1_g=p[0:2], ln1_b=p[2:4], ln2_g=p[4:6], ln2_b=p[6:8],
                    conv_w=p[8, :192].reshape(2, 3, 32), f_bias=p[9, :2 * NH].reshape(2, NH),
                    rel_bias=p[10, :32 * NH].reshape(32, NH))

    sm = [unpack(p) for p in ups]

    def group(k):
        return (up_in[k], sm[k]["f_bias"], sm[k]["conv_w"], up_out[k], sm[k]["rel_bias"], sm[k]["ln1_g"],
                sm[k]["ln1_b"], up_gate[k], up_up[k], up_down[k], sm[k]["ln2_g"], sm[k]["ln2_b"])

    grads = (g_w_in, g_fb, g_conv, g_w_out, g_rb, g_ln1_g, g_ln1_b, g_w_gate, g_w_up, g_w_down, g_ln2_g, g_ln2_b)
    return (loss, grad_x) + grads + group(0) + group(1) + group(2)
```

```python
import math

import numpy as np
import jax
import jax.numpy as jnp
from jax import lax
from jax.experimental import pallas as pl
from jax.experimental.pallas import tpu as pltpu

F32 = jnp.float32
BF16 = jnp.bfloat16
MESH = pl.DeviceIdType.MESH

D = 1024
S = 2048
BL = 2
T = BL * S
NH = 4
DFF = 2816
NPROJ = 3076
NPAD = 3200
QKVW = 2304
CONVW = 768
GATEW = 128
PAIRW = 384
BQ = 128
HB = 2 * BQ
NB = S // BQ
NDEV = 8
NSTAT = BL * NH
ALPHA = 4.0 ** 0.25
SCALE = 0.125
NEG = -1e30
LN_EPS = 1e-5
ADAM_LR, ADAM_B1, ADAM_B2, ADAM_EPS, ADAM_WD, ADAM_STEP = 0.001, 0.9, 0.999, 1e-08, 0.01, 10
VMEM_LIMIT = 56 * 1024 * 1024
SMALL_ROWS = 16


def _bucket_thresholds():
    d = np.arange(0, S)
    nf = np.maximum(d, 1).astype(np.float32)
    large = 16 + (np.log(nf / np.float32(16)) / np.float32(math.log(128)) * np.float32(16)).astype(np.int32)
    b = np.where(d < 16, d, np.minimum(large, 31))
    return [int(np.argmax(b >= k)) for k in range(32)]


BUCKET_TH = _bucket_thresholds()


def _cp(sem=None):
    return pltpu.CompilerParams(dimension_semantics=sem, vmem_limit_bytes=VMEM_LIMIT)


def _dot(a, b):
    return lax.dot_general(a, b, (((1,), (0,)), ((), ())), preferred_element_type=F32)


def _dot_nt(a, b):
    return lax.dot_general(a, b, (((1,), (1,)), ((), ())), preferred_element_type=F32)


def _dot_tn(a, b):
    return lax.dot_general(a, b, (((0,), (0,)), ((), ())), preferred_element_type=F32)


def _split2(x):
    hi = x.astype(BF16)
    mid = (x - hi.astype(F32)).astype(BF16)
    return jnp.concatenate([hi, mid], axis=1)


def _split3(x):
    hi = x.astype(BF16)
    r = x - hi.astype(F32)
    mid = r.astype(BF16)
    lo = (r - mid.astype(F32)).astype(BF16)
    return jnp.concatenate([hi, mid, lo], axis=1)


def _log_sigmoid(u):
    return jnp.minimum(u, 0.0) - jnp.log1p(jnp.exp(-jnp.abs(u)))


def _log_sigmoid_tile(u):
    return jnp.minimum(u, 0.0) - jnp.log(1.0 + jnp.exp(jnp.minimum(u, -u)))


def _iota(shape, dim):
    return lax.broadcasted_iota(jnp.int32, shape, dim)


ANY_SPEC = pl.BlockSpec(memory_space=pl.ANY)
VMEM_SPEC = pl.BlockSpec(memory_space=pltpu.VMEM)


def _mm(pairs, *, nt, M, N, tm, tn, out_dtype, name, res=None, res_scale=1.0):
    n = len(pairs)

    def body(*refs):
        acc = None
        for p in range(n):
            a = refs[2 * p][...].astype(BF16)
            b = refs[2 * p + 1][...]
            d = _dot_nt(a, b) if nt else _dot(a, b)
            acc = d if acc is None else acc + d
        if res is not None:
            acc = acc + res_scale * refs[2 * n][...]
        refs[-1][...] = acc.astype(out_dtype)

    ops, specs = [], []
    for a, asp, b, bsp in pairs:
        ops += [a, b]
        specs += [asp, bsp]
    if res is not None:
        ops.append(res)
        specs.append(pl.BlockSpec((tm, tn), lambda i, j: (i, j)))
    return pl.pallas_call(
        body, name=name, grid=(M // tm, N // tn), in_specs=specs,
        out_specs=pl.BlockSpec((tm, tn), lambda i, j: (i, j)),
        out_shape=jax.ShapeDtypeStruct((M, N), out_dtype),
        compiler_params=_cp(("parallel", "parallel")))(*ops)


def _mm_tn(a, b, gbuf, *, C, Ka, N, tm, tn, tk, ooff, name):
    def body(*refs):
        a_ref, b_ref, o_ref = refs[0], refs[1], refs[-1]
        k = pl.program_id(2)
        d = _dot_tn(a_ref[...].astype(BF16), b_ref[...].astype(BF16))

        @pl.when(k == 0)
        def _():
            o_ref[...] = d

        @pl.when(k > 0)
        def _():
            o_ref[...] += d

    ops = [a, b] + ([] if gbuf is None else [gbuf])
    return pl.pallas_call(
        body, name=name, grid=(Ka // tm, N // tn, T // tk),
        in_specs=[pl.BlockSpec((tk, tm), lambda i, j, k: (k, i)),
                  pl.BlockSpec((tk, tn), lambda i, j, k: (k, j))] + ([] if gbuf is None else [ANY_SPEC]),
        out_specs=pl.BlockSpec((tm, tn), lambda i, j, k: (i, ooff + j)),
        out_shape=jax.ShapeDtypeStruct((Ka, C), F32),
        input_output_aliases={} if gbuf is None else {2: 0},
        compiler_params=_cp(("parallel", "parallel", "arbitrary")))(*ops)


def _ffn_up(x1, wgt, wut):
    tm, tn = 1024, 256

    def body(x_ref, wg_ref, wu_ref, s_ref, t_ref, a_ref):
        xb = x_ref[...]
        g = _dot_nt(xb, wg_ref[...])
        u = _dot_nt(xb, wu_ref[...])
        sg = jax.nn.sigmoid(g)
        s = g * sg
        s_ref[...] = s.astype(BF16)
        t_ref[...] = (u * (sg * (1.0 + g * (1.0 - sg)))).astype(BF16)
        a_ref[...] = (s * u).astype(BF16)

    wspec = pl.BlockSpec((tn, D), lambda i, j: (j, 0))
    ospec = pl.BlockSpec((tm, tn), lambda i, j: (i, j))
    return pl.pallas_call(
        body, name="ffn_up", grid=(T // tm, DFF // tn),
        in_specs=[pl.BlockSpec((tm, D), lambda i, j: (i, 0)), wspec, wspec],
        out_specs=[ospec, ospec, ospec],
        out_shape=[jax.ShapeDtypeStruct((T, DFF), BF16)] * 3,
        compiler_params=_cp(("parallel", "parallel")))(x1, wgt, wut)


def _ffn_da(dffn, wd, s, t):
    tm, tn = 1024, 256

    def body(d_ref, wd_ref, s_ref, t_ref, dg_ref, du_ref):
        da = _dot_nt(d_ref[...], wd_ref[...])
        dg_ref[...] = (da * t_ref[...].astype(F32)).astype(BF16)
        du_ref[...] = (da * s_ref[...].astype(F32)).astype(BF16)

    ospec = pl.BlockSpec((tm, tn), lambda i, j: (i, j))
    return pl.pallas_call(
        body, name="ffn_da", grid=(T // tm, DFF // tn),
        in_specs=[pl.BlockSpec((tm, D), lambda i, j: (i, 0)),
                  pl.BlockSpec((tn, D), lambda i, j: (j, 0)), ospec, ospec],
        out_specs=[ospec, ospec],
        out_shape=[jax.ShapeDtypeStruct((T, DFF), BF16), jax.ShapeDtypeStruct((T, DFF), BF16)],
        compiler_params=_cp(("parallel", "parallel")))(dffn, wd, s, t)


def _mm_ln(a, w, x, gam, bet, name):
    tm = 256
    K = a.shape[1]

    def body(a_ref, w_ref, x_ref, g_ref, b_ref, y_ref, xh_ref, r_ref, yb_ref):
        s = ALPHA * x_ref[...] + _dot(a_ref[...], w_ref[...])
        mu = jnp.mean(s, axis=-1, keepdims=True)
        xc = s - mu
        var = jnp.mean(xc * xc, axis=-1, keepdims=True)
        r = lax.rsqrt(var + LN_EPS)
        xh = xc * r
        xh_ref[...] = xh
        r_ref[...] = r
        y = xh * g_ref[...] + b_ref[...]
        y_ref[...] = y
        yb_ref[...] = y.astype(BF16)

    row = pl.BlockSpec((tm, D), lambda i: (i, 0))
    vec = pl.BlockSpec((1, D), lambda i: (0, 0))
    return pl.pallas_call(
        body, name=name, grid=(T // tm,),
        in_specs=[pl.BlockSpec((tm, K), lambda i: (i, 0)), pl.BlockSpec((K, D), lambda i: (0, 0)), row, vec, vec],
        out_specs=[row, row, pl.BlockSpec((tm, 1), lambda i: (i, 0)), row],
        out_shape=[jax.ShapeDtypeStruct((T, D), F32), jax.ShapeDtypeStruct((T, D), F32),
                   jax.ShapeDtypeStruct((T, 1), F32), jax.ShapeDtypeStruct((T, D), BF16)],
        compiler_params=_cp(("parallel",)))(a, w, x, gam, bet)


def _ln_bwd(dy, xh, r, gam):
    tm = 256

    def body(dy_ref, xh_ref, r_ref, g_ref, ds_ref, dg_ref, db_ref, dsb_ref):
        i = pl.program_id(0)
        dyv = dy_ref[...]
        xhv = xh_ref[...]
        dxh = dyv * g_ref[...]
        m1 = jnp.mean(dxh, axis=-1, keepdims=True)
        m2 = jnp.mean(dxh * xhv, axis=-1, keepdims=True)
        ds = r_ref[...] * (dxh - m1 - xhv * m2)
        ds_ref[...] = ds
        dsb_ref[...] = ds.astype(BF16)
        pg = jnp.sum(dyv * xhv, axis=0, keepdims=True)
        pb = jnp.sum(dyv, axis=0, keepdims=True)

        @pl.when(i == 0)
        def _():
            dg_ref[...] = pg
            db_ref[...] = pb

        @pl.when(i > 0)
        def _():
            dg_ref[...] += pg
            db_ref[...] += pb

    row = pl.BlockSpec((tm, D), lambda i: (i, 0))
    vec = pl.BlockSpec((1, D), lambda i: (0, 0))
    return pl.pallas_call(
        body, name="ln_bwd", grid=(T // tm,),
        in_specs=[row, row, pl.BlockSpec((tm, 1), lambda i: (i, 0)), vec],
        out_specs=[row, vec, vec, row],
        out_shape=[jax.ShapeDtypeStruct((T, D), F32), jax.ShapeDtypeStruct((1, D), F32),
                   jax.ShapeDtypeStruct((1, D), F32), jax.ShapeDtypeStruct((T, D), BF16)],
        compiler_params=_cp(("arbitrary",)))(dy, xh, r, gam)


def _loss_grad(y, tgt):
    tm = 256

    def body(y_ref, t_ref, l_ref, dy_ref):
        i = pl.program_id(0)
        e = y_ref[...] - t_ref[...]
        dy_ref[...] = e * (1.0 / D)
        p = jnp.sum(jnp.sum(e * e, axis=1, keepdims=True), axis=0, keepdims=True)

        @pl.when(i == 0)
        def _():
            l_ref[...] = p

        @pl.when(i > 0)
        def _():
            l_ref[...] += p

    row = pl.BlockSpec((tm, D), lambda i: (i, 0))
    return pl.pallas_call(
        body, name="loss_grad", grid=(T // tm,), in_specs=[row, row],
        out_specs=[pl.BlockSpec((1, 1), lambda i: (0, 0)), row],
        out_shape=[jax.ShapeDtypeStruct((1, 1), F32), jax.ShapeDtypeStruct((T, D), F32)],
        compiler_params=_cp(("arbitrary",)))(y, tgt)


def _adamw(w, g, m, v, tr):
    R, C = w.shape

    def body(w_ref, g_ref, m_ref, v_ref, d_ref, m2_ref, v2_ref):
        gv = g_ref[...]
        m2 = ADAM_B1 * m_ref[...] + (1.0 - ADAM_B1) * gv
        v2 = ADAM_B2 * v_ref[...] + (1.0 - ADAM_B2) * (gv * gv)
        m_hat = m2 / (1.0 - ADAM_B1 ** ADAM_STEP)
        v_hat = v2 / (1.0 - ADAM_B2 ** ADAM_STEP)
        d_ref[...] = -ADAM_LR * (m_hat / (jnp.sqrt(v_hat) + ADAM_EPS) + ADAM_WD * w_ref[...])
        m2_ref[...] = m2
        v2_ref[...] = v2

    blk = pl.BlockSpec((tr, C), lambda i: (i, 0))
    sh = jax.ShapeDtypeStruct((R, C), F32)
    return pl.pallas_call(
        body, name="adamw", grid=(R // tr,), in_specs=[blk] * 4, out_specs=[blk] * 3,
        out_shape=[sh, sh, sh], compiler_params=_cp(("parallel",)))(w, g, m, v)


class _Job:
    def __init__(self, ins, out_shapes, aliases, sems, start, finish):
        self.ins, self.out_shapes, self.aliases, self.sems = list(ins), list(out_shapes), dict(aliases), list(sems)
        self.start, self.finish = start, finish


def _host_call(body, name, ins, in_specs, out_shapes, out_specs, scratch, aliases, job):
    n_in, n_out, n_scr = len(ins), len(out_shapes), len(scratch)
    jins = job.ins if job else []
    jouts = job.out_shapes if job else []
    jsems = job.sems if job else []

    def wrapped(*refs):
        a = n_in
        b = a + len(jins)
        c = b + n_out
        d = c + len(jouts)
        e = d + n_scr
        comm = None
        if job:
            jrefs = (refs[a:b], refs[c:d], refs[e:])
            comm = (lambda: job.start(*jrefs), lambda st: job.finish(st, *jrefs))
        body(refs[:a], refs[b:c], refs[d:e], comm)

    al = dict(aliases)
    if job:
        for ji, jo in job.aliases.items():
            al[n_in + ji] = n_out + jo
    res = pl.pallas_call(
        wrapped, name=name, in_specs=list(in_specs) + [ANY_SPEC] * len(jins),
        out_specs=list(out_specs) + [ANY_SPEC] * len(jouts), out_shape=list(out_shapes) + list(jouts),
        scratch_shapes=list(scratch) + list(jsems), input_output_aliases=al,
        compiler_params=_cp())(*ins, *jins)
    return res[:n_out], res[n_out:]


def _copy_in(src, dst, sem):
    cp = pltpu.make_async_copy(src, dst, sem)
    cp.start()
    cp.wait()


CHAINS = [(p, b) for p in range(2) for b in range(BL)]
NC = len(CHAINS)
ROWS_SHAPE = jax.ShapeDtypeStruct((NSTAT, S), F32)
SLAB_QKV = pltpu.VMEM((T, 2 * PAIRW), BF16)
SLAB_OUT = pltpu.VMEM((T, 2 * BQ), BF16)
SLAB_O32 = pltpu.VMEM((T, 2 * BQ), F32)
SLAB_T = pltpu.VMEM((2, BQ, T), BF16)
SLAB_KEYB = pltpu.VMEM((NSTAT, S, BQ), F32)
ACC_KV = pltpu.VMEM((2, T, BQ), F32)


def _lane_masks():
    lane = _iota((1, BQ), 1)
    m0 = (lane < 64).astype(BF16)
    return m0, 1.0 - m0


def _row_masks():
    r = _iota((BQ, 1), 0)
    m0 = (r < 64).astype(BF16)
    return m0, 1.0 - m0


def _stack(x, m0, m1):
    return jnp.concatenate([x * m0, x * m1], axis=0)


def _stack_t(xt, r0, r1):
    return jnp.concatenate([xt * r0, xt * r1], axis=1)


def _tr(x):
    return x.astype(F32).T.astype(BF16)


def _rows(b, i):
    return pl.ds(pl.multiple_of(b * S + i * BQ, BQ), BQ)


def _transpose_slab(src, dst, col0):
    def blk(n, _):
        r = pl.ds(pl.multiple_of(n * BQ, BQ), BQ)
        for p in range(2):
            dst[p, :, r] = _tr(src[r, col0(p):col0(p) + BQ])
        return 0

    lax.fori_loop(0, T // BQ, blk, 0)


def _heads(x):
    return x[:BQ], x[BQ:]


def _bcast_heads(r0, r1):
    return jnp.concatenate([jnp.broadcast_to(r0, (BQ, BQ)), jnp.broadcast_to(r1, (BQ, BQ))], axis=0)


def _by_channel(r0, r1):
    return jnp.where(_iota((BQ, BQ), 0) < 64, r0, r1)


def _colsum2(x):
    return jnp.sum(x[:BQ], axis=0, keepdims=True), jnp.sum(x[BQ:], axis=0, keepdims=True)


def _stat_row(ref, p, b, h, i):
    c = b * NH + 2 * p + h
    return ref[c:c + 1, pl.ds(pl.multiple_of(i * BQ, BQ), BQ)]


def _put_row(ref, p, b, h, i, v):
    c = b * NH + 2 * p + h
    ref[c:c + 1, pl.ds(pl.multiple_of(i * BQ, BQ), BQ)] = v


def _valid_t(strict):
    r = _iota((HB, BQ), 0) & (BQ - 1)
    c = _iota((HB, BQ), 1)
    return (r < c) if strict else (r <= c)


def _tri_blockdiag(later):
    r = _iota((HB, HB), 0)
    c = _iota((HB, HB), 1)
    same = (r >= BQ) == (c >= BQ)
    return (same & ((c > r) if later else (c < r))).astype(BF16)


def _cum_mm(tri, x):
    y = _dot(tri, _split2(x))
    return y[:, :BQ] + y[:, BQ:]


def _kv_tiles(qkv_v, p, b, j):
    r = _rows(b, j)
    return qkv_v[r, p * PAIRW + BQ:p * PAIRW + 2 * BQ], qkv_v[r, p * PAIRW + 2 * BQ:p * PAIRW + 3 * BQ]


def _q_tile(qkv_v, p, b, i):
    return qkv_v[_rows(b, i), p * PAIRW:p * PAIRW + BQ] * SCALE


def _sb_fwd(qkv, job=None):
    def body(ins, outs, scr, comm):
        (qkv_hbm,), (o_hbm, r_ref), (qkv_v, o_v, sem, vt_v) = ins, outs, scr
        _copy_in(qkv_hbm.at[:, pl.ds(0, 2 * PAIRW)], qkv_v, sem)
        st = comm[0]() if comm else None
        _transpose_slab(qkv_v, vt_v, lambda p: p * PAIRW + 2 * BQ)
        m0, m1 = _lane_masks()
        r0, r1 = _row_masks()
        valid = _valid_t(True)
        later = _tri_blockdiag(True)

        def steps(qts, i, j, cs, diag):
            ks = [_stack(_kv_tiles(qkv_v, p, b, j)[0], m0, m1) for p, b in CHAINS]
            zs = [_dot(ks[c], qts[c]) for c in range(NC)]
            lbs, lrs = [], []
            for c in range(NC):
                lb = _log_sigmoid_tile(zs[c])
                lr = lb - zs[c]
                if diag:
                    lr = jnp.where(valid, lr, 0.0)
                lbs.append(lb)
                lrs.append(lr)
            tails = [_cum_mm(later, lrs[c]) for c in range(NC)]
            avs = []
            for c in range(NC):
                a = jnp.exp(lbs[c] + tails[c] + _bcast_heads(*cs[c][0]))
                if diag:
                    a = jnp.where(valid, a, 0.0)
                avs.append(a.astype(BF16))
            out = []
            for c, (p, b) in enumerate(CHAINS):
                vts = _stack_t(vt_v[p, :, _rows(b, j)], r0, r1)
                s0, s1 = _colsum2(lrs[c])
                out.append(((cs[c][0][0] + s0, cs[c][0][1] + s1), cs[c][1] + _dot(vts, avs[c])))
            return tuple(out)

        def qblock(i, _):
            qts = [_tr(_q_tile(qkv_v, p, b, i)) for p, b in CHAINS]
            zr = jnp.zeros((1, BQ), F32)
            cs = steps(qts, i, i, (((zr, zr), jnp.zeros((BQ, BQ), F32)),) * NC, True)
            cs = lax.fori_loop(1, i + 1, lambda jj, cs: steps(qts, i, i - jj, cs, False), cs)
            for c, (p, b) in enumerate(CHAINS):
                o_v[_rows(b, i), p * BQ:(p + 1) * BQ] = cs[c][1].T.astype(BF16)
                for h in range(2):
                    _put_row(r_ref, p, b, h, i, cs[c][0][h])
            return 0

        lax.fori_loop(0, NB, qblock, 0)
        _copy_in(o_v, o_hbm.at[:, pl.ds(0, 2 * BQ)], sem)
        if comm:
            comm[1](st)

    (mixed, rtot), extra = _host_call(
        body, "sb_fwd", [qkv], [ANY_SPEC], [jax.ShapeDtypeStruct((T, D), BF16), ROWS_SHAPE], [ANY_SPEC, VMEM_SPEC],
        [SLAB_QKV, SLAB_OUT, pltpu.SemaphoreType.DMA, SLAB_T], {}, job)
    return mixed, rtot, extra


def _sb_bwd(qkv, dmixed, rtot, job=None):
    def body(ins, outs, scr, comm):
        (qkv_hbm, do_hbm, r_ref), (dqkv_hbm,), (qkv_v, do_v, dq_v, dk_s, dv_s, sem, kt_v) = ins, outs, scr
        _copy_in(qkv_hbm.at[:, pl.ds(0, 2 * PAIRW)], qkv_v, sem)
        _copy_in(do_hbm.at[:, pl.ds(0, 2 * BQ)], do_v, sem)
        st = comm[0]() if comm else None
        _transpose_slab(qkv_v, kt_v, lambda p: p * PAIRW + BQ)
        m0, m1 = _lane_masks()
        f0, f1 = m0.astype(F32), m1.astype(F32)
        r0, r1 = _row_masks()
        valid = _valid_t(True)
        later = _tri_blockdiag(True)
        earlier = _tri_blockdiag(False)
        dk_s[...] = jnp.zeros_like(dk_s)
        dv_s[...] = jnp.zeros_like(dv_s)

        def steps(qns, qts, dns, dts, rts, i, j, cs, diag):
            kv = [_kv_tiles(qkv_v, p, b, j) for p, b in CHAINS]
            ks = [_stack(kv[c][0], m0, m1) for c in range(NC)]
            vs = [_stack(kv[c][1], m0, m1) for c in range(NC)]
            zs = [_dot(ks[c], qts[c]) for c in range(NC)]
            das = [_dot(vs[c], dts[c]) for c in range(NC)]
            lbs, lrs, pls = [], [], []
            for c in range(NC):
                lb = _log_sigmoid_tile(zs[c])
                lr = lb - zs[c]
                if diag:
                    lr = jnp.where(valid, lr, 0.0)
                s0, s1 = _colsum2(lr)
                lbs.append(lb)
                lrs.append(lr)
                pls.append((cs[c][0][0] + s0, cs[c][0][1] + s1))
            tails = [_cum_mm(later, lrs[c]) for c in range(NC)]
            avs, gms = [], []
            for c in range(NC):
                a = jnp.exp(lbs[c] + tails[c] + _bcast_heads(rts[c][0] - pls[c][0], rts[c][1] - pls[c][1]))
                if diag:
                    a = jnp.where(valid, a, 0.0)
                avs.append(a)
                gms.append(das[c] * a)
            befores = [_cum_mm(earlier, gms[c]) for c in range(NC)]
            dzbs = []
            for c in range(NC):
                beta = jnp.exp(lbs[c])
                dz = gms[c] * (1.0 - beta) - beta * (befores[c] + _bcast_heads(*cs[c][1]))
                if diag:
                    dz = jnp.where(valid, dz, 0.0)
                dzbs.append(dz.astype(BF16))
            out = []
            for c, (p, b) in enumerate(CHAINS):
                dq = cs[c][2] + _dot(_stack_t(kt_v[p, :, _rows(b, j)], r0, r1), dzbs[c])
                dk = _dot(dzbs[c], qns[c])
                dv = _dot(avs[c].astype(BF16), dns[c])
                dk_s[p, _rows(b, j), :] += dk[:BQ] * f0 + dk[BQ:] * f1
                dv_s[p, _rows(b, j), :] += dv[:BQ] * f0 + dv[BQ:] * f1
                g0, g1 = _colsum2(gms[c])
                out.append((pls[c], (cs[c][1][0] + g0, cs[c][1][1] + g1), dq))
            return tuple(out)

        def qblock(i, _):
            qns = [_q_tile(qkv_v, p, b, i) for p, b in CHAINS]
            dns = [do_v[_rows(b, i), p * BQ:(p + 1) * BQ] for p, b in CHAINS]
            qts = [_tr(t) for t in qns]
            dts = [_tr(t) for t in dns]
            rts = [(_stat_row(r_ref, p, b, 0, i), _stat_row(r_ref, p, b, 1, i)) for p, b in CHAINS]
            zr = jnp.zeros((1, BQ), F32)
            cs = (((zr, zr), (zr, zr), jnp.zeros((BQ, BQ), F32)),) * NC
            cs = lax.fori_loop(0, i, lambda j, cs: steps(qns, qts, dns, dts, rts, i, j, cs, False), cs)
            cs = steps(qns, qts, dns, dts, rts, i, i, cs, True)
            for c, (p, b) in enumerate(CHAINS):
                dq_v[_rows(b, i), p * PAIRW:p * PAIRW + BQ] = (cs[c][2].T * SCALE).astype(BF16)
            return 0

        lax.fori_loop(0, NB, qblock, 0)
        for p in range(2):
            dq_v[:, p * PAIRW + BQ:p * PAIRW + 2 * BQ] = dk_s[p].astype(BF16)
            dq_v[:, p * PAIRW + 2 * BQ:p * PAIRW + 3 * BQ] = dv_s[p].astype(BF16)
        _copy_in(dq_v, dqkv_hbm.at[:, pl.ds(0, 2 * PAIRW)], sem)
        if comm:
            comm[1](st)

    (dqkv,), extra = _host_call(
        body, "sb_bwd", [qkv, dmixed, rtot], [ANY_SPEC, ANY_SPEC, VMEM_SPEC],
        [jax.ShapeDtypeStruct((T, QKVW), BF16)], [ANY_SPEC],
        [SLAB_QKV, SLAB_OUT, SLAB_QKV, ACC_KV, ACC_KV, pltpu.SemaphoreType.DMA, SLAB_T], {}, job)
    return dqkv, extra


def _flash_fwd(qkv, mixed, g, fox, bias, job=None):
    def body(ins, outs, scr, comm):
        if fox:
            qkv_hbm, cq_ref, ckb_hbm, _ = ins
            (o_hbm, lse_ref, o32_hbm), (qkv_v, o_v, sem, vt_v, o32_v, ckb_v) = outs, scr
        else:
            qkv_hbm, tbl_ref, _ = ins
            (o_hbm, lse_ref), (qkv_v, o_v, sem, vt_v) = outs, scr
        _copy_in(qkv_hbm.at[:, pl.ds(g * 2 * PAIRW, 2 * PAIRW)], qkv_v, sem)
        if fox:
            _copy_in(ckb_hbm, ckb_v, sem)
        st = comm[0]() if comm else None
        _transpose_slab(qkv_v, vt_v, lambda p: p * PAIRW + 2 * BQ)
        m0, m1 = _lane_masks()
        r0, r1 = _row_masks()
        valid = _valid_t(False)

        def steps(qts, cqs, i, j, cs, diag):
            ks = [_stack(_kv_tiles(qkv_v, p, b, j)[0], m0, m1) for p, b in CHAINS]
            zs = [_dot(ks[c], qts[c]) for c in range(NC)]
            prs, alphas, out = [], [], []
            for c, (p, b) in enumerate(CHAINS):
                (ma, mb), (la, lb_), _ = cs[c]
                if fox:
                    kk = pl.ds(pl.multiple_of(j * BQ, BQ), BQ)
                    col = b * NH + 2 * p
                    z = zs[c] + (cqs[c] - jnp.concatenate([ckb_v[col, kk, :], ckb_v[col + 1, kk, :]], axis=0))
                    if diag:
                        z = jnp.where(valid, z, NEG)
                else:
                    z = zs[c] + tbl_ref[p, i - j]
                za, zb = _heads(z)
                na = jnp.maximum(ma, jnp.max(za, axis=0, keepdims=True))
                nb = jnp.maximum(mb, jnp.max(zb, axis=0, keepdims=True))
                aa, ab = jnp.exp(ma - na), jnp.exp(mb - nb)
                pr = jnp.exp(z - _bcast_heads(na, nb))
                sa, sb = _colsum2(pr)
                prs.append(_split2(pr) if fox else pr.astype(BF16))
                alphas.append((aa, ab))
                out.append(((na, nb), (aa * la + sa, ab * lb_ + sb)))
            pvs = []
            for c, (p, b) in enumerate(CHAINS):
                vts = _stack_t(vt_v[p, :, _rows(b, j)], r0, r1)
                if fox:
                    pvs.append(_dot(vts, prs[c][:, :BQ]) + _dot(vts, prs[c][:, BQ:]))
                else:
                    pvs.append(_dot(vts, prs[c]))
            return tuple((out[c][0], out[c][1], _by_channel(*alphas[c]) * cs[c][2] + pvs[c]) for c in range(NC))

        def qblock(i, _):
            qts = [_tr(_q_tile(qkv_v, p, b, i)) for p, b in CHAINS]
            if fox:
                cqs = [_bcast_heads(_stat_row(cq_ref, p, b, 0, i), _stat_row(cq_ref, p, b, 1, i)) for p, b in CHAINS]
            else:
                cqs = [None] * NC
            ng = jnp.full((1, BQ), NEG, F32)
            zr = jnp.zeros((1, BQ), F32)
            cs = steps(qts, cqs, i, i, (((ng, ng), (zr, zr), jnp.zeros((BQ, BQ), F32)),) * NC, True)
            cs = lax.fori_loop(1, i + 1, lambda jj, cs: steps(qts, cqs, i, i - jj, cs, False), cs)
            for c, (p, b) in enumerate(CHAINS):
                (ma, mb), (la, lb_), acc = cs[c]
                o = (acc / _by_channel(la, lb_)).T
                o_v[_rows(b, i), p * BQ:(p + 1) * BQ] = o.astype(BF16)
                if fox:
                    o32_v[_rows(b, i), p * BQ:(p + 1) * BQ] = o
                _put_row(lse_ref, p, b, 0, i, ma + jnp.log(la))
                _put_row(lse_ref, p, b, 1, i, mb + jnp.log(lb_))
            return 0

        lax.fori_loop(0, NB, qblock, 0)
        _copy_in(o_v, o_hbm.at[:, pl.ds(g * 2 * BQ, 2 * BQ)], sem)
        if fox:
            _copy_in(o32_v, o32_hbm, sem)
        if comm:
            comm[1](st)

    bias_specs = [VMEM_SPEC, ANY_SPEC] if fox else [VMEM_SPEC]
    n_in = 2 + len(bias_specs)
    o32 = [jax.ShapeDtypeStruct((T, 2 * BQ), F32)] if fox else []
    res, extra = _host_call(
        body, "fox_fwd" if fox else "dil_fwd", [qkv, *bias, mixed], [ANY_SPEC] + bias_specs + [ANY_SPEC],
        [jax.ShapeDtypeStruct((T, D), BF16), ROWS_SHAPE] + o32, [ANY_SPEC, VMEM_SPEC] + [ANY_SPEC] * len(o32),
        [SLAB_QKV, SLAB_OUT, pltpu.SemaphoreType.DMA, SLAB_T] + ([SLAB_O32, SLAB_KEYB] if fox else []),
        {n_in - 1: 0}, job)
    return (*res, extra)


def _flash_bwd(qkv, o, dmixed, lse, dqkv, g, fox, bias, job=None):
    def body(ins, outs, scr, comm):
        if fox:
            qkv_hbm, o_hbm, do_hbm, lse_ref, cq_ref, ckb_hbm, _ = ins
            (dqkv_hbm, db_ref), (qkv_v, o_v, do_v, dq_v, dk_s, dv_s, sem, kt_v, ckb_v, dc_s) = outs, scr
        else:
            qkv_hbm, o_hbm, do_hbm, lse_ref, tbl_ref, _ = ins
            (dqkv_hbm, db_ref), (qkv_v, o_v, do_v, dq_v, dk_s, dv_s, sem, kt_v) = outs, scr
        _copy_in(qkv_hbm.at[:, pl.ds(g * 2 * PAIRW, 2 * PAIRW)], qkv_v, sem)
        _copy_in(do_hbm.at[:, pl.ds(g * 2 * BQ, 2 * BQ)], do_v, sem)
        if fox:
            _copy_in(o_hbm, o_v, sem)
            _copy_in(ckb_hbm, ckb_v, sem)
        else:
            _copy_in(o_hbm.at[:, pl.ds(g * 2 * BQ, 2 * BQ)], o_v, sem)
        st = comm[0]() if comm else None
        _transpose_slab(qkv_v, kt_v, lambda p: p * PAIRW + BQ)
        m0, m1 = _lane_masks()
        f0, f1 = m0.astype(F32), m1.astype(F32)
        r0, r1 = _row_masks()
        valid = _valid_t(False)
        dk_s[...] = jnp.zeros_like(dk_s)
        dv_s[...] = jnp.zeros_like(dv_s)
        if fox:
            dc_s[...] = jnp.zeros_like(dc_s)
        else:
            db_ref[...] = jnp.zeros_like(db_ref)

        def steps(qns, qts, dns, dts, cqs, lses, deltas, i, j, dqs, diag):
            kv = [_kv_tiles(qkv_v, p, b, j) for p, b in CHAINS]
            ks = [_stack(kv[c][0], m0, m1) for c in range(NC)]
            vs = [_stack(kv[c][1], m0, m1) for c in range(NC)]
            zs = [_dot(ks[c], qts[c]) for c in range(NC)]
            dps = [_dot(vs[c], dts[c]) for c in range(NC)]
            prs, dzl = [], []
            for c, (p, b) in enumerate(CHAINS):
                if fox:
                    kk = pl.ds(pl.multiple_of(j * BQ, BQ), BQ)
                    col = b * NH + 2 * p
                    z = zs[c] + (cqs[c] - jnp.concatenate([ckb_v[col, kk, :], ckb_v[col + 1, kk, :]], axis=0))
                    if diag:
                        z = jnp.where(valid, z, NEG)
                else:
                    z = zs[c] + tbl_ref[p, i - j]
                pr = jnp.exp(z - lses[c])
                prs.append(pr.astype(BF16))
                dzl.append(pr * (dps[c] - deltas[c]))
            dzbs = [dz.astype(BF16) for dz in dzl]
            new = []
            for c, (p, b) in enumerate(CHAINS):
                new.append(dqs[c] + _dot(_stack_t(kt_v[p, :, _rows(b, j)], r0, r1), dzbs[c]))
                dk = _dot(dzbs[c], qns[c])
                dv = _dot(prs[c], dns[c])
                dk_s[p, _rows(b, j), :] += dk[:BQ] * f0 + dk[BQ:] * f1
                dv_s[p, _rows(b, j), :] += dv[:BQ] * f0 + dv[BQ:] * f1
                if fox:
                    dc_s[c, pl.ds(pl.multiple_of(j * HB, HB), HB), :] += dzl[c]
            if not fox:
                for p in range(2):
                    db_ref[p, i - j] = db_ref[p, i - j] + (dzl[2 * p] + dzl[2 * p + 1])
            return tuple(new)

        def qblock(i, _):
            qns = [_q_tile(qkv_v, p, b, i) for p, b in CHAINS]
            dns = [do_v[_rows(b, i), p * BQ:(p + 1) * BQ] for p, b in CHAINS]
            qts = [_tr(t) for t in qns]
            dts = [_tr(t) for t in dns]
            lses = [_bcast_heads(_stat_row(lse_ref, p, b, 0, i), _stat_row(lse_ref, p, b, 1, i)) for p, b in CHAINS]
            if fox:
                cqs = [_bcast_heads(_stat_row(cq_ref, p, b, 0, i), _stat_row(cq_ref, p, b, 1, i)) for p, b in CHAINS]
            else:
                cqs = [None] * NC
            deltas = []
            for c, (p, b) in enumerate(CHAINS):
                pt = (dns[c].astype(F32) * o_v[_rows(b, i), p * BQ:(p + 1) * BQ].astype(F32)).T
                deltas.append(_bcast_heads(jnp.sum(pt[:64], axis=0, keepdims=True), jnp.sum(pt[64:], axis=0, keepdims=True)))
            dqs = (jnp.zeros((BQ, BQ), F32),) * NC
            dqs = lax.fori_loop(0, i, lambda j, d: steps(qns, qts, dns, dts, cqs, lses, deltas, i, j, d, False), dqs)
            dqs = steps(qns, qts, dns, dts, cqs, lses, deltas, i, i, dqs, True)
            for c, (p, b) in enumerate(CHAINS):
                dq_v[_rows(b, i), p * PAIRW:p * PAIRW + BQ] = (dqs[c].T * SCALE).astype(BF16)
            return 0

        lax.fori_loop(0, NB, qblock, 0)
        for p in range(2):
            dq_v[:, p * PAIRW + BQ:p * PAIRW + 2 * BQ] = dk_s[p].astype(BF16)
            dq_v[:, p * PAIRW + 2 * BQ:p * PAIRW + 3 * BQ] = dv_s[p].astype(BF16)
        _copy_in(dq_v, dqkv_hbm.at[:, pl.ds(g * 2 * PAIRW, 2 * PAIRW)], sem)
        if fox:
            lane = _iota((BQ, NSTAT), 1)

            def fold(n, _):
                t = jnp.zeros((BQ, NSTAT), F32)
                for c, (p, b) in enumerate(CHAINS):
                    s = jnp.sum(dc_s[c, pl.ds(pl.multiple_of(n * HB, HB), HB), :], axis=1, keepdims=True)
                    col = b * NH + 2 * p
                    t = t - jnp.where(lane == col, s[:BQ], 0.0) - jnp.where(lane == col + 1, s[BQ:], 0.0)
                db_ref[pl.ds(pl.multiple_of(n * BQ, BQ), BQ), :] = t
                return 0

            lax.fori_loop(0, NB, fold, 0)
        if comm:
            comm[1](st)

    if fox:
        bias_specs = [VMEM_SPEC, ANY_SPEC]
        db_shape = jax.ShapeDtypeStruct((S, NSTAT), F32)
        more = [SLAB_KEYB, pltpu.VMEM((NC, NB * HB, BQ), F32)]
    else:
        bias_specs = [VMEM_SPEC]
        db_shape = jax.ShapeDtypeStruct((2, NB, HB, BQ), F32)
        more = []
    n_in = 5 + len(bias_specs)
    (dqkv, db), extra = _host_call(
        body, "fox_bwd" if fox else "dil_bwd", [qkv, o, dmixed, lse, *bias, dqkv],
        [ANY_SPEC, ANY_SPEC, ANY_SPEC, VMEM_SPEC] + bias_specs + [ANY_SPEC],
        [jax.ShapeDtypeStruct((T, QKVW), BF16), db_shape], [ANY_SPEC, VMEM_SPEC],
        [SLAB_QKV, SLAB_O32 if fox else SLAB_OUT, SLAB_OUT, SLAB_QKV, ACC_KV, ACC_KV, pltpu.SemaphoreType.DMA, SLAB_T]
        + more, {n_in - 1: 0}, job)
    return dqkv, db, extra


def _delta_t(d):
    return d * BQ + _iota((HB, BQ), 1) - (_iota((HB, BQ), 0) & (BQ - 1))


def _buckets_in(d):
    lo, hi = max(d * BQ - (BQ - 1), 0), d * BQ + BQ - 1
    return [b for b in range(32) if BUCKET_TH[b] <= hi and (b == 31 or BUCKET_TH[b + 1] > lo)]


def _in_bucket(delta, b):
    m = delta >= BUCKET_TH[b]
    return m if b == 31 else m & (delta < BUCKET_TH[b + 1])


def _dil_table(rel_bias):
    def body(rb_ref, o_ref):
        for d in range(NB):
            delta = _delta_t(d)
            pos = delta >= 0
            n = ((pos & (delta <= 128)).astype(jnp.int32)
                 + (pos & (delta <= 512) & ((delta & 3) == 0)).astype(jnp.int32)
                 + (pos & ((delta & 15) == 0)).astype(jnp.int32))
            logn = jnp.where(n == 3, math.log(3.0), jnp.where(n == 2, math.log(2.0), jnp.where(n == 1, 0.0, NEG)))
            head1 = _iota((HB, BQ), 0) >= BQ
            for p in range(2):
                val = jnp.zeros((HB, BQ), F32)
                for b in _buckets_in(d):
                    val = jnp.where(_in_bucket(delta, b), jnp.where(head1, rb_ref[b, 2 * p + 1], rb_ref[b, 2 * p]), val)
                o_ref[p, d] = val + logn

    return pl.pallas_call(
        body, name="dil_table", in_specs=[pl.BlockSpec(memory_space=pltpu.SMEM)], out_specs=VMEM_SPEC,
        out_shape=jax.ShapeDtypeStruct((2, NB, HB, BQ), F32), compiler_params=_cp())(rel_bias)


def _dil_table_bwd(dtbl):
    def body(dt_ref, o_ref):
        p = pl.program_id(0)
        rowi = _iota((32, BQ), 0)
        lanei = _iota((32, BQ), 1)

        @pl.when(p == 0)
        def _():
            o_ref[...] = jnp.zeros_like(o_ref)

        out = jnp.zeros((32, BQ), F32)
        for b in range(32):
            acc = None
            for d in range(NB):
                if b in _buckets_in(d):
                    t = jnp.where(_in_bucket(_delta_t(d), b), dt_ref[d], 0.0)
                    acc = t if acc is None else acc + t
            rs = jnp.sum(acc, axis=1, keepdims=True)
            s0 = jnp.sum(rs[:BQ], axis=0, keepdims=True)
            s1 = jnp.sum(rs[BQ:], axis=0, keepdims=True)
            out = (out + jnp.where((rowi == b) & (lanei == 2 * p), s0, 0.0)
                   + jnp.where((rowi == b) & (lanei == 2 * p + 1), s1, 0.0))
        o_ref[...] += out

    return pl.pallas_call(
        body, name="dil_table_bwd", grid=(2,),
        in_specs=[pl.BlockSpec((None, NB, HB, BQ), lambda p: (p, 0, 0, 0))],
        out_specs=pl.BlockSpec((32, BQ), lambda p: (0, 0)),
        out_shape=jax.ShapeDtypeStruct((32, BQ), F32),
        compiler_params=_cp(("arbitrary",)))(dtbl)


def _fox_prep(gate, fb):
    def body(g_ref, fb_ref, c_ref):
        tri = (_iota((BQ, BQ), 0) >= _iota((BQ, BQ), 1)).astype(BF16)

        def blk(i, carry):
            r0 = pl.multiple_of(i * BQ, BQ)
            lf = _log_sigmoid(g_ref[pl.ds(r0, BQ), :] + fb_ref[...])
            c = _dot(tri, _split3(lf))
            c_ref[pl.ds(r0, BQ), :] = c[:, 0:BQ] + c[:, BQ:2 * BQ] + c[:, 2 * BQ:3 * BQ] + carry
            return carry + jnp.sum(lf, axis=0, keepdims=True)

        lax.fori_loop(0, NB, blk, jnp.zeros((1, BQ), F32))

    blk = pl.BlockSpec((S, GATEW), lambda b: (b, 0))
    return pl.pallas_call(
        body, name="fox_prep", grid=(BL,), in_specs=[blk, pl.BlockSpec((1, GATEW), lambda b: (0, 0))],
        out_specs=blk, out_shape=jax.ShapeDtypeStruct((T, GATEW), F32),
        compiler_params=_cp(("parallel",)))(gate, fb)


def _fox_post(dcum, gate, fb):
    def body(dc_ref, g_ref, fb_ref, dg_ref, dfb_ref):
        b = pl.program_id(0)
        tri = (_iota((BQ, BQ), 0) <= _iota((BQ, BQ), 1)).astype(BF16)

        def blk(ii, carry):
            csum, dfb = carry
            r0 = pl.multiple_of((NB - 1 - ii) * BQ, BQ)
            dc = dc_ref[pl.ds(r0, BQ), :]
            c = _dot(tri, _split3(dc))
            dlf = c[:, 0:BQ] + c[:, BQ:2 * BQ] + c[:, 2 * BQ:3 * BQ] + csum
            dg = dlf * jnp.exp(_log_sigmoid(-(g_ref[pl.ds(r0, BQ), :] + fb_ref[...])))
            dg_ref[pl.ds(r0, BQ), :] = dg
            return csum + jnp.sum(dc, axis=0, keepdims=True), dfb + jnp.sum(dg, axis=0, keepdims=True)

        z = jnp.zeros((1, BQ), F32)
        _, dfb = lax.fori_loop(0, NB, blk, (z, z))

        @pl.when(b == 0)
        def _():
            dfb_ref[...] = dfb

        @pl.when(b > 0)
        def _():
            dfb_ref[...] += dfb

    blk = pl.BlockSpec((S, GATEW), lambda b: (b, 0))
    vec = pl.BlockSpec((1, GATEW), lambda b: (0, 0))
    return pl.pallas_call(
        body, name="fox_post", grid=(BL,), in_specs=[blk, blk, vec], out_specs=[blk, vec],
        out_shape=[jax.ShapeDtypeStruct((T, GATEW), F32), jax.ShapeDtypeStruct((1, GATEW), F32)],
        compiler_params=_cp(("arbitrary",)))(dcum, gate, fb)


def _shift_down(x, n):
    return jnp.where(_iota(x.shape, 0) >= n, pltpu.roll(x, n, 0), 0.0)


def _shift_up(x, n):
    return jnp.where(_iota(x.shape, 0) < S - n, pltpu.roll(x, S - n, 0), 0.0)


def _conv_fwd(conv, cw, mixed):
    W = 256

    def body(c_ref, w_ref, _, o_ref):
        u = c_ref[:, W:2 * W] * c_ref[:, 2 * W:3 * W]
        y = w_ref[0:1, :] * _shift_down(u, 2) + w_ref[1:2, :] * _shift_down(u, 1) + w_ref[2:3, :] * u
        o_ref[...] = (c_ref[:, 0:W] * y).astype(BF16)

    return pl.pallas_call(
        body, name="conv_fwd", grid=(BL,),
        in_specs=[pl.BlockSpec((S, CONVW), lambda b: (b, 0)), pl.BlockSpec((8, W), lambda b: (0, 0)), ANY_SPEC],
        out_specs=pl.BlockSpec((S, W), lambda b: (b, 3)),
        out_shape=jax.ShapeDtypeStruct((T, D), BF16), input_output_aliases={2: 0},
        compiler_params=_cp(("parallel",)))(conv, cw, mixed)


def _conv_bwd(conv, cw, dmixed):
    W = 256

    def body(c_ref, w_ref, do_ref, dc_ref, dw_ref):
        b = pl.program_id(0)
        bg = c_ref[:, 0:W]
        cg = c_ref[:, W:2 * W]
        hv = c_ref[:, 2 * W:3 * W]
        do = do_ref[...].astype(F32)
        u = cg * hv
        u1 = _shift_down(u, 1)
        u2 = _shift_down(u, 2)
        y = w_ref[0:1, :] * u2 + w_ref[1:2, :] * u1 + w_ref[2:3, :] * u
        dy = do * bg
        du = w_ref[2:3, :] * dy + w_ref[1:2, :] * _shift_up(dy, 1) + w_ref[0:1, :] * _shift_up(dy, 2)
        dc_ref[:, 0:W] = (do * y).astype(BF16)
        dc_ref[:, W:2 * W] = (du * hv).astype(BF16)
        dc_ref[:, 2 * W:3 * W] = (du * cg).astype(BF16)
        rowi = _iota((8, W), 0)
        dw = (jnp.where(rowi == 0, jnp.sum(dy * u2, axis=0, keepdims=True), 0.0)
              + jnp.where(rowi == 1, jnp.sum(dy * u1, axis=0, keepdims=True), 0.0)
              + jnp.where(rowi == 2, jnp.sum(dy * u, axis=0, keepdims=True), 0.0))

        @pl.when(b == 0)
        def _():
            dw_ref[...] = dw

        @pl.when(b > 0)
        def _():
            dw_ref[...] += dw

    return pl.pallas_call(
        body, name="conv_bwd", grid=(BL,),
        in_specs=[pl.BlockSpec((S, CONVW), lambda b: (b, 0)), pl.BlockSpec((8, W), lambda b: (0, 0)),
                  pl.BlockSpec((S, W), lambda b: (b, 3))],
        out_specs=[pl.BlockSpec((S, CONVW), lambda b: (b, 0)), pl.BlockSpec((8, W), lambda b: (0, 0))],
        out_shape=[jax.ShapeDtypeStruct((T, CONVW), BF16), jax.ShapeDtypeStruct((8, W), F32)],
        compiler_params=_cp(("arbitrary",)))(conv, cw, dmixed)


def _place():
    x, y, c = lax.axis_index("x"), lax.axis_index("y"), lax.axis_index("c")
    return x, y, c


def _chips_of(x, y):
    return [(1 - x, y), (x, 1 - y), (1 - x, 1 - y)]


def _dev(p):
    return 4 * p[0] + 2 * p[1] + p[2]


def _gather_job_a(shards):
    n = len(shards)

    def peers(x, y, c):
        return [(x, y, 1 - c)] + [(*chip, c) for chip in _chips_of(x, y)]

    def start(ins, outs, sems):
        send, recv, loc = sems
        x, y, c = _place()
        me = (x, y, c)
        cps = []
        for a in range(n):
            cps.append(pltpu.make_async_copy(ins[a], outs[a].at[_dev(me)], loc.at[a]))
            for k, peer in enumerate(peers(x, y, c)):
                cps.append(pltpu.make_async_remote_copy(
                    src_ref=ins[a], dst_ref=outs[a].at[_dev(me)], send_sem=send.at[a, k], recv_sem=recv.at[a, k],
                    device_id=peer, device_id_type=MESH))
        for cp in cps:
            cp.start()
        return cps

    def finish(cps, ins, outs, sems):
        send, recv, loc = sems
        x, y, c = _place()
        for a in range(n):
            for k, peer in enumerate(peers(x, y, c)):
                pltpu.make_async_remote_copy(
                    src_ref=ins[a], dst_ref=outs[a].at[_dev(peer)], send_sem=send.at[a, k], recv_sem=recv.at[a, k],
                    device_id=(x, y, c), device_id_type=MESH).wait_recv()
        for a in range(n):
            cps[5 * a].wait()
            for k in range(4):
                cps[5 * a + 1 + k].wait_send()

    return _Job(shards, [jax.ShapeDtypeStruct((NDEV,) + s.shape, s.dtype) for s in shards], {},
                [pltpu.SemaphoreType.DMA((n, 4)), pltpu.SemaphoreType.DMA((n, 4)), pltpu.SemaphoreType.DMA((n,))],
                start, finish)


def _gather_job_b(gathered):
    n = len(gathered)

    def start(ins, outs, sems):
        send, recv = sems
        x, y, c = _place()
        cps = []
        for a in range(n):
            for j, chip in enumerate(_chips_of(x, y)):
                blk = outs[a].at[_dev((*chip, c))]
                cps.append(pltpu.make_async_remote_copy(
                    src_ref=blk, dst_ref=blk, send_sem=send.at[a, j], recv_sem=recv.at[a, j],
                    device_id=(x, y, 1 - c), device_id_type=MESH))
        for cp in cps:
            cp.start()
        return cps

    def finish(cps, ins, outs, sems):
        send, recv = sems
        x, y, c = _place()
        for a in range(n):
            for j, chip in enumerate(_chips_of(x, y)):
                blk = outs[a].at[_dev((*chip, 1 - c))]
                pltpu.make_async_remote_copy(
                    src_ref=blk, dst_ref=blk, send_sem=send.at[a, j], recv_sem=recv.at[a, j],
                    device_id=(x, y, c), device_id_type=MESH).wait_recv()
        for cp in cps:
            cp.wait_send()

    return _Job(gathered, [jax.ShapeDtypeStruct(g.shape, g.dtype) for g in gathered], {a: a for a in range(n)},
                [pltpu.SemaphoreType.DMA((n, 3)), pltpu.SemaphoreType.DMA((n, 3))], start, finish)


def _sibling_job(grads):
    n = len(grads)

    def start(ins, outs, sems):
        send, recv = sems
        x, y, c = _place()
        cps = [pltpu.make_async_remote_copy(
            src_ref=ins[a].at[:, 1 - c], dst_ref=outs[a], send_sem=send.at[a], recv_sem=recv.at[a],
            device_id=(x, y, 1 - c), device_id_type=MESH) for a in range(n)]
        for cp in cps:
            cp.start()
        return cps

    def finish(cps, ins, outs, sems):
        for cp in cps:
            cp.wait()

    return _Job(grads, [jax.ShapeDtypeStruct(g.shape[:1] + g.shape[2:], F32) for g in grads], {},
                [pltpu.SemaphoreType.DMA((n,)), pltpu.SemaphoreType.DMA((n,))], start, finish)


def _chip_job(psums):
    n = len(psums)

    def start(ins, outs, sems):
        send, recv, loc = sems
        x, y, c = _place()
        mychip = 2 * x + y
        cps = []
        for a in range(n):
            cps.append(pltpu.make_async_copy(ins[a].at[mychip], outs[a].at[mychip], loc.at[a]))
            for j, chip in enumerate(_chips_of(x, y)):
                cps.append(pltpu.make_async_remote_copy(
                    src_ref=ins[a].at[2 * chip[0] + chip[1]], dst_ref=outs[a].at[mychip],
                    send_sem=send.at[a, j], recv_sem=recv.at[a, j], device_id=(*chip, c), device_id_type=MESH))
        for cp in cps:
            cp.start()
        return cps

    def finish(cps, ins, outs, sems):
        send, recv, loc = sems
        x, y, c = _place()
        mychip = 2 * x + y
        for a in range(n):
            for j, chip in enumerate(_chips_of(x, y)):
                pltpu.make_async_remote_copy(
                    src_ref=ins[a].at[mychip], dst_ref=outs[a].at[2 * chip[0] + chip[1]],
                    send_sem=send.at[a, j], recv_sem=recv.at[a, j], device_id=(x, y, c), device_id_type=MESH).wait_recv()
        for a in range(n):
            cps[4 * a].wait()
            for j in range(3):
                cps[4 * a + 1 + j].wait_send()

    return _Job(psums, [jax.ShapeDtypeStruct(p.shape, BF16) for p in psums], {},
                [pltpu.SemaphoreType.DMA((n, 3)), pltpu.SemaphoreType.DMA((n, 3)), pltpu.SemaphoreType.DMA((n,))],
                start, finish)


def _join_jobs(*jobs):
    jobs = [j for j in jobs if j is not None]
    if len(jobs) <= 1:
        return jobs[0] if jobs else None
    cut = lambda seq, sizes: [seq[sum(sizes[:k]):sum(sizes[:k + 1])] for k in range(len(sizes))]
    n_in = [len(j.ins) for j in jobs]
    n_out = [len(j.out_shapes) for j in jobs]
    n_sem = [len(j.sems) for j in jobs]
    aliases = {}
    for k, j in enumerate(jobs):
        for a, b in j.aliases.items():
            aliases[sum(n_in[:k]) + a] = sum(n_out[:k]) + b

    def start(ins, outs, sems):
        return [j.start(i, o, s) for j, i, o, s in zip(jobs, cut(ins, n_in), cut(outs, n_out), cut(sems, n_sem))]

    def finish(sts, ins, outs, sems):
        for j, st, i, o, s in zip(jobs, sts, cut(ins, n_in), cut(outs, n_out), cut(sems, n_sem)):
            j.finish(st, i, o, s)

    return _Job([t for j in jobs for t in j.ins], [t for j in jobs for t in j.out_shapes], aliases,
                [t for j in jobs for t in j.sems], start, finish)


def _run_job(job, name):
    def body(ins, outs, scr, comm):
        comm[1](comm[0]())

    return _host_call(body, name, [], [], [], [], [], {}, job)[1]


def _allreduce_small(v):
    def body(v_ref, o_ref, slots, send_sems, recv_sems):
        x, y, c = _place()
        me = 4 * x + 2 * y + c
        slots[me] = v_ref[...]

        def copy(k):
            peer = (x ^ ((k >> 2) & 1), y ^ ((k >> 1) & 1), c ^ (k & 1))
            return pltpu.make_async_remote_copy(
                src_ref=v_ref, dst_ref=slots.at[me], send_sem=send_sems.at[k - 1], recv_sem=recv_sems.at[k - 1],
                device_id=peer, device_id_type=MESH)

        def arrival(k):
            return pltpu.make_async_remote_copy(
                src_ref=v_ref, dst_ref=slots.at[me ^ k], send_sem=send_sems.at[k - 1], recv_sem=recv_sems.at[k - 1],
                device_id=(x, y, c), device_id_type=MESH)

        sends = [copy(k) for k in range(1, NDEV)]
        for cp in sends:
            cp.start()
        for k in range(1, NDEV):
            arrival(k).wait_recv()
        for cp in sends:
            cp.wait_send()
        acc = slots[0]
        for d in range(1, NDEV):
            acc = acc + slots[d]
        o_ref[...] = acc

    return pl.pallas_call(
        body, name="allreduce_small", in_specs=[VMEM_SPEC], out_specs=VMEM_SPEC,
        out_shape=jax.ShapeDtypeStruct(v.shape, F32),
        scratch_shapes=[pltpu.VMEM((NDEV,) + v.shape, F32), pltpu.SemaphoreType.DMA((NDEV - 1,)),
                        pltpu.SemaphoreType.DMA((NDEV - 1,))],
        )(v)


def _pair_sums(views, gots, core):
    n = len(views)

    def body(c_ref, *refs):
        for a in range(n):
            refs[2 * n + a][...] = (refs[a][...] + refs[n + a][...]).astype(BF16)

    def vspec(v):
        return pl.BlockSpec((None, None) + v.shape[2:], lambda k, c: (k, c[0], 0, 0))

    def gspec(g):
        return pl.BlockSpec((None,) + g.shape[1:], lambda k, c: (k, 0, 0))

    return pl.pallas_call(
        body, name="pair_sums",
        grid_spec=pltpu.PrefetchScalarGridSpec(
            num_scalar_prefetch=1, grid=(4,),
            in_specs=[vspec(v) for v in views] + [gspec(g) for g in gots],
            out_specs=[gspec(g) for g in gots]),
        out_shape=[jax.ShapeDtypeStruct(g.shape, BF16) for g in gots],
        compiler_params=_cp(("parallel",)))(core, *views, *gots)


def _chip_sums(parts):
    n = len(parts)

    def body(*refs):
        for a in range(n):
            acc = refs[a][0].astype(F32)
            for k in range(1, 4):
                acc = acc + refs[a][k].astype(F32)
            refs[n + a][...] = acc

    return pl.pallas_call(
        body, name="chip_sums", in_specs=[VMEM_SPEC] * n, out_specs=[VMEM_SPEC] * n,
        out_shape=[jax.ShapeDtypeStruct(p.shape[1:], F32) for p in parts], compiler_params=_cp())(*parts)


def _permute_in(w):
    lead = w.shape[:-1]
    return w.reshape(lead + (3, 3, 2, BQ)).swapaxes(-2, -3).reshape(lead + (QKVW,))


def _unpermute_in(w):
    lead = w.shape[:-1]
    return w.reshape(lead + (3, 2, 3, BQ)).swapaxes(-2, -3).reshape(lead + (QKVW,))


def _row(v):
    v = v.reshape(-1)
    return jnp.pad(v, (0, D - v.shape[0])).reshape(1, D)


def kernel(x, w_in, f_bias, conv_w, w_out, rel_bias, ln1_g, ln1_b, w_gate, w_up, w_down, ln2_g, ln2_b, loss_target, m_w_in, m_f_bias, m_conv_w, m_w_out, m_rel_bias, m_ln1_g, m_ln1_b, m_w_gate, m_w_up, m_w_down, m_ln2_g, m_ln2_b, v_w_in, v_f_bias, v_conv_w, v_w_out, v_rel_bias, v_ln1_g, v_ln1_b, v_w_gate, v_w_up, v_w_down, v_ln2_g, v_ln2_b):
    xi, yi, ci = _place()
    me = 4 * xi + 2 * yi + ci
    core = jnp.reshape(ci, (1,)).astype(jnp.int32)

    win_s = jnp.concatenate([_permute_in(w_in[..., :QKVW]), w_in[..., QKVW:]], axis=-1)
    win_s = jnp.pad(win_s, ((0, 0), (0, 0), (0, NPAD - NPROJ))).astype(BF16)
    per_layer = [win_s, w_out.astype(BF16), jnp.swapaxes(w_gate, 1, 2).astype(BF16),
                 jnp.swapaxes(w_up, 1, 2).astype(BF16), w_down.astype(BF16)]
    sh = [[s[l] for s in per_layer] for l in range(2)]

    def whole(g):
        return g.reshape(NDEV * g.shape[1], g.shape[2])

    first = _run_job(_gather_job_b(_run_job(_gather_job_a(sh[0][:1]), "gather_a")), "gather_b")
    W = [{"win": whole(first[0])}, {}]

    cw_rows = lax.dynamic_update_slice(jnp.zeros((2, 3, 256), F32), conv_w, (0, 0, me * 32))
    small = jnp.concatenate([_row(cw_rows[0]), _row(cw_rows[1]), jnp.zeros((SMALL_ROWS - 2, D), F32)], axis=0)
    small = _allreduce_small(small)
    cw_full = small[0:2, :CONVW].reshape(2, 3, 256)
    cw8 = jnp.pad(cw_full, ((0, 0), (0, 5), (0, 0)))
    fb = jnp.pad(f_bias, ((0, 0), (0, GATEW - NH))).reshape(2, 1, GATEW)
    tbl = _dil_table(rel_bias)

    def wcol(K, tn, off):
        return pl.BlockSpec((K, tn), lambda i, j: (0, off + j))

    def wrow(tn, K, blk=0):
        return pl.BlockSpec((tn, K), lambda i, j: (j, blk))

    def arow(tm, K, blk=0):
        return pl.BlockSpec((tm, K), lambda i, j: (i, blk))

    h = x.reshape(T, D)
    hb = h.astype(BF16)
    saved = []
    for l in range(2):
        Win = W[l]["win"]
        qkv = _mm([(hb, arow(512, D), Win, wcol(D, 768, 0))], nt=False, M=T, N=QKVW, tm=512, tn=768,
                  out_dtype=BF16, name="proj_qkv")
        conv = _mm([(hb, arow(512, D), Win, wcol(D, 768, 3))], nt=False, M=T, N=CONVW, tm=512, tn=768,
                   out_dtype=F32, name="proj_conv")
        gate = _mm([(hb, arow(512, D), Win, wcol(D, 128, 24))], nt=False, M=T, N=GATEW, tm=512, tn=128,
                   out_dtype=F32, name="proj_gate")
        cum = _fox_prep(gate, fb[l])
        cq = cum[:, :NH].reshape(BL, S, NH).transpose(0, 2, 1).reshape(NSTAT, S)
        ckb = jnp.broadcast_to(cq[:, :, None], (NSTAT, S, BQ))
        if l == 0:
            mixed, rtot, a0 = _sb_fwd(qkv, job=_gather_job_a(sh[0][1:]))
            mixed, lse_d, ex = _flash_fwd(qkv, mixed, 1, False, (tbl,),
                                          job=_join_jobs(_gather_job_b(list(a0)), _gather_job_a(sh[1][:2])))
            W[0].update(zip(("wout", "wgT", "wuT", "wd"), [whole(t) for t in ex[:4]]))
            mixed, lse_f, o_fox, ex = _flash_fwd(qkv, mixed, 2, True, (cq, ckb),
                                                 job=_join_jobs(_gather_job_b(list(ex[4:])), _gather_job_a(sh[1][2:])))
            W[1].update(zip(("win", "wout"), [whole(t) for t in ex[:2]]))
            a2 = list(ex[2:])
        else:
            mixed, rtot, ex = _sb_fwd(qkv, job=_gather_job_b(a2))
            W[1].update(zip(("wgT", "wuT", "wd"), [whole(t) for t in ex]))
            mixed, lse_d, _ = _flash_fwd(qkv, mixed, 1, False, (tbl,))
            mixed, lse_f, o_fox, _ = _flash_fwd(qkv, mixed, 2, True, (cq, ckb))
        Wout, WgT, WuT, Wd = W[l]["wout"], W[l]["wgT"], W[l]["wuT"], W[l]["wd"]
        mixed = _conv_fwd(conv, cw8[l], mixed)
        x1, xh1, r1, x1b = _mm_ln(mixed, Wout, h, ln1_g[l:l + 1], ln1_b[l:l + 1], "out_proj_ln")
        fs, ft, a = _ffn_up(x1b, WgT, WuT)
        x2, xh2, r2, x2b = _mm_ln(a, Wd, x1, ln2_g[l:l + 1], ln2_b[l:l + 1], "ffn_down_ln")
        saved.append(dict(h=hb, qkv=qkv, conv=conv, gate=gate, cq=cq, ckb=ckb, mixed=mixed, rtot=rtot, lse_d=lse_d,
                          lse_f=lse_f, o_fox=o_fox, x1=x1b, xh1=xh1, r1=r1, fs=fs, ft=ft, a=a, xh2=xh2, r2=r2))
        h, hb = x2, x2b

    sq, dy = _loss_grad(h, loss_target.reshape(T, D))
    loss = lax.psum(sq[0, 0], ("x", "y", "c")) * (0.5 / D)

    def view(gr):
        return gr.reshape(4, 2, gr.shape[0] // NDEV, gr.shape[1])

    G = [None, None]
    small_g = {}
    shard_g = {}
    for l in (1, 0):
        sv = saved[l]
        Win, Wout, WgT, WuT, Wd = W[l]["win"], W[l]["wout"], W[l]["wgT"], W[l]["wuT"], W[l]["wd"]
        ds2, dg2, db2, ds2b = _ln_bwd(dy, sv["xh2"], sv["r2"], ln2_g[l:l + 1])
        dgt, dut = _ffn_da(ds2b, Wd, sv["fs"], sv["ft"])
        G_d = _mm_tn(sv["a"], ds2b, None, C=D, Ka=DFF, N=D, tm=1408, tn=1024, tk=1024, ooff=0, name="grad_w_down")
        G_g = _mm_tn(dgt, sv["x1"], None, C=D, Ka=DFF, N=D, tm=1408, tn=1024, tk=1024, ooff=0, name="grad_w_gate")
        G_u = _mm_tn(dut, sv["x1"], None, C=D, Ka=DFF, N=D, tm=1408, tn=1024, tk=1024, ooff=0, name="grad_w_up")
        dx1 = _mm([(dgt, arow(512, DFF), WgT, wcol(DFF, 512, 0)), (dut, arow(512, DFF), WuT, wcol(DFF, 512, 0))],
                  nt=False, M=T, N=D, tm=512, tn=512, out_dtype=F32, name="ffn_dx", res=ds2, res_scale=ALPHA)
        ds1, dg1, db1, ds1b = _ln_bwd(dx1, sv["xh1"], sv["r1"], ln1_g[l:l + 1])
        G_out = _mm_tn(sv["mixed"], ds1b, None, C=D, Ka=D, N=D, tm=1024, tn=1024, tk=1024, ooff=0, name="grad_w_out")
        dmixed = _mm([(ds1b, arow(512, D), Wout, wrow(512, D))], nt=True, M=T, N=D, tm=512, tn=512,
                     out_dtype=BF16, name="out_proj_dx")
        if l == 0:
            early = [view(t) for t in (G[1]["in"], G[1]["out"], G[1]["g"], G[1]["u"], G[1]["d"], G_g, G_u, G_d, G_out)]
            dqkv, gots = _sb_bwd(sv["qkv"], dmixed, sv["rtot"], job=_sibling_job(early))
            ps1 = _pair_sums(early[:5], list(gots[:5]), core)
            ps0 = _pair_sums(early[5:], list(gots[5:]), core)
            dqkv, dtbl, parts0 = _flash_bwd(sv["qkv"], sv["mixed"], dmixed, sv["lse_d"], dqkv, 1, False, (tbl,),
                                            job=_chip_job(ps0))
            dqkv, dck, parts1 = _flash_bwd(sv["qkv"], sv["o_fox"], dmixed, sv["lse_f"], dqkv, 2, True,
                                           (sv["cq"], sv["ckb"]), job=_chip_job(ps1))
            shard_g[1] = dict(zip(("in", "out", "g", "u", "d"), _chip_sums(list(parts1))))
            shard_g[0] = dict(zip(("g", "u", "d", "out"), _chip_sums(list(parts0))))
        else:
            dqkv, _ = _sb_bwd(sv["qkv"], dmixed, sv["rtot"])
            dqkv, dtbl, _ = _flash_bwd(sv["qkv"], sv["mixed"], dmixed, sv["lse_d"], dqkv, 1, False, (tbl,))
            dqkv, dck, _ = _flash_bwd(sv["qkv"], sv["o_fox"], dmixed, sv["lse_f"], dqkv, 2, True, (sv["cq"], sv["ckb"]))
        dconv, dcw = _conv_bwd(sv["conv"], cw8[l], dmixed)
        dcum = jnp.pad(dck.reshape(S, BL, NH).transpose(1, 0, 2).reshape(T, NH), ((0, 0), (0, GATEW - NH)))
        dgate, dfb = _fox_post(dcum, sv["gate"], fb[l])
        drb = _dil_table_bwd(dtbl)
        G_in = _mm_tn(sv["h"], dqkv, None, C=NPAD, Ka=D, N=QKVW, tm=1024, tn=768, tk=1024, ooff=0, name="grad_w_in_qkv")
        G_in = _mm_tn(sv["h"], dconv, G_in, C=NPAD, Ka=D, N=CONVW, tm=1024, tn=768, tk=1024, ooff=3,
                      name="grad_w_in_conv")
        G_in = _mm_tn(sv["h"], dgate, G_in, C=NPAD, Ka=D, N=GATEW, tm=1024, tn=128, tk=1024, ooff=24,
                      name="grad_w_in_gate")
        G[l] = {"in": G_in, "out": G_out, "g": G_g, "u": G_u, "d": G_d}
        dy = _mm([(dqkv, arow(512, QKVW), Win, wrow(512, QKVW, 0)),
                  (dconv, arow(512, CONVW), Win, wrow(512, CONVW, 3)),
                  (dgate, arow(512, GATEW), Win, wrow(512, GATEW, 24))],
                 nt=True, M=T, N=D, tm=512, tn=512, out_dtype=F32, name="proj_dx", res=ds1, res_scale=ALPHA)
        small_g[l] = dict(ln1_g=dg1, ln1_b=db1, ln2_g=dg2, ln2_b=db2, cw=dcw[0:3].reshape(1, CONVW),
                          fb=dfb[:, :NH], rb=drb[:, :NH])
    grad_x = dy.reshape(BL, S, D)

    late = [view(G[0]["in"])]
    gots = list(_run_job(_sibling_job(late), "sibling_exchange"))
    shard_g[0]["in"] = _chip_sums(_run_job(_chip_job(_pair_sums(late, gots, core)), "chip_exchange"))[0]

    rows = []
    for name in ("ln1_g", "ln1_b", "ln2_g", "ln2_b"):
        rows += [small_g[0][name], small_g[1][name]]
    rows += [_row(small_g[0]["cw"]), _row(small_g[1]["cw"]),
             _row(jnp.concatenate([small_g[0]["fb"], small_g[1]["fb"]], axis=0)),
             _row(small_g[0]["rb"] + small_g[1]["rb"])]
    rows.append(jnp.zeros((SMALL_ROWS - len(rows), D), F32))
    sg = _allreduce_small(jnp.concatenate(rows, axis=0))
    g_ln1_g, g_ln1_b, g_ln2_g, g_ln2_b = sg[0:2], sg[2:4], sg[4:6], sg[6:8]
    g_conv_full = sg[8:10, :CONVW].reshape(2, 3, 256)
    g_conv = lax.dynamic_slice(g_conv_full, (0, 0, me * 32), (2, 3, 32))
    g_fb = sg[10, :2 * NH].reshape(2, NH)
    g_rb = sg[11, :32 * NH].reshape(32, NH)

    def both(name):
        return jnp.stack([shard_g[0][name], shard_g[1][name]])

    g_in = both("in")
    g_w_in = jnp.concatenate([_unpermute_in(g_in[..., :QKVW]), g_in[..., QKVW:NPROJ]], axis=-1)
    g_w_out = both("out")
    g_w_gate = jnp.swapaxes(both("g"), 1, 2)
    g_w_up = jnp.swapaxes(both("u"), 1, 2)
    g_w_down = both("d")

    def big(w, g, m, v, tr):
        shp = w.shape
        f = lambda t: t.reshape(-1, shp[-1])
        return [t.reshape(shp) for t in _adamw(f(w), f(g), f(m), f(v), tr)]

    up_in = big(w_in, g_w_in, m_w_in, v_w_in, 64)
    up_out = big(w_out, g_w_out, m_w_out, v_w_out, 128)
    up_gate = big(w_gate, g_w_gate, m_w_gate, v_w_gate, 256)
    up_up = big(w_up, g_w_up, m_w_up, v_w_up, 256)
    up_down = big(w_down, g_w_down, m_w_down, v_w_down, 352)

    def pack(fbv, cwv, rbv, l1g, l1b, l2g, l2b):
        r = [l1g, l1b, l2g, l2b, _row(cwv), _row(fbv), _row(rbv)]
        r.append(jnp.zeros((SMALL_ROWS - 11, D), F32))
        return jnp.concatenate(r, axis=0)

    pw = pack(f_bias, conv_w, rel_bias, ln1_g, ln1_b, ln2_g, ln2_b)
    pg = pack(g_fb, g_conv, g_rb, g_ln1_g, g_ln1_b, g_ln2_g, g_ln2_b)
    pm = pack(m_f_bias, m_conv_w, m_rel_bias, m_ln1_g, m_ln1_b, m_ln2_g, m_ln2_b)
    pv = pack(v_f_bias, v_conv_w, v_rel_bias, v_ln1_g, v_ln1_b, v_ln2_g, v_ln2_b)
    ups = _adamw(pw, pg, pm, pv, SMALL_ROWS)

    def unpack(p):
        return dict(ln1_g=p[0:2], ln1_b=p[2:4], ln2_g=p[4:6], ln2_b=p[6:8],
                    conv_w=p[8, :192].reshape(2, 3, 32), f_bias=p[9, :2 * NH].reshape(2, NH),
                    rel_bias=p[10, :32 * NH].reshape(32, NH))

    sm = [unpack(p) for p in ups]

    def group(k):
        return (up_in[k], sm[k]["f_bias"], sm[k]["conv_w"], up_out[k], sm[k]["rel_bias"], sm[k]["ln1_g"],
                sm[k]["ln1_b"], up_gate[k], up_up[k], up_down[k], sm[k]["ln2_g"], sm[k]["ln2_b"])

    grads = (g_w_in, g_fb, g_conv, g_w_out, g_rb, g_ln1_g, g_ln1_b, g_w_gate, g_w_up, g_w_down, g_ln2_g, g_ln2_b)
    return (loss, grad_x) + grads + group(0) + group(1) + group(2)
```

```python
import math

import numpy as np
import jax
import jax.numpy as jnp
from jax import lax
from jax.experimental import pallas as pl
from jax.experimental.pallas import tpu as pltpu

F32 = jnp.float32
BF16 = jnp.bfloat16
MESH = pl.DeviceIdType.MESH

D = 1024
S = 2048
BL = 2
T = BL * S
NH = 4
DFF = 2816
NPROJ = 3076
NPAD = 3200
QKVW = 2304
CONVW = 768
GATEW = 128
PAIRW = 384
BQ = 128
HB = 2 * BQ
NB = S // BQ
NDEV = 8
NSTAT = BL * NH
ALPHA = 4.0 ** 0.25
SCALE = 0.125
NEG = -1e30
LN_EPS = 1e-5
ADAM_LR, ADAM_B1, ADAM_B2, ADAM_EPS, ADAM_WD, ADAM_STEP = 0.001, 0.9, 0.999, 1e-08, 0.01, 10
VMEM_LIMIT = 56 * 1024 * 1024
SMALL_ROWS = 16


def _bucket_thresholds():
    d = np.arange(0, S)
    nf = np.maximum(d, 1).astype(np.float32)
    large = 16 + (np.log(nf / np.float32(16)) / np.float32(math.log(128)) * np.float32(16)).astype(np.int32)
    b = np.where(d < 16, d, np.minimum(large, 31))
    return [int(np.argmax(b >= k)) for k in range(32)]


BUCKET_TH = _bucket_thresholds()


def _cp(sem=None):
    return pltpu.CompilerParams(dimension_semantics=sem, vmem_limit_bytes=VMEM_LIMIT)


def _dot(a, b):
    return lax.dot_general(a, b, (((1,), (0,)), ((), ())), preferred_element_type=F32)


def _dot_nt(a, b):
    return lax.dot_general(a, b, (((1,), (1,)), ((), ())), preferred_element_type=F32)


def _dot_tn(a, b):
    return lax.dot_general(a, b, (((0,), (0,)), ((), ())), preferred_element_type=F32)


def _split2(x):
    hi = x.astype(BF16)
    mid = (x - hi.astype(F32)).astype(BF16)
    return jnp.concatenate([hi, mid], axis=1)


def _split3(x):
    hi = x.astype(BF16)
    r = x - hi.astype(F32)
    mid = r.astype(BF16)
    lo = (r - mid.astype(F32)).astype(BF16)
    return jnp.concatenate([hi, mid, lo], axis=1)


def _log_sigmoid(u):
    return jnp.minimum(u, 0.0) - jnp.log1p(jnp.exp(-jnp.abs(u)))


def _log_sigmoid_tile(u):
    return jnp.minimum(u, 0.0) - jnp.log(1.0 + jnp.exp(jnp.minimum(u, -u)))


def _iota(shape, dim):
    return lax.broadcasted_iota(jnp.int32, shape, dim)


ANY_SPEC = pl.BlockSpec(memory_space=pl.ANY)
VMEM_SPEC = pl.BlockSpec(memory_space=pltpu.VMEM)


def _mm(pairs, *, nt, M, N, tm, tn, out_dtype, name, res=None, res_scale=1.0):
    n = len(pairs)

    def body(*refs):
        acc = None
        for p in range(n):
            a = refs[2 * p][...].astype(BF16)
            b = refs[2 * p + 1][...]
            d = _dot_nt(a, b) if nt else _dot(a, b)
            acc = d if acc is None else acc + d
        if res is not None:
            acc = acc + res_scale * refs[2 * n][...]
        refs[-1][...] = acc.astype(out_dtype)

    ops, specs = [], []
    for a, asp, b, bsp in pairs:
        ops += [a, b]
        specs += [asp, bsp]
    if res is not None:
        ops.append(res)
        specs.append(pl.BlockSpec((tm, tn), lambda i, j: (i, j)))
    return pl.pallas_call(
        body, name=name, grid=(M // tm, N // tn), in_specs=specs,
        out_specs=pl.BlockSpec((tm, tn), lambda i, j: (i, j)),
        out_shape=jax.ShapeDtypeStruct((M, N), out_dtype),
        compiler_params=_cp(("parallel", "parallel")))(*ops)


def _mm_tn(a, b, gbuf, *, C, Ka, N, tm, tn, tk, ooff, name):
    def body(*refs):
        a_ref, b_ref, o_ref = refs[0], refs[1], refs[-1]
        k = pl.program_id(2)
        d = _dot_tn(a_ref[...].astype(BF16), b_ref[...].astype(BF16))

        @pl.when(k == 0)
        def _():
            o_ref[...] = d

        @pl.when(k > 0)
        def _():
            o_ref[...] += d

    ops = [a, b] + ([] if gbuf is None else [gbuf])
    return pl.pallas_call(
        body, name=name, grid=(Ka // tm, N // tn, T // tk),
        in_specs=[pl.BlockSpec((tk, tm), lambda i, j, k: (k, i)),
                  pl.BlockSpec((tk, tn), lambda i, j, k: (k, j))] + ([] if gbuf is None else [ANY_SPEC]),
        out_specs=pl.BlockSpec((tm, tn), lambda i, j, k: (i, ooff + j)),
        out_shape=jax.ShapeDtypeStruct((Ka, C), F32),
        input_output_aliases={} if gbuf is None else {2: 0},
        compiler_params=_cp(("parallel", "parallel", "arbitrary")))(*ops)


def _ffn_up(x1, wgt, wut):
    tm, tn = 1024, 256

    def body(x_ref, wg_ref, wu_ref, s_ref, t_ref, a_ref):
        xb = x_ref[...]
        g = _dot_nt(xb, wg_ref[...])
        u = _dot_nt(xb, wu_ref[...])
        sg = jax.nn.sigmoid(g)
        s = g * sg
        s_ref[...] = s.astype(BF16)
        t_ref[...] = (u * (sg * (1.0 + g * (1.0 - sg)))).astype(BF16)
        a_ref[...] = (s * u).astype(BF16)

    wspec = pl.BlockSpec((tn, D), lambda i, j: (j, 0))
    ospec = pl.BlockSpec((tm, tn), lambda i, j: (i, j))
    return pl.pallas_call(
        body, name="ffn_up", grid=(T // tm, DFF // tn),
        in_specs=[pl.BlockSpec((tm, D), lambda i, j: (i, 0)), wspec, wspec],
        out_specs=[ospec, ospec, ospec],
        out_shape=[jax.ShapeDtypeStruct((T, DFF), BF16)] * 3,
        compiler_params=_cp(("parallel", "parallel")))(x1, wgt, wut)


def _ffn_da(dffn, wd, s, t):
    tm, tn = 1024, 256

    def body(d_ref, wd_ref, s_ref, t_ref, dg_ref, du_ref):
        da = _dot_nt(d_ref[...], wd_ref[...])
        dg_ref[...] = (da * t_ref[...].astype(F32)).astype(BF16)
        du_ref[...] = (da * s_ref[...].astype(F32)).astype(BF16)

    ospec = pl.BlockSpec((tm, tn), lambda i, j: (i, j))
    return pl.pallas_call(
        body, name="ffn_da", grid=(T // tm, DFF // tn),
        in_specs=[pl.BlockSpec((tm, D), lambda i, j: (i, 0)),
                  pl.BlockSpec((tn, D), lambda i, j: (j, 0)), ospec, ospec],
        out_specs=[ospec, ospec],
        out_shape=[jax.ShapeDtypeStruct((T, DFF), BF16), jax.ShapeDtypeStruct((T, DFF), BF16)],
        compiler_params=_cp(("parallel", "parallel")))(dffn, wd, s, t)


def _mm_ln(a, w, x, gam, bet, name):
    tm = 256
    K = a.shape[1]

    def body(a_ref, w_ref, x_ref, g_ref, b_ref, y_ref, xh_ref, r_ref, yb_ref):
        s = ALPHA * x_ref[...] + _dot(a_ref[...], w_ref[...])
        mu = jnp.mean(s, axis=-1, keepdims=True)
        xc = s - mu
        var = jnp.mean(xc * xc, axis=-1, keepdims=True)
        r = lax.rsqrt(var + LN_EPS)
        xh = xc * r
        xh_ref[...] = xh
        r_ref[...] = r
        y = xh * g_ref[...] + b_ref[...]
        y_ref[...] = y
        yb_ref[...] = y.astype(BF16)

    row = pl.BlockSpec((tm, D), lambda i: (i, 0))
    vec = pl.BlockSpec((1, D), lambda i: (0, 0))
    return pl.pallas_call(
        body, name=name, grid=(T // tm,),
        in_specs=[pl.BlockSpec((tm, K), lambda i: (i, 0)), pl.BlockSpec((K, D), lambda i: (0, 0)), row, vec, vec],
        out_specs=[row, row, pl.BlockSpec((tm, 1), lambda i: (i, 0)), row],
        out_shape=[jax.ShapeDtypeStruct((T, D), F32), jax.ShapeDtypeStruct((T, D), F32),
                   jax.ShapeDtypeStruct((T, 1), F32), jax.ShapeDtypeStruct((T, D), BF16)],
        compiler_params=_cp(("parallel",)))(a, w, x, gam, bet)


def _ln_bwd(dy, xh, r, gam):
    tm = 256

    def body(dy_ref, xh_ref, r_ref, g_ref, ds_ref, dg_ref, db_ref, dsb_ref):
        i = pl.program_id(0)
        dyv = dy_ref[...]
        xhv = xh_ref[...]
        dxh = dyv * g_ref[...]
        m1 = jnp.mean(dxh, axis=-1, keepdims=True)
        m2 = jnp.mean(dxh * xhv, axis=-1, keepdims=True)
        ds = r_ref[...] * (dxh - m1 - xhv * m2)
        ds_ref[...] = ds
        dsb_ref[...] = ds.astype(BF16)
        pg = jnp.sum(dyv * xhv, axis=0, keepdims=True)
        pb = jnp.sum(dyv, axis=0, keepdims=True)

        @pl.when(i == 0)
        def _():
            dg_ref[...] = pg
            db_ref[...] = pb

        @pl.when(i > 0)
        def _():
            dg_ref[...] += pg
            db_ref[...] += pb

    row = pl.BlockSpec((tm, D), lambda i: (i, 0))
    vec = pl.BlockSpec((1, D), lambda i: (0, 0))
    return pl.pallas_call(
        body, name="ln_bwd", grid=(T // tm,),
        in_specs=[row, row, pl.BlockSpec((tm, 1), lambda i: (i, 0)), vec],
        out_specs=[row, vec, vec, row],
        out_shape=[jax.ShapeDtypeStruct((T, D), F32), jax.ShapeDtypeStruct((1, D), F32),
                   jax.ShapeDtypeStruct((1, D), F32), jax.ShapeDtypeStruct((T, D), BF16)],
        compiler_params=_cp(("arbitrary",)))(dy, xh, r, gam)


def _loss_grad(y, tgt):
    tm = 256

    def body(y_ref, t_ref, l_ref, dy_ref):
        i = pl.program_id(0)
        e = y_ref[...] - t_ref[...]
        dy_ref[...] = e * (1.0 / D)
        p = jnp.sum(jnp.sum(e * e, axis=1, keepdims=True), axis=0, keepdims=True)

        @pl.when(i == 0)
        def _():
            l_ref[...] = p

        @pl.when(i > 0)
        def _():
            l_ref[...] += p

    row = pl.BlockSpec((tm, D), lambda i: (i, 0))
    return pl.pallas_call(
        body, name="loss_grad", grid=(T // tm,), in_specs=[row, row],
        out_specs=[pl.BlockSpec((1, 1), lambda i: (0, 0)), row],
        out_shape=[jax.ShapeDtypeStruct((1, 1), F32), jax.ShapeDtypeStruct((T, D), F32)],
        compiler_params=_cp(("arbitrary",)))(y, tgt)


def _adamw(w, g, m, v, tr):
    R, C = w.shape

    def body(w_ref, g_ref, m_ref, v_ref, d_ref, m2_ref, v2_ref):
        gv = g_ref[...]
        m2 = ADAM_B1 * m_ref[...] + (1.0 - ADAM_B1) * gv
        v2 = ADAM_B2 * v_ref[...] + (1.0 - ADAM_B2) * (gv * gv)
        m_hat = m2 / (1.0 - ADAM_B1 ** ADAM_STEP)
        v_hat = v2 / (1.0 - ADAM_B2 ** ADAM_STEP)
        d_ref[...] = -ADAM_LR * (m_hat / (jnp.sqrt(v_hat) + ADAM_EPS) + ADAM_WD * w_ref[...])
        m2_ref[...] = m2
        v2_ref[...] = v2

    blk = pl.BlockSpec((tr, C), lambda i: (i, 0))
    sh = jax.ShapeDtypeStruct((R, C), F32)
    return pl.pallas_call(
        body, name="adamw", grid=(R // tr,), in_specs=[blk] * 4, out_specs=[blk] * 3,
        out_shape=[sh, sh, sh], compiler_params=_cp(("parallel",)))(w, g, m, v)


class _Job:
    def __init__(self, ins, out_shapes, aliases, sems, start, finish):
        self.ins, self.out_shapes, self.aliases, self.sems = list(ins), list(out_shapes), dict(aliases), list(sems)
        self.start, self.finish = start, finish


def _host_call(body, name, ins, in_specs, out_shapes, out_specs, scratch, aliases, job):
    n_in, n_out, n_scr = len(ins), len(out_shapes), len(scratch)
    jins = job.ins if job else []
    jouts = job.out_shapes if job else []
    jsems = job.sems if job else []

    def wrapped(*refs):
        a = n_in
        b = a + len(jins)
        c = b + n_out
        d = c + len(jouts)
        e = d + n_scr
        comm = None
        if job:
            jrefs = (refs[a:b], refs[c:d], refs[e:])
            comm = (lambda: job.start(*jrefs), lambda st: job.finish(st, *jrefs))
        body(refs[:a], refs[b:c], refs[d:e], comm)

    al = dict(aliases)
    if job:
        for ji, jo in job.aliases.items():
            al[n_in + ji] = n_out + jo
    res = pl.pallas_call(
        wrapped, name=name, in_specs=list(in_specs) + [ANY_SPEC] * len(jins),
        out_specs=list(out_specs) + [ANY_SPEC] * len(jouts), out_shape=list(out_shapes) + list(jouts),
        scratch_shapes=list(scratch) + list(jsems), input_output_aliases=al,
        compiler_params=_cp())(*ins, *jins)
    return res[:n_out], res[n_out:]


def _copy_in(src, dst, sem):
    cp = pltpu.make_async_copy(src, dst, sem)
    cp.start()
    cp.wait()


CHAINS = [(p, b) for p in range(2) for b in range(BL)]
NC = len(CHAINS)
ROWS_SHAPE = jax.ShapeDtypeStruct((NSTAT, S), F32)
SLAB_QKV = pltpu.VMEM((T, 2 * PAIRW), BF16)
SLAB_OUT = pltpu.VMEM((T, 2 * BQ), BF16)
SLAB_O32 = pltpu.VMEM((T, 2 * BQ), F32)
SLAB_T = pltpu.VMEM((2, BQ, T), BF16)
SLAB_KEYB = pltpu.VMEM((NSTAT, S, BQ), F32)
ACC_KV = pltpu.VMEM((2, T, BQ), F32)


def _lane_masks():
    lane = _iota((1, BQ), 1)
    m0 = (lane < 64).astype(BF16)
    return m0, 1.0 - m0


def _row_masks():
    r = _iota((BQ, 1), 0)
    m0 = (r < 64).astype(BF16)
    return m0, 1.0 - m0


def _stack(x, m0, m1):
    return jnp.concatenate([x * m0, x * m1], axis=0)


def _stack_t(xt, r0, r1):
    return jnp.concatenate([xt * r0, xt * r1], axis=1)


def _tr(x):
    return x.astype(F32).T.astype(BF16)


def _rows(b, i):
    return pl.ds(pl.multiple_of(b * S + i * BQ, BQ), BQ)


def _transpose_slab(src, dst, col0):
    def blk(n, _):
        r = pl.ds(pl.multiple_of(n * BQ, BQ), BQ)
        for p in range(2):
            dst[p, :, r] = _tr(src[r, col0(p):col0(p) + BQ])
        return 0

    lax.fori_loop(0, T // BQ, blk, 0)


def _heads(x):
    return x[:BQ], x[BQ:]


def _bcast_heads(r0, r1):
    return jnp.concatenate([jnp.broadcast_to(r0, (BQ, BQ)), jnp.broadcast_to(r1, (BQ, BQ))], axis=0)


def _by_channel(r0, r1):
    return jnp.where(_iota((BQ, BQ), 0) < 64, r0, r1)


def _colsum2(x):
    return jnp.sum(x[:BQ], axis=0, keepdims=True), jnp.sum(x[BQ:], axis=0, keepdims=True)


def _stat_row(ref, p, b, h, i):
    c = b * NH + 2 * p + h
    return ref[c:c + 1, pl.ds(pl.multiple_of(i * BQ, BQ), BQ)]


def _put_row(ref, p, b, h, i, v):
    c = b * NH + 2 * p + h
    ref[c:c + 1, pl.ds(pl.multiple_of(i * BQ, BQ), BQ)] = v


def _valid_t(strict):
    r = _iota((HB, BQ), 0) & (BQ - 1)
    c = _iota((HB, BQ), 1)
    return (r < c) if strict else (r <= c)


def _tri_blockdiag(later):
    r = _iota((HB, HB), 0)
    c = _iota((HB, HB), 1)
    same = (r >= BQ) == (c >= BQ)
    return (same & ((c > r) if later else (c < r))).astype(BF16)


def _cum_mm(tri, x):
    y = _dot(tri, _split2(x))
    return y[:, :BQ] + y[:, BQ:]


def _kv_tiles(qkv_v, p, b, j):
    r = _rows(b, j)
    return qkv_v[r, p * PAIRW + BQ:p * PAIRW + 2 * BQ], qkv_v[r, p * PAIRW + 2 * BQ:p * PAIRW + 3 * BQ]


def _q_tile(qkv_v, p, b, i):
    return qkv_v[_rows(b, i), p * PAIRW:p * PAIRW + BQ] * SCALE


def _sb_fwd(qkv, job=None):
    def body(ins, outs, scr, comm):
        (qkv_hbm,), (o_hbm, r_ref), (qkv_v, o_v, sem, vt_v) = ins, outs, scr
        _copy_in(qkv_hbm.at[:, pl.ds(0, 2 * PAIRW)], qkv_v, sem)
        st = comm[0]() if comm else None
        _transpose_slab(qkv_v, vt_v, lambda p: p * PAIRW + 2 * BQ)
        m0, m1 = _lane_masks()
        r0, r1 = _row_masks()
        valid = _valid_t(True)
        later = _tri_blockdiag(True)

        def scores(qts, j):
            return tuple(_dot(_stack(_kv_tiles(qkv_v, p, b, j)[0], m0, m1), qts[c]) for c, (p, b) in enumerate(CHAINS))

        def steps(qts, i, j, zs, cs, diag):
            zs_next = scores(qts, jnp.maximum(j - 1, 0))
            lbs, lrs = [], []
            for c in range(NC):
                lb = _log_sigmoid_tile(zs[c])
                lr = lb - zs[c]
                if diag:
                    lr = jnp.where(valid, lr, 0.0)
                lbs.append(lb)
                lrs.append(lr)
            tails = [_cum_mm(later, lrs[c]) for c in range(NC)]
            avs = []
            for c in range(NC):
                a = jnp.exp(lbs[c] + tails[c] + _bcast_heads(*cs[c][0]))
                if diag:
                    a = jnp.where(valid, a, 0.0)
                avs.append(a.astype(BF16))
            out = []
            for c, (p, b) in enumerate(CHAINS):
                vts = _stack_t(vt_v[p, :, _rows(b, j)], r0, r1)
                s0, s1 = _colsum2(lrs[c])
                out.append(((cs[c][0][0] + s0, cs[c][0][1] + s1), cs[c][1] + _dot(vts, avs[c])))
            return zs_next, tuple(out)

        def qblock(i, _):
            qts = [_tr(_q_tile(qkv_v, p, b, i)) for p, b in CHAINS]
            zr = jnp.zeros((1, BQ), F32)
            zc = steps(qts, i, i, scores(qts, i), (((zr, zr), jnp.zeros((BQ, BQ), F32)),) * NC, True)
            _, cs = lax.fori_loop(1, i + 1, lambda jj, zc: steps(qts, i, i - jj, zc[0], zc[1], False), zc)
            for c, (p, b) in enumerate(CHAINS):
                o_v[_rows(b, i), p * BQ:(p + 1) * BQ] = cs[c][1].T.astype(BF16)
                for h in range(2):
                    _put_row(r_ref, p, b, h, i, cs[c][0][h])
            return 0

        lax.fori_loop(0, NB, qblock, 0)
        _copy_in(o_v, o_hbm.at[:, pl.ds(0, 2 * BQ)], sem)
        if comm:
            comm[1](st)

    (mixed, rtot), extra = _host_call(
        body, "sb_fwd", [qkv], [ANY_SPEC], [jax.ShapeDtypeStruct((T, D), BF16), ROWS_SHAPE], [ANY_SPEC, VMEM_SPEC],
        [SLAB_QKV, SLAB_OUT, pltpu.SemaphoreType.DMA, SLAB_T], {}, job)
    return mixed, rtot, extra


def _sb_bwd(qkv, dmixed, rtot, job=None):
    def body(ins, outs, scr, comm):
        (qkv_hbm, do_hbm, r_ref), (dqkv_hbm,), (qkv_v, do_v, dq_v, dk_s, dv_s, sem, kt_v) = ins, outs, scr
        _copy_in(qkv_hbm.at[:, pl.ds(0, 2 * PAIRW)], qkv_v, sem)
        _copy_in(do_hbm.at[:, pl.ds(0, 2 * BQ)], do_v, sem)
        st = comm[0]() if comm else None
        _transpose_slab(qkv_v, kt_v, lambda p: p * PAIRW + BQ)
        m0, m1 = _lane_masks()
        f0, f1 = m0.astype(F32), m1.astype(F32)
        r0, r1 = _row_masks()
        valid = _valid_t(True)
        later = _tri_blockdiag(True)
        earlier = _tri_blockdiag(False)
        dk_s[...] = jnp.zeros_like(dk_s)
        dv_s[...] = jnp.zeros_like(dv_s)

        def lead(qts, dts, j):
            kv = [_kv_tiles(qkv_v, p, b, j) for p, b in CHAINS]
            zs = tuple(_dot(_stack(kv[c][0], m0, m1), qts[c]) for c in range(NC))
            das = tuple(_dot(_stack(kv[c][1], m0, m1), dts[c]) for c in range(NC))
            return zs, das

        def steps(qns, qts, dns, dts, rts, i, j, ahead, cs, diag):
            zs, das = ahead
            nxt = ahead if diag else lead(qts, dts, j + 1)
            lbs, lrs, pls = [], [], []
            for c in range(NC):
                lb = _log_sigmoid_tile(zs[c])
                lr = lb - zs[c]
                if diag:
                    lr = jnp.where(valid, lr, 0.0)
                s0, s1 = _colsum2(lr)
                lbs.append(lb)
                lrs.append(lr)
                pls.append((cs[c][0][0] + s0, cs[c][0][1] + s1))
            tails = [_cum_mm(later, lrs[c]) for c in range(NC)]
            avs, gms = [], []
            for c in range(NC):
                a = jnp.exp(lbs[c] + tails[c] + _bcast_heads(rts[c][0] - pls[c][0], rts[c][1] - pls[c][1]))
                if diag:
                    a = jnp.where(valid, a, 0.0)
                avs.append(a)
                gms.append(das[c] * a)
            befores = [_cum_mm(earlier, gms[c]) for c in range(NC)]
            dzbs = []
            for c in range(NC):
                beta = jnp.exp(lbs[c])
                dz = gms[c] * (1.0 - beta) - beta * (befores[c] + _bcast_heads(*cs[c][1]))
                if diag:
                    dz = jnp.where(valid, dz, 0.0)
                dzbs.append(dz.astype(BF16))
            out = []
            for c, (p, b) in enumerate(CHAINS):
                dq = cs[c][2] + _dot(_stack_t(kt_v[p, :, _rows(b, j)], r0, r1), dzbs[c])
                dk = _dot(dzbs[c], qns[c])
                dv = _dot(avs[c].astype(BF16), dns[c])
                dk_s[p, _rows(b, j), :] += dk[:BQ] * f0 + dk[BQ:] * f1
                dv_s[p, _rows(b, j), :] += dv[:BQ] * f0 + dv[BQ:] * f1
                g0, g1 = _colsum2(gms[c])
                out.append((pls[c], (cs[c][1][0] + g0, cs[c][1][1] + g1), dq))
            return nxt, tuple(out)

        def qblock(i, _):
            qns = [_q_tile(qkv_v, p, b, i) for p, b in CHAINS]
            dns = [do_v[_rows(b, i), p * BQ:(p + 1) * BQ] for p, b in CHAINS]
            qts = [_tr(t) for t in qns]
            dts = [_tr(t) for t in dns]
            rts = [(_stat_row(r_ref, p, b, 0, i), _stat_row(r_ref, p, b, 1, i)) for p, b in CHAINS]
            zr = jnp.zeros((1, BQ), F32)
            cs = (((zr, zr), (zr, zr), jnp.zeros((BQ, BQ), F32)),) * NC
            ac = lax.fori_loop(0, i, lambda j, ac: steps(qns, qts, dns, dts, rts, i, j, ac[0], ac[1], False),
                               (lead(qts, dts, 0), cs))
            _, cs = steps(qns, qts, dns, dts, rts, i, i, ac[0], ac[1], True)
            for c, (p, b) in enumerate(CHAINS):
                dq_v[_rows(b, i), p * PAIRW:p * PAIRW + BQ] = (cs[c][2].T * SCALE).astype(BF16)
            return 0

        lax.fori_loop(0, NB, qblock, 0)
        for p in range(2):
            dq_v[:, p * PAIRW + BQ:p * PAIRW + 2 * BQ] = dk_s[p].astype(BF16)
            dq_v[:, p * PAIRW + 2 * BQ:p * PAIRW + 3 * BQ] = dv_s[p].astype(BF16)
        _copy_in(dq_v, dqkv_hbm.at[:, pl.ds(0, 2 * PAIRW)], sem)
        if comm:
            comm[1](st)

    (dqkv,), extra = _host_call(
        body, "sb_bwd", [qkv, dmixed, rtot], [ANY_SPEC, ANY_SPEC, VMEM_SPEC],
        [jax.ShapeDtypeStruct((T, QKVW), BF16)], [ANY_SPEC],
        [SLAB_QKV, SLAB_OUT, SLAB_QKV, ACC_KV, ACC_KV, pltpu.SemaphoreType.DMA, SLAB_T], {}, job)
    return dqkv, extra


def _flash_fwd(qkv, mixed, g, fox, bias, job=None):
    def body(ins, outs, scr, comm):
        if fox:
            qkv_hbm, cq_ref, ckb_hbm, _ = ins
            (o_hbm, lse_ref, o32_hbm), (qkv_v, o_v, sem, vt_v, o32_v, ckb_v) = outs, scr
        else:
            qkv_hbm, tbl_ref, _ = ins
            (o_hbm, lse_ref), (qkv_v, o_v, sem, vt_v) = outs, scr
        _copy_in(qkv_hbm.at[:, pl.ds(g * 2 * PAIRW, 2 * PAIRW)], qkv_v, sem)
        if fox:
            _copy_in(ckb_hbm, ckb_v, sem)
        st = comm[0]() if comm else None
        _transpose_slab(qkv_v, vt_v, lambda p: p * PAIRW + 2 * BQ)
        m0, m1 = _lane_masks()
        r0, r1 = _row_masks()
        valid = _valid_t(False)

        def scores(qts, j):
            return tuple(_dot(_stack(_kv_tiles(qkv_v, p, b, j)[0], m0, m1), qts[c]) for c, (p, b) in enumerate(CHAINS))

        def steps(qts, cqs, i, j, zs, cs, diag):
            zs_next = scores(qts, jnp.maximum(j - 1, 0))
            prs, alphas, out = [], [], []
            for c, (p, b) in enumerate(CHAINS):
                (ma, mb), (la, lb_), _ = cs[c]
                if fox:
                    kk = pl.ds(pl.multiple_of(j * BQ, BQ), BQ)
                    col = b * NH + 2 * p
                    z = zs[c] + (cqs[c] - jnp.concatenate([ckb_v[col, kk, :], ckb_v[col + 1, kk, :]], axis=0))
                    if diag:
                        z = jnp.where(valid, z, NEG)
                else:
                    z = zs[c] + tbl_ref[p, i - j]
                za, zb = _heads(z)
                na = jnp.maximum(ma, jnp.max(za, axis=0, keepdims=True))
                nb = jnp.maximum(mb, jnp.max(zb, axis=0, keepdims=True))
                aa, ab = jnp.exp(ma - na), jnp.exp(mb - nb)
                pr = jnp.exp(z - _bcast_heads(na, nb))
                sa, sb = _colsum2(pr)
                prs.append(_split2(pr) if fox else pr.astype(BF16))
                alphas.append((aa, ab))
                out.append(((na, nb), (aa * la + sa, ab * lb_ + sb)))
            pvs = []
            for c, (p, b) in enumerate(CHAINS):
                vts = _stack_t(vt_v[p, :, _rows(b, j)], r0, r1)
                if fox:
                    pvs.append(_dot(vts, prs[c][:, :BQ]) + _dot(vts, prs[c][:, BQ:]))
                else:
                    pvs.append(_dot(vts, prs[c]))
            return zs_next, tuple((out[c][0], out[c][1], _by_channel(*alphas[c]) * cs[c][2] + pvs[c]) for c in range(NC))

        def qblock(i, _):
            qts = [_tr(_q_tile(qkv_v, p, b, i)) for p, b in CHAINS]
            if fox:
                cqs = [_bcast_heads(_stat_row(cq_ref, p, b, 0, i), _stat_row(cq_ref, p, b, 1, i)) for p, b in CHAINS]
            else:
                cqs = [None] * NC
            ng = jnp.full((1, BQ), NEG, F32)
            zr = jnp.zeros((1, BQ), F32)
            zc = steps(qts, cqs, i, i, scores(qts, i), (((ng, ng), (zr, zr), jnp.zeros((BQ, BQ), F32)),) * NC, True)
            _, cs = lax.fori_loop(1, i + 1, lambda jj, zc: steps(qts, cqs, i, i - jj, zc[0], zc[1], False), zc)
            for c, (p, b) in enumerate(CHAINS):
                (ma, mb), (la, lb_), acc = cs[c]
                o = (acc / _by_channel(la, lb_)).T
                o_v[_rows(b, i), p * BQ:(p + 1) * BQ] = o.astype(BF16)
                if fox:
                    o32_v[_rows(b, i), p * BQ:(p + 1) * BQ] = o
                _put_row(lse_ref, p, b, 0, i, ma + jnp.log(la))
                _put_row(lse_ref, p, b, 1, i, mb + jnp.log(lb_))
            return 0

        lax.fori_loop(0, NB, qblock, 0)
        _copy_in(o_v, o_hbm.at[:, pl.ds(g * 2 * BQ, 2 * BQ)], sem)
        if fox:
            _copy_in(o32_v, o32_hbm, sem)
        if comm:
            comm[1](st)

    bias_specs = [VMEM_SPEC, ANY_SPEC] if fox else [VMEM_SPEC]
    n_in = 2 + len(bias_specs)
    o32 = [jax.ShapeDtypeStruct((T, 2 * BQ), F32)] if fox else []
    res, extra = _host_call(
        body, "fox_fwd" if fox else "dil_fwd", [qkv, *bias, mixed], [ANY_SPEC] + bias_specs + [ANY_SPEC],
        [jax.ShapeDtypeStruct((T, D), BF16), ROWS_SHAPE] + o32, [ANY_SPEC, VMEM_SPEC] + [ANY_SPEC] * len(o32),
        [SLAB_QKV, SLAB_OUT, pltpu.SemaphoreType.DMA, SLAB_T] + ([SLAB_O32, SLAB_KEYB] if fox else []),
        {n_in - 1: 0}, job)
    return (*res, extra)


def _flash_bwd(qkv, o, dmixed, lse, dqkv, g, fox, bias, job=None):
    def body(ins, outs, scr, comm):
        if fox:
            qkv_hbm, o_hbm, do_hbm, lse_ref, cq_ref, ckb_hbm, _ = ins
            (dqkv_hbm, db_ref), (qkv_v, o_v, do_v, dq_v, dk_s, dv_s, sem, kt_v, ckb_v, dc_s) = outs, scr
        else:
            qkv_hbm, o_hbm, do_hbm, lse_ref, tbl_ref, _ = ins
            (dqkv_hbm, db_ref), (qkv_v, o_v, do_v, dq_v, dk_s, dv_s, sem, kt_v) = outs, scr
        _copy_in(qkv_hbm.at[:, pl.ds(g * 2 * PAIRW, 2 * PAIRW)], qkv_v, sem)
        _copy_in(do_hbm.at[:, pl.ds(g * 2 * BQ, 2 * BQ)], do_v, sem)
        if fox:
            _copy_in(o_hbm, o_v, sem)
            _copy_in(ckb_hbm, ckb_v, sem)
        else:
            _copy_in(o_hbm.at[:, pl.ds(g * 2 * BQ, 2 * BQ)], o_v, sem)
        st = comm[0]() if comm else None
        _transpose_slab(qkv_v, kt_v, lambda p: p * PAIRW + BQ)
        m0, m1 = _lane_masks()
        f0, f1 = m0.astype(F32), m1.astype(F32)
        r0, r1 = _row_masks()
        valid = _valid_t(False)
        dk_s[...] = jnp.zeros_like(dk_s)
        dv_s[...] = jnp.zeros_like(dv_s)
        if fox:
            dc_s[...] = jnp.zeros_like(dc_s)
        else:
            db_ref[...] = jnp.zeros_like(db_ref)

        def lead(qts, dts, j):
            kv = [_kv_tiles(qkv_v, p, b, j) for p, b in CHAINS]
            zs = tuple(_dot(_stack(kv[c][0], m0, m1), qts[c]) for c in range(NC))
            dps = tuple(_dot(_stack(kv[c][1], m0, m1), dts[c]) for c in range(NC))
            return zs, dps

        def steps(qns, qts, dns, dts, cqs, lses, deltas, i, j, ahead, dqs, diag):
            zs, dps = ahead
            nxt = ahead if diag else lead(qts, dts, j + 1)
            prs, dzl = [], []
            for c, (p, b) in enumerate(CHAINS):
                if fox:
                    kk = pl.ds(pl.multiple_of(j * BQ, BQ), BQ)
                    col = b * NH + 2 * p
                    z = zs[c] + (cqs[c] - jnp.concatenate([ckb_v[col, kk, :], ckb_v[col + 1, kk, :]], axis=0))
                    if diag:
                        z = jnp.where(valid, z, NEG)
                else:
                    z = zs[c] + tbl_ref[p, i - j]
                pr = jnp.exp(z - lses[c])
                prs.append(pr.astype(BF16))
                dzl.append(pr * (dps[c] - deltas[c]))
            dzbs = [dz.astype(BF16) for dz in dzl]
            new = []
            for c, (p, b) in enumerate(CHAINS):
                new.append(dqs[c] + _dot(_stack_t(kt_v[p, :, _rows(b, j)], r0, r1), dzbs[c]))
                dk = _dot(dzbs[c], qns[c])
                dv = _dot(prs[c], dns[c])
                dk_s[p, _rows(b, j), :] += dk[:BQ] * f0 + dk[BQ:] * f1
                dv_s[p, _rows(b, j), :] += dv[:BQ] * f0 + dv[BQ:] * f1
                if fox:
                    dc_s[c, pl.ds(pl.multiple_of(j * HB, HB), HB), :] += dzl[c]
            if not fox:
                for p in range(2):
                    db_ref[p, i - j] = db_ref[p, i - j] + (dzl[2 * p] + dzl[2 * p + 1])
            return nxt, tuple(new)

        def qblock(i, _):
            qns = [_q_tile(qkv_v, p, b, i) for p, b in CHAINS]
            dns = [do_v[_rows(b, i), p * BQ:(p + 1) * BQ] for p, b in CHAINS]
            qts = [_tr(t) for t in qns]
            dts = [_tr(t) for t in dns]
            lses = [_bcast_heads(_stat_row(lse_ref, p, b, 0, i), _stat_row(lse_ref, p, b, 1, i)) for p, b in CHAINS]
            if fox:
                cqs = [_bcast_heads(_stat_row(cq_ref, p, b, 0, i), _stat_row(cq_ref, p, b, 1, i)) for p, b in CHAINS]
            else:
                cqs = [None] * NC
            deltas = []
            for c, (p, b) in enumerate(CHAINS):
                pt = (dns[c].astype(F32) * o_v[_rows(b, i), p * BQ:(p + 1) * BQ].astype(F32)).T
                deltas.append(_bcast_heads(jnp.sum(pt[:64], axis=0, keepdims=True), jnp.sum(pt[64:], axis=0, keepdims=True)))
            dqs = (jnp.zeros((BQ, BQ), F32),) * NC
            ad = lax.fori_loop(
                0, i, lambda j, ad: steps(qns, qts, dns, dts, cqs, lses, deltas, i, j, ad[0], ad[1], False),
                (lead(qts, dts, 0), dqs))
            _, dqs = steps(qns, qts, dns, dts, cqs, lses, deltas, i, i, ad[0], ad[1], True)
            for c, (p, b) in enumerate(CHAINS):
                dq_v[_rows(b, i), p * PAIRW:p * PAIRW + BQ] = (dqs[c].T * SCALE).astype(BF16)
            return 0

        lax.fori_loop(0, NB, qblock, 0)
        for p in range(2):
            dq_v[:, p * PAIRW + BQ:p * PAIRW + 2 * BQ] = dk_s[p].astype(BF16)
            dq_v[:, p * PAIRW + 2 * BQ:p * PAIRW + 3 * BQ] = dv_s[p].astype(BF16)
        _copy_in(dq_v, dqkv_hbm.at[:, pl.ds(g * 2 * PAIRW, 2 * PAIRW)], sem)
        if fox:
            lane = _iota((BQ, NSTAT), 1)

            def fold(n, _):
                t = jnp.zeros((BQ, NSTAT), F32)
                for c, (p, b) in enumerate(CHAINS):
                    s = jnp.sum(dc_s[c, pl.ds(pl.multiple_of(n * HB, HB), HB), :], axis=1, keepdims=True)
                    col = b * NH + 2 * p
                    t = t - jnp.where(lane == col, s[:BQ], 0.0) - jnp.where(lane == col + 1, s[BQ:], 0.0)
                db_ref[pl.ds(pl.multiple_of(n * BQ, BQ), BQ), :] = t
                return 0

            lax.fori_loop(0, NB, fold, 0)
        if comm:
            comm[1](st)

    if fox:
        bias_specs = [VMEM_SPEC, ANY_SPEC]
        db_shape = jax.ShapeDtypeStruct((S, NSTAT), F32)
        more = [SLAB_KEYB, pltpu.VMEM((NC, NB * HB, BQ), F32)]
    else:
        bias_specs = [VMEM_SPEC]
        db_shape = jax.ShapeDtypeStruct((2, NB, HB, BQ), F32)
        more = []
    n_in = 5 + len(bias_specs)
    (dqkv, db), extra = _host_call(
        body, "fox_bwd" if fox else "dil_bwd", [qkv, o, dmixed, lse, *bias, dqkv],
        [ANY_SPEC, ANY_SPEC, ANY_SPEC, VMEM_SPEC] + bias_specs + [ANY_SPEC],
        [jax.ShapeDtypeStruct((T, QKVW), BF16), db_shape], [ANY_SPEC, VMEM_SPEC],
        [SLAB_QKV, SLAB_O32 if fox else SLAB_OUT, SLAB_OUT, SLAB_QKV, ACC_KV, ACC_KV, pltpu.SemaphoreType.DMA, SLAB_T]
        + more, {n_in - 1: 0}, job)
    return dqkv, db, extra


def _delta_t(d):
    return d * BQ + _iota((HB, BQ), 1) - (_iota((HB, BQ), 0) & (BQ - 1))


def _buckets_in(d):
    lo, hi = max(d * BQ - (BQ - 1), 0), d * BQ + BQ - 1
    return [b for b in range(32) if BUCKET_TH[b] <= hi and (b == 31 or BUCKET_TH[b + 1] > lo)]


def _in_bucket(delta, b):
    m = delta >= BUCKET_TH[b]
    return m if b == 31 else m & (delta < BUCKET_TH[b + 1])


def _dil_table(rel_bias):
    def body(rb_ref, o_ref):
        for d in range(NB):
            delta = _delta_t(d)
            pos = delta >= 0
            n = ((pos & (delta <= 128)).astype(jnp.int32)
                 + (pos & (delta <= 512) & ((delta & 3) == 0)).astype(jnp.int32)
                 + (pos & ((delta & 15) == 0)).astype(jnp.int32))
            logn = jnp.where(n == 3, math.log(3.0), jnp.where(n == 2, math.log(2.0), jnp.where(n == 1, 0.0, NEG)))
            head1 = _iota((HB, BQ), 0) >= BQ
            for p in range(2):
                val = jnp.zeros((HB, BQ), F32)
                for b in _buckets_in(d):
                    val = jnp.where(_in_bucket(delta, b), jnp.where(head1, rb_ref[b, 2 * p + 1], rb_ref[b, 2 * p]), val)
                o_ref[p, d] = val + logn

    return pl.pallas_call(
        body, name="dil_table", in_specs=[pl.BlockSpec(memory_space=pltpu.SMEM)], out_specs=VMEM_SPEC,
        out_shape=jax.ShapeDtypeStruct((2, NB, HB, BQ), F32), compiler_params=_cp())(rel_bias)


def _dil_table_bwd(dtbl):
    def body(dt_ref, o_ref):
        p = pl.program_id(0)
        rowi = _iota((32, BQ), 0)
        lanei = _iota((32, BQ), 1)

        @pl.when(p == 0)
        def _():
            o_ref[...] = jnp.zeros_like(o_ref)

        out = jnp.zeros((32, BQ), F32)
        for b in range(32):
            acc = None
            for d in range(NB):
                if b in _buckets_in(d):
                    t = jnp.where(_in_bucket(_delta_t(d), b), dt_ref[d], 0.0)
                    acc = t if acc is None else acc + t
            rs = jnp.sum(acc, axis=1, keepdims=True)
            s0 = jnp.sum(rs[:BQ], axis=0, keepdims=True)
            s1 = jnp.sum(rs[BQ:], axis=0, keepdims=True)
            out = (out + jnp.where((rowi == b) & (lanei == 2 * p), s0, 0.0)
                   + jnp.where((rowi == b) & (lanei == 2 * p + 1), s1, 0.0))
        o_ref[...] += out

    return pl.pallas_call(
        body, name="dil_table_bwd", grid=(2,),
        in_specs=[pl.BlockSpec((None, NB, HB, BQ), lambda p: (p, 0, 0, 0))],
        out_specs=pl.BlockSpec((32, BQ), lambda p: (0, 0)),
        out_shape=jax.ShapeDtypeStruct((32, BQ), F32),
        compiler_params=_cp(("arbitrary",)))(dtbl)


def _fox_prep(gate, fb):
    def body(g_ref, fb_ref, c_ref):
        tri = (_iota((BQ, BQ), 0) >= _iota((BQ, BQ), 1)).astype(BF16)

        def blk(i, carry):
            r0 = pl.multiple_of(i * BQ, BQ)
            lf = _log_sigmoid(g_ref[pl.ds(r0, BQ), :] + fb_ref[...])
            c = _dot(tri, _split3(lf))
            c_ref[pl.ds(r0, BQ), :] = c[:, 0:BQ] + c[:, BQ:2 * BQ] + c[:, 2 * BQ:3 * BQ] + carry
            return carry + jnp.sum(lf, axis=0, keepdims=True)

        lax.fori_loop(0, NB, blk, jnp.zeros((1, BQ), F32))

    blk = pl.BlockSpec((S, GATEW), lambda b: (b, 0))
    return pl.pallas_call(
        body, name="fox_prep", grid=(BL,), in_specs=[blk, pl.BlockSpec((1, GATEW), lambda b: (0, 0))],
        out_specs=blk, out_shape=jax.ShapeDtypeStruct((T, GATEW), F32),
        compiler_params=_cp(("parallel",)))(gate, fb)


def _fox_post(dcum, gate, fb):
    def body(dc_ref, g_ref, fb_ref, dg_ref, dfb_ref):
        b = pl.program_id(0)
        tri = (_iota((BQ, BQ), 0) <= _iota((BQ, BQ), 1)).astype(BF16)

        def blk(ii, carry):
            csum, dfb = carry
            r0 = pl.multiple_of((NB - 1 - ii) * BQ, BQ)
            dc = dc_ref[pl.ds(r0, BQ), :]
            c = _dot(tri, _split3(dc))
            dlf = c[:, 0:BQ] + c[:, BQ:2 * BQ] + c[:, 2 * BQ:3 * BQ] + csum
            dg = dlf * jnp.exp(_log_sigmoid(-(g_ref[pl.ds(r0, BQ), :] + fb_ref[...])))
            dg_ref[pl.ds(r0, BQ), :] = dg
            return csum + jnp.sum(dc, axis=0, keepdims=True), dfb + jnp.sum(dg, axis=0, keepdims=True)

        z = jnp.zeros((1, BQ), F32)
        _, dfb = lax.fori_loop(0, NB, blk, (z, z))

        @pl.when(b == 0)
        def _():
            dfb_ref[...] = dfb

        @pl.when(b > 0)
        def _():
            dfb_ref[...] += dfb

    blk = pl.BlockSpec((S, GATEW), lambda b: (b, 0))
    vec = pl.BlockSpec((1, GATEW), lambda b: (0, 0))
    return pl.pallas_call(
        body, name="fox_post", grid=(BL,), in_specs=[blk, blk, vec], out_specs=[blk, vec],
        out_shape=[jax.ShapeDtypeStruct((T, GATEW), F32), jax.ShapeDtypeStruct((1, GATEW), F32)],
        compiler_params=_cp(("arbitrary",)))(dcum, gate, fb)


def _shift_down(x, n):
    return jnp.where(_iota(x.shape, 0) >= n, pltpu.roll(x, n, 0), 0.0)


def _shift_up(x, n):
    return jnp.where(_iota(x.shape, 0) < S - n, pltpu.roll(x, S - n, 0), 0.0)


def _conv_fwd(conv, cw, mixed):
    W = 256

    def body(c_ref, w_ref, _, o_ref):
        u = c_ref[:, W:2 * W] * c_ref[:, 2 * W:3 * W]
        y = w_ref[0:1, :] * _shift_down(u, 2) + w_ref[1:2, :] * _shift_down(u, 1) + w_ref[2:3, :] * u
        o_ref[...] = (c_ref[:, 0:W] * y).astype(BF16)

    return pl.pallas_call(
        body, name="conv_fwd", grid=(BL,),
        in_specs=[pl.BlockSpec((S, CONVW), lambda b: (b, 0)), pl.BlockSpec((8, W), lambda b: (0, 0)), ANY_SPEC],
        out_specs=pl.BlockSpec((S, W), lambda b: (b, 3)),
        out_shape=jax.ShapeDtypeStruct((T, D), BF16), input_output_aliases={2: 0},
        compiler_params=_cp(("parallel",)))(conv, cw, mixed)


def _conv_bwd(conv, cw, dmixed):
    W = 256

    def body(c_ref, w_ref, do_ref, dc_ref, dw_ref):
        b = pl.program_id(0)
        bg = c_ref[:, 0:W]
        cg = c_ref[:, W:2 * W]
        hv = c_ref[:, 2 * W:3 * W]
        do = do_ref[...].astype(F32)
        u = cg * hv
        u1 = _shift_down(u, 1)
        u2 = _shift_down(u, 2)
        y = w_ref[0:1, :] * u2 + w_ref[1:2, :] * u1 + w_ref[2:3, :] * u
        dy = do * bg
        du = w_ref[2:3, :] * dy + w_ref[1:2, :] * _shift_up(dy, 1) + w_ref[0:1, :] * _shift_up(dy, 2)
        dc_ref[:, 0:W] = (do * y).astype(BF16)
        dc_ref[:, W:2 * W] = (du * hv).astype(BF16)
        dc_ref[:, 2 * W:3 * W] = (du * cg).astype(BF16)
        rowi = _iota((8, W), 0)
        dw = (jnp.where(rowi == 0, jnp.sum(dy * u2, axis=0, keepdims=True), 0.0)
              + jnp.where(rowi == 1, jnp.sum(dy * u1, axis=0, keepdims=True), 0.0)
              + jnp.where(rowi == 2, jnp.sum(dy * u, axis=0, keepdims=True), 0.0))

        @pl.when(b == 0)
        def _():
            dw_ref[...] = dw

        @pl.when(b > 0)
        def _():
            dw_ref[...] += dw

    return pl.pallas_call(
        body, name="conv_bwd", grid=(BL,),
        in_specs=[pl.BlockSpec((S, CONVW), lambda b: (b, 0)), pl.BlockSpec((8, W), lambda b: (0, 0)),
                  pl.BlockSpec((S, W), lambda b: (b, 3))],
        out_specs=[pl.BlockSpec((S, CONVW), lambda b: (b, 0)), pl.BlockSpec((8, W), lambda b: (0, 0))],
        out_shape=[jax.ShapeDtypeStruct((T, CONVW), BF16), jax.ShapeDtypeStruct((8, W), F32)],
        compiler_params=_cp(("arbitrary",)))(conv, cw, dmixed)


def _place():
    x, y, c = lax.axis_index("x"), lax.axis_index("y"), lax.axis_index("c")
    return x, y, c


def _chips_of(x, y):
    return [(1 - x, y), (x, 1 - y), (1 - x, 1 - y)]


def _dev(p):
    return 4 * p[0] + 2 * p[1] + p[2]


def _gather_job_a(shards):
    n = len(shards)

    def peers(x, y, c):
        return [(x, y, 1 - c)] + [(*chip, c) for chip in _chips_of(x, y)]

    def start(ins, outs, sems):
        send, recv, loc = sems
        x, y, c = _place()
        me = (x, y, c)
        cps = []
        for a in range(n):
            cps.append(pltpu.make_async_copy(ins[a], outs[a].at[_dev(me)], loc.at[a]))
            for k, peer in enumerate(peers(x, y, c)):
                cps.append(pltpu.make_async_remote_copy(
                    src_ref=ins[a], dst_ref=outs[a].at[_dev(me)], send_sem=send.at[a, k], recv_sem=recv.at[a, k],
                    device_id=peer, device_id_type=MESH))
        for cp in cps:
            cp.start()
        return cps

    def finish(cps, ins, outs, sems):
        send, recv, loc = sems
        x, y, c = _place()
        for a in range(n):
            for k, peer in enumerate(peers(x, y, c)):
                pltpu.make_async_remote_copy(
                    src_ref=ins[a], dst_ref=outs[a].at[_dev(peer)], send_sem=send.at[a, k], recv_sem=recv.at[a, k],
                    device_id=(x, y, c), device_id_type=MESH).wait_recv()
        for a in range(n):
            cps[5 * a].wait()
            for k in range(4):
                cps[5 * a + 1 + k].wait_send()

    return _Job(shards, [jax.ShapeDtypeStruct((NDEV,) + s.shape, s.dtype) for s in shards], {},
                [pltpu.SemaphoreType.DMA((n, 4)), pltpu.SemaphoreType.DMA((n, 4)), pltpu.SemaphoreType.DMA((n,))],
                start, finish)


def _gather_job_b(gathered):
    n = len(gathered)

    def start(ins, outs, sems):
        send, recv = sems
        x, y, c = _place()
        cps = []
        for a in range(n):
            for j, chip in enumerate(_chips_of(x, y)):
                blk = outs[a].at[_dev((*chip, c))]
                cps.append(pltpu.make_async_remote_copy(
                    src_ref=blk, dst_ref=blk, send_sem=send.at[a, j], recv_sem=recv.at[a, j],
                    device_id=(x, y, 1 - c), device_id_type=MESH))
        for cp in cps:
            cp.start()
        return cps

    def finish(cps, ins, outs, sems):
        send, recv = sems
        x, y, c = _place()
        for a in range(n):
            for j, chip in enumerate(_chips_of(x, y)):
                blk = outs[a].at[_dev((*chip, 1 - c))]
                pltpu.make_async_remote_copy(
                    src_ref=blk, dst_ref=blk, send_sem=send.at[a, j], recv_sem=recv.at[a, j],
                    device_id=(x, y, c), device_id_type=MESH).wait_recv()
        for cp in cps:
            cp.wait_send()

    return _Job(gathered, [jax.ShapeDtypeStruct(g.shape, g.dtype) for g in gathered], {a: a for a in range(n)},
                [pltpu.SemaphoreType.DMA((n, 3)), pltpu.SemaphoreType.DMA((n, 3))], start, finish)


def _sibling_job(grads):
    n = len(grads)

    def start(ins, outs, sems):
        send, recv = sems
        x, y, c = _place()
        cps = [pltpu.make_async_remote_copy(
            src_ref=ins[a].at[:, 1 - c], dst_ref=outs[a], send_sem=send.at[a], recv_sem=recv.at[a],
            device_id=(x, y, 1 - c), device_id_type=MESH) for a in range(n)]
        for cp in cps:
            cp.start()
        return cps

    def finish(cps, ins, outs, sems):
        for cp in cps:
            cp.wait()

    return _Job(grads, [jax.ShapeDtypeStruct(g.shape[:1] + g.shape[2:], F32) for g in grads], {},
                [pltpu.SemaphoreType.DMA((n,)), pltpu.SemaphoreType.DMA((n,))], start, finish)


def _chip_job(psums):
    n = len(psums)

    def start(ins, outs, sems):
        send, recv, loc = sems
        x, y, c = _place()
        mychip = 2 * x + y
        cps = []
        for a in range(n):
            cps.append(pltpu.make_async_copy(ins[a].at[mychip], outs[a].at[mychip], loc.at[a]))
            for j, chip in enumerate(_chips_of(x, y)):
                cps.append(pltpu.make_async_remote_copy(
                    src_ref=ins[a].at[2 * chip[0] + chip[1]], dst_ref=outs[a].at[mychip],
                    send_sem=send.at[a, j], recv_sem=recv.at[a, j], device_id=(*chip, c), device_id_type=MESH))
        for cp in cps:
            cp.start()
        return cps

    def finish(cps, ins, outs, sems):
        send, recv, loc = sems
        x, y, c = _place()
        mychip = 2 * x + y
        for a in range(n):
            for j, chip in enumerate(_chips_of(x, y)):
                pltpu.make_async_remote_copy(
                    src_ref=ins[a].at[mychip], dst_ref=outs[a].at[2 * chip[0] + chip[1]],
                    send_sem=send.at[a, j], recv_sem=recv.at[a, j], device_id=(x, y, c), device_id_type=MESH).wait_recv()
        for a in range(n):
            cps[4 * a].wait()
            for j in range(3):
                cps[4 * a + 1 + j].wait_send()

    return _Job(psums, [jax.ShapeDtypeStruct(p.shape, BF16) for p in psums], {},
                [pltpu.SemaphoreType.DMA((n, 3)), pltpu.SemaphoreType.DMA((n, 3)), pltpu.SemaphoreType.DMA((n,))],
                start, finish)


def _join_jobs(*jobs):
    jobs = [j for j in jobs if j is not None]
    if len(jobs) <= 1:
        return jobs[0] if jobs else None
    cut = lambda seq, sizes: [seq[sum(sizes[:k]):sum(sizes[:k + 1])] for k in range(len(sizes))]
    n_in = [len(j.ins) for j in jobs]
    n_out = [len(j.out_shapes) for j in jobs]
    n_sem = [len(j.sems) for j in jobs]
    aliases = {}
    for k, j in enumerate(jobs):
        for a, b in j.aliases.items():
            aliases[sum(n_in[:k]) + a] = sum(n_out[:k]) + b

    def start(ins, outs, sems):
        return [j.start(i, o, s) for j, i, o, s in zip(jobs, cut(ins, n_in), cut(outs, n_out), cut(sems, n_sem))]

    def finish(sts, ins, outs, sems):
        for j, st, i, o, s in zip(jobs, sts, cut(ins, n_in), cut(outs, n_out), cut(sems, n_sem)):
            j.finish(st, i, o, s)

    return _Job([t for j in jobs for t in j.ins], [t for j in jobs for t in j.out_shapes], aliases,
                [t for j in jobs for t in j.sems], start, finish)


def _run_job(job, name):
    def body(ins, outs, scr, comm):
        comm[1](comm[0]())

    return _host_call(body, name, [], [], [], [], [], {}, job)[1]


def _allreduce_small(v):
    def body(v_ref, o_ref, slots, send_sems, recv_sems):
        x, y, c = _place()
        me = 4 * x + 2 * y + c
        slots[me] = v_ref[...]

        def copy(k):
            peer = (x ^ ((k >> 2) & 1), y ^ ((k >> 1) & 1), c ^ (k & 1))
            return pltpu.make_async_remote_copy(
                src_ref=v_ref, dst_ref=slots.at[me], send_sem=send_sems.at[k - 1], recv_sem=recv_sems.at[k - 1],
                device_id=peer, device_id_type=MESH)

        def arrival(k):
            return pltpu.make_async_remote_copy(
                src_ref=v_ref, dst_ref=slots.at[me ^ k], send_sem=send_sems.at[k - 1], recv_sem=recv_sems.at[k - 1],
                device_id=(x, y, c), device_id_type=MESH)

        sends = [copy(k) for k in range(1, NDEV)]
        for cp in sends:
            cp.start()
        for k in range(1, NDEV):
            arrival(k).wait_recv()
        for cp in sends:
            cp.wait_send()
        acc = slots[0]
        for d in range(1, NDEV):
            acc = acc + slots[d]
        o_ref[...] = acc

    return pl.pallas_call(
        body, name="allreduce_small", in_specs=[VMEM_SPEC], out_specs=VMEM_SPEC,
        out_shape=jax.ShapeDtypeStruct(v.shape, F32),
        scratch_shapes=[pltpu.VMEM((NDEV,) + v.shape, F32), pltpu.SemaphoreType.DMA((NDEV - 1,)),
                        pltpu.SemaphoreType.DMA((NDEV - 1,))],
        )(v)


def _pair_sums(views, gots, core):
    n = len(views)

    def body(c_ref, *refs):
        for a in range(n):
            refs[2 * n + a][...] = (refs[a][...] + refs[n + a][...]).astype(BF16)

    def vspec(v):
        return pl.BlockSpec((None, None) + v.shape[2:], lambda k, c: (k, c[0], 0, 0))

    def gspec(g):
        return pl.BlockSpec((None,) + g.shape[1:], lambda k, c: (k, 0, 0))

    return pl.pallas_call(
        body, name="pair_sums",
        grid_spec=pltpu.PrefetchScalarGridSpec(
            num_scalar_prefetch=1, grid=(4,),
            in_specs=[vspec(v) for v in views] + [gspec(g) for g in gots],
            out_specs=[gspec(g) for g in gots]),
        out_shape=[jax.ShapeDtypeStruct(g.shape, BF16) for g in gots],
        compiler_params=_cp(("parallel",)))(core, *views, *gots)


def _chip_sums(parts):
    n = len(parts)

    def body(*refs):
        for a in range(n):
            acc = refs[a][0].astype(F32)
            for k in range(1, 4):
                acc = acc + refs[a][k].astype(F32)
            refs[n + a][...] = acc

    return pl.pallas_call(
        body, name="chip_sums", in_specs=[VMEM_SPEC] * n, out_specs=[VMEM_SPEC] * n,
        out_shape=[jax.ShapeDtypeStruct(p.shape[1:], F32) for p in parts], compiler_params=_cp())(*parts)


def _permute_in(w):
    lead = w.shape[:-1]
    return w.reshape(lead + (3, 3, 2, BQ)).swapaxes(-2, -3).reshape(lead + (QKVW,))


def _unpermute_in(w):
    lead = w.shape[:-1]
    return w.reshape(lead + (3, 2, 3, BQ)).swapaxes(-2, -3).reshape(lead + (QKVW,))


def _row(v):
    v = v.reshape(-1)
    return jnp.pad(v, (0, D - v.shape[0])).reshape(1, D)


def kernel(x, w_in, f_bias, conv_w, w_out, rel_bias, ln1_g, ln1_b, w_gate, w_up, w_down, ln2_g, ln2_b, loss_target, m_w_in, m_f_bias, m_conv_w, m_w_out, m_rel_bias, m_ln1_g, m_ln1_b, m_w_gate, m_w_up, m_w_down, m_ln2_g, m_ln2_b, v_w_in, v_f_bias, v_conv_w, v_w_out, v_rel_bias, v_ln1_g, v_ln1_b, v_w_gate, v_w_up, v_w_down, v_ln2_g, v_ln2_b):
    xi, yi, ci = _place()
    me = 4 * xi + 2 * yi + ci
    core = jnp.reshape(ci, (1,)).astype(jnp.int32)

    win_s = jnp.concatenate([_permute_in(w_in[..., :QKVW]), w_in[..., QKVW:]], axis=-1)
    win_s = jnp.pad(win_s, ((0, 0), (0, 0), (0, NPAD - NPROJ))).astype(BF16)
    per_layer = [win_s, w_out.astype(BF16), jnp.swapaxes(w_gate, 1, 2).astype(BF16),
                 jnp.swapaxes(w_up, 1, 2).astype(BF16), w_down.astype(BF16)]
    sh = [[s[l] for s in per_layer] for l in range(2)]

    def whole(g):
        return g.reshape(NDEV * g.shape[1], g.shape[2])

    first = _run_job(_gather_job_b(_run_job(_gather_job_a(sh[0][:1]), "gather_a")), "gather_b")
    W = [{"win": whole(first[0])}, {}]

    cw_rows = lax.dynamic_update_slice(jnp.zeros((2, 3, 256), F32), conv_w, (0, 0, me * 32))
    small = jnp.concatenate([_row(cw_rows[0]), _row(cw_rows[1]), jnp.zeros((SMALL_ROWS - 2, D), F32)], axis=0)
    small = _allreduce_small(small)
    cw_full = small[0:2, :CONVW].reshape(2, 3, 256)
    cw8 = jnp.pad(cw_full, ((0, 0), (0, 5), (0, 0)))
    fb = jnp.pad(f_bias, ((0, 0), (0, GATEW - NH))).reshape(2, 1, GATEW)
    tbl = _dil_table(rel_bias)

    def wcol(K, tn, off):
        return pl.BlockSpec((K, tn), lambda i, j: (0, off + j))

    def wrow(tn, K, blk=0):
        return pl.BlockSpec((tn, K), lambda i, j: (j, blk))

    def arow(tm, K, blk=0):
        return pl.BlockSpec((tm, K), lambda i, j: (i, blk))

    h = x.reshape(T, D)
    hb = h.astype(BF16)
    saved = []
    for l in range(2):
        Win = W[l]["win"]
        qkv = _mm([(hb, arow(1024, D), Win, wcol(D, 768, 0))], nt=False, M=T, N=QKVW, tm=1024, tn=768,
                  out_dtype=BF16, name="proj_qkv")
        conv = _mm([(hb, arow(512, D), Win, wcol(D, 768, 3))], nt=False, M=T, N=CONVW, tm=512, tn=768,
                   out_dtype=F32, name="proj_conv")
        gate = _mm([(hb, arow(512, D), Win, wcol(D, 128, 24))], nt=False, M=T, N=GATEW, tm=512, tn=128,
                   out_dtype=F32, name="proj_gate")
        cum = _fox_prep(gate, fb[l])
        cq = cum[:, :NH].reshape(BL, S, NH).transpose(0, 2, 1).reshape(NSTAT, S)
        ckb = jnp.broadcast_to(cq[:, :, None], (NSTAT, S, BQ))
        if l == 0:
            mixed, rtot, a0 = _sb_fwd(qkv, job=_gather_job_a(sh[0][1:]))
            mixed, lse_d, ex = _flash_fwd(qkv, mixed, 1, False, (tbl,),
                                          job=_join_jobs(_gather_job_b(list(a0)), _gather_job_a(sh[1][:2])))
            W[0].update(zip(("wout", "wgT", "wuT", "wd"), [whole(t) for t in ex[:4]]))
            mixed, lse_f, o_fox, ex = _flash_fwd(qkv, mixed, 2, True, (cq, ckb),
                                                 job=_join_jobs(_gather_job_b(list(ex[4:])), _gather_job_a(sh[1][2:])))
            W[1].update(zip(("win", "wout"), [whole(t) for t in ex[:2]]))
            a2 = list(ex[2:])
        else:
            mixed, rtot, ex = _sb_fwd(qkv, job=_gather_job_b(a2))
            W[1].update(zip(("wgT", "wuT", "wd"), [whole(t) for t in ex]))
            mixed, lse_d, _ = _flash_fwd(qkv, mixed, 1, False, (tbl,))
            mixed, lse_f, o_fox, _ = _flash_fwd(qkv, mixed, 2, True, (cq, ckb))
        Wout, WgT, WuT, Wd = W[l]["wout"], W[l]["wgT"], W[l]["wuT"], W[l]["wd"]
        mixed = _conv_fwd(conv, cw8[l], mixed)
        x1, xh1, r1, x1b = _mm_ln(mixed, Wout, h, ln1_g[l:l + 1], ln1_b[l:l + 1], "out_proj_ln")
        fs, ft, a = _ffn_up(x1b, WgT, WuT)
        x2, xh2, r2, x2b = _mm_ln(a, Wd, x1, ln2_g[l:l + 1], ln2_b[l:l + 1], "ffn_down_ln")
        saved.append(dict(h=hb, qkv=qkv, conv=conv, gate=gate, cq=cq, ckb=ckb, mixed=mixed, rtot=rtot, lse_d=lse_d,
                          lse_f=lse_f, o_fox=o_fox, x1=x1b, xh1=xh1, r1=r1, fs=fs, ft=ft, a=a, xh2=xh2, r2=r2))
        h, hb = x2, x2b

    sq, dy = _loss_grad(h, loss_target.reshape(T, D))
    loss = lax.psum(sq[0, 0], ("x", "y", "c")) * (0.5 / D)

    def view(gr):
        return gr.reshape(4, 2, gr.shape[0] // NDEV, gr.shape[1])

    G = [None, None]
    small_g = {}
    shard_g = {}
    for l in (1, 0):
        sv = saved[l]
        Win, Wout, WgT, WuT, Wd = W[l]["win"], W[l]["wout"], W[l]["wgT"], W[l]["wuT"], W[l]["wd"]
        ds2, dg2, db2, ds2b = _ln_bwd(dy, sv["xh2"], sv["r2"], ln2_g[l:l + 1])
        dgt, dut = _ffn_da(ds2b, Wd, sv["fs"], sv["ft"])
        G_d = _mm_tn(sv["a"], ds2b, None, C=D, Ka=DFF, N=D, tm=1408, tn=1024, tk=1024, ooff=0, name="grad_w_down")
        G_g = _mm_tn(dgt, sv["x1"], None, C=D, Ka=DFF, N=D, tm=1408, tn=1024, tk=1024, ooff=0, name="grad_w_gate")
        G_u = _mm_tn(dut, sv["x1"], None, C=D, Ka=DFF, N=D, tm=1408, tn=1024, tk=1024, ooff=0, name="grad_w_up")
        dx1 = _mm([(dgt, arow(1024, DFF), WgT, wcol(DFF, 512, 0)), (dut, arow(1024, DFF), WuT, wcol(DFF, 512, 0))],
                  nt=False, M=T, N=D, tm=1024, tn=512, out_dtype=F32, name="ffn_dx", res=ds2, res_scale=ALPHA)
        ds1, dg1, db1, ds1b = _ln_bwd(dx1, sv["xh1"], sv["r1"], ln1_g[l:l + 1])
        G_out = _mm_tn(sv["mixed"], ds1b, None, C=D, Ka=D, N=D, tm=1024, tn=1024, tk=1024, ooff=0, name="grad_w_out")
        dmixed = _mm([(ds1b, arow(512, D), Wout, wrow(512, D))], nt=True, M=T, N=D, tm=512, tn=512,
                     out_dtype=BF16, name="out_proj_dx")
        if l == 0:
            early = [view(t) for t in (G[1]["in"], G[1]["out"], G[1]["g"], G[1]["u"], G[1]["d"], G_g, G_u, G_d, G_out)]
            dqkv, gots = _sb_bwd(sv["qkv"], dmixed, sv["rtot"], job=_sibling_job(early))
            ps1 = _pair_sums(early[:5], list(gots[:5]), core)
            ps0 = _pair_sums(early[5:], list(gots[5:]), core)
            dqkv, dtbl, parts0 = _flash_bwd(sv["qkv"], sv["mixed"], dmixed, sv["lse_d"], dqkv, 1, False, (tbl,),
                                            job=_chip_job(ps0))
            dqkv, dck, parts1 = _flash_bwd(sv["qkv"], sv["o_fox"], dmixed, sv["lse_f"], dqkv, 2, True,
                                           (sv["cq"], sv["ckb"]), job=_chip_job(ps1))
            shard_g[1] = dict(zip(("in", "out", "g", "u", "d"), _chip_sums(list(parts1))))
            shard_g[0] = dict(zip(("g", "u", "d", "out"), _chip_sums(list(parts0))))
        else:
            dqkv, _ = _sb_bwd(sv["qkv"], dmixed, sv["rtot"])
            dqkv, dtbl, _ = _flash_bwd(sv["qkv"], sv["mixed"], dmixed, sv["lse_d"], dqkv, 1, False, (tbl,))
            dqkv, dck, _ = _flash_bwd(sv["qkv"], sv["o_fox"], dmixed, sv["lse_f"], dqkv, 2, True, (sv["cq"], sv["ckb"]))
        dconv, dcw = _conv_bwd(sv["conv"], cw8[l], dmixed)
        dcum = jnp.pad(dck.reshape(S, BL, NH).transpose(1, 0, 2).reshape(T, NH), ((0, 0), (0, GATEW - NH)))
        dgate, dfb = _fox_post(dcum, sv["gate"], fb[l])
        drb = _dil_table_bwd(dtbl)
        G_in = _mm_tn(sv["h"], dqkv, None, C=NPAD, Ka=D, N=QKVW, tm=1024, tn=768, tk=1024, ooff=0, name="grad_w_in_qkv")
        G_in = _mm_tn(sv["h"], dconv, G_in, C=NPAD, Ka=D, N=CONVW, tm=1024, tn=768, tk=1024, ooff=3,
                      name="grad_w_in_conv")
        G_in = _mm_tn(sv["h"], dgate, G_in, C=NPAD, Ka=D, N=GATEW, tm=1024, tn=128, tk=1024, ooff=24,
                      name="grad_w_in_gate")
        G[l] = {"in": G_in, "out": G_out, "g": G_g, "u": G_u, "d": G_d}
        dy = _mm([(dqkv, arow(1024, QKVW), Win, wrow(512, QKVW, 0)),
                  (dconv, arow(1024, CONVW), Win, wrow(512, CONVW, 3)),
                  (dgate, arow(1024, GATEW), Win, wrow(512, GATEW, 24))],
                 nt=True, M=T, N=D, tm=1024, tn=512, out_dtype=F32, name="proj_dx", res=ds1, res_scale=ALPHA)
        small_g[l] = dict(ln1_g=dg1, ln1_b=db1, ln2_g=dg2, ln2_b=db2, cw=dcw[0:3].reshape(1, CONVW),
                          fb=dfb[:, :NH], rb=drb[:, :NH])
    grad_x = dy.reshape(BL, S, D)

    late = [view(G[0]["in"])]
    gots = list(_run_job(_sibling_job(late), "sibling_exchange"))
    shard_g[0]["in"] = _chip_sums(_run_job(_chip_job(_pair_sums(late, gots, core)), "chip_exchange"))[0]

    rows = []
    for name in ("ln1_g", "ln1_b", "ln2_g", "ln2_b"):
        rows += [small_g[0][name], small_g[1][name]]
    rows += [_row(small_g[0]["cw"]), _row(small_g[1]["cw"]),
             _row(jnp.concatenate([small_g[0]["fb"], small_g[1]["fb"]], axis=0)),
             _row(small_g[0]["rb"] + small_g[1]["rb"])]
    rows.append(jnp.zeros((SMALL_ROWS - len(rows), D), F32))
    sg = _allreduce_small(jnp.concatenate(rows, axis=0))
    g_ln1_g, g_ln1_b, g_ln2_g, g_ln2_b = sg[0:2], sg[2:4], sg[4:6], sg[6:8]
    g_conv_full = sg[8:10, :CONVW].reshape(2, 3, 256)
    g_conv = lax.dynamic_slice(g_conv_full, (0, 0, me * 32), (2, 3, 32))
    g_fb = sg[10, :2 * NH].reshape(2, NH)
    g_rb = sg[11, :32 * NH].reshape(32, NH)

    def both(name):
        return jnp.stack([shard_g[0][name], shard_g[1][name]])

    g_in = both("in")
    g_w_in = jnp.concatenate([_unpermute_in(g_in[..., :QKVW]), g_in[..., QKVW:NPROJ]], axis=-1)
    g_w_out = both("out")
    g_w_gate = jnp.swapaxes(both("g"), 1, 2)
    g_w_up = jnp.swapaxes(both("u"), 1, 2)
    g_w_down = both("d")

    def big(w, g, m, v, tr):
        shp = w.shape
        f = lambda t: t.reshape(-1, shp[-1])
        return [t.reshape(shp) for t in _adamw(f(w), f(g), f(m), f(v), tr)]

    up_in = big(w_in, g_w_in, m_w_in, v_w_in, 64)
    up_out = big(w_out, g_w_out, m_w_out, v_w_out, 128)
    up_gate = big(w_gate, g_w_gate, m_w_gate, v_w_gate, 256)
    up_up = big(w_up, g_w_up, m_w_up, v_w_up, 256)
    up_down = big(w_down, g_w_down, m_w_down, v_w_down, 352)

    def pack(fbv, cwv, rbv, l1g, l1b, l2g, l2b):
        r = [l1g, l1b, l2g, l2b, _row(cwv), _row(fbv), _row(rbv)]
        r.append(jnp.zeros((SMALL_ROWS - 11, D), F32))
        return jnp.concatenate(r, axis=0)

    pw = pack(f_bias, conv_w, rel_bias, ln1_g, ln1_b, ln2_g, ln2_b)
    pg = pack(g_fb, g_conv, g_rb, g_ln1_g, g_ln1_b, g_ln2_g, g_ln2_b)
    pm = pack(m_f_bias, m_conv_w, m_rel_bias, m_ln1_g, m_ln1_b, m_ln2_g, m_ln2_b)
    pv = pack(v_f_bias, v_conv_w, v_rel_bias, v_ln1_g, v_ln1_b, v_ln2_g, v_ln2_b)
    ups = _adamw(pw, pg, pm, pv, SMALL_ROWS)

    def unpack(p):
        return dict(ln1_g=p[0:2], ln1_b=p[2:4], ln2_g=p[4:6], ln2_b=p[6:8],
                    conv_w=p[8, :192].reshape(2, 3, 32), f_bias=p[9, :2 * NH].reshape(2, NH),
                    rel_bias=p[10, :32 * NH].reshape(32, NH))

    sm = [unpack(p) for p in ups]

    def group(k):
        return (up_in[k], sm[k]["f_bias"], sm[k]["conv_w"], up_out[k], sm[k]["rel_bias"], sm[k]["ln1_g"],
                sm[k]["ln1_b"], up_gate[k], up_up[k], up_down[k], sm[k]["ln2_g"], sm[k]["ln2_b"])

    grads = (g_w_in, g_fb, g_conv, g_w_out, g_rb, g_ln1_g, g_ln1_b, g_w_gate, g_w_up, g_w_down, g_ln2_g, g_ln2_b)
    return (loss, grad_x) + grads + group(0) + group(1) + group(2)
```

```python
import math

import numpy as np
import jax
import jax.numpy as jnp
from jax import lax
from jax.experimental import pallas as pl
from jax.experimental.pallas import tpu as pltpu

F32 = jnp.float32
BF16 = jnp.bfloat16
MESH = pl.DeviceIdType.MESH

D = 1024
S = 2048
BL = 2
T = BL * S
NH = 4
DFF = 2816
NPROJ = 3076
NPAD = 3200
QKVW = 2304
CONVW = 768
GATEW = 128
PAIRW = 384
BQ = 128
HB = 2 * BQ
NB = S // BQ
NDEV = 8
NSTAT = BL * NH
ALPHA = 4.0 ** 0.25
SCALE = 0.125
NEG = -1e30
LN_EPS = 1e-5
ADAM_LR, ADAM_B1, ADAM_B2, ADAM_EPS, ADAM_WD, ADAM_STEP = 0.001, 0.9, 0.999, 1e-08, 0.01, 10
VMEM_LIMIT = 56 * 1024 * 1024
SMALL_ROWS = 16


def _bucket_thresholds():
    d = np.arange(0, S)
    nf = np.maximum(d, 1).astype(np.float32)
    large = 16 + (np.log(nf / np.float32(16)) / np.float32(math.log(128)) * np.float32(16)).astype(np.int32)
    b = np.where(d < 16, d, np.minimum(large, 31))
    return [int(np.argmax(b >= k)) for k in range(32)]


BUCKET_TH = _bucket_thresholds()


def _cp(sem=None):
    return pltpu.CompilerParams(dimension_semantics=sem, vmem_limit_bytes=VMEM_LIMIT)


def _dot(a, b):
    return lax.dot_general(a, b, (((1,), (0,)), ((), ())), preferred_element_type=F32)


def _dot_nt(a, b):
    return lax.dot_general(a, b, (((1,), (1,)), ((), ())), preferred_element_type=F32)


def _dot_tn(a, b):
    return lax.dot_general(a, b, (((0,), (0,)), ((), ())), preferred_element_type=F32)


def _split2(x):
    hi = x.astype(BF16)
    mid = (x - hi.astype(F32)).astype(BF16)
    return jnp.concatenate([hi, mid], axis=1)


def _split3(x):
    hi = x.astype(BF16)
    r = x - hi.astype(F32)
    mid = r.astype(BF16)
    lo = (r - mid.astype(F32)).astype(BF16)
    return jnp.concatenate([hi, mid, lo], axis=1)


def _log_sigmoid(u):
    return jnp.minimum(u, 0.0) - jnp.log1p(jnp.exp(-jnp.abs(u)))


def _log_sigmoid_tile(u):
    return jnp.minimum(u, 0.0) - jnp.log(1.0 + jnp.exp(jnp.minimum(u, -u)))


def _iota(shape, dim):
    return lax.broadcasted_iota(jnp.int32, shape, dim)


ANY_SPEC = pl.BlockSpec(memory_space=pl.ANY)
VMEM_SPEC = pl.BlockSpec(memory_space=pltpu.VMEM)


def _mm(pairs, *, nt, M, N, tm, tn, out_dtype, name, res=None, res_scale=1.0):
    n = len(pairs)

    def body(*refs):
        acc = None
        for p in range(n):
            a = refs[2 * p][...].astype(BF16)
            b = refs[2 * p + 1][...]
            d = _dot_nt(a, b) if nt else _dot(a, b)
            acc = d if acc is None else acc + d
        if res is not None:
            acc = acc + res_scale * refs[2 * n][...]
        refs[-1][...] = acc.astype(out_dtype)

    ops, specs = [], []
    for a, asp, b, bsp in pairs:
        ops += [a, b]
        specs += [asp, bsp]
    if res is not None:
        ops.append(res)
        specs.append(pl.BlockSpec((tm, tn), lambda i, j: (i, j)))
    return pl.pallas_call(
        body, name=name, grid=(M // tm, N // tn), in_specs=specs,
        out_specs=pl.BlockSpec((tm, tn), lambda i, j: (i, j)),
        out_shape=jax.ShapeDtypeStruct((M, N), out_dtype),
        compiler_params=_cp(("parallel", "parallel")))(*ops)


def _mm_tn(a, b, gbuf, *, C, Ka, N, tm, tn, tk, ooff, name):
    def body(*refs):
        a_ref, b_ref, o_ref = refs[0], refs[1], refs[-1]
        k = pl.program_id(2)
        d = _dot_tn(a_ref[...].astype(BF16), b_ref[...].astype(BF16))

        @pl.when(k == 0)
        def _():
            o_ref[...] = d

        @pl.when(k > 0)
        def _():
            o_ref[...] += d

    ops = [a, b] + ([] if gbuf is None else [gbuf])
    return pl.pallas_call(
        body, name=name, grid=(Ka // tm, N // tn, T // tk),
        in_specs=[pl.BlockSpec((tk, tm), lambda i, j, k: (k, i)),
                  pl.BlockSpec((tk, tn), lambda i, j, k: (k, j))] + ([] if gbuf is None else [ANY_SPEC]),
        out_specs=pl.BlockSpec((tm, tn), lambda i, j, k: (i, ooff + j)),
        out_shape=jax.ShapeDtypeStruct((Ka, C), F32),
        input_output_aliases={} if gbuf is None else {2: 0},
        compiler_params=_cp(("parallel", "parallel", "arbitrary")))(*ops)


def _ffn_up(x1, wgt, wut):
    tm, tn = 1024, 256

    def body(x_ref, wg_ref, wu_ref, s_ref, t_ref, a_ref):
        xb = x_ref[...]
        g = _dot_nt(xb, wg_ref[...])
        u = _dot_nt(xb, wu_ref[...])
        sg = jax.nn.sigmoid(g)
        s = g * sg
        s_ref[...] = s.astype(BF16)
        t_ref[...] = (u * (sg * (1.0 + g * (1.0 - sg)))).astype(BF16)
        a_ref[...] = (s * u).astype(BF16)

    wspec = pl.BlockSpec((tn, D), lambda i, j: (j, 0))
    ospec = pl.BlockSpec((tm, tn), lambda i, j: (i, j))
    return pl.pallas_call(
        body, name="ffn_up", grid=(T // tm, DFF // tn),
        in_specs=[pl.BlockSpec((tm, D), lambda i, j: (i, 0)), wspec, wspec],
        out_specs=[ospec, ospec, ospec],
        out_shape=[jax.ShapeDtypeStruct((T, DFF), BF16)] * 3,
        compiler_params=_cp(("parallel", "parallel")))(x1, wgt, wut)


def _ffn_da(dffn, wd, s, t):
    tm, tn = 1024, 256

    def body(d_ref, wd_ref, s_ref, t_ref, dg_ref, du_ref):
        da = _dot_nt(d_ref[...], wd_ref[...])
        dg_ref[...] = (da * t_ref[...].astype(F32)).astype(BF16)
        du_ref[...] = (da * s_ref[...].astype(F32)).astype(BF16)

    ospec = pl.BlockSpec((tm, tn), lambda i, j: (i, j))
    return pl.pallas_call(
        body, name="ffn_da", grid=(T // tm, DFF // tn),
        in_specs=[pl.BlockSpec((tm, D), lambda i, j: (i, 0)),
                  pl.BlockSpec((tn, D), lambda i, j: (j, 0)), ospec, ospec],
        out_specs=[ospec, ospec],
        out_shape=[jax.ShapeDtypeStruct((T, DFF), BF16), jax.ShapeDtypeStruct((T, DFF), BF16)],
        compiler_params=_cp(("parallel", "parallel")))(dffn, wd, s, t)


def _mm_ln(a, w, x, gam, bet, name):
    tm = 256
    K = a.shape[1]

    def body(a_ref, w_ref, x_ref, g_ref, b_ref, y_ref, xh_ref, r_ref, yb_ref):
        s = ALPHA * x_ref[...] + _dot(a_ref[...], w_ref[...])
        mu = jnp.mean(s, axis=-1, keepdims=True)
        xc = s - mu
        var = jnp.mean(xc * xc, axis=-1, keepdims=True)
        r = lax.rsqrt(var + LN_EPS)
        xh = xc * r
        xh_ref[...] = xh
        r_ref[...] = r
        y = xh * g_ref[...] + b_ref[...]
        y_ref[...] = y
        yb_ref[...] = y.astype(BF16)

    row = pl.BlockSpec((tm, D), lambda i: (i, 0))
    vec = pl.BlockSpec((1, D), lambda i: (0, 0))
    return pl.pallas_call(
        body, name=name, grid=(T // tm,),
        in_specs=[pl.BlockSpec((tm, K), lambda i: (i, 0)), pl.BlockSpec((K, D), lambda i: (0, 0)), row, vec, vec],
        out_specs=[row, row, pl.BlockSpec((tm, 1), lambda i: (i, 0)), row],
        out_shape=[jax.ShapeDtypeStruct((T, D), F32), jax.ShapeDtypeStruct((T, D), F32),
                   jax.ShapeDtypeStruct((T, 1), F32), jax.ShapeDtypeStruct((T, D), BF16)],
        compiler_params=_cp(("parallel",)))(a, w, x, gam, bet)


def _ln_bwd(dy, xh, r, gam):
    tm = 256

    def body(dy_ref, xh_ref, r_ref, g_ref, ds_ref, dg_ref, db_ref, dsb_ref):
        i = pl.program_id(0)
        dyv = dy_ref[...]
        xhv = xh_ref[...]
        dxh = dyv * g_ref[...]
        m1 = jnp.mean(dxh, axis=-1, keepdims=True)
        m2 = jnp.mean(dxh * xhv, axis=-1, keepdims=True)
        ds = r_ref[...] * (dxh - m1 - xhv * m2)
        ds_ref[...] = ds
        dsb_ref[...] = ds.astype(BF16)
        pg = jnp.sum(dyv * xhv, axis=0, keepdims=True)
        pb = jnp.sum(dyv, axis=0, keepdims=True)

        @pl.when(i == 0)
        def _():
            dg_ref[...] = pg
            db_ref[...] = pb

        @pl.when(i > 0)
        def _():
            dg_ref[...] += pg
            db_ref[...] += pb

    row = pl.BlockSpec((tm, D), lambda i: (i, 0))
    vec = pl.BlockSpec((1, D), lambda i: (0, 0))
    return pl.pallas_call(
        body, name="ln_bwd", grid=(T // tm,),
        in_specs=[row, row, pl.BlockSpec((tm, 1), lambda i: (i, 0)), vec],
        out_specs=[row, vec, vec, row],
        out_shape=[jax.ShapeDtypeStruct((T, D), F32), jax.ShapeDtypeStruct((1, D), F32),
                   jax.ShapeDtypeStruct((1, D), F32), jax.ShapeDtypeStruct((T, D), BF16)],
        compiler_params=_cp(("arbitrary",)))(dy, xh, r, gam)


def _loss_grad(y, tgt):
    tm = 256

    def body(y_ref, t_ref, l_ref, dy_ref):
        i = pl.program_id(0)
        e = y_ref[...] - t_ref[...]
        dy_ref[...] = e * (1.0 / D)
        p = jnp.sum(jnp.sum(e * e, axis=1, keepdims=True), axis=0, keepdims=True)

        @pl.when(i == 0)
        def _():
            l_ref[...] = p

        @pl.when(i > 0)
        def _():
            l_ref[...] += p

    row = pl.BlockSpec((tm, D), lambda i: (i, 0))
    return pl.pallas_call(
        body, name="loss_grad", grid=(T // tm,), in_specs=[row, row],
        out_specs=[pl.BlockSpec((1, 1), lambda i: (0, 0)), row],
        out_shape=[jax.ShapeDtypeStruct((1, 1), F32), jax.ShapeDtypeStruct((T, D), F32)],
        compiler_params=_cp(("arbitrary",)))(y, tgt)


def _adamw(w, g, m, v, tr):
    L, R, C = w.shape

    def body(w_ref, g_ref, m_ref, v_ref, d_ref, m2_ref, v2_ref):
        gv = g_ref[...]
        m2 = ADAM_B1 * m_ref[...] + (1.0 - ADAM_B1) * gv
        v2 = ADAM_B2 * v_ref[...] + (1.0 - ADAM_B2) * (gv * gv)
        m_hat = m2 / (1.0 - ADAM_B1 ** ADAM_STEP)
        v_hat = v2 / (1.0 - ADAM_B2 ** ADAM_STEP)
        d_ref[...] = -ADAM_LR * (m_hat / (jnp.sqrt(v_hat) + ADAM_EPS) + ADAM_WD * w_ref[...])
        m2_ref[...] = m2
        v2_ref[...] = v2

    blk = pl.BlockSpec((None, tr, C), lambda l, i: (l, i, 0))
    sh = jax.ShapeDtypeStruct((L, R, C), F32)
    return pl.pallas_call(
        body, name="adamw", grid=(L, R // tr), in_specs=[blk] * 4, out_specs=[blk] * 3,
        out_shape=[sh, sh, sh], compiler_params=_cp(("parallel", "parallel")))(w, g, m, v)


class _Job:
    def __init__(self, ins, out_shapes, aliases, sems, start, finish):
        self.ins, self.out_shapes, self.aliases, self.sems = list(ins), list(out_shapes), dict(aliases), list(sems)
        self.start, self.finish = start, finish


def _host_call(body, name, ins, in_specs, out_shapes, out_specs, scratch, aliases, job):
    n_in, n_out, n_scr = len(ins), len(out_shapes), len(scratch)
    jins = job.ins if job else []
    jouts = job.out_shapes if job else []
    jsems = job.sems if job else []

    def wrapped(*refs):
        a = n_in
        b = a + len(jins)
        c = b + n_out
        d = c + len(jouts)
        e = d + n_scr
        comm = None
        if job:
            jrefs = (refs[a:b], refs[c:d], refs[e:])
            comm = (lambda: job.start(*jrefs), lambda st: job.finish(st, *jrefs))
        body(refs[:a], refs[b:c], refs[d:e], comm)

    al = dict(aliases)
    if job:
        for ji, jo in job.aliases.items():
            al[n_in + ji] = n_out + jo
    res = pl.pallas_call(
        wrapped, name=name, in_specs=list(in_specs) + [ANY_SPEC] * len(jins),
        out_specs=list(out_specs) + [ANY_SPEC] * len(jouts), out_shape=list(out_shapes) + list(jouts),
        scratch_shapes=list(scratch) + list(jsems), input_output_aliases=al,
        compiler_params=_cp())(*ins, *jins)
    return res[:n_out], res[n_out:]


def _copy_in(src, dst, sem):
    cp = pltpu.make_async_copy(src, dst, sem)
    cp.start()
    cp.wait()


CHAINS = [(p, b) for p in range(2) for b in range(BL)]
NC = len(CHAINS)
ROWS_SHAPE = jax.ShapeDtypeStruct((NSTAT, S), F32)
SLAB_QKV = pltpu.VMEM((T, 2 * PAIRW), BF16)
SLAB_OUT = pltpu.VMEM((T, 2 * BQ), BF16)
SLAB_O32 = pltpu.VMEM((T, 2 * BQ), F32)
SLAB_T = pltpu.VMEM((2, BQ, T), BF16)
SLAB_KEYB = pltpu.VMEM((NSTAT, S, BQ), F32)
ACC_KV = pltpu.VMEM((2, T, BQ), F32)


def _lane_masks():
    lane = _iota((1, BQ), 1)
    m0 = (lane < 64).astype(BF16)
    return m0, 1.0 - m0


def _row_masks():
    r = _iota((BQ, 1), 0)
    m0 = (r < 64).astype(BF16)
    return m0, 1.0 - m0


def _stack(x, m0, m1):
    return jnp.concatenate([x * m0, x * m1], axis=0)


def _stack_t(xt, r0, r1):
    return jnp.concatenate([xt * r0, xt * r1], axis=1)


def _tr(x):
    return x.T


def _rows(b, i):
    return pl.ds(pl.multiple_of(b * S + i * BQ, BQ), BQ)


def _transpose_slab(src, dst, col0):
    def blk(n, _):
        r = pl.ds(pl.multiple_of(n * BQ, BQ), BQ)
        for p in range(2):
            dst[p, :, r] = _tr(src[r, col0(p):col0(p) + BQ])
        return 0

    lax.fori_loop(0, T // BQ, blk, 0)


def _heads(x):
    return x[:BQ], x[BQ:]


def _bcast_heads(r0, r1):
    return jnp.concatenate([jnp.broadcast_to(r0, (BQ, BQ)), jnp.broadcast_to(r1, (BQ, BQ))], axis=0)


def _by_channel(r0, r1):
    return jnp.where(_iota((BQ, BQ), 0) < 64, r0, r1)


def _colsum2(x):
    return jnp.sum(x[:BQ], axis=0, keepdims=True), jnp.sum(x[BQ:], axis=0, keepdims=True)


def _stat_row(ref, p, b, h, i):
    c = b * NH + 2 * p + h
    return ref[c:c + 1, pl.ds(pl.multiple_of(i * BQ, BQ), BQ)]


def _put_row(ref, p, b, h, i, v):
    c = b * NH + 2 * p + h
    ref[c:c + 1, pl.ds(pl.multiple_of(i * BQ, BQ), BQ)] = v


def _valid_t(strict):
    r = _iota((HB, BQ), 0) & (BQ - 1)
    c = _iota((HB, BQ), 1)
    return (r < c) if strict else (r <= c)


def _tri_blockdiag(later):
    r = _iota((HB, HB), 0)
    c = _iota((HB, HB), 1)
    same = (r >= BQ) == (c >= BQ)
    return (same & ((c > r) if later else (c < r))).astype(BF16)


def _cum_mm(tri, x):
    y = _dot(tri, _split2(x))
    return y[:, :BQ] + y[:, BQ:]


def _kv_tiles(qkv_v, p, b, j):
    r = _rows(b, j)
    return qkv_v[r, p * PAIRW + BQ:p * PAIRW + 2 * BQ], qkv_v[r, p * PAIRW + 2 * BQ:p * PAIRW + 3 * BQ]


def _q_tile(qkv_v, p, b, i):
    return qkv_v[_rows(b, i), p * PAIRW:p * PAIRW + BQ] * SCALE


def _sb_fwd(qkv, job=None):
    def body(ins, outs, scr, comm):
        (qkv_hbm,), (o_hbm, r_ref), (qkv_v, o_v, sem, vt_v) = ins, outs, scr
        _copy_in(qkv_hbm.at[:, pl.ds(0, 2 * PAIRW)], qkv_v, sem)
        st = comm[0]() if comm else None
        _transpose_slab(qkv_v, vt_v, lambda p: p * PAIRW + 2 * BQ)
        m0, m1 = _lane_masks()
        r0, r1 = _row_masks()
        valid = _valid_t(True)
        later = _tri_blockdiag(True)

        def steps(qts, i, j, cs, diag):
            ks = [_stack(_kv_tiles(qkv_v, p, b, j)[0], m0, m1) for p, b in CHAINS]
            zs = [_dot(ks[c], qts[c]) for c in range(NC)]
            lbs, lrs = [], []
            for c in range(NC):
                lb = _log_sigmoid_tile(zs[c])
                lr = lb - zs[c]
                if diag:
                    lr = jnp.where(valid, lr, 0.0)
                lbs.append(lb)
                lrs.append(lr)
            tails = [_cum_mm(later, lrs[c]) for c in range(NC)]
            avs = []
            for c in range(NC):
                a = jnp.exp(lbs[c] + tails[c] + _bcast_heads(*cs[c][0]))
                if diag:
                    a = jnp.where(valid, a, 0.0)
                avs.append(a.astype(BF16))
            out = []
            for c, (p, b) in enumerate(CHAINS):
                vts = _stack_t(vt_v[p, :, _rows(b, j)], r0, r1)
                s0, s1 = _colsum2(lrs[c])
                out.append(((cs[c][0][0] + s0, cs[c][0][1] + s1), cs[c][1] + _dot(vts, avs[c])))
            return tuple(out)

        def qblock(i, _):
            qts = [_tr(_q_tile(qkv_v, p, b, i)) for p, b in CHAINS]
            zr = jnp.zeros((1, BQ), F32)
            cs = steps(qts, i, i, (((zr, zr), jnp.zeros((BQ, BQ), F32)),) * NC, True)
            cs = lax.fori_loop(1, i + 1, lambda jj, cs: steps(qts, i, i - jj, cs, False), cs)
            for c, (p, b) in enumerate(CHAINS):
                o_v[_rows(b, i), p * BQ:(p + 1) * BQ] = cs[c][1].T.astype(BF16)
                for h in range(2):
                    _put_row(r_ref, p, b, h, i, cs[c][0][h])
            return 0

        lax.fori_loop(0, NB, qblock, 0)
        _copy_in(o_v, o_hbm.at[:, pl.ds(0, 2 * BQ)], sem)
        if comm:
            comm[1](st)

    (mixed, rtot), extra = _host_call(
        body, "sb_fwd", [qkv], [ANY_SPEC], [jax.ShapeDtypeStruct((T, D), BF16), ROWS_SHAPE], [ANY_SPEC, VMEM_SPEC],
        [SLAB_QKV, SLAB_OUT, pltpu.SemaphoreType.DMA, SLAB_T], {}, job)
    return mixed, rtot, extra


def _sb_bwd(qkv, dmixed, rtot, job=None):
    def body(ins, outs, scr, comm):
        (qkv_hbm, do_hbm, r_ref), (dqkv_hbm,), (qkv_v, do_v, dq_v, dk_s, dv_s, sem, kt_v) = ins, outs, scr
        _copy_in(qkv_hbm.at[:, pl.ds(0, 2 * PAIRW)], qkv_v, sem)
        _copy_in(do_hbm.at[:, pl.ds(0, 2 * BQ)], do_v, sem)
        st = comm[0]() if comm else None
        _transpose_slab(qkv_v, kt_v, lambda p: p * PAIRW + BQ)
        m0, m1 = _lane_masks()
        f0, f1 = m0.astype(F32), m1.astype(F32)
        r0, r1 = _row_masks()
        valid = _valid_t(True)
        later = _tri_blockdiag(True)
        earlier = _tri_blockdiag(False)
        dk_s[...] = jnp.zeros_like(dk_s)
        dv_s[...] = jnp.zeros_like(dv_s)

        def steps(qns, qts, dns, dts, rts, i, j, cs, diag):
            kv = [_kv_tiles(qkv_v, p, b, j) for p, b in CHAINS]
            ks = [_stack(kv[c][0], m0, m1) for c in range(NC)]
            vs = [_stack(kv[c][1], m0, m1) for c in range(NC)]
            zs = [_dot(ks[c], qts[c]) for c in range(NC)]
            das = [_dot(vs[c], dts[c]) for c in range(NC)]
            lbs, lrs, pls = [], [], []
            for c in range(NC):
                lb = _log_sigmoid_tile(zs[c])
                lr = lb - zs[c]
                if diag:
                    lr = jnp.where(valid, lr, 0.0)
                s0, s1 = _colsum2(lr)
                lbs.append(lb)
                lrs.append(lr)
                pls.append((cs[c][0][0] + s0, cs[c][0][1] + s1))
            tails = [_cum_mm(later, lrs[c]) for c in range(NC)]
            avs, gms = [], []
            for c in range(NC):
                a = jnp.exp(lbs[c] + tails[c] + _bcast_heads(rts[c][0] - pls[c][0], rts[c][1] - pls[c][1]))
                if diag:
                    a = jnp.where(valid, a, 0.0)
                avs.append(a)
                gms.append(das[c] * a)
            befores = [_cum_mm(earlier, gms[c]) for c in range(NC)]
            dzbs = []
            for c in range(NC):
                beta = jnp.exp(lbs[c])
                dz = gms[c] * (1.0 - beta) - beta * (befores[c] + _bcast_heads(*cs[c][1]))
                if diag:
                    dz = jnp.where(valid, dz, 0.0)
                dzbs.append(dz.astype(BF16))
            out = []
            for c, (p, b) in enumerate(CHAINS):
                dq = cs[c][2] + _dot(_stack_t(kt_v[p, :, _rows(b, j)], r0, r1), dzbs[c])
                dk = _dot(dzbs[c], qns[c])
                dv = _dot(avs[c].astype(BF16), dns[c])
                dk_s[p, _rows(b, j), :] += dk[:BQ] * f0 + dk[BQ:] * f1
                dv_s[p, _rows(b, j), :] += dv[:BQ] * f0 + dv[BQ:] * f1
                g0, g1 = _colsum2(gms[c])
                out.append((pls[c], (cs[c][1][0] + g0, cs[c][1][1] + g1), dq))
            return tuple(out)

        def qblock(i, _):
            qns = [_q_tile(qkv_v, p, b, i) for p, b in CHAINS]
            dns = [do_v[_rows(b, i), p * BQ:(p + 1) * BQ] for p, b in CHAINS]
            qts = [_tr(t) for t in qns]
            dts = [_tr(t) for t in dns]
            rts = [(_stat_row(r_ref, p, b, 0, i), _stat_row(r_ref, p, b, 1, i)) for p, b in CHAINS]
            zr = jnp.zeros((1, BQ), F32)
            cs = (((zr, zr), (zr, zr), jnp.zeros((BQ, BQ), F32)),) * NC
            cs = lax.fori_loop(0, i, lambda j, cs: steps(qns, qts, dns, dts, rts, i, j, cs, False), cs)
            cs = steps(qns, qts, dns, dts, rts, i, i, cs, True)
            for c, (p, b) in enumerate(CHAINS):
                dq_v[_rows(b, i), p * PAIRW:p * PAIRW + BQ] = (cs[c][2].T * SCALE).astype(BF16)
            return 0

        lax.fori_loop(0, NB, qblock, 0)
        for p in range(2):
            dq_v[:, p * PAIRW + BQ:p * PAIRW + 2 * BQ] = dk_s[p].astype(BF16)
            dq_v[:, p * PAIRW + 2 * BQ:p * PAIRW + 3 * BQ] = dv_s[p].astype(BF16)
        _copy_in(dq_v, dqkv_hbm.at[:, pl.ds(0, 2 * PAIRW)], sem)
        if comm:
            comm[1](st)

    (dqkv,), extra = _host_call(
        body, "sb_bwd", [qkv, dmixed, rtot], [ANY_SPEC, ANY_SPEC, VMEM_SPEC],
        [jax.ShapeDtypeStruct((T, QKVW), BF16)], [ANY_SPEC],
        [SLAB_QKV, SLAB_OUT, SLAB_QKV, ACC_KV, ACC_KV, pltpu.SemaphoreType.DMA, SLAB_T], {}, job)
    return dqkv, extra


def _flash_fwd(qkv, mixed, g, fox, bias, job=None):
    def body(ins, outs, scr, comm):
        if fox:
            qkv_hbm, cq_ref, ckb_hbm, _ = ins
            (o_hbm, lse_ref, o32_hbm), (qkv_v, o_v, sem, vt_v, o32_v, ckb_v) = outs, scr
        else:
            qkv_hbm, tbl_ref, _ = ins
            (o_hbm, lse_ref), (qkv_v, o_v, sem, vt_v) = outs, scr
        _copy_in(qkv_hbm.at[:, pl.ds(g * 2 * PAIRW, 2 * PAIRW)], qkv_v, sem)
        if fox:
            _copy_in(ckb_hbm, ckb_v, sem)
        st = comm[0]() if comm else None
        _transpose_slab(qkv_v, vt_v, lambda p: p * PAIRW + 2 * BQ)
        m0, m1 = _lane_masks()
        r0, r1 = _row_masks()
        valid = _valid_t(False)

        def steps(qts, cqs, i, j, cs, diag):
            ks = [_stack(_kv_tiles(qkv_v, p, b, j)[0], m0, m1) for p, b in CHAINS]
            zs = [_dot(ks[c], qts[c]) for c in range(NC)]
            prs, alphas, out = [], [], []
            for c, (p, b) in enumerate(CHAINS):
                (ma, mb), (la, lb_), _ = cs[c]
                if fox:
                    kk = pl.ds(pl.multiple_of(j * BQ, BQ), BQ)
                    col = b * NH + 2 * p
                    z = zs[c] + (cqs[c] - jnp.concatenate([ckb_v[col, kk, :], ckb_v[col + 1, kk, :]], axis=0))
                    if diag:
                        z = jnp.where(valid, z, NEG)
                else:
                    z = zs[c] + tbl_ref[p, i - j]
                za, zb = _heads(z)
                na = jnp.maximum(ma, jnp.max(za, axis=0, keepdims=True))
                nb = jnp.maximum(mb, jnp.max(zb, axis=0, keepdims=True))
                aa, ab = jnp.exp(ma - na), jnp.exp(mb - nb)
                pr = jnp.exp(z - _bcast_heads(na, nb))
                sa, sb = _colsum2(pr)
                prs.append(_split2(pr) if fox else pr.astype(BF16))
                alphas.append((aa, ab))
                out.append(((na, nb), (aa * la + sa, ab * lb_ + sb)))
            pvs = []
            for c, (p, b) in enumerate(CHAINS):
                vts = _stack_t(vt_v[p, :, _rows(b, j)], r0, r1)
                if fox:
                    pvs.append(_dot(vts, prs[c][:, :BQ]) + _dot(vts, prs[c][:, BQ:]))
                else:
                    pvs.append(_dot(vts, prs[c]))
            return tuple((out[c][0], out[c][1], _by_channel(*alphas[c]) * cs[c][2] + pvs[c]) for c in range(NC))

        def qblock(i, _):
            qts = [_tr(_q_tile(qkv_v, p, b, i)) for p, b in CHAINS]
            if fox:
                cqs = [_bcast_heads(_stat_row(cq_ref, p, b, 0, i), _stat_row(cq_ref, p, b, 1, i)) for p, b in CHAINS]
            else:
                cqs = [None] * NC
            ng = jnp.full((1, BQ), NEG, F32)
            zr = jnp.zeros((1, BQ), F32)
            cs = steps(qts, cqs, i, i, (((ng, ng), (zr, zr), jnp.zeros((BQ, BQ), F32)),) * NC, True)
            cs = lax.fori_loop(1, i + 1, lambda jj, cs: steps(qts, cqs, i, i - jj, cs, False), cs)
            for c, (p, b) in enumerate(CHAINS):
                (ma, mb), (la, lb_), acc = cs[c]
                o = (acc / _by_channel(la, lb_)).T
                o_v[_rows(b, i), p * BQ:(p + 1) * BQ] = o.astype(BF16)
                if fox:
                    o32_v[_rows(b, i), p * BQ:(p + 1) * BQ] = o
                _put_row(lse_ref, p, b, 0, i, ma + jnp.log(la))
                _put_row(lse_ref, p, b, 1, i, mb + jnp.log(lb_))
            return 0

        lax.fori_loop(0, NB, qblock, 0)
        _copy_in(o_v, o_hbm.at[:, pl.ds(g * 2 * BQ, 2 * BQ)], sem)
        if fox:
            _copy_in(o32_v, o32_hbm, sem)
        if comm:
            comm[1](st)

    bias_specs = [VMEM_SPEC, ANY_SPEC] if fox else [VMEM_SPEC]
    n_in = 2 + len(bias_specs)
    o32 = [jax.ShapeDtypeStruct((T, 2 * BQ), F32)] if fox else []
    res, extra = _host_call(
        body, "fox_fwd" if fox else "dil_fwd", [qkv, *bias, mixed], [ANY_SPEC] + bias_specs + [ANY_SPEC],
        [jax.ShapeDtypeStruct((T, D), BF16), ROWS_SHAPE] + o32, [ANY_SPEC, VMEM_SPEC] + [ANY_SPEC] * len(o32),
        [SLAB_QKV, SLAB_OUT, pltpu.SemaphoreType.DMA, SLAB_T] + ([SLAB_O32, SLAB_KEYB] if fox else []),
        {n_in - 1: 0}, job)
    return (*res, extra)


def _flash_bwd(qkv, o, dmixed, lse, dqkv, g, fox, bias, job=None):
    def body(ins, outs, scr, comm):
        if fox:
            qkv_hbm, o_hbm, do_hbm, lse_ref, cq_ref, ckb_hbm, _ = ins
            (dqkv_hbm, db_ref), (qkv_v, o_v, do_v, dq_v, dk_s, dv_s, sem, kt_v, ckb_v, dc_s) = outs, scr
        else:
            qkv_hbm, o_hbm, do_hbm, lse_ref, tbl_ref, _ = ins
            (dqkv_hbm, db_ref), (qkv_v, o_v, do_v, dq_v, dk_s, dv_s, sem, kt_v) = outs, scr
        _copy_in(qkv_hbm.at[:, pl.ds(g * 2 * PAIRW, 2 * PAIRW)], qkv_v, sem)
        _copy_in(do_hbm.at[:, pl.ds(g * 2 * BQ, 2 * BQ)], do_v, sem)
        if fox:
            _copy_in(o_hbm, o_v, sem)
            _copy_in(ckb_hbm, ckb_v, sem)
        else:
            _copy_in(o_hbm.at[:, pl.ds(g * 2 * BQ, 2 * BQ)], o_v, sem)
        st = comm[0]() if comm else None
        _transpose_slab(qkv_v, kt_v, lambda p: p * PAIRW + BQ)
        m0, m1 = _lane_masks()
        f0, f1 = m0.astype(F32), m1.astype(F32)
        r0, r1 = _row_masks()
        valid = _valid_t(False)
        dk_s[...] = jnp.zeros_like(dk_s)
        dv_s[...] = jnp.zeros_like(dv_s)
        if fox:
            dc_s[...] = jnp.zeros_like(dc_s)
        else:
            db_ref[...] = jnp.zeros_like(db_ref)

        def steps(qns, qts, dns, dts, cqs, lses, deltas, i, j, dqs, diag):
            kv = [_kv_tiles(qkv_v, p, b, j) for p, b in CHAINS]
            ks = [_stack(kv[c][0], m0, m1) for c in range(NC)]
            vs = [_stack(kv[c][1], m0, m1) for c in range(NC)]
            zs = [_dot(ks[c], qts[c]) for c in range(NC)]
            dps = [_dot(vs[c], dts[c]) for c in range(NC)]
            prs, dzl = [], []
            for c, (p, b) in enumerate(CHAINS):
                if fox:
                    kk = pl.ds(pl.multiple_of(j * BQ, BQ), BQ)
                    col = b * NH + 2 * p
                    z = zs[c] + (cqs[c] - jnp.concatenate([ckb_v[col, kk, :], ckb_v[col + 1, kk, :]], axis=0))
                    if diag:
                        z = jnp.where(valid, z, NEG)
                else:
                    z = zs[c] + tbl_ref[p, i - j]
                pr = jnp.exp(z - lses[c])
                prs.append(pr.astype(BF16))
                dzl.append(pr * (dps[c] - deltas[c]))
            dzbs = [dz.astype(BF16) for dz in dzl]
            new = []
            for c, (p, b) in enumerate(CHAINS):
                new.append(dqs[c] + _dot(_stack_t(kt_v[p, :, _rows(b, j)], r0, r1), dzbs[c]))
                dk = _dot(dzbs[c], qns[c])
                dv = _dot(prs[c], dns[c])
                dk_s[p, _rows(b, j), :] += dk[:BQ] * f0 + dk[BQ:] * f1
                dv_s[p, _rows(b, j), :] += dv[:BQ] * f0 + dv[BQ:] * f1
                if fox:
                    dc_s[c, pl.ds(pl.multiple_of(j * HB, HB), HB), :] += dzl[c]
            if not fox:
                for p in range(2):
                    db_ref[p, i - j] = db_ref[p, i - j] + (dzl[2 * p] + dzl[2 * p + 1])
            return tuple(new)

        def qblock(i, _):
            qns = [_q_tile(qkv_v, p, b, i) for p, b in CHAINS]
            dns = [do_v[_rows(b, i), p * BQ:(p + 1) * BQ] for p, b in CHAINS]
            qts = [_tr(t) for t in qns]
            dts = [_tr(t) for t in dns]
            lses = [_bcast_heads(_stat_row(lse_ref, p, b, 0, i), _stat_row(lse_ref, p, b, 1, i)) for p, b in CHAINS]
            if fox:
                cqs = [_bcast_heads(_stat_row(cq_ref, p, b, 0, i), _stat_row(cq_ref, p, b, 1, i)) for p, b in CHAINS]
            else:
                cqs = [None] * NC
            deltas = []
            for c, (p, b) in enumerate(CHAINS):
                pt = (dns[c].astype(F32) * o_v[_rows(b, i), p * BQ:(p + 1) * BQ].astype(F32)).T
                deltas.append(_bcast_heads(jnp.sum(pt[:64], axis=0, keepdims=True), jnp.sum(pt[64:], axis=0, keepdims=True)))
            dqs = (jnp.zeros((BQ, BQ), F32),) * NC
            dqs = lax.fori_loop(0, i, lambda j, d: steps(qns, qts, dns, dts, cqs, lses, deltas, i, j, d, False), dqs)
            dqs = steps(qns, qts, dns, dts, cqs, lses, deltas, i, i, dqs, True)
            for c, (p, b) in enumerate(CHAINS):
                dq_v[_rows(b, i), p * PAIRW:p * PAIRW + BQ] = (dqs[c].T * SCALE).astype(BF16)
            return 0

        lax.fori_loop(0, NB, qblock, 0)
        for p in range(2):
            dq_v[:, p * PAIRW + BQ:p * PAIRW + 2 * BQ] = dk_s[p].astype(BF16)
            dq_v[:, p * PAIRW + 2 * BQ:p * PAIRW + 3 * BQ] = dv_s[p].astype(BF16)
        _copy_in(dq_v, dqkv_hbm.at[:, pl.ds(g * 2 * PAIRW, 2 * PAIRW)], sem)
        if fox:
            lane = _iota((BQ, NSTAT), 1)

            def fold(n, _):
                t = jnp.zeros((BQ, NSTAT), F32)
                for c, (p, b) in enumerate(CHAINS):
                    s = jnp.sum(dc_s[c, pl.ds(pl.multiple_of(n * HB, HB), HB), :], axis=1, keepdims=True)
                    col = b * NH + 2 * p
                    t = t - jnp.where(lane == col, s[:BQ], 0.0) - jnp.where(lane == col + 1, s[BQ:], 0.0)
                db_ref[pl.ds(pl.multiple_of(n * BQ, BQ), BQ), :] = t
                return 0

            lax.fori_loop(0, NB, fold, 0)
        if comm:
            comm[1](st)

    if fox:
        bias_specs = [VMEM_SPEC, ANY_SPEC]
        db_shape = jax.ShapeDtypeStruct((S, NSTAT), F32)
        more = [SLAB_KEYB, pltpu.VMEM((NC, NB * HB, BQ), F32)]
    else:
        bias_specs = [VMEM_SPEC]
        db_shape = jax.ShapeDtypeStruct((2, NB, HB, BQ), F32)
        more = []
    n_in = 5 + len(bias_specs)
    (dqkv, db), extra = _host_call(
        body, "fox_bwd" if fox else "dil_bwd", [qkv, o, dmixed, lse, *bias, dqkv],
        [ANY_SPEC, ANY_SPEC, ANY_SPEC, VMEM_SPEC] + bias_specs + [ANY_SPEC],
        [jax.ShapeDtypeStruct((T, QKVW), BF16), db_shape], [ANY_SPEC, VMEM_SPEC],
        [SLAB_QKV, SLAB_O32 if fox else SLAB_OUT, SLAB_OUT, SLAB_QKV, ACC_KV, ACC_KV, pltpu.SemaphoreType.DMA, SLAB_T]
        + more, {n_in - 1: 0}, job)
    return dqkv, db, extra


def _delta_t(d):
    return d * BQ + _iota((HB, BQ), 1) - (_iota((HB, BQ), 0) & (BQ - 1))


def _buckets_in(d):
    lo, hi = max(d * BQ - (BQ - 1), 0), d * BQ + BQ - 1
    return [b for b in range(32) if BUCKET_TH[b] <= hi and (b == 31 or BUCKET_TH[b + 1] > lo)]


def _in_bucket(delta, b):
    m = delta >= BUCKET_TH[b]
    return m if b == 31 else m & (delta < BUCKET_TH[b + 1])


def _dil_table(rel_bias):
    def body(rb_ref, o_ref):
        for d in range(NB):
            delta = _delta_t(d)
            pos = delta >= 0
            n = ((pos & (delta <= 128)).astype(jnp.int32)
                 + (pos & (delta <= 512) & ((delta & 3) == 0)).astype(jnp.int32)
                 + (pos & ((delta & 15) == 0)).astype(jnp.int32))
            logn = jnp.where(n == 3, math.log(3.0), jnp.where(n == 2, math.log(2.0), jnp.where(n == 1, 0.0, NEG)))
            head1 = _iota((HB, BQ), 0) >= BQ
            for p in range(2):
                val = jnp.zeros((HB, BQ), F32)
                for b in _buckets_in(d):
                    val = jnp.where(_in_bucket(delta, b), jnp.where(head1, rb_ref[b, 2 * p + 1], rb_ref[b, 2 * p]), val)
                o_ref[p, d] = val + logn

    return pl.pallas_call(
        body, name="dil_table", in_specs=[pl.BlockSpec(memory_space=pltpu.SMEM)], out_specs=VMEM_SPEC,
        out_shape=jax.ShapeDtypeStruct((2, NB, HB, BQ), F32), compiler_params=_cp())(rel_bias)


def _dil_table_bwd(dtbl):
    def body(dt_ref, o_ref):
        p = pl.program_id(0)
        rowi = _iota((32, BQ), 0)
        lanei = _iota((32, BQ), 1)

        @pl.when(p == 0)
        def _():
            o_ref[...] = jnp.zeros_like(o_ref)

        out = jnp.zeros((32, BQ), F32)
        for b in range(32):
            acc = None
            for d in range(NB):
                if b in _buckets_in(d):
                    t = jnp.where(_in_bucket(_delta_t(d), b), dt_ref[d], 0.0)
                    acc = t if acc is None else acc + t
            rs = jnp.sum(acc, axis=1, keepdims=True)
            s0 = jnp.sum(rs[:BQ], axis=0, keepdims=True)
            s1 = jnp.sum(rs[BQ:], axis=0, keepdims=True)
            out = (out + jnp.where((rowi == b) & (lanei == 2 * p), s0, 0.0)
                   + jnp.where((rowi == b) & (lanei == 2 * p + 1), s1, 0.0))
        o_ref[...] += out

    return pl.pallas_call(
        body, name="dil_table_bwd", grid=(2,),
        in_specs=[pl.BlockSpec((None, NB, HB, BQ), lambda p: (p, 0, 0, 0))],
        out_specs=pl.BlockSpec((32, BQ), lambda p: (0, 0)),
        out_shape=jax.ShapeDtypeStruct((32, BQ), F32),
        compiler_params=_cp(("arbitrary",)))(dtbl)


def _fox_prep(gate, fb):
    def body(g_ref, fb_ref, c_ref):
        tri = (_iota((BQ, BQ), 0) >= _iota((BQ, BQ), 1)).astype(BF16)

        def blk(i, carry):
            r0 = pl.multiple_of(i * BQ, BQ)
            lf = _log_sigmoid(g_ref[pl.ds(r0, BQ), :] + fb_ref[...])
            c = _dot(tri, _split3(lf))
            c_ref[pl.ds(r0, BQ), :] = c[:, 0:BQ] + c[:, BQ:2 * BQ] + c[:, 2 * BQ:3 * BQ] + carry
            return carry + jnp.sum(lf, axis=0, keepdims=True)

        lax.fori_loop(0, NB, blk, jnp.zeros((1, BQ), F32))

    blk = pl.BlockSpec((S, GATEW), lambda b: (b, 0))
    return pl.pallas_call(
        body, name="fox_prep", grid=(BL,), in_specs=[blk, pl.BlockSpec((1, GATEW), lambda b: (0, 0))],
        out_specs=blk, out_shape=jax.ShapeDtypeStruct((T, GATEW), F32),
        compiler_params=_cp(("parallel",)))(gate, fb)


def _fox_post(dcum, gate, fb):
    def body(dc_ref, g_ref, fb_ref, dg_ref, dfb_ref):
        b = pl.program_id(0)
        tri = (_iota((BQ, BQ), 0) <= _iota((BQ, BQ), 1)).astype(BF16)

        def blk(ii, carry):
            csum, dfb = carry
            r0 = pl.multiple_of((NB - 1 - ii) * BQ, BQ)
            dc = dc_ref[pl.ds(r0, BQ), :]
            c = _dot(tri, _split3(dc))
            dlf = c[:, 0:BQ] + c[:, BQ:2 * BQ] + c[:, 2 * BQ:3 * BQ] + csum
            dg = dlf * jnp.exp(_log_sigmoid(-(g_ref[pl.ds(r0, BQ), :] + fb_ref[...])))
            dg_ref[pl.ds(r0, BQ), :] = dg
            return csum + jnp.sum(dc, axis=0, keepdims=True), dfb + jnp.sum(dg, axis=0, keepdims=True)

        z = jnp.zeros((1, BQ), F32)
        _, dfb = lax.fori_loop(0, NB, blk, (z, z))

        @pl.when(b == 0)
        def _():
            dfb_ref[...] = dfb

        @pl.when(b > 0)
        def _():
            dfb_ref[...] += dfb

    blk = pl.BlockSpec((S, GATEW), lambda b: (b, 0))
    vec = pl.BlockSpec((1, GATEW), lambda b: (0, 0))
    return pl.pallas_call(
        body, name="fox_post", grid=(BL,), in_specs=[blk, blk, vec], out_specs=[blk, vec],
        out_shape=[jax.ShapeDtypeStruct((T, GATEW), F32), jax.ShapeDtypeStruct((1, GATEW), F32)],
        compiler_params=_cp(("arbitrary",)))(dcum, gate, fb)


def _shift_down(x, n):
    return jnp.where(_iota(x.shape, 0) >= n, pltpu.roll(x, n, 0), 0.0)


def _shift_up(x, n):
    return jnp.where(_iota(x.shape, 0) < S - n, pltpu.roll(x, S - n, 0), 0.0)


def _conv_fwd(conv, cw, mixed):
    W = 256

    def body(c_ref, w_ref, _, o_ref):
        u = c_ref[:, W:2 * W] * c_ref[:, 2 * W:3 * W]
        y = w_ref[0:1, :] * _shift_down(u, 2) + w_ref[1:2, :] * _shift_down(u, 1) + w_ref[2:3, :] * u
        o_ref[...] = (c_ref[:, 0:W] * y).astype(BF16)

    return pl.pallas_call(
        body, name="conv_fwd", grid=(BL,),
        in_specs=[pl.BlockSpec((S, CONVW), lambda b: (b, 0)), pl.BlockSpec((8, W), lambda b: (0, 0)), ANY_SPEC],
        out_specs=pl.BlockSpec((S, W), lambda b: (b, 3)),
        out_shape=jax.ShapeDtypeStruct((T, D), BF16), input_output_aliases={2: 0},
        compiler_params=_cp(("parallel",)))(conv, cw, mixed)


def _conv_bwd(conv, cw, dmixed):
    W = 256

    def body(c_ref, w_ref, do_ref, dc_ref, dw_ref):
        b = pl.program_id(0)
        bg = c_ref[:, 0:W]
        cg = c_ref[:, W:2 * W]
        hv = c_ref[:, 2 * W:3 * W]
        do = do_ref[...].astype(F32)
        u = cg * hv
        u1 = _shift_down(u, 1)
        u2 = _shift_down(u, 2)
        y = w_ref[0:1, :] * u2 + w_ref[1:2, :] * u1 + w_ref[2:3, :] * u
        dy = do * bg
        du = w_ref[2:3, :] * dy + w_ref[1:2, :] * _shift_up(dy, 1) + w_ref[0:1, :] * _shift_up(dy, 2)
        dc_ref[:, 0:W] = (do * y).astype(BF16)
        dc_ref[:, W:2 * W] = (du * hv).astype(BF16)
        dc_ref[:, 2 * W:3 * W] = (du * cg).astype(BF16)
        rowi = _iota((8, W), 0)
        dw = (jnp.where(rowi == 0, jnp.sum(dy * u2, axis=0, keepdims=True), 0.0)
              + jnp.where(rowi == 1, jnp.sum(dy * u1, axis=0, keepdims=True), 0.0)
              + jnp.where(rowi == 2, jnp.sum(dy * u, axis=0, keepdims=True), 0.0))

        @pl.when(b == 0)
        def _():
            dw_ref[...] = dw

        @pl.when(b > 0)
        def _():
            dw_ref[...] += dw

    return pl.pallas_call(
        body, name="conv_bwd", grid=(BL,),
        in_specs=[pl.BlockSpec((S, CONVW), lambda b: (b, 0)), pl.BlockSpec((8, W), lambda b: (0, 0)),
                  pl.BlockSpec((S, W), lambda b: (b, 3))],
        out_specs=[pl.BlockSpec((S, CONVW), lambda b: (b, 0)), pl.BlockSpec((8, W), lambda b: (0, 0))],
        out_shape=[jax.ShapeDtypeStruct((T, CONVW), BF16), jax.ShapeDtypeStruct((8, W), F32)],
        compiler_params=_cp(("arbitrary",)))(conv, cw, dmixed)


def _place():
    x, y, c = lax.axis_index("x"), lax.axis_index("y"), lax.axis_index("c")
    return x, y, c


def _chips_of(x, y):
    return [(1 - x, y), (x, 1 - y), (1 - x, 1 - y)]


def _dev(p):
    return 4 * p[0] + 2 * p[1] + p[2]


def _gather_job_a(shards):
    n = len(shards)

    def peers(x, y, c):
        return [(x, y, 1 - c)] + [(*chip, c) for chip in _chips_of(x, y)]

    def start(ins, outs, sems):
        send, recv, loc = sems
        x, y, c = _place()
        me = (x, y, c)
        cps = []
        for a in range(n):
            cps.append(pltpu.make_async_copy(ins[a], outs[a].at[_dev(me)], loc.at[a]))
            for k, peer in enumerate(peers(x, y, c)):
                cps.append(pltpu.make_async_remote_copy(
                    src_ref=ins[a], dst_ref=outs[a].at[_dev(me)], send_sem=send.at[a, k], recv_sem=recv.at[a, k],
                    device_id=peer, device_id_type=MESH))
        for cp in cps:
            cp.start()
        return cps

    def finish(cps, ins, outs, sems):
        send, recv, loc = sems
        x, y, c = _place()
        for a in range(n):
            for k, peer in enumerate(peers(x, y, c)):
                pltpu.make_async_remote_copy(
                    src_ref=ins[a], dst_ref=outs[a].at[_dev(peer)], send_sem=send.at[a, k], recv_sem=recv.at[a, k],
                    device_id=(x, y, c), device_id_type=MESH).wait_recv()
        for a in range(n):
            cps[5 * a].wait()
            for k in range(4):
                cps[5 * a + 1 + k].wait_send()

    return _Job(shards, [jax.ShapeDtypeStruct((NDEV,) + s.shape, s.dtype) for s in shards], {},
                [pltpu.SemaphoreType.DMA((n, 4)), pltpu.SemaphoreType.DMA((n, 4)), pltpu.SemaphoreType.DMA((n,))],
                start, finish)


def _gather_job_b(gathered):
    n = len(gathered)

    def start(ins, outs, sems):
        send, recv = sems
        x, y, c = _place()
        cps = []
        for a in range(n):
            for j, chip in enumerate(_chips_of(x, y)):
                blk = outs[a].at[_dev((*chip, c))]
                cps.append(pltpu.make_async_remote_copy(
                    src_ref=blk, dst_ref=blk, send_sem=send.at[a, j], recv_sem=recv.at[a, j],
                    device_id=(x, y, 1 - c), device_id_type=MESH))
        for cp in cps:
            cp.start()
        return cps

    def finish(cps, ins, outs, sems):
        send, recv = sems
        x, y, c = _place()
        for a in range(n):
            for j, chip in enumerate(_chips_of(x, y)):
                blk = outs[a].at[_dev((*chip, 1 - c))]
                pltpu.make_async_remote_copy(
                    src_ref=blk, dst_ref=blk, send_sem=send.at[a, j], recv_sem=recv.at[a, j],
                    device_id=(x, y, c), device_id_type=MESH).wait_recv()
        for cp in cps:
            cp.wait_send()

    return _Job(gathered, [jax.ShapeDtypeStruct(g.shape, g.dtype) for g in gathered], {a: a for a in range(n)},
                [pltpu.SemaphoreType.DMA((n, 3)), pltpu.SemaphoreType.DMA((n, 3))], start, finish)


def _sibling_job(grads):
    n = len(grads)

    def start(ins, outs, sems):
        send, recv = sems
        x, y, c = _place()
        cps = [pltpu.make_async_remote_copy(
            src_ref=ins[a].at[:, 1 - c], dst_ref=outs[a], send_sem=send.at[a], recv_sem=recv.at[a],
            device_id=(x, y, 1 - c), device_id_type=MESH) for a in range(n)]
        for cp in cps:
            cp.start()
        return cps

    def finish(cps, ins, outs, sems):
        for cp in cps:
            cp.wait()

    return _Job(grads, [jax.ShapeDtypeStruct(g.shape[:1] + g.shape[2:], F32) for g in grads], {},
                [pltpu.SemaphoreType.DMA((n,)), pltpu.SemaphoreType.DMA((n,))], start, finish)


def _chip_job(psums):
    n = len(psums)

    def start(ins, outs, sems):
        send, recv, loc = sems
        x, y, c = _place()
        mychip = 2 * x + y
        cps = []
        for a in range(n):
            cps.append(pltpu.make_async_copy(ins[a].at[mychip], outs[a].at[mychip], loc.at[a]))
            for j, chip in enumerate(_chips_of(x, y)):
                cps.append(pltpu.make_async_remote_copy(
                    src_ref=ins[a].at[2 * chip[0] + chip[1]], dst_ref=outs[a].at[mychip],
                    send_sem=send.at[a, j], recv_sem=recv.at[a, j], device_id=(*chip, c), device_id_type=MESH))
        for cp in cps:
            cp.start()
        return cps

    def finish(cps, ins, outs, sems):
        send, recv, loc = sems
        x, y, c = _place()
        mychip = 2 * x + y
        for a in range(n):
            for j, chip in enumerate(_chips_of(x, y)):
                pltpu.make_async_remote_copy(
                    src_ref=ins[a].at[mychip], dst_ref=outs[a].at[2 * chip[0] + chip[1]],
                    send_sem=send.at[a, j], recv_sem=recv.at[a, j], device_id=(x, y, c), device_id_type=MESH).wait_recv()
        for a in range(n):
            cps[4 * a].wait()
            for j in range(3):
                cps[4 * a + 1 + j].wait_send()

    return _Job(psums, [jax.ShapeDtypeStruct(p.shape, BF16) for p in psums], {},
                [pltpu.SemaphoreType.DMA((n, 3)), pltpu.SemaphoreType.DMA((n, 3)), pltpu.SemaphoreType.DMA((n,))],
                start, finish)


def _join_jobs(*jobs):
    jobs = [j for j in jobs if j is not None]
    if len(jobs) <= 1:
        return jobs[0] if jobs else None
    cut = lambda seq, sizes: [seq[sum(sizes[:k]):sum(sizes[:k + 1])] for k in range(len(sizes))]
    n_in = [len(j.ins) for j in jobs]
    n_out = [len(j.out_shapes) for j in jobs]
    n_sem = [len(j.sems) for j in jobs]
    aliases = {}
    for k, j in enumerate(jobs):
        for a, b in j.aliases.items():
            aliases[sum(n_in[:k]) + a] = sum(n_out[:k]) + b

    def start(ins, outs, sems):
        return [j.start(i, o, s) for j, i, o, s in zip(jobs, cut(ins, n_in), cut(outs, n_out), cut(sems, n_sem))]

    def finish(sts, ins, outs, sems):
        for j, st, i, o, s in zip(jobs, sts, cut(ins, n_in), cut(outs, n_out), cut(sems, n_sem)):
            j.finish(st, i, o, s)

    return _Job([t for j in jobs for t in j.ins], [t for j in jobs for t in j.out_shapes], aliases,
                [t for j in jobs for t in j.sems], start, finish)


def _run_job(job, name):
    def body(ins, outs, scr, comm):
        comm[1](comm[0]())

    return _host_call(body, name, [], [], [], [], [], {}, job)[1]


def _allreduce_small(v):
    def body(v_ref, o_ref, slots, send_sems, recv_sems):
        x, y, c = _place()
        me = 4 * x + 2 * y + c
        slots[me] = v_ref[...]

        def copy(k):
            peer = (x ^ ((k >> 2) & 1), y ^ ((k >> 1) & 1), c ^ (k & 1))
            return pltpu.make_async_remote_copy(
                src_ref=v_ref, dst_ref=slots.at[me], send_sem=send_sems.at[k - 1], recv_sem=recv_sems.at[k - 1],
                device_id=peer, device_id_type=MESH)

        def arrival(k):
            return pltpu.make_async_remote_copy(
                src_ref=v_ref, dst_ref=slots.at[me ^ k], send_sem=send_sems.at[k - 1], recv_sem=recv_sems.at[k - 1],
                device_id=(x, y, c), device_id_type=MESH)

        sends = [copy(k) for k in range(1, NDEV)]
        for cp in sends:
            cp.start()
        for k in range(1, NDEV):
            arrival(k).wait_recv()
        for cp in sends:
            cp.wait_send()
        acc = slots[0]
        for d in range(1, NDEV):
            acc = acc + slots[d]
        o_ref[...] = acc

    return pl.pallas_call(
        body, name="allreduce_small", in_specs=[VMEM_SPEC], out_specs=VMEM_SPEC,
        out_shape=jax.ShapeDtypeStruct(v.shape, F32),
        scratch_shapes=[pltpu.VMEM((NDEV,) + v.shape, F32), pltpu.SemaphoreType.DMA((NDEV - 1,)),
                        pltpu.SemaphoreType.DMA((NDEV - 1,))],
        )(v)


def _pair_sums(views, gots, core):
    n = len(views)

    def body(c_ref, *refs):
        for a in range(n):
            refs[2 * n + a][...] = (refs[a][...] + refs[n + a][...]).astype(BF16)

    def vspec(v):
        return pl.BlockSpec((None, None) + v.shape[2:], lambda k, c: (k, c[0], 0, 0))

    def gspec(g):
        return pl.BlockSpec((None,) + g.shape[1:], lambda k, c: (k, 0, 0))

    return pl.pallas_call(
        body, name="pair_sums",
        grid_spec=pltpu.PrefetchScalarGridSpec(
            num_scalar_prefetch=1, grid=(4,),
            in_specs=[vspec(v) for v in views] + [gspec(g) for g in gots],
            out_specs=[gspec(g) for g in gots]),
        out_shape=[jax.ShapeDtypeStruct(g.shape, BF16) for g in gots],
        compiler_params=_cp(("parallel",)))(core, *views, *gots)


def _chip_sums(parts):
    n = len(parts)

    def body(*refs):
        for a in range(n):
            acc = refs[a][0].astype(F32)
            for k in range(1, 4):
                acc = acc + refs[a][k].astype(F32)
            refs[n + a][...] = acc

    return pl.pallas_call(
        body, name="chip_sums", in_specs=[VMEM_SPEC] * n, out_specs=[VMEM_SPEC] * n,
        out_shape=[jax.ShapeDtypeStruct(p.shape[1:], F32) for p in parts], compiler_params=_cp())(*parts)


def _permute_in(w):
    lead = w.shape[:-1]
    return w.reshape(lead + (3, 3, 2, BQ)).swapaxes(-2, -3).reshape(lead + (QKVW,))


def _unpermute_in(w):
    lead = w.shape[:-1]
    return w.reshape(lead + (3, 2, 3, BQ)).swapaxes(-2, -3).reshape(lead + (QKVW,))


def _row(v):
    v = v.reshape(-1)
    return jnp.pad(v, (0, D - v.shape[0])).reshape(1, D)


def kernel(x, w_in, f_bias, conv_w, w_out, rel_bias, ln1_g, ln1_b, w_gate, w_up, w_down, ln2_g, ln2_b, loss_target, m_w_in, m_f_bias, m_conv_w, m_w_out, m_rel_bias, m_ln1_g, m_ln1_b, m_w_gate, m_w_up, m_w_down, m_ln2_g, m_ln2_b, v_w_in, v_f_bias, v_conv_w, v_w_out, v_rel_bias, v_ln1_g, v_ln1_b, v_w_gate, v_w_up, v_w_down, v_ln2_g, v_ln2_b):
    xi, yi, ci = _place()
    me = 4 * xi + 2 * yi + ci
    core = jnp.reshape(ci, (1,)).astype(jnp.int32)

    win_s = jnp.concatenate([_permute_in(w_in[..., :QKVW]), w_in[..., QKVW:]], axis=-1)
    win_s = jnp.pad(win_s, ((0, 0), (0, 0), (0, NPAD - NPROJ))).astype(BF16)
    per_layer = [win_s, w_out.astype(BF16), jnp.swapaxes(w_gate, 1, 2).astype(BF16),
                 jnp.swapaxes(w_up, 1, 2).astype(BF16), w_down.astype(BF16)]
    sh = [[s[l] for s in per_layer] for l in range(2)]

    def whole(g):
        return g.reshape(NDEV * g.shape[1], g.shape[2])

    first = _run_job(_gather_job_b(_run_job(_gather_job_a(sh[0][:1]), "gather_a")), "gather_b")
    W = [{"win": whole(first[0])}, {}]

    cw_rows = lax.dynamic_update_slice(jnp.zeros((2, 3, 256), F32), conv_w, (0, 0, me * 32))
    small = jnp.concatenate([_row(cw_rows[0]), _row(cw_rows[1]), jnp.zeros((SMALL_ROWS - 2, D), F32)], axis=0)
    small = _allreduce_small(small)
    cw_full = small[0:2, :CONVW].reshape(2, 3, 256)
    cw8 = jnp.pad(cw_full, ((0, 0), (0, 5), (0, 0)))
    fb = jnp.pad(f_bias, ((0, 0), (0, GATEW - NH))).reshape(2, 1, GATEW)
    tbl = _dil_table(rel_bias)

    def wcol(K, tn, off):
        return pl.BlockSpec((K, tn), lambda i, j: (0, off + j))

    def wrow(tn, K, blk=0):
        return pl.BlockSpec((tn, K), lambda i, j: (j, blk))

    def arow(tm, K, blk=0):
        return pl.BlockSpec((tm, K), lambda i, j: (i, blk))

    h = x.reshape(T, D)
    hb = h.astype(BF16)
    saved = []
    for l in range(2):
        Win = W[l]["win"]
        qkv = _mm([(hb, arow(1024, D), Win, wcol(D, 768, 0))], nt=False, M=T, N=QKVW, tm=1024, tn=768,
                  out_dtype=BF16, name="proj_qkv")
        conv = _mm([(hb, arow(512, D), Win, wcol(D, 768, 3))], nt=False, M=T, N=CONVW, tm=512, tn=768,
                   out_dtype=F32, name="proj_conv")
        gate = _mm([(hb, arow(512, D), Win, wcol(D, 128, 24))], nt=False, M=T, N=GATEW, tm=512, tn=128,
                   out_dtype=F32, name="proj_gate")
        cum = _fox_prep(gate, fb[l])
        cq = cum[:, :NH].reshape(BL, S, NH).transpose(0, 2, 1).reshape(NSTAT, S)
        ckb = jnp.broadcast_to(cq[:, :, None], (NSTAT, S, BQ))
        if l == 0:
            mixed, rtot, a0 = _sb_fwd(qkv, job=_gather_job_a(sh[0][1:]))
            mixed, lse_d, ex = _flash_fwd(qkv, mixed, 1, False, (tbl,),
                                          job=_join_jobs(_gather_job_b(list(a0)), _gather_job_a(sh[1][:2])))
            W[0].update(zip(("wout", "wgT", "wuT", "wd"), [whole(t) for t in ex[:4]]))
            mixed, lse_f, o_fox, ex = _flash_fwd(qkv, mixed, 2, True, (cq, ckb),
                                                 job=_join_jobs(_gather_job_b(list(ex[4:])), _gather_job_a(sh[1][2:])))
            W[1].update(zip(("win", "wout"), [whole(t) for t in ex[:2]]))
            a2 = list(ex[2:])
        else:
            mixed, rtot, ex = _sb_fwd(qkv, job=_gather_job_b(a2))
            W[1].update(zip(("wgT", "wuT", "wd"), [whole(t) for t in ex]))
            mixed, lse_d, _ = _flash_fwd(qkv, mixed, 1, False, (tbl,))
            mixed, lse_f, o_fox, _ = _flash_fwd(qkv, mixed, 2, True, (cq, ckb))
        Wout, WgT, WuT, Wd = W[l]["wout"], W[l]["wgT"], W[l]["wuT"], W[l]["wd"]
        mixed = _conv_fwd(conv, cw8[l], mixed)
        x1, xh1, r1, x1b = _mm_ln(mixed, Wout, h, ln1_g[l:l + 1], ln1_b[l:l + 1], "out_proj_ln")
        fs, ft, a = _ffn_up(x1b, WgT, WuT)
        x2, xh2, r2, x2b = _mm_ln(a, Wd, x1, ln2_g[l:l + 1], ln2_b[l:l + 1], "ffn_down_ln")
        saved.append(dict(h=hb, qkv=qkv, conv=conv, gate=gate, cq=cq, ckb=ckb, mixed=mixed, rtot=rtot, lse_d=lse_d,
                          lse_f=lse_f, o_fox=o_fox, x1=x1b, xh1=xh1, r1=r1, fs=fs, ft=ft, a=a, xh2=xh2, r2=r2))
        h, hb = x2, x2b

    sq, dy = _loss_grad(h, loss_target.reshape(T, D))
    loss = lax.psum(sq[0, 0], ("x", "y", "c")) * (0.5 / D)

    def view(gr):
        return gr.reshape(4, 2, gr.shape[0] // NDEV, gr.shape[1])

    G = [None, None]
    small_g = {}
    shard_g = {}
    for l in (1, 0):
        sv = saved[l]
        Win, Wout, WgT, WuT, Wd = W[l]["win"], W[l]["wout"], W[l]["wgT"], W[l]["wuT"], W[l]["wd"]
        ds2, dg2, db2, ds2b = _ln_bwd(dy, sv["xh2"], sv["r2"], ln2_g[l:l + 1])
        dgt, dut = _ffn_da(ds2b, Wd, sv["fs"], sv["ft"])
        G_d = _mm_tn(sv["a"], ds2b, None, C=D, Ka=DFF, N=D, tm=1408, tn=1024, tk=1024, ooff=0, name="grad_w_down")
        G_g = _mm_tn(dgt, sv["x1"], None, C=D, Ka=DFF, N=D, tm=1408, tn=1024, tk=1024, ooff=0, name="grad_w_gate")
        G_u = _mm_tn(dut, sv["x1"], None, C=D, Ka=DFF, N=D, tm=1408, tn=1024, tk=1024, ooff=0, name="grad_w_up")
        dx1 = _mm([(dgt, arow(1024, DFF), WgT, wcol(DFF, 512, 0)), (dut, arow(1024, DFF), WuT, wcol(DFF, 512, 0))],
                  nt=False, M=T, N=D, tm=1024, tn=512, out_dtype=F32, name="ffn_dx", res=ds2, res_scale=ALPHA)
        ds1, dg1, db1, ds1b = _ln_bwd(dx1, sv["xh1"], sv["r1"], ln1_g[l:l + 1])
        G_out = _mm_tn(sv["mixed"], ds1b, None, C=D, Ka=D, N=D, tm=1024, tn=1024, tk=1024, ooff=0, name="grad_w_out")
        dmixed = _mm([(ds1b, arow(512, D), Wout, wrow(512, D))], nt=True, M=T, N=D, tm=512, tn=512,
                     out_dtype=BF16, name="out_proj_dx")
        early = [view(t) for t in (G_g, G_u, G_d, G_out)] + ([view(G[1]["in"])] if l == 0 else [])
        dqkv, gots = _sb_bwd(sv["qkv"], dmixed, sv["rtot"], job=_sibling_job(early))
        ps = _pair_sums(early, list(gots), core)
        dqkv, dtbl, pa = _flash_bwd(sv["qkv"], sv["mixed"], dmixed, sv["lse_d"], dqkv, 1, False, (tbl,),
                                    job=_chip_job(ps[:2]))
        dqkv, dck, pb = _flash_bwd(sv["qkv"], sv["o_fox"], dmixed, sv["lse_f"], dqkv, 2, True,
                                   (sv["cq"], sv["ckb"]), job=_chip_job(ps[2:]))
        sums = _chip_sums(list(pa) + list(pb))
        shard_g[l] = dict(zip(("g", "u", "d", "out"), sums[:4]))
        if l == 0:
            shard_g[1]["in"] = sums[4]
        dconv, dcw = _conv_bwd(sv["conv"], cw8[l], dmixed)
        dcum = jnp.pad(dck.reshape(S, BL, NH).transpose(1, 0, 2).reshape(T, NH), ((0, 0), (0, GATEW - NH)))
        dgate, dfb = _fox_post(dcum, sv["gate"], fb[l])
        drb = _dil_table_bwd(dtbl)
        G_in = _mm_tn(sv["h"], dqkv, None, C=NPAD, Ka=D, N=QKVW, tm=1024, tn=768, tk=1024, ooff=0, name="grad_w_in_qkv")
        G_in = _mm_tn(sv["h"], dconv, G_in, C=NPAD, Ka=D, N=CONVW, tm=1024, tn=768, tk=1024, ooff=3,
                      name="grad_w_in_conv")
        G_in = _mm_tn(sv["h"], dgate, G_in, C=NPAD, Ka=D, N=GATEW, tm=1024, tn=128, tk=1024, ooff=24,
                      name="grad_w_in_gate")
        G[l] = {"in": G_in, "out": G_out, "g": G_g, "u": G_u, "d": G_d}
        dy = _mm([(dqkv, arow(1024, QKVW), Win, wrow(512, QKVW, 0)),
                  (dconv, arow(1024, CONVW), Win, wrow(512, CONVW, 3)),
                  (dgate, arow(1024, GATEW), Win, wrow(512, GATEW, 24))],
                 nt=True, M=T, N=D, tm=1024, tn=512, out_dtype=F32, name="proj_dx", res=ds1, res_scale=ALPHA)
        small_g[l] = dict(ln1_g=dg1, ln1_b=db1, ln2_g=dg2, ln2_b=db2, cw=dcw[0:3].reshape(1, CONVW),
                          fb=dfb[:, :NH], rb=drb[:, :NH])
    grad_x = dy.reshape(BL, S, D)

    late = [view(G[0]["in"])]
    gots = list(_run_job(_sibling_job(late), "sibling_exchange"))
    shard_g[0]["in"] = _chip_sums(_run_job(_chip_job(_pair_sums(late, gots, core)), "chip_exchange"))[0]

    rows = []
    for name in ("ln1_g", "ln1_b", "ln2_g", "ln2_b"):
        rows += [small_g[0][name], small_g[1][name]]
    rows += [_row(small_g[0]["cw"]), _row(small_g[1]["cw"]),
             _row(jnp.concatenate([small_g[0]["fb"], small_g[1]["fb"]], axis=0)),
             _row(small_g[0]["rb"] + small_g[1]["rb"])]
    rows.append(jnp.zeros((SMALL_ROWS - len(rows), D), F32))
    sg = _allreduce_small(jnp.concatenate(rows, axis=0))
    g_ln1_g, g_ln1_b, g_ln2_g, g_ln2_b = sg[0:2], sg[2:4], sg[4:6], sg[6:8]
    g_conv_full = sg[8:10, :CONVW].reshape(2, 3, 256)
    g_conv = lax.dynamic_slice(g_conv_full, (0, 0, me * 32), (2, 3, 32))
    g_fb = sg[10, :2 * NH].reshape(2, NH)
    g_rb = sg[11, :32 * NH].reshape(32, NH)

    def both(name):
        return jnp.stack([shard_g[0][name], shard_g[1][name]])

    g_in = both("in")
    g_w_in = jnp.concatenate([_unpermute_in(g_in[..., :QKVW]), g_in[..., QKVW:NPROJ]], axis=-1)
    g_w_out = both("out")
    g_w_gate = jnp.swapaxes(both("g"), 1, 2)
    g_w_up = jnp.swapaxes(both("u"), 1, 2)
    g_w_down = both("d")

    up_in = _adamw(w_in, g_w_in, m_w_in, v_w_in, 64)
    up_out = _adamw(w_out, g_w_out, m_w_out, v_w_out, 128)
    up_gate = _adamw(w_gate, g_w_gate, m_w_gate, v_w_gate, 256)
    up_up = _adamw(w_up, g_w_up, m_w_up, v_w_up, 256)
    up_down = _adamw(w_down, g_w_down, m_w_down, v_w_down, 352)

    def pack(fbv, cwv, rbv, l1g, l1b, l2g, l2b):
        r = [l1g, l1b, l2g, l2b, _row(cwv), _row(fbv), _row(rbv)]
        r.append(jnp.zeros((SMALL_ROWS - 11, D), F32))
        return jnp.concatenate(r, axis=0)

    pw = pack(f_bias, conv_w, rel_bias, ln1_g, ln1_b, ln2_g, ln2_b)
    pg = pack(g_fb, g_conv, g_rb, g_ln1_g, g_ln1_b, g_ln2_g, g_ln2_b)
    pm = pack(m_f_bias, m_conv_w, m_rel_bias, m_ln1_g, m_ln1_b, m_ln2_g, m_ln2_b)
    pv = pack(v_f_bias, v_conv_w, v_rel_bias, v_ln1_g, v_ln1_b, v_ln2_g, v_ln2_b)
    ups = [u[0] for u in _adamw(pw[None], pg[None], pm[None], pv[None], SMALL_ROWS)]

    def unpack(p):
        return dict(ln1_g=p[0:2], ln1_b=p[2:4], ln2_g=p[4:6], ln2_b=p[6:8],
                    conv_w=p[8, :192].reshape(2, 3, 32), f_bias=p[9, :2 * NH].reshape(2, NH),
                    rel_bias=p[10, :32 * NH].reshape(32, NH))

    sm = [unpack(p) for p in ups]

    def group(k):
        return (up_in[k], sm[k]["f_bias"], sm[k]["conv_w"], up_out[k], sm[k]["rel_bias"], sm[k]["ln1_g"],
                sm[k]["ln1_b"], up_gate[k], up_up[k], up_down[k], sm[k]["ln2_g"], sm[k]["ln2_b"])

    grads = (g_w_in, g_fb, g_conv, g_w_out, g_rb, g_ln1_g, g_ln1_b, g_w_gate, g_w_up, g_w_down, g_ln2_g, g_ln2_b)
    return (loss, grad_x) + grads + group(0) + group(1) + group(2)
```

```python
import math

import numpy as np
import jax
import jax.numpy as jnp
from jax import lax
from jax.experimental import pallas as pl
from jax.experimental.pallas import tpu as pltpu

F32 = jnp.float32
BF16 = jnp.bfloat16
MESH = pl.DeviceIdType.MESH

D = 1024
S = 2048
BL = 2
T = BL * S
NH = 4
DFF = 2816
NPROJ = 3076
NPAD = 3200
QKVW = 2304
CONVW = 768
GATEW = 128
PAIRW = 384
BQ = 128
HB = 2 * BQ
NB = S // BQ
NDEV = 8
NSTAT = BL * NH
ALPHA = 4.0 ** 0.25
SCALE = 0.125
NEG = -1e30
LN_EPS = 1e-5
ADAM_LR, ADAM_B1, ADAM_B2, ADAM_EPS, ADAM_WD, ADAM_STEP = 0.001, 0.9, 0.999, 1e-08, 0.01, 10
VMEM_LIMIT = 56 * 1024 * 1024
SMALL_ROWS = 16


def _bucket_thresholds():
    d = np.arange(0, S)
    nf = np.maximum(d, 1).astype(np.float32)
    large = 16 + (np.log(nf / np.float32(16)) / np.float32(math.log(128)) * np.float32(16)).astype(np.int32)
    b = np.where(d < 16, d, np.minimum(large, 31))
    return [int(np.argmax(b >= k)) for k in range(32)]


BUCKET_TH = _bucket_thresholds()


def _cp(sem=None):
    return pltpu.CompilerParams(dimension_semantics=sem, vmem_limit_bytes=VMEM_LIMIT)


def _dot(a, b):
    return lax.dot_general(a, b, (((1,), (0,)), ((), ())), preferred_element_type=F32)


def _dot_nt(a, b):
    return lax.dot_general(a, b, (((1,), (1,)), ((), ())), preferred_element_type=F32)


def _dot_tn(a, b):
    return lax.dot_general(a, b, (((0,), (0,)), ((), ())), preferred_element_type=F32)


def _split2(x):
    hi = x.astype(BF16)
    mid = (x - hi.astype(F32)).astype(BF16)
    return jnp.concatenate([hi, mid], axis=1)


def _split3(x):
    hi = x.astype(BF16)
    r = x - hi.astype(F32)
    mid = r.astype(BF16)
    lo = (r - mid.astype(F32)).astype(BF16)
    return jnp.concatenate([hi, mid, lo], axis=1)


def _log_sigmoid(u):
    return jnp.minimum(u, 0.0) - jnp.log1p(jnp.exp(-jnp.abs(u)))


def _log_sigmoid_tile(u):
    return jnp.minimum(u, 0.0) - jnp.log(1.0 + jnp.exp(jnp.minimum(u, -u)))


def _iota(shape, dim):
    return lax.broadcasted_iota(jnp.int32, shape, dim)


ANY_SPEC = pl.BlockSpec(memory_space=pl.ANY)
VMEM_SPEC = pl.BlockSpec(memory_space=pltpu.VMEM)


def _mm(pairs, *, nt, M, N, tm, tn, out_dtype, name, res=None, res_scale=1.0):
    n = len(pairs)

    def body(*refs):
        acc = None
        for p in range(n):
            a = refs[2 * p][...].astype(BF16)
            b = refs[2 * p + 1][...]
            d = _dot_nt(a, b) if nt else _dot(a, b)
            acc = d if acc is None else acc + d
        if res is not None:
            acc = acc + res_scale * refs[2 * n][...]
        refs[-1][...] = acc.astype(out_dtype)

    ops, specs = [], []
    for a, asp, b, bsp in pairs:
        ops += [a, b]
        specs += [asp, bsp]
    if res is not None:
        ops.append(res)
        specs.append(pl.BlockSpec((tm, tn), lambda i, j: (i, j)))
    return pl.pallas_call(
        body, name=name, grid=(M // tm, N // tn), in_specs=specs,
        out_specs=pl.BlockSpec((tm, tn), lambda i, j: (i, j)),
        out_shape=jax.ShapeDtypeStruct((M, N), out_dtype),
        compiler_params=_cp(("parallel", "parallel")))(*ops)


def _mm_tn(a, b, gbuf, *, C, Ka, N, tm, tn, tk, ooff, name):
    def body(*refs):
        a_ref, b_ref, o_ref = refs[0], refs[1], refs[-1]
        k = pl.program_id(2)
        d = _dot_tn(a_ref[...].astype(BF16), b_ref[...].astype(BF16))

        @pl.when(k == 0)
        def _():
            o_ref[...] = d

        @pl.when(k > 0)
        def _():
            o_ref[...] += d

    ops = [a, b] + ([] if gbuf is None else [gbuf])
    return pl.pallas_call(
        body, name=name, grid=(Ka // tm, N // tn, T // tk),
        in_specs=[pl.BlockSpec((tk, tm), lambda i, j, k: (k, i)),
                  pl.BlockSpec((tk, tn), lambda i, j, k: (k, j))] + ([] if gbuf is None else [ANY_SPEC]),
        out_specs=pl.BlockSpec((tm, tn), lambda i, j, k: (i, ooff + j)),
        out_shape=jax.ShapeDtypeStruct((Ka, C), F32),
        input_output_aliases={} if gbuf is None else {2: 0},
        compiler_params=_cp(("parallel", "parallel", "arbitrary")))(*ops)


def _ffn_up(x1, wgt, wut):
    tm, tn = 1024, 256

    def body(x_ref, wg_ref, wu_ref, g_ref, u_ref, a_ref):
        ch = 256
        for r in range(0, tm, ch):
            xb = x_ref[r:r + ch, :]
            g = _dot_nt(xb, wg_ref[...])
            u = _dot_nt(xb, wu_ref[...])
            g_ref[r:r + ch, :] = g.astype(BF16)
            u_ref[r:r + ch, :] = u.astype(BF16)
            a_ref[r:r + ch, :] = (g * jax.nn.sigmoid(g) * u).astype(BF16)

    wspec = pl.BlockSpec((tn, D), lambda i, j: (j, 0))
    ospec = pl.BlockSpec((tm, tn), lambda i, j: (i, j))
    return pl.pallas_call(
        body, name="ffn_up", grid=(T // tm, DFF // tn),
        in_specs=[pl.BlockSpec((tm, D), lambda i, j: (i, 0)), wspec, wspec],
        out_specs=[ospec, ospec, ospec],
        out_shape=[jax.ShapeDtypeStruct((T, DFF), BF16)] * 3,
        compiler_params=_cp(("parallel", "parallel")))(x1, wgt, wut)


def _ffn_da(dffn, wd, s, t):
    tm, tn = 1024, 256

    def body(d_ref, wd_ref, s_ref, t_ref, dg_ref, du_ref):
        ch = 256
        for r in range(0, tm, ch):
            da = _dot_nt(d_ref[r:r + ch, :], wd_ref[...])
            gv = s_ref[r:r + ch, :].astype(F32)
            sg = jax.nn.sigmoid(gv)
            dg_ref[r:r + ch, :] = (da * t_ref[r:r + ch, :].astype(F32) * (sg * (1.0 + gv * (1.0 - sg)))).astype(BF16)
            du_ref[r:r + ch, :] = (da * (gv * sg)).astype(BF16)

    ospec = pl.BlockSpec((tm, tn), lambda i, j: (i, j))
    return pl.pallas_call(
        body, name="ffn_da", grid=(T // tm, DFF // tn),
        in_specs=[pl.BlockSpec((tm, D), lambda i, j: (i, 0)),
                  pl.BlockSpec((tn, D), lambda i, j: (j, 0)), ospec, ospec],
        out_specs=[ospec, ospec],
        out_shape=[jax.ShapeDtypeStruct((T, DFF), BF16), jax.ShapeDtypeStruct((T, DFF), BF16)],
        compiler_params=_cp(("parallel", "parallel")))(dffn, wd, s, t)


def _mm_ln(a, w, x, gam, bet, name):
    tm = 256
    K = a.shape[1]

    def body(a_ref, w_ref, x_ref, g_ref, b_ref, y_ref, xh_ref, r_ref, yb_ref):
        s = ALPHA * x_ref[...] + _dot(a_ref[...], w_ref[...])
        mu = jnp.mean(s, axis=-1, keepdims=True)
        xc = s - mu
        var = jnp.mean(xc * xc, axis=-1, keepdims=True)
        r = lax.rsqrt(var + LN_EPS)
        xh = xc * r
        xh_ref[...] = xh.astype(BF16)
        r_ref[...] = r
        y = xh * g_ref[...] + b_ref[...]
        y_ref[...] = y
        yb_ref[...] = y.astype(BF16)

    row = pl.BlockSpec((tm, D), lambda i: (i, 0))
    vec = pl.BlockSpec((1, D), lambda i: (0, 0))
    return pl.pallas_call(
        body, name=name, grid=(T // tm,),
        in_specs=[pl.BlockSpec((tm, K), lambda i: (i, 0)), pl.BlockSpec((K, D), lambda i: (0, 0)), row, vec, vec],
        out_specs=[row, row, pl.BlockSpec((tm, 1), lambda i: (i, 0)), row],
        out_shape=[jax.ShapeDtypeStruct((T, D), F32), jax.ShapeDtypeStruct((T, D), BF16),
                   jax.ShapeDtypeStruct((T, 1), F32), jax.ShapeDtypeStruct((T, D), BF16)],
        compiler_params=_cp(("parallel",)))(a, w, x, gam, bet)


def _ln_bwd(dy, xh, r, gam):
    tm = 256

    def body(dy_ref, xh_ref, r_ref, g_ref, ds_ref, dg_ref, db_ref, dsb_ref):
        i = pl.program_id(0)
        dyv = dy_ref[...]
        xhv = xh_ref[...].astype(F32)
        dxh = dyv * g_ref[...]
        m1 = jnp.mean(dxh, axis=-1, keepdims=True)
        m2 = jnp.mean(dxh * xhv, axis=-1, keepdims=True)
        ds = r_ref[...] * (dxh - m1 - xhv * m2)
        ds_ref[...] = ds
        dsb_ref[...] = ds.astype(BF16)
        pg = jnp.sum(dyv * xhv, axis=0, keepdims=True)
        pb = jnp.sum(dyv, axis=0, keepdims=True)

        @pl.when(i == 0)
        def _():
            dg_ref[...] = pg
            db_ref[...] = pb

        @pl.when(i > 0)
        def _():
            dg_ref[...] += pg
            db_ref[...] += pb

    row = pl.BlockSpec((tm, D), lambda i: (i, 0))
    vec = pl.BlockSpec((1, D), lambda i: (0, 0))
    return pl.pallas_call(
        body, name="ln_bwd", grid=(T // tm,),
        in_specs=[row, row, pl.BlockSpec((tm, 1), lambda i: (i, 0)), vec],
        out_specs=[row, vec, vec, row],
        out_shape=[jax.ShapeDtypeStruct((T, D), F32), jax.ShapeDtypeStruct((1, D), F32),
                   jax.ShapeDtypeStruct((1, D), F32), jax.ShapeDtypeStruct((T, D), BF16)],
        compiler_params=_cp(("arbitrary",)))(dy, xh, r, gam)


def _loss_grad(y, tgt):
    tm = 256

    def body(y_ref, t_ref, l_ref, dy_ref):
        i = pl.program_id(0)
        e = y_ref[...] - t_ref[...]
        dy_ref[...] = e * (1.0 / D)
        p = jnp.sum(jnp.sum(e * e, axis=1, keepdims=True), axis=0, keepdims=True)

        @pl.when(i == 0)
        def _():
            l_ref[...] = p

        @pl.when(i > 0)
        def _():
            l_ref[...] += p

    row = pl.BlockSpec((tm, D), lambda i: (i, 0))
    return pl.pallas_call(
        body, name="loss_grad", grid=(T // tm,), in_specs=[row, row],
        out_specs=[pl.BlockSpec((1, 1), lambda i: (0, 0)), row],
        out_shape=[jax.ShapeDtypeStruct((1, 1), F32), jax.ShapeDtypeStruct((T, D), F32)],
        compiler_params=_cp(("arbitrary",)))(y, tgt)


def _adamw(w, g, m, v, tr):
    L, R, C = w.shape

    def body(w_ref, g_ref, m_ref, v_ref, d_ref, m2_ref, v2_ref):
        gv = g_ref[...]
        m2 = ADAM_B1 * m_ref[...] + (1.0 - ADAM_B1) * gv
        v2 = ADAM_B2 * v_ref[...] + (1.0 - ADAM_B2) * (gv * gv)
        m_hat = m2 / (1.0 - ADAM_B1 ** ADAM_STEP)
        v_hat = v2 / (1.0 - ADAM_B2 ** ADAM_STEP)
        d_ref[...] = -ADAM_LR * (m_hat / (jnp.sqrt(v_hat) + ADAM_EPS) + ADAM_WD * w_ref[...])
        m2_ref[...] = m2
        v2_ref[...] = v2

    blk = pl.BlockSpec((None, tr, C), lambda l, i: (l, i, 0))
    sh = jax.ShapeDtypeStruct((L, R, C), F32)
    return pl.pallas_call(
        body, name="adamw", grid=(L, R // tr), in_specs=[blk] * 4, out_specs=[blk] * 3,
        out_shape=[sh, sh, sh], compiler_params=_cp(("parallel", "parallel")))(w, g, m, v)


class _Job:
    def __init__(self, ins, out_shapes, aliases, sems, start, finish):
        self.ins, self.out_shapes, self.aliases, self.sems = list(ins), list(out_shapes), dict(aliases), list(sems)
        self.start, self.finish = start, finish


def _host_call(body, name, ins, in_specs, out_shapes, out_specs, scratch, aliases, job):
    n_in, n_out, n_scr = len(ins), len(out_shapes), len(scratch)
    jins = job.ins if job else []
    jouts = job.out_shapes if job else []
    jsems = job.sems if job else []

    def wrapped(*refs):
        a = n_in
        b = a + len(jins)
        c = b + n_out
        d = c + len(jouts)
        e = d + n_scr
        comm = None
        if job:
            jrefs = (refs[a:b], refs[c:d], refs[e:])
            comm = (lambda: job.start(*jrefs), lambda st: job.finish(st, *jrefs))
        body(refs[:a], refs[b:c], refs[d:e], comm)

    al = dict(aliases)
    if job:
        for ji, jo in job.aliases.items():
            al[n_in + ji] = n_out + jo
    res = pl.pallas_call(
        wrapped, name=name, in_specs=list(in_specs) + [ANY_SPEC] * len(jins),
        out_specs=list(out_specs) + [ANY_SPEC] * len(jouts), out_shape=list(out_shapes) + list(jouts),
        scratch_shapes=list(scratch) + list(jsems), input_output_aliases=al,
        compiler_params=_cp())(*ins, *jins)
    return res[:n_out], res[n_out:]


def _copy_in(src, dst, sem):
    cp = pltpu.make_async_copy(src, dst, sem)
    cp.start()
    cp.wait()


CHAINS = [(p, b) for p in range(2) for b in range(BL)]
NC = len(CHAINS)
ROWS_SHAPE = jax.ShapeDtypeStruct((NSTAT, S), F32)
SLAB_QKV = pltpu.VMEM((T, 2 * PAIRW), BF16)
SLAB_OUT = pltpu.VMEM((T, 2 * BQ), BF16)
SLAB_O32 = pltpu.VMEM((T, 2 * BQ), F32)
SLAB_T = pltpu.VMEM((2, BQ, T), BF16)
SLAB_KEYB = pltpu.VMEM((NSTAT, S, BQ), F32)
ACC_KV = pltpu.VMEM((2, T, BQ), F32)


def _lane_masks():
    lane = _iota((1, BQ), 1)
    m0 = (lane < 64).astype(BF16)
    return m0, 1.0 - m0


def _row_masks():
    r = _iota((BQ, 1), 0)
    m0 = (r < 64).astype(BF16)
    return m0, 1.0 - m0


def _stack(x, m0, m1):
    return jnp.concatenate([x * m0, x * m1], axis=0)


def _stack_t(xt, r0, r1):
    return jnp.concatenate([xt * r0, xt * r1], axis=1)


def _tr(x):
    return x.T


def _rows(b, i):
    return pl.ds(pl.multiple_of(b * S + i * BQ, BQ), BQ)


def _transpose_slab(src, dst, col0):
    def blk(n, _):
        r = pl.ds(pl.multiple_of(n * BQ, BQ), BQ)
        for p in range(2):
            dst[p, :, r] = _tr(src[r, col0(p):col0(p) + BQ])
        return 0

    lax.fori_loop(0, T // BQ, blk, 0)


def _heads(x):
    return x[:BQ], x[BQ:]


def _bcast_heads(r0, r1):
    return jnp.concatenate([jnp.broadcast_to(r0, (BQ, BQ)), jnp.broadcast_to(r1, (BQ, BQ))], axis=0)


def _by_channel(r0, r1):
    return jnp.where(_iota((BQ, BQ), 0) < 64, r0, r1)


def _colsum2(x):
    return jnp.sum(x[:BQ], axis=0, keepdims=True), jnp.sum(x[BQ:], axis=0, keepdims=True)


def _stat_row(ref, p, b, h, i):
    c = b * NH + 2 * p + h
    return ref[c:c + 1, pl.ds(pl.multiple_of(i * BQ, BQ), BQ)]


def _put_row(ref, p, b, h, i, v):
    c = b * NH + 2 * p + h
    ref[c:c + 1, pl.ds(pl.multiple_of(i * BQ, BQ), BQ)] = v


def _valid_t(strict):
    r = _iota((HB, BQ), 0) & (BQ - 1)
    c = _iota((HB, BQ), 1)
    return (r < c) if strict else (r <= c)


def _tri_blockdiag(later):
    r = _iota((HB, HB), 0)
    c = _iota((HB, HB), 1)
    same = (r >= BQ) == (c >= BQ)
    return (same & ((c > r) if later else (c < r))).astype(BF16)


def _cum_mm(tri, x):
    y = _dot(tri, _split2(x))
    return y[:, :BQ] + y[:, BQ:]


def _kv_tiles(qkv_v, p, b, j):
    r = _rows(b, j)
    return qkv_v[r, p * PAIRW + BQ:p * PAIRW + 2 * BQ], qkv_v[r, p * PAIRW + 2 * BQ:p * PAIRW + 3 * BQ]


def _q_tile(qkv_v, p, b, i):
    return qkv_v[_rows(b, i), p * PAIRW:p * PAIRW + BQ] * SCALE


def _sb_fwd(qkv, job=None):
    def body(ins, outs, scr, comm):
        (qkv_hbm,), (o_hbm, r_ref), (qkv_v, o_v, sem, vt_v) = ins, outs, scr
        _copy_in(qkv_hbm.at[:, pl.ds(0, 2 * PAIRW)], qkv_v, sem)
        st = comm[0]() if comm else None
        _transpose_slab(qkv_v, vt_v, lambda p: p * PAIRW + 2 * BQ)
        m0, m1 = _lane_masks()
        r0, r1 = _row_masks()
        valid = _valid_t(True)
        later = _tri_blockdiag(True)

        def steps(qts, i, j, cs, diag):
            ks = [_stack(_kv_tiles(qkv_v, p, b, j)[0], m0, m1) for p, b in CHAINS]
            zs = [_dot(ks[c], qts[c]) for c in range(NC)]
            lbs, lrs = [], []
            for c in range(NC):
                lb = _log_sigmoid_tile(zs[c])
                lr = lb - zs[c]
                if diag:
                    lr = jnp.where(valid, lr, 0.0)
                lbs.append(lb)
                lrs.append(lr)
            tails = [_cum_mm(later, lrs[c]) for c in range(NC)]
            avs = []
            for c in range(NC):
                a = jnp.exp(lbs[c] + tails[c] + _bcast_heads(*cs[c][0]))
                if diag:
                    a = jnp.where(valid, a, 0.0)
                avs.append(a.astype(BF16))
            out = []
            for c, (p, b) in enumerate(CHAINS):
                vts = _stack_t(vt_v[p, :, _rows(b, j)], r0, r1)
                s0, s1 = _colsum2(lrs[c])
                out.append(((cs[c][0][0] + s0, cs[c][0][1] + s1), cs[c][1] + _dot(vts, avs[c])))
            return tuple(out)

        def qblock(i, _):
            qts = [_tr(_q_tile(qkv_v, p, b, i)) for p, b in CHAINS]
            zr = jnp.zeros((1, BQ), F32)
            cs = steps(qts, i, i, (((zr, zr), jnp.zeros((BQ, BQ), F32)),) * NC, True)
            cs = lax.fori_loop(1, i + 1, lambda jj, cs: steps(qts, i, i - jj, cs, False), cs)
            for c, (p, b) in enumerate(CHAINS):
                o_v[_rows(b, i), p * BQ:(p + 1) * BQ] = cs[c][1].T.astype(BF16)
                for h in range(2):
                    _put_row(r_ref, p, b, h, i, cs[c][0][h])
            return 0

        lax.fori_loop(0, NB, qblock, 0)
        _copy_in(o_v, o_hbm.at[:, pl.ds(0, 2 * BQ)], sem)
        if comm:
            comm[1](st)

    (mixed, rtot), extra = _host_call(
        body, "sb_fwd", [qkv], [ANY_SPEC], [jax.ShapeDtypeStruct((T, D), BF16), ROWS_SHAPE], [ANY_SPEC, VMEM_SPEC],
        [SLAB_QKV, SLAB_OUT, pltpu.SemaphoreType.DMA, SLAB_T], {}, job)
    return mixed, rtot, extra


def _sb_bwd(qkv, dmixed, rtot, job=None):
    def body(ins, outs, scr, comm):
        (qkv_hbm, do_hbm, r_ref), (dqkv_hbm,), (qkv_v, do_v, dq_v, dk_s, dv_s, sem, kt_v) = ins, outs, scr
        _copy_in(qkv_hbm.at[:, pl.ds(0, 2 * PAIRW)], qkv_v, sem)
        _copy_in(do_hbm.at[:, pl.ds(0, 2 * BQ)], do_v, sem)
        st = comm[0]() if comm else None
        _transpose_slab(qkv_v, kt_v, lambda p: p * PAIRW + BQ)
        m0, m1 = _lane_masks()
        f0, f1 = m0.astype(F32), m1.astype(F32)
        r0, r1 = _row_masks()
        valid = _valid_t(True)
        later = _tri_blockdiag(True)
        earlier = _tri_blockdiag(False)
        dk_s[...] = jnp.zeros_like(dk_s)
        dv_s[...] = jnp.zeros_like(dv_s)

        def steps(qns, qts, dns, dts, rts, i, j, cs, diag):
            kv = [_kv_tiles(qkv_v, p, b, j) for p, b in CHAINS]
            ks = [_stack(kv[c][0], m0, m1) for c in range(NC)]
            vs = [_stack(kv[c][1], m0, m1) for c in range(NC)]
            zs = [_dot(ks[c], qts[c]) for c in range(NC)]
            das = [_dot(vs[c], dts[c]) for c in range(NC)]
            lbs, lrs, pls = [], [], []
            for c in range(NC):
                lb = _log_sigmoid_tile(zs[c])
                lr = lb - zs[c]
                if diag:
                    lr = jnp.where(valid, lr, 0.0)
                s0, s1 = _colsum2(lr)
                lbs.append(lb)
                lrs.append(lr)
                pls.append((cs[c][0][0] + s0, cs[c][0][1] + s1))
            tails = [_cum_mm(later, lrs[c]) for c in range(NC)]
            avs, gms = [], []
            for c in range(NC):
                a = jnp.exp(lbs[c] + tails[c] + _bcast_heads(rts[c][0] - pls[c][0], rts[c][1] - pls[c][1]))
                if diag:
                    a = jnp.where(valid, a, 0.0)
                avs.append(a)
                gms.append(das[c] * a)
            befores = [_cum_mm(earlier, gms[c]) for c in range(NC)]
            dzbs = []
            for c in range(NC):
                beta = jnp.exp(lbs[c])
                dz = gms[c] - beta * (gms[c] + befores[c] + _bcast_heads(*cs[c][1]))
                if diag:
                    dz = jnp.where(valid, dz, 0.0)
                dzbs.append(dz.astype(BF16))
            out = []
            for c, (p, b) in enumerate(CHAINS):
                dq = cs[c][2] + _dot(_stack_t(kt_v[p, :, _rows(b, j)], r0, r1), dzbs[c])
                dk = _dot(dzbs[c], qns[c])
                dv = _dot(avs[c].astype(BF16), dns[c])
                dk_s[p, _rows(b, j), :] += dk[:BQ] * f0 + dk[BQ:] * f1
                dv_s[p, _rows(b, j), :] += dv[:BQ] * f0 + dv[BQ:] * f1
                g0, g1 = _colsum2(gms[c])
                out.append((pls[c], (cs[c][1][0] + g0, cs[c][1][1] + g1), dq))
            return tuple(out)

        def qblock(i, _):
            qns = [_q_tile(qkv_v, p, b, i) for p, b in CHAINS]
            dns = [do_v[_rows(b, i), p * BQ:(p + 1) * BQ] for p, b in CHAINS]
            qts = [_tr(t) for t in qns]
            dts = [_tr(t) for t in dns]
            rts = [(_stat_row(r_ref, p, b, 0, i), _stat_row(r_ref, p, b, 1, i)) for p, b in CHAINS]
            zr = jnp.zeros((1, BQ), F32)
            cs = (((zr, zr), (zr, zr), jnp.zeros((BQ, BQ), F32)),) * NC
            cs = lax.fori_loop(0, i, lambda j, cs: steps(qns, qts, dns, dts, rts, i, j, cs, False), cs)
            cs = steps(qns, qts, dns, dts, rts, i, i, cs, True)
            for c, (p, b) in enumerate(CHAINS):
                dq_v[_rows(b, i), p * PAIRW:p * PAIRW + BQ] = (cs[c][2].T * SCALE).astype(BF16)
            return 0

        lax.fori_loop(0, NB, qblock, 0)
        for p in range(2):
            dq_v[:, p * PAIRW + BQ:p * PAIRW + 2 * BQ] = dk_s[p].astype(BF16)
            dq_v[:, p * PAIRW + 2 * BQ:p * PAIRW + 3 * BQ] = dv_s[p].astype(BF16)
        _copy_in(dq_v, dqkv_hbm.at[:, pl.ds(0, 2 * PAIRW)], sem)
        if comm:
            comm[1](st)

    (dqkv,), extra = _host_call(
        body, "sb_bwd", [qkv, dmixed, rtot], [ANY_SPEC, ANY_SPEC, VMEM_SPEC],
        [jax.ShapeDtypeStruct((T, QKVW), BF16)], [ANY_SPEC],
        [SLAB_QKV, SLAB_OUT, SLAB_QKV, ACC_KV, ACC_KV, pltpu.SemaphoreType.DMA, SLAB_T], {}, job)
    return dqkv, extra


def _flash_fwd(qkv, mixed, g, fox, bias, job=None):
    def body(ins, outs, scr, comm):
        if fox:
            qkv_hbm, cq_ref, ckb_hbm, _ = ins
            (o_hbm, lse_ref, o32_hbm), (qkv_v, o_v, sem, vt_v, o32_v, ckb_v) = outs, scr
        else:
            qkv_hbm, tbl_ref, _ = ins
            (o_hbm, lse_ref), (qkv_v, o_v, sem, vt_v) = outs, scr
        _copy_in(qkv_hbm.at[:, pl.ds(g * 2 * PAIRW, 2 * PAIRW)], qkv_v, sem)
        if fox:
            _copy_in(ckb_hbm, ckb_v, sem)
        st = comm[0]() if comm else None
        _transpose_slab(qkv_v, vt_v, lambda p: p * PAIRW + 2 * BQ)
        m0, m1 = _lane_masks()
        r0, r1 = _row_masks()
        valid = _valid_t(False)

        def steps(qts, cqs, i, j, cs, diag):
            ks = [_stack(_kv_tiles(qkv_v, p, b, j)[0], m0, m1) for p, b in CHAINS]
            zs = [_dot(ks[c], qts[c]) for c in range(NC)]
            prs, alphas, out = [], [], []
            for c, (p, b) in enumerate(CHAINS):
                (ma, mb), (la, lb_), _ = cs[c]
                if fox:
                    kk = pl.ds(pl.multiple_of(j * BQ, BQ), BQ)
                    col = b * NH + 2 * p
                    z = zs[c] + (cqs[c] - jnp.concatenate([ckb_v[col, kk, :], ckb_v[col + 1, kk, :]], axis=0))
                    if diag:
                        z = jnp.where(valid, z, NEG)
                else:
                    z = zs[c] + tbl_ref[p, i - j]
                za, zb = _heads(z)
                na = jnp.maximum(ma, jnp.max(za, axis=0, keepdims=True))
                nb = jnp.maximum(mb, jnp.max(zb, axis=0, keepdims=True))
                aa, ab = jnp.exp(ma - na), jnp.exp(mb - nb)
                pr = jnp.exp(z - _bcast_heads(na, nb))
                sa, sb = _colsum2(pr)
                prs.append(_split2(pr) if fox else pr.astype(BF16))
                alphas.append((aa, ab))
                out.append(((na, nb), (aa * la + sa, ab * lb_ + sb)))
            pvs = []
            for c, (p, b) in enumerate(CHAINS):
                vts = _stack_t(vt_v[p, :, _rows(b, j)], r0, r1)
                if fox:
                    pvs.append(_dot(vts, prs[c][:, :BQ]) + _dot(vts, prs[c][:, BQ:]))
                else:
                    pvs.append(_dot(vts, prs[c]))
            return tuple((out[c][0], out[c][1], _by_channel(*alphas[c]) * cs[c][2] + pvs[c]) for c in range(NC))

        def qblock(i, _):
            qts = [_tr(_q_tile(qkv_v, p, b, i)) for p, b in CHAINS]
            if fox:
                cqs = [_bcast_heads(_stat_row(cq_ref, p, b, 0, i), _stat_row(cq_ref, p, b, 1, i)) for p, b in CHAINS]
            else:
                cqs = [None] * NC
            ng = jnp.full((1, BQ), NEG, F32)
            zr = jnp.zeros((1, BQ), F32)
            cs = steps(qts, cqs, i, i, (((ng, ng), (zr, zr), jnp.zeros((BQ, BQ), F32)),) * NC, True)
            cs = lax.fori_loop(1, i + 1, lambda jj, cs: steps(qts, cqs, i, i - jj, cs, False), cs)
            for c, (p, b) in enumerate(CHAINS):
                (ma, mb), (la, lb_), acc = cs[c]
                o = (acc / _by_channel(la, lb_)).T
                o_v[_rows(b, i), p * BQ:(p + 1) * BQ] = o.astype(BF16)
                if fox:
                    o32_v[_rows(b, i), p * BQ:(p + 1) * BQ] = o
                _put_row(lse_ref, p, b, 0, i, ma + jnp.log(la))
                _put_row(lse_ref, p, b, 1, i, mb + jnp.log(lb_))
            return 0

        lax.fori_loop(0, NB, qblock, 0)
        _copy_in(o_v, o_hbm.at[:, pl.ds(g * 2 * BQ, 2 * BQ)], sem)
        if fox:
            _copy_in(o32_v, o32_hbm, sem)
        if comm:
            comm[1](st)

    bias_specs = [VMEM_SPEC, ANY_SPEC] if fox else [VMEM_SPEC]
    n_in = 2 + len(bias_specs)
    o32 = [jax.ShapeDtypeStruct((T, 2 * BQ), F32)] if fox else []
    res, extra = _host_call(
        body, "fox_fwd" if fox else "dil_fwd", [qkv, *bias, mixed], [ANY_SPEC] + bias_specs + [ANY_SPEC],
        [jax.ShapeDtypeStruct((T, D), BF16), ROWS_SHAPE] + o32, [ANY_SPEC, VMEM_SPEC] + [ANY_SPEC] * len(o32),
        [SLAB_QKV, SLAB_OUT, pltpu.SemaphoreType.DMA, SLAB_T] + ([SLAB_O32, SLAB_KEYB] if fox else []),
        {n_in - 1: 0}, job)
    return (*res, extra)


def _flash_bwd(qkv, o, dmixed, lse, dqkv, g, fox, bias, job=None):
    def body(ins, outs, scr, comm):
        if fox:
            qkv_hbm, o_hbm, do_hbm, lse_ref, cq_ref, ckb_hbm, _ = ins
            (dqkv_hbm, db_ref), (qkv_v, o_v, do_v, dq_v, dk_s, dv_s, sem, kt_v, ckb_v, dc_s) = outs, scr
        else:
            qkv_hbm, o_hbm, do_hbm, lse_ref, tbl_ref, _ = ins
            (dqkv_hbm, db_ref), (qkv_v, o_v, do_v, dq_v, dk_s, dv_s, sem, kt_v) = outs, scr
        _copy_in(qkv_hbm.at[:, pl.ds(g * 2 * PAIRW, 2 * PAIRW)], qkv_v, sem)
        _copy_in(do_hbm.at[:, pl.ds(g * 2 * BQ, 2 * BQ)], do_v, sem)
        if fox:
            _copy_in(o_hbm, o_v, sem)
            _copy_in(ckb_hbm, ckb_v, sem)
        else:
            _copy_in(o_hbm.at[:, pl.ds(g * 2 * BQ, 2 * BQ)], o_v, sem)
        st = comm[0]() if comm else None
        _transpose_slab(qkv_v, kt_v, lambda p: p * PAIRW + BQ)
        m0, m1 = _lane_masks()
        f0, f1 = m0.astype(F32), m1.astype(F32)
        r0, r1 = _row_masks()
        valid = _valid_t(False)
        dk_s[...] = jnp.zeros_like(dk_s)
        dv_s[...] = jnp.zeros_like(dv_s)
        if fox:
            dc_s[...] = jnp.zeros_like(dc_s)
        else:
            db_ref[...] = jnp.zeros_like(db_ref)

        def steps(qns, qts, dns, dts, cqs, lses, deltas, i, j, dqs, diag):
            kv = [_kv_tiles(qkv_v, p, b, j) for p, b in CHAINS]
            ks = [_stack(kv[c][0], m0, m1) for c in range(NC)]
            vs = [_stack(kv[c][1], m0, m1) for c in range(NC)]
            zs = [_dot(ks[c], qts[c]) for c in range(NC)]
            dps = [_dot(vs[c], dts[c]) for c in range(NC)]
            prs, dzl = [], []
            for c, (p, b) in enumerate(CHAINS):
                if fox:
                    kk = pl.ds(pl.multiple_of(j * BQ, BQ), BQ)
                    col = b * NH + 2 * p
                    z = zs[c] + (cqs[c] - jnp.concatenate([ckb_v[col, kk, :], ckb_v[col + 1, kk, :]], axis=0))
                    if diag:
                        z = jnp.where(valid, z, NEG)
                else:
                    z = zs[c] + tbl_ref[p, i - j]
                pr = jnp.exp(z - lses[c])
                prs.append(pr.astype(BF16))
                dzl.append(pr * (dps[c] - deltas[c]))
            dzbs = [dz.astype(BF16) for dz in dzl]
            new = []
            for c, (p, b) in enumerate(CHAINS):
                new.append(dqs[c] + _dot(_stack_t(kt_v[p, :, _rows(b, j)], r0, r1), dzbs[c]))
                dk = _dot(dzbs[c], qns[c])
                dv = _dot(prs[c], dns[c])
                dk_s[p, _rows(b, j), :] += dk[:BQ] * f0 + dk[BQ:] * f1
                dv_s[p, _rows(b, j), :] += dv[:BQ] * f0 + dv[BQ:] * f1
                if fox:
                    dc_s[c, pl.ds(pl.multiple_of(j * HB, HB), HB), :] += dzl[c]
            if not fox:
                for p in range(2):
                    db_ref[p, i - j] = db_ref[p, i - j] + (dzl[2 * p] + dzl[2 * p + 1])
            return tuple(new)

        def qblock(i, _):
            qns = [_q_tile(qkv_v, p, b, i) for p, b in CHAINS]
            dns = [do_v[_rows(b, i), p * BQ:(p + 1) * BQ] for p, b in CHAINS]
            qts = [_tr(t) for t in qns]
            dts = [_tr(t) for t in dns]
            lses = [_bcast_heads(_stat_row(lse_ref, p, b, 0, i), _stat_row(lse_ref, p, b, 1, i)) for p, b in CHAINS]
            if fox:
                cqs = [_bcast_heads(_stat_row(cq_ref, p, b, 0, i), _stat_row(cq_ref, p, b, 1, i)) for p, b in CHAINS]
            else:
                cqs = [None] * NC
            deltas = []
            for c, (p, b) in enumerate(CHAINS):
                pt = (dns[c].astype(F32) * o_v[_rows(b, i), p * BQ:(p + 1) * BQ].astype(F32)).T
                deltas.append(_bcast_heads(jnp.sum(pt[:64], axis=0, keepdims=True), jnp.sum(pt[64:], axis=0, keepdims=True)))
            dqs = (jnp.zeros((BQ, BQ), F32),) * NC
            dqs = lax.fori_loop(0, i, lambda j, d: steps(qns, qts, dns, dts, cqs, lses, deltas, i, j, d, False), dqs)
            dqs = steps(qns, qts, dns, dts, cqs, lses, deltas, i, i, dqs, True)
            for c, (p, b) in enumerate(CHAINS):
                dq_v[_rows(b, i), p * PAIRW:p * PAIRW + BQ] = (dqs[c].T * SCALE).astype(BF16)
            return 0

        lax.fori_loop(0, NB, qblock, 0)
        for p in range(2):
            dq_v[:, p * PAIRW + BQ:p * PAIRW + 2 * BQ] = dk_s[p].astype(BF16)
            dq_v[:, p * PAIRW + 2 * BQ:p * PAIRW + 3 * BQ] = dv_s[p].astype(BF16)
        _copy_in(dq_v, dqkv_hbm.at[:, pl.ds(g * 2 * PAIRW, 2 * PAIRW)], sem)
        if fox:
            lane = _iota((BQ, NSTAT), 1)

            def fold(n, _):
                t = jnp.zeros((BQ, NSTAT), F32)
                for c, (p, b) in enumerate(CHAINS):
                    s = jnp.sum(dc_s[c, pl.ds(pl.multiple_of(n * HB, HB), HB), :], axis=1, keepdims=True)
                    col = b * NH + 2 * p
                    t = t - jnp.where(lane == col, s[:BQ], 0.0) - jnp.where(lane == col + 1, s[BQ:], 0.0)
                db_ref[pl.ds(pl.multiple_of(n * BQ, BQ), BQ), :] = t
                return 0

            lax.fori_loop(0, NB, fold, 0)
        if comm:
            comm[1](st)

    if fox:
        bias_specs = [VMEM_SPEC, ANY_SPEC]
        db_shape = jax.ShapeDtypeStruct((S, NSTAT), F32)
        more = [SLAB_KEYB, pltpu.VMEM((NC, NB * HB, BQ), F32)]
    else:
        bias_specs = [VMEM_SPEC]
        db_shape = jax.ShapeDtypeStruct((2, NB, HB, BQ), F32)
        more = []
    n_in = 5 + len(bias_specs)
    (dqkv, db), extra = _host_call(
        body, "fox_bwd" if fox else "dil_bwd", [qkv, o, dmixed, lse, *bias, dqkv],
        [ANY_SPEC, ANY_SPEC, ANY_SPEC, VMEM_SPEC] + bias_specs + [ANY_SPEC],
        [jax.ShapeDtypeStruct((T, QKVW), BF16), db_shape], [ANY_SPEC, VMEM_SPEC],
        [SLAB_QKV, SLAB_O32 if fox else SLAB_OUT, SLAB_OUT, SLAB_QKV, ACC_KV, ACC_KV, pltpu.SemaphoreType.DMA, SLAB_T]
        + more, {n_in - 1: 0}, job)
    return dqkv, db, extra


def _delta_t(d):
    return d * BQ + _iota((HB, BQ), 1) - (_iota((HB, BQ), 0) & (BQ - 1))


def _buckets_in(d):
    lo, hi = max(d * BQ - (BQ - 1), 0), d * BQ + BQ - 1
    return [b for b in range(32) if BUCKET_TH[b] <= hi and (b == 31 or BUCKET_TH[b + 1] > lo)]


def _in_bucket(delta, b):
    m = delta >= BUCKET_TH[b]
    return m if b == 31 else m & (delta < BUCKET_TH[b + 1])


def _dil_table(rel_bias):
    def body(rb_ref, o_ref):
        for d in range(NB):
            delta = _delta_t(d)
            pos = delta >= 0
            n = ((pos & (delta <= 128)).astype(jnp.int32)
                 + (pos & (delta <= 512) & ((delta & 3) == 0)).astype(jnp.int32)
                 + (pos & ((delta & 15) == 0)).astype(jnp.int32))
            logn = jnp.where(n == 3, math.log(3.0), jnp.where(n == 2, math.log(2.0), jnp.where(n == 1, 0.0, NEG)))
            head1 = _iota((HB, BQ), 0) >= BQ
            for p in range(2):
                val = jnp.zeros((HB, BQ), F32)
                for b in _buckets_in(d):
                    val = jnp.where(_in_bucket(delta, b), jnp.where(head1, rb_ref[b, 2 * p + 1], rb_ref[b, 2 * p]), val)
                o_ref[p, d] = val + logn

    return pl.pallas_call(
        body, name="dil_table", in_specs=[pl.BlockSpec(memory_space=pltpu.SMEM)], out_specs=VMEM_SPEC,
        out_shape=jax.ShapeDtypeStruct((2, NB, HB, BQ), F32), compiler_params=_cp())(rel_bias)


def _dil_table_bwd(dtbl):
    def body(dt_ref, o_ref):
        p = pl.program_id(0)
        rowi = _iota((32, BQ), 0)
        lanei = _iota((32, BQ), 1)

        @pl.when(p == 0)
        def _():
            o_ref[...] = jnp.zeros_like(o_ref)

        out = jnp.zeros((32, BQ), F32)
        for b in range(32):
            acc = None
            for d in range(NB):
                if b in _buckets_in(d):
                    t = jnp.where(_in_bucket(_delta_t(d), b), dt_ref[d], 0.0)
                    acc = t if acc is None else acc + t
            rs = jnp.sum(acc, axis=1, keepdims=True)
            s0 = jnp.sum(rs[:BQ], axis=0, keepdims=True)
            s1 = jnp.sum(rs[BQ:], axis=0, keepdims=True)
            out = (out + jnp.where((rowi == b) & (lanei == 2 * p), s0, 0.0)
                   + jnp.where((rowi == b) & (lanei == 2 * p + 1), s1, 0.0))
        o_ref[...] += out

    return pl.pallas_call(
        body, name="dil_table_bwd", grid=(2,),
        in_specs=[pl.BlockSpec((None, NB, HB, BQ), lambda p: (p, 0, 0, 0))],
        out_specs=pl.BlockSpec((32, BQ), lambda p: (0, 0)),
        out_shape=jax.ShapeDtypeStruct((32, BQ), F32),
        compiler_params=_cp(("arbitrary",)))(dtbl)


def _fox_prep(gate, fb):
    def body(g_ref, fb_ref, c_ref):
        tri = (_iota((BQ, BQ), 0) >= _iota((BQ, BQ), 1)).astype(BF16)

        def blk(i, carry):
            r0 = pl.multiple_of(i * BQ, BQ)
            lf = _log_sigmoid(g_ref[pl.ds(r0, BQ), :] + fb_ref[...])
            c = _dot(tri, _split3(lf))
            c_ref[pl.ds(r0, BQ), :] = c[:, 0:BQ] + c[:, BQ:2 * BQ] + c[:, 2 * BQ:3 * BQ] + carry
            return carry + jnp.sum(lf, axis=0, keepdims=True)

        lax.fori_loop(0, NB, blk, jnp.zeros((1, BQ), F32))

    blk = pl.BlockSpec((S, GATEW), lambda b: (b, 0))
    return pl.pallas_call(
        body, name="fox_prep", grid=(BL,), in_specs=[blk, pl.BlockSpec((1, GATEW), lambda b: (0, 0))],
        out_specs=blk, out_shape=jax.ShapeDtypeStruct((T, GATEW), F32),
        compiler_params=_cp(("parallel",)))(gate, fb)


def _fox_post(dcum, gate, fb):
    def body(dc_ref, g_ref, fb_ref, dg_ref, dfb_ref):
        b = pl.program_id(0)
        tri = (_iota((BQ, BQ), 0) <= _iota((BQ, BQ), 1)).astype(BF16)

        def blk(ii, carry):
            csum, dfb = carry
            r0 = pl.multiple_of((NB - 1 - ii) * BQ, BQ)
            dc = dc_ref[pl.ds(r0, BQ), :]
            c = _dot(tri, _split3(dc))
            dlf = c[:, 0:BQ] + c[:, BQ:2 * BQ] + c[:, 2 * BQ:3 * BQ] + csum
            dg = dlf * jnp.exp(_log_sigmoid(-(g_ref[pl.ds(r0, BQ), :] + fb_ref[...])))
            dg_ref[pl.ds(r0, BQ), :] = dg
            return csum + jnp.sum(dc, axis=0, keepdims=True), dfb + jnp.sum(dg, axis=0, keepdims=True)

        z = jnp.zeros((1, BQ), F32)
        _, dfb = lax.fori_loop(0, NB, blk, (z, z))

        @pl.when(b == 0)
        def _():
            dfb_ref[...] = dfb

        @pl.when(b > 0)
        def _():
            dfb_ref[...] += dfb

    blk = pl.BlockSpec((S, GATEW), lambda b: (b, 0))
    vec = pl.BlockSpec((1, GATEW), lambda b: (0, 0))
    return pl.pallas_call(
        body, name="fox_post", grid=(BL,), in_specs=[blk, blk, vec], out_specs=[blk, vec],
        out_shape=[jax.ShapeDtypeStruct((T, GATEW), F32), jax.ShapeDtypeStruct((1, GATEW), F32)],
        compiler_params=_cp(("arbitrary",)))(dcum, gate, fb)


def _shift_down(x, n):
    return jnp.where(_iota(x.shape, 0) >= n, pltpu.roll(x, n, 0), 0.0)


def _shift_up(x, n):
    return jnp.where(_iota(x.shape, 0) < S - n, pltpu.roll(x, S - n, 0), 0.0)


def _conv_fwd(conv, cw, mixed):
    W = 256

    def body(c_ref, w_ref, _, o_ref):
        u = c_ref[:, W:2 * W] * c_ref[:, 2 * W:3 * W]
        y = w_ref[0:1, :] * _shift_down(u, 2) + w_ref[1:2, :] * _shift_down(u, 1) + w_ref[2:3, :] * u
        o_ref[...] = (c_ref[:, 0:W] * y).astype(BF16)

    return pl.pallas_call(
        body, name="conv_fwd", grid=(BL,),
        in_specs=[pl.BlockSpec((S, CONVW), lambda b: (b, 0)), pl.BlockSpec((8, W), lambda b: (0, 0)), ANY_SPEC],
        out_specs=pl.BlockSpec((S, W), lambda b: (b, 3)),
        out_shape=jax.ShapeDtypeStruct((T, D), BF16), input_output_aliases={2: 0},
        compiler_params=_cp(("parallel",)))(conv, cw, mixed)


def _conv_bwd(conv, cw, dmixed):
    W = 256

    def body(c_ref, w_ref, do_ref, dc_ref, dw_ref):
        b = pl.program_id(0)
        bg = c_ref[:, 0:W]
        cg = c_ref[:, W:2 * W]
        hv = c_ref[:, 2 * W:3 * W]
        do = do_ref[...].astype(F32)
        u = cg * hv
        u1 = _shift_down(u, 1)
        u2 = _shift_down(u, 2)
        y = w_ref[0:1, :] * u2 + w_ref[1:2, :] * u1 + w_ref[2:3, :] * u
        dy = do * bg
        du = w_ref[2:3, :] * dy + w_ref[1:2, :] * _shift_up(dy, 1) + w_ref[0:1, :] * _shift_up(dy, 2)
        dc_ref[:, 0:W] = (do * y).astype(BF16)
        dc_ref[:, W:2 * W] = (du * hv).astype(BF16)
        dc_ref[:, 2 * W:3 * W] = (du * cg).astype(BF16)
        rowi = _iota((8, W), 0)
        dw = (jnp.where(rowi == 0, jnp.sum(dy * u2, axis=0, keepdims=True), 0.0)
              + jnp.where(rowi == 1, jnp.sum(dy * u1, axis=0, keepdims=True), 0.0)
              + jnp.where(rowi == 2, jnp.sum(dy * u, axis=0, keepdims=True), 0.0))

        @pl.when(b == 0)
        def _():
            dw_ref[...] = dw

        @pl.when(b > 0)
        def _():
            dw_ref[...] += dw

    return pl.pallas_call(
        body, name="conv_bwd", grid=(BL,),
        in_specs=[pl.BlockSpec((S, CONVW), lambda b: (b, 0)), pl.BlockSpec((8, W), lambda b: (0, 0)),
                  pl.BlockSpec((S, W), lambda b: (b, 3))],
        out_specs=[pl.BlockSpec((S, CONVW), lambda b: (b, 0)), pl.BlockSpec((8, W), lambda b: (0, 0))],
        out_shape=[jax.ShapeDtypeStruct((T, CONVW), BF16), jax.ShapeDtypeStruct((8, W), F32)],
        compiler_params=_cp(("arbitrary",)))(conv, cw, dmixed)


def _place():
    x, y, c = lax.axis_index("x"), lax.axis_index("y"), lax.axis_index("c")
    return x, y, c


def _chips_of(x, y):
    return [(1 - x, y), (x, 1 - y), (1 - x, 1 - y)]


def _dev(p):
    return 4 * p[0] + 2 * p[1] + p[2]


def _gather_job_a(shards):
    n = len(shards)

    def peers(x, y, c):
        return [(x, y, 1 - c)] + [(*chip, c) for chip in _chips_of(x, y)]

    def start(ins, outs, sems):
        send, recv, loc = sems
        x, y, c = _place()
        me = (x, y, c)
        cps = []
        for a in range(n):
            cps.append(pltpu.make_async_copy(ins[a], outs[a].at[_dev(me)], loc.at[a]))
            for k, peer in enumerate(peers(x, y, c)):
                cps.append(pltpu.make_async_remote_copy(
                    src_ref=ins[a], dst_ref=outs[a].at[_dev(me)], send_sem=send.at[a, k], recv_sem=recv.at[a, k],
                    device_id=peer, device_id_type=MESH))
        for cp in cps:
            cp.start()
        return cps

    def finish(cps, ins, outs, sems):
        send, recv, loc = sems
        x, y, c = _place()
        for a in range(n):
            for k, peer in enumerate(peers(x, y, c)):
                pltpu.make_async_remote_copy(
                    src_ref=ins[a], dst_ref=outs[a].at[_dev(peer)], send_sem=send.at[a, k], recv_sem=recv.at[a, k],
                    device_id=(x, y, c), device_id_type=MESH).wait_recv()
        for a in range(n):
            cps[5 * a].wait()
            for k in range(4):
                cps[5 * a + 1 + k].wait_send()

    return _Job(shards, [jax.ShapeDtypeStruct((NDEV,) + s.shape, s.dtype) for s in shards], {},
                [pltpu.SemaphoreType.DMA((n, 4)), pltpu.SemaphoreType.DMA((n, 4)), pltpu.SemaphoreType.DMA((n,))],
                start, finish)


def _gather_job_b(gathered):
    n = len(gathered)

    def start(ins, outs, sems):
        send, recv = sems
        x, y, c = _place()
        cps = []
        for a in range(n):
            for j, chip in enumerate(_chips_of(x, y)):
                blk = outs[a].at[_dev((*chip, c))]
                cps.append(pltpu.make_async_remote_copy(
                    src_ref=blk, dst_ref=blk, send_sem=send.at[a, j], recv_sem=recv.at[a, j],
                    device_id=(x, y, 1 - c), device_id_type=MESH))
        for cp in cps:
            cp.start()
        return cps

    def finish(cps, ins, outs, sems):
        send, recv = sems
        x, y, c = _place()
        for a in range(n):
            for j, chip in enumerate(_chips_of(x, y)):
                blk = outs[a].at[_dev((*chip, 1 - c))]
                pltpu.make_async_remote_copy(
                    src_ref=blk, dst_ref=blk, send_sem=send.at[a, j], recv_sem=recv.at[a, j],
                    device_id=(x, y, c), device_id_type=MESH).wait_recv()
        for cp in cps:
            cp.wait_send()

    return _Job(gathered, [jax.ShapeDtypeStruct(g.shape, g.dtype) for g in gathered], {a: a for a in range(n)},
                [pltpu.SemaphoreType.DMA((n, 3)), pltpu.SemaphoreType.DMA((n, 3))], start, finish)


def _sibling_job(grads):
    n = len(grads)

    def start(ins, outs, sems):
        send, recv = sems
        x, y, c = _place()
        cps = [pltpu.make_async_remote_copy(
            src_ref=ins[a].at[:, 1 - c], dst_ref=outs[a], send_sem=send.at[a], recv_sem=recv.at[a],
            device_id=(x, y, 1 - c), device_id_type=MESH) for a in range(n)]
        for cp in cps:
            cp.start()
        return cps

    def finish(cps, ins, outs, sems):
        for cp in cps:
            cp.wait()

    return _Job(grads, [jax.ShapeDtypeStruct(g.shape[:1] + g.shape[2:], F32) for g in grads], {},
                [pltpu.SemaphoreType.DMA((n,)), pltpu.SemaphoreType.DMA((n,))], start, finish)


def _chip_job(psums):
    n = len(psums)

    def start(ins, outs, sems):
        send, recv, loc = sems
        x, y, c = _place()
        mychip = 2 * x + y
        cps = []
        for a in range(n):
            cps.append(pltpu.make_async_copy(ins[a].at[mychip], outs[a].at[mychip], loc.at[a]))
            for j, chip in enumerate(_chips_of(x, y)):
                cps.append(pltpu.make_async_remote_copy(
                    src_ref=ins[a].at[2 * chip[0] + chip[1]], dst_ref=outs[a].at[mychip],
                    send_sem=send.at[a, j], recv_sem=recv.at[a, j], device_id=(*chip, c), device_id_type=MESH))
        for cp in cps:
            cp.start()
        return cps

    def finish(cps, ins, outs, sems):
        send, recv, loc = sems
        x, y, c = _place()
        mychip = 2 * x + y
        for a in range(n):
            for j, chip in enumerate(_chips_of(x, y)):
                pltpu.make_async_remote_copy(
                    src_ref=ins[a].at[mychip], dst_ref=outs[a].at[2 * chip[0] + chip[1]],
                    send_sem=send.at[a, j], recv_sem=recv.at[a, j], device_id=(x, y, c), device_id_type=MESH).wait_recv()
        for a in range(n):
            cps[4 * a].wait()
            for j in range(3):
                cps[4 * a + 1 + j].wait_send()

    return _Job(psums, [jax.ShapeDtypeStruct(p.shape, BF16) for p in psums], {},
                [pltpu.SemaphoreType.DMA((n, 3)), pltpu.SemaphoreType.DMA((n, 3)), pltpu.SemaphoreType.DMA((n,))],
                start, finish)


def _join_jobs(*jobs):
    jobs = [j for j in jobs if j is not None]
    if len(jobs) <= 1:
        return jobs[0] if jobs else None
    cut = lambda seq, sizes: [seq[sum(sizes[:k]):sum(sizes[:k + 1])] for k in range(len(sizes))]
    n_in = [len(j.ins) for j in jobs]
    n_out = [len(j.out_shapes) for j in jobs]
    n_sem = [len(j.sems) for j in jobs]
    aliases = {}
    for k, j in enumerate(jobs):
        for a, b in j.aliases.items():
            aliases[sum(n_in[:k]) + a] = sum(n_out[:k]) + b

    def start(ins, outs, sems):
        return [j.start(i, o, s) for j, i, o, s in zip(jobs, cut(ins, n_in), cut(outs, n_out), cut(sems, n_sem))]

    def finish(sts, ins, outs, sems):
        for j, st, i, o, s in zip(jobs, sts, cut(ins, n_in), cut(outs, n_out), cut(sems, n_sem)):
            j.finish(st, i, o, s)

    return _Job([t for j in jobs for t in j.ins], [t for j in jobs for t in j.out_shapes], aliases,
                [t for j in jobs for t in j.sems], start, finish)


def _run_job(job, name):
    def body(ins, outs, scr, comm):
        comm[1](comm[0]())

    return _host_call(body, name, [], [], [], [], [], {}, job)[1]


def _allreduce_small(v):
    def body(v_ref, o_ref, slots, send_sems, recv_sems):
        x, y, c = _place()
        me = 4 * x + 2 * y + c
        slots[me] = v_ref[...]

        def copy(k):
            peer = (x ^ ((k >> 2) & 1), y ^ ((k >> 1) & 1), c ^ (k & 1))
            return pltpu.make_async_remote_copy(
                src_ref=v_ref, dst_ref=slots.at[me], send_sem=send_sems.at[k - 1], recv_sem=recv_sems.at[k - 1],
                device_id=peer, device_id_type=MESH)

        def arrival(k):
            return pltpu.make_async_remote_copy(
                src_ref=v_ref, dst_ref=slots.at[me ^ k], send_sem=send_sems.at[k - 1], recv_sem=recv_sems.at[k - 1],
                device_id=(x, y, c), device_id_type=MESH)

        sends = [copy(k) for k in range(1, NDEV)]
        for cp in sends:
            cp.start()
        for k in range(1, NDEV):
            arrival(k).wait_recv()
        for cp in sends:
            cp.wait_send()
        acc = slots[0]
        for d in range(1, NDEV):
            acc = acc + slots[d]
        o_ref[...] = acc

    return pl.pallas_call(
        body, name="allreduce_small", in_specs=[VMEM_SPEC], out_specs=VMEM_SPEC,
        out_shape=jax.ShapeDtypeStruct(v.shape, F32),
        scratch_shapes=[pltpu.VMEM((NDEV,) + v.shape, F32), pltpu.SemaphoreType.DMA((NDEV - 1,)),
                        pltpu.SemaphoreType.DMA((NDEV - 1,))],
        )(v)


def _pair_sums(views, gots, core):
    n = len(views)

    def body(c_ref, *refs):
        for a in range(n):
            refs[2 * n + a][...] = (refs[a][...] + refs[n + a][...]).astype(BF16)

    def vspec(v):
        return pl.BlockSpec((None, None) + v.shape[2:], lambda k, c: (k, c[0], 0, 0))

    def gspec(g):
        return pl.BlockSpec((None,) + g.shape[1:], lambda k, c: (k, 0, 0))

    return pl.pallas_call(
        body, name="pair_sums",
        grid_spec=pltpu.PrefetchScalarGridSpec(
            num_scalar_prefetch=1, grid=(4,),
            in_specs=[vspec(v) for v in views] + [gspec(g) for g in gots],
            out_specs=[gspec(g) for g in gots]),
        out_shape=[jax.ShapeDtypeStruct(g.shape, BF16) for g in gots],
        compiler_params=_cp(("parallel",)))(core, *views, *gots)


def _chip_sums(parts):
    n = len(parts)

    def body(*refs):
        for a in range(n):
            acc = refs[a][0].astype(F32)
            for k in range(1, 4):
                acc = acc + refs[a][k].astype(F32)
            refs[n + a][...] = acc

    return pl.pallas_call(
        body, name="chip_sums", in_specs=[VMEM_SPEC] * n, out_specs=[VMEM_SPEC] * n,
        out_shape=[jax.ShapeDtypeStruct(p.shape[1:], F32) for p in parts], compiler_params=_cp())(*parts)


def _permute_in(w):
    lead = w.shape[:-1]
    return w.reshape(lead + (3, 3, 2, BQ)).swapaxes(-2, -3).reshape(lead + (QKVW,))


def _unpermute_in(w):
    lead = w.shape[:-1]
    return w.reshape(lead + (3, 2, 3, BQ)).swapaxes(-2, -3).reshape(lead + (QKVW,))


def _row(v):
    v = v.reshape(-1)
    return jnp.pad(v, (0, D - v.shape[0])).reshape(1, D)


def kernel(x, w_in, f_bias, conv_w, w_out, rel_bias, ln1_g, ln1_b, w_gate, w_up, w_down, ln2_g, ln2_b, loss_target, m_w_in, m_f_bias, m_conv_w, m_w_out, m_rel_bias, m_ln1_g, m_ln1_b, m_w_gate, m_w_up, m_w_down, m_ln2_g, m_ln2_b, v_w_in, v_f_bias, v_conv_w, v_w_out, v_rel_bias, v_ln1_g, v_ln1_b, v_w_gate, v_w_up, v_w_down, v_ln2_g, v_ln2_b):
    xi, yi, ci = _place()
    me = 4 * xi + 2 * yi + ci
    core = jnp.reshape(ci, (1,)).astype(jnp.int32)

    win_s = jnp.concatenate([_permute_in(w_in[..., :QKVW]), w_in[..., QKVW:]], axis=-1)
    win_s = jnp.pad(win_s, ((0, 0), (0, 0), (0, NPAD - NPROJ))).astype(BF16)
    per_layer = [win_s, w_out.astype(BF16), jnp.swapaxes(w_gate, 1, 2).astype(BF16),
                 jnp.swapaxes(w_up, 1, 2).astype(BF16), w_down.astype(BF16)]
    sh = [[s[l] for s in per_layer] for l in range(2)]

    def whole(g):
        return g.reshape(NDEV * g.shape[1], g.shape[2])

    first = _run_job(_gather_job_b(_run_job(_gather_job_a(sh[0][:1]), "gather_a")), "gather_b")
    W = [{"win": whole(first[0])}, {}]

    cw_rows = lax.dynamic_update_slice(jnp.zeros((2, 3, 256), F32), conv_w, (0, 0, me * 32))
    small = jnp.concatenate([_row(cw_rows[0]), _row(cw_rows[1]), jnp.zeros((SMALL_ROWS - 2, D), F32)], axis=0)
    small = _allreduce_small(small)
    cw_full = small[0:2, :CONVW].reshape(2, 3, 256)
    cw8 = jnp.pad(cw_full, ((0, 0), (0, 5), (0, 0)))
    fb = jnp.pad(f_bias, ((0, 0), (0, GATEW - NH))).reshape(2, 1, GATEW)
    tbl = _dil_table(rel_bias)

    def wcol(K, tn, off):
        return pl.BlockSpec((K, tn), lambda i, j: (0, off + j))

    def wrow(tn, K, blk=0):
        return pl.BlockSpec((tn, K), lambda i, j: (j, blk))

    def arow(tm, K, blk=0):
        return pl.BlockSpec((tm, K), lambda i, j: (i, blk))

    h = x.reshape(T, D)
    hb = h.astype(BF16)
    saved = []
    for l in range(2):
        Win = W[l]["win"]
        qkv = _mm([(hb, arow(1024, D), Win, wcol(D, 768, 0))], nt=False, M=T, N=QKVW, tm=1024, tn=768,
                  out_dtype=BF16, name="proj_qkv")
        conv = _mm([(hb, arow(512, D), Win, wcol(D, 768, 3))], nt=False, M=T, N=CONVW, tm=512, tn=768,
                   out_dtype=F32, name="proj_conv")
        gate = _mm([(hb, arow(512, D), Win, wcol(D, 128, 24))], nt=False, M=T, N=GATEW, tm=512, tn=128,
                   out_dtype=F32, name="proj_gate")
        cum = _fox_prep(gate, fb[l])
        cq = cum[:, :NH].reshape(BL, S, NH).transpose(0, 2, 1).reshape(NSTAT, S)
        ckb = jnp.broadcast_to(cq[:, :, None], (NSTAT, S, BQ))
        if l == 0:
            mixed, rtot, a0 = _sb_fwd(qkv, job=_gather_job_a(sh[0][1:]))
            mixed, lse_d, ex = _flash_fwd(qkv, mixed, 1, False, (tbl,),
                                          job=_join_jobs(_gather_job_b(list(a0)), _gather_job_a(sh[1][:2])))
            W[0].update(zip(("wout", "wgT", "wuT", "wd"), [whole(t) for t in ex[:4]]))
            mixed, lse_f, o_fox, ex = _flash_fwd(qkv, mixed, 2, True, (cq, ckb),
                                                 job=_join_jobs(_gather_job_b(list(ex[4:])), _gather_job_a(sh[1][2:])))
            W[1].update(zip(("win", "wout"), [whole(t) for t in ex[:2]]))
            a2 = list(ex[2:])
        else:
            mixed, rtot, ex = _sb_fwd(qkv, job=_gather_job_b(a2))
            W[1].update(zip(("wgT", "wuT", "wd"), [whole(t) for t in ex]))
            mixed, lse_d, _ = _flash_fwd(qkv, mixed, 1, False, (tbl,))
            mixed, lse_f, o_fox, _ = _flash_fwd(qkv, mixed, 2, True, (cq, ckb))
        Wout, WgT, WuT, Wd = W[l]["wout"], W[l]["wgT"], W[l]["wuT"], W[l]["wd"]
        mixed = _conv_fwd(conv, cw8[l], mixed)
        x1, xh1, r1, x1b = _mm_ln(mixed, Wout, h, ln1_g[l:l + 1], ln1_b[l:l + 1], "out_proj_ln")
        fs, ft, a = _ffn_up(x1b, WgT, WuT)
        x2, xh2, r2, x2b = _mm_ln(a, Wd, x1, ln2_g[l:l + 1], ln2_b[l:l + 1], "ffn_down_ln")
        saved.append(dict(h=hb, qkv=qkv, conv=conv, gate=gate, cq=cq, ckb=ckb, mixed=mixed, rtot=rtot, lse_d=lse_d,
                          lse_f=lse_f, o_fox=o_fox, x1=x1b, xh1=xh1, r1=r1, fs=fs, ft=ft, a=a, xh2=xh2, r2=r2))
        h, hb = x2, x2b

    sq, dy = _loss_grad(h, loss_target.reshape(T, D))
    loss = lax.psum(sq[0, 0], ("x", "y", "c")) * (0.5 / D)

    def view(gr):
        return gr.reshape(4, 2, gr.shape[0] // NDEV, gr.shape[1])

    G = [None, None]
    small_g = {}
    shard_g = {}
    for l in (1, 0):
        sv = saved[l]
        Win, Wout, WgT, WuT, Wd = W[l]["win"], W[l]["wout"], W[l]["wgT"], W[l]["wuT"], W[l]["wd"]
        ds2, dg2, db2, ds2b = _ln_bwd(dy, sv["xh2"], sv["r2"], ln2_g[l:l + 1])
        dgt, dut = _ffn_da(ds2b, Wd, sv["fs"], sv["ft"])
        G_d = _mm_tn(sv["a"], ds2b, None, C=D, Ka=DFF, N=D, tm=1408, tn=1024, tk=1024, ooff=0, name="grad_w_down")
        G_g = _mm_tn(dgt, sv["x1"], None, C=D, Ka=DFF, N=D, tm=1408, tn=1024, tk=1024, ooff=0, name="grad_w_gate")
        G_u = _mm_tn(dut, sv["x1"], None, C=D, Ka=DFF, N=D, tm=1408, tn=1024, tk=1024, ooff=0, name="grad_w_up")
        dx1 = _mm([(dgt, arow(1024, DFF), WgT, wcol(DFF, 512, 0)), (dut, arow(1024, DFF), WuT, wcol(DFF, 512, 0))],
                  nt=False, M=T, N=D, tm=1024, tn=512, out_dtype=F32, name="ffn_dx", res=ds2, res_scale=ALPHA)
        ds1, dg1, db1, ds1b = _ln_bwd(dx1, sv["xh1"], sv["r1"], ln1_g[l:l + 1])
        G_out = _mm_tn(sv["mixed"], ds1b, None, C=D, Ka=D, N=D, tm=1024, tn=1024, tk=1024, ooff=0, name="grad_w_out")
        dmixed = _mm([(ds1b, arow(512, D), Wout, wrow(512, D))], nt=True, M=T, N=D, tm=512, tn=512,
                     out_dtype=BF16, name="out_proj_dx")
        early = [view(t) for t in (G_g, G_u, G_d, G_out)] + ([view(G[1]["in"])] if l == 0 else [])
        dqkv, gots = _sb_bwd(sv["qkv"], dmixed, sv["rtot"], job=_sibling_job(early))
        ps = _pair_sums(early, list(gots), core)
        dqkv, dtbl, pa = _flash_bwd(sv["qkv"], sv["mixed"], dmixed, sv["lse_d"], dqkv, 1, False, (tbl,),
                                    job=_chip_job(ps[:2]))
        dqkv, dck, pb = _flash_bwd(sv["qkv"], sv["o_fox"], dmixed, sv["lse_f"], dqkv, 2, True,
                                   (sv["cq"], sv["ckb"]), job=_chip_job(ps[2:]))
        sums = _chip_sums(list(pa) + list(pb))
        shard_g[l] = dict(zip(("g", "u", "d", "out"), sums[:4]))
        if l == 0:
            shard_g[1]["in"] = sums[4]
        dconv, dcw = _conv_bwd(sv["conv"], cw8[l], dmixed)
        dcum = jnp.pad(dck.reshape(S, BL, NH).transpose(1, 0, 2).reshape(T, NH), ((0, 0), (0, GATEW - NH)))
        dgate, dfb = _fox_post(dcum, sv["gate"], fb[l])
        drb = _dil_table_bwd(dtbl)
        G_in = _mm_tn(sv["h"], dqkv, None, C=NPAD, Ka=D, N=QKVW, tm=1024, tn=768, tk=1024, ooff=0, name="grad_w_in_qkv")
        G_in = _mm_tn(sv["h"], dconv, G_in, C=NPAD, Ka=D, N=CONVW, tm=1024, tn=768, tk=1024, ooff=3,
                      name="grad_w_in_conv")
        G_in = _mm_tn(sv["h"], dgate, G_in, C=NPAD, Ka=D, N=GATEW, tm=1024, tn=128, tk=1024, ooff=24,
                      name="grad_w_in_gate")
        G[l] = {"in": G_in, "out": G_out, "g": G_g, "u": G_u, "d": G_d}
        dy = _mm([(dqkv, arow(1024, QKVW), Win, wrow(512, QKVW, 0)),
                  (dconv, arow(1024, CONVW), Win, wrow(512, CONVW, 3)),
                  (dgate, arow(1024, GATEW), Win, wrow(512, GATEW, 24))],
                 nt=True, M=T, N=D, tm=1024, tn=512, out_dtype=F32, name="proj_dx", res=ds1, res_scale=ALPHA)
        small_g[l] = dict(ln1_g=dg1, ln1_b=db1, ln2_g=dg2, ln2_b=db2, cw=dcw[0:3].reshape(1, CONVW),
                          fb=dfb[:, :NH], rb=drb[:, :NH])
    grad_x = dy.reshape(BL, S, D)

    late = [view(G[0]["in"])]
    gots = list(_run_job(_sibling_job(late), "sibling_exchange"))
    shard_g[0]["in"] = _chip_sums(_run_job(_chip_job(_pair_sums(late, gots, core)), "chip_exchange"))[0]

    rows = []
    for name in ("ln1_g", "ln1_b", "ln2_g", "ln2_b"):
        rows += [small_g[0][name], small_g[1][name]]
    rows += [_row(small_g[0]["cw"]), _row(small_g[1]["cw"]),
             _row(jnp.concatenate([small_g[0]["fb"], small_g[1]["fb"]], axis=0)),
             _row(small_g[0]["rb"] + small_g[1]["rb"])]
    rows.append(jnp.zeros((SMALL_ROWS - len(rows), D), F32))
    sg = _allreduce_small(jnp.concatenate(rows, axis=0))
    g_ln1_g, g_ln1_b, g_ln2_g, g_ln2_b = sg[0:2], sg[2:4], sg[4:6], sg[6:8]
    g_conv_full = sg[8:10, :CONVW].reshape(2, 3, 256)
    g_conv = lax.dynamic_slice(g_conv_full, (0, 0, me * 32), (2, 3, 32))
    g_fb = sg[10, :2 * NH].reshape(2, NH)
    g_rb = sg[11, :32 * NH].reshape(32, NH)

    def both(name):
        return jnp.stack([shard_g[0][name], shard_g[1][name]])

    g_in = both("in")
    g_w_in = jnp.concatenate([_unpermute_in(g_in[..., :QKVW]), g_in[..., QKVW:NPROJ]], axis=-1)
    g_w_out = both("out")
    g_w_gate = jnp.swapaxes(both("g"), 1, 2)
    g_w_up = jnp.swapaxes(both("u"), 1, 2)
    g_w_down = both("d")

    up_in = _adamw(w_in, g_w_in, m_w_in, v_w_in, 64)
    up_out = _adamw(w_out, g_w_out, m_w_out, v_w_out, 128)
    up_gate = _adamw(w_gate, g_w_gate, m_w_gate, v_w_gate, 256)
    up_up = _adamw(w_up, g_w_up, m_w_up, v_w_up, 256)
    up_down = _adamw(w_down, g_w_down, m_w_down, v_w_down, 352)

    def pack(fbv, cwv, rbv, l1g, l1b, l2g, l2b):
        r = [l1g, l1b, l2g, l2b, _row(cwv), _row(fbv), _row(rbv)]
        r.append(jnp.zeros((SMALL_ROWS - 11, D), F32))
        return jnp.concatenate(r, axis=0)

    pw = pack(f_bias, conv_w, rel_bias, ln1_g, ln1_b, ln2_g, ln2_b)
    pg = pack(g_fb, g_conv, g_rb, g_ln1_g, g_ln1_b, g_ln2_g, g_ln2_b)
    pm = pack(m_f_bias, m_conv_w, m_rel_bias, m_ln1_g, m_ln1_b, m_ln2_g, m_ln2_b)
    pv = pack(v_f_bias, v_conv_w, v_rel_bias, v_ln1_g, v_ln1_b, v_ln2_g, v_ln2_b)
    ups = [u[0] for u in _adamw(pw[None], pg[None], pm[None], pv[None], SMALL_ROWS)]

    def unpack(p):
        return dict(ln1_g=p[0:2], ln1_b=p[2:4], ln2_g=p[4:6], ln2_b=p[6:8],
                    conv_w=p[8, :192].reshape(2, 3, 32), f_bias=p[9, :2 * NH].reshape(2, NH),
                    rel_bias=p[10, :32 * NH].reshape(32, NH))

    sm = [unpack(p) for p in ups]

    def group(k):
        return (up_in[k], sm[k]["f_bias"], sm[k]["conv_w"], up_out[k], sm[k]["rel_bias"], sm[k]["ln1_g"],
                sm[k]["ln1_b"], up_gate[k], up_up[k], up_down[k], sm[k]["ln2_g"], sm[k]["ln2_b"])

    grads = (g_w_in, g_fb, g_conv, g_w_out, g_rb, g_ln1_g, g_ln1_b, g_w_gate, g_w_up, g_w_down, g_ln2_g, g_ln2_b)
    return (loss, grad_x) + grads + group(0) + group(1) + group(2)
```

```python
import math

import numpy as np
import jax
import jax.numpy as jnp
from jax import lax
from jax.experimental import pallas as pl
from jax.experimental.pallas import tpu as pltpu

F32 = jnp.float32
BF16 = jnp.bfloat16
MESH = pl.DeviceIdType.MESH

D = 1024
S = 2048
BL = 2
T = BL * S
NH = 4
DFF = 2816
NPROJ = 3076
NPAD = 3200
QKVW = 2304
CONVW = 768
GATEW = 128
PAIRW = 384
BQ = 128
HB = 2 * BQ
NB = S // BQ
NDEV = 8
NSTAT = BL * NH
ALPHA = 4.0 ** 0.25
SCALE = 0.125
NEG = -1e30
LN_EPS = 1e-5
ADAM_LR, ADAM_B1, ADAM_B2, ADAM_EPS, ADAM_WD, ADAM_STEP = 0.001, 0.9, 0.999, 1e-08, 0.01, 10
VMEM_LIMIT = 56 * 1024 * 1024
SMALL_ROWS = 16


def _bucket_thresholds():
    d = np.arange(0, S)
    nf = np.maximum(d, 1).astype(np.float32)
    large = 16 + (np.log(nf / np.float32(16)) / np.float32(math.log(128)) * np.float32(16)).astype(np.int32)
    b = np.where(d < 16, d, np.minimum(large, 31))
    return [int(np.argmax(b >= k)) for k in range(32)]


BUCKET_TH = _bucket_thresholds()


def _cp(sem=None):
    return pltpu.CompilerParams(dimension_semantics=sem, vmem_limit_bytes=VMEM_LIMIT)


def _dot(a, b):
    return lax.dot_general(a, b, (((1,), (0,)), ((), ())), preferred_element_type=F32)


def _dot_nt(a, b):
    return lax.dot_general(a, b, (((1,), (1,)), ((), ())), preferred_element_type=F32)


def _dot_tn(a, b):
    return lax.dot_general(a, b, (((0,), (0,)), ((), ())), preferred_element_type=F32)


def _split2(x):
    hi = x.astype(BF16)
    mid = (x - hi.astype(F32)).astype(BF16)
    return jnp.concatenate([hi, mid], axis=1)


def _split3(x):
    hi = x.astype(BF16)
    r = x - hi.astype(F32)
    mid = r.astype(BF16)
    lo = (r - mid.astype(F32)).astype(BF16)
    return jnp.concatenate([hi, mid, lo], axis=1)


def _log_sigmoid(u):
    return jnp.minimum(u, 0.0) - jnp.log1p(jnp.exp(-jnp.abs(u)))


def _log_sigmoid_tile(u):
    return jnp.minimum(u, 0.0) - jnp.log(1.0 + jnp.exp(jnp.minimum(u, -u)))


def _iota(shape, dim):
    return lax.broadcasted_iota(jnp.int32, shape, dim)


ANY_SPEC = pl.BlockSpec(memory_space=pl.ANY)
VMEM_SPEC = pl.BlockSpec(memory_space=pltpu.VMEM)


def _mm(pairs, *, nt, M, N, tm, tn, out_dtype, name, res=None, res_scale=1.0):
    n = len(pairs)

    def body(*refs):
        acc = None
        for p in range(n):
            a = refs[2 * p][...].astype(BF16)
            b = refs[2 * p + 1][...]
            d = _dot_nt(a, b) if nt else _dot(a, b)
            acc = d if acc is None else acc + d
        if res is not None:
            acc = acc + res_scale * refs[2 * n][...]
        refs[-1][...] = acc.astype(out_dtype)

    ops, specs = [], []
    for a, asp, b, bsp in pairs:
        ops += [a, b]
        specs += [asp, bsp]
    if res is not None:
        ops.append(res)
        specs.append(pl.BlockSpec((tm, tn), lambda i, j: (i, j)))
    return pl.pallas_call(
        body, name=name, grid=(M // tm, N // tn), in_specs=specs,
        out_specs=pl.BlockSpec((tm, tn), lambda i, j: (i, j)),
        out_shape=jax.ShapeDtypeStruct((M, N), out_dtype),
        compiler_params=_cp(("parallel", "parallel")))(*ops)


def _mm_tn(a, b, gbuf, *, C, Ka, N, tm, tn, tk, ooff, name):
    def body(*refs):
        a_ref, b_ref, o_ref = refs[0], refs[1], refs[-1]
        k = pl.program_id(2)
        d = _dot_tn(a_ref[...].astype(BF16), b_ref[...].astype(BF16))

        @pl.when(k == 0)
        def _():
            o_ref[...] = d

        @pl.when(k > 0)
        def _():
            o_ref[...] += d

    ops = [a, b] + ([] if gbuf is None else [gbuf])
    return pl.pallas_call(
        body, name=name, grid=(Ka // tm, N // tn, T // tk),
        in_specs=[pl.BlockSpec((tk, tm), lambda i, j, k: (k, i)),
                  pl.BlockSpec((tk, tn), lambda i, j, k: (k, j))] + ([] if gbuf is None else [ANY_SPEC]),
        out_specs=pl.BlockSpec((tm, tn), lambda i, j, k: (i, ooff + j)),
        out_shape=jax.ShapeDtypeStruct((Ka, C), F32),
        input_output_aliases={} if gbuf is None else {2: 0},
        compiler_params=_cp(("parallel", "parallel", "arbitrary")))(*ops)


def _ffn_up(x1, wgt, wut):
    tm, tn = 1024, 256

    def body(x_ref, wg_ref, wu_ref, g_ref, u_ref, a_ref):
        ch = 256
        for r in range(0, tm, ch):
            xb = x_ref[r:r + ch, :]
            g = _dot_nt(xb, wg_ref[...])
            u = _dot_nt(xb, wu_ref[...])
            g_ref[r:r + ch, :] = g.astype(BF16)
            u_ref[r:r + ch, :] = u.astype(BF16)
            a_ref[r:r + ch, :] = (g * jax.nn.sigmoid(g) * u).astype(BF16)

    wspec = pl.BlockSpec((tn, D), lambda i, j: (j, 0))
    ospec = pl.BlockSpec((tm, tn), lambda i, j: (i, j))
    return pl.pallas_call(
        body, name="ffn_up", grid=(T // tm, DFF // tn),
        in_specs=[pl.BlockSpec((tm, D), lambda i, j: (i, 0)), wspec, wspec],
        out_specs=[ospec, ospec, ospec],
        out_shape=[jax.ShapeDtypeStruct((T, DFF), BF16)] * 3,
        compiler_params=_cp(("parallel", "parallel")))(x1, wgt, wut)


def _ffn_da(dffn, wd, s, t):
    tm, tn = 1024, 256

    def body(d_ref, wd_ref, s_ref, t_ref, dg_ref, du_ref):
        ch = 256
        for r in range(0, tm, ch):
            da = _dot_nt(d_ref[r:r + ch, :], wd_ref[...])
            gv = s_ref[r:r + ch, :].astype(F32)
            sg = jax.nn.sigmoid(gv)
            dg_ref[r:r + ch, :] = (da * t_ref[r:r + ch, :].astype(F32) * (sg * (1.0 + gv * (1.0 - sg)))).astype(BF16)
            du_ref[r:r + ch, :] = (da * (gv * sg)).astype(BF16)

    ospec = pl.BlockSpec((tm, tn), lambda i, j: (i, j))
    return pl.pallas_call(
        body, name="ffn_da", grid=(T // tm, DFF // tn),
        in_specs=[pl.BlockSpec((tm, D), lambda i, j: (i, 0)),
                  pl.BlockSpec((tn, D), lambda i, j: (j, 0)), ospec, ospec],
        out_specs=[ospec, ospec],
        out_shape=[jax.ShapeDtypeStruct((T, DFF), BF16), jax.ShapeDtypeStruct((T, DFF), BF16)],
        compiler_params=_cp(("parallel", "parallel")))(dffn, wd, s, t)


def _ffn_fwd(xb, x, wgt, wut, wd, gam, bet):
    tm, ch = 512, 256

    def body(xb_ref, x_ref, g_ref, b_ref, wg_hbm, wu_hbm, wd_hbm,
             go_ref, uo_ref, ao_ref, y_ref, xh_ref, r_ref, yb_ref, wg_v, wu_v, wd_v, sem):
        @pl.when(pl.program_id(0) == 0)
        def _():
            _copy_in(wg_hbm, wg_v, sem)
            _copy_in(wu_hbm, wu_v, sem)
            _copy_in(wd_hbm, wd_v, sem)

        xv = xb_ref[...]
        s = ALPHA * x_ref[...]
        for c in range(0, DFF, ch):
            gv = _dot_nt(xv, wg_v[c:c + ch, :])
            uv = _dot_nt(xv, wu_v[c:c + ch, :])
            av = (gv * jax.nn.sigmoid(gv) * uv).astype(BF16)
            go_ref[:, c:c + ch] = gv.astype(BF16)
            uo_ref[:, c:c + ch] = uv.astype(BF16)
            ao_ref[:, c:c + ch] = av
            s = s + _dot(av, wd_v[c:c + ch, :])
        mu = jnp.mean(s, axis=-1, keepdims=True)
        xc = s - mu
        var = jnp.mean(xc * xc, axis=-1, keepdims=True)
        r = lax.rsqrt(var + LN_EPS)
        xh = xc * r
        xh_ref[...] = xh.astype(BF16)
        r_ref[...] = r
        y = xh * g_ref[...] + b_ref[...]
        y_ref[...] = y
        yb_ref[...] = y.astype(BF16)

    row = pl.BlockSpec((tm, D), lambda i: (i, 0))
    wide = pl.BlockSpec((tm, DFF), lambda i: (i, 0))
    vec = pl.BlockSpec((1, D), lambda i: (0, 0))
    wsl = pltpu.VMEM((DFF, D), BF16)
    hid = jax.ShapeDtypeStruct((T, DFF), BF16)
    return pl.pallas_call(
        body, name="ffn_fwd", grid=(T // tm,),
        in_specs=[row, row, vec, vec, ANY_SPEC, ANY_SPEC, ANY_SPEC],
        out_specs=[wide, wide, wide, row, row, pl.BlockSpec((tm, 1), lambda i: (i, 0)), row],
        out_shape=[hid, hid, hid, jax.ShapeDtypeStruct((T, D), F32), jax.ShapeDtypeStruct((T, D), BF16),
                   jax.ShapeDtypeStruct((T, 1), F32), jax.ShapeDtypeStruct((T, D), BF16)],
        scratch_shapes=[wsl, wsl, wsl, pltpu.SemaphoreType.DMA],
        compiler_params=_cp(("arbitrary",)))(xb, x, gam, bet, wgt, wut, wd)


def _ffn_bwd(dffn, res, g, u, wd, wgt, wut):
    tm, ch = 512, 256

    def body(d_ref, r_ref, g_ref, u_ref, wd_hbm, wg_hbm, wu_hbm, dg_ref, du_ref, dx_ref, wd_v, wg_v, wu_v, sem):
        @pl.when(pl.program_id(0) == 0)
        def _():
            _copy_in(wd_hbm, wd_v, sem)
            _copy_in(wg_hbm, wg_v, sem)
            _copy_in(wu_hbm, wu_v, sem)

        db = d_ref[...]
        acc = ALPHA * r_ref[...]
        for c in range(0, DFF, ch):
            da = _dot_nt(db, wd_v[c:c + ch, :])
            gv = g_ref[:, c:c + ch].astype(F32)
            sg = jax.nn.sigmoid(gv)
            dg = (da * u_ref[:, c:c + ch].astype(F32) * (sg * (1.0 + gv * (1.0 - sg)))).astype(BF16)
            du = (da * (gv * sg)).astype(BF16)
            dg_ref[:, c:c + ch] = dg
            du_ref[:, c:c + ch] = du
            acc = acc + _dot(dg, wg_v[c:c + ch, :]) + _dot(du, wu_v[c:c + ch, :])
        dx_ref[...] = acc

    row = pl.BlockSpec((tm, D), lambda i: (i, 0))
    wide = pl.BlockSpec((tm, DFF), lambda i: (i, 0))
    wsl = pltpu.VMEM((DFF, D), BF16)
    return pl.pallas_call(
        body, name="ffn_bwd", grid=(T // tm,),
        in_specs=[row, row, wide, wide, ANY_SPEC, ANY_SPEC, ANY_SPEC], out_specs=[wide, wide, row],
        out_shape=[jax.ShapeDtypeStruct((T, DFF), BF16), jax.ShapeDtypeStruct((T, DFF), BF16),
                   jax.ShapeDtypeStruct((T, D), F32)],
        scratch_shapes=[wsl, wsl, wsl, pltpu.SemaphoreType.DMA],
        compiler_params=_cp(("arbitrary",)))(dffn, res, g, u, wd, wgt, wut)


def _mm_ln(a, w, x, gam, bet, name):
    tm = 256
    K = a.shape[1]

    def body(a_ref, w_ref, x_ref, g_ref, b_ref, y_ref, xh_ref, r_ref, yb_ref):
        s = ALPHA * x_ref[...] + _dot(a_ref[...], w_ref[...])
        mu = jnp.mean(s, axis=-1, keepdims=True)
        xc = s - mu
        var = jnp.mean(xc * xc, axis=-1, keepdims=True)
        r = lax.rsqrt(var + LN_EPS)
        xh = xc * r
        xh_ref[...] = xh.astype(BF16)
        r_ref[...] = r
        y = xh * g_ref[...] + b_ref[...]
        y_ref[...] = y
        yb_ref[...] = y.astype(BF16)

    row = pl.BlockSpec((tm, D), lambda i: (i, 0))
    vec = pl.BlockSpec((1, D), lambda i: (0, 0))
    return pl.pallas_call(
        body, name=name, grid=(T // tm,),
        in_specs=[pl.BlockSpec((tm, K), lambda i: (i, 0)), pl.BlockSpec((K, D), lambda i: (0, 0)), row, vec, vec],
        out_specs=[row, row, pl.BlockSpec((tm, 1), lambda i: (i, 0)), row],
        out_shape=[jax.ShapeDtypeStruct((T, D), F32), jax.ShapeDtypeStruct((T, D), BF16),
                   jax.ShapeDtypeStruct((T, 1), F32), jax.ShapeDtypeStruct((T, D), BF16)],
        compiler_params=_cp(("parallel",)))(a, w, x, gam, bet)


def _ln_bwd(dy, xh, r, gam):
    tm = 256

    def body(dy_ref, xh_ref, r_ref, g_ref, ds_ref, dg_ref, db_ref, dsb_ref):
        i = pl.program_id(0)
        dyv = dy_ref[...]
        xhv = xh_ref[...].astype(F32)
        dxh = dyv * g_ref[...]
        m1 = jnp.mean(dxh, axis=-1, keepdims=True)
        m2 = jnp.mean(dxh * xhv, axis=-1, keepdims=True)
        ds = r_ref[...] * (dxh - m1 - xhv * m2)
        ds_ref[...] = ds
        dsb_ref[...] = ds.astype(BF16)
        pg = jnp.sum(dyv * xhv, axis=0, keepdims=True)
        pb = jnp.sum(dyv, axis=0, keepdims=True)

        @pl.when(i == 0)
        def _():
            dg_ref[...] = pg
            db_ref[...] = pb

        @pl.when(i > 0)
        def _():
            dg_ref[...] += pg
            db_ref[...] += pb

    row = pl.BlockSpec((tm, D), lambda i: (i, 0))
    vec = pl.BlockSpec((1, D), lambda i: (0, 0))
    return pl.pallas_call(
        body, name="ln_bwd", grid=(T // tm,),
        in_specs=[row, row, pl.BlockSpec((tm, 1), lambda i: (i, 0)), vec],
        out_specs=[row, vec, vec, row],
        out_shape=[jax.ShapeDtypeStruct((T, D), F32), jax.ShapeDtypeStruct((1, D), F32),
                   jax.ShapeDtypeStruct((1, D), F32), jax.ShapeDtypeStruct((T, D), BF16)],
        compiler_params=_cp(("arbitrary",)))(dy, xh, r, gam)


def _loss_grad(y, tgt):
    tm = 256

    def body(y_ref, t_ref, l_ref, dy_ref):
        i = pl.program_id(0)
        e = y_ref[...] - t_ref[...]
        dy_ref[...] = e * (1.0 / D)
        p = jnp.sum(jnp.sum(e * e, axis=1, keepdims=True), axis=0, keepdims=True)

        @pl.when(i == 0)
        def _():
            l_ref[...] = p

        @pl.when(i > 0)
        def _():
            l_ref[...] += p

    row = pl.BlockSpec((tm, D), lambda i: (i, 0))
    return pl.pallas_call(
        body, name="loss_grad", grid=(T // tm,), in_specs=[row, row],
        out_specs=[pl.BlockSpec((1, 1), lambda i: (0, 0)), row],
        out_shape=[jax.ShapeDtypeStruct((1, 1), F32), jax.ShapeDtypeStruct((T, D), F32)],
        compiler_params=_cp(("arbitrary",)))(y, tgt)


def _adamw(w, g, m, v, tr):
    L, R, C = w.shape

    def body(w_ref, g_ref, m_ref, v_ref, d_ref, m2_ref, v2_ref):
        gv = g_ref[...]
        m2 = ADAM_B1 * m_ref[...] + (1.0 - ADAM_B1) * gv
        v2 = ADAM_B2 * v_ref[...] + (1.0 - ADAM_B2) * (gv * gv)
        m_hat = m2 / (1.0 - ADAM_B1 ** ADAM_STEP)
        v_hat = v2 / (1.0 - ADAM_B2 ** ADAM_STEP)
        d_ref[...] = -ADAM_LR * (m_hat / (jnp.sqrt(v_hat) + ADAM_EPS) + ADAM_WD * w_ref[...])
        m2_ref[...] = m2
        v2_ref[...] = v2

    blk = pl.BlockSpec((None, tr, C), lambda l, i: (l, i, 0))
    sh = jax.ShapeDtypeStruct((L, R, C), F32)
    return pl.pallas_call(
        body, name="adamw", grid=(L, R // tr), in_specs=[blk] * 4, out_specs=[blk] * 3,
        out_shape=[sh, sh, sh], compiler_params=_cp(("parallel", "parallel")))(w, g, m, v)


class _Job:
    def __init__(self, ins, out_shapes, aliases, sems, start, finish):
        self.ins, self.out_shapes, self.aliases, self.sems = list(ins), list(out_shapes), dict(aliases), list(sems)
        self.start, self.finish = start, finish


def _host_call(body, name, ins, in_specs, out_shapes, out_specs, scratch, aliases, job):
    n_in, n_out, n_scr = len(ins), len(out_shapes), len(scratch)
    jins = job.ins if job else []
    jouts = job.out_shapes if job else []
    jsems = job.sems if job else []

    def wrapped(*refs):
        a = n_in
        b = a + len(jins)
        c = b + n_out
        d = c + len(jouts)
        e = d + n_scr
        comm = None
        if job:
            jrefs = (refs[a:b], refs[c:d], refs[e:])
            comm = (lambda: job.start(*jrefs), lambda st: job.finish(st, *jrefs))
        body(refs[:a], refs[b:c], refs[d:e], comm)

    al = dict(aliases)
    if job:
        for ji, jo in job.aliases.items():
            al[n_in + ji] = n_out + jo
    res = pl.pallas_call(
        wrapped, name=name, in_specs=list(in_specs) + [ANY_SPEC] * len(jins),
        out_specs=list(out_specs) + [ANY_SPEC] * len(jouts), out_shape=list(out_shapes) + list(jouts),
        scratch_shapes=list(scratch) + list(jsems), input_output_aliases=al,
        compiler_params=_cp())(*ins, *jins)
    return res[:n_out], res[n_out:]


def _copy_in(src, dst, sem):
    cp = pltpu.make_async_copy(src, dst, sem)
    cp.start()
    cp.wait()


CHAINS = [(p, b) for p in range(2) for b in range(BL)]
NC = len(CHAINS)
ROWS_SHAPE = jax.ShapeDtypeStruct((NSTAT, S), F32)
SLAB_QKV = pltpu.VMEM((T, 2 * PAIRW), BF16)
SLAB_OUT = pltpu.VMEM((T, 2 * BQ), BF16)
SLAB_O32 = pltpu.VMEM((T, 2 * BQ), F32)
SLAB_T = pltpu.VMEM((2, BQ, T), BF16)
SLAB_KEYB = pltpu.VMEM((NSTAT, S, BQ), F32)
ACC_KV = pltpu.VMEM((2, T, BQ), F32)


def _lane_masks():
    lane = _iota((1, BQ), 1)
    m0 = (lane < 64).astype(BF16)
    return m0, 1.0 - m0


def _row_masks():
    r = _iota((BQ, 1), 0)
    m0 = (r < 64).astype(BF16)
    return m0, 1.0 - m0


def _stack(x, m0, m1):
    return jnp.concatenate([x * m0, x * m1], axis=0)


def _stack_t(xt, r0, r1):
    return jnp.concatenate([xt * r0, xt * r1], axis=1)


def _tr(x):
    return x.T


def _rows(b, i):
    return pl.ds(pl.multiple_of(b * S + i * BQ, BQ), BQ)


def _transpose_slab(src, dst, col0):
    def blk(n, _):
        r = pl.ds(pl.multiple_of(n * BQ, BQ), BQ)
        for p in range(2):
            dst[p, :, r] = _tr(src[r, col0(p):col0(p) + BQ])
        return 0

    lax.fori_loop(0, T // BQ, blk, 0)


def _heads(x):
    return x[:BQ], x[BQ:]


def _bcast_heads(r0, r1):
    return jnp.concatenate([jnp.broadcast_to(r0, (BQ, BQ)), jnp.broadcast_to(r1, (BQ, BQ))], axis=0)


def _by_channel(r0, r1):
    return jnp.where(_iota((BQ, BQ), 0) < 64, r0, r1)


def _colsum2(x):
    return jnp.sum(x[:BQ], axis=0, keepdims=True), jnp.sum(x[BQ:], axis=0, keepdims=True)


def _stat_row(ref, p, b, h, i):
    c = b * NH + 2 * p + h
    return ref[c:c + 1, pl.ds(pl.multiple_of(i * BQ, BQ), BQ)]


def _put_row(ref, p, b, h, i, v):
    c = b * NH + 2 * p + h
    ref[c:c + 1, pl.ds(pl.multiple_of(i * BQ, BQ), BQ)] = v


def _valid_t(strict):
    r = _iota((HB, BQ), 0) & (BQ - 1)
    c = _iota((HB, BQ), 1)
    return (r < c) if strict else (r <= c)


def _tri_blockdiag(later):
    r = _iota((HB, HB), 0)
    c = _iota((HB, HB), 1)
    same = (r >= BQ) == (c >= BQ)
    return (same & ((c > r) if later else (c < r))).astype(BF16)


def _cum_mm(tri, x):
    y = _dot(tri, _split2(x))
    return y[:, :BQ] + y[:, BQ:]


def _kv_tiles(qkv_v, p, b, j):
    r = _rows(b, j)
    return qkv_v[r, p * PAIRW + BQ:p * PAIRW + 2 * BQ], qkv_v[r, p * PAIRW + 2 * BQ:p * PAIRW + 3 * BQ]


def _q_tile(qkv_v, p, b, i):
    return qkv_v[_rows(b, i), p * PAIRW:p * PAIRW + BQ] * SCALE


def _sb_fwd(qkv, job=None):
    def body(ins, outs, scr, comm):
        (qkv_hbm,), (o_hbm, r_ref), (qkv_v, o_v, sem, vt_v) = ins, outs, scr
        _copy_in(qkv_hbm.at[:, pl.ds(0, 2 * PAIRW)], qkv_v, sem)
        st = comm[0]() if comm else None
        _transpose_slab(qkv_v, vt_v, lambda p: p * PAIRW + 2 * BQ)
        m0, m1 = _lane_masks()
        r0, r1 = _row_masks()
        valid = _valid_t(True)
        later = _tri_blockdiag(True)

        def steps(qts, i, j, cs, diag):
            ks = [_stack(_kv_tiles(qkv_v, p, b, j)[0], m0, m1) for p, b in CHAINS]
            zs = [_dot(ks[c], qts[c]) for c in range(NC)]
            lbs, lrs = [], []
            for c in range(NC):
                lb = _log_sigmoid_tile(zs[c])
                lr = lb - zs[c]
                if diag:
                    lr = jnp.where(valid, lr, 0.0)
                lbs.append(lb)
                lrs.append(lr)
            tails = [_cum_mm(later, lrs[c]) for c in range(NC)]
            avs = []
            for c in range(NC):
                a = jnp.exp(lbs[c] + tails[c] + _bcast_heads(*cs[c][0]))
                if diag:
                    a = jnp.where(valid, a, 0.0)
                avs.append(a.astype(BF16))
            out = []
            for c, (p, b) in enumerate(CHAINS):
                vts = _stack_t(vt_v[p, :, _rows(b, j)], r0, r1)
                s0, s1 = _colsum2(lrs[c])
                out.append(((cs[c][0][0] + s0, cs[c][0][1] + s1), cs[c][1] + _dot(vts, avs[c])))
            return tuple(out)

        def qblock(i, _):
            qts = [_tr(_q_tile(qkv_v, p, b, i)) for p, b in CHAINS]
            zr = jnp.zeros((1, BQ), F32)
            cs = steps(qts, i, i, (((zr, zr), jnp.zeros((BQ, BQ), F32)),) * NC, True)
            cs = lax.fori_loop(1, i + 1, lambda jj, cs: steps(qts, i, i - jj, cs, False), cs)
            for c, (p, b) in enumerate(CHAINS):
                o_v[_rows(b, i), p * BQ:(p + 1) * BQ] = cs[c][1].T.astype(BF16)
                for h in range(2):
                    _put_row(r_ref, p, b, h, i, cs[c][0][h])
            return 0

        lax.fori_loop(0, NB, qblock, 0)
        _copy_in(o_v, o_hbm.at[:, pl.ds(0, 2 * BQ)], sem)
        if comm:
            comm[1](st)

    (mixed, rtot), extra = _host_call(
        body, "sb_fwd", [qkv], [ANY_SPEC], [jax.ShapeDtypeStruct((T, D), BF16), ROWS_SHAPE], [ANY_SPEC, VMEM_SPEC],
        [SLAB_QKV, SLAB_OUT, pltpu.SemaphoreType.DMA, SLAB_T], {}, job)
    return mixed, rtot, extra


def _sb_bwd(qkv, dmixed, rtot, job=None):
    def body(ins, outs, scr, comm):
        (qkv_hbm, do_hbm, r_ref), (dqkv_hbm,), (qkv_v, do_v, dq_v, dk_s, dv_s, sem, kt_v) = ins, outs, scr
        _copy_in(qkv_hbm.at[:, pl.ds(0, 2 * PAIRW)], qkv_v, sem)
        _copy_in(do_hbm.at[:, pl.ds(0, 2 * BQ)], do_v, sem)
        st = comm[0]() if comm else None
        _transpose_slab(qkv_v, kt_v, lambda p: p * PAIRW + BQ)
        m0, m1 = _lane_masks()
        f0, f1 = m0.astype(F32), m1.astype(F32)
        r0, r1 = _row_masks()
        valid = _valid_t(True)
        later = _tri_blockdiag(True)
        earlier = _tri_blockdiag(False)
        dk_s[...] = jnp.zeros_like(dk_s)
        dv_s[...] = jnp.zeros_like(dv_s)

        def steps(qns, qts, dns, dts, rts, i, j, cs, diag):
            kv = [_kv_tiles(qkv_v, p, b, j) for p, b in CHAINS]
            ks = [_stack(kv[c][0], m0, m1) for c in range(NC)]
            vs = [_stack(kv[c][1], m0, m1) for c in range(NC)]
            zs = [_dot(ks[c], qts[c]) for c in range(NC)]
            das = [_dot(vs[c], dts[c]) for c in range(NC)]
            lbs, lrs, pls = [], [], []
            for c in range(NC):
                lb = _log_sigmoid_tile(zs[c])
                lr = lb - zs[c]
                if diag:
                    lr = jnp.where(valid, lr, 0.0)
                s0, s1 = _colsum2(lr)
                lbs.append(lb)
                lrs.append(lr)
                pls.append((cs[c][0][0] + s0, cs[c][0][1] + s1))
            tails = [_cum_mm(later, lrs[c]) for c in range(NC)]
            avs, gms = [], []
            for c in range(NC):
                a = jnp.exp(lbs[c] + tails[c] + _bcast_heads(rts[c][0] - pls[c][0], rts[c][1] - pls[c][1]))
                if diag:
                    a = jnp.where(valid, a, 0.0)
                avs.append(a)
                gms.append(das[c] * a)
            befores = [_cum_mm(earlier, gms[c]) for c in range(NC)]
            dzbs = []
            for c in range(NC):
                beta = jnp.exp(lbs[c])
                dz = gms[c] - beta * (gms[c] + befores[c] + _bcast_heads(*cs[c][1]))
                if diag:
                    dz = jnp.where(valid, dz, 0.0)
                dzbs.append(dz.astype(BF16))
            out = []
            for c, (p, b) in enumerate(CHAINS):
                dq = cs[c][2] + _dot(_stack_t(kt_v[p, :, _rows(b, j)], r0, r1), dzbs[c])
                dk = _dot(dzbs[c], qns[c])
                dv = _dot(avs[c].astype(BF16), dns[c])
                dk_s[p, _rows(b, j), :] += dk[:BQ] * f0 + dk[BQ:] * f1
                dv_s[p, _rows(b, j), :] += dv[:BQ] * f0 + dv[BQ:] * f1
                g0, g1 = _colsum2(gms[c])
                out.append((pls[c], (cs[c][1][0] + g0, cs[c][1][1] + g1), dq))
            return tuple(out)

        def qblock(i, _):
            qns = [_q_tile(qkv_v, p, b, i) for p, b in CHAINS]
            dns = [do_v[_rows(b, i), p * BQ:(p + 1) * BQ] for p, b in CHAINS]
            qts = [_tr(t) for t in qns]
            dts = [_tr(t) for t in dns]
            rts = [(_stat_row(r_ref, p, b, 0, i), _stat_row(r_ref, p, b, 1, i)) for p, b in CHAINS]
            zr = jnp.zeros((1, BQ), F32)
            cs = (((zr, zr), (zr, zr), jnp.zeros((BQ, BQ), F32)),) * NC
            cs = lax.fori_loop(0, i, lambda j, cs: steps(qns, qts, dns, dts, rts, i, j, cs, False), cs)
            cs = steps(qns, qts, dns, dts, rts, i, i, cs, True)
            for c, (p, b) in enumerate(CHAINS):
                dq_v[_rows(b, i), p * PAIRW:p * PAIRW + BQ] = (cs[c][2].T * SCALE).astype(BF16)
            return 0

        lax.fori_loop(0, NB, qblock, 0)
        for p in range(2):
            dq_v[:, p * PAIRW + BQ:p * PAIRW + 2 * BQ] = dk_s[p].astype(BF16)
            dq_v[:, p * PAIRW + 2 * BQ:p * PAIRW + 3 * BQ] = dv_s[p].astype(BF16)
        _copy_in(dq_v, dqkv_hbm.at[:, pl.ds(0, 2 * PAIRW)], sem)
        if comm:
            comm[1](st)

    (dqkv,), extra = _host_call(
        body, "sb_bwd", [qkv, dmixed, rtot], [ANY_SPEC, ANY_SPEC, VMEM_SPEC],
        [jax.ShapeDtypeStruct((T, QKVW), BF16)], [ANY_SPEC],
        [SLAB_QKV, SLAB_OUT, SLAB_QKV, ACC_KV, ACC_KV, pltpu.SemaphoreType.DMA, SLAB_T], {}, job)
    return dqkv, extra


def _flash_fwd(qkv, mixed, g, fox, bias, job=None):
    def body(ins, outs, scr, comm):
        if fox:
            qkv_hbm, cq_ref, ckb_hbm, _ = ins
            (o_hbm, lse_ref, o32_hbm), (qkv_v, o_v, sem, vt_v, o32_v, ckb_v) = outs, scr
        else:
            qkv_hbm, tbl_ref, _ = ins
            (o_hbm, lse_ref), (qkv_v, o_v, sem, vt_v) = outs, scr
        _copy_in(qkv_hbm.at[:, pl.ds(g * 2 * PAIRW, 2 * PAIRW)], qkv_v, sem)
        if fox:
            _copy_in(ckb_hbm, ckb_v, sem)
        st = comm[0]() if comm else None
        _transpose_slab(qkv_v, vt_v, lambda p: p * PAIRW + 2 * BQ)
        m0, m1 = _lane_masks()
        r0, r1 = _row_masks()
        valid = _valid_t(False)

        def steps(qts, cqs, i, j, cs, diag):
            ks = [_stack(_kv_tiles(qkv_v, p, b, j)[0], m0, m1) for p, b in CHAINS]
            zs = [_dot(ks[c], qts[c]) for c in range(NC)]
            prs, alphas, out = [], [], []
            for c, (p, b) in enumerate(CHAINS):
                (ma, mb), (la, lb_), _ = cs[c]
                if fox:
                    kk = pl.ds(pl.multiple_of(j * BQ, BQ), BQ)
                    col = b * NH + 2 * p
                    z = zs[c] + (cqs[c] - jnp.concatenate([ckb_v[col, kk, :], ckb_v[col + 1, kk, :]], axis=0))
                    if diag:
                        z = jnp.where(valid, z, NEG)
                else:
                    z = zs[c] + tbl_ref[p, i - j]
                za, zb = _heads(z)
                na = jnp.maximum(ma, jnp.max(za, axis=0, keepdims=True))
                nb = jnp.maximum(mb, jnp.max(zb, axis=0, keepdims=True))
                aa, ab = jnp.exp(ma - na), jnp.exp(mb - nb)
                pr = jnp.exp(z - _bcast_heads(na, nb))
                sa, sb = _colsum2(pr)
                prs.append(_split2(pr) if fox else pr.astype(BF16))
                alphas.append((aa, ab))
                out.append(((na, nb), (aa * la + sa, ab * lb_ + sb)))
            pvs = []
            for c, (p, b) in enumerate(CHAINS):
                vts = _stack_t(vt_v[p, :, _rows(b, j)], r0, r1)
                if fox:
                    pvs.append(_dot(vts, prs[c][:, :BQ]) + _dot(vts, prs[c][:, BQ:]))
                else:
                    pvs.append(_dot(vts, prs[c]))
            return tuple((out[c][0], out[c][1], _by_channel(*alphas[c]) * cs[c][2] + pvs[c]) for c in range(NC))

        def qblock(i, _):
            qts = [_tr(_q_tile(qkv_v, p, b, i)) for p, b in CHAINS]
            if fox:
                cqs = [_bcast_heads(_stat_row(cq_ref, p, b, 0, i), _stat_row(cq_ref, p, b, 1, i)) for p, b in CHAINS]
            else:
                cqs = [None] * NC
            ng = jnp.full((1, BQ), NEG, F32)
            zr = jnp.zeros((1, BQ), F32)
            cs = steps(qts, cqs, i, i, (((ng, ng), (zr, zr), jnp.zeros((BQ, BQ), F32)),) * NC, True)
            cs = lax.fori_loop(1, i + 1, lambda jj, cs: steps(qts, cqs, i, i - jj, cs, False), cs)
            for c, (p, b) in enumerate(CHAINS):
                (ma, mb), (la, lb_), acc = cs[c]
                o = (acc / _by_channel(la, lb_)).T
                o_v[_rows(b, i), p * BQ:(p + 1) * BQ] = o.astype(BF16)
                if fox:
                    o32_v[_rows(b, i), p * BQ:(p + 1) * BQ] = o
                _put_row(lse_ref, p, b, 0, i, ma + jnp.log(la))
                _put_row(lse_ref, p, b, 1, i, mb + jnp.log(lb_))
            return 0

        lax.fori_loop(0, NB, qblock, 0)
        _copy_in(o_v, o_hbm.at[:, pl.ds(g * 2 * BQ, 2 * BQ)], sem)
        if fox:
            _copy_in(o32_v, o32_hbm, sem)
        if comm:
            comm[1](st)

    bias_specs = [VMEM_SPEC, ANY_SPEC] if fox else [VMEM_SPEC]
    n_in = 2 + len(bias_specs)
    o32 = [jax.ShapeDtypeStruct((T, 2 * BQ), F32)] if fox else []
    res, extra = _host_call(
        body, "fox_fwd" if fox else "dil_fwd", [qkv, *bias, mixed], [ANY_SPEC] + bias_specs + [ANY_SPEC],
        [jax.ShapeDtypeStruct((T, D), BF16), ROWS_SHAPE] + o32, [ANY_SPEC, VMEM_SPEC] + [ANY_SPEC] * len(o32),
        [SLAB_QKV, SLAB_OUT, pltpu.SemaphoreType.DMA, SLAB_T] + ([SLAB_O32, SLAB_KEYB] if fox else []),
        {n_in - 1: 0}, job)
    return (*res, extra)


def _flash_bwd(qkv, o, dmixed, lse, dqkv, g, fox, bias, job=None):
    def body(ins, outs, scr, comm):
        if fox:
            qkv_hbm, o_hbm, do_hbm, lse_ref, cq_ref, ckb_hbm, _ = ins
            (dqkv_hbm, db_ref), (qkv_v, o_v, do_v, dq_v, dk_s, dv_s, sem, kt_v, ckb_v, dc_s) = outs, scr
        else:
            qkv_hbm, o_hbm, do_hbm, lse_ref, tbl_ref, _ = ins
            (dqkv_hbm, db_ref), (qkv_v, o_v, do_v, dq_v, dk_s, dv_s, sem, kt_v) = outs, scr
        _copy_in(qkv_hbm.at[:, pl.ds(g * 2 * PAIRW, 2 * PAIRW)], qkv_v, sem)
        _copy_in(do_hbm.at[:, pl.ds(g * 2 * BQ, 2 * BQ)], do_v, sem)
        if fox:
            _copy_in(o_hbm, o_v, sem)
            _copy_in(ckb_hbm, ckb_v, sem)
        else:
            _copy_in(o_hbm.at[:, pl.ds(g * 2 * BQ, 2 * BQ)], o_v, sem)
        st = comm[0]() if comm else None
        _transpose_slab(qkv_v, kt_v, lambda p: p * PAIRW + BQ)
        m0, m1 = _lane_masks()
        f0, f1 = m0.astype(F32), m1.astype(F32)
        r0, r1 = _row_masks()
        valid = _valid_t(False)
        dk_s[...] = jnp.zeros_like(dk_s)
        dv_s[...] = jnp.zeros_like(dv_s)
        if fox:
            dc_s[...] = jnp.zeros_like(dc_s)
        else:
            db_ref[...] = jnp.zeros_like(db_ref)

        def steps(qns, qts, dns, dts, cqs, lses, deltas, i, j, dqs, diag):
            kv = [_kv_tiles(qkv_v, p, b, j) for p, b in CHAINS]
            ks = [_stack(kv[c][0], m0, m1) for c in range(NC)]
            vs = [_stack(kv[c][1], m0, m1) for c in range(NC)]
            zs = [_dot(ks[c], qts[c]) for c in range(NC)]
            dps = [_dot(vs[c], dts[c]) for c in range(NC)]
            prs, dzl = [], []
            for c, (p, b) in enumerate(CHAINS):
                if fox:
                    kk = pl.ds(pl.multiple_of(j * BQ, BQ), BQ)
                    col = b * NH + 2 * p
                    z = zs[c] + (cqs[c] - jnp.concatenate([ckb_v[col, kk, :], ckb_v[col + 1, kk, :]], axis=0))
                    if diag:
                        z = jnp.where(valid, z, NEG)
                else:
                    z = zs[c] + tbl_ref[p, i - j]
                pr = jnp.exp(z - lses[c])
                prs.append(pr.astype(BF16))
                dzl.append(pr * (dps[c] - deltas[c]))
            dzbs = [dz.astype(BF16) for dz in dzl]
            new = []
            for c, (p, b) in enumerate(CHAINS):
                new.append(dqs[c] + _dot(_stack_t(kt_v[p, :, _rows(b, j)], r0, r1), dzbs[c]))
                dk = _dot(dzbs[c], qns[c])
                dv = _dot(prs[c], dns[c])
                dk_s[p, _rows(b, j), :] += dk[:BQ] * f0 + dk[BQ:] * f1
                dv_s[p, _rows(b, j), :] += dv[:BQ] * f0 + dv[BQ:] * f1
                if fox:
                    dc_s[c, pl.ds(pl.multiple_of(j * HB, HB), HB), :] += dzl[c]
            if not fox:
                for p in range(2):
                    db_ref[p, i - j] = db_ref[p, i - j] + (dzl[2 * p] + dzl[2 * p + 1])
            return tuple(new)

        def qblock(i, _):
            qns = [_q_tile(qkv_v, p, b, i) for p, b in CHAINS]
            dns = [do_v[_rows(b, i), p * BQ:(p + 1) * BQ] for p, b in CHAINS]
            qts = [_tr(t) for t in qns]
            dts = [_tr(t) for t in dns]
            lses = [_bcast_heads(_stat_row(lse_ref, p, b, 0, i), _stat_row(lse_ref, p, b, 1, i)) for p, b in CHAINS]
            if fox:
                cqs = [_bcast_heads(_stat_row(cq_ref, p, b, 0, i), _stat_row(cq_ref, p, b, 1, i)) for p, b in CHAINS]
            else:
                cqs = [None] * NC
            deltas = []
            for c, (p, b) in enumerate(CHAINS):
                pt = (dns[c].astype(F32) * o_v[_rows(b, i), p * BQ:(p + 1) * BQ].astype(F32)).T
                deltas.append(_bcast_heads(jnp.sum(pt[:64], axis=0, keepdims=True), jnp.sum(pt[64:], axis=0, keepdims=True)))
            dqs = (jnp.zeros((BQ, BQ), F32),) * NC
            dqs = lax.fori_loop(0, i, lambda j, d: steps(qns, qts, dns, dts, cqs, lses, deltas, i, j, d, False), dqs)
            dqs = steps(qns, qts, dns, dts, cqs, lses, deltas, i, i, dqs, True)
            for c, (p, b) in enumerate(CHAINS):
                dq_v[_rows(b, i), p * PAIRW:p * PAIRW + BQ] = (dqs[c].T * SCALE).astype(BF16)
            return 0

        lax.fori_loop(0, NB, qblock, 0)
        for p in range(2):
            dq_v[:, p * PAIRW + BQ:p * PAIRW + 2 * BQ] = dk_s[p].astype(BF16)
            dq_v[:, p * PAIRW + 2 * BQ:p * PAIRW + 3 * BQ] = dv_s[p].astype(BF16)
        _copy_in(dq_v, dqkv_hbm.at[:, pl.ds(g * 2 * PAIRW, 2 * PAIRW)], sem)
        if fox:
            lane = _iota((BQ, NSTAT), 1)

            def fold(n, _):
                t = jnp.zeros((BQ, NSTAT), F32)
                for c, (p, b) in enumerate(CHAINS):
                    s = jnp.sum(dc_s[c, pl.ds(pl.multiple_of(n * HB, HB), HB), :], axis=1, keepdims=True)
                    col = b * NH + 2 * p
                    t = t - jnp.where(lane == col, s[:BQ], 0.0) - jnp.where(lane == col + 1, s[BQ:], 0.0)
                db_ref[pl.ds(pl.multiple_of(n * BQ, BQ), BQ), :] = t
                return 0

            lax.fori_loop(0, NB, fold, 0)
        if comm:
            comm[1](st)

    if fox:
        bias_specs = [VMEM_SPEC, ANY_SPEC]
        db_shape = jax.ShapeDtypeStruct((S, NSTAT), F32)
        more = [SLAB_KEYB, pltpu.VMEM((NC, NB * HB, BQ), F32)]
    else:
        bias_specs = [VMEM_SPEC]
        db_shape = jax.ShapeDtypeStruct((2, NB, HB, BQ), F32)
        more = []
    n_in = 5 + len(bias_specs)
    (dqkv, db), extra = _host_call(
        body, "fox_bwd" if fox else "dil_bwd", [qkv, o, dmixed, lse, *bias, dqkv],
        [ANY_SPEC, ANY_SPEC, ANY_SPEC, VMEM_SPEC] + bias_specs + [ANY_SPEC],
        [jax.ShapeDtypeStruct((T, QKVW), BF16), db_shape], [ANY_SPEC, VMEM_SPEC],
        [SLAB_QKV, SLAB_O32 if fox else SLAB_OUT, SLAB_OUT, SLAB_QKV, ACC_KV, ACC_KV, pltpu.SemaphoreType.DMA, SLAB_T]
        + more, {n_in - 1: 0}, job)
    return dqkv, db, extra


def _delta_t(d):
    return d * BQ + _iota((HB, BQ), 1) - (_iota((HB, BQ), 0) & (BQ - 1))


def _buckets_in(d):
    lo, hi = max(d * BQ - (BQ - 1), 0), d * BQ + BQ - 1
    return [b for b in range(32) if BUCKET_TH[b] <= hi and (b == 31 or BUCKET_TH[b + 1] > lo)]


def _in_bucket(delta, b):
    m = delta >= BUCKET_TH[b]
    return m if b == 31 else m & (delta < BUCKET_TH[b + 1])


def _dil_table(rel_bias):
    def body(rb_ref, o_ref):
        for d in range(NB):
            delta = _delta_t(d)
            pos = delta >= 0
            n = ((pos & (delta <= 128)).astype(jnp.int32)
                 + (pos & (delta <= 512) & ((delta & 3) == 0)).astype(jnp.int32)
                 + (pos & ((delta & 15) == 0)).astype(jnp.int32))
            logn = jnp.where(n == 3, math.log(3.0), jnp.where(n == 2, math.log(2.0), jnp.where(n == 1, 0.0, NEG)))
            head1 = _iota((HB, BQ), 0) >= BQ
            for p in range(2):
                val = jnp.zeros((HB, BQ), F32)
                for b in _buckets_in(d):
                    val = jnp.where(_in_bucket(delta, b), jnp.where(head1, rb_ref[b, 2 * p + 1], rb_ref[b, 2 * p]), val)
                o_ref[p, d] = val + logn

    return pl.pallas_call(
        body, name="dil_table", in_specs=[pl.BlockSpec(memory_space=pltpu.SMEM)], out_specs=VMEM_SPEC,
        out_shape=jax.ShapeDtypeStruct((2, NB, HB, BQ), F32), compiler_params=_cp())(rel_bias)


def _dil_table_bwd(dtbl):
    def body(dt_ref, o_ref):
        p = pl.program_id(0)
        rowi = _iota((32, BQ), 0)
        lanei = _iota((32, BQ), 1)

        @pl.when(p == 0)
        def _():
            o_ref[...] = jnp.zeros_like(o_ref)

        out = jnp.zeros((32, BQ), F32)
        for b in range(32):
            acc = None
            for d in range(NB):
                if b in _buckets_in(d):
                    t = jnp.where(_in_bucket(_delta_t(d), b), dt_ref[d], 0.0)
                    acc = t if acc is None else acc + t
            rs = jnp.sum(acc, axis=1, keepdims=True)
            s0 = jnp.sum(rs[:BQ], axis=0, keepdims=True)
            s1 = jnp.sum(rs[BQ:], axis=0, keepdims=True)
            out = (out + jnp.where((rowi == b) & (lanei == 2 * p), s0, 0.0)
                   + jnp.where((rowi == b) & (lanei == 2 * p + 1), s1, 0.0))
        o_ref[...] += out

    return pl.pallas_call(
        body, name="dil_table_bwd", grid=(2,),
        in_specs=[pl.BlockSpec((None, NB, HB, BQ), lambda p: (p, 0, 0, 0))],
        out_specs=pl.BlockSpec((32, BQ), lambda p: (0, 0)),
        out_shape=jax.ShapeDtypeStruct((32, BQ), F32),
        compiler_params=_cp(("arbitrary",)))(dtbl)


def _fox_prep(gate, fb):
    def body(g_ref, fb_ref, c_ref):
        tri = (_iota((BQ, BQ), 0) >= _iota((BQ, BQ), 1)).astype(BF16)

        def blk(i, carry):
            r0 = pl.multiple_of(i * BQ, BQ)
            lf = _log_sigmoid(g_ref[pl.ds(r0, BQ), :] + fb_ref[...])
            c = _dot(tri, _split3(lf))
            c_ref[pl.ds(r0, BQ), :] = c[:, 0:BQ] + c[:, BQ:2 * BQ] + c[:, 2 * BQ:3 * BQ] + carry
            return carry + jnp.sum(lf, axis=0, keepdims=True)

        lax.fori_loop(0, NB, blk, jnp.zeros((1, BQ), F32))

    blk = pl.BlockSpec((S, GATEW), lambda b: (b, 0))
    return pl.pallas_call(
        body, name="fox_prep", grid=(BL,), in_specs=[blk, pl.BlockSpec((1, GATEW), lambda b: (0, 0))],
        out_specs=blk, out_shape=jax.ShapeDtypeStruct((T, GATEW), F32),
        compiler_params=_cp(("parallel",)))(gate, fb)


def _fox_post(dcum, gate, fb):
    def body(dc_ref, g_ref, fb_ref, dg_ref, dfb_ref):
        b = pl.program_id(0)
        tri = (_iota((BQ, BQ), 0) <= _iota((BQ, BQ), 1)).astype(BF16)

        def blk(ii, carry):
            csum, dfb = carry
            r0 = pl.multiple_of((NB - 1 - ii) * BQ, BQ)
            dc = dc_ref[pl.ds(r0, BQ), :]
            c = _dot(tri, _split3(dc))
            dlf = c[:, 0:BQ] + c[:, BQ:2 * BQ] + c[:, 2 * BQ:3 * BQ] + csum
            dg = dlf * jnp.exp(_log_sigmoid(-(g_ref[pl.ds(r0, BQ), :] + fb_ref[...])))
            dg_ref[pl.ds(r0, BQ), :] = dg
            return csum + jnp.sum(dc, axis=0, keepdims=True), dfb + jnp.sum(dg, axis=0, keepdims=True)

        z = jnp.zeros((1, BQ), F32)
        _, dfb = lax.fori_loop(0, NB, blk, (z, z))

        @pl.when(b == 0)
        def _():
            dfb_ref[...] = dfb

        @pl.when(b > 0)
        def _():
            dfb_ref[...] += dfb

    blk = pl.BlockSpec((S, GATEW), lambda b: (b, 0))
    vec = pl.BlockSpec((1, GATEW), lambda b: (0, 0))
    return pl.pallas_call(
        body, name="fox_post", grid=(BL,), in_specs=[blk, blk, vec], out_specs=[blk, vec],
        out_shape=[jax.ShapeDtypeStruct((T, GATEW), F32), jax.ShapeDtypeStruct((1, GATEW), F32)],
        compiler_params=_cp(("arbitrary",)))(dcum, gate, fb)


def _shift_down(x, n):
    return jnp.where(_iota(x.shape, 0) >= n, pltpu.roll(x, n, 0), 0.0)


def _shift_up(x, n):
    return jnp.where(_iota(x.shape, 0) < S - n, pltpu.roll(x, S - n, 0), 0.0)


def _conv_fwd(conv, cw, mixed):
    W = 256

    def body(c_ref, w_ref, _, o_ref):
        u = c_ref[:, W:2 * W] * c_ref[:, 2 * W:3 * W]
        y = w_ref[0:1, :] * _shift_down(u, 2) + w_ref[1:2, :] * _shift_down(u, 1) + w_ref[2:3, :] * u
        o_ref[...] = (c_ref[:, 0:W] * y).astype(BF16)

    return pl.pallas_call(
        body, name="conv_fwd", grid=(BL,),
        in_specs=[pl.BlockSpec((S, CONVW), lambda b: (b, 0)), pl.BlockSpec((8, W), lambda b: (0, 0)), ANY_SPEC],
        out_specs=pl.BlockSpec((S, W), lambda b: (b, 3)),
        out_shape=jax.ShapeDtypeStruct((T, D), BF16), input_output_aliases={2: 0},
        compiler_params=_cp(("parallel",)))(conv, cw, mixed)


def _conv_bwd(conv, cw, dmixed):
    W = 256

    def body(c_ref, w_ref, do_ref, dc_ref, dw_ref):
        b = pl.program_id(0)
        bg = c_ref[:, 0:W]
        cg = c_ref[:, W:2 * W]
        hv = c_ref[:, 2 * W:3 * W]
        do = do_ref[...].astype(F32)
        u = cg * hv
        u1 = _shift_down(u, 1)
        u2 = _shift_down(u, 2)
        y = w_ref[0:1, :] * u2 + w_ref[1:2, :] * u1 + w_ref[2:3, :] * u
        dy = do * bg
        du = w_ref[2:3, :] * dy + w_ref[1:2, :] * _shift_up(dy, 1) + w_ref[0:1, :] * _shift_up(dy, 2)
        dc_ref[:, 0:W] = (do * y).astype(BF16)
        dc_ref[:, W:2 * W] = (du * hv).astype(BF16)
        dc_ref[:, 2 * W:3 * W] = (du * cg).astype(BF16)
        rowi = _iota((8, W), 0)
        dw = (jnp.where(rowi == 0, jnp.sum(dy * u2, axis=0, keepdims=True), 0.0)
              + jnp.where(rowi == 1, jnp.sum(dy * u1, axis=0, keepdims=True), 0.0)
              + jnp.where(rowi == 2, jnp.sum(dy * u, axis=0, keepdims=True), 0.0))

        @pl.when(b == 0)
        def _():
            dw_ref[...] = dw

        @pl.when(b > 0)
        def _():
            dw_ref[...] += dw

    return pl.pallas_call(
        body, name="conv_bwd", grid=(BL,),
        in_specs=[pl.BlockSpec((S, CONVW), lambda b: (b, 0)), pl.BlockSpec((8, W), lambda b: (0, 0)),
                  pl.BlockSpec((S, W), lambda b: (b, 3))],
        out_specs=[pl.BlockSpec((S, CONVW), lambda b: (b, 0)), pl.BlockSpec((8, W), lambda b: (0, 0))],
        out_shape=[jax.ShapeDtypeStruct((T, CONVW), BF16), jax.ShapeDtypeStruct((8, W), F32)],
        compiler_params=_cp(("arbitrary",)))(conv, cw, dmixed)


def _place():
    x, y, c = lax.axis_index("x"), lax.axis_index("y"), lax.axis_index("c")
    return x, y, c


def _chips_of(x, y):
    return [(1 - x, y), (x, 1 - y), (1 - x, 1 - y)]


def _dev(p):
    return 4 * p[0] + 2 * p[1] + p[2]


def _gather_job_a(shards):
    n = len(shards)

    def peers(x, y, c):
        return [(x, y, 1 - c)] + [(*chip, c) for chip in _chips_of(x, y)]

    def start(ins, outs, sems):
        send, recv, loc = sems
        x, y, c = _place()
        me = (x, y, c)
        cps = []
        for a in range(n):
            cps.append(pltpu.make_async_copy(ins[a], outs[a].at[_dev(me)], loc.at[a]))
            for k, peer in enumerate(peers(x, y, c)):
                cps.append(pltpu.make_async_remote_copy(
                    src_ref=ins[a], dst_ref=outs[a].at[_dev(me)], send_sem=send.at[a, k], recv_sem=recv.at[a, k],
                    device_id=peer, device_id_type=MESH))
        for cp in cps:
            cp.start()
        return cps

    def finish(cps, ins, outs, sems):
        send, recv, loc = sems
        x, y, c = _place()
        for a in range(n):
            for k, peer in enumerate(peers(x, y, c)):
                pltpu.make_async_remote_copy(
                    src_ref=ins[a], dst_ref=outs[a].at[_dev(peer)], send_sem=send.at[a, k], recv_sem=recv.at[a, k],
                    device_id=(x, y, c), device_id_type=MESH).wait_recv()
        for a in range(n):
            cps[5 * a].wait()
            for k in range(4):
                cps[5 * a + 1 + k].wait_send()

    return _Job(shards, [jax.ShapeDtypeStruct((NDEV,) + s.shape, s.dtype) for s in shards], {},
                [pltpu.SemaphoreType.DMA((n, 4)), pltpu.SemaphoreType.DMA((n, 4)), pltpu.SemaphoreType.DMA((n,))],
                start, finish)


def _gather_job_b(gathered):
    n = len(gathered)

    def start(ins, outs, sems):
        send, recv = sems
        x, y, c = _place()
        cps = []
        for a in range(n):
            for j, chip in enumerate(_chips_of(x, y)):
                blk = outs[a].at[_dev((*chip, c))]
                cps.append(pltpu.make_async_remote_copy(
                    src_ref=blk, dst_ref=blk, send_sem=send.at[a, j], recv_sem=recv.at[a, j],
                    device_id=(x, y, 1 - c), device_id_type=MESH))
        for cp in cps:
            cp.start()
        return cps

    def finish(cps, ins, outs, sems):
        send, recv = sems
        x, y, c = _place()
        for a in range(n):
            for j, chip in enumerate(_chips_of(x, y)):
                blk = outs[a].at[_dev((*chip, 1 - c))]
                pltpu.make_async_remote_copy(
                    src_ref=blk, dst_ref=blk, send_sem=send.at[a, j], recv_sem=recv.at[a, j],
                    device_id=(x, y, c), device_id_type=MESH).wait_recv()
        for cp in cps:
            cp.wait_send()

    return _Job(gathered, [jax.ShapeDtypeStruct(g.shape, g.dtype) for g in gathered], {a: a for a in range(n)},
                [pltpu.SemaphoreType.DMA((n, 3)), pltpu.SemaphoreType.DMA((n, 3))], start, finish)


def _sibling_job(grads):
    n = len(grads)

    def start(ins, outs, sems):
        send, recv = sems
        x, y, c = _place()
        cps = [pltpu.make_async_remote_copy(
            src_ref=ins[a].at[:, 1 - c], dst_ref=outs[a], send_sem=send.at[a], recv_sem=recv.at[a],
            device_id=(x, y, 1 - c), device_id_type=MESH) for a in range(n)]
        for cp in cps:
            cp.start()
        return cps

    def finish(cps, ins, outs, sems):
        for cp in cps:
            cp.wait()

    return _Job(grads, [jax.ShapeDtypeStruct(g.shape[:1] + g.shape[2:], F32) for g in grads], {},
                [pltpu.SemaphoreType.DMA((n,)), pltpu.SemaphoreType.DMA((n,))], start, finish)


def _chip_job(psums):
    n = len(psums)

    def start(ins, outs, sems):
        send, recv, loc = sems
        x, y, c = _place()
        mychip = 2 * x + y
        cps = []
        for a in range(n):
            cps.append(pltpu.make_async_copy(ins[a].at[mychip], outs[a].at[mychip], loc.at[a]))
            for j, chip in enumerate(_chips_of(x, y)):
                cps.append(pltpu.make_async_remote_copy(
                    src_ref=ins[a].at[2 * chip[0] + chip[1]], dst_ref=outs[a].at[mychip],
                    send_sem=send.at[a, j], recv_sem=recv.at[a, j], device_id=(*chip, c), device_id_type=MESH))
        for cp in cps:
            cp.start()
        return cps

    def finish(cps, ins, outs, sems):
        send, recv, loc = sems
        x, y, c = _place()
        mychip = 2 * x + y
        for a in range(n):
            for j, chip in enumerate(_chips_of(x, y)):
                pltpu.make_async_remote_copy(
                    src_ref=ins[a].at[mychip], dst_ref=outs[a].at[2 * chip[0] + chip[1]],
                    send_sem=send.at[a, j], recv_sem=recv.at[a, j], device_id=(x, y, c), device_id_type=MESH).wait_recv()
        for a in range(n):
            cps[4 * a].wait()
            for j in range(3):
                cps[4 * a + 1 + j].wait_send()

    return _Job(psums, [jax.ShapeDtypeStruct(p.shape, BF16) for p in psums], {},
                [pltpu.SemaphoreType.DMA((n, 3)), pltpu.SemaphoreType.DMA((n, 3)), pltpu.SemaphoreType.DMA((n,))],
                start, finish)


def _join_jobs(*jobs):
    jobs = [j for j in jobs if j is not None]
    if len(jobs) <= 1:
        return jobs[0] if jobs else None
    cut = lambda seq, sizes: [seq[sum(sizes[:k]):sum(sizes[:k + 1])] for k in range(len(sizes))]
    n_in = [len(j.ins) for j in jobs]
    n_out = [len(j.out_shapes) for j in jobs]
    n_sem = [len(j.sems) for j in jobs]
    aliases = {}
    for k, j in enumerate(jobs):
        for a, b in j.aliases.items():
            aliases[sum(n_in[:k]) + a] = sum(n_out[:k]) + b

    def start(ins, outs, sems):
        return [j.start(i, o, s) for j, i, o, s in zip(jobs, cut(ins, n_in), cut(outs, n_out), cut(sems, n_sem))]

    def finish(sts, ins, outs, sems):
        for j, st, i, o, s in zip(jobs, sts, cut(ins, n_in), cut(outs, n_out), cut(sems, n_sem)):
            j.finish(st, i, o, s)

    return _Job([t for j in jobs for t in j.ins], [t for j in jobs for t in j.out_shapes], aliases,
                [t for j in jobs for t in j.sems], start, finish)


def _run_job(job, name):
    def body(ins, outs, scr, comm):
        comm[1](comm[0]())

    return _host_call(body, name, [], [], [], [], [], {}, job)[1]


def _allreduce_small(v):
    def body(v_ref, o_ref, slots, send_sems, recv_sems):
        x, y, c = _place()
        me = 4 * x + 2 * y + c
        slots[me] = v_ref[...]

        def copy(k):
            peer = (x ^ ((k >> 2) & 1), y ^ ((k >> 1) & 1), c ^ (k & 1))
            return pltpu.make_async_remote_copy(
                src_ref=v_ref, dst_ref=slots.at[me], send_sem=send_sems.at[k - 1], recv_sem=recv_sems.at[k - 1],
                device_id=peer, device_id_type=MESH)

        def arrival(k):
            return pltpu.make_async_remote_copy(
                src_ref=v_ref, dst_ref=slots.at[me ^ k], send_sem=send_sems.at[k - 1], recv_sem=recv_sems.at[k - 1],
                device_id=(x, y, c), device_id_type=MESH)

        sends = [copy(k) for k in range(1, NDEV)]
        for cp in sends:
            cp.start()
        for k in range(1, NDEV):
            arrival(k).wait_recv()
        for cp in sends:
            cp.wait_send()
        acc = slots[0]
        for d in range(1, NDEV):
            acc = acc + slots[d]
        o_ref[...] = acc

    return pl.pallas_call(
        body, name="allreduce_small", in_specs=[VMEM_SPEC], out_specs=VMEM_SPEC,
        out_shape=jax.ShapeDtypeStruct(v.shape, F32),
        scratch_shapes=[pltpu.VMEM((NDEV,) + v.shape, F32), pltpu.SemaphoreType.DMA((NDEV - 1,)),
                        pltpu.SemaphoreType.DMA((NDEV - 1,))],
        )(v)


def _pair_sums(views, gots, core):
    n = len(views)

    def body(c_ref, *refs):
        for a in range(n):
            refs[2 * n + a][...] = (refs[a][...] + refs[n + a][...]).astype(BF16)

    def vspec(v):
        return pl.BlockSpec((None, None) + v.shape[2:], lambda k, c: (k, c[0], 0, 0))

    def gspec(g):
        return pl.BlockSpec((None,) + g.shape[1:], lambda k, c: (k, 0, 0))

    return pl.pallas_call(
        body, name="pair_sums",
        grid_spec=pltpu.PrefetchScalarGridSpec(
            num_scalar_prefetch=1, grid=(4,),
            in_specs=[vspec(v) for v in views] + [gspec(g) for g in gots],
            out_specs=[gspec(g) for g in gots]),
        out_shape=[jax.ShapeDtypeStruct(g.shape, BF16) for g in gots],
        compiler_params=_cp(("parallel",)))(core, *views, *gots)


def _chip_sums(parts):
    n = len(parts)

    def body(*refs):
        for a in range(n):
            acc = refs[a][0].astype(F32)
            for k in range(1, 4):
                acc = acc + refs[a][k].astype(F32)
            refs[n + a][...] = acc

    return pl.pallas_call(
        body, name="chip_sums", in_specs=[VMEM_SPEC] * n, out_specs=[VMEM_SPEC] * n,
        out_shape=[jax.ShapeDtypeStruct(p.shape[1:], F32) for p in parts], compiler_params=_cp())(*parts)


def _permute_in(w):
    lead = w.shape[:-1]
    return w.reshape(lead + (3, 3, 2, BQ)).swapaxes(-2, -3).reshape(lead + (QKVW,))


def _unpermute_in(w):
    lead = w.shape[:-1]
    return w.reshape(lead + (3, 2, 3, BQ)).swapaxes(-2, -3).reshape(lead + (QKVW,))


def _row(v):
    v = v.reshape(-1)
    return jnp.pad(v, (0, D - v.shape[0])).reshape(1, D)


def kernel(x, w_in, f_bias, conv_w, w_out, rel_bias, ln1_g, ln1_b, w_gate, w_up, w_down, ln2_g, ln2_b, loss_target, m_w_in, m_f_bias, m_conv_w, m_w_out, m_rel_bias, m_ln1_g, m_ln1_b, m_w_gate, m_w_up, m_w_down, m_ln2_g, m_ln2_b, v_w_in, v_f_bias, v_conv_w, v_w_out, v_rel_bias, v_ln1_g, v_ln1_b, v_w_gate, v_w_up, v_w_down, v_ln2_g, v_ln2_b):
    xi, yi, ci = _place()
    me = 4 * xi + 2 * yi + ci
    core = jnp.reshape(ci, (1,)).astype(jnp.int32)

    win_s = jnp.concatenate([_permute_in(w_in[..., :QKVW]), w_in[..., QKVW:]], axis=-1)
    win_s = jnp.pad(win_s, ((0, 0), (0, 0), (0, NPAD - NPROJ))).astype(BF16)
    per_layer = [win_s, w_out.astype(BF16), jnp.swapaxes(w_gate, 1, 2).astype(BF16),
                 jnp.swapaxes(w_up, 1, 2).astype(BF16), w_down.astype(BF16)]
    sh = [[s[l] for s in per_layer] for l in range(2)]

    def whole(g):
        return g.reshape(NDEV * g.shape[1], g.shape[2])

    first = _run_job(_gather_job_b(_run_job(_gather_job_a(sh[0][:1]), "gather_a")), "gather_b")
    W = [{"win": whole(first[0])}, {}]

    cw_rows = lax.dynamic_update_slice(jnp.zeros((2, 3, 256), F32), conv_w, (0, 0, me * 32))
    small = jnp.concatenate([_row(cw_rows[0]), _row(cw_rows[1]), jnp.zeros((SMALL_ROWS - 2, D), F32)], axis=0)
    small = _allreduce_small(small)
    cw_full = small[0:2, :CONVW].reshape(2, 3, 256)
    cw8 = jnp.pad(cw_full, ((0, 0), (0, 5), (0, 0)))
    fb = jnp.pad(f_bias, ((0, 0), (0, GATEW - NH))).reshape(2, 1, GATEW)
    tbl = _dil_table(rel_bias)

    def wcol(K, tn, off):
        return pl.BlockSpec((K, tn), lambda i, j: (0, off + j))

    def wrow(tn, K, blk=0):
        return pl.BlockSpec((tn, K), lambda i, j: (j, blk))

    def arow(tm, K, blk=0):
        return pl.BlockSpec((tm, K), lambda i, j: (i, blk))

    h = x.reshape(T, D)
    hb = h.astype(BF16)
    saved = []
    for l in range(2):
        Win = W[l]["win"]
        qkv = _mm([(hb, arow(1024, D), Win, wcol(D, 768, 0))], nt=False, M=T, N=QKVW, tm=1024, tn=768,
                  out_dtype=BF16, name="proj_qkv")
        conv = _mm([(hb, arow(512, D), Win, wcol(D, 768, 3))], nt=False, M=T, N=CONVW, tm=512, tn=768,
                   out_dtype=F32, name="proj_conv")
        gate = _mm([(hb, arow(512, D), Win, wcol(D, 128, 24))], nt=False, M=T, N=GATEW, tm=512, tn=128,
                   out_dtype=F32, name="proj_gate")
        cum = _fox_prep(gate, fb[l])
        cq = cum[:, :NH].reshape(BL, S, NH).transpose(0, 2, 1).reshape(NSTAT, S)
        ckb = jnp.broadcast_to(cq[:, :, None], (NSTAT, S, BQ))
        if l == 0:
            mixed, rtot, a0 = _sb_fwd(qkv, job=_gather_job_a(sh[0][1:]))
            mixed, lse_d, ex = _flash_fwd(qkv, mixed, 1, False, (tbl,),
                                          job=_join_jobs(_gather_job_b(list(a0)), _gather_job_a(sh[1][:2])))
            W[0].update(zip(("wout", "wgT", "wuT", "wd"), [whole(t) for t in ex[:4]]))
            mixed, lse_f, o_fox, ex = _flash_fwd(qkv, mixed, 2, True, (cq, ckb),
                                                 job=_join_jobs(_gather_job_b(list(ex[4:])), _gather_job_a(sh[1][2:])))
            W[1].update(zip(("win", "wout"), [whole(t) for t in ex[:2]]))
            a2 = list(ex[2:])
        else:
            mixed, rtot, ex = _sb_fwd(qkv, job=_gather_job_b(a2))
            W[1].update(zip(("wgT", "wuT", "wd"), [whole(t) for t in ex]))
            mixed, lse_d, _ = _flash_fwd(qkv, mixed, 1, False, (tbl,))
            mixed, lse_f, o_fox, _ = _flash_fwd(qkv, mixed, 2, True, (cq, ckb))
        Wout, WgT, WuT, Wd = W[l]["wout"], W[l]["wgT"], W[l]["wuT"], W[l]["wd"]
        mixed = _conv_fwd(conv, cw8[l], mixed)
        x1, xh1, r1, x1b = _mm_ln(mixed, Wout, h, ln1_g[l:l + 1], ln1_b[l:l + 1], "out_proj_ln")
        fs, ft, a, x2, xh2, r2, x2b = _ffn_fwd(x1b, x1, WgT, WuT, Wd, ln2_g[l:l + 1], ln2_b[l:l + 1])
        saved.append(dict(h=hb, qkv=qkv, conv=conv, gate=gate, cq=cq, ckb=ckb, mixed=mixed, rtot=rtot, lse_d=lse_d,
                          lse_f=lse_f, o_fox=o_fox, x1=x1b, xh1=xh1, r1=r1, fs=fs, ft=ft, a=a, xh2=xh2, r2=r2))
        h, hb = x2, x2b

    sq, dy = _loss_grad(h, loss_target.reshape(T, D))
    loss = lax.psum(sq[0, 0], ("x", "y", "c")) * (0.5 / D)

    def view(gr):
        return gr.reshape(4, 2, gr.shape[0] // NDEV, gr.shape[1])

    G = [None, None]
    small_g = {}
    shard_g = {}
    for l in (1, 0):
        sv = saved[l]
        Win, Wout, WgT, WuT, Wd = W[l]["win"], W[l]["wout"], W[l]["wgT"], W[l]["wuT"], W[l]["wd"]
        ds2, dg2, db2, ds2b = _ln_bwd(dy, sv["xh2"], sv["r2"], ln2_g[l:l + 1])
        dgt, dut, dx1 = _ffn_bwd(ds2b, ds2, sv["fs"], sv["ft"], Wd, WgT, WuT)
        G_d = _mm_tn(sv["a"], ds2b, None, C=D, Ka=DFF, N=D, tm=1408, tn=1024, tk=1024, ooff=0, name="grad_w_down")
        G_g = _mm_tn(dgt, sv["x1"], None, C=D, Ka=DFF, N=D, tm=1408, tn=1024, tk=1024, ooff=0, name="grad_w_gate")
        G_u = _mm_tn(dut, sv["x1"], None, C=D, Ka=DFF, N=D, tm=1408, tn=1024, tk=1024, ooff=0, name="grad_w_up")
        ds1, dg1, db1, ds1b = _ln_bwd(dx1, sv["xh1"], sv["r1"], ln1_g[l:l + 1])
        G_out = _mm_tn(sv["mixed"], ds1b, None, C=D, Ka=D, N=D, tm=1024, tn=1024, tk=1024, ooff=0, name="grad_w_out")
        dmixed = _mm([(ds1b, arow(512, D), Wout, wrow(512, D))], nt=True, M=T, N=D, tm=512, tn=512,
                     out_dtype=BF16, name="out_proj_dx")
        early = [view(t) for t in (G_g, G_u, G_d, G_out)] + ([view(G[1]["in"])] if l == 0 else [])
        dqkv, gots = _sb_bwd(sv["qkv"], dmixed, sv["rtot"], job=_sibling_job(early))
        ps = _pair_sums(early, list(gots), core)
        dqkv, dtbl, pa = _flash_bwd(sv["qkv"], sv["mixed"], dmixed, sv["lse_d"], dqkv, 1, False, (tbl,),
                                    job=_chip_job(ps[:2]))
        dqkv, dck, pb = _flash_bwd(sv["qkv"], sv["o_fox"], dmixed, sv["lse_f"], dqkv, 2, True,
                                   (sv["cq"], sv["ckb"]), job=_chip_job(ps[2:]))
        sums = _chip_sums(list(pa) + list(pb))
        shard_g[l] = dict(zip(("g", "u", "d", "out"), sums[:4]))
        if l == 0:
            shard_g[1]["in"] = sums[4]
        dconv, dcw = _conv_bwd(sv["conv"], cw8[l], dmixed)
        dcum = jnp.pad(dck.reshape(S, BL, NH).transpose(1, 0, 2).reshape(T, NH), ((0, 0), (0, GATEW - NH)))
        dgate, dfb = _fox_post(dcum, sv["gate"], fb[l])
        drb = _dil_table_bwd(dtbl)
        G_in = _mm_tn(sv["h"], dqkv, None, C=NPAD, Ka=D, N=QKVW, tm=1024, tn=768, tk=1024, ooff=0, name="grad_w_in_qkv")
        G_in = _mm_tn(sv["h"], dconv, G_in, C=NPAD, Ka=D, N=CONVW, tm=1024, tn=768, tk=1024, ooff=3,
                      name="grad_w_in_conv")
        G_in = _mm_tn(sv["h"], dgate, G_in, C=NPAD, Ka=D, N=GATEW, tm=1024, tn=128, tk=1024, ooff=24,
                      name="grad_w_in_gate")
        G[l] = {"in": G_in, "out": G_out, "g": G_g, "u": G_u, "d": G_d}
        dy = _mm([(dqkv, arow(1024, QKVW), Win, wrow(512, QKVW, 0)),
                  (dconv, arow(1024, CONVW), Win, wrow(512, CONVW, 3)),
                  (dgate, arow(1024, GATEW), Win, wrow(512, GATEW, 24))],
                 nt=True, M=T, N=D, tm=1024, tn=512, out_dtype=F32, name="proj_dx", res=ds1, res_scale=ALPHA)
        small_g[l] = dict(ln1_g=dg1, ln1_b=db1, ln2_g=dg2, ln2_b=db2, cw=dcw[0:3].reshape(1, CONVW),
                          fb=dfb[:, :NH], rb=drb[:, :NH])
    grad_x = dy.reshape(BL, S, D)

    late = [view(G[0]["in"])]
    gots = list(_run_job(_sibling_job(late), "sibling_exchange"))
    shard_g[0]["in"] = _chip_sums(_run_job(_chip_job(_pair_sums(late, gots, core)), "chip_exchange"))[0]

    rows = []
    for name in ("ln1_g", "ln1_b", "ln2_g", "ln2_b"):
        rows += [small_g[0][name], small_g[1][name]]
    rows += [_row(small_g[0]["cw"]), _row(small_g[1]["cw"]),
             _row(jnp.concatenate([small_g[0]["fb"], small_g[1]["fb"]], axis=0)),
             _row(small_g[0]["rb"] + small_g[1]["rb"])]
    rows.append(jnp.zeros((SMALL_ROWS - len(rows), D), F32))
    sg = _allreduce_small(jnp.concatenate(rows, axis=0))
    g_ln1_g, g_ln1_b, g_ln2_g, g_ln2_b = sg[0:2], sg[2:4], sg[4:6], sg[6:8]
    g_conv_full = sg[8:10, :CONVW].reshape(2, 3, 256)
    g_conv = lax.dynamic_slice(g_conv_full, (0, 0, me * 32), (2, 3, 32))
    g_fb = sg[10, :2 * NH].reshape(2, NH)
    g_rb = sg[11, :32 * NH].reshape(32, NH)

    def both(name):
        return jnp.stack([shard_g[0][name], shard_g[1][name]])

    g_in = both("in")
    g_w_in = jnp.concatenate([_unpermute_in(g_in[..., :QKVW]), g_in[..., QKVW:NPROJ]], axis=-1)
    g_w_out = both("out")
    g_w_gate = jnp.swapaxes(both("g"), 1, 2)
    g_w_up = jnp.swapaxes(both("u"), 1, 2)
    g_w_down = both("d")

    up_in = _adamw(w_in, g_w_in, m_w_in, v_w_in, 64)
    up_out = _adamw(w_out, g_w_out, m_w_out, v_w_out, 128)
    up_gate = _adamw(w_gate, g_w_gate, m_w_gate, v_w_gate, 256)
    up_up = _adamw(w_up, g_w_up, m_w_up, v_w_up, 256)
    up_down = _adamw(w_down, g_w_down, m_w_down, v_w_down, 352)

    def pack(fbv, cwv, rbv, l1g, l1b, l2g, l2b):
        r = [l1g, l1b, l2g, l2b, _row(cwv), _row(fbv), _row(rbv)]
        r.append(jnp.zeros((SMALL_ROWS - 11, D), F32))
        return jnp.concatenate(r, axis=0)

    pw = pack(f_bias, conv_w, rel_bias, ln1_g, ln1_b, ln2_g, ln2_b)
    pg = pack(g_fb, g_conv, g_rb, g_ln1_g, g_ln1_b, g_ln2_g, g_ln2_b)
    pm = pack(m_f_bias, m_conv_w, m_rel_bias, m_ln1_g, m_ln1_b, m_ln2_g, m_ln2_b)
    pv = pack(v_f_bias, v_conv_w, v_rel_bias, v_ln1_g, v_ln1_b, v_ln2_g, v_ln2_b)
    ups = [u[0] for u in _adamw(pw[None], pg[None], pm[None], pv[None], SMALL_ROWS)]

    def unpack(p):
        return dict(ln1_g=p[0:2], ln1_b=p[2:4], ln2_g=p[4:6], ln2_b=p[6:8],
                    conv_w=p[8, :192].reshape(2, 3, 32), f_bias=p[9, :2 * NH].reshape(2, NH),
                    rel_bias=p[10, :32 * NH].reshape(32, NH))

    sm = [unpack(p) for p in ups]

    def group(k):
        return (up_in[k], sm[k]["f_bias"], sm[k]["conv_w"], up_out[k], sm[k]["rel_bias"], sm[k]["ln1_g"],
                sm[k]["ln1_b"], up_gate[k], up_up[k], up_down[k], sm[k]["ln2_g"], sm[k]["ln2_b"])

    grads = (g_w_in, g_fb, g_conv, g_w_out, g_rb, g_ln1_g, g_ln1_b, g_w_gate, g_w_up, g_w_down, g_ln2_g, g_ln2_b)
    return (loss, grad_x) + grads + group(0) + group(1) + group(2)
```

```python
import math

import numpy as np
import jax
import jax.numpy as jnp
from jax import lax
from jax.experimental import pallas as pl
from jax.experimental.pallas import tpu as pltpu

F32 = jnp.float32
BF16 = jnp.bfloat16
MESH = pl.DeviceIdType.MESH

D = 1024
S = 2048
BL = 2
T = BL * S
NH = 4
DFF = 2816
NPROJ = 3076
NPAD = 3200
QKVW = 2304
CONVW = 768
GATEW = 128
PAIRW = 384
BQ = 128
HB = 2 * BQ
NB = S // BQ
NDEV = 8
NSTAT = BL * NH
ALPHA = 4.0 ** 0.25
SCALE = 0.125
NEG = -1e30
LN_EPS = 1e-5
ADAM_LR, ADAM_B1, ADAM_B2, ADAM_EPS, ADAM_WD, ADAM_STEP = 0.001, 0.9, 0.999, 1e-08, 0.01, 10
VMEM_LIMIT = 56 * 1024 * 1024
SMALL_ROWS = 16


def _bucket_thresholds():
    d = np.arange(0, S)
    nf = np.maximum(d, 1).astype(np.float32)
    large = 16 + (np.log(nf / np.float32(16)) / np.float32(math.log(128)) * np.float32(16)).astype(np.int32)
    b = np.where(d < 16, d, np.minimum(large, 31))
    return [int(np.argmax(b >= k)) for k in range(32)]


BUCKET_TH = _bucket_thresholds()


def _cp(sem=None):
    return pltpu.CompilerParams(dimension_semantics=sem, vmem_limit_bytes=VMEM_LIMIT)


def _dot(a, b):
    return lax.dot_general(a, b, (((1,), (0,)), ((), ())), preferred_element_type=F32)


def _dot_nt(a, b):
    return lax.dot_general(a, b, (((1,), (1,)), ((), ())), preferred_element_type=F32)


def _dot_tn(a, b):
    return lax.dot_general(a, b, (((0,), (0,)), ((), ())), preferred_element_type=F32)


def _split2(x):
    hi = x.astype(BF16)
    mid = (x - hi.astype(F32)).astype(BF16)
    return jnp.concatenate([hi, mid], axis=1)


def _split3(x):
    hi = x.astype(BF16)
    r = x - hi.astype(F32)
    mid = r.astype(BF16)
    lo = (r - mid.astype(F32)).astype(BF16)
    return jnp.concatenate([hi, mid, lo], axis=1)


def _log_sigmoid(u):
    return jnp.minimum(u, 0.0) - jnp.log1p(jnp.exp(-jnp.abs(u)))


def _log_sigmoid_tile(u):
    return jnp.minimum(u, 0.0) - jnp.log(1.0 + jnp.exp(jnp.minimum(u, -u)))


def _iota(shape, dim):
    return lax.broadcasted_iota(jnp.int32, shape, dim)


ANY_SPEC = pl.BlockSpec(memory_space=pl.ANY)
VMEM_SPEC = pl.BlockSpec(memory_space=pltpu.VMEM)


def _mm(pairs, *, nt, M, N, tm, tn, out_dtype, name, res=None, res_scale=1.0):
    n = len(pairs)

    def body(*refs):
        acc = None
        for p in range(n):
            a = refs[2 * p][...].astype(BF16)
            b = refs[2 * p + 1][...]
            d = _dot_nt(a, b) if nt else _dot(a, b)
            acc = d if acc is None else acc + d
        if res is not None:
            acc = acc + res_scale * refs[2 * n][...]
        refs[-1][...] = acc.astype(out_dtype)

    ops, specs = [], []
    for a, asp, b, bsp in pairs:
        ops += [a, b]
        specs += [asp, bsp]
    if res is not None:
        ops.append(res)
        specs.append(pl.BlockSpec((tm, tn), lambda i, j: (i, j)))
    return pl.pallas_call(
        body, name=name, grid=(M // tm, N // tn), in_specs=specs,
        out_specs=pl.BlockSpec((tm, tn), lambda i, j: (i, j)),
        out_shape=jax.ShapeDtypeStruct((M, N), out_dtype),
        compiler_params=_cp(("parallel", "parallel")))(*ops)


def _mm_tn(a, b, gbuf, *, C, Ka, N, tm, tn, tk, ooff, name):
    def body(*refs):
        a_ref, b_ref, o_ref = refs[0], refs[1], refs[-1]
        k = pl.program_id(2)
        d = _dot_tn(a_ref[...].astype(BF16), b_ref[...].astype(BF16))

        @pl.when(k == 0)
        def _():
            o_ref[...] = d

        @pl.when(k > 0)
        def _():
            o_ref[...] += d

    ops = [a, b] + ([] if gbuf is None else [gbuf])
    return pl.pallas_call(
        body, name=name, grid=(Ka // tm, N // tn, T // tk),
        in_specs=[pl.BlockSpec((tk, tm), lambda i, j, k: (k, i)),
                  pl.BlockSpec((tk, tn), lambda i, j, k: (k, j))] + ([] if gbuf is None else [ANY_SPEC]),
        out_specs=pl.BlockSpec((tm, tn), lambda i, j, k: (i, ooff + j)),
        out_shape=jax.ShapeDtypeStruct((Ka, C), F32),
        input_output_aliases={} if gbuf is None else {2: 0},
        compiler_params=_cp(("parallel", "parallel", "arbitrary")))(*ops)


def _ffn_up(x1, wgt, wut):
    tm, tn = 1024, 256

    def body(x_ref, wg_ref, wu_ref, g_ref, u_ref, a_ref):
        ch = 256
        for r in range(0, tm, ch):
            xb = x_ref[r:r + ch, :]
            g = _dot_nt(xb, wg_ref[...])
            u = _dot_nt(xb, wu_ref[...])
            g_ref[r:r + ch, :] = g.astype(BF16)
            u_ref[r:r + ch, :] = u.astype(BF16)
            a_ref[r:r + ch, :] = (g * jax.nn.sigmoid(g) * u).astype(BF16)

    wspec = pl.BlockSpec((tn, D), lambda i, j: (j, 0))
    ospec = pl.BlockSpec((tm, tn), lambda i, j: (i, j))
    return pl.pallas_call(
        body, name="ffn_up", grid=(T // tm, DFF // tn),
        in_specs=[pl.BlockSpec((tm, D), lambda i, j: (i, 0)), wspec, wspec],
        out_specs=[ospec, ospec, ospec],
        out_shape=[jax.ShapeDtypeStruct((T, DFF), BF16)] * 3,
        compiler_params=_cp(("parallel", "parallel")))(x1, wgt, wut)


def _ffn_da(dffn, wd, s, t):
    tm, tn = 1024, 256

    def body(d_ref, wd_ref, s_ref, t_ref, dg_ref, du_ref):
        ch = 256
        for r in range(0, tm, ch):
            da = _dot_nt(d_ref[r:r + ch, :], wd_ref[...])
            gv = s_ref[r:r + ch, :].astype(F32)
            sg = jax.nn.sigmoid(gv)
            dg_ref[r:r + ch, :] = (da * t_ref[r:r + ch, :].astype(F32) * (sg * (1.0 + gv * (1.0 - sg)))).astype(BF16)
            du_ref[r:r + ch, :] = (da * (gv * sg)).astype(BF16)

    ospec = pl.BlockSpec((tm, tn), lambda i, j: (i, j))
    return pl.pallas_call(
        body, name="ffn_da", grid=(T // tm, DFF // tn),
        in_specs=[pl.BlockSpec((tm, D), lambda i, j: (i, 0)),
                  pl.BlockSpec((tn, D), lambda i, j: (j, 0)), ospec, ospec],
        out_specs=[ospec, ospec],
        out_shape=[jax.ShapeDtypeStruct((T, DFF), BF16), jax.ShapeDtypeStruct((T, DFF), BF16)],
        compiler_params=_cp(("parallel", "parallel")))(dffn, wd, s, t)


def _ffn_fwd(xb, x, wgt, wut, wd, gam, bet):
    tm, ch = 512, 256

    def body(xb_ref, x_ref, g_ref, b_ref, wg_hbm, wu_hbm, wd_hbm,
             go_ref, uo_ref, ao_ref, y_ref, xh_ref, r_ref, yb_ref, wg_v, wu_v, wd_v, sem):
        @pl.when(pl.program_id(0) == 0)
        def _():
            _copy_in(wg_hbm, wg_v, sem)
            _copy_in(wu_hbm, wu_v, sem)
            _copy_in(wd_hbm, wd_v, sem)

        xv = xb_ref[...]
        for c in range(0, DFF, ch):
            gv = _dot_nt(xv, wg_v[c:c + ch, :])
            uv = _dot_nt(xv, wu_v[c:c + ch, :])
            go_ref[:, c:c + ch] = gv.astype(BF16)
            uo_ref[:, c:c + ch] = uv.astype(BF16)
            ao_ref[:, c:c + ch] = (gv * jax.nn.sigmoid(gv) * uv).astype(BF16)
        s = ALPHA * x_ref[...] + _dot(ao_ref[...], wd_v[...])
        mu = jnp.mean(s, axis=-1, keepdims=True)
        xc = s - mu
        var = jnp.mean(xc * xc, axis=-1, keepdims=True)
        r = lax.rsqrt(var + LN_EPS)
        xh = xc * r
        xh_ref[...] = xh.astype(BF16)
        r_ref[...] = r
        y = xh * g_ref[...] + b_ref[...]
        y_ref[...] = y
        yb_ref[...] = y.astype(BF16)

    row = pl.BlockSpec((tm, D), lambda i: (i, 0))
    wide = pl.BlockSpec((tm, DFF), lambda i: (i, 0))
    vec = pl.BlockSpec((1, D), lambda i: (0, 0))
    wsl = pltpu.VMEM((DFF, D), BF16)
    hid = jax.ShapeDtypeStruct((T, DFF), BF16)
    return pl.pallas_call(
        body, name="ffn_fwd", grid=(T // tm,),
        in_specs=[row, row, vec, vec, ANY_SPEC, ANY_SPEC, ANY_SPEC],
        out_specs=[wide, wide, wide, row, row, pl.BlockSpec((tm, 1), lambda i: (i, 0)), row],
        out_shape=[hid, hid, hid, jax.ShapeDtypeStruct((T, D), F32), jax.ShapeDtypeStruct((T, D), BF16),
                   jax.ShapeDtypeStruct((T, 1), F32), jax.ShapeDtypeStruct((T, D), BF16)],
        scratch_shapes=[wsl, wsl, wsl, pltpu.SemaphoreType.DMA],
        compiler_params=_cp(("arbitrary",)))(xb, x, gam, bet, wgt, wut, wd)


def _ffn_bwd(dffn, res, g, u, wd, wgt, wut):
    tm, ch = 512, 256

    def body(d_ref, r_ref, g_ref, u_ref, wd_hbm, wg_hbm, wu_hbm, dg_ref, du_ref, dx_ref, wd_v, wg_v, wu_v, sem):
        @pl.when(pl.program_id(0) == 0)
        def _():
            _copy_in(wd_hbm, wd_v, sem)
            _copy_in(wg_hbm, wg_v, sem)
            _copy_in(wu_hbm, wu_v, sem)

        db = d_ref[...]
        for c in range(0, DFF, ch):
            da = _dot_nt(db, wd_v[c:c + ch, :])
            gv = g_ref[:, c:c + ch].astype(F32)
            sg = jax.nn.sigmoid(gv)
            dg_ref[:, c:c + ch] = (da * u_ref[:, c:c + ch].astype(F32) * (sg * (1.0 + gv * (1.0 - sg)))).astype(BF16)
            du_ref[:, c:c + ch] = (da * (gv * sg)).astype(BF16)
        dx_ref[...] = ALPHA * r_ref[...] + _dot(dg_ref[...], wg_v[...]) + _dot(du_ref[...], wu_v[...])

    row = pl.BlockSpec((tm, D), lambda i: (i, 0))
    wide = pl.BlockSpec((tm, DFF), lambda i: (i, 0))
    wsl = pltpu.VMEM((DFF, D), BF16)
    return pl.pallas_call(
        body, name="ffn_bwd", grid=(T // tm,),
        in_specs=[row, row, wide, wide, ANY_SPEC, ANY_SPEC, ANY_SPEC], out_specs=[wide, wide, row],
        out_shape=[jax.ShapeDtypeStruct((T, DFF), BF16), jax.ShapeDtypeStruct((T, DFF), BF16),
                   jax.ShapeDtypeStruct((T, D), F32)],
        scratch_shapes=[wsl, wsl, wsl, pltpu.SemaphoreType.DMA],
        compiler_params=_cp(("arbitrary",)))(dffn, res, g, u, wd, wgt, wut)


def _mm_ln(a, w, x, gam, bet, name):
    tm = 256
    K = a.shape[1]

    def body(a_ref, w_ref, x_ref, g_ref, b_ref, y_ref, xh_ref, r_ref, yb_ref):
        s = ALPHA * x_ref[...] + _dot(a_ref[...], w_ref[...])
        mu = jnp.mean(s, axis=-1, keepdims=True)
        xc = s - mu
        var = jnp.mean(xc * xc, axis=-1, keepdims=True)
        r = lax.rsqrt(var + LN_EPS)
        xh = xc * r
        xh_ref[...] = xh.astype(BF16)
        r_ref[...] = r
        y = xh * g_ref[...] + b_ref[...]
        y_ref[...] = y
        yb_ref[...] = y.astype(BF16)

    row = pl.BlockSpec((tm, D), lambda i: (i, 0))
    vec = pl.BlockSpec((1, D), lambda i: (0, 0))
    return pl.pallas_call(
        body, name=name, grid=(T // tm,),
        in_specs=[pl.BlockSpec((tm, K), lambda i: (i, 0)), pl.BlockSpec((K, D), lambda i: (0, 0)), row, vec, vec],
        out_specs=[row, row, pl.BlockSpec((tm, 1), lambda i: (i, 0)), row],
        out_shape=[jax.ShapeDtypeStruct((T, D), F32), jax.ShapeDtypeStruct((T, D), BF16),
                   jax.ShapeDtypeStruct((T, 1), F32), jax.ShapeDtypeStruct((T, D), BF16)],
        compiler_params=_cp(("parallel",)))(a, w, x, gam, bet)


def _ln_bwd(dy, xh, r, gam):
    tm = 256

    def body(dy_ref, xh_ref, r_ref, g_ref, ds_ref, dg_ref, db_ref, dsb_ref):
        i = pl.program_id(0)
        dyv = dy_ref[...]
        xhv = xh_ref[...].astype(F32)
        dxh = dyv * g_ref[...]
        m1 = jnp.mean(dxh, axis=-1, keepdims=True)
        m2 = jnp.mean(dxh * xhv, axis=-1, keepdims=True)
        ds = r_ref[...] * (dxh - m1 - xhv * m2)
        ds_ref[...] = ds
        dsb_ref[...] = ds.astype(BF16)
        pg = jnp.sum(dyv * xhv, axis=0, keepdims=True)
        pb = jnp.sum(dyv, axis=0, keepdims=True)

        @pl.when(i == 0)
        def _():
            dg_ref[...] = pg
            db_ref[...] = pb

        @pl.when(i > 0)
        def _():
            dg_ref[...] += pg
            db_ref[...] += pb

    row = pl.BlockSpec((tm, D), lambda i: (i, 0))
    vec = pl.BlockSpec((1, D), lambda i: (0, 0))
    return pl.pallas_call(
        body, name="ln_bwd", grid=(T // tm,),
        in_specs=[row, row, pl.BlockSpec((tm, 1), lambda i: (i, 0)), vec],
        out_specs=[row, vec, vec, row],
        out_shape=[jax.ShapeDtypeStruct((T, D), F32), jax.ShapeDtypeStruct((1, D), F32),
                   jax.ShapeDtypeStruct((1, D), F32), jax.ShapeDtypeStruct((T, D), BF16)],
        compiler_params=_cp(("arbitrary",)))(dy, xh, r, gam)


def _loss_grad(y, tgt):
    tm = 256

    def body(y_ref, t_ref, l_ref, dy_ref):
        i = pl.program_id(0)
        e = y_ref[...] - t_ref[...]
        dy_ref[...] = e * (1.0 / D)
        p = jnp.sum(jnp.sum(e * e, axis=1, keepdims=True), axis=0, keepdims=True)

        @pl.when(i == 0)
        def _():
            l_ref[...] = p

        @pl.when(i > 0)
        def _():
            l_ref[...] += p

    row = pl.BlockSpec((tm, D), lambda i: (i, 0))
    return pl.pallas_call(
        body, name="loss_grad", grid=(T // tm,), in_specs=[row, row],
        out_specs=[pl.BlockSpec((1, 1), lambda i: (0, 0)), row],
        out_shape=[jax.ShapeDtypeStruct((1, 1), F32), jax.ShapeDtypeStruct((T, D), F32)],
        compiler_params=_cp(("arbitrary",)))(y, tgt)


def _adamw(w, g, m, v, tr):
    L, R, C = w.shape

    def body(w_ref, g_ref, m_ref, v_ref, d_ref, m2_ref, v2_ref):
        gv = g_ref[...]
        m2 = ADAM_B1 * m_ref[...] + (1.0 - ADAM_B1) * gv
        v2 = ADAM_B2 * v_ref[...] + (1.0 - ADAM_B2) * (gv * gv)
        m_hat = m2 / (1.0 - ADAM_B1 ** ADAM_STEP)
        v_hat = v2 / (1.0 - ADAM_B2 ** ADAM_STEP)
        d_ref[...] = -ADAM_LR * (m_hat / (jnp.sqrt(v_hat) + ADAM_EPS) + ADAM_WD * w_ref[...])
        m2_ref[...] = m2
        v2_ref[...] = v2

    blk = pl.BlockSpec((None, tr, C), lambda l, i: (l, i, 0))
    sh = jax.ShapeDtypeStruct((L, R, C), F32)
    return pl.pallas_call(
        body, name="adamw", grid=(L, R // tr), in_specs=[blk] * 4, out_specs=[blk] * 3,
        out_shape=[sh, sh, sh], compiler_params=_cp(("parallel", "parallel")))(w, g, m, v)


class _Job:
    def __init__(self, ins, out_shapes, aliases, sems, start, finish):
        self.ins, self.out_shapes, self.aliases, self.sems = list(ins), list(out_shapes), dict(aliases), list(sems)
        self.start, self.finish = start, finish


def _host_call(body, name, ins, in_specs, out_shapes, out_specs, scratch, aliases, job):
    n_in, n_out, n_scr = len(ins), len(out_shapes), len(scratch)
    jins = job.ins if job else []
    jouts = job.out_shapes if job else []
    jsems = job.sems if job else []

    def wrapped(*refs):
        a = n_in
        b = a + len(jins)
        c = b + n_out
        d = c + len(jouts)
        e = d + n_scr
        comm = None
        if job:
            jrefs = (refs[a:b], refs[c:d], refs[e:])
            comm = (lambda: job.start(*jrefs), lambda st: job.finish(st, *jrefs))
        body(refs[:a], refs[b:c], refs[d:e], comm)

    al = dict(aliases)
    if job:
        for ji, jo in job.aliases.items():
            al[n_in + ji] = n_out + jo
    res = pl.pallas_call(
        wrapped, name=name, in_specs=list(in_specs) + [ANY_SPEC] * len(jins),
        out_specs=list(out_specs) + [ANY_SPEC] * len(jouts), out_shape=list(out_shapes) + list(jouts),
        scratch_shapes=list(scratch) + list(jsems), input_output_aliases=al,
        compiler_params=_cp())(*ins, *jins)
    return res[:n_out], res[n_out:]


def _copy_in(src, dst, sem):
    cp = pltpu.make_async_copy(src, dst, sem)
    cp.start()
    cp.wait()


CHAINS = [(p, b) for p in range(2) for b in range(BL)]
NC = len(CHAINS)
ROWS_SHAPE = jax.ShapeDtypeStruct((NSTAT, S), F32)
SLAB_QKV = pltpu.VMEM((T, 2 * PAIRW), BF16)
SLAB_OUT = pltpu.VMEM((T, 2 * BQ), BF16)
SLAB_O32 = pltpu.VMEM((T, 2 * BQ), F32)
SLAB_T = pltpu.VMEM((2, BQ, T), BF16)
SLAB_KEYB = pltpu.VMEM((NSTAT, S, BQ), F32)
ACC_KV = pltpu.VMEM((2, T, BQ), F32)


def _lane_masks():
    lane = _iota((1, BQ), 1)
    m0 = (lane < 64).astype(BF16)
    return m0, 1.0 - m0


def _row_masks():
    r = _iota((BQ, 1), 0)
    m0 = (r < 64).astype(BF16)
    return m0, 1.0 - m0


def _stack(x, m0, m1):
    return jnp.concatenate([x * m0, x * m1], axis=0)


def _stack_t(xt, r0, r1):
    return jnp.concatenate([xt * r0, xt * r1], axis=1)


def _tr(x):
    return x.T


def _rows(b, i):
    return pl.ds(pl.multiple_of(b * S + i * BQ, BQ), BQ)


def _transpose_slab(src, dst, col0):
    def blk(n, _):
        r = pl.ds(pl.multiple_of(n * BQ, BQ), BQ)
        for p in range(2):
            dst[p, :, r] = _tr(src[r, col0(p):col0(p) + BQ])
        return 0

    lax.fori_loop(0, T // BQ, blk, 0)


def _heads(x):
    return x[:BQ], x[BQ:]


def _bcast_heads(r0, r1):
    return jnp.concatenate([jnp.broadcast_to(r0, (BQ, BQ)), jnp.broadcast_to(r1, (BQ, BQ))], axis=0)


def _by_channel(r0, r1):
    return jnp.where(_iota((BQ, BQ), 0) < 64, r0, r1)


def _colsum2(x):
    return jnp.sum(x[:BQ], axis=0, keepdims=True), jnp.sum(x[BQ:], axis=0, keepdims=True)


def _stat_row(ref, p, b, h, i):
    c = b * NH + 2 * p + h
    return ref[c:c + 1, pl.ds(pl.multiple_of(i * BQ, BQ), BQ)]


def _put_row(ref, p, b, h, i, v):
    c = b * NH + 2 * p + h
    ref[c:c + 1, pl.ds(pl.multiple_of(i * BQ, BQ), BQ)] = v


def _valid_t(strict):
    r = _iota((HB, BQ), 0) & (BQ - 1)
    c = _iota((HB, BQ), 1)
    return (r < c) if strict else (r <= c)


def _tri_blockdiag(later):
    r = _iota((HB, HB), 0)
    c = _iota((HB, HB), 1)
    same = (r >= BQ) == (c >= BQ)
    return (same & ((c > r) if later else (c < r))).astype(BF16)


def _cum_mm(tri, x):
    y = _dot(tri, _split2(x))
    return y[:, :BQ] + y[:, BQ:]


def _kv_tiles(qkv_v, p, b, j):
    r = _rows(b, j)
    return qkv_v[r, p * PAIRW + BQ:p * PAIRW + 2 * BQ], qkv_v[r, p * PAIRW + 2 * BQ:p * PAIRW + 3 * BQ]


def _q_tile(qkv_v, p, b, i):
    return qkv_v[_rows(b, i), p * PAIRW:p * PAIRW + BQ] * SCALE


def _sb_fwd(qkv, job=None):
    def body(ins, outs, scr, comm):
        (qkv_hbm,), (o_hbm, r_ref), (qkv_v, o_v, sem, vt_v) = ins, outs, scr
        _copy_in(qkv_hbm.at[:, pl.ds(0, 2 * PAIRW)], qkv_v, sem)
        st = comm[0]() if comm else None
        _transpose_slab(qkv_v, vt_v, lambda p: p * PAIRW + 2 * BQ)
        m0, m1 = _lane_masks()
        r0, r1 = _row_masks()
        valid = _valid_t(True)
        later = _tri_blockdiag(True)

        def steps(qts, i, j, cs, diag):
            ks = [_stack(_kv_tiles(qkv_v, p, b, j)[0], m0, m1) for p, b in CHAINS]
            zs = [_dot(ks[c], qts[c]) for c in range(NC)]
            lbs, lrs = [], []
            for c in range(NC):
                lb = _log_sigmoid_tile(zs[c])
                lr = lb - zs[c]
                if diag:
                    lr = jnp.where(valid, lr, 0.0)
                lbs.append(lb)
                lrs.append(lr)
            tails = [_cum_mm(later, lrs[c]) for c in range(NC)]
            avs = []
            for c in range(NC):
                a = jnp.exp(lbs[c] + tails[c] + _bcast_heads(*cs[c][0]))
                if diag:
                    a = jnp.where(valid, a, 0.0)
                avs.append(a.astype(BF16))
            out = []
            for c, (p, b) in enumerate(CHAINS):
                vts = _stack_t(vt_v[p, :, _rows(b, j)], r0, r1)
                s0, s1 = _colsum2(lrs[c])
                out.append(((cs[c][0][0] + s0, cs[c][0][1] + s1), cs[c][1] + _dot(vts, avs[c])))
            return tuple(out)

        def qblock(i, _):
            qts = [_tr(_q_tile(qkv_v, p, b, i)) for p, b in CHAINS]
            zr = jnp.zeros((1, BQ), F32)
            cs = steps(qts, i, i, (((zr, zr), jnp.zeros((BQ, BQ), F32)),) * NC, True)
            cs = lax.fori_loop(1, i + 1, lambda jj, cs: steps(qts, i, i - jj, cs, False), cs)
            for c, (p, b) in enumerate(CHAINS):
                o_v[_rows(b, i), p * BQ:(p + 1) * BQ] = cs[c][1].T.astype(BF16)
                for h in range(2):
                    _put_row(r_ref, p, b, h, i, cs[c][0][h])
            return 0

        lax.fori_loop(0, NB, qblock, 0)
        _copy_in(o_v, o_hbm.at[:, pl.ds(0, 2 * BQ)], sem)
        if comm:
            comm[1](st)

    (mixed, rtot), extra = _host_call(
        body, "sb_fwd", [qkv], [ANY_SPEC], [jax.ShapeDtypeStruct((T, D), BF16), ROWS_SHAPE], [ANY_SPEC, VMEM_SPEC],
        [SLAB_QKV, SLAB_OUT, pltpu.SemaphoreType.DMA, SLAB_T], {}, job)
    return mixed, rtot, extra


def _sb_bwd(qkv, dmixed, rtot, job=None):
    def body(ins, outs, scr, comm):
        (qkv_hbm, do_hbm, r_ref), (dqkv_hbm,), (qkv_v, do_v, dq_v, dk_s, dv_s, sem, kt_v) = ins, outs, scr
        _copy_in(qkv_hbm.at[:, pl.ds(0, 2 * PAIRW)], qkv_v, sem)
        _copy_in(do_hbm.at[:, pl.ds(0, 2 * BQ)], do_v, sem)
        st = comm[0]() if comm else None
        _transpose_slab(qkv_v, kt_v, lambda p: p * PAIRW + BQ)
        m0, m1 = _lane_masks()
        f0, f1 = m0.astype(F32), m1.astype(F32)
        r0, r1 = _row_masks()
        valid = _valid_t(True)
        later = _tri_blockdiag(True)
        earlier = _tri_blockdiag(False)
        dk_s[...] = jnp.zeros_like(dk_s)
        dv_s[...] = jnp.zeros_like(dv_s)

        def steps(qns, qts, dns, dts, rts, i, j, cs, diag):
            kv = [_kv_tiles(qkv_v, p, b, j) for p, b in CHAINS]
            ks = [_stack(kv[c][0], m0, m1) for c in range(NC)]
            vs = [_stack(kv[c][1], m0, m1) for c in range(NC)]
            zs = [_dot(ks[c], qts[c]) for c in range(NC)]
            das = [_dot(vs[c], dts[c]) for c in range(NC)]
            lbs, lrs, pls = [], [], []
            for c in range(NC):
                lb = _log_sigmoid_tile(zs[c])
                lr = lb - zs[c]
                if diag:
                    lr = jnp.where(valid, lr, 0.0)
                s0, s1 = _colsum2(lr)
                lbs.append(lb)
                lrs.append(lr)
                pls.append((cs[c][0][0] + s0, cs[c][0][1] + s1))
            tails = [_cum_mm(later, lrs[c]) for c in range(NC)]
            avs, gms = [], []
            for c in range(NC):
                a = jnp.exp(lbs[c] + tails[c] + _bcast_heads(rts[c][0] - pls[c][0], rts[c][1] - pls[c][1]))
                if diag:
                    a = jnp.where(valid, a, 0.0)
                avs.append(a)
                gms.append(das[c] * a)
            befores = [_cum_mm(earlier, gms[c]) for c in range(NC)]
            dzbs = []
            for c in range(NC):
                beta = jnp.exp(lbs[c])
                dz = gms[c] - beta * (gms[c] + befores[c] + _bcast_heads(*cs[c][1]))
                if diag:
                    dz = jnp.where(valid, dz, 0.0)
                dzbs.append(dz.astype(BF16))
            out = []
            for c, (p, b) in enumerate(CHAINS):
                dq = cs[c][2] + _dot(_stack_t(kt_v[p, :, _rows(b, j)], r0, r1), dzbs[c])
                dk = _dot(dzbs[c], qns[c])
                dv = _dot(avs[c].astype(BF16), dns[c])
                dk_s[p, _rows(b, j), :] += dk[:BQ] * f0 + dk[BQ:] * f1
                dv_s[p, _rows(b, j), :] += dv[:BQ] * f0 + dv[BQ:] * f1
                g0, g1 = _colsum2(gms[c])
                out.append((pls[c], (cs[c][1][0] + g0, cs[c][1][1] + g1), dq))
            return tuple(out)

        def qblock(i, _):
            qns = [_q_tile(qkv_v, p, b, i) for p, b in CHAINS]
            dns = [do_v[_rows(b, i), p * BQ:(p + 1) * BQ] for p, b in CHAINS]
            qts = [_tr(t) for t in qns]
            dts = [_tr(t) for t in dns]
            rts = [(_stat_row(r_ref, p, b, 0, i), _stat_row(r_ref, p, b, 1, i)) for p, b in CHAINS]
            zr = jnp.zeros((1, BQ), F32)
            cs = (((zr, zr), (zr, zr), jnp.zeros((BQ, BQ), F32)),) * NC
            cs = lax.fori_loop(0, i, lambda j, cs: steps(qns, qts, dns, dts, rts, i, j, cs, False), cs)
            cs = steps(qns, qts, dns, dts, rts, i, i, cs, True)
            for c, (p, b) in enumerate(CHAINS):
                dq_v[_rows(b, i), p * PAIRW:p * PAIRW + BQ] = (cs[c][2].T * SCALE).astype(BF16)
            return 0

        lax.fori_loop(0, NB, qblock, 0)
        for p in range(2):
            dq_v[:, p * PAIRW + BQ:p * PAIRW + 2 * BQ] = dk_s[p].astype(BF16)
            dq_v[:, p * PAIRW + 2 * BQ:p * PAIRW + 3 * BQ] = dv_s[p].astype(BF16)
        _copy_in(dq_v, dqkv_hbm.at[:, pl.ds(0, 2 * PAIRW)], sem)
        if comm:
            comm[1](st)

    (dqkv,), extra = _host_call(
        body, "sb_bwd", [qkv, dmixed, rtot], [ANY_SPEC, ANY_SPEC, VMEM_SPEC],
        [jax.ShapeDtypeStruct((T, QKVW), BF16)], [ANY_SPEC],
        [SLAB_QKV, SLAB_OUT, SLAB_QKV, ACC_KV, ACC_KV, pltpu.SemaphoreType.DMA, SLAB_T], {}, job)
    return dqkv, extra


def _flash_fwd(qkv, mixed, g, fox, bias, job=None):
    def body(ins, outs, scr, comm):
        if fox:
            qkv_hbm, cq_ref, ckb_hbm, _ = ins
            (o_hbm, lse_ref, o32_hbm), (qkv_v, o_v, sem, vt_v, o32_v, ckb_v) = outs, scr
        else:
            qkv_hbm, tbl_ref, _ = ins
            (o_hbm, lse_ref), (qkv_v, o_v, sem, vt_v) = outs, scr
        _copy_in(qkv_hbm.at[:, pl.ds(g * 2 * PAIRW, 2 * PAIRW)], qkv_v, sem)
        if fox:
            _copy_in(ckb_hbm, ckb_v, sem)
        st = comm[0]() if comm else None
        _transpose_slab(qkv_v, vt_v, lambda p: p * PAIRW + 2 * BQ)
        m0, m1 = _lane_masks()
        r0, r1 = _row_masks()
        valid = _valid_t(False)

        def steps(qts, cqs, i, j, cs, diag):
            ks = [_stack(_kv_tiles(qkv_v, p, b, j)[0], m0, m1) for p, b in CHAINS]
            zs = [_dot(ks[c], qts[c]) for c in range(NC)]
            prs, alphas, out = [], [], []
            for c, (p, b) in enumerate(CHAINS):
                (ma, mb), (la, lb_), _ = cs[c]
                if fox:
                    kk = pl.ds(pl.multiple_of(j * BQ, BQ), BQ)
                    col = b * NH + 2 * p
                    z = zs[c] + (cqs[c] - jnp.concatenate([ckb_v[col, kk, :], ckb_v[col + 1, kk, :]], axis=0))
                    if diag:
                        z = jnp.where(valid, z, NEG)
                else:
                    z = zs[c] + tbl_ref[p, i - j]
                za, zb = _heads(z)
                na = jnp.maximum(ma, jnp.max(za, axis=0, keepdims=True))
                nb = jnp.maximum(mb, jnp.max(zb, axis=0, keepdims=True))
                aa, ab = jnp.exp(ma - na), jnp.exp(mb - nb)
                pr = jnp.exp(z - _bcast_heads(na, nb))
                sa, sb = _colsum2(pr)
                prs.append(_split2(pr) if fox else pr.astype(BF16))
                alphas.append((aa, ab))
                out.append(((na, nb), (aa * la + sa, ab * lb_ + sb)))
            pvs = []
            for c, (p, b) in enumerate(CHAINS):
                vts = _stack_t(vt_v[p, :, _rows(b, j)], r0, r1)
                if fox:
                    pvs.append(_dot(vts, prs[c][:, :BQ]) + _dot(vts, prs[c][:, BQ:]))
                else:
                    pvs.append(_dot(vts, prs[c]))
            return tuple((out[c][0], out[c][1], _by_channel(*alphas[c]) * cs[c][2] + pvs[c]) for c in range(NC))

        def qblock(i, _):
            qts = [_tr(_q_tile(qkv_v, p, b, i)) for p, b in CHAINS]
            if fox:
                cqs = [_bcast_heads(_stat_row(cq_ref, p, b, 0, i), _stat_row(cq_ref, p, b, 1, i)) for p, b in CHAINS]
            else:
                cqs = [None] * NC
            ng = jnp.full((1, BQ), NEG, F32)
            zr = jnp.zeros((1, BQ), F32)
            cs = steps(qts, cqs, i, i, (((ng, ng), (zr, zr), jnp.zeros((BQ, BQ), F32)),) * NC, True)
            cs = lax.fori_loop(1, i + 1, lambda jj, cs: steps(qts, cqs, i, i - jj, cs, False), cs)
            for c, (p, b) in enumerate(CHAINS):
                (ma, mb), (la, lb_), acc = cs[c]
                o = (acc / _by_channel(la, lb_)).T
                o_v[_rows(b, i), p * BQ:(p + 1) * BQ] = o.astype(BF16)
                if fox:
                    o32_v[_rows(b, i), p * BQ:(p + 1) * BQ] = o
                _put_row(lse_ref, p, b, 0, i, ma + jnp.log(la))
                _put_row(lse_ref, p, b, 1, i, mb + jnp.log(lb_))
            return 0

        lax.fori_loop(0, NB, qblock, 0)
        _copy_in(o_v, o_hbm.at[:, pl.ds(g * 2 * BQ, 2 * BQ)], sem)
        if fox:
            _copy_in(o32_v, o32_hbm, sem)
        if comm:
            comm[1](st)

    bias_specs = [VMEM_SPEC, ANY_SPEC] if fox else [VMEM_SPEC]
    n_in = 2 + len(bias_specs)
    o32 = [jax.ShapeDtypeStruct((T, 2 * BQ), F32)] if fox else []
    res, extra = _host_call(
        body, "fox_fwd" if fox else "dil_fwd", [qkv, *bias, mixed], [ANY_SPEC] + bias_specs + [ANY_SPEC],
        [jax.ShapeDtypeStruct((T, D), BF16), ROWS_SHAPE] + o32, [ANY_SPEC, VMEM_SPEC] + [ANY_SPEC] * len(o32),
        [SLAB_QKV, SLAB_OUT, pltpu.SemaphoreType.DMA, SLAB_T] + ([SLAB_O32, SLAB_KEYB] if fox else []),
        {n_in - 1: 0}, job)
    return (*res, extra)


def _flash_bwd(qkv, o, dmixed, lse, dqkv, g, fox, bias, job=None):
    def body(ins, outs, scr, comm):
        if fox:
            qkv_hbm, o_hbm, do_hbm, lse_ref, cq_ref, ckb_hbm, _ = ins
            (dqkv_hbm, db_ref), (qkv_v, o_v, do_v, dq_v, dk_s, dv_s, sem, kt_v, ckb_v, dc_s) = outs, scr
        else:
            qkv_hbm, o_hbm, do_hbm, lse_ref, tbl_ref, _ = ins
            (dqkv_hbm, db_ref), (qkv_v, o_v, do_v, dq_v, dk_s, dv_s, sem, kt_v) = outs, scr
        _copy_in(qkv_hbm.at[:, pl.ds(g * 2 * PAIRW, 2 * PAIRW)], qkv_v, sem)
        _copy_in(do_hbm.at[:, pl.ds(g * 2 * BQ, 2 * BQ)], do_v, sem)
        if fox:
            _copy_in(o_hbm, o_v, sem)
            _copy_in(ckb_hbm, ckb_v, sem)
        else:
            _copy_in(o_hbm.at[:, pl.ds(g * 2 * BQ, 2 * BQ)], o_v, sem)
        st = comm[0]() if comm else None
        _transpose_slab(qkv_v, kt_v, lambda p: p * PAIRW + BQ)
        m0, m1 = _lane_masks()
        f0, f1 = m0.astype(F32), m1.astype(F32)
        r0, r1 = _row_masks()
        valid = _valid_t(False)
        dk_s[...] = jnp.zeros_like(dk_s)
        dv_s[...] = jnp.zeros_like(dv_s)
        if fox:
            dc_s[...] = jnp.zeros_like(dc_s)
        else:
            db_ref[...] = jnp.zeros_like(db_ref)

        def steps(qns, qts, dns, dts, cqs, lses, deltas, i, j, dqs, diag):
            kv = [_kv_tiles(qkv_v, p, b, j) for p, b in CHAINS]
            ks = [_stack(kv[c][0], m0, m1) for c in range(NC)]
            vs = [_stack(kv[c][1], m0, m1) for c in range(NC)]
            zs = [_dot(ks[c], qts[c]) for c in range(NC)]
            dps = [_dot(vs[c], dts[c]) for c in range(NC)]
            prs, dzl = [], []
            for c, (p, b) in enumerate(CHAINS):
                if fox:
                    kk = pl.ds(pl.multiple_of(j * BQ, BQ), BQ)
                    col = b * NH + 2 * p
                    z = zs[c] + (cqs[c] - jnp.concatenate([ckb_v[col, kk, :], ckb_v[col + 1, kk, :]], axis=0))
                    if diag:
                        z = jnp.where(valid, z, NEG)
                else:
                    z = zs[c] + tbl_ref[p, i - j]
                pr = jnp.exp(z - lses[c])
                prs.append(pr.astype(BF16))
                dzl.append(pr * (dps[c] - deltas[c]))
            dzbs = [dz.astype(BF16) for dz in dzl]
            new = []
            for c, (p, b) in enumerate(CHAINS):
                new.append(dqs[c] + _dot(_stack_t(kt_v[p, :, _rows(b, j)], r0, r1), dzbs[c]))
                dk = _dot(dzbs[c], qns[c])
                dv = _dot(prs[c], dns[c])
                dk_s[p, _rows(b, j), :] += dk[:BQ] * f0 + dk[BQ:] * f1
                dv_s[p, _rows(b, j), :] += dv[:BQ] * f0 + dv[BQ:] * f1
                if fox:
                    dc_s[c, pl.ds(pl.multiple_of(j * HB, HB), HB), :] += dzl[c]
            if not fox:
                for p in range(2):
                    db_ref[p, i - j] = db_ref[p, i - j] + (dzl[2 * p] + dzl[2 * p + 1])
            return tuple(new)

        def qblock(i, _):
            qns = [_q_tile(qkv_v, p, b, i) for p, b in CHAINS]
            dns = [do_v[_rows(b, i), p * BQ:(p + 1) * BQ] for p, b in CHAINS]
            qts = [_tr(t) for t in qns]
            dts = [_tr(t) for t in dns]
            lses = [_bcast_heads(_stat_row(lse_ref, p, b, 0, i), _stat_row(lse_ref, p, b, 1, i)) for p, b in CHAINS]
            if fox:
                cqs = [_bcast_heads(_stat_row(cq_ref, p, b, 0, i), _stat_row(cq_ref, p, b, 1, i)) for p, b in CHAINS]
            else:
                cqs = [None] * NC
            deltas = []
            for c, (p, b) in enumerate(CHAINS):
                pt = (dns[c].astype(F32) * o_v[_rows(b, i), p * BQ:(p + 1) * BQ].astype(F32)).T
                deltas.append(_bcast_heads(jnp.sum(pt[:64], axis=0, keepdims=True), jnp.sum(pt[64:], axis=0, keepdims=True)))
            dqs = (jnp.zeros((BQ, BQ), F32),) * NC
            dqs = lax.fori_loop(0, i, lambda j, d: steps(qns, qts, dns, dts, cqs, lses, deltas, i, j, d, False), dqs)
            dqs = steps(qns, qts, dns, dts, cqs, lses, deltas, i, i, dqs, True)
            for c, (p, b) in enumerate(CHAINS):
                dq_v[_rows(b, i), p * PAIRW:p * PAIRW + BQ] = (dqs[c].T * SCALE).astype(BF16)
            return 0

        lax.fori_loop(0, NB, qblock, 0)
        for p in range(2):
            dq_v[:, p * PAIRW + BQ:p * PAIRW + 2 * BQ] = dk_s[p].astype(BF16)
            dq_v[:, p * PAIRW + 2 * BQ:p * PAIRW + 3 * BQ] = dv_s[p].astype(BF16)
        _copy_in(dq_v, dqkv_hbm.at[:, pl.ds(g * 2 * PAIRW, 2 * PAIRW)], sem)
        if fox:
            lane = _iota((BQ, NSTAT), 1)

            def fold(n, _):
                t = jnp.zeros((BQ, NSTAT), F32)
                for c, (p, b) in enumerate(CHAINS):
                    s = jnp.sum(dc_s[c, pl.ds(pl.multiple_of(n * HB, HB), HB), :], axis=1, keepdims=True)
                    col = b * NH + 2 * p
                    t = t - jnp.where(lane == col, s[:BQ], 0.0) - jnp.where(lane == col + 1, s[BQ:], 0.0)
                db_ref[pl.ds(pl.multiple_of(n * BQ, BQ), BQ), :] = t
                return 0

            lax.fori_loop(0, NB, fold, 0)
        if comm:
            comm[1](st)

    if fox:
        bias_specs = [VMEM_SPEC, ANY_SPEC]
        db_shape = jax.ShapeDtypeStruct((S, NSTAT), F32)
        more = [SLAB_KEYB, pltpu.VMEM((NC, NB * HB, BQ), F32)]
    else:
        bias_specs = [VMEM_SPEC]
        db_shape = jax.ShapeDtypeStruct((2, NB, HB, BQ), F32)
        more = []
    n_in = 5 + len(bias_specs)
    (dqkv, db), extra = _host_call(
        body, "fox_bwd" if fox else "dil_bwd", [qkv, o, dmixed, lse, *bias, dqkv],
        [ANY_SPEC, ANY_SPEC, ANY_SPEC, VMEM_SPEC] + bias_specs + [ANY_SPEC],
        [jax.ShapeDtypeStruct((T, QKVW), BF16), db_shape], [ANY_SPEC, VMEM_SPEC],
        [SLAB_QKV, SLAB_O32 if fox else SLAB_OUT, SLAB_OUT, SLAB_QKV, ACC_KV, ACC_KV, pltpu.SemaphoreType.DMA, SLAB_T]
        + more, {n_in - 1: 0}, job)
    return dqkv, db, extra


def _delta_t(d):
    return d * BQ + _iota((HB, BQ), 1) - (_iota((HB, BQ), 0) & (BQ - 1))


def _buckets_in(d):
    lo, hi = max(d * BQ - (BQ - 1), 0), d * BQ + BQ - 1
    return [b for b in range(32) if BUCKET_TH[b] <= hi and (b == 31 or BUCKET_TH[b + 1] > lo)]


def _in_bucket(delta, b):
    m = delta >= BUCKET_TH[b]
    return m if b == 31 else m & (delta < BUCKET_TH[b + 1])


def _dil_table(rel_bias):
    def body(rb_ref, o_ref):
        for d in range(NB):
            delta = _delta_t(d)
            pos = delta >= 0
            n = ((pos & (delta <= 128)).astype(jnp.int32)
                 + (pos & (delta <= 512) & ((delta & 3) == 0)).astype(jnp.int32)
                 + (pos & ((delta & 15) == 0)).astype(jnp.int32))
            logn = jnp.where(n == 3, math.log(3.0), jnp.where(n == 2, math.log(2.0), jnp.where(n == 1, 0.0, NEG)))
            head1 = _iota((HB, BQ), 0) >= BQ
            for p in range(2):
                val = jnp.zeros((HB, BQ), F32)
                for b in _buckets_in(d):
                    val = jnp.where(_in_bucket(delta, b), jnp.where(head1, rb_ref[b, 2 * p + 1], rb_ref[b, 2 * p]), val)
                o_ref[p, d] = val + logn

    return pl.pallas_call(
        body, name="dil_table", in_specs=[pl.BlockSpec(memory_space=pltpu.SMEM)], out_specs=VMEM_SPEC,
        out_shape=jax.ShapeDtypeStruct((2, NB, HB, BQ), F32), compiler_params=_cp())(rel_bias)


def _dil_table_bwd(dtbl):
    def body(dt_ref, o_ref):
        p = pl.program_id(0)
        rowi = _iota((32, BQ), 0)
        lanei = _iota((32, BQ), 1)

        @pl.when(p == 0)
        def _():
            o_ref[...] = jnp.zeros_like(o_ref)

        out = jnp.zeros((32, BQ), F32)
        for b in range(32):
            acc = None
            for d in range(NB):
                if b in _buckets_in(d):
                    t = jnp.where(_in_bucket(_delta_t(d), b), dt_ref[d], 0.0)
                    acc = t if acc is None else acc + t
            rs = jnp.sum(acc, axis=1, keepdims=True)
            s0 = jnp.sum(rs[:BQ], axis=0, keepdims=True)
            s1 = jnp.sum(rs[BQ:], axis=0, keepdims=True)
            out = (out + jnp.where((rowi == b) & (lanei == 2 * p), s0, 0.0)
                   + jnp.where((rowi == b) & (lanei == 2 * p + 1), s1, 0.0))
        o_ref[...] += out

    return pl.pallas_call(
        body, name="dil_table_bwd", grid=(2,),
        in_specs=[pl.BlockSpec((None, NB, HB, BQ), lambda p: (p, 0, 0, 0))],
        out_specs=pl.BlockSpec((32, BQ), lambda p: (0, 0)),
        out_shape=jax.ShapeDtypeStruct((32, BQ), F32),
        compiler_params=_cp(("arbitrary",)))(dtbl)


def _fox_prep(gate, fb):
    def body(g_ref, fb_ref, c_ref):
        tri = (_iota((BQ, BQ), 0) >= _iota((BQ, BQ), 1)).astype(BF16)

        def blk(i, carry):
            r0 = pl.multiple_of(i * BQ, BQ)
            lf = _log_sigmoid(g_ref[pl.ds(r0, BQ), :] + fb_ref[...])
            c = _dot(tri, _split3(lf))
            c_ref[pl.ds(r0, BQ), :] = c[:, 0:BQ] + c[:, BQ:2 * BQ] + c[:, 2 * BQ:3 * BQ] + carry
            return carry + jnp.sum(lf, axis=0, keepdims=True)

        lax.fori_loop(0, NB, blk, jnp.zeros((1, BQ), F32))

    blk = pl.BlockSpec((S, GATEW), lambda b: (b, 0))
    return pl.pallas_call(
        body, name="fox_prep", grid=(BL,), in_specs=[blk, pl.BlockSpec((1, GATEW), lambda b: (0, 0))],
        out_specs=blk, out_shape=jax.ShapeDtypeStruct((T, GATEW), F32),
        compiler_params=_cp(("parallel",)))(gate, fb)


def _fox_post(dcum, gate, fb):
    def body(dc_ref, g_ref, fb_ref, dg_ref, dfb_ref):
        b = pl.program_id(0)
        tri = (_iota((BQ, BQ), 0) <= _iota((BQ, BQ), 1)).astype(BF16)

        def blk(ii, carry):
            csum, dfb = carry
            r0 = pl.multiple_of((NB - 1 - ii) * BQ, BQ)
            dc = dc_ref[pl.ds(r0, BQ), :]
            c = _dot(tri, _split3(dc))
            dlf = c[:, 0:BQ] + c[:, BQ:2 * BQ] + c[:, 2 * BQ:3 * BQ] + csum
            dg = dlf * jnp.exp(_log_sigmoid(-(g_ref[pl.ds(r0, BQ), :] + fb_ref[...])))
            dg_ref[pl.ds(r0, BQ), :] = dg
            return csum + jnp.sum(dc, axis=0, keepdims=True), dfb + jnp.sum(dg, axis=0, keepdims=True)

        z = jnp.zeros((1, BQ), F32)
        _, dfb = lax.fori_loop(0, NB, blk, (z, z))

        @pl.when(b == 0)
        def _():
            dfb_ref[...] = dfb

        @pl.when(b > 0)
        def _():
            dfb_ref[...] += dfb

    blk = pl.BlockSpec((S, GATEW), lambda b: (b, 0))
    vec = pl.BlockSpec((1, GATEW), lambda b: (0, 0))
    return pl.pallas_call(
        body, name="fox_post", grid=(BL,), in_specs=[blk, blk, vec], out_specs=[blk, vec],
        out_shape=[jax.ShapeDtypeStruct((T, GATEW), F32), jax.ShapeDtypeStruct((1, GATEW), F32)],
        compiler_params=_cp(("arbitrary",)))(dcum, gate, fb)


def _shift_down(x, n):
    return jnp.where(_iota(x.shape, 0) >= n, pltpu.roll(x, n, 0), 0.0)


def _shift_up(x, n):
    return jnp.where(_iota(x.shape, 0) < S - n, pltpu.roll(x, S - n, 0), 0.0)


def _conv_fwd(conv, cw, mixed):
    W = 256

    def body(c_ref, w_ref, _, o_ref):
        u = c_ref[:, W:2 * W] * c_ref[:, 2 * W:3 * W]
        y = w_ref[0:1, :] * _shift_down(u, 2) + w_ref[1:2, :] * _shift_down(u, 1) + w_ref[2:3, :] * u
        o_ref[...] = (c_ref[:, 0:W] * y).astype(BF16)

    return pl.pallas_call(
        body, name="conv_fwd", grid=(BL,),
        in_specs=[pl.BlockSpec((S, CONVW), lambda b: (b, 0)), pl.BlockSpec((8, W), lambda b: (0, 0)), ANY_SPEC],
        out_specs=pl.BlockSpec((S, W), lambda b: (b, 3)),
        out_shape=jax.ShapeDtypeStruct((T, D), BF16), input_output_aliases={2: 0},
        compiler_params=_cp(("parallel",)))(conv, cw, mixed)


def _conv_bwd(conv, cw, dmixed):
    W = 256

    def body(c_ref, w_ref, do_ref, dc_ref, dw_ref):
        b = pl.program_id(0)
        bg = c_ref[:, 0:W]
        cg = c_ref[:, W:2 * W]
        hv = c_ref[:, 2 * W:3 * W]
        do = do_ref[...].astype(F32)
        u = cg * hv
        u1 = _shift_down(u, 1)
        u2 = _shift_down(u, 2)
        y = w_ref[0:1, :] * u2 + w_ref[1:2, :] * u1 + w_ref[2:3, :] * u
        dy = do * bg
        du = w_ref[2:3, :] * dy + w_ref[1:2, :] * _shift_up(dy, 1) + w_ref[0:1, :] * _shift_up(dy, 2)
        dc_ref[:, 0:W] = (do * y).astype(BF16)
        dc_ref[:, W:2 * W] = (du * hv).astype(BF16)
        dc_ref[:, 2 * W:3 * W] = (du * cg).astype(BF16)
        rowi = _iota((8, W), 0)
        dw = (jnp.where(rowi == 0, jnp.sum(dy * u2, axis=0, keepdims=True), 0.0)
              + jnp.where(rowi == 1, jnp.sum(dy * u1, axis=0, keepdims=True), 0.0)
              + jnp.where(rowi == 2, jnp.sum(dy * u, axis=0, keepdims=True), 0.0))

        @pl.when(b == 0)
        def _():
            dw_ref[...] = dw

        @pl.when(b > 0)
        def _():
            dw_ref[...] += dw

    return pl.pallas_call(
        body, name="conv_bwd", grid=(BL,),
        in_specs=[pl.BlockSpec((S, CONVW), lambda b: (b, 0)), pl.BlockSpec((8, W), lambda b: (0, 0)),
                  pl.BlockSpec((S, W), lambda b: (b, 3))],
        out_specs=[pl.BlockSpec((S, CONVW), lambda b: (b, 0)), pl.BlockSpec((8, W), lambda b: (0, 0))],
        out_shape=[jax.ShapeDtypeStruct((T, CONVW), BF16), jax.ShapeDtypeStruct((8, W), F32)],
        compiler_params=_cp(("arbitrary",)))(conv, cw, dmixed)


def _place():
    x, y, c = lax.axis_index("x"), lax.axis_index("y"), lax.axis_index("c")
    return x, y, c


def _chips_of(x, y):
    return [(1 - x, y), (x, 1 - y), (1 - x, 1 - y)]


def _dev(p):
    return 4 * p[0] + 2 * p[1] + p[2]


def _gather_job_a(shards):
    n = len(shards)

    def peers(x, y, c):
        return [(x, y, 1 - c)] + [(*chip, c) for chip in _chips_of(x, y)]

    def start(ins, outs, sems):
        send, recv, loc = sems
        x, y, c = _place()
        me = (x, y, c)
        cps = []
        for a in range(n):
            cps.append(pltpu.make_async_copy(ins[a], outs[a].at[_dev(me)], loc.at[a]))
            for k, peer in enumerate(peers(x, y, c)):
                cps.append(pltpu.make_async_remote_copy(
                    src_ref=ins[a], dst_ref=outs[a].at[_dev(me)], send_sem=send.at[a, k], recv_sem=recv.at[a, k],
                    device_id=peer, device_id_type=MESH))
        for cp in cps:
            cp.start()
        return cps

    def finish(cps, ins, outs, sems):
        send, recv, loc = sems
        x, y, c = _place()
        for a in range(n):
            for k, peer in enumerate(peers(x, y, c)):
                pltpu.make_async_remote_copy(
                    src_ref=ins[a], dst_ref=outs[a].at[_dev(peer)], send_sem=send.at[a, k], recv_sem=recv.at[a, k],
                    device_id=(x, y, c), device_id_type=MESH).wait_recv()
        for a in range(n):
            cps[5 * a].wait()
            for k in range(4):
                cps[5 * a + 1 + k].wait_send()

    return _Job(shards, [jax.ShapeDtypeStruct((NDEV,) + s.shape, s.dtype) for s in shards], {},
                [pltpu.SemaphoreType.DMA((n, 4)), pltpu.SemaphoreType.DMA((n, 4)), pltpu.SemaphoreType.DMA((n,))],
                start, finish)


def _gather_job_b(gathered):
    n = len(gathered)

    def start(ins, outs, sems):
        send, recv = sems
        x, y, c = _place()
        cps = []
        for a in range(n):
            for j, chip in enumerate(_chips_of(x, y)):
                blk = outs[a].at[_dev((*chip, c))]
                cps.append(pltpu.make_async_remote_copy(
                    src_ref=blk, dst_ref=blk, send_sem=send.at[a, j], recv_sem=recv.at[a, j],
                    device_id=(x, y, 1 - c), device_id_type=MESH))
        for cp in cps:
            cp.start()
        return cps

    def finish(cps, ins, outs, sems):
        send, recv = sems
        x, y, c = _place()
        for a in range(n):
            for j, chip in enumerate(_chips_of(x, y)):
                blk = outs[a].at[_dev((*chip, 1 - c))]
                pltpu.make_async_remote_copy(
                    src_ref=blk, dst_ref=blk, send_sem=send.at[a, j], recv_sem=recv.at[a, j],
                    device_id=(x, y, c), device_id_type=MESH).wait_recv()
        for cp in cps:
            cp.wait_send()

    return _Job(gathered, [jax.ShapeDtypeStruct(g.shape, g.dtype) for g in gathered], {a: a for a in range(n)},
                [pltpu.SemaphoreType.DMA((n, 3)), pltpu.SemaphoreType.DMA((n, 3))], start, finish)


def _sibling_job(grads):
    n = len(grads)

    def start(ins, outs, sems):
        send, recv = sems
        x, y, c = _place()
        cps = [pltpu.make_async_remote_copy(
            src_ref=ins[a].at[:, 1 - c], dst_ref=outs[a], send_sem=send.at[a], recv_sem=recv.at[a],
            device_id=(x, y, 1 - c), device_id_type=MESH) for a in range(n)]
        for cp in cps:
            cp.start()
        return cps

    def finish(cps, ins, outs, sems):
        for cp in cps:
            cp.wait()

    return _Job(grads, [jax.ShapeDtypeStruct(g.shape[:1] + g.shape[2:], F32) for g in grads], {},
                [pltpu.SemaphoreType.DMA((n,)), pltpu.SemaphoreType.DMA((n,))], start, finish)


def _chip_job(psums):
    n = len(psums)

    def start(ins, outs, sems):
        send, recv, loc = sems
        x, y, c = _place()
        mychip = 2 * x + y
        cps = []
        for a in range(n):
            cps.append(pltpu.make_async_copy(ins[a].at[mychip], outs[a].at[mychip], loc.at[a]))
            for j, chip in enumerate(_chips_of(x, y)):
                cps.append(pltpu.make_async_remote_copy(
                    src_ref=ins[a].at[2 * chip[0] + chip[1]], dst_ref=outs[a].at[mychip],
                    send_sem=send.at[a, j], recv_sem=recv.at[a, j], device_id=(*chip, c), device_id_type=MESH))
        for cp in cps:
            cp.start()
        return cps

    def finish(cps, ins, outs, sems):
        send, recv, loc = sems
        x, y, c = _place()
        mychip = 2 * x + y
        for a in range(n):
            for j, chip in enumerate(_chips_of(x, y)):
                pltpu.make_async_remote_copy(
                    src_ref=ins[a].at[mychip], dst_ref=outs[a].at[2 * chip[0] + chip[1]],
                    send_sem=send.at[a, j], recv_sem=recv.at[a, j], device_id=(x, y, c), device_id_type=MESH).wait_recv()
        for a in range(n):
            cps[4 * a].wait()
            for j in range(3):
                cps[4 * a + 1 + j].wait_send()

    return _Job(psums, [jax.ShapeDtypeStruct(p.shape, BF16) for p in psums], {},
                [pltpu.SemaphoreType.DMA((n, 3)), pltpu.SemaphoreType.DMA((n, 3)), pltpu.SemaphoreType.DMA((n,))],
                start, finish)


def _join_jobs(*jobs):
    jobs = [j for j in jobs if j is not None]
    if len(jobs) <= 1:
        return jobs[0] if jobs else None
    cut = lambda seq, sizes: [seq[sum(sizes[:k]):sum(sizes[:k + 1])] for k in range(len(sizes))]
    n_in = [len(j.ins) for j in jobs]
    n_out = [len(j.out_shapes) for j in jobs]
    n_sem = [len(j.sems) for j in jobs]
    aliases = {}
    for k, j in enumerate(jobs):
        for a, b in j.aliases.items():
            aliases[sum(n_in[:k]) + a] = sum(n_out[:k]) + b

    def start(ins, outs, sems):
        return [j.start(i, o, s) for j, i, o, s in zip(jobs, cut(ins, n_in), cut(outs, n_out), cut(sems, n_sem))]

    def finish(sts, ins, outs, sems):
        for j, st, i, o, s in zip(jobs, sts, cut(ins, n_in), cut(outs, n_out), cut(sems, n_sem)):
            j.finish(st, i, o, s)

    return _Job([t for j in jobs for t in j.ins], [t for j in jobs for t in j.out_shapes], aliases,
                [t for j in jobs for t in j.sems], start, finish)


def _run_job(job, name):
    def body(ins, outs, scr, comm):
        comm[1](comm[0]())

    return _host_call(body, name, [], [], [], [], [], {}, job)[1]


def _allreduce_small(v):
    def body(v_ref, o_ref, slots, send_sems, recv_sems):
        x, y, c = _place()
        me = 4 * x + 2 * y + c
        slots[me] = v_ref[...]

        def copy(k):
            peer = (x ^ ((k >> 2) & 1), y ^ ((k >> 1) & 1), c ^ (k & 1))
            return pltpu.make_async_remote_copy(
                src_ref=v_ref, dst_ref=slots.at[me], send_sem=send_sems.at[k - 1], recv_sem=recv_sems.at[k - 1],
                device_id=peer, device_id_type=MESH)

        def arrival(k):
            return pltpu.make_async_remote_copy(
                src_ref=v_ref, dst_ref=slots.at[me ^ k], send_sem=send_sems.at[k - 1], recv_sem=recv_sems.at[k - 1],
                device_id=(x, y, c), device_id_type=MESH)

        sends = [copy(k) for k in range(1, NDEV)]
        for cp in sends:
            cp.start()
        for k in range(1, NDEV):
            arrival(k).wait_recv()
        for cp in sends:
            cp.wait_send()
        acc = slots[0]
        for d in range(1, NDEV):
            acc = acc + slots[d]
        o_ref[...] = acc

    return pl.pallas_call(
        body, name="allreduce_small", in_specs=[VMEM_SPEC], out_specs=VMEM_SPEC,
        out_shape=jax.ShapeDtypeStruct(v.shape, F32),
        scratch_shapes=[pltpu.VMEM((NDEV,) + v.shape, F32), pltpu.SemaphoreType.DMA((NDEV - 1,)),
                        pltpu.SemaphoreType.DMA((NDEV - 1,))],
        )(v)


def _pair_sums(views, gots, core):
    n = len(views)

    def body(c_ref, *refs):
        for a in range(n):
            refs[2 * n + a][...] = (refs[a][...] + refs[n + a][...]).astype(BF16)

    def vspec(v):
        return pl.BlockSpec((None, None) + v.shape[2:], lambda k, c: (k, c[0], 0, 0))

    def gspec(g):
        return pl.BlockSpec((None,) + g.shape[1:], lambda k, c: (k, 0, 0))

    return pl.pallas_call(
        body, name="pair_sums",
        grid_spec=pltpu.PrefetchScalarGridSpec(
            num_scalar_prefetch=1, grid=(4,),
            in_specs=[vspec(v) for v in views] + [gspec(g) for g in gots],
            out_specs=[gspec(g) for g in gots]),
        out_shape=[jax.ShapeDtypeStruct(g.shape, BF16) for g in gots],
        compiler_params=_cp(("parallel",)))(core, *views, *gots)


def _chip_sums(parts):
    n = len(parts)

    def body(*refs):
        for a in range(n):
            acc = refs[a][0].astype(F32)
            for k in range(1, 4):
                acc = acc + refs[a][k].astype(F32)
            refs[n + a][...] = acc

    return pl.pallas_call(
        body, name="chip_sums", in_specs=[VMEM_SPEC] * n, out_specs=[VMEM_SPEC] * n,
        out_shape=[jax.ShapeDtypeStruct(p.shape[1:], F32) for p in parts], compiler_params=_cp())(*parts)


def _permute_in(w):
    lead = w.shape[:-1]
    return w.reshape(lead + (3, 3, 2, BQ)).swapaxes(-2, -3).reshape(lead + (QKVW,))


def _unpermute_in(w):
    lead = w.shape[:-1]
    return w.reshape(lead + (3, 2, 3, BQ)).swapaxes(-2, -3).reshape(lead + (QKVW,))


def _row(v):
    v = v.reshape(-1)
    return jnp.pad(v, (0, D - v.shape[0])).reshape(1, D)


def kernel(x, w_in, f_bias, conv_w, w_out, rel_bias, ln1_g, ln1_b, w_gate, w_up, w_down, ln2_g, ln2_b, loss_target, m_w_in, m_f_bias, m_conv_w, m_w_out, m_rel_bias, m_ln1_g, m_ln1_b, m_w_gate, m_w_up, m_w_down, m_ln2_g, m_ln2_b, v_w_in, v_f_bias, v_conv_w, v_w_out, v_rel_bias, v_ln1_g, v_ln1_b, v_w_gate, v_w_up, v_w_down, v_ln2_g, v_ln2_b):
    xi, yi, ci = _place()
    me = 4 * xi + 2 * yi + ci
    core = jnp.reshape(ci, (1,)).astype(jnp.int32)

    win_s = jnp.concatenate([_permute_in(w_in[..., :QKVW]), w_in[..., QKVW:]], axis=-1)
    win_s = jnp.pad(win_s, ((0, 0), (0, 0), (0, NPAD - NPROJ))).astype(BF16)
    per_layer = [win_s, w_out.astype(BF16), jnp.swapaxes(w_gate, 1, 2).astype(BF16),
                 jnp.swapaxes(w_up, 1, 2).astype(BF16), w_down.astype(BF16)]
    sh = [[s[l] for s in per_layer] for l in range(2)]

    def whole(g):
        return g.reshape(NDEV * g.shape[1], g.shape[2])

    first = _run_job(_gather_job_b(_run_job(_gather_job_a(sh[0][:1]), "gather_a")), "gather_b")
    W = [{"win": whole(first[0])}, {}]

    cw_rows = lax.dynamic_update_slice(jnp.zeros((2, 3, 256), F32), conv_w, (0, 0, me * 32))
    small = jnp.concatenate([_row(cw_rows[0]), _row(cw_rows[1]), jnp.zeros((SMALL_ROWS - 2, D), F32)], axis=0)
    small = _allreduce_small(small)
    cw_full = small[0:2, :CONVW].reshape(2, 3, 256)
    cw8 = jnp.pad(cw_full, ((0, 0), (0, 5), (0, 0)))
    fb = jnp.pad(f_bias, ((0, 0), (0, GATEW - NH))).reshape(2, 1, GATEW)
    tbl = _dil_table(rel_bias)

    def wcol(K, tn, off):
        return pl.BlockSpec((K, tn), lambda i, j: (0, off + j))

    def wrow(tn, K, blk=0):
        return pl.BlockSpec((tn, K), lambda i, j: (j, blk))

    def arow(tm, K, blk=0):
        return pl.BlockSpec((tm, K), lambda i, j: (i, blk))

    h = x.reshape(T, D)
    hb = h.astype(BF16)
    saved = []
    for l in range(2):
        Win = W[l]["win"]
        qkv = _mm([(hb, arow(1024, D), Win, wcol(D, 768, 0))], nt=False, M=T, N=QKVW, tm=1024, tn=768,
                  out_dtype=BF16, name="proj_qkv")
        conv = _mm([(hb, arow(512, D), Win, wcol(D, 768, 3))], nt=False, M=T, N=CONVW, tm=512, tn=768,
                   out_dtype=F32, name="proj_conv")
        gate = _mm([(hb, arow(512, D), Win, wcol(D, 128, 24))], nt=False, M=T, N=GATEW, tm=512, tn=128,
                   out_dtype=F32, name="proj_gate")
        cum = _fox_prep(gate, fb[l])
        cq = cum[:, :NH].reshape(BL, S, NH).transpose(0, 2, 1).reshape(NSTAT, S)
        ckb = jnp.broadcast_to(cq[:, :, None], (NSTAT, S, BQ))
        if l == 0:
            mixed, rtot, a0 = _sb_fwd(qkv, job=_gather_job_a(sh[0][1:]))
            mixed, lse_d, ex = _flash_fwd(qkv, mixed, 1, False, (tbl,),
                                          job=_join_jobs(_gather_job_b(list(a0)), _gather_job_a(sh[1][:2])))
            W[0].update(zip(("wout", "wgT", "wuT", "wd"), [whole(t) for t in ex[:4]]))
            mixed, lse_f, o_fox, ex = _flash_fwd(qkv, mixed, 2, True, (cq, ckb),
                                                 job=_join_jobs(_gather_job_b(list(ex[4:])), _gather_job_a(sh[1][2:])))
            W[1].update(zip(("win", "wout"), [whole(t) for t in ex[:2]]))
            a2 = list(ex[2:])
        else:
            mixed, rtot, ex = _sb_fwd(qkv, job=_gather_job_b(a2))
            W[1].update(zip(("wgT", "wuT", "wd"), [whole(t) for t in ex]))
            mixed, lse_d, _ = _flash_fwd(qkv, mixed, 1, False, (tbl,))
            mixed, lse_f, o_fox, _ = _flash_fwd(qkv, mixed, 2, True, (cq, ckb))
        Wout, WgT, WuT, Wd = W[l]["wout"], W[l]["wgT"], W[l]["wuT"], W[l]["wd"]
        mixed = _conv_fwd(conv, cw8[l], mixed)
        x1, xh1, r1, x1b = _mm_ln(mixed, Wout, h, ln1_g[l:l + 1], ln1_b[l:l + 1], "out_proj_ln")
        fs, ft, a, x2, xh2, r2, x2b = _ffn_fwd(x1b, x1, WgT, WuT, Wd, ln2_g[l:l + 1], ln2_b[l:l + 1])
        saved.append(dict(h=hb, qkv=qkv, conv=conv, gate=gate, cq=cq, ckb=ckb, mixed=mixed, rtot=rtot, lse_d=lse_d,
                          lse_f=lse_f, o_fox=o_fox, x1=x1b, xh1=xh1, r1=r1, fs=fs, ft=ft, a=a, xh2=xh2, r2=r2))
        h, hb = x2, x2b

    sq, dy = _loss_grad(h, loss_target.reshape(T, D))
    loss = lax.psum(sq[0, 0], ("x", "y", "c")) * (0.5 / D)

    def view(gr):
        return gr.reshape(4, 2, gr.shape[0] // NDEV, gr.shape[1])

    G = [None, None]
    small_g = {}
    shard_g = {}
    for l in (1, 0):
        sv = saved[l]
        Win, Wout, WgT, WuT, Wd = W[l]["win"], W[l]["wout"], W[l]["wgT"], W[l]["wuT"], W[l]["wd"]
        ds2, dg2, db2, ds2b = _ln_bwd(dy, sv["xh2"], sv["r2"], ln2_g[l:l + 1])
        dgt, dut, dx1 = _ffn_bwd(ds2b, ds2, sv["fs"], sv["ft"], Wd, WgT, WuT)
        G_d = _mm_tn(sv["a"], ds2b, None, C=D, Ka=DFF, N=D, tm=1408, tn=1024, tk=1024, ooff=0, name="grad_w_down")
        G_g = _mm_tn(dgt, sv["x1"], None, C=D, Ka=DFF, N=D, tm=1408, tn=1024, tk=1024, ooff=0, name="grad_w_gate")
        G_u = _mm_tn(dut, sv["x1"], None, C=D, Ka=DFF, N=D, tm=1408, tn=1024, tk=1024, ooff=0, name="grad_w_up")
        ds1, dg1, db1, ds1b = _ln_bwd(dx1, sv["xh1"], sv["r1"], ln1_g[l:l + 1])
        G_out = _mm_tn(sv["mixed"], ds1b, None, C=D, Ka=D, N=D, tm=1024, tn=1024, tk=1024, ooff=0, name="grad_w_out")
        dmixed = _mm([(ds1b, arow(512, D), Wout, wrow(512, D))], nt=True, M=T, N=D, tm=512, tn=512,
                     out_dtype=BF16, name="out_proj_dx")
        early = [view(t) for t in (G_g, G_u, G_d, G_out)] + ([view(G[1]["in"])] if l == 0 else [])
        dqkv, gots = _sb_bwd(sv["qkv"], dmixed, sv["rtot"], job=_sibling_job(early))
        ps = _pair_sums(early, list(gots), core)
        dqkv, dtbl, pa = _flash_bwd(sv["qkv"], sv["mixed"], dmixed, sv["lse_d"], dqkv, 1, False, (tbl,),
                                    job=_chip_job(ps[:2]))
        dqkv, dck, pb = _flash_bwd(sv["qkv"], sv["o_fox"], dmixed, sv["lse_f"], dqkv, 2, True,
                                   (sv["cq"], sv["ckb"]), job=_chip_job(ps[2:]))
        sums = _chip_sums(list(pa) + list(pb))
        shard_g[l] = dict(zip(("g", "u", "d", "out"), sums[:4]))
        if l == 0:
            shard_g[1]["in"] = sums[4]
        dconv, dcw = _conv_bwd(sv["conv"], cw8[l], dmixed)
        dcum = jnp.pad(dck.reshape(S, BL, NH).transpose(1, 0, 2).reshape(T, NH), ((0, 0), (0, GATEW - NH)))
        dgate, dfb = _fox_post(dcum, sv["gate"], fb[l])
        drb = _dil_table_bwd(dtbl)
        G_in = _mm_tn(sv["h"], dqkv, None, C=NPAD, Ka=D, N=QKVW, tm=1024, tn=768, tk=1024, ooff=0, name="grad_w_in_qkv")
        G_in = _mm_tn(sv["h"], dconv, G_in, C=NPAD, Ka=D, N=CONVW, tm=1024, tn=768, tk=1024, ooff=3,
                      name="grad_w_in_conv")
        G_in = _mm_tn(sv["h"], dgate, G_in, C=NPAD, Ka=D, N=GATEW, tm=1024, tn=128, tk=1024, ooff=24,
                      name="grad_w_in_gate")
        G[l] = {"in": G_in, "out": G_out, "g": G_g, "u": G_u, "d": G_d}
        dy = _mm([(dqkv, arow(1024, QKVW), Win, wrow(512, QKVW, 0)),
                  (dconv, arow(1024, CONVW), Win, wrow(512, CONVW, 3)),
                  (dgate, arow(1024, GATEW), Win, wrow(512, GATEW, 24))],
                 nt=True, M=T, N=D, tm=1024, tn=512, out_dtype=F32, name="proj_dx", res=ds1, res_scale=ALPHA)
        small_g[l] = dict(ln1_g=dg1, ln1_b=db1, ln2_g=dg2, ln2_b=db2, cw=dcw[0:3].reshape(1, CONVW),
                          fb=dfb[:, :NH], rb=drb[:, :NH])
    grad_x = dy.reshape(BL, S, D)

    late = [view(G[0]["in"])]
    gots = list(_run_job(_sibling_job(late), "sibling_exchange"))
    shard_g[0]["in"] = _chip_sums(_run_job(_chip_job(_pair_sums(late, gots, core)), "chip_exchange"))[0]

    rows = []
    for name in ("ln1_g", "ln1_b", "ln2_g", "ln2_b"):
        rows += [small_g[0][name], small_g[1][name]]
    rows += [_row(small_g[0]["cw"]), _row(small_g[1]["cw"]),
             _row(jnp.concatenate([small_g[0]["fb"], small_g[1]["fb"]], axis=0)),
             _row(small_g[0]["rb"] + small_g[1]["rb"])]
    rows.append(jnp.zeros((SMALL_ROWS - len(rows), D), F32))
    sg = _allreduce_small(jnp.concatenate(rows, axis=0))
    g_ln1_g, g_ln1_b, g_ln2_g, g_ln2_b = sg[0:2], sg[2:4], sg[4:6], sg[6:8]
    g_conv_full = sg[8:10, :CONVW].reshape(2, 3, 256)
    g_conv = lax.dynamic_slice(g_conv_full, (0, 0, me * 32), (2, 3, 32))
    g_fb = sg[10, :2 * NH].reshape(2, NH)
    g_rb = sg[11, :32 * NH].reshape(32, NH)

    def both(name):
        return jnp.stack([shard_g[0][name], shard_g[1][name]])

    g_in = both("in")
    g_w_in = jnp.concatenate([_unpermute_in(g_in[..., :QKVW]), g_in[..., QKVW:NPROJ]], axis=-1)
    g_w_out = both("out")
    g_w_gate = jnp.swapaxes(both("g"), 1, 2)
    g_w_up = jnp.swapaxes(both("u"), 1, 2)
    g_w_down = both("d")

    up_in = _adamw(w_in, g_w_in, m_w_in, v_w_in, 64)
    up_out = _adamw(w_out, g_w_out, m_w_out, v_w_out, 128)
    up_gate = _adamw(w_gate, g_w_gate, m_w_gate, v_w_gate, 256)
    up_up = _adamw(w_up, g_w_up, m_w_up, v_w_up, 256)
    up_down = _adamw(w_down, g_w_down, m_w_down, v_w_down, 352)

    def pack(fbv, cwv, rbv, l1g, l1b, l2g, l2b):
        r = [l1g, l1b, l2g, l2b, _row(cwv), _row(fbv), _row(rbv)]
        r.append(jnp.zeros((SMALL_ROWS - 11, D), F32))
        return jnp.concatenate(r, axis=0)

    pw = pack(f_bias, conv_w, rel_bias, ln1_g, ln1_b, ln2_g, ln2_b)
    pg = pack(g_fb, g_conv, g_rb, g_ln1_g, g_ln1_b, g_ln2_g, g_ln2_b)
    pm = pack(m_f_bias, m_conv_w, m_rel_bias, m_ln1_g, m_ln1_b, m_ln2_g, m_ln2_b)
    pv = pack(v_f_bias, v_conv_w, v_rel_bias, v_ln1_g, v_ln1_b, v_ln2_g, v_ln2_b)
    ups = [u[0] for u in _adamw(pw[None], pg[None], pm[None], pv[None], SMALL_ROWS)]

    def unpack(p):
        return dict(ln1_g=p[0:2], ln1_b=p[2:4], ln2_g=p[4:6], ln2_b=p[6:8],
                    conv_w=p[8, :192].reshape(2, 3, 32), f_bias=p[9, :2 * NH].reshape(2, NH),
                    rel_bias=p[10, :32 * NH].reshape(32, NH))

    sm = [unpack(p) for p in ups]

    def group(k):
        return (up_in[k], sm[k]["f_bias"], sm[k]["conv_w"], up_out[k], sm[k]["rel_bias"], sm[k]["ln1_g"],
                sm[k]["ln1_b"], up_gate[k], up_up[k], up_down[k], sm[k]["ln2_g"], sm[k]["ln2_b"])

    grads = (g_w_in, g_fb, g_conv, g_w_out, g_rb, g_ln1_g, g_ln1_b, g_w_gate, g_w_up, g_w_down, g_ln2_g, g_ln2_b)
    return (loss, grad_x) + grads + group(0) + group(1) + group(2)
```

```python
import math

import numpy as np
import jax
import jax.numpy as jnp
from jax import lax
from jax.experimental import pallas as pl
from jax.experimental.pallas import tpu as pltpu

F32 = jnp.float32
BF16 = jnp.bfloat16
MESH = pl.DeviceIdType.MESH

D = 1024
S = 2048
BL = 2
T = BL * S
NH = 4
DFF = 2816
NPROJ = 3076
NPAD = 3200
QKVW = 2304
CONVW = 768
GATEW = 128
PAIRW = 384
BQ = 128
HB = 2 * BQ
NB = S // BQ
NDEV = 8
NSTAT = BL * NH
ALPHA = 4.0 ** 0.25
SCALE = 0.125
NEG = -1e30
LN_EPS = 1e-5
ADAM_LR, ADAM_B1, ADAM_B2, ADAM_EPS, ADAM_WD, ADAM_STEP = 0.001, 0.9, 0.999, 1e-08, 0.01, 10
VMEM_LIMIT = 56 * 1024 * 1024
SMALL_ROWS = 16


def _bucket_thresholds():
    d = np.arange(0, S)
    nf = np.maximum(d, 1).astype(np.float32)
    large = 16 + (np.log(nf / np.float32(16)) / np.float32(math.log(128)) * np.float32(16)).astype(np.int32)
    b = np.where(d < 16, d, np.minimum(large, 31))
    return [int(np.argmax(b >= k)) for k in range(32)]


BUCKET_TH = _bucket_thresholds()


def _cp(sem=None):
    return pltpu.CompilerParams(dimension_semantics=sem, vmem_limit_bytes=VMEM_LIMIT)


def _dot(a, b):
    return lax.dot_general(a, b, (((1,), (0,)), ((), ())), preferred_element_type=F32)


def _dot_nt(a, b):
    return lax.dot_general(a, b, (((1,), (1,)), ((), ())), preferred_element_type=F32)


def _dot_tn(a, b):
    return lax.dot_general(a, b, (((0,), (0,)), ((), ())), preferred_element_type=F32)


def _split2(x):
    hi = x.astype(BF16)
    mid = (x - hi.astype(F32)).astype(BF16)
    return jnp.concatenate([hi, mid], axis=1)


def _split3(x):
    hi = x.astype(BF16)
    r = x - hi.astype(F32)
    mid = r.astype(BF16)
    lo = (r - mid.astype(F32)).astype(BF16)
    return jnp.concatenate([hi, mid, lo], axis=1)


def _log_sigmoid(u):
    return jnp.minimum(u, 0.0) - jnp.log1p(jnp.exp(-jnp.abs(u)))


def _log_sigmoid_tile(u):
    return jnp.minimum(u, 0.0) - jnp.log(1.0 + jnp.exp(jnp.minimum(u, -u)))


def _iota(shape, dim):
    return lax.broadcasted_iota(jnp.int32, shape, dim)


ANY_SPEC = pl.BlockSpec(memory_space=pl.ANY)
VMEM_SPEC = pl.BlockSpec(memory_space=pltpu.VMEM)


def _mm(pairs, *, nt, M, N, tm, tn, out_dtype, name, res=None, res_scale=1.0, job=None):
    n = len(pairs)
    n_in = 2 * n + (res is not None)
    jins = job.ins if job else []
    jouts = job.out_shapes if job else []
    gi, gj = M // tm, N // tn

    def body(*refs):
        o_ref = refs[n_in + len(jins)]
        if job:
            jrefs = (refs[n_in:n_in + len(jins)], refs[n_in + len(jins) + 1:n_in + len(jins) + 1 + len(jouts)],
                     refs[n_in + len(jins) + 1 + len(jouts):])

            @pl.when((pl.program_id(0) == 0) & (pl.program_id(1) == 0))
            def _():
                job.start(*jrefs)

        acc = None
        for p in range(n):
            a = refs[2 * p][...].astype(BF16)
            b = refs[2 * p + 1][...]
            d = _dot_nt(a, b) if nt else _dot(a, b)
            acc = d if acc is None else acc + d
        if res is not None:
            acc = acc + res_scale * refs[2 * n][...]
        o_ref[...] = acc.astype(out_dtype)
        if job:
            @pl.when((pl.program_id(0) == gi - 1) & (pl.program_id(1) == gj - 1))
            def _():
                job.finish(None, *jrefs)

    ops, specs = [], []
    for a, asp, b, bsp in pairs:
        ops += [a, b]
        specs += [asp, bsp]
    if res is not None:
        ops.append(res)
        specs.append(pl.BlockSpec((tm, tn), lambda i, j: (i, j)))
    out = pl.pallas_call(
        body, name=name, grid=(gi, gj), in_specs=specs + [ANY_SPEC] * len(jins),
        out_specs=[pl.BlockSpec((tm, tn), lambda i, j: (i, j))] + [ANY_SPEC] * len(jouts),
        out_shape=[jax.ShapeDtypeStruct((M, N), out_dtype)] + list(jouts),
        scratch_shapes=list(job.sems) if job else [],
        input_output_aliases={n_in + a: 1 + b for a, b in job.aliases.items()} if job else {},
        compiler_params=_cp(("arbitrary", "arbitrary") if job else ("parallel", "parallel")))(*ops, *jins)
    return (out[0], out[1:]) if job else out[0]


def _mm_tn(a, b, gbuf, *, C, Ka, N, tm, tn, tk, ooff, name):
    def body(*refs):
        a_ref, b_ref, o_ref = refs[0], refs[1], refs[-1]
        k = pl.program_id(2)
        d = _dot_tn(a_ref[...].astype(BF16), b_ref[...].astype(BF16))

        @pl.when(k == 0)
        def _():
            o_ref[...] = d

        @pl.when(k > 0)
        def _():
            o_ref[...] += d

    ops = [a, b] + ([] if gbuf is None else [gbuf])
    return pl.pallas_call(
        body, name=name, grid=(Ka // tm, N // tn, T // tk),
        in_specs=[pl.BlockSpec((tk, tm), lambda i, j, k: (k, i)),
                  pl.BlockSpec((tk, tn), lambda i, j, k: (k, j))] + ([] if gbuf is None else [ANY_SPEC]),
        out_specs=pl.BlockSpec((tm, tn), lambda i, j, k: (i, ooff + j)),
        out_shape=jax.ShapeDtypeStruct((Ka, C), F32),
        input_output_aliases={} if gbuf is None else {2: 0},
        compiler_params=_cp(("parallel", "parallel", "arbitrary")))(*ops)


def _ffn_up(x1, wgt, wut):
    tm, tn = 1024, 256

    def body(x_ref, wg_ref, wu_ref, g_ref, u_ref, a_ref):
        ch = 256
        for r in range(0, tm, ch):
            xb = x_ref[r:r + ch, :]
            g = _dot_nt(xb, wg_ref[...])
            u = _dot_nt(xb, wu_ref[...])
            g_ref[r:r + ch, :] = g.astype(BF16)
            u_ref[r:r + ch, :] = u.astype(BF16)
            a_ref[r:r + ch, :] = (g * jax.nn.sigmoid(g) * u).astype(BF16)

    wspec = pl.BlockSpec((tn, D), lambda i, j: (j, 0))
    ospec = pl.BlockSpec((tm, tn), lambda i, j: (i, j))
    return pl.pallas_call(
        body, name="ffn_up", grid=(T // tm, DFF // tn),
        in_specs=[pl.BlockSpec((tm, D), lambda i, j: (i, 0)), wspec, wspec],
        out_specs=[ospec, ospec, ospec],
        out_shape=[jax.ShapeDtypeStruct((T, DFF), BF16)] * 3,
        compiler_params=_cp(("parallel", "parallel")))(x1, wgt, wut)


def _ffn_da(dffn, wd, s, t):
    tm, tn = 1024, 256

    def body(d_ref, wd_ref, s_ref, t_ref, dg_ref, du_ref):
        ch = 256
        for r in range(0, tm, ch):
            da = _dot_nt(d_ref[r:r + ch, :], wd_ref[...])
            gv = s_ref[r:r + ch, :].astype(F32)
            sg = jax.nn.sigmoid(gv)
            dg_ref[r:r + ch, :] = (da * t_ref[r:r + ch, :].astype(F32) * (sg * (1.0 + gv * (1.0 - sg)))).astype(BF16)
            du_ref[r:r + ch, :] = (da * (gv * sg)).astype(BF16)

    ospec = pl.BlockSpec((tm, tn), lambda i, j: (i, j))
    return pl.pallas_call(
        body, name="ffn_da", grid=(T // tm, DFF // tn),
        in_specs=[pl.BlockSpec((tm, D), lambda i, j: (i, 0)),
                  pl.BlockSpec((tn, D), lambda i, j: (j, 0)), ospec, ospec],
        out_specs=[ospec, ospec],
        out_shape=[jax.ShapeDtypeStruct((T, DFF), BF16), jax.ShapeDtypeStruct((T, DFF), BF16)],
        compiler_params=_cp(("parallel", "parallel")))(dffn, wd, s, t)


def _ffn_fwd(xb, x, wgt, wut, wd, gam, bet):
    tm, ch = 512, 256

    def body(xb_ref, x_ref, g_ref, b_ref, wg_hbm, wu_hbm, wd_hbm,
             go_ref, uo_ref, ao_ref, y_ref, xh_ref, r_ref, yb_ref, wg_v, wu_v, wd_v, sem):
        @pl.when(pl.program_id(0) == 0)
        def _():
            _copy_in(wg_hbm, wg_v, sem)
            _copy_in(wu_hbm, wu_v, sem)
            _copy_in(wd_hbm, wd_v, sem)

        xv = xb_ref[...]
        for c in range(0, DFF, ch):
            gv = _dot_nt(xv, wg_v[c:c + ch, :])
            uv = _dot_nt(xv, wu_v[c:c + ch, :])
            go_ref[:, c:c + ch] = gv.astype(BF16)
            uo_ref[:, c:c + ch] = uv.astype(BF16)
            ao_ref[:, c:c + ch] = (gv * jax.nn.sigmoid(gv) * uv).astype(BF16)
        s = ALPHA * x_ref[...] + _dot(ao_ref[...], wd_v[...])
        mu = jnp.mean(s, axis=-1, keepdims=True)
        xc = s - mu
        var = jnp.mean(xc * xc, axis=-1, keepdims=True)
        r = lax.rsqrt(var + LN_EPS)
        xh = xc * r
        xh_ref[...] = xh.astype(BF16)
        r_ref[...] = r
        y = xh * g_ref[...] + b_ref[...]
        y_ref[...] = y
        yb_ref[...] = y.astype(BF16)

    row = pl.BlockSpec((tm, D), lambda i: (i, 0))
    wide = pl.BlockSpec((tm, DFF), lambda i: (i, 0))
    vec = pl.BlockSpec((1, D), lambda i: (0, 0))
    wsl = pltpu.VMEM((DFF, D), BF16)
    hid = jax.ShapeDtypeStruct((T, DFF), BF16)
    return pl.pallas_call(
        body, name="ffn_fwd", grid=(T // tm,),
        in_specs=[row, row, vec, vec, ANY_SPEC, ANY_SPEC, ANY_SPEC],
        out_specs=[wide, wide, wide, row, row, pl.BlockSpec((tm, 1), lambda i: (i, 0)), row],
        out_shape=[hid, hid, hid, jax.ShapeDtypeStruct((T, D), F32), jax.ShapeDtypeStruct((T, D), BF16),
                   jax.ShapeDtypeStruct((T, 1), F32), jax.ShapeDtypeStruct((T, D), BF16)],
        scratch_shapes=[wsl, wsl, wsl, pltpu.SemaphoreType.DMA],
        compiler_params=_cp(("arbitrary",)))(xb, x, gam, bet, wgt, wut, wd)


def _ffn_bwd(dffn, res, g, u, wd, wgt, wut):
    tm, ch = 512, 256

    def body(d_ref, r_ref, g_ref, u_ref, wd_hbm, wg_hbm, wu_hbm, dg_ref, du_ref, dx_ref, wd_v, wg_v, wu_v, sem):
        @pl.when(pl.program_id(0) == 0)
        def _():
            _copy_in(wd_hbm, wd_v, sem)
            _copy_in(wg_hbm, wg_v, sem)
            _copy_in(wu_hbm, wu_v, sem)

        db = d_ref[...]
        for c in range(0, DFF, ch):
            da = _dot_nt(db, wd_v[c:c + ch, :])
            gv = g_ref[:, c:c + ch].astype(F32)
            sg = jax.nn.sigmoid(gv)
            dg_ref[:, c:c + ch] = (da * u_ref[:, c:c + ch].astype(F32) * (sg * (1.0 + gv * (1.0 - sg)))).astype(BF16)
            du_ref[:, c:c + ch] = (da * (gv * sg)).astype(BF16)
        dx_ref[...] = ALPHA * r_ref[...] + _dot(dg_ref[...], wg_v[...]) + _dot(du_ref[...], wu_v[...])

    row = pl.BlockSpec((tm, D), lambda i: (i, 0))
    wide = pl.BlockSpec((tm, DFF), lambda i: (i, 0))
    wsl = pltpu.VMEM((DFF, D), BF16)
    return pl.pallas_call(
        body, name="ffn_bwd", grid=(T // tm,),
        in_specs=[row, row, wide, wide, ANY_SPEC, ANY_SPEC, ANY_SPEC], out_specs=[wide, wide, row],
        out_shape=[jax.ShapeDtypeStruct((T, DFF), BF16), jax.ShapeDtypeStruct((T, DFF), BF16),
                   jax.ShapeDtypeStruct((T, D), F32)],
        scratch_shapes=[wsl, wsl, wsl, pltpu.SemaphoreType.DMA],
        compiler_params=_cp(("arbitrary",)))(dffn, res, g, u, wd, wgt, wut)


def _mm_ln(a, w, x, gam, bet, name):
    tm = 256
    K = a.shape[1]

    def body(a_ref, w_ref, x_ref, g_ref, b_ref, y_ref, xh_ref, r_ref, yb_ref):
        s = ALPHA * x_ref[...] + _dot(a_ref[...], w_ref[...])
        mu = jnp.mean(s, axis=-1, keepdims=True)
        xc = s - mu
        var = jnp.mean(xc * xc, axis=-1, keepdims=True)
        r = lax.rsqrt(var + LN_EPS)
        xh = xc * r
        xh_ref[...] = xh.astype(BF16)
        r_ref[...] = r
        y = xh * g_ref[...] + b_ref[...]
        y_ref[...] = y
        yb_ref[...] = y.astype(BF16)

    row = pl.BlockSpec((tm, D), lambda i: (i, 0))
    vec = pl.BlockSpec((1, D), lambda i: (0, 0))
    return pl.pallas_call(
        body, name=name, grid=(T // tm,),
        in_specs=[pl.BlockSpec((tm, K), lambda i: (i, 0)), pl.BlockSpec((K, D), lambda i: (0, 0)), row, vec, vec],
        out_specs=[row, row, pl.BlockSpec((tm, 1), lambda i: (i, 0)), row],
        out_shape=[jax.ShapeDtypeStruct((T, D), F32), jax.ShapeDtypeStruct((T, D), BF16),
                   jax.ShapeDtypeStruct((T, 1), F32), jax.ShapeDtypeStruct((T, D), BF16)],
        compiler_params=_cp(("parallel",)))(a, w, x, gam, bet)


def _ln_bwd(dy, xh, r, gam):
    tm = 256

    def body(dy_ref, xh_ref, r_ref, g_ref, ds_ref, dg_ref, db_ref, dsb_ref):
        i = pl.program_id(0)
        dyv = dy_ref[...]
        xhv = xh_ref[...].astype(F32)
        dxh = dyv * g_ref[...]
        m1 = jnp.mean(dxh, axis=-1, keepdims=True)
        m2 = jnp.mean(dxh * xhv, axis=-1, keepdims=True)
        ds = r_ref[...] * (dxh - m1 - xhv * m2)
        ds_ref[...] = ds
        dsb_ref[...] = ds.astype(BF16)
        pg = jnp.sum(dyv * xhv, axis=0, keepdims=True)
        pb = jnp.sum(dyv, axis=0, keepdims=True)

        @pl.when(i == 0)
        def _():
            dg_ref[...] = pg
            db_ref[...] = pb

        @pl.when(i > 0)
        def _():
            dg_ref[...] += pg
            db_ref[...] += pb

    row = pl.BlockSpec((tm, D), lambda i: (i, 0))
    vec = pl.BlockSpec((1, D), lambda i: (0, 0))
    return pl.pallas_call(
        body, name="ln_bwd", grid=(T // tm,),
        in_specs=[row, row, pl.BlockSpec((tm, 1), lambda i: (i, 0)), vec],
        out_specs=[row, vec, vec, row],
        out_shape=[jax.ShapeDtypeStruct((T, D), F32), jax.ShapeDtypeStruct((1, D), F32),
                   jax.ShapeDtypeStruct((1, D), F32), jax.ShapeDtypeStruct((T, D), BF16)],
        compiler_params=_cp(("arbitrary",)))(dy, xh, r, gam)


def _loss_grad(y, tgt):
    tm = 256

    def body(y_ref, t_ref, l_ref, dy_ref):
        i = pl.program_id(0)
        e = y_ref[...] - t_ref[...]
        dy_ref[...] = e * (1.0 / D)
        p = jnp.sum(jnp.sum(e * e, axis=1, keepdims=True), axis=0, keepdims=True)

        @pl.when(i == 0)
        def _():
            l_ref[...] = p

        @pl.when(i > 0)
        def _():
            l_ref[...] += p

    row = pl.BlockSpec((tm, D), lambda i: (i, 0))
    return pl.pallas_call(
        body, name="loss_grad", grid=(T // tm,), in_specs=[row, row],
        out_specs=[pl.BlockSpec((1, 1), lambda i: (0, 0)), row],
        out_shape=[jax.ShapeDtypeStruct((1, 1), F32), jax.ShapeDtypeStruct((T, D), F32)],
        compiler_params=_cp(("arbitrary",)))(y, tgt)


def _adamw(w, g, m, v, tr):
    L, R, C = w.shape

    def body(w_ref, g_ref, m_ref, v_ref, d_ref, m2_ref, v2_ref):
        gv = g_ref[...]
        m2 = ADAM_B1 * m_ref[...] + (1.0 - ADAM_B1) * gv
        v2 = ADAM_B2 * v_ref[...] + (1.0 - ADAM_B2) * (gv * gv)
        m_hat = m2 / (1.0 - ADAM_B1 ** ADAM_STEP)
        v_hat = v2 / (1.0 - ADAM_B2 ** ADAM_STEP)
        d_ref[...] = -ADAM_LR * (m_hat / (jnp.sqrt(v_hat) + ADAM_EPS) + ADAM_WD * w_ref[...])
        m2_ref[...] = m2
        v2_ref[...] = v2

    blk = pl.BlockSpec((None, tr, C), lambda l, i: (l, i, 0))
    sh = jax.ShapeDtypeStruct((L, R, C), F32)
    return pl.pallas_call(
        body, name="adamw", grid=(L, R // tr), in_specs=[blk] * 4, out_specs=[blk] * 3,
        out_shape=[sh, sh, sh], compiler_params=_cp(("parallel", "parallel")))(w, g, m, v)


class _Job:
    def __init__(self, ins, out_shapes, aliases, sems, start, finish):
        self.ins, self.out_shapes, self.aliases, self.sems = list(ins), list(out_shapes), dict(aliases), list(sems)
        self.start, self.finish = start, finish


def _host_call(body, name, ins, in_specs, out_shapes, out_specs, scratch, aliases, job):
    n_in, n_out, n_scr = len(ins), len(out_shapes), len(scratch)
    jins = job.ins if job else []
    jouts = job.out_shapes if job else []
    jsems = job.sems if job else []

    def wrapped(*refs):
        a = n_in
        b = a + len(jins)
        c = b + n_out
        d = c + len(jouts)
        e = d + n_scr
        comm = None
        if job:
            jrefs = (refs[a:b], refs[c:d], refs[e:])
            comm = (lambda: job.start(*jrefs), lambda st: job.finish(st, *jrefs))
        body(refs[:a], refs[b:c], refs[d:e], comm)

    al = dict(aliases)
    if job:
        for ji, jo in job.aliases.items():
            al[n_in + ji] = n_out + jo
    res = pl.pallas_call(
        wrapped, name=name, in_specs=list(in_specs) + [ANY_SPEC] * len(jins),
        out_specs=list(out_specs) + [ANY_SPEC] * len(jouts), out_shape=list(out_shapes) + list(jouts),
        scratch_shapes=list(scratch) + list(jsems), input_output_aliases=al,
        compiler_params=_cp())(*ins, *jins)
    return res[:n_out], res[n_out:]


def _copy_in(src, dst, sem):
    cp = pltpu.make_async_copy(src, dst, sem)
    cp.start()
    cp.wait()


CHAINS = [(p, b) for p in range(2) for b in range(BL)]
NC = len(CHAINS)
ROWS_SHAPE = jax.ShapeDtypeStruct((NSTAT, S), F32)
SLAB_QKV = pltpu.VMEM((T, 2 * PAIRW), BF16)
SLAB_OUT = pltpu.VMEM((T, 2 * BQ), BF16)
SLAB_O32 = pltpu.VMEM((T, 2 * BQ), F32)
SLAB_T = pltpu.VMEM((2, BQ, T), BF16)
SLAB_KEYB = pltpu.VMEM((NSTAT, S, BQ), F32)
ACC_KV = pltpu.VMEM((2, T, BQ), F32)


def _lane_masks():
    lane = _iota((1, BQ), 1)
    m0 = (lane < 64).astype(BF16)
    return m0, 1.0 - m0


def _row_masks():
    r = _iota((BQ, 1), 0)
    m0 = (r < 64).astype(BF16)
    return m0, 1.0 - m0


def _stack(x, m0, m1):
    return jnp.concatenate([x * m0, x * m1], axis=0)


def _stack_t(xt, r0, r1):
    return jnp.concatenate([xt * r0, xt * r1], axis=1)


def _tr(x):
    return x.T


def _rows(b, i):
    return pl.ds(pl.multiple_of(b * S + i * BQ, BQ), BQ)


def _transpose_slab(src, dst, col0):
    def blk(n, _):
        r = pl.ds(pl.multiple_of(n * BQ, BQ), BQ)
        for p in range(2):
            dst[p, :, r] = _tr(src[r, col0(p):col0(p) + BQ])
        return 0

    lax.fori_loop(0, T // BQ, blk, 0)


def _heads(x):
    return x[:BQ], x[BQ:]


def _bcast_heads(r0, r1):
    return jnp.concatenate([jnp.broadcast_to(r0, (BQ, BQ)), jnp.broadcast_to(r1, (BQ, BQ))], axis=0)


def _by_channel(r0, r1):
    return jnp.where(_iota((BQ, BQ), 0) < 64, r0, r1)


def _colsum2(x):
    return jnp.sum(x[:BQ], axis=0, keepdims=True), jnp.sum(x[BQ:], axis=0, keepdims=True)


def _stat_row(ref, p, b, h, i):
    c = b * NH + 2 * p + h
    return ref[c:c + 1, pl.ds(pl.multiple_of(i * BQ, BQ), BQ)]


def _put_row(ref, p, b, h, i, v):
    c = b * NH + 2 * p + h
    ref[c:c + 1, pl.ds(pl.multiple_of(i * BQ, BQ), BQ)] = v


def _valid_t(strict):
    r = _iota((HB, BQ), 0) & (BQ - 1)
    c = _iota((HB, BQ), 1)
    return (r < c) if strict else (r <= c)


def _tri_blockdiag(later):
    r = _iota((HB, HB), 0)
    c = _iota((HB, HB), 1)
    same = (r >= BQ) == (c >= BQ)
    return (same & ((c > r) if later else (c < r))).astype(BF16)


def _cum_mm(tri, x):
    y = _dot(tri, _split2(x))
    return y[:, :BQ] + y[:, BQ:]


def _kv_tiles(qkv_v, p, b, j):
    r = _rows(b, j)
    return qkv_v[r, p * PAIRW + BQ:p * PAIRW + 2 * BQ], qkv_v[r, p * PAIRW + 2 * BQ:p * PAIRW + 3 * BQ]


def _q_tile(qkv_v, p, b, i):
    return qkv_v[_rows(b, i), p * PAIRW:p * PAIRW + BQ] * SCALE


def _sb_fwd(qkv, job=None):
    def body(ins, outs, scr, comm):
        (qkv_hbm,), (o_hbm, r_ref), (qkv_v, o_v, sem, vt_v) = ins, outs, scr
        _copy_in(qkv_hbm.at[:, pl.ds(0, 2 * PAIRW)], qkv_v, sem)
        st = comm[0]() if comm else None
        _transpose_slab(qkv_v, vt_v, lambda p: p * PAIRW + 2 * BQ)
        m0, m1 = _lane_masks()
        r0, r1 = _row_masks()
        valid = _valid_t(True)
        later = _tri_blockdiag(True)

        def steps(qts, i, j, cs, diag):
            ks = [_stack(_kv_tiles(qkv_v, p, b, j)[0], m0, m1) for p, b in CHAINS]
            zs = [_dot(ks[c], qts[c]) for c in range(NC)]
            lbs, lrs = [], []
            for c in range(NC):
                lb = _log_sigmoid_tile(zs[c])
                lr = lb - zs[c]
                if diag:
                    lr = jnp.where(valid, lr, 0.0)
                lbs.append(lb)
                lrs.append(lr)
            tails = [_cum_mm(later, lrs[c]) for c in range(NC)]
            avs = []
            for c in range(NC):
                a = jnp.exp(lbs[c] + tails[c] + _bcast_heads(*cs[c][0]))
                if diag:
                    a = jnp.where(valid, a, 0.0)
                avs.append(a.astype(BF16))
            out = []
            for c, (p, b) in enumerate(CHAINS):
                vts = _stack_t(vt_v[p, :, _rows(b, j)], r0, r1)
                s0, s1 = _colsum2(lrs[c])
                out.append(((cs[c][0][0] + s0, cs[c][0][1] + s1), cs[c][1] + _dot(vts, avs[c])))
            return tuple(out)

        def qblock(i, _):
            qts = [_tr(_q_tile(qkv_v, p, b, i)) for p, b in CHAINS]
            zr = jnp.zeros((1, BQ), F32)
            cs = steps(qts, i, i, (((zr, zr), jnp.zeros((BQ, BQ), F32)),) * NC, True)
            cs = lax.fori_loop(1, i + 1, lambda jj, cs: steps(qts, i, i - jj, cs, False), cs)
            for c, (p, b) in enumerate(CHAINS):
                o_v[_rows(b, i), p * BQ:(p + 1) * BQ] = cs[c][1].T.astype(BF16)
                for h in range(2):
                    _put_row(r_ref, p, b, h, i, cs[c][0][h])
            return 0

        lax.fori_loop(0, NB, qblock, 0)
        _copy_in(o_v, o_hbm.at[:, pl.ds(0, 2 * BQ)], sem)
        if comm:
            comm[1](st)

    (mixed, rtot), extra = _host_call(
        body, "sb_fwd", [qkv], [ANY_SPEC], [jax.ShapeDtypeStruct((T, D), BF16), ROWS_SHAPE], [ANY_SPEC, VMEM_SPEC],
        [SLAB_QKV, SLAB_OUT, pltpu.SemaphoreType.DMA, SLAB_T], {}, job)
    return mixed, rtot, extra


def _sb_bwd(qkv, dmixed, rtot, job=None):
    def body(ins, outs, scr, comm):
        (qkv_hbm, do_hbm, r_ref), (dqkv_hbm,), (qkv_v, do_v, dq_v, dk_s, dv_s, sem, kt_v) = ins, outs, scr
        _copy_in(qkv_hbm.at[:, pl.ds(0, 2 * PAIRW)], qkv_v, sem)
        _copy_in(do_hbm.at[:, pl.ds(0, 2 * BQ)], do_v, sem)
        st = comm[0]() if comm else None
        _transpose_slab(qkv_v, kt_v, lambda p: p * PAIRW + BQ)
        m0, m1 = _lane_masks()
        f0, f1 = m0.astype(F32), m1.astype(F32)
        r0, r1 = _row_masks()
        valid = _valid_t(True)
        later = _tri_blockdiag(True)
        earlier = _tri_blockdiag(False)
        dk_s[...] = jnp.zeros_like(dk_s)
        dv_s[...] = jnp.zeros_like(dv_s)

        def steps(qns, qts, dns, dts, rts, i, j, cs, diag):
            kv = [_kv_tiles(qkv_v, p, b, j) for p, b in CHAINS]
            ks = [_stack(kv[c][0], m0, m1) for c in range(NC)]
            vs = [_stack(kv[c][1], m0, m1) for c in range(NC)]
            zs = [_dot(ks[c], qts[c]) for c in range(NC)]
            das = [_dot(vs[c], dts[c]) for c in range(NC)]
            lbs, lrs, pls = [], [], []
            for c in range(NC):
                lb = _log_sigmoid_tile(zs[c])
                lr = lb - zs[c]
                if diag:
                    lr = jnp.where(valid, lr, 0.0)
                s0, s1 = _colsum2(lr)
                lbs.append(lb)
                lrs.append(lr)
                pls.append((cs[c][0][0] + s0, cs[c][0][1] + s1))
            tails = [_cum_mm(later, lrs[c]) for c in range(NC)]
            avs, gms = [], []
            for c in range(NC):
                a = jnp.exp(lbs[c] + tails[c] + _bcast_heads(rts[c][0] - pls[c][0], rts[c][1] - pls[c][1]))
                if diag:
                    a = jnp.where(valid, a, 0.0)
                avs.append(a)
                gms.append(das[c] * a)
            befores = [_cum_mm(earlier, gms[c]) for c in range(NC)]
            dzbs = []
            for c in range(NC):
                beta = jnp.exp(lbs[c])
                dz = gms[c] - beta * (gms[c] + befores[c] + _bcast_heads(*cs[c][1]))
                if diag:
                    dz = jnp.where(valid, dz, 0.0)
                dzbs.append(dz.astype(BF16))
            out = []
            for c, (p, b) in enumerate(CHAINS):
                dq = cs[c][2] + _dot(_stack_t(kt_v[p, :, _rows(b, j)], r0, r1), dzbs[c])
                dk = _dot(dzbs[c], qns[c])
                dv = _dot(avs[c].astype(BF16), dns[c])
                dk_s[p, _rows(b, j), :] += dk[:BQ] * f0 + dk[BQ:] * f1
                dv_s[p, _rows(b, j), :] += dv[:BQ] * f0 + dv[BQ:] * f1
                g0, g1 = _colsum2(gms[c])
                out.append((pls[c], (cs[c][1][0] + g0, cs[c][1][1] + g1), dq))
            return tuple(out)

        def qblock(i, _):
            qns = [_q_tile(qkv_v, p, b, i) for p, b in CHAINS]
            dns = [do_v[_rows(b, i), p * BQ:(p + 1) * BQ] for p, b in CHAINS]
            qts = [_tr(t) for t in qns]
            dts = [_tr(t) for t in dns]
            rts = [(_stat_row(r_ref, p, b, 0, i), _stat_row(r_ref, p, b, 1, i)) for p, b in CHAINS]
            zr = jnp.zeros((1, BQ), F32)
            cs = (((zr, zr), (zr, zr), jnp.zeros((BQ, BQ), F32)),) * NC
            cs = lax.fori_loop(0, i, lambda j, cs: steps(qns, qts, dns, dts, rts, i, j, cs, False), cs)
            cs = steps(qns, qts, dns, dts, rts, i, i, cs, True)
            for c, (p, b) in enumerate(CHAINS):
                dq_v[_rows(b, i), p * PAIRW:p * PAIRW + BQ] = (cs[c][2].T * SCALE).astype(BF16)
            return 0

        lax.fori_loop(0, NB, qblock, 0)
        for p in range(2):
            dq_v[:, p * PAIRW + BQ:p * PAIRW + 2 * BQ] = dk_s[p].astype(BF16)
            dq_v[:, p * PAIRW + 2 * BQ:p * PAIRW + 3 * BQ] = dv_s[p].astype(BF16)
        _copy_in(dq_v, dqkv_hbm.at[:, pl.ds(0, 2 * PAIRW)], sem)
        if comm:
            comm[1](st)

    (dqkv,), extra = _host_call(
        body, "sb_bwd", [qkv, dmixed, rtot], [ANY_SPEC, ANY_SPEC, VMEM_SPEC],
        [jax.ShapeDtypeStruct((T, QKVW), BF16)], [ANY_SPEC],
        [SLAB_QKV, SLAB_OUT, SLAB_QKV, ACC_KV, ACC_KV, pltpu.SemaphoreType.DMA, SLAB_T], {}, job)
    return dqkv, extra


def _flash_fwd(qkv, mixed, g, fox, bias, job=None):
    def body(ins, outs, scr, comm):
        if fox:
            qkv_hbm, cq_ref, ckb_hbm, _ = ins
            (o_hbm, lse_ref, o32_hbm), (qkv_v, o_v, sem, vt_v, o32_v, ckb_v) = outs, scr
        else:
            qkv_hbm, tbl_ref, _ = ins
            (o_hbm, lse_ref), (qkv_v, o_v, sem, vt_v) = outs, scr
        _copy_in(qkv_hbm.at[:, pl.ds(g * 2 * PAIRW, 2 * PAIRW)], qkv_v, sem)
        if fox:
            _copy_in(ckb_hbm, ckb_v, sem)
        st = comm[0]() if comm else None
        _transpose_slab(qkv_v, vt_v, lambda p: p * PAIRW + 2 * BQ)
        m0, m1 = _lane_masks()
        r0, r1 = _row_masks()
        valid = _valid_t(False)

        def steps(qts, cqs, i, j, cs, diag):
            ks = [_stack(_kv_tiles(qkv_v, p, b, j)[0], m0, m1) for p, b in CHAINS]
            zs = [_dot(ks[c], qts[c]) for c in range(NC)]
            prs, alphas, out = [], [], []
            for c, (p, b) in enumerate(CHAINS):
                (ma, mb), (la, lb_), _ = cs[c]
                if fox:
                    kk = pl.ds(pl.multiple_of(j * BQ, BQ), BQ)
                    col = b * NH + 2 * p
                    z = zs[c] + (cqs[c] - jnp.concatenate([ckb_v[col, kk, :], ckb_v[col + 1, kk, :]], axis=0))
                    if diag:
                        z = jnp.where(valid, z, NEG)
                else:
                    z = zs[c] + tbl_ref[p, i - j]
                za, zb = _heads(z)
                na = jnp.maximum(ma, jnp.max(za, axis=0, keepdims=True))
                nb = jnp.maximum(mb, jnp.max(zb, axis=0, keepdims=True))
                aa, ab = jnp.exp(ma - na), jnp.exp(mb - nb)
                pr = jnp.exp(z - _bcast_heads(na, nb))
                sa, sb = _colsum2(pr)
                prs.append(_split2(pr) if fox else pr.astype(BF16))
                alphas.append((aa, ab))
                out.append(((na, nb), (aa * la + sa, ab * lb_ + sb)))
            pvs = []
            for c, (p, b) in enumerate(CHAINS):
                vts = _stack_t(vt_v[p, :, _rows(b, j)], r0, r1)
                if fox:
                    pvs.append(_dot(vts, prs[c][:, :BQ]) + _dot(vts, prs[c][:, BQ:]))
                else:
                    pvs.append(_dot(vts, prs[c]))
            return tuple((out[c][0], out[c][1], _by_channel(*alphas[c]) * cs[c][2] + pvs[c]) for c in range(NC))

        def qblock(i, _):
            qts = [_tr(_q_tile(qkv_v, p, b, i)) for p, b in CHAINS]
            if fox:
                cqs = [_bcast_heads(_stat_row(cq_ref, p, b, 0, i), _stat_row(cq_ref, p, b, 1, i)) for p, b in CHAINS]
            else:
                cqs = [None] * NC
            ng = jnp.full((1, BQ), NEG, F32)
            zr = jnp.zeros((1, BQ), F32)
            cs = steps(qts, cqs, i, i, (((ng, ng), (zr, zr), jnp.zeros((BQ, BQ), F32)),) * NC, True)
            cs = lax.fori_loop(1, i + 1, lambda jj, cs: steps(qts, cqs, i, i - jj, cs, False), cs)
            for c, (p, b) in enumerate(CHAINS):
                (ma, mb), (la, lb_), acc = cs[c]
                o = (acc / _by_channel(la, lb_)).T
                o_v[_rows(b, i), p * BQ:(p + 1) * BQ] = o.astype(BF16)
                if fox:
                    o32_v[_rows(b, i), p * BQ:(p + 1) * BQ] = o
                _put_row(lse_ref, p, b, 0, i, ma + jnp.log(la))
                _put_row(lse_ref, p, b, 1, i, mb + jnp.log(lb_))
            return 0

        lax.fori_loop(0, NB, qblock, 0)
        _copy_in(o_v, o_hbm.at[:, pl.ds(g * 2 * BQ, 2 * BQ)], sem)
        if fox:
            _copy_in(o32_v, o32_hbm, sem)
        if comm:
            comm[1](st)

    bias_specs = [VMEM_SPEC, ANY_SPEC] if fox else [VMEM_SPEC]
    n_in = 2 + len(bias_specs)
    o32 = [jax.ShapeDtypeStruct((T, 2 * BQ), F32)] if fox else []
    res, extra = _host_call(
        body, "fox_fwd" if fox else "dil_fwd", [qkv, *bias, mixed], [ANY_SPEC] + bias_specs + [ANY_SPEC],
        [jax.ShapeDtypeStruct((T, D), BF16), ROWS_SHAPE] + o32, [ANY_SPEC, VMEM_SPEC] + [ANY_SPEC] * len(o32),
        [SLAB_QKV, SLAB_OUT, pltpu.SemaphoreType.DMA, SLAB_T] + ([SLAB_O32, SLAB_KEYB] if fox else []),
        {n_in - 1: 0}, job)
    return (*res, extra)


def _flash_bwd(qkv, o, dmixed, lse, dqkv, g, fox, bias, job=None):
    def body(ins, outs, scr, comm):
        if fox:
            qkv_hbm, o_hbm, do_hbm, lse_ref, cq_ref, ckb_hbm, _ = ins
            (dqkv_hbm, db_ref), (qkv_v, o_v, do_v, dq_v, dk_s, dv_s, sem, kt_v, ckb_v, dc_s) = outs, scr
        else:
            qkv_hbm, o_hbm, do_hbm, lse_ref, tbl_ref, _ = ins
            (dqkv_hbm, db_ref), (qkv_v, o_v, do_v, dq_v, dk_s, dv_s, sem, kt_v) = outs, scr
        _copy_in(qkv_hbm.at[:, pl.ds(g * 2 * PAIRW, 2 * PAIRW)], qkv_v, sem)
        _copy_in(do_hbm.at[:, pl.ds(g * 2 * BQ, 2 * BQ)], do_v, sem)
        if fox:
            _copy_in(o_hbm, o_v, sem)
            _copy_in(ckb_hbm, ckb_v, sem)
        else:
            _copy_in(o_hbm.at[:, pl.ds(g * 2 * BQ, 2 * BQ)], o_v, sem)
        st = comm[0]() if comm else None
        _transpose_slab(qkv_v, kt_v, lambda p: p * PAIRW + BQ)
        m0, m1 = _lane_masks()
        f0, f1 = m0.astype(F32), m1.astype(F32)
        r0, r1 = _row_masks()
        valid = _valid_t(False)
        dk_s[...] = jnp.zeros_like(dk_s)
        dv_s[...] = jnp.zeros_like(dv_s)
        if fox:
            dc_s[...] = jnp.zeros_like(dc_s)
        else:
            db_ref[...] = jnp.zeros_like(db_ref)

        def steps(qns, qts, dns, dts, cqs, lses, deltas, i, j, dqs, diag):
            kv = [_kv_tiles(qkv_v, p, b, j) for p, b in CHAINS]
            ks = [_stack(kv[c][0], m0, m1) for c in range(NC)]
            vs = [_stack(kv[c][1], m0, m1) for c in range(NC)]
            zs = [_dot(ks[c], qts[c]) for c in range(NC)]
            dps = [_dot(vs[c], dts[c]) for c in range(NC)]
            prs, dzl = [], []
            for c, (p, b) in enumerate(CHAINS):
                if fox:
                    kk = pl.ds(pl.multiple_of(j * BQ, BQ), BQ)
                    col = b * NH + 2 * p
                    z = zs[c] + (cqs[c] - jnp.concatenate([ckb_v[col, kk, :], ckb_v[col + 1, kk, :]], axis=0))
                    if diag:
                        z = jnp.where(valid, z, NEG)
                else:
                    z = zs[c] + tbl_ref[p, i - j]
                pr = jnp.exp(z - lses[c])
                prs.append(pr.astype(BF16))
                dzl.append(pr * (dps[c] - deltas[c]))
            dzbs = [dz.astype(BF16) for dz in dzl]
            new = []
            for c, (p, b) in enumerate(CHAINS):
                new.append(dqs[c] + _dot(_stack_t(kt_v[p, :, _rows(b, j)], r0, r1), dzbs[c]))
                dk = _dot(dzbs[c], qns[c])
                dv = _dot(prs[c], dns[c])
                dk_s[p, _rows(b, j), :] += dk[:BQ] * f0 + dk[BQ:] * f1
                dv_s[p, _rows(b, j), :] += dv[:BQ] * f0 + dv[BQ:] * f1
                if fox:
                    dc_s[c, pl.ds(pl.multiple_of(j * HB, HB), HB), :] += dzl[c]
            if not fox:
                for p in range(2):
                    db_ref[p, i - j] = db_ref[p, i - j] + (dzl[2 * p] + dzl[2 * p + 1])
            return tuple(new)

        def qblock(i, _):
            qns = [_q_tile(qkv_v, p, b, i) for p, b in CHAINS]
            dns = [do_v[_rows(b, i), p * BQ:(p + 1) * BQ] for p, b in CHAINS]
            qts = [_tr(t) for t in qns]
            dts = [_tr(t) for t in dns]
            lses = [_bcast_heads(_stat_row(lse_ref, p, b, 0, i), _stat_row(lse_ref, p, b, 1, i)) for p, b in CHAINS]
            if fox:
                cqs = [_bcast_heads(_stat_row(cq_ref, p, b, 0, i), _stat_row(cq_ref, p, b, 1, i)) for p, b in CHAINS]
            else:
                cqs = [None] * NC
            deltas = []
            for c, (p, b) in enumerate(CHAINS):
                pt = (dns[c].astype(F32) * o_v[_rows(b, i), p * BQ:(p + 1) * BQ].astype(F32)).T
                deltas.append(_bcast_heads(jnp.sum(pt[:64], axis=0, keepdims=True), jnp.sum(pt[64:], axis=0, keepdims=True)))
            dqs = (jnp.zeros((BQ, BQ), F32),) * NC
            dqs = lax.fori_loop(0, i, lambda j, d: steps(qns, qts, dns, dts, cqs, lses, deltas, i, j, d, False), dqs)
            dqs = steps(qns, qts, dns, dts, cqs, lses, deltas, i, i, dqs, True)
            for c, (p, b) in enumerate(CHAINS):
                dq_v[_rows(b, i), p * PAIRW:p * PAIRW + BQ] = (dqs[c].T * SCALE).astype(BF16)
            return 0

        lax.fori_loop(0, NB, qblock, 0)
        for p in range(2):
            dq_v[:, p * PAIRW + BQ:p * PAIRW + 2 * BQ] = dk_s[p].astype(BF16)
            dq_v[:, p * PAIRW + 2 * BQ:p * PAIRW + 3 * BQ] = dv_s[p].astype(BF16)
        _copy_in(dq_v, dqkv_hbm.at[:, pl.ds(g * 2 * PAIRW, 2 * PAIRW)], sem)
        if fox:
            lane = _iota((BQ, NSTAT), 1)

            def fold(n, _):
                t = jnp.zeros((BQ, NSTAT), F32)
                for c, (p, b) in enumerate(CHAINS):
                    s = jnp.sum(dc_s[c, pl.ds(pl.multiple_of(n * HB, HB), HB), :], axis=1, keepdims=True)
                    col = b * NH + 2 * p
                    t = t - jnp.where(lane == col, s[:BQ], 0.0) - jnp.where(lane == col + 1, s[BQ:], 0.0)
                db_ref[pl.ds(pl.multiple_of(n * BQ, BQ), BQ), :] = t
                return 0

            lax.fori_loop(0, NB, fold, 0)
        if comm:
            comm[1](st)

    if fox:
        bias_specs = [VMEM_SPEC, ANY_SPEC]
        db_shape = jax.ShapeDtypeStruct((S, NSTAT), F32)
        more = [SLAB_KEYB, pltpu.VMEM((NC, NB * HB, BQ), F32)]
    else:
        bias_specs = [VMEM_SPEC]
        db_shape = jax.ShapeDtypeStruct((2, NB, HB, BQ), F32)
        more = []
    n_in = 5 + len(bias_specs)
    (dqkv, db), extra = _host_call(
        body, "fox_bwd" if fox else "dil_bwd", [qkv, o, dmixed, lse, *bias, dqkv],
        [ANY_SPEC, ANY_SPEC, ANY_SPEC, VMEM_SPEC] + bias_specs + [ANY_SPEC],
        [jax.ShapeDtypeStruct((T, QKVW), BF16), db_shape], [ANY_SPEC, VMEM_SPEC],
        [SLAB_QKV, SLAB_O32 if fox else SLAB_OUT, SLAB_OUT, SLAB_QKV, ACC_KV, ACC_KV, pltpu.SemaphoreType.DMA, SLAB_T]
        + more, {n_in - 1: 0}, job)
    return dqkv, db, extra


def _delta_t(d):
    return d * BQ + _iota((HB, BQ), 1) - (_iota((HB, BQ), 0) & (BQ - 1))


def _buckets_in(d):
    lo, hi = max(d * BQ - (BQ - 1), 0), d * BQ + BQ - 1
    return [b for b in range(32) if BUCKET_TH[b] <= hi and (b == 31 or BUCKET_TH[b + 1] > lo)]


def _in_bucket(delta, b):
    m = delta >= BUCKET_TH[b]
    return m if b == 31 else m & (delta < BUCKET_TH[b + 1])


def _dil_table(rel_bias):
    def body(rb_ref, o_ref):
        for d in range(NB):
            delta = _delta_t(d)
            pos = delta >= 0
            n = ((pos & (delta <= 128)).astype(jnp.int32)
                 + (pos & (delta <= 512) & ((delta & 3) == 0)).astype(jnp.int32)
                 + (pos & ((delta & 15) == 0)).astype(jnp.int32))
            logn = jnp.where(n == 3, math.log(3.0), jnp.where(n == 2, math.log(2.0), jnp.where(n == 1, 0.0, NEG)))
            head1 = _iota((HB, BQ), 0) >= BQ
            for p in range(2):
                val = jnp.zeros((HB, BQ), F32)
                for b in _buckets_in(d):
                    val = jnp.where(_in_bucket(delta, b), jnp.where(head1, rb_ref[b, 2 * p + 1], rb_ref[b, 2 * p]), val)
                o_ref[p, d] = val + logn

    return pl.pallas_call(
        body, name="dil_table", in_specs=[pl.BlockSpec(memory_space=pltpu.SMEM)], out_specs=VMEM_SPEC,
        out_shape=jax.ShapeDtypeStruct((2, NB, HB, BQ), F32), compiler_params=_cp())(rel_bias)


def _dil_table_bwd(dtbl):
    def body(dt_ref, o_ref):
        p = pl.program_id(0)
        rowi = _iota((32, BQ), 0)
        lanei = _iota((32, BQ), 1)

        @pl.when(p == 0)
        def _():
            o_ref[...] = jnp.zeros_like(o_ref)

        out = jnp.zeros((32, BQ), F32)
        for b in range(32):
            acc = None
            for d in range(NB):
                if b in _buckets_in(d):
                    t = jnp.where(_in_bucket(_delta_t(d), b), dt_ref[d], 0.0)
                    acc = t if acc is None else acc + t
            rs = jnp.sum(acc, axis=1, keepdims=True)
            s0 = jnp.sum(rs[:BQ], axis=0, keepdims=True)
            s1 = jnp.sum(rs[BQ:], axis=0, keepdims=True)
            out = (out + jnp.where((rowi == b) & (lanei == 2 * p), s0, 0.0)
                   + jnp.where((rowi == b) & (lanei == 2 * p + 1), s1, 0.0))
        o_ref[...] += out

    return pl.pallas_call(
        body, name="dil_table_bwd", grid=(2,),
        in_specs=[pl.BlockSpec((None, NB, HB, BQ), lambda p: (p, 0, 0, 0))],
        out_specs=pl.BlockSpec((32, BQ), lambda p: (0, 0)),
        out_shape=jax.ShapeDtypeStruct((32, BQ), F32),
        compiler_params=_cp(("arbitrary",)))(dtbl)


def _fox_prep(gate, fb):
    def body(g_ref, fb_ref, c_ref):
        tri = (_iota((BQ, BQ), 0) >= _iota((BQ, BQ), 1)).astype(BF16)

        def blk(i, carry):
            r0 = pl.multiple_of(i * BQ, BQ)
            lf = _log_sigmoid(g_ref[pl.ds(r0, BQ), :] + fb_ref[...])
            c = _dot(tri, _split3(lf))
            c_ref[pl.ds(r0, BQ), :] = c[:, 0:BQ] + c[:, BQ:2 * BQ] + c[:, 2 * BQ:3 * BQ] + carry
            return carry + jnp.sum(lf, axis=0, keepdims=True)

        lax.fori_loop(0, NB, blk, jnp.zeros((1, BQ), F32))

    blk = pl.BlockSpec((S, GATEW), lambda b: (b, 0))
    return pl.pallas_call(
        body, name="fox_prep", grid=(BL,), in_specs=[blk, pl.BlockSpec((1, GATEW), lambda b: (0, 0))],
        out_specs=blk, out_shape=jax.ShapeDtypeStruct((T, GATEW), F32),
        compiler_params=_cp(("parallel",)))(gate, fb)


def _fox_post(dcum, gate, fb):
    def body(dc_ref, g_ref, fb_ref, dg_ref, dfb_ref):
        b = pl.program_id(0)
        tri = (_iota((BQ, BQ), 0) <= _iota((BQ, BQ), 1)).astype(BF16)

        def blk(ii, carry):
            csum, dfb = carry
            r0 = pl.multiple_of((NB - 1 - ii) * BQ, BQ)
            dc = dc_ref[pl.ds(r0, BQ), :]
            c = _dot(tri, _split3(dc))
            dlf = c[:, 0:BQ] + c[:, BQ:2 * BQ] + c[:, 2 * BQ:3 * BQ] + csum
            dg = dlf * jnp.exp(_log_sigmoid(-(g_ref[pl.ds(r0, BQ), :] + fb_ref[...])))
            dg_ref[pl.ds(r0, BQ), :] = dg
            return csum + jnp.sum(dc, axis=0, keepdims=True), dfb + jnp.sum(dg, axis=0, keepdims=True)

        z = jnp.zeros((1, BQ), F32)
        _, dfb = lax.fori_loop(0, NB, blk, (z, z))

        @pl.when(b == 0)
        def _():
            dfb_ref[...] = dfb

        @pl.when(b > 0)
        def _():
            dfb_ref[...] += dfb

    blk = pl.BlockSpec((S, GATEW), lambda b: (b, 0))
    vec = pl.BlockSpec((1, GATEW), lambda b: (0, 0))
    return pl.pallas_call(
        body, name="fox_post", grid=(BL,), in_specs=[blk, blk, vec], out_specs=[blk, vec],
        out_shape=[jax.ShapeDtypeStruct((T, GATEW), F32), jax.ShapeDtypeStruct((1, GATEW), F32)],
        compiler_params=_cp(("arbitrary",)))(dcum, gate, fb)


def _shift_down(x, n):
    return jnp.where(_iota(x.shape, 0) >= n, pltpu.roll(x, n, 0), 0.0)


def _shift_up(x, n):
    return jnp.where(_iota(x.shape, 0) < S - n, pltpu.roll(x, S - n, 0), 0.0)


def _conv_fwd(conv, cw, mixed):
    W = 256

    def body(c_ref, w_ref, _, o_ref):
        u = c_ref[:, W:2 * W] * c_ref[:, 2 * W:3 * W]
        y = w_ref[0:1, :] * _shift_down(u, 2) + w_ref[1:2, :] * _shift_down(u, 1) + w_ref[2:3, :] * u
        o_ref[...] = (c_ref[:, 0:W] * y).astype(BF16)

    return pl.pallas_call(
        body, name="conv_fwd", grid=(BL,),
        in_specs=[pl.BlockSpec((S, CONVW), lambda b: (b, 0)), pl.BlockSpec((8, W), lambda b: (0, 0)), ANY_SPEC],
        out_specs=pl.BlockSpec((S, W), lambda b: (b, 3)),
        out_shape=jax.ShapeDtypeStruct((T, D), BF16), input_output_aliases={2: 0},
        compiler_params=_cp(("parallel",)))(conv, cw, mixed)


def _conv_bwd(conv, cw, dmixed):
    W = 256

    def body(c_ref, w_ref, do_ref, dc_ref, dw_ref):
        b = pl.program_id(0)
        bg = c_ref[:, 0:W]
        cg = c_ref[:, W:2 * W]
        hv = c_ref[:, 2 * W:3 * W]
        do = do_ref[...].astype(F32)
        u = cg * hv
        u1 = _shift_down(u, 1)
        u2 = _shift_down(u, 2)
        y = w_ref[0:1, :] * u2 + w_ref[1:2, :] * u1 + w_ref[2:3, :] * u
        dy = do * bg
        du = w_ref[2:3, :] * dy + w_ref[1:2, :] * _shift_up(dy, 1) + w_ref[0:1, :] * _shift_up(dy, 2)
        dc_ref[:, 0:W] = (do * y).astype(BF16)
        dc_ref[:, W:2 * W] = (du * hv).astype(BF16)
        dc_ref[:, 2 * W:3 * W] = (du * cg).astype(BF16)
        rowi = _iota((8, W), 0)
        dw = (jnp.where(rowi == 0, jnp.sum(dy * u2, axis=0, keepdims=True), 0.0)
              + jnp.where(rowi == 1, jnp.sum(dy * u1, axis=0, keepdims=True), 0.0)
              + jnp.where(rowi == 2, jnp.sum(dy * u, axis=0, keepdims=True), 0.0))

        @pl.when(b == 0)
        def _():
            dw_ref[...] = dw

        @pl.when(b > 0)
        def _():
            dw_ref[...] += dw

    return pl.pallas_call(
        body, name="conv_bwd", grid=(BL,),
        in_specs=[pl.BlockSpec((S, CONVW), lambda b: (b, 0)), pl.BlockSpec((8, W), lambda b: (0, 0)),
                  pl.BlockSpec((S, W), lambda b: (b, 3))],
        out_specs=[pl.BlockSpec((S, CONVW), lambda b: (b, 0)), pl.BlockSpec((8, W), lambda b: (0, 0))],
        out_shape=[jax.ShapeDtypeStruct((T, CONVW), BF16), jax.ShapeDtypeStruct((8, W), F32)],
        compiler_params=_cp(("arbitrary",)))(conv, cw, dmixed)


def _place():
    x, y, c = lax.axis_index("x"), lax.axis_index("y"), lax.axis_index("c")
    return x, y, c


def _chips_of(x, y):
    return [(1 - x, y), (x, 1 - y), (1 - x, 1 - y)]


def _dev(p):
    return 4 * p[0] + 2 * p[1] + p[2]


def _gather_job_a(shards):
    n = len(shards)

    def peers(x, y, c):
        return [(x, y, 1 - c)] + [(*chip, c) for chip in _chips_of(x, y)]

    def start(ins, outs, sems):
        send, recv, loc = sems
        x, y, c = _place()
        me = (x, y, c)
        cps = []
        for a in range(n):
            cps.append(pltpu.make_async_copy(ins[a], outs[a].at[_dev(me)], loc.at[a]))
            for k, peer in enumerate(peers(x, y, c)):
                cps.append(pltpu.make_async_remote_copy(
                    src_ref=ins[a], dst_ref=outs[a].at[_dev(me)], send_sem=send.at[a, k], recv_sem=recv.at[a, k],
                    device_id=peer, device_id_type=MESH))
        for cp in cps:
            cp.start()
        return cps

    def finish(cps, ins, outs, sems):
        send, recv, loc = sems
        x, y, c = _place()
        for a in range(n):
            for k, peer in enumerate(peers(x, y, c)):
                pltpu.make_async_remote_copy(
                    src_ref=ins[a], dst_ref=outs[a].at[_dev(peer)], send_sem=send.at[a, k], recv_sem=recv.at[a, k],
                    device_id=(x, y, c), device_id_type=MESH).wait_recv()
        for a in range(n):
            cps[5 * a].wait()
            for k in range(4):
                cps[5 * a + 1 + k].wait_send()

    return _Job(shards, [jax.ShapeDtypeStruct((NDEV,) + s.shape, s.dtype) for s in shards], {},
                [pltpu.SemaphoreType.DMA((n, 4)), pltpu.SemaphoreType.DMA((n, 4)), pltpu.SemaphoreType.DMA((n,))],
                start, finish)


def _gather_job_b(gathered):
    n = len(gathered)

    def start(ins, outs, sems):
        send, recv = sems
        x, y, c = _place()
        cps = []
        for a in range(n):
            for j, chip in enumerate(_chips_of(x, y)):
                blk = outs[a].at[_dev((*chip, c))]
                cps.append(pltpu.make_async_remote_copy(
                    src_ref=blk, dst_ref=blk, send_sem=send.at[a, j], recv_sem=recv.at[a, j],
                    device_id=(x, y, 1 - c), device_id_type=MESH))
        for cp in cps:
            cp.start()
        return cps

    def finish(cps, ins, outs, sems):
        send, recv = sems
        x, y, c = _place()
        for a in range(n):
            for j, chip in enumerate(_chips_of(x, y)):
                blk = outs[a].at[_dev((*chip, 1 - c))]
                pltpu.make_async_remote_copy(
                    src_ref=blk, dst_ref=blk, send_sem=send.at[a, j], recv_sem=recv.at[a, j],
                    device_id=(x, y, c), device_id_type=MESH).wait_recv()
        for cp in cps:
            cp.wait_send()

    return _Job(gathered, [jax.ShapeDtypeStruct(g.shape, g.dtype) for g in gathered], {a: a for a in range(n)},
                [pltpu.SemaphoreType.DMA((n, 3)), pltpu.SemaphoreType.DMA((n, 3))], start, finish)


def _sibling_job(grads):
    n = len(grads)

    def start(ins, outs, sems):
        send, recv = sems
        x, y, c = _place()
        cps = [pltpu.make_async_remote_copy(
            src_ref=ins[a].at[:, 1 - c], dst_ref=outs[a], send_sem=send.at[a], recv_sem=recv.at[a],
            device_id=(x, y, 1 - c), device_id_type=MESH) for a in range(n)]
        for cp in cps:
            cp.start()
        return cps

    def finish(cps, ins, outs, sems):
        for cp in cps:
            cp.wait()

    return _Job(grads, [jax.ShapeDtypeStruct(g.shape[:1] + g.shape[2:], F32) for g in grads], {},
                [pltpu.SemaphoreType.DMA((n,)), pltpu.SemaphoreType.DMA((n,))], start, finish)


def _chip_job(psums):
    n = len(psums)

    def copies(ins, outs, sems):
        send, recv, loc = sems
        x, y, c = _place()
        mychip = 2 * x + y
        cps = []
        for a in range(n):
            cps.append(pltpu.make_async_copy(ins[a].at[mychip], outs[a].at[mychip], loc.at[a]))
            for j, chip in enumerate(_chips_of(x, y)):
                cps.append(pltpu.make_async_remote_copy(
                    src_ref=ins[a].at[2 * chip[0] + chip[1]], dst_ref=outs[a].at[mychip],
                    send_sem=send.at[a, j], recv_sem=recv.at[a, j], device_id=(*chip, c), device_id_type=MESH))
        return cps

    def start(ins, outs, sems):
        for cp in copies(ins, outs, sems):
            cp.start()

    def finish(_, ins, outs, sems):
        cps = copies(ins, outs, sems)
        send, recv, loc = sems
        x, y, c = _place()
        mychip = 2 * x + y
        for a in range(n):
            for j, chip in enumerate(_chips_of(x, y)):
                pltpu.make_async_remote_copy(
                    src_ref=ins[a].at[mychip], dst_ref=outs[a].at[2 * chip[0] + chip[1]],
                    send_sem=send.at[a, j], recv_sem=recv.at[a, j], device_id=(x, y, c), device_id_type=MESH).wait_recv()
        for a in range(n):
            cps[4 * a].wait()
            for j in range(3):
                cps[4 * a + 1 + j].wait_send()

    return _Job(psums, [jax.ShapeDtypeStruct(p.shape, BF16) for p in psums], {},
                [pltpu.SemaphoreType.DMA((n, 3)), pltpu.SemaphoreType.DMA((n, 3)), pltpu.SemaphoreType.DMA((n,))],
                start, finish)


def _join_jobs(*jobs):
    jobs = [j for j in jobs if j is not None]
    if len(jobs) <= 1:
        return jobs[0] if jobs else None
    cut = lambda seq, sizes: [seq[sum(sizes[:k]):sum(sizes[:k + 1])] for k in range(len(sizes))]
    n_in = [len(j.ins) for j in jobs]
    n_out = [len(j.out_shapes) for j in jobs]
    n_sem = [len(j.sems) for j in jobs]
    aliases = {}
    for k, j in enumerate(jobs):
        for a, b in j.aliases.items():
            aliases[sum(n_in[:k]) + a] = sum(n_out[:k]) + b

    def start(ins, outs, sems):
        return [j.start(i, o, s) for j, i, o, s in zip(jobs, cut(ins, n_in), cut(outs, n_out), cut(sems, n_sem))]

    def finish(sts, ins, outs, sems):
        for j, st, i, o, s in zip(jobs, sts, cut(ins, n_in), cut(outs, n_out), cut(sems, n_sem)):
            j.finish(st, i, o, s)

    return _Job([t for j in jobs for t in j.ins], [t for j in jobs for t in j.out_shapes], aliases,
                [t for j in jobs for t in j.sems], start, finish)


def _run_job(job, name):
    def body(ins, outs, scr, comm):
        comm[1](comm[0]())

    return _host_call(body, name, [], [], [], [], [], {}, job)[1]


def _allreduce_small(v):
    def body(v_ref, o_ref, slots, send_sems, recv_sems):
        x, y, c = _place()
        me = 4 * x + 2 * y + c
        slots[me] = v_ref[...]

        def copy(k):
            peer = (x ^ ((k >> 2) & 1), y ^ ((k >> 1) & 1), c ^ (k & 1))
            return pltpu.make_async_remote_copy(
                src_ref=v_ref, dst_ref=slots.at[me], send_sem=send_sems.at[k - 1], recv_sem=recv_sems.at[k - 1],
                device_id=peer, device_id_type=MESH)

        def arrival(k):
            return pltpu.make_async_remote_copy(
                src_ref=v_ref, dst_ref=slots.at[me ^ k], send_sem=send_sems.at[k - 1], recv_sem=recv_sems.at[k - 1],
                device_id=(x, y, c), device_id_type=MESH)

        sends = [copy(k) for k in range(1, NDEV)]
        for cp in sends:
            cp.start()
        for k in range(1, NDEV):
            arrival(k).wait_recv()
        for cp in sends:
            cp.wait_send()
        acc = slots[0]
        for d in range(1, NDEV):
            acc = acc + slots[d]
        o_ref[...] = acc

    return pl.pallas_call(
        body, name="allreduce_small", in_specs=[VMEM_SPEC], out_specs=VMEM_SPEC,
        out_shape=jax.ShapeDtypeStruct(v.shape, F32),
        scratch_shapes=[pltpu.VMEM((NDEV,) + v.shape, F32), pltpu.SemaphoreType.DMA((NDEV - 1,)),
                        pltpu.SemaphoreType.DMA((NDEV - 1,))],
        )(v)


def _pair_sums(views, gots, core):
    n = len(views)

    def body(c_ref, *refs):
        for a in range(n):
            refs[2 * n + a][...] = (refs[a][...] + refs[n + a][...]).astype(BF16)

    def vspec(v):
        return pl.BlockSpec((None, None) + v.shape[2:], lambda k, c: (k, c[0], 0, 0))

    def gspec(g):
        return pl.BlockSpec((None,) + g.shape[1:], lambda k, c: (k, 0, 0))

    return pl.pallas_call(
        body, name="pair_sums",
        grid_spec=pltpu.PrefetchScalarGridSpec(
            num_scalar_prefetch=1, grid=(4,),
            in_specs=[vspec(v) for v in views] + [gspec(g) for g in gots],
            out_specs=[gspec(g) for g in gots]),
        out_shape=[jax.ShapeDtypeStruct(g.shape, BF16) for g in gots],
        compiler_params=_cp(("parallel",)))(core, *views, *gots)


def _chip_sums(parts):
    n = len(parts)

    def body(*refs):
        for a in range(n):
            acc = refs[a][0].astype(F32)
            for k in range(1, 4):
                acc = acc + refs[a][k].astype(F32)
            refs[n + a][...] = acc

    return pl.pallas_call(
        body, name="chip_sums", in_specs=[VMEM_SPEC] * n, out_specs=[VMEM_SPEC] * n,
        out_shape=[jax.ShapeDtypeStruct(p.shape[1:], F32) for p in parts], compiler_params=_cp())(*parts)


def _permute_in(w):
    lead = w.shape[:-1]
    return w.reshape(lead + (3, 3, 2, BQ)).swapaxes(-2, -3).reshape(lead + (QKVW,))


def _unpermute_in(w):
    lead = w.shape[:-1]
    return w.reshape(lead + (3, 2, 3, BQ)).swapaxes(-2, -3).reshape(lead + (QKVW,))


def _row(v):
    v = v.reshape(-1)
    return jnp.pad(v, (0, D - v.shape[0])).reshape(1, D)


def kernel(x, w_in, f_bias, conv_w, w_out, rel_bias, ln1_g, ln1_b, w_gate, w_up, w_down, ln2_g, ln2_b, loss_target, m_w_in, m_f_bias, m_conv_w, m_w_out, m_rel_bias, m_ln1_g, m_ln1_b, m_w_gate, m_w_up, m_w_down, m_ln2_g, m_ln2_b, v_w_in, v_f_bias, v_conv_w, v_w_out, v_rel_bias, v_ln1_g, v_ln1_b, v_w_gate, v_w_up, v_w_down, v_ln2_g, v_ln2_b):
    xi, yi, ci = _place()
    me = 4 * xi + 2 * yi + ci
    core = jnp.reshape(ci, (1,)).astype(jnp.int32)

    win_s = jnp.concatenate([_permute_in(w_in[..., :QKVW]), w_in[..., QKVW:]], axis=-1)
    win_s = jnp.pad(win_s, ((0, 0), (0, 0), (0, NPAD - NPROJ))).astype(BF16)
    per_layer = [win_s, w_out.astype(BF16), jnp.swapaxes(w_gate, 1, 2).astype(BF16),
                 jnp.swapaxes(w_up, 1, 2).astype(BF16), w_down.astype(BF16)]
    sh = [[s[l] for s in per_layer] for l in range(2)]

    def whole(g):
        return g.reshape(NDEV * g.shape[1], g.shape[2])

    first = _run_job(_gather_job_b(_run_job(_gather_job_a(sh[0][:1]), "gather_a")), "gather_b")
    W = [{"win": whole(first[0])}, {}]

    cw_rows = lax.dynamic_update_slice(jnp.zeros((2, 3, 256), F32), conv_w, (0, 0, me * 32))
    small = jnp.concatenate([_row(cw_rows[0]), _row(cw_rows[1]), jnp.zeros((SMALL_ROWS - 2, D), F32)], axis=0)
    small = _allreduce_small(small)
    cw_full = small[0:2, :CONVW].reshape(2, 3, 256)
    cw8 = jnp.pad(cw_full, ((0, 0), (0, 5), (0, 0)))
    fb = jnp.pad(f_bias, ((0, 0), (0, GATEW - NH))).reshape(2, 1, GATEW)
    tbl = _dil_table(rel_bias)

    def wcol(K, tn, off):
        return pl.BlockSpec((K, tn), lambda i, j: (0, off + j))

    def wrow(tn, K, blk=0):
        return pl.BlockSpec((tn, K), lambda i, j: (j, blk))

    def arow(tm, K, blk=0):
        return pl.BlockSpec((tm, K), lambda i, j: (i, blk))

    h = x.reshape(T, D)
    hb = h.astype(BF16)
    saved = []
    for l in range(2):
        Win = W[l]["win"]
        qkv = _mm([(hb, arow(1024, D), Win, wcol(D, 768, 0))], nt=False, M=T, N=QKVW, tm=1024, tn=768,
                  out_dtype=BF16, name="proj_qkv")
        conv = _mm([(hb, arow(512, D), Win, wcol(D, 768, 3))], nt=False, M=T, N=CONVW, tm=512, tn=768,
                   out_dtype=F32, name="proj_conv")
        gate = _mm([(hb, arow(512, D), Win, wcol(D, 128, 24))], nt=False, M=T, N=GATEW, tm=512, tn=128,
                   out_dtype=F32, name="proj_gate")
        cum = _fox_prep(gate, fb[l])
        cq = cum[:, :NH].reshape(BL, S, NH).transpose(0, 2, 1).reshape(NSTAT, S)
        ckb = jnp.broadcast_to(cq[:, :, None], (NSTAT, S, BQ))
        if l == 0:
            mixed, rtot, a0 = _sb_fwd(qkv, job=_gather_job_a(sh[0][1:]))
            mixed, lse_d, ex = _flash_fwd(qkv, mixed, 1, False, (tbl,),
                                          job=_join_jobs(_gather_job_b(list(a0)), _gather_job_a(sh[1][:2])))
            W[0].update(zip(("wout", "wgT", "wuT", "wd"), [whole(t) for t in ex[:4]]))
            mixed, lse_f, o_fox, ex = _flash_fwd(qkv, mixed, 2, True, (cq, ckb),
                                                 job=_join_jobs(_gather_job_b(list(ex[4:])), _gather_job_a(sh[1][2:])))
            W[1].update(zip(("win", "wout"), [whole(t) for t in ex[:2]]))
            a2 = list(ex[2:])
        else:
            mixed, rtot, ex = _sb_fwd(qkv, job=_gather_job_b(a2))
            W[1].update(zip(("wgT", "wuT", "wd"), [whole(t) for t in ex]))
            mixed, lse_d, _ = _flash_fwd(qkv, mixed, 1, False, (tbl,))
            mixed, lse_f, o_fox, _ = _flash_fwd(qkv, mixed, 2, True, (cq, ckb))
        Wout, WgT, WuT, Wd = W[l]["wout"], W[l]["wgT"], W[l]["wuT"], W[l]["wd"]
        mixed = _conv_fwd(conv, cw8[l], mixed)
        x1, xh1, r1, x1b = _mm_ln(mixed, Wout, h, ln1_g[l:l + 1], ln1_b[l:l + 1], "out_proj_ln")
        fs, ft, a, x2, xh2, r2, x2b = _ffn_fwd(x1b, x1, WgT, WuT, Wd, ln2_g[l:l + 1], ln2_b[l:l + 1])
        saved.append(dict(h=hb, qkv=qkv, conv=conv, gate=gate, cq=cq, ckb=ckb, mixed=mixed, rtot=rtot, lse_d=lse_d,
                          lse_f=lse_f, o_fox=o_fox, x1=x1b, xh1=xh1, r1=r1, fs=fs, ft=ft, a=a, xh2=xh2, r2=r2))
        h, hb = x2, x2b

    sq, dy = _loss_grad(h, loss_target.reshape(T, D))

    def view(gr):
        return gr.reshape(4, 2, gr.shape[0] // NDEV, gr.shape[1])

    G = [None, None]
    small_g = {}
    shard_g = {}
    for l in (1, 0):
        sv = saved[l]
        Win, Wout, WgT, WuT, Wd = W[l]["win"], W[l]["wout"], W[l]["wgT"], W[l]["wuT"], W[l]["wd"]
        ds2, dg2, db2, ds2b = _ln_bwd(dy, sv["xh2"], sv["r2"], ln2_g[l:l + 1])
        dgt, dut, dx1 = _ffn_bwd(ds2b, ds2, sv["fs"], sv["ft"], Wd, WgT, WuT)
        G_d = _mm_tn(sv["a"], ds2b, None, C=D, Ka=DFF, N=D, tm=1408, tn=1024, tk=1024, ooff=0, name="grad_w_down")
        G_g = _mm_tn(dgt, sv["x1"], None, C=D, Ka=DFF, N=D, tm=1408, tn=1024, tk=1024, ooff=0, name="grad_w_gate")
        G_u = _mm_tn(dut, sv["x1"], None, C=D, Ka=DFF, N=D, tm=1408, tn=1024, tk=1024, ooff=0, name="grad_w_up")
        ds1, dg1, db1, ds1b = _ln_bwd(dx1, sv["xh1"], sv["r1"], ln1_g[l:l + 1])
        G_out = _mm_tn(sv["mixed"], ds1b, None, C=D, Ka=D, N=D, tm=1024, tn=1024, tk=1024, ooff=0, name="grad_w_out")
        dmixed = _mm([(ds1b, arow(512, D), Wout, wrow(512, D))], nt=True, M=T, N=D, tm=512, tn=512,
                     out_dtype=BF16, name="out_proj_dx")
        early = [view(t) for t in (G_g, G_u, G_d, G_out)] + ([view(G[1]["in"])] if l == 0 else [])
        dqkv, gots = _sb_bwd(sv["qkv"], dmixed, sv["rtot"], job=_sibling_job(early))
        ps = _pair_sums(early, list(gots), core)
        dqkv, dtbl, pa = _flash_bwd(sv["qkv"], sv["mixed"], dmixed, sv["lse_d"], dqkv, 1, False, (tbl,),
                                    job=_chip_job(ps[:2]))
        dqkv, dck, pb = _flash_bwd(sv["qkv"], sv["o_fox"], dmixed, sv["lse_f"], dqkv, 2, True,
                                   (sv["cq"], sv["ckb"]), job=_chip_job(ps[2:]))
        sums = _chip_sums(list(pa) + list(pb))
        shard_g[l] = dict(zip(("g", "u", "d", "out"), sums[:4]))
        if l == 0:
            shard_g[1]["in"] = sums[4]
        dconv, dcw = _conv_bwd(sv["conv"], cw8[l], dmixed)
        dcum = jnp.pad(dck.reshape(S, BL, NH).transpose(1, 0, 2).reshape(T, NH), ((0, 0), (0, GATEW - NH)))
        dgate, dfb = _fox_post(dcum, sv["gate"], fb[l])
        drb = _dil_table_bwd(dtbl)
        G_in = _mm_tn(sv["h"], dqkv, None, C=NPAD, Ka=D, N=QKVW, tm=1024, tn=768, tk=1024, ooff=0, name="grad_w_in_qkv")
        G_in = _mm_tn(sv["h"], dconv, G_in, C=NPAD, Ka=D, N=CONVW, tm=1024, tn=768, tk=1024, ooff=3,
                      name="grad_w_in_conv")
        G_in = _mm_tn(sv["h"], dgate, G_in, C=NPAD, Ka=D, N=GATEW, tm=1024, tn=128, tk=1024, ooff=24,
                      name="grad_w_in_gate")
        G[l] = {"in": G_in, "out": G_out, "g": G_g, "u": G_u, "d": G_d}
        tail = None
        if l == 0:
            late = [view(G_in)]
            tail = _chip_job(_pair_sums(late, list(_run_job(_sibling_job(late), "sibling_exchange")), core))
        dy = _mm([(dqkv, arow(1024, QKVW), Win, wrow(512, QKVW, 0)),
                  (dconv, arow(1024, CONVW), Win, wrow(512, CONVW, 3)),
                  (dgate, arow(1024, GATEW), Win, wrow(512, GATEW, 24))],
                 nt=True, M=T, N=D, tm=1024, tn=512, out_dtype=F32, name="proj_dx", res=ds1, res_scale=ALPHA, job=tail)
        if l == 0:
            dy, parts = dy
            shard_g[0]["in"] = _chip_sums(list(parts))[0]
        small_g[l] = dict(ln1_g=dg1, ln1_b=db1, ln2_g=dg2, ln2_b=db2, cw=dcw[0:3].reshape(1, CONVW),
                          fb=dfb[:, :NH], rb=drb[:, :NH])
    grad_x = dy.reshape(BL, S, D)

    rows = []
    for name in ("ln1_g", "ln1_b", "ln2_g", "ln2_b"):
        rows += [small_g[0][name], small_g[1][name]]
    rows += [_row(small_g[0]["cw"]), _row(small_g[1]["cw"]),
             _row(jnp.concatenate([small_g[0]["fb"], small_g[1]["fb"]], axis=0)),
             _row(small_g[0]["rb"] + small_g[1]["rb"]), _row(sq)]
    rows.append(jnp.zeros((SMALL_ROWS - len(rows), D), F32))
    sg = _allreduce_small(jnp.concatenate(rows, axis=0))
    loss = sg[12, 0] * (0.5 / D)
    g_ln1_g, g_ln1_b, g_ln2_g, g_ln2_b = sg[0:2], sg[2:4], sg[4:6], sg[6:8]
    g_conv_full = sg[8:10, :CONVW].reshape(2, 3, 256)
    g_conv = lax.dynamic_slice(g_conv_full, (0, 0, me * 32), (2, 3, 32))
    g_fb = sg[10, :2 * NH].reshape(2, NH)
    g_rb = sg[11, :32 * NH].reshape(32, NH)

    def both(name):
        return jnp.stack([shard_g[0][name], shard_g[1][name]])

    g_in = both("in")
    g_w_in = jnp.concatenate([_unpermute_in(g_in[..., :QKVW]), g_in[..., QKVW:NPROJ]], axis=-1)
    g_w_out = both("out")
    g_w_gate = jnp.swapaxes(both("g"), 1, 2)
    g_w_up = jnp.swapaxes(both("u"), 1, 2)
    g_w_down = both("d")

    up_in = _adamw(w_in, g_w_in, m_w_in, v_w_in, 64)
    up_out = _adamw(w_out, g_w_out, m_w_out, v_w_out, 128)
    up_gate = _adamw(w_gate, g_w_gate, m_w_gate, v_w_gate, 256)
    up_up = _adamw(w_up, g_w_up, m_w_up, v_w_up, 256)
    up_down = _adamw(w_down, g_w_down, m_w_down, v_w_down, 352)

    def pack(fbv, cwv, rbv, l1g, l1b, l2g, l2b):
        r = [l1g, l1b, l2g, l2b, _row(cwv), _row(fbv), _row(rbv)]
        r.append(jnp.zeros((SMALL_ROWS - 11, D), F32))
        return jnp.concatenate(r, axis=0)

    pw = pack(f_bias, conv_w, rel_bias, ln1_g, ln1_b, ln2_g, ln2_b)
    pg = pack(g_fb, g_conv, g_rb, g_ln1_g, g_ln1_b, g_ln2_g, g_ln2_b)
    pm = pack(m_f_bias, m_conv_w, m_rel_bias, m_ln1_g, m_ln1_b, m_ln2_g, m_ln2_b)
    pv = pack(v_f_bias, v_conv_w, v_rel_bias, v_ln1_g, v_ln1_b, v_ln2_g, v_ln2_b)
    ups = [u[0] for u in _adamw(pw[None], pg[None], pm[None], pv[None], SMALL_ROWS)]

    def unpack(p):
        return dict(ln1_g=p[0:2], ln1_b=p[2:4], ln2_g=p[4:6], ln2_b=p[6:8],
                    conv_w=p[8, :192].reshape(2, 3, 32), f_bias=p[9, :2 * NH].reshape(2, NH),
                    rel_bias=p[10, :32 * NH].reshape(32, NH))

    sm = [unpack(p) for p in ups]

    def group(k):
        return (up_in[k], sm[k]["f_bias"], sm[k]["conv_w"], up_out[k], sm[k]["rel_bias"], sm[k]["ln1_g"],
                sm[k]["ln1_b"], up_gate[k], up_up[k], up_down[k], sm[k]["ln2_g"], sm[k]["ln2_b"])

    grads = (g_w_in, g_fb, g_conv, g_w_out, g_rb, g_ln1_g, g_ln1_b, g_w_gate, g_w_up, g_w_down, g_ln2_g, g_ln2_b)
    return (loss, grad_x) + grads + group(0) + group(1) + group(2)
```

```python
import math

import numpy as np
import jax
import jax.numpy as jnp
from jax import lax
from jax.experimental import pallas as pl
from jax.experimental.pallas import tpu as pltpu

F32 = jnp.float32
BF16 = jnp.bfloat16
MESH = pl.DeviceIdType.MESH

D = 1024
S = 2048
BL = 2
T = BL * S
NH = 4
DFF = 2816
NPROJ = 3076
NPAD = 3200
QKVW = 2304
CONVW = 768
GATEW = 128
PAIRW = 384
BQ = 128
HB = 2 * BQ
NB = S // BQ
NDEV = 8
NSTAT = BL * NH
ALPHA = 4.0 ** 0.25
SCALE = 0.125
NEG = -1e30
LN_EPS = 1e-5
ADAM_LR, ADAM_B1, ADAM_B2, ADAM_EPS, ADAM_WD, ADAM_STEP = 0.001, 0.9, 0.999, 1e-08, 0.01, 10
VMEM_LIMIT = 56 * 1024 * 1024
SMALL_ROWS = 16


def _bucket_thresholds():
    d = np.arange(0, S)
    nf = np.maximum(d, 1).astype(np.float32)
    large = 16 + (np.log(nf / np.float32(16)) / np.float32(math.log(128)) * np.float32(16)).astype(np.int32)
    b = np.where(d < 16, d, np.minimum(large, 31))
    return [int(np.argmax(b >= k)) for k in range(32)]


BUCKET_TH = _bucket_thresholds()


def _cp(sem=None):
    return pltpu.CompilerParams(dimension_semantics=sem, vmem_limit_bytes=VMEM_LIMIT)


def _dot(a, b):
    return lax.dot_general(a, b, (((1,), (0,)), ((), ())), preferred_element_type=F32)


def _dot_nt(a, b):
    return lax.dot_general(a, b, (((1,), (1,)), ((), ())), preferred_element_type=F32)


def _dot_tn(a, b):
    return lax.dot_general(a, b, (((0,), (0,)), ((), ())), preferred_element_type=F32)


def _split2(x):
    hi = x.astype(BF16)
    mid = (x - hi.astype(F32)).astype(BF16)
    return jnp.concatenate([hi, mid], axis=1)


def _split3(x):
    hi = x.astype(BF16)
    r = x - hi.astype(F32)
    mid = r.astype(BF16)
    lo = (r - mid.astype(F32)).astype(BF16)
    return jnp.concatenate([hi, mid, lo], axis=1)


def _log_sigmoid(u):
    return jnp.minimum(u, 0.0) - jnp.log1p(jnp.exp(-jnp.abs(u)))


def _log_sigmoid_tile(u):
    return jnp.minimum(u, 0.0) - jnp.log(1.0 + jnp.exp(jnp.minimum(u, -u)))


def _iota(shape, dim):
    return lax.broadcasted_iota(jnp.int32, shape, dim)


ANY_SPEC = pl.BlockSpec(memory_space=pl.ANY)
VMEM_SPEC = pl.BlockSpec(memory_space=pltpu.VMEM)


def _mm(pairs, *, nt, M, N, tm, tn, out_dtype, name, res=None, res_scale=1.0, job=None):
    n = len(pairs)
    n_in = 2 * n + (res is not None)
    jins = job.ins if job else []
    jouts = job.out_shapes if job else []
    gi, gj = M // tm, N // tn

    def body(*refs):
        o_ref = refs[n_in + len(jins)]
        if job:
            jrefs = (refs[n_in:n_in + len(jins)], refs[n_in + len(jins) + 1:n_in + len(jins) + 1 + len(jouts)],
                     refs[n_in + len(jins) + 1 + len(jouts):])

            @pl.when((pl.program_id(0) == 0) & (pl.program_id(1) == 0))
            def _():
                job.start(*jrefs)

        acc = None
        for p in range(n):
            a = refs[2 * p][...].astype(BF16)
            b = refs[2 * p + 1][...]
            d = _dot_nt(a, b) if nt else _dot(a, b)
            acc = d if acc is None else acc + d
        if res is not None:
            acc = acc + res_scale * refs[2 * n][...]
        o_ref[...] = acc.astype(out_dtype)
        if job:
            @pl.when((pl.program_id(0) == gi - 1) & (pl.program_id(1) == gj - 1))
            def _():
                job.finish(None, *jrefs)

    ops, specs = [], []
    for a, asp, b, bsp in pairs:
        ops += [a, b]
        specs += [asp, bsp]
    if res is not None:
        ops.append(res)
        specs.append(pl.BlockSpec((tm, tn), lambda i, j: (i, j)))
    out = pl.pallas_call(
        body, name=name, grid=(gi, gj), in_specs=specs + [ANY_SPEC] * len(jins),
        out_specs=[pl.BlockSpec((tm, tn), lambda i, j: (i, j))] + [ANY_SPEC] * len(jouts),
        out_shape=[jax.ShapeDtypeStruct((M, N), out_dtype)] + list(jouts),
        scratch_shapes=list(job.sems) if job else [],
        input_output_aliases={n_in + a: 1 + b for a, b in job.aliases.items()} if job else {},
        compiler_params=_cp(("arbitrary", "arbitrary") if job else ("parallel", "parallel")))(*ops, *jins)
    return (out[0], out[1:]) if job else out[0]


def _mm_tn(a, b, gbuf, *, C, Ka, N, tm, tn, tk, ooff, name):
    def body(*refs):
        a_ref, b_ref, o_ref = refs[0], refs[1], refs[-1]
        k = pl.program_id(2)
        d = _dot_tn(a_ref[...].astype(BF16), b_ref[...].astype(BF16))

        @pl.when(k == 0)
        def _():
            o_ref[...] = d

        @pl.when(k > 0)
        def _():
            o_ref[...] += d

    ops = [a, b] + ([] if gbuf is None else [gbuf])
    return pl.pallas_call(
        body, name=name, grid=(Ka // tm, N // tn, T // tk),
        in_specs=[pl.BlockSpec((tk, tm), lambda i, j, k: (k, i)),
                  pl.BlockSpec((tk, tn), lambda i, j, k: (k, j))] + ([] if gbuf is None else [ANY_SPEC]),
        out_specs=pl.BlockSpec((tm, tn), lambda i, j, k: (i, ooff + j)),
        out_shape=jax.ShapeDtypeStruct((Ka, C), F32),
        input_output_aliases={} if gbuf is None else {2: 0},
        compiler_params=_cp(("parallel", "parallel", "arbitrary")))(*ops)


def _ffn_up(x1, wgt, wut):
    tm, tn = 1024, 256

    def body(x_ref, wg_ref, wu_ref, g_ref, u_ref, a_ref):
        ch = 256
        for r in range(0, tm, ch):
            xb = x_ref[r:r + ch, :]
            g = _dot_nt(xb, wg_ref[...])
            u = _dot_nt(xb, wu_ref[...])
            g_ref[r:r + ch, :] = g.astype(BF16)
            u_ref[r:r + ch, :] = u.astype(BF16)
            a_ref[r:r + ch, :] = (g * jax.nn.sigmoid(g) * u).astype(BF16)

    wspec = pl.BlockSpec((tn, D), lambda i, j: (j, 0))
    ospec = pl.BlockSpec((tm, tn), lambda i, j: (i, j))
    return pl.pallas_call(
        body, name="ffn_up", grid=(T // tm, DFF // tn),
        in_specs=[pl.BlockSpec((tm, D), lambda i, j: (i, 0)), wspec, wspec],
        out_specs=[ospec, ospec, ospec],
        out_shape=[jax.ShapeDtypeStruct((T, DFF), BF16)] * 3,
        compiler_params=_cp(("parallel", "parallel")))(x1, wgt, wut)


def _ffn_da(dffn, wd, s, t):
    tm, tn = 1024, 256

    def body(d_ref, wd_ref, s_ref, t_ref, dg_ref, du_ref):
        ch = 256
        for r in range(0, tm, ch):
            da = _dot_nt(d_ref[r:r + ch, :], wd_ref[...])
            gv = s_ref[r:r + ch, :].astype(F32)
            sg = jax.nn.sigmoid(gv)
            dg_ref[r:r + ch, :] = (da * t_ref[r:r + ch, :].astype(F32) * (sg * (1.0 + gv * (1.0 - sg)))).astype(BF16)
            du_ref[r:r + ch, :] = (da * (gv * sg)).astype(BF16)

    ospec = pl.BlockSpec((tm, tn), lambda i, j: (i, j))
    return pl.pallas_call(
        body, name="ffn_da", grid=(T // tm, DFF // tn),
        in_specs=[pl.BlockSpec((tm, D), lambda i, j: (i, 0)),
                  pl.BlockSpec((tn, D), lambda i, j: (j, 0)), ospec, ospec],
        out_specs=[ospec, ospec],
        out_shape=[jax.ShapeDtypeStruct((T, DFF), BF16), jax.ShapeDtypeStruct((T, DFF), BF16)],
        compiler_params=_cp(("parallel", "parallel")))(dffn, wd, s, t)


def _ffn_fwd(xb, x, wgt, wut, wd, gam, bet):
    tm, ch = 512, 256

    def body(xb_ref, x_ref, g_ref, b_ref, wg_hbm, wu_hbm, wd_hbm,
             go_ref, uo_ref, ao_ref, y_ref, xh_ref, r_ref, yb_ref, wg_v, wu_v, wd_v, sem):
        @pl.when(pl.program_id(0) == 0)
        def _():
            _copy_in(wg_hbm, wg_v, sem)
            _copy_in(wu_hbm, wu_v, sem)
            _copy_in(wd_hbm, wd_v, sem)

        xv = xb_ref[...]
        for c in range(0, DFF, ch):
            gv = _dot_nt(xv, wg_v[c:c + ch, :])
            uv = _dot_nt(xv, wu_v[c:c + ch, :])
            go_ref[:, c:c + ch] = gv.astype(BF16)
            uo_ref[:, c:c + ch] = uv.astype(BF16)
            ao_ref[:, c:c + ch] = (gv * jax.nn.sigmoid(gv) * uv).astype(BF16)
        s = ALPHA * x_ref[...] + _dot(ao_ref[...], wd_v[...])
        mu = jnp.mean(s, axis=-1, keepdims=True)
        xc = s - mu
        var = jnp.mean(xc * xc, axis=-1, keepdims=True)
        r = lax.rsqrt(var + LN_EPS)
        xh = xc * r
        xh_ref[...] = xh.astype(BF16)
        r_ref[...] = r
        y = xh * g_ref[...] + b_ref[...]
        y_ref[...] = y
        yb_ref[...] = y.astype(BF16)

    row = pl.BlockSpec((tm, D), lambda i: (i, 0))
    wide = pl.BlockSpec((tm, DFF), lambda i: (i, 0))
    vec = pl.BlockSpec((1, D), lambda i: (0, 0))
    wsl = pltpu.VMEM((DFF, D), BF16)
    hid = jax.ShapeDtypeStruct((T, DFF), BF16)
    return pl.pallas_call(
        body, name="ffn_fwd", grid=(T // tm,),
        in_specs=[row, row, vec, vec, ANY_SPEC, ANY_SPEC, ANY_SPEC],
        out_specs=[wide, wide, wide, row, row, pl.BlockSpec((tm, 1), lambda i: (i, 0)), row],
        out_shape=[hid, hid, hid, jax.ShapeDtypeStruct((T, D), F32), jax.ShapeDtypeStruct((T, D), BF16),
                   jax.ShapeDtypeStruct((T, 1), F32), jax.ShapeDtypeStruct((T, D), BF16)],
        scratch_shapes=[wsl, wsl, wsl, pltpu.SemaphoreType.DMA],
        compiler_params=_cp(("arbitrary",)))(xb, x, gam, bet, wgt, wut, wd)


def _ffn_bwd(dffn, res, g, u, wd, wgt, wut):
    tm, ch = 512, 256

    def body(d_ref, r_ref, g_ref, u_ref, wd_hbm, wg_hbm, wu_hbm, dg_ref, du_ref, dx_ref, wd_v, wg_v, wu_v, sem):
        @pl.when(pl.program_id(0) == 0)
        def _():
            _copy_in(wd_hbm, wd_v, sem)
            _copy_in(wg_hbm, wg_v, sem)
            _copy_in(wu_hbm, wu_v, sem)

        db = d_ref[...]
        for c in range(0, DFF, ch):
            da = _dot_nt(db, wd_v[c:c + ch, :])
            gv = g_ref[:, c:c + ch].astype(F32)
            sg = jax.nn.sigmoid(gv)
            dg_ref[:, c:c + ch] = (da * u_ref[:, c:c + ch].astype(F32) * (sg * (1.0 + gv * (1.0 - sg)))).astype(BF16)
            du_ref[:, c:c + ch] = (da * (gv * sg)).astype(BF16)
        dx_ref[...] = ALPHA * r_ref[...] + _dot(dg_ref[...], wg_v[...]) + _dot(du_ref[...], wu_v[...])

    row = pl.BlockSpec((tm, D), lambda i: (i, 0))
    wide = pl.BlockSpec((tm, DFF), lambda i: (i, 0))
    wsl = pltpu.VMEM((DFF, D), BF16)
    return pl.pallas_call(
        body, name="ffn_bwd", grid=(T // tm,),
        in_specs=[row, row, wide, wide, ANY_SPEC, ANY_SPEC, ANY_SPEC], out_specs=[wide, wide, row],
        out_shape=[jax.ShapeDtypeStruct((T, DFF), BF16), jax.ShapeDtypeStruct((T, DFF), BF16),
                   jax.ShapeDtypeStruct((T, D), F32)],
        scratch_shapes=[wsl, wsl, wsl, pltpu.SemaphoreType.DMA],
        compiler_params=_cp(("arbitrary",)))(dffn, res, g, u, wd, wgt, wut)


def _mm_ln(a, w, x, gam, bet, name):
    tm = 256
    K = a.shape[1]

    def body(a_ref, w_ref, x_ref, g_ref, b_ref, y_ref, xh_ref, r_ref, yb_ref):
        s = ALPHA * x_ref[...] + _dot(a_ref[...], w_ref[...])
        mu = jnp.mean(s, axis=-1, keepdims=True)
        xc = s - mu
        var = jnp.mean(xc * xc, axis=-1, keepdims=True)
        r = lax.rsqrt(var + LN_EPS)
        xh = xc * r
        xh_ref[...] = xh.astype(BF16)
        r_ref[...] = r
        y = xh * g_ref[...] + b_ref[...]
        y_ref[...] = y
        yb_ref[...] = y.astype(BF16)

    row = pl.BlockSpec((tm, D), lambda i: (i, 0))
    vec = pl.BlockSpec((1, D), lambda i: (0, 0))
    return pl.pallas_call(
        body, name=name, grid=(T // tm,),
        in_specs=[pl.BlockSpec((tm, K), lambda i: (i, 0)), pl.BlockSpec((K, D), lambda i: (0, 0)), row, vec, vec],
        out_specs=[row, row, pl.BlockSpec((tm, 1), lambda i: (i, 0)), row],
        out_shape=[jax.ShapeDtypeStruct((T, D), F32), jax.ShapeDtypeStruct((T, D), BF16),
                   jax.ShapeDtypeStruct((T, 1), F32), jax.ShapeDtypeStruct((T, D), BF16)],
        compiler_params=_cp(("parallel",)))(a, w, x, gam, bet)


def _ln_bwd(dy, xh, r, gam):
    tm = 256

    def body(dy_ref, xh_ref, r_ref, g_ref, ds_ref, dg_ref, db_ref, dsb_ref):
        i = pl.program_id(0)
        dyv = dy_ref[...]
        xhv = xh_ref[...].astype(F32)
        dxh = dyv * g_ref[...]
        m1 = jnp.mean(dxh, axis=-1, keepdims=True)
        m2 = jnp.mean(dxh * xhv, axis=-1, keepdims=True)
        ds = r_ref[...] * (dxh - m1 - xhv * m2)
        ds_ref[...] = ds
        dsb_ref[...] = ds.astype(BF16)
        pg = jnp.sum(dyv * xhv, axis=0, keepdims=True)
        pb = jnp.sum(dyv, axis=0, keepdims=True)

        @pl.when(i == 0)
        def _():
            dg_ref[...] = pg
            db_ref[...] = pb

        @pl.when(i > 0)
        def _():
            dg_ref[...] += pg
            db_ref[...] += pb

    row = pl.BlockSpec((tm, D), lambda i: (i, 0))
    vec = pl.BlockSpec((1, D), lambda i: (0, 0))
    return pl.pallas_call(
        body, name="ln_bwd", grid=(T // tm,),
        in_specs=[row, row, pl.BlockSpec((tm, 1), lambda i: (i, 0)), vec],
        out_specs=[row, vec, vec, row],
        out_shape=[jax.ShapeDtypeStruct((T, D), F32), jax.ShapeDtypeStruct((1, D), F32),
                   jax.ShapeDtypeStruct((1, D), F32), jax.ShapeDtypeStruct((T, D), BF16)],
        compiler_params=_cp(("arbitrary",)))(dy, xh, r, gam)


def _loss_grad(y, tgt):
    tm = 256

    def body(y_ref, t_ref, l_ref, dy_ref):
        i = pl.program_id(0)
        e = y_ref[...] - t_ref[...]
        dy_ref[...] = e * (1.0 / D)
        p = jnp.sum(jnp.sum(e * e, axis=1, keepdims=True), axis=0, keepdims=True)

        @pl.when(i == 0)
        def _():
            l_ref[...] = p

        @pl.when(i > 0)
        def _():
            l_ref[...] += p

    row = pl.BlockSpec((tm, D), lambda i: (i, 0))
    return pl.pallas_call(
        body, name="loss_grad", grid=(T // tm,), in_specs=[row, row],
        out_specs=[pl.BlockSpec((1, 1), lambda i: (0, 0)), row],
        out_shape=[jax.ShapeDtypeStruct((1, 1), F32), jax.ShapeDtypeStruct((T, D), F32)],
        compiler_params=_cp(("arbitrary",)))(y, tgt)


def _adamw(w, g, m, v, tr):
    L, R, C = w.shape

    def body(w_ref, g_ref, m_ref, v_ref, d_ref, m2_ref, v2_ref):
        gv = g_ref[...]
        m2 = ADAM_B1 * m_ref[...] + (1.0 - ADAM_B1) * gv
        v2 = ADAM_B2 * v_ref[...] + (1.0 - ADAM_B2) * (gv * gv)
        m_hat = m2 / (1.0 - ADAM_B1 ** ADAM_STEP)
        v_hat = v2 / (1.0 - ADAM_B2 ** ADAM_STEP)
        d_ref[...] = -ADAM_LR * (m_hat / (jnp.sqrt(v_hat) + ADAM_EPS) + ADAM_WD * w_ref[...])
        m2_ref[...] = m2
        v2_ref[...] = v2

    blk = pl.BlockSpec((None, tr, C), lambda l, i: (l, i, 0))
    sh = jax.ShapeDtypeStruct((L, R, C), F32)
    return pl.pallas_call(
        body, name="adamw", grid=(L, R // tr), in_specs=[blk] * 4, out_specs=[blk] * 3,
        out_shape=[sh, sh, sh], compiler_params=_cp(("parallel", "parallel")))(w, g, m, v)


class _Job:
    def __init__(self, ins, out_shapes, aliases, sems, start, finish):
        self.ins, self.out_shapes, self.aliases, self.sems = list(ins), list(out_shapes), dict(aliases), list(sems)
        self.start, self.finish = start, finish


def _host_call(body, name, ins, in_specs, out_shapes, out_specs, scratch, aliases, job):
    n_in, n_out, n_scr = len(ins), len(out_shapes), len(scratch)
    jins = job.ins if job else []
    jouts = job.out_shapes if job else []
    jsems = job.sems if job else []

    def wrapped(*refs):
        a = n_in
        b = a + len(jins)
        c = b + n_out
        d = c + len(jouts)
        e = d + n_scr
        comm = None
        if job:
            jrefs = (refs[a:b], refs[c:d], refs[e:])
            comm = (lambda: job.start(*jrefs), lambda st: job.finish(st, *jrefs))
        body(refs[:a], refs[b:c], refs[d:e], comm)

    al = dict(aliases)
    if job:
        for ji, jo in job.aliases.items():
            al[n_in + ji] = n_out + jo
    res = pl.pallas_call(
        wrapped, name=name, in_specs=list(in_specs) + [ANY_SPEC] * len(jins),
        out_specs=list(out_specs) + [ANY_SPEC] * len(jouts), out_shape=list(out_shapes) + list(jouts),
        scratch_shapes=list(scratch) + list(jsems), input_output_aliases=al,
        compiler_params=_cp())(*ins, *jins)
    return res[:n_out], res[n_out:]


def _copy_in(src, dst, sem):
    cp = pltpu.make_async_copy(src, dst, sem)
    cp.start()
    cp.wait()


CHAINS = [(p, b) for p in range(2) for b in range(BL)]
NC = len(CHAINS)
ROWS_SHAPE = jax.ShapeDtypeStruct((NSTAT, S), F32)
SLAB_QKV = pltpu.VMEM((T, 2 * PAIRW), BF16)
SLAB_OUT = pltpu.VMEM((T, 2 * BQ), BF16)
SLAB_O32 = pltpu.VMEM((T, 2 * BQ), F32)
SLAB_T = pltpu.VMEM((2, BQ, T), BF16)
SLAB_KEYB = pltpu.VMEM((NSTAT, S, BQ), F32)
ACC_KV = pltpu.VMEM((2, T, BQ), F32)


def _lane_masks():
    lane = _iota((1, BQ), 1)
    m0 = (lane < 64).astype(BF16)
    return m0, 1.0 - m0


def _row_masks():
    r = _iota((BQ, 1), 0)
    m0 = (r < 64).astype(BF16)
    return m0, 1.0 - m0


def _stack(x, m0, m1):
    return jnp.concatenate([x * m0, x * m1], axis=0)


def _stack_t(xt, r0, r1):
    return jnp.concatenate([xt * r0, xt * r1], axis=1)


def _tr(x):
    return x.T


def _rows(b, i):
    return pl.ds(pl.multiple_of(b * S + i * BQ, BQ), BQ)


def _transpose_slab(src, dst, col0):
    def blk(n, _):
        r = pl.ds(pl.multiple_of(n * BQ, BQ), BQ)
        for p in range(2):
            dst[p, :, r] = _tr(src[r, col0(p):col0(p) + BQ])
        return 0

    lax.fori_loop(0, T // BQ, blk, 0)


def _heads(x):
    return x[:BQ], x[BQ:]


def _bcast_heads(r0, r1):
    return jnp.concatenate([jnp.broadcast_to(r0, (BQ, BQ)), jnp.broadcast_to(r1, (BQ, BQ))], axis=0)


def _by_channel(r0, r1):
    return jnp.where(_iota((BQ, BQ), 0) < 64, r0, r1)


def _colsum2(x):
    return jnp.sum(x[:BQ], axis=0, keepdims=True), jnp.sum(x[BQ:], axis=0, keepdims=True)


def _stat_row(ref, p, b, h, i):
    c = b * NH + 2 * p + h
    return ref[c:c + 1, pl.ds(pl.multiple_of(i * BQ, BQ), BQ)]


def _put_row(ref, p, b, h, i, v):
    c = b * NH + 2 * p + h
    ref[c:c + 1, pl.ds(pl.multiple_of(i * BQ, BQ), BQ)] = v


def _valid_t(strict):
    r = _iota((HB, BQ), 0) & (BQ - 1)
    c = _iota((HB, BQ), 1)
    return (r < c) if strict else (r <= c)


def _tri_blockdiag(later):
    r = _iota((HB, HB), 0)
    c = _iota((HB, HB), 1)
    same = (r >= BQ) == (c >= BQ)
    return (same & ((c > r) if later else (c < r))).astype(BF16)


def _cum_mm(tri, x):
    y = _dot(tri, _split2(x))
    return y[:, :BQ] + y[:, BQ:]


def _kv_tiles(qkv_v, p, b, j):
    r = _rows(b, j)
    return qkv_v[r, p * PAIRW + BQ:p * PAIRW + 2 * BQ], qkv_v[r, p * PAIRW + 2 * BQ:p * PAIRW + 3 * BQ]


def _q_tile(qkv_v, p, b, i):
    return qkv_v[_rows(b, i), p * PAIRW:p * PAIRW + BQ] * SCALE


def _sb_fwd(qkv, job=None):
    def body(ins, outs, scr, comm):
        (qkv_hbm,), (o_hbm, r_ref), (qkv_v, o_v, sem, vt_v) = ins, outs, scr
        _copy_in(qkv_hbm.at[:, pl.ds(0, 2 * PAIRW)], qkv_v, sem)
        st = comm[0]() if comm else None
        _transpose_slab(qkv_v, vt_v, lambda p: p * PAIRW + 2 * BQ)
        m0, m1 = _lane_masks()
        r0, r1 = _row_masks()
        valid = _valid_t(True)
        later = _tri_blockdiag(True)

        def steps(qts, i, j, cs, diag):
            ks = [_stack(_kv_tiles(qkv_v, p, b, j)[0], m0, m1) for p, b in CHAINS]
            zs = [_dot(ks[c], qts[c]) for c in range(NC)]
            lbs, lrs = [], []
            for c in range(NC):
                lb = _log_sigmoid_tile(zs[c])
                lr = lb - zs[c]
                if diag:
                    lr = jnp.where(valid, lr, 0.0)
                lbs.append(lb)
                lrs.append(lr)
            tails = [_cum_mm(later, lrs[c]) for c in range(NC)]
            avs = []
            for c in range(NC):
                a = jnp.exp(lbs[c] + tails[c] + _bcast_heads(*cs[c][0]))
                if diag:
                    a = jnp.where(valid, a, 0.0)
                avs.append(a.astype(BF16))
            out = []
            for c, (p, b) in enumerate(CHAINS):
                vts = _stack_t(vt_v[p, :, _rows(b, j)], r0, r1)
                s0, s1 = _colsum2(lrs[c])
                out.append(((cs[c][0][0] + s0, cs[c][0][1] + s1), cs[c][1] + _dot(vts, avs[c])))
            return tuple(out)

        def qblock(i, _):
            qts = [_tr(_q_tile(qkv_v, p, b, i)) for p, b in CHAINS]
            zr = jnp.zeros((1, BQ), F32)
            cs = steps(qts, i, i, (((zr, zr), jnp.zeros((BQ, BQ), F32)),) * NC, True)
            cs = lax.fori_loop(1, i + 1, lambda jj, cs: steps(qts, i, i - jj, cs, False), cs)
            for c, (p, b) in enumerate(CHAINS):
                o_v[_rows(b, i), p * BQ:(p + 1) * BQ] = cs[c][1].T.astype(BF16)
                for h in range(2):
                    _put_row(r_ref, p, b, h, i, cs[c][0][h])
            return 0

        lax.fori_loop(0, NB, qblock, 0)
        _copy_in(o_v, o_hbm.at[:, pl.ds(0, 2 * BQ)], sem)
        if comm:
            comm[1](st)

    (mixed, rtot), extra = _host_call(
        body, "sb_fwd", [qkv], [ANY_SPEC], [jax.ShapeDtypeStruct((T, D), BF16), ROWS_SHAPE], [ANY_SPEC, VMEM_SPEC],
        [SLAB_QKV, SLAB_OUT, pltpu.SemaphoreType.DMA, SLAB_T], {}, job)
    return mixed, rtot, extra


def _sb_bwd(qkv, dmixed, rtot, job=None):
    def body(ins, outs, scr, comm):
        (qkv_hbm, do_hbm, r_ref), (dqkv_hbm,), (qkv_v, do_v, dq_v, dk_s, dv_s, sem, kt_v) = ins, outs, scr
        _copy_in(qkv_hbm.at[:, pl.ds(0, 2 * PAIRW)], qkv_v, sem)
        _copy_in(do_hbm.at[:, pl.ds(0, 2 * BQ)], do_v, sem)
        st = comm[0]() if comm else None
        _transpose_slab(qkv_v, kt_v, lambda p: p * PAIRW + BQ)
        m0, m1 = _lane_masks()
        f0, f1 = m0.astype(F32), m1.astype(F32)
        r0, r1 = _row_masks()
        valid = _valid_t(True)
        later = _tri_blockdiag(True)
        earlier = _tri_blockdiag(False)
        dk_s[...] = jnp.zeros_like(dk_s)
        dv_s[...] = jnp.zeros_like(dv_s)

        def steps(qns, qts, dns, dts, rts, i, j, cs, diag):
            kv = [_kv_tiles(qkv_v, p, b, j) for p, b in CHAINS]
            ks = [_stack(kv[c][0], m0, m1) for c in range(NC)]
            vs = [_stack(kv[c][1], m0, m1) for c in range(NC)]
            zs = [_dot(ks[c], qts[c]) for c in range(NC)]
            das = [_dot(vs[c], dts[c]) for c in range(NC)]
            lbs, lrs, pls = [], [], []
            for c in range(NC):
                lb = _log_sigmoid_tile(zs[c])
                lr = lb - zs[c]
                if diag:
                    lr = jnp.where(valid, lr, 0.0)
                s0, s1 = _colsum2(lr)
                lbs.append(lb)
                lrs.append(lr)
                pls.append((cs[c][0][0] + s0, cs[c][0][1] + s1))
            tails = [_cum_mm(later, lrs[c]) for c in range(NC)]
            avs, gms = [], []
            for c in range(NC):
                a = jnp.exp(lbs[c] + tails[c] + _bcast_heads(rts[c][0] - pls[c][0], rts[c][1] - pls[c][1]))
                if diag:
                    a = jnp.where(valid, a, 0.0)
                avs.append(a)
                gms.append(das[c] * a)
            befores = [_cum_mm(earlier, gms[c]) for c in range(NC)]
            dzbs = []
            for c in range(NC):
                beta = jnp.exp(lbs[c])
                dz = gms[c] - beta * (gms[c] + befores[c] + _bcast_heads(*cs[c][1]))
                if diag:
                    dz = jnp.where(valid, dz, 0.0)
                dzbs.append(dz.astype(BF16))
            out = []
            for c, (p, b) in enumerate(CHAINS):
                dq = cs[c][2] + _dot(_stack_t(kt_v[p, :, _rows(b, j)], r0, r1), dzbs[c])
                dk = _dot(dzbs[c], qns[c])
                dv = _dot(avs[c].astype(BF16), dns[c])
                dk_s[p, _rows(b, j), :] += dk[:BQ] * f0 + dk[BQ:] * f1
                dv_s[p, _rows(b, j), :] += dv[:BQ] * f0 + dv[BQ:] * f1
                g0, g1 = _colsum2(gms[c])
                out.append((pls[c], (cs[c][1][0] + g0, cs[c][1][1] + g1), dq))
            return tuple(out)

        def qblock(i, _):
            qns = [_q_tile(qkv_v, p, b, i) for p, b in CHAINS]
            dns = [do_v[_rows(b, i), p * BQ:(p + 1) * BQ] for p, b in CHAINS]
            qts = [_tr(t) for t in qns]
            dts = [_tr(t) for t in dns]
            rts = [(_stat_row(r_ref, p, b, 0, i), _stat_row(r_ref, p, b, 1, i)) for p, b in CHAINS]
            zr = jnp.zeros((1, BQ), F32)
            cs = (((zr, zr), (zr, zr), jnp.zeros((BQ, BQ), F32)),) * NC
            cs = lax.fori_loop(0, i, lambda j, cs: steps(qns, qts, dns, dts, rts, i, j, cs, False), cs)
            cs = steps(qns, qts, dns, dts, rts, i, i, cs, True)
            for c, (p, b) in enumerate(CHAINS):
                dq_v[_rows(b, i), p * PAIRW:p * PAIRW + BQ] = (cs[c][2].T * SCALE).astype(BF16)
            return 0

        lax.fori_loop(0, NB, qblock, 0)
        for p in range(2):
            dq_v[:, p * PAIRW + BQ:p * PAIRW + 2 * BQ] = dk_s[p].astype(BF16)
            dq_v[:, p * PAIRW + 2 * BQ:p * PAIRW + 3 * BQ] = dv_s[p].astype(BF16)
        _copy_in(dq_v, dqkv_hbm.at[:, pl.ds(0, 2 * PAIRW)], sem)
        if comm:
            comm[1](st)

    (dqkv,), extra = _host_call(
        body, "sb_bwd", [qkv, dmixed, rtot], [ANY_SPEC, ANY_SPEC, VMEM_SPEC],
        [jax.ShapeDtypeStruct((T, QKVW), BF16)], [ANY_SPEC],
        [SLAB_QKV, SLAB_OUT, SLAB_QKV, ACC_KV, ACC_KV, pltpu.SemaphoreType.DMA, SLAB_T], {}, job)
    return dqkv, extra


def _flash_fwd(qkv, mixed, g, fox, bias, job=None):
    def body(ins, outs, scr, comm):
        if fox:
            qkv_hbm, cq_ref, ckb_hbm, _ = ins
            (o_hbm, lse_ref, o32_hbm), (qkv_v, o_v, sem, vt_v, o32_v, ckb_v) = outs, scr
        else:
            qkv_hbm, tbl_ref, _ = ins
            (o_hbm, lse_ref), (qkv_v, o_v, sem, vt_v) = outs, scr
        _copy_in(qkv_hbm.at[:, pl.ds(g * 2 * PAIRW, 2 * PAIRW)], qkv_v, sem)
        if fox:
            _copy_in(ckb_hbm, ckb_v, sem)
        st = comm[0]() if comm else None
        _transpose_slab(qkv_v, vt_v, lambda p: p * PAIRW + 2 * BQ)
        m0, m1 = _lane_masks()
        r0, r1 = _row_masks()
        valid = _valid_t(False)

        def steps(qts, cqs, i, j, cs, diag):
            ks = [_stack(_kv_tiles(qkv_v, p, b, j)[0], m0, m1) for p, b in CHAINS]
            zs = [_dot(ks[c], qts[c]) for c in range(NC)]
            prs, alphas, out = [], [], []
            for c, (p, b) in enumerate(CHAINS):
                (ma, mb), (la, lb_), _ = cs[c]
                if fox:
                    kk = pl.ds(pl.multiple_of(j * BQ, BQ), BQ)
                    col = b * NH + 2 * p
                    z = zs[c] + (cqs[c] - jnp.concatenate([ckb_v[col, kk, :], ckb_v[col + 1, kk, :]], axis=0))
                    if diag:
                        z = jnp.where(valid, z, NEG)
                else:
                    z = zs[c] + tbl_ref[p, i - j]
                za, zb = _heads(z)
                na = jnp.maximum(ma, jnp.max(za, axis=0, keepdims=True))
                nb = jnp.maximum(mb, jnp.max(zb, axis=0, keepdims=True))
                aa, ab = jnp.exp(ma - na), jnp.exp(mb - nb)
                pr = jnp.exp(z - _bcast_heads(na, nb))
                sa, sb = _colsum2(pr)
                prs.append(_split2(pr) if fox else pr.astype(BF16))
                alphas.append((aa, ab))
                out.append(((na, nb), (aa * la + sa, ab * lb_ + sb)))
            pvs = []
            for c, (p, b) in enumerate(CHAINS):
                vts = _stack_t(vt_v[p, :, _rows(b, j)], r0, r1)
                if fox:
                    pvs.append(_dot(vts, prs[c][:, :BQ]) + _dot(vts, prs[c][:, BQ:]))
                else:
                    pvs.append(_dot(vts, prs[c]))
            return tuple((out[c][0], out[c][1], _by_channel(*alphas[c]) * cs[c][2] + pvs[c]) for c in range(NC))

        def qblock(i, _):
            qts = [_tr(_q_tile(qkv_v, p, b, i)) for p, b in CHAINS]
            if fox:
                cqs = [_bcast_heads(_stat_row(cq_ref, p, b, 0, i), _stat_row(cq_ref, p, b, 1, i)) for p, b in CHAINS]
            else:
                cqs = [None] * NC
            ng = jnp.full((1, BQ), NEG, F32)
            zr = jnp.zeros((1, BQ), F32)
            cs = steps(qts, cqs, i, i, (((ng, ng), (zr, zr), jnp.zeros((BQ, BQ), F32)),) * NC, True)
            cs = lax.fori_loop(1, i + 1, lambda jj, cs: steps(qts, cqs, i, i - jj, cs, False), cs)
            for c, (p, b) in enumerate(CHAINS):
                (ma, mb), (la, lb_), acc = cs[c]
                o = (acc / _by_channel(la, lb_)).T
                o_v[_rows(b, i), p * BQ:(p + 1) * BQ] = o.astype(BF16)
                if fox:
                    o32_v[_rows(b, i), p * BQ:(p + 1) * BQ] = o
                _put_row(lse_ref, p, b, 0, i, ma + jnp.log(la))
                _put_row(lse_ref, p, b, 1, i, mb + jnp.log(lb_))
            return 0

        lax.fori_loop(0, NB, qblock, 0)
        _copy_in(o_v, o_hbm.at[:, pl.ds(g * 2 * BQ, 2 * BQ)], sem)
        if fox:
            _copy_in(o32_v, o32_hbm, sem)
        if comm:
            comm[1](st)

    bias_specs = [VMEM_SPEC, ANY_SPEC] if fox else [VMEM_SPEC]
    n_in = 2 + len(bias_specs)
    o32 = [jax.ShapeDtypeStruct((T, 2 * BQ), F32)] if fox else []
    res, extra = _host_call(
        body, "fox_fwd" if fox else "dil_fwd", [qkv, *bias, mixed], [ANY_SPEC] + bias_specs + [ANY_SPEC],
        [jax.ShapeDtypeStruct((T, D), BF16), ROWS_SHAPE] + o32, [ANY_SPEC, VMEM_SPEC] + [ANY_SPEC] * len(o32),
        [SLAB_QKV, SLAB_OUT, pltpu.SemaphoreType.DMA, SLAB_T] + ([SLAB_O32, SLAB_KEYB] if fox else []),
        {n_in - 1: 0}, job)
    return (*res, extra)


def _flash_bwd(qkv, o, dmixed, lse, dqkv, g, fox, bias, job=None):
    def body(ins, outs, scr, comm):
        if fox:
            qkv_hbm, o_hbm, do_hbm, lse_ref, cq_ref, ckb_hbm, _ = ins
            (dqkv_hbm, db_ref), (qkv_v, o_v, do_v, dq_v, dk_s, dv_s, sem, kt_v, ckb_v, dc_s) = outs, scr
        else:
            qkv_hbm, o_hbm, do_hbm, lse_ref, tbl_ref, _ = ins
            (dqkv_hbm, db_ref), (qkv_v, o_v, do_v, dq_v, dk_s, dv_s, sem, kt_v) = outs, scr
        _copy_in(qkv_hbm.at[:, pl.ds(g * 2 * PAIRW, 2 * PAIRW)], qkv_v, sem)
        _copy_in(do_hbm.at[:, pl.ds(g * 2 * BQ, 2 * BQ)], do_v, sem)
        if fox:
            _copy_in(o_hbm, o_v, sem)
            _copy_in(ckb_hbm, ckb_v, sem)
        else:
            _copy_in(o_hbm.at[:, pl.ds(g * 2 * BQ, 2 * BQ)], o_v, sem)
        st = comm[0]() if comm else None
        _transpose_slab(qkv_v, kt_v, lambda p: p * PAIRW + BQ)
        m0, m1 = _lane_masks()
        f0, f1 = m0.astype(F32), m1.astype(F32)
        r0, r1 = _row_masks()
        valid = _valid_t(False)
        dk_s[...] = jnp.zeros_like(dk_s)
        dv_s[...] = jnp.zeros_like(dv_s)
        if fox:
            dc_s[...] = jnp.zeros_like(dc_s)
        else:
            db_ref[...] = jnp.zeros_like(db_ref)

        def steps(qns, qts, dns, dts, cqs, lses, deltas, i, j, dqs, diag):
            kv = [_kv_tiles(qkv_v, p, b, j) for p, b in CHAINS]
            ks = [_stack(kv[c][0], m0, m1) for c in range(NC)]
            vs = [_stack(kv[c][1], m0, m1) for c in range(NC)]
            zs = [_dot(ks[c], qts[c]) for c in range(NC)]
            dps = [_dot(vs[c], dts[c]) for c in range(NC)]
            prs, dzl = [], []
            for c, (p, b) in enumerate(CHAINS):
                if fox:
                    kk = pl.ds(pl.multiple_of(j * BQ, BQ), BQ)
                    col = b * NH + 2 * p
                    z = zs[c] + (cqs[c] - jnp.concatenate([ckb_v[col, kk, :], ckb_v[col + 1, kk, :]], axis=0))
                    if diag:
                        z = jnp.where(valid, z, NEG)
                else:
                    z = zs[c] + tbl_ref[p, i - j]
                pr = jnp.exp(z - lses[c])
                prs.append(pr.astype(BF16))
                dzl.append(pr * (dps[c] - deltas[c]))
            dzbs = [dz.astype(BF16) for dz in dzl]
            new = []
            for c, (p, b) in enumerate(CHAINS):
                new.append(dqs[c] + _dot(_stack_t(kt_v[p, :, _rows(b, j)], r0, r1), dzbs[c]))
                dk = _dot(dzbs[c], qns[c])
                dv = _dot(prs[c], dns[c])
                dk_s[p, _rows(b, j), :] += dk[:BQ] * f0 + dk[BQ:] * f1
                dv_s[p, _rows(b, j), :] += dv[:BQ] * f0 + dv[BQ:] * f1
                if fox:
                    dc_s[c, pl.ds(pl.multiple_of(j * HB, HB), HB), :] += dzl[c]
            if not fox:
                for p in range(2):
                    db_ref[p, i - j] = db_ref[p, i - j] + (dzl[2 * p] + dzl[2 * p + 1])
            return tuple(new)

        def qblock(i, _):
            qns = [_q_tile(qkv_v, p, b, i) for p, b in CHAINS]
            dns = [do_v[_rows(b, i), p * BQ:(p + 1) * BQ] for p, b in CHAINS]
            qts = [_tr(t) for t in qns]
            dts = [_tr(t) for t in dns]
            lses = [_bcast_heads(_stat_row(lse_ref, p, b, 0, i), _stat_row(lse_ref, p, b, 1, i)) for p, b in CHAINS]
            if fox:
                cqs = [_bcast_heads(_stat_row(cq_ref, p, b, 0, i), _stat_row(cq_ref, p, b, 1, i)) for p, b in CHAINS]
            else:
                cqs = [None] * NC
            deltas = []
            for c, (p, b) in enumerate(CHAINS):
                pt = (dns[c].astype(F32) * o_v[_rows(b, i), p * BQ:(p + 1) * BQ].astype(F32)).T
                deltas.append(_bcast_heads(jnp.sum(pt[:64], axis=0, keepdims=True), jnp.sum(pt[64:], axis=0, keepdims=True)))
            dqs = (jnp.zeros((BQ, BQ), F32),) * NC
            dqs = lax.fori_loop(0, i, lambda j, d: steps(qns, qts, dns, dts, cqs, lses, deltas, i, j, d, False), dqs)
            dqs = steps(qns, qts, dns, dts, cqs, lses, deltas, i, i, dqs, True)
            for c, (p, b) in enumerate(CHAINS):
                dq_v[_rows(b, i), p * PAIRW:p * PAIRW + BQ] = (dqs[c].T * SCALE).astype(BF16)
            return 0

        lax.fori_loop(0, NB, qblock, 0)
        for p in range(2):
            dq_v[:, p * PAIRW + BQ:p * PAIRW + 2 * BQ] = dk_s[p].astype(BF16)
            dq_v[:, p * PAIRW + 2 * BQ:p * PAIRW + 3 * BQ] = dv_s[p].astype(BF16)
        _copy_in(dq_v, dqkv_hbm.at[:, pl.ds(g * 2 * PAIRW, 2 * PAIRW)], sem)
        if fox:
            lane = _iota((BQ, NSTAT), 1)

            def fold(n, _):
                t = jnp.zeros((BQ, NSTAT), F32)
                for c, (p, b) in enumerate(CHAINS):
                    s = jnp.sum(dc_s[c, pl.ds(pl.multiple_of(n * HB, HB), HB), :], axis=1, keepdims=True)
                    col = b * NH + 2 * p
                    t = t - jnp.where(lane == col, s[:BQ], 0.0) - jnp.where(lane == col + 1, s[BQ:], 0.0)
                db_ref[pl.ds(pl.multiple_of(n * BQ, BQ), BQ), :] = t
                return 0

            lax.fori_loop(0, NB, fold, 0)
        if comm:
            comm[1](st)

    if fox:
        bias_specs = [VMEM_SPEC, ANY_SPEC]
        db_shape = jax.ShapeDtypeStruct((S, NSTAT), F32)
        more = [SLAB_KEYB, pltpu.VMEM((NC, NB * HB, BQ), F32)]
    else:
        bias_specs = [VMEM_SPEC]
        db_shape = jax.ShapeDtypeStruct((2, NB, HB, BQ), F32)
        more = []
    n_in = 5 + len(bias_specs)
    (dqkv, db), extra = _host_call(
        body, "fox_bwd" if fox else "dil_bwd", [qkv, o, dmixed, lse, *bias, dqkv],
        [ANY_SPEC, ANY_SPEC, ANY_SPEC, VMEM_SPEC] + bias_specs + [ANY_SPEC],
        [jax.ShapeDtypeStruct((T, QKVW), BF16), db_shape], [ANY_SPEC, VMEM_SPEC],
        [SLAB_QKV, SLAB_O32 if fox else SLAB_OUT, SLAB_OUT, SLAB_QKV, ACC_KV, ACC_KV, pltpu.SemaphoreType.DMA, SLAB_T]
        + more, {n_in - 1: 0}, job)
    return dqkv, db, extra


def _delta_t(d):
    return d * BQ + _iota((HB, BQ), 1) - (_iota((HB, BQ), 0) & (BQ - 1))


def _buckets_in(d):
    lo, hi = max(d * BQ - (BQ - 1), 0), d * BQ + BQ - 1
    return [b for b in range(32) if BUCKET_TH[b] <= hi and (b == 31 or BUCKET_TH[b + 1] > lo)]


def _in_bucket(delta, b):
    m = delta >= BUCKET_TH[b]
    return m if b == 31 else m & (delta < BUCKET_TH[b + 1])


def _dil_table(rel_bias):
    def body(rb_ref, o_ref):
        for d in range(NB):
            delta = _delta_t(d)
            pos = delta >= 0
            n = ((pos & (delta <= 128)).astype(jnp.int32)
                 + (pos & (delta <= 512) & ((delta & 3) == 0)).astype(jnp.int32)
                 + (pos & ((delta & 15) == 0)).astype(jnp.int32))
            logn = jnp.where(n == 3, math.log(3.0), jnp.where(n == 2, math.log(2.0), jnp.where(n == 1, 0.0, NEG)))
            head1 = _iota((HB, BQ), 0) >= BQ
            for p in range(2):
                val = jnp.zeros((HB, BQ), F32)
                for b in _buckets_in(d):
                    val = jnp.where(_in_bucket(delta, b), jnp.where(head1, rb_ref[b, 2 * p + 1], rb_ref[b, 2 * p]), val)
                o_ref[p, d] = val + logn

    return pl.pallas_call(
        body, name="dil_table", in_specs=[pl.BlockSpec(memory_space=pltpu.SMEM)], out_specs=VMEM_SPEC,
        out_shape=jax.ShapeDtypeStruct((2, NB, HB, BQ), F32), compiler_params=_cp())(rel_bias)


def _dil_table_bwd(dtbl):
    def body(dt_ref, o_ref):
        p = pl.program_id(0)
        rowi = _iota((32, BQ), 0)
        lanei = _iota((32, BQ), 1)

        @pl.when(p == 0)
        def _():
            o_ref[...] = jnp.zeros_like(o_ref)

        out = jnp.zeros((32, BQ), F32)
        for b in range(32):
            acc = None
            for d in range(NB):
                if b in _buckets_in(d):
                    t = jnp.where(_in_bucket(_delta_t(d), b), dt_ref[d], 0.0)
                    acc = t if acc is None else acc + t
            rs = jnp.sum(acc, axis=1, keepdims=True)
            s0 = jnp.sum(rs[:BQ], axis=0, keepdims=True)
            s1 = jnp.sum(rs[BQ:], axis=0, keepdims=True)
            out = (out + jnp.where((rowi == b) & (lanei == 2 * p), s0, 0.0)
                   + jnp.where((rowi == b) & (lanei == 2 * p + 1), s1, 0.0))
        o_ref[...] += out

    return pl.pallas_call(
        body, name="dil_table_bwd", grid=(2,),
        in_specs=[pl.BlockSpec((None, NB, HB, BQ), lambda p: (p, 0, 0, 0))],
        out_specs=pl.BlockSpec((32, BQ), lambda p: (0, 0)),
        out_shape=jax.ShapeDtypeStruct((32, BQ), F32),
        compiler_params=_cp(("arbitrary",)))(dtbl)


def _fox_prep(gate, fb):
    def body(g_ref, fb_ref, c_ref):
        tri = (_iota((BQ, BQ), 0) >= _iota((BQ, BQ), 1)).astype(BF16)

        def blk(i, carry):
            r0 = pl.multiple_of(i * BQ, BQ)
            lf = _log_sigmoid(g_ref[pl.ds(r0, BQ), :] + fb_ref[...])
            c = _dot(tri, _split3(lf))
            c_ref[pl.ds(r0, BQ), :] = c[:, 0:BQ] + c[:, BQ:2 * BQ] + c[:, 2 * BQ:3 * BQ] + carry
            return carry + jnp.sum(lf, axis=0, keepdims=True)

        lax.fori_loop(0, NB, blk, jnp.zeros((1, BQ), F32))

    blk = pl.BlockSpec((S, GATEW), lambda b: (b, 0))
    return pl.pallas_call(
        body, name="fox_prep", grid=(BL,), in_specs=[blk, pl.BlockSpec((1, GATEW), lambda b: (0, 0))],
        out_specs=blk, out_shape=jax.ShapeDtypeStruct((T, GATEW), F32),
        compiler_params=_cp(("parallel",)))(gate, fb)


def _fox_post(dcum, gate, fb):
    def body(dc_ref, g_ref, fb_ref, dg_ref, dfb_ref):
        b = pl.program_id(0)
        tri = (_iota((BQ, BQ), 0) <= _iota((BQ, BQ), 1)).astype(BF16)

        def blk(ii, carry):
            csum, dfb = carry
            r0 = pl.multiple_of((NB - 1 - ii) * BQ, BQ)
            dc = dc_ref[pl.ds(r0, BQ), :]
            c = _dot(tri, _split3(dc))
            dlf = c[:, 0:BQ] + c[:, BQ:2 * BQ] + c[:, 2 * BQ:3 * BQ] + csum
            dg = dlf * jnp.exp(_log_sigmoid(-(g_ref[pl.ds(r0, BQ), :] + fb_ref[...])))
            dg_ref[pl.ds(r0, BQ), :] = dg
            return csum + jnp.sum(dc, axis=0, keepdims=True), dfb + jnp.sum(dg, axis=0, keepdims=True)

        z = jnp.zeros((1, BQ), F32)
        _, dfb = lax.fori_loop(0, NB, blk, (z, z))

        @pl.when(b == 0)
        def _():
            dfb_ref[...] = dfb

        @pl.when(b > 0)
        def _():
            dfb_ref[...] += dfb

    blk = pl.BlockSpec((S, GATEW), lambda b: (b, 0))
    vec = pl.BlockSpec((1, GATEW), lambda b: (0, 0))
    return pl.pallas_call(
        body, name="fox_post", grid=(BL,), in_specs=[blk, blk, vec], out_specs=[blk, vec],
        out_shape=[jax.ShapeDtypeStruct((T, GATEW), F32), jax.ShapeDtypeStruct((1, GATEW), F32)],
        compiler_params=_cp(("arbitrary",)))(dcum, gate, fb)


def _shift_down(x, n):
    return jnp.where(_iota(x.shape, 0) >= n, pltpu.roll(x, n, 0), 0.0)


def _shift_up(x, n):
    return jnp.where(_iota(x.shape, 0) < S - n, pltpu.roll(x, S - n, 0), 0.0)


def _conv_fwd(conv, cw, mixed):
    W = 256

    def body(c_ref, w_ref, _, o_ref):
        u = c_ref[:, W:2 * W] * c_ref[:, 2 * W:3 * W]
        y = w_ref[0:1, :] * _shift_down(u, 2) + w_ref[1:2, :] * _shift_down(u, 1) + w_ref[2:3, :] * u
        o_ref[...] = (c_ref[:, 0:W] * y).astype(BF16)

    return pl.pallas_call(
        body, name="conv_fwd", grid=(BL,),
        in_specs=[pl.BlockSpec((S, CONVW), lambda b: (b, 0)), pl.BlockSpec((8, W), lambda b: (0, 0)), ANY_SPEC],
        out_specs=pl.BlockSpec((S, W), lambda b: (b, 3)),
        out_shape=jax.ShapeDtypeStruct((T, D), BF16), input_output_aliases={2: 0},
        compiler_params=_cp(("parallel",)))(conv, cw, mixed)


def _conv_bwd(conv, cw, dmixed):
    W = 256

    def body(c_ref, w_ref, do_ref, dc_ref, dw_ref):
        b = pl.program_id(0)
        bg = c_ref[:, 0:W]
        cg = c_ref[:, W:2 * W]
        hv = c_ref[:, 2 * W:3 * W]
        do = do_ref[...].astype(F32)
        u = cg * hv
        u1 = _shift_down(u, 1)
        u2 = _shift_down(u, 2)
        y = w_ref[0:1, :] * u2 + w_ref[1:2, :] * u1 + w_ref[2:3, :] * u
        dy = do * bg
        du = w_ref[2:3, :] * dy + w_ref[1:2, :] * _shift_up(dy, 1) + w_ref[0:1, :] * _shift_up(dy, 2)
        dc_ref[:, 0:W] = (do * y).astype(BF16)
        dc_ref[:, W:2 * W] = (du * hv).astype(BF16)
        dc_ref[:, 2 * W:3 * W] = (du * cg).astype(BF16)
        rowi = _iota((8, W), 0)
        dw = (jnp.where(rowi == 0, jnp.sum(dy * u2, axis=0, keepdims=True), 0.0)
              + jnp.where(rowi == 1, jnp.sum(dy * u1, axis=0, keepdims=True), 0.0)
              + jnp.where(rowi == 2, jnp.sum(dy * u, axis=0, keepdims=True), 0.0))

        @pl.when(b == 0)
        def _():
            dw_ref[...] = dw

        @pl.when(b > 0)
        def _():
            dw_ref[...] += dw

    return pl.pallas_call(
        body, name="conv_bwd", grid=(BL,),
        in_specs=[pl.BlockSpec((S, CONVW), lambda b: (b, 0)), pl.BlockSpec((8, W), lambda b: (0, 0)),
                  pl.BlockSpec((S, W), lambda b: (b, 3))],
        out_specs=[pl.BlockSpec((S, CONVW), lambda b: (b, 0)), pl.BlockSpec((8, W), lambda b: (0, 0))],
        out_shape=[jax.ShapeDtypeStruct((T, CONVW), BF16), jax.ShapeDtypeStruct((8, W), F32)],
        compiler_params=_cp(("arbitrary",)))(conv, cw, dmixed)


def _place():
    x, y, c = lax.axis_index("x"), lax.axis_index("y"), lax.axis_index("c")
    return x, y, c


def _chips_of(x, y):
    return [(1 - x, y), (x, 1 - y), (1 - x, 1 - y)]


def _dev(p):
    return 4 * p[0] + 2 * p[1] + p[2]


def _gather_job_a(shards):
    n = len(shards)

    def peers(x, y, c):
        return [(x, y, 1 - c)] + [(*chip, c) for chip in _chips_of(x, y)]

    def start(ins, outs, sems):
        send, recv, loc = sems
        x, y, c = _place()
        me = (x, y, c)
        cps = []
        for a in range(n):
            cps.append(pltpu.make_async_copy(ins[a], outs[a].at[_dev(me)], loc.at[a]))
            for k, peer in enumerate(peers(x, y, c)):
                cps.append(pltpu.make_async_remote_copy(
                    src_ref=ins[a], dst_ref=outs[a].at[_dev(me)], send_sem=send.at[a, k], recv_sem=recv.at[a, k],
                    device_id=peer, device_id_type=MESH))
        for cp in cps:
            cp.start()
        return cps

    def finish(cps, ins, outs, sems):
        send, recv, loc = sems
        x, y, c = _place()
        for a in range(n):
            for k, peer in enumerate(peers(x, y, c)):
                pltpu.make_async_remote_copy(
                    src_ref=ins[a], dst_ref=outs[a].at[_dev(peer)], send_sem=send.at[a, k], recv_sem=recv.at[a, k],
                    device_id=(x, y, c), device_id_type=MESH).wait_recv()
        for a in range(n):
            cps[5 * a].wait()
            for k in range(4):
                cps[5 * a + 1 + k].wait_send()

    return _Job(shards, [jax.ShapeDtypeStruct((NDEV,) + s.shape, s.dtype) for s in shards], {},
                [pltpu.SemaphoreType.DMA((n, 4)), pltpu.SemaphoreType.DMA((n, 4)), pltpu.SemaphoreType.DMA((n,))],
                start, finish)


def _gather_job_b(gathered):
    n = len(gathered)

    def start(ins, outs, sems):
        send, recv = sems
        x, y, c = _place()
        cps = []
        for a in range(n):
            for j, chip in enumerate(_chips_of(x, y)):
                blk = outs[a].at[_dev((*chip, c))]
                cps.append(pltpu.make_async_remote_copy(
                    src_ref=blk, dst_ref=blk, send_sem=send.at[a, j], recv_sem=recv.at[a, j],
                    device_id=(x, y, 1 - c), device_id_type=MESH))
        for cp in cps:
            cp.start()
        return cps

    def finish(cps, ins, outs, sems):
        send, recv = sems
        x, y, c = _place()
        for a in range(n):
            for j, chip in enumerate(_chips_of(x, y)):
                blk = outs[a].at[_dev((*chip, 1 - c))]
                pltpu.make_async_remote_copy(
                    src_ref=blk, dst_ref=blk, send_sem=send.at[a, j], recv_sem=recv.at[a, j],
                    device_id=(x, y, c), device_id_type=MESH).wait_recv()
        for cp in cps:
            cp.wait_send()

    return _Job(gathered, [jax.ShapeDtypeStruct(g.shape, g.dtype) for g in gathered], {a: a for a in range(n)},
                [pltpu.SemaphoreType.DMA((n, 3)), pltpu.SemaphoreType.DMA((n, 3))], start, finish)


def _sibling_job(grads):
    n = len(grads)

    def start(ins, outs, sems):
        send, recv = sems
        x, y, c = _place()
        cps = [pltpu.make_async_remote_copy(
            src_ref=ins[a].at[:, 1 - c], dst_ref=outs[a], send_sem=send.at[a], recv_sem=recv.at[a],
            device_id=(x, y, 1 - c), device_id_type=MESH) for a in range(n)]
        for cp in cps:
            cp.start()
        return cps

    def finish(cps, ins, outs, sems):
        for cp in cps:
            cp.wait()

    return _Job(grads, [jax.ShapeDtypeStruct(g.shape[:1] + g.shape[2:], F32) for g in grads], {},
                [pltpu.SemaphoreType.DMA((n,)), pltpu.SemaphoreType.DMA((n,))], start, finish)


def _chip_job(psums):
    n = len(psums)

    def copies(ins, outs, sems):
        send, recv, loc = sems
        x, y, c = _place()
        mychip = 2 * x + y
        cps = []
        for a in range(n):
            cps.append(pltpu.make_async_copy(ins[a].at[mychip], outs[a].at[mychip], loc.at[a]))
            for j, chip in enumerate(_chips_of(x, y)):
                cps.append(pltpu.make_async_remote_copy(
                    src_ref=ins[a].at[2 * chip[0] + chip[1]], dst_ref=outs[a].at[mychip],
                    send_sem=send.at[a, j], recv_sem=recv.at[a, j], device_id=(*chip, c), device_id_type=MESH))
        return cps

    def start(ins, outs, sems):
        for cp in copies(ins, outs, sems):
            cp.start()

    def finish(_, ins, outs, sems):
        cps = copies(ins, outs, sems)
        send, recv, loc = sems
        x, y, c = _place()
        mychip = 2 * x + y
        for a in range(n):
            for j, chip in enumerate(_chips_of(x, y)):
                pltpu.make_async_remote_copy(
                    src_ref=ins[a].at[mychip], dst_ref=outs[a].at[2 * chip[0] + chip[1]],
                    send_sem=send.at[a, j], recv_sem=recv.at[a, j], device_id=(x, y, c), device_id_type=MESH).wait_recv()
        for a in range(n):
            cps[4 * a].wait()
            for j in range(3):
                cps[4 * a + 1 + j].wait_send()

    return _Job(psums, [jax.ShapeDtypeStruct(p.shape, BF16) for p in psums], {},
                [pltpu.SemaphoreType.DMA((n, 3)), pltpu.SemaphoreType.DMA((n, 3)), pltpu.SemaphoreType.DMA((n,))],
                start, finish)


def _join_jobs(*jobs):
    jobs = [j for j in jobs if j is not None]
    if len(jobs) <= 1:
        return jobs[0] if jobs else None
    cut = lambda seq, sizes: [seq[sum(sizes[:k]):sum(sizes[:k + 1])] for k in range(len(sizes))]
    n_in = [len(j.ins) for j in jobs]
    n_out = [len(j.out_shapes) for j in jobs]
    n_sem = [len(j.sems) for j in jobs]
    aliases = {}
    for k, j in enumerate(jobs):
        for a, b in j.aliases.items():
            aliases[sum(n_in[:k]) + a] = sum(n_out[:k]) + b

    def start(ins, outs, sems):
        return [j.start(i, o, s) for j, i, o, s in zip(jobs, cut(ins, n_in), cut(outs, n_out), cut(sems, n_sem))]

    def finish(sts, ins, outs, sems):
        for j, st, i, o, s in zip(jobs, sts, cut(ins, n_in), cut(outs, n_out), cut(sems, n_sem)):
            j.finish(st, i, o, s)

    return _Job([t for j in jobs for t in j.ins], [t for j in jobs for t in j.out_shapes], aliases,
                [t for j in jobs for t in j.sems], start, finish)


def _run_job(job, name):
    def body(ins, outs, scr, comm):
        comm[1](comm[0]())

    return _host_call(body, name, [], [], [], [], [], {}, job)[1]


def _allreduce_small(v):
    def body(v_ref, o_ref, slots, send_sems, recv_sems):
        x, y, c = _place()
        me = 4 * x + 2 * y + c
        slots[me] = v_ref[...]

        def copy(k):
            peer = (x ^ ((k >> 2) & 1), y ^ ((k >> 1) & 1), c ^ (k & 1))
            return pltpu.make_async_remote_copy(
                src_ref=v_ref, dst_ref=slots.at[me], send_sem=send_sems.at[k - 1], recv_sem=recv_sems.at[k - 1],
                device_id=peer, device_id_type=MESH)

        def arrival(k):
            return pltpu.make_async_remote_copy(
                src_ref=v_ref, dst_ref=slots.at[me ^ k], send_sem=send_sems.at[k - 1], recv_sem=recv_sems.at[k - 1],
                device_id=(x, y, c), device_id_type=MESH)

        sends = [copy(k) for k in range(1, NDEV)]
        for cp in sends:
            cp.start()
        for k in range(1, NDEV):
            arrival(k).wait_recv()
        for cp in sends:
            cp.wait_send()
        acc = slots[0]
        for d in range(1, NDEV):
            acc = acc + slots[d]
        o_ref[...] = acc

    return pl.pallas_call(
        body, name="allreduce_small", in_specs=[VMEM_SPEC], out_specs=VMEM_SPEC,
        out_shape=jax.ShapeDtypeStruct(v.shape, F32),
        scratch_shapes=[pltpu.VMEM((NDEV,) + v.shape, F32), pltpu.SemaphoreType.DMA((NDEV - 1,)),
                        pltpu.SemaphoreType.DMA((NDEV - 1,))],
        )(v)


def _pair_sums(views, gots, core):
    n = len(views)

    def body(c_ref, *refs):
        for a in range(n):
            refs[2 * n + a][...] = (refs[a][...] + refs[n + a][...]).astype(BF16)

    def vspec(v):
        return pl.BlockSpec((None, None) + v.shape[2:], lambda k, c: (k, c[0], 0, 0))

    def gspec(g):
        return pl.BlockSpec((None,) + g.shape[1:], lambda k, c: (k, 0, 0))

    return pl.pallas_call(
        body, name="pair_sums",
        grid_spec=pltpu.PrefetchScalarGridSpec(
            num_scalar_prefetch=1, grid=(4,),
            in_specs=[vspec(v) for v in views] + [gspec(g) for g in gots],
            out_specs=[gspec(g) for g in gots]),
        out_shape=[jax.ShapeDtypeStruct(g.shape, BF16) for g in gots],
        compiler_params=_cp(("parallel",)))(core, *views, *gots)


def _chip_sums(parts):
    n = len(parts)

    def body(*refs):
        for a in range(n):
            acc = refs[a][0].astype(F32)
            for k in range(1, 4):
                acc = acc + refs[a][k].astype(F32)
            refs[n + a][...] = acc

    return pl.pallas_call(
        body, name="chip_sums", in_specs=[VMEM_SPEC] * n, out_specs=[VMEM_SPEC] * n,
        out_shape=[jax.ShapeDtypeStruct(p.shape[1:], F32) for p in parts], compiler_params=_cp())(*parts)


def _permute_in(w):
    lead = w.shape[:-1]
    return w.reshape(lead + (3, 3, 2, BQ)).swapaxes(-2, -3).reshape(lead + (QKVW,))


def _unpermute_in(w):
    lead = w.shape[:-1]
    return w.reshape(lead + (3, 2, 3, BQ)).swapaxes(-2, -3).reshape(lead + (QKVW,))


def _row(v):
    v = v.reshape(-1)
    return jnp.pad(v, (0, D - v.shape[0])).reshape(1, D)


def kernel(x, w_in, f_bias, conv_w, w_out, rel_bias, ln1_g, ln1_b, w_gate, w_up, w_down, ln2_g, ln2_b, loss_target, m_w_in, m_f_bias, m_conv_w, m_w_out, m_rel_bias, m_ln1_g, m_ln1_b, m_w_gate, m_w_up, m_w_down, m_ln2_g, m_ln2_b, v_w_in, v_f_bias, v_conv_w, v_w_out, v_rel_bias, v_ln1_g, v_ln1_b, v_w_gate, v_w_up, v_w_down, v_ln2_g, v_ln2_b):
    xi, yi, ci = _place()
    me = 4 * xi + 2 * yi + ci
    core = jnp.reshape(ci, (1,)).astype(jnp.int32)

    win_s = jnp.concatenate([_permute_in(w_in[..., :QKVW]), w_in[..., QKVW:]], axis=-1)
    win_s = jnp.pad(win_s, ((0, 0), (0, 0), (0, NPAD - NPROJ))).astype(BF16)
    per_layer = [win_s, w_out.astype(BF16), jnp.swapaxes(w_gate, 1, 2).astype(BF16),
                 jnp.swapaxes(w_up, 1, 2).astype(BF16), w_down.astype(BF16)]
    sh = [[s[l] for s in per_layer] for l in range(2)]

    def whole(g):
        return g.reshape(NDEV * g.shape[1], g.shape[2])

    first = _run_job(_gather_job_b(_run_job(_gather_job_a(sh[0][:1]), "gather_a")), "gather_b")
    W = [{"win": whole(first[0])}, {}]

    cw_rows = lax.dynamic_update_slice(jnp.zeros((2, 3, 256), F32), conv_w, (0, 0, me * 32))
    small = jnp.concatenate([_row(cw_rows[0]), _row(cw_rows[1]), jnp.zeros((SMALL_ROWS - 2, D), F32)], axis=0)
    small = _allreduce_small(small)
    cw_full = small[0:2, :CONVW].reshape(2, 3, 256)
    cw8 = jnp.pad(cw_full, ((0, 0), (0, 5), (0, 0)))
    fb = jnp.pad(f_bias, ((0, 0), (0, GATEW - NH))).reshape(2, 1, GATEW)
    tbl = _dil_table(rel_bias)

    def wcol(K, tn, off):
        return pl.BlockSpec((K, tn), lambda i, j: (0, off + j))

    def wrow(tn, K, blk=0):
        return pl.BlockSpec((tn, K), lambda i, j: (j, blk))

    def arow(tm, K, blk=0):
        return pl.BlockSpec((tm, K), lambda i, j: (i, blk))

    h = x.reshape(T, D)
    hb = h.astype(BF16)
    saved = []
    for l in range(2):
        Win = W[l]["win"]
        qkv = _mm([(hb, arow(1024, D), Win, wcol(D, 768, 0))], nt=False, M=T, N=QKVW, tm=1024, tn=768,
                  out_dtype=BF16, name="proj_qkv")
        conv = _mm([(hb, arow(512, D), Win, wcol(D, 768, 3))], nt=False, M=T, N=CONVW, tm=512, tn=768,
                   out_dtype=F32, name="proj_conv")
        gate = _mm([(hb, arow(512, D), Win, wcol(D, 128, 24))], nt=False, M=T, N=GATEW, tm=512, tn=128,
                   out_dtype=F32, name="proj_gate")
        cum = _fox_prep(gate, fb[l])
        cq = cum[:, :NH].reshape(BL, S, NH).transpose(0, 2, 1).reshape(NSTAT, S)
        ckb = jnp.broadcast_to(cq[:, :, None], (NSTAT, S, BQ))
        if l == 0:
            mixed, rtot, a0 = _sb_fwd(qkv, job=_gather_job_a(sh[0][1:]))
            mixed, lse_d, ex = _flash_fwd(qkv, mixed, 1, False, (tbl,),
                                          job=_join_jobs(_gather_job_b(list(a0)), _gather_job_a(sh[1][:2])))
            W[0].update(zip(("wout", "wgT", "wuT", "wd"), [whole(t) for t in ex[:4]]))
            mixed, lse_f, o_fox, ex = _flash_fwd(qkv, mixed, 2, True, (cq, ckb),
                                                 job=_join_jobs(_gather_job_b(list(ex[4:])), _gather_job_a(sh[1][2:])))
            W[1].update(zip(("win", "wout"), [whole(t) for t in ex[:2]]))
            a2 = list(ex[2:])
        else:
            mixed, rtot, ex = _sb_fwd(qkv, job=_gather_job_b(a2))
            W[1].update(zip(("wgT", "wuT", "wd"), [whole(t) for t in ex]))
            mixed, lse_d, _ = _flash_fwd(qkv, mixed, 1, False, (tbl,))
            mixed, lse_f, o_fox, _ = _flash_fwd(qkv, mixed, 2, True, (cq, ckb))
        Wout, WgT, WuT, Wd = W[l]["wout"], W[l]["wgT"], W[l]["wuT"], W[l]["wd"]
        mixed = _conv_fwd(conv, cw8[l], mixed)
        x1, xh1, r1, x1b = _mm_ln(mixed, Wout, h, ln1_g[l:l + 1], ln1_b[l:l + 1], "out_proj_ln")
        fs, ft, a, x2, xh2, r2, x2b = _ffn_fwd(x1b, x1, WgT, WuT, Wd, ln2_g[l:l + 1], ln2_b[l:l + 1])
        saved.append(dict(h=hb, qkv=qkv, conv=conv, gate=gate, cq=cq, ckb=ckb, mixed=mixed, rtot=rtot, lse_d=lse_d,
                          lse_f=lse_f, o_fox=o_fox, x1=x1b, xh1=xh1, r1=r1, fs=fs, ft=ft, a=a, xh2=xh2, r2=r2))
        h, hb = x2, x2b

    sq, dy = _loss_grad(h, loss_target.reshape(T, D))

    def view(gr):
        return gr.reshape(4, 2, gr.shape[0] // NDEV, gr.shape[1])

    G = [None, None]
    small_g = {}
    shard_g = {}
    for l in (1, 0):
        sv = saved[l]
        Win, Wout, WgT, WuT, Wd = W[l]["win"], W[l]["wout"], W[l]["wgT"], W[l]["wuT"], W[l]["wd"]
        ds2, dg2, db2, ds2b = _ln_bwd(dy, sv["xh2"], sv["r2"], ln2_g[l:l + 1])
        dgt, dut, dx1 = _ffn_bwd(ds2b, ds2, sv["fs"], sv["ft"], Wd, WgT, WuT)
        G_d = _mm_tn(sv["a"], ds2b, None, C=D, Ka=DFF, N=D, tm=256, tn=1024, tk=T, ooff=0, name="grad_w_down")
        G_g = _mm_tn(dgt, sv["x1"], None, C=D, Ka=DFF, N=D, tm=256, tn=1024, tk=T, ooff=0, name="grad_w_gate")
        G_u = _mm_tn(dut, sv["x1"], None, C=D, Ka=DFF, N=D, tm=256, tn=1024, tk=T, ooff=0, name="grad_w_up")
        ds1, dg1, db1, ds1b = _ln_bwd(dx1, sv["xh1"], sv["r1"], ln1_g[l:l + 1])
        G_out = _mm_tn(sv["mixed"], ds1b, None, C=D, Ka=D, N=D, tm=256, tn=1024, tk=T, ooff=0, name="grad_w_out")
        dmixed = _mm([(ds1b, arow(512, D), Wout, wrow(512, D))], nt=True, M=T, N=D, tm=512, tn=512,
                     out_dtype=BF16, name="out_proj_dx")
        early = [view(t) for t in (G_g, G_u, G_d, G_out)] + ([view(G[1]["in"])] if l == 0 else [])
        dqkv, gots = _sb_bwd(sv["qkv"], dmixed, sv["rtot"], job=_sibling_job(early))
        ps = _pair_sums(early, list(gots), core)
        dqkv, dtbl, pa = _flash_bwd(sv["qkv"], sv["mixed"], dmixed, sv["lse_d"], dqkv, 1, False, (tbl,),
                                    job=_chip_job(ps[:2]))
        dqkv, dck, pb = _flash_bwd(sv["qkv"], sv["o_fox"], dmixed, sv["lse_f"], dqkv, 2, True,
                                   (sv["cq"], sv["ckb"]), job=_chip_job(ps[2:]))
        sums = _chip_sums(list(pa) + list(pb))
        shard_g[l] = dict(zip(("g", "u", "d", "out"), sums[:4]))
        if l == 0:
            shard_g[1]["in"] = sums[4]
        dconv, dcw = _conv_bwd(sv["conv"], cw8[l], dmixed)
        dcum = jnp.pad(dck.reshape(S, BL, NH).transpose(1, 0, 2).reshape(T, NH), ((0, 0), (0, GATEW - NH)))
        dgate, dfb = _fox_post(dcum, sv["gate"], fb[l])
        drb = _dil_table_bwd(dtbl)
        G_in = _mm_tn(sv["h"], dqkv, None, C=NPAD, Ka=D, N=QKVW, tm=512, tn=768, tk=T, ooff=0, name="grad_w_in_qkv")
        G_in = _mm_tn(sv["h"], dconv, G_in, C=NPAD, Ka=D, N=CONVW, tm=256, tn=768, tk=T, ooff=3,
                      name="grad_w_in_conv")
        G_in = _mm_tn(sv["h"], dgate, G_in, C=NPAD, Ka=D, N=GATEW, tm=1024, tn=128, tk=1024, ooff=24,
                      name="grad_w_in_gate")
        G[l] = {"in": G_in, "out": G_out, "g": G_g, "u": G_u, "d": G_d}
        tail = None
        if l == 0:
            late = [view(G_in)]
            tail = _chip_job(_pair_sums(late, list(_run_job(_sibling_job(late), "sibling_exchange")), core))
        dy = _mm([(dqkv, arow(1024, QKVW), Win, wrow(512, QKVW, 0)),
                  (dconv, arow(1024, CONVW), Win, wrow(512, CONVW, 3)),
                  (dgate, arow(1024, GATEW), Win, wrow(512, GATEW, 24))],
                 nt=True, M=T, N=D, tm=1024, tn=512, out_dtype=F32, name="proj_dx", res=ds1, res_scale=ALPHA, job=tail)
        if l == 0:
            dy, parts = dy
            shard_g[0]["in"] = _chip_sums(list(parts))[0]
        small_g[l] = dict(ln1_g=dg1, ln1_b=db1, ln2_g=dg2, ln2_b=db2, cw=dcw[0:3].reshape(1, CONVW),
                          fb=dfb[:, :NH], rb=drb[:, :NH])
    grad_x = dy.reshape(BL, S, D)

    rows = []
    for name in ("ln1_g", "ln1_b", "ln2_g", "ln2_b"):
        rows += [small_g[0][name], small_g[1][name]]
    rows += [_row(small_g[0]["cw"]), _row(small_g[1]["cw"]),
             _row(jnp.concatenate([small_g[0]["fb"], small_g[1]["fb"]], axis=0)),
             _row(small_g[0]["rb"] + small_g[1]["rb"]), _row(sq)]
    rows.append(jnp.zeros((SMALL_ROWS - len(rows), D), F32))
    sg = _allreduce_small(jnp.concatenate(rows, axis=0))
    loss = sg[12, 0] * (0.5 / D)
    g_ln1_g, g_ln1_b, g_ln2_g, g_ln2_b = sg[0:2], sg[2:4], sg[4:6], sg[6:8]
    g_conv_full = sg[8:10, :CONVW].reshape(2, 3, 256)
    g_conv = lax.dynamic_slice(g_conv_full, (0, 0, me * 32), (2, 3, 32))
    g_fb = sg[10, :2 * NH].reshape(2, NH)
    g_rb = sg[11, :32 * NH].reshape(32, NH)

    def both(name):
        return jnp.stack([shard_g[0][name], shard_g[1][name]])

    g_in = both("in")
    g_w_in = jnp.concatenate([_unpermute_in(g_in[..., :QKVW]), g_in[..., QKVW:NPROJ]], axis=-1)
    g_w_out = both("out")
    g_w_gate = jnp.swapaxes(both("g"), 1, 2)
    g_w_up = jnp.swapaxes(both("u"), 1, 2)
    g_w_down = both("d")

    up_in = _adamw(w_in, g_w_in, m_w_in, v_w_in, 64)
    up_out = _adamw(w_out, g_w_out, m_w_out, v_w_out, 128)
    up_gate = _adamw(w_gate, g_w_gate, m_w_gate, v_w_gate, 256)
    up_up = _adamw(w_up, g_w_up, m_w_up, v_w_up, 256)
    up_down = _adamw(w_down, g_w_down, m_w_down, v_w_down, 352)

    def pack(fbv, cwv, rbv, l1g, l1b, l2g, l2b):
        r = [l1g, l1b, l2g, l2b, _row(cwv), _row(fbv), _row(rbv)]
        r.append(jnp.zeros((SMALL_ROWS - 11, D), F32))
        return jnp.concatenate(r, axis=0)

    pw = pack(f_bias, conv_w, rel_bias, ln1_g, ln1_b, ln2_g, ln2_b)
    pg = pack(g_fb, g_conv, g_rb, g_ln1_g, g_ln1_b, g_ln2_g, g_ln2_b)
    pm = pack(m_f_bias, m_conv_w, m_rel_bias, m_ln1_g, m_ln1_b, m_ln2_g, m_ln2_b)
    pv = pack(v_f_bias, v_conv_w, v_rel_bias, v_ln1_g, v_ln1_b, v_ln2_g, v_ln2_b)
    ups = [u[0] for u in _adamw(pw[None], pg[None], pm[None], pv[None], SMALL_ROWS)]

    def unpack(p):
        return dict(ln1_g=p[0:2], ln1_b=p[2:4], ln2_g=p[4:6], ln2_b=p[6:8],
                    conv_w=p[8, :192].reshape(2, 3, 32), f_bias=p[9, :2 * NH].reshape(2, NH),
                    rel_bias=p[10, :32 * NH].reshape(32, NH))

    sm = [unpack(p) for p in ups]

    def group(k):
        return (up_in[k], sm[k]["f_bias"], sm[k]["conv_w"], up_out[k], sm[k]["rel_bias"], sm[k]["ln1_g"],
                sm[k]["ln1_b"], up_gate[k], up_up[k], up_down[k], sm[k]["ln2_g"], sm[k]["ln2_b"])

    grads = (g_w_in, g_fb, g_conv, g_w_out, g_rb, g_ln1_g, g_ln1_b, g_w_gate, g_w_up, g_w_down, g_ln2_g, g_ln2_b)
    return (loss, grad_x) + grads + group(0) + group(1) + group(2)
```

```python
import math

import numpy as np
import jax
import jax.numpy as jnp
from jax import lax
from jax.experimental import pallas as pl
from jax.experimental.pallas import tpu as pltpu

F32 = jnp.float32
BF16 = jnp.bfloat16
MESH = pl.DeviceIdType.MESH

D = 1024
S = 2048
BL = 2
T = BL * S
NH = 4
DFF = 2816
NPROJ = 3076
NPAD = 3200
QKVW = 2304
CONVW = 768
GATEW = 128
PAIRW = 384
BQ = 128
HB = 2 * BQ
NB = S // BQ
NDEV = 8
NSTAT = BL * NH
ALPHA = 4.0 ** 0.25
SCALE = 0.125
NEG = -1e30
LN_EPS = 1e-5
ADAM_LR, ADAM_B1, ADAM_B2, ADAM_EPS, ADAM_WD, ADAM_STEP = 0.001, 0.9, 0.999, 1e-08, 0.01, 10
VMEM_LIMIT = 56 * 1024 * 1024
SMALL_ROWS = 16


def _bucket_thresholds():
    d = np.arange(0, S)
    nf = np.maximum(d, 1).astype(np.float32)
    large = 16 + (np.log(nf / np.float32(16)) / np.float32(math.log(128)) * np.float32(16)).astype(np.int32)
    b = np.where(d < 16, d, np.minimum(large, 31))
    return [int(np.argmax(b >= k)) for k in range(32)]


BUCKET_TH = _bucket_thresholds()


def _cp(sem=None, vmem=VMEM_LIMIT):
    return pltpu.CompilerParams(dimension_semantics=sem, vmem_limit_bytes=vmem)


def _dot(a, b):
    return lax.dot_general(a, b, (((1,), (0,)), ((), ())), preferred_element_type=F32)


def _dot_nt(a, b):
    return lax.dot_general(a, b, (((1,), (1,)), ((), ())), preferred_element_type=F32)


def _dot_tn(a, b):
    return lax.dot_general(a, b, (((0,), (0,)), ((), ())), preferred_element_type=F32)


def _split2(x):
    hi = x.astype(BF16)
    mid = (x - hi.astype(F32)).astype(BF16)
    return jnp.concatenate([hi, mid], axis=1)


def _split3(x):
    hi = x.astype(BF16)
    r = x - hi.astype(F32)
    mid = r.astype(BF16)
    lo = (r - mid.astype(F32)).astype(BF16)
    return jnp.concatenate([hi, mid, lo], axis=1)


def _log_sigmoid(u):
    return jnp.minimum(u, 0.0) - jnp.log1p(jnp.exp(-jnp.abs(u)))


def _log_sigmoid_tile(u):
    return jnp.minimum(u, 0.0) - jnp.log(1.0 + jnp.exp(jnp.minimum(u, -u)))


def _iota(shape, dim):
    return lax.broadcasted_iota(jnp.int32, shape, dim)


ANY_SPEC = pl.BlockSpec(memory_space=pl.ANY)
VMEM_SPEC = pl.BlockSpec(memory_space=pltpu.VMEM)


def _mm(pairs, *, nt, M, N, tm, tn, out_dtype, name, res=None, res_scale=1.0, job=None):
    n = len(pairs)
    n_in = 2 * n + (res is not None)
    jins = job.ins if job else []
    jouts = job.out_shapes if job else []
    gi, gj = M // tm, N // tn

    def body(*refs):
        o_ref = refs[n_in + len(jins)]
        if job:
            jrefs = (refs[n_in:n_in + len(jins)], refs[n_in + len(jins) + 1:n_in + len(jins) + 1 + len(jouts)],
                     refs[n_in + len(jins) + 1 + len(jouts):])

            @pl.when((pl.program_id(0) == 0) & (pl.program_id(1) == 0))
            def _():
                job.start(*jrefs)

        acc = None
        for p in range(n):
            a = refs[2 * p][...].astype(BF16)
            b = refs[2 * p + 1][...]
            d = _dot_nt(a, b) if nt else _dot(a, b)
            acc = d if acc is None else acc + d
        if res is not None:
            acc = acc + res_scale * refs[2 * n][...]
        o_ref[...] = acc.astype(out_dtype)
        if job:
            @pl.when((pl.program_id(0) == gi - 1) & (pl.program_id(1) == gj - 1))
            def _():
                job.finish(None, *jrefs)

    ops, specs = [], []
    for a, asp, b, bsp in pairs:
        ops += [a, b]
        specs += [asp, bsp]
    if res is not None:
        ops.append(res)
        specs.append(pl.BlockSpec((tm, tn), lambda i, j: (i, j)))
    out = pl.pallas_call(
        body, name=name, grid=(gi, gj), in_specs=specs + [ANY_SPEC] * len(jins),
        out_specs=[pl.BlockSpec((tm, tn), lambda i, j: (i, j))] + [ANY_SPEC] * len(jouts),
        out_shape=[jax.ShapeDtypeStruct((M, N), out_dtype)] + list(jouts),
        scratch_shapes=list(job.sems) if job else [],
        input_output_aliases={n_in + a: 1 + b for a, b in job.aliases.items()} if job else {},
        compiler_params=_cp(("arbitrary", "arbitrary") if job else ("parallel", "parallel")))(*ops, *jins)
    return (out[0], out[1:]) if job else out[0]


def _mm_tn(a, b, gbuf, *, C, Ka, N, tm, tn, tk, ooff, name):
    def body(*refs):
        a_ref, b_ref, o_ref = refs[0], refs[1], refs[-1]
        k = pl.program_id(2)
        d = _dot_tn(a_ref[...].astype(BF16), b_ref[...].astype(BF16))

        @pl.when(k == 0)
        def _():
            o_ref[...] = d

        @pl.when(k > 0)
        def _():
            o_ref[...] += d

    ops = [a, b] + ([] if gbuf is None else [gbuf])
    return pl.pallas_call(
        body, name=name, grid=(Ka // tm, N // tn, T // tk),
        in_specs=[pl.BlockSpec((tk, tm), lambda i, j, k: (k, i)),
                  pl.BlockSpec((tk, tn), lambda i, j, k: (k, j))] + ([] if gbuf is None else [ANY_SPEC]),
        out_specs=pl.BlockSpec((tm, tn), lambda i, j, k: (i, ooff + j)),
        out_shape=jax.ShapeDtypeStruct((Ka, C), F32),
        input_output_aliases={} if gbuf is None else {2: 0},
        compiler_params=_cp(("parallel", "parallel", "arbitrary")))(*ops)


def _ffn_up(x1, wgt, wut):
    tm, tn = 1024, 256

    def body(x_ref, wg_ref, wu_ref, g_ref, u_ref, a_ref):
        ch = 256
        for r in range(0, tm, ch):
            xb = x_ref[r:r + ch, :]
            g = _dot_nt(xb, wg_ref[...])
            u = _dot_nt(xb, wu_ref[...])
            g_ref[r:r + ch, :] = g.astype(BF16)
            u_ref[r:r + ch, :] = u.astype(BF16)
            a_ref[r:r + ch, :] = (g * jax.nn.sigmoid(g) * u).astype(BF16)

    wspec = pl.BlockSpec((tn, D), lambda i, j: (j, 0))
    ospec = pl.BlockSpec((tm, tn), lambda i, j: (i, j))
    return pl.pallas_call(
        body, name="ffn_up", grid=(T // tm, DFF // tn),
        in_specs=[pl.BlockSpec((tm, D), lambda i, j: (i, 0)), wspec, wspec],
        out_specs=[ospec, ospec, ospec],
        out_shape=[jax.ShapeDtypeStruct((T, DFF), BF16)] * 3,
        compiler_params=_cp(("parallel", "parallel")))(x1, wgt, wut)


def _ffn_da(dffn, wd, s, t):
    tm, tn = 1024, 256

    def body(d_ref, wd_ref, s_ref, t_ref, dg_ref, du_ref):
        ch = 256
        for r in range(0, tm, ch):
            da = _dot_nt(d_ref[r:r + ch, :], wd_ref[...])
            gv = s_ref[r:r + ch, :].astype(F32)
            sg = jax.nn.sigmoid(gv)
            dg_ref[r:r + ch, :] = (da * t_ref[r:r + ch, :].astype(F32) * (sg * (1.0 + gv * (1.0 - sg)))).astype(BF16)
            du_ref[r:r + ch, :] = (da * (gv * sg)).astype(BF16)

    ospec = pl.BlockSpec((tm, tn), lambda i, j: (i, j))
    return pl.pallas_call(
        body, name="ffn_da", grid=(T // tm, DFF // tn),
        in_specs=[pl.BlockSpec((tm, D), lambda i, j: (i, 0)),
                  pl.BlockSpec((tn, D), lambda i, j: (j, 0)), ospec, ospec],
        out_specs=[ospec, ospec],
        out_shape=[jax.ShapeDtypeStruct((T, DFF), BF16), jax.ShapeDtypeStruct((T, DFF), BF16)],
        compiler_params=_cp(("parallel", "parallel")))(dffn, wd, s, t)


def _ffn_fwd(xb, x, wgt, wut, wd, gam, bet):
    tm, ch = 512, 256

    def body(xb_ref, x_ref, g_ref, b_ref, wg_hbm, wu_hbm, wd_hbm,
             go_ref, uo_ref, ao_ref, y_ref, xh_ref, r_ref, yb_ref, wg_v, wu_v, wd_v, sem):
        @pl.when(pl.program_id(0) == 0)
        def _():
            _copy_in(wg_hbm, wg_v, sem)
            _copy_in(wu_hbm, wu_v, sem)
            _copy_in(wd_hbm, wd_v, sem)

        xv = xb_ref[...]
        for c in range(0, DFF, ch):
            gv = _dot_nt(xv, wg_v[c:c + ch, :])
            uv = _dot_nt(xv, wu_v[c:c + ch, :])
            go_ref[:, c:c + ch] = gv.astype(BF16)
            uo_ref[:, c:c + ch] = uv.astype(BF16)
            ao_ref[:, c:c + ch] = (gv * jax.nn.sigmoid(gv) * uv).astype(BF16)
        s = ALPHA * x_ref[...] + _dot(ao_ref[...], wd_v[...])
        mu = jnp.mean(s, axis=-1, keepdims=True)
        xc = s - mu
        var = jnp.mean(xc * xc, axis=-1, keepdims=True)
        r = lax.rsqrt(var + LN_EPS)
        xh = xc * r
        xh_ref[...] = xh.astype(BF16)
        r_ref[...] = r
        y = xh * g_ref[...] + b_ref[...]
        y_ref[...] = y
        yb_ref[...] = y.astype(BF16)

    row = pl.BlockSpec((tm, D), lambda i: (i, 0))
    wide = pl.BlockSpec((tm, DFF), lambda i: (i, 0))
    vec = pl.BlockSpec((1, D), lambda i: (0, 0))
    wsl = pltpu.VMEM((DFF, D), BF16)
    hid = jax.ShapeDtypeStruct((T, DFF), BF16)
    return pl.pallas_call(
        body, name="ffn_fwd", grid=(T // tm,),
        in_specs=[row, row, vec, vec, ANY_SPEC, ANY_SPEC, ANY_SPEC],
        out_specs=[wide, wide, wide, row, row, pl.BlockSpec((tm, 1), lambda i: (i, 0)), row],
        out_shape=[hid, hid, hid, jax.ShapeDtypeStruct((T, D), F32), jax.ShapeDtypeStruct((T, D), BF16),
                   jax.ShapeDtypeStruct((T, 1), F32), jax.ShapeDtypeStruct((T, D), BF16)],
        scratch_shapes=[wsl, wsl, wsl, pltpu.SemaphoreType.DMA],
        compiler_params=_cp(("arbitrary",)))(xb, x, gam, bet, wgt, wut, wd)


def _ffn_bwd(dy, xh, r, gam, g, u, wd, wgt, wut, target=None):
    tm, ch = 256, 256

    def body(*refs):
        if target is None:
            (dy_ref, xh_ref, r_ref, gam_ref, g_ref, u_ref, wd_hbm, wg_hbm, wu_hbm,
             dg_ref, du_ref, dsb_ref, dx_ref, dgam_ref, dbet_ref, wd_v, wg_v, wu_v, sem) = refs
        else:
            (dy_ref, t_ref, xh_ref, r_ref, gam_ref, g_ref, u_ref, wd_hbm, wg_hbm, wu_hbm,
             dg_ref, du_ref, dsb_ref, dx_ref, dgam_ref, dbet_ref, sq_ref, wd_v, wg_v, wu_v, sem) = refs
        @pl.when(pl.program_id(0) == 0)
        def _():
            _copy_in(wd_hbm, wd_v, sem)
            _copy_in(wg_hbm, wg_v, sem)
            _copy_in(wu_hbm, wu_v, sem)

        if target is None:
            dyv = dy_ref[...]
        else:
            e = dy_ref[...] - t_ref[...]
            dyv = e * (1.0 / D)
            p = jnp.sum(jnp.sum(e * e, axis=1, keepdims=True), axis=0, keepdims=True)

            @pl.when(pl.program_id(0) == 0)
            def _():
                sq_ref[...] = p

            @pl.when(pl.program_id(0) > 0)
            def _():
                sq_ref[...] += p

        xhv = xh_ref[...].astype(F32)
        dxh = dyv * gam_ref[...]
        m1 = jnp.mean(dxh, axis=-1, keepdims=True)
        m2 = jnp.mean(dxh * xhv, axis=-1, keepdims=True)
        ds = r_ref[...] * (dxh - m1 - xhv * m2)
        pg = jnp.sum(dyv * xhv, axis=0, keepdims=True)
        pb = jnp.sum(dyv, axis=0, keepdims=True)

        @pl.when(pl.program_id(0) == 0)
        def _():
            dgam_ref[...] = pg
            dbet_ref[...] = pb

        @pl.when(pl.program_id(0) > 0)
        def _():
            dgam_ref[...] += pg
            dbet_ref[...] += pb

        db = ds.astype(BF16)
        dsb_ref[...] = db
        for c in range(0, DFF, ch):
            da = _dot_nt(db, wd_v[c:c + ch, :])
            gv = g_ref[:, c:c + ch].astype(F32)
            sg = jax.nn.sigmoid(gv)
            dg_ref[:, c:c + ch] = (da * u_ref[:, c:c + ch].astype(F32) * (sg * (1.0 + gv * (1.0 - sg)))).astype(BF16)
            du_ref[:, c:c + ch] = (da * (gv * sg)).astype(BF16)
        dx_ref[...] = ALPHA * ds + _dot(dg_ref[...], wg_v[...]) + _dot(du_ref[...], wu_v[...])

    row = pl.BlockSpec((tm, D), lambda i: (i, 0))
    wide = pl.BlockSpec((tm, DFF), lambda i: (i, 0))
    vec = pl.BlockSpec((1, D), lambda i: (0, 0))
    wsl = pltpu.VMEM((DFF, D), BF16)
    last = target is not None
    return pl.pallas_call(
        body, name="ffn_bwd_loss" if last else "ffn_bwd", grid=(T // tm,),
        in_specs=[row] + ([row] if last else [])
        + [row, pl.BlockSpec((tm, 1), lambda i: (i, 0)), vec, wide, wide, ANY_SPEC, ANY_SPEC, ANY_SPEC],
        out_specs=[wide, wide, row, row, vec, vec] + ([pl.BlockSpec((1, 1), lambda i: (0, 0))] if last else []),
        out_shape=[jax.ShapeDtypeStruct((T, DFF), BF16), jax.ShapeDtypeStruct((T, DFF), BF16),
                   jax.ShapeDtypeStruct((T, D), BF16), jax.ShapeDtypeStruct((T, D), F32),
                   jax.ShapeDtypeStruct((1, D), F32), jax.ShapeDtypeStruct((1, D), F32)]
        + ([jax.ShapeDtypeStruct((1, 1), F32)] if last else []),
        scratch_shapes=[wsl, wsl, wsl, pltpu.SemaphoreType.DMA],
        compiler_params=_cp(("arbitrary",)))(dy, *([target] if last else []), xh, r, gam, g, u, wd, wgt, wut)


def _mm_ln(a, w, x, gam, bet, name):
    tm = 256
    K = a.shape[1]

    def body(a_ref, w_ref, x_ref, g_ref, b_ref, y_ref, xh_ref, r_ref, yb_ref):
        s = ALPHA * x_ref[...] + _dot(a_ref[...], w_ref[...])
        mu = jnp.mean(s, axis=-1, keepdims=True)
        xc = s - mu
        var = jnp.mean(xc * xc, axis=-1, keepdims=True)
        r = lax.rsqrt(var + LN_EPS)
        xh = xc * r
        xh_ref[...] = xh.astype(BF16)
        r_ref[...] = r
        y = xh * g_ref[...] + b_ref[...]
        y_ref[...] = y
        yb_ref[...] = y.astype(BF16)

    row = pl.BlockSpec((tm, D), lambda i: (i, 0))
    vec = pl.BlockSpec((1, D), lambda i: (0, 0))
    return pl.pallas_call(
        body, name=name, grid=(T // tm,),
        in_specs=[pl.BlockSpec((tm, K), lambda i: (i, 0)), pl.BlockSpec((K, D), lambda i: (0, 0)), row, vec, vec],
        out_specs=[row, row, pl.BlockSpec((tm, 1), lambda i: (i, 0)), row],
        out_shape=[jax.ShapeDtypeStruct((T, D), F32), jax.ShapeDtypeStruct((T, D), BF16),
                   jax.ShapeDtypeStruct((T, 1), F32), jax.ShapeDtypeStruct((T, D), BF16)],
        compiler_params=_cp(("parallel",)))(a, w, x, gam, bet)


def _ln_bwd(dy, xh, r, gam):
    tm = 256

    def body(dy_ref, xh_ref, r_ref, g_ref, ds_ref, dg_ref, db_ref, dsb_ref):
        i = pl.program_id(0)
        dyv = dy_ref[...]
        xhv = xh_ref[...].astype(F32)
        dxh = dyv * g_ref[...]
        m1 = jnp.mean(dxh, axis=-1, keepdims=True)
        m2 = jnp.mean(dxh * xhv, axis=-1, keepdims=True)
        ds = r_ref[...] * (dxh - m1 - xhv * m2)
        ds_ref[...] = ds
        dsb_ref[...] = ds.astype(BF16)
        pg = jnp.sum(dyv * xhv, axis=0, keepdims=True)
        pb = jnp.sum(dyv, axis=0, keepdims=True)

        @pl.when(i == 0)
        def _():
            dg_ref[...] = pg
            db_ref[...] = pb

        @pl.when(i > 0)
        def _():
            dg_ref[...] += pg
            db_ref[...] += pb

    row = pl.BlockSpec((tm, D), lambda i: (i, 0))
    vec = pl.BlockSpec((1, D), lambda i: (0, 0))
    return pl.pallas_call(
        body, name="ln_bwd", grid=(T // tm,),
        in_specs=[row, row, pl.BlockSpec((tm, 1), lambda i: (i, 0)), vec],
        out_specs=[row, vec, vec, row],
        out_shape=[jax.ShapeDtypeStruct((T, D), F32), jax.ShapeDtypeStruct((1, D), F32),
                   jax.ShapeDtypeStruct((1, D), F32), jax.ShapeDtypeStruct((T, D), BF16)],
        compiler_params=_cp(("arbitrary",)))(dy, xh, r, gam)


def _loss_grad(y, tgt):
    tm = 256

    def body(y_ref, t_ref, l_ref, dy_ref):
        i = pl.program_id(0)
        e = y_ref[...] - t_ref[...]
        dy_ref[...] = e * (1.0 / D)
        p = jnp.sum(jnp.sum(e * e, axis=1, keepdims=True), axis=0, keepdims=True)

        @pl.when(i == 0)
        def _():
            l_ref[...] = p

        @pl.when(i > 0)
        def _():
            l_ref[...] += p

    row = pl.BlockSpec((tm, D), lambda i: (i, 0))
    return pl.pallas_call(
        body, name="loss_grad", grid=(T // tm,), in_specs=[row, row],
        out_specs=[pl.BlockSpec((1, 1), lambda i: (0, 0)), row],
        out_shape=[jax.ShapeDtypeStruct((1, 1), F32), jax.ShapeDtypeStruct((T, D), F32)],
        compiler_params=_cp(("arbitrary",)))(y, tgt)


def _adamw(w, g, m, v, tr):
    L, R, C = w.shape

    def body(w_ref, g_ref, m_ref, v_ref, d_ref, m2_ref, v2_ref):
        gv = g_ref[...]
        m2 = ADAM_B1 * m_ref[...] + (1.0 - ADAM_B1) * gv
        v2 = ADAM_B2 * v_ref[...] + (1.0 - ADAM_B2) * (gv * gv)
        m_hat = m2 / (1.0 - ADAM_B1 ** ADAM_STEP)
        v_hat = v2 / (1.0 - ADAM_B2 ** ADAM_STEP)
        d_ref[...] = -ADAM_LR * (m_hat / (jnp.sqrt(v_hat) + ADAM_EPS) + ADAM_WD * w_ref[...])
        m2_ref[...] = m2
        v2_ref[...] = v2

    blk = pl.BlockSpec((None, tr, C), lambda l, i: (l, i, 0))
    sh = jax.ShapeDtypeStruct((L, R, C), F32)
    return pl.pallas_call(
        body, name="adamw", grid=(L, R // tr), in_specs=[blk] * 4, out_specs=[blk] * 3,
        out_shape=[sh, sh, sh], compiler_params=_cp(("parallel", "parallel")))(w, g, m, v)


class _Job:
    def __init__(self, ins, out_shapes, aliases, sems, start, finish):
        self.ins, self.out_shapes, self.aliases, self.sems = list(ins), list(out_shapes), dict(aliases), list(sems)
        self.start, self.finish = start, finish


def _host_call(body, name, ins, in_specs, out_shapes, out_specs, scratch, aliases, job):
    n_in, n_out, n_scr = len(ins), len(out_shapes), len(scratch)
    jins = job.ins if job else []
    jouts = job.out_shapes if job else []
    jsems = job.sems if job else []

    def wrapped(*refs):
        a = n_in
        b = a + len(jins)
        c = b + n_out
        d = c + len(jouts)
        e = d + n_scr
        comm = None
        if job:
            jrefs = (refs[a:b], refs[c:d], refs[e:])
            comm = (lambda: job.start(*jrefs), lambda st: job.finish(st, *jrefs))
        body(refs[:a], refs[b:c], refs[d:e], comm)

    al = dict(aliases)
    if job:
        for ji, jo in job.aliases.items():
            al[n_in + ji] = n_out + jo
    res = pl.pallas_call(
        wrapped, name=name, in_specs=list(in_specs) + [ANY_SPEC] * len(jins),
        out_specs=list(out_specs) + [ANY_SPEC] * len(jouts), out_shape=list(out_shapes) + list(jouts),
        scratch_shapes=list(scratch) + list(jsems), input_output_aliases=al,
        compiler_params=_cp())(*ins, *jins)
    return res[:n_out], res[n_out:]


def _copy_in(src, dst, sem):
    cp = pltpu.make_async_copy(src, dst, sem)
    cp.start()
    cp.wait()


CHAINS = [(p, b) for p in range(2) for b in range(BL)]
NC = len(CHAINS)
ROWS_SHAPE = jax.ShapeDtypeStruct((NSTAT, S), F32)
SLAB_QKV = pltpu.VMEM((T, 2 * PAIRW), BF16)
SLAB_OUT = pltpu.VMEM((T, 2 * BQ), BF16)
SLAB_O32 = pltpu.VMEM((T, 2 * BQ), F32)
SLAB_T = pltpu.VMEM((2, BQ, T), BF16)
SLAB_KEYB = pltpu.VMEM((NSTAT, S, BQ), F32)
ACC_KV = pltpu.VMEM((2, T, BQ), F32)


def _lane_masks():
    lane = _iota((1, BQ), 1)
    m0 = (lane < 64).astype(BF16)
    return m0, 1.0 - m0


def _row_masks():
    r = _iota((BQ, 1), 0)
    m0 = (r < 64).astype(BF16)
    return m0, 1.0 - m0


def _stack(x, m0, m1):
    return jnp.concatenate([x * m0, x * m1], axis=0)


def _stack_t(xt, r0, r1):
    return jnp.concatenate([xt * r0, xt * r1], axis=1)


def _tr(x):
    return x.T


def _rows(b, i):
    return pl.ds(pl.multiple_of(b * S + i * BQ, BQ), BQ)


def _transpose_slab(src, dst, col0):
    def blk(n, _):
        r = pl.ds(pl.multiple_of(n * BQ, BQ), BQ)
        for p in range(2):
            dst[p, :, r] = _tr(src[r, col0(p):col0(p) + BQ])
        return 0

    lax.fori_loop(0, T // BQ, blk, 0)


def _heads(x):
    return x[:BQ], x[BQ:]


def _bcast_heads(r0, r1):
    return jnp.concatenate([jnp.broadcast_to(r0, (BQ, BQ)), jnp.broadcast_to(r1, (BQ, BQ))], axis=0)


def _by_channel(r0, r1):
    return jnp.where(_iota((BQ, BQ), 0) < 64, r0, r1)


def _colsum2(x):
    return jnp.sum(x[:BQ], axis=0, keepdims=True), jnp.sum(x[BQ:], axis=0, keepdims=True)


def _stat_row(ref, p, b, h, i):
    c = b * NH + 2 * p + h
    return ref[c:c + 1, pl.ds(pl.multiple_of(i * BQ, BQ), BQ)]


def _put_row(ref, p, b, h, i, v):
    c = b * NH + 2 * p + h
    ref[c:c + 1, pl.ds(pl.multiple_of(i * BQ, BQ), BQ)] = v


def _valid_t(strict):
    r = _iota((HB, BQ), 0) & (BQ - 1)
    c = _iota((HB, BQ), 1)
    return (r < c) if strict else (r <= c)


def _tri_blockdiag(later):
    r = _iota((HB, HB), 0)
    c = _iota((HB, HB), 1)
    same = (r >= BQ) == (c >= BQ)
    return (same & ((c > r) if later else (c < r))).astype(BF16)


def _cum_mm(tri, x):
    y = _dot(tri, _split2(x))
    return y[:, :BQ] + y[:, BQ:]


def _kv_tiles(qkv_v, p, b, j):
    r = _rows(b, j)
    return qkv_v[r, p * PAIRW + BQ:p * PAIRW + 2 * BQ], qkv_v[r, p * PAIRW + 2 * BQ:p * PAIRW + 3 * BQ]


def _q_tile(qkv_v, p, b, i):
    return qkv_v[_rows(b, i), p * PAIRW:p * PAIRW + BQ] * SCALE


def _sb_fwd(qkv, job=None):
    def body(ins, outs, scr, comm):
        (qkv_hbm,), (o_hbm, r_ref), (qkv_v, o_v, sem, vt_v) = ins, outs, scr
        _copy_in(qkv_hbm.at[:, pl.ds(0, 2 * PAIRW)], qkv_v, sem)
        st = comm[0]() if comm else None
        _transpose_slab(qkv_v, vt_v, lambda p: p * PAIRW + 2 * BQ)
        m0, m1 = _lane_masks()
        r0, r1 = _row_masks()
        valid = _valid_t(True)
        later = _tri_blockdiag(True)

        def steps(qts, i, j, cs, diag):
            ks = [_stack(_kv_tiles(qkv_v, p, b, j)[0], m0, m1) for p, b in CHAINS]
            zs = [_dot(ks[c], qts[c]) for c in range(NC)]
            lbs, lrs = [], []
            for c in range(NC):
                lb = _log_sigmoid_tile(zs[c])
                lr = lb - zs[c]
                if diag:
                    lr = jnp.where(valid, lr, 0.0)
                lbs.append(lb)
                lrs.append(lr)
            tails = [_cum_mm(later, lrs[c]) for c in range(NC)]
            avs = []
            for c in range(NC):
                a = jnp.exp(lbs[c] + tails[c] + _bcast_heads(*cs[c][0]))
                if diag:
                    a = jnp.where(valid, a, 0.0)
                avs.append(a.astype(BF16))
            out = []
            for c, (p, b) in enumerate(CHAINS):
                vts = _stack_t(vt_v[p, :, _rows(b, j)], r0, r1)
                s0, s1 = _colsum2(lrs[c])
                out.append(((cs[c][0][0] + s0, cs[c][0][1] + s1), cs[c][1] + _dot(vts, avs[c])))
            return tuple(out)

        def qblock(i, _):
            qts = [_tr(_q_tile(qkv_v, p, b, i)) for p, b in CHAINS]
            zr = jnp.zeros((1, BQ), F32)
            cs = steps(qts, i, i, (((zr, zr), jnp.zeros((BQ, BQ), F32)),) * NC, True)
            cs = lax.fori_loop(1, i + 1, lambda jj, cs: steps(qts, i, i - jj, cs, False), cs)
            for c, (p, b) in enumerate(CHAINS):
                o_v[_rows(b, i), p * BQ:(p + 1) * BQ] = cs[c][1].T.astype(BF16)
                for h in range(2):
                    _put_row(r_ref, p, b, h, i, cs[c][0][h])
            return 0

        lax.fori_loop(0, NB, qblock, 0)
        _copy_in(o_v, o_hbm.at[:, pl.ds(0, 2 * BQ)], sem)
        if comm:
            comm[1](st)

    (mixed, rtot), extra = _host_call(
        body, "sb_fwd", [qkv], [ANY_SPEC], [jax.ShapeDtypeStruct((T, D), BF16), ROWS_SHAPE], [ANY_SPEC, VMEM_SPEC],
        [SLAB_QKV, SLAB_OUT, pltpu.SemaphoreType.DMA, SLAB_T], {}, job)
    return mixed, rtot, extra


def _sb_bwd(qkv, dmixed, rtot, job=None):
    def body(ins, outs, scr, comm):
        (qkv_hbm, do_hbm, r_ref), (dqkv_hbm,), (qkv_v, do_v, dq_v, dk_s, dv_s, sem, kt_v) = ins, outs, scr
        _copy_in(qkv_hbm.at[:, pl.ds(0, 2 * PAIRW)], qkv_v, sem)
        _copy_in(do_hbm.at[:, pl.ds(0, 2 * BQ)], do_v, sem)
        st = comm[0]() if comm else None
        _transpose_slab(qkv_v, kt_v, lambda p: p * PAIRW + BQ)
        m0, m1 = _lane_masks()
        f0, f1 = m0.astype(F32), m1.astype(F32)
        r0, r1 = _row_masks()
        valid = _valid_t(True)
        later = _tri_blockdiag(True)
        earlier = _tri_blockdiag(False)
        dk_s[...] = jnp.zeros_like(dk_s)
        dv_s[...] = jnp.zeros_like(dv_s)

        def steps(qns, qts, dns, dts, rts, i, j, cs, diag):
            kv = [_kv_tiles(qkv_v, p, b, j) for p, b in CHAINS]
            ks = [_stack(kv[c][0], m0, m1) for c in range(NC)]
            vs = [_stack(kv[c][1], m0, m1) for c in range(NC)]
            zs = [_dot(ks[c], qts[c]) for c in range(NC)]
            das = [_dot(vs[c], dts[c]) for c in range(NC)]
            lbs, lrs, pls = [], [], []
            for c in range(NC):
                lb = _log_sigmoid_tile(zs[c])
                lr = lb - zs[c]
                if diag:
                    lr = jnp.where(valid, lr, 0.0)
                s0, s1 = _colsum2(lr)
                lbs.append(lb)
                lrs.append(lr)
                pls.append((cs[c][0][0] + s0, cs[c][0][1] + s1))
            tails = [_cum_mm(later, lrs[c]) for c in range(NC)]
            avs, gms = [], []
            for c in range(NC):
                a = jnp.exp(lbs[c] + tails[c] + _bcast_heads(rts[c][0] - pls[c][0], rts[c][1] - pls[c][1]))
                if diag:
                    a = jnp.where(valid, a, 0.0)
                avs.append(a)
                gms.append(das[c] * a)
            befores = [_cum_mm(earlier, gms[c]) for c in range(NC)]
            dzbs = []
            for c in range(NC):
                beta = jnp.exp(lbs[c])
                dz = gms[c] - beta * (gms[c] + befores[c] + _bcast_heads(*cs[c][1]))
                if diag:
                    dz = jnp.where(valid, dz, 0.0)
                dzbs.append(dz.astype(BF16))
            out = []
            for c, (p, b) in enumerate(CHAINS):
                dq = cs[c][2] + _dot(_stack_t(kt_v[p, :, _rows(b, j)], r0, r1), dzbs[c])
                dk = _dot(dzbs[c], qns[c])
                dv = _dot(avs[c].astype(BF16), dns[c])
                dk_s[p, _rows(b, j), :] += dk[:BQ] * f0 + dk[BQ:] * f1
                dv_s[p, _rows(b, j), :] += dv[:BQ] * f0 + dv[BQ:] * f1
                g0, g1 = _colsum2(gms[c])
                out.append((pls[c], (cs[c][1][0] + g0, cs[c][1][1] + g1), dq))
            return tuple(out)

        def qblock(i, _):
            qns = [_q_tile(qkv_v, p, b, i) for p, b in CHAINS]
            dns = [do_v[_rows(b, i), p * BQ:(p + 1) * BQ] for p, b in CHAINS]
            qts = [_tr(t) for t in qns]
            dts = [_tr(t) for t in dns]
            rts = [(_stat_row(r_ref, p, b, 0, i), _stat_row(r_ref, p, b, 1, i)) for p, b in CHAINS]
            zr = jnp.zeros((1, BQ), F32)
            cs = (((zr, zr), (zr, zr), jnp.zeros((BQ, BQ), F32)),) * NC
            cs = lax.fori_loop(0, i, lambda j, cs: steps(qns, qts, dns, dts, rts, i, j, cs, False), cs)
            cs = steps(qns, qts, dns, dts, rts, i, i, cs, True)
            for c, (p, b) in enumerate(CHAINS):
                dq_v[_rows(b, i), p * PAIRW:p * PAIRW + BQ] = (cs[c][2].T * SCALE).astype(BF16)
            return 0

        lax.fori_loop(0, NB, qblock, 0)
        for p in range(2):
            dq_v[:, p * PAIRW + BQ:p * PAIRW + 2 * BQ] = dk_s[p].astype(BF16)
            dq_v[:, p * PAIRW + 2 * BQ:p * PAIRW + 3 * BQ] = dv_s[p].astype(BF16)
        _copy_in(dq_v, dqkv_hbm.at[:, pl.ds(0, 2 * PAIRW)], sem)
        if comm:
            comm[1](st)

    (dqkv,), extra = _host_call(
        body, "sb_bwd", [qkv, dmixed, rtot], [ANY_SPEC, ANY_SPEC, VMEM_SPEC],
        [jax.ShapeDtypeStruct((T, QKVW), BF16)], [ANY_SPEC],
        [SLAB_QKV, SLAB_OUT, SLAB_QKV, ACC_KV, ACC_KV, pltpu.SemaphoreType.DMA, SLAB_T], {}, job)
    return dqkv, extra


def _flash_fwd(qkv, mixed, g, fox, bias, job=None):
    def body(ins, outs, scr, comm):
        if fox:
            qkv_hbm, cq_ref, ckb_hbm, _ = ins
            (o_hbm, lse_ref, o32_hbm), (qkv_v, o_v, sem, vt_v, o32_v, ckb_v) = outs, scr
        else:
            qkv_hbm, tbl_ref, _ = ins
            (o_hbm, lse_ref), (qkv_v, o_v, sem, vt_v) = outs, scr
        _copy_in(qkv_hbm.at[:, pl.ds(g * 2 * PAIRW, 2 * PAIRW)], qkv_v, sem)
        if fox:
            _copy_in(ckb_hbm, ckb_v, sem)
        st = comm[0]() if comm else None
        _transpose_slab(qkv_v, vt_v, lambda p: p * PAIRW + 2 * BQ)
        m0, m1 = _lane_masks()
        r0, r1 = _row_masks()
        valid = _valid_t(False)

        def steps(qts, cqs, i, j, cs, diag):
            ks = [_stack(_kv_tiles(qkv_v, p, b, j)[0], m0, m1) for p, b in CHAINS]
            zs = [_dot(ks[c], qts[c]) for c in range(NC)]
            prs, alphas, out = [], [], []
            for c, (p, b) in enumerate(CHAINS):
                (ma, mb), (la, lb_), _ = cs[c]
                if fox:
                    kk = pl.ds(pl.multiple_of(j * BQ, BQ), BQ)
                    col = b * NH + 2 * p
                    z = zs[c] + (cqs[c] - jnp.concatenate([ckb_v[col, kk, :], ckb_v[col + 1, kk, :]], axis=0))
                    if diag:
                        z = jnp.where(valid, z, NEG)
                else:
                    z = zs[c] + tbl_ref[p, i - j]
                za, zb = _heads(z)
                na = jnp.maximum(ma, jnp.max(za, axis=0, keepdims=True))
                nb = jnp.maximum(mb, jnp.max(zb, axis=0, keepdims=True))
                aa, ab = jnp.exp(ma - na), jnp.exp(mb - nb)
                pr = jnp.exp(z - _bcast_heads(na, nb))
                sa, sb = _colsum2(pr)
                prs.append(_split2(pr) if fox else pr.astype(BF16))
                alphas.append((aa, ab))
                out.append(((na, nb), (aa * la + sa, ab * lb_ + sb)))
            pvs = []
            for c, (p, b) in enumerate(CHAINS):
                vts = _stack_t(vt_v[p, :, _rows(b, j)], r0, r1)
                if fox:
                    pvs.append(_dot(vts, prs[c][:, :BQ]) + _dot(vts, prs[c][:, BQ:]))
                else:
                    pvs.append(_dot(vts, prs[c]))
            return tuple((out[c][0], out[c][1], _by_channel(*alphas[c]) * cs[c][2] + pvs[c]) for c in range(NC))

        def qblock(i, _):
            qts = [_tr(_q_tile(qkv_v, p, b, i)) for p, b in CHAINS]
            if fox:
                cqs = [_bcast_heads(_stat_row(cq_ref, p, b, 0, i), _stat_row(cq_ref, p, b, 1, i)) for p, b in CHAINS]
            else:
                cqs = [None] * NC
            ng = jnp.full((1, BQ), NEG, F32)
            zr = jnp.zeros((1, BQ), F32)
            cs = steps(qts, cqs, i, i, (((ng, ng), (zr, zr), jnp.zeros((BQ, BQ), F32)),) * NC, True)
            cs = lax.fori_loop(1, i + 1, lambda jj, cs: steps(qts, cqs, i, i - jj, cs, False), cs)
            for c, (p, b) in enumerate(CHAINS):
                (ma, mb), (la, lb_), acc = cs[c]
                o = (acc / _by_channel(la, lb_)).T
                o_v[_rows(b, i), p * BQ:(p + 1) * BQ] = o.astype(BF16)
                if fox:
                    o32_v[_rows(b, i), p * BQ:(p + 1) * BQ] = o
                _put_row(lse_ref, p, b, 0, i, ma + jnp.log(la))
                _put_row(lse_ref, p, b, 1, i, mb + jnp.log(lb_))
            return 0

        lax.fori_loop(0, NB, qblock, 0)
        _copy_in(o_v, o_hbm.at[:, pl.ds(g * 2 * BQ, 2 * BQ)], sem)
        if fox:
            _copy_in(o32_v, o32_hbm, sem)
        if comm:
            comm[1](st)

    bias_specs = [VMEM_SPEC, ANY_SPEC] if fox else [VMEM_SPEC]
    n_in = 2 + len(bias_specs)
    o32 = [jax.ShapeDtypeStruct((T, 2 * BQ), F32)] if fox else []
    res, extra = _host_call(
        body, "fox_fwd" if fox else "dil_fwd", [qkv, *bias, mixed], [ANY_SPEC] + bias_specs + [ANY_SPEC],
        [jax.ShapeDtypeStruct((T, D), BF16), ROWS_SHAPE] + o32, [ANY_SPEC, VMEM_SPEC] + [ANY_SPEC] * len(o32),
        [SLAB_QKV, SLAB_OUT, pltpu.SemaphoreType.DMA, SLAB_T] + ([SLAB_O32, SLAB_KEYB] if fox else []),
        {n_in - 1: 0}, job)
    return (*res, extra)


def _flash_bwd(qkv, o, dmixed, lse, dqkv, g, fox, bias, job=None):
    def body(ins, outs, scr, comm):
        if fox:
            qkv_hbm, o_hbm, do_hbm, lse_ref, cq_ref, ckb_hbm, _ = ins
            (dqkv_hbm, db_ref), (qkv_v, o_v, do_v, dq_v, dk_s, dv_s, sem, kt_v, ckb_v, dc_s) = outs, scr
        else:
            qkv_hbm, o_hbm, do_hbm, lse_ref, tbl_ref, _ = ins
            (dqkv_hbm, db_ref), (qkv_v, o_v, do_v, dq_v, dk_s, dv_s, sem, kt_v) = outs, scr
        _copy_in(qkv_hbm.at[:, pl.ds(g * 2 * PAIRW, 2 * PAIRW)], qkv_v, sem)
        _copy_in(do_hbm.at[:, pl.ds(g * 2 * BQ, 2 * BQ)], do_v, sem)
        if fox:
            _copy_in(o_hbm, o_v, sem)
            _copy_in(ckb_hbm, ckb_v, sem)
        else:
            _copy_in(o_hbm.at[:, pl.ds(g * 2 * BQ, 2 * BQ)], o_v, sem)
        st = comm[0]() if comm else None
        _transpose_slab(qkv_v, kt_v, lambda p: p * PAIRW + BQ)
        m0, m1 = _lane_masks()
        f0, f1 = m0.astype(F32), m1.astype(F32)
        r0, r1 = _row_masks()
        valid = _valid_t(False)
        dk_s[...] = jnp.zeros_like(dk_s)
        dv_s[...] = jnp.zeros_like(dv_s)
        if fox:
            dc_s[...] = jnp.zeros_like(dc_s)
        else:
            db_ref[...] = jnp.zeros_like(db_ref)

        def steps(qns, qts, dns, dts, cqs, lses, deltas, i, j, dqs, diag):
            kv = [_kv_tiles(qkv_v, p, b, j) for p, b in CHAINS]
            ks = [_stack(kv[c][0], m0, m1) for c in range(NC)]
            vs = [_stack(kv[c][1], m0, m1) for c in range(NC)]
            zs = [_dot(ks[c], qts[c]) for c in range(NC)]
            dps = [_dot(vs[c], dts[c]) for c in range(NC)]
            prs, dzl = [], []
            for c, (p, b) in enumerate(CHAINS):
                if fox:
                    kk = pl.ds(pl.multiple_of(j * BQ, BQ), BQ)
                    col = b * NH + 2 * p
                    z = zs[c] + (cqs[c] - jnp.concatenate([ckb_v[col, kk, :], ckb_v[col + 1, kk, :]], axis=0))
                    if diag:
                        z = jnp.where(valid, z, NEG)
                else:
                    z = zs[c] + tbl_ref[p, i - j]
                pr = jnp.exp(z - lses[c])
                prs.append(pr.astype(BF16))
                dzl.append(pr * (dps[c] - deltas[c]))
            dzbs = [dz.astype(BF16) for dz in dzl]
            new = []
            for c, (p, b) in enumerate(CHAINS):
                new.append(dqs[c] + _dot(_stack_t(kt_v[p, :, _rows(b, j)], r0, r1), dzbs[c]))
                dk = _dot(dzbs[c], qns[c])
                dv = _dot(prs[c], dns[c])
                dk_s[p, _rows(b, j), :] += dk[:BQ] * f0 + dk[BQ:] * f1
                dv_s[p, _rows(b, j), :] += dv[:BQ] * f0 + dv[BQ:] * f1
                if fox:
                    dc_s[c, pl.ds(pl.multiple_of(j * HB, HB), HB), :] += dzl[c]
            if not fox:
                for p in range(2):
                    db_ref[p, i - j] = db_ref[p, i - j] + (dzl[2 * p] + dzl[2 * p + 1])
            return tuple(new)

        def qblock(i, _):
            qns = [_q_tile(qkv_v, p, b, i) for p, b in CHAINS]
            dns = [do_v[_rows(b, i), p * BQ:(p + 1) * BQ] for p, b in CHAINS]
            qts = [_tr(t) for t in qns]
            dts = [_tr(t) for t in dns]
            lses = [_bcast_heads(_stat_row(lse_ref, p, b, 0, i), _stat_row(lse_ref, p, b, 1, i)) for p, b in CHAINS]
            if fox:
                cqs = [_bcast_heads(_stat_row(cq_ref, p, b, 0, i), _stat_row(cq_ref, p, b, 1, i)) for p, b in CHAINS]
            else:
                cqs = [None] * NC
            deltas = []
            for c, (p, b) in enumerate(CHAINS):
                pt = (dns[c].astype(F32) * o_v[_rows(b, i), p * BQ:(p + 1) * BQ].astype(F32)).T
                deltas.append(_bcast_heads(jnp.sum(pt[:64], axis=0, keepdims=True), jnp.sum(pt[64:], axis=0, keepdims=True)))
            dqs = (jnp.zeros((BQ, BQ), F32),) * NC
            dqs = lax.fori_loop(0, i, lambda j, d: steps(qns, qts, dns, dts, cqs, lses, deltas, i, j, d, False), dqs)
            dqs = steps(qns, qts, dns, dts, cqs, lses, deltas, i, i, dqs, True)
            for c, (p, b) in enumerate(CHAINS):
                dq_v[_rows(b, i), p * PAIRW:p * PAIRW + BQ] = (dqs[c].T * SCALE).astype(BF16)
            return 0

        lax.fori_loop(0, NB, qblock, 0)
        for p in range(2):
            dq_v[:, p * PAIRW + BQ:p * PAIRW + 2 * BQ] = dk_s[p].astype(BF16)
            dq_v[:, p * PAIRW + 2 * BQ:p * PAIRW + 3 * BQ] = dv_s[p].astype(BF16)
        _copy_in(dq_v, dqkv_hbm.at[:, pl.ds(g * 2 * PAIRW, 2 * PAIRW)], sem)
        if fox:
            lane = _iota((BQ, NSTAT), 1)

            def fold(n, _):
                t = jnp.zeros((BQ, NSTAT), F32)
                for c, (p, b) in enumerate(CHAINS):
                    s = jnp.sum(dc_s[c, pl.ds(pl.multiple_of(n * HB, HB), HB), :], axis=1, keepdims=True)
                    col = b * NH + 2 * p
                    t = t - jnp.where(lane == col, s[:BQ], 0.0) - jnp.where(lane == col + 1, s[BQ:], 0.0)
                db_ref[pl.ds(pl.multiple_of(n * BQ, BQ), BQ), :] = t
                return 0

            lax.fori_loop(0, NB, fold, 0)
        if comm:
            comm[1](st)

    if fox:
        bias_specs = [VMEM_SPEC, ANY_SPEC]
        db_shape = jax.ShapeDtypeStruct((S, NSTAT), F32)
        more = [SLAB_KEYB, pltpu.VMEM((NC, NB * HB, BQ), F32)]
    else:
        bias_specs = [VMEM_SPEC]
        db_shape = jax.ShapeDtypeStruct((2, NB, HB, BQ), F32)
        more = []
    n_in = 5 + len(bias_specs)
    (dqkv, db), extra = _host_call(
        body, "fox_bwd" if fox else "dil_bwd", [qkv, o, dmixed, lse, *bias, dqkv],
        [ANY_SPEC, ANY_SPEC, ANY_SPEC, VMEM_SPEC] + bias_specs + [ANY_SPEC],
        [jax.ShapeDtypeStruct((T, QKVW), BF16), db_shape], [ANY_SPEC, VMEM_SPEC],
        [SLAB_QKV, SLAB_O32 if fox else SLAB_OUT, SLAB_OUT, SLAB_QKV, ACC_KV, ACC_KV, pltpu.SemaphoreType.DMA, SLAB_T]
        + more, {n_in - 1: 0}, job)
    return dqkv, db, extra


def _delta_t(d):
    return d * BQ + _iota((HB, BQ), 1) - (_iota((HB, BQ), 0) & (BQ - 1))


def _buckets_in(d):
    lo, hi = max(d * BQ - (BQ - 1), 0), d * BQ + BQ - 1
    return [b for b in range(32) if BUCKET_TH[b] <= hi and (b == 31 or BUCKET_TH[b + 1] > lo)]


def _in_bucket(delta, b):
    m = delta >= BUCKET_TH[b]
    return m if b == 31 else m & (delta < BUCKET_TH[b + 1])


def _dil_table(rel_bias):
    def body(rb_ref, o_ref):
        for d in range(NB):
            delta = _delta_t(d)
            pos = delta >= 0
            n = ((pos & (delta <= 128)).astype(jnp.int32)
                 + (pos & (delta <= 512) & ((delta & 3) == 0)).astype(jnp.int32)
                 + (pos & ((delta & 15) == 0)).astype(jnp.int32))
            logn = jnp.where(n == 3, math.log(3.0), jnp.where(n == 2, math.log(2.0), jnp.where(n == 1, 0.0, NEG)))
            head1 = _iota((HB, BQ), 0) >= BQ
            for p in range(2):
                val = jnp.zeros((HB, BQ), F32)
                for b in _buckets_in(d):
                    val = jnp.where(_in_bucket(delta, b), jnp.where(head1, rb_ref[b, 2 * p + 1], rb_ref[b, 2 * p]), val)
                o_ref[p, d] = val + logn

    return pl.pallas_call(
        body, name="dil_table", in_specs=[pl.BlockSpec(memory_space=pltpu.SMEM)], out_specs=VMEM_SPEC,
        out_shape=jax.ShapeDtypeStruct((2, NB, HB, BQ), F32), compiler_params=_cp())(rel_bias)


def _dil_table_bwd(dtbl):
    def body(dt_ref, o_ref):
        p = pl.program_id(0)
        rowi = _iota((32, BQ), 0)
        lanei = _iota((32, BQ), 1)

        @pl.when(p == 0)
        def _():
            o_ref[...] = jnp.zeros_like(o_ref)

        out = jnp.zeros((32, BQ), F32)
        for b in range(32):
            acc = None
            for d in range(NB):
                if b in _buckets_in(d):
                    t = jnp.where(_in_bucket(_delta_t(d), b), dt_ref[d], 0.0)
                    acc = t if acc is None else acc + t
            rs = jnp.sum(acc, axis=1, keepdims=True)
            s0 = jnp.sum(rs[:BQ], axis=0, keepdims=True)
            s1 = jnp.sum(rs[BQ:], axis=0, keepdims=True)
            out = (out + jnp.where((rowi == b) & (lanei == 2 * p), s0, 0.0)
                   + jnp.where((rowi == b) & (lanei == 2 * p + 1), s1, 0.0))
        o_ref[...] += out

    return pl.pallas_call(
        body, name="dil_table_bwd", grid=(2,),
        in_specs=[pl.BlockSpec((None, NB, HB, BQ), lambda p: (p, 0, 0, 0))],
        out_specs=pl.BlockSpec((32, BQ), lambda p: (0, 0)),
        out_shape=jax.ShapeDtypeStruct((32, BQ), F32),
        compiler_params=_cp(("arbitrary",)))(dtbl)


def _fox_prep(gate, fb):
    def body(g_ref, fb_ref, c_ref):
        tri = (_iota((BQ, BQ), 0) >= _iota((BQ, BQ), 1)).astype(BF16)

        def blk(i, carry):
            r0 = pl.multiple_of(i * BQ, BQ)
            lf = _log_sigmoid(g_ref[pl.ds(r0, BQ), :] + fb_ref[...])
            c = _dot(tri, _split3(lf))
            c_ref[pl.ds(r0, BQ), :] = c[:, 0:BQ] + c[:, BQ:2 * BQ] + c[:, 2 * BQ:3 * BQ] + carry
            return carry + jnp.sum(lf, axis=0, keepdims=True)

        lax.fori_loop(0, NB, blk, jnp.zeros((1, BQ), F32))

    blk = pl.BlockSpec((S, GATEW), lambda b: (b, 0))
    return pl.pallas_call(
        body, name="fox_prep", grid=(BL,), in_specs=[blk, pl.BlockSpec((1, GATEW), lambda b: (0, 0))],
        out_specs=blk, out_shape=jax.ShapeDtypeStruct((T, GATEW), F32),
        compiler_params=_cp(("parallel",)))(gate, fb)


def _fox_post(dcum, gate, fb):
    def body(dc_ref, g_ref, fb_ref, dg_ref, dfb_ref):
        b = pl.program_id(0)
        tri = (_iota((BQ, BQ), 0) <= _iota((BQ, BQ), 1)).astype(BF16)

        def blk(ii, carry):
            csum, dfb = carry
            r0 = pl.multiple_of((NB - 1 - ii) * BQ, BQ)
            dc = dc_ref[pl.ds(r0, BQ), :]
            c = _dot(tri, _split3(dc))
            dlf = c[:, 0:BQ] + c[:, BQ:2 * BQ] + c[:, 2 * BQ:3 * BQ] + csum
            dg = dlf * jnp.exp(_log_sigmoid(-(g_ref[pl.ds(r0, BQ), :] + fb_ref[...])))
            dg_ref[pl.ds(r0, BQ), :] = dg
            return csum + jnp.sum(dc, axis=0, keepdims=True), dfb + jnp.sum(dg, axis=0, keepdims=True)

        z = jnp.zeros((1, BQ), F32)
        _, dfb = lax.fori_loop(0, NB, blk, (z, z))

        @pl.when(b == 0)
        def _():
            dfb_ref[...] = dfb

        @pl.when(b > 0)
        def _():
            dfb_ref[...] += dfb

    blk = pl.BlockSpec((S, GATEW), lambda b: (b, 0))
    vec = pl.BlockSpec((1, GATEW), lambda b: (0, 0))
    return pl.pallas_call(
        body, name="fox_post", grid=(BL,), in_specs=[blk, blk, vec], out_specs=[blk, vec],
        out_shape=[jax.ShapeDtypeStruct((T, GATEW), F32), jax.ShapeDtypeStruct((1, GATEW), F32)],
        compiler_params=_cp(("arbitrary",)))(dcum, gate, fb)


def _shift_down(x, n):
    return jnp.where(_iota(x.shape, 0) >= n, pltpu.roll(x, n, 0), 0.0)


def _shift_up(x, n):
    return jnp.where(_iota(x.shape, 0) < S - n, pltpu.roll(x, S - n, 0), 0.0)


def _conv_fwd(conv, cw, mixed):
    W = 256

    def body(c_ref, w_ref, _, o_ref):
        u = c_ref[:, W:2 * W] * c_ref[:, 2 * W:3 * W]
        y = w_ref[0:1, :] * _shift_down(u, 2) + w_ref[1:2, :] * _shift_down(u, 1) + w_ref[2:3, :] * u
        o_ref[...] = (c_ref[:, 0:W] * y).astype(BF16)

    return pl.pallas_call(
        body, name="conv_fwd", grid=(BL,),
        in_specs=[pl.BlockSpec((S, CONVW), lambda b: (b, 0)), pl.BlockSpec((8, W), lambda b: (0, 0)), ANY_SPEC],
        out_specs=pl.BlockSpec((S, W), lambda b: (b, 3)),
        out_shape=jax.ShapeDtypeStruct((T, D), BF16), input_output_aliases={2: 0},
        compiler_params=_cp(("parallel",)))(conv, cw, mixed)


def _conv_bwd(conv, cw, dmixed):
    W = 256

    def body(c_ref, w_ref, do_ref, dc_ref, dw_ref):
        b = pl.program_id(0)
        bg = c_ref[:, 0:W]
        cg = c_ref[:, W:2 * W]
        hv = c_ref[:, 2 * W:3 * W]
        do = do_ref[...].astype(F32)
        u = cg * hv
        u1 = _shift_down(u, 1)
        u2 = _shift_down(u, 2)
        y = w_ref[0:1, :] * u2 + w_ref[1:2, :] * u1 + w_ref[2:3, :] * u
        dy = do * bg
        du = w_ref[2:3, :] * dy + w_ref[1:2, :] * _shift_up(dy, 1) + w_ref[0:1, :] * _shift_up(dy, 2)
        dc_ref[:, 0:W] = (do * y).astype(BF16)
        dc_ref[:, W:2 * W] = (du * hv).astype(BF16)
        dc_ref[:, 2 * W:3 * W] = (du * cg).astype(BF16)
        rowi = _iota((8, W), 0)
        dw = (jnp.where(rowi == 0, jnp.sum(dy * u2, axis=0, keepdims=True), 0.0)
              + jnp.where(rowi == 1, jnp.sum(dy * u1, axis=0, keepdims=True), 0.0)
              + jnp.where(rowi == 2, jnp.sum(dy * u, axis=0, keepdims=True), 0.0))

        @pl.when(b == 0)
        def _():
            dw_ref[...] = dw

        @pl.when(b > 0)
        def _():
            dw_ref[...] += dw

    return pl.pallas_call(
        body, name="conv_bwd", grid=(BL,),
        in_specs=[pl.BlockSpec((S, CONVW), lambda b: (b, 0)), pl.BlockSpec((8, W), lambda b: (0, 0)),
                  pl.BlockSpec((S, W), lambda b: (b, 3))],
        out_specs=[pl.BlockSpec((S, CONVW), lambda b: (b, 0)), pl.BlockSpec((8, W), lambda b: (0, 0))],
        out_shape=[jax.ShapeDtypeStruct((T, CONVW), BF16), jax.ShapeDtypeStruct((8, W), F32)],
        compiler_params=_cp(("arbitrary",)))(conv, cw, dmixed)


def _place():
    x, y, c = lax.axis_index("x"), lax.axis_index("y"), lax.axis_index("c")
    return x, y, c


def _chips_of(x, y):
    return [(1 - x, y), (x, 1 - y), (1 - x, 1 - y)]


def _dev(p):
    return 4 * p[0] + 2 * p[1] + p[2]


def _gather_job_a(shards):
    n = len(shards)

    def peers(x, y, c):
        return [(x, y, 1 - c)] + [(*chip, c) for chip in _chips_of(x, y)]

    def start(ins, outs, sems):
        send, recv, loc = sems
        x, y, c = _place()
        me = (x, y, c)
        cps = []
        for a in range(n):
            cps.append(pltpu.make_async_copy(ins[a], outs[a].at[_dev(me)], loc.at[a]))
            for k, peer in enumerate(peers(x, y, c)):
                cps.append(pltpu.make_async_remote_copy(
                    src_ref=ins[a], dst_ref=outs[a].at[_dev(me)], send_sem=send.at[a, k], recv_sem=recv.at[a, k],
                    device_id=peer, device_id_type=MESH))
        for cp in cps:
            cp.start()
        return cps

    def finish(cps, ins, outs, sems):
        send, recv, loc = sems
        x, y, c = _place()
        for a in range(n):
            for k, peer in enumerate(peers(x, y, c)):
                pltpu.make_async_remote_copy(
                    src_ref=ins[a], dst_ref=outs[a].at[_dev(peer)], send_sem=send.at[a, k], recv_sem=recv.at[a, k],
                    device_id=(x, y, c), device_id_type=MESH).wait_recv()
        for a in range(n):
            cps[5 * a].wait()
            for k in range(4):
                cps[5 * a + 1 + k].wait_send()

    return _Job(shards, [jax.ShapeDtypeStruct((NDEV,) + s.shape, s.dtype) for s in shards], {},
                [pltpu.SemaphoreType.DMA((n, 4)), pltpu.SemaphoreType.DMA((n, 4)), pltpu.SemaphoreType.DMA((n,))],
                start, finish)


def _gather_job_b(gathered):
    n = len(gathered)

    def start(ins, outs, sems):
        send, recv = sems
        x, y, c = _place()
        cps = []
        for a in range(n):
            for j, chip in enumerate(_chips_of(x, y)):
                blk = outs[a].at[_dev((*chip, c))]
                cps.append(pltpu.make_async_remote_copy(
                    src_ref=blk, dst_ref=blk, send_sem=send.at[a, j], recv_sem=recv.at[a, j],
                    device_id=(x, y, 1 - c), device_id_type=MESH))
        for cp in cps:
            cp.start()
        return cps

    def finish(cps, ins, outs, sems):
        send, recv = sems
        x, y, c = _place()
        for a in range(n):
            for j, chip in enumerate(_chips_of(x, y)):
                blk = outs[a].at[_dev((*chip, 1 - c))]
                pltpu.make_async_remote_copy(
                    src_ref=blk, dst_ref=blk, send_sem=send.at[a, j], recv_sem=recv.at[a, j],
                    device_id=(x, y, c), device_id_type=MESH).wait_recv()
        for cp in cps:
            cp.wait_send()

    return _Job(gathered, [jax.ShapeDtypeStruct(g.shape, g.dtype) for g in gathered], {a: a for a in range(n)},
                [pltpu.SemaphoreType.DMA((n, 3)), pltpu.SemaphoreType.DMA((n, 3))], start, finish)


def _sibling_job(grads):
    n = len(grads)

    def start(ins, outs, sems):
        send, recv = sems
        x, y, c = _place()
        cps = [pltpu.make_async_remote_copy(
            src_ref=ins[a].at[:, 1 - c], dst_ref=outs[a], send_sem=send.at[a], recv_sem=recv.at[a],
            device_id=(x, y, 1 - c), device_id_type=MESH) for a in range(n)]
        for cp in cps:
            cp.start()
        return cps

    def finish(cps, ins, outs, sems):
        for cp in cps:
            cp.wait()

    return _Job(grads, [jax.ShapeDtypeStruct(g.shape[:1] + g.shape[2:], F32) for g in grads], {},
                [pltpu.SemaphoreType.DMA((n,)), pltpu.SemaphoreType.DMA((n,))], start, finish)


def _chip_job(psums):
    n = len(psums)

    def copies(ins, outs, sems):
        send, recv, loc = sems
        x, y, c = _place()
        mychip = 2 * x + y
        cps = []
        for a in range(n):
            cps.append(pltpu.make_async_copy(ins[a].at[mychip], outs[a].at[mychip], loc.at[a]))
            for j, chip in enumerate(_chips_of(x, y)):
                cps.append(pltpu.make_async_remote_copy(
                    src_ref=ins[a].at[2 * chip[0] + chip[1]], dst_ref=outs[a].at[mychip],
                    send_sem=send.at[a, j], recv_sem=recv.at[a, j], device_id=(*chip, c), device_id_type=MESH))
        return cps

    def start(ins, outs, sems):
        for cp in copies(ins, outs, sems):
            cp.start()

    def finish(_, ins, outs, sems):
        cps = copies(ins, outs, sems)
        send, recv, loc = sems
        x, y, c = _place()
        mychip = 2 * x + y
        for a in range(n):
            for j, chip in enumerate(_chips_of(x, y)):
                pltpu.make_async_remote_copy(
                    src_ref=ins[a].at[mychip], dst_ref=outs[a].at[2 * chip[0] + chip[1]],
                    send_sem=send.at[a, j], recv_sem=recv.at[a, j], device_id=(x, y, c), device_id_type=MESH).wait_recv()
        for a in range(n):
            cps[4 * a].wait()
            for j in range(3):
                cps[4 * a + 1 + j].wait_send()

    return _Job(psums, [jax.ShapeDtypeStruct(p.shape, BF16) for p in psums], {},
                [pltpu.SemaphoreType.DMA((n, 3)), pltpu.SemaphoreType.DMA((n, 3)), pltpu.SemaphoreType.DMA((n,))],
                start, finish)


def _join_jobs(*jobs):
    jobs = [j for j in jobs if j is not None]
    if len(jobs) <= 1:
        return jobs[0] if jobs else None
    cut = lambda seq, sizes: [seq[sum(sizes[:k]):sum(sizes[:k + 1])] for k in range(len(sizes))]
    n_in = [len(j.ins) for j in jobs]
    n_out = [len(j.out_shapes) for j in jobs]
    n_sem = [len(j.sems) for j in jobs]
    aliases = {}
    for k, j in enumerate(jobs):
        for a, b in j.aliases.items():
            aliases[sum(n_in[:k]) + a] = sum(n_out[:k]) + b

    def start(ins, outs, sems):
        return [j.start(i, o, s) for j, i, o, s in zip(jobs, cut(ins, n_in), cut(outs, n_out), cut(sems, n_sem))]

    def finish(sts, ins, outs, sems):
        for j, st, i, o, s in zip(jobs, sts, cut(ins, n_in), cut(outs, n_out), cut(sems, n_sem)):
            j.finish(st, i, o, s)

    return _Job([t for j in jobs for t in j.ins], [t for j in jobs for t in j.out_shapes], aliases,
                [t for j in jobs for t in j.sems], start, finish)


def _run_job(job, name):
    def body(ins, outs, scr, comm):
        comm[1](comm[0]())

    return _host_call(body, name, [], [], [], [], [], {}, job)[1]


def _allreduce_small(v):
    def body(v_ref, o_ref, slots, send_sems, recv_sems):
        x, y, c = _place()
        me = 4 * x + 2 * y + c
        slots[me] = v_ref[...]

        def copy(k):
            peer = (x ^ ((k >> 2) & 1), y ^ ((k >> 1) & 1), c ^ (k & 1))
            return pltpu.make_async_remote_copy(
                src_ref=v_ref, dst_ref=slots.at[me], send_sem=send_sems.at[k - 1], recv_sem=recv_sems.at[k - 1],
                device_id=peer, device_id_type=MESH)

        def arrival(k):
            return pltpu.make_async_remote_copy(
                src_ref=v_ref, dst_ref=slots.at[me ^ k], send_sem=send_sems.at[k - 1], recv_sem=recv_sems.at[k - 1],
                device_id=(x, y, c), device_id_type=MESH)

        sends = [copy(k) for k in range(1, NDEV)]
        for cp in sends:
            cp.start()
        for k in range(1, NDEV):
            arrival(k).wait_recv()
        for cp in sends:
            cp.wait_send()
        acc = slots[0]
        for d in range(1, NDEV):
            acc = acc + slots[d]
        o_ref[...] = acc

    return pl.pallas_call(
        body, name="allreduce_small", in_specs=[VMEM_SPEC], out_specs=VMEM_SPEC,
        out_shape=jax.ShapeDtypeStruct(v.shape, F32),
        scratch_shapes=[pltpu.VMEM((NDEV,) + v.shape, F32), pltpu.SemaphoreType.DMA((NDEV - 1,)),
                        pltpu.SemaphoreType.DMA((NDEV - 1,))],
        )(v)


def _pair_sums(views, gots, core):
    n = len(views)

    def body(c_ref, *refs):
        for a in range(n):
            refs[2 * n + a][...] = (refs[a][...] + refs[n + a][...]).astype(BF16)

    def vspec(v):
        return pl.BlockSpec((None, None) + v.shape[2:], lambda k, c: (k, c[0], 0, 0))

    def gspec(g):
        return pl.BlockSpec((None,) + g.shape[1:], lambda k, c: (k, 0, 0))

    return pl.pallas_call(
        body, name="pair_sums",
        grid_spec=pltpu.PrefetchScalarGridSpec(
            num_scalar_prefetch=1, grid=(4,),
            in_specs=[vspec(v) for v in views] + [gspec(g) for g in gots],
            out_specs=[gspec(g) for g in gots]),
        out_shape=[jax.ShapeDtypeStruct(g.shape, BF16) for g in gots],
        compiler_params=_cp(("parallel",)))(core, *views, *gots)


def _chip_sums(parts):
    n = len(parts)

    def body(*refs):
        for a in range(n):
            acc = refs[a][0].astype(F32)
            for k in range(1, 4):
                acc = acc + refs[a][k].astype(F32)
            refs[n + a][...] = acc

    return pl.pallas_call(
        body, name="chip_sums", in_specs=[VMEM_SPEC] * n, out_specs=[VMEM_SPEC] * n,
        out_shape=[jax.ShapeDtypeStruct(p.shape[1:], F32) for p in parts], compiler_params=_cp())(*parts)


def _permute_in(w):
    lead = w.shape[:-1]
    return w.reshape(lead + (3, 3, 2, BQ)).swapaxes(-2, -3).reshape(lead + (QKVW,))


def _unpermute_in(w):
    lead = w.shape[:-1]
    return w.reshape(lead + (3, 2, 3, BQ)).swapaxes(-2, -3).reshape(lead + (QKVW,))


def _row(v):
    v = v.reshape(-1)
    return jnp.pad(v, (0, D - v.shape[0])).reshape(1, D)


def kernel(x, w_in, f_bias, conv_w, w_out, rel_bias, ln1_g, ln1_b, w_gate, w_up, w_down, ln2_g, ln2_b, loss_target, m_w_in, m_f_bias, m_conv_w, m_w_out, m_rel_bias, m_ln1_g, m_ln1_b, m_w_gate, m_w_up, m_w_down, m_ln2_g, m_ln2_b, v_w_in, v_f_bias, v_conv_w, v_w_out, v_rel_bias, v_ln1_g, v_ln1_b, v_w_gate, v_w_up, v_w_down, v_ln2_g, v_ln2_b):
    xi, yi, ci = _place()
    me = 4 * xi + 2 * yi + ci
    core = jnp.reshape(ci, (1,)).astype(jnp.int32)

    win_s = jnp.concatenate([_permute_in(w_in[..., :QKVW]), w_in[..., QKVW:]], axis=-1)
    win_s = jnp.pad(win_s, ((0, 0), (0, 0), (0, NPAD - NPROJ))).astype(BF16)
    per_layer = [win_s, w_out.astype(BF16), jnp.swapaxes(w_gate, 1, 2).astype(BF16),
                 jnp.swapaxes(w_up, 1, 2).astype(BF16), w_down.astype(BF16)]
    sh = [[s[l] for s in per_layer] for l in range(2)]

    def whole(g):
        return g.reshape(NDEV * g.shape[1], g.shape[2])

    first = _run_job(_gather_job_b(_run_job(_gather_job_a(sh[0][:1]), "gather_a")), "gather_b")
    W = [{"win": whole(first[0])}, {}]

    cw_rows = lax.dynamic_update_slice(jnp.zeros((2, 3, 256), F32), conv_w, (0, 0, me * 32))
    small = jnp.concatenate([_row(cw_rows[0]), _row(cw_rows[1]), jnp.zeros((SMALL_ROWS - 2, D), F32)], axis=0)
    small = _allreduce_small(small)
    cw_full = small[0:2, :CONVW].reshape(2, 3, 256)
    cw8 = jnp.pad(cw_full, ((0, 0), (0, 5), (0, 0)))
    fb = jnp.pad(f_bias, ((0, 0), (0, GATEW - NH))).reshape(2, 1, GATEW)
    tbl = _dil_table(rel_bias)

    def wcol(K, tn, off):
        return pl.BlockSpec((K, tn), lambda i, j: (0, off + j))

    def wrow(tn, K, blk=0):
        return pl.BlockSpec((tn, K), lambda i, j: (j, blk))

    def arow(tm, K, blk=0):
        return pl.BlockSpec((tm, K), lambda i, j: (i, blk))

    h = x.reshape(T, D)
    hb = h.astype(BF16)
    saved = []
    for l in range(2):
        Win = W[l]["win"]
        qkv = _mm([(hb, arow(1024, D), Win, wcol(D, 768, 0))], nt=False, M=T, N=QKVW, tm=1024, tn=768,
                  out_dtype=BF16, name="proj_qkv")
        conv = _mm([(hb, arow(512, D), Win, wcol(D, 768, 3))], nt=False, M=T, N=CONVW, tm=512, tn=768,
                   out_dtype=F32, name="proj_conv")
        gate = _mm([(hb, arow(512, D), Win, wcol(D, 128, 24))], nt=False, M=T, N=GATEW, tm=512, tn=128,
                   out_dtype=F32, name="proj_gate")
        cum = _fox_prep(gate, fb[l])
        cq = cum[:, :NH].reshape(BL, S, NH).transpose(0, 2, 1).reshape(NSTAT, S)
        ckb = jnp.broadcast_to(cq[:, :, None], (NSTAT, S, BQ))
        if l == 0:
            mixed, rtot, a0 = _sb_fwd(qkv, job=_gather_job_a(sh[0][1:]))
            mixed, lse_d, ex = _flash_fwd(qkv, mixed, 1, False, (tbl,),
                                          job=_join_jobs(_gather_job_b(list(a0)), _gather_job_a(sh[1][:2])))
            W[0].update(zip(("wout", "wgT", "wuT", "wd"), [whole(t) for t in ex[:4]]))
            mixed, lse_f, o_fox, ex = _flash_fwd(qkv, mixed, 2, True, (cq, ckb),
                                                 job=_join_jobs(_gather_job_b(list(ex[4:])), _gather_job_a(sh[1][2:])))
            W[1].update(zip(("win", "wout"), [whole(t) for t in ex[:2]]))
            a2 = list(ex[2:])
        else:
            mixed, rtot, ex = _sb_fwd(qkv, job=_gather_job_b(a2))
            W[1].update(zip(("wgT", "wuT", "wd"), [whole(t) for t in ex]))
            mixed, lse_d, _ = _flash_fwd(qkv, mixed, 1, False, (tbl,))
            mixed, lse_f, o_fox, _ = _flash_fwd(qkv, mixed, 2, True, (cq, ckb))
        Wout, WgT, WuT, Wd = W[l]["wout"], W[l]["wgT"], W[l]["wuT"], W[l]["wd"]
        mixed = _conv_fwd(conv, cw8[l], mixed)
        x1, xh1, r1, x1b = _mm_ln(mixed, Wout, h, ln1_g[l:l + 1], ln1_b[l:l + 1], "out_proj_ln")
        fs, ft, a, x2, xh2, r2, x2b = _ffn_fwd(x1b, x1, WgT, WuT, Wd, ln2_g[l:l + 1], ln2_b[l:l + 1])
        saved.append(dict(h=hb, qkv=qkv, conv=conv, gate=gate, cq=cq, ckb=ckb, mixed=mixed, rtot=rtot, lse_d=lse_d,
                          lse_f=lse_f, o_fox=o_fox, x1=x1b, xh1=xh1, r1=r1, fs=fs, ft=ft, a=a, xh2=xh2, r2=r2))
        h, hb = x2, x2b

    dy = h

    def view(gr):
        return gr.reshape(4, 2, gr.shape[0] // NDEV, gr.shape[1])

    G = [None, None]
    small_g = {}
    shard_g = {}
    for l in (1, 0):
        sv = saved[l]
        Win, Wout, WgT, WuT, Wd = W[l]["win"], W[l]["wout"], W[l]["wgT"], W[l]["wuT"], W[l]["wd"]
        res = _ffn_bwd(dy, sv["xh2"], sv["r2"], ln2_g[l:l + 1], sv["fs"], sv["ft"], Wd, WgT, WuT,
                       target=loss_target.reshape(T, D) if l == 1 else None)
        dgt, dut, ds2b, dx1, dg2, db2 = res[:6]
        if l == 1:
            sq = res[6]
        G_d = _mm_tn(sv["a"], ds2b, None, C=D, Ka=DFF, N=D, tm=256, tn=1024, tk=T, ooff=0, name="grad_w_down")
        G_g = _mm_tn(dgt, sv["x1"], None, C=D, Ka=DFF, N=D, tm=256, tn=1024, tk=T, ooff=0, name="grad_w_gate")
        G_u = _mm_tn(dut, sv["x1"], None, C=D, Ka=DFF, N=D, tm=256, tn=1024, tk=T, ooff=0, name="grad_w_up")
        ds1, dg1, db1, ds1b = _ln_bwd(dx1, sv["xh1"], sv["r1"], ln1_g[l:l + 1])
        G_out = _mm_tn(sv["mixed"], ds1b, None, C=D, Ka=D, N=D, tm=256, tn=1024, tk=T, ooff=0, name="grad_w_out")
        dmixed = _mm([(ds1b, arow(512, D), Wout, wrow(512, D))], nt=True, M=T, N=D, tm=512, tn=512,
                     out_dtype=BF16, name="out_proj_dx")
        early = [view(t) for t in (G_g, G_u, G_d, G_out)] + ([view(G[1]["in"])] if l == 0 else [])
        dqkv, gots = _sb_bwd(sv["qkv"], dmixed, sv["rtot"], job=_sibling_job(early))
        ps = _pair_sums(early, list(gots), core)
        dqkv, dtbl, pa = _flash_bwd(sv["qkv"], sv["mixed"], dmixed, sv["lse_d"], dqkv, 1, False, (tbl,),
                                    job=_chip_job(ps[:2]))
        dqkv, dck, pb = _flash_bwd(sv["qkv"], sv["o_fox"], dmixed, sv["lse_f"], dqkv, 2, True,
                                   (sv["cq"], sv["ckb"]), job=_chip_job(ps[2:]))
        sums = _chip_sums(list(pa) + list(pb))
        shard_g[l] = dict(zip(("g", "u", "d", "out"), sums[:4]))
        if l == 0:
            shard_g[1]["in"] = sums[4]
        dconv, dcw = _conv_bwd(sv["conv"], cw8[l], dmixed)
        dcum = jnp.pad(dck.reshape(S, BL, NH).transpose(1, 0, 2).reshape(T, NH), ((0, 0), (0, GATEW - NH)))
        dgate, dfb = _fox_post(dcum, sv["gate"], fb[l])
        drb = _dil_table_bwd(dtbl)
        G_in = _mm_tn(sv["h"], dqkv, None, C=NPAD, Ka=D, N=QKVW, tm=512, tn=768, tk=T, ooff=0, name="grad_w_in_qkv")
        G_in = _mm_tn(sv["h"], dconv, G_in, C=NPAD, Ka=D, N=CONVW, tm=256, tn=768, tk=T, ooff=3,
                      name="grad_w_in_conv")
        G_in = _mm_tn(sv["h"], dgate, G_in, C=NPAD, Ka=D, N=GATEW, tm=1024, tn=128, tk=1024, ooff=24,
                      name="grad_w_in_gate")
        G[l] = {"in": G_in, "out": G_out, "g": G_g, "u": G_u, "d": G_d}
        tail = None
        if l == 0:
            late = [view(G_in)]
            tail = _chip_job(_pair_sums(late, list(_run_job(_sibling_job(late), "sibling_exchange")), core))
        dy = _mm([(dqkv, arow(1024, QKVW), Win, wrow(512, QKVW, 0)),
                  (dconv, arow(1024, CONVW), Win, wrow(512, CONVW, 3)),
                  (dgate, arow(1024, GATEW), Win, wrow(512, GATEW, 24))],
                 nt=True, M=T, N=D, tm=1024, tn=512, out_dtype=F32, name="proj_dx", res=ds1, res_scale=ALPHA, job=tail)
        if l == 0:
            dy, parts = dy
            shard_g[0]["in"] = _chip_sums(list(parts))[0]
        small_g[l] = dict(ln1_g=dg1, ln1_b=db1, ln2_g=dg2, ln2_b=db2, cw=dcw[0:3].reshape(1, CONVW),
                          fb=dfb[:, :NH], rb=drb[:, :NH])
    grad_x = dy.reshape(BL, S, D)

    rows = []
    for name in ("ln1_g", "ln1_b", "ln2_g", "ln2_b"):
        rows += [small_g[0][name], small_g[1][name]]
    rows += [_row(small_g[0]["cw"]), _row(small_g[1]["cw"]),
             _row(jnp.concatenate([small_g[0]["fb"], small_g[1]["fb"]], axis=0)),
             _row(small_g[0]["rb"] + small_g[1]["rb"]), _row(sq)]
    rows.append(jnp.zeros((SMALL_ROWS - len(rows), D), F32))
    sg = _allreduce_small(jnp.concatenate(rows, axis=0))
    loss = sg[12, 0] * (0.5 / D)
    g_ln1_g, g_ln1_b, g_ln2_g, g_ln2_b = sg[0:2], sg[2:4], sg[4:6], sg[6:8]
    g_conv_full = sg[8:10, :CONVW].reshape(2, 3, 256)
    g_conv = lax.dynamic_slice(g_conv_full, (0, 0, me * 32), (2, 3, 32))
    g_fb = sg[10, :2 * NH].reshape(2, NH)
    g_rb = sg[11, :32 * NH].reshape(32, NH)

    def both(name):
        return jnp.stack([shard_g[0][name], shard_g[1][name]])

    g_in = both("in")
    g_w_in = jnp.concatenate([_unpermute_in(g_in[..., :QKVW]), g_in[..., QKVW:NPROJ]], axis=-1)
    g_w_out = both("out")
    g_w_gate = jnp.swapaxes(both("g"), 1, 2)
    g_w_up = jnp.swapaxes(both("u"), 1, 2)
    g_w_down = both("d")

    up_in = _adamw(w_in, g_w_in, m_w_in, v_w_in, 64)
    up_out = _adamw(w_out, g_w_out, m_w_out, v_w_out, 128)
    up_gate = _adamw(w_gate, g_w_gate, m_w_gate, v_w_gate, 256)
    up_up = _adamw(w_up, g_w_up, m_w_up, v_w_up, 256)
    up_down = _adamw(w_down, g_w_down, m_w_down, v_w_down, 352)

    def pack(fbv, cwv, rbv, l1g, l1b, l2g, l2b):
        r = [l1g, l1b, l2g, l2b, _row(cwv), _row(fbv), _row(rbv)]
        r.append(jnp.zeros((SMALL_ROWS - 11, D), F32))
        return jnp.concatenate(r, axis=0)

    pw = pack(f_bias, conv_w, rel_bias, ln1_g, ln1_b, ln2_g, ln2_b)
    pg = pack(g_fb, g_conv, g_rb, g_ln1_g, g_ln1_b, g_ln2_g, g_ln2_b)
    pm = pack(m_f_bias, m_conv_w, m_rel_bias, m_ln1_g, m_ln1_b, m_ln2_g, m_ln2_b)
    pv = pack(v_f_bias, v_conv_w, v_rel_bias, v_ln1_g, v_ln1_b, v_ln2_g, v_ln2_b)
    ups = [u[0] for u in _adamw(pw[None], pg[None], pm[None], pv[None], SMALL_ROWS)]

    def unpack(p):
        return dict(ln1_g=p[0:2], ln1_b=p[2:4], ln2_g=p[4:6], ln2_b=p[6:8],
                    conv_w=p[8, :192].reshape(2, 3, 32), f_bias=p[9, :2 * NH].reshape(2, NH),
                    rel_bias=p[10, :32 * NH].reshape(32, NH))

    sm = [unpack(p) for p in ups]

    def group(k):
        return (up_in[k], sm[k]["f_bias"], sm[k]["conv_w"], up_out[k], sm[k]["rel_bias"], sm[k]["ln1_g"],
                sm[k]["ln1_b"], up_gate[k], up_up[k], up_down[k], sm[k]["ln2_g"], sm[k]["ln2_b"])

    grads = (g_w_in, g_fb, g_conv, g_w_out, g_rb, g_ln1_g, g_ln1_b, g_w_gate, g_w_up, g_w_down, g_ln2_g, g_ln2_b)
    return (loss, grad_x) + grads + group(0) + group(1) + group(2)
```

```python
import math

import numpy as np
import jax
import jax.numpy as jnp
from jax import lax
from jax.experimental import pallas as pl
from jax.experimental.pallas import tpu as pltpu

F32 = jnp.float32
BF16 = jnp.bfloat16
MESH = pl.DeviceIdType.MESH

D = 1024
S = 2048
BL = 2
T = BL * S
NH = 4
DFF = 2816
NPROJ = 3076
NPAD = 3200
QKVW = 2304
CONVW = 768
GATEW = 128
PAIRW = 384
BQ = 128
HB = 2 * BQ
NB = S // BQ
NDEV = 8
NSTAT = BL * NH
ALPHA = 4.0 ** 0.25
SCALE = 0.125
NEG = -1e30
LN_EPS = 1e-5
ADAM_LR, ADAM_B1, ADAM_B2, ADAM_EPS, ADAM_WD, ADAM_STEP = 0.001, 0.9, 0.999, 1e-08, 0.01, 10
VMEM_LIMIT = 56 * 1024 * 1024
SMALL_ROWS = 16


def _bucket_thresholds():
    d = np.arange(0, S)
    nf = np.maximum(d, 1).astype(np.float32)
    large = 16 + (np.log(nf / np.float32(16)) / np.float32(math.log(128)) * np.float32(16)).astype(np.int32)
    b = np.where(d < 16, d, np.minimum(large, 31))
    return [int(np.argmax(b >= k)) for k in range(32)]


BUCKET_TH = _bucket_thresholds()


def _cp(sem=None, vmem=VMEM_LIMIT):
    return pltpu.CompilerParams(dimension_semantics=sem, vmem_limit_bytes=vmem)


def _dot(a, b):
    return lax.dot_general(a, b, (((1,), (0,)), ((), ())), preferred_element_type=F32)


def _dot_nt(a, b):
    return lax.dot_general(a, b, (((1,), (1,)), ((), ())), preferred_element_type=F32)


def _dot_tn(a, b):
    return lax.dot_general(a, b, (((0,), (0,)), ((), ())), preferred_element_type=F32)


def _split2(x):
    hi = x.astype(BF16)
    mid = (x - hi.astype(F32)).astype(BF16)
    return jnp.concatenate([hi, mid], axis=1)


def _split3(x):
    hi = x.astype(BF16)
    r = x - hi.astype(F32)
    mid = r.astype(BF16)
    lo = (r - mid.astype(F32)).astype(BF16)
    return jnp.concatenate([hi, mid, lo], axis=1)


def _log_sigmoid(u):
    return jnp.minimum(u, 0.0) - jnp.log1p(jnp.exp(-jnp.abs(u)))


def _log_sigmoid_tile(u):
    return jnp.minimum(u, 0.0) - jnp.log(1.0 + jnp.exp(jnp.minimum(u, -u)))


def _iota(shape, dim):
    return lax.broadcasted_iota(jnp.int32, shape, dim)


ANY_SPEC = pl.BlockSpec(memory_space=pl.ANY)
VMEM_SPEC = pl.BlockSpec(memory_space=pltpu.VMEM)


def _mm(pairs, *, nt, M, N, tm, tn, out_dtype, name, res=None, res_scale=1.0, job=None):
    n = len(pairs)
    n_in = 2 * n + (res is not None)
    jins = job.ins if job else []
    jouts = job.out_shapes if job else []
    gi, gj = M // tm, N // tn

    def body(*refs):
        o_ref = refs[n_in + len(jins)]
        if job:
            jrefs = (refs[n_in:n_in + len(jins)], refs[n_in + len(jins) + 1:n_in + len(jins) + 1 + len(jouts)],
                     refs[n_in + len(jins) + 1 + len(jouts):])

            @pl.when((pl.program_id(0) == 0) & (pl.program_id(1) == 0))
            def _():
                job.start(*jrefs)

        acc = None
        for p in range(n):
            a = refs[2 * p][...].astype(BF16)
            b = refs[2 * p + 1][...]
            d = _dot_nt(a, b) if nt else _dot(a, b)
            acc = d if acc is None else acc + d
        if res is not None:
            acc = acc + res_scale * refs[2 * n][...]
        o_ref[...] = acc.astype(out_dtype)
        if job:
            @pl.when((pl.program_id(0) == gi - 1) & (pl.program_id(1) == gj - 1))
            def _():
                job.finish(None, *jrefs)

    ops, specs = [], []
    for a, asp, b, bsp in pairs:
        ops += [a, b]
        specs += [asp, bsp]
    if res is not None:
        ops.append(res)
        specs.append(pl.BlockSpec((tm, tn), lambda i, j: (i, j)))
    out = pl.pallas_call(
        body, name=name, grid=(gi, gj), in_specs=specs + [ANY_SPEC] * len(jins),
        out_specs=[pl.BlockSpec((tm, tn), lambda i, j: (i, j))] + [ANY_SPEC] * len(jouts),
        out_shape=[jax.ShapeDtypeStruct((M, N), out_dtype)] + list(jouts),
        scratch_shapes=list(job.sems) if job else [],
        input_output_aliases={n_in + a: 1 + b for a, b in job.aliases.items()} if job else {},
        compiler_params=_cp(("arbitrary", "arbitrary") if job else ("parallel", "parallel")))(*ops, *jins)
    return (out[0], out[1:]) if job else out[0]


def _mm_tn(a, b, gbuf, *, C, Ka, N, tm, tn, tk, ooff, name):
    def body(*refs):
        a_ref, b_ref, o_ref = refs[0], refs[1], refs[-1]
        k = pl.program_id(2)
        d = _dot_tn(a_ref[...].astype(BF16), b_ref[...].astype(BF16))

        @pl.when(k == 0)
        def _():
            o_ref[...] = d

        @pl.when(k > 0)
        def _():
            o_ref[...] += d

    ops = [a, b] + ([] if gbuf is None else [gbuf])
    return pl.pallas_call(
        body, name=name, grid=(Ka // tm, N // tn, T // tk),
        in_specs=[pl.BlockSpec((tk, tm), lambda i, j, k: (k, i)),
                  pl.BlockSpec((tk, tn), lambda i, j, k: (k, j))] + ([] if gbuf is None else [ANY_SPEC]),
        out_specs=pl.BlockSpec((tm, tn), lambda i, j, k: (i, ooff + j)),
        out_shape=jax.ShapeDtypeStruct((Ka, C), F32),
        input_output_aliases={} if gbuf is None else {2: 0},
        compiler_params=_cp(("parallel", "parallel", "arbitrary")))(*ops)


def _ffn_up(x1, wgt, wut):
    tm, tn = 1024, 256

    def body(x_ref, wg_ref, wu_ref, g_ref, u_ref, a_ref):
        ch = 256
        for r in range(0, tm, ch):
            xb = x_ref[r:r + ch, :]
            g = _dot_nt(xb, wg_ref[...])
            u = _dot_nt(xb, wu_ref[...])
            g_ref[r:r + ch, :] = g.astype(BF16)
            u_ref[r:r + ch, :] = u.astype(BF16)
            a_ref[r:r + ch, :] = (g * jax.nn.sigmoid(g) * u).astype(BF16)

    wspec = pl.BlockSpec((tn, D), lambda i, j: (j, 0))
    ospec = pl.BlockSpec((tm, tn), lambda i, j: (i, j))
    return pl.pallas_call(
        body, name="ffn_up", grid=(T // tm, DFF // tn),
        in_specs=[pl.BlockSpec((tm, D), lambda i, j: (i, 0)), wspec, wspec],
        out_specs=[ospec, ospec, ospec],
        out_shape=[jax.ShapeDtypeStruct((T, DFF), BF16)] * 3,
        compiler_params=_cp(("parallel", "parallel")))(x1, wgt, wut)


def _ffn_da(dffn, wd, s, t):
    tm, tn = 1024, 256

    def body(d_ref, wd_ref, s_ref, t_ref, dg_ref, du_ref):
        ch = 256
        for r in range(0, tm, ch):
            da = _dot_nt(d_ref[r:r + ch, :], wd_ref[...])
            gv = s_ref[r:r + ch, :].astype(F32)
            sg = jax.nn.sigmoid(gv)
            dg_ref[r:r + ch, :] = (da * t_ref[r:r + ch, :].astype(F32) * (sg * (1.0 + gv * (1.0 - sg)))).astype(BF16)
            du_ref[r:r + ch, :] = (da * (gv * sg)).astype(BF16)

    ospec = pl.BlockSpec((tm, tn), lambda i, j: (i, j))
    return pl.pallas_call(
        body, name="ffn_da", grid=(T // tm, DFF // tn),
        in_specs=[pl.BlockSpec((tm, D), lambda i, j: (i, 0)),
                  pl.BlockSpec((tn, D), lambda i, j: (j, 0)), ospec, ospec],
        out_specs=[ospec, ospec],
        out_shape=[jax.ShapeDtypeStruct((T, DFF), BF16), jax.ShapeDtypeStruct((T, DFF), BF16)],
        compiler_params=_cp(("parallel", "parallel")))(dffn, wd, s, t)


def _proj(xb, w):
    tm = 512

    def body(x_ref, w_ref, qkv_ref, conv_ref, gate_ref):
        xv = x_ref[...]
        qkv_ref[...] = _dot(xv, w_ref[:, 0:QKVW]).astype(BF16)
        conv_ref[...] = _dot(xv, w_ref[:, QKVW:QKVW + CONVW])
        gate_ref[...] = _dot(xv, w_ref[:, QKVW + CONVW:NPAD])

    def rows(n):
        return pl.BlockSpec((tm, n), lambda i: (i, 0))

    return pl.pallas_call(
        body, name="proj", grid=(T // tm,),
        in_specs=[rows(D), pl.BlockSpec((D, NPAD), lambda i: (0, 0))],
        out_specs=[rows(QKVW), rows(CONVW), rows(GATEW)],
        out_shape=[jax.ShapeDtypeStruct((T, QKVW), BF16), jax.ShapeDtypeStruct((T, CONVW), F32),
                   jax.ShapeDtypeStruct((T, GATEW), F32)],
        compiler_params=_cp(("parallel",)))(xb, w)


def _ffn_fwd(xb, x, wgt, wut, wd, gam, bet):
    tm, ch = 512, 256

    def body(xb_ref, x_ref, g_ref, b_ref, wg_hbm, wu_hbm, wd_hbm,
             go_ref, uo_ref, ao_ref, y_ref, xh_ref, r_ref, yb_ref, wg_v, wu_v, wd_v, sem):
        @pl.when(pl.program_id(0) == 0)
        def _():
            _copy_in(wg_hbm, wg_v, sem)
            _copy_in(wu_hbm, wu_v, sem)
            _copy_in(wd_hbm, wd_v, sem)

        xv = xb_ref[...]
        for c in range(0, DFF, ch):
            gv = _dot_nt(xv, wg_v[c:c + ch, :])
            uv = _dot_nt(xv, wu_v[c:c + ch, :])
            go_ref[:, c:c + ch] = gv.astype(BF16)
            uo_ref[:, c:c + ch] = uv.astype(BF16)
            ao_ref[:, c:c + ch] = (gv * jax.nn.sigmoid(gv) * uv).astype(BF16)
        s = ALPHA * x_ref[...] + _dot(ao_ref[...], wd_v[...])
        mu = jnp.mean(s, axis=-1, keepdims=True)
        xc = s - mu
        var = jnp.mean(xc * xc, axis=-1, keepdims=True)
        r = lax.rsqrt(var + LN_EPS)
        xh = xc * r
        xh_ref[...] = xh.astype(BF16)
        r_ref[...] = r
        y = xh * g_ref[...] + b_ref[...]
        y_ref[...] = y
        yb_ref[...] = y.astype(BF16)

    row = pl.BlockSpec((tm, D), lambda i: (i, 0))
    wide = pl.BlockSpec((tm, DFF), lambda i: (i, 0))
    vec = pl.BlockSpec((1, D), lambda i: (0, 0))
    wsl = pltpu.VMEM((DFF, D), BF16)
    hid = jax.ShapeDtypeStruct((T, DFF), BF16)
    return pl.pallas_call(
        body, name="ffn_fwd", grid=(T // tm,),
        in_specs=[row, row, vec, vec, ANY_SPEC, ANY_SPEC, ANY_SPEC],
        out_specs=[wide, wide, wide, row, row, pl.BlockSpec((tm, 1), lambda i: (i, 0)), row],
        out_shape=[hid, hid, hid, jax.ShapeDtypeStruct((T, D), F32), jax.ShapeDtypeStruct((T, D), BF16),
                   jax.ShapeDtypeStruct((T, 1), F32), jax.ShapeDtypeStruct((T, D), BF16)],
        scratch_shapes=[wsl, wsl, wsl, pltpu.SemaphoreType.DMA],
        compiler_params=_cp(("arbitrary",)))(xb, x, gam, bet, wgt, wut, wd)


def _ffn_bwd(dy, xh, r, gam, g, u, wd, wgt, wut, target=None):
    tm, ch = 256, 256

    def body(*refs):
        if target is None:
            (dy_ref, xh_ref, r_ref, gam_ref, g_ref, u_ref, wd_hbm, wg_hbm, wu_hbm,
             dg_ref, du_ref, dsb_ref, dx_ref, dgam_ref, dbet_ref, wd_v, wg_v, wu_v, sem) = refs
        else:
            (dy_ref, t_ref, xh_ref, r_ref, gam_ref, g_ref, u_ref, wd_hbm, wg_hbm, wu_hbm,
             dg_ref, du_ref, dsb_ref, dx_ref, dgam_ref, dbet_ref, sq_ref, wd_v, wg_v, wu_v, sem) = refs
        @pl.when(pl.program_id(0) == 0)
        def _():
            _copy_in(wd_hbm, wd_v, sem)
            _copy_in(wg_hbm, wg_v, sem)
            _copy_in(wu_hbm, wu_v, sem)

        if target is None:
            dyv = dy_ref[...]
        else:
            e = dy_ref[...] - t_ref[...]
            dyv = e * (1.0 / D)
            p = jnp.sum(jnp.sum(e * e, axis=1, keepdims=True), axis=0, keepdims=True)

            @pl.when(pl.program_id(0) == 0)
            def _():
                sq_ref[...] = p

            @pl.when(pl.program_id(0) > 0)
            def _():
                sq_ref[...] += p

        xhv = xh_ref[...].astype(F32)
        dxh = dyv * gam_ref[...]
        m1 = jnp.mean(dxh, axis=-1, keepdims=True)
        m2 = jnp.mean(dxh * xhv, axis=-1, keepdims=True)
        ds = r_ref[...] * (dxh - m1 - xhv * m2)
        pg = jnp.sum(dyv * xhv, axis=0, keepdims=True)
        pb = jnp.sum(dyv, axis=0, keepdims=True)

        @pl.when(pl.program_id(0) == 0)
        def _():
            dgam_ref[...] = pg
            dbet_ref[...] = pb

        @pl.when(pl.program_id(0) > 0)
        def _():
            dgam_ref[...] += pg
            dbet_ref[...] += pb

        db = ds.astype(BF16)
        dsb_ref[...] = db
        for c in range(0, DFF, ch):
            da = _dot_nt(db, wd_v[c:c + ch, :])
            gv = g_ref[:, c:c + ch].astype(F32)
            sg = jax.nn.sigmoid(gv)
            dg_ref[:, c:c + ch] = (da * u_ref[:, c:c + ch].astype(F32) * (sg * (1.0 + gv * (1.0 - sg)))).astype(BF16)
            du_ref[:, c:c + ch] = (da * (gv * sg)).astype(BF16)
        dx_ref[...] = ALPHA * ds + _dot(dg_ref[...], wg_v[...]) + _dot(du_ref[...], wu_v[...])

    row = pl.BlockSpec((tm, D), lambda i: (i, 0))
    wide = pl.BlockSpec((tm, DFF), lambda i: (i, 0))
    vec = pl.BlockSpec((1, D), lambda i: (0, 0))
    wsl = pltpu.VMEM((DFF, D), BF16)
    last = target is not None
    return pl.pallas_call(
        body, name="ffn_bwd_loss" if last else "ffn_bwd", grid=(T // tm,),
        in_specs=[row] + ([row] if last else [])
        + [row, pl.BlockSpec((tm, 1), lambda i: (i, 0)), vec, wide, wide, ANY_SPEC, ANY_SPEC, ANY_SPEC],
        out_specs=[wide, wide, row, row, vec, vec] + ([pl.BlockSpec((1, 1), lambda i: (0, 0))] if last else []),
        out_shape=[jax.ShapeDtypeStruct((T, DFF), BF16), jax.ShapeDtypeStruct((T, DFF), BF16),
                   jax.ShapeDtypeStruct((T, D), BF16), jax.ShapeDtypeStruct((T, D), F32),
                   jax.ShapeDtypeStruct((1, D), F32), jax.ShapeDtypeStruct((1, D), F32)]
        + ([jax.ShapeDtypeStruct((1, 1), F32)] if last else []),
        scratch_shapes=[wsl, wsl, wsl, pltpu.SemaphoreType.DMA],
        compiler_params=_cp(("arbitrary",)))(dy, *([target] if last else []), xh, r, gam, g, u, wd, wgt, wut)


def _mm_ln(a, w, x, gam, bet, name):
    tm = 256
    K = a.shape[1]

    def body(a_ref, w_ref, x_ref, g_ref, b_ref, y_ref, xh_ref, r_ref, yb_ref):
        s = ALPHA * x_ref[...] + _dot(a_ref[...], w_ref[...])
        mu = jnp.mean(s, axis=-1, keepdims=True)
        xc = s - mu
        var = jnp.mean(xc * xc, axis=-1, keepdims=True)
        r = lax.rsqrt(var + LN_EPS)
        xh = xc * r
        xh_ref[...] = xh.astype(BF16)
        r_ref[...] = r
        y = xh * g_ref[...] + b_ref[...]
        y_ref[...] = y
        yb_ref[...] = y.astype(BF16)

    row = pl.BlockSpec((tm, D), lambda i: (i, 0))
    vec = pl.BlockSpec((1, D), lambda i: (0, 0))
    return pl.pallas_call(
        body, name=name, grid=(T // tm,),
        in_specs=[pl.BlockSpec((tm, K), lambda i: (i, 0)), pl.BlockSpec((K, D), lambda i: (0, 0)), row, vec, vec],
        out_specs=[row, row, pl.BlockSpec((tm, 1), lambda i: (i, 0)), row],
        out_shape=[jax.ShapeDtypeStruct((T, D), F32), jax.ShapeDtypeStruct((T, D), BF16),
                   jax.ShapeDtypeStruct((T, 1), F32), jax.ShapeDtypeStruct((T, D), BF16)],
        compiler_params=_cp(("parallel",)))(a, w, x, gam, bet)


def _ln_bwd(dy, xh, r, gam, w):
    tm = 256

    def body(dy_ref, xh_ref, r_ref, g_ref, w_ref, ds_ref, dg_ref, db_ref, dsb_ref, dm_ref):
        i = pl.program_id(0)
        dyv = dy_ref[...]
        xhv = xh_ref[...].astype(F32)
        dxh = dyv * g_ref[...]
        m1 = jnp.mean(dxh, axis=-1, keepdims=True)
        m2 = jnp.mean(dxh * xhv, axis=-1, keepdims=True)
        ds = r_ref[...] * (dxh - m1 - xhv * m2)
        ds_ref[...] = ds
        dsb = ds.astype(BF16)
        dsb_ref[...] = dsb
        dm_ref[...] = _dot_nt(dsb, w_ref[...]).astype(BF16)
        pg = jnp.sum(dyv * xhv, axis=0, keepdims=True)
        pb = jnp.sum(dyv, axis=0, keepdims=True)

        @pl.when(i == 0)
        def _():
            dg_ref[...] = pg
            db_ref[...] = pb

        @pl.when(i > 0)
        def _():
            dg_ref[...] += pg
            db_ref[...] += pb

    row = pl.BlockSpec((tm, D), lambda i: (i, 0))
    vec = pl.BlockSpec((1, D), lambda i: (0, 0))
    return pl.pallas_call(
        body, name="ln_bwd_proj", grid=(T // tm,),
        in_specs=[row, row, pl.BlockSpec((tm, 1), lambda i: (i, 0)), vec, pl.BlockSpec((D, D), lambda i: (0, 0))],
        out_specs=[row, vec, vec, row, row],
        out_shape=[jax.ShapeDtypeStruct((T, D), F32), jax.ShapeDtypeStruct((1, D), F32),
                   jax.ShapeDtypeStruct((1, D), F32), jax.ShapeDtypeStruct((T, D), BF16),
                   jax.ShapeDtypeStruct((T, D), BF16)],
        compiler_params=_cp(("arbitrary",)))(dy, xh, r, gam, w)


def _loss_grad(y, tgt):
    tm = 256

    def body(y_ref, t_ref, l_ref, dy_ref):
        i = pl.program_id(0)
        e = y_ref[...] - t_ref[...]
        dy_ref[...] = e * (1.0 / D)
        p = jnp.sum(jnp.sum(e * e, axis=1, keepdims=True), axis=0, keepdims=True)

        @pl.when(i == 0)
        def _():
            l_ref[...] = p

        @pl.when(i > 0)
        def _():
            l_ref[...] += p

    row = pl.BlockSpec((tm, D), lambda i: (i, 0))
    return pl.pallas_call(
        body, name="loss_grad", grid=(T // tm,), in_specs=[row, row],
        out_specs=[pl.BlockSpec((1, 1), lambda i: (0, 0)), row],
        out_shape=[jax.ShapeDtypeStruct((1, 1), F32), jax.ShapeDtypeStruct((T, D), F32)],
        compiler_params=_cp(("arbitrary",)))(y, tgt)


def _adamw(w, g, m, v, tr):
    L, R, C = w.shape

    def body(w_ref, g_ref, m_ref, v_ref, d_ref, m2_ref, v2_ref):
        gv = g_ref[...]
        m2 = ADAM_B1 * m_ref[...] + (1.0 - ADAM_B1) * gv
        v2 = ADAM_B2 * v_ref[...] + (1.0 - ADAM_B2) * (gv * gv)
        m_hat = m2 / (1.0 - ADAM_B1 ** ADAM_STEP)
        v_hat = v2 / (1.0 - ADAM_B2 ** ADAM_STEP)
        d_ref[...] = -ADAM_LR * (m_hat / (jnp.sqrt(v_hat) + ADAM_EPS) + ADAM_WD * w_ref[...])
        m2_ref[...] = m2
        v2_ref[...] = v2

    blk = pl.BlockSpec((None, tr, C), lambda l, i: (l, i, 0))
    sh = jax.ShapeDtypeStruct((L, R, C), F32)
    return pl.pallas_call(
        body, name="adamw", grid=(L, R // tr), in_specs=[blk] * 4, out_specs=[blk] * 3,
        out_shape=[sh, sh, sh], compiler_params=_cp(("parallel", "parallel")))(w, g, m, v)


class _Job:
    def __init__(self, ins, out_shapes, aliases, sems, start, finish):
        self.ins, self.out_shapes, self.aliases, self.sems = list(ins), list(out_shapes), dict(aliases), list(sems)
        self.start, self.finish = start, finish


def _host_call(body, name, ins, in_specs, out_shapes, out_specs, scratch, aliases, job):
    n_in, n_out, n_scr = len(ins), len(out_shapes), len(scratch)
    jins = job.ins if job else []
    jouts = job.out_shapes if job else []
    jsems = job.sems if job else []

    def wrapped(*refs):
        a = n_in
        b = a + len(jins)
        c = b + n_out
        d = c + len(jouts)
        e = d + n_scr
        comm = None
        if job:
            jrefs = (refs[a:b], refs[c:d], refs[e:])
            comm = (lambda: job.start(*jrefs), lambda st: job.finish(st, *jrefs))
        body(refs[:a], refs[b:c], refs[d:e], comm)

    al = dict(aliases)
    if job:
        for ji, jo in job.aliases.items():
            al[n_in + ji] = n_out + jo
    res = pl.pallas_call(
        wrapped, name=name, in_specs=list(in_specs) + [ANY_SPEC] * len(jins),
        out_specs=list(out_specs) + [ANY_SPEC] * len(jouts), out_shape=list(out_shapes) + list(jouts),
        scratch_shapes=list(scratch) + list(jsems), input_output_aliases=al,
        compiler_params=_cp())(*ins, *jins)
    return res[:n_out], res[n_out:]


def _copy_in(src, dst, sem):
    cp = pltpu.make_async_copy(src, dst, sem)
    cp.start()
    cp.wait()


CHAINS = [(p, b) for p in range(2) for b in range(BL)]
NC = len(CHAINS)
ROWS_SHAPE = jax.ShapeDtypeStruct((NSTAT, S), F32)
SLAB_QKV = pltpu.VMEM((T, 2 * PAIRW), BF16)
SLAB_OUT = pltpu.VMEM((T, 2 * BQ), BF16)
SLAB_O32 = pltpu.VMEM((T, 2 * BQ), F32)
SLAB_T = pltpu.VMEM((2, BQ, T), BF16)
SLAB_KEYB = pltpu.VMEM((NSTAT, S, BQ), F32)
ACC_KV = pltpu.VMEM((2, T, BQ), F32)


def _lane_masks():
    lane = _iota((1, BQ), 1)
    m0 = (lane < 64).astype(BF16)
    return m0, 1.0 - m0


def _row_masks():
    r = _iota((BQ, 1), 0)
    m0 = (r < 64).astype(BF16)
    return m0, 1.0 - m0


def _stack(x, m0, m1):
    return jnp.concatenate([x * m0, x * m1], axis=0)


def _stack_t(xt, r0, r1):
    return jnp.concatenate([xt * r0, xt * r1], axis=1)


def _tr(x):
    return x.T


def _rows(b, i):
    return pl.ds(pl.multiple_of(b * S + i * BQ, BQ), BQ)


def _transpose_slab(src, dst, col0):
    def blk(n, _):
        r = pl.ds(pl.multiple_of(n * BQ, BQ), BQ)
        for p in range(2):
            dst[p, :, r] = _tr(src[r, col0(p):col0(p) + BQ])
        return 0

    lax.fori_loop(0, T // BQ, blk, 0)


def _heads(x):
    return x[:BQ], x[BQ:]


def _bcast_heads(r0, r1):
    return jnp.concatenate([jnp.broadcast_to(r0, (BQ, BQ)), jnp.broadcast_to(r1, (BQ, BQ))], axis=0)


def _by_channel(r0, r1):
    return jnp.where(_iota((BQ, BQ), 0) < 64, r0, r1)


def _colsum2(x):
    return jnp.sum(x[:BQ], axis=0, keepdims=True), jnp.sum(x[BQ:], axis=0, keepdims=True)


def _stat_row(ref, p, b, h, i):
    c = b * NH + 2 * p + h
    return ref[c:c + 1, pl.ds(pl.multiple_of(i * BQ, BQ), BQ)]


def _put_row(ref, p, b, h, i, v):
    c = b * NH + 2 * p + h
    ref[c:c + 1, pl.ds(pl.multiple_of(i * BQ, BQ), BQ)] = v


def _valid_t(strict):
    r = _iota((HB, BQ), 0) & (BQ - 1)
    c = _iota((HB, BQ), 1)
    return (r < c) if strict else (r <= c)


def _tri_blockdiag(later):
    r = _iota((HB, HB), 0)
    c = _iota((HB, HB), 1)
    same = (r >= BQ) == (c >= BQ)
    return (same & ((c > r) if later else (c < r))).astype(BF16)


def _cum_mm(tri, x):
    y = _dot(tri, _split2(x))
    return y[:, :BQ] + y[:, BQ:]


def _kv_tiles(qkv_v, p, b, j):
    r = _rows(b, j)
    return qkv_v[r, p * PAIRW + BQ:p * PAIRW + 2 * BQ], qkv_v[r, p * PAIRW + 2 * BQ:p * PAIRW + 3 * BQ]


def _q_tile(qkv_v, p, b, i):
    return qkv_v[_rows(b, i), p * PAIRW:p * PAIRW + BQ] * SCALE


def _sb_fwd(qkv, job=None):
    def body(ins, outs, scr, comm):
        (qkv_hbm,), (o_hbm, r_ref), (qkv_v, o_v, sem, vt_v) = ins, outs, scr
        _copy_in(qkv_hbm.at[:, pl.ds(0, 2 * PAIRW)], qkv_v, sem)
        st = comm[0]() if comm else None
        _transpose_slab(qkv_v, vt_v, lambda p: p * PAIRW + 2 * BQ)
        m0, m1 = _lane_masks()
        r0, r1 = _row_masks()
        valid = _valid_t(True)
        later = _tri_blockdiag(True)

        def steps(qts, i, j, cs, diag):
            ks = [_stack(_kv_tiles(qkv_v, p, b, j)[0], m0, m1) for p, b in CHAINS]
            zs = [_dot(ks[c], qts[c]) for c in range(NC)]
            lbs, lrs = [], []
            for c in range(NC):
                lb = _log_sigmoid_tile(zs[c])
                lr = lb - zs[c]
                if diag:
                    lr = jnp.where(valid, lr, 0.0)
                lbs.append(lb)
                lrs.append(lr)
            tails = [_cum_mm(later, lrs[c]) for c in range(NC)]
            avs = []
            for c in range(NC):
                a = jnp.exp(lbs[c] + tails[c] + _bcast_heads(*cs[c][0]))
                if diag:
                    a = jnp.where(valid, a, 0.0)
                avs.append(a.astype(BF16))
            out = []
            for c, (p, b) in enumerate(CHAINS):
                vts = _stack_t(vt_v[p, :, _rows(b, j)], r0, r1)
                s0, s1 = _colsum2(lrs[c])
                out.append(((cs[c][0][0] + s0, cs[c][0][1] + s1), cs[c][1] + _dot(vts, avs[c])))
            return tuple(out)

        def qblock(i, _):
            qts = [_tr(_q_tile(qkv_v, p, b, i)) for p, b in CHAINS]
            zr = jnp.zeros((1, BQ), F32)
            cs = steps(qts, i, i, (((zr, zr), jnp.zeros((BQ, BQ), F32)),) * NC, True)
            cs = lax.fori_loop(1, i + 1, lambda jj, cs: steps(qts, i, i - jj, cs, False), cs)
            for c, (p, b) in enumerate(CHAINS):
                o_v[_rows(b, i), p * BQ:(p + 1) * BQ] = cs[c][1].T.astype(BF16)
                for h in range(2):
                    _put_row(r_ref, p, b, h, i, cs[c][0][h])
            return 0

        lax.fori_loop(0, NB, qblock, 0)
        _copy_in(o_v, o_hbm.at[:, pl.ds(0, 2 * BQ)], sem)
        if comm:
            comm[1](st)

    (mixed, rtot), extra = _host_call(
        body, "sb_fwd", [qkv], [ANY_SPEC], [jax.ShapeDtypeStruct((T, D), BF16), ROWS_SHAPE], [ANY_SPEC, VMEM_SPEC],
        [SLAB_QKV, SLAB_OUT, pltpu.SemaphoreType.DMA, SLAB_T], {}, job)
    return mixed, rtot, extra


def _sb_bwd(qkv, dmixed, rtot, job=None):
    def body(ins, outs, scr, comm):
        (qkv_hbm, do_hbm, r_ref), (dqkv_hbm,), (qkv_v, do_v, dq_v, dk_s, dv_s, sem, kt_v) = ins, outs, scr
        _copy_in(qkv_hbm.at[:, pl.ds(0, 2 * PAIRW)], qkv_v, sem)
        _copy_in(do_hbm.at[:, pl.ds(0, 2 * BQ)], do_v, sem)
        st = comm[0]() if comm else None
        _transpose_slab(qkv_v, kt_v, lambda p: p * PAIRW + BQ)
        m0, m1 = _lane_masks()
        f0, f1 = m0.astype(F32), m1.astype(F32)
        r0, r1 = _row_masks()
        valid = _valid_t(True)
        later = _tri_blockdiag(True)
        earlier = _tri_blockdiag(False)
        dk_s[...] = jnp.zeros_like(dk_s)
        dv_s[...] = jnp.zeros_like(dv_s)

        def steps(qns, qts, dns, dts, rts, i, j, cs, diag):
            kv = [_kv_tiles(qkv_v, p, b, j) for p, b in CHAINS]
            ks = [_stack(kv[c][0], m0, m1) for c in range(NC)]
            vs = [_stack(kv[c][1], m0, m1) for c in range(NC)]
            zs = [_dot(ks[c], qts[c]) for c in range(NC)]
            das = [_dot(vs[c], dts[c]) for c in range(NC)]
            lbs, lrs, pls = [], [], []
            for c in range(NC):
                lb = _log_sigmoid_tile(zs[c])
                lr = lb - zs[c]
                if diag:
                    lr = jnp.where(valid, lr, 0.0)
                s0, s1 = _colsum2(lr)
                lbs.append(lb)
                lrs.append(lr)
                pls.append((cs[c][0][0] + s0, cs[c][0][1] + s1))
            tails = [_cum_mm(later, lrs[c]) for c in range(NC)]
            avs, gms = [], []
            for c in range(NC):
                a = jnp.exp(lbs[c] + tails[c] + _bcast_heads(rts[c][0] - pls[c][0], rts[c][1] - pls[c][1]))
                if diag:
                    a = jnp.where(valid, a, 0.0)
                avs.append(a)
                gms.append(das[c] * a)
            befores = [_cum_mm(earlier, gms[c]) for c in range(NC)]
            dzbs = []
            for c in range(NC):
                beta = jnp.exp(lbs[c])
                dz = gms[c] - beta * (gms[c] + befores[c] + _bcast_heads(*cs[c][1]))
                if diag:
                    dz = jnp.where(valid, dz, 0.0)
                dzbs.append(dz.astype(BF16))
            out = []
            for c, (p, b) in enumerate(CHAINS):
                dq = cs[c][2] + _dot(_stack_t(kt_v[p, :, _rows(b, j)], r0, r1), dzbs[c])
                dk = _dot(dzbs[c], qns[c])
                dv = _dot(avs[c].astype(BF16), dns[c])
                dk_s[p, _rows(b, j), :] += dk[:BQ] * f0 + dk[BQ:] * f1
                dv_s[p, _rows(b, j), :] += dv[:BQ] * f0 + dv[BQ:] * f1
                g0, g1 = _colsum2(gms[c])
                out.append((pls[c], (cs[c][1][0] + g0, cs[c][1][1] + g1), dq))
            return tuple(out)

        def qblock(i, _):
            qns = [_q_tile(qkv_v, p, b, i) for p, b in CHAINS]
            dns = [do_v[_rows(b, i), p * BQ:(p + 1) * BQ] for p, b in CHAINS]
            qts = [_tr(t) for t in qns]
            dts = [_tr(t) for t in dns]
            rts = [(_stat_row(r_ref, p, b, 0, i), _stat_row(r_ref, p, b, 1, i)) for p, b in CHAINS]
            zr = jnp.zeros((1, BQ), F32)
            cs = (((zr, zr), (zr, zr), jnp.zeros((BQ, BQ), F32)),) * NC
            cs = lax.fori_loop(0, i, lambda j, cs: steps(qns, qts, dns, dts, rts, i, j, cs, False), cs)
            cs = steps(qns, qts, dns, dts, rts, i, i, cs, True)
            for c, (p, b) in enumerate(CHAINS):
                dq_v[_rows(b, i), p * PAIRW:p * PAIRW + BQ] = (cs[c][2].T * SCALE).astype(BF16)
            return 0

        lax.fori_loop(0, NB, qblock, 0)
        for p in range(2):
            dq_v[:, p * PAIRW + BQ:p * PAIRW + 2 * BQ] = dk_s[p].astype(BF16)
            dq_v[:, p * PAIRW + 2 * BQ:p * PAIRW + 3 * BQ] = dv_s[p].astype(BF16)
        _copy_in(dq_v, dqkv_hbm.at[:, pl.ds(0, 2 * PAIRW)], sem)
        if comm:
            comm[1](st)

    (dqkv,), extra = _host_call(
        body, "sb_bwd", [qkv, dmixed, rtot], [ANY_SPEC, ANY_SPEC, VMEM_SPEC],
        [jax.ShapeDtypeStruct((T, QKVW), BF16)], [ANY_SPEC],
        [SLAB_QKV, SLAB_OUT, SLAB_QKV, ACC_KV, ACC_KV, pltpu.SemaphoreType.DMA, SLAB_T], {}, job)
    return dqkv, extra


def _flash_fwd(qkv, mixed, g, fox, bias, job=None):
    def body(ins, outs, scr, comm):
        if fox:
            qkv_hbm, cq_ref, ckb_hbm, _ = ins
            (o_hbm, lse_ref, o32_hbm), (qkv_v, o_v, sem, vt_v, o32_v, ckb_v) = outs, scr
        else:
            qkv_hbm, tbl_ref, _ = ins
            (o_hbm, lse_ref), (qkv_v, o_v, sem, vt_v) = outs, scr
        _copy_in(qkv_hbm.at[:, pl.ds(g * 2 * PAIRW, 2 * PAIRW)], qkv_v, sem)
        if fox:
            _copy_in(ckb_hbm, ckb_v, sem)
        st = comm[0]() if comm else None
        _transpose_slab(qkv_v, vt_v, lambda p: p * PAIRW + 2 * BQ)
        m0, m1 = _lane_masks()
        r0, r1 = _row_masks()
        valid = _valid_t(False)

        def steps(qts, cqs, i, j, cs, diag):
            ks = [_stack(_kv_tiles(qkv_v, p, b, j)[0], m0, m1) for p, b in CHAINS]
            zs = [_dot(ks[c], qts[c]) for c in range(NC)]
            prs, alphas, out = [], [], []
            for c, (p, b) in enumerate(CHAINS):
                (ma, mb), (la, lb_), _ = cs[c]
                if fox:
                    kk = pl.ds(pl.multiple_of(j * BQ, BQ), BQ)
                    col = b * NH + 2 * p
                    z = zs[c] + (cqs[c] - jnp.concatenate([ckb_v[col, kk, :], ckb_v[col + 1, kk, :]], axis=0))
                    if diag:
                        z = jnp.where(valid, z, NEG)
                else:
                    z = zs[c] + tbl_ref[p, i - j]
                za, zb = _heads(z)
                na = jnp.maximum(ma, jnp.max(za, axis=0, keepdims=True))
                nb = jnp.maximum(mb, jnp.max(zb, axis=0, keepdims=True))
                aa, ab = jnp.exp(ma - na), jnp.exp(mb - nb)
                pr = jnp.exp(z - _bcast_heads(na, nb))
                sa, sb = _colsum2(pr)
                prs.append(_split2(pr) if fox else pr.astype(BF16))
                alphas.append((aa, ab))
                out.append(((na, nb), (aa * la + sa, ab * lb_ + sb)))
            pvs = []
            for c, (p, b) in enumerate(CHAINS):
                vts = _stack_t(vt_v[p, :, _rows(b, j)], r0, r1)
                if fox:
                    pvs.append(_dot(vts, prs[c][:, :BQ]) + _dot(vts, prs[c][:, BQ:]))
                else:
                    pvs.append(_dot(vts, prs[c]))
            return tuple((out[c][0], out[c][1], _by_channel(*alphas[c]) * cs[c][2] + pvs[c]) for c in range(NC))

        def qblock(i, _):
            qts = [_tr(_q_tile(qkv_v, p, b, i)) for p, b in CHAINS]
            if fox:
                cqs = [_bcast_heads(_stat_row(cq_ref, p, b, 0, i), _stat_row(cq_ref, p, b, 1, i)) for p, b in CHAINS]
            else:
                cqs = [None] * NC
            ng = jnp.full((1, BQ), NEG, F32)
            zr = jnp.zeros((1, BQ), F32)
            cs = steps(qts, cqs, i, i, (((ng, ng), (zr, zr), jnp.zeros((BQ, BQ), F32)),) * NC, True)
            cs = lax.fori_loop(1, i + 1, lambda jj, cs: steps(qts, cqs, i, i - jj, cs, False), cs)
            for c, (p, b) in enumerate(CHAINS):
                (ma, mb), (la, lb_), acc = cs[c]
                o = (acc / _by_channel(la, lb_)).T
                o_v[_rows(b, i), p * BQ:(p + 1) * BQ] = o.astype(BF16)
                if fox:
                    o32_v[_rows(b, i), p * BQ:(p + 1) * BQ] = o
                _put_row(lse_ref, p, b, 0, i, ma + jnp.log(la))
                _put_row(lse_ref, p, b, 1, i, mb + jnp.log(lb_))
            return 0

        lax.fori_loop(0, NB, qblock, 0)
        _copy_in(o_v, o_hbm.at[:, pl.ds(g * 2 * BQ, 2 * BQ)], sem)
        if fox:
            _copy_in(o32_v, o32_hbm, sem)
        if comm:
            comm[1](st)

    bias_specs = [VMEM_SPEC, ANY_SPEC] if fox else [VMEM_SPEC]
    n_in = 2 + len(bias_specs)
    o32 = [jax.ShapeDtypeStruct((T, 2 * BQ), F32)] if fox else []
    res, extra = _host_call(
        body, "fox_fwd" if fox else "dil_fwd", [qkv, *bias, mixed], [ANY_SPEC] + bias_specs + [ANY_SPEC],
        [jax.ShapeDtypeStruct((T, D), BF16), ROWS_SHAPE] + o32, [ANY_SPEC, VMEM_SPEC] + [ANY_SPEC] * len(o32),
        [SLAB_QKV, SLAB_OUT, pltpu.SemaphoreType.DMA, SLAB_T] + ([SLAB_O32, SLAB_KEYB] if fox else []),
        {n_in - 1: 0}, job)
    return (*res, extra)


def _flash_bwd(qkv, o, dmixed, lse, dqkv, g, fox, bias, job=None):
    def body(ins, outs, scr, comm):
        if fox:
            qkv_hbm, o_hbm, do_hbm, lse_ref, cq_ref, ckb_hbm, _ = ins
            (dqkv_hbm, db_ref), (qkv_v, o_v, do_v, dq_v, dk_s, dv_s, sem, kt_v, ckb_v, dc_s) = outs, scr
        else:
            qkv_hbm, o_hbm, do_hbm, lse_ref, tbl_ref, _ = ins
            (dqkv_hbm, db_ref), (qkv_v, o_v, do_v, dq_v, dk_s, dv_s, sem, kt_v) = outs, scr
        _copy_in(qkv_hbm.at[:, pl.ds(g * 2 * PAIRW, 2 * PAIRW)], qkv_v, sem)
        _copy_in(do_hbm.at[:, pl.ds(g * 2 * BQ, 2 * BQ)], do_v, sem)
        if fox:
            _copy_in(o_hbm, o_v, sem)
            _copy_in(ckb_hbm, ckb_v, sem)
        else:
            _copy_in(o_hbm.at[:, pl.ds(g * 2 * BQ, 2 * BQ)], o_v, sem)
        st = comm[0]() if comm else None
        _transpose_slab(qkv_v, kt_v, lambda p: p * PAIRW + BQ)
        m0, m1 = _lane_masks()
        f0, f1 = m0.astype(F32), m1.astype(F32)
        r0, r1 = _row_masks()
        valid = _valid_t(False)
        dk_s[...] = jnp.zeros_like(dk_s)
        dv_s[...] = jnp.zeros_like(dv_s)
        if fox:
            dc_s[...] = jnp.zeros_like(dc_s)
        else:
            db_ref[...] = jnp.zeros_like(db_ref)

        def steps(qns, qts, dns, dts, cqs, lses, deltas, i, j, dqs, diag):
            kv = [_kv_tiles(qkv_v, p, b, j) for p, b in CHAINS]
            ks = [_stack(kv[c][0], m0, m1) for c in range(NC)]
            vs = [_stack(kv[c][1], m0, m1) for c in range(NC)]
            zs = [_dot(ks[c], qts[c]) for c in range(NC)]
            dps = [_dot(vs[c], dts[c]) for c in range(NC)]
            prs, dzl = [], []
            for c, (p, b) in enumerate(CHAINS):
                if fox:
                    kk = pl.ds(pl.multiple_of(j * BQ, BQ), BQ)
                    col = b * NH + 2 * p
                    z = zs[c] + (cqs[c] - jnp.concatenate([ckb_v[col, kk, :], ckb_v[col + 1, kk, :]], axis=0))
                    if diag:
                        z = jnp.where(valid, z, NEG)
                else:
                    z = zs[c] + tbl_ref[p, i - j]
                pr = jnp.exp(z - lses[c])
                prs.append(pr.astype(BF16))
                dzl.append(pr * (dps[c] - deltas[c]))
            dzbs = [dz.astype(BF16) for dz in dzl]
            new = []
            for c, (p, b) in enumerate(CHAINS):
                new.append(dqs[c] + _dot(_stack_t(kt_v[p, :, _rows(b, j)], r0, r1), dzbs[c]))
                dk = _dot(dzbs[c], qns[c])
                dv = _dot(prs[c], dns[c])
                dk_s[p, _rows(b, j), :] += dk[:BQ] * f0 + dk[BQ:] * f1
                dv_s[p, _rows(b, j), :] += dv[:BQ] * f0 + dv[BQ:] * f1
                if fox:
                    dc_s[c, pl.ds(pl.multiple_of(j * HB, HB), HB), :] += dzl[c]
            if not fox:
                for p in range(2):
                    db_ref[p, i - j] = db_ref[p, i - j] + (dzl[2 * p] + dzl[2 * p + 1])
            return tuple(new)

        def qblock(i, _):
            qns = [_q_tile(qkv_v, p, b, i) for p, b in CHAINS]
            dns = [do_v[_rows(b, i), p * BQ:(p + 1) * BQ] for p, b in CHAINS]
            qts = [_tr(t) for t in qns]
            dts = [_tr(t) for t in dns]
            lses = [_bcast_heads(_stat_row(lse_ref, p, b, 0, i), _stat_row(lse_ref, p, b, 1, i)) for p, b in CHAINS]
            if fox:
                cqs = [_bcast_heads(_stat_row(cq_ref, p, b, 0, i), _stat_row(cq_ref, p, b, 1, i)) for p, b in CHAINS]
            else:
                cqs = [None] * NC
            deltas = []
            for c, (p, b) in enumerate(CHAINS):
                pt = (dns[c].astype(F32) * o_v[_rows(b, i), p * BQ:(p + 1) * BQ].astype(F32)).T
                deltas.append(_bcast_heads(jnp.sum(pt[:64], axis=0, keepdims=True), jnp.sum(pt[64:], axis=0, keepdims=True)))
            dqs = (jnp.zeros((BQ, BQ), F32),) * NC
            dqs = lax.fori_loop(0, i, lambda j, d: steps(qns, qts, dns, dts, cqs, lses, deltas, i, j, d, False), dqs)
            dqs = steps(qns, qts, dns, dts, cqs, lses, deltas, i, i, dqs, True)
            for c, (p, b) in enumerate(CHAINS):
                dq_v[_rows(b, i), p * PAIRW:p * PAIRW + BQ] = (dqs[c].T * SCALE).astype(BF16)
            return 0

        lax.fori_loop(0, NB, qblock, 0)
        for p in range(2):
            dq_v[:, p * PAIRW + BQ:p * PAIRW + 2 * BQ] = dk_s[p].astype(BF16)
            dq_v[:, p * PAIRW + 2 * BQ:p * PAIRW + 3 * BQ] = dv_s[p].astype(BF16)
        _copy_in(dq_v, dqkv_hbm.at[:, pl.ds(g * 2 * PAIRW, 2 * PAIRW)], sem)
        if fox:
            lane = _iota((BQ, NSTAT), 1)

            def fold(n, _):
                t = jnp.zeros((BQ, NSTAT), F32)
                for c, (p, b) in enumerate(CHAINS):
                    s = jnp.sum(dc_s[c, pl.ds(pl.multiple_of(n * HB, HB), HB), :], axis=1, keepdims=True)
                    col = b * NH + 2 * p
                    t = t - jnp.where(lane == col, s[:BQ], 0.0) - jnp.where(lane == col + 1, s[BQ:], 0.0)
                db_ref[pl.ds(pl.multiple_of(n * BQ, BQ), BQ), :] = t
                return 0

            lax.fori_loop(0, NB, fold, 0)
        if comm:
            comm[1](st)

    if fox:
        bias_specs = [VMEM_SPEC, ANY_SPEC]
        db_shape = jax.ShapeDtypeStruct((S, NSTAT), F32)
        more = [SLAB_KEYB, pltpu.VMEM((NC, NB * HB, BQ), F32)]
    else:
        bias_specs = [VMEM_SPEC]
        db_shape = jax.ShapeDtypeStruct((2, NB, HB, BQ), F32)
        more = []
    n_in = 5 + len(bias_specs)
    (dqkv, db), extra = _host_call(
        body, "fox_bwd" if fox else "dil_bwd", [qkv, o, dmixed, lse, *bias, dqkv],
        [ANY_SPEC, ANY_SPEC, ANY_SPEC, VMEM_SPEC] + bias_specs + [ANY_SPEC],
        [jax.ShapeDtypeStruct((T, QKVW), BF16), db_shape], [ANY_SPEC, VMEM_SPEC],
        [SLAB_QKV, SLAB_O32 if fox else SLAB_OUT, SLAB_OUT, SLAB_QKV, ACC_KV, ACC_KV, pltpu.SemaphoreType.DMA, SLAB_T]
        + more, {n_in - 1: 0}, job)
    return dqkv, db, extra


def _delta_t(d):
    return d * BQ + _iota((HB, BQ), 1) - (_iota((HB, BQ), 0) & (BQ - 1))


def _buckets_in(d):
    lo, hi = max(d * BQ - (BQ - 1), 0), d * BQ + BQ - 1
    return [b for b in range(32) if BUCKET_TH[b] <= hi and (b == 31 or BUCKET_TH[b + 1] > lo)]


def _in_bucket(delta, b):
    m = delta >= BUCKET_TH[b]
    return m if b == 31 else m & (delta < BUCKET_TH[b + 1])


def _dil_table(rel_bias):
    def body(rb_ref, o_ref):
        for d in range(NB):
            delta = _delta_t(d)
            pos = delta >= 0
            n = ((pos & (delta <= 128)).astype(jnp.int32)
                 + (pos & (delta <= 512) & ((delta & 3) == 0)).astype(jnp.int32)
                 + (pos & ((delta & 15) == 0)).astype(jnp.int32))
            logn = jnp.where(n == 3, math.log(3.0), jnp.where(n == 2, math.log(2.0), jnp.where(n == 1, 0.0, NEG)))
            head1 = _iota((HB, BQ), 0) >= BQ
            for p in range(2):
                val = jnp.zeros((HB, BQ), F32)
                for b in _buckets_in(d):
                    val = jnp.where(_in_bucket(delta, b), jnp.where(head1, rb_ref[b, 2 * p + 1], rb_ref[b, 2 * p]), val)
                o_ref[p, d] = val + logn

    return pl.pallas_call(
        body, name="dil_table", in_specs=[pl.BlockSpec(memory_space=pltpu.SMEM)], out_specs=VMEM_SPEC,
        out_shape=jax.ShapeDtypeStruct((2, NB, HB, BQ), F32), compiler_params=_cp())(rel_bias)


def _dil_table_bwd(dtbl):
    def body(dt_ref, o_ref):
        p = pl.program_id(0)
        rowi = _iota((32, BQ), 0)
        lanei = _iota((32, BQ), 1)

        @pl.when(p == 0)
        def _():
            o_ref[...] = jnp.zeros_like(o_ref)

        out = jnp.zeros((32, BQ), F32)
        for b in range(32):
            acc = None
            for d in range(NB):
                if b in _buckets_in(d):
                    t = jnp.where(_in_bucket(_delta_t(d), b), dt_ref[d], 0.0)
                    acc = t if acc is None else acc + t
            rs = jnp.sum(acc, axis=1, keepdims=True)
            s0 = jnp.sum(rs[:BQ], axis=0, keepdims=True)
            s1 = jnp.sum(rs[BQ:], axis=0, keepdims=True)
            out = (out + jnp.where((rowi == b) & (lanei == 2 * p), s0, 0.0)
                   + jnp.where((rowi == b) & (lanei == 2 * p + 1), s1, 0.0))
        o_ref[...] += out

    return pl.pallas_call(
        body, name="dil_table_bwd", grid=(2,),
        in_specs=[pl.BlockSpec((None, NB, HB, BQ), lambda p: (p, 0, 0, 0))],
        out_specs=pl.BlockSpec((32, BQ), lambda p: (0, 0)),
        out_shape=jax.ShapeDtypeStruct((32, BQ), F32),
        compiler_params=_cp(("arbitrary",)))(dtbl)


def _fox_prep(gate, fb):
    def body(g_ref, fb_ref, c_ref):
        tri = (_iota((BQ, BQ), 0) >= _iota((BQ, BQ), 1)).astype(BF16)

        def blk(i, carry):
            r0 = pl.multiple_of(i * BQ, BQ)
            lf = _log_sigmoid(g_ref[pl.ds(r0, BQ), :] + fb_ref[...])
            c = _dot(tri, _split3(lf))
            c_ref[pl.ds(r0, BQ), :] = c[:, 0:BQ] + c[:, BQ:2 * BQ] + c[:, 2 * BQ:3 * BQ] + carry
            return carry + jnp.sum(lf, axis=0, keepdims=True)

        lax.fori_loop(0, NB, blk, jnp.zeros((1, BQ), F32))

    blk = pl.BlockSpec((S, GATEW), lambda b: (b, 0))
    return pl.pallas_call(
        body, name="fox_prep", grid=(BL,), in_specs=[blk, pl.BlockSpec((1, GATEW), lambda b: (0, 0))],
        out_specs=blk, out_shape=jax.ShapeDtypeStruct((T, GATEW), F32),
        compiler_params=_cp(("parallel",)))(gate, fb)


def _fox_post(dcum, gate, fb):
    def body(dc_ref, g_ref, fb_ref, dg_ref, dfb_ref):
        b = pl.program_id(0)
        tri = (_iota((BQ, BQ), 0) <= _iota((BQ, BQ), 1)).astype(BF16)

        def blk(ii, carry):
            csum, dfb = carry
            r0 = pl.multiple_of((NB - 1 - ii) * BQ, BQ)
            dc = dc_ref[pl.ds(r0, BQ), :]
            c = _dot(tri, _split3(dc))
            dlf = c[:, 0:BQ] + c[:, BQ:2 * BQ] + c[:, 2 * BQ:3 * BQ] + csum
            dg = dlf * jnp.exp(_log_sigmoid(-(g_ref[pl.ds(r0, BQ), :] + fb_ref[...])))
            dg_ref[pl.ds(r0, BQ), :] = dg
            return csum + jnp.sum(dc, axis=0, keepdims=True), dfb + jnp.sum(dg, axis=0, keepdims=True)

        z = jnp.zeros((1, BQ), F32)
        _, dfb = lax.fori_loop(0, NB, blk, (z, z))

        @pl.when(b == 0)
        def _():
            dfb_ref[...] = dfb

        @pl.when(b > 0)
        def _():
            dfb_ref[...] += dfb

    blk = pl.BlockSpec((S, GATEW), lambda b: (b, 0))
    vec = pl.BlockSpec((1, GATEW), lambda b: (0, 0))
    return pl.pallas_call(
        body, name="fox_post", grid=(BL,), in_specs=[blk, blk, vec], out_specs=[blk, vec],
        out_shape=[jax.ShapeDtypeStruct((T, GATEW), F32), jax.ShapeDtypeStruct((1, GATEW), F32)],
        compiler_params=_cp(("arbitrary",)))(dcum, gate, fb)


def _shift_down(x, n):
    return jnp.where(_iota(x.shape, 0) >= n, pltpu.roll(x, n, 0), 0.0)


def _shift_up(x, n):
    return jnp.where(_iota(x.shape, 0) < S - n, pltpu.roll(x, S - n, 0), 0.0)


def _conv_fwd(conv, cw, mixed):
    W = 256

    def body(c_ref, w_ref, _, o_ref):
        u = c_ref[:, W:2 * W] * c_ref[:, 2 * W:3 * W]
        y = w_ref[0:1, :] * _shift_down(u, 2) + w_ref[1:2, :] * _shift_down(u, 1) + w_ref[2:3, :] * u
        o_ref[...] = (c_ref[:, 0:W] * y).astype(BF16)

    return pl.pallas_call(
        body, name="conv_fwd", grid=(BL,),
        in_specs=[pl.BlockSpec((S, CONVW), lambda b: (b, 0)), pl.BlockSpec((8, W), lambda b: (0, 0)), ANY_SPEC],
        out_specs=pl.BlockSpec((S, W), lambda b: (b, 3)),
        out_shape=jax.ShapeDtypeStruct((T, D), BF16), input_output_aliases={2: 0},
        compiler_params=_cp(("parallel",)))(conv, cw, mixed)


def _conv_bwd(conv, cw, dmixed):
    W = 256

    def body(c_ref, w_ref, do_ref, dc_ref, dw_ref):
        b = pl.program_id(0)
        bg = c_ref[:, 0:W]
        cg = c_ref[:, W:2 * W]
        hv = c_ref[:, 2 * W:3 * W]
        do = do_ref[...].astype(F32)
        u = cg * hv
        u1 = _shift_down(u, 1)
        u2 = _shift_down(u, 2)
        y = w_ref[0:1, :] * u2 + w_ref[1:2, :] * u1 + w_ref[2:3, :] * u
        dy = do * bg
        du = w_ref[2:3, :] * dy + w_ref[1:2, :] * _shift_up(dy, 1) + w_ref[0:1, :] * _shift_up(dy, 2)
        dc_ref[:, 0:W] = (do * y).astype(BF16)
        dc_ref[:, W:2 * W] = (du * hv).astype(BF16)
        dc_ref[:, 2 * W:3 * W] = (du * cg).astype(BF16)
        rowi = _iota((8, W), 0)
        dw = (jnp.where(rowi == 0, jnp.sum(dy * u2, axis=0, keepdims=True), 0.0)
              + jnp.where(rowi == 1, jnp.sum(dy * u1, axis=0, keepdims=True), 0.0)
              + jnp.where(rowi == 2, jnp.sum(dy * u, axis=0, keepdims=True), 0.0))

        @pl.when(b == 0)
        def _():
            dw_ref[...] = dw

        @pl.when(b > 0)
        def _():
            dw_ref[...] += dw

    return pl.pallas_call(
        body, name="conv_bwd", grid=(BL,),
        in_specs=[pl.BlockSpec((S, CONVW), lambda b: (b, 0)), pl.BlockSpec((8, W), lambda b: (0, 0)),
                  pl.BlockSpec((S, W), lambda b: (b, 3))],
        out_specs=[pl.BlockSpec((S, CONVW), lambda b: (b, 0)), pl.BlockSpec((8, W), lambda b: (0, 0))],
        out_shape=[jax.ShapeDtypeStruct((T, CONVW), BF16), jax.ShapeDtypeStruct((8, W), F32)],
        compiler_params=_cp(("arbitrary",)))(conv, cw, dmixed)


def _place():
    x, y, c = lax.axis_index("x"), lax.axis_index("y"), lax.axis_index("c")
    return x, y, c


def _chips_of(x, y):
    return [(1 - x, y), (x, 1 - y), (1 - x, 1 - y)]


def _dev(p):
    return 4 * p[0] + 2 * p[1] + p[2]


def _gather_job_a(shards):
    n = len(shards)

    def peers(x, y, c):
        return [(x, y, 1 - c)] + [(*chip, c) for chip in _chips_of(x, y)]

    def start(ins, outs, sems):
        send, recv, loc = sems
        x, y, c = _place()
        me = (x, y, c)
        cps = []
        for a in range(n):
            cps.append(pltpu.make_async_copy(ins[a], outs[a].at[_dev(me)], loc.at[a]))
            for k, peer in enumerate(peers(x, y, c)):
                cps.append(pltpu.make_async_remote_copy(
                    src_ref=ins[a], dst_ref=outs[a].at[_dev(me)], send_sem=send.at[a, k], recv_sem=recv.at[a, k],
                    device_id=peer, device_id_type=MESH))
        for cp in cps:
            cp.start()
        return cps

    def finish(cps, ins, outs, sems):
        send, recv, loc = sems
        x, y, c = _place()
        for a in range(n):
            for k, peer in enumerate(peers(x, y, c)):
                pltpu.make_async_remote_copy(
                    src_ref=ins[a], dst_ref=outs[a].at[_dev(peer)], send_sem=send.at[a, k], recv_sem=recv.at[a, k],
                    device_id=(x, y, c), device_id_type=MESH).wait_recv()
        for a in range(n):
            cps[5 * a].wait()
            for k in range(4):
                cps[5 * a + 1 + k].wait_send()

    return _Job(shards, [jax.ShapeDtypeStruct((NDEV,) + s.shape, s.dtype) for s in shards], {},
                [pltpu.SemaphoreType.DMA((n, 4)), pltpu.SemaphoreType.DMA((n, 4)), pltpu.SemaphoreType.DMA((n,))],
                start, finish)


def _gather_job_b(gathered):
    n = len(gathered)

    def start(ins, outs, sems):
        send, recv = sems
        x, y, c = _place()
        cps = []
        for a in range(n):
            for j, chip in enumerate(_chips_of(x, y)):
                blk = outs[a].at[_dev((*chip, c))]
                cps.append(pltpu.make_async_remote_copy(
                    src_ref=blk, dst_ref=blk, send_sem=send.at[a, j], recv_sem=recv.at[a, j],
                    device_id=(x, y, 1 - c), device_id_type=MESH))
        for cp in cps:
            cp.start()
        return cps

    def finish(cps, ins, outs, sems):
        send, recv = sems
        x, y, c = _place()
        for a in range(n):
            for j, chip in enumerate(_chips_of(x, y)):
                blk = outs[a].at[_dev((*chip, 1 - c))]
                pltpu.make_async_remote_copy(
                    src_ref=blk, dst_ref=blk, send_sem=send.at[a, j], recv_sem=recv.at[a, j],
                    device_id=(x, y, c), device_id_type=MESH).wait_recv()
        for cp in cps:
            cp.wait_send()

    return _Job(gathered, [jax.ShapeDtypeStruct(g.shape, g.dtype) for g in gathered], {a: a for a in range(n)},
                [pltpu.SemaphoreType.DMA((n, 3)), pltpu.SemaphoreType.DMA((n, 3))], start, finish)


def _sibling_job(grads):
    n = len(grads)

    def start(ins, outs, sems):
        send, recv = sems
        x, y, c = _place()
        cps = [pltpu.make_async_remote_copy(
            src_ref=ins[a].at[:, 1 - c], dst_ref=outs[a], send_sem=send.at[a], recv_sem=recv.at[a],
            device_id=(x, y, 1 - c), device_id_type=MESH) for a in range(n)]
        for cp in cps:
            cp.start()
        return cps

    def finish(cps, ins, outs, sems):
        for cp in cps:
            cp.wait()

    return _Job(grads, [jax.ShapeDtypeStruct(g.shape[:1] + g.shape[2:], F32) for g in grads], {},
                [pltpu.SemaphoreType.DMA((n,)), pltpu.SemaphoreType.DMA((n,))], start, finish)


def _chip_job(psums):
    n = len(psums)

    def copies(ins, outs, sems):
        send, recv, loc = sems
        x, y, c = _place()
        mychip = 2 * x + y
        cps = []
        for a in range(n):
            cps.append(pltpu.make_async_copy(ins[a].at[mychip], outs[a].at[mychip], loc.at[a]))
            for j, chip in enumerate(_chips_of(x, y)):
                cps.append(pltpu.make_async_remote_copy(
                    src_ref=ins[a].at[2 * chip[0] + chip[1]], dst_ref=outs[a].at[mychip],
                    send_sem=send.at[a, j], recv_sem=recv.at[a, j], device_id=(*chip, c), device_id_type=MESH))
        return cps

    def start(ins, outs, sems):
        for cp in copies(ins, outs, sems):
            cp.start()

    def finish(_, ins, outs, sems):
        cps = copies(ins, outs, sems)
        send, recv, loc = sems
        x, y, c = _place()
        mychip = 2 * x + y
        for a in range(n):
            for j, chip in enumerate(_chips_of(x, y)):
                pltpu.make_async_remote_copy(
                    src_ref=ins[a].at[mychip], dst_ref=outs[a].at[2 * chip[0] + chip[1]],
                    send_sem=send.at[a, j], recv_sem=recv.at[a, j], device_id=(x, y, c), device_id_type=MESH).wait_recv()
        for a in range(n):
            cps[4 * a].wait()
            for j in range(3):
                cps[4 * a + 1 + j].wait_send()

    return _Job(psums, [jax.ShapeDtypeStruct(p.shape, BF16) for p in psums], {},
                [pltpu.SemaphoreType.DMA((n, 3)), pltpu.SemaphoreType.DMA((n, 3)), pltpu.SemaphoreType.DMA((n,))],
                start, finish)


def _join_jobs(*jobs):
    jobs = [j for j in jobs if j is not None]
    if len(jobs) <= 1:
        return jobs[0] if jobs else None
    cut = lambda seq, sizes: [seq[sum(sizes[:k]):sum(sizes[:k + 1])] for k in range(len(sizes))]
    n_in = [len(j.ins) for j in jobs]
    n_out = [len(j.out_shapes) for j in jobs]
    n_sem = [len(j.sems) for j in jobs]
    aliases = {}
    for k, j in enumerate(jobs):
        for a, b in j.aliases.items():
            aliases[sum(n_in[:k]) + a] = sum(n_out[:k]) + b

    def start(ins, outs, sems):
        return [j.start(i, o, s) for j, i, o, s in zip(jobs, cut(ins, n_in), cut(outs, n_out), cut(sems, n_sem))]

    def finish(sts, ins, outs, sems):
        for j, st, i, o, s in zip(jobs, sts, cut(ins, n_in), cut(outs, n_out), cut(sems, n_sem)):
            j.finish(st, i, o, s)

    return _Job([t for j in jobs for t in j.ins], [t for j in jobs for t in j.out_shapes], aliases,
                [t for j in jobs for t in j.sems], start, finish)


def _run_job(job, name):
    def body(ins, outs, scr, comm):
        comm[1](comm[0]())

    return _host_call(body, name, [], [], [], [], [], {}, job)[1]


def _allreduce_small(v):
    def body(v_ref, o_ref, slots, send_sems, recv_sems):
        x, y, c = _place()
        me = 4 * x + 2 * y + c
        slots[me] = v_ref[...]

        def copy(k):
            peer = (x ^ ((k >> 2) & 1), y ^ ((k >> 1) & 1), c ^ (k & 1))
            return pltpu.make_async_remote_copy(
                src_ref=v_ref, dst_ref=slots.at[me], send_sem=send_sems.at[k - 1], recv_sem=recv_sems.at[k - 1],
                device_id=peer, device_id_type=MESH)

        def arrival(k):
            return pltpu.make_async_remote_copy(
                src_ref=v_ref, dst_ref=slots.at[me ^ k], send_sem=send_sems.at[k - 1], recv_sem=recv_sems.at[k - 1],
                device_id=(x, y, c), device_id_type=MESH)

        sends = [copy(k) for k in range(1, NDEV)]
        for cp in sends:
            cp.start()
        for k in range(1, NDEV):
            arrival(k).wait_recv()
        for cp in sends:
            cp.wait_send()
        acc = slots[0]
        for d in range(1, NDEV):
            acc = acc + slots[d]
        o_ref[...] = acc

    return pl.pallas_call(
        body, name="allreduce_small", in_specs=[VMEM_SPEC], out_specs=VMEM_SPEC,
        out_shape=jax.ShapeDtypeStruct(v.shape, F32),
        scratch_shapes=[pltpu.VMEM((NDEV,) + v.shape, F32), pltpu.SemaphoreType.DMA((NDEV - 1,)),
                        pltpu.SemaphoreType.DMA((NDEV - 1,))],
        )(v)


def _pair_sums(views, gots, core):
    n = len(views)

    def body(c_ref, *refs):
        for a in range(n):
            refs[2 * n + a][...] = (refs[a][...] + refs[n + a][...]).astype(BF16)

    def vspec(v):
        return pl.BlockSpec((None, None) + v.shape[2:], lambda k, c: (k, c[0], 0, 0))

    def gspec(g):
        return pl.BlockSpec((None,) + g.shape[1:], lambda k, c: (k, 0, 0))

    return pl.pallas_call(
        body, name="pair_sums",
        grid_spec=pltpu.PrefetchScalarGridSpec(
            num_scalar_prefetch=1, grid=(4,),
            in_specs=[vspec(v) for v in views] + [gspec(g) for g in gots],
            out_specs=[gspec(g) for g in gots]),
        out_shape=[jax.ShapeDtypeStruct(g.shape, BF16) for g in gots],
        compiler_params=_cp(("parallel",)))(core, *views, *gots)


def _chip_sums(parts):
    n = len(parts)

    def body(*refs):
        for a in range(n):
            acc = refs[a][0].astype(F32)
            for k in range(1, 4):
                acc = acc + refs[a][k].astype(F32)
            refs[n + a][...] = acc

    return pl.pallas_call(
        body, name="chip_sums", in_specs=[VMEM_SPEC] * n, out_specs=[VMEM_SPEC] * n,
        out_shape=[jax.ShapeDtypeStruct(p.shape[1:], F32) for p in parts], compiler_params=_cp())(*parts)


def _permute_in(w):
    lead = w.shape[:-1]
    return w.reshape(lead + (3, 3, 2, BQ)).swapaxes(-2, -3).reshape(lead + (QKVW,))


def _unpermute_in(w):
    lead = w.shape[:-1]
    return w.reshape(lead + (3, 2, 3, BQ)).swapaxes(-2, -3).reshape(lead + (QKVW,))


def _row(v):
    v = v.reshape(-1)
    return jnp.pad(v, (0, D - v.shape[0])).reshape(1, D)


def kernel(x, w_in, f_bias, conv_w, w_out, rel_bias, ln1_g, ln1_b, w_gate, w_up, w_down, ln2_g, ln2_b, loss_target, m_w_in, m_f_bias, m_conv_w, m_w_out, m_rel_bias, m_ln1_g, m_ln1_b, m_w_gate, m_w_up, m_w_down, m_ln2_g, m_ln2_b, v_w_in, v_f_bias, v_conv_w, v_w_out, v_rel_bias, v_ln1_g, v_ln1_b, v_w_gate, v_w_up, v_w_down, v_ln2_g, v_ln2_b):
    xi, yi, ci = _place()
    me = 4 * xi + 2 * yi + ci
    core = jnp.reshape(ci, (1,)).astype(jnp.int32)

    win_s = jnp.concatenate([_permute_in(w_in[..., :QKVW]), w_in[..., QKVW:]], axis=-1)
    win_s = jnp.pad(win_s, ((0, 0), (0, 0), (0, NPAD - NPROJ))).astype(BF16)
    per_layer = [win_s, w_out.astype(BF16), jnp.swapaxes(w_gate, 1, 2).astype(BF16),
                 jnp.swapaxes(w_up, 1, 2).astype(BF16), w_down.astype(BF16)]
    sh = [[s[l] for s in per_layer] for l in range(2)]

    def whole(g):
        return g.reshape(NDEV * g.shape[1], g.shape[2])

    first = _run_job(_gather_job_b(_run_job(_gather_job_a(sh[0][:1]), "gather_a")), "gather_b")
    W = [{"win": whole(first[0])}, {}]

    cw_rows = lax.dynamic_update_slice(jnp.zeros((2, 3, 256), F32), conv_w, (0, 0, me * 32))
    small = jnp.concatenate([_row(cw_rows[0]), _row(cw_rows[1]), jnp.zeros((SMALL_ROWS - 2, D), F32)], axis=0)
    small = _allreduce_small(small)
    cw_full = small[0:2, :CONVW].reshape(2, 3, 256)
    cw8 = jnp.pad(cw_full, ((0, 0), (0, 5), (0, 0)))
    fb = jnp.pad(f_bias, ((0, 0), (0, GATEW - NH))).reshape(2, 1, GATEW)
    tbl = _dil_table(rel_bias)

    def wcol(K, tn, off):
        return pl.BlockSpec((K, tn), lambda i, j: (0, off + j))

    def wrow(tn, K, blk=0):
        return pl.BlockSpec((tn, K), lambda i, j: (j, blk))

    def arow(tm, K, blk=0):
        return pl.BlockSpec((tm, K), lambda i, j: (i, blk))

    h = x.reshape(T, D)
    hb = h.astype(BF16)
    saved = []
    for l in range(2):
        Win = W[l]["win"]
        qkv, conv, gate = _proj(hb, Win)
        cum = _fox_prep(gate, fb[l])
        cq = cum[:, :NH].reshape(BL, S, NH).transpose(0, 2, 1).reshape(NSTAT, S)
        ckb = jnp.broadcast_to(cq[:, :, None], (NSTAT, S, BQ))
        if l == 0:
            mixed, rtot, a0 = _sb_fwd(qkv, job=_gather_job_a(sh[0][1:]))
            mixed, lse_d, ex = _flash_fwd(qkv, mixed, 1, False, (tbl,),
                                          job=_join_jobs(_gather_job_b(list(a0)), _gather_job_a(sh[1][:2])))
            W[0].update(zip(("wout", "wgT", "wuT", "wd"), [whole(t) for t in ex[:4]]))
            mixed, lse_f, o_fox, ex = _flash_fwd(qkv, mixed, 2, True, (cq, ckb),
                                                 job=_join_jobs(_gather_job_b(list(ex[4:])), _gather_job_a(sh[1][2:])))
            W[1].update(zip(("win", "wout"), [whole(t) for t in ex[:2]]))
            a2 = list(ex[2:])
        else:
            mixed, rtot, ex = _sb_fwd(qkv, job=_gather_job_b(a2))
            W[1].update(zip(("wgT", "wuT", "wd"), [whole(t) for t in ex]))
            mixed, lse_d, _ = _flash_fwd(qkv, mixed, 1, False, (tbl,))
            mixed, lse_f, o_fox, _ = _flash_fwd(qkv, mixed, 2, True, (cq, ckb))
        Wout, WgT, WuT, Wd = W[l]["wout"], W[l]["wgT"], W[l]["wuT"], W[l]["wd"]
        mixed = _conv_fwd(conv, cw8[l], mixed)
        x1, xh1, r1, x1b = _mm_ln(mixed, Wout, h, ln1_g[l:l + 1], ln1_b[l:l + 1], "out_proj_ln")
        fs, ft, a, x2, xh2, r2, x2b = _ffn_fwd(x1b, x1, WgT, WuT, Wd, ln2_g[l:l + 1], ln2_b[l:l + 1])
        saved.append(dict(h=hb, qkv=qkv, conv=conv, gate=gate, cq=cq, ckb=ckb, mixed=mixed, rtot=rtot, lse_d=lse_d,
                          lse_f=lse_f, o_fox=o_fox, x1=x1b, xh1=xh1, r1=r1, fs=fs, ft=ft, a=a, xh2=xh2, r2=r2))
        h, hb = x2, x2b

    dy = h

    def view(gr):
        return gr.reshape(4, 2, gr.shape[0] // NDEV, gr.shape[1])

    G = [None, None]
    small_g = {}
    shard_g = {}
    for l in (1, 0):
        sv = saved[l]
        Win, Wout, WgT, WuT, Wd = W[l]["win"], W[l]["wout"], W[l]["wgT"], W[l]["wuT"], W[l]["wd"]
        res = _ffn_bwd(dy, sv["xh2"], sv["r2"], ln2_g[l:l + 1], sv["fs"], sv["ft"], Wd, WgT, WuT,
                       target=loss_target.reshape(T, D) if l == 1 else None)
        dgt, dut, ds2b, dx1, dg2, db2 = res[:6]
        if l == 1:
            sq = res[6]
        G_d = _mm_tn(sv["a"], ds2b, None, C=D, Ka=DFF, N=D, tm=256, tn=1024, tk=T, ooff=0, name="grad_w_down")
        G_g = _mm_tn(dgt, sv["x1"], None, C=D, Ka=DFF, N=D, tm=256, tn=1024, tk=T, ooff=0, name="grad_w_gate")
        G_u = _mm_tn(dut, sv["x1"], None, C=D, Ka=DFF, N=D, tm=256, tn=1024, tk=T, ooff=0, name="grad_w_up")
        ds1, dg1, db1, ds1b, dmixed = _ln_bwd(dx1, sv["xh1"], sv["r1"], ln1_g[l:l + 1], Wout)
        G_out = _mm_tn(sv["mixed"], ds1b, None, C=D, Ka=D, N=D, tm=256, tn=1024, tk=T, ooff=0, name="grad_w_out")
        early = [view(t) for t in (G_g, G_u, G_d, G_out)] + ([view(G[1]["in"])] if l == 0 else [])
        dqkv, gots = _sb_bwd(sv["qkv"], dmixed, sv["rtot"], job=_sibling_job(early))
        ps = _pair_sums(early, list(gots), core)
        dqkv, dtbl, pa = _flash_bwd(sv["qkv"], sv["mixed"], dmixed, sv["lse_d"], dqkv, 1, False, (tbl,),
                                    job=_chip_job(ps[:2]))
        dqkv, dck, pb = _flash_bwd(sv["qkv"], sv["o_fox"], dmixed, sv["lse_f"], dqkv, 2, True,
                                   (sv["cq"], sv["ckb"]), job=_chip_job(ps[2:]))
        sums = _chip_sums(list(pa) + list(pb))
        shard_g[l] = dict(zip(("g", "u", "d", "out"), sums[:4]))
        if l == 0:
            shard_g[1]["in"] = sums[4]
        dconv, dcw = _conv_bwd(sv["conv"], cw8[l], dmixed)
        dcum = jnp.pad(dck.reshape(S, BL, NH).transpose(1, 0, 2).reshape(T, NH), ((0, 0), (0, GATEW - NH)))
        dgate, dfb = _fox_post(dcum, sv["gate"], fb[l])
        drb = _dil_table_bwd(dtbl)
        G_in = _mm_tn(sv["h"], dqkv, None, C=NPAD, Ka=D, N=QKVW, tm=512, tn=768, tk=T, ooff=0, name="grad_w_in_qkv")
        G_in = _mm_tn(sv["h"], dconv, G_in, C=NPAD, Ka=D, N=CONVW, tm=256, tn=768, tk=T, ooff=3,
                      name="grad_w_in_conv")
        G_in = _mm_tn(sv["h"], dgate, G_in, C=NPAD, Ka=D, N=GATEW, tm=1024, tn=128, tk=1024, ooff=24,
                      name="grad_w_in_gate")
        G[l] = {"in": G_in, "out": G_out, "g": G_g, "u": G_u, "d": G_d}
        tail = None
        if l == 0:
            late = [view(G_in)]
            tail = _chip_job(_pair_sums(late, list(_run_job(_sibling_job(late), "sibling_exchange")), core))
        dy = _mm([(dqkv, arow(1024, QKVW), Win, wrow(512, QKVW, 0)),
                  (dconv, arow(1024, CONVW), Win, wrow(512, CONVW, 3)),
                  (dgate, arow(1024, GATEW), Win, wrow(512, GATEW, 24))],
                 nt=True, M=T, N=D, tm=1024, tn=512, out_dtype=F32, name="proj_dx", res=ds1, res_scale=ALPHA, job=tail)
        if l == 0:
            dy, parts = dy
            shard_g[0]["in"] = _chip_sums(list(parts))[0]
        small_g[l] = dict(ln1_g=dg1, ln1_b=db1, ln2_g=dg2, ln2_b=db2, cw=dcw[0:3].reshape(1, CONVW),
                          fb=dfb[:, :NH], rb=drb[:, :NH])
    grad_x = dy.reshape(BL, S, D)

    rows = []
    for name in ("ln1_g", "ln1_b", "ln2_g", "ln2_b"):
        rows += [small_g[0][name], small_g[1][name]]
    rows += [_row(small_g[0]["cw"]), _row(small_g[1]["cw"]),
             _row(jnp.concatenate([small_g[0]["fb"], small_g[1]["fb"]], axis=0)),
             _row(small_g[0]["rb"] + small_g[1]["rb"]), _row(sq)]
    rows.append(jnp.zeros((SMALL_ROWS - len(rows), D), F32))
    sg = _allreduce_small(jnp.concatenate(rows, axis=0))
    loss = sg[12, 0] * (0.5 / D)
    g_ln1_g, g_ln1_b, g_ln2_g, g_ln2_b = sg[0:2], sg[2:4], sg[4:6], sg[6:8]
    g_conv_full = sg[8:10, :CONVW].reshape(2, 3, 256)
    g_conv = lax.dynamic_slice(g_conv_full, (0, 0, me * 32), (2, 3, 32))
    g_fb = sg[10, :2 * NH].reshape(2, NH)
    g_rb = sg[11, :32 * NH].reshape(32, NH)

    def both(name):
        return jnp.stack([shard_g[0][name], shard_g[1][name]])

    g_in = both("in")
    g_w_in = jnp.concatenate([_unpermute_in(g_in[..., :QKVW]), g_in[..., QKVW:NPROJ]], axis=-1)
    g_w_out = both("out")
    g_w_gate = jnp.swapaxes(both("g"), 1, 2)
    g_w_up = jnp.swapaxes(both("u"), 1, 2)
    g_w_down = both("d")

    up_in = _adamw(w_in, g_w_in, m_w_in, v_w_in, 64)
    up_out = _adamw(w_out, g_w_out, m_w_out, v_w_out, 128)
    up_gate = _adamw(w_gate, g_w_gate, m_w_gate, v_w_gate, 256)
    up_up = _adamw(w_up, g_w_up, m_w_up, v_w_up, 256)
    up_down = _adamw(w_down, g_w_down, m_w_down, v_w_down, 352)

    def pack(fbv, cwv, rbv, l1g, l1b, l2g, l2b):
        r = [l1g, l1b, l2g, l2b, _row(cwv), _row(fbv), _row(rbv)]
        r.append(jnp.zeros((SMALL_ROWS - 11, D), F32))
        return jnp.concatenate(r, axis=0)

    pw = pack(f_bias, conv_w, rel_bias, ln1_g, ln1_b, ln2_g, ln2_b)
    pg = pack(g_fb, g_conv, g_rb, g_ln1_g, g_ln1_b, g_ln2_g, g_ln2_b)
    pm = pack(m_f_bias, m_conv_w, m_rel_bias, m_ln1_g, m_ln1_b, m_ln2_g, m_ln2_b)
    pv = pack(v_f_bias, v_conv_w, v_rel_bias, v_ln1_g, v_ln1_b, v_ln2_g, v_ln2_b)
    ups = [u[0] for u in _adamw(pw[None], pg[None], pm[None], pv[None], SMALL_ROWS)]

    def unpack(p):
        return dict(ln1_g=p[0:2], ln1_b=p[2:4], ln2_g=p[4:6], ln2_b=p[6:8],
                    conv_w=p[8, :192].reshape(2, 3, 32), f_bias=p[9, :2 * NH].reshape(2, NH),
                    rel_bias=p[10, :32 * NH].reshape(32, NH))

    sm = [unpack(p) for p in ups]

    def group(k):
        return (up_in[k], sm[k]["f_bias"], sm[k]["conv_w"], up_out[k], sm[k]["rel_bias"], sm[k]["ln1_g"],
                sm[k]["ln1_b"], up_gate[k], up_up[k], up_down[k], sm[k]["ln2_g"], sm[k]["ln2_b"])

    grads = (g_w_in, g_fb, g_conv, g_w_out, g_rb, g_ln1_g, g_ln1_b, g_w_gate, g_w_up, g_w_down, g_ln2_g, g_ln2_b)
    return (loss, grad_x) + grads + group(0) + group(1) + group(2)
```

```python
import math

import numpy as np
import jax
import jax.numpy as jnp
from jax import lax
from jax.experimental import pallas as pl
from jax.experimental.pallas import tpu as pltpu

F32 = jnp.float32
BF16 = jnp.bfloat16
MESH = pl.DeviceIdType.MESH

D = 1024
S = 2048
BL = 2
T = BL * S
NH = 4
DFF = 2816
NPROJ = 3076
NPAD = 3200
QKVW = 2304
CONVW = 768
GATEW = 128
PAIRW = 384
BQ = 128
HB = 2 * BQ
NB = S // BQ
NDEV = 8
NSTAT = BL * NH
ALPHA = 4.0 ** 0.25
SCALE = 0.125
NEG = -1e30
LN_EPS = 1e-5
ADAM_LR, ADAM_B1, ADAM_B2, ADAM_EPS, ADAM_WD, ADAM_STEP = 0.001, 0.9, 0.999, 1e-08, 0.01, 10
VMEM_LIMIT = 56 * 1024 * 1024
SMALL_ROWS = 16


def _bucket_thresholds():
    d = np.arange(0, S)
    nf = np.maximum(d, 1).astype(np.float32)
    large = 16 + (np.log(nf / np.float32(16)) / np.float32(math.log(128)) * np.float32(16)).astype(np.int32)
    b = np.where(d < 16, d, np.minimum(large, 31))
    return [int(np.argmax(b >= k)) for k in range(32)]


BUCKET_TH = _bucket_thresholds()


def _cp(sem=None, vmem=VMEM_LIMIT):
    return pltpu.CompilerParams(dimension_semantics=sem, vmem_limit_bytes=vmem)


def _dot(a, b):
    return lax.dot_general(a, b, (((1,), (0,)), ((), ())), preferred_element_type=F32)


def _dot_nt(a, b):
    return lax.dot_general(a, b, (((1,), (1,)), ((), ())), preferred_element_type=F32)


def _dot_tn(a, b):
    return lax.dot_general(a, b, (((0,), (0,)), ((), ())), preferred_element_type=F32)


def _split2(x):
    hi = x.astype(BF16)
    mid = (x - hi.astype(F32)).astype(BF16)
    return jnp.concatenate([hi, mid], axis=1)


def _split3(x):
    hi = x.astype(BF16)
    r = x - hi.astype(F32)
    mid = r.astype(BF16)
    lo = (r - mid.astype(F32)).astype(BF16)
    return jnp.concatenate([hi, mid, lo], axis=1)


def _log_sigmoid(u):
    return jnp.minimum(u, 0.0) - jnp.log1p(jnp.exp(-jnp.abs(u)))


def _log_sigmoid_tile(u):
    return jnp.minimum(u, 0.0) - jnp.log(1.0 + jnp.exp(jnp.minimum(u, -u)))


def _iota(shape, dim):
    return lax.broadcasted_iota(jnp.int32, shape, dim)


ANY_SPEC = pl.BlockSpec(memory_space=pl.ANY)
VMEM_SPEC = pl.BlockSpec(memory_space=pltpu.VMEM)


def _mm(pairs, *, nt, M, N, tm, tn, out_dtype, name, res=None, res_scale=1.0, job=None):
    n = len(pairs)
    n_in = 2 * n + (res is not None)
    jins = job.ins if job else []
    jouts = job.out_shapes if job else []
    gi, gj = M // tm, N // tn

    def body(*refs):
        o_ref = refs[n_in + len(jins)]
        if job:
            jrefs = (refs[n_in:n_in + len(jins)], refs[n_in + len(jins) + 1:n_in + len(jins) + 1 + len(jouts)],
                     refs[n_in + len(jins) + 1 + len(jouts):])

            @pl.when((pl.program_id(0) == 0) & (pl.program_id(1) == 0))
            def _():
                job.start(*jrefs)

        acc = None
        for p in range(n):
            a = refs[2 * p][...].astype(BF16)
            b = refs[2 * p + 1][...]
            d = _dot_nt(a, b) if nt else _dot(a, b)
            acc = d if acc is None else acc + d
        if res is not None:
            acc = acc + res_scale * refs[2 * n][...]
        o_ref[...] = acc.astype(out_dtype)
        if job:
            @pl.when((pl.program_id(0) == gi - 1) & (pl.program_id(1) == gj - 1))
            def _():
                job.finish(None, *jrefs)

    ops, specs = [], []
    for a, asp, b, bsp in pairs:
        ops += [a, b]
        specs += [asp, bsp]
    if res is not None:
        ops.append(res)
        specs.append(pl.BlockSpec((tm, tn), lambda i, j: (i, j)))
    out = pl.pallas_call(
        body, name=name, grid=(gi, gj), in_specs=specs + [ANY_SPEC] * len(jins),
        out_specs=[pl.BlockSpec((tm, tn), lambda i, j: (i, j))] + [ANY_SPEC] * len(jouts),
        out_shape=[jax.ShapeDtypeStruct((M, N), out_dtype)] + list(jouts),
        scratch_shapes=list(job.sems) if job else [],
        input_output_aliases={n_in + a: 1 + b for a, b in job.aliases.items()} if job else {},
        compiler_params=_cp(("arbitrary", "arbitrary") if job else ("parallel", "parallel")))(*ops, *jins)
    return (out[0], out[1:]) if job else out[0]


def _mm_tn(a, b, gbuf, *, C, Ka, N, tm, tn, tk, ooff, name):
    def body(*refs):
        a_ref, b_ref, o_ref = refs[0], refs[1], refs[-1]
        k = pl.program_id(2)
        d = _dot_tn(a_ref[...].astype(BF16), b_ref[...].astype(BF16))

        @pl.when(k == 0)
        def _():
            o_ref[...] = d

        @pl.when(k > 0)
        def _():
            o_ref[...] += d

    ops = [a, b] + ([] if gbuf is None else [gbuf])
    return pl.pallas_call(
        body, name=name, grid=(Ka // tm, N // tn, T // tk),
        in_specs=[pl.BlockSpec((tk, tm), lambda i, j, k: (k, i)),
                  pl.BlockSpec((tk, tn), lambda i, j, k: (k, j))] + ([] if gbuf is None else [ANY_SPEC]),
        out_specs=pl.BlockSpec((tm, tn), lambda i, j, k: (i, ooff + j)),
        out_shape=jax.ShapeDtypeStruct((Ka, C), F32),
        input_output_aliases={} if gbuf is None else {2: 0},
        compiler_params=_cp(("parallel", "parallel", "arbitrary")))(*ops)


def _ffn_up(x1, wgt, wut):
    tm, tn = 1024, 256

    def body(x_ref, wg_ref, wu_ref, g_ref, u_ref, a_ref):
        ch = 256
        for r in range(0, tm, ch):
            xb = x_ref[r:r + ch, :]
            g = _dot_nt(xb, wg_ref[...])
            u = _dot_nt(xb, wu_ref[...])
            g_ref[r:r + ch, :] = g.astype(BF16)
            u_ref[r:r + ch, :] = u.astype(BF16)
            a_ref[r:r + ch, :] = (g * jax.nn.sigmoid(g) * u).astype(BF16)

    wspec = pl.BlockSpec((tn, D), lambda i, j: (j, 0))
    ospec = pl.BlockSpec((tm, tn), lambda i, j: (i, j))
    return pl.pallas_call(
        body, name="ffn_up", grid=(T // tm, DFF // tn),
        in_specs=[pl.BlockSpec((tm, D), lambda i, j: (i, 0)), wspec, wspec],
        out_specs=[ospec, ospec, ospec],
        out_shape=[jax.ShapeDtypeStruct((T, DFF), BF16)] * 3,
        compiler_params=_cp(("parallel", "parallel")))(x1, wgt, wut)


def _ffn_da(dffn, wd, s, t):
    tm, tn = 1024, 256

    def body(d_ref, wd_ref, s_ref, t_ref, dg_ref, du_ref):
        ch = 256
        for r in range(0, tm, ch):
            da = _dot_nt(d_ref[r:r + ch, :], wd_ref[...])
            gv = s_ref[r:r + ch, :].astype(F32)
            sg = jax.nn.sigmoid(gv)
            dg_ref[r:r + ch, :] = (da * t_ref[r:r + ch, :].astype(F32) * (sg * (1.0 + gv * (1.0 - sg)))).astype(BF16)
            du_ref[r:r + ch, :] = (da * (gv * sg)).astype(BF16)

    ospec = pl.BlockSpec((tm, tn), lambda i, j: (i, j))
    return pl.pallas_call(
        body, name="ffn_da", grid=(T // tm, DFF // tn),
        in_specs=[pl.BlockSpec((tm, D), lambda i, j: (i, 0)),
                  pl.BlockSpec((tn, D), lambda i, j: (j, 0)), ospec, ospec],
        out_specs=[ospec, ospec],
        out_shape=[jax.ShapeDtypeStruct((T, DFF), BF16), jax.ShapeDtypeStruct((T, DFF), BF16)],
        compiler_params=_cp(("parallel", "parallel")))(dffn, wd, s, t)


def _proj(xb, w):
    tm = 512

    def body(x_ref, w_ref, qkv_ref, conv_ref, gate_ref):
        xv = x_ref[...]
        qkv_ref[...] = _dot(xv, w_ref[:, 0:QKVW]).astype(BF16)
        conv_ref[...] = _dot(xv, w_ref[:, QKVW:QKVW + CONVW])
        gate_ref[...] = _dot(xv, w_ref[:, QKVW + CONVW:NPAD])

    def rows(n):
        return pl.BlockSpec((tm, n), lambda i: (i, 0))

    return pl.pallas_call(
        body, name="proj", grid=(T // tm,),
        in_specs=[rows(D), pl.BlockSpec((D, NPAD), lambda i: (0, 0))],
        out_specs=[rows(QKVW), rows(CONVW), rows(GATEW)],
        out_shape=[jax.ShapeDtypeStruct((T, QKVW), BF16), jax.ShapeDtypeStruct((T, CONVW), F32),
                   jax.ShapeDtypeStruct((T, GATEW), F32)],
        compiler_params=_cp(("parallel",)))(xb, w)


def _ffn_fwd(xb, x, wgt, wut, wd, gam, bet):
    tm, ch = 512, 256

    def body(xb_ref, x_ref, g_ref, b_ref, wg_hbm, wu_hbm, wd_hbm,
             go_ref, uo_ref, ao_ref, y_ref, xh_ref, r_ref, yb_ref, wg_v, wu_v, wd_v, sem):
        loads = [pltpu.make_async_copy(s, d, sem.at[k])
                 for k, (s, d) in enumerate(((wg_hbm, wg_v), (wu_hbm, wu_v), (wd_hbm, wd_v)))]

        @pl.when(pl.program_id(0) == 0)
        def _():
            for cp in loads:
                cp.start()
            loads[0].wait()
            loads[1].wait()

        xv = xb_ref[...]
        for c in range(0, DFF, ch):
            gv = _dot_nt(xv, wg_v[c:c + ch, :])
            uv = _dot_nt(xv, wu_v[c:c + ch, :])
            go_ref[:, c:c + ch] = gv.astype(BF16)
            uo_ref[:, c:c + ch] = uv.astype(BF16)
            ao_ref[:, c:c + ch] = (gv * jax.nn.sigmoid(gv) * uv).astype(BF16)
        @pl.when(pl.program_id(0) == 0)
        def _():
            loads[2].wait()

        s = ALPHA * x_ref[...] + _dot(ao_ref[...], wd_v[...])
        mu = jnp.mean(s, axis=-1, keepdims=True)
        xc = s - mu
        var = jnp.mean(xc * xc, axis=-1, keepdims=True)
        r = lax.rsqrt(var + LN_EPS)
        xh = xc * r
        xh_ref[...] = xh.astype(BF16)
        r_ref[...] = r
        y = xh * g_ref[...] + b_ref[...]
        y_ref[...] = y
        yb_ref[...] = y.astype(BF16)

    row = pl.BlockSpec((tm, D), lambda i: (i, 0))
    wide = pl.BlockSpec((tm, DFF), lambda i: (i, 0))
    vec = pl.BlockSpec((1, D), lambda i: (0, 0))
    wsl = pltpu.VMEM((DFF, D), BF16)
    hid = jax.ShapeDtypeStruct((T, DFF), BF16)
    return pl.pallas_call(
        body, name="ffn_fwd", grid=(T // tm,),
        in_specs=[row, row, vec, vec, ANY_SPEC, ANY_SPEC, ANY_SPEC],
        out_specs=[wide, wide, wide, row, row, pl.BlockSpec((tm, 1), lambda i: (i, 0)), row],
        out_shape=[hid, hid, hid, jax.ShapeDtypeStruct((T, D), F32), jax.ShapeDtypeStruct((T, D), BF16),
                   jax.ShapeDtypeStruct((T, 1), F32), jax.ShapeDtypeStruct((T, D), BF16)],
        scratch_shapes=[wsl, wsl, wsl, pltpu.SemaphoreType.DMA((3,))],
        compiler_params=_cp(("arbitrary",)))(xb, x, gam, bet, wgt, wut, wd)


def _ffn_bwd(dy, xh, r, gam, g, u, wd, wgt, wut, target=None):
    tm, ch = 256, 256

    def body(*refs):
        if target is None:
            (dy_ref, xh_ref, r_ref, gam_ref, g_ref, u_ref, wd_hbm, wg_hbm, wu_hbm,
             dg_ref, du_ref, dsb_ref, dx_ref, dgam_ref, dbet_ref, wd_v, wg_v, wu_v, sem) = refs
        else:
            (dy_ref, t_ref, xh_ref, r_ref, gam_ref, g_ref, u_ref, wd_hbm, wg_hbm, wu_hbm,
             dg_ref, du_ref, dsb_ref, dx_ref, dgam_ref, dbet_ref, sq_ref, wd_v, wg_v, wu_v, sem) = refs
        loads = [pltpu.make_async_copy(s, d, sem.at[k])
                 for k, (s, d) in enumerate(((wd_hbm, wd_v), (wg_hbm, wg_v), (wu_hbm, wu_v)))]

        @pl.when(pl.program_id(0) == 0)
        def _():
            for cp in loads:
                cp.start()
            loads[0].wait()

        if target is None:
            dyv = dy_ref[...]
        else:
            e = dy_ref[...] - t_ref[...]
            dyv = e * (1.0 / D)
            p = jnp.sum(jnp.sum(e * e, axis=1, keepdims=True), axis=0, keepdims=True)

            @pl.when(pl.program_id(0) == 0)
            def _():
                sq_ref[...] = p

            @pl.when(pl.program_id(0) > 0)
            def _():
                sq_ref[...] += p

        xhv = xh_ref[...].astype(F32)
        dxh = dyv * gam_ref[...]
        m1 = jnp.mean(dxh, axis=-1, keepdims=True)
        m2 = jnp.mean(dxh * xhv, axis=-1, keepdims=True)
        ds = r_ref[...] * (dxh - m1 - xhv * m2)
        pg = jnp.sum(dyv * xhv, axis=0, keepdims=True)
        pb = jnp.sum(dyv, axis=0, keepdims=True)

        @pl.when(pl.program_id(0) == 0)
        def _():
            dgam_ref[...] = pg
            dbet_ref[...] = pb

        @pl.when(pl.program_id(0) > 0)
        def _():
            dgam_ref[...] += pg
            dbet_ref[...] += pb

        db = ds.astype(BF16)
        dsb_ref[...] = db
        for c in range(0, DFF, ch):
            da = _dot_nt(db, wd_v[c:c + ch, :])
            gv = g_ref[:, c:c + ch].astype(F32)
            sg = jax.nn.sigmoid(gv)
            dg_ref[:, c:c + ch] = (da * u_ref[:, c:c + ch].astype(F32) * (sg * (1.0 + gv * (1.0 - sg)))).astype(BF16)
            du_ref[:, c:c + ch] = (da * (gv * sg)).astype(BF16)
        @pl.when(pl.program_id(0) == 0)
        def _():
            loads[1].wait()
            loads[2].wait()

        dx_ref[...] = ALPHA * ds + _dot(dg_ref[...], wg_v[...]) + _dot(du_ref[...], wu_v[...])

    row = pl.BlockSpec((tm, D), lambda i: (i, 0))
    wide = pl.BlockSpec((tm, DFF), lambda i: (i, 0))
    vec = pl.BlockSpec((1, D), lambda i: (0, 0))
    wsl = pltpu.VMEM((DFF, D), BF16)
    last = target is not None
    return pl.pallas_call(
        body, name="ffn_bwd_loss" if last else "ffn_bwd", grid=(T // tm,),
        in_specs=[row] + ([row] if last else [])
        + [row, pl.BlockSpec((tm, 1), lambda i: (i, 0)), vec, wide, wide, ANY_SPEC, ANY_SPEC, ANY_SPEC],
        out_specs=[wide, wide, row, row, vec, vec] + ([pl.BlockSpec((1, 1), lambda i: (0, 0))] if last else []),
        out_shape=[jax.ShapeDtypeStruct((T, DFF), BF16), jax.ShapeDtypeStruct((T, DFF), BF16),
                   jax.ShapeDtypeStruct((T, D), BF16), jax.ShapeDtypeStruct((T, D), F32),
                   jax.ShapeDtypeStruct((1, D), F32), jax.ShapeDtypeStruct((1, D), F32)]
        + ([jax.ShapeDtypeStruct((1, 1), F32)] if last else []),
        scratch_shapes=[wsl, wsl, wsl, pltpu.SemaphoreType.DMA((3,))],
        compiler_params=_cp(("arbitrary",)))(dy, *([target] if last else []), xh, r, gam, g, u, wd, wgt, wut)


def _mm_ln(a, w, x, gam, bet, name):
    tm = 256
    K = a.shape[1]

    def body(a_ref, w_ref, x_ref, g_ref, b_ref, y_ref, xh_ref, r_ref, yb_ref):
        s = ALPHA * x_ref[...] + _dot(a_ref[...], w_ref[...])
        mu = jnp.mean(s, axis=-1, keepdims=True)
        xc = s - mu
        var = jnp.mean(xc * xc, axis=-1, keepdims=True)
        r = lax.rsqrt(var + LN_EPS)
        xh = xc * r
        xh_ref[...] = xh.astype(BF16)
        r_ref[...] = r
        y = xh * g_ref[...] + b_ref[...]
        y_ref[...] = y
        yb_ref[...] = y.astype(BF16)

    row = pl.BlockSpec((tm, D), lambda i: (i, 0))
    vec = pl.BlockSpec((1, D), lambda i: (0, 0))
    return pl.pallas_call(
        body, name=name, grid=(T // tm,),
        in_specs=[pl.BlockSpec((tm, K), lambda i: (i, 0)), pl.BlockSpec((K, D), lambda i: (0, 0)), row, vec, vec],
        out_specs=[row, row, pl.BlockSpec((tm, 1), lambda i: (i, 0)), row],
        out_shape=[jax.ShapeDtypeStruct((T, D), F32), jax.ShapeDtypeStruct((T, D), BF16),
                   jax.ShapeDtypeStruct((T, 1), F32), jax.ShapeDtypeStruct((T, D), BF16)],
        compiler_params=_cp(("parallel",)))(a, w, x, gam, bet)


def _ln_bwd(dy, xh, r, gam, w):
    tm = 256

    def body(dy_ref, xh_ref, r_ref, g_ref, w_ref, ds_ref, dg_ref, db_ref, dsb_ref, dm_ref):
        i = pl.program_id(0)
        dyv = dy_ref[...]
        xhv = xh_ref[...].astype(F32)
        dxh = dyv * g_ref[...]
        m1 = jnp.mean(dxh, axis=-1, keepdims=True)
        m2 = jnp.mean(dxh * xhv, axis=-1, keepdims=True)
        ds = r_ref[...] * (dxh - m1 - xhv * m2)
        ds_ref[...] = ds
        dsb = ds.astype(BF16)
        dsb_ref[...] = dsb
        dm_ref[...] = _dot_nt(dsb, w_ref[...]).astype(BF16)
        pg = jnp.sum(dyv * xhv, axis=0, keepdims=True)
        pb = jnp.sum(dyv, axis=0, keepdims=True)

        @pl.when(i == 0)
        def _():
            dg_ref[...] = pg
            db_ref[...] = pb

        @pl.when(i > 0)
        def _():
            dg_ref[...] += pg
            db_ref[...] += pb

    row = pl.BlockSpec((tm, D), lambda i: (i, 0))
    vec = pl.BlockSpec((1, D), lambda i: (0, 0))
    return pl.pallas_call(
        body, name="ln_bwd_proj", grid=(T // tm,),
        in_specs=[row, row, pl.BlockSpec((tm, 1), lambda i: (i, 0)), vec, pl.BlockSpec((D, D), lambda i: (0, 0))],
        out_specs=[row, vec, vec, row, row],
        out_shape=[jax.ShapeDtypeStruct((T, D), F32), jax.ShapeDtypeStruct((1, D), F32),
                   jax.ShapeDtypeStruct((1, D), F32), jax.ShapeDtypeStruct((T, D), BF16),
                   jax.ShapeDtypeStruct((T, D), BF16)],
        compiler_params=_cp(("arbitrary",)))(dy, xh, r, gam, w)


def _loss_grad(y, tgt):
    tm = 256

    def body(y_ref, t_ref, l_ref, dy_ref):
        i = pl.program_id(0)
        e = y_ref[...] - t_ref[...]
        dy_ref[...] = e * (1.0 / D)
        p = jnp.sum(jnp.sum(e * e, axis=1, keepdims=True), axis=0, keepdims=True)

        @pl.when(i == 0)
        def _():
            l_ref[...] = p

        @pl.when(i > 0)
        def _():
            l_ref[...] += p

    row = pl.BlockSpec((tm, D), lambda i: (i, 0))
    return pl.pallas_call(
        body, name="loss_grad", grid=(T // tm,), in_specs=[row, row],
        out_specs=[pl.BlockSpec((1, 1), lambda i: (0, 0)), row],
        out_shape=[jax.ShapeDtypeStruct((1, 1), F32), jax.ShapeDtypeStruct((T, D), F32)],
        compiler_params=_cp(("arbitrary",)))(y, tgt)


def _adamw(w, g, m, v, tr):
    L, R, C = w.shape

    def body(w_ref, g_ref, m_ref, v_ref, d_ref, m2_ref, v2_ref):
        gv = g_ref[...]
        m2 = ADAM_B1 * m_ref[...] + (1.0 - ADAM_B1) * gv
        v2 = ADAM_B2 * v_ref[...] + (1.0 - ADAM_B2) * (gv * gv)
        m_hat = m2 / (1.0 - ADAM_B1 ** ADAM_STEP)
        v_hat = v2 / (1.0 - ADAM_B2 ** ADAM_STEP)
        d_ref[...] = -ADAM_LR * (m_hat / (jnp.sqrt(v_hat) + ADAM_EPS) + ADAM_WD * w_ref[...])
        m2_ref[...] = m2
        v2_ref[...] = v2

    blk = pl.BlockSpec((None, tr, C), lambda l, i: (l, i, 0))
    sh = jax.ShapeDtypeStruct((L, R, C), F32)
    return pl.pallas_call(
        body, name="adamw", grid=(L, R // tr), in_specs=[blk] * 4, out_specs=[blk] * 3,
        out_shape=[sh, sh, sh], compiler_params=_cp(("parallel", "parallel")))(w, g, m, v)


class _Job:
    def __init__(self, ins, out_shapes, aliases, sems, start, finish):
        self.ins, self.out_shapes, self.aliases, self.sems = list(ins), list(out_shapes), dict(aliases), list(sems)
        self.start, self.finish = start, finish


def _host_call(body, name, ins, in_specs, out_shapes, out_specs, scratch, aliases, job):
    n_in, n_out, n_scr = len(ins), len(out_shapes), len(scratch)
    jins = job.ins if job else []
    jouts = job.out_shapes if job else []
    jsems = job.sems if job else []

    def wrapped(*refs):
        a = n_in
        b = a + len(jins)
        c = b + n_out
        d = c + len(jouts)
        e = d + n_scr
        comm = None
        if job:
            jrefs = (refs[a:b], refs[c:d], refs[e:])
            comm = (lambda: job.start(*jrefs), lambda st: job.finish(st, *jrefs))
        body(refs[:a], refs[b:c], refs[d:e], comm)

    al = dict(aliases)
    if job:
        for ji, jo in job.aliases.items():
            al[n_in + ji] = n_out + jo
    res = pl.pallas_call(
        wrapped, name=name, in_specs=list(in_specs) + [ANY_SPEC] * len(jins),
        out_specs=list(out_specs) + [ANY_SPEC] * len(jouts), out_shape=list(out_shapes) + list(jouts),
        scratch_shapes=list(scratch) + list(jsems), input_output_aliases=al,
        compiler_params=_cp())(*ins, *jins)
    return res[:n_out], res[n_out:]


def _copy_in(src, dst, sem):
    cp = pltpu.make_async_copy(src, dst, sem)
    cp.start()
    cp.wait()


CHAINS = [(p, b) for p in range(2) for b in range(BL)]
NC = len(CHAINS)
ROWS_SHAPE = jax.ShapeDtypeStruct((NSTAT, S), F32)
SLAB_QKV = pltpu.VMEM((T, 2 * PAIRW), BF16)
SLAB_OUT = pltpu.VMEM((T, 2 * BQ), BF16)
SLAB_O32 = pltpu.VMEM((T, 2 * BQ), F32)
SLAB_T = pltpu.VMEM((2, BQ, T), BF16)
SLAB_KEYB = pltpu.VMEM((NSTAT, S, BQ), F32)
ACC_KV = pltpu.VMEM((2, T, BQ), F32)


def _lane_masks():
    lane = _iota((1, BQ), 1)
    m0 = (lane < 64).astype(BF16)
    return m0, 1.0 - m0


def _row_masks():
    r = _iota((BQ, 1), 0)
    m0 = (r < 64).astype(BF16)
    return m0, 1.0 - m0


def _stack(x, m0, m1):
    return jnp.concatenate([x * m0, x * m1], axis=0)


def _stack_t(xt, r0, r1):
    return jnp.concatenate([xt * r0, xt * r1], axis=1)


def _tr(x):
    return x.T


def _rows(b, i):
    return pl.ds(pl.multiple_of(b * S + i * BQ, BQ), BQ)


def _transpose_slab(src, dst, col0):
    def blk(n, _):
        r = pl.ds(pl.multiple_of(n * BQ, BQ), BQ)
        for p in range(2):
            dst[p, :, r] = _tr(src[r, col0(p):col0(p) + BQ])
        return 0

    lax.fori_loop(0, T // BQ, blk, 0)


def _heads(x):
    return x[:BQ], x[BQ:]


def _bcast_heads(r0, r1):
    return jnp.concatenate([jnp.broadcast_to(r0, (BQ, BQ)), jnp.broadcast_to(r1, (BQ, BQ))], axis=0)


def _by_channel(r0, r1):
    return jnp.where(_iota((BQ, BQ), 0) < 64, r0, r1)


def _colsum2(x):
    return jnp.sum(x[:BQ], axis=0, keepdims=True), jnp.sum(x[BQ:], axis=0, keepdims=True)


def _stat_row(ref, p, b, h, i):
    c = b * NH + 2 * p + h
    return ref[c:c + 1, pl.ds(pl.multiple_of(i * BQ, BQ), BQ)]


def _put_row(ref, p, b, h, i, v):
    c = b * NH + 2 * p + h
    ref[c:c + 1, pl.ds(pl.multiple_of(i * BQ, BQ), BQ)] = v


def _valid_t(strict):
    r = _iota((HB, BQ), 0) & (BQ - 1)
    c = _iota((HB, BQ), 1)
    return (r < c) if strict else (r <= c)


def _tri_blockdiag(later):
    r = _iota((HB, HB), 0)
    c = _iota((HB, HB), 1)
    same = (r >= BQ) == (c >= BQ)
    return (same & ((c > r) if later else (c < r))).astype(BF16)


def _cum_mm(tri, x):
    y = _dot(tri, _split2(x))
    return y[:, :BQ] + y[:, BQ:]


def _kv_tiles(qkv_v, p, b, j):
    r = _rows(b, j)
    return qkv_v[r, p * PAIRW + BQ:p * PAIRW + 2 * BQ], qkv_v[r, p * PAIRW + 2 * BQ:p * PAIRW + 3 * BQ]


def _q_tile(qkv_v, p, b, i):
    return qkv_v[_rows(b, i), p * PAIRW:p * PAIRW + BQ] * SCALE


def _sb_fwd(qkv, job=None):
    def body(ins, outs, scr, comm):
        (qkv_hbm,), (o_hbm, r_ref), (qkv_v, o_v, sem, vt_v) = ins, outs, scr
        _copy_in(qkv_hbm.at[:, pl.ds(0, 2 * PAIRW)], qkv_v, sem)
        st = comm[0]() if comm else None
        _transpose_slab(qkv_v, vt_v, lambda p: p * PAIRW + 2 * BQ)
        m0, m1 = _lane_masks()
        r0, r1 = _row_masks()
        valid = _valid_t(True)
        later = _tri_blockdiag(True)

        def steps(qts, i, j, cs, diag):
            ks = [_stack(_kv_tiles(qkv_v, p, b, j)[0], m0, m1) for p, b in CHAINS]
            zs = [_dot(ks[c], qts[c]) for c in range(NC)]
            lbs, lrs = [], []
            for c in range(NC):
                lb = _log_sigmoid_tile(zs[c])
                lr = lb - zs[c]
                if diag:
                    lr = jnp.where(valid, lr, 0.0)
                lbs.append(lb)
                lrs.append(lr)
            tails = [_cum_mm(later, lrs[c]) for c in range(NC)]
            avs = []
            for c in range(NC):
                a = jnp.exp(lbs[c] + tails[c] + _bcast_heads(*cs[c][0]))
                if diag:
                    a = jnp.where(valid, a, 0.0)
                avs.append(a.astype(BF16))
            out = []
            for c, (p, b) in enumerate(CHAINS):
                vts = _stack_t(vt_v[p, :, _rows(b, j)], r0, r1)
                s0, s1 = _colsum2(lrs[c])
                out.append(((cs[c][0][0] + s0, cs[c][0][1] + s1), cs[c][1] + _dot(vts, avs[c])))
            return tuple(out)

        def qblock(i, _):
            qts = [_tr(_q_tile(qkv_v, p, b, i)) for p, b in CHAINS]
            zr = jnp.zeros((1, BQ), F32)
            cs = steps(qts, i, i, (((zr, zr), jnp.zeros((BQ, BQ), F32)),) * NC, True)
            cs = lax.fori_loop(1, i + 1, lambda jj, cs: steps(qts, i, i - jj, cs, False), cs)
            for c, (p, b) in enumerate(CHAINS):
                o_v[_rows(b, i), p * BQ:(p + 1) * BQ] = cs[c][1].T.astype(BF16)
                for h in range(2):
                    _put_row(r_ref, p, b, h, i, cs[c][0][h])
            return 0

        lax.fori_loop(0, NB, qblock, 0)
        _copy_in(o_v, o_hbm.at[:, pl.ds(0, 2 * BQ)], sem)
        if comm:
            comm[1](st)

    (mixed, rtot), extra = _host_call(
        body, "sb_fwd", [qkv], [ANY_SPEC], [jax.ShapeDtypeStruct((T, D), BF16), ROWS_SHAPE], [ANY_SPEC, VMEM_SPEC],
        [SLAB_QKV, SLAB_OUT, pltpu.SemaphoreType.DMA, SLAB_T], {}, job)
    return mixed, rtot, extra


def _sb_bwd(qkv, dmixed, rtot, job=None):
    def body(ins, outs, scr, comm):
        (qkv_hbm, do_hbm, r_ref), (dqkv_hbm,), (qkv_v, do_v, dq_v, dk_s, dv_s, sems, kt_v) = ins, outs, scr
        sem = sems.at[0]
        later = [pltpu.make_async_copy(do_hbm.at[:, pl.ds(0, 2 * BQ)], do_v, sems.at[1])]
        for cp in later:
            cp.start()
        _copy_in(qkv_hbm.at[:, pl.ds(0, 2 * PAIRW)], qkv_v, sem)
        st = comm[0]() if comm else None
        _transpose_slab(qkv_v, kt_v, lambda p: p * PAIRW + BQ)
        for cp in later:
            cp.wait()
        m0, m1 = _lane_masks()
        f0, f1 = m0.astype(F32), m1.astype(F32)
        r0, r1 = _row_masks()
        valid = _valid_t(True)
        later = _tri_blockdiag(True)
        earlier = _tri_blockdiag(False)
        dk_s[...] = jnp.zeros_like(dk_s)
        dv_s[...] = jnp.zeros_like(dv_s)

        def steps(qns, qts, dns, dts, rts, i, j, cs, diag):
            kv = [_kv_tiles(qkv_v, p, b, j) for p, b in CHAINS]
            ks = [_stack(kv[c][0], m0, m1) for c in range(NC)]
            vs = [_stack(kv[c][1], m0, m1) for c in range(NC)]
            zs = [_dot(ks[c], qts[c]) for c in range(NC)]
            das = [_dot(vs[c], dts[c]) for c in range(NC)]
            lbs, lrs, pls = [], [], []
            for c in range(NC):
                lb = _log_sigmoid_tile(zs[c])
                lr = lb - zs[c]
                if diag:
                    lr = jnp.where(valid, lr, 0.0)
                s0, s1 = _colsum2(lr)
                lbs.append(lb)
                lrs.append(lr)
                pls.append((cs[c][0][0] + s0, cs[c][0][1] + s1))
            tails = [_cum_mm(later, lrs[c]) for c in range(NC)]
            avs, gms = [], []
            for c in range(NC):
                a = jnp.exp(lbs[c] + tails[c] + _bcast_heads(rts[c][0] - pls[c][0], rts[c][1] - pls[c][1]))
                if diag:
                    a = jnp.where(valid, a, 0.0)
                avs.append(a)
                gms.append(das[c] * a)
            befores = [_cum_mm(earlier, gms[c]) for c in range(NC)]
            dzbs = []
            for c in range(NC):
                beta = jnp.exp(lbs[c])
                dz = gms[c] - beta * (gms[c] + befores[c] + _bcast_heads(*cs[c][1]))
                if diag:
                    dz = jnp.where(valid, dz, 0.0)
                dzbs.append(dz.astype(BF16))
            out = []
            for c, (p, b) in enumerate(CHAINS):
                dq = cs[c][2] + _dot(_stack_t(kt_v[p, :, _rows(b, j)], r0, r1), dzbs[c])
                dk = _dot(dzbs[c], qns[c])
                dv = _dot(avs[c].astype(BF16), dns[c])
                dk_s[p, _rows(b, j), :] += dk[:BQ] * f0 + dk[BQ:] * f1
                dv_s[p, _rows(b, j), :] += dv[:BQ] * f0 + dv[BQ:] * f1
                g0, g1 = _colsum2(gms[c])
                out.append((pls[c], (cs[c][1][0] + g0, cs[c][1][1] + g1), dq))
            return tuple(out)

        def qblock(i, _):
            qns = [_q_tile(qkv_v, p, b, i) for p, b in CHAINS]
            dns = [do_v[_rows(b, i), p * BQ:(p + 1) * BQ] for p, b in CHAINS]
            qts = [_tr(t) for t in qns]
            dts = [_tr(t) for t in dns]
            rts = [(_stat_row(r_ref, p, b, 0, i), _stat_row(r_ref, p, b, 1, i)) for p, b in CHAINS]
            zr = jnp.zeros((1, BQ), F32)
            cs = (((zr, zr), (zr, zr), jnp.zeros((BQ, BQ), F32)),) * NC
            cs = lax.fori_loop(0, i, lambda j, cs: steps(qns, qts, dns, dts, rts, i, j, cs, False), cs)
            cs = steps(qns, qts, dns, dts, rts, i, i, cs, True)
            for c, (p, b) in enumerate(CHAINS):
                dq_v[_rows(b, i), p * PAIRW:p * PAIRW + BQ] = (cs[c][2].T * SCALE).astype(BF16)
            return 0

        lax.fori_loop(0, NB, qblock, 0)
        for p in range(2):
            dq_v[:, p * PAIRW + BQ:p * PAIRW + 2 * BQ] = dk_s[p].astype(BF16)
            dq_v[:, p * PAIRW + 2 * BQ:p * PAIRW + 3 * BQ] = dv_s[p].astype(BF16)
        _copy_in(dq_v, dqkv_hbm.at[:, pl.ds(0, 2 * PAIRW)], sem)
        if comm:
            comm[1](st)

    (dqkv,), extra = _host_call(
        body, "sb_bwd", [qkv, dmixed, rtot], [ANY_SPEC, ANY_SPEC, VMEM_SPEC],
        [jax.ShapeDtypeStruct((T, QKVW), BF16)], [ANY_SPEC],
        [SLAB_QKV, SLAB_OUT, SLAB_QKV, ACC_KV, ACC_KV, pltpu.SemaphoreType.DMA((4,)), SLAB_T], {}, job)
    return dqkv, extra


def _flash_fwd(qkv, mixed, g, fox, bias, job=None):
    def body(ins, outs, scr, comm):
        if fox:
            qkv_hbm, cq_ref, ckb_hbm, _ = ins
            (o_hbm, lse_ref, o32_hbm), (qkv_v, o_v, sem, vt_v, o32_v, ckb_v) = outs, scr
        else:
            qkv_hbm, tbl_ref, _ = ins
            (o_hbm, lse_ref), (qkv_v, o_v, sem, vt_v) = outs, scr
        sems = sem
        sem = sems.at[0]
        later = [pltpu.make_async_copy(ckb_hbm, ckb_v, sems.at[1])] if fox else []
        for cp in later:
            cp.start()
        _copy_in(qkv_hbm.at[:, pl.ds(g * 2 * PAIRW, 2 * PAIRW)], qkv_v, sem)
        st = comm[0]() if comm else None
        _transpose_slab(qkv_v, vt_v, lambda p: p * PAIRW + 2 * BQ)
        for cp in later:
            cp.wait()
        m0, m1 = _lane_masks()
        r0, r1 = _row_masks()
        valid = _valid_t(False)

        def steps(qts, cqs, i, j, cs, diag):
            ks = [_stack(_kv_tiles(qkv_v, p, b, j)[0], m0, m1) for p, b in CHAINS]
            zs = [_dot(ks[c], qts[c]) for c in range(NC)]
            prs, alphas, out = [], [], []
            for c, (p, b) in enumerate(CHAINS):
                (ma, mb), (la, lb_), _ = cs[c]
                if fox:
                    kk = pl.ds(pl.multiple_of(j * BQ, BQ), BQ)
                    col = b * NH + 2 * p
                    z = zs[c] + (cqs[c] - jnp.concatenate([ckb_v[col, kk, :], ckb_v[col + 1, kk, :]], axis=0))
                    if diag:
                        z = jnp.where(valid, z, NEG)
                else:
                    z = zs[c] + tbl_ref[p, i - j]
                za, zb = _heads(z)
                na = jnp.maximum(ma, jnp.max(za, axis=0, keepdims=True))
                nb = jnp.maximum(mb, jnp.max(zb, axis=0, keepdims=True))
                aa, ab = jnp.exp(ma - na), jnp.exp(mb - nb)
                pr = jnp.exp(z - _bcast_heads(na, nb))
                sa, sb = _colsum2(pr)
                prs.append(_split2(pr) if fox else pr.astype(BF16))
                alphas.append((aa, ab))
                out.append(((na, nb), (aa * la + sa, ab * lb_ + sb)))
            pvs = []
            for c, (p, b) in enumerate(CHAINS):
                vts = _stack_t(vt_v[p, :, _rows(b, j)], r0, r1)
                if fox:
                    pvs.append(_dot(vts, prs[c][:, :BQ]) + _dot(vts, prs[c][:, BQ:]))
                else:
                    pvs.append(_dot(vts, prs[c]))
            return tuple((out[c][0], out[c][1], _by_channel(*alphas[c]) * cs[c][2] + pvs[c]) for c in range(NC))

        def qblock(i, _):
            qts = [_tr(_q_tile(qkv_v, p, b, i)) for p, b in CHAINS]
            if fox:
                cqs = [_bcast_heads(_stat_row(cq_ref, p, b, 0, i), _stat_row(cq_ref, p, b, 1, i)) for p, b in CHAINS]
            else:
                cqs = [None] * NC
            ng = jnp.full((1, BQ), NEG, F32)
            zr = jnp.zeros((1, BQ), F32)
            cs = steps(qts, cqs, i, i, (((ng, ng), (zr, zr), jnp.zeros((BQ, BQ), F32)),) * NC, True)
            cs = lax.fori_loop(1, i + 1, lambda jj, cs: steps(qts, cqs, i, i - jj, cs, False), cs)
            for c, (p, b) in enumerate(CHAINS):
                (ma, mb), (la, lb_), acc = cs[c]
                o = (acc / _by_channel(la, lb_)).T
                o_v[_rows(b, i), p * BQ:(p + 1) * BQ] = o.astype(BF16)
                if fox:
                    o32_v[_rows(b, i), p * BQ:(p + 1) * BQ] = o
                _put_row(lse_ref, p, b, 0, i, ma + jnp.log(la))
                _put_row(lse_ref, p, b, 1, i, mb + jnp.log(lb_))
            return 0

        lax.fori_loop(0, NB, qblock, 0)
        _copy_in(o_v, o_hbm.at[:, pl.ds(g * 2 * BQ, 2 * BQ)], sem)
        if fox:
            _copy_in(o32_v, o32_hbm, sem)
        if comm:
            comm[1](st)

    bias_specs = [VMEM_SPEC, ANY_SPEC] if fox else [VMEM_SPEC]
    n_in = 2 + len(bias_specs)
    o32 = [jax.ShapeDtypeStruct((T, 2 * BQ), F32)] if fox else []
    res, extra = _host_call(
        body, "fox_fwd" if fox else "dil_fwd", [qkv, *bias, mixed], [ANY_SPEC] + bias_specs + [ANY_SPEC],
        [jax.ShapeDtypeStruct((T, D), BF16), ROWS_SHAPE] + o32, [ANY_SPEC, VMEM_SPEC] + [ANY_SPEC] * len(o32),
        [SLAB_QKV, SLAB_OUT, pltpu.SemaphoreType.DMA((4,)), SLAB_T] + ([SLAB_O32, SLAB_KEYB] if fox else []),
        {n_in - 1: 0}, job)
    return (*res, extra)


def _flash_bwd(qkv, o, dmixed, lse, dqkv, g, fox, bias, job=None):
    def body(ins, outs, scr, comm):
        if fox:
            qkv_hbm, o_hbm, do_hbm, lse_ref, cq_ref, ckb_hbm, _ = ins
            (dqkv_hbm, db_ref), (qkv_v, o_v, do_v, dq_v, dk_s, dv_s, sem, kt_v, ckb_v, dc_s) = outs, scr
        else:
            qkv_hbm, o_hbm, do_hbm, lse_ref, tbl_ref, _ = ins
            (dqkv_hbm, db_ref), (qkv_v, o_v, do_v, dq_v, dk_s, dv_s, sem, kt_v) = outs, scr
        sems = sem
        sem = sems.at[0]
        later = [pltpu.make_async_copy(do_hbm.at[:, pl.ds(g * 2 * BQ, 2 * BQ)], do_v, sems.at[1])]
        if fox:
            later += [pltpu.make_async_copy(o_hbm, o_v, sems.at[2]), pltpu.make_async_copy(ckb_hbm, ckb_v, sems.at[3])]
        else:
            later += [pltpu.make_async_copy(o_hbm.at[:, pl.ds(g * 2 * BQ, 2 * BQ)], o_v, sems.at[2])]
        for cp in later:
            cp.start()
        _copy_in(qkv_hbm.at[:, pl.ds(g * 2 * PAIRW, 2 * PAIRW)], qkv_v, sem)
        st = comm[0]() if comm else None
        _transpose_slab(qkv_v, kt_v, lambda p: p * PAIRW + BQ)
        for cp in later:
            cp.wait()
        m0, m1 = _lane_masks()
        f0, f1 = m0.astype(F32), m1.astype(F32)
        r0, r1 = _row_masks()
        valid = _valid_t(False)
        dk_s[...] = jnp.zeros_like(dk_s)
        dv_s[...] = jnp.zeros_like(dv_s)
        if fox:
            dc_s[...] = jnp.zeros_like(dc_s)
        else:
            db_ref[...] = jnp.zeros_like(db_ref)

        def steps(qns, qts, dns, dts, cqs, lses, deltas, i, j, dqs, diag):
            kv = [_kv_tiles(qkv_v, p, b, j) for p, b in CHAINS]
            ks = [_stack(kv[c][0], m0, m1) for c in range(NC)]
            vs = [_stack(kv[c][1], m0, m1) for c in range(NC)]
            zs = [_dot(ks[c], qts[c]) for c in range(NC)]
            dps = [_dot(vs[c], dts[c]) for c in range(NC)]
            prs, dzl = [], []
            for c, (p, b) in enumerate(CHAINS):
                if fox:
                    kk = pl.ds(pl.multiple_of(j * BQ, BQ), BQ)
                    col = b * NH + 2 * p
                    z = zs[c] + (cqs[c] - jnp.concatenate([ckb_v[col, kk, :], ckb_v[col + 1, kk, :]], axis=0))
                    if diag:
                        z = jnp.where(valid, z, NEG)
                else:
                    z = zs[c] + tbl_ref[p, i - j]
                pr = jnp.exp(z - lses[c])
                prs.append(pr.astype(BF16))
                dzl.append(pr * (dps[c] - deltas[c]))
            dzbs = [dz.astype(BF16) for dz in dzl]
            new = []
            for c, (p, b) in enumerate(CHAINS):
                new.append(dqs[c] + _dot(_stack_t(kt_v[p, :, _rows(b, j)], r0, r1), dzbs[c]))
                dk = _dot(dzbs[c], qns[c])
                dv = _dot(prs[c], dns[c])
                dk_s[p, _rows(b, j), :] += dk[:BQ] * f0 + dk[BQ:] * f1
                dv_s[p, _rows(b, j), :] += dv[:BQ] * f0 + dv[BQ:] * f1
                if fox:
                    dc_s[c, pl.ds(pl.multiple_of(j * HB, HB), HB), :] += dzl[c]
            if not fox:
                for p in range(2):
                    db_ref[p, i - j] = db_ref[p, i - j] + (dzl[2 * p] + dzl[2 * p + 1])
            return tuple(new)

        def qblock(i, _):
            qns = [_q_tile(qkv_v, p, b, i) for p, b in CHAINS]
            dns = [do_v[_rows(b, i), p * BQ:(p + 1) * BQ] for p, b in CHAINS]
            qts = [_tr(t) for t in qns]
            dts = [_tr(t) for t in dns]
            lses = [_bcast_heads(_stat_row(lse_ref, p, b, 0, i), _stat_row(lse_ref, p, b, 1, i)) for p, b in CHAINS]
            if fox:
                cqs = [_bcast_heads(_stat_row(cq_ref, p, b, 0, i), _stat_row(cq_ref, p, b, 1, i)) for p, b in CHAINS]
            else:
                cqs = [None] * NC
            deltas = []
            for c, (p, b) in enumerate(CHAINS):
                pt = (dns[c].astype(F32) * o_v[_rows(b, i), p * BQ:(p + 1) * BQ].astype(F32)).T
                deltas.append(_bcast_heads(jnp.sum(pt[:64], axis=0, keepdims=True), jnp.sum(pt[64:], axis=0, keepdims=True)))
            dqs = (jnp.zeros((BQ, BQ), F32),) * NC
            dqs = lax.fori_loop(0, i, lambda j, d: steps(qns, qts, dns, dts, cqs, lses, deltas, i, j, d, False), dqs)
            dqs = steps(qns, qts, dns, dts, cqs, lses, deltas, i, i, dqs, True)
            for c, (p, b) in enumerate(CHAINS):
                dq_v[_rows(b, i), p * PAIRW:p * PAIRW + BQ] = (dqs[c].T * SCALE).astype(BF16)
            return 0

        lax.fori_loop(0, NB, qblock, 0)
        for p in range(2):
            dq_v[:, p * PAIRW + BQ:p * PAIRW + 2 * BQ] = dk_s[p].astype(BF16)
            dq_v[:, p * PAIRW + 2 * BQ:p * PAIRW + 3 * BQ] = dv_s[p].astype(BF16)
        _copy_in(dq_v, dqkv_hbm.at[:, pl.ds(g * 2 * PAIRW, 2 * PAIRW)], sem)
        if fox:
            lane = _iota((BQ, NSTAT), 1)

            def fold(n, _):
                t = jnp.zeros((BQ, NSTAT), F32)
                for c, (p, b) in enumerate(CHAINS):
                    s = jnp.sum(dc_s[c, pl.ds(pl.multiple_of(n * HB, HB), HB), :], axis=1, keepdims=True)
                    col = b * NH + 2 * p
                    t = t - jnp.where(lane == col, s[:BQ], 0.0) - jnp.where(lane == col + 1, s[BQ:], 0.0)
                db_ref[pl.ds(pl.multiple_of(n * BQ, BQ), BQ), :] = t
                return 0

            lax.fori_loop(0, NB, fold, 0)
        if comm:
            comm[1](st)

    if fox:
        bias_specs = [VMEM_SPEC, ANY_SPEC]
        db_shape = jax.ShapeDtypeStruct((S, NSTAT), F32)
        more = [SLAB_KEYB, pltpu.VMEM((NC, NB * HB, BQ), F32)]
    else:
        bias_specs = [VMEM_SPEC]
        db_shape = jax.ShapeDtypeStruct((2, NB, HB, BQ), F32)
        more = []
    n_in = 5 + len(bias_specs)
    (dqkv, db), extra = _host_call(
        body, "fox_bwd" if fox else "dil_bwd", [qkv, o, dmixed, lse, *bias, dqkv],
        [ANY_SPEC, ANY_SPEC, ANY_SPEC, VMEM_SPEC] + bias_specs + [ANY_SPEC],
        [jax.ShapeDtypeStruct((T, QKVW), BF16), db_shape], [ANY_SPEC, VMEM_SPEC],
        [SLAB_QKV, SLAB_O32 if fox else SLAB_OUT, SLAB_OUT, SLAB_QKV, ACC_KV, ACC_KV, pltpu.SemaphoreType.DMA((4,)), SLAB_T]
        + more, {n_in - 1: 0}, job)
    return dqkv, db, extra


def _delta_t(d):
    return d * BQ + _iota((HB, BQ), 1) - (_iota((HB, BQ), 0) & (BQ - 1))


def _buckets_in(d):
    lo, hi = max(d * BQ - (BQ - 1), 0), d * BQ + BQ - 1
    return [b for b in range(32) if BUCKET_TH[b] <= hi and (b == 31 or BUCKET_TH[b + 1] > lo)]


def _in_bucket(delta, b):
    m = delta >= BUCKET_TH[b]
    return m if b == 31 else m & (delta < BUCKET_TH[b + 1])


def _dil_table(rel_bias):
    def body(rb_ref, o_ref):
        for d in range(NB):
            delta = _delta_t(d)
            pos = delta >= 0
            n = ((pos & (delta <= 128)).astype(jnp.int32)
                 + (pos & (delta <= 512) & ((delta & 3) == 0)).astype(jnp.int32)
                 + (pos & ((delta & 15) == 0)).astype(jnp.int32))
            logn = jnp.where(n == 3, math.log(3.0), jnp.where(n == 2, math.log(2.0), jnp.where(n == 1, 0.0, NEG)))
            head1 = _iota((HB, BQ), 0) >= BQ
            for p in range(2):
                val = jnp.zeros((HB, BQ), F32)
                for b in _buckets_in(d):
                    val = jnp.where(_in_bucket(delta, b), jnp.where(head1, rb_ref[b, 2 * p + 1], rb_ref[b, 2 * p]), val)
                o_ref[p, d] = val + logn

    return pl.pallas_call(
        body, name="dil_table", in_specs=[pl.BlockSpec(memory_space=pltpu.SMEM)], out_specs=VMEM_SPEC,
        out_shape=jax.ShapeDtypeStruct((2, NB, HB, BQ), F32), compiler_params=_cp())(rel_bias)


def _dil_table_bwd(dtbl):
    def body(dt_ref, o_ref):
        p = pl.program_id(0)
        rowi = _iota((32, BQ), 0)
        lanei = _iota((32, BQ), 1)

        @pl.when(p == 0)
        def _():
            o_ref[...] = jnp.zeros_like(o_ref)

        out = jnp.zeros((32, BQ), F32)
        for b in range(32):
            acc = None
            for d in range(NB):
                if b in _buckets_in(d):
                    t = jnp.where(_in_bucket(_delta_t(d), b), dt_ref[d], 0.0)
                    acc = t if acc is None else acc + t
            rs = jnp.sum(acc, axis=1, keepdims=True)
            s0 = jnp.sum(rs[:BQ], axis=0, keepdims=True)
            s1 = jnp.sum(rs[BQ:], axis=0, keepdims=True)
            out = (out + jnp.where((rowi == b) & (lanei == 2 * p), s0, 0.0)
                   + jnp.where((rowi == b) & (lanei == 2 * p + 1), s1, 0.0))
        o_ref[...] += out

    return pl.pallas_call(
        body, name="dil_table_bwd", grid=(2,),
        in_specs=[pl.BlockSpec((None, NB, HB, BQ), lambda p: (p, 0, 0, 0))],
        out_specs=pl.BlockSpec((32, BQ), lambda p: (0, 0)),
        out_shape=jax.ShapeDtypeStruct((32, BQ), F32),
        compiler_params=_cp(("arbitrary",)))(dtbl)


def _fox_prep(gate, fb):
    def body(g_ref, fb_ref, c_ref):
        tri = (_iota((BQ, BQ), 0) >= _iota((BQ, BQ), 1)).astype(BF16)

        def blk(i, carry):
            r0 = pl.multiple_of(i * BQ, BQ)
            lf = _log_sigmoid(g_ref[pl.ds(r0, BQ), :] + fb_ref[...])
            c = _dot(tri, _split3(lf))
            c_ref[pl.ds(r0, BQ), :] = c[:, 0:BQ] + c[:, BQ:2 * BQ] + c[:, 2 * BQ:3 * BQ] + carry
            return carry + jnp.sum(lf, axis=0, keepdims=True)

        lax.fori_loop(0, NB, blk, jnp.zeros((1, BQ), F32))

    blk = pl.BlockSpec((S, GATEW), lambda b: (b, 0))
    return pl.pallas_call(
        body, name="fox_prep", grid=(BL,), in_specs=[blk, pl.BlockSpec((1, GATEW), lambda b: (0, 0))],
        out_specs=blk, out_shape=jax.ShapeDtypeStruct((T, GATEW), F32),
        compiler_params=_cp(("parallel",)))(gate, fb)


def _fox_post(dcum, gate, fb):
    def body(dc_ref, g_ref, fb_ref, dg_ref, dfb_ref):
        b = pl.program_id(0)
        tri = (_iota((BQ, BQ), 0) <= _iota((BQ, BQ), 1)).astype(BF16)

        def blk(ii, carry):
            csum, dfb = carry
            r0 = pl.multiple_of((NB - 1 - ii) * BQ, BQ)
            dc = dc_ref[pl.ds(r0, BQ), :]
            c = _dot(tri, _split3(dc))
            dlf = c[:, 0:BQ] + c[:, BQ:2 * BQ] + c[:, 2 * BQ:3 * BQ] + csum
            dg = dlf * jnp.exp(_log_sigmoid(-(g_ref[pl.ds(r0, BQ), :] + fb_ref[...])))
            dg_ref[pl.ds(r0, BQ), :] = dg
            return csum + jnp.sum(dc, axis=0, keepdims=True), dfb + jnp.sum(dg, axis=0, keepdims=True)

        z = jnp.zeros((1, BQ), F32)
        _, dfb = lax.fori_loop(0, NB, blk, (z, z))

        @pl.when(b == 0)
        def _():
            dfb_ref[...] = dfb

        @pl.when(b > 0)
        def _():
            dfb_ref[...] += dfb

    blk = pl.BlockSpec((S, GATEW), lambda b: (b, 0))
    vec = pl.BlockSpec((1, GATEW), lambda b: (0, 0))
    return pl.pallas_call(
        body, name="fox_post", grid=(BL,), in_specs=[blk, blk, vec], out_specs=[blk, vec],
        out_shape=[jax.ShapeDtypeStruct((T, GATEW), F32), jax.ShapeDtypeStruct((1, GATEW), F32)],
        compiler_params=_cp(("arbitrary",)))(dcum, gate, fb)


def _shift_down(x, n):
    return jnp.where(_iota(x.shape, 0) >= n, pltpu.roll(x, n, 0), 0.0)


def _shift_up(x, n):
    return jnp.where(_iota(x.shape, 0) < S - n, pltpu.roll(x, S - n, 0), 0.0)


def _conv_fwd(conv, cw, mixed):
    W = 256

    def body(c_ref, w_ref, _, o_ref):
        u = c_ref[:, W:2 * W] * c_ref[:, 2 * W:3 * W]
        y = w_ref[0:1, :] * _shift_down(u, 2) + w_ref[1:2, :] * _shift_down(u, 1) + w_ref[2:3, :] * u
        o_ref[...] = (c_ref[:, 0:W] * y).astype(BF16)

    return pl.pallas_call(
        body, name="conv_fwd", grid=(BL,),
        in_specs=[pl.BlockSpec((S, CONVW), lambda b: (b, 0)), pl.BlockSpec((8, W), lambda b: (0, 0)), ANY_SPEC],
        out_specs=pl.BlockSpec((S, W), lambda b: (b, 3)),
        out_shape=jax.ShapeDtypeStruct((T, D), BF16), input_output_aliases={2: 0},
        compiler_params=_cp(("parallel",)))(conv, cw, mixed)


def _conv_bwd(conv, cw, dmixed):
    W = 256

    def body(c_ref, w_ref, do_ref, dc_ref, dw_ref):
        b = pl.program_id(0)
        bg = c_ref[:, 0:W]
        cg = c_ref[:, W:2 * W]
        hv = c_ref[:, 2 * W:3 * W]
        do = do_ref[...].astype(F32)
        u = cg * hv
        u1 = _shift_down(u, 1)
        u2 = _shift_down(u, 2)
        y = w_ref[0:1, :] * u2 + w_ref[1:2, :] * u1 + w_ref[2:3, :] * u
        dy = do * bg
        du = w_ref[2:3, :] * dy + w_ref[1:2, :] * _shift_up(dy, 1) + w_ref[0:1, :] * _shift_up(dy, 2)
        dc_ref[:, 0:W] = (do * y).astype(BF16)
        dc_ref[:, W:2 * W] = (du * hv).astype(BF16)
        dc_ref[:, 2 * W:3 * W] = (du * cg).astype(BF16)
        rowi = _iota((8, W), 0)
        dw = (jnp.where(rowi == 0, jnp.sum(dy * u2, axis=0, keepdims=True), 0.0)
              + jnp.where(rowi == 1, jnp.sum(dy * u1, axis=0, keepdims=True), 0.0)
              + jnp.where(rowi == 2, jnp.sum(dy * u, axis=0, keepdims=True), 0.0))

        @pl.when(b == 0)
        def _():
            dw_ref[...] = dw

        @pl.when(b > 0)
        def _():
            dw_ref[...] += dw

    return pl.pallas_call(
        body, name="conv_bwd", grid=(BL,),
        in_specs=[pl.BlockSpec((S, CONVW), lambda b: (b, 0)), pl.BlockSpec((8, W), lambda b: (0, 0)),
                  pl.BlockSpec((S, W), lambda b: (b, 3))],
        out_specs=[pl.BlockSpec((S, CONVW), lambda b: (b, 0)), pl.BlockSpec((8, W), lambda b: (0, 0))],
        out_shape=[jax.ShapeDtypeStruct((T, CONVW), BF16), jax.ShapeDtypeStruct((8, W), F32)],
        compiler_params=_cp(("arbitrary",)))(conv, cw, dmixed)


def _place():
    x, y, c = lax.axis_index("x"), lax.axis_index("y"), lax.axis_index("c")
    return x, y, c


def _chips_of(x, y):
    return [(1 - x, y), (x, 1 - y), (1 - x, 1 - y)]


def _dev(p):
    return 4 * p[0] + 2 * p[1] + p[2]


def _gather_job_a(shards):
    n = len(shards)

    def peers(x, y, c):
        return [(x, y, 1 - c)] + [(*chip, c) for chip in _chips_of(x, y)]

    def start(ins, outs, sems):
        send, recv, loc = sems
        x, y, c = _place()
        me = (x, y, c)
        cps = []
        for a in range(n):
            cps.append(pltpu.make_async_copy(ins[a], outs[a].at[_dev(me)], loc.at[a]))
            for k, peer in enumerate(peers(x, y, c)):
                cps.append(pltpu.make_async_remote_copy(
                    src_ref=ins[a], dst_ref=outs[a].at[_dev(me)], send_sem=send.at[a, k], recv_sem=recv.at[a, k],
                    device_id=peer, device_id_type=MESH))
        for cp in cps:
            cp.start()
        return cps

    def finish(cps, ins, outs, sems):
        send, recv, loc = sems
        x, y, c = _place()
        for a in range(n):
            for k, peer in enumerate(peers(x, y, c)):
                pltpu.make_async_remote_copy(
                    src_ref=ins[a], dst_ref=outs[a].at[_dev(peer)], send_sem=send.at[a, k], recv_sem=recv.at[a, k],
                    device_id=(x, y, c), device_id_type=MESH).wait_recv()
        for a in range(n):
            cps[5 * a].wait()
            for k in range(4):
                cps[5 * a + 1 + k].wait_send()

    return _Job(shards, [jax.ShapeDtypeStruct((NDEV,) + s.shape, s.dtype) for s in shards], {},
                [pltpu.SemaphoreType.DMA((n, 4)), pltpu.SemaphoreType.DMA((n, 4)), pltpu.SemaphoreType.DMA((n,))],
                start, finish)


def _gather_job_b(gathered):
    n = len(gathered)

    def start(ins, outs, sems):
        send, recv = sems
        x, y, c = _place()
        cps = []
        for a in range(n):
            for j, chip in enumerate(_chips_of(x, y)):
                blk = outs[a].at[_dev((*chip, c))]
                cps.append(pltpu.make_async_remote_copy(
                    src_ref=blk, dst_ref=blk, send_sem=send.at[a, j], recv_sem=recv.at[a, j],
                    device_id=(x, y, 1 - c), device_id_type=MESH))
        for cp in cps:
            cp.start()
        return cps

    def finish(cps, ins, outs, sems):
        send, recv = sems
        x, y, c = _place()
        for a in range(n):
            for j, chip in enumerate(_chips_of(x, y)):
                blk = outs[a].at[_dev((*chip, 1 - c))]
                pltpu.make_async_remote_copy(
                    src_ref=blk, dst_ref=blk, send_sem=send.at[a, j], recv_sem=recv.at[a, j],
                    device_id=(x, y, c), device_id_type=MESH).wait_recv()
        for cp in cps:
            cp.wait_send()

    return _Job(gathered, [jax.ShapeDtypeStruct(g.shape, g.dtype) for g in gathered], {a: a for a in range(n)},
                [pltpu.SemaphoreType.DMA((n, 3)), pltpu.SemaphoreType.DMA((n, 3))], start, finish)


def _sibling_job(grads):
    n = len(grads)

    def start(ins, outs, sems):
        send, recv = sems
        x, y, c = _place()
        cps = [pltpu.make_async_remote_copy(
            src_ref=ins[a].at[:, 1 - c], dst_ref=outs[a], send_sem=send.at[a], recv_sem=recv.at[a],
            device_id=(x, y, 1 - c), device_id_type=MESH) for a in range(n)]
        for cp in cps:
            cp.start()
        return cps

    def finish(cps, ins, outs, sems):
        for cp in cps:
            cp.wait()

    return _Job(grads, [jax.ShapeDtypeStruct(g.shape[:1] + g.shape[2:], F32) for g in grads], {},
                [pltpu.SemaphoreType.DMA((n,)), pltpu.SemaphoreType.DMA((n,))], start, finish)


def _chip_job(psums):
    n = len(psums)

    def copies(ins, outs, sems):
        send, recv, loc = sems
        x, y, c = _place()
        mychip = 2 * x + y
        cps = []
        for a in range(n):
            cps.append(pltpu.make_async_copy(ins[a].at[mychip], outs[a].at[mychip], loc.at[a]))
            for j, chip in enumerate(_chips_of(x, y)):
                cps.append(pltpu.make_async_remote_copy(
                    src_ref=ins[a].at[2 * chip[0] + chip[1]], dst_ref=outs[a].at[mychip],
                    send_sem=send.at[a, j], recv_sem=recv.at[a, j], device_id=(*chip, c), device_id_type=MESH))
        return cps

    def start(ins, outs, sems):
        for cp in copies(ins, outs, sems):
            cp.start()

    def finish(_, ins, outs, sems):
        cps = copies(ins, outs, sems)
        send, recv, loc = sems
        x, y, c = _place()
        mychip = 2 * x + y
        for a in range(n):
            for j, chip in enumerate(_chips_of(x, y)):
                pltpu.make_async_remote_copy(
                    src_ref=ins[a].at[mychip], dst_ref=outs[a].at[2 * chip[0] + chip[1]],
                    send_sem=send.at[a, j], recv_sem=recv.at[a, j], device_id=(x, y, c), device_id_type=MESH).wait_recv()
        for a in range(n):
            cps[4 * a].wait()
            for j in range(3):
                cps[4 * a + 1 + j].wait_send()

    return _Job(psums, [jax.ShapeDtypeStruct(p.shape, BF16) for p in psums], {},
                [pltpu.SemaphoreType.DMA((n, 3)), pltpu.SemaphoreType.DMA((n, 3)), pltpu.SemaphoreType.DMA((n,))],
                start, finish)


def _join_jobs(*jobs):
    jobs = [j for j in jobs if j is not None]
    if len(jobs) <= 1:
        return jobs[0] if jobs else None
    cut = lambda seq, sizes: [seq[sum(sizes[:k]):sum(sizes[:k + 1])] for k in range(len(sizes))]
    n_in = [len(j.ins) for j in jobs]
    n_out = [len(j.out_shapes) for j in jobs]
    n_sem = [len(j.sems) for j in jobs]
    aliases = {}
    for k, j in enumerate(jobs):
        for a, b in j.aliases.items():
            aliases[sum(n_in[:k]) + a] = sum(n_out[:k]) + b

    def start(ins, outs, sems):
        return [j.start(i, o, s) for j, i, o, s in zip(jobs, cut(ins, n_in), cut(outs, n_out), cut(sems, n_sem))]

    def finish(sts, ins, outs, sems):
        for j, st, i, o, s in zip(jobs, sts, cut(ins, n_in), cut(outs, n_out), cut(sems, n_sem)):
            j.finish(st, i, o, s)

    return _Job([t for j in jobs for t in j.ins], [t for j in jobs for t in j.out_shapes], aliases,
                [t for j in jobs for t in j.sems], start, finish)


def _run_job(job, name):
    def body(ins, outs, scr, comm):
        comm[1](comm[0]())

    return _host_call(body, name, [], [], [], [], [], {}, job)[1]


def _allreduce_small(v):
    def body(v_ref, o_ref, slots, send_sems, recv_sems):
        x, y, c = _place()
        me = 4 * x + 2 * y + c
        slots[me] = v_ref[...]

        def copy(k):
            peer = (x ^ ((k >> 2) & 1), y ^ ((k >> 1) & 1), c ^ (k & 1))
            return pltpu.make_async_remote_copy(
                src_ref=v_ref, dst_ref=slots.at[me], send_sem=send_sems.at[k - 1], recv_sem=recv_sems.at[k - 1],
                device_id=peer, device_id_type=MESH)

        def arrival(k):
            return pltpu.make_async_remote_copy(
                src_ref=v_ref, dst_ref=slots.at[me ^ k], send_sem=send_sems.at[k - 1], recv_sem=recv_sems.at[k - 1],
                device_id=(x, y, c), device_id_type=MESH)

        sends = [copy(k) for k in range(1, NDEV)]
        for cp in sends:
            cp.start()
        for k in range(1, NDEV):
            arrival(k).wait_recv()
        for cp in sends:
            cp.wait_send()
        acc = slots[0]
        for d in range(1, NDEV):
            acc = acc + slots[d]
        o_ref[...] = acc

    return pl.pallas_call(
        body, name="allreduce_small", in_specs=[VMEM_SPEC], out_specs=VMEM_SPEC,
        out_shape=jax.ShapeDtypeStruct(v.shape, F32),
        scratch_shapes=[pltpu.VMEM((NDEV,) + v.shape, F32), pltpu.SemaphoreType.DMA((NDEV - 1,)),
                        pltpu.SemaphoreType.DMA((NDEV - 1,))],
        )(v)


def _pair_sums(views, gots, core):
    n = len(views)

    def body(c_ref, *refs):
        for a in range(n):
            refs[2 * n + a][...] = (refs[a][...] + refs[n + a][...]).astype(BF16)

    def vspec(v):
        return pl.BlockSpec((None, None, v.shape[2] // 2, v.shape[3]), lambda k, h, c: (k, c[0], h, 0))

    def gspec(g):
        return pl.BlockSpec((None, g.shape[1] // 2, g.shape[2]), lambda k, h, c: (k, h, 0))

    return pl.pallas_call(
        body, name="pair_sums",
        grid_spec=pltpu.PrefetchScalarGridSpec(
            num_scalar_prefetch=1, grid=(4, 2),
            in_specs=[vspec(v) for v in views] + [gspec(g) for g in gots],
            out_specs=[gspec(g) for g in gots]),
        out_shape=[jax.ShapeDtypeStruct(g.shape, BF16) for g in gots],
        compiler_params=_cp(("parallel", "parallel")))(core, *views, *gots)


def _chip_sums(parts):
    n = len(parts)

    def body(*refs):
        for a in range(n):
            acc = refs[a][0].astype(F32)
            for k in range(1, 4):
                acc = acc + refs[a][k].astype(F32)
            refs[n + a][...] = acc

    return pl.pallas_call(
        body, name="chip_sums", in_specs=[VMEM_SPEC] * n, out_specs=[VMEM_SPEC] * n,
        out_shape=[jax.ShapeDtypeStruct(p.shape[1:], F32) for p in parts], compiler_params=_cp())(*parts)


def _permute_in(w):
    lead = w.shape[:-1]
    return w.reshape(lead + (3, 3, 2, BQ)).swapaxes(-2, -3).reshape(lead + (QKVW,))


def _unpermute_in(w):
    lead = w.shape[:-1]
    return w.reshape(lead + (3, 2, 3, BQ)).swapaxes(-2, -3).reshape(lead + (QKVW,))


def _row(v):
    v = v.reshape(-1)
    return jnp.pad(v, (0, D - v.shape[0])).reshape(1, D)


def kernel(x, w_in, f_bias, conv_w, w_out, rel_bias, ln1_g, ln1_b, w_gate, w_up, w_down, ln2_g, ln2_b, loss_target, m_w_in, m_f_bias, m_conv_w, m_w_out, m_rel_bias, m_ln1_g, m_ln1_b, m_w_gate, m_w_up, m_w_down, m_ln2_g, m_ln2_b, v_w_in, v_f_bias, v_conv_w, v_w_out, v_rel_bias, v_ln1_g, v_ln1_b, v_w_gate, v_w_up, v_w_down, v_ln2_g, v_ln2_b):
    xi, yi, ci = _place()
    me = 4 * xi + 2 * yi + ci
    core = jnp.reshape(ci, (1,)).astype(jnp.int32)

    win_s = jnp.concatenate([_permute_in(w_in[..., :QKVW]), w_in[..., QKVW:]], axis=-1)
    win_s = jnp.pad(win_s, ((0, 0), (0, 0), (0, NPAD - NPROJ))).astype(BF16)
    per_layer = [win_s, w_out.astype(BF16), jnp.swapaxes(w_gate, 1, 2).astype(BF16),
                 jnp.swapaxes(w_up, 1, 2).astype(BF16), w_down.astype(BF16)]
    sh = [[s[l] for s in per_layer] for l in range(2)]

    def whole(g):
        return g.reshape(NDEV * g.shape[1], g.shape[2])

    first = _run_job(_gather_job_b(_run_job(_gather_job_a(sh[0][:1]), "gather_a")), "gather_b")
    W = [{"win": whole(first[0])}, {}]

    cw_rows = lax.dynamic_update_slice(jnp.zeros((2, 3, 256), F32), conv_w, (0, 0, me * 32))
    small = jnp.concatenate([_row(cw_rows[0]), _row(cw_rows[1]), jnp.zeros((SMALL_ROWS - 2, D), F32)], axis=0)
    small = _allreduce_small(small)
    cw_full = small[0:2, :CONVW].reshape(2, 3, 256)
    cw8 = jnp.pad(cw_full, ((0, 0), (0, 5), (0, 0)))
    fb = jnp.pad(f_bias, ((0, 0), (0, GATEW - NH))).reshape(2, 1, GATEW)
    tbl = _dil_table(rel_bias)

    def wcol(K, tn, off):
        return pl.BlockSpec((K, tn), lambda i, j: (0, off + j))

    def wrow(tn, K, blk=0):
        return pl.BlockSpec((tn, K), lambda i, j: (j, blk))

    def arow(tm, K, blk=0):
        return pl.BlockSpec((tm, K), lambda i, j: (i, blk))

    h = x.reshape(T, D)
    hb = h.astype(BF16)
    saved = []
    for l in range(2):
        Win = W[l]["win"]
        qkv, conv, gate = _proj(hb, Win)
        cum = _fox_prep(gate, fb[l])
        cq = cum[:, :NH].reshape(BL, S, NH).transpose(0, 2, 1).reshape(NSTAT, S)
        ckb = jnp.broadcast_to(cq[:, :, None], (NSTAT, S, BQ))
        if l == 0:
            mixed, rtot, a0 = _sb_fwd(qkv, job=_gather_job_a(sh[0][1:]))
            mixed, lse_d, ex = _flash_fwd(qkv, mixed, 1, False, (tbl,),
                                          job=_join_jobs(_gather_job_b(list(a0)), _gather_job_a(sh[1][:2])))
            W[0].update(zip(("wout", "wgT", "wuT", "wd"), [whole(t) for t in ex[:4]]))
            mixed, lse_f, o_fox, ex = _flash_fwd(qkv, mixed, 2, True, (cq, ckb),
                                                 job=_join_jobs(_gather_job_b(list(ex[4:])), _gather_job_a(sh[1][2:])))
            W[1].update(zip(("win", "wout"), [whole(t) for t in ex[:2]]))
            a2 = list(ex[2:])
        else:
            mixed, rtot, ex = _sb_fwd(qkv, job=_gather_job_b(a2))
            W[1].update(zip(("wgT", "wuT", "wd"), [whole(t) for t in ex]))
            mixed, lse_d, _ = _flash_fwd(qkv, mixed, 1, False, (tbl,))
            mixed, lse_f, o_fox, _ = _flash_fwd(qkv, mixed, 2, True, (cq, ckb))
        Wout, WgT, WuT, Wd = W[l]["wout"], W[l]["wgT"], W[l]["wuT"], W[l]["wd"]
        mixed = _conv_fwd(conv, cw8[l], mixed)
        x1, xh1, r1, x1b = _mm_ln(mixed, Wout, h, ln1_g[l:l + 1], ln1_b[l:l + 1], "out_proj_ln")
        fs, ft, a, x2, xh2, r2, x2b = _ffn_fwd(x1b, x1, WgT, WuT, Wd, ln2_g[l:l + 1], ln2_b[l:l + 1])
        saved.append(dict(h=hb, qkv=qkv, conv=conv, gate=gate, cq=cq, ckb=ckb, mixed=mixed, rtot=rtot, lse_d=lse_d,
                          lse_f=lse_f, o_fox=o_fox, x1=x1b, xh1=xh1, r1=r1, fs=fs, ft=ft, a=a, xh2=xh2, r2=r2))
        h, hb = x2, x2b

    dy = h

    def view(gr):
        return gr.reshape(4, 2, gr.shape[0] // NDEV, gr.shape[1])

    G = [None, None]
    small_g = {}
    shard_g = {}
    for l in (1, 0):
        sv = saved[l]
        Win, Wout, WgT, WuT, Wd = W[l]["win"], W[l]["wout"], W[l]["wgT"], W[l]["wuT"], W[l]["wd"]
        res = _ffn_bwd(dy, sv["xh2"], sv["r2"], ln2_g[l:l + 1], sv["fs"], sv["ft"], Wd, WgT, WuT,
                       target=loss_target.reshape(T, D) if l == 1 else None)
        dgt, dut, ds2b, dx1, dg2, db2 = res[:6]
        if l == 1:
            sq = res[6]
        G_d = _mm_tn(sv["a"], ds2b, None, C=D, Ka=DFF, N=D, tm=256, tn=1024, tk=T, ooff=0, name="grad_w_down")
        G_g = _mm_tn(dgt, sv["x1"], None, C=D, Ka=DFF, N=D, tm=256, tn=1024, tk=T, ooff=0, name="grad_w_gate")
        G_u = _mm_tn(dut, sv["x1"], None, C=D, Ka=DFF, N=D, tm=256, tn=1024, tk=T, ooff=0, name="grad_w_up")
        ds1, dg1, db1, ds1b, dmixed = _ln_bwd(dx1, sv["xh1"], sv["r1"], ln1_g[l:l + 1], Wout)
        G_out = _mm_tn(sv["mixed"], ds1b, None, C=D, Ka=D, N=D, tm=256, tn=1024, tk=T, ooff=0, name="grad_w_out")
        early = [view(t) for t in (G_g, G_u, G_d, G_out)] + ([view(G[1]["in"])] if l == 0 else [])
        dqkv, gots = _sb_bwd(sv["qkv"], dmixed, sv["rtot"], job=_sibling_job(early))
        ps = _pair_sums(early, list(gots), core)
        dqkv, dtbl, pa = _flash_bwd(sv["qkv"], sv["mixed"], dmixed, sv["lse_d"], dqkv, 1, False, (tbl,),
                                    job=_chip_job(ps[:2]))
        dqkv, dck, pb = _flash_bwd(sv["qkv"], sv["o_fox"], dmixed, sv["lse_f"], dqkv, 2, True,
                                   (sv["cq"], sv["ckb"]), job=_chip_job(ps[2:]))
        sums = _chip_sums(list(pa) + list(pb))
        shard_g[l] = dict(zip(("g", "u", "d", "out"), sums[:4]))
        if l == 0:
            shard_g[1]["in"] = sums[4]
        dconv, dcw = _conv_bwd(sv["conv"], cw8[l], dmixed)
        dcum = jnp.pad(dck.reshape(S, BL, NH).transpose(1, 0, 2).reshape(T, NH), ((0, 0), (0, GATEW - NH)))
        dgate, dfb = _fox_post(dcum, sv["gate"], fb[l])
        drb = _dil_table_bwd(dtbl)
        G_in = _mm_tn(sv["h"], dqkv, None, C=NPAD, Ka=D, N=QKVW, tm=512, tn=768, tk=T, ooff=0, name="grad_w_in_qkv")
        G_in = _mm_tn(sv["h"], dconv, G_in, C=NPAD, Ka=D, N=CONVW, tm=256, tn=768, tk=T, ooff=3,
                      name="grad_w_in_conv")
        G_in = _mm_tn(sv["h"], dgate, G_in, C=NPAD, Ka=D, N=GATEW, tm=1024, tn=128, tk=1024, ooff=24,
                      name="grad_w_in_gate")
        G[l] = {"in": G_in, "out": G_out, "g": G_g, "u": G_u, "d": G_d}
        tail = None
        if l == 0:
            late = [view(G_in)]
            tail = _chip_job(_pair_sums(late, list(_run_job(_sibling_job(late), "sibling_exchange")), core))
        dy = _mm([(dqkv, arow(1024, QKVW), Win, wrow(512, QKVW, 0)),
                  (dconv, arow(1024, CONVW), Win, wrow(512, CONVW, 3)),
                  (dgate, arow(1024, GATEW), Win, wrow(512, GATEW, 24))],
                 nt=True, M=T, N=D, tm=1024, tn=512, out_dtype=F32, name="proj_dx", res=ds1, res_scale=ALPHA, job=tail)
        if l == 0:
            dy, parts = dy
            shard_g[0]["in"] = _chip_sums(list(parts))[0]
        small_g[l] = dict(ln1_g=dg1, ln1_b=db1, ln2_g=dg2, ln2_b=db2, cw=dcw[0:3].reshape(1, CONVW),
                          fb=dfb[:, :NH], rb=drb[:, :NH])
    grad_x = dy.reshape(BL, S, D)

    rows = []
    for name in ("ln1_g", "ln1_b", "ln2_g", "ln2_b"):
        rows += [small_g[0][name], small_g[1][name]]
    rows += [_row(small_g[0]["cw"]), _row(small_g[1]["cw"]),
             _row(jnp.concatenate([small_g[0]["fb"], small_g[1]["fb"]], axis=0)),
             _row(small_g[0]["rb"] + small_g[1]["rb"]), _row(sq)]
    rows.append(jnp.zeros((SMALL_ROWS - len(rows), D), F32))
    sg = _allreduce_small(jnp.concatenate(rows, axis=0))
    loss = sg[12, 0] * (0.5 / D)
    g_ln1_g, g_ln1_b, g_ln2_g, g_ln2_b = sg[0:2], sg[2:4], sg[4:6], sg[6:8]
    g_conv_full = sg[8:10, :CONVW].reshape(2, 3, 256)
    g_conv = lax.dynamic_slice(g_conv_full, (0, 0, me * 32), (2, 3, 32))
    g_fb = sg[10, :2 * NH].reshape(2, NH)
    g_rb = sg[11, :32 * NH].reshape(32, NH)

    def both(name):
        return jnp.stack([shard_g[0][name], shard_g[1][name]])

    g_in = both("in")
    g_w_in = jnp.concatenate([_unpermute_in(g_in[..., :QKVW]), g_in[..., QKVW:NPROJ]], axis=-1)
    g_w_out = both("out")
    g_w_gate = jnp.swapaxes(both("g"), 1, 2)
    g_w_up = jnp.swapaxes(both("u"), 1, 2)
    g_w_down = both("d")

    up_in = _adamw(w_in, g_w_in, m_w_in, v_w_in, 64)
    up_out = _adamw(w_out, g_w_out, m_w_out, v_w_out, 128)
    up_gate = _adamw(w_gate, g_w_gate, m_w_gate, v_w_gate, 256)
    up_up = _adamw(w_up, g_w_up, m_w_up, v_w_up, 256)
    up_down = _adamw(w_down, g_w_down, m_w_down, v_w_down, 352)

    def pack(fbv, cwv, rbv, l1g, l1b, l2g, l2b):
        r = [l1g, l1b, l2g, l2b, _row(cwv), _row(fbv), _row(rbv)]
        r.append(jnp.zeros((SMALL_ROWS - 11, D), F32))
        return jnp.concatenate(r, axis=0)

    pw = pack(f_bias, conv_w, rel_bias, ln1_g, ln1_b, ln2_g, ln2_b)
    pg = pack(g_fb, g_conv, g_rb, g_ln1_g, g_ln1_b, g_ln2_g, g_ln2_b)
    pm = pack(m_f_bias, m_conv_w, m_rel_bias, m_ln1_g, m_ln1_b, m_ln2_g, m_ln2_b)
    pv = pack(v_f_bias, v_conv_w, v_rel_bias, v_ln1_g, v_ln1_b, v_ln2_g, v_ln2_b)
    ups = [u[0] for u in _adamw(pw[None], pg[None], pm[None], pv[None], SMALL_ROWS)]

    def unpack(p):
        return dict(ln1_g=p[0:2], ln1_b=p[2:4], ln2_g=p[4:6], ln2_b=p[6:8],
                    conv_w=p[8, :192].reshape(2, 3, 32), f_bias=p[9, :2 * NH].reshape(2, NH),
                    rel_bias=p[10, :32 * NH].reshape(32, NH))

    sm = [unpack(p) for p in ups]

    def group(k):
        return (up_in[k], sm[k]["f_bias"], sm[k]["conv_w"], up_out[k], sm[k]["rel_bias"], sm[k]["ln1_g"],
                sm[k]["ln1_b"], up_gate[k], up_up[k], up_down[k], sm[k]["ln2_g"], sm[k]["ln2_b"])

    grads = (g_w_in, g_fb, g_conv, g_w_out, g_rb, g_ln1_g, g_ln1_b, g_w_gate, g_w_up, g_w_down, g_ln2_g, g_ln2_b)
    return (loss, grad_x) + grads + group(0) + group(1) + group(2)
```

```python
import math

import numpy as np
import jax
import jax.numpy as jnp
from jax import lax
from jax.experimental import pallas as pl
from jax.experimental.pallas import tpu as pltpu

F32 = jnp.float32
BF16 = jnp.bfloat16
MESH = pl.DeviceIdType.MESH

D = 1024
S = 2048
BL = 2
T = BL * S
NH = 4
DFF = 2816
NPROJ = 3076
NPAD = 3200
QKVW = 2304
CONVW = 768
GATEW = 128
PAIRW = 384
BQ = 128
HB = 2 * BQ
NB = S // BQ
NDEV = 8
NSTAT = BL * NH
ALPHA = 4.0 ** 0.25
SCALE = 0.125
NEG = -1e30
LN_EPS = 1e-5
ADAM_LR, ADAM_B1, ADAM_B2, ADAM_EPS, ADAM_WD, ADAM_STEP = 0.001, 0.9, 0.999, 1e-08, 0.01, 10
VMEM_LIMIT = 56 * 1024 * 1024
SMALL_ROWS = 16


def _bucket_thresholds():
    d = np.arange(0, S)
    nf = np.maximum(d, 1).astype(np.float32)
    large = 16 + (np.log(nf / np.float32(16)) / np.float32(math.log(128)) * np.float32(16)).astype(np.int32)
    b = np.where(d < 16, d, np.minimum(large, 31))
    return [int(np.argmax(b >= k)) for k in range(32)]


BUCKET_TH = _bucket_thresholds()


def _cp(sem=None, vmem=VMEM_LIMIT):
    return pltpu.CompilerParams(dimension_semantics=sem, vmem_limit_bytes=vmem)


def _dot(a, b):
    return lax.dot_general(a, b, (((1,), (0,)), ((), ())), preferred_element_type=F32)


def _dot_nt(a, b):
    return lax.dot_general(a, b, (((1,), (1,)), ((), ())), preferred_element_type=F32)


def _dot_tn(a, b):
    return lax.dot_general(a, b, (((0,), (0,)), ((), ())), preferred_element_type=F32)


def _split2(x):
    hi = x.astype(BF16)
    mid = (x - hi.astype(F32)).astype(BF16)
    return jnp.concatenate([hi, mid], axis=1)


def _split3(x):
    hi = x.astype(BF16)
    r = x - hi.astype(F32)
    mid = r.astype(BF16)
    lo = (r - mid.astype(F32)).astype(BF16)
    return jnp.concatenate([hi, mid, lo], axis=1)


def _log_sigmoid(u):
    return jnp.minimum(u, 0.0) - jnp.log1p(jnp.exp(-jnp.abs(u)))


def _log_sigmoid_tile(u):
    return jnp.minimum(u, 0.0) - jnp.log(1.0 + jnp.exp(jnp.minimum(u, -u)))


def _iota(shape, dim):
    return lax.broadcasted_iota(jnp.int32, shape, dim)


ANY_SPEC = pl.BlockSpec(memory_space=pl.ANY)
VMEM_SPEC = pl.BlockSpec(memory_space=pltpu.VMEM)


def _mm(pairs, *, nt, M, N, tm, tn, out_dtype, name, res=None, res_scale=1.0, job=None):
    n = len(pairs)
    n_in = 2 * n + (res is not None)
    jins = job.ins if job else []
    jouts = job.out_shapes if job else []
    gi, gj = M // tm, N // tn

    def body(*refs):
        o_ref = refs[n_in + len(jins)]
        if job:
            jrefs = (refs[n_in:n_in + len(jins)], refs[n_in + len(jins) + 1:n_in + len(jins) + 1 + len(jouts)],
                     refs[n_in + len(jins) + 1 + len(jouts):])

            @pl.when((pl.program_id(0) == 0) & (pl.program_id(1) == 0))
            def _():
                job.start(*jrefs)

        acc = None
        for p in range(n):
            a = refs[2 * p][...].astype(BF16)
            b = refs[2 * p + 1][...]
            d = _dot_nt(a, b) if nt else _dot(a, b)
            acc = d if acc is None else acc + d
        if res is not None:
            acc = acc + res_scale * refs[2 * n][...]
        o_ref[...] = acc.astype(out_dtype)
        if job:
            @pl.when((pl.program_id(0) == gi - 1) & (pl.program_id(1) == gj - 1))
            def _():
                job.finish(None, *jrefs)

    ops, specs = [], []
    for a, asp, b, bsp in pairs:
        ops += [a, b]
        specs += [asp, bsp]
    if res is not None:
        ops.append(res)
        specs.append(pl.BlockSpec((tm, tn), lambda i, j: (i, j)))
    out = pl.pallas_call(
        body, name=name, grid=(gi, gj), in_specs=specs + [ANY_SPEC] * len(jins),
        out_specs=[pl.BlockSpec((tm, tn), lambda i, j: (i, j))] + [ANY_SPEC] * len(jouts),
        out_shape=[jax.ShapeDtypeStruct((M, N), out_dtype)] + list(jouts),
        scratch_shapes=list(job.sems) if job else [],
        input_output_aliases={n_in + a: 1 + b for a, b in job.aliases.items()} if job else {},
        compiler_params=_cp(("arbitrary", "arbitrary") if job else ("parallel", "parallel")))(*ops, *jins)
    return (out[0], out[1:]) if job else out[0]


def _mm_tn(a, b, gbuf, *, C, Ka, N, tm, tn, tk, ooff, name):
    def body(*refs):
        a_ref, b_ref, o_ref = refs[0], refs[1], refs[-1]
        k = pl.program_id(2)
        d = _dot_tn(a_ref[...].astype(BF16), b_ref[...].astype(BF16))

        @pl.when(k == 0)
        def _():
            o_ref[...] = d

        @pl.when(k > 0)
        def _():
            o_ref[...] += d

    ops = [a, b] + ([] if gbuf is None else [gbuf])
    return pl.pallas_call(
        body, name=name, grid=(Ka // tm, N // tn, T // tk),
        in_specs=[pl.BlockSpec((tk, tm), lambda i, j, k: (k, i)),
                  pl.BlockSpec((tk, tn), lambda i, j, k: (k, j))] + ([] if gbuf is None else [ANY_SPEC]),
        out_specs=pl.BlockSpec((tm, tn), lambda i, j, k: (i, ooff + j)),
        out_shape=jax.ShapeDtypeStruct((Ka, C), F32),
        input_output_aliases={} if gbuf is None else {2: 0},
        compiler_params=_cp(("parallel", "parallel", "arbitrary")))(*ops)


def _proj(xb, w):
    tm = 512

    def body(x_ref, w_ref, qkv_ref, conv_ref, gate_ref):
        xv = x_ref[...]
        qkv_ref[...] = _dot(xv, w_ref[:, 0:QKVW]).astype(BF16)
        conv_ref[...] = _dot(xv, w_ref[:, QKVW:QKVW + CONVW])
        gate_ref[...] = _dot(xv, w_ref[:, QKVW + CONVW:NPAD])

    def rows(n):
        return pl.BlockSpec((tm, n), lambda i: (i, 0))

    return pl.pallas_call(
        body, name="proj", grid=(T // tm,),
        in_specs=[rows(D), pl.BlockSpec((D, NPAD), lambda i: (0, 0))],
        out_specs=[rows(QKVW), rows(CONVW), rows(GATEW)],
        out_shape=[jax.ShapeDtypeStruct((T, QKVW), BF16), jax.ShapeDtypeStruct((T, CONVW), F32),
                   jax.ShapeDtypeStruct((T, GATEW), F32)],
        compiler_params=_cp(("parallel",)))(xb, w)


def _proj_bwd(dqkv, dconv, dgate, w, res):
    tm = 512

    def body(a_ref, b_ref, c_ref, w_ref, r_ref, o_ref):
        acc = ALPHA * r_ref[...] + _dot_nt(a_ref[...], w_ref[:, 0:QKVW])
        acc = acc + _dot_nt(b_ref[...], w_ref[:, QKVW:QKVW + CONVW])
        o_ref[...] = acc + _dot_nt(c_ref[...].astype(BF16), w_ref[:, QKVW + CONVW:NPAD])

    def rows(n):
        return pl.BlockSpec((tm, n), lambda i: (i, 0))

    return pl.pallas_call(
        body, name="proj_bwd", grid=(T // tm,),
        in_specs=[rows(QKVW), rows(CONVW), rows(GATEW), pl.BlockSpec((D, NPAD), lambda i: (0, 0)), rows(D)],
        out_specs=rows(D), out_shape=jax.ShapeDtypeStruct((T, D), F32),
        compiler_params=_cp(("parallel",)))(dqkv, dconv, dgate, w, res)


def _ffn_fwd(xb, x, wgt, wut, wd, gam, bet):
    tm, ch = 512, 256

    def body(xb_ref, x_ref, g_ref, b_ref, wg_hbm, wu_hbm, wd_hbm,
             go_ref, uo_ref, ao_ref, y_ref, xh_ref, r_ref, yb_ref, wg_v, wu_v, wd_v, sem):
        loads = [pltpu.make_async_copy(s, d, sem.at[k])
                 for k, (s, d) in enumerate(((wg_hbm, wg_v), (wu_hbm, wu_v), (wd_hbm, wd_v)))]

        @pl.when(pl.program_id(0) == 0)
        def _():
            for cp in loads:
                cp.start()
            loads[0].wait()
            loads[1].wait()

        xv = xb_ref[...]
        for c in range(0, DFF, ch):
            gv = _dot_nt(xv, wg_v[c:c + ch, :])
            uv = _dot_nt(xv, wu_v[c:c + ch, :])
            go_ref[:, c:c + ch] = gv.astype(BF16)
            uo_ref[:, c:c + ch] = uv.astype(BF16)
            ao_ref[:, c:c + ch] = (gv * jax.nn.sigmoid(gv) * uv).astype(BF16)
        @pl.when(pl.program_id(0) == 0)
        def _():
            loads[2].wait()

        s = ALPHA * x_ref[...] + _dot(ao_ref[...], wd_v[...])
        mu = jnp.mean(s, axis=-1, keepdims=True)
        xc = s - mu
        var = jnp.mean(xc * xc, axis=-1, keepdims=True)
        r = lax.rsqrt(var + LN_EPS)
        xh = xc * r
        xh_ref[...] = xh.astype(BF16)
        r_ref[...] = r
        y = xh * g_ref[...] + b_ref[...]
        y_ref[...] = y
        yb_ref[...] = y.astype(BF16)

    row = pl.BlockSpec((tm, D), lambda i: (i, 0))
    wide = pl.BlockSpec((tm, DFF), lambda i: (i, 0))
    vec = pl.BlockSpec((1, D), lambda i: (0, 0))
    wsl = pltpu.VMEM((DFF, D), BF16)
    hid = jax.ShapeDtypeStruct((T, DFF), BF16)
    return pl.pallas_call(
        body, name="ffn_fwd", grid=(T // tm,),
        in_specs=[row, row, vec, vec, ANY_SPEC, ANY_SPEC, ANY_SPEC],
        out_specs=[wide, wide, wide, row, row, pl.BlockSpec((tm, 1), lambda i: (i, 0)), row],
        out_shape=[hid, hid, hid, jax.ShapeDtypeStruct((T, D), F32), jax.ShapeDtypeStruct((T, D), BF16),
                   jax.ShapeDtypeStruct((T, 1), F32), jax.ShapeDtypeStruct((T, D), BF16)],
        scratch_shapes=[wsl, wsl, wsl, pltpu.SemaphoreType.DMA((3,))],
        compiler_params=_cp(("arbitrary",)))(xb, x, gam, bet, wgt, wut, wd)


def _ffn_bwd(dy, xh, r, gam, g, u, wd, wgt, wut, target=None):
    tm, ch = 256, 256

    def body(*refs):
        if target is None:
            (dy_ref, xh_ref, r_ref, gam_ref, g_ref, u_ref, wd_hbm, wg_hbm, wu_hbm,
             dg_ref, du_ref, dsb_ref, dx_ref, dgam_ref, dbet_ref, wd_v, wg_v, wu_v, sem) = refs
        else:
            (dy_ref, t_ref, xh_ref, r_ref, gam_ref, g_ref, u_ref, wd_hbm, wg_hbm, wu_hbm,
             dg_ref, du_ref, dsb_ref, dx_ref, dgam_ref, dbet_ref, sq_ref, wd_v, wg_v, wu_v, sem) = refs
        loads = [pltpu.make_async_copy(s, d, sem.at[k])
                 for k, (s, d) in enumerate(((wd_hbm, wd_v), (wg_hbm, wg_v), (wu_hbm, wu_v)))]

        @pl.when(pl.program_id(0) == 0)
        def _():
            for cp in loads:
                cp.start()
            loads[0].wait()

        if target is None:
            dyv = dy_ref[...]
        else:
            e = dy_ref[...] - t_ref[...]
            dyv = e * (1.0 / D)
            p = jnp.sum(jnp.sum(e * e, axis=1, keepdims=True), axis=0, keepdims=True)

            @pl.when(pl.program_id(0) == 0)
            def _():
                sq_ref[...] = p

            @pl.when(pl.program_id(0) > 0)
            def _():
                sq_ref[...] += p

        xhv = xh_ref[...].astype(F32)
        dxh = dyv * gam_ref[...]
        m1 = jnp.mean(dxh, axis=-1, keepdims=True)
        m2 = jnp.mean(dxh * xhv, axis=-1, keepdims=True)
        ds = r_ref[...] * (dxh - m1 - xhv * m2)
        pg = jnp.sum(dyv * xhv, axis=0, keepdims=True)
        pb = jnp.sum(dyv, axis=0, keepdims=True)

        @pl.when(pl.program_id(0) == 0)
        def _():
            dgam_ref[...] = pg
            dbet_ref[...] = pb

        @pl.when(pl.program_id(0) > 0)
        def _():
            dgam_ref[...] += pg
            dbet_ref[...] += pb

        db = ds.astype(BF16)
        dsb_ref[...] = db
        for c in range(0, DFF, ch):
            da = _dot_nt(db, wd_v[c:c + ch, :])
            gv = g_ref[:, c:c + ch].astype(F32)
            sg = jax.nn.sigmoid(gv)
            dg_ref[:, c:c + ch] = (da * u_ref[:, c:c + ch].astype(F32) * (sg * (1.0 + gv * (1.0 - sg)))).astype(BF16)
            du_ref[:, c:c + ch] = (da * (gv * sg)).astype(BF16)
        @pl.when(pl.program_id(0) == 0)
        def _():
            loads[1].wait()
            loads[2].wait()

        dx_ref[...] = ALPHA * ds + _dot(dg_ref[...], wg_v[...]) + _dot(du_ref[...], wu_v[...])

    row = pl.BlockSpec((tm, D), lambda i: (i, 0))
    wide = pl.BlockSpec((tm, DFF), lambda i: (i, 0))
    vec = pl.BlockSpec((1, D), lambda i: (0, 0))
    wsl = pltpu.VMEM((DFF, D), BF16)
    last = target is not None
    return pl.pallas_call(
        body, name="ffn_bwd_loss" if last else "ffn_bwd", grid=(T // tm,),
        in_specs=[row] + ([row] if last else [])
        + [row, pl.BlockSpec((tm, 1), lambda i: (i, 0)), vec, wide, wide, ANY_SPEC, ANY_SPEC, ANY_SPEC],
        out_specs=[wide, wide, row, row, vec, vec] + ([pl.BlockSpec((1, 1), lambda i: (0, 0))] if last else []),
        out_shape=[jax.ShapeDtypeStruct((T, DFF), BF16), jax.ShapeDtypeStruct((T, DFF), BF16),
                   jax.ShapeDtypeStruct((T, D), BF16), jax.ShapeDtypeStruct((T, D), F32),
                   jax.ShapeDtypeStruct((1, D), F32), jax.ShapeDtypeStruct((1, D), F32)]
        + ([jax.ShapeDtypeStruct((1, 1), F32)] if last else []),
        scratch_shapes=[wsl, wsl, wsl, pltpu.SemaphoreType.DMA((3,))],
        compiler_params=_cp(("arbitrary",)))(dy, *([target] if last else []), xh, r, gam, g, u, wd, wgt, wut)


def _mm_ln(a, w, x, gam, bet, name):
    tm = 256
    K = a.shape[1]

    def body(a_ref, w_ref, x_ref, g_ref, b_ref, y_ref, xh_ref, r_ref, yb_ref):
        s = ALPHA * x_ref[...] + _dot(a_ref[...], w_ref[...])
        mu = jnp.mean(s, axis=-1, keepdims=True)
        xc = s - mu
        var = jnp.mean(xc * xc, axis=-1, keepdims=True)
        r = lax.rsqrt(var + LN_EPS)
        xh = xc * r
        xh_ref[...] = xh.astype(BF16)
        r_ref[...] = r
        y = xh * g_ref[...] + b_ref[...]
        y_ref[...] = y
        yb_ref[...] = y.astype(BF16)

    row = pl.BlockSpec((tm, D), lambda i: (i, 0))
    vec = pl.BlockSpec((1, D), lambda i: (0, 0))
    return pl.pallas_call(
        body, name=name, grid=(T // tm,),
        in_specs=[pl.BlockSpec((tm, K), lambda i: (i, 0)), pl.BlockSpec((K, D), lambda i: (0, 0)), row, vec, vec],
        out_specs=[row, row, pl.BlockSpec((tm, 1), lambda i: (i, 0)), row],
        out_shape=[jax.ShapeDtypeStruct((T, D), F32), jax.ShapeDtypeStruct((T, D), BF16),
                   jax.ShapeDtypeStruct((T, 1), F32), jax.ShapeDtypeStruct((T, D), BF16)],
        compiler_params=_cp(("parallel",)))(a, w, x, gam, bet)


def _ln_bwd(dy, xh, r, gam, w):
    tm = 256

    def body(dy_ref, xh_ref, r_ref, g_ref, w_ref, ds_ref, dg_ref, db_ref, dsb_ref, dm_ref):
        i = pl.program_id(0)
        dyv = dy_ref[...]
        xhv = xh_ref[...].astype(F32)
        dxh = dyv * g_ref[...]
        m1 = jnp.mean(dxh, axis=-1, keepdims=True)
        m2 = jnp.mean(dxh * xhv, axis=-1, keepdims=True)
        ds = r_ref[...] * (dxh - m1 - xhv * m2)
        ds_ref[...] = ds
        dsb = ds.astype(BF16)
        dsb_ref[...] = dsb
        dm_ref[...] = _dot_nt(dsb, w_ref[...]).astype(BF16)
        pg = jnp.sum(dyv * xhv, axis=0, keepdims=True)
        pb = jnp.sum(dyv, axis=0, keepdims=True)

        @pl.when(i == 0)
        def _():
            dg_ref[...] = pg
            db_ref[...] = pb

        @pl.when(i > 0)
        def _():
            dg_ref[...] += pg
            db_ref[...] += pb

    row = pl.BlockSpec((tm, D), lambda i: (i, 0))
    vec = pl.BlockSpec((1, D), lambda i: (0, 0))
    return pl.pallas_call(
        body, name="ln_bwd_proj", grid=(T // tm,),
        in_specs=[row, row, pl.BlockSpec((tm, 1), lambda i: (i, 0)), vec, pl.BlockSpec((D, D), lambda i: (0, 0))],
        out_specs=[row, vec, vec, row, row],
        out_shape=[jax.ShapeDtypeStruct((T, D), F32), jax.ShapeDtypeStruct((1, D), F32),
                   jax.ShapeDtypeStruct((1, D), F32), jax.ShapeDtypeStruct((T, D), BF16),
                   jax.ShapeDtypeStruct((T, D), BF16)],
        compiler_params=_cp(("arbitrary",)))(dy, xh, r, gam, w)


def _adamw(w, g, m, v, tr):
    L, R, C = w.shape

    def body(w_ref, g_ref, m_ref, v_ref, d_ref, m2_ref, v2_ref):
        gv = g_ref[...]
        m2 = ADAM_B1 * m_ref[...] + (1.0 - ADAM_B1) * gv
        v2 = ADAM_B2 * v_ref[...] + (1.0 - ADAM_B2) * (gv * gv)
        m_hat = m2 / (1.0 - ADAM_B1 ** ADAM_STEP)
        v_hat = v2 / (1.0 - ADAM_B2 ** ADAM_STEP)
        d_ref[...] = -ADAM_LR * (m_hat / (jnp.sqrt(v_hat) + ADAM_EPS) + ADAM_WD * w_ref[...])
        m2_ref[...] = m2
        v2_ref[...] = v2

    blk = pl.BlockSpec((None, tr, C), lambda l, i: (l, i, 0))
    sh = jax.ShapeDtypeStruct((L, R, C), F32)
    return pl.pallas_call(
        body, name="adamw", grid=(L, R // tr), in_specs=[blk] * 4, out_specs=[blk] * 3,
        out_shape=[sh, sh, sh], compiler_params=_cp(("parallel", "parallel")))(w, g, m, v)


class _Job:
    def __init__(self, ins, out_shapes, aliases, sems, start, finish):
        self.ins, self.out_shapes, self.aliases, self.sems = list(ins), list(out_shapes), dict(aliases), list(sems)
        self.start, self.finish = start, finish


def _host_call(body, name, ins, in_specs, out_shapes, out_specs, scratch, aliases, job):
    n_in, n_out, n_scr = len(ins), len(out_shapes), len(scratch)
    jins = job.ins if job else []
    jouts = job.out_shapes if job else []
    jsems = job.sems if job else []

    def wrapped(*refs):
        a = n_in
        b = a + len(jins)
        c = b + n_out
        d = c + len(jouts)
        e = d + n_scr
        comm = None
        if job:
            jrefs = (refs[a:b], refs[c:d], refs[e:])
            comm = (lambda: job.start(*jrefs), lambda st: job.finish(st, *jrefs))
        body(refs[:a], refs[b:c], refs[d:e], comm)

    al = dict(aliases)
    if job:
        for ji, jo in job.aliases.items():
            al[n_in + ji] = n_out + jo
    res = pl.pallas_call(
        wrapped, name=name, in_specs=list(in_specs) + [ANY_SPEC] * len(jins),
        out_specs=list(out_specs) + [ANY_SPEC] * len(jouts), out_shape=list(out_shapes) + list(jouts),
        scratch_shapes=list(scratch) + list(jsems), input_output_aliases=al,
        compiler_params=_cp())(*ins, *jins)
    return res[:n_out], res[n_out:]


def _copy_in(src, dst, sem):
    cp = pltpu.make_async_copy(src, dst, sem)
    cp.start()
    cp.wait()


CHAINS = [(p, b) for p in range(2) for b in range(BL)]
NC = len(CHAINS)
ROWS_SHAPE = jax.ShapeDtypeStruct((NSTAT, S), F32)
SLAB_QKV = pltpu.VMEM((T, 2 * PAIRW), BF16)
SLAB_OUT = pltpu.VMEM((T, 2 * BQ), BF16)
SLAB_O32 = pltpu.VMEM((T, 2 * BQ), F32)
SLAB_T = pltpu.VMEM((2, BQ, T), BF16)
SLAB_KEYB = pltpu.VMEM((NSTAT, S, BQ), F32)
ACC_KV = pltpu.VMEM((2, T, BQ), F32)


def _lane_masks():
    lane = _iota((1, BQ), 1)
    m0 = (lane < 64).astype(BF16)
    return m0, 1.0 - m0


def _row_masks():
    r = _iota((BQ, 1), 0)
    m0 = (r < 64).astype(BF16)
    return m0, 1.0 - m0


def _stack(x, m0, m1):
    return jnp.concatenate([x * m0, x * m1], axis=0)


def _stack_t(xt, r0, r1):
    return jnp.concatenate([xt * r0, xt * r1], axis=1)


def _tr(x):
    return x.T


def _rows(b, i):
    return pl.ds(pl.multiple_of(b * S + i * BQ, BQ), BQ)


def _transpose_slab(src, dst, col0):
    def blk(n, _):
        r = pl.ds(pl.multiple_of(n * BQ, BQ), BQ)
        for p in range(2):
            dst[p, :, r] = _tr(src[r, col0(p):col0(p) + BQ])
        return 0

    lax.fori_loop(0, T // BQ, blk, 0)


def _heads(x):
    return x[:BQ], x[BQ:]


def _bcast_heads(r0, r1):
    return jnp.concatenate([jnp.broadcast_to(r0, (BQ, BQ)), jnp.broadcast_to(r1, (BQ, BQ))], axis=0)


def _by_channel(r0, r1):
    return jnp.where(_iota((BQ, BQ), 0) < 64, r0, r1)


def _colsum2(x):
    return jnp.sum(x[:BQ], axis=0, keepdims=True), jnp.sum(x[BQ:], axis=0, keepdims=True)


def _stat_row(ref, p, b, h, i):
    c = b * NH + 2 * p + h
    return ref[c:c + 1, pl.ds(pl.multiple_of(i * BQ, BQ), BQ)]


def _put_row(ref, p, b, h, i, v):
    c = b * NH + 2 * p + h
    ref[c:c + 1, pl.ds(pl.multiple_of(i * BQ, BQ), BQ)] = v


def _valid_t(strict):
    r = _iota((HB, BQ), 0) & (BQ - 1)
    c = _iota((HB, BQ), 1)
    return (r < c) if strict else (r <= c)


def _tri_blockdiag(later):
    r = _iota((HB, HB), 0)
    c = _iota((HB, HB), 1)
    same = (r >= BQ) == (c >= BQ)
    return (same & ((c > r) if later else (c < r))).astype(BF16)


def _cum_mm(tri, x):
    y = _dot(tri, _split2(x))
    return y[:, :BQ] + y[:, BQ:]


def _kv_tiles(qkv_v, p, b, j):
    r = _rows(b, j)
    return qkv_v[r, p * PAIRW + BQ:p * PAIRW + 2 * BQ], qkv_v[r, p * PAIRW + 2 * BQ:p * PAIRW + 3 * BQ]


def _q_tile(qkv_v, p, b, i):
    return qkv_v[_rows(b, i), p * PAIRW:p * PAIRW + BQ] * SCALE


def _sb_fwd(qkv, job=None):
    def body(ins, outs, scr, comm):
        (qkv_hbm,), (o_hbm, r_ref), (qkv_v, o_v, sem, vt_v) = ins, outs, scr
        _copy_in(qkv_hbm.at[:, pl.ds(0, 2 * PAIRW)], qkv_v, sem)
        st = comm[0]() if comm else None
        _transpose_slab(qkv_v, vt_v, lambda p: p * PAIRW + 2 * BQ)
        m0, m1 = _lane_masks()
        r0, r1 = _row_masks()
        valid = _valid_t(True)
        later = _tri_blockdiag(True)

        def steps(qts, i, j, cs, diag):
            ks = [_stack(_kv_tiles(qkv_v, p, b, j)[0], m0, m1) for p, b in CHAINS]
            zs = [_dot(ks[c], qts[c]) for c in range(NC)]
            lbs, lrs = [], []
            for c in range(NC):
                lb = _log_sigmoid_tile(zs[c])
                lr = lb - zs[c]
                if diag:
                    lr = jnp.where(valid, lr, 0.0)
                lbs.append(lb)
                lrs.append(lr)
            tails = [_cum_mm(later, lrs[c]) for c in range(NC)]
            avs = []
            for c in range(NC):
                a = jnp.exp(lbs[c] + tails[c] + _bcast_heads(*cs[c][0]))
                if diag:
                    a = jnp.where(valid, a, 0.0)
                avs.append(a.astype(BF16))
            out = []
            for c, (p, b) in enumerate(CHAINS):
                vts = _stack_t(vt_v[p, :, _rows(b, j)], r0, r1)
                s0, s1 = _colsum2(lrs[c])
                out.append(((cs[c][0][0] + s0, cs[c][0][1] + s1), cs[c][1] + _dot(vts, avs[c])))
            return tuple(out)

        def qblock(i, _):
            qts = [_tr(_q_tile(qkv_v, p, b, i)) for p, b in CHAINS]
            zr = jnp.zeros((1, BQ), F32)
            cs = steps(qts, i, i, (((zr, zr), jnp.zeros((BQ, BQ), F32)),) * NC, True)
            cs = lax.fori_loop(1, i + 1, lambda jj, cs: steps(qts, i, i - jj, cs, False), cs)
            for c, (p, b) in enumerate(CHAINS):
                o_v[_rows(b, i), p * BQ:(p + 1) * BQ] = cs[c][1].T.astype(BF16)
                for h in range(2):
                    _put_row(r_ref, p, b, h, i, cs[c][0][h])
            return 0

        lax.fori_loop(0, NB, qblock, 0)
        _copy_in(o_v, o_hbm.at[:, pl.ds(0, 2 * BQ)], sem)
        if comm:
            comm[1](st)

    (mixed, rtot), extra = _host_call(
        body, "sb_fwd", [qkv], [ANY_SPEC], [jax.ShapeDtypeStruct((T, D), BF16), ROWS_SHAPE], [ANY_SPEC, VMEM_SPEC],
        [SLAB_QKV, SLAB_OUT, pltpu.SemaphoreType.DMA, SLAB_T], {}, job)
    return mixed, rtot, extra


def _sb_bwd(qkv, dmixed, rtot, job=None):
    def body(ins, outs, scr, comm):
        (qkv_hbm, do_hbm, r_ref), (dqkv_hbm,), (qkv_v, do_v, dq_v, dk_s, dv_s, sems, kt_v) = ins, outs, scr
        sem = sems.at[0]
        later = [pltpu.make_async_copy(do_hbm.at[:, pl.ds(0, 2 * BQ)], do_v, sems.at[1])]
        for cp in later:
            cp.start()
        _copy_in(qkv_hbm.at[:, pl.ds(0, 2 * PAIRW)], qkv_v, sem)
        st = comm[0]() if comm else None
        _transpose_slab(qkv_v, kt_v, lambda p: p * PAIRW + BQ)
        for cp in later:
            cp.wait()
        m0, m1 = _lane_masks()
        f0, f1 = m0.astype(F32), m1.astype(F32)
        r0, r1 = _row_masks()
        valid = _valid_t(True)
        later = _tri_blockdiag(True)
        earlier = _tri_blockdiag(False)
        dk_s[...] = jnp.zeros_like(dk_s)
        dv_s[...] = jnp.zeros_like(dv_s)

        def steps(qns, qts, dns, dts, rts, i, j, cs, diag):
            kv = [_kv_tiles(qkv_v, p, b, j) for p, b in CHAINS]
            ks = [_stack(kv[c][0], m0, m1) for c in range(NC)]
            vs = [_stack(kv[c][1], m0, m1) for c in range(NC)]
            zs = [_dot(ks[c], qts[c]) for c in range(NC)]
            das = [_dot(vs[c], dts[c]) for c in range(NC)]
            lbs, lrs, pls = [], [], []
            for c in range(NC):
                lb = _log_sigmoid_tile(zs[c])
                lr = lb - zs[c]
                if diag:
                    lr = jnp.where(valid, lr, 0.0)
                s0, s1 = _colsum2(lr)
                lbs.append(lb)
                lrs.append(lr)
                pls.append((cs[c][0][0] + s0, cs[c][0][1] + s1))
            tails = [_cum_mm(later, lrs[c]) for c in range(NC)]
            avs, gms = [], []
            for c in range(NC):
                a = jnp.exp(lbs[c] + tails[c] + _bcast_heads(rts[c][0] - pls[c][0], rts[c][1] - pls[c][1]))
                if diag:
                    a = jnp.where(valid, a, 0.0)
                avs.append(a)
                gms.append(das[c] * a)
            befores = [_cum_mm(earlier, gms[c]) for c in range(NC)]
            dzbs = []
            for c in range(NC):
                beta = jnp.exp(lbs[c])
                dz = gms[c] - beta * (gms[c] + befores[c] + _bcast_heads(*cs[c][1]))
                if diag:
                    dz = jnp.where(valid, dz, 0.0)
                dzbs.append(dz.astype(BF16))
            out = []
            for c, (p, b) in enumerate(CHAINS):
                dq = cs[c][2] + _dot(_stack_t(kt_v[p, :, _rows(b, j)], r0, r1), dzbs[c])
                dk = _dot(dzbs[c], qns[c])
                dv = _dot(avs[c].astype(BF16), dns[c])
                dk_s[p, _rows(b, j), :] += dk[:BQ] * f0 + dk[BQ:] * f1
                dv_s[p, _rows(b, j), :] += dv[:BQ] * f0 + dv[BQ:] * f1
                g0, g1 = _colsum2(gms[c])
                out.append((pls[c], (cs[c][1][0] + g0, cs[c][1][1] + g1), dq))
            return tuple(out)

        def qblock(i, _):
            qns = [_q_tile(qkv_v, p, b, i) for p, b in CHAINS]
            dns = [do_v[_rows(b, i), p * BQ:(p + 1) * BQ] for p, b in CHAINS]
            qts = [_tr(t) for t in qns]
            dts = [_tr(t) for t in dns]
            rts = [(_stat_row(r_ref, p, b, 0, i), _stat_row(r_ref, p, b, 1, i)) for p, b in CHAINS]
            zr = jnp.zeros((1, BQ), F32)
            cs = (((zr, zr), (zr, zr), jnp.zeros((BQ, BQ), F32)),) * NC
            cs = lax.fori_loop(0, i, lambda j, cs: steps(qns, qts, dns, dts, rts, i, j, cs, False), cs)
            cs = steps(qns, qts, dns, dts, rts, i, i, cs, True)
            for c, (p, b) in enumerate(CHAINS):
                dq_v[_rows(b, i), p * PAIRW:p * PAIRW + BQ] = (cs[c][2].T * SCALE).astype(BF16)
            return 0

        lax.fori_loop(0, NB, qblock, 0)
        for p in range(2):
            dq_v[:, p * PAIRW + BQ:p * PAIRW + 2 * BQ] = dk_s[p].astype(BF16)
            dq_v[:, p * PAIRW + 2 * BQ:p * PAIRW + 3 * BQ] = dv_s[p].astype(BF16)
        _copy_in(dq_v, dqkv_hbm.at[:, pl.ds(0, 2 * PAIRW)], sem)
        if comm:
            comm[1](st)

    (dqkv,), extra = _host_call(
        body, "sb_bwd", [qkv, dmixed, rtot], [ANY_SPEC, ANY_SPEC, VMEM_SPEC],
        [jax.ShapeDtypeStruct((T, QKVW), BF16)], [ANY_SPEC],
        [SLAB_QKV, SLAB_OUT, SLAB_QKV, ACC_KV, ACC_KV, pltpu.SemaphoreType.DMA((4,)), SLAB_T], {}, job)
    return dqkv, extra


def _flash_fwd(qkv, mixed, g, fox, bias, job=None):
    def body(ins, outs, scr, comm):
        if fox:
            qkv_hbm, cq_ref, ckb_hbm, _ = ins
            (o_hbm, lse_ref, o32_hbm), (qkv_v, o_v, sem, vt_v, o32_v, ckb_v) = outs, scr
        else:
            qkv_hbm, tbl_ref, _ = ins
            (o_hbm, lse_ref), (qkv_v, o_v, sem, vt_v) = outs, scr
        sems = sem
        sem = sems.at[0]
        later = [pltpu.make_async_copy(ckb_hbm, ckb_v, sems.at[1])] if fox else []
        for cp in later:
            cp.start()
        _copy_in(qkv_hbm.at[:, pl.ds(g * 2 * PAIRW, 2 * PAIRW)], qkv_v, sem)
        st = comm[0]() if comm else None
        _transpose_slab(qkv_v, vt_v, lambda p: p * PAIRW + 2 * BQ)
        for cp in later:
            cp.wait()
        m0, m1 = _lane_masks()
        r0, r1 = _row_masks()
        valid = _valid_t(False)

        def steps(qts, cqs, i, j, cs, diag):
            ks = [_stack(_kv_tiles(qkv_v, p, b, j)[0], m0, m1) for p, b in CHAINS]
            zs = [_dot(ks[c], qts[c]) for c in range(NC)]
            prs, alphas, out = [], [], []
            for c, (p, b) in enumerate(CHAINS):
                (ma, mb), (la, lb_), _ = cs[c]
                if fox:
                    kk = pl.ds(pl.multiple_of(j * BQ, BQ), BQ)
                    col = b * NH + 2 * p
                    z = zs[c] + (cqs[c] - jnp.concatenate([ckb_v[col, kk, :], ckb_v[col + 1, kk, :]], axis=0))
                    if diag:
                        z = jnp.where(valid, z, NEG)
                else:
                    z = zs[c] + tbl_ref[p, i - j]
                za, zb = _heads(z)
                na = jnp.maximum(ma, jnp.max(za, axis=0, keepdims=True))
                nb = jnp.maximum(mb, jnp.max(zb, axis=0, keepdims=True))
                aa, ab = jnp.exp(ma - na), jnp.exp(mb - nb)
                pr = jnp.exp(z - _bcast_heads(na, nb))
                sa, sb = _colsum2(pr)
                prs.append(_split2(pr) if fox else pr.astype(BF16))
                alphas.append((aa, ab))
                out.append(((na, nb), (aa * la + sa, ab * lb_ + sb)))
            pvs = []
            for c, (p, b) in enumerate(CHAINS):
                vts = _stack_t(vt_v[p, :, _rows(b, j)], r0, r1)
                if fox:
                    pvs.append(_dot(vts, prs[c][:, :BQ]) + _dot(vts, prs[c][:, BQ:]))
                else:
                    pvs.append(_dot(vts, prs[c]))
            return tuple((out[c][0], out[c][1], _by_channel(*alphas[c]) * cs[c][2] + pvs[c]) for c in range(NC))

        def qblock(i, _):
            qts = [_tr(_q_tile(qkv_v, p, b, i)) for p, b in CHAINS]
            if fox:
                cqs = [_bcast_heads(_stat_row(cq_ref, p, b, 0, i), _stat_row(cq_ref, p, b, 1, i)) for p, b in CHAINS]
            else:
                cqs = [None] * NC
            ng = jnp.full((1, BQ), NEG, F32)
            zr = jnp.zeros((1, BQ), F32)
            cs = steps(qts, cqs, i, i, (((ng, ng), (zr, zr), jnp.zeros((BQ, BQ), F32)),) * NC, True)
            cs = lax.fori_loop(1, i + 1, lambda jj, cs: steps(qts, cqs, i, i - jj, cs, False), cs)
            for c, (p, b) in enumerate(CHAINS):
                (ma, mb), (la, lb_), acc = cs[c]
                o = (acc / _by_channel(la, lb_)).T
                o_v[_rows(b, i), p * BQ:(p + 1) * BQ] = o.astype(BF16)
                if fox:
                    o32_v[_rows(b, i), p * BQ:(p + 1) * BQ] = o
                _put_row(lse_ref, p, b, 0, i, ma + jnp.log(la))
                _put_row(lse_ref, p, b, 1, i, mb + jnp.log(lb_))
            return 0

        lax.fori_loop(0, NB, qblock, 0)
        _copy_in(o_v, o_hbm.at[:, pl.ds(g * 2 * BQ, 2 * BQ)], sem)
        if fox:
            _copy_in(o32_v, o32_hbm, sem)
        if comm:
            comm[1](st)

    bias_specs = [VMEM_SPEC, ANY_SPEC] if fox else [VMEM_SPEC]
    n_in = 2 + len(bias_specs)
    o32 = [jax.ShapeDtypeStruct((T, 2 * BQ), F32)] if fox else []
    res, extra = _host_call(
        body, "fox_fwd" if fox else "dil_fwd", [qkv, *bias, mixed], [ANY_SPEC] + bias_specs + [ANY_SPEC],
        [jax.ShapeDtypeStruct((T, D), BF16), ROWS_SHAPE] + o32, [ANY_SPEC, VMEM_SPEC] + [ANY_SPEC] * len(o32),
        [SLAB_QKV, SLAB_OUT, pltpu.SemaphoreType.DMA((4,)), SLAB_T] + ([SLAB_O32, SLAB_KEYB] if fox else []),
        {n_in - 1: 0}, job)
    return (*res, extra)


def _flash_bwd(qkv, o, dmixed, lse, dqkv, g, fox, bias, job=None):
    def body(ins, outs, scr, comm):
        if fox:
            qkv_hbm, o_hbm, do_hbm, lse_ref, cq_ref, ckb_hbm, _ = ins
            (dqkv_hbm, db_ref), (qkv_v, o_v, do_v, dq_v, dk_s, dv_s, sem, kt_v, ckb_v, dc_s) = outs, scr
        else:
            qkv_hbm, o_hbm, do_hbm, lse_ref, tbl_ref, _ = ins
            (dqkv_hbm, db_ref), (qkv_v, o_v, do_v, dq_v, dk_s, dv_s, sem, kt_v) = outs, scr
        sems = sem
        sem = sems.at[0]
        later = [pltpu.make_async_copy(do_hbm.at[:, pl.ds(g * 2 * BQ, 2 * BQ)], do_v, sems.at[1])]
        if fox:
            later += [pltpu.make_async_copy(o_hbm, o_v, sems.at[2]), pltpu.make_async_copy(ckb_hbm, ckb_v, sems.at[3])]
        else:
            later += [pltpu.make_async_copy(o_hbm.at[:, pl.ds(g * 2 * BQ, 2 * BQ)], o_v, sems.at[2])]
        for cp in later:
            cp.start()
        _copy_in(qkv_hbm.at[:, pl.ds(g * 2 * PAIRW, 2 * PAIRW)], qkv_v, sem)
        st = comm[0]() if comm else None
        _transpose_slab(qkv_v, kt_v, lambda p: p * PAIRW + BQ)
        for cp in later:
            cp.wait()
        m0, m1 = _lane_masks()
        f0, f1 = m0.astype(F32), m1.astype(F32)
        r0, r1 = _row_masks()
        valid = _valid_t(False)
        dk_s[...] = jnp.zeros_like(dk_s)
        dv_s[...] = jnp.zeros_like(dv_s)
        if fox:
            dc_s[...] = jnp.zeros_like(dc_s)
        else:
            db_ref[...] = jnp.zeros_like(db_ref)

        def steps(qns, qts, dns, dts, cqs, lses, deltas, i, j, dqs, diag):
            kv = [_kv_tiles(qkv_v, p, b, j) for p, b in CHAINS]
            ks = [_stack(kv[c][0], m0, m1) for c in range(NC)]
            vs = [_stack(kv[c][1], m0, m1) for c in range(NC)]
            zs = [_dot(ks[c], qts[c]) for c in range(NC)]
            dps = [_dot(vs[c], dts[c]) for c in range(NC)]
            prs, dzl = [], []
            for c, (p, b) in enumerate(CHAINS):
                if fox:
                    kk = pl.ds(pl.multiple_of(j * BQ, BQ), BQ)
                    col = b * NH + 2 * p
                    z = zs[c] + (cqs[c] - jnp.concatenate([ckb_v[col, kk, :], ckb_v[col + 1, kk, :]], axis=0))
                    if diag:
                        z = jnp.where(valid, z, NEG)
                else:
                    z = zs[c] + tbl_ref[p, i - j]
                pr = jnp.exp(z - lses[c])
                prs.append(pr.astype(BF16))
                dzl.append(pr * (dps[c] - deltas[c]))
            dzbs = [dz.astype(BF16) for dz in dzl]
            new = []
            for c, (p, b) in enumerate(CHAINS):
                new.append(dqs[c] + _dot(_stack_t(kt_v[p, :, _rows(b, j)], r0, r1), dzbs[c]))
                dk = _dot(dzbs[c], qns[c])
                dv = _dot(prs[c], dns[c])
                dk_s[p, _rows(b, j), :] += dk[:BQ] * f0 + dk[BQ:] * f1
                dv_s[p, _rows(b, j), :] += dv[:BQ] * f0 + dv[BQ:] * f1
                if fox:
                    dc_s[c, pl.ds(pl.multiple_of(j * HB, HB), HB), :] += dzl[c]
            if not fox:
                for p in range(2):
                    db_ref[p, i - j] = db_ref[p, i - j] + (dzl[2 * p] + dzl[2 * p + 1])
            return tuple(new)

        def qblock(i, _):
            qns = [_q_tile(qkv_v, p, b, i) for p, b in CHAINS]
            dns = [do_v[_rows(b, i), p * BQ:(p + 1) * BQ] for p, b in CHAINS]
            qts = [_tr(t) for t in qns]
            dts = [_tr(t) for t in dns]
            lses = [_bcast_heads(_stat_row(lse_ref, p, b, 0, i), _stat_row(lse_ref, p, b, 1, i)) for p, b in CHAINS]
            if fox:
                cqs = [_bcast_heads(_stat_row(cq_ref, p, b, 0, i), _stat_row(cq_ref, p, b, 1, i)) for p, b in CHAINS]
            else:
                cqs = [None] * NC
            deltas = []
            for c, (p, b) in enumerate(CHAINS):
                pt = (dns[c].astype(F32) * o_v[_rows(b, i), p * BQ:(p + 1) * BQ].astype(F32)).T
                deltas.append(_bcast_heads(jnp.sum(pt[:64], axis=0, keepdims=True), jnp.sum(pt[64:], axis=0, keepdims=True)))
            dqs = (jnp.zeros((BQ, BQ), F32),) * NC
            dqs = lax.fori_loop(0, i, lambda j, d: steps(qns, qts, dns, dts, cqs, lses, deltas, i, j, d, False), dqs)
            dqs = steps(qns, qts, dns, dts, cqs, lses, deltas, i, i, dqs, True)
            for c, (p, b) in enumerate(CHAINS):
                dq_v[_rows(b, i), p * PAIRW:p * PAIRW + BQ] = (dqs[c].T * SCALE).astype(BF16)
            return 0

        lax.fori_loop(0, NB, qblock, 0)
        for p in range(2):
            dq_v[:, p * PAIRW + BQ:p * PAIRW + 2 * BQ] = dk_s[p].astype(BF16)
            dq_v[:, p * PAIRW + 2 * BQ:p * PAIRW + 3 * BQ] = dv_s[p].astype(BF16)
        _copy_in(dq_v, dqkv_hbm.at[:, pl.ds(g * 2 * PAIRW, 2 * PAIRW)], sem)
        if fox:
            lane = _iota((BQ, NSTAT), 1)

            def fold(n, _):
                t = jnp.zeros((BQ, NSTAT), F32)
                for c, (p, b) in enumerate(CHAINS):
                    s = jnp.sum(dc_s[c, pl.ds(pl.multiple_of(n * HB, HB), HB), :], axis=1, keepdims=True)
                    col = b * NH + 2 * p
                    t = t - jnp.where(lane == col, s[:BQ], 0.0) - jnp.where(lane == col + 1, s[BQ:], 0.0)
                db_ref[pl.ds(pl.multiple_of(n * BQ, BQ), BQ), :] = t
                return 0

            lax.fori_loop(0, NB, fold, 0)
        if comm:
            comm[1](st)

    if fox:
        bias_specs = [VMEM_SPEC, ANY_SPEC]
        db_shape = jax.ShapeDtypeStruct((S, NSTAT), F32)
        more = [SLAB_KEYB, pltpu.VMEM((NC, NB * HB, BQ), F32)]
    else:
        bias_specs = [VMEM_SPEC]
        db_shape = jax.ShapeDtypeStruct((2, NB, HB, BQ), F32)
        more = []
    n_in = 5 + len(bias_specs)
    (dqkv, db), extra = _host_call(
        body, "fox_bwd" if fox else "dil_bwd", [qkv, o, dmixed, lse, *bias, dqkv],
        [ANY_SPEC, ANY_SPEC, ANY_SPEC, VMEM_SPEC] + bias_specs + [ANY_SPEC],
        [jax.ShapeDtypeStruct((T, QKVW), BF16), db_shape], [ANY_SPEC, VMEM_SPEC],
        [SLAB_QKV, SLAB_O32 if fox else SLAB_OUT, SLAB_OUT, SLAB_QKV, ACC_KV, ACC_KV, pltpu.SemaphoreType.DMA((4,)), SLAB_T]
        + more, {n_in - 1: 0}, job)
    return dqkv, db, extra


def _delta_t(d):
    return d * BQ + _iota((HB, BQ), 1) - (_iota((HB, BQ), 0) & (BQ - 1))


def _buckets_in(d):
    lo, hi = max(d * BQ - (BQ - 1), 0), d * BQ + BQ - 1
    return [b for b in range(32) if BUCKET_TH[b] <= hi and (b == 31 or BUCKET_TH[b + 1] > lo)]


def _in_bucket(delta, b):
    m = delta >= BUCKET_TH[b]
    return m if b == 31 else m & (delta < BUCKET_TH[b + 1])


def _dil_table(rel_bias, job=None):
    def body(ins, outs, scr, comm):
        (rb_ref,), (o_ref,) = ins, outs
        st = comm[0]() if comm else None
        for d in range(NB):
            delta = _delta_t(d)
            pos = delta >= 0
            n = ((pos & (delta <= 128)).astype(jnp.int32)
                 + (pos & (delta <= 512) & ((delta & 3) == 0)).astype(jnp.int32)
                 + (pos & ((delta & 15) == 0)).astype(jnp.int32))
            logn = jnp.where(n == 3, math.log(3.0), jnp.where(n == 2, math.log(2.0), jnp.where(n == 1, 0.0, NEG)))
            head1 = _iota((HB, BQ), 0) >= BQ
            for p in range(2):
                val = jnp.zeros((HB, BQ), F32)
                for b in _buckets_in(d):
                    val = jnp.where(_in_bucket(delta, b), jnp.where(head1, rb_ref[b, 2 * p + 1], rb_ref[b, 2 * p]), val)
                o_ref[p, d] = val + logn
        if comm:
            comm[1](st)

    (tbl,), extra = _host_call(
        body, "dil_table", [rel_bias], [pl.BlockSpec(memory_space=pltpu.SMEM)],
        [jax.ShapeDtypeStruct((2, NB, HB, BQ), F32)], [VMEM_SPEC], [], {}, job)
    return (tbl, extra) if job else tbl


def _dil_table_bwd(dtbl):
    def body(dt_ref, o_ref):
        p = pl.program_id(0)
        rowi = _iota((32, BQ), 0)
        lanei = _iota((32, BQ), 1)

        @pl.when(p == 0)
        def _():
            o_ref[...] = jnp.zeros_like(o_ref)

        out = jnp.zeros((32, BQ), F32)
        for b in range(32):
            acc = None
            for d in range(NB):
                if b in _buckets_in(d):
                    t = jnp.where(_in_bucket(_delta_t(d), b), dt_ref[d], 0.0)
                    acc = t if acc is None else acc + t
            rs = jnp.sum(acc, axis=1, keepdims=True)
            s0 = jnp.sum(rs[:BQ], axis=0, keepdims=True)
            s1 = jnp.sum(rs[BQ:], axis=0, keepdims=True)
            out = (out + jnp.where((rowi == b) & (lanei == 2 * p), s0, 0.0)
                   + jnp.where((rowi == b) & (lanei == 2 * p + 1), s1, 0.0))
        o_ref[...] += out

    return pl.pallas_call(
        body, name="dil_table_bwd", grid=(2,),
        in_specs=[pl.BlockSpec((None, NB, HB, BQ), lambda p: (p, 0, 0, 0))],
        out_specs=pl.BlockSpec((32, BQ), lambda p: (0, 0)),
        out_shape=jax.ShapeDtypeStruct((32, BQ), F32),
        compiler_params=_cp(("arbitrary",)))(dtbl)


def _fox_prep(gate, fb):
    def body(g_ref, fb_ref, c_ref):
        tri = (_iota((BQ, BQ), 0) >= _iota((BQ, BQ), 1)).astype(BF16)

        def blk(i, carry):
            r0 = pl.multiple_of(i * BQ, BQ)
            lf = _log_sigmoid(g_ref[pl.ds(r0, BQ), :] + fb_ref[...])
            c = _dot(tri, _split3(lf))
            c_ref[pl.ds(r0, BQ), :] = c[:, 0:BQ] + c[:, BQ:2 * BQ] + c[:, 2 * BQ:3 * BQ] + carry
            return carry + jnp.sum(lf, axis=0, keepdims=True)

        lax.fori_loop(0, NB, blk, jnp.zeros((1, BQ), F32))

    blk = pl.BlockSpec((S, GATEW), lambda b: (b, 0))
    return pl.pallas_call(
        body, name="fox_prep", grid=(BL,), in_specs=[blk, pl.BlockSpec((1, GATEW), lambda b: (0, 0))],
        out_specs=blk, out_shape=jax.ShapeDtypeStruct((T, GATEW), F32),
        compiler_params=_cp(("parallel",)))(gate, fb)


def _fox_post(dcum, gate, fb):
    def body(dc_ref, g_ref, fb_ref, dg_ref, dfb_ref):
        b = pl.program_id(0)
        tri = (_iota((BQ, BQ), 0) <= _iota((BQ, BQ), 1)).astype(BF16)

        def blk(ii, carry):
            csum, dfb = carry
            r0 = pl.multiple_of((NB - 1 - ii) * BQ, BQ)
            dc = dc_ref[pl.ds(r0, BQ), :]
            c = _dot(tri, _split3(dc))
            dlf = c[:, 0:BQ] + c[:, BQ:2 * BQ] + c[:, 2 * BQ:3 * BQ] + csum
            dg = dlf * jnp.exp(_log_sigmoid(-(g_ref[pl.ds(r0, BQ), :] + fb_ref[...])))
            dg_ref[pl.ds(r0, BQ), :] = dg
            return csum + jnp.sum(dc, axis=0, keepdims=True), dfb + jnp.sum(dg, axis=0, keepdims=True)

        z = jnp.zeros((1, BQ), F32)
        _, dfb = lax.fori_loop(0, NB, blk, (z, z))

        @pl.when(b == 0)
        def _():
            dfb_ref[...] = dfb

        @pl.when(b > 0)
        def _():
            dfb_ref[...] += dfb

    blk = pl.BlockSpec((S, GATEW), lambda b: (b, 0))
    vec = pl.BlockSpec((1, GATEW), lambda b: (0, 0))
    return pl.pallas_call(
        body, name="fox_post", grid=(BL,), in_specs=[blk, blk, vec], out_specs=[blk, vec],
        out_shape=[jax.ShapeDtypeStruct((T, GATEW), F32), jax.ShapeDtypeStruct((1, GATEW), F32)],
        compiler_params=_cp(("arbitrary",)))(dcum, gate, fb)


def _shift_down(x, n):
    return jnp.where(_iota(x.shape, 0) >= n, pltpu.roll(x, n, 0), 0.0)


def _shift_up(x, n):
    return jnp.where(_iota(x.shape, 0) < S - n, pltpu.roll(x, S - n, 0), 0.0)


def _conv_fwd(conv, cw, mixed):
    W = 256

    def body(c_ref, w_ref, _, o_ref):
        u = c_ref[:, W:2 * W] * c_ref[:, 2 * W:3 * W]
        y = w_ref[0:1, :] * _shift_down(u, 2) + w_ref[1:2, :] * _shift_down(u, 1) + w_ref[2:3, :] * u
        o_ref[...] = (c_ref[:, 0:W] * y).astype(BF16)

    return pl.pallas_call(
        body, name="conv_fwd", grid=(BL,),
        in_specs=[pl.BlockSpec((S, CONVW), lambda b: (b, 0)), pl.BlockSpec((8, W), lambda b: (0, 0)), ANY_SPEC],
        out_specs=pl.BlockSpec((S, W), lambda b: (b, 3)),
        out_shape=jax.ShapeDtypeStruct((T, D), BF16), input_output_aliases={2: 0},
        compiler_params=_cp(("parallel",)))(conv, cw, mixed)


def _conv_bwd(conv, cw, dmixed):
    W = 256

    def body(c_ref, w_ref, do_ref, dc_ref, dw_ref):
        b = pl.program_id(0)
        bg = c_ref[:, 0:W]
        cg = c_ref[:, W:2 * W]
        hv = c_ref[:, 2 * W:3 * W]
        do = do_ref[...].astype(F32)
        u = cg * hv
        u1 = _shift_down(u, 1)
        u2 = _shift_down(u, 2)
        y = w_ref[0:1, :] * u2 + w_ref[1:2, :] * u1 + w_ref[2:3, :] * u
        dy = do * bg
        du = w_ref[2:3, :] * dy + w_ref[1:2, :] * _shift_up(dy, 1) + w_ref[0:1, :] * _shift_up(dy, 2)
        dc_ref[:, 0:W] = (do * y).astype(BF16)
        dc_ref[:, W:2 * W] = (du * hv).astype(BF16)
        dc_ref[:, 2 * W:3 * W] = (du * cg).astype(BF16)
        rowi = _iota((8, W), 0)
        dw = (jnp.where(rowi == 0, jnp.sum(dy * u2, axis=0, keepdims=True), 0.0)
              + jnp.where(rowi == 1, jnp.sum(dy * u1, axis=0, keepdims=True), 0.0)
              + jnp.where(rowi == 2, jnp.sum(dy * u, axis=0, keepdims=True), 0.0))

        @pl.when(b == 0)
        def _():
            dw_ref[...] = dw

        @pl.when(b > 0)
        def _():
            dw_ref[...] += dw

    return pl.pallas_call(
        body, name="conv_bwd", grid=(BL,),
        in_specs=[pl.BlockSpec((S, CONVW), lambda b: (b, 0)), pl.BlockSpec((8, W), lambda b: (0, 0)),
                  pl.BlockSpec((S, W), lambda b: (b, 3))],
        out_specs=[pl.BlockSpec((S, CONVW), lambda b: (b, 0)), pl.BlockSpec((8, W), lambda b: (0, 0))],
        out_shape=[jax.ShapeDtypeStruct((T, CONVW), BF16), jax.ShapeDtypeStruct((8, W), F32)],
        compiler_params=_cp(("arbitrary",)))(conv, cw, dmixed)


def _place():
    x, y, c = lax.axis_index("x"), lax.axis_index("y"), lax.axis_index("c")
    return x, y, c


def _chips_of(x, y):
    return [(1 - x, y), (x, 1 - y), (1 - x, 1 - y)]


def _dev(p):
    return 4 * p[0] + 2 * p[1] + p[2]


def _gather_job_a(shards):
    n = len(shards)

    def peers(x, y, c):
        return [(x, y, 1 - c)] + [(*chip, c) for chip in _chips_of(x, y)]

    def start(ins, outs, sems):
        send, recv, loc = sems
        x, y, c = _place()
        me = (x, y, c)
        cps = []
        for a in range(n):
            cps.append(pltpu.make_async_copy(ins[a], outs[a].at[_dev(me)], loc.at[a]))
            for k, peer in enumerate(peers(x, y, c)):
                cps.append(pltpu.make_async_remote_copy(
                    src_ref=ins[a], dst_ref=outs[a].at[_dev(me)], send_sem=send.at[a, k], recv_sem=recv.at[a, k],
                    device_id=peer, device_id_type=MESH))
        for cp in cps:
            cp.start()
        return cps

    def finish(cps, ins, outs, sems):
        send, recv, loc = sems
        x, y, c = _place()
        for a in range(n):
            for k, peer in enumerate(peers(x, y, c)):
                pltpu.make_async_remote_copy(
                    src_ref=ins[a], dst_ref=outs[a].at[_dev(peer)], send_sem=send.at[a, k], recv_sem=recv.at[a, k],
                    device_id=(x, y, c), device_id_type=MESH).wait_recv()
        for a in range(n):
            cps[5 * a].wait()
            for k in range(4):
                cps[5 * a + 1 + k].wait_send()

    return _Job(shards, [jax.ShapeDtypeStruct((NDEV,) + s.shape, s.dtype) for s in shards], {},
                [pltpu.SemaphoreType.DMA((n, 4)), pltpu.SemaphoreType.DMA((n, 4)), pltpu.SemaphoreType.DMA((n,))],
                start, finish)


def _gather_job_b(gathered):
    n = len(gathered)

    def start(ins, outs, sems):
        send, recv = sems
        x, y, c = _place()
        cps = []
        for a in range(n):
            for j, chip in enumerate(_chips_of(x, y)):
                blk = outs[a].at[_dev((*chip, c))]
                cps.append(pltpu.make_async_remote_copy(
                    src_ref=blk, dst_ref=blk, send_sem=send.at[a, j], recv_sem=recv.at[a, j],
                    device_id=(x, y, 1 - c), device_id_type=MESH))
        for cp in cps:
            cp.start()
        return cps

    def finish(cps, ins, outs, sems):
        send, recv = sems
        x, y, c = _place()
        for a in range(n):
            for j, chip in enumerate(_chips_of(x, y)):
                blk = outs[a].at[_dev((*chip, 1 - c))]
                pltpu.make_async_remote_copy(
                    src_ref=blk, dst_ref=blk, send_sem=send.at[a, j], recv_sem=recv.at[a, j],
                    device_id=(x, y, c), device_id_type=MESH).wait_recv()
        for cp in cps:
            cp.wait_send()

    return _Job(gathered, [jax.ShapeDtypeStruct(g.shape, g.dtype) for g in gathered], {a: a for a in range(n)},
                [pltpu.SemaphoreType.DMA((n, 3)), pltpu.SemaphoreType.DMA((n, 3))], start, finish)


def _sibling_job(grads):
    n = len(grads)

    def start(ins, outs, sems):
        send, recv = sems
        x, y, c = _place()
        cps = [pltpu.make_async_remote_copy(
            src_ref=ins[a].at[:, 1 - c], dst_ref=outs[a], send_sem=send.at[a], recv_sem=recv.at[a],
            device_id=(x, y, 1 - c), device_id_type=MESH) for a in range(n)]
        for cp in cps:
            cp.start()
        return cps

    def finish(cps, ins, outs, sems):
        for cp in cps:
            cp.wait()

    return _Job(grads, [jax.ShapeDtypeStruct(g.shape[:1] + g.shape[2:], F32) for g in grads], {},
                [pltpu.SemaphoreType.DMA((n,)), pltpu.SemaphoreType.DMA((n,))], start, finish)


def _chip_job(psums):
    n = len(psums)

    def copies(ins, outs, sems):
        send, recv, loc = sems
        x, y, c = _place()
        mychip = 2 * x + y
        cps = []
        for a in range(n):
            cps.append(pltpu.make_async_copy(ins[a].at[mychip], outs[a].at[mychip], loc.at[a]))
            for j, chip in enumerate(_chips_of(x, y)):
                cps.append(pltpu.make_async_remote_copy(
                    src_ref=ins[a].at[2 * chip[0] + chip[1]], dst_ref=outs[a].at[mychip],
                    send_sem=send.at[a, j], recv_sem=recv.at[a, j], device_id=(*chip, c), device_id_type=MESH))
        return cps

    def start(ins, outs, sems):
        for cp in copies(ins, outs, sems):
            cp.start()

    def finish(_, ins, outs, sems):
        cps = copies(ins, outs, sems)
        send, recv, loc = sems
        x, y, c = _place()
        mychip = 2 * x + y
        for a in range(n):
            for j, chip in enumerate(_chips_of(x, y)):
                pltpu.make_async_remote_copy(
                    src_ref=ins[a].at[mychip], dst_ref=outs[a].at[2 * chip[0] + chip[1]],
                    send_sem=send.at[a, j], recv_sem=recv.at[a, j], device_id=(x, y, c), device_id_type=MESH).wait_recv()
        for a in range(n):
            cps[4 * a].wait()
            for j in range(3):
                cps[4 * a + 1 + j].wait_send()

    return _Job(psums, [jax.ShapeDtypeStruct(p.shape, BF16) for p in psums], {},
                [pltpu.SemaphoreType.DMA((n, 3)), pltpu.SemaphoreType.DMA((n, 3)), pltpu.SemaphoreType.DMA((n,))],
                start, finish)


def _join_jobs(*jobs):
    jobs = [j for j in jobs if j is not None]
    if len(jobs) <= 1:
        return jobs[0] if jobs else None
    cut = lambda seq, sizes: [seq[sum(sizes[:k]):sum(sizes[:k + 1])] for k in range(len(sizes))]
    n_in = [len(j.ins) for j in jobs]
    n_out = [len(j.out_shapes) for j in jobs]
    n_sem = [len(j.sems) for j in jobs]
    aliases = {}
    for k, j in enumerate(jobs):
        for a, b in j.aliases.items():
            aliases[sum(n_in[:k]) + a] = sum(n_out[:k]) + b

    def start(ins, outs, sems):
        return [j.start(i, o, s) for j, i, o, s in zip(jobs, cut(ins, n_in), cut(outs, n_out), cut(sems, n_sem))]

    def finish(sts, ins, outs, sems):
        for j, st, i, o, s in zip(jobs, sts, cut(ins, n_in), cut(outs, n_out), cut(sems, n_sem)):
            j.finish(st, i, o, s)

    return _Job([t for j in jobs for t in j.ins], [t for j in jobs for t in j.out_shapes], aliases,
                [t for j in jobs for t in j.sems], start, finish)


def _run_job(job, name):
    def body(ins, outs, scr, comm):
        comm[1](comm[0]())

    return _host_call(body, name, [], [], [], [], [], {}, job)[1]


def _allreduce_small(v, job=None):
    def body(ins, outs, scr, comm):
        (v_ref,), (o_ref,), (slots, send_sems, recv_sems) = ins, outs, scr
        st = comm[0]() if comm else None
        x, y, c = _place()
        me = 4 * x + 2 * y + c
        slots[me] = v_ref[...]

        def copy(k):
            peer = (x ^ ((k >> 2) & 1), y ^ ((k >> 1) & 1), c ^ (k & 1))
            return pltpu.make_async_remote_copy(
                src_ref=v_ref, dst_ref=slots.at[me], send_sem=send_sems.at[k - 1], recv_sem=recv_sems.at[k - 1],
                device_id=peer, device_id_type=MESH)

        def arrival(k):
            return pltpu.make_async_remote_copy(
                src_ref=v_ref, dst_ref=slots.at[me ^ k], send_sem=send_sems.at[k - 1], recv_sem=recv_sems.at[k - 1],
                device_id=(x, y, c), device_id_type=MESH)

        sends = [copy(k) for k in range(1, NDEV)]
        for cp in sends:
            cp.start()
        for k in range(1, NDEV):
            arrival(k).wait_recv()
        for cp in sends:
            cp.wait_send()
        acc = slots[0]
        for d in range(1, NDEV):
            acc = acc + slots[d]
        o_ref[...] = acc
        if comm:
            comm[1](st)

    (out,), extra = _host_call(
        body, "allreduce_small", [v], [VMEM_SPEC], [jax.ShapeDtypeStruct(v.shape, F32)], [VMEM_SPEC],
        [pltpu.VMEM((NDEV,) + v.shape, F32), pltpu.SemaphoreType.DMA((NDEV - 1,)),
         pltpu.SemaphoreType.DMA((NDEV - 1,))], {}, job)
    return (out, extra) if job else out


def _pair_sums(views, gots, core):
    n = len(views)

    def body(c_ref, *refs):
        for a in range(n):
            refs[2 * n + a][...] = (refs[a][...] + refs[n + a][...]).astype(BF16)

    def vspec(v):
        return pl.BlockSpec((None, None, v.shape[2] // 2, v.shape[3]), lambda k, h, c: (k, c[0], h, 0))

    def gspec(g):
        return pl.BlockSpec((None, g.shape[1] // 2, g.shape[2]), lambda k, h, c: (k, h, 0))

    return pl.pallas_call(
        body, name="pair_sums",
        grid_spec=pltpu.PrefetchScalarGridSpec(
            num_scalar_prefetch=1, grid=(4, 2),
            in_specs=[vspec(v) for v in views] + [gspec(g) for g in gots],
            out_specs=[gspec(g) for g in gots]),
        out_shape=[jax.ShapeDtypeStruct(g.shape, BF16) for g in gots],
        compiler_params=_cp(("parallel", "parallel")))(core, *views, *gots)


def _chip_sums(parts):
    n = len(parts)

    def body(*refs):
        for a in range(n):
            acc = refs[a][0].astype(F32)
            for k in range(1, 4):
                acc = acc + refs[a][k].astype(F32)
            refs[n + a][...] = acc

    return pl.pallas_call(
        body, name="chip_sums", in_specs=[VMEM_SPEC] * n, out_specs=[VMEM_SPEC] * n,
        out_shape=[jax.ShapeDtypeStruct(p.shape[1:], F32) for p in parts], compiler_params=_cp())(*parts)


def _permute_in(w):
    lead = w.shape[:-1]
    return w.reshape(lead + (3, 3, 2, BQ)).swapaxes(-2, -3).reshape(lead + (QKVW,))


def _unpermute_in(w):
    lead = w.shape[:-1]
    return w.reshape(lead + (3, 2, 3, BQ)).swapaxes(-2, -3).reshape(lead + (QKVW,))


def _row(v):
    v = v.reshape(-1)
    return jnp.pad(v, (0, D - v.shape[0])).reshape(1, D)


def kernel(x, w_in, f_bias, conv_w, w_out, rel_bias, ln1_g, ln1_b, w_gate, w_up, w_down, ln2_g, ln2_b, loss_target, m_w_in, m_f_bias, m_conv_w, m_w_out, m_rel_bias, m_ln1_g, m_ln1_b, m_w_gate, m_w_up, m_w_down, m_ln2_g, m_ln2_b, v_w_in, v_f_bias, v_conv_w, v_w_out, v_rel_bias, v_ln1_g, v_ln1_b, v_w_gate, v_w_up, v_w_down, v_ln2_g, v_ln2_b):
    xi, yi, ci = _place()
    me = 4 * xi + 2 * yi + ci
    core = jnp.reshape(ci, (1,)).astype(jnp.int32)

    win_s = jnp.concatenate([_permute_in(w_in[..., :QKVW]), w_in[..., QKVW:]], axis=-1)
    win_s = jnp.pad(win_s, ((0, 0), (0, 0), (0, NPAD - NPROJ))).astype(BF16)
    per_layer = [win_s, w_out.astype(BF16), jnp.swapaxes(w_gate, 1, 2).astype(BF16),
                 jnp.swapaxes(w_up, 1, 2).astype(BF16), w_down.astype(BF16)]
    sh = [[s[l] for s in per_layer] for l in range(2)]

    def whole(g):
        return g.reshape(NDEV * g.shape[1], g.shape[2])

    cw_rows = lax.dynamic_update_slice(jnp.zeros((2, 3, 256), F32), conv_w, (0, 0, me * 32))
    small = jnp.concatenate([_row(cw_rows[0]), _row(cw_rows[1]), jnp.zeros((SMALL_ROWS - 2, D), F32)], axis=0)
    small, leg_a = _allreduce_small(small, job=_gather_job_a(sh[0][:1]))
    cw_full = small[0:2, :CONVW].reshape(2, 3, 256)
    cw8 = jnp.pad(cw_full, ((0, 0), (0, 5), (0, 0)))
    fb = jnp.pad(f_bias, ((0, 0), (0, GATEW - NH))).reshape(2, 1, GATEW)
    tbl, leg_b = _dil_table(rel_bias, job=_gather_job_b(list(leg_a)))
    W = [{"win": whole(leg_b[0])}, {}]

    def wrow(tn, K, blk=0):
        return pl.BlockSpec((tn, K), lambda i, j: (j, blk))

    def arow(tm, K, blk=0):
        return pl.BlockSpec((tm, K), lambda i, j: (i, blk))

    h = x.reshape(T, D)
    hb = h.astype(BF16)
    saved = []
    for l in range(2):
        Win = W[l]["win"]
        qkv, conv, gate = _proj(hb, Win)
        cum = _fox_prep(gate, fb[l])
        cq = cum[:, :NH].reshape(BL, S, NH).transpose(0, 2, 1).reshape(NSTAT, S)
        ckb = jnp.broadcast_to(cq[:, :, None], (NSTAT, S, BQ))
        if l == 0:
            mixed, rtot, a0 = _sb_fwd(qkv, job=_gather_job_a(sh[0][1:]))
            mixed, lse_d, ex = _flash_fwd(qkv, mixed, 1, False, (tbl,),
                                          job=_join_jobs(_gather_job_b(list(a0)), _gather_job_a(sh[1][:2])))
            W[0].update(zip(("wout", "wgT", "wuT", "wd"), [whole(t) for t in ex[:4]]))
            mixed, lse_f, o_fox, ex = _flash_fwd(qkv, mixed, 2, True, (cq, ckb),
                                                 job=_join_jobs(_gather_job_b(list(ex[4:])), _gather_job_a(sh[1][2:])))
            W[1].update(zip(("win", "wout"), [whole(t) for t in ex[:2]]))
            a2 = list(ex[2:])
        else:
            mixed, rtot, ex = _sb_fwd(qkv, job=_gather_job_b(a2))
            W[1].update(zip(("wgT", "wuT", "wd"), [whole(t) for t in ex]))
            mixed, lse_d, _ = _flash_fwd(qkv, mixed, 1, False, (tbl,))
            mixed, lse_f, o_fox, _ = _flash_fwd(qkv, mixed, 2, True, (cq, ckb))
        Wout, WgT, WuT, Wd = W[l]["wout"], W[l]["wgT"], W[l]["wuT"], W[l]["wd"]
        mixed = _conv_fwd(conv, cw8[l], mixed)
        x1, xh1, r1, x1b = _mm_ln(mixed, Wout, h, ln1_g[l:l + 1], ln1_b[l:l + 1], "out_proj_ln")
        fs, ft, a, x2, xh2, r2, x2b = _ffn_fwd(x1b, x1, WgT, WuT, Wd, ln2_g[l:l + 1], ln2_b[l:l + 1])
        saved.append(dict(h=hb, qkv=qkv, conv=conv, gate=gate, cq=cq, ckb=ckb, mixed=mixed, rtot=rtot, lse_d=lse_d,
                          lse_f=lse_f, o_fox=o_fox, x1=x1b, xh1=xh1, r1=r1, fs=fs, ft=ft, a=a, xh2=xh2, r2=r2))
        h, hb = x2, x2b

    dy = h

    def view(gr):
        return gr.reshape(4, 2, gr.shape[0] // NDEV, gr.shape[1])

    G = [None, None]
    small_g = {}
    shard_g = {}
    for l in (1, 0):
        sv = saved[l]
        Win, Wout, WgT, WuT, Wd = W[l]["win"], W[l]["wout"], W[l]["wgT"], W[l]["wuT"], W[l]["wd"]
        res = _ffn_bwd(dy, sv["xh2"], sv["r2"], ln2_g[l:l + 1], sv["fs"], sv["ft"], Wd, WgT, WuT,
                       target=loss_target.reshape(T, D) if l == 1 else None)
        dgt, dut, ds2b, dx1, dg2, db2 = res[:6]
        if l == 1:
            sq = res[6]
        G_d = _mm_tn(sv["a"], ds2b, None, C=D, Ka=DFF, N=D, tm=256, tn=1024, tk=T, ooff=0, name="grad_w_down")
        G_g = _mm_tn(dgt, sv["x1"], None, C=D, Ka=DFF, N=D, tm=256, tn=1024, tk=T, ooff=0, name="grad_w_gate")
        G_u = _mm_tn(dut, sv["x1"], None, C=D, Ka=DFF, N=D, tm=256, tn=1024, tk=T, ooff=0, name="grad_w_up")
        ds1, dg1, db1, ds1b, dmixed = _ln_bwd(dx1, sv["xh1"], sv["r1"], ln1_g[l:l + 1], Wout)
        G_out = _mm_tn(sv["mixed"], ds1b, None, C=D, Ka=D, N=D, tm=256, tn=1024, tk=T, ooff=0, name="grad_w_out")
        early = [view(t) for t in (G_g, G_u, G_d, G_out)] + ([view(G[1]["in"])] if l == 0 else [])
        dqkv, gots = _sb_bwd(sv["qkv"], dmixed, sv["rtot"], job=_sibling_job(early))
        ps = _pair_sums(early, list(gots), core)
        dqkv, dtbl, pa = _flash_bwd(sv["qkv"], sv["mixed"], dmixed, sv["lse_d"], dqkv, 1, False, (tbl,),
                                    job=_chip_job(ps[:2]))
        dqkv, dck, pb = _flash_bwd(sv["qkv"], sv["o_fox"], dmixed, sv["lse_f"], dqkv, 2, True,
                                   (sv["cq"], sv["ckb"]), job=_chip_job(ps[2:]))
        sums = _chip_sums(list(pa) + list(pb))
        shard_g[l] = dict(zip(("g", "u", "d", "out"), sums[:4]))
        if l == 0:
            shard_g[1]["in"] = sums[4]
        dconv, dcw = _conv_bwd(sv["conv"], cw8[l], dmixed)
        dcum = jnp.pad(dck.reshape(S, BL, NH).transpose(1, 0, 2).reshape(T, NH), ((0, 0), (0, GATEW - NH)))
        dgate, dfb = _fox_post(dcum, sv["gate"], fb[l])
        drb = _dil_table_bwd(dtbl)
        G_in = _mm_tn(sv["h"], dqkv, None, C=NPAD, Ka=D, N=QKVW, tm=512, tn=768, tk=T, ooff=0, name="grad_w_in_qkv")
        G_in = _mm_tn(sv["h"], dconv, G_in, C=NPAD, Ka=D, N=CONVW, tm=256, tn=768, tk=T, ooff=3,
                      name="grad_w_in_conv")
        G_in = _mm_tn(sv["h"], dgate, G_in, C=NPAD, Ka=D, N=GATEW, tm=1024, tn=128, tk=1024, ooff=24,
                      name="grad_w_in_gate")
        G[l] = {"in": G_in, "out": G_out, "g": G_g, "u": G_u, "d": G_d}
        if l == 0:
            late = [view(G_in)]
            tail = _chip_job(_pair_sums(late, list(_run_job(_sibling_job(late), "sibling_exchange")), core))
            dy, parts = _mm([(dqkv, arow(1024, QKVW), Win, wrow(512, QKVW, 0)),
                             (dconv, arow(1024, CONVW), Win, wrow(512, CONVW, 3)),
                             (dgate, arow(1024, GATEW), Win, wrow(512, GATEW, 24))],
                            nt=True, M=T, N=D, tm=1024, tn=512, out_dtype=F32, name="proj_dx", res=ds1,
                            res_scale=ALPHA, job=tail)
            shard_g[0]["in"] = _chip_sums(list(parts))[0]
        else:
            dy = _proj_bwd(dqkv, dconv, dgate, Win, ds1)
        small_g[l] = dict(ln1_g=dg1, ln1_b=db1, ln2_g=dg2, ln2_b=db2, cw=dcw[0:3].reshape(1, CONVW),
                          fb=dfb[:, :NH], rb=drb[:, :NH])
    grad_x = dy.reshape(BL, S, D)

    rows = []
    for name in ("ln1_g", "ln1_b", "ln2_g", "ln2_b"):
        rows += [small_g[0][name], small_g[1][name]]
    rows += [_row(small_g[0]["cw"]), _row(small_g[1]["cw"]),
             _row(jnp.concatenate([small_g[0]["fb"], small_g[1]["fb"]], axis=0)),
             _row(small_g[0]["rb"] + small_g[1]["rb"]), _row(sq)]
    rows.append(jnp.zeros((SMALL_ROWS - len(rows), D), F32))
    sg = _allreduce_small(jnp.concatenate(rows, axis=0))
    loss = sg[12, 0] * (0.5 / D)
    g_ln1_g, g_ln1_b, g_ln2_g, g_ln2_b = sg[0:2], sg[2:4], sg[4:6], sg[6:8]
    g_conv_full = sg[8:10, :CONVW].reshape(2, 3, 256)
    g_conv = lax.dynamic_slice(g_conv_full, (0, 0, me * 32), (2, 3, 32))
    g_fb = sg[10, :2 * NH].reshape(2, NH)
    g_rb = sg[11, :32 * NH].reshape(32, NH)

    def both(name):
        return jnp.stack([shard_g[0][name], shard_g[1][name]])

    g_in = both("in")
    g_w_in = jnp.concatenate([_unpermute_in(g_in[..., :QKVW]), g_in[..., QKVW:NPROJ]], axis=-1)
    g_w_out = both("out")
    g_w_gate = jnp.swapaxes(both("g"), 1, 2)
    g_w_up = jnp.swapaxes(both("u"), 1, 2)
    g_w_down = both("d")

    up_in = _adamw(w_in, g_w_in, m_w_in, v_w_in, 64)
    up_out = _adamw(w_out, g_w_out, m_w_out, v_w_out, 128)
    up_gate = _adamw(w_gate, g_w_gate, m_w_gate, v_w_gate, 256)
    up_up = _adamw(w_up, g_w_up, m_w_up, v_w_up, 256)
    up_down = _adamw(w_down, g_w_down, m_w_down, v_w_down, 352)

    def pack(fbv, cwv, rbv, l1g, l1b, l2g, l2b):
        r = [l1g, l1b, l2g, l2b, _row(cwv), _row(fbv), _row(rbv)]
        r.append(jnp.zeros((SMALL_ROWS - 11, D), F32))
        return jnp.concatenate(r, axis=0)

    pw = pack(f_bias, conv_w, rel_bias, ln1_g, ln1_b, ln2_g, ln2_b)
    pg = pack(g_fb, g_conv, g_rb, g_ln1_g, g_ln1_b, g_ln2_g, g_ln2_b)
    pm = pack(m_f_bias, m_conv_w, m_rel_bias, m_ln1_g, m_ln1_b, m_ln2_g, m_ln2_b)
    pv = pack(v_f_bias, v_conv_w, v_rel_bias, v_ln1_g, v_ln1_b, v_ln2_g, v_ln2_b)
    ups = [u[0] for u in _adamw(pw[None], pg[None], pm[None], pv[None], SMALL_ROWS)]

    def unpack(p):
        return dict(ln1_g=p[0:2], ln1_b=p[2:4], ln2_g=p[4:6], ln2_b=p[6:8],
                    conv_w=p[8, :192].reshape(2, 3, 32), f_bias=p[9, :2 * NH].reshape(2, NH),
                    rel_bias=p[10, :32 * NH].reshape(32, NH))

    sm = [unpack(p) for p in ups]

    def group(k):
        return (up_in[k], sm[k]["f_bias"], sm[k]["conv_w"], up_out[k], sm[k]["rel_bias"], sm[k]["ln1_g"],
                sm[k]["ln1_b"], up_gate[k], up_up[k], up_down[k], sm[k]["ln2_g"], sm[k]["ln2_b"])

    grads = (g_w_in, g_fb, g_conv, g_w_out, g_rb, g_ln1_g, g_ln1_b, g_w_gate, g_w_up, g_w_down, g_ln2_g, g_ln2_b)
    return (loss, grad_x) + grads + group(0) + group(1) + group(2)
```

```python
import math

import numpy as np
import jax
import jax.numpy as jnp
from jax import lax
from jax.experimental import pallas as pl
from jax.experimental.pallas import tpu as pltpu

F32 = jnp.float32
BF16 = jnp.bfloat16
MESH = pl.DeviceIdType.MESH

D = 1024
S = 2048
BL = 2
T = BL * S
NH = 4
DFF = 2816
NPROJ = 3076
NPAD = 3200
QKVW = 2304
CONVW = 768
GATEW = 128
PAIRW = 384
BQ = 128
HB = 2 * BQ
NB = S // BQ
NDEV = 8
NSTAT = BL * NH
ALPHA = 4.0 ** 0.25
SCALE = 0.125
NEG = -1e30
LN_EPS = 1e-5
ADAM_LR, ADAM_B1, ADAM_B2, ADAM_EPS, ADAM_WD, ADAM_STEP = 0.001, 0.9, 0.999, 1e-08, 0.01, 10
VMEM_LIMIT = 56 * 1024 * 1024
SMALL_ROWS = 16


def _bucket_thresholds():
    d = np.arange(0, S)
    nf = np.maximum(d, 1).astype(np.float32)
    large = 16 + (np.log(nf / np.float32(16)) / np.float32(math.log(128)) * np.float32(16)).astype(np.int32)
    b = np.where(d < 16, d, np.minimum(large, 31))
    return [int(np.argmax(b >= k)) for k in range(32)]


BUCKET_TH = _bucket_thresholds()


def _cp(sem=None, vmem=VMEM_LIMIT):
    return pltpu.CompilerParams(dimension_semantics=sem, vmem_limit_bytes=vmem)


def _dot(a, b):
    return lax.dot_general(a, b, (((1,), (0,)), ((), ())), preferred_element_type=F32)


def _dot_nt(a, b):
    return lax.dot_general(a, b, (((1,), (1,)), ((), ())), preferred_element_type=F32)


def _dot_tn(a, b):
    return lax.dot_general(a, b, (((0,), (0,)), ((), ())), preferred_element_type=F32)


def _split2(x):
    hi = x.astype(BF16)
    mid = (x - hi.astype(F32)).astype(BF16)
    return jnp.concatenate([hi, mid], axis=1)


def _split3(x):
    hi = x.astype(BF16)
    r = x - hi.astype(F32)
    mid = r.astype(BF16)
    lo = (r - mid.astype(F32)).astype(BF16)
    return jnp.concatenate([hi, mid, lo], axis=1)


def _log_sigmoid(u):
    return jnp.minimum(u, 0.0) - jnp.log1p(jnp.exp(-jnp.abs(u)))


def _log_sigmoid_tile(u):
    return jnp.minimum(u, 0.0) - jnp.log(1.0 + jnp.exp(jnp.minimum(u, -u)))


def _iota(shape, dim):
    return lax.broadcasted_iota(jnp.int32, shape, dim)


ANY_SPEC = pl.BlockSpec(memory_space=pl.ANY)
VMEM_SPEC = pl.BlockSpec(memory_space=pltpu.VMEM)


def _mm(pairs, *, nt, M, N, tm, tn, out_dtype, name, res=None, res_scale=1.0, job=None):
    n = len(pairs)
    n_in = 2 * n + (res is not None)
    jins = job.ins if job else []
    jouts = job.out_shapes if job else []
    gi, gj = M // tm, N // tn

    def body(*refs):
        o_ref = refs[n_in + len(jins)]
        if job:
            jrefs = (refs[n_in:n_in + len(jins)], refs[n_in + len(jins) + 1:n_in + len(jins) + 1 + len(jouts)],
                     refs[n_in + len(jins) + 1 + len(jouts):])

            @pl.when((pl.program_id(0) == 0) & (pl.program_id(1) == 0))
            def _():
                job.start(*jrefs)

        acc = None
        for p in range(n):
            a = refs[2 * p][...].astype(BF16)
            b = refs[2 * p + 1][...]
            d = _dot_nt(a, b) if nt else _dot(a, b)
            acc = d if acc is None else acc + d
        if res is not None:
            acc = acc + res_scale * refs[2 * n][...]
        o_ref[...] = acc.astype(out_dtype)
        if job:
            @pl.when((pl.program_id(0) == gi - 1) & (pl.program_id(1) == gj - 1))
            def _():
                job.finish(None, *jrefs)

    ops, specs = [], []
    for a, asp, b, bsp in pairs:
        ops += [a, b]
        specs += [asp, bsp]
    if res is not None:
        ops.append(res)
        specs.append(pl.BlockSpec((tm, tn), lambda i, j: (i, j)))
    out = pl.pallas_call(
        body, name=name, grid=(gi, gj), in_specs=specs + [ANY_SPEC] * len(jins),
        out_specs=[pl.BlockSpec((tm, tn), lambda i, j: (i, j))] + [ANY_SPEC] * len(jouts),
        out_shape=[jax.ShapeDtypeStruct((M, N), out_dtype)] + list(jouts),
        scratch_shapes=list(job.sems) if job else [],
        input_output_aliases={n_in + a: 1 + b for a, b in job.aliases.items()} if job else {},
        compiler_params=_cp(("arbitrary", "arbitrary") if job else ("parallel", "parallel")))(*ops, *jins)
    return (out[0], out[1:]) if job else out[0]


def _mm_tn(a, b, gbuf, *, C, Ka, N, tm, tn, tk, ooff, name):
    def body(*refs):
        a_ref, b_ref, o_ref = refs[0], refs[1], refs[-1]
        k = pl.program_id(2)
        d = _dot_tn(a_ref[...].astype(BF16), b_ref[...].astype(BF16))

        @pl.when(k == 0)
        def _():
            o_ref[...] = d

        @pl.when(k > 0)
        def _():
            o_ref[...] += d

    ops = [a, b] + ([] if gbuf is None else [gbuf])
    return pl.pallas_call(
        body, name=name, grid=(Ka // tm, N // tn, T // tk),
        in_specs=[pl.BlockSpec((tk, tm), lambda i, j, k: (k, i)),
                  pl.BlockSpec((tk, tn), lambda i, j, k: (k, j))] + ([] if gbuf is None else [ANY_SPEC]),
        out_specs=pl.BlockSpec((tm, tn), lambda i, j, k: (i, ooff + j)),
        out_shape=jax.ShapeDtypeStruct((Ka, C), F32),
        input_output_aliases={} if gbuf is None else {2: 0},
        compiler_params=_cp(("parallel", "parallel", "arbitrary")))(*ops)


def _proj(xb, w):
    tm = 512

    def body(x_ref, w_ref, qkv_ref, conv_ref, gate_ref):
        xv = x_ref[...]
        qkv_ref[...] = _dot(xv, w_ref[:, 0:QKVW]).astype(BF16)
        conv_ref[...] = _dot(xv, w_ref[:, QKVW:QKVW + CONVW])
        gate_ref[...] = _dot(xv, w_ref[:, QKVW + CONVW:NPAD])

    def rows(n):
        return pl.BlockSpec((tm, n), lambda i: (i, 0))

    return pl.pallas_call(
        body, name="proj", grid=(T // tm,),
        in_specs=[rows(D), pl.BlockSpec((D, NPAD), lambda i: (0, 0))],
        out_specs=[rows(QKVW), rows(CONVW), rows(GATEW)],
        out_shape=[jax.ShapeDtypeStruct((T, QKVW), BF16), jax.ShapeDtypeStruct((T, CONVW), F32),
                   jax.ShapeDtypeStruct((T, GATEW), F32)],
        compiler_params=_cp(("parallel",)))(xb, w)


def _proj_bwd(dqkv, dconv, dgate, w, res):
    tm = 512

    def body(a_ref, b_ref, c_ref, w_ref, r_ref, o_ref):
        acc = ALPHA * r_ref[...] + _dot_nt(a_ref[...], w_ref[:, 0:QKVW])
        acc = acc + _dot_nt(b_ref[...], w_ref[:, QKVW:QKVW + CONVW])
        o_ref[...] = acc + _dot_nt(c_ref[...].astype(BF16), w_ref[:, QKVW + CONVW:NPAD])

    def rows(n):
        return pl.BlockSpec((tm, n), lambda i: (i, 0))

    return pl.pallas_call(
        body, name="proj_bwd", grid=(T // tm,),
        in_specs=[rows(QKVW), rows(CONVW), rows(GATEW), pl.BlockSpec((D, NPAD), lambda i: (0, 0)), rows(D)],
        out_specs=rows(D), out_shape=jax.ShapeDtypeStruct((T, D), F32),
        compiler_params=_cp(("parallel",)))(dqkv, dconv, dgate, w, res)


def _ffn_fwd(xb, x, wgt, wut, wd, gam, bet):
    tm, ch = 512, 256

    def body(xb_ref, x_ref, g_ref, b_ref, wg_hbm, wu_hbm, wd_hbm,
             go_ref, uo_ref, ao_ref, y_ref, xh_ref, r_ref, yb_ref, wg_v, wu_v, wd_v, sem):
        loads = [pltpu.make_async_copy(s, d, sem.at[k])
                 for k, (s, d) in enumerate(((wg_hbm, wg_v), (wu_hbm, wu_v), (wd_hbm, wd_v)))]

        @pl.when(pl.program_id(0) == 0)
        def _():
            for cp in loads:
                cp.start()
            loads[0].wait()
            loads[1].wait()

        xv = xb_ref[...]
        for c in range(0, DFF, ch):
            gv = _dot_nt(xv, wg_v[c:c + ch, :])
            uv = _dot_nt(xv, wu_v[c:c + ch, :])
            go_ref[:, c:c + ch] = gv.astype(BF16)
            uo_ref[:, c:c + ch] = uv.astype(BF16)
            ao_ref[:, c:c + ch] = (gv * jax.nn.sigmoid(gv) * uv).astype(BF16)
        @pl.when(pl.program_id(0) == 0)
        def _():
            loads[2].wait()

        s = ALPHA * x_ref[...] + _dot(ao_ref[...], wd_v[...])
        mu = jnp.mean(s, axis=-1, keepdims=True)
        xc = s - mu
        var = jnp.mean(xc * xc, axis=-1, keepdims=True)
        r = lax.rsqrt(var + LN_EPS)
        xh = xc * r
        xh_ref[...] = xh.astype(BF16)
        r_ref[...] = r
        y = xh * g_ref[...] + b_ref[...]
        y_ref[...] = y
        yb_ref[...] = y.astype(BF16)

    row = pl.BlockSpec((tm, D), lambda i: (i, 0))
    wide = pl.BlockSpec((tm, DFF), lambda i: (i, 0))
    vec = pl.BlockSpec((1, D), lambda i: (0, 0))
    wsl = pltpu.VMEM((DFF, D), BF16)
    hid = jax.ShapeDtypeStruct((T, DFF), BF16)
    return pl.pallas_call(
        body, name="ffn_fwd", grid=(T // tm,),
        in_specs=[row, row, vec, vec, ANY_SPEC, ANY_SPEC, ANY_SPEC],
        out_specs=[wide, wide, wide, row, row, pl.BlockSpec((tm, 1), lambda i: (i, 0)), row],
        out_shape=[hid, hid, hid, jax.ShapeDtypeStruct((T, D), F32), jax.ShapeDtypeStruct((T, D), BF16),
                   jax.ShapeDtypeStruct((T, 1), F32), jax.ShapeDtypeStruct((T, D), BF16)],
        scratch_shapes=[wsl, wsl, wsl, pltpu.SemaphoreType.DMA((3,))],
        compiler_params=_cp(("arbitrary",)))(xb, x, gam, bet, wgt, wut, wd)


def _ffn_bwd(dy, xh, r, gam, g, u, wd, wgt, wut, target=None):
    tm, ch = 256, 256

    def body(*refs):
        if target is None:
            (dy_ref, xh_ref, r_ref, gam_ref, g_ref, u_ref, wd_hbm, wg_hbm, wu_hbm,
             dg_ref, du_ref, dsb_ref, dx_ref, dgam_ref, dbet_ref, wd_v, wg_v, wu_v, sem) = refs
        else:
            (dy_ref, t_ref, xh_ref, r_ref, gam_ref, g_ref, u_ref, wd_hbm, wg_hbm, wu_hbm,
             dg_ref, du_ref, dsb_ref, dx_ref, dgam_ref, dbet_ref, sq_ref, wd_v, wg_v, wu_v, sem) = refs
        loads = [pltpu.make_async_copy(s, d, sem.at[k])
                 for k, (s, d) in enumerate(((wd_hbm, wd_v), (wg_hbm, wg_v), (wu_hbm, wu_v)))]

        @pl.when(pl.program_id(0) == 0)
        def _():
            for cp in loads:
                cp.start()
            loads[0].wait()

        if target is None:
            dyv = dy_ref[...]
        else:
            e = dy_ref[...] - t_ref[...]
            dyv = e * (1.0 / D)
            p = jnp.sum(jnp.sum(e * e, axis=1, keepdims=True), axis=0, keepdims=True)

            @pl.when(pl.program_id(0) == 0)
            def _():
                sq_ref[...] = p

            @pl.when(pl.program_id(0) > 0)
            def _():
                sq_ref[...] += p

        xhv = xh_ref[...].astype(F32)
        dxh = dyv * gam_ref[...]
        m1 = jnp.mean(dxh, axis=-1, keepdims=True)
        m2 = jnp.mean(dxh * xhv, axis=-1, keepdims=True)
        ds = r_ref[...] * (dxh - m1 - xhv * m2)
        pg = jnp.sum(dyv * xhv, axis=0, keepdims=True)
        pb = jnp.sum(dyv, axis=0, keepdims=True)

        @pl.when(pl.program_id(0) == 0)
        def _():
            dgam_ref[...] = pg
            dbet_ref[...] = pb

        @pl.when(pl.program_id(0) > 0)
        def _():
            dgam_ref[...] += pg
            dbet_ref[...] += pb

        db = ds.astype(BF16)
        dsb_ref[...] = db
        for c in range(0, DFF, ch):
            da = _dot_nt(db, wd_v[c:c + ch, :])
            gv = g_ref[:, c:c + ch].astype(F32)
            sg = jax.nn.sigmoid(gv)
            dg_ref[:, c:c + ch] = (da * u_ref[:, c:c + ch].astype(F32) * (sg * (1.0 + gv * (1.0 - sg)))).astype(BF16)
            du_ref[:, c:c + ch] = (da * (gv * sg)).astype(BF16)
        @pl.when(pl.program_id(0) == 0)
        def _():
            loads[1].wait()
            loads[2].wait()

        dx_ref[...] = ALPHA * ds + _dot(dg_ref[...], wg_v[...]) + _dot(du_ref[...], wu_v[...])

    row = pl.BlockSpec((tm, D), lambda i: (i, 0))
    wide = pl.BlockSpec((tm, DFF), lambda i: (i, 0))
    vec = pl.BlockSpec((1, D), lambda i: (0, 0))
    wsl = pltpu.VMEM((DFF, D), BF16)
    last = target is not None
    return pl.pallas_call(
        body, name="ffn_bwd_loss" if last else "ffn_bwd", grid=(T // tm,),
        in_specs=[row] + ([row] if last else [])
        + [row, pl.BlockSpec((tm, 1), lambda i: (i, 0)), vec, wide, wide, ANY_SPEC, ANY_SPEC, ANY_SPEC],
        out_specs=[wide, wide, row, row, vec, vec] + ([pl.BlockSpec((1, 1), lambda i: (0, 0))] if last else []),
        out_shape=[jax.ShapeDtypeStruct((T, DFF), BF16), jax.ShapeDtypeStruct((T, DFF), BF16),
                   jax.ShapeDtypeStruct((T, D), BF16), jax.ShapeDtypeStruct((T, D), F32),
                   jax.ShapeDtypeStruct((1, D), F32), jax.ShapeDtypeStruct((1, D), F32)]
        + ([jax.ShapeDtypeStruct((1, 1), F32)] if last else []),
        scratch_shapes=[wsl, wsl, wsl, pltpu.SemaphoreType.DMA((3,))],
        compiler_params=_cp(("arbitrary",)))(dy, *([target] if last else []), xh, r, gam, g, u, wd, wgt, wut)


def _mm_ln(a, w, x, gam, bet, name):
    tm = 256
    K = a.shape[1]

    def body(a_ref, w_ref, x_ref, g_ref, b_ref, y_ref, xh_ref, r_ref, yb_ref):
        s = ALPHA * x_ref[...] + _dot(a_ref[...], w_ref[...])
        mu = jnp.mean(s, axis=-1, keepdims=True)
        xc = s - mu
        var = jnp.mean(xc * xc, axis=-1, keepdims=True)
        r = lax.rsqrt(var + LN_EPS)
        xh = xc * r
        xh_ref[...] = xh.astype(BF16)
        r_ref[...] = r
        y = xh * g_ref[...] + b_ref[...]
        y_ref[...] = y
        yb_ref[...] = y.astype(BF16)

    row = pl.BlockSpec((tm, D), lambda i: (i, 0))
    vec = pl.BlockSpec((1, D), lambda i: (0, 0))
    return pl.pallas_call(
        body, name=name, grid=(T // tm,),
        in_specs=[pl.BlockSpec((tm, K), lambda i: (i, 0)), pl.BlockSpec((K, D), lambda i: (0, 0)), row, vec, vec],
        out_specs=[row, row, pl.BlockSpec((tm, 1), lambda i: (i, 0)), row],
        out_shape=[jax.ShapeDtypeStruct((T, D), F32), jax.ShapeDtypeStruct((T, D), BF16),
                   jax.ShapeDtypeStruct((T, 1), F32), jax.ShapeDtypeStruct((T, D), BF16)],
        compiler_params=_cp(("parallel",)))(a, w, x, gam, bet)


def _ln_bwd(dy, xh, r, gam, w):
    tm = 256

    def body(dy_ref, xh_ref, r_ref, g_ref, w_ref, ds_ref, dg_ref, db_ref, dsb_ref, dm_ref):
        i = pl.program_id(0)
        dyv = dy_ref[...]
        xhv = xh_ref[...].astype(F32)
        dxh = dyv * g_ref[...]
        m1 = jnp.mean(dxh, axis=-1, keepdims=True)
        m2 = jnp.mean(dxh * xhv, axis=-1, keepdims=True)
        ds = r_ref[...] * (dxh - m1 - xhv * m2)
        ds_ref[...] = ds
        dsb = ds.astype(BF16)
        dsb_ref[...] = dsb
        dm_ref[...] = _dot_nt(dsb, w_ref[...]).astype(BF16)
        pg = jnp.sum(dyv * xhv, axis=0, keepdims=True)
        pb = jnp.sum(dyv, axis=0, keepdims=True)

        @pl.when(i == 0)
        def _():
            dg_ref[...] = pg
            db_ref[...] = pb

        @pl.when(i > 0)
        def _():
            dg_ref[...] += pg
            db_ref[...] += pb

    row = pl.BlockSpec((tm, D), lambda i: (i, 0))
    vec = pl.BlockSpec((1, D), lambda i: (0, 0))
    return pl.pallas_call(
        body, name="ln_bwd_proj", grid=(T // tm,),
        in_specs=[row, row, pl.BlockSpec((tm, 1), lambda i: (i, 0)), vec, pl.BlockSpec((D, D), lambda i: (0, 0))],
        out_specs=[row, vec, vec, row, row],
        out_shape=[jax.ShapeDtypeStruct((T, D), F32), jax.ShapeDtypeStruct((1, D), F32),
                   jax.ShapeDtypeStruct((1, D), F32), jax.ShapeDtypeStruct((T, D), BF16),
                   jax.ShapeDtypeStruct((T, D), BF16)],
        compiler_params=_cp(("arbitrary",)))(dy, xh, r, gam, w)


def _adamw(w, g, m, v, tr):
    L, R, C = w.shape

    def body(w_ref, g_ref, m_ref, v_ref, d_ref, m2_ref, v2_ref):
        gv = g_ref[...]
        m2 = ADAM_B1 * m_ref[...] + (1.0 - ADAM_B1) * gv
        v2 = ADAM_B2 * v_ref[...] + (1.0 - ADAM_B2) * (gv * gv)
        m_hat = m2 / (1.0 - ADAM_B1 ** ADAM_STEP)
        v_hat = v2 / (1.0 - ADAM_B2 ** ADAM_STEP)
        d_ref[...] = -ADAM_LR * (m_hat / (jnp.sqrt(v_hat) + ADAM_EPS) + ADAM_WD * w_ref[...])
        m2_ref[...] = m2
        v2_ref[...] = v2

    blk = pl.BlockSpec((None, tr, C), lambda l, i: (l, i, 0))
    sh = jax.ShapeDtypeStruct((L, R, C), F32)
    return pl.pallas_call(
        body, name="adamw", grid=(L, R // tr), in_specs=[blk] * 4, out_specs=[blk] * 3,
        out_shape=[sh, sh, sh], compiler_params=_cp(("parallel", "parallel")))(w, g, m, v)


class _Job:
    def __init__(self, ins, out_shapes, aliases, sems, start, finish):
        self.ins, self.out_shapes, self.aliases, self.sems = list(ins), list(out_shapes), dict(aliases), list(sems)
        self.start, self.finish = start, finish


def _host_call(body, name, ins, in_specs, out_shapes, out_specs, scratch, aliases, job):
    n_in, n_out, n_scr = len(ins), len(out_shapes), len(scratch)
    jins = job.ins if job else []
    jouts = job.out_shapes if job else []
    jsems = job.sems if job else []

    def wrapped(*refs):
        a = n_in
        b = a + len(jins)
        c = b + n_out
        d = c + len(jouts)
        e = d + n_scr
        comm = None
        if job:
            jrefs = (refs[a:b], refs[c:d], refs[e:])
            comm = (lambda: job.start(*jrefs), lambda st: job.finish(st, *jrefs))
        body(refs[:a], refs[b:c], refs[d:e], comm)

    al = dict(aliases)
    if job:
        for ji, jo in job.aliases.items():
            al[n_in + ji] = n_out + jo
    res = pl.pallas_call(
        wrapped, name=name, in_specs=list(in_specs) + [ANY_SPEC] * len(jins),
        out_specs=list(out_specs) + [ANY_SPEC] * len(jouts), out_shape=list(out_shapes) + list(jouts),
        scratch_shapes=list(scratch) + list(jsems), input_output_aliases=al,
        compiler_params=_cp())(*ins, *jins)
    return res[:n_out], res[n_out:]


def _copy_in(src, dst, sem):
    cp = pltpu.make_async_copy(src, dst, sem)
    cp.start()
    cp.wait()


CHAINS = [(p, b) for p in range(2) for b in range(BL)]
NC = len(CHAINS)
ROWS_SHAPE = jax.ShapeDtypeStruct((NSTAT, S), F32)
SLAB_QKV = pltpu.VMEM((T, 2 * PAIRW), BF16)
SLAB_OUT = pltpu.VMEM((T, 2 * BQ), BF16)
SLAB_O32 = pltpu.VMEM((T, 2 * BQ), F32)
SLAB_T = pltpu.VMEM((2, BQ, T), BF16)
SLAB_KEYB = pltpu.VMEM((NSTAT, S, BQ), F32)
ACC_KV = pltpu.VMEM((2, T, BQ), F32)
A_TILES = jax.ShapeDtypeStruct((NC, NB * (NB + 1) // 2, HB, BQ), BF16)
A_STAGE = pltpu.VMEM((NC, 2, HB, BQ), BF16)


def _lane_masks():
    lane = _iota((1, BQ), 1)
    m0 = (lane < 64).astype(BF16)
    return m0, 1.0 - m0


def _row_masks():
    r = _iota((BQ, 1), 0)
    m0 = (r < 64).astype(BF16)
    return m0, 1.0 - m0


def _stack(x, m0, m1):
    return jnp.concatenate([x * m0, x * m1], axis=0)


def _stack_t(xt, r0, r1):
    return jnp.concatenate([xt * r0, xt * r1], axis=1)


def _tr(x):
    return x.T


def _rows(b, i):
    return pl.ds(pl.multiple_of(b * S + i * BQ, BQ), BQ)


def _transpose_slab(src, dst, col0):
    def blk(n, _):
        r = pl.ds(pl.multiple_of(n * BQ, BQ), BQ)
        for p in range(2):
            dst[p, :, r] = _tr(src[r, col0(p):col0(p) + BQ])
        return 0

    lax.fori_loop(0, T // BQ, blk, 0)


def _heads(x):
    return x[:BQ], x[BQ:]


def _bcast_heads(r0, r1):
    return jnp.concatenate([jnp.broadcast_to(r0, (BQ, BQ)), jnp.broadcast_to(r1, (BQ, BQ))], axis=0)


def _by_channel(r0, r1):
    return jnp.where(_iota((BQ, BQ), 0) < 64, r0, r1)


def _colsum2(x):
    return jnp.sum(x[:BQ], axis=0, keepdims=True), jnp.sum(x[BQ:], axis=0, keepdims=True)


def _stat_row(ref, p, b, h, i):
    c = b * NH + 2 * p + h
    return ref[c:c + 1, pl.ds(pl.multiple_of(i * BQ, BQ), BQ)]


def _put_row(ref, p, b, h, i, v):
    c = b * NH + 2 * p + h
    ref[c:c + 1, pl.ds(pl.multiple_of(i * BQ, BQ), BQ)] = v


def _valid_t(strict):
    r = _iota((HB, BQ), 0) & (BQ - 1)
    c = _iota((HB, BQ), 1)
    return (r < c) if strict else (r <= c)


def _tri_blockdiag(later):
    r = _iota((HB, HB), 0)
    c = _iota((HB, HB), 1)
    same = (r >= BQ) == (c >= BQ)
    return (same & ((c > r) if later else (c < r))).astype(BF16)


def _cum_mm(tri, x):
    y = _dot(tri, _split2(x))
    return y[:, :BQ] + y[:, BQ:]


def _kv_tiles(qkv_v, p, b, j):
    r = _rows(b, j)
    return qkv_v[r, p * PAIRW + BQ:p * PAIRW + 2 * BQ], qkv_v[r, p * PAIRW + 2 * BQ:p * PAIRW + 3 * BQ]


def _q_tile(qkv_v, p, b, i):
    return qkv_v[_rows(b, i), p * PAIRW:p * PAIRW + BQ] * SCALE


def _sb_fwd(qkv, job=None):
    def body(ins, outs, scr, comm):
        (qkv_hbm,), (o_hbm, a_hbm), (qkv_v, o_v, sem, vt_v, a_st, a_sems) = ins, outs, scr
        _copy_in(qkv_hbm.at[:, pl.ds(0, 2 * PAIRW)], qkv_v, sem)
        st = comm[0]() if comm else None
        _transpose_slab(qkv_v, vt_v, lambda p: p * PAIRW + 2 * BQ)
        m0, m1 = _lane_masks()
        r0, r1 = _row_masks()
        valid = _valid_t(True)
        later = _tri_blockdiag(True)

        def a_copy(c, slot, t):
            return pltpu.make_async_copy(a_st.at[c, slot], a_hbm.at[c, t], a_sems.at[c, slot])

        def steps(qts, i, jj, cs, diag):
            j = i - jj
            ks = [_stack(_kv_tiles(qkv_v, p, b, j)[0], m0, m1) for p, b in CHAINS]
            zs = [_dot(ks[c], qts[c]) for c in range(NC)]
            lbs, lrs = [], []
            for c in range(NC):
                lb = _log_sigmoid_tile(zs[c])
                lr = lb - zs[c]
                if diag:
                    lr = jnp.where(valid, lr, 0.0)
                lbs.append(lb)
                lrs.append(lr)
            tails = [_cum_mm(later, lrs[c]) for c in range(NC)]
            avs = []
            for c in range(NC):
                a = jnp.exp(lbs[c] + tails[c] + _bcast_heads(*cs[c][0]))
                if diag:
                    a = jnp.where(valid, a, 0.0)
                avs.append(a.astype(BF16))
            slot = jj & 1
            t = (i * (i + 1)) // 2 + j
            for c in range(NC):
                if not diag:
                    @pl.when(jj >= 2)
                    def _():
                        a_copy(c, slot, 0).wait()
                a_st[c, slot] = avs[c]
                a_copy(c, slot, t).start()
            out = []
            for c, (p, b) in enumerate(CHAINS):
                vts = _stack_t(vt_v[p, :, _rows(b, j)], r0, r1)
                s0, s1 = _colsum2(lrs[c])
                out.append(((cs[c][0][0] + s0, cs[c][0][1] + s1), cs[c][1] + _dot(vts, avs[c])))
            return tuple(out)

        def qblock(i, _):
            qts = [_tr(_q_tile(qkv_v, p, b, i)) for p, b in CHAINS]
            zr = jnp.zeros((1, BQ), F32)
            cs = steps(qts, i, 0, (((zr, zr), jnp.zeros((BQ, BQ), F32)),) * NC, True)
            cs = lax.fori_loop(1, i + 1, lambda jj, cs: steps(qts, i, jj, cs, False), cs)
            for c, (p, b) in enumerate(CHAINS):
                o_v[_rows(b, i), p * BQ:(p + 1) * BQ] = cs[c][1].T.astype(BF16)
                a_copy(c, i & 1, 0).wait()

                @pl.when(i >= 1)
                def _():
                    a_copy(c, 1 - (i & 1), 0).wait()
            return 0

        lax.fori_loop(0, NB, qblock, 0)
        _copy_in(o_v, o_hbm.at[:, pl.ds(0, 2 * BQ)], sem)
        if comm:
            comm[1](st)

    (mixed, amat), extra = _host_call(
        body, "sb_fwd", [qkv], [ANY_SPEC], [jax.ShapeDtypeStruct((T, D), BF16), A_TILES], [ANY_SPEC, ANY_SPEC],
        [SLAB_QKV, SLAB_OUT, pltpu.SemaphoreType.DMA, SLAB_T, A_STAGE, pltpu.SemaphoreType.DMA((NC, 2))], {}, job)
    return mixed, amat, extra


def _sb_bwd(qkv, dmixed, amat, job=None):
    def body(ins, outs, scr, comm):
        (qkv_hbm, do_hbm, a_hbm), (dqkv_hbm,), (qkv_v, do_v, dq_v, dk_s, dv_s, sems, kt_v, a_st, a_sems) = ins, outs, scr
        sem = sems.at[0]

        def a_copy(c, slot, t):
            return pltpu.make_async_copy(a_hbm.at[c, t], a_st.at[c, slot], a_sems.at[c, slot])

        later = [pltpu.make_async_copy(do_hbm.at[:, pl.ds(0, 2 * BQ)], do_v, sems.at[1])]
        for cp in later:
            cp.start()
        _copy_in(qkv_hbm.at[:, pl.ds(0, 2 * PAIRW)], qkv_v, sem)
        st = comm[0]() if comm else None
        _transpose_slab(qkv_v, kt_v, lambda p: p * PAIRW + BQ)
        for cp in later:
            cp.wait()
        m0, m1 = _lane_masks()
        f0, f1 = m0.astype(F32), m1.astype(F32)
        r0, r1 = _row_masks()
        valid = _valid_t(True)
        earlier = _tri_blockdiag(False)
        dk_s[...] = jnp.zeros_like(dk_s)
        dv_s[...] = jnp.zeros_like(dv_s)

        def steps(qns, qts, dns, dts, i, j, cs, diag):
            slot = j & 1
            t = (i * (i + 1)) // 2 + j
            for c in range(NC):
                a_copy(c, slot, 0).wait()
                if not diag:
                    a_copy(c, 1 - slot, t + 1).start()
            kv = [_kv_tiles(qkv_v, p, b, j) for p, b in CHAINS]
            ks = [_stack(kv[c][0], m0, m1) for c in range(NC)]
            vs = [_stack(kv[c][1], m0, m1) for c in range(NC)]
            zs = [_dot(ks[c], qts[c]) for c in range(NC)]
            das = [_dot(vs[c], dts[c]) for c in range(NC)]
            avs = [a_st[c, slot] for c in range(NC)]
            gms = [das[c] * avs[c].astype(F32) for c in range(NC)]
            befores = [_cum_mm(earlier, gms[c]) for c in range(NC)]
            dzbs = []
            for c in range(NC):
                dz = gms[c] - jax.nn.sigmoid(zs[c]) * (gms[c] + befores[c] + _bcast_heads(*cs[c][0]))
                if diag:
                    dz = jnp.where(valid, dz, 0.0)
                dzbs.append(dz.astype(BF16))
            out = []
            for c, (p, b) in enumerate(CHAINS):
                dq = cs[c][1] + _dot(_stack_t(kt_v[p, :, _rows(b, j)], r0, r1), dzbs[c])
                dk = _dot(dzbs[c], qns[c])
                dv = _dot(avs[c], dns[c])
                dk_s[p, _rows(b, j), :] += dk[:BQ] * f0 + dk[BQ:] * f1
                dv_s[p, _rows(b, j), :] += dv[:BQ] * f0 + dv[BQ:] * f1
                g0, g1 = _colsum2(gms[c])
                out.append(((cs[c][0][0] + g0, cs[c][0][1] + g1), dq))
            return tuple(out)

        def qblock(i, _):
            for c in range(NC):
                a_copy(c, 0, (i * (i + 1)) // 2).start()
            qns = [_q_tile(qkv_v, p, b, i) for p, b in CHAINS]
            dns = [do_v[_rows(b, i), p * BQ:(p + 1) * BQ] for p, b in CHAINS]
            qts = [_tr(t) for t in qns]
            dts = [_tr(t) for t in dns]
            zr = jnp.zeros((1, BQ), F32)
            cs = (((zr, zr), jnp.zeros((BQ, BQ), F32)),) * NC
            cs = lax.fori_loop(0, i, lambda j, cs: steps(qns, qts, dns, dts, i, j, cs, False), cs)
            cs = steps(qns, qts, dns, dts, i, i, cs, True)
            for c, (p, b) in enumerate(CHAINS):
                dq_v[_rows(b, i), p * PAIRW:p * PAIRW + BQ] = (cs[c][1].T * SCALE).astype(BF16)
            return 0

        lax.fori_loop(0, NB, qblock, 0)
        for p in range(2):
            dq_v[:, p * PAIRW + BQ:p * PAIRW + 2 * BQ] = dk_s[p].astype(BF16)
            dq_v[:, p * PAIRW + 2 * BQ:p * PAIRW + 3 * BQ] = dv_s[p].astype(BF16)
        _copy_in(dq_v, dqkv_hbm.at[:, pl.ds(0, 2 * PAIRW)], sem)
        if comm:
            comm[1](st)

    (dqkv,), extra = _host_call(
        body, "sb_bwd", [qkv, dmixed, amat], [ANY_SPEC, ANY_SPEC, ANY_SPEC],
        [jax.ShapeDtypeStruct((T, QKVW), BF16)], [ANY_SPEC],
        [SLAB_QKV, SLAB_OUT, SLAB_QKV, ACC_KV, ACC_KV, pltpu.SemaphoreType.DMA((4,)), SLAB_T, A_STAGE,
         pltpu.SemaphoreType.DMA((NC, 2))], {}, job)
    return dqkv, extra


def _flash_fwd(qkv, mixed, g, fox, bias, job=None):
    def body(ins, outs, scr, comm):
        if fox:
            qkv_hbm, cq_ref, ckb_hbm, _ = ins
            (o_hbm, lse_ref, o32_hbm), (qkv_v, o_v, sem, vt_v, o32_v, ckb_v) = outs, scr
        else:
            qkv_hbm, tbl_ref, _ = ins
            (o_hbm, lse_ref), (qkv_v, o_v, sem, vt_v) = outs, scr
        sems = sem
        sem = sems.at[0]
        later = [pltpu.make_async_copy(ckb_hbm, ckb_v, sems.at[1])] if fox else []
        for cp in later:
            cp.start()
        _copy_in(qkv_hbm.at[:, pl.ds(g * 2 * PAIRW, 2 * PAIRW)], qkv_v, sem)
        st = comm[0]() if comm else None
        _transpose_slab(qkv_v, vt_v, lambda p: p * PAIRW + 2 * BQ)
        for cp in later:
            cp.wait()
        m0, m1 = _lane_masks()
        r0, r1 = _row_masks()
        valid = _valid_t(False)

        def steps(qts, cqs, i, j, cs, diag):
            ks = [_stack(_kv_tiles(qkv_v, p, b, j)[0], m0, m1) for p, b in CHAINS]
            zs = [_dot(ks[c], qts[c]) for c in range(NC)]
            prs, alphas, out = [], [], []
            for c, (p, b) in enumerate(CHAINS):
                (ma, mb), (la, lb_), _ = cs[c]
                if fox:
                    kk = pl.ds(pl.multiple_of(j * BQ, BQ), BQ)
                    col = b * NH + 2 * p
                    z = zs[c] + (cqs[c] - jnp.concatenate([ckb_v[col, kk, :], ckb_v[col + 1, kk, :]], axis=0))
                    if diag:
                        z = jnp.where(valid, z, NEG)
                else:
                    z = zs[c] + tbl_ref[p, i - j]
                za, zb = _heads(z)
                na = jnp.maximum(ma, jnp.max(za, axis=0, keepdims=True))
                nb = jnp.maximum(mb, jnp.max(zb, axis=0, keepdims=True))
                aa, ab = jnp.exp(ma - na), jnp.exp(mb - nb)
                pr = jnp.exp(z - _bcast_heads(na, nb))
                sa, sb = _colsum2(pr)
                prs.append(_split2(pr) if fox else pr.astype(BF16))
                alphas.append((aa, ab))
                out.append(((na, nb), (aa * la + sa, ab * lb_ + sb)))
            pvs = []
            for c, (p, b) in enumerate(CHAINS):
                vts = _stack_t(vt_v[p, :, _rows(b, j)], r0, r1)
                if fox:
                    pvs.append(_dot(vts, prs[c][:, :BQ]) + _dot(vts, prs[c][:, BQ:]))
                else:
                    pvs.append(_dot(vts, prs[c]))
            return tuple((out[c][0], out[c][1], _by_channel(*alphas[c]) * cs[c][2] + pvs[c]) for c in range(NC))

        def qblock(i, _):
            qts = [_tr(_q_tile(qkv_v, p, b, i)) for p, b in CHAINS]
            if fox:
                cqs = [_bcast_heads(_stat_row(cq_ref, p, b, 0, i), _stat_row(cq_ref, p, b, 1, i)) for p, b in CHAINS]
            else:
                cqs = [None] * NC
            ng = jnp.full((1, BQ), NEG, F32)
            zr = jnp.zeros((1, BQ), F32)
            cs = steps(qts, cqs, i, i, (((ng, ng), (zr, zr), jnp.zeros((BQ, BQ), F32)),) * NC, True)
            cs = lax.fori_loop(1, i + 1, lambda jj, cs: steps(qts, cqs, i, i - jj, cs, False), cs)
            for c, (p, b) in enumerate(CHAINS):
                (ma, mb), (la, lb_), acc = cs[c]
                o = (acc / _by_channel(la, lb_)).T
                o_v[_rows(b, i), p * BQ:(p + 1) * BQ] = o.astype(BF16)
                if fox:
                    o32_v[_rows(b, i), p * BQ:(p + 1) * BQ] = o
                _put_row(lse_ref, p, b, 0, i, ma + jnp.log(la))
                _put_row(lse_ref, p, b, 1, i, mb + jnp.log(lb_))
            return 0

        lax.fori_loop(0, NB, qblock, 0)
        _copy_in(o_v, o_hbm.at[:, pl.ds(g * 2 * BQ, 2 * BQ)], sem)
        if fox:
            _copy_in(o32_v, o32_hbm, sem)
        if comm:
            comm[1](st)

    bias_specs = [VMEM_SPEC, ANY_SPEC] if fox else [VMEM_SPEC]
    n_in = 2 + len(bias_specs)
    o32 = [jax.ShapeDtypeStruct((T, 2 * BQ), F32)] if fox else []
    res, extra = _host_call(
        body, "fox_fwd" if fox else "dil_fwd", [qkv, *bias, mixed], [ANY_SPEC] + bias_specs + [ANY_SPEC],
        [jax.ShapeDtypeStruct((T, D), BF16), ROWS_SHAPE] + o32, [ANY_SPEC, VMEM_SPEC] + [ANY_SPEC] * len(o32),
        [SLAB_QKV, SLAB_OUT, pltpu.SemaphoreType.DMA((4,)), SLAB_T] + ([SLAB_O32, SLAB_KEYB] if fox else []),
        {n_in - 1: 0}, job)
    return (*res, extra)


def _flash_bwd(qkv, o, dmixed, lse, dqkv, g, fox, bias, job=None):
    def body(ins, outs, scr, comm):
        if fox:
            qkv_hbm, o_hbm, do_hbm, lse_ref, cq_ref, ckb_hbm, _ = ins
            (dqkv_hbm, db_ref), (qkv_v, o_v, do_v, dq_v, dk_s, dv_s, sem, kt_v, ckb_v, dc_s) = outs, scr
        else:
            qkv_hbm, o_hbm, do_hbm, lse_ref, tbl_ref, _ = ins
            (dqkv_hbm, db_ref), (qkv_v, o_v, do_v, dq_v, dk_s, dv_s, sem, kt_v) = outs, scr
        sems = sem
        sem = sems.at[0]
        later = [pltpu.make_async_copy(do_hbm.at[:, pl.ds(g * 2 * BQ, 2 * BQ)], do_v, sems.at[1])]
        if fox:
            later += [pltpu.make_async_copy(o_hbm, o_v, sems.at[2]), pltpu.make_async_copy(ckb_hbm, ckb_v, sems.at[3])]
        else:
            later += [pltpu.make_async_copy(o_hbm.at[:, pl.ds(g * 2 * BQ, 2 * BQ)], o_v, sems.at[2])]
        for cp in later:
            cp.start()
        _copy_in(qkv_hbm.at[:, pl.ds(g * 2 * PAIRW, 2 * PAIRW)], qkv_v, sem)
        st = comm[0]() if comm else None
        _transpose_slab(qkv_v, kt_v, lambda p: p * PAIRW + BQ)
        for cp in later:
            cp.wait()
        m0, m1 = _lane_masks()
        f0, f1 = m0.astype(F32), m1.astype(F32)
        r0, r1 = _row_masks()
        valid = _valid_t(False)
        dk_s[...] = jnp.zeros_like(dk_s)
        dv_s[...] = jnp.zeros_like(dv_s)
        if fox:
            dc_s[...] = jnp.zeros_like(dc_s)
        else:
            db_ref[...] = jnp.zeros_like(db_ref)

        def steps(qns, qts, dns, dts, cqs, lses, deltas, i, j, dqs, diag):
            kv = [_kv_tiles(qkv_v, p, b, j) for p, b in CHAINS]
            ks = [_stack(kv[c][0], m0, m1) for c in range(NC)]
            vs = [_stack(kv[c][1], m0, m1) for c in range(NC)]
            zs = [_dot(ks[c], qts[c]) for c in range(NC)]
            dps = [_dot(vs[c], dts[c]) for c in range(NC)]
            prs, dzl = [], []
            for c, (p, b) in enumerate(CHAINS):
                if fox:
                    kk = pl.ds(pl.multiple_of(j * BQ, BQ), BQ)
                    col = b * NH + 2 * p
                    z = zs[c] + (cqs[c] - jnp.concatenate([ckb_v[col, kk, :], ckb_v[col + 1, kk, :]], axis=0))
                    if diag:
                        z = jnp.where(valid, z, NEG)
                else:
                    z = zs[c] + tbl_ref[p, i - j]
                pr = jnp.exp(z - lses[c])
                prs.append(pr.astype(BF16))
                dzl.append(pr * (dps[c] - deltas[c]))
            dzbs = [dz.astype(BF16) for dz in dzl]
            new = []
            for c, (p, b) in enumerate(CHAINS):
                new.append(dqs[c] + _dot(_stack_t(kt_v[p, :, _rows(b, j)], r0, r1), dzbs[c]))
                dk = _dot(dzbs[c], qns[c])
                dv = _dot(prs[c], dns[c])
                dk_s[p, _rows(b, j), :] += dk[:BQ] * f0 + dk[BQ:] * f1
                dv_s[p, _rows(b, j), :] += dv[:BQ] * f0 + dv[BQ:] * f1
                if fox:
                    dc_s[c, pl.ds(pl.multiple_of(j * HB, HB), HB), :] += dzl[c]
            if not fox:
                for p in range(2):
                    db_ref[p, i - j] = db_ref[p, i - j] + (dzl[2 * p] + dzl[2 * p + 1])
            return tuple(new)

        def qblock(i, _):
            qns = [_q_tile(qkv_v, p, b, i) for p, b in CHAINS]
            dns = [do_v[_rows(b, i), p * BQ:(p + 1) * BQ] for p, b in CHAINS]
            qts = [_tr(t) for t in qns]
            dts = [_tr(t) for t in dns]
            lses = [_bcast_heads(_stat_row(lse_ref, p, b, 0, i), _stat_row(lse_ref, p, b, 1, i)) for p, b in CHAINS]
            if fox:
                cqs = [_bcast_heads(_stat_row(cq_ref, p, b, 0, i), _stat_row(cq_ref, p, b, 1, i)) for p, b in CHAINS]
            else:
                cqs = [None] * NC
            deltas = []
            for c, (p, b) in enumerate(CHAINS):
                pt = (dns[c].astype(F32) * o_v[_rows(b, i), p * BQ:(p + 1) * BQ].astype(F32)).T
                deltas.append(_bcast_heads(jnp.sum(pt[:64], axis=0, keepdims=True), jnp.sum(pt[64:], axis=0, keepdims=True)))
            dqs = (jnp.zeros((BQ, BQ), F32),) * NC
            dqs = lax.fori_loop(0, i, lambda j, d: steps(qns, qts, dns, dts, cqs, lses, deltas, i, j, d, False), dqs)
            dqs = steps(qns, qts, dns, dts, cqs, lses, deltas, i, i, dqs, True)
            for c, (p, b) in enumerate(CHAINS):
                dq_v[_rows(b, i), p * PAIRW:p * PAIRW + BQ] = (dqs[c].T * SCALE).astype(BF16)
            return 0

        lax.fori_loop(0, NB, qblock, 0)
        for p in range(2):
            dq_v[:, p * PAIRW + BQ:p * PAIRW + 2 * BQ] = dk_s[p].astype(BF16)
            dq_v[:, p * PAIRW + 2 * BQ:p * PAIRW + 3 * BQ] = dv_s[p].astype(BF16)
        _copy_in(dq_v, dqkv_hbm.at[:, pl.ds(g * 2 * PAIRW, 2 * PAIRW)], sem)
        if fox:
            lane = _iota((BQ, NSTAT), 1)

            def fold(n, _):
                t = jnp.zeros((BQ, NSTAT), F32)
                for c, (p, b) in enumerate(CHAINS):
                    s = jnp.sum(dc_s[c, pl.ds(pl.multiple_of(n * HB, HB), HB), :], axis=1, keepdims=True)
                    col = b * NH + 2 * p
                    t = t - jnp.where(lane == col, s[:BQ], 0.0) - jnp.where(lane == col + 1, s[BQ:], 0.0)
                db_ref[pl.ds(pl.multiple_of(n * BQ, BQ), BQ), :] = t
                return 0

            lax.fori_loop(0, NB, fold, 0)
        if comm:
            comm[1](st)

    if fox:
        bias_specs = [VMEM_SPEC, ANY_SPEC]
        db_shape = jax.ShapeDtypeStruct((S, NSTAT), F32)
        more = [SLAB_KEYB, pltpu.VMEM((NC, NB * HB, BQ), F32)]
    else:
        bias_specs = [VMEM_SPEC]
        db_shape = jax.ShapeDtypeStruct((2, NB, HB, BQ), F32)
        more = []
    n_in = 5 + len(bias_specs)
    (dqkv, db), extra = _host_call(
        body, "fox_bwd" if fox else "dil_bwd", [qkv, o, dmixed, lse, *bias, dqkv],
        [ANY_SPEC, ANY_SPEC, ANY_SPEC, VMEM_SPEC] + bias_specs + [ANY_SPEC],
        [jax.ShapeDtypeStruct((T, QKVW), BF16), db_shape], [ANY_SPEC, VMEM_SPEC],
        [SLAB_QKV, SLAB_O32 if fox else SLAB_OUT, SLAB_OUT, SLAB_QKV, ACC_KV, ACC_KV, pltpu.SemaphoreType.DMA((4,)), SLAB_T]
        + more, {n_in - 1: 0}, job)
    return dqkv, db, extra


def _delta_t(d):
    return d * BQ + _iota((HB, BQ), 1) - (_iota((HB, BQ), 0) & (BQ - 1))


def _buckets_in(d):
    lo, hi = max(d * BQ - (BQ - 1), 0), d * BQ + BQ - 1
    return [b for b in range(32) if BUCKET_TH[b] <= hi and (b == 31 or BUCKET_TH[b + 1] > lo)]


def _in_bucket(delta, b):
    m = delta >= BUCKET_TH[b]
    return m if b == 31 else m & (delta < BUCKET_TH[b + 1])


def _dil_table(rel_bias, job=None):
    def body(ins, outs, scr, comm):
        (rb_ref,), (o_ref,) = ins, outs
        st = comm[0]() if comm else None
        for d in range(NB):
            delta = _delta_t(d)
            pos = delta >= 0
            n = ((pos & (delta <= 128)).astype(jnp.int32)
                 + (pos & (delta <= 512) & ((delta & 3) == 0)).astype(jnp.int32)
                 + (pos & ((delta & 15) == 0)).astype(jnp.int32))
            logn = jnp.where(n == 3, math.log(3.0), jnp.where(n == 2, math.log(2.0), jnp.where(n == 1, 0.0, NEG)))
            head1 = _iota((HB, BQ), 0) >= BQ
            for p in range(2):
                val = jnp.zeros((HB, BQ), F32)
                for b in _buckets_in(d):
                    val = jnp.where(_in_bucket(delta, b), jnp.where(head1, rb_ref[b, 2 * p + 1], rb_ref[b, 2 * p]), val)
                o_ref[p, d] = val + logn
        if comm:
            comm[1](st)

    (tbl,), extra = _host_call(
        body, "dil_table", [rel_bias], [pl.BlockSpec(memory_space=pltpu.SMEM)],
        [jax.ShapeDtypeStruct((2, NB, HB, BQ), F32)], [VMEM_SPEC], [], {}, job)
    return (tbl, extra) if job else tbl


def _dil_table_bwd(dtbl):
    def body(dt_ref, o_ref):
        p = pl.program_id(0)
        rowi = _iota((32, BQ), 0)
        lanei = _iota((32, BQ), 1)

        @pl.when(p == 0)
        def _():
            o_ref[...] = jnp.zeros_like(o_ref)

        out = jnp.zeros((32, BQ), F32)
        for b in range(32):
            acc = None
            for d in range(NB):
                if b in _buckets_in(d):
                    t = jnp.where(_in_bucket(_delta_t(d), b), dt_ref[d], 0.0)
                    acc = t if acc is None else acc + t
            rs = jnp.sum(acc, axis=1, keepdims=True)
            s0 = jnp.sum(rs[:BQ], axis=0, keepdims=True)
            s1 = jnp.sum(rs[BQ:], axis=0, keepdims=True)
            out = (out + jnp.where((rowi == b) & (lanei == 2 * p), s0, 0.0)
                   + jnp.where((rowi == b) & (lanei == 2 * p + 1), s1, 0.0))
        o_ref[...] += out

    return pl.pallas_call(
        body, name="dil_table_bwd", grid=(2,),
        in_specs=[pl.BlockSpec((None, NB, HB, BQ), lambda p: (p, 0, 0, 0))],
        out_specs=pl.BlockSpec((32, BQ), lambda p: (0, 0)),
        out_shape=jax.ShapeDtypeStruct((32, BQ), F32),
        compiler_params=_cp(("arbitrary",)))(dtbl)


def _fox_prep(gate, fb):
    def body(g_ref, fb_ref, c_ref):
        tri = (_iota((BQ, BQ), 0) >= _iota((BQ, BQ), 1)).astype(BF16)

        def blk(i, carry):
            r0 = pl.multiple_of(i * BQ, BQ)
            lf = _log_sigmoid(g_ref[pl.ds(r0, BQ), :] + fb_ref[...])
            c = _dot(tri, _split3(lf))
            c_ref[pl.ds(r0, BQ), :] = c[:, 0:BQ] + c[:, BQ:2 * BQ] + c[:, 2 * BQ:3 * BQ] + carry
            return carry + jnp.sum(lf, axis=0, keepdims=True)

        lax.fori_loop(0, NB, blk, jnp.zeros((1, BQ), F32))

    blk = pl.BlockSpec((S, GATEW), lambda b: (b, 0))
    return pl.pallas_call(
        body, name="fox_prep", grid=(BL,), in_specs=[blk, pl.BlockSpec((1, GATEW), lambda b: (0, 0))],
        out_specs=blk, out_shape=jax.ShapeDtypeStruct((T, GATEW), F32),
        compiler_params=_cp(("parallel",)))(gate, fb)


def _fox_post(dcum, gate, fb):
    def body(dc_ref, g_ref, fb_ref, dg_ref, dfb_ref):
        b = pl.program_id(0)
        tri = (_iota((BQ, BQ), 0) <= _iota((BQ, BQ), 1)).astype(BF16)

        def blk(ii, carry):
            csum, dfb = carry
            r0 = pl.multiple_of((NB - 1 - ii) * BQ, BQ)
            dc = dc_ref[pl.ds(r0, BQ), :]
            c = _dot(tri, _split3(dc))
            dlf = c[:, 0:BQ] + c[:, BQ:2 * BQ] + c[:, 2 * BQ:3 * BQ] + csum
            dg = dlf * jnp.exp(_log_sigmoid(-(g_ref[pl.ds(r0, BQ), :] + fb_ref[...])))
            dg_ref[pl.ds(r0, BQ), :] = dg
            return csum + jnp.sum(dc, axis=0, keepdims=True), dfb + jnp.sum(dg, axis=0, keepdims=True)

        z = jnp.zeros((1, BQ), F32)
        _, dfb = lax.fori_loop(0, NB, blk, (z, z))

        @pl.when(b == 0)
        def _():
            dfb_ref[...] = dfb

        @pl.when(b > 0)
        def _():
            dfb_ref[...] += dfb

    blk = pl.BlockSpec((S, GATEW), lambda b: (b, 0))
    vec = pl.BlockSpec((1, GATEW), lambda b: (0, 0))
    return pl.pallas_call(
        body, name="fox_post", grid=(BL,), in_specs=[blk, blk, vec], out_specs=[blk, vec],
        out_shape=[jax.ShapeDtypeStruct((T, GATEW), F32), jax.ShapeDtypeStruct((1, GATEW), F32)],
        compiler_params=_cp(("arbitrary",)))(dcum, gate, fb)


def _shift_down(x, n):
    return jnp.where(_iota(x.shape, 0) >= n, pltpu.roll(x, n, 0), 0.0)


def _shift_up(x, n):
    return jnp.where(_iota(x.shape, 0) < S - n, pltpu.roll(x, S - n, 0), 0.0)


def _conv_fwd(conv, cw, mixed):
    W = 256

    def body(c_ref, w_ref, _, o_ref):
        u = c_ref[:, W:2 * W] * c_ref[:, 2 * W:3 * W]
        y = w_ref[0:1, :] * _shift_down(u, 2) + w_ref[1:2, :] * _shift_down(u, 1) + w_ref[2:3, :] * u
        o_ref[...] = (c_ref[:, 0:W] * y).astype(BF16)

    return pl.pallas_call(
        body, name="conv_fwd", grid=(BL,),
        in_specs=[pl.BlockSpec((S, CONVW), lambda b: (b, 0)), pl.BlockSpec((8, W), lambda b: (0, 0)), ANY_SPEC],
        out_specs=pl.BlockSpec((S, W), lambda b: (b, 3)),
        out_shape=jax.ShapeDtypeStruct((T, D), BF16), input_output_aliases={2: 0},
        compiler_params=_cp(("parallel",)))(conv, cw, mixed)


def _conv_bwd(conv, cw, dmixed):
    W = 256

    def body(c_ref, w_ref, do_ref, dc_ref, dw_ref):
        b = pl.program_id(0)
        bg = c_ref[:, 0:W]
        cg = c_ref[:, W:2 * W]
        hv = c_ref[:, 2 * W:3 * W]
        do = do_ref[...].astype(F32)
        u = cg * hv
        u1 = _shift_down(u, 1)
        u2 = _shift_down(u, 2)
        y = w_ref[0:1, :] * u2 + w_ref[1:2, :] * u1 + w_ref[2:3, :] * u
        dy = do * bg
        du = w_ref[2:3, :] * dy + w_ref[1:2, :] * _shift_up(dy, 1) + w_ref[0:1, :] * _shift_up(dy, 2)
        dc_ref[:, 0:W] = (do * y).astype(BF16)
        dc_ref[:, W:2 * W] = (du * hv).astype(BF16)
        dc_ref[:, 2 * W:3 * W] = (du * cg).astype(BF16)
        rowi = _iota((8, W), 0)
        dw = (jnp.where(rowi == 0, jnp.sum(dy * u2, axis=0, keepdims=True), 0.0)
              + jnp.where(rowi == 1, jnp.sum(dy * u1, axis=0, keepdims=True), 0.0)
              + jnp.where(rowi == 2, jnp.sum(dy * u, axis=0, keepdims=True), 0.0))

        @pl.when(b == 0)
        def _():
            dw_ref[...] = dw

        @pl.when(b > 0)
        def _():
            dw_ref[...] += dw

    return pl.pallas_call(
        body, name="conv_bwd", grid=(BL,),
        in_specs=[pl.BlockSpec((S, CONVW), lambda b: (b, 0)), pl.BlockSpec((8, W), lambda b: (0, 0)),
                  pl.BlockSpec((S, W), lambda b: (b, 3))],
        out_specs=[pl.BlockSpec((S, CONVW), lambda b: (b, 0)), pl.BlockSpec((8, W), lambda b: (0, 0))],
        out_shape=[jax.ShapeDtypeStruct((T, CONVW), BF16), jax.ShapeDtypeStruct((8, W), F32)],
        compiler_params=_cp(("arbitrary",)))(conv, cw, dmixed)


def _place():
    x, y, c = lax.axis_index("x"), lax.axis_index("y"), lax.axis_index("c")
    return x, y, c


def _chips_of(x, y):
    return [(1 - x, y), (x, 1 - y), (1 - x, 1 - y)]


def _dev(p):
    return 4 * p[0] + 2 * p[1] + p[2]


def _gather_job_a(shards):
    n = len(shards)

    def peers(x, y, c):
        return [(x, y, 1 - c)] + [(*chip, c) for chip in _chips_of(x, y)]

    def start(ins, outs, sems):
        send, recv, loc = sems
        x, y, c = _place()
        me = (x, y, c)
        cps = []
        for a in range(n):
            cps.append(pltpu.make_async_copy(ins[a], outs[a].at[_dev(me)], loc.at[a]))
            for k, peer in enumerate(peers(x, y, c)):
                cps.append(pltpu.make_async_remote_copy(
                    src_ref=ins[a], dst_ref=outs[a].at[_dev(me)], send_sem=send.at[a, k], recv_sem=recv.at[a, k],
                    device_id=peer, device_id_type=MESH))
        for cp in cps:
            cp.start()
        return cps

    def finish(cps, ins, outs, sems):
        send, recv, loc = sems
        x, y, c = _place()
        for a in range(n):
            for k, peer in enumerate(peers(x, y, c)):
                pltpu.make_async_remote_copy(
                    src_ref=ins[a], dst_ref=outs[a].at[_dev(peer)], send_sem=send.at[a, k], recv_sem=recv.at[a, k],
                    device_id=(x, y, c), device_id_type=MESH).wait_recv()
        for a in range(n):
            cps[5 * a].wait()
            for k in range(4):
                cps[5 * a + 1 + k].wait_send()

    return _Job(shards, [jax.ShapeDtypeStruct((NDEV,) + s.shape, s.dtype) for s in shards], {},
                [pltpu.SemaphoreType.DMA((n, 4)), pltpu.SemaphoreType.DMA((n, 4)), pltpu.SemaphoreType.DMA((n,))],
                start, finish)


def _gather_job_b(gathered):
    n = len(gathered)

    def start(ins, outs, sems):
        send, recv = sems
        x, y, c = _place()
        cps = []
        for a in range(n):
            for j, chip in enumerate(_chips_of(x, y)):
                blk = outs[a].at[_dev((*chip, c))]
                cps.append(pltpu.make_async_remote_copy(
                    src_ref=blk, dst_ref=blk, send_sem=send.at[a, j], recv_sem=recv.at[a, j],
                    device_id=(x, y, 1 - c), device_id_type=MESH))
        for cp in cps:
            cp.start()
        return cps

    def finish(cps, ins, outs, sems):
        send, recv = sems
        x, y, c = _place()
        for a in range(n):
            for j, chip in enumerate(_chips_of(x, y)):
                blk = outs[a].at[_dev((*chip, 1 - c))]
                pltpu.make_async_remote_copy(
                    src_ref=blk, dst_ref=blk, send_sem=send.at[a, j], recv_sem=recv.at[a, j],
                    device_id=(x, y, c), device_id_type=MESH).wait_recv()
        for cp in cps:
            cp.wait_send()

    return _Job(gathered, [jax.ShapeDtypeStruct(g.shape, g.dtype) for g in gathered], {a: a for a in range(n)},
                [pltpu.SemaphoreType.DMA((n, 3)), pltpu.SemaphoreType.DMA((n, 3))], start, finish)


def _sibling_job(grads):
    n = len(grads)

    def start(ins, outs, sems):
        send, recv = sems
        x, y, c = _place()
        cps = [pltpu.make_async_remote_copy(
            src_ref=ins[a].at[:, 1 - c], dst_ref=outs[a], send_sem=send.at[a], recv_sem=recv.at[a],
            device_id=(x, y, 1 - c), device_id_type=MESH) for a in range(n)]
        for cp in cps:
            cp.start()
        return cps

    def finish(cps, ins, outs, sems):
        for cp in cps:
            cp.wait()

    return _Job(grads, [jax.ShapeDtypeStruct(g.shape[:1] + g.shape[2:], F32) for g in grads], {},
                [pltpu.SemaphoreType.DMA((n,)), pltpu.SemaphoreType.DMA((n,))], start, finish)


def _chip_job(psums):
    n = len(psums)

    def copies(ins, outs, sems):
        send, recv, loc = sems
        x, y, c = _place()
        mychip = 2 * x + y
        cps = []
        for a in range(n):
            cps.append(pltpu.make_async_copy(ins[a].at[mychip], outs[a].at[mychip], loc.at[a]))
            for j, chip in enumerate(_chips_of(x, y)):
                cps.append(pltpu.make_async_remote_copy(
                    src_ref=ins[a].at[2 * chip[0] + chip[1]], dst_ref=outs[a].at[mychip],
                    send_sem=send.at[a, j], recv_sem=recv.at[a, j], device_id=(*chip, c), device_id_type=MESH))
        return cps

    def start(ins, outs, sems):
        for cp in copies(ins, outs, sems):
            cp.start()

    def finish(_, ins, outs, sems):
        cps = copies(ins, outs, sems)
        send, recv, loc = sems
        x, y, c = _place()
        mychip = 2 * x + y
        for a in range(n):
            for j, chip in enumerate(_chips_of(x, y)):
                pltpu.make_async_remote_copy(
                    src_ref=ins[a].at[mychip], dst_ref=outs[a].at[2 * chip[0] + chip[1]],
                    send_sem=send.at[a, j], recv_sem=recv.at[a, j], device_id=(x, y, c), device_id_type=MESH).wait_recv()
        for a in range(n):
            cps[4 * a].wait()
            for j in range(3):
                cps[4 * a + 1 + j].wait_send()

    return _Job(psums, [jax.ShapeDtypeStruct(p.shape, BF16) for p in psums], {},
                [pltpu.SemaphoreType.DMA((n, 3)), pltpu.SemaphoreType.DMA((n, 3)), pltpu.SemaphoreType.DMA((n,))],
                start, finish)


def _join_jobs(*jobs):
    jobs = [j for j in jobs if j is not None]
    if len(jobs) <= 1:
        return jobs[0] if jobs else None
    cut = lambda seq, sizes: [seq[sum(sizes[:k]):sum(sizes[:k + 1])] for k in range(len(sizes))]
    n_in = [len(j.ins) for j in jobs]
    n_out = [len(j.out_shapes) for j in jobs]
    n_sem = [len(j.sems) for j in jobs]
    aliases = {}
    for k, j in enumerate(jobs):
        for a, b in j.aliases.items():
            aliases[sum(n_in[:k]) + a] = sum(n_out[:k]) + b

    def start(ins, outs, sems):
        return [j.start(i, o, s) for j, i, o, s in zip(jobs, cut(ins, n_in), cut(outs, n_out), cut(sems, n_sem))]

    def finish(sts, ins, outs, sems):
        for j, st, i, o, s in zip(jobs, sts, cut(ins, n_in), cut(outs, n_out), cut(sems, n_sem)):
            j.finish(st, i, o, s)

    return _Job([t for j in jobs for t in j.ins], [t for j in jobs for t in j.out_shapes], aliases,
                [t for j in jobs for t in j.sems], start, finish)


def _run_job(job, name):
    def body(ins, outs, scr, comm):
        comm[1](comm[0]())

    return _host_call(body, name, [], [], [], [], [], {}, job)[1]


def _allreduce_small(v, job=None):
    def body(ins, outs, scr, comm):
        (v_ref,), (o_ref,), (slots, send_sems, recv_sems) = ins, outs, scr
        st = comm[0]() if comm else None
        x, y, c = _place()
        me = 4 * x + 2 * y + c
        slots[me] = v_ref[...]

        def copy(k):
            peer = (x ^ ((k >> 2) & 1), y ^ ((k >> 1) & 1), c ^ (k & 1))
            return pltpu.make_async_remote_copy(
                src_ref=v_ref, dst_ref=slots.at[me], send_sem=send_sems.at[k - 1], recv_sem=recv_sems.at[k - 1],
                device_id=peer, device_id_type=MESH)

        def arrival(k):
            return pltpu.make_async_remote_copy(
                src_ref=v_ref, dst_ref=slots.at[me ^ k], send_sem=send_sems.at[k - 1], recv_sem=recv_sems.at[k - 1],
                device_id=(x, y, c), device_id_type=MESH)

        sends = [copy(k) for k in range(1, NDEV)]
        for cp in sends:
            cp.start()
        for k in range(1, NDEV):
            arrival(k).wait_recv()
        for cp in sends:
            cp.wait_send()
        acc = slots[0]
        for d in range(1, NDEV):
            acc = acc + slots[d]
        o_ref[...] = acc
        if comm:
            comm[1](st)

    (out,), extra = _host_call(
        body, "allreduce_small", [v], [VMEM_SPEC], [jax.ShapeDtypeStruct(v.shape, F32)], [VMEM_SPEC],
        [pltpu.VMEM((NDEV,) + v.shape, F32), pltpu.SemaphoreType.DMA((NDEV - 1,)),
         pltpu.SemaphoreType.DMA((NDEV - 1,))], {}, job)
    return (out, extra) if job else out


def _pair_sums(views, gots, core):
    n = len(views)

    def body(c_ref, *refs):
        for a in range(n):
            refs[2 * n + a][...] = (refs[a][...] + refs[n + a][...]).astype(BF16)

    def vspec(v):
        return pl.BlockSpec((None, None, v.shape[2] // 2, v.shape[3]), lambda k, h, c: (k, c[0], h, 0))

    def gspec(g):
        return pl.BlockSpec((None, g.shape[1] // 2, g.shape[2]), lambda k, h, c: (k, h, 0))

    return pl.pallas_call(
        body, name="pair_sums",
        grid_spec=pltpu.PrefetchScalarGridSpec(
            num_scalar_prefetch=1, grid=(4, 2),
            in_specs=[vspec(v) for v in views] + [gspec(g) for g in gots],
            out_specs=[gspec(g) for g in gots]),
        out_shape=[jax.ShapeDtypeStruct(g.shape, BF16) for g in gots],
        compiler_params=_cp(("parallel", "parallel")))(core, *views, *gots)


def _chip_sums(parts):
    n = len(parts)

    def body(*refs):
        for a in range(n):
            acc = refs[a][0].astype(F32)
            for k in range(1, 4):
                acc = acc + refs[a][k].astype(F32)
            refs[n + a][...] = acc

    return pl.pallas_call(
        body, name="chip_sums", in_specs=[VMEM_SPEC] * n, out_specs=[VMEM_SPEC] * n,
        out_shape=[jax.ShapeDtypeStruct(p.shape[1:], F32) for p in parts], compiler_params=_cp())(*parts)


def _permute_in(w):
    lead = w.shape[:-1]
    return w.reshape(lead + (3, 3, 2, BQ)).swapaxes(-2, -3).reshape(lead + (QKVW,))


def _unpermute_in(w):
    lead = w.shape[:-1]
    return w.reshape(lead + (3, 2, 3, BQ)).swapaxes(-2, -3).reshape(lead + (QKVW,))


def _row(v):
    v = v.reshape(-1)
    return jnp.pad(v, (0, D - v.shape[0])).reshape(1, D)


def kernel(x, w_in, f_bias, conv_w, w_out, rel_bias, ln1_g, ln1_b, w_gate, w_up, w_down, ln2_g, ln2_b, loss_target, m_w_in, m_f_bias, m_conv_w, m_w_out, m_rel_bias, m_ln1_g, m_ln1_b, m_w_gate, m_w_up, m_w_down, m_ln2_g, m_ln2_b, v_w_in, v_f_bias, v_conv_w, v_w_out, v_rel_bias, v_ln1_g, v_ln1_b, v_w_gate, v_w_up, v_w_down, v_ln2_g, v_ln2_b):
    xi, yi, ci = _place()
    me = 4 * xi + 2 * yi + ci
    core = jnp.reshape(ci, (1,)).astype(jnp.int32)

    win_s = jnp.concatenate([_permute_in(w_in[..., :QKVW]), w_in[..., QKVW:]], axis=-1)
    win_s = jnp.pad(win_s, ((0, 0), (0, 0), (0, NPAD - NPROJ))).astype(BF16)
    per_layer = [win_s, w_out.astype(BF16), jnp.swapaxes(w_gate, 1, 2).astype(BF16),
                 jnp.swapaxes(w_up, 1, 2).astype(BF16), w_down.astype(BF16)]
    sh = [[s[l] for s in per_layer] for l in range(2)]

    def whole(g):
        return g.reshape(NDEV * g.shape[1], g.shape[2])

    cw_rows = lax.dynamic_update_slice(jnp.zeros((2, 3, 256), F32), conv_w, (0, 0, me * 32))
    small = jnp.concatenate([_row(cw_rows[0]), _row(cw_rows[1]), jnp.zeros((SMALL_ROWS - 2, D), F32)], axis=0)
    small, leg_a = _allreduce_small(small, job=_gather_job_a(sh[0][:1]))
    cw_full = small[0:2, :CONVW].reshape(2, 3, 256)
    cw8 = jnp.pad(cw_full, ((0, 0), (0, 5), (0, 0)))
    fb = jnp.pad(f_bias, ((0, 0), (0, GATEW - NH))).reshape(2, 1, GATEW)
    tbl, leg_b = _dil_table(rel_bias, job=_gather_job_b(list(leg_a)))
    W = [{"win": whole(leg_b[0])}, {}]

    def wrow(tn, K, blk=0):
        return pl.BlockSpec((tn, K), lambda i, j: (j, blk))

    def arow(tm, K, blk=0):
        return pl.BlockSpec((tm, K), lambda i, j: (i, blk))

    h = x.reshape(T, D)
    hb = h.astype(BF16)
    saved = []
    for l in range(2):
        Win = W[l]["win"]
        qkv, conv, gate = _proj(hb, Win)
        cum = _fox_prep(gate, fb[l])
        cq = cum[:, :NH].reshape(BL, S, NH).transpose(0, 2, 1).reshape(NSTAT, S)
        ckb = jnp.broadcast_to(cq[:, :, None], (NSTAT, S, BQ))
        if l == 0:
            mixed, amat, a0 = _sb_fwd(qkv, job=_gather_job_a(sh[0][1:]))
            mixed, lse_d, ex = _flash_fwd(qkv, mixed, 1, False, (tbl,),
                                          job=_join_jobs(_gather_job_b(list(a0)), _gather_job_a(sh[1][:2])))
            W[0].update(zip(("wout", "wgT", "wuT", "wd"), [whole(t) for t in ex[:4]]))
            mixed, lse_f, o_fox, ex = _flash_fwd(qkv, mixed, 2, True, (cq, ckb),
                                                 job=_join_jobs(_gather_job_b(list(ex[4:])), _gather_job_a(sh[1][2:])))
            W[1].update(zip(("win", "wout"), [whole(t) for t in ex[:2]]))
            a2 = list(ex[2:])
        else:
            mixed, amat, ex = _sb_fwd(qkv, job=_gather_job_b(a2))
            W[1].update(zip(("wgT", "wuT", "wd"), [whole(t) for t in ex]))
            mixed, lse_d, _ = _flash_fwd(qkv, mixed, 1, False, (tbl,))
            mixed, lse_f, o_fox, _ = _flash_fwd(qkv, mixed, 2, True, (cq, ckb))
        Wout, WgT, WuT, Wd = W[l]["wout"], W[l]["wgT"], W[l]["wuT"], W[l]["wd"]
        mixed = _conv_fwd(conv, cw8[l], mixed)
        x1, xh1, r1, x1b = _mm_ln(mixed, Wout, h, ln1_g[l:l + 1], ln1_b[l:l + 1], "out_proj_ln")
        fs, ft, a, x2, xh2, r2, x2b = _ffn_fwd(x1b, x1, WgT, WuT, Wd, ln2_g[l:l + 1], ln2_b[l:l + 1])
        saved.append(dict(h=hb, qkv=qkv, conv=conv, gate=gate, cq=cq, ckb=ckb, mixed=mixed, amat=amat, lse_d=lse_d,
                          lse_f=lse_f, o_fox=o_fox, x1=x1b, xh1=xh1, r1=r1, fs=fs, ft=ft, a=a, xh2=xh2, r2=r2))
        h, hb = x2, x2b

    dy = h

    def view(gr):
        return gr.reshape(4, 2, gr.shape[0] // NDEV, gr.shape[1])

    G = [None, None]
    small_g = {}
    shard_g = {}
    for l in (1, 0):
        sv = saved[l]
        Win, Wout, WgT, WuT, Wd = W[l]["win"], W[l]["wout"], W[l]["wgT"], W[l]["wuT"], W[l]["wd"]
        res = _ffn_bwd(dy, sv["xh2"], sv["r2"], ln2_g[l:l + 1], sv["fs"], sv["ft"], Wd, WgT, WuT,
                       target=loss_target.reshape(T, D) if l == 1 else None)
        dgt, dut, ds2b, dx1, dg2, db2 = res[:6]
        if l == 1:
            sq = res[6]
        G_d = _mm_tn(sv["a"], ds2b, None, C=D, Ka=DFF, N=D, tm=256, tn=1024, tk=T, ooff=0, name="grad_w_down")
        G_g = _mm_tn(dgt, sv["x1"], None, C=D, Ka=DFF, N=D, tm=256, tn=1024, tk=T, ooff=0, name="grad_w_gate")
        G_u = _mm_tn(dut, sv["x1"], None, C=D, Ka=DFF, N=D, tm=256, tn=1024, tk=T, ooff=0, name="grad_w_up")
        ds1, dg1, db1, ds1b, dmixed = _ln_bwd(dx1, sv["xh1"], sv["r1"], ln1_g[l:l + 1], Wout)
        G_out = _mm_tn(sv["mixed"], ds1b, None, C=D, Ka=D, N=D, tm=256, tn=1024, tk=T, ooff=0, name="grad_w_out")
        early = [view(t) for t in (G_g, G_u, G_d, G_out)] + ([view(G[1]["in"])] if l == 0 else [])
        dqkv, gots = _sb_bwd(sv["qkv"], dmixed, sv["amat"], job=_sibling_job(early))
        ps = _pair_sums(early, list(gots), core)
        dqkv, dtbl, pa = _flash_bwd(sv["qkv"], sv["mixed"], dmixed, sv["lse_d"], dqkv, 1, False, (tbl,),
                                    job=_chip_job(ps[:2]))
        dqkv, dck, pb = _flash_bwd(sv["qkv"], sv["o_fox"], dmixed, sv["lse_f"], dqkv, 2, True,
                                   (sv["cq"], sv["ckb"]), job=_chip_job(ps[2:]))
        sums = _chip_sums(list(pa) + list(pb))
        shard_g[l] = dict(zip(("g", "u", "d", "out"), sums[:4]))
        if l == 0:
            shard_g[1]["in"] = sums[4]
        dconv, dcw = _conv_bwd(sv["conv"], cw8[l], dmixed)
        dcum = jnp.pad(dck.reshape(S, BL, NH).transpose(1, 0, 2).reshape(T, NH), ((0, 0), (0, GATEW - NH)))
        dgate, dfb = _fox_post(dcum, sv["gate"], fb[l])
        drb = _dil_table_bwd(dtbl)
        G_in = _mm_tn(sv["h"], dqkv, None, C=NPAD, Ka=D, N=QKVW, tm=512, tn=768, tk=T, ooff=0, name="grad_w_in_qkv")
        G_in = _mm_tn(sv["h"], dconv, G_in, C=NPAD, Ka=D, N=CONVW, tm=256, tn=768, tk=T, ooff=3,
                      name="grad_w_in_conv")
        G_in = _mm_tn(sv["h"], dgate, G_in, C=NPAD, Ka=D, N=GATEW, tm=1024, tn=128, tk=1024, ooff=24,
                      name="grad_w_in_gate")
        G[l] = {"in": G_in, "out": G_out, "g": G_g, "u": G_u, "d": G_d}
        if l == 0:
            late = [view(G_in)]
            tail = _chip_job(_pair_sums(late, list(_run_job(_sibling_job(late), "sibling_exchange")), core))
            dy, parts = _mm([(dqkv, arow(1024, QKVW), Win, wrow(512, QKVW, 0)),
                             (dconv, arow(1024, CONVW), Win, wrow(512, CONVW, 3)),
                             (dgate, arow(1024, GATEW), Win, wrow(512, GATEW, 24))],
                            nt=True, M=T, N=D, tm=1024, tn=512, out_dtype=F32, name="proj_dx", res=ds1,
                            res_scale=ALPHA, job=tail)
            shard_g[0]["in"] = _chip_sums(list(parts))[0]
        else:
            dy = _proj_bwd(dqkv, dconv, dgate, Win, ds1)
        small_g[l] = dict(ln1_g=dg1, ln1_b=db1, ln2_g=dg2, ln2_b=db2, cw=dcw[0:3].reshape(1, CONVW),
                          fb=dfb[:, :NH], rb=drb[:, :NH])
    grad_x = dy.reshape(BL, S, D)

    rows = []
    for name in ("ln1_g", "ln1_b", "ln2_g", "ln2_b"):
        rows += [small_g[0][name], small_g[1][name]]
    rows += [_row(small_g[0]["cw"]), _row(small_g[1]["cw"]),
             _row(jnp.concatenate([small_g[0]["fb"], small_g[1]["fb"]], axis=0)),
             _row(small_g[0]["rb"] + small_g[1]["rb"]), _row(sq)]
    rows.append(jnp.zeros((SMALL_ROWS - len(rows), D), F32))
    sg = _allreduce_small(jnp.concatenate(rows, axis=0))
    loss = sg[12, 0] * (0.5 / D)
    g_ln1_g, g_ln1_b, g_ln2_g, g_ln2_b = sg[0:2], sg[2:4], sg[4:6], sg[6:8]
    g_conv_full = sg[8:10, :CONVW].reshape(2, 3, 256)
    g_conv = lax.dynamic_slice(g_conv_full, (0, 0, me * 32), (2, 3, 32))
    g_fb = sg[10, :2 * NH].reshape(2, NH)
    g_rb = sg[11, :32 * NH].reshape(32, NH)

    def both(name):
        return jnp.stack([shard_g[0][name], shard_g[1][name]])

    g_in = both("in")
    g_w_in = jnp.concatenate([_unpermute_in(g_in[..., :QKVW]), g_in[..., QKVW:NPROJ]], axis=-1)
    g_w_out = both("out")
    g_w_gate = jnp.swapaxes(both("g"), 1, 2)
    g_w_up = jnp.swapaxes(both("u"), 1, 2)
    g_w_down = both("d")

    up_in = _adamw(w_in, g_w_in, m_w_in, v_w_in, 64)
    up_out = _adamw(w_out, g_w_out, m_w_out, v_w_out, 128)
    up_gate = _adamw(w_gate, g_w_gate, m_w_gate, v_w_gate, 256)
    up_up = _adamw(w_up, g_w_up, m_w_up, v_w_up, 256)
    up_down = _adamw(w_down, g_w_down, m_w_down, v_w_down, 352)

    def pack(fbv, cwv, rbv, l1g, l1b, l2g, l2b):
        r = [l1g, l1b, l2g, l2b, _row(cwv), _row(fbv), _row(rbv)]
        r.append(jnp.zeros((SMALL_ROWS - 11, D), F32))
        return jnp.concatenate(r, axis=0)

    pw = pack(f_bias, conv_w, rel_bias, ln1_g, ln1_b, ln2_g, ln2_b)
    pg = pack(g_fb, g_conv, g_rb, g_ln1_g, g_ln1_b, g_ln2_g, g_ln2_b)
    pm = pack(m_f_bias, m_conv_w, m_rel_bias, m_ln1_g, m_ln1_b, m_ln2_g, m_ln2_b)
    pv = pack(v_f_bias, v_conv_w, v_rel_bias, v_ln1_g, v_ln1_b, v_ln2_g, v_ln2_b)
    ups = [u[0] for u in _adamw(pw[None], pg[None], pm[None], pv[None], SMALL_ROWS)]

    def unpack(p):
        return dict(ln1_g=p[0:2], ln1_b=p[2:4], ln2_g=p[4:6], ln2_b=p[6:8],
                    conv_w=p[8, :192].reshape(2, 3, 32), f_bias=p[9, :2 * NH].reshape(2, NH),
                    rel_bias=p[10, :32 * NH].reshape(32, NH))

    sm = [unpack(p) for p in ups]

    def group(k):
        return (up_in[k], sm[k]["f_bias"], sm[k]["conv_w"], up_out[k], sm[k]["rel_bias"], sm[k]["ln1_g"],
                sm[k]["ln1_b"], up_gate[k], up_up[k], up_down[k], sm[k]["ln2_g"], sm[k]["ln2_b"])

    grads = (g_w_in, g_fb, g_conv, g_w_out, g_rb, g_ln1_g, g_ln1_b, g_w_gate, g_w_up, g_w_down, g_ln2_g, g_ln2_b)
    return (loss, grad_x) + grads + group(0) + group(1) + group(2)
```

```python
import math

import numpy as np
import jax
import jax.numpy as jnp
from jax import lax
from jax.experimental import pallas as pl
from jax.experimental.pallas import tpu as pltpu

F32 = jnp.float32
BF16 = jnp.bfloat16
MESH = pl.DeviceIdType.MESH

D = 1024
S = 2048
BL = 2
T = BL * S
NH = 4
DFF = 2816
NPROJ = 3076
NPAD = 3200
QKVW = 2304
CONVW = 768
GATEW = 128
PAIRW = 384
BQ = 128
HB = 2 * BQ
NB = S // BQ
NDEV = 8
NSTAT = BL * NH
ALPHA = 4.0 ** 0.25
SCALE = 0.125
NEG = -1e30
LN_EPS = 1e-5
ADAM_LR, ADAM_B1, ADAM_B2, ADAM_EPS, ADAM_WD, ADAM_STEP = 0.001, 0.9, 0.999, 1e-08, 0.01, 10
VMEM_LIMIT = 56 * 1024 * 1024
SMALL_ROWS = 16


def _bucket_thresholds():
    d = np.arange(0, S)
    nf = np.maximum(d, 1).astype(np.float32)
    large = 16 + (np.log(nf / np.float32(16)) / np.float32(math.log(128)) * np.float32(16)).astype(np.int32)
    b = np.where(d < 16, d, np.minimum(large, 31))
    return [int(np.argmax(b >= k)) for k in range(32)]


BUCKET_TH = _bucket_thresholds()


def _cp(sem=None, vmem=VMEM_LIMIT):
    return pltpu.CompilerParams(dimension_semantics=sem, vmem_limit_bytes=vmem)


def _dot(a, b):
    return lax.dot_general(a, b, (((1,), (0,)), ((), ())), preferred_element_type=F32)


def _dot_nt(a, b):
    return lax.dot_general(a, b, (((1,), (1,)), ((), ())), preferred_element_type=F32)


def _dot_tn(a, b):
    return lax.dot_general(a, b, (((0,), (0,)), ((), ())), preferred_element_type=F32)


def _split2(x):
    hi = x.astype(BF16)
    mid = (x - hi.astype(F32)).astype(BF16)
    return jnp.concatenate([hi, mid], axis=1)


def _split3(x):
    hi = x.astype(BF16)
    r = x - hi.astype(F32)
    mid = r.astype(BF16)
    lo = (r - mid.astype(F32)).astype(BF16)
    return jnp.concatenate([hi, mid, lo], axis=1)


def _log_sigmoid(u):
    return jnp.minimum(u, 0.0) - jnp.log1p(jnp.exp(-jnp.abs(u)))


def _log_sigmoid_tile(u):
    return jnp.minimum(u, 0.0) - jnp.log(1.0 + jnp.exp(jnp.minimum(u, -u)))


def _iota(shape, dim):
    return lax.broadcasted_iota(jnp.int32, shape, dim)


ANY_SPEC = pl.BlockSpec(memory_space=pl.ANY)
VMEM_SPEC = pl.BlockSpec(memory_space=pltpu.VMEM)


def _mm(pairs, *, nt, M, N, tm, tn, out_dtype, name, res=None, res_scale=1.0, job=None):
    n = len(pairs)
    n_in = 2 * n + (res is not None)
    jins = job.ins if job else []
    jouts = job.out_shapes if job else []
    gi, gj = M // tm, N // tn

    def body(*refs):
        o_ref = refs[n_in + len(jins)]
        if job:
            jrefs = (refs[n_in:n_in + len(jins)], refs[n_in + len(jins) + 1:n_in + len(jins) + 1 + len(jouts)],
                     refs[n_in + len(jins) + 1 + len(jouts):])

            @pl.when((pl.program_id(0) == 0) & (pl.program_id(1) == 0))
            def _():
                job.start(*jrefs)

        acc = None
        for p in range(n):
            a = refs[2 * p][...].astype(BF16)
            b = refs[2 * p + 1][...]
            d = _dot_nt(a, b) if nt else _dot(a, b)
            acc = d if acc is None else acc + d
        if res is not None:
            acc = acc + res_scale * refs[2 * n][...]
        o_ref[...] = acc.astype(out_dtype)
        if job:
            @pl.when((pl.program_id(0) == gi - 1) & (pl.program_id(1) == gj - 1))
            def _():
                job.finish(None, *jrefs)

    ops, specs = [], []
    for a, asp, b, bsp in pairs:
        ops += [a, b]
        specs += [asp, bsp]
    if res is not None:
        ops.append(res)
        specs.append(pl.BlockSpec((tm, tn), lambda i, j: (i, j)))
    out = pl.pallas_call(
        body, name=name, grid=(gi, gj), in_specs=specs + [ANY_SPEC] * len(jins),
        out_specs=[pl.BlockSpec((tm, tn), lambda i, j: (i, j))] + [ANY_SPEC] * len(jouts),
        out_shape=[jax.ShapeDtypeStruct((M, N), out_dtype)] + list(jouts),
        scratch_shapes=list(job.sems) if job else [],
        input_output_aliases={n_in + a: 1 + b for a, b in job.aliases.items()} if job else {},
        compiler_params=_cp(("arbitrary", "arbitrary") if job else ("parallel", "parallel")))(*ops, *jins)
    return (out[0], out[1:]) if job else out[0]


def _mm_tn(a, b, gbuf, *, C, Ka, N, tm, tn, tk, ooff, name):
    def body(*refs):
        a_ref, b_ref, o_ref = refs[0], refs[1], refs[-1]
        k = pl.program_id(2)
        d = _dot_tn(a_ref[...].astype(BF16), b_ref[...].astype(BF16))

        @pl.when(k == 0)
        def _():
            o_ref[...] = d

        @pl.when(k > 0)
        def _():
            o_ref[...] += d

    ops = [a, b] + ([] if gbuf is None else [gbuf])
    return pl.pallas_call(
        body, name=name, grid=(Ka // tm, N // tn, T // tk),
        in_specs=[pl.BlockSpec((tk, tm), lambda i, j, k: (k, i)),
                  pl.BlockSpec((tk, tn), lambda i, j, k: (k, j))] + ([] if gbuf is None else [ANY_SPEC]),
        out_specs=pl.BlockSpec((tm, tn), lambda i, j, k: (i, ooff + j)),
        out_shape=jax.ShapeDtypeStruct((Ka, C), F32),
        input_output_aliases={} if gbuf is None else {2: 0},
        compiler_params=_cp(("parallel", "parallel", "arbitrary")))(*ops)


def _proj(xb, w):
    tm = 512

    def body(x_ref, w_ref, qkv_ref, conv_ref, gate_ref):
        xv = x_ref[...]
        qkv_ref[...] = _dot(xv, w_ref[:, 0:QKVW]).astype(BF16)
        conv_ref[...] = _dot(xv, w_ref[:, QKVW:QKVW + CONVW])
        gate_ref[...] = _dot(xv, w_ref[:, QKVW + CONVW:NPAD])

    def rows(n):
        return pl.BlockSpec((tm, n), lambda i: (i, 0))

    return pl.pallas_call(
        body, name="proj", grid=(T // tm,),
        in_specs=[rows(D), pl.BlockSpec((D, NPAD), lambda i: (0, 0))],
        out_specs=[rows(QKVW), rows(CONVW), rows(GATEW)],
        out_shape=[jax.ShapeDtypeStruct((T, QKVW), BF16), jax.ShapeDtypeStruct((T, CONVW), F32),
                   jax.ShapeDtypeStruct((T, GATEW), F32)],
        compiler_params=_cp(("parallel",)))(xb, w)


def _proj_bwd(dqkv, dconv, dgate, w, res):
    tm = 512

    def body(a_ref, b_ref, c_ref, w_ref, r_ref, o_ref):
        acc = ALPHA * r_ref[...] + _dot_nt(a_ref[...], w_ref[:, 0:QKVW])
        acc = acc + _dot_nt(b_ref[...], w_ref[:, QKVW:QKVW + CONVW])
        o_ref[...] = acc + _dot_nt(c_ref[...].astype(BF16), w_ref[:, QKVW + CONVW:NPAD])

    def rows(n):
        return pl.BlockSpec((tm, n), lambda i: (i, 0))

    return pl.pallas_call(
        body, name="proj_bwd", grid=(T // tm,),
        in_specs=[rows(QKVW), rows(CONVW), rows(GATEW), pl.BlockSpec((D, NPAD), lambda i: (0, 0)), rows(D)],
        out_specs=rows(D), out_shape=jax.ShapeDtypeStruct((T, D), F32),
        compiler_params=_cp(("parallel",)))(dqkv, dconv, dgate, w, res)


def _ffn_fwd(xb, x, wgt, wut, wd, gam, bet):
    tm, ch = 512, 256

    def body(xb_ref, x_ref, g_ref, b_ref, wg_hbm, wu_hbm, wd_hbm,
             go_ref, uo_ref, ao_ref, y_ref, xh_ref, r_ref, yb_ref, wg_v, wu_v, wd_v, sem):
        loads = [pltpu.make_async_copy(s, d, sem.at[k])
                 for k, (s, d) in enumerate(((wg_hbm, wg_v), (wu_hbm, wu_v), (wd_hbm, wd_v)))]

        @pl.when(pl.program_id(0) == 0)
        def _():
            for cp in loads:
                cp.start()
            loads[0].wait()
            loads[1].wait()

        xv = xb_ref[...]
        for c in range(0, DFF, ch):
            gv = _dot_nt(xv, wg_v[c:c + ch, :])
            uv = _dot_nt(xv, wu_v[c:c + ch, :])
            go_ref[:, c:c + ch] = gv.astype(BF16)
            uo_ref[:, c:c + ch] = uv.astype(BF16)
            ao_ref[:, c:c + ch] = (gv * jax.nn.sigmoid(gv) * uv).astype(BF16)
        @pl.when(pl.program_id(0) == 0)
        def _():
            loads[2].wait()

        s = ALPHA * x_ref[...] + _dot(ao_ref[...], wd_v[...])
        mu = jnp.mean(s, axis=-1, keepdims=True)
        xc = s - mu
        var = jnp.mean(xc * xc, axis=-1, keepdims=True)
        r = lax.rsqrt(var + LN_EPS)
        xh = xc * r
        xh_ref[...] = xh.astype(BF16)
        r_ref[...] = r
        y = xh * g_ref[...] + b_ref[...]
        y_ref[...] = y
        yb_ref[...] = y.astype(BF16)

    row = pl.BlockSpec((tm, D), lambda i: (i, 0))
    wide = pl.BlockSpec((tm, DFF), lambda i: (i, 0))
    vec = pl.BlockSpec((1, D), lambda i: (0, 0))
    wsl = pltpu.VMEM((DFF, D), BF16)
    hid = jax.ShapeDtypeStruct((T, DFF), BF16)
    return pl.pallas_call(
        body, name="ffn_fwd", grid=(T // tm,),
        in_specs=[row, row, vec, vec, ANY_SPEC, ANY_SPEC, ANY_SPEC],
        out_specs=[wide, wide, wide, row, row, pl.BlockSpec((tm, 1), lambda i: (i, 0)), row],
        out_shape=[hid, hid, hid, jax.ShapeDtypeStruct((T, D), F32), jax.ShapeDtypeStruct((T, D), BF16),
                   jax.ShapeDtypeStruct((T, 1), F32), jax.ShapeDtypeStruct((T, D), BF16)],
        scratch_shapes=[wsl, wsl, wsl, pltpu.SemaphoreType.DMA((3,))],
        compiler_params=_cp(("arbitrary",)))(xb, x, gam, bet, wgt, wut, wd)


def _ffn_bwd(dy, xh, r, gam, g, u, wd, wgt, wut, target=None):
    tm, ch = 256, 256

    def body(*refs):
        if target is None:
            (dy_ref, xh_ref, r_ref, gam_ref, g_ref, u_ref, wd_hbm, wg_hbm, wu_hbm,
             dg_ref, du_ref, dsb_ref, dx_ref, dgam_ref, dbet_ref, wd_v, wg_v, wu_v, sem) = refs
        else:
            (dy_ref, t_ref, xh_ref, r_ref, gam_ref, g_ref, u_ref, wd_hbm, wg_hbm, wu_hbm,
             dg_ref, du_ref, dsb_ref, dx_ref, dgam_ref, dbet_ref, sq_ref, wd_v, wg_v, wu_v, sem) = refs
        loads = [pltpu.make_async_copy(s, d, sem.at[k])
                 for k, (s, d) in enumerate(((wd_hbm, wd_v), (wg_hbm, wg_v), (wu_hbm, wu_v)))]

        @pl.when(pl.program_id(0) == 0)
        def _():
            for cp in loads:
                cp.start()
            loads[0].wait()

        if target is None:
            dyv = dy_ref[...]
        else:
            e = dy_ref[...] - t_ref[...]
            dyv = e * (1.0 / D)
            p = jnp.sum(jnp.sum(e * e, axis=1, keepdims=True), axis=0, keepdims=True)

            @pl.when(pl.program_id(0) == 0)
            def _():
                sq_ref[...] = p

            @pl.when(pl.program_id(0) > 0)
            def _():
                sq_ref[...] += p

        xhv = xh_ref[...].astype(F32)
        dxh = dyv * gam_ref[...]
        m1 = jnp.mean(dxh, axis=-1, keepdims=True)
        m2 = jnp.mean(dxh * xhv, axis=-1, keepdims=True)
        ds = r_ref[...] * (dxh - m1 - xhv * m2)
        pg = jnp.sum(dyv * xhv, axis=0, keepdims=True)
        pb = jnp.sum(dyv, axis=0, keepdims=True)

        @pl.when(pl.program_id(0) == 0)
        def _():
            dgam_ref[...] = pg
            dbet_ref[...] = pb

        @pl.when(pl.program_id(0) > 0)
        def _():
            dgam_ref[...] += pg
            dbet_ref[...] += pb

        db = ds.astype(BF16)
        dsb_ref[...] = db
        for c in range(0, DFF, ch):
            da = _dot_nt(db, wd_v[c:c + ch, :])
            gv = g_ref[:, c:c + ch].astype(F32)
            sg = jax.nn.sigmoid(gv)
            dg_ref[:, c:c + ch] = (da * u_ref[:, c:c + ch].astype(F32) * (sg * (1.0 + gv * (1.0 - sg)))).astype(BF16)
            du_ref[:, c:c + ch] = (da * (gv * sg)).astype(BF16)
        @pl.when(pl.program_id(0) == 0)
        def _():
            loads[1].wait()
            loads[2].wait()

        dx_ref[...] = ALPHA * ds + _dot(dg_ref[...], wg_v[...]) + _dot(du_ref[...], wu_v[...])

    row = pl.BlockSpec((tm, D), lambda i: (i, 0))
    wide = pl.BlockSpec((tm, DFF), lambda i: (i, 0))
    vec = pl.BlockSpec((1, D), lambda i: (0, 0))
    wsl = pltpu.VMEM((DFF, D), BF16)
    last = target is not None
    return pl.pallas_call(
        body, name="ffn_bwd_loss" if last else "ffn_bwd", grid=(T // tm,),
        in_specs=[row] + ([row] if last else [])
        + [row, pl.BlockSpec((tm, 1), lambda i: (i, 0)), vec, wide, wide, ANY_SPEC, ANY_SPEC, ANY_SPEC],
        out_specs=[wide, wide, row, row, vec, vec] + ([pl.BlockSpec((1, 1), lambda i: (0, 0))] if last else []),
        out_shape=[jax.ShapeDtypeStruct((T, DFF), BF16), jax.ShapeDtypeStruct((T, DFF), BF16),
                   jax.ShapeDtypeStruct((T, D), BF16), jax.ShapeDtypeStruct((T, D), F32),
                   jax.ShapeDtypeStruct((1, D), F32), jax.ShapeDtypeStruct((1, D), F32)]
        + ([jax.ShapeDtypeStruct((1, 1), F32)] if last else []),
        scratch_shapes=[wsl, wsl, wsl, pltpu.SemaphoreType.DMA((3,))],
        compiler_params=_cp(("arbitrary",)))(dy, *([target] if last else []), xh, r, gam, g, u, wd, wgt, wut)


def _mm_ln(a, w, x, gam, bet, name):
    tm = 256
    K = a.shape[1]

    def body(a_ref, w_ref, x_ref, g_ref, b_ref, y_ref, xh_ref, r_ref, yb_ref):
        s = ALPHA * x_ref[...] + _dot(a_ref[...], w_ref[...])
        mu = jnp.mean(s, axis=-1, keepdims=True)
        xc = s - mu
        var = jnp.mean(xc * xc, axis=-1, keepdims=True)
        r = lax.rsqrt(var + LN_EPS)
        xh = xc * r
        xh_ref[...] = xh.astype(BF16)
        r_ref[...] = r
        y = xh * g_ref[...] + b_ref[...]
        y_ref[...] = y
        yb_ref[...] = y.astype(BF16)

    row = pl.BlockSpec((tm, D), lambda i: (i, 0))
    vec = pl.BlockSpec((1, D), lambda i: (0, 0))
    return pl.pallas_call(
        body, name=name, grid=(T // tm,),
        in_specs=[pl.BlockSpec((tm, K), lambda i: (i, 0)), pl.BlockSpec((K, D), lambda i: (0, 0)), row, vec, vec],
        out_specs=[row, row, pl.BlockSpec((tm, 1), lambda i: (i, 0)), row],
        out_shape=[jax.ShapeDtypeStruct((T, D), F32), jax.ShapeDtypeStruct((T, D), BF16),
                   jax.ShapeDtypeStruct((T, 1), F32), jax.ShapeDtypeStruct((T, D), BF16)],
        compiler_params=_cp(("parallel",)))(a, w, x, gam, bet)


def _ln_bwd(dy, xh, r, gam, w):
    tm = 256

    def body(dy_ref, xh_ref, r_ref, g_ref, w_ref, ds_ref, dg_ref, db_ref, dsb_ref, dm_ref):
        i = pl.program_id(0)
        dyv = dy_ref[...]
        xhv = xh_ref[...].astype(F32)
        dxh = dyv * g_ref[...]
        m1 = jnp.mean(dxh, axis=-1, keepdims=True)
        m2 = jnp.mean(dxh * xhv, axis=-1, keepdims=True)
        ds = r_ref[...] * (dxh - m1 - xhv * m2)
        ds_ref[...] = ds
        dsb = ds.astype(BF16)
        dsb_ref[...] = dsb
        dm_ref[...] = _dot_nt(dsb, w_ref[...]).astype(BF16)
        pg = jnp.sum(dyv * xhv, axis=0, keepdims=True)
        pb = jnp.sum(dyv, axis=0, keepdims=True)

        @pl.when(i == 0)
        def _():
            dg_ref[...] = pg
            db_ref[...] = pb

        @pl.when(i > 0)
        def _():
            dg_ref[...] += pg
            db_ref[...] += pb

    row = pl.BlockSpec((tm, D), lambda i: (i, 0))
    vec = pl.BlockSpec((1, D), lambda i: (0, 0))
    return pl.pallas_call(
        body, name="ln_bwd_proj", grid=(T // tm,),
        in_specs=[row, row, pl.BlockSpec((tm, 1), lambda i: (i, 0)), vec, pl.BlockSpec((D, D), lambda i: (0, 0))],
        out_specs=[row, vec, vec, row, row],
        out_shape=[jax.ShapeDtypeStruct((T, D), F32), jax.ShapeDtypeStruct((1, D), F32),
                   jax.ShapeDtypeStruct((1, D), F32), jax.ShapeDtypeStruct((T, D), BF16),
                   jax.ShapeDtypeStruct((T, D), BF16)],
        compiler_params=_cp(("arbitrary",)))(dy, xh, r, gam, w)


def _adamw(w, g, m, v, tr):
    L, R, C = w.shape

    def body(w_ref, g_ref, m_ref, v_ref, d_ref, m2_ref, v2_ref):
        gv = g_ref[...]
        m2 = ADAM_B1 * m_ref[...] + (1.0 - ADAM_B1) * gv
        v2 = ADAM_B2 * v_ref[...] + (1.0 - ADAM_B2) * (gv * gv)
        m_hat = m2 / (1.0 - ADAM_B1 ** ADAM_STEP)
        v_hat = v2 / (1.0 - ADAM_B2 ** ADAM_STEP)
        d_ref[...] = -ADAM_LR * (m_hat / (jnp.sqrt(v_hat) + ADAM_EPS) + ADAM_WD * w_ref[...])
        m2_ref[...] = m2
        v2_ref[...] = v2

    blk = pl.BlockSpec((None, tr, C), lambda l, i: (l, i, 0))
    sh = jax.ShapeDtypeStruct((L, R, C), F32)
    return pl.pallas_call(
        body, name="adamw", grid=(L, R // tr), in_specs=[blk] * 4, out_specs=[blk] * 3,
        out_shape=[sh, sh, sh], compiler_params=_cp(("parallel", "parallel")))(w, g, m, v)


class _Job:
    def __init__(self, ins, out_shapes, aliases, sems, start, finish):
        self.ins, self.out_shapes, self.aliases, self.sems = list(ins), list(out_shapes), dict(aliases), list(sems)
        self.start, self.finish = start, finish


def _host_call(body, name, ins, in_specs, out_shapes, out_specs, scratch, aliases, job):
    n_in, n_out, n_scr = len(ins), len(out_shapes), len(scratch)
    jins = job.ins if job else []
    jouts = job.out_shapes if job else []
    jsems = job.sems if job else []

    def wrapped(*refs):
        a = n_in
        b = a + len(jins)
        c = b + n_out
        d = c + len(jouts)
        e = d + n_scr
        comm = None
        if job:
            jrefs = (refs[a:b], refs[c:d], refs[e:])
            comm = (lambda: job.start(*jrefs), lambda st: job.finish(st, *jrefs))
        body(refs[:a], refs[b:c], refs[d:e], comm)

    al = dict(aliases)
    if job:
        for ji, jo in job.aliases.items():
            al[n_in + ji] = n_out + jo
    res = pl.pallas_call(
        wrapped, name=name, in_specs=list(in_specs) + [ANY_SPEC] * len(jins),
        out_specs=list(out_specs) + [ANY_SPEC] * len(jouts), out_shape=list(out_shapes) + list(jouts),
        scratch_shapes=list(scratch) + list(jsems), input_output_aliases=al,
        compiler_params=_cp())(*ins, *jins)
    return res[:n_out], res[n_out:]


def _copy_in(src, dst, sem):
    cp = pltpu.make_async_copy(src, dst, sem)
    cp.start()
    cp.wait()


CHAINS = [(p, b) for p in range(2) for b in range(BL)]
NC = len(CHAINS)
ROWS_SHAPE = jax.ShapeDtypeStruct((NSTAT, S), F32)
SLAB_QKV = pltpu.VMEM((T, 2 * PAIRW), BF16)
SLAB_OUT = pltpu.VMEM((T, 2 * BQ), BF16)
SLAB_O32 = pltpu.VMEM((T, 2 * BQ), F32)
SLAB_T = pltpu.VMEM((2, BQ, T), BF16)
SLAB_KEYB = pltpu.VMEM((NSTAT, S, BQ), F32)
ACC_KV = pltpu.VMEM((2, T, BQ), F32)
NTRI = NB * (NB + 1) // 2
A_TILES = jax.ShapeDtypeStruct((NTRI, NC, HB, BQ), BF16)
A_SLOTS_OUT, A_SLOTS_IN = 2, 4


def _lane_masks():
    lane = _iota((1, BQ), 1)
    m0 = (lane < 64).astype(BF16)
    return m0, 1.0 - m0


def _row_masks():
    r = _iota((BQ, 1), 0)
    m0 = (r < 64).astype(BF16)
    return m0, 1.0 - m0


def _stack(x, m0, m1):
    return jnp.concatenate([x * m0, x * m1], axis=0)


def _stack_t(xt, r0, r1):
    return jnp.concatenate([xt * r0, xt * r1], axis=1)


def _tr(x):
    return x.T


def _rows(b, i):
    return pl.ds(pl.multiple_of(b * S + i * BQ, BQ), BQ)


def _transpose_slab(src, dst, col0):
    def blk(n, _):
        r = pl.ds(pl.multiple_of(n * BQ, BQ), BQ)
        for p in range(2):
            dst[p, :, r] = _tr(src[r, col0(p):col0(p) + BQ])
        return 0

    lax.fori_loop(0, T // BQ, blk, 0)


def _heads(x):
    return x[:BQ], x[BQ:]


def _bcast_heads(r0, r1):
    return jnp.concatenate([jnp.broadcast_to(r0, (BQ, BQ)), jnp.broadcast_to(r1, (BQ, BQ))], axis=0)


def _by_channel(r0, r1):
    return jnp.where(_iota((BQ, BQ), 0) < 64, r0, r1)


def _colsum2(x):
    return jnp.sum(x[:BQ], axis=0, keepdims=True), jnp.sum(x[BQ:], axis=0, keepdims=True)


def _stat_row(ref, p, b, h, i):
    c = b * NH + 2 * p + h
    return ref[c:c + 1, pl.ds(pl.multiple_of(i * BQ, BQ), BQ)]


def _put_row(ref, p, b, h, i, v):
    c = b * NH + 2 * p + h
    ref[c:c + 1, pl.ds(pl.multiple_of(i * BQ, BQ), BQ)] = v


def _valid_t(strict):
    r = _iota((HB, BQ), 0) & (BQ - 1)
    c = _iota((HB, BQ), 1)
    return (r < c) if strict else (r <= c)


def _tri_blockdiag(later):
    r = _iota((HB, HB), 0)
    c = _iota((HB, HB), 1)
    same = (r >= BQ) == (c >= BQ)
    return (same & ((c > r) if later else (c < r))).astype(BF16)


def _cum_mm(tri, x):
    y = _dot(tri, _split2(x))
    return y[:, :BQ] + y[:, BQ:]


def _kv_tiles(qkv_v, p, b, j):
    r = _rows(b, j)
    return qkv_v[r, p * PAIRW + BQ:p * PAIRW + 2 * BQ], qkv_v[r, p * PAIRW + 2 * BQ:p * PAIRW + 3 * BQ]


def _q_tile(qkv_v, p, b, i):
    return qkv_v[_rows(b, i), p * PAIRW:p * PAIRW + BQ] * SCALE


def _sb_fwd(qkv, job=None):
    def body(ins, outs, scr, comm):
        (qkv_hbm,), (o_hbm, a_hbm), (qkv_v, o_v, sem, vt_v, a_st, a_sems) = ins, outs, scr
        _copy_in(qkv_hbm.at[:, pl.ds(0, 2 * PAIRW)], qkv_v, sem)
        st = comm[0]() if comm else None
        _transpose_slab(qkv_v, vt_v, lambda p: p * PAIRW + 2 * BQ)
        m0, m1 = _lane_masks()
        r0, r1 = _row_masks()
        valid = _valid_t(True)
        later = _tri_blockdiag(True)

        def a_copy(n, t):
            slot = n % A_SLOTS_OUT
            return pltpu.make_async_copy(a_st.at[slot], a_hbm.at[t], a_sems.at[slot])

        def steps(qts, i, jj, cs, diag):
            j = i - jj
            ks = [_stack(_kv_tiles(qkv_v, p, b, j)[0], m0, m1) for p, b in CHAINS]
            zs = [_dot(ks[c], qts[c]) for c in range(NC)]
            lbs, lrs = [], []
            for c in range(NC):
                lb = _log_sigmoid_tile(zs[c])
                lr = lb - zs[c]
                if diag:
                    lr = jnp.where(valid, lr, 0.0)
                lbs.append(lb)
                lrs.append(lr)
            tails = [_cum_mm(later, lrs[c]) for c in range(NC)]
            avs = []
            for c in range(NC):
                a = jnp.exp(lbs[c] + tails[c] + _bcast_heads(*cs[c][0]))
                if diag:
                    a = jnp.where(valid, a, 0.0)
                avs.append(a.astype(BF16))
            n = (i * (i + 1)) // 2 + jj

            @pl.when(n >= A_SLOTS_OUT)
            def _():
                a_copy(n, 0).wait()
            for c in range(NC):
                a_st[n % A_SLOTS_OUT, c] = avs[c]
            a_copy(n, n - jj + j).start()
            out = []
            for c, (p, b) in enumerate(CHAINS):
                vts = _stack_t(vt_v[p, :, _rows(b, j)], r0, r1)
                s0, s1 = _colsum2(lrs[c])
                out.append(((cs[c][0][0] + s0, cs[c][0][1] + s1), cs[c][1] + _dot(vts, avs[c])))
            return tuple(out)

        def qblock(i, _):
            qts = [_tr(_q_tile(qkv_v, p, b, i)) for p, b in CHAINS]
            zr = jnp.zeros((1, BQ), F32)
            cs = steps(qts, i, 0, (((zr, zr), jnp.zeros((BQ, BQ), F32)),) * NC, True)
            cs = lax.fori_loop(1, i + 1, lambda jj, cs: steps(qts, i, jj, cs, False), cs)
            for c, (p, b) in enumerate(CHAINS):
                o_v[_rows(b, i), p * BQ:(p + 1) * BQ] = cs[c][1].T.astype(BF16)
            return 0

        lax.fori_loop(0, NB, qblock, 0)
        for n in range(NTRI - A_SLOTS_OUT, NTRI):
            a_copy(n, 0).wait()
        _copy_in(o_v, o_hbm.at[:, pl.ds(0, 2 * BQ)], sem)
        if comm:
            comm[1](st)

    (mixed, amat), extra = _host_call(
        body, "sb_fwd", [qkv], [ANY_SPEC], [jax.ShapeDtypeStruct((T, D), BF16), A_TILES], [ANY_SPEC, ANY_SPEC],
        [SLAB_QKV, SLAB_OUT, pltpu.SemaphoreType.DMA, SLAB_T, pltpu.VMEM((A_SLOTS_OUT, NC, HB, BQ), BF16),
         pltpu.SemaphoreType.DMA((A_SLOTS_OUT,))], {}, job)
    return mixed, amat, extra


def _sb_bwd(qkv, dmixed, amat, job=None):
    def body(ins, outs, scr, comm):
        (qkv_hbm, do_hbm, a_hbm), (dqkv_hbm,), (qkv_v, do_v, dq_v, dk_s, dv_s, sems, kt_v, a_st, a_sems) = ins, outs, scr
        sem = sems.at[0]

        def a_copy(t):
            slot = t % A_SLOTS_IN
            return pltpu.make_async_copy(a_hbm.at[t], a_st.at[slot], a_sems.at[slot])

        later = [pltpu.make_async_copy(do_hbm.at[:, pl.ds(0, 2 * BQ)], do_v, sems.at[1])]
        for cp in later:
            cp.start()
        for t in range(A_SLOTS_IN - 1):
            a_copy(t).start()
        _copy_in(qkv_hbm.at[:, pl.ds(0, 2 * PAIRW)], qkv_v, sem)
        st = comm[0]() if comm else None
        _transpose_slab(qkv_v, kt_v, lambda p: p * PAIRW + BQ)
        for cp in later:
            cp.wait()
        m0, m1 = _lane_masks()
        f0, f1 = m0.astype(F32), m1.astype(F32)
        r0, r1 = _row_masks()
        valid = _valid_t(True)
        earlier = _tri_blockdiag(False)
        dk_s[...] = jnp.zeros_like(dk_s)
        dv_s[...] = jnp.zeros_like(dv_s)

        def steps(qns, qts, dns, dts, i, j, cs, diag):
            t = (i * (i + 1)) // 2 + j
            a_copy(t).wait()

            @pl.when(t + A_SLOTS_IN - 1 < NTRI)
            def _():
                a_copy(t + A_SLOTS_IN - 1).start()
            slot = t % A_SLOTS_IN
            kv =[_kv_tiles(qkv_v, p, b, j) for p, b in CHAINS]
            ks = [_stack(kv[c][0], m0, m1) for c in range(NC)]
            vs = [_stack(kv[c][1], m0, m1) for c in range(NC)]
            zs = [_dot(ks[c], qts[c]) for c in range(NC)]
            das = [_dot(vs[c], dts[c]) for c in range(NC)]
            avs = [a_st[slot, c] for c in range(NC)]
            gms = [das[c] * avs[c].astype(F32) for c in range(NC)]
            befores = [_cum_mm(earlier, gms[c]) for c in range(NC)]
            dzbs = []
            for c in range(NC):
                dz = gms[c] - jax.nn.sigmoid(zs[c]) * (gms[c] + befores[c] + _bcast_heads(*cs[c][0]))
                if diag:
                    dz = jnp.where(valid, dz, 0.0)
                dzbs.append(dz.astype(BF16))
            out = []
            for c, (p, b) in enumerate(CHAINS):
                dq = cs[c][1] + _dot(_stack_t(kt_v[p, :, _rows(b, j)], r0, r1), dzbs[c])
                dk = _dot(dzbs[c], qns[c])
                dv = _dot(avs[c], dns[c])
                dk_s[p, _rows(b, j), :] += dk[:BQ] * f0 + dk[BQ:] * f1
                dv_s[p, _rows(b, j), :] += dv[:BQ] * f0 + dv[BQ:] * f1
                g0, g1 = _colsum2(gms[c])
                out.append(((cs[c][0][0] + g0, cs[c][0][1] + g1), dq))
            return tuple(out)

        def qblock(i, _):
            qns =[_q_tile(qkv_v, p, b, i) for p, b in CHAINS]
            dns = [do_v[_rows(b, i), p * BQ:(p + 1) * BQ] for p, b in CHAINS]
            qts = [_tr(t) for t in qns]
            dts = [_tr(t) for t in dns]
            zr = jnp.zeros((1, BQ), F32)
            cs = (((zr, zr), jnp.zeros((BQ, BQ), F32)),) * NC
            cs = lax.fori_loop(0, i, lambda j, cs: steps(qns, qts, dns, dts, i, j, cs, False), cs)
            cs = steps(qns, qts, dns, dts, i, i, cs, True)
            for c, (p, b) in enumerate(CHAINS):
                dq_v[_rows(b, i), p * PAIRW:p * PAIRW + BQ] = (cs[c][1].T * SCALE).astype(BF16)
            return 0

        lax.fori_loop(0, NB, qblock, 0)
        for p in range(2):
            dq_v[:, p * PAIRW + BQ:p * PAIRW + 2 * BQ] = dk_s[p].astype(BF16)
            dq_v[:, p * PAIRW + 2 * BQ:p * PAIRW + 3 * BQ] = dv_s[p].astype(BF16)
        _copy_in(dq_v, dqkv_hbm.at[:, pl.ds(0, 2 * PAIRW)], sem)
        if comm:
            comm[1](st)

    (dqkv,), extra = _host_call(
        body, "sb_bwd", [qkv, dmixed, amat], [ANY_SPEC, ANY_SPEC, ANY_SPEC],
        [jax.ShapeDtypeStruct((T, QKVW), BF16)], [ANY_SPEC],
        [SLAB_QKV, SLAB_OUT, SLAB_QKV, ACC_KV, ACC_KV, pltpu.SemaphoreType.DMA((4,)), SLAB_T,
         pltpu.VMEM((A_SLOTS_IN, NC, HB, BQ), BF16), pltpu.SemaphoreType.DMA((A_SLOTS_IN,))], {}, job)
    return dqkv, extra


def _flash_fwd(qkv, mixed, g, fox, bias, job=None):
    def body(ins, outs, scr, comm):
        if fox:
            qkv_hbm, cq_ref, ckb_hbm, _ = ins
            (o_hbm, lse_ref, o32_hbm), (qkv_v, o_v, sem, vt_v, o32_v, ckb_v) = outs, scr
        else:
            qkv_hbm, tbl_ref, _ = ins
            (o_hbm, lse_ref), (qkv_v, o_v, sem, vt_v) = outs, scr
        sems = sem
        sem = sems.at[0]
        later = [pltpu.make_async_copy(ckb_hbm, ckb_v, sems.at[1])] if fox else []
        for cp in later:
            cp.start()
        _copy_in(qkv_hbm.at[:, pl.ds(g * 2 * PAIRW, 2 * PAIRW)], qkv_v, sem)
        st = comm[0]() if comm else None
        _transpose_slab(qkv_v, vt_v, lambda p: p * PAIRW + 2 * BQ)
        for cp in later:
            cp.wait()
        m0, m1 = _lane_masks()
        r0, r1 = _row_masks()
        valid = _valid_t(False)

        def steps(qts, cqs, i, j, cs, diag):
            ks = [_stack(_kv_tiles(qkv_v, p, b, j)[0], m0, m1) for p, b in CHAINS]
            zs = [_dot(ks[c], qts[c]) for c in range(NC)]
            prs, alphas, out = [], [], []
            for c, (p, b) in enumerate(CHAINS):
                (ma, mb), (la, lb_), _ = cs[c]
                if fox:
                    kk = pl.ds(pl.multiple_of(j * BQ, BQ), BQ)
                    col = b * NH + 2 * p
                    z = zs[c] + (cqs[c] - jnp.concatenate([ckb_v[col, kk, :], ckb_v[col + 1, kk, :]], axis=0))
                    if diag:
                        z = jnp.where(valid, z, NEG)
                else:
                    z = zs[c] + tbl_ref[p, i - j]
                za, zb = _heads(z)
                na = jnp.maximum(ma, jnp.max(za, axis=0, keepdims=True))
                nb = jnp.maximum(mb, jnp.max(zb, axis=0, keepdims=True))
                aa, ab = jnp.exp(ma - na), jnp.exp(mb - nb)
                pr = jnp.exp(z - _bcast_heads(na, nb))
                sa, sb = _colsum2(pr)
                prs.append(_split2(pr) if fox else pr.astype(BF16))
                alphas.append((aa, ab))
                out.append(((na, nb), (aa * la + sa, ab * lb_ + sb)))
            pvs = []
            for c, (p, b) in enumerate(CHAINS):
                vts = _stack_t(vt_v[p, :, _rows(b, j)], r0, r1)
                if fox:
                    pvs.append(_dot(vts, prs[c][:, :BQ]) + _dot(vts, prs[c][:, BQ:]))
                else:
                    pvs.append(_dot(vts, prs[c]))
            return tuple((out[c][0], out[c][1], _by_channel(*alphas[c]) * cs[c][2] + pvs[c]) for c in range(NC))

        def qblock(i, _):
            qts = [_tr(_q_tile(qkv_v, p, b, i)) for p, b in CHAINS]
            if fox:
                cqs = [_bcast_heads(_stat_row(cq_ref, p, b, 0, i), _stat_row(cq_ref, p, b, 1, i)) for p, b in CHAINS]
            else:
                cqs = [None] * NC
            ng = jnp.full((1, BQ), NEG, F32)
            zr = jnp.zeros((1, BQ), F32)
            cs = steps(qts, cqs, i, i, (((ng, ng), (zr, zr), jnp.zeros((BQ, BQ), F32)),) * NC, True)
            cs = lax.fori_loop(1, i + 1, lambda jj, cs: steps(qts, cqs, i, i - jj, cs, False), cs)
            for c, (p, b) in enumerate(CHAINS):
                (ma, mb), (la, lb_), acc = cs[c]
                o = (acc / _by_channel(la, lb_)).T
                o_v[_rows(b, i), p * BQ:(p + 1) * BQ] = o.astype(BF16)
                if fox:
                    o32_v[_rows(b, i), p * BQ:(p + 1) * BQ] = o
                _put_row(lse_ref, p, b, 0, i, ma + jnp.log(la))
                _put_row(lse_ref, p, b, 1, i, mb + jnp.log(lb_))
            return 0

        lax.fori_loop(0, NB, qblock, 0)
        _copy_in(o_v, o_hbm.at[:, pl.ds(g * 2 * BQ, 2 * BQ)], sem)
        if fox:
            _copy_in(o32_v, o32_hbm, sem)
        if comm:
            comm[1](st)

    bias_specs = [VMEM_SPEC, ANY_SPEC] if fox else [VMEM_SPEC]
    n_in = 2 + len(bias_specs)
    o32 = [jax.ShapeDtypeStruct((T, 2 * BQ), F32)] if fox else []
    res, extra = _host_call(
        body, "fox_fwd" if fox else "dil_fwd", [qkv, *bias, mixed], [ANY_SPEC] + bias_specs + [ANY_SPEC],
        [jax.ShapeDtypeStruct((T, D), BF16), ROWS_SHAPE] + o32, [ANY_SPEC, VMEM_SPEC] + [ANY_SPEC] * len(o32),
        [SLAB_QKV, SLAB_OUT, pltpu.SemaphoreType.DMA((4,)), SLAB_T] + ([SLAB_O32, SLAB_KEYB] if fox else []),
        {n_in - 1: 0}, job)
    return (*res, extra)


def _flash_bwd(qkv, o, dmixed, lse, dqkv, g, fox, bias, job=None):
    def body(ins, outs, scr, comm):
        if fox:
            qkv_hbm, o_hbm, do_hbm, lse_ref, cq_ref, ckb_hbm, _ = ins
            (dqkv_hbm, db_ref), (qkv_v, o_v, do_v, dq_v, dk_s, dv_s, sem, kt_v, ckb_v, dc_s) = outs, scr
        else:
            qkv_hbm, o_hbm, do_hbm, lse_ref, tbl_ref, _ = ins
            (dqkv_hbm, db_ref), (qkv_v, o_v, do_v, dq_v, dk_s, dv_s, sem, kt_v) = outs, scr
        sems = sem
        sem = sems.at[0]
        later = [pltpu.make_async_copy(do_hbm.at[:, pl.ds(g * 2 * BQ, 2 * BQ)], do_v, sems.at[1])]
        if fox:
            later += [pltpu.make_async_copy(o_hbm, o_v, sems.at[2]), pltpu.make_async_copy(ckb_hbm, ckb_v, sems.at[3])]
        else:
            later += [pltpu.make_async_copy(o_hbm.at[:, pl.ds(g * 2 * BQ, 2 * BQ)], o_v, sems.at[2])]
        for cp in later:
            cp.start()
        _copy_in(qkv_hbm.at[:, pl.ds(g * 2 * PAIRW, 2 * PAIRW)], qkv_v, sem)
        st = comm[0]() if comm else None
        _transpose_slab(qkv_v, kt_v, lambda p: p * PAIRW + BQ)
        for cp in later:
            cp.wait()
        m0, m1 = _lane_masks()
        f0, f1 = m0.astype(F32), m1.astype(F32)
        r0, r1 = _row_masks()
        valid = _valid_t(False)
        dk_s[...] = jnp.zeros_like(dk_s)
        dv_s[...] = jnp.zeros_like(dv_s)
        if fox:
            dc_s[...] = jnp.zeros_like(dc_s)
        else:
            db_ref[...] = jnp.zeros_like(db_ref)

        def steps(qns, qts, dns, dts, cqs, lses, deltas, i, j, dqs, diag):
            kv = [_kv_tiles(qkv_v, p, b, j) for p, b in CHAINS]
            ks = [_stack(kv[c][0], m0, m1) for c in range(NC)]
            vs = [_stack(kv[c][1], m0, m1) for c in range(NC)]
            zs = [_dot(ks[c], qts[c]) for c in range(NC)]
            dps = [_dot(vs[c], dts[c]) for c in range(NC)]
            prs, dzl = [], []
            for c, (p, b) in enumerate(CHAINS):
                if fox:
                    kk = pl.ds(pl.multiple_of(j * BQ, BQ), BQ)
                    col = b * NH + 2 * p
                    z = zs[c] + (cqs[c] - jnp.concatenate([ckb_v[col, kk, :], ckb_v[col + 1, kk, :]], axis=0))
                    if diag:
                        z = jnp.where(valid, z, NEG)
                else:
                    z = zs[c] + tbl_ref[p, i - j]
                pr = jnp.exp(z - lses[c])
                prs.append(pr.astype(BF16))
                dzl.append(pr * (dps[c] - deltas[c]))
            dzbs = [dz.astype(BF16) for dz in dzl]
            new = []
            for c, (p, b) in enumerate(CHAINS):
                new.append(dqs[c] + _dot(_stack_t(kt_v[p, :, _rows(b, j)], r0, r1), dzbs[c]))
                dk = _dot(dzbs[c], qns[c])
                dv = _dot(prs[c], dns[c])
                dk_s[p, _rows(b, j), :] += dk[:BQ] * f0 + dk[BQ:] * f1
                dv_s[p, _rows(b, j), :] += dv[:BQ] * f0 + dv[BQ:] * f1
                if fox:
                    dc_s[c, pl.ds(pl.multiple_of(j * HB, HB), HB), :] += dzl[c]
            if not fox:
                for p in range(2):
                    db_ref[p, i - j] = db_ref[p, i - j] + (dzl[2 * p] + dzl[2 * p + 1])
            return tuple(new)

        def qblock(i, _):
            qns = [_q_tile(qkv_v, p, b, i) for p, b in CHAINS]
            dns = [do_v[_rows(b, i), p * BQ:(p + 1) * BQ] for p, b in CHAINS]
            qts = [_tr(t) for t in qns]
            dts = [_tr(t) for t in dns]
            lses = [_bcast_heads(_stat_row(lse_ref, p, b, 0, i), _stat_row(lse_ref, p, b, 1, i)) for p, b in CHAINS]
            if fox:
                cqs = [_bcast_heads(_stat_row(cq_ref, p, b, 0, i), _stat_row(cq_ref, p, b, 1, i)) for p, b in CHAINS]
            else:
                cqs = [None] * NC
            deltas = []
            for c, (p, b) in enumerate(CHAINS):
                pt = (dns[c].astype(F32) * o_v[_rows(b, i), p * BQ:(p + 1) * BQ].astype(F32)).T
                deltas.append(_bcast_heads(jnp.sum(pt[:64], axis=0, keepdims=True), jnp.sum(pt[64:], axis=0, keepdims=True)))
            dqs = (jnp.zeros((BQ, BQ), F32),) * NC
            dqs = lax.fori_loop(0, i, lambda j, d: steps(qns, qts, dns, dts, cqs, lses, deltas, i, j, d, False), dqs)
            dqs = steps(qns, qts, dns, dts, cqs, lses, deltas, i, i, dqs, True)
            for c, (p, b) in enumerate(CHAINS):
                dq_v[_rows(b, i), p * PAIRW:p * PAIRW + BQ] = (dqs[c].T * SCALE).astype(BF16)
            return 0

        lax.fori_loop(0, NB, qblock, 0)
        for p in range(2):
            dq_v[:, p * PAIRW + BQ:p * PAIRW + 2 * BQ] = dk_s[p].astype(BF16)
            dq_v[:, p * PAIRW + 2 * BQ:p * PAIRW + 3 * BQ] = dv_s[p].astype(BF16)
        _copy_in(dq_v, dqkv_hbm.at[:, pl.ds(g * 2 * PAIRW, 2 * PAIRW)], sem)
        if fox:
            lane = _iota((BQ, NSTAT), 1)

            def fold(n, _):
                t = jnp.zeros((BQ, NSTAT), F32)
                for c, (p, b) in enumerate(CHAINS):
                    s = jnp.sum(dc_s[c, pl.ds(pl.multiple_of(n * HB, HB), HB), :], axis=1, keepdims=True)
                    col = b * NH + 2 * p
                    t = t - jnp.where(lane == col, s[:BQ], 0.0) - jnp.where(lane == col + 1, s[BQ:], 0.0)
                db_ref[pl.ds(pl.multiple_of(n * BQ, BQ), BQ), :] = t
                return 0

            lax.fori_loop(0, NB, fold, 0)
        if comm:
            comm[1](st)

    if fox:
        bias_specs = [VMEM_SPEC, ANY_SPEC]
        db_shape = jax.ShapeDtypeStruct((S, NSTAT), F32)
        more = [SLAB_KEYB, pltpu.VMEM((NC, NB * HB, BQ), F32)]
    else:
        bias_specs = [VMEM_SPEC]
        db_shape = jax.ShapeDtypeStruct((2, NB, HB, BQ), F32)
        more = []
    n_in = 5 + len(bias_specs)
    (dqkv, db), extra = _host_call(
        body, "fox_bwd" if fox else "dil_bwd", [qkv, o, dmixed, lse, *bias, dqkv],
        [ANY_SPEC, ANY_SPEC, ANY_SPEC, VMEM_SPEC] + bias_specs + [ANY_SPEC],
        [jax.ShapeDtypeStruct((T, QKVW), BF16), db_shape], [ANY_SPEC, VMEM_SPEC],
        [SLAB_QKV, SLAB_O32 if fox else SLAB_OUT, SLAB_OUT, SLAB_QKV, ACC_KV, ACC_KV, pltpu.SemaphoreType.DMA((4,)), SLAB_T]
        + more, {n_in - 1: 0}, job)
    return dqkv, db, extra


def _delta_t(d):
    return d * BQ + _iota((HB, BQ), 1) - (_iota((HB, BQ), 0) & (BQ - 1))


def _buckets_in(d):
    lo, hi = max(d * BQ - (BQ - 1), 0), d * BQ + BQ - 1
    return [b for b in range(32) if BUCKET_TH[b] <= hi and (b == 31 or BUCKET_TH[b + 1] > lo)]


def _in_bucket(delta, b):
    m = delta >= BUCKET_TH[b]
    return m if b == 31 else m & (delta < BUCKET_TH[b + 1])


def _dil_table(rel_bias, job=None):
    def body(ins, outs, scr, comm):
        (rb_ref,), (o_ref,) = ins, outs
        st = comm[0]() if comm else None
        for d in range(NB):
            delta = _delta_t(d)
            pos = delta >= 0
            n = ((pos & (delta <= 128)).astype(jnp.int32)
                 + (pos & (delta <= 512) & ((delta & 3) == 0)).astype(jnp.int32)
                 + (pos & ((delta & 15) == 0)).astype(jnp.int32))
            logn = jnp.where(n == 3, math.log(3.0), jnp.where(n == 2, math.log(2.0), jnp.where(n == 1, 0.0, NEG)))
            head1 = _iota((HB, BQ), 0) >= BQ
            for p in range(2):
                val = jnp.zeros((HB, BQ), F32)
                for b in _buckets_in(d):
                    val = jnp.where(_in_bucket(delta, b), jnp.where(head1, rb_ref[b, 2 * p + 1], rb_ref[b, 2 * p]), val)
                o_ref[p, d] = val + logn
        if comm:
            comm[1](st)

    (tbl,), extra = _host_call(
        body, "dil_table", [rel_bias], [pl.BlockSpec(memory_space=pltpu.SMEM)],
        [jax.ShapeDtypeStruct((2, NB, HB, BQ), F32)], [VMEM_SPEC], [], {}, job)
    return (tbl, extra) if job else tbl


def _dil_table_bwd(dtbl):
    def body(dt_ref, o_ref):
        p = pl.program_id(0)
        rowi = _iota((32, BQ), 0)
        lanei = _iota((32, BQ), 1)

        @pl.when(p == 0)
        def _():
            o_ref[...] = jnp.zeros_like(o_ref)

        out = jnp.zeros((32, BQ), F32)
        for b in range(32):
            acc = None
            for d in range(NB):
                if b in _buckets_in(d):
                    t = jnp.where(_in_bucket(_delta_t(d), b), dt_ref[d], 0.0)
                    acc = t if acc is None else acc + t
            rs = jnp.sum(acc, axis=1, keepdims=True)
            s0 = jnp.sum(rs[:BQ], axis=0, keepdims=True)
            s1 = jnp.sum(rs[BQ:], axis=0, keepdims=True)
            out = (out + jnp.where((rowi == b) & (lanei == 2 * p), s0, 0.0)
                   + jnp.where((rowi == b) & (lanei == 2 * p + 1), s1, 0.0))
        o_ref[...] += out

    return pl.pallas_call(
        body, name="dil_table_bwd", grid=(2,),
        in_specs=[pl.BlockSpec((None, NB, HB, BQ), lambda p: (p, 0, 0, 0))],
        out_specs=pl.BlockSpec((32, BQ), lambda p: (0, 0)),
        out_shape=jax.ShapeDtypeStruct((32, BQ), F32),
        compiler_params=_cp(("arbitrary",)))(dtbl)


def _fox_prep(gate, fb):
    def body(g_ref, fb_ref, c_ref):
        tri = (_iota((BQ, BQ), 0) >= _iota((BQ, BQ), 1)).astype(BF16)

        def blk(i, carry):
            r0 = pl.multiple_of(i * BQ, BQ)
            lf = _log_sigmoid(g_ref[pl.ds(r0, BQ), :] + fb_ref[...])
            c = _dot(tri, _split3(lf))
            c_ref[pl.ds(r0, BQ), :] = c[:, 0:BQ] + c[:, BQ:2 * BQ] + c[:, 2 * BQ:3 * BQ] + carry
            return carry + jnp.sum(lf, axis=0, keepdims=True)

        lax.fori_loop(0, NB, blk, jnp.zeros((1, BQ), F32))

    blk = pl.BlockSpec((S, GATEW), lambda b: (b, 0))
    return pl.pallas_call(
        body, name="fox_prep", grid=(BL,), in_specs=[blk, pl.BlockSpec((1, GATEW), lambda b: (0, 0))],
        out_specs=blk, out_shape=jax.ShapeDtypeStruct((T, GATEW), F32),
        compiler_params=_cp(("parallel",)))(gate, fb)


def _fox_post(dcum, gate, fb):
    def body(dc_ref, g_ref, fb_ref, dg_ref, dfb_ref):
        b = pl.program_id(0)
        tri = (_iota((BQ, BQ), 0) <= _iota((BQ, BQ), 1)).astype(BF16)

        def blk(ii, carry):
            csum, dfb = carry
            r0 = pl.multiple_of((NB - 1 - ii) * BQ, BQ)
            dc = dc_ref[pl.ds(r0, BQ), :]
            c = _dot(tri, _split3(dc))
            dlf = c[:, 0:BQ] + c[:, BQ:2 * BQ] + c[:, 2 * BQ:3 * BQ] + csum
            dg = dlf * jnp.exp(_log_sigmoid(-(g_ref[pl.ds(r0, BQ), :] + fb_ref[...])))
            dg_ref[pl.ds(r0, BQ), :] = dg
            return csum + jnp.sum(dc, axis=0, keepdims=True), dfb + jnp.sum(dg, axis=0, keepdims=True)

        z = jnp.zeros((1, BQ), F32)
        _, dfb = lax.fori_loop(0, NB, blk, (z, z))

        @pl.when(b == 0)
        def _():
            dfb_ref[...] = dfb

        @pl.when(b > 0)
        def _():
            dfb_ref[...] += dfb

    blk = pl.BlockSpec((S, GATEW), lambda b: (b, 0))
    vec = pl.BlockSpec((1, GATEW), lambda b: (0, 0))
    return pl.pallas_call(
        body, name="fox_post", grid=(BL,), in_specs=[blk, blk, vec], out_specs=[blk, vec],
        out_shape=[jax.ShapeDtypeStruct((T, GATEW), F32), jax.ShapeDtypeStruct((1, GATEW), F32)],
        compiler_params=_cp(("arbitrary",)))(dcum, gate, fb)


def _shift_down(x, n):
    return jnp.where(_iota(x.shape, 0) >= n, pltpu.roll(x, n, 0), 0.0)


def _shift_up(x, n):
    return jnp.where(_iota(x.shape, 0) < S - n, pltpu.roll(x, S - n, 0), 0.0)


def _conv_fwd(conv, cw, mixed):
    W = 256

    def body(c_ref, w_ref, _, o_ref):
        u = c_ref[:, W:2 * W] * c_ref[:, 2 * W:3 * W]
        y = w_ref[0:1, :] * _shift_down(u, 2) + w_ref[1:2, :] * _shift_down(u, 1) + w_ref[2:3, :] * u
        o_ref[...] = (c_ref[:, 0:W] * y).astype(BF16)

    return pl.pallas_call(
        body, name="conv_fwd", grid=(BL,),
        in_specs=[pl.BlockSpec((S, CONVW), lambda b: (b, 0)), pl.BlockSpec((8, W), lambda b: (0, 0)), ANY_SPEC],
        out_specs=pl.BlockSpec((S, W), lambda b: (b, 3)),
        out_shape=jax.ShapeDtypeStruct((T, D), BF16), input_output_aliases={2: 0},
        compiler_params=_cp(("parallel",)))(conv, cw, mixed)


def _conv_bwd(conv, cw, dmixed):
    W = 256

    def body(c_ref, w_ref, do_ref, dc_ref, dw_ref):
        b = pl.program_id(0)
        bg = c_ref[:, 0:W]
        cg = c_ref[:, W:2 * W]
        hv = c_ref[:, 2 * W:3 * W]
        do = do_ref[...].astype(F32)
        u = cg * hv
        u1 = _shift_down(u, 1)
        u2 = _shift_down(u, 2)
        y = w_ref[0:1, :] * u2 + w_ref[1:2, :] * u1 + w_ref[2:3, :] * u
        dy = do * bg
        du = w_ref[2:3, :] * dy + w_ref[1:2, :] * _shift_up(dy, 1) + w_ref[0:1, :] * _shift_up(dy, 2)
        dc_ref[:, 0:W] = (do * y).astype(BF16)
        dc_ref[:, W:2 * W] = (du * hv).astype(BF16)
        dc_ref[:, 2 * W:3 * W] = (du * cg).astype(BF16)
        rowi = _iota((8, W), 0)
        dw = (jnp.where(rowi == 0, jnp.sum(dy * u2, axis=0, keepdims=True), 0.0)
              + jnp.where(rowi == 1, jnp.sum(dy * u1, axis=0, keepdims=True), 0.0)
              + jnp.where(rowi == 2, jnp.sum(dy * u, axis=0, keepdims=True), 0.0))

        @pl.when(b == 0)
        def _():
            dw_ref[...] = dw

        @pl.when(b > 0)
        def _():
            dw_ref[...] += dw

    return pl.pallas_call(
        body, name="conv_bwd", grid=(BL,),
        in_specs=[pl.BlockSpec((S, CONVW), lambda b: (b, 0)), pl.BlockSpec((8, W), lambda b: (0, 0)),
                  pl.BlockSpec((S, W), lambda b: (b, 3))],
        out_specs=[pl.BlockSpec((S, CONVW), lambda b: (b, 0)), pl.BlockSpec((8, W), lambda b: (0, 0))],
        out_shape=[jax.ShapeDtypeStruct((T, CONVW), BF16), jax.ShapeDtypeStruct((8, W), F32)],
        compiler_params=_cp(("arbitrary",)))(conv, cw, dmixed)


def _place():
    x, y, c = lax.axis_index("x"), lax.axis_index("y"), lax.axis_index("c")
    return x, y, c


def _chips_of(x, y):
    return [(1 - x, y), (x, 1 - y), (1 - x, 1 - y)]


def _dev(p):
    return 4 * p[0] + 2 * p[1] + p[2]


def _gather_job_a(shards):
    n = len(shards)

    def peers(x, y, c):
        return [(x, y, 1 - c)] + [(*chip, c) for chip in _chips_of(x, y)]

    def start(ins, outs, sems):
        send, recv, loc = sems
        x, y, c = _place()
        me = (x, y, c)
        cps = []
        for a in range(n):
            cps.append(pltpu.make_async_copy(ins[a], outs[a].at[_dev(me)], loc.at[a]))
            for k, peer in enumerate(peers(x, y, c)):
                cps.append(pltpu.make_async_remote_copy(
                    src_ref=ins[a], dst_ref=outs[a].at[_dev(me)], send_sem=send.at[a, k], recv_sem=recv.at[a, k],
                    device_id=peer, device_id_type=MESH))
        for cp in cps:
            cp.start()
        return cps

    def finish(cps, ins, outs, sems):
        send, recv, loc = sems
        x, y, c = _place()
        for a in range(n):
            for k, peer in enumerate(peers(x, y, c)):
                pltpu.make_async_remote_copy(
                    src_ref=ins[a], dst_ref=outs[a].at[_dev(peer)], send_sem=send.at[a, k], recv_sem=recv.at[a, k],
                    device_id=(x, y, c), device_id_type=MESH).wait_recv()
        for a in range(n):
            cps[5 * a].wait()
            for k in range(4):
                cps[5 * a + 1 + k].wait_send()

    return _Job(shards, [jax.ShapeDtypeStruct((NDEV,) + s.shape, s.dtype) for s in shards], {},
                [pltpu.SemaphoreType.DMA((n, 4)), pltpu.SemaphoreType.DMA((n, 4)), pltpu.SemaphoreType.DMA((n,))],
                start, finish)


def _gather_job_b(gathered):
    n = len(gathered)

    def start(ins, outs, sems):
        send, recv = sems
        x, y, c = _place()
        cps = []
        for a in range(n):
            for j, chip in enumerate(_chips_of(x, y)):
                blk = outs[a].at[_dev((*chip, c))]
                cps.append(pltpu.make_async_remote_copy(
                    src_ref=blk, dst_ref=blk, send_sem=send.at[a, j], recv_sem=recv.at[a, j],
                    device_id=(x, y, 1 - c), device_id_type=MESH))
        for cp in cps:
            cp.start()
        return cps

    def finish(cps, ins, outs, sems):
        send, recv = sems
        x, y, c = _place()
        for a in range(n):
            for j, chip in enumerate(_chips_of(x, y)):
                blk = outs[a].at[_dev((*chip, 1 - c))]
                pltpu.make_async_remote_copy(
                    src_ref=blk, dst_ref=blk, send_sem=send.at[a, j], recv_sem=recv.at[a, j],
                    device_id=(x, y, c), device_id_type=MESH).wait_recv()
        for cp in cps:
            cp.wait_send()

    return _Job(gathered, [jax.ShapeDtypeStruct(g.shape, g.dtype) for g in gathered], {a: a for a in range(n)},
                [pltpu.SemaphoreType.DMA((n, 3)), pltpu.SemaphoreType.DMA((n, 3))], start, finish)


def _sibling_job(grads):
    n = len(grads)

    def start(ins, outs, sems):
        send, recv = sems
        x, y, c = _place()
        cps = [pltpu.make_async_remote_copy(
            src_ref=ins[a].at[:, 1 - c], dst_ref=outs[a], send_sem=send.at[a], recv_sem=recv.at[a],
            device_id=(x, y, 1 - c), device_id_type=MESH) for a in range(n)]
        for cp in cps:
            cp.start()
        return cps

    def finish(cps, ins, outs, sems):
        for cp in cps:
            cp.wait()

    return _Job(grads, [jax.ShapeDtypeStruct(g.shape[:1] + g.shape[2:], F32) for g in grads], {},
                [pltpu.SemaphoreType.DMA((n,)), pltpu.SemaphoreType.DMA((n,))], start, finish)


def _chip_job(psums):
    n = len(psums)

    def copies(ins, outs, sems):
        send, recv, loc = sems
        x, y, c = _place()
        mychip = 2 * x + y
        cps = []
        for a in range(n):
            cps.append(pltpu.make_async_copy(ins[a].at[mychip], outs[a].at[mychip], loc.at[a]))
            for j, chip in enumerate(_chips_of(x, y)):
                cps.append(pltpu.make_async_remote_copy(
                    src_ref=ins[a].at[2 * chip[0] + chip[1]], dst_ref=outs[a].at[mychip],
                    send_sem=send.at[a, j], recv_sem=recv.at[a, j], device_id=(*chip, c), device_id_type=MESH))
        return cps

    def start(ins, outs, sems):
        for cp in copies(ins, outs, sems):
            cp.start()

    def finish(_, ins, outs, sems):
        cps = copies(ins, outs, sems)
        send, recv, loc = sems
        x, y, c = _place()
        mychip = 2 * x + y
        for a in range(n):
            for j, chip in enumerate(_chips_of(x, y)):
                pltpu.make_async_remote_copy(
                    src_ref=ins[a].at[mychip], dst_ref=outs[a].at[2 * chip[0] + chip[1]],
                    send_sem=send.at[a, j], recv_sem=recv.at[a, j], device_id=(x, y, c), device_id_type=MESH).wait_recv()
        for a in range(n):
            cps[4 * a].wait()
            for j in range(3):
                cps[4 * a + 1 + j].wait_send()

    return _Job(psums, [jax.ShapeDtypeStruct(p.shape, BF16) for p in psums], {},
                [pltpu.SemaphoreType.DMA((n, 3)), pltpu.SemaphoreType.DMA((n, 3)), pltpu.SemaphoreType.DMA((n,))],
                start, finish)


def _join_jobs(*jobs):
    jobs = [j for j in jobs if j is not None]
    if len(jobs) <= 1:
        return jobs[0] if jobs else None
    cut = lambda seq, sizes: [seq[sum(sizes[:k]):sum(sizes[:k + 1])] for k in range(len(sizes))]
    n_in = [len(j.ins) for j in jobs]
    n_out = [len(j.out_shapes) for j in jobs]
    n_sem = [len(j.sems) for j in jobs]
    aliases = {}
    for k, j in enumerate(jobs):
        for a, b in j.aliases.items():
            aliases[sum(n_in[:k]) + a] = sum(n_out[:k]) + b

    def start(ins, outs, sems):
        return [j.start(i, o, s) for j, i, o, s in zip(jobs, cut(ins, n_in), cut(outs, n_out), cut(sems, n_sem))]

    def finish(sts, ins, outs, sems):
        for j, st, i, o, s in zip(jobs, sts, cut(ins, n_in), cut(outs, n_out), cut(sems, n_sem)):
            j.finish(st, i, o, s)

    return _Job([t for j in jobs for t in j.ins], [t for j in jobs for t in j.out_shapes], aliases,
                [t for j in jobs for t in j.sems], start, finish)


def _run_job(job, name):
    def body(ins, outs, scr, comm):
        comm[1](comm[0]())

    return _host_call(body, name, [], [], [], [], [], {}, job)[1]


def _allreduce_small(v, job=None):
    def body(ins, outs, scr, comm):
        (v_ref,), (o_ref,), (slots, send_sems, recv_sems) = ins, outs, scr
        st = comm[0]() if comm else None
        x, y, c = _place()
        me = 4 * x + 2 * y + c
        slots[me] = v_ref[...]

        def copy(k):
            peer = (x ^ ((k >> 2) & 1), y ^ ((k >> 1) & 1), c ^ (k & 1))
            return pltpu.make_async_remote_copy(
                src_ref=v_ref, dst_ref=slots.at[me], send_sem=send_sems.at[k - 1], recv_sem=recv_sems.at[k - 1],
                device_id=peer, device_id_type=MESH)

        def arrival(k):
            return pltpu.make_async_remote_copy(
                src_ref=v_ref, dst_ref=slots.at[me ^ k], send_sem=send_sems.at[k - 1], recv_sem=recv_sems.at[k - 1],
                device_id=(x, y, c), device_id_type=MESH)

        sends = [copy(k) for k in range(1, NDEV)]
        for cp in sends:
            cp.start()
        for k in range(1, NDEV):
            arrival(k).wait_recv()
        for cp in sends:
            cp.wait_send()
        acc = slots[0]
        for d in range(1, NDEV):
            acc = acc + slots[d]
        o_ref[...] = acc
        if comm:
            comm[1](st)

    (out,), extra = _host_call(
        body, "allreduce_small", [v], [VMEM_SPEC], [jax.ShapeDtypeStruct(v.shape, F32)], [VMEM_SPEC],
        [pltpu.VMEM((NDEV,) + v.shape, F32), pltpu.SemaphoreType.DMA((NDEV - 1,)),
         pltpu.SemaphoreType.DMA((NDEV - 1,))], {}, job)
    return (out, extra) if job else out


def _pair_sums(views, gots, core):
    n = len(views)

    def body(c_ref, *refs):
        for a in range(n):
            refs[2 * n + a][...] = (refs[a][...] + refs[n + a][...]).astype(BF16)

    def vspec(v):
        return pl.BlockSpec((None, None, v.shape[2] // 2, v.shape[3]), lambda k, h, c: (k, c[0], h, 0))

    def gspec(g):
        return pl.BlockSpec((None, g.shape[1] // 2, g.shape[2]), lambda k, h, c: (k, h, 0))

    return pl.pallas_call(
        body, name="pair_sums",
        grid_spec=pltpu.PrefetchScalarGridSpec(
            num_scalar_prefetch=1, grid=(4, 2),
            in_specs=[vspec(v) for v in views] + [gspec(g) for g in gots],
            out_specs=[gspec(g) for g in gots]),
        out_shape=[jax.ShapeDtypeStruct(g.shape, BF16) for g in gots],
        compiler_params=_cp(("parallel", "parallel")))(core, *views, *gots)


def _chip_sums(parts):
    n = len(parts)

    def body(*refs):
        for a in range(n):
            acc = refs[a][0].astype(F32)
            for k in range(1, 4):
                acc = acc + refs[a][k].astype(F32)
            refs[n + a][...] = acc

    return pl.pallas_call(
        body, name="chip_sums", in_specs=[VMEM_SPEC] * n, out_specs=[VMEM_SPEC] * n,
        out_shape=[jax.ShapeDtypeStruct(p.shape[1:], F32) for p in parts], compiler_params=_cp())(*parts)


def _permute_in(w):
    lead = w.shape[:-1]
    return w.reshape(lead + (3, 3, 2, BQ)).swapaxes(-2, -3).reshape(lead + (QKVW,))


def _unpermute_in(w):
    lead = w.shape[:-1]
    return w.reshape(lead + (3, 2, 3, BQ)).swapaxes(-2, -3).reshape(lead + (QKVW,))


def _row(v):
    v = v.reshape(-1)
    return jnp.pad(v, (0, D - v.shape[0])).reshape(1, D)


def kernel(x, w_in, f_bias, conv_w, w_out, rel_bias, ln1_g, ln1_b, w_gate, w_up, w_down, ln2_g, ln2_b, loss_target, m_w_in, m_f_bias, m_conv_w, m_w_out, m_rel_bias, m_ln1_g, m_ln1_b, m_w_gate, m_w_up, m_w_down, m_ln2_g, m_ln2_b, v_w_in, v_f_bias, v_conv_w, v_w_out, v_rel_bias, v_ln1_g, v_ln1_b, v_w_gate, v_w_up, v_w_down, v_ln2_g, v_ln2_b):
    xi, yi, ci = _place()
    me = 4 * xi + 2 * yi + ci
    core = jnp.reshape(ci, (1,)).astype(jnp.int32)

    win_s = jnp.concatenate([_permute_in(w_in[..., :QKVW]), w_in[..., QKVW:]], axis=-1)
    win_s = jnp.pad(win_s, ((0, 0), (0, 0), (0, NPAD - NPROJ))).astype(BF16)
    per_layer = [win_s, w_out.astype(BF16), jnp.swapaxes(w_gate, 1, 2).astype(BF16),
                 jnp.swapaxes(w_up, 1, 2).astype(BF16), w_down.astype(BF16)]
    sh = [[s[l] for s in per_layer] for l in range(2)]

    def whole(g):
        return g.reshape(NDEV * g.shape[1], g.shape[2])

    cw_rows = lax.dynamic_update_slice(jnp.zeros((2, 3, 256), F32), conv_w, (0, 0, me * 32))
    small = jnp.concatenate([_row(cw_rows[0]), _row(cw_rows[1]), jnp.zeros((SMALL_ROWS - 2, D), F32)], axis=0)
    small, leg_a = _allreduce_small(small, job=_gather_job_a(sh[0][:1]))
    cw_full = small[0:2, :CONVW].reshape(2, 3, 256)
    cw8 = jnp.pad(cw_full, ((0, 0), (0, 5), (0, 0)))
    fb = jnp.pad(f_bias, ((0, 0), (0, GATEW - NH))).reshape(2, 1, GATEW)
    tbl, leg_b = _dil_table(rel_bias, job=_gather_job_b(list(leg_a)))
    W = [{"win": whole(leg_b[0])}, {}]

    def wrow(tn, K, blk=0):
        return pl.BlockSpec((tn, K), lambda i, j: (j, blk))

    def arow(tm, K, blk=0):
        return pl.BlockSpec((tm, K), lambda i, j: (i, blk))

    h = x.reshape(T, D)
    hb = h.astype(BF16)
    saved = []
    for l in range(2):
        Win = W[l]["win"]
        qkv, conv, gate = _proj(hb, Win)
        cum = _fox_prep(gate, fb[l])
        cq = cum[:, :NH].reshape(BL, S, NH).transpose(0, 2, 1).reshape(NSTAT, S)
        ckb = jnp.broadcast_to(cq[:, :, None], (NSTAT, S, BQ))
        if l == 0:
            mixed, amat, a0 = _sb_fwd(qkv, job=_gather_job_a(sh[0][1:]))
            mixed, lse_d, ex = _flash_fwd(qkv, mixed, 1, False, (tbl,),
                                          job=_join_jobs(_gather_job_b(list(a0)), _gather_job_a(sh[1][:2])))
            W[0].update(zip(("wout", "wgT", "wuT", "wd"), [whole(t) for t in ex[:4]]))
            mixed, lse_f, o_fox, ex = _flash_fwd(qkv, mixed, 2, True, (cq, ckb),
                                                 job=_join_jobs(_gather_job_b(list(ex[4:])), _gather_job_a(sh[1][2:])))
            W[1].update(zip(("win", "wout"), [whole(t) for t in ex[:2]]))
            a2 = list(ex[2:])
        else:
            mixed, amat, ex = _sb_fwd(qkv, job=_gather_job_b(a2))
            W[1].update(zip(("wgT", "wuT", "wd"), [whole(t) for t in ex]))
            mixed, lse_d, _ = _flash_fwd(qkv, mixed, 1, False, (tbl,))
            mixed, lse_f, o_fox, _ = _flash_fwd(qkv, mixed, 2, True, (cq, ckb))
        Wout, WgT, WuT, Wd = W[l]["wout"], W[l]["wgT"], W[l]["wuT"], W[l]["wd"]
        mixed = _conv_fwd(conv, cw8[l], mixed)
        x1, xh1, r1, x1b = _mm_ln(mixed, Wout, h, ln1_g[l:l + 1], ln1_b[l:l + 1], "out_proj_ln")
        fs, ft, a, x2, xh2, r2, x2b = _ffn_fwd(x1b, x1, WgT, WuT, Wd, ln2_g[l:l + 1], ln2_b[l:l + 1])
        saved.append(dict(h=hb, qkv=qkv, conv=conv, gate=gate, cq=cq, ckb=ckb, mixed=mixed, amat=amat, lse_d=lse_d,
                          lse_f=lse_f, o_fox=o_fox, x1=x1b, xh1=xh1, r1=r1, fs=fs, ft=ft, a=a, xh2=xh2, r2=r2))
        h, hb = x2, x2b

    dy = h

    def view(gr):
        return gr.reshape(4, 2, gr.shape[0] // NDEV, gr.shape[1])

    G = [None, None]
    small_g = {}
    shard_g = {}
    for l in (1, 0):
        sv = saved[l]
        Win, Wout, WgT, WuT, Wd = W[l]["win"], W[l]["wout"], W[l]["wgT"], W[l]["wuT"], W[l]["wd"]
        res = _ffn_bwd(dy, sv["xh2"], sv["r2"], ln2_g[l:l + 1], sv["fs"], sv["ft"], Wd, WgT, WuT,
                       target=loss_target.reshape(T, D) if l == 1 else None)
        dgt, dut, ds2b, dx1, dg2, db2 = res[:6]
        if l == 1:
            sq = res[6]
        G_d = _mm_tn(sv["a"], ds2b, None, C=D, Ka=DFF, N=D, tm=256, tn=1024, tk=T, ooff=0, name="grad_w_down")
        G_g = _mm_tn(dgt, sv["x1"], None, C=D, Ka=DFF, N=D, tm=256, tn=1024, tk=T, ooff=0, name="grad_w_gate")
        G_u = _mm_tn(dut, sv["x1"], None, C=D, Ka=DFF, N=D, tm=256, tn=1024, tk=T, ooff=0, name="grad_w_up")
        ds1, dg1, db1, ds1b, dmixed = _ln_bwd(dx1, sv["xh1"], sv["r1"], ln1_g[l:l + 1], Wout)
        G_out = _mm_tn(sv["mixed"], ds1b, None, C=D, Ka=D, N=D, tm=256, tn=1024, tk=T, ooff=0, name="grad_w_out")
        early = [view(t) for t in (G_g, G_u, G_d, G_out)] + ([view(G[1]["in"])] if l == 0 else [])
        dqkv, gots = _sb_bwd(sv["qkv"], dmixed, sv["amat"], job=_sibling_job(early))
        ps = _pair_sums(early, list(gots), core)
        dqkv, dtbl, pa = _flash_bwd(sv["qkv"], sv["mixed"], dmixed, sv["lse_d"], dqkv, 1, False, (tbl,),
                                    job=_chip_job(ps[:2]))
        dqkv, dck, pb = _flash_bwd(sv["qkv"], sv["o_fox"], dmixed, sv["lse_f"], dqkv, 2, True,
                                   (sv["cq"], sv["ckb"]), job=_chip_job(ps[2:]))
        sums = _chip_sums(list(pa) + list(pb))
        shard_g[l] = dict(zip(("g", "u", "d", "out"), sums[:4]))
        if l == 0:
            shard_g[1]["in"] = sums[4]
        dconv, dcw = _conv_bwd(sv["conv"], cw8[l], dmixed)
        dcum = jnp.pad(dck.reshape(S, BL, NH).transpose(1, 0, 2).reshape(T, NH), ((0, 0), (0, GATEW - NH)))
        dgate, dfb = _fox_post(dcum, sv["gate"], fb[l])
        drb = _dil_table_bwd(dtbl)
        G_in = _mm_tn(sv["h"], dqkv, None, C=NPAD, Ka=D, N=QKVW, tm=512, tn=768, tk=T, ooff=0, name="grad_w_in_qkv")
        G_in = _mm_tn(sv["h"], dconv, G_in, C=NPAD, Ka=D, N=CONVW, tm=256, tn=768, tk=T, ooff=3,
                      name="grad_w_in_conv")
        G_in = _mm_tn(sv["h"], dgate, G_in, C=NPAD, Ka=D, N=GATEW, tm=1024, tn=128, tk=1024, ooff=24,
                      name="grad_w_in_gate")
        G[l] = {"in": G_in, "out": G_out, "g": G_g, "u": G_u, "d": G_d}
        if l == 0:
            late = [view(G_in)]
            tail = _chip_job(_pair_sums(late, list(_run_job(_sibling_job(late), "sibling_exchange")), core))
            dy, parts = _mm([(dqkv, arow(1024, QKVW), Win, wrow(512, QKVW, 0)),
                             (dconv, arow(1024, CONVW), Win, wrow(512, CONVW, 3)),
                             (dgate, arow(1024, GATEW), Win, wrow(512, GATEW, 24))],
                            nt=True, M=T, N=D, tm=1024, tn=512, out_dtype=F32, name="proj_dx", res=ds1,
                            res_scale=ALPHA, job=tail)
            shard_g[0]["in"] = _chip_sums(list(parts))[0]
        else:
            dy = _proj_bwd(dqkv, dconv, dgate, Win, ds1)
        small_g[l] = dict(ln1_g=dg1, ln1_b=db1, ln2_g=dg2, ln2_b=db2, cw=dcw[0:3].reshape(1, CONVW),
                          fb=dfb[:, :NH], rb=drb[:, :NH])
    grad_x = dy.reshape(BL, S, D)

    rows = []
    for name in ("ln1_g", "ln1_b", "ln2_g", "ln2_b"):
        rows += [small_g[0][name], small_g[1][name]]
    rows += [_row(small_g[0]["cw"]), _row(small_g[1]["cw"]),
             _row(jnp.concatenate([small_g[0]["fb"], small_g[1]["fb"]], axis=0)),
             _row(small_g[0]["rb"] + small_g[1]["rb"]), _row(sq)]
    rows.append(jnp.zeros((SMALL_ROWS - len(rows), D), F32))
    sg = _allreduce_small(jnp.concatenate(rows, axis=0))
    loss = sg[12, 0] * (0.5 / D)
    g_ln1_g, g_ln1_b, g_ln2_g, g_ln2_b = sg[0:2], sg[2:4], sg[4:6], sg[6:8]
    g_conv_full = sg[8:10, :CONVW].reshape(2, 3, 256)
    g_conv = lax.dynamic_slice(g_conv_full, (0, 0, me * 32), (2, 3, 32))
    g_fb = sg[10, :2 * NH].reshape(2, NH)
    g_rb = sg[11, :32 * NH].reshape(32, NH)

    def both(name):
        return jnp.stack([shard_g[0][name], shard_g[1][name]])

    g_in = both("in")
    g_w_in = jnp.concatenate([_unpermute_in(g_in[..., :QKVW]), g_in[..., QKVW:NPROJ]], axis=-1)
    g_w_out = both("out")
    g_w_gate = jnp.swapaxes(both("g"), 1, 2)
    g_w_up = jnp.swapaxes(both("u"), 1, 2)
    g_w_down = both("d")

    up_in = _adamw(w_in, g_w_in, m_w_in, v_w_in, 64)
    up_out = _adamw(w_out, g_w_out, m_w_out, v_w_out, 128)
    up_gate = _adamw(w_gate, g_w_gate, m_w_gate, v_w_gate, 256)
    up_up = _adamw(w_up, g_w_up, m_w_up, v_w_up, 256)
    up_down = _adamw(w_down, g_w_down, m_w_down, v_w_down, 352)

    def pack(fbv, cwv, rbv, l1g, l1b, l2g, l2b):
        r = [l1g, l1b, l2g, l2b, _row(cwv), _row(fbv), _row(rbv)]
        r.append(jnp.zeros((SMALL_ROWS - 11, D), F32))
        return jnp.concatenate(r, axis=0)

    pw = pack(f_bias, conv_w, rel_bias, ln1_g, ln1_b, ln2_g, ln2_b)
    pg = pack(g_fb, g_conv, g_rb, g_ln1_g, g_ln1_b, g_ln2_g, g_ln2_b)
    pm = pack(m_f_bias, m_conv_w, m_rel_bias, m_ln1_g, m_ln1_b, m_ln2_g, m_ln2_b)
    pv = pack(v_f_bias, v_conv_w, v_rel_bias, v_ln1_g, v_ln1_b, v_ln2_g, v_ln2_b)
    ups = [u[0] for u in _adamw(pw[None], pg[None], pm[None], pv[None], SMALL_ROWS)]

    def unpack(p):
        return dict(ln1_g=p[0:2], ln1_b=p[2:4], ln2_g=p[4:6], ln2_b=p[6:8],
                    conv_w=p[8, :192].reshape(2, 3, 32), f_bias=p[9, :2 * NH].reshape(2, NH),
                    rel_bias=p[10, :32 * NH].reshape(32, NH))

    sm = [unpack(p) for p in ups]

    def group(k):
        return (up_in[k], sm[k]["f_bias"], sm[k]["conv_w"], up_out[k], sm[k]["rel_bias"], sm[k]["ln1_g"],
                sm[k]["ln1_b"], up_gate[k], up_up[k], up_down[k], sm[k]["ln2_g"], sm[k]["ln2_b"])

    grads = (g_w_in, g_fb, g_conv, g_w_out, g_rb, g_ln1_g, g_ln1_b, g_w_gate, g_w_up, g_w_down, g_ln2_g, g_ln2_b)
    return (loss, grad_x) + grads + group(0) + group(1) + group(2)
```

```python
import math

import numpy as np
import jax
import jax.numpy as jnp
from jax import lax
from jax.experimental import pallas as pl
from jax.experimental.pallas import tpu as pltpu

F32 = jnp.float32
BF16 = jnp.bfloat16
MESH = pl.DeviceIdType.MESH

D = 1024
S = 2048
BL = 2
T = BL * S
NH = 4
DFF = 2816
NPROJ = 3076
NPAD = 3200
QKVW = 2304
CONVW = 768
GATEW = 128
PAIRW = 384
BQ = 128
HB = 2 * BQ
NB = S // BQ
NDEV = 8
NSTAT = BL * NH
ALPHA = 4.0 ** 0.25
SCALE = 0.125
NEG = -1e30
LN_EPS = 1e-5
ADAM_LR, ADAM_B1, ADAM_B2, ADAM_EPS, ADAM_WD, ADAM_STEP = 0.001, 0.9, 0.999, 1e-08, 0.01, 10
VMEM_LIMIT = 56 * 1024 * 1024
SMALL_ROWS = 16


def _bucket_thresholds():
    d = np.arange(0, S)
    nf = np.maximum(d, 1).astype(np.float32)
    large = 16 + (np.log(nf / np.float32(16)) / np.float32(math.log(128)) * np.float32(16)).astype(np.int32)
    b = np.where(d < 16, d, np.minimum(large, 31))
    return [int(np.argmax(b >= k)) for k in range(32)]


BUCKET_TH = _bucket_thresholds()


def _cp(sem=None, vmem=VMEM_LIMIT):
    return pltpu.CompilerParams(dimension_semantics=sem, vmem_limit_bytes=vmem)


def _dot(a, b):
    return lax.dot_general(a, b, (((1,), (0,)), ((), ())), preferred_element_type=F32)


def _dot_nt(a, b):
    return lax.dot_general(a, b, (((1,), (1,)), ((), ())), preferred_element_type=F32)


def _dot_tn(a, b):
    return lax.dot_general(a, b, (((0,), (0,)), ((), ())), preferred_element_type=F32)


def _split2(x):
    hi = x.astype(BF16)
    mid = (x - hi.astype(F32)).astype(BF16)
    return jnp.concatenate([hi, mid], axis=1)


def _split3(x):
    hi = x.astype(BF16)
    r = x - hi.astype(F32)
    mid = r.astype(BF16)
    lo = (r - mid.astype(F32)).astype(BF16)
    return jnp.concatenate([hi, mid, lo], axis=1)


def _log_sigmoid(u):
    return jnp.minimum(u, 0.0) - jnp.log1p(jnp.exp(-jnp.abs(u)))


def _log_sigmoid_tile(u):
    return jnp.minimum(u, 0.0) - jnp.log(1.0 + jnp.exp(jnp.minimum(u, -u)))


def _iota(shape, dim):
    return lax.broadcasted_iota(jnp.int32, shape, dim)


ANY_SPEC = pl.BlockSpec(memory_space=pl.ANY)
VMEM_SPEC = pl.BlockSpec(memory_space=pltpu.VMEM)


def _mm(pairs, *, nt, M, N, tm, tn, out_dtype, name, res=None, res_scale=1.0, job=None):
    n = len(pairs)
    n_in = 2 * n + (res is not None)
    jins = job.ins if job else []
    jouts = job.out_shapes if job else []
    gi, gj = M // tm, N // tn

    def body(*refs):
        o_ref = refs[n_in + len(jins)]
        if job:
            jrefs = (refs[n_in:n_in + len(jins)], refs[n_in + len(jins) + 1:n_in + len(jins) + 1 + len(jouts)],
                     refs[n_in + len(jins) + 1 + len(jouts):])

            @pl.when((pl.program_id(0) == 0) & (pl.program_id(1) == 0))
            def _():
                job.start(*jrefs)

        acc = None
        for p in range(n):
            a = refs[2 * p][...].astype(BF16)
            b = refs[2 * p + 1][...]
            d = _dot_nt(a, b) if nt else _dot(a, b)
            acc = d if acc is None else acc + d
        if res is not None:
            acc = acc + res_scale * refs[2 * n][...]
        o_ref[...] = acc.astype(out_dtype)
        if job:
            @pl.when((pl.program_id(0) == gi - 1) & (pl.program_id(1) == gj - 1))
            def _():
                job.finish(None, *jrefs)

    ops, specs = [], []
    for a, asp, b, bsp in pairs:
        ops += [a, b]
        specs += [asp, bsp]
    if res is not None:
        ops.append(res)
        specs.append(pl.BlockSpec((tm, tn), lambda i, j: (i, j)))
    out = pl.pallas_call(
        body, name=name, grid=(gi, gj), in_specs=specs + [ANY_SPEC] * len(jins),
        out_specs=[pl.BlockSpec((tm, tn), lambda i, j: (i, j))] + [ANY_SPEC] * len(jouts),
        out_shape=[jax.ShapeDtypeStruct((M, N), out_dtype)] + list(jouts),
        scratch_shapes=list(job.sems) if job else [],
        input_output_aliases={n_in + a: 1 + b for a, b in job.aliases.items()} if job else {},
        compiler_params=_cp(("arbitrary", "arbitrary") if job else ("parallel", "parallel")))(*ops, *jins)
    return (out[0], out[1:]) if job else out[0]


def _mm_tn(a, b, gbuf, *, C, Ka, N, tm, tn, tk, ooff, name):
    def body(*refs):
        a_ref, b_ref, o_ref = refs[0], refs[1], refs[-1]
        k = pl.program_id(2)
        d = _dot_tn(a_ref[...].astype(BF16), b_ref[...].astype(BF16))

        @pl.when(k == 0)
        def _():
            o_ref[...] = d

        @pl.when(k > 0)
        def _():
            o_ref[...] += d

    ops = [a, b] + ([] if gbuf is None else [gbuf])
    return pl.pallas_call(
        body, name=name, grid=(Ka // tm, N // tn, T // tk),
        in_specs=[pl.BlockSpec((tk, tm), lambda i, j, k: (k, i)),
                  pl.BlockSpec((tk, tn), lambda i, j, k: (k, j))] + ([] if gbuf is None else [ANY_SPEC]),
        out_specs=pl.BlockSpec((tm, tn), lambda i, j, k: (i, ooff + j)),
        out_shape=jax.ShapeDtypeStruct((Ka, C), F32),
        input_output_aliases={} if gbuf is None else {2: 0},
        compiler_params=_cp(("parallel", "parallel", "arbitrary")))(*ops)


def _proj(xb, w):
    tm = 512

    def body(x_ref, w_ref, qkv_ref, conv_ref, gate_ref):
        xv = x_ref[...]
        qkv_ref[...] = _dot(xv, w_ref[:, 0:QKVW]).astype(BF16)
        conv_ref[...] = _dot(xv, w_ref[:, QKVW:QKVW + CONVW])
        gate_ref[...] = _dot(xv, w_ref[:, QKVW + CONVW:NPAD])

    def rows(n):
        return pl.BlockSpec((tm, n), lambda i: (i, 0))

    return pl.pallas_call(
        body, name="proj", grid=(T // tm,),
        in_specs=[rows(D), pl.BlockSpec((D, NPAD), lambda i: (0, 0))],
        out_specs=[rows(QKVW), rows(CONVW), rows(GATEW)],
        out_shape=[jax.ShapeDtypeStruct((T, QKVW), BF16), jax.ShapeDtypeStruct((T, CONVW), F32),
                   jax.ShapeDtypeStruct((T, GATEW), F32)],
        compiler_params=_cp(("parallel",)))(xb, w)


def _proj_bwd(dqkv, dconv, dgate, w, res):
    tm = 512

    def body(a_ref, b_ref, c_ref, w_ref, r_ref, o_ref):
        acc = ALPHA * r_ref[...] + _dot_nt(a_ref[...], w_ref[:, 0:QKVW])
        acc = acc + _dot_nt(b_ref[...], w_ref[:, QKVW:QKVW + CONVW])
        o_ref[...] = acc + _dot_nt(c_ref[...].astype(BF16), w_ref[:, QKVW + CONVW:NPAD])

    def rows(n):
        return pl.BlockSpec((tm, n), lambda i: (i, 0))

    return pl.pallas_call(
        body, name="proj_bwd", grid=(T // tm,),
        in_specs=[rows(QKVW), rows(CONVW), rows(GATEW), pl.BlockSpec((D, NPAD), lambda i: (0, 0)), rows(D)],
        out_specs=rows(D), out_shape=jax.ShapeDtypeStruct((T, D), F32),
        compiler_params=_cp(("parallel",)))(dqkv, dconv, dgate, w, res)


def _ffn_fwd(xb, x, wgt, wut, wd, gam, bet):
    tm, ch = 512, 256

    def body(xb_ref, x_ref, g_ref, b_ref, wg_hbm, wu_hbm, wd_hbm,
             go_ref, uo_ref, ao_ref, y_ref, xh_ref, r_ref, yb_ref, wg_v, wu_v, wd_v, sem):
        loads = [pltpu.make_async_copy(s, d, sem.at[k])
                 for k, (s, d) in enumerate(((wg_hbm, wg_v), (wu_hbm, wu_v), (wd_hbm, wd_v)))]

        @pl.when(pl.program_id(0) == 0)
        def _():
            for cp in loads:
                cp.start()
            loads[0].wait()
            loads[1].wait()

        xv = xb_ref[...]
        for c in range(0, DFF, ch):
            gv = _dot_nt(xv, wg_v[c:c + ch, :])
            uv = _dot_nt(xv, wu_v[c:c + ch, :])
            go_ref[:, c:c + ch] = gv.astype(BF16)
            uo_ref[:, c:c + ch] = uv.astype(BF16)
            ao_ref[:, c:c + ch] = (gv * jax.nn.sigmoid(gv) * uv).astype(BF16)
        @pl.when(pl.program_id(0) == 0)
        def _():
            loads[2].wait()

        s = ALPHA * x_ref[...] + _dot(ao_ref[...], wd_v[...])
        mu = jnp.mean(s, axis=-1, keepdims=True)
        xc = s - mu
        var = jnp.mean(xc * xc, axis=-1, keepdims=True)
        r = lax.rsqrt(var + LN_EPS)
        xh = xc * r
        xh_ref[...] = xh.astype(BF16)
        r_ref[...] = r
        y = xh * g_ref[...] + b_ref[...]
        y_ref[...] = y
        yb_ref[...] = y.astype(BF16)

    row = pl.BlockSpec((tm, D), lambda i: (i, 0))
    wide = pl.BlockSpec((tm, DFF), lambda i: (i, 0))
    vec = pl.BlockSpec((1, D), lambda i: (0, 0))
    wsl = pltpu.VMEM((DFF, D), BF16)
    hid = jax.ShapeDtypeStruct((T, DFF), BF16)
    return pl.pallas_call(
        body, name="ffn_fwd", grid=(T // tm,),
        in_specs=[row, row, vec, vec, ANY_SPEC, ANY_SPEC, ANY_SPEC],
        out_specs=[wide, wide, wide, row, row, pl.BlockSpec((tm, 1), lambda i: (i, 0)), row],
        out_shape=[hid, hid, hid, jax.ShapeDtypeStruct((T, D), F32), jax.ShapeDtypeStruct((T, D), BF16),
                   jax.ShapeDtypeStruct((T, 1), F32), jax.ShapeDtypeStruct((T, D), BF16)],
        scratch_shapes=[wsl, wsl, wsl, pltpu.SemaphoreType.DMA((3,))],
        compiler_params=_cp(("arbitrary",)))(xb, x, gam, bet, wgt, wut, wd)


def _ffn_bwd(dy, xh, r, gam, g, u, wd, wgt, wut, target=None):
    tm, ch = 256, 256

    def body(*refs):
        if target is None:
            (dy_ref, xh_ref, r_ref, gam_ref, g_ref, u_ref, wd_hbm, wg_hbm, wu_hbm,
             dg_ref, du_ref, dsb_ref, dx_ref, dgam_ref, dbet_ref, wd_v, wg_v, wu_v, sem) = refs
        else:
            (dy_ref, t_ref, xh_ref, r_ref, gam_ref, g_ref, u_ref, wd_hbm, wg_hbm, wu_hbm,
             dg_ref, du_ref, dsb_ref, dx_ref, dgam_ref, dbet_ref, sq_ref, wd_v, wg_v, wu_v, sem) = refs
        loads = [pltpu.make_async_copy(s, d, sem.at[k])
                 for k, (s, d) in enumerate(((wd_hbm, wd_v), (wg_hbm, wg_v), (wu_hbm, wu_v)))]

        @pl.when(pl.program_id(0) == 0)
        def _():
            for cp in loads:
                cp.start()
            loads[0].wait()

        if target is None:
            dyv = dy_ref[...]
        else:
            e = dy_ref[...] - t_ref[...]
            dyv = e * (1.0 / D)
            p = jnp.sum(jnp.sum(e * e, axis=1, keepdims=True), axis=0, keepdims=True)

            @pl.when(pl.program_id(0) == 0)
            def _():
                sq_ref[...] = p

            @pl.when(pl.program_id(0) > 0)
            def _():
                sq_ref[...] += p

        xhv = xh_ref[...].astype(F32)
        dxh = dyv * gam_ref[...]
        m1 = jnp.mean(dxh, axis=-1, keepdims=True)
        m2 = jnp.mean(dxh * xhv, axis=-1, keepdims=True)
        ds = r_ref[...] * (dxh - m1 - xhv * m2)
        pg = jnp.sum(dyv * xhv, axis=0, keepdims=True)
        pb = jnp.sum(dyv, axis=0, keepdims=True)

        @pl.when(pl.program_id(0) == 0)
        def _():
            dgam_ref[...] = pg
            dbet_ref[...] = pb

        @pl.when(pl.program_id(0) > 0)
        def _():
            dgam_ref[...] += pg
            dbet_ref[...] += pb

        db = ds.astype(BF16)
        dsb_ref[...] = db
        for c in range(0, DFF, ch):
            da = _dot_nt(db, wd_v[c:c + ch, :])
            gv = g_ref[:, c:c + ch].astype(F32)
            sg = jax.nn.sigmoid(gv)
            dg_ref[:, c:c + ch] = (da * u_ref[:, c:c + ch].astype(F32) * (sg * (1.0 + gv * (1.0 - sg)))).astype(BF16)
            du_ref[:, c:c + ch] = (da * (gv * sg)).astype(BF16)
        @pl.when(pl.program_id(0) == 0)
        def _():
            loads[1].wait()
            loads[2].wait()

        dx_ref[...] = ALPHA * ds + _dot(dg_ref[...], wg_v[...]) + _dot(du_ref[...], wu_v[...])

    row = pl.BlockSpec((tm, D), lambda i: (i, 0))
    wide = pl.BlockSpec((tm, DFF), lambda i: (i, 0))
    vec = pl.BlockSpec((1, D), lambda i: (0, 0))
    wsl = pltpu.VMEM((DFF, D), BF16)
    last = target is not None
    return pl.pallas_call(
        body, name="ffn_bwd_loss" if last else "ffn_bwd", grid=(T // tm,),
        in_specs=[row] + ([row] if last else [])
        + [row, pl.BlockSpec((tm, 1), lambda i: (i, 0)), vec, wide, wide, ANY_SPEC, ANY_SPEC, ANY_SPEC],
        out_specs=[wide, wide, row, row, vec, vec] + ([pl.BlockSpec((1, 1), lambda i: (0, 0))] if last else []),
        out_shape=[jax.ShapeDtypeStruct((T, DFF), BF16), jax.ShapeDtypeStruct((T, DFF), BF16),
                   jax.ShapeDtypeStruct((T, D), BF16), jax.ShapeDtypeStruct((T, D), F32),
                   jax.ShapeDtypeStruct((1, D), F32), jax.ShapeDtypeStruct((1, D), F32)]
        + ([jax.ShapeDtypeStruct((1, 1), F32)] if last else []),
        scratch_shapes=[wsl, wsl, wsl, pltpu.SemaphoreType.DMA((3,))],
        compiler_params=_cp(("arbitrary",)))(dy, *([target] if last else []), xh, r, gam, g, u, wd, wgt, wut)


def _mm_ln(a, w, x, gam, bet, name):
    tm = 256
    K = a.shape[1]

    def body(a_ref, w_ref, x_ref, g_ref, b_ref, y_ref, xh_ref, r_ref, yb_ref):
        s = ALPHA * x_ref[...] + _dot(a_ref[...], w_ref[...])
        mu = jnp.mean(s, axis=-1, keepdims=True)
        xc = s - mu
        var = jnp.mean(xc * xc, axis=-1, keepdims=True)
        r = lax.rsqrt(var + LN_EPS)
        xh = xc * r
        xh_ref[...] = xh.astype(BF16)
        r_ref[...] = r
        y = xh * g_ref[...] + b_ref[...]
        y_ref[...] = y
        yb_ref[...] = y.astype(BF16)

    row = pl.BlockSpec((tm, D), lambda i: (i, 0))
    vec = pl.BlockSpec((1, D), lambda i: (0, 0))
    return pl.pallas_call(
        body, name=name, grid=(T // tm,),
        in_specs=[pl.BlockSpec((tm, K), lambda i: (i, 0)), pl.BlockSpec((K, D), lambda i: (0, 0)), row, vec, vec],
        out_specs=[row, row, pl.BlockSpec((tm, 1), lambda i: (i, 0)), row],
        out_shape=[jax.ShapeDtypeStruct((T, D), F32), jax.ShapeDtypeStruct((T, D), BF16),
                   jax.ShapeDtypeStruct((T, 1), F32), jax.ShapeDtypeStruct((T, D), BF16)],
        compiler_params=_cp(("parallel",)))(a, w, x, gam, bet)


def _ln_bwd(dy, xh, r, gam, w):
    tm = 256

    def body(dy_ref, xh_ref, r_ref, g_ref, w_ref, ds_ref, dg_ref, db_ref, dsb_ref, dm_ref):
        i = pl.program_id(0)
        dyv = dy_ref[...]
        xhv = xh_ref[...].astype(F32)
        dxh = dyv * g_ref[...]
        m1 = jnp.mean(dxh, axis=-1, keepdims=True)
        m2 = jnp.mean(dxh * xhv, axis=-1, keepdims=True)
        ds = r_ref[...] * (dxh - m1 - xhv * m2)
        ds_ref[...] = ds
        dsb = ds.astype(BF16)
        dsb_ref[...] = dsb
        dm_ref[...] = _dot_nt(dsb, w_ref[...]).astype(BF16)
        pg = jnp.sum(dyv * xhv, axis=0, keepdims=True)
        pb = jnp.sum(dyv, axis=0, keepdims=True)

        @pl.when(i == 0)
        def _():
            dg_ref[...] = pg
            db_ref[...] = pb

        @pl.when(i > 0)
        def _():
            dg_ref[...] += pg
            db_ref[...] += pb

    row = pl.BlockSpec((tm, D), lambda i: (i, 0))
    vec = pl.BlockSpec((1, D), lambda i: (0, 0))
    return pl.pallas_call(
        body, name="ln_bwd_proj", grid=(T // tm,),
        in_specs=[row, row, pl.BlockSpec((tm, 1), lambda i: (i, 0)), vec, pl.BlockSpec((D, D), lambda i: (0, 0))],
        out_specs=[row, vec, vec, row, row],
        out_shape=[jax.ShapeDtypeStruct((T, D), F32), jax.ShapeDtypeStruct((1, D), F32),
                   jax.ShapeDtypeStruct((1, D), F32), jax.ShapeDtypeStruct((T, D), BF16),
                   jax.ShapeDtypeStruct((T, D), BF16)],
        compiler_params=_cp(("arbitrary",)))(dy, xh, r, gam, w)


def _adamw(w, g, m, v, tr):
    L, R, C = w.shape

    def body(w_ref, g_ref, m_ref, v_ref, d_ref, m2_ref, v2_ref):
        gv = g_ref[...]
        m2 = ADAM_B1 * m_ref[...] + (1.0 - ADAM_B1) * gv
        v2 = ADAM_B2 * v_ref[...] + (1.0 - ADAM_B2) * (gv * gv)
        m_hat = m2 / (1.0 - ADAM_B1 ** ADAM_STEP)
        v_hat = v2 / (1.0 - ADAM_B2 ** ADAM_STEP)
        d_ref[...] = -ADAM_LR * (m_hat / (jnp.sqrt(v_hat) + ADAM_EPS) + ADAM_WD * w_ref[...])
        m2_ref[...] = m2
        v2_ref[...] = v2

    blk = pl.BlockSpec((None, tr, C), lambda l, i: (l, i, 0))
    sh = jax.ShapeDtypeStruct((L, R, C), F32)
    return pl.pallas_call(
        body, name="adamw", grid=(L, R // tr), in_specs=[blk] * 4, out_specs=[blk] * 3,
        out_shape=[sh, sh, sh], compiler_params=_cp(("parallel", "parallel")))(w, g, m, v)


class _Job:
    def __init__(self, ins, out_shapes, aliases, sems, start, finish):
        self.ins, self.out_shapes, self.aliases, self.sems = list(ins), list(out_shapes), dict(aliases), list(sems)
        self.start, self.finish = start, finish


def _host_call(body, name, ins, in_specs, out_shapes, out_specs, scratch, aliases, job):
    n_in, n_out, n_scr = len(ins), len(out_shapes), len(scratch)
    jins = job.ins if job else []
    jouts = job.out_shapes if job else []
    jsems = job.sems if job else []

    def wrapped(*refs):
        a = n_in
        b = a + len(jins)
        c = b + n_out
        d = c + len(jouts)
        e = d + n_scr
        comm = None
        if job:
            jrefs = (refs[a:b], refs[c:d], refs[e:])
            comm = (lambda: job.start(*jrefs), lambda st: job.finish(st, *jrefs))
        body(refs[:a], refs[b:c], refs[d:e], comm)

    al = dict(aliases)
    if job:
        for ji, jo in job.aliases.items():
            al[n_in + ji] = n_out + jo
    res = pl.pallas_call(
        wrapped, name=name, in_specs=list(in_specs) + [ANY_SPEC] * len(jins),
        out_specs=list(out_specs) + [ANY_SPEC] * len(jouts), out_shape=list(out_shapes) + list(jouts),
        scratch_shapes=list(scratch) + list(jsems), input_output_aliases=al,
        compiler_params=_cp())(*ins, *jins)
    return res[:n_out], res[n_out:]


def _copy_in(src, dst, sem):
    cp = pltpu.make_async_copy(src, dst, sem)
    cp.start()
    cp.wait()


CHAINS = [(p, b) for p in range(2) for b in range(BL)]
NC = len(CHAINS)
ROWS_SHAPE = jax.ShapeDtypeStruct((NSTAT, S), F32)
SLAB_QKV = pltpu.VMEM((T, 2 * PAIRW), BF16)
SLAB_OUT = pltpu.VMEM((T, 2 * BQ), BF16)
SLAB_O32 = pltpu.VMEM((T, 2 * BQ), F32)
SLAB_T = pltpu.VMEM((2, BQ, T), BF16)
SLAB_KEYB = pltpu.VMEM((NSTAT, S, BQ), F32)
ACC_KV = pltpu.VMEM((2, T, BQ), F32)
NTRI = NB * (NB + 1) // 2
A_TILES = jax.ShapeDtypeStruct((NTRI, NC, HB, BQ), BF16)
PAIR_DIAG = pltpu.VMEM((NC, 2, HB, HB), BF16)
A_SLOTS_OUT, A_SLOTS_IN = 2, 4


def _lane_masks():
    lane = _iota((1, BQ), 1)
    m0 = (lane < 64).astype(BF16)
    return m0, 1.0 - m0


def _merge_heads(x, first):
    return jnp.where(first, x[:BQ], x[BQ:])


def _row_masks():
    r = _iota((BQ, 1), 0)
    m0 = (r < 64).astype(BF16)
    return m0, 1.0 - m0


def _stack(x, m0, m1):
    return jnp.concatenate([x * m0, x * m1], axis=0)


def _stack_t(xt, r0, r1):
    return jnp.concatenate([xt * r0, xt * r1], axis=1)


def _tr(x):
    return x.T


def _rows(b, i):
    return pl.ds(pl.multiple_of(b * S + i * BQ, BQ), BQ)


def _transpose_slab(src, dst, col0):
    def blk(n, _):
        r = pl.ds(pl.multiple_of(n * BQ, BQ), BQ)
        for p in range(2):
            dst[p, :, r] = _tr(src[r, col0(p):col0(p) + BQ])
        return 0

    lax.fori_loop(0, T // BQ, blk, 0)


def _heads(x):
    return x[:BQ], x[BQ:]


def _bcast_heads(r0, r1):
    return jnp.concatenate([jnp.broadcast_to(r0, (BQ, BQ)), jnp.broadcast_to(r1, (BQ, BQ))], axis=0)


def _by_channel(r0, r1):
    return jnp.where(_iota((BQ, BQ), 0) < 64, r0, r1)


def _colsum2(x):
    return jnp.sum(x[:BQ], axis=0, keepdims=True), jnp.sum(x[BQ:], axis=0, keepdims=True)


def _stat_row(ref, p, b, h, i):
    c = b * NH + 2 * p + h
    return ref[c:c + 1, pl.ds(pl.multiple_of(i * BQ, BQ), BQ)]


def _put_row(ref, p, b, h, i, v):
    c = b * NH + 2 * p + h
    ref[c:c + 1, pl.ds(pl.multiple_of(i * BQ, BQ), BQ)] = v


def _valid_t(strict):
    r = _iota((HB, BQ), 0) & (BQ - 1)
    c = _iota((HB, BQ), 1)
    return (r < c) if strict else (r <= c)


def _tri_blockdiag(later):
    r = _iota((HB, HB), 0)
    c = _iota((HB, HB), 1)
    same = (r >= BQ) == (c >= BQ)
    return (same & ((c > r) if later else (c < r))).astype(BF16)


def _cum_mm(tri, x):
    y = _dot(tri, _split2(x))
    return y[:, :BQ] + y[:, BQ:]


def _kv_tiles(qkv_v, p, b, j):
    r = _rows(b, j)
    return qkv_v[r, p * PAIRW + BQ:p * PAIRW + 2 * BQ], qkv_v[r, p * PAIRW + 2 * BQ:p * PAIRW + 3 * BQ]


def _q_tile(qkv_v, p, b, i):
    return qkv_v[_rows(b, i), p * PAIRW:p * PAIRW + BQ] * SCALE


def _sb_fwd(qkv, job=None):
    def body(ins, outs, scr, comm):
        (qkv_hbm,), (o_hbm, a_hbm), (qkv_v, o_v, sem, vt_v, a_st, a_sems) = ins, outs, scr
        _copy_in(qkv_hbm.at[:, pl.ds(0, 2 * PAIRW)], qkv_v, sem)
        st = comm[0]() if comm else None
        _transpose_slab(qkv_v, vt_v, lambda p: p * PAIRW + 2 * BQ)
        m0, m1 = _lane_masks()
        r0, r1 = _row_masks()
        valid = _valid_t(True)
        later = _tri_blockdiag(True)

        def a_copy(n, t):
            slot = n % A_SLOTS_OUT
            return pltpu.make_async_copy(a_st.at[slot], a_hbm.at[t], a_sems.at[slot])

        def steps(qts, i, jj, cs, diag):
            j = i - jj
            ks = [_stack(_kv_tiles(qkv_v, p, b, j)[0], m0, m1) for p, b in CHAINS]
            zs = [_dot(ks[c], qts[c]) for c in range(NC)]
            lbs, lrs = [], []
            for c in range(NC):
                lb = _log_sigmoid_tile(zs[c])
                lr = lb - zs[c]
                if diag:
                    lr = jnp.where(valid, lr, 0.0)
                lbs.append(lb)
                lrs.append(lr)
            tails = [_cum_mm(later, lrs[c]) for c in range(NC)]
            avs = []
            for c in range(NC):
                a = jnp.exp(lbs[c] + tails[c] + _bcast_heads(*cs[c][0]))
                if diag:
                    a = jnp.where(valid, a, 0.0)
                avs.append(a.astype(BF16))
            n = (i * (i + 1)) // 2 + jj

            @pl.when(n >= A_SLOTS_OUT)
            def _():
                a_copy(n, 0).wait()
            for c in range(NC):
                a_st[n % A_SLOTS_OUT, c] = avs[c]
            a_copy(n, n - jj + j).start()
            out = []
            for c, (p, b) in enumerate(CHAINS):
                vts = _stack_t(vt_v[p, :, _rows(b, j)], r0, r1)
                s0, s1 = _colsum2(lrs[c])
                out.append(((cs[c][0][0] + s0, cs[c][0][1] + s1), cs[c][1] + _dot(vts, avs[c])))
            return tuple(out)

        def qblock(i, _):
            qts = [_tr(_q_tile(qkv_v, p, b, i)) for p, b in CHAINS]
            zr = jnp.zeros((1, BQ), F32)
            cs = steps(qts, i, 0, (((zr, zr), jnp.zeros((BQ, BQ), F32)),) * NC, True)
            cs = lax.fori_loop(1, i + 1, lambda jj, cs: steps(qts, i, jj, cs, False), cs)
            for c, (p, b) in enumerate(CHAINS):
                o_v[_rows(b, i), p * BQ:(p + 1) * BQ] = cs[c][1].T.astype(BF16)
            return 0

        lax.fori_loop(0, NB, qblock, 0)
        for n in range(NTRI - A_SLOTS_OUT, NTRI):
            a_copy(n, 0).wait()
        _copy_in(o_v, o_hbm.at[:, pl.ds(0, 2 * BQ)], sem)
        if comm:
            comm[1](st)

    (mixed, amat), extra = _host_call(
        body, "sb_fwd", [qkv], [ANY_SPEC], [jax.ShapeDtypeStruct((T, D), BF16), A_TILES], [ANY_SPEC, ANY_SPEC],
        [SLAB_QKV, SLAB_OUT, pltpu.SemaphoreType.DMA, SLAB_T, pltpu.VMEM((A_SLOTS_OUT, NC, HB, BQ), BF16),
         pltpu.SemaphoreType.DMA((A_SLOTS_OUT,))], {}, job)
    return mixed, amat, extra


def _sb_bwd(qkv, dmixed, amat, job=None):
    def body(ins, outs, scr, comm):
        (qkv_hbm, do_hbm, a_hbm), (dqkv_hbm,), (qkv_v, do_v, dq_v, dk_s, dv_s, sems, kt_v, a_st, a_sems, w_v) = ins, outs, scr
        sem = sems.at[0]
        w_v[...] = jnp.zeros_like(w_v)

        def a_copy(t):
            slot = t % A_SLOTS_IN
            return pltpu.make_async_copy(a_hbm.at[t], a_st.at[slot], a_sems.at[slot])

        later = [pltpu.make_async_copy(do_hbm.at[:, pl.ds(0, 2 * BQ)], do_v, sems.at[1])]
        for cp in later:
            cp.start()
        for t in range(A_SLOTS_IN - 1):
            a_copy(t).start()
        _copy_in(qkv_hbm.at[:, pl.ds(0, 2 * PAIRW)], qkv_v, sem)
        st = comm[0]() if comm else None
        _transpose_slab(qkv_v, kt_v, lambda p: p * PAIRW + BQ)
        for cp in later:
            cp.wait()
        m0, m1 = _lane_masks()
        first = _iota((BQ, BQ), 1) < 64
        r0, r1 = _row_masks()
        valid = _valid_t(True)
        earlier = _tri_blockdiag(False)
        dk_s[...] = jnp.zeros_like(dk_s)
        dv_s[...] = jnp.zeros_like(dv_s)

        def steps(i, j, cs, diag):
            t = (i * (i + 1)) // 2 + j
            a_copy(t).wait()

            @pl.when(t + A_SLOTS_IN - 1 < NTRI)
            def _():
                a_copy(t + A_SLOTS_IN - 1).start()
            slot = t % A_SLOTS_IN
            kv = [_kv_tiles(qkv_v, p, b, j) for p, b in CHAINS]
            zd = [_dot(jnp.concatenate([_stack(kv[c][0], m0, m1), _stack(kv[c][1], m0, m1)], axis=1), w_v[c, 0])
                  for c in range(NC)]
            zs = [x[:, :BQ] for x in zd]
            das = [x[:, BQ:] for x in zd]
            avs = [a_st[slot, c] for c in range(NC)]
            gms = [das[c] * avs[c].astype(F32) for c in range(NC)]
            befores = [_dot(earlier, gms[c].astype(BF16)) for c in range(NC)]
            dzbs = []
            for c in range(NC):
                dz = gms[c] - jax.nn.sigmoid(zs[c]) * (gms[c] + befores[c] + _bcast_heads(*cs[c][0]))
                if diag:
                    dz = jnp.where(valid, dz, 0.0)
                dzbs.append(dz.astype(BF16))
            out = []
            for c, (p, b) in enumerate(CHAINS):
                dq = cs[c][1] + _dot(_stack_t(kt_v[p, :, _rows(b, j)], r0, r1), dzbs[c])
                kd = _dot(jnp.concatenate([dzbs[c], avs[c]], axis=1), w_v[c, 1])
                dk_s[p, _rows(b, j), :] += _merge_heads(kd[:, :BQ], first)
                dv_s[p, _rows(b, j), :] += _merge_heads(kd[:, BQ:], first)
                g0, g1 = _colsum2(gms[c])
                out.append(((cs[c][0][0] + g0, cs[c][0][1] + g1), dq))
            return tuple(out)

        def qblock(i, _):
            for c, (p, b) in enumerate(CHAINS):
                qn = _q_tile(qkv_v, p, b, i)
                dn = do_v[_rows(b, i), p * BQ:(p + 1) * BQ]
                for r, (x, y) in enumerate(((_tr(qn), _tr(dn)), (qn, dn))):
                    w_v[c, r, :BQ, :BQ] = x
                    w_v[c, r, BQ:, BQ:] = y
            zr = jnp.zeros((1, BQ), F32)
            cs = (((zr, zr), jnp.zeros((BQ, BQ), F32)),) * NC
            cs = lax.fori_loop(0, i, lambda j, cs: steps(i, j, cs, False), cs)
            cs = steps(i, i, cs, True)
            for c, (p, b) in enumerate(CHAINS):
                dq_v[_rows(b, i), p * PAIRW:p * PAIRW + BQ] = (cs[c][1].T * SCALE).astype(BF16)
            return 0

        lax.fori_loop(0, NB, qblock, 0)
        for p in range(2):
            dq_v[:, p * PAIRW + BQ:p * PAIRW + 2 * BQ] = dk_s[p].astype(BF16)
            dq_v[:, p * PAIRW + 2 * BQ:p * PAIRW + 3 * BQ] = dv_s[p].astype(BF16)
        _copy_in(dq_v, dqkv_hbm.at[:, pl.ds(0, 2 * PAIRW)], sem)
        if comm:
            comm[1](st)

    (dqkv,), extra = _host_call(
        body, "sb_bwd", [qkv, dmixed, amat], [ANY_SPEC, ANY_SPEC, ANY_SPEC],
        [jax.ShapeDtypeStruct((T, QKVW), BF16)], [ANY_SPEC],
        [SLAB_QKV, SLAB_OUT, SLAB_QKV, ACC_KV, ACC_KV, pltpu.SemaphoreType.DMA((4,)), SLAB_T,
         pltpu.VMEM((A_SLOTS_IN, NC, HB, BQ), BF16), pltpu.SemaphoreType.DMA((A_SLOTS_IN,)), PAIR_DIAG], {}, job)
    return dqkv, extra


def _flash_fwd(qkv, mixed, g, fox, bias, job=None):
    def body(ins, outs, scr, comm):
        if fox:
            qkv_hbm, cq_ref, ckb_hbm, _ = ins
            (o_hbm, lse_ref, o32_hbm), (qkv_v, o_v, sem, vt_v, o32_v, ckb_v) = outs, scr
        else:
            qkv_hbm, tbl_ref, _ = ins
            (o_hbm, lse_ref), (qkv_v, o_v, sem, vt_v) = outs, scr
        sems = sem
        sem = sems.at[0]
        later = [pltpu.make_async_copy(ckb_hbm, ckb_v, sems.at[1])] if fox else []
        for cp in later:
            cp.start()
        _copy_in(qkv_hbm.at[:, pl.ds(g * 2 * PAIRW, 2 * PAIRW)], qkv_v, sem)
        st = comm[0]() if comm else None
        _transpose_slab(qkv_v, vt_v, lambda p: p * PAIRW + 2 * BQ)
        for cp in later:
            cp.wait()
        m0, m1 = _lane_masks()
        r0, r1 = _row_masks()
        valid = _valid_t(False)

        def steps(qts, cqs, i, j, cs, diag):
            ks = [_stack(_kv_tiles(qkv_v, p, b, j)[0], m0, m1) for p, b in CHAINS]
            zs = [_dot(ks[c], qts[c]) for c in range(NC)]
            prs, alphas, out = [], [], []
            for c, (p, b) in enumerate(CHAINS):
                (ma, mb), (la, lb_), _ = cs[c]
                if fox:
                    kk = pl.ds(pl.multiple_of(j * BQ, BQ), BQ)
                    col = b * NH + 2 * p
                    z = zs[c] + (cqs[c] - jnp.concatenate([ckb_v[col, kk, :], ckb_v[col + 1, kk, :]], axis=0))
                    if diag:
                        z = jnp.where(valid, z, NEG)
                else:
                    z = zs[c] + tbl_ref[p, i - j]
                za, zb = _heads(z)
                na = jnp.maximum(ma, jnp.max(za, axis=0, keepdims=True))
                nb = jnp.maximum(mb, jnp.max(zb, axis=0, keepdims=True))
                aa, ab = jnp.exp(ma - na), jnp.exp(mb - nb)
                pr = jnp.exp(z - _bcast_heads(na, nb))
                sa, sb = _colsum2(pr)
                prs.append(_split2(pr) if fox else pr.astype(BF16))
                alphas.append((aa, ab))
                out.append(((na, nb), (aa * la + sa, ab * lb_ + sb)))
            pvs = []
            for c, (p, b) in enumerate(CHAINS):
                vts = _stack_t(vt_v[p, :, _rows(b, j)], r0, r1)
                if fox:
                    pvs.append(_dot(vts, prs[c][:, :BQ]) + _dot(vts, prs[c][:, BQ:]))
                else:
                    pvs.append(_dot(vts, prs[c]))
            return tuple((out[c][0], out[c][1], _by_channel(*alphas[c]) * cs[c][2] + pvs[c]) for c in range(NC))

        def qblock(i, _):
            qts = [_tr(_q_tile(qkv_v, p, b, i)) for p, b in CHAINS]
            if fox:
                cqs = [_bcast_heads(_stat_row(cq_ref, p, b, 0, i), _stat_row(cq_ref, p, b, 1, i)) for p, b in CHAINS]
            else:
                cqs = [None] * NC
            ng = jnp.full((1, BQ), NEG, F32)
            zr = jnp.zeros((1, BQ), F32)
            cs = steps(qts, cqs, i, i, (((ng, ng), (zr, zr), jnp.zeros((BQ, BQ), F32)),) * NC, True)
            cs = lax.fori_loop(1, i + 1, lambda jj, cs: steps(qts, cqs, i, i - jj, cs, False), cs)
            for c, (p, b) in enumerate(CHAINS):
                (ma, mb), (la, lb_), acc = cs[c]
                o = (acc / _by_channel(la, lb_)).T
                o_v[_rows(b, i), p * BQ:(p + 1) * BQ] = o.astype(BF16)
                if fox:
                    o32_v[_rows(b, i), p * BQ:(p + 1) * BQ] = o
                _put_row(lse_ref, p, b, 0, i, ma + jnp.log(la))
                _put_row(lse_ref, p, b, 1, i, mb + jnp.log(lb_))
            return 0

        lax.fori_loop(0, NB, qblock, 0)
        _copy_in(o_v, o_hbm.at[:, pl.ds(g * 2 * BQ, 2 * BQ)], sem)
        if fox:
            _copy_in(o32_v, o32_hbm, sem)
        if comm:
            comm[1](st)

    bias_specs = [VMEM_SPEC, ANY_SPEC] if fox else [VMEM_SPEC]
    n_in = 2 + len(bias_specs)
    o32 = [jax.ShapeDtypeStruct((T, 2 * BQ), F32)] if fox else []
    res, extra = _host_call(
        body, "fox_fwd" if fox else "dil_fwd", [qkv, *bias, mixed], [ANY_SPEC] + bias_specs + [ANY_SPEC],
        [jax.ShapeDtypeStruct((T, D), BF16), ROWS_SHAPE] + o32, [ANY_SPEC, VMEM_SPEC] + [ANY_SPEC] * len(o32),
        [SLAB_QKV, SLAB_OUT, pltpu.SemaphoreType.DMA((4,)), SLAB_T] + ([SLAB_O32, SLAB_KEYB] if fox else []),
        {n_in - 1: 0}, job)
    return (*res, extra)


def _flash_bwd(qkv, o, dmixed, lse, dqkv, g, fox, bias, job=None):
    def body(ins, outs, scr, comm):
        if fox:
            qkv_hbm, o_hbm, do_hbm, lse_ref, cq_ref, ckb_hbm, _ = ins
            (dqkv_hbm, db_ref), (qkv_v, o_v, do_v, dq_v, dk_s, dv_s, sem, kt_v, ckb_v, dc_s) = outs, scr
        else:
            qkv_hbm, o_hbm, do_hbm, lse_ref, tbl_ref, _ = ins
            (dqkv_hbm, db_ref), (qkv_v, o_v, do_v, dq_v, dk_s, dv_s, sem, kt_v) = outs, scr
        sems = sem
        sem = sems.at[0]
        later = [pltpu.make_async_copy(do_hbm.at[:, pl.ds(g * 2 * BQ, 2 * BQ)], do_v, sems.at[1])]
        if fox:
            later += [pltpu.make_async_copy(o_hbm, o_v, sems.at[2]), pltpu.make_async_copy(ckb_hbm, ckb_v, sems.at[3])]
        else:
            later += [pltpu.make_async_copy(o_hbm.at[:, pl.ds(g * 2 * BQ, 2 * BQ)], o_v, sems.at[2])]
        for cp in later:
            cp.start()
        _copy_in(qkv_hbm.at[:, pl.ds(g * 2 * PAIRW, 2 * PAIRW)], qkv_v, sem)
        st = comm[0]() if comm else None
        _transpose_slab(qkv_v, kt_v, lambda p: p * PAIRW + BQ)
        for cp in later:
            cp.wait()
        m0, m1 = _lane_masks()
        first = _iota((BQ, BQ), 1) < 64
        r0, r1 = _row_masks()
        valid = _valid_t(False)
        dk_s[...] = jnp.zeros_like(dk_s)
        dv_s[...] = jnp.zeros_like(dv_s)
        if fox:
            dc_s[...] = jnp.zeros_like(dc_s)
        else:
            db_ref[...] = jnp.zeros_like(db_ref)

        def steps(qns, qts, dns, dts, cqs, lses, deltas, i, j, dqs, diag):
            kv = [_kv_tiles(qkv_v, p, b, j) for p, b in CHAINS]
            ks = [_stack(kv[c][0], m0, m1) for c in range(NC)]
            vs = [_stack(kv[c][1], m0, m1) for c in range(NC)]
            zs = [_dot(ks[c], qts[c]) for c in range(NC)]
            dps = [_dot(vs[c], dts[c]) for c in range(NC)]
            prs, dzl = [], []
            for c, (p, b) in enumerate(CHAINS):
                if fox:
                    kk = pl.ds(pl.multiple_of(j * BQ, BQ), BQ)
                    col = b * NH + 2 * p
                    z = zs[c] + (cqs[c] - jnp.concatenate([ckb_v[col, kk, :], ckb_v[col + 1, kk, :]], axis=0))
                    if diag:
                        z = jnp.where(valid, z, NEG)
                else:
                    z = zs[c] + tbl_ref[p, i - j]
                pr = jnp.exp(z - lses[c])
                prs.append(pr.astype(BF16))
                dzl.append(pr * (dps[c] - deltas[c]))
            dzbs = [dz.astype(BF16) for dz in dzl]
            new = []
            for c, (p, b) in enumerate(CHAINS):
                new.append(dqs[c] + _dot(_stack_t(kt_v[p, :, _rows(b, j)], r0, r1), dzbs[c]))
                dk = _dot(dzbs[c], qns[c])
                dv = _dot(prs[c], dns[c])
                dk_s[p, _rows(b, j), :] += _merge_heads(dk, first)
                dv_s[p, _rows(b, j), :] += _merge_heads(dv, first)
                if fox:
                    dc_s[c, pl.ds(pl.multiple_of(j * HB, HB), HB), :] += dzl[c]
            if not fox:
                for p in range(2):
                    db_ref[p, i - j] = db_ref[p, i - j] + (dzl[2 * p] + dzl[2 * p + 1])
            return tuple(new)

        def qblock(i, _):
            qns = [_q_tile(qkv_v, p, b, i) for p, b in CHAINS]
            dns = [do_v[_rows(b, i), p * BQ:(p + 1) * BQ] for p, b in CHAINS]
            qts = [_tr(t) for t in qns]
            dts = [_tr(t) for t in dns]
            lses = [_bcast_heads(_stat_row(lse_ref, p, b, 0, i), _stat_row(lse_ref, p, b, 1, i)) for p, b in CHAINS]
            if fox:
                cqs = [_bcast_heads(_stat_row(cq_ref, p, b, 0, i), _stat_row(cq_ref, p, b, 1, i)) for p, b in CHAINS]
            else:
                cqs = [None] * NC
            deltas = []
            for c, (p, b) in enumerate(CHAINS):
                pt = (dns[c].astype(F32) * o_v[_rows(b, i), p * BQ:(p + 1) * BQ].astype(F32)).T
                deltas.append(_bcast_heads(jnp.sum(pt[:64], axis=0, keepdims=True), jnp.sum(pt[64:], axis=0, keepdims=True)))
            dqs = (jnp.zeros((BQ, BQ), F32),) * NC
            dqs = lax.fori_loop(0, i, lambda j, d: steps(qns, qts, dns, dts, cqs, lses, deltas, i, j, d, False), dqs)
            dqs = steps(qns, qts, dns, dts, cqs, lses, deltas, i, i, dqs, True)
            for c, (p, b) in enumerate(CHAINS):
                dq_v[_rows(b, i), p * PAIRW:p * PAIRW + BQ] = (dqs[c].T * SCALE).astype(BF16)
            return 0

        lax.fori_loop(0, NB, qblock, 0)
        for p in range(2):
            dq_v[:, p * PAIRW + BQ:p * PAIRW + 2 * BQ] = dk_s[p].astype(BF16)
            dq_v[:, p * PAIRW + 2 * BQ:p * PAIRW + 3 * BQ] = dv_s[p].astype(BF16)
        _copy_in(dq_v, dqkv_hbm.at[:, pl.ds(g * 2 * PAIRW, 2 * PAIRW)], sem)
        if fox:
            lane = _iota((BQ, NSTAT), 1)

            def fold(n, _):
                t = jnp.zeros((BQ, NSTAT), F32)
                for c, (p, b) in enumerate(CHAINS):
                    s = jnp.sum(dc_s[c, pl.ds(pl.multiple_of(n * HB, HB), HB), :], axis=1, keepdims=True)
                    col = b * NH + 2 * p
                    t = t - jnp.where(lane == col, s[:BQ], 0.0) - jnp.where(lane == col + 1, s[BQ:], 0.0)
                db_ref[pl.ds(pl.multiple_of(n * BQ, BQ), BQ), :] = t
                return 0

            lax.fori_loop(0, NB, fold, 0)
        if comm:
            comm[1](st)

    if fox:
        bias_specs = [VMEM_SPEC, ANY_SPEC]
        db_shape = jax.ShapeDtypeStruct((S, NSTAT), F32)
        more = [SLAB_KEYB, pltpu.VMEM((NC, NB * HB, BQ), F32)]
    else:
        bias_specs = [VMEM_SPEC]
        db_shape = jax.ShapeDtypeStruct((2, NB, HB, BQ), F32)
        more = []
    n_in = 5 + len(bias_specs)
    (dqkv, db), extra = _host_call(
        body, "fox_bwd" if fox else "dil_bwd", [qkv, o, dmixed, lse, *bias, dqkv],
        [ANY_SPEC, ANY_SPEC, ANY_SPEC, VMEM_SPEC] + bias_specs + [ANY_SPEC],
        [jax.ShapeDtypeStruct((T, QKVW), BF16), db_shape], [ANY_SPEC, VMEM_SPEC],
        [SLAB_QKV, SLAB_O32 if fox else SLAB_OUT, SLAB_OUT, SLAB_QKV, ACC_KV, ACC_KV, pltpu.SemaphoreType.DMA((4,)), SLAB_T]
        + more, {n_in - 1: 0}, job)
    return dqkv, db, extra


def _delta_t(d):
    return d * BQ + _iota((HB, BQ), 1) - (_iota((HB, BQ), 0) & (BQ - 1))


def _buckets_in(d):
    lo, hi = max(d * BQ - (BQ - 1), 0), d * BQ + BQ - 1
    return [b for b in range(32) if BUCKET_TH[b] <= hi and (b == 31 or BUCKET_TH[b + 1] > lo)]


def _in_bucket(delta, b):
    m = delta >= BUCKET_TH[b]
    return m if b == 31 else m & (delta < BUCKET_TH[b + 1])


def _dil_table(rel_bias, job=None):
    def body(ins, outs, scr, comm):
        (rb_ref,), (o_ref,) = ins, outs
        st = comm[0]() if comm else None
        for d in range(NB):
            delta = _delta_t(d)
            pos = delta >= 0
            n = ((pos & (delta <= 128)).astype(jnp.int32)
                 + (pos & (delta <= 512) & ((delta & 3) == 0)).astype(jnp.int32)
                 + (pos & ((delta & 15) == 0)).astype(jnp.int32))
            logn = jnp.where(n == 3, math.log(3.0), jnp.where(n == 2, math.log(2.0), jnp.where(n == 1, 0.0, NEG)))
            head1 = _iota((HB, BQ), 0) >= BQ
            for p in range(2):
                val = jnp.zeros((HB, BQ), F32)
                for b in _buckets_in(d):
                    val = jnp.where(_in_bucket(delta, b), jnp.where(head1, rb_ref[b, 2 * p + 1], rb_ref[b, 2 * p]), val)
                o_ref[p, d] = val + logn
        if comm:
            comm[1](st)

    (tbl,), extra = _host_call(
        body, "dil_table", [rel_bias], [pl.BlockSpec(memory_space=pltpu.SMEM)],
        [jax.ShapeDtypeStruct((2, NB, HB, BQ), F32)], [VMEM_SPEC], [], {}, job)
    return (tbl, extra) if job else tbl


def _dil_table_bwd(dtbl):
    def body(dt_ref, o_ref):
        p = pl.program_id(0)
        rowi = _iota((32, BQ), 0)
        lanei = _iota((32, BQ), 1)

        @pl.when(p == 0)
        def _():
            o_ref[...] = jnp.zeros_like(o_ref)

        out = jnp.zeros((32, BQ), F32)
        for b in range(32):
            acc = None
            for d in range(NB):
                if b in _buckets_in(d):
                    t = jnp.where(_in_bucket(_delta_t(d), b), dt_ref[d], 0.0)
                    acc = t if acc is None else acc + t
            rs = jnp.sum(acc, axis=1, keepdims=True)
            s0 = jnp.sum(rs[:BQ], axis=0, keepdims=True)
            s1 = jnp.sum(rs[BQ:], axis=0, keepdims=True)
            out = (out + jnp.where((rowi == b) & (lanei == 2 * p), s0, 0.0)
                   + jnp.where((rowi == b) & (lanei == 2 * p + 1), s1, 0.0))
        o_ref[...] += out

    return pl.pallas_call(
        body, name="dil_table_bwd", grid=(2,),
        in_specs=[pl.BlockSpec((None, NB, HB, BQ), lambda p: (p, 0, 0, 0))],
        out_specs=pl.BlockSpec((32, BQ), lambda p: (0, 0)),
        out_shape=jax.ShapeDtypeStruct((32, BQ), F32),
        compiler_params=_cp(("arbitrary",)))(dtbl)


def _fox_prep(gate, fb):
    def body(g_ref, fb_ref, c_ref):
        tri = (_iota((BQ, BQ), 0) >= _iota((BQ, BQ), 1)).astype(BF16)

        def blk(i, carry):
            r0 = pl.multiple_of(i * BQ, BQ)
            lf = _log_sigmoid(g_ref[pl.ds(r0, BQ), :] + fb_ref[...])
            c = _dot(tri, _split3(lf))
            c_ref[pl.ds(r0, BQ), :] = c[:, 0:BQ] + c[:, BQ:2 * BQ] + c[:, 2 * BQ:3 * BQ] + carry
            return carry + jnp.sum(lf, axis=0, keepdims=True)

        lax.fori_loop(0, NB, blk, jnp.zeros((1, BQ), F32))

    blk = pl.BlockSpec((S, GATEW), lambda b: (b, 0))
    return pl.pallas_call(
        body, name="fox_prep", grid=(BL,), in_specs=[blk, pl.BlockSpec((1, GATEW), lambda b: (0, 0))],
        out_specs=blk, out_shape=jax.ShapeDtypeStruct((T, GATEW), F32),
        compiler_params=_cp(("parallel",)))(gate, fb)


def _fox_post(dcum, gate, fb):
    def body(dc_ref, g_ref, fb_ref, dg_ref, dfb_ref):
        b = pl.program_id(0)
        tri = (_iota((BQ, BQ), 0) <= _iota((BQ, BQ), 1)).astype(BF16)

        def blk(ii, carry):
            csum, dfb = carry
            r0 = pl.multiple_of((NB - 1 - ii) * BQ, BQ)
            dc = dc_ref[pl.ds(r0, BQ), :]
            c = _dot(tri, _split3(dc))
            dlf = c[:, 0:BQ] + c[:, BQ:2 * BQ] + c[:, 2 * BQ:3 * BQ] + csum
            dg = dlf * jnp.exp(_log_sigmoid(-(g_ref[pl.ds(r0, BQ), :] + fb_ref[...])))
            dg_ref[pl.ds(r0, BQ), :] = dg
            return csum + jnp.sum(dc, axis=0, keepdims=True), dfb + jnp.sum(dg, axis=0, keepdims=True)

        z = jnp.zeros((1, BQ), F32)
        _, dfb = lax.fori_loop(0, NB, blk, (z, z))

        @pl.when(b == 0)
        def _():
            dfb_ref[...] = dfb

        @pl.when(b > 0)
        def _():
            dfb_ref[...] += dfb

    blk = pl.BlockSpec((S, GATEW), lambda b: (b, 0))
    vec = pl.BlockSpec((1, GATEW), lambda b: (0, 0))
    return pl.pallas_call(
        body, name="fox_post", grid=(BL,), in_specs=[blk, blk, vec], out_specs=[blk, vec],
        out_shape=[jax.ShapeDtypeStruct((T, GATEW), F32), jax.ShapeDtypeStruct((1, GATEW), F32)],
        compiler_params=_cp(("arbitrary",)))(dcum, gate, fb)


def _shift_down(x, n):
    return jnp.where(_iota(x.shape, 0) >= n, pltpu.roll(x, n, 0), 0.0)


def _shift_up(x, n):
    return jnp.where(_iota(x.shape, 0) < S - n, pltpu.roll(x, S - n, 0), 0.0)


def _conv_fwd(conv, cw, mixed):
    W = 256

    def body(c_ref, w_ref, _, o_ref):
        u = c_ref[:, W:2 * W] * c_ref[:, 2 * W:3 * W]
        y = w_ref[0:1, :] * _shift_down(u, 2) + w_ref[1:2, :] * _shift_down(u, 1) + w_ref[2:3, :] * u
        o_ref[...] = (c_ref[:, 0:W] * y).astype(BF16)

    return pl.pallas_call(
        body, name="conv_fwd", grid=(BL,),
        in_specs=[pl.BlockSpec((S, CONVW), lambda b: (b, 0)), pl.BlockSpec((8, W), lambda b: (0, 0)), ANY_SPEC],
        out_specs=pl.BlockSpec((S, W), lambda b: (b, 3)),
        out_shape=jax.ShapeDtypeStruct((T, D), BF16), input_output_aliases={2: 0},
        compiler_params=_cp(("parallel",)))(conv, cw, mixed)


def _conv_bwd(conv, cw, dmixed):
    W = 256

    def body(c_ref, w_ref, do_ref, dc_ref, dw_ref):
        b = pl.program_id(0)
        bg = c_ref[:, 0:W]
        cg = c_ref[:, W:2 * W]
        hv = c_ref[:, 2 * W:3 * W]
        do = do_ref[...].astype(F32)
        u = cg * hv
        u1 = _shift_down(u, 1)
        u2 = _shift_down(u, 2)
        y = w_ref[0:1, :] * u2 + w_ref[1:2, :] * u1 + w_ref[2:3, :] * u
        dy = do * bg
        du = w_ref[2:3, :] * dy + w_ref[1:2, :] * _shift_up(dy, 1) + w_ref[0:1, :] * _shift_up(dy, 2)
        dc_ref[:, 0:W] = (do * y).astype(BF16)
        dc_ref[:, W:2 * W] = (du * hv).astype(BF16)
        dc_ref[:, 2 * W:3 * W] = (du * cg).astype(BF16)
        rowi = _iota((8, W), 0)
        dw = (jnp.where(rowi == 0, jnp.sum(dy * u2, axis=0, keepdims=True), 0.0)
              + jnp.where(rowi == 1, jnp.sum(dy * u1, axis=0, keepdims=True), 0.0)
              + jnp.where(rowi == 2, jnp.sum(dy * u, axis=0, keepdims=True), 0.0))

        @pl.when(b == 0)
        def _():
            dw_ref[...] = dw

        @pl.when(b > 0)
        def _():
            dw_ref[...] += dw

    return pl.pallas_call(
        body, name="conv_bwd", grid=(BL,),
        in_specs=[pl.BlockSpec((S, CONVW), lambda b: (b, 0)), pl.BlockSpec((8, W), lambda b: (0, 0)),
                  pl.BlockSpec((S, W), lambda b: (b, 3))],
        out_specs=[pl.BlockSpec((S, CONVW), lambda b: (b, 0)), pl.BlockSpec((8, W), lambda b: (0, 0))],
        out_shape=[jax.ShapeDtypeStruct((T, CONVW), BF16), jax.ShapeDtypeStruct((8, W), F32)],
        compiler_params=_cp(("arbitrary",)))(conv, cw, dmixed)


def _place():
    x, y, c = lax.axis_index("x"), lax.axis_index("y"), lax.axis_index("c")
    return x, y, c


def _chips_of(x, y):
    return [(1 - x, y), (x, 1 - y), (1 - x, 1 - y)]


def _dev(p):
    return 4 * p[0] + 2 * p[1] + p[2]


def _gather_job_a(shards):
    n = len(shards)

    def peers(x, y, c):
        return [(x, y, 1 - c)] + [(*chip, c) for chip in _chips_of(x, y)]

    def start(ins, outs, sems):
        send, recv, loc = sems
        x, y, c = _place()
        me = (x, y, c)
        cps = []
        for a in range(n):
            cps.append(pltpu.make_async_copy(ins[a], outs[a].at[_dev(me)], loc.at[a]))
            for k, peer in enumerate(peers(x, y, c)):
                cps.append(pltpu.make_async_remote_copy(
                    src_ref=ins[a], dst_ref=outs[a].at[_dev(me)], send_sem=send.at[a, k], recv_sem=recv.at[a, k],
                    device_id=peer, device_id_type=MESH))
        for cp in cps:
            cp.start()
        return cps

    def finish(cps, ins, outs, sems):
        send, recv, loc = sems
        x, y, c = _place()
        for a in range(n):
            for k, peer in enumerate(peers(x, y, c)):
                pltpu.make_async_remote_copy(
                    src_ref=ins[a], dst_ref=outs[a].at[_dev(peer)], send_sem=send.at[a, k], recv_sem=recv.at[a, k],
                    device_id=(x, y, c), device_id_type=MESH).wait_recv()
        for a in range(n):
            cps[5 * a].wait()
            for k in range(4):
                cps[5 * a + 1 + k].wait_send()

    return _Job(shards, [jax.ShapeDtypeStruct((NDEV,) + s.shape, s.dtype) for s in shards], {},
                [pltpu.SemaphoreType.DMA((n, 4)), pltpu.SemaphoreType.DMA((n, 4)), pltpu.SemaphoreType.DMA((n,))],
                start, finish)


def _gather_job_b(gathered):
    n = len(gathered)

    def start(ins, outs, sems):
        send, recv = sems
        x, y, c = _place()
        cps = []
        for a in range(n):
            for j, chip in enumerate(_chips_of(x, y)):
                blk = outs[a].at[_dev((*chip, c))]
                cps.append(pltpu.make_async_remote_copy(
                    src_ref=blk, dst_ref=blk, send_sem=send.at[a, j], recv_sem=recv.at[a, j],
                    device_id=(x, y, 1 - c), device_id_type=MESH))
        for cp in cps:
            cp.start()
        return cps

    def finish(cps, ins, outs, sems):
        send, recv = sems
        x, y, c = _place()
        for a in range(n):
            for j, chip in enumerate(_chips_of(x, y)):
                blk = outs[a].at[_dev((*chip, 1 - c))]
                pltpu.make_async_remote_copy(
                    src_ref=blk, dst_ref=blk, send_sem=send.at[a, j], recv_sem=recv.at[a, j],
                    device_id=(x, y, c), device_id_type=MESH).wait_recv()
        for cp in cps:
            cp.wait_send()

    return _Job(gathered, [jax.ShapeDtypeStruct(g.shape, g.dtype) for g in gathered], {a: a for a in range(n)},
                [pltpu.SemaphoreType.DMA((n, 3)), pltpu.SemaphoreType.DMA((n, 3))], start, finish)


def _sibling_job(grads):
    n = len(grads)

    def start(ins, outs, sems):
        send, recv = sems
        x, y, c = _place()
        cps = [pltpu.make_async_remote_copy(
            src_ref=ins[a].at[:, 1 - c], dst_ref=outs[a], send_sem=send.at[a], recv_sem=recv.at[a],
            device_id=(x, y, 1 - c), device_id_type=MESH) for a in range(n)]
        for cp in cps:
            cp.start()
        return cps

    def finish(cps, ins, outs, sems):
        for cp in cps:
            cp.wait()

    return _Job(grads, [jax.ShapeDtypeStruct(g.shape[:1] + g.shape[2:], F32) for g in grads], {},
                [pltpu.SemaphoreType.DMA((n,)), pltpu.SemaphoreType.DMA((n,))], start, finish)


def _chip_job(psums):
    n = len(psums)

    def copies(ins, outs, sems):
        send, recv, loc = sems
        x, y, c = _place()
        mychip = 2 * x + y
        cps = []
        for a in range(n):
            cps.append(pltpu.make_async_copy(ins[a].at[mychip], outs[a].at[mychip], loc.at[a]))
            for j, chip in enumerate(_chips_of(x, y)):
                cps.append(pltpu.make_async_remote_copy(
                    src_ref=ins[a].at[2 * chip[0] + chip[1]], dst_ref=outs[a].at[mychip],
                    send_sem=send.at[a, j], recv_sem=recv.at[a, j], device_id=(*chip, c), device_id_type=MESH))
        return cps

    def start(ins, outs, sems):
        for cp in copies(ins, outs, sems):
            cp.start()

    def finish(_, ins, outs, sems):
        cps = copies(ins, outs, sems)
        send, recv, loc = sems
        x, y, c = _place()
        mychip = 2 * x + y
        for a in range(n):
            for j, chip in enumerate(_chips_of(x, y)):
                pltpu.make_async_remote_copy(
                    src_ref=ins[a].at[mychip], dst_ref=outs[a].at[2 * chip[0] + chip[1]],
                    send_sem=send.at[a, j], recv_sem=recv.at[a, j], device_id=(x, y, c), device_id_type=MESH).wait_recv()
        for a in range(n):
            cps[4 * a].wait()
            for j in range(3):
                cps[4 * a + 1 + j].wait_send()

    return _Job(psums, [jax.ShapeDtypeStruct(p.shape, BF16) for p in psums], {},
                [pltpu.SemaphoreType.DMA((n, 3)), pltpu.SemaphoreType.DMA((n, 3)), pltpu.SemaphoreType.DMA((n,))],
                start, finish)


def _join_jobs(*jobs):
    jobs = [j for j in jobs if j is not None]
    if len(jobs) <= 1:
        return jobs[0] if jobs else None
    cut = lambda seq, sizes: [seq[sum(sizes[:k]):sum(sizes[:k + 1])] for k in range(len(sizes))]
    n_in = [len(j.ins) for j in jobs]
    n_out = [len(j.out_shapes) for j in jobs]
    n_sem = [len(j.sems) for j in jobs]
    aliases = {}
    for k, j in enumerate(jobs):
        for a, b in j.aliases.items():
            aliases[sum(n_in[:k]) + a] = sum(n_out[:k]) + b

    def start(ins, outs, sems):
        return [j.start(i, o, s) for j, i, o, s in zip(jobs, cut(ins, n_in), cut(outs, n_out), cut(sems, n_sem))]

    def finish(sts, ins, outs, sems):
        for j, st, i, o, s in zip(jobs, sts, cut(ins, n_in), cut(outs, n_out), cut(sems, n_sem)):
            j.finish(st, i, o, s)

    return _Job([t for j in jobs for t in j.ins], [t for j in jobs for t in j.out_shapes], aliases,
                [t for j in jobs for t in j.sems], start, finish)


def _run_job(job, name):
    def body(ins, outs, scr, comm):
        comm[1](comm[0]())

    return _host_call(body, name, [], [], [], [], [], {}, job)[1]


def _allreduce_small(v, job=None):
    def body(ins, outs, scr, comm):
        (v_ref,), (o_ref,), (slots, send_sems, recv_sems) = ins, outs, scr
        st = comm[0]() if comm else None
        x, y, c = _place()
        me = 4 * x + 2 * y + c
        slots[me] = v_ref[...]

        def copy(k):
            peer = (x ^ ((k >> 2) & 1), y ^ ((k >> 1) & 1), c ^ (k & 1))
            return pltpu.make_async_remote_copy(
                src_ref=v_ref, dst_ref=slots.at[me], send_sem=send_sems.at[k - 1], recv_sem=recv_sems.at[k - 1],
                device_id=peer, device_id_type=MESH)

        def arrival(k):
            return pltpu.make_async_remote_copy(
                src_ref=v_ref, dst_ref=slots.at[me ^ k], send_sem=send_sems.at[k - 1], recv_sem=recv_sems.at[k - 1],
                device_id=(x, y, c), device_id_type=MESH)

        sends = [copy(k) for k in range(1, NDEV)]
        for cp in sends:
            cp.start()
        for k in range(1, NDEV):
            arrival(k).wait_recv()
        for cp in sends:
            cp.wait_send()
        acc = slots[0]
        for d in range(1, NDEV):
            acc = acc + slots[d]
        o_ref[...] = acc
        if comm:
            comm[1](st)

    (out,), extra = _host_call(
        body, "allreduce_small", [v], [VMEM_SPEC], [jax.ShapeDtypeStruct(v.shape, F32)], [VMEM_SPEC],
        [pltpu.VMEM((NDEV,) + v.shape, F32), pltpu.SemaphoreType.DMA((NDEV - 1,)),
         pltpu.SemaphoreType.DMA((NDEV - 1,))], {}, job)
    return (out, extra) if job else out


def _pair_sums(views, gots, core):
    n = len(views)

    def body(c_ref, *refs):
        for a in range(n):
            refs[2 * n + a][...] = (refs[a][...] + refs[n + a][...]).astype(BF16)

    def vspec(v):
        return pl.BlockSpec((None, None, v.shape[2] // 2, v.shape[3]), lambda k, h, c: (k, c[0], h, 0))

    def gspec(g):
        return pl.BlockSpec((None, g.shape[1] // 2, g.shape[2]), lambda k, h, c: (k, h, 0))

    return pl.pallas_call(
        body, name="pair_sums",
        grid_spec=pltpu.PrefetchScalarGridSpec(
            num_scalar_prefetch=1, grid=(4, 2),
            in_specs=[vspec(v) for v in views] + [gspec(g) for g in gots],
            out_specs=[gspec(g) for g in gots]),
        out_shape=[jax.ShapeDtypeStruct(g.shape, BF16) for g in gots],
        compiler_params=_cp(("parallel", "parallel")))(core, *views, *gots)


def _chip_sums(parts):
    n = len(parts)

    def body(*refs):
        for a in range(n):
            acc = refs[a][0].astype(F32)
            for k in range(1, 4):
                acc = acc + refs[a][k].astype(F32)
            refs[n + a][...] = acc

    return pl.pallas_call(
        body, name="chip_sums", in_specs=[VMEM_SPEC] * n, out_specs=[VMEM_SPEC] * n,
        out_shape=[jax.ShapeDtypeStruct(p.shape[1:], F32) for p in parts], compiler_params=_cp())(*parts)


def _permute_in(w):
    lead = w.shape[:-1]
    return w.reshape(lead + (3, 3, 2, BQ)).swapaxes(-2, -3).reshape(lead + (QKVW,))


def _unpermute_in(w):
    lead = w.shape[:-1]
    return w.reshape(lead + (3, 2, 3, BQ)).swapaxes(-2, -3).reshape(lead + (QKVW,))


def _row(v):
    v = v.reshape(-1)
    return jnp.pad(v, (0, D - v.shape[0])).reshape(1, D)


def kernel(x, w_in, f_bias, conv_w, w_out, rel_bias, ln1_g, ln1_b, w_gate, w_up, w_down, ln2_g, ln2_b, loss_target, m_w_in, m_f_bias, m_conv_w, m_w_out, m_rel_bias, m_ln1_g, m_ln1_b, m_w_gate, m_w_up, m_w_down, m_ln2_g, m_ln2_b, v_w_in, v_f_bias, v_conv_w, v_w_out, v_rel_bias, v_ln1_g, v_ln1_b, v_w_gate, v_w_up, v_w_down, v_ln2_g, v_ln2_b):
    xi, yi, ci = _place()
    me = 4 * xi + 2 * yi + ci
    core = jnp.reshape(ci, (1,)).astype(jnp.int32)

    win_s = jnp.concatenate([_permute_in(w_in[..., :QKVW]), w_in[..., QKVW:]], axis=-1)
    win_s = jnp.pad(win_s, ((0, 0), (0, 0), (0, NPAD - NPROJ))).astype(BF16)
    per_layer = [win_s, w_out.astype(BF16), jnp.swapaxes(w_gate, 1, 2).astype(BF16),
                 jnp.swapaxes(w_up, 1, 2).astype(BF16), w_down.astype(BF16)]
    sh = [[s[l] for s in per_layer] for l in range(2)]

    def whole(g):
        return g.reshape(NDEV * g.shape[1], g.shape[2])

    cw_rows = lax.dynamic_update_slice(jnp.zeros((2, 3, 256), F32), conv_w, (0, 0, me * 32))
    small = jnp.concatenate([_row(cw_rows[0]), _row(cw_rows[1]), jnp.zeros((SMALL_ROWS - 2, D), F32)], axis=0)
    small, leg_a = _allreduce_small(small, job=_gather_job_a(sh[0][:1]))
    cw_full = small[0:2, :CONVW].reshape(2, 3, 256)
    cw8 = jnp.pad(cw_full, ((0, 0), (0, 5), (0, 0)))
    fb = jnp.pad(f_bias, ((0, 0), (0, GATEW - NH))).reshape(2, 1, GATEW)
    tbl, leg_b = _dil_table(rel_bias, job=_gather_job_b(list(leg_a)))
    W = [{"win": whole(leg_b[0])}, {}]

    def wrow(tn, K, blk=0):
        return pl.BlockSpec((tn, K), lambda i, j: (j, blk))

    def arow(tm, K, blk=0):
        return pl.BlockSpec((tm, K), lambda i, j: (i, blk))

    h = x.reshape(T, D)
    hb = h.astype(BF16)
    saved = []
    for l in range(2):
        Win = W[l]["win"]
        qkv, conv, gate = _proj(hb, Win)
        cum = _fox_prep(gate, fb[l])
        cq = cum[:, :NH].reshape(BL, S, NH).transpose(0, 2, 1).reshape(NSTAT, S)
        ckb = jnp.broadcast_to(cq[:, :, None], (NSTAT, S, BQ))
        if l == 0:
            mixed, amat, a0 = _sb_fwd(qkv, job=_gather_job_a(sh[0][1:]))
            mixed, lse_d, ex = _flash_fwd(qkv, mixed, 1, False, (tbl,),
                                          job=_join_jobs(_gather_job_b(list(a0)), _gather_job_a(sh[1][:2])))
            W[0].update(zip(("wout", "wgT", "wuT", "wd"), [whole(t) for t in ex[:4]]))
            mixed, lse_f, o_fox, ex = _flash_fwd(qkv, mixed, 2, True, (cq, ckb),
                                                 job=_join_jobs(_gather_job_b(list(ex[4:])), _gather_job_a(sh[1][2:])))
            W[1].update(zip(("win", "wout"), [whole(t) for t in ex[:2]]))
            a2 = list(ex[2:])
        else:
            mixed, amat, ex = _sb_fwd(qkv, job=_gather_job_b(a2))
            W[1].update(zip(("wgT", "wuT", "wd"), [whole(t) for t in ex]))
            mixed, lse_d, _ = _flash_fwd(qkv, mixed, 1, False, (tbl,))
            mixed, lse_f, o_fox, _ = _flash_fwd(qkv, mixed, 2, True, (cq, ckb))
        Wout, WgT, WuT, Wd = W[l]["wout"], W[l]["wgT"], W[l]["wuT"], W[l]["wd"]
        mixed = _conv_fwd(conv, cw8[l], mixed)
        x1, xh1, r1, x1b = _mm_ln(mixed, Wout, h, ln1_g[l:l + 1], ln1_b[l:l + 1], "out_proj_ln")
        fs, ft, a, x2, xh2, r2, x2b = _ffn_fwd(x1b, x1, WgT, WuT, Wd, ln2_g[l:l + 1], ln2_b[l:l + 1])
        saved.append(dict(h=hb, qkv=qkv, conv=conv, gate=gate, cq=cq, ckb=ckb, mixed=mixed, amat=amat, lse_d=lse_d,
                          lse_f=lse_f, o_fox=o_fox, x1=x1b, xh1=xh1, r1=r1, fs=fs, ft=ft, a=a, xh2=xh2, r2=r2))
        h, hb = x2, x2b

    dy = h

    def view(gr):
        return gr.reshape(4, 2, gr.shape[0] // NDEV, gr.shape[1])

    G = [None, None]
    small_g = {}
    shard_g = {}
    for l in (1, 0):
        sv = saved[l]
        Win, Wout, WgT, WuT, Wd = W[l]["win"], W[l]["wout"], W[l]["wgT"], W[l]["wuT"], W[l]["wd"]
        res = _ffn_bwd(dy, sv["xh2"], sv["r2"], ln2_g[l:l + 1], sv["fs"], sv["ft"], Wd, WgT, WuT,
                       target=loss_target.reshape(T, D) if l == 1 else None)
        dgt, dut, ds2b, dx1, dg2, db2 = res[:6]
        if l == 1:
            sq = res[6]
        G_d = _mm_tn(sv["a"], ds2b, None, C=D, Ka=DFF, N=D, tm=256, tn=1024, tk=T, ooff=0, name="grad_w_down")
        G_g = _mm_tn(dgt, sv["x1"], None, C=D, Ka=DFF, N=D, tm=256, tn=1024, tk=T, ooff=0, name="grad_w_gate")
        G_u = _mm_tn(dut, sv["x1"], None, C=D, Ka=DFF, N=D, tm=256, tn=1024, tk=T, ooff=0, name="grad_w_up")
        ds1, dg1, db1, ds1b, dmixed = _ln_bwd(dx1, sv["xh1"], sv["r1"], ln1_g[l:l + 1], Wout)
        G_out = _mm_tn(sv["mixed"], ds1b, None, C=D, Ka=D, N=D, tm=256, tn=1024, tk=T, ooff=0, name="grad_w_out")
        early = [view(t) for t in (G_g, G_u, G_d, G_out)] + ([view(G[1]["in"])] if l == 0 else [])
        dqkv, gots = _sb_bwd(sv["qkv"], dmixed, sv["amat"], job=_sibling_job(early))
        ps = _pair_sums(early, list(gots), core)
        dqkv, dtbl, pa = _flash_bwd(sv["qkv"], sv["mixed"], dmixed, sv["lse_d"], dqkv, 1, False, (tbl,),
                                    job=_chip_job(ps[:2]))
        dqkv, dck, pb = _flash_bwd(sv["qkv"], sv["o_fox"], dmixed, sv["lse_f"], dqkv, 2, True,
                                   (sv["cq"], sv["ckb"]), job=_chip_job(ps[2:]))
        sums = _chip_sums(list(pa) + list(pb))
        shard_g[l] = dict(zip(("g", "u", "d", "out"), sums[:4]))
        if l == 0:
            shard_g[1]["in"] = sums[4]
        dconv, dcw = _conv_bwd(sv["conv"], cw8[l], dmixed)
        dcum = jnp.pad(dck.reshape(S, BL, NH).transpose(1, 0, 2).reshape(T, NH), ((0, 0), (0, GATEW - NH)))
        dgate, dfb = _fox_post(dcum, sv["gate"], fb[l])
        drb = _dil_table_bwd(dtbl)
        G_in = _mm_tn(sv["h"], dqkv, None, C=NPAD, Ka=D, N=QKVW, tm=512, tn=768, tk=T, ooff=0, name="grad_w_in_qkv")
        G_in = _mm_tn(sv["h"], dconv, G_in, C=NPAD, Ka=D, N=CONVW, tm=256, tn=768, tk=T, ooff=3,
                      name="grad_w_in_conv")
        G_in = _mm_tn(sv["h"], dgate, G_in, C=NPAD, Ka=D, N=GATEW, tm=1024, tn=128, tk=1024, ooff=24,
                      name="grad_w_in_gate")
        G[l] = {"in": G_in, "out": G_out, "g": G_g, "u": G_u, "d": G_d}
        if l == 0:
            late = [view(G_in)]
            tail = _chip_job(_pair_sums(late, list(_run_job(_sibling_job(late), "sibling_exchange")), core))
            dy, parts = _mm([(dqkv, arow(1024, QKVW), Win, wrow(512, QKVW, 0)),
                             (dconv, arow(1024, CONVW), Win, wrow(512, CONVW, 3)),
                             (dgate, arow(1024, GATEW), Win, wrow(512, GATEW, 24))],
                            nt=True, M=T, N=D, tm=1024, tn=512, out_dtype=F32, name="proj_dx", res=ds1,
                            res_scale=ALPHA, job=tail)
            shard_g[0]["in"] = _chip_sums(list(parts))[0]
        else:
            dy = _proj_bwd(dqkv, dconv, dgate, Win, ds1)
        small_g[l] = dict(ln1_g=dg1, ln1_b=db1, ln2_g=dg2, ln2_b=db2, cw=dcw[0:3].reshape(1, CONVW),
                          fb=dfb[:, :NH], rb=drb[:, :NH])
    grad_x = dy.reshape(BL, S, D)

    rows = []
    for name in ("ln1_g", "ln1_b", "ln2_g", "ln2_b"):
        rows += [small_g[0][name], small_g[1][name]]
    rows += [_row(small_g[0]["cw"]), _row(small_g[1]["cw"]),
             _row(jnp.concatenate([small_g[0]["fb"], small_g[1]["fb"]], axis=0)),
             _row(small_g[0]["rb"] + small_g[1]["rb"]), _row(sq)]
    rows.append(jnp.zeros((SMALL_ROWS - len(rows), D), F32))
    sg = _allreduce_small(jnp.concatenate(rows, axis=0))
    loss = sg[12, 0] * (0.5 / D)
    g_ln1_g, g_ln1_b, g_ln2_g, g_ln2_b = sg[0:2], sg[2:4], sg[4:6], sg[6:8]
    g_conv_full = sg[8:10, :CONVW].reshape(2, 3, 256)
    g_conv = lax.dynamic_slice(g_conv_full, (0, 0, me * 32), (2, 3, 32))
    g_fb = sg[10, :2 * NH].reshape(2, NH)
    g_rb = sg[11, :32 * NH].reshape(32, NH)

    def both(name):
        return jnp.stack([shard_g[0][name], shard_g[1][name]])

    g_in = both("in")
    g_w_in = jnp.concatenate([_unpermute_in(g_in[..., :QKVW]), g_in[..., QKVW:NPROJ]], axis=-1)
    g_w_out = both("out")
    g_w_gate = jnp.swapaxes(both("g"), 1, 2)
    g_w_up = jnp.swapaxes(both("u"), 1, 2)
    g_w_down = both("d")

    up_in = _adamw(w_in, g_w_in, m_w_in, v_w_in, 64)
    up_out = _adamw(w_out, g_w_out, m_w_out, v_w_out, 128)
    up_gate = _adamw(w_gate, g_w_gate, m_w_gate, v_w_gate, 256)
    up_up = _adamw(w_up, g_w_up, m_w_up, v_w_up, 256)
    up_down = _adamw(w_down, g_w_down, m_w_down, v_w_down, 352)

    def pack(fbv, cwv, rbv, l1g, l1b, l2g, l2b):
        r = [l1g, l1b, l2g, l2b, _row(cwv), _row(fbv), _row(rbv)]
        r.append(jnp.zeros((SMALL_ROWS - 11, D), F32))
        return jnp.concatenate(r, axis=0)

    pw = pack(f_bias, conv_w, rel_bias, ln1_g, ln1_b, ln2_g, ln2_b)
    pg = pack(g_fb, g_conv, g_rb, g_ln1_g, g_ln1_b, g_ln2_g, g_ln2_b)
    pm = pack(m_f_bias, m_conv_w, m_rel_bias, m_ln1_g, m_ln1_b, m_ln2_g, m_ln2_b)
    pv = pack(v_f_bias, v_conv_w, v_rel_bias, v_ln1_g, v_ln1_b, v_ln2_g, v_ln2_b)
    ups = [u[0] for u in _adamw(pw[None], pg[None], pm[None], pv[None], SMALL_ROWS)]

    def unpack(p):
        return dict(ln1_g=p[0:2], ln1_b=p[2:4], ln2_g=p[4:6], ln2_b=p[6:8],
                    conv_w=p[8, :192].reshape(2, 3, 32), f_bias=p[9, :2 * NH].reshape(2, NH),
                    rel_bias=p[10, :32 * NH].reshape(32, NH))

    sm = [unpack(p) for p in ups]

    def group(k):
        return (up_in[k], sm[k]["f_bias"], sm[k]["conv_w"], up_out[k], sm[k]["rel_bias"], sm[k]["ln1_g"],
                sm[k]["ln1_b"], up_gate[k], up_up[k], up_down[k], sm[k]["ln2_g"], sm[k]["ln2_b"])

    grads = (g_w_in, g_fb, g_conv, g_w_out, g_rb, g_ln1_g, g_ln1_b, g_w_gate, g_w_up, g_w_down, g_ln2_g, g_ln2_b)
    return (loss, grad_x) + grads + group(0) + group(1) + group(2)
```

```python
import math

import numpy as np
import jax
import jax.numpy as jnp
from jax import lax
from jax.experimental import pallas as pl
from jax.experimental.pallas import tpu as pltpu

F32 = jnp.float32
BF16 = jnp.bfloat16
MESH = pl.DeviceIdType.MESH

D = 1024
S = 2048
BL = 2
T = BL * S
NH = 4
DFF = 2816
NPROJ = 3076
NPAD = 3200
QKVW = 2304
CONVW = 768
GATEW = 128
PAIRW = 384
BQ = 128
HB = 2 * BQ
NB = S // BQ
NDEV = 8
NSTAT = BL * NH
ALPHA = 4.0 ** 0.25
SCALE = 0.125
NEG = -1e30
LN_EPS = 1e-5
ADAM_LR, ADAM_B1, ADAM_B2, ADAM_EPS, ADAM_WD, ADAM_STEP = 0.001, 0.9, 0.999, 1e-08, 0.01, 10
VMEM_LIMIT = 56 * 1024 * 1024
SMALL_ROWS = 16


def _bucket_thresholds():
    d = np.arange(0, S)
    nf = np.maximum(d, 1).astype(np.float32)
    large = 16 + (np.log(nf / np.float32(16)) / np.float32(math.log(128)) * np.float32(16)).astype(np.int32)
    b = np.where(d < 16, d, np.minimum(large, 31))
    return [int(np.argmax(b >= k)) for k in range(32)]


BUCKET_TH = _bucket_thresholds()


def _cp(sem=None, vmem=VMEM_LIMIT):
    return pltpu.CompilerParams(dimension_semantics=sem, vmem_limit_bytes=vmem)


def _dot(a, b):
    return lax.dot_general(a, b, (((1,), (0,)), ((), ())), preferred_element_type=F32)


def _dot_nt(a, b):
    return lax.dot_general(a, b, (((1,), (1,)), ((), ())), preferred_element_type=F32)


def _dot_tn(a, b):
    return lax.dot_general(a, b, (((0,), (0,)), ((), ())), preferred_element_type=F32)


def _split2(x):
    hi = x.astype(BF16)
    mid = (x - hi.astype(F32)).astype(BF16)
    return jnp.concatenate([hi, mid], axis=1)


def _split3(x):
    hi = x.astype(BF16)
    r = x - hi.astype(F32)
    mid = r.astype(BF16)
    lo = (r - mid.astype(F32)).astype(BF16)
    return jnp.concatenate([hi, mid, lo], axis=1)


def _log_sigmoid(u):
    return jnp.minimum(u, 0.0) - jnp.log1p(jnp.exp(-jnp.abs(u)))


def _log_sigmoid_tile(u):
    return jnp.minimum(u, 0.0) - jnp.log(1.0 + jnp.exp(jnp.minimum(u, -u)))


def _iota(shape, dim):
    return lax.broadcasted_iota(jnp.int32, shape, dim)


ANY_SPEC = pl.BlockSpec(memory_space=pl.ANY)
VMEM_SPEC = pl.BlockSpec(memory_space=pltpu.VMEM)


def _mm(pairs, *, nt, M, N, tm, tn, out_dtype, name, res=None, res_scale=1.0, job=None):
    n = len(pairs)
    n_in = 2 * n + (res is not None)
    jins = job.ins if job else []
    jouts = job.out_shapes if job else []
    gi, gj = M // tm, N // tn

    def body(*refs):
        o_ref = refs[n_in + len(jins)]
        if job:
            jrefs = (refs[n_in:n_in + len(jins)], refs[n_in + len(jins) + 1:n_in + len(jins) + 1 + len(jouts)],
                     refs[n_in + len(jins) + 1 + len(jouts):])

            @pl.when((pl.program_id(0) == 0) & (pl.program_id(1) == 0))
            def _():
                job.start(*jrefs)

        acc = None
        for p in range(n):
            a = refs[2 * p][...].astype(BF16)
            b = refs[2 * p + 1][...]
            d = _dot_nt(a, b) if nt else _dot(a, b)
            acc = d if acc is None else acc + d
        if res is not None:
            acc = acc + res_scale * refs[2 * n][...]
        o_ref[...] = acc.astype(out_dtype)
        if job:
            @pl.when((pl.program_id(0) == gi - 1) & (pl.program_id(1) == gj - 1))
            def _():
                job.finish(None, *jrefs)

    ops, specs = [], []
    for a, asp, b, bsp in pairs:
        ops += [a, b]
        specs += [asp, bsp]
    if res is not None:
        ops.append(res)
        specs.append(pl.BlockSpec((tm, tn), lambda i, j: (i, j)))
    out = pl.pallas_call(
        body, name=name, grid=(gi, gj), in_specs=specs + [ANY_SPEC] * len(jins),
        out_specs=[pl.BlockSpec((tm, tn), lambda i, j: (i, j))] + [ANY_SPEC] * len(jouts),
        out_shape=[jax.ShapeDtypeStruct((M, N), out_dtype)] + list(jouts),
        scratch_shapes=list(job.sems) if job else [],
        input_output_aliases={n_in + a: 1 + b for a, b in job.aliases.items()} if job else {},
        compiler_params=_cp(("arbitrary", "arbitrary") if job else ("parallel", "parallel")))(*ops, *jins)
    return (out[0], out[1:]) if job else out[0]


def _mm_tn(a, b, gbuf, *, C, Ka, N, tm, tn, tk, ooff, name):
    def body(*refs):
        a_ref, b_ref, o_ref = refs[0], refs[1], refs[-1]
        k = pl.program_id(2)
        d = _dot_tn(a_ref[...].astype(BF16), b_ref[...].astype(BF16))

        @pl.when(k == 0)
        def _():
            o_ref[...] = d

        @pl.when(k > 0)
        def _():
            o_ref[...] += d

    ops = [a, b] + ([] if gbuf is None else [gbuf])
    return pl.pallas_call(
        body, name=name, grid=(Ka // tm, N // tn, T // tk),
        in_specs=[pl.BlockSpec((tk, tm), lambda i, j, k: (k, i)),
                  pl.BlockSpec((tk, tn), lambda i, j, k: (k, j))] + ([] if gbuf is None else [ANY_SPEC]),
        out_specs=pl.BlockSpec((tm, tn), lambda i, j, k: (i, ooff + j)),
        out_shape=jax.ShapeDtypeStruct((Ka, C), F32),
        input_output_aliases={} if gbuf is None else {2: 0},
        compiler_params=_cp(("parallel", "parallel", "arbitrary")))(*ops)


def _proj(xb, w):
    tm = 512

    def body(x_ref, w_ref, qkv_ref, conv_ref, gate_ref):
        xv = x_ref[...]
        qkv_ref[...] = _dot(xv, w_ref[:, 0:QKVW]).astype(BF16)
        conv_ref[...] = _dot(xv, w_ref[:, QKVW:QKVW + CONVW])
        gate_ref[...] = _dot(xv, w_ref[:, QKVW + CONVW:NPAD])

    def rows(n):
        return pl.BlockSpec((tm, n), lambda i: (i, 0))

    return pl.pallas_call(
        body, name="proj", grid=(T // tm,),
        in_specs=[rows(D), pl.BlockSpec((D, NPAD), lambda i: (0, 0))],
        out_specs=[rows(QKVW), rows(CONVW), rows(GATEW)],
        out_shape=[jax.ShapeDtypeStruct((T, QKVW), BF16), jax.ShapeDtypeStruct((T, CONVW), F32),
                   jax.ShapeDtypeStruct((T, GATEW), F32)],
        compiler_params=_cp(("parallel",)))(xb, w)


def _proj_bwd(dqkv, dconv, dgate, w, res):
    tm = 512

    def body(a_ref, b_ref, c_ref, w_ref, r_ref, o_ref):
        acc = ALPHA * r_ref[...] + _dot_nt(a_ref[...], w_ref[:, 0:QKVW])
        acc = acc + _dot_nt(b_ref[...], w_ref[:, QKVW:QKVW + CONVW])
        o_ref[...] = acc + _dot_nt(c_ref[...].astype(BF16), w_ref[:, QKVW + CONVW:NPAD])

    def rows(n):
        return pl.BlockSpec((tm, n), lambda i: (i, 0))

    return pl.pallas_call(
        body, name="proj_bwd", grid=(T // tm,),
        in_specs=[rows(QKVW), rows(CONVW), rows(GATEW), pl.BlockSpec((D, NPAD), lambda i: (0, 0)), rows(D)],
        out_specs=rows(D), out_shape=jax.ShapeDtypeStruct((T, D), F32),
        compiler_params=_cp(("parallel",)))(dqkv, dconv, dgate, w, res)


def _ffn_fwd(xb, x, wgt, wut, wd, gam, bet):
    tm, ch = 512, 256

    def body(xb_ref, x_ref, g_ref, b_ref, wg_hbm, wu_hbm, wd_hbm,
             go_ref, uo_ref, ao_ref, y_ref, xh_ref, r_ref, yb_ref, wg_v, wu_v, wd_v, sem):
        loads = [pltpu.make_async_copy(s, d, sem.at[k])
                 for k, (s, d) in enumerate(((wg_hbm, wg_v), (wu_hbm, wu_v), (wd_hbm, wd_v)))]

        @pl.when(pl.program_id(0) == 0)
        def _():
            for cp in loads:
                cp.start()
            loads[0].wait()
            loads[1].wait()

        xv = xb_ref[...]
        for c in range(0, DFF, ch):
            gv = _dot_nt(xv, wg_v[c:c + ch, :])
            uv = _dot_nt(xv, wu_v[c:c + ch, :])
            go_ref[:, c:c + ch] = gv.astype(BF16)
            uo_ref[:, c:c + ch] = uv.astype(BF16)
            ao_ref[:, c:c + ch] = (gv * jax.nn.sigmoid(gv) * uv).astype(BF16)
        @pl.when(pl.program_id(0) == 0)
        def _():
            loads[2].wait()

        s = ALPHA * x_ref[...] + _dot(ao_ref[...], wd_v[...])
        mu = jnp.mean(s, axis=-1, keepdims=True)
        xc = s - mu
        var = jnp.mean(xc * xc, axis=-1, keepdims=True)
        r = lax.rsqrt(var + LN_EPS)
        xh = xc * r
        xh_ref[...] = xh.astype(BF16)
        r_ref[...] = r
        y = xh * g_ref[...] + b_ref[...]
        y_ref[...] = y
        yb_ref[...] = y.astype(BF16)

    row = pl.BlockSpec((tm, D), lambda i: (i, 0))
    wide = pl.BlockSpec((tm, DFF), lambda i: (i, 0))
    vec = pl.BlockSpec((1, D), lambda i: (0, 0))
    wsl = pltpu.VMEM((DFF, D), BF16)
    hid = jax.ShapeDtypeStruct((T, DFF), BF16)
    return pl.pallas_call(
        body, name="ffn_fwd", grid=(T // tm,),
        in_specs=[row, row, vec, vec, ANY_SPEC, ANY_SPEC, ANY_SPEC],
        out_specs=[wide, wide, wide, row, row, pl.BlockSpec((tm, 1), lambda i: (i, 0)), row],
        out_shape=[hid, hid, hid, jax.ShapeDtypeStruct((T, D), F32), jax.ShapeDtypeStruct((T, D), BF16),
                   jax.ShapeDtypeStruct((T, 1), F32), jax.ShapeDtypeStruct((T, D), BF16)],
        scratch_shapes=[wsl, wsl, wsl, pltpu.SemaphoreType.DMA((3,))],
        compiler_params=_cp(("arbitrary",)))(xb, x, gam, bet, wgt, wut, wd)


def _ffn_bwd(dy, xh, r, gam, g, u, wd, wgt, wut, target=None):
    tm, ch = 256, 256

    def body(*refs):
        if target is None:
            (dy_ref, xh_ref, r_ref, gam_ref, g_ref, u_ref, wd_hbm, wg_hbm, wu_hbm,
             dg_ref, du_ref, dsb_ref, dx_ref, dgam_ref, dbet_ref, wd_v, wg_v, wu_v, sem) = refs
        else:
            (dy_ref, t_ref, xh_ref, r_ref, gam_ref, g_ref, u_ref, wd_hbm, wg_hbm, wu_hbm,
             dg_ref, du_ref, dsb_ref, dx_ref, dgam_ref, dbet_ref, sq_ref, wd_v, wg_v, wu_v, sem) = refs
        loads = [pltpu.make_async_copy(s, d, sem.at[k])
                 for k, (s, d) in enumerate(((wd_hbm, wd_v), (wg_hbm, wg_v), (wu_hbm, wu_v)))]

        @pl.when(pl.program_id(0) == 0)
        def _():
            for cp in loads:
                cp.start()
            loads[0].wait()

        if target is None:
            dyv = dy_ref[...]
        else:
            e = dy_ref[...] - t_ref[...]
            dyv = e * (1.0 / D)
            p = jnp.sum(jnp.sum(e * e, axis=1, keepdims=True), axis=0, keepdims=True)

            @pl.when(pl.program_id(0) == 0)
            def _():
                sq_ref[...] = p

            @pl.when(pl.program_id(0) > 0)
            def _():
                sq_ref[...] += p

        xhv = xh_ref[...].astype(F32)
        dxh = dyv * gam_ref[...]
        m1 = jnp.mean(dxh, axis=-1, keepdims=True)
        m2 = jnp.mean(dxh * xhv, axis=-1, keepdims=True)
        ds = r_ref[...] * (dxh - m1 - xhv * m2)
        pg = jnp.sum(dyv * xhv, axis=0, keepdims=True)
        pb = jnp.sum(dyv, axis=0, keepdims=True)

        @pl.when(pl.program_id(0) == 0)
        def _():
            dgam_ref[...] = pg
            dbet_ref[...] = pb

        @pl.when(pl.program_id(0) > 0)
        def _():
            dgam_ref[...] += pg
            dbet_ref[...] += pb

        db = ds.astype(BF16)
        dsb_ref[...] = db
        for c in range(0, DFF, ch):
            da = _dot_nt(db, wd_v[c:c + ch, :])
            gv = g_ref[:, c:c + ch].astype(F32)
            sg = jax.nn.sigmoid(gv)
            dg_ref[:, c:c + ch] = (da * u_ref[:, c:c + ch].astype(F32) * (sg * (1.0 + gv * (1.0 - sg)))).astype(BF16)
            du_ref[:, c:c + ch] = (da * (gv * sg)).astype(BF16)
        @pl.when(pl.program_id(0) == 0)
        def _():
            loads[1].wait()
            loads[2].wait()

        dx_ref[...] = ALPHA * ds + _dot(dg_ref[...], wg_v[...]) + _dot(du_ref[...], wu_v[...])

    row = pl.BlockSpec((tm, D), lambda i: (i, 0))
    wide = pl.BlockSpec((tm, DFF), lambda i: (i, 0))
    vec = pl.BlockSpec((1, D), lambda i: (0, 0))
    wsl = pltpu.VMEM((DFF, D), BF16)
    last = target is not None
    return pl.pallas_call(
        body, name="ffn_bwd_loss" if last else "ffn_bwd", grid=(T // tm,),
        in_specs=[row] + ([row] if last else [])
        + [row, pl.BlockSpec((tm, 1), lambda i: (i, 0)), vec, wide, wide, ANY_SPEC, ANY_SPEC, ANY_SPEC],
        out_specs=[wide, wide, row, row, vec, vec] + ([pl.BlockSpec((1, 1), lambda i: (0, 0))] if last else []),
        out_shape=[jax.ShapeDtypeStruct((T, DFF), BF16), jax.ShapeDtypeStruct((T, DFF), BF16),
                   jax.ShapeDtypeStruct((T, D), BF16), jax.ShapeDtypeStruct((T, D), F32),
                   jax.ShapeDtypeStruct((1, D), F32), jax.ShapeDtypeStruct((1, D), F32)]
        + ([jax.ShapeDtypeStruct((1, 1), F32)] if last else []),
        scratch_shapes=[wsl, wsl, wsl, pltpu.SemaphoreType.DMA((3,))],
        compiler_params=_cp(("arbitrary",)))(dy, *([target] if last else []), xh, r, gam, g, u, wd, wgt, wut)


def _mm_ln(a, w, x, gam, bet, name):
    tm = 256
    K = a.shape[1]

    def body(a_ref, w_ref, x_ref, g_ref, b_ref, y_ref, xh_ref, r_ref, yb_ref):
        s = ALPHA * x_ref[...] + _dot(a_ref[...], w_ref[...])
        mu = jnp.mean(s, axis=-1, keepdims=True)
        xc = s - mu
        var = jnp.mean(xc * xc, axis=-1, keepdims=True)
        r = lax.rsqrt(var + LN_EPS)
        xh = xc * r
        xh_ref[...] = xh.astype(BF16)
        r_ref[...] = r
        y = xh * g_ref[...] + b_ref[...]
        y_ref[...] = y
        yb_ref[...] = y.astype(BF16)

    row = pl.BlockSpec((tm, D), lambda i: (i, 0))
    vec = pl.BlockSpec((1, D), lambda i: (0, 0))
    return pl.pallas_call(
        body, name=name, grid=(T // tm,),
        in_specs=[pl.BlockSpec((tm, K), lambda i: (i, 0)), pl.BlockSpec((K, D), lambda i: (0, 0)), row, vec, vec],
        out_specs=[row, row, pl.BlockSpec((tm, 1), lambda i: (i, 0)), row],
        out_shape=[jax.ShapeDtypeStruct((T, D), F32), jax.ShapeDtypeStruct((T, D), BF16),
                   jax.ShapeDtypeStruct((T, 1), F32), jax.ShapeDtypeStruct((T, D), BF16)],
        compiler_params=_cp(("parallel",)))(a, w, x, gam, bet)


def _ln_bwd(dy, xh, r, gam, w):
    tm = 256

    def body(dy_ref, xh_ref, r_ref, g_ref, w_ref, ds_ref, dg_ref, db_ref, dsb_ref, dm_ref):
        i = pl.program_id(0)
        dyv = dy_ref[...]
        xhv = xh_ref[...].astype(F32)
        dxh = dyv * g_ref[...]
        m1 = jnp.mean(dxh, axis=-1, keepdims=True)
        m2 = jnp.mean(dxh * xhv, axis=-1, keepdims=True)
        ds = r_ref[...] * (dxh - m1 - xhv * m2)
        ds_ref[...] = ds
        dsb = ds.astype(BF16)
        dsb_ref[...] = dsb
        dm_ref[...] = _dot_nt(dsb, w_ref[...]).astype(BF16)
        pg = jnp.sum(dyv * xhv, axis=0, keepdims=True)
        pb = jnp.sum(dyv, axis=0, keepdims=True)

        @pl.when(i == 0)
        def _():
            dg_ref[...] = pg
            db_ref[...] = pb

        @pl.when(i > 0)
        def _():
            dg_ref[...] += pg
            db_ref[...] += pb

    row = pl.BlockSpec((tm, D), lambda i: (i, 0))
    vec = pl.BlockSpec((1, D), lambda i: (0, 0))
    return pl.pallas_call(
        body, name="ln_bwd_proj", grid=(T // tm,),
        in_specs=[row, row, pl.BlockSpec((tm, 1), lambda i: (i, 0)), vec, pl.BlockSpec((D, D), lambda i: (0, 0))],
        out_specs=[row, vec, vec, row, row],
        out_shape=[jax.ShapeDtypeStruct((T, D), F32), jax.ShapeDtypeStruct((1, D), F32),
                   jax.ShapeDtypeStruct((1, D), F32), jax.ShapeDtypeStruct((T, D), BF16),
                   jax.ShapeDtypeStruct((T, D), BF16)],
        compiler_params=_cp(("arbitrary",)))(dy, xh, r, gam, w)


def _adamw(w, g, m, v, tr):
    L, R, C = w.shape

    def body(w_ref, g_ref, m_ref, v_ref, d_ref, m2_ref, v2_ref):
        gv = g_ref[...]
        m2 = ADAM_B1 * m_ref[...] + (1.0 - ADAM_B1) * gv
        v2 = ADAM_B2 * v_ref[...] + (1.0 - ADAM_B2) * (gv * gv)
        m_hat = m2 / (1.0 - ADAM_B1 ** ADAM_STEP)
        v_hat = v2 / (1.0 - ADAM_B2 ** ADAM_STEP)
        d_ref[...] = -ADAM_LR * (m_hat / (jnp.sqrt(v_hat) + ADAM_EPS) + ADAM_WD * w_ref[...])
        m2_ref[...] = m2
        v2_ref[...] = v2

    blk = pl.BlockSpec((None, tr, C), lambda l, i: (l, i, 0))
    sh = jax.ShapeDtypeStruct((L, R, C), F32)
    return pl.pallas_call(
        body, name="adamw", grid=(L, R // tr), in_specs=[blk] * 4, out_specs=[blk] * 3,
        out_shape=[sh, sh, sh], compiler_params=_cp(("parallel", "parallel")))(w, g, m, v)


class _Job:
    def __init__(self, ins, out_shapes, aliases, sems, start, finish):
        self.ins, self.out_shapes, self.aliases, self.sems = list(ins), list(out_shapes), dict(aliases), list(sems)
        self.start, self.finish = start, finish


def _host_call(body, name, ins, in_specs, out_shapes, out_specs, scratch, aliases, job):
    n_in, n_out, n_scr = len(ins), len(out_shapes), len(scratch)
    jins = job.ins if job else []
    jouts = job.out_shapes if job else []
    jsems = job.sems if job else []

    def wrapped(*refs):
        a = n_in
        b = a + len(jins)
        c = b + n_out
        d = c + len(jouts)
        e = d + n_scr
        comm = None
        if job:
            jrefs = (refs[a:b], refs[c:d], refs[e:])
            comm = (lambda: job.start(*jrefs), lambda st: job.finish(st, *jrefs))
        body(refs[:a], refs[b:c], refs[d:e], comm)

    al = dict(aliases)
    if job:
        for ji, jo in job.aliases.items():
            al[n_in + ji] = n_out + jo
    res = pl.pallas_call(
        wrapped, name=name, in_specs=list(in_specs) + [ANY_SPEC] * len(jins),
        out_specs=list(out_specs) + [ANY_SPEC] * len(jouts), out_shape=list(out_shapes) + list(jouts),
        scratch_shapes=list(scratch) + list(jsems), input_output_aliases=al,
        compiler_params=_cp())(*ins, *jins)
    return res[:n_out], res[n_out:]


def _copy_in(src, dst, sem):
    cp = pltpu.make_async_copy(src, dst, sem)
    cp.start()
    cp.wait()


CHAINS = [(p, b) for p in range(2) for b in range(BL)]
NC = len(CHAINS)
ROWS_SHAPE = jax.ShapeDtypeStruct((NSTAT, S), F32)
SLAB_QKV = pltpu.VMEM((T, 2 * PAIRW), BF16)
SLAB_OUT = pltpu.VMEM((T, 2 * BQ), BF16)
SLAB_O32 = pltpu.VMEM((T, 2 * BQ), F32)
SLAB_T = pltpu.VMEM((2, BQ, T), BF16)
SLAB_KEYB = pltpu.VMEM((NSTAT, S, BQ), F32)
ACC_KV = pltpu.VMEM((2, T, BQ), F32)
NTRI = NB * (NB + 1) // 2
A_TILES = jax.ShapeDtypeStruct((NTRI, NC, HB, BQ), BF16)
PAIR_DIAG = pltpu.VMEM((NC, 2, HB, HB), BF16)
A_SLOTS_OUT, A_SLOTS_IN = 2, 4


def _lane_masks():
    lane = _iota((1, BQ), 1)
    m0 = (lane < 64).astype(BF16)
    return m0, 1.0 - m0


def _merge_heads(x, first):
    return jnp.where(first, x[:BQ], x[BQ:])


def _row_masks():
    r = _iota((BQ, 1), 0)
    m0 = (r < 64).astype(BF16)
    return m0, 1.0 - m0


def _stack(x, m0, m1):
    return jnp.concatenate([x * m0, x * m1], axis=0)


def _stack_t(xt, r0, r1):
    return jnp.concatenate([xt * r0, xt * r1], axis=1)


def _tr(x):
    return x.T


def _rows(b, i):
    return pl.ds(pl.multiple_of(b * S + i * BQ, BQ), BQ)


def _transpose_slab(src, dst, col0):
    def blk(n, _):
        r = pl.ds(pl.multiple_of(n * BQ, BQ), BQ)
        for p in range(2):
            dst[p, :, r] = _tr(src[r, col0(p):col0(p) + BQ])
        return 0

    lax.fori_loop(0, T // BQ, blk, 0)


def _heads(x):
    return x[:BQ], x[BQ:]


def _bcast_heads(r0, r1):
    return jnp.concatenate([jnp.broadcast_to(r0, (BQ, BQ)), jnp.broadcast_to(r1, (BQ, BQ))], axis=0)


def _by_channel(r0, r1):
    return jnp.where(_iota((BQ, BQ), 0) < 64, r0, r1)


def _colsum2(x):
    return jnp.sum(x[:BQ], axis=0, keepdims=True), jnp.sum(x[BQ:], axis=0, keepdims=True)


def _stat_row(ref, p, b, h, i):
    c = b * NH + 2 * p + h
    return ref[c:c + 1, pl.ds(pl.multiple_of(i * BQ, BQ), BQ)]


def _put_row(ref, p, b, h, i, v):
    c = b * NH + 2 * p + h
    ref[c:c + 1, pl.ds(pl.multiple_of(i * BQ, BQ), BQ)] = v


def _valid_t(strict):
    r = _iota((HB, BQ), 0) & (BQ - 1)
    c = _iota((HB, BQ), 1)
    return (r < c) if strict else (r <= c)


def _tri_blockdiag(later):
    r = _iota((HB, HB), 0)
    c = _iota((HB, HB), 1)
    same = (r >= BQ) == (c >= BQ)
    return (same & ((c > r) if later else (c < r))).astype(BF16)


def _cum_mm(tri, x):
    y = _dot(tri, _split2(x))
    return y[:, :BQ] + y[:, BQ:]


def _kv_tiles(qkv_v, p, b, j):
    r = _rows(b, j)
    return qkv_v[r, p * PAIRW + BQ:p * PAIRW + 2 * BQ], qkv_v[r, p * PAIRW + 2 * BQ:p * PAIRW + 3 * BQ]


def _q_tile(qkv_v, p, b, i):
    return qkv_v[_rows(b, i), p * PAIRW:p * PAIRW + BQ] * SCALE


def _sb_fwd(qkv, job=None):
    def body(ins, outs, scr, comm):
        (qkv_hbm,), (o_hbm, a_hbm), (qkv_v, o_v, sem, vt_v, a_st, a_sems) = ins, outs, scr
        _copy_in(qkv_hbm.at[:, pl.ds(0, 2 * PAIRW)], qkv_v, sem)
        st = comm[0]() if comm else None
        _transpose_slab(qkv_v, vt_v, lambda p: p * PAIRW + 2 * BQ)
        m0, m1 = _lane_masks()
        r0, r1 = _row_masks()
        valid = _valid_t(True)
        later = _tri_blockdiag(True)

        def a_copy(n, t):
            slot = n % A_SLOTS_OUT
            return pltpu.make_async_copy(a_st.at[slot], a_hbm.at[t], a_sems.at[slot])

        def steps(qts, i, jj, cs, diag):
            j = i - jj
            ks = [_stack(_kv_tiles(qkv_v, p, b, j)[0], m0, m1) for p, b in CHAINS]
            zs = [_dot(ks[c], qts[c]) for c in range(NC)]
            lbs, lrs = [], []
            for c in range(NC):
                lb = _log_sigmoid_tile(zs[c])
                lr = lb - zs[c]
                if diag:
                    lr = jnp.where(valid, lr, 0.0)
                lbs.append(lb)
                lrs.append(lr)
            tails = [_cum_mm(later, lrs[c]) for c in range(NC)]
            avs = []
            for c in range(NC):
                a = jnp.exp(lbs[c] + tails[c] + _bcast_heads(*cs[c][0]))
                if diag:
                    a = jnp.where(valid, a, 0.0)
                avs.append(a.astype(BF16))
            n = (i * (i + 1)) // 2 + jj

            @pl.when(n >= A_SLOTS_OUT)
            def _():
                a_copy(n, 0).wait()
            for c in range(NC):
                a_st[n % A_SLOTS_OUT, c] = avs[c]
            a_copy(n, n - jj + j).start()
            out = []
            for c, (p, b) in enumerate(CHAINS):
                vts = _stack_t(vt_v[p, :, _rows(b, j)], r0, r1)
                s0, s1 = _colsum2(lrs[c])
                out.append(((cs[c][0][0] + s0, cs[c][0][1] + s1), cs[c][1] + _dot(vts, avs[c])))
            return tuple(out)

        def qblock(i, _):
            qts = [_tr(_q_tile(qkv_v, p, b, i)) for p, b in CHAINS]
            zr = jnp.zeros((1, BQ), F32)
            cs = steps(qts, i, 0, (((zr, zr), jnp.zeros((BQ, BQ), F32)),) * NC, True)
            cs = lax.fori_loop(1, i + 1, lambda jj, cs: steps(qts, i, jj, cs, False), cs)
            for c, (p, b) in enumerate(CHAINS):
                o_v[_rows(b, i), p * BQ:(p + 1) * BQ] = cs[c][1].T.astype(BF16)
            return 0

        lax.fori_loop(0, NB, qblock, 0)
        for n in range(NTRI - A_SLOTS_OUT, NTRI):
            a_copy(n, 0).wait()
        _copy_in(o_v, o_hbm.at[:, pl.ds(0, 2 * BQ)], sem)
        if comm:
            comm[1](st)

    (mixed, amat), extra = _host_call(
        body, "sb_fwd", [qkv], [ANY_SPEC], [jax.ShapeDtypeStruct((T, D), BF16), A_TILES], [ANY_SPEC, ANY_SPEC],
        [SLAB_QKV, SLAB_OUT, pltpu.SemaphoreType.DMA, SLAB_T, pltpu.VMEM((A_SLOTS_OUT, NC, HB, BQ), BF16),
         pltpu.SemaphoreType.DMA((A_SLOTS_OUT,))], {}, job)
    return mixed, amat, extra


def _sb_bwd(qkv, dmixed, amat, job=None):
    def body(ins, outs, scr, comm):
        (qkv_hbm, do_hbm, a_hbm), (dqkv_hbm,), (qkv_v, do_v, dq_v, dk_s, dv_s, sems, kt_v, a_st, a_sems, w_v) = ins, outs, scr
        sem = sems.at[0]
        w_v[...] = jnp.zeros_like(w_v)

        def a_copy(t):
            slot = t % A_SLOTS_IN
            return pltpu.make_async_copy(a_hbm.at[t], a_st.at[slot], a_sems.at[slot])

        later = [pltpu.make_async_copy(do_hbm.at[:, pl.ds(0, 2 * BQ)], do_v, sems.at[1])]
        for cp in later:
            cp.start()
        for t in range(A_SLOTS_IN - 1):
            a_copy(t).start()
        _copy_in(qkv_hbm.at[:, pl.ds(0, 2 * PAIRW)], qkv_v, sem)
        st = comm[0]() if comm else None
        _transpose_slab(qkv_v, kt_v, lambda p: p * PAIRW + BQ)
        for cp in later:
            cp.wait()
        m0, m1 = _lane_masks()
        first = _iota((BQ, BQ), 1) < 64
        r0, r1 = _row_masks()
        valid = _valid_t(True)
        earlier = _tri_blockdiag(False)
        dk_s[...] = jnp.zeros_like(dk_s)
        dv_s[...] = jnp.zeros_like(dv_s)

        def steps(i, j, cs, diag):
            t = (i * (i + 1)) // 2 + j
            a_copy(t).wait()

            @pl.when(t + A_SLOTS_IN - 1 < NTRI)
            def _():
                a_copy(t + A_SLOTS_IN - 1).start()
            slot = t % A_SLOTS_IN
            kv = [_kv_tiles(qkv_v, p, b, j) for p, b in CHAINS]
            zd = [_dot(jnp.concatenate([_stack(kv[c][0], m0, m1), _stack(kv[c][1], m0, m1)], axis=1), w_v[c, 0])
                  for c in range(NC)]
            zs = [x[:, :BQ] for x in zd]
            das = [x[:, BQ:] for x in zd]
            avs = [a_st[slot, c] for c in range(NC)]
            gms = [das[c] * avs[c].astype(F32) for c in range(NC)]
            befores = [_dot(earlier, gms[c].astype(BF16)) for c in range(NC)]
            dzbs = []
            for c in range(NC):
                dz = gms[c] - jax.nn.sigmoid(zs[c]) * (gms[c] + befores[c] + _bcast_heads(*cs[c][0]))
                if diag:
                    dz = jnp.where(valid, dz, 0.0)
                dzbs.append(dz.astype(BF16))
            out = []
            for c, (p, b) in enumerate(CHAINS):
                dq = cs[c][1] + _dot(_stack_t(kt_v[p, :, _rows(b, j)], r0, r1), dzbs[c])
                kd = _dot(jnp.concatenate([dzbs[c], avs[c]], axis=1), w_v[c, 1])
                dk_s[p, _rows(b, j), :] += _merge_heads(kd[:, :BQ], first)
                dv_s[p, _rows(b, j), :] += _merge_heads(kd[:, BQ:], first)
                g0, g1 = _colsum2(gms[c])
                out.append(((cs[c][0][0] + g0, cs[c][0][1] + g1), dq))
            return tuple(out)

        def qblock(i, _):
            for c, (p, b) in enumerate(CHAINS):
                qn = _q_tile(qkv_v, p, b, i)
                dn = do_v[_rows(b, i), p * BQ:(p + 1) * BQ]
                for r, (x, y) in enumerate(((_tr(qn), _tr(dn)), (qn, dn))):
                    w_v[c, r, :BQ, :BQ] = x
                    w_v[c, r, BQ:, BQ:] = y
            zr = jnp.zeros((1, BQ), F32)
            cs = (((zr, zr), jnp.zeros((BQ, BQ), F32)),) * NC
            cs = lax.fori_loop(0, i, lambda j, cs: steps(i, j, cs, False), cs)
            cs = steps(i, i, cs, True)
            for c, (p, b) in enumerate(CHAINS):
                dq_v[_rows(b, i), p * PAIRW:p * PAIRW + BQ] = (cs[c][1].T * SCALE).astype(BF16)
            return 0

        lax.fori_loop(0, NB, qblock, 0)
        for p in range(2):
            dq_v[:, p * PAIRW + BQ:p * PAIRW + 2 * BQ] = dk_s[p].astype(BF16)
            dq_v[:, p * PAIRW + 2 * BQ:p * PAIRW + 3 * BQ] = dv_s[p].astype(BF16)
        _copy_in(dq_v, dqkv_hbm.at[:, pl.ds(0, 2 * PAIRW)], sem)
        if comm:
            comm[1](st)

    (dqkv,), extra = _host_call(
        body, "sb_bwd", [qkv, dmixed, amat], [ANY_SPEC, ANY_SPEC, ANY_SPEC],
        [jax.ShapeDtypeStruct((T, QKVW), BF16)], [ANY_SPEC],
        [SLAB_QKV, SLAB_OUT, SLAB_QKV, ACC_KV, ACC_KV, pltpu.SemaphoreType.DMA((4,)), SLAB_T,
         pltpu.VMEM((A_SLOTS_IN, NC, HB, BQ), BF16), pltpu.SemaphoreType.DMA((A_SLOTS_IN,)), PAIR_DIAG], {}, job)
    return dqkv, extra


def _flash_fwd(qkv, mixed, g, fox, bias, job=None):
    def body(ins, outs, scr, comm):
        if fox:
            qkv_hbm, cq_ref, ckb_hbm, _ = ins
            (o_hbm, lse_ref, o32_hbm), (qkv_v, o_v, sem, vt_v, o32_v, ckb_v) = outs, scr
        else:
            qkv_hbm, tbl_ref, _ = ins
            (o_hbm, lse_ref), (qkv_v, o_v, sem, vt_v) = outs, scr
        sems = sem
        sem = sems.at[0]
        later = [pltpu.make_async_copy(ckb_hbm, ckb_v, sems.at[1])] if fox else []
        for cp in later:
            cp.start()
        _copy_in(qkv_hbm.at[:, pl.ds(g * 2 * PAIRW, 2 * PAIRW)], qkv_v, sem)
        st = comm[0]() if comm else None
        _transpose_slab(qkv_v, vt_v, lambda p: p * PAIRW + 2 * BQ)
        for cp in later:
            cp.wait()
        m0, m1 = _lane_masks()
        r0, r1 = _row_masks()
        valid = _valid_t(False)

        def steps(qts, cqs, i, j, cs, diag):
            ks = [_stack(_kv_tiles(qkv_v, p, b, j)[0], m0, m1) for p, b in CHAINS]
            zs = [_dot(ks[c], qts[c]) for c in range(NC)]
            prs, alphas, out = [], [], []
            for c, (p, b) in enumerate(CHAINS):
                (ma, mb), (la, lb_), _ = cs[c]
                if fox:
                    kk = pl.ds(pl.multiple_of(j * BQ, BQ), BQ)
                    col = b * NH + 2 * p
                    z = zs[c] + (cqs[c] - jnp.concatenate([ckb_v[col, kk, :], ckb_v[col + 1, kk, :]], axis=0))
                    if diag:
                        z = jnp.where(valid, z, NEG)
                else:
                    z = zs[c] + tbl_ref[p, i - j]
                za, zb = _heads(z)
                na = jnp.maximum(ma, jnp.max(za, axis=0, keepdims=True))
                nb = jnp.maximum(mb, jnp.max(zb, axis=0, keepdims=True))
                aa, ab = jnp.exp(ma - na), jnp.exp(mb - nb)
                pr = jnp.exp(z - _bcast_heads(na, nb))
                sa, sb = _colsum2(pr)
                prs.append(_split2(pr) if fox else pr.astype(BF16))
                alphas.append((aa, ab))
                out.append(((na, nb), (aa * la + sa, ab * lb_ + sb)))
            pvs = []
            for c, (p, b) in enumerate(CHAINS):
                vts = _stack_t(vt_v[p, :, _rows(b, j)], r0, r1)
                if fox:
                    pvs.append(_dot(vts, prs[c][:, :BQ]) + _dot(vts, prs[c][:, BQ:]))
                else:
                    pvs.append(_dot(vts, prs[c]))
            return tuple((out[c][0], out[c][1], _by_channel(*alphas[c]) * cs[c][2] + pvs[c]) for c in range(NC))

        def qblock(i, _):
            qts = [_tr(_q_tile(qkv_v, p, b, i)) for p, b in CHAINS]
            if fox:
                cqs = [_bcast_heads(_stat_row(cq_ref, p, b, 0, i), _stat_row(cq_ref, p, b, 1, i)) for p, b in CHAINS]
            else:
                cqs = [None] * NC
            ng = jnp.full((1, BQ), NEG, F32)
            zr = jnp.zeros((1, BQ), F32)
            cs = steps(qts, cqs, i, i, (((ng, ng), (zr, zr), jnp.zeros((BQ, BQ), F32)),) * NC, True)
            cs = lax.fori_loop(1, i + 1, lambda jj, cs: steps(qts, cqs, i, i - jj, cs, False), cs)
            for c, (p, b) in enumerate(CHAINS):
                (ma, mb), (la, lb_), acc = cs[c]
                o = (acc / _by_channel(la, lb_)).T
                o_v[_rows(b, i), p * BQ:(p + 1) * BQ] = o.astype(BF16)
                if fox:
                    o32_v[_rows(b, i), p * BQ:(p + 1) * BQ] = o
                _put_row(lse_ref, p, b, 0, i, ma + jnp.log(la))
                _put_row(lse_ref, p, b, 1, i, mb + jnp.log(lb_))
            return 0

        lax.fori_loop(0, NB, qblock, 0)
        _copy_in(o_v, o_hbm.at[:, pl.ds(g * 2 * BQ, 2 * BQ)], sem)
        if fox:
            _copy_in(o32_v, o32_hbm, sem)
        if comm:
            comm[1](st)

    bias_specs = [VMEM_SPEC, ANY_SPEC] if fox else [VMEM_SPEC]
    n_in = 2 + len(bias_specs)
    o32 = [jax.ShapeDtypeStruct((T, 2 * BQ), F32)] if fox else []
    res, extra = _host_call(
        body, "fox_fwd" if fox else "dil_fwd", [qkv, *bias, mixed], [ANY_SPEC] + bias_specs + [ANY_SPEC],
        [jax.ShapeDtypeStruct((T, D), BF16), ROWS_SHAPE] + o32, [ANY_SPEC, VMEM_SPEC] + [ANY_SPEC] * len(o32),
        [SLAB_QKV, SLAB_OUT, pltpu.SemaphoreType.DMA((4,)), SLAB_T] + ([SLAB_O32, SLAB_KEYB] if fox else []),
        {n_in - 1: 0}, job)
    return (*res, extra)


def _flash_bwd(qkv, o, dmixed, lse, dqkv, g, fox, bias, job=None):
    def body(ins, outs, scr, comm):
        if fox:
            qkv_hbm, o_hbm, do_hbm, lse_ref, cq_ref, ckb_hbm, _ = ins
            (dqkv_hbm, db_ref), (qkv_v, o_v, do_v, dq_v, dk_s, dv_s, sem, kt_v, w_v, ckb_v, dc_s) = outs, scr
        else:
            qkv_hbm, o_hbm, do_hbm, lse_ref, tbl_ref, _ = ins
            (dqkv_hbm, db_ref), (qkv_v, o_v, do_v, dq_v, dk_s, dv_s, sem, kt_v, w_v) = outs, scr
        w_v[...] = jnp.zeros_like(w_v)
        sems = sem
        sem = sems.at[0]
        later = [pltpu.make_async_copy(do_hbm.at[:, pl.ds(g * 2 * BQ, 2 * BQ)], do_v, sems.at[1])]
        if fox:
            later += [pltpu.make_async_copy(o_hbm, o_v, sems.at[2]), pltpu.make_async_copy(ckb_hbm, ckb_v, sems.at[3])]
        else:
            later += [pltpu.make_async_copy(o_hbm.at[:, pl.ds(g * 2 * BQ, 2 * BQ)], o_v, sems.at[2])]
        for cp in later:
            cp.start()
        _copy_in(qkv_hbm.at[:, pl.ds(g * 2 * PAIRW, 2 * PAIRW)], qkv_v, sem)
        st = comm[0]() if comm else None
        _transpose_slab(qkv_v, kt_v, lambda p: p * PAIRW + BQ)
        for cp in later:
            cp.wait()
        m0, m1 = _lane_masks()
        first = _iota((BQ, BQ), 1) < 64
        r0, r1 = _row_masks()
        valid = _valid_t(False)
        dk_s[...] = jnp.zeros_like(dk_s)
        dv_s[...] = jnp.zeros_like(dv_s)
        if fox:
            dc_s[...] = jnp.zeros_like(dc_s)
        else:
            db_ref[...] = jnp.zeros_like(db_ref)

        def steps(cqs, lses, deltas, i, j, dqs, diag):
            kv = [_kv_tiles(qkv_v, p, b, j) for p, b in CHAINS]
            zd = [_dot(jnp.concatenate([_stack(kv[c][0], m0, m1), _stack(kv[c][1], m0, m1)], axis=1), w_v[c, 0])
                  for c in range(NC)]
            zs = [x[:, :BQ] for x in zd]
            dps = [x[:, BQ:] for x in zd]
            prs, dzl = [], []
            for c, (p, b) in enumerate(CHAINS):
                if fox:
                    kk = pl.ds(pl.multiple_of(j * BQ, BQ), BQ)
                    col = b * NH + 2 * p
                    z = zs[c] + (cqs[c] - jnp.concatenate([ckb_v[col, kk, :], ckb_v[col + 1, kk, :]], axis=0))
                    if diag:
                        z = jnp.where(valid, z, NEG)
                else:
                    z = zs[c] + tbl_ref[p, i - j]
                pr = jnp.exp(z - lses[c])
                prs.append(pr.astype(BF16))
                dzl.append(pr * (dps[c] - deltas[c]))
            dzbs = [dz.astype(BF16) for dz in dzl]
            new = []
            for c, (p, b) in enumerate(CHAINS):
                new.append(dqs[c] + _dot(_stack_t(kt_v[p, :, _rows(b, j)], r0, r1), dzbs[c]))
                kd = _dot(jnp.concatenate([dzbs[c], prs[c]], axis=1), w_v[c, 1])
                dk_s[p, _rows(b, j), :] += _merge_heads(kd[:, :BQ], first)
                dv_s[p, _rows(b, j), :] += _merge_heads(kd[:, BQ:], first)
                if fox:
                    dc_s[c, pl.ds(pl.multiple_of(j * HB, HB), HB), :] += dzl[c]
            if not fox:
                for p in range(2):
                    db_ref[p, i - j] = db_ref[p, i - j] + (dzl[2 * p] + dzl[2 * p + 1])
            return tuple(new)

        def qblock(i, _):
            qns = [_q_tile(qkv_v, p, b, i) for p, b in CHAINS]
            dns = [do_v[_rows(b, i), p * BQ:(p + 1) * BQ] for p, b in CHAINS]
            for c in range(NC):
                for r, (x, y) in enumerate(((_tr(qns[c]), _tr(dns[c])), (qns[c], dns[c]))):
                    w_v[c, r, :BQ, :BQ] = x
                    w_v[c, r, BQ:, BQ:] = y
            lses = [_bcast_heads(_stat_row(lse_ref, p, b, 0, i), _stat_row(lse_ref, p, b, 1, i)) for p, b in CHAINS]
            if fox:
                cqs = [_bcast_heads(_stat_row(cq_ref, p, b, 0, i), _stat_row(cq_ref, p, b, 1, i)) for p, b in CHAINS]
            else:
                cqs = [None] * NC
            deltas = []
            for c, (p, b) in enumerate(CHAINS):
                pt = (dns[c].astype(F32) * o_v[_rows(b, i), p * BQ:(p + 1) * BQ].astype(F32)).T
                deltas.append(_bcast_heads(jnp.sum(pt[:64], axis=0, keepdims=True), jnp.sum(pt[64:], axis=0, keepdims=True)))
            dqs = (jnp.zeros((BQ, BQ), F32),) * NC
            dqs = lax.fori_loop(0, i, lambda j, d: steps(cqs, lses, deltas, i, j, d, False), dqs)
            dqs = steps(cqs, lses, deltas, i, i, dqs, True)
            for c, (p, b) in enumerate(CHAINS):
                dq_v[_rows(b, i), p * PAIRW:p * PAIRW + BQ] = (dqs[c].T * SCALE).astype(BF16)
            return 0

        lax.fori_loop(0, NB, qblock, 0)
        for p in range(2):
            dq_v[:, p * PAIRW + BQ:p * PAIRW + 2 * BQ] = dk_s[p].astype(BF16)
            dq_v[:, p * PAIRW + 2 * BQ:p * PAIRW + 3 * BQ] = dv_s[p].astype(BF16)
        _copy_in(dq_v, dqkv_hbm.at[:, pl.ds(g * 2 * PAIRW, 2 * PAIRW)], sem)
        if fox:
            lane = _iota((BQ, NSTAT), 1)

            def fold(n, _):
                t = jnp.zeros((BQ, NSTAT), F32)
                for c, (p, b) in enumerate(CHAINS):
                    s = jnp.sum(dc_s[c, pl.ds(pl.multiple_of(n * HB, HB), HB), :], axis=1, keepdims=True)
                    col = b * NH + 2 * p
                    t = t - jnp.where(lane == col, s[:BQ], 0.0) - jnp.where(lane == col + 1, s[BQ:], 0.0)
                db_ref[pl.ds(pl.multiple_of(n * BQ, BQ), BQ), :] = t
                return 0

            lax.fori_loop(0, NB, fold, 0)
        if comm:
            comm[1](st)

    if fox:
        bias_specs = [VMEM_SPEC, ANY_SPEC]
        db_shape = jax.ShapeDtypeStruct((S, NSTAT), F32)
        more = [SLAB_KEYB, pltpu.VMEM((NC, NB * HB, BQ), F32)]
    else:
        bias_specs = [VMEM_SPEC]
        db_shape = jax.ShapeDtypeStruct((2, NB, HB, BQ), F32)
        more = []
    n_in = 5 + len(bias_specs)
    (dqkv, db), extra = _host_call(
        body, "fox_bwd" if fox else "dil_bwd", [qkv, o, dmixed, lse, *bias, dqkv],
        [ANY_SPEC, ANY_SPEC, ANY_SPEC, VMEM_SPEC] + bias_specs + [ANY_SPEC],
        [jax.ShapeDtypeStruct((T, QKVW), BF16), db_shape], [ANY_SPEC, VMEM_SPEC],
        [SLAB_QKV, SLAB_O32 if fox else SLAB_OUT, SLAB_OUT, SLAB_QKV, ACC_KV, ACC_KV, pltpu.SemaphoreType.DMA((4,)), SLAB_T,
         PAIR_DIAG] + more, {n_in - 1: 0}, job)
    return dqkv, db, extra


def _delta_t(d):
    return d * BQ + _iota((HB, BQ), 1) - (_iota((HB, BQ), 0) & (BQ - 1))


def _buckets_in(d):
    lo, hi = max(d * BQ - (BQ - 1), 0), d * BQ + BQ - 1
    return [b for b in range(32) if BUCKET_TH[b] <= hi and (b == 31 or BUCKET_TH[b + 1] > lo)]


def _in_bucket(delta, b):
    m = delta >= BUCKET_TH[b]
    return m if b == 31 else m & (delta < BUCKET_TH[b + 1])


def _dil_table(rel_bias, job=None):
    def body(ins, outs, scr, comm):
        (rb_ref,), (o_ref,) = ins, outs
        st = comm[0]() if comm else None
        for d in range(NB):
            delta = _delta_t(d)
            pos = delta >= 0
            n = ((pos & (delta <= 128)).astype(jnp.int32)
                 + (pos & (delta <= 512) & ((delta & 3) == 0)).astype(jnp.int32)
                 + (pos & ((delta & 15) == 0)).astype(jnp.int32))
            logn = jnp.where(n == 3, math.log(3.0), jnp.where(n == 2, math.log(2.0), jnp.where(n == 1, 0.0, NEG)))
            head1 = _iota((HB, BQ), 0) >= BQ
            for p in range(2):
                val = jnp.zeros((HB, BQ), F32)
                for b in _buckets_in(d):
                    val = jnp.where(_in_bucket(delta, b), jnp.where(head1, rb_ref[b, 2 * p + 1], rb_ref[b, 2 * p]), val)
                o_ref[p, d] = val + logn
        if comm:
            comm[1](st)

    (tbl,), extra = _host_call(
        body, "dil_table", [rel_bias], [pl.BlockSpec(memory_space=pltpu.SMEM)],
        [jax.ShapeDtypeStruct((2, NB, HB, BQ), F32)], [VMEM_SPEC], [], {}, job)
    return (tbl, extra) if job else tbl


def _dil_table_bwd(dtbl):
    def body(dt_ref, o_ref):
        p = pl.program_id(0)
        rowi = _iota((32, BQ), 0)
        lanei = _iota((32, BQ), 1)

        @pl.when(p == 0)
        def _():
            o_ref[...] = jnp.zeros_like(o_ref)

        out = jnp.zeros((32, BQ), F32)
        for b in range(32):
            acc = None
            for d in range(NB):
                if b in _buckets_in(d):
                    t = jnp.where(_in_bucket(_delta_t(d), b), dt_ref[d], 0.0)
                    acc = t if acc is None else acc + t
            rs = jnp.sum(acc, axis=1, keepdims=True)
            s0 = jnp.sum(rs[:BQ], axis=0, keepdims=True)
            s1 = jnp.sum(rs[BQ:], axis=0, keepdims=True)
            out = (out + jnp.where((rowi == b) & (lanei == 2 * p), s0, 0.0)
                   + jnp.where((rowi == b) & (lanei == 2 * p + 1), s1, 0.0))
        o_ref[...] += out

    return pl.pallas_call(
        body, name="dil_table_bwd", grid=(2,),
        in_specs=[pl.BlockSpec((None, NB, HB, BQ), lambda p: (p, 0, 0, 0))],
        out_specs=pl.BlockSpec((32, BQ), lambda p: (0, 0)),
        out_shape=jax.ShapeDtypeStruct((32, BQ), F32),
        compiler_params=_cp(("arbitrary",)))(dtbl)


def _fox_prep(gate, fb):
    def body(g_ref, fb_ref, c_ref):
        tri = (_iota((BQ, BQ), 0) >= _iota((BQ, BQ), 1)).astype(BF16)

        def blk(i, carry):
            r0 = pl.multiple_of(i * BQ, BQ)
            lf = _log_sigmoid(g_ref[pl.ds(r0, BQ), :] + fb_ref[...])
            c = _dot(tri, _split3(lf))
            c_ref[pl.ds(r0, BQ), :] = c[:, 0:BQ] + c[:, BQ:2 * BQ] + c[:, 2 * BQ:3 * BQ] + carry
            return carry + jnp.sum(lf, axis=0, keepdims=True)

        lax.fori_loop(0, NB, blk, jnp.zeros((1, BQ), F32))

    blk = pl.BlockSpec((S, GATEW), lambda b: (b, 0))
    return pl.pallas_call(
        body, name="fox_prep", grid=(BL,), in_specs=[blk, pl.BlockSpec((1, GATEW), lambda b: (0, 0))],
        out_specs=blk, out_shape=jax.ShapeDtypeStruct((T, GATEW), F32),
        compiler_params=_cp(("parallel",)))(gate, fb)


def _fox_post(dcum, gate, fb):
    def body(dc_ref, g_ref, fb_ref, dg_ref, dfb_ref):
        b = pl.program_id(0)
        tri = (_iota((BQ, BQ), 0) <= _iota((BQ, BQ), 1)).astype(BF16)

        def blk(ii, carry):
            csum, dfb = carry
            r0 = pl.multiple_of((NB - 1 - ii) * BQ, BQ)
            dc = dc_ref[pl.ds(r0, BQ), :]
            c = _dot(tri, _split3(dc))
            dlf = c[:, 0:BQ] + c[:, BQ:2 * BQ] + c[:, 2 * BQ:3 * BQ] + csum
            dg = dlf * jnp.exp(_log_sigmoid(-(g_ref[pl.ds(r0, BQ), :] + fb_ref[...])))
            dg_ref[pl.ds(r0, BQ), :] = dg
            return csum + jnp.sum(dc, axis=0, keepdims=True), dfb + jnp.sum(dg, axis=0, keepdims=True)

        z = jnp.zeros((1, BQ), F32)
        _, dfb = lax.fori_loop(0, NB, blk, (z, z))

        @pl.when(b == 0)
        def _():
            dfb_ref[...] = dfb

        @pl.when(b > 0)
        def _():
            dfb_ref[...] += dfb

    blk = pl.BlockSpec((S, GATEW), lambda b: (b, 0))
    vec = pl.BlockSpec((1, GATEW), lambda b: (0, 0))
    return pl.pallas_call(
        body, name="fox_post", grid=(BL,), in_specs=[blk, blk, vec], out_specs=[blk, vec],
        out_shape=[jax.ShapeDtypeStruct((T, GATEW), F32), jax.ShapeDtypeStruct((1, GATEW), F32)],
        compiler_params=_cp(("arbitrary",)))(dcum, gate, fb)


def _shift_down(x, n):
    return jnp.where(_iota(x.shape, 0) >= n, pltpu.roll(x, n, 0), 0.0)


def _shift_up(x, n):
    return jnp.where(_iota(x.shape, 0) < S - n, pltpu.roll(x, S - n, 0), 0.0)


def _conv_fwd(conv, cw, mixed):
    W = 256

    def body(c_ref, w_ref, _, o_ref):
        u = c_ref[:, W:2 * W] * c_ref[:, 2 * W:3 * W]
        y = w_ref[0:1, :] * _shift_down(u, 2) + w_ref[1:2, :] * _shift_down(u, 1) + w_ref[2:3, :] * u
        o_ref[...] = (c_ref[:, 0:W] * y).astype(BF16)

    return pl.pallas_call(
        body, name="conv_fwd", grid=(BL,),
        in_specs=[pl.BlockSpec((S, CONVW), lambda b: (b, 0)), pl.BlockSpec((8, W), lambda b: (0, 0)), ANY_SPEC],
        out_specs=pl.BlockSpec((S, W), lambda b: (b, 3)),
        out_shape=jax.ShapeDtypeStruct((T, D), BF16), input_output_aliases={2: 0},
        compiler_params=_cp(("parallel",)))(conv, cw, mixed)


def _conv_bwd(conv, cw, dmixed):
    W = 256

    def body(c_ref, w_ref, do_ref, dc_ref, dw_ref):
        b = pl.program_id(0)
        bg = c_ref[:, 0:W]
        cg = c_ref[:, W:2 * W]
        hv = c_ref[:, 2 * W:3 * W]
        do = do_ref[...].astype(F32)
        u = cg * hv
        u1 = _shift_down(u, 1)
        u2 = _shift_down(u, 2)
        y = w_ref[0:1, :] * u2 + w_ref[1:2, :] * u1 + w_ref[2:3, :] * u
        dy = do * bg
        du = w_ref[2:3, :] * dy + w_ref[1:2, :] * _shift_up(dy, 1) + w_ref[0:1, :] * _shift_up(dy, 2)
        dc_ref[:, 0:W] = (do * y).astype(BF16)
        dc_ref[:, W:2 * W] = (du * hv).astype(BF16)
        dc_ref[:, 2 * W:3 * W] = (du * cg).astype(BF16)
        rowi = _iota((8, W), 0)
        dw = (jnp.where(rowi == 0, jnp.sum(dy * u2, axis=0, keepdims=True), 0.0)
              + jnp.where(rowi == 1, jnp.sum(dy * u1, axis=0, keepdims=True), 0.0)
              + jnp.where(rowi == 2, jnp.sum(dy * u, axis=0, keepdims=True), 0.0))

        @pl.when(b == 0)
        def _():
            dw_ref[...] = dw

        @pl.when(b > 0)
        def _():
            dw_ref[...] += dw

    return pl.pallas_call(
        body, name="conv_bwd", grid=(BL,),
        in_specs=[pl.BlockSpec((S, CONVW), lambda b: (b, 0)), pl.BlockSpec((8, W), lambda b: (0, 0)),
                  pl.BlockSpec((S, W), lambda b: (b, 3))],
        out_specs=[pl.BlockSpec((S, CONVW), lambda b: (b, 0)), pl.BlockSpec((8, W), lambda b: (0, 0))],
        out_shape=[jax.ShapeDtypeStruct((T, CONVW), BF16), jax.ShapeDtypeStruct((8, W), F32)],
        compiler_params=_cp(("arbitrary",)))(conv, cw, dmixed)


def _place():
    x, y, c = lax.axis_index("x"), lax.axis_index("y"), lax.axis_index("c")
    return x, y, c


def _chips_of(x, y):
    return [(1 - x, y), (x, 1 - y), (1 - x, 1 - y)]


def _dev(p):
    return 4 * p[0] + 2 * p[1] + p[2]


def _gather_job_a(shards):
    n = len(shards)

    def peers(x, y, c):
        return [(x, y, 1 - c)] + [(*chip, c) for chip in _chips_of(x, y)]

    def start(ins, outs, sems):
        send, recv, loc = sems
        x, y, c = _place()
        me = (x, y, c)
        cps = []
        for a in range(n):
            cps.append(pltpu.make_async_copy(ins[a], outs[a].at[_dev(me)], loc.at[a]))
            for k, peer in enumerate(peers(x, y, c)):
                cps.append(pltpu.make_async_remote_copy(
                    src_ref=ins[a], dst_ref=outs[a].at[_dev(me)], send_sem=send.at[a, k], recv_sem=recv.at[a, k],
                    device_id=peer, device_id_type=MESH))
        for cp in cps:
            cp.start()
        return cps

    def finish(cps, ins, outs, sems):
        send, recv, loc = sems
        x, y, c = _place()
        for a in range(n):
            for k, peer in enumerate(peers(x, y, c)):
                pltpu.make_async_remote_copy(
                    src_ref=ins[a], dst_ref=outs[a].at[_dev(peer)], send_sem=send.at[a, k], recv_sem=recv.at[a, k],
                    device_id=(x, y, c), device_id_type=MESH).wait_recv()
        for a in range(n):
            cps[5 * a].wait()
            for k in range(4):
                cps[5 * a + 1 + k].wait_send()

    return _Job(shards, [jax.ShapeDtypeStruct((NDEV,) + s.shape, s.dtype) for s in shards], {},
                [pltpu.SemaphoreType.DMA((n, 4)), pltpu.SemaphoreType.DMA((n, 4)), pltpu.SemaphoreType.DMA((n,))],
                start, finish)


def _gather_job_b(gathered):
    n = len(gathered)

    def start(ins, outs, sems):
        send, recv = sems
        x, y, c = _place()
        cps = []
        for a in range(n):
            for j, chip in enumerate(_chips_of(x, y)):
                blk = outs[a].at[_dev((*chip, c))]
                cps.append(pltpu.make_async_remote_copy(
                    src_ref=blk, dst_ref=blk, send_sem=send.at[a, j], recv_sem=recv.at[a, j],
                    device_id=(x, y, 1 - c), device_id_type=MESH))
        for cp in cps:
            cp.start()
        return cps

    def finish(cps, ins, outs, sems):
        send, recv = sems
        x, y, c = _place()
        for a in range(n):
            for j, chip in enumerate(_chips_of(x, y)):
                blk = outs[a].at[_dev((*chip, 1 - c))]
                pltpu.make_async_remote_copy(
                    src_ref=blk, dst_ref=blk, send_sem=send.at[a, j], recv_sem=recv.at[a, j],
                    device_id=(x, y, c), device_id_type=MESH).wait_recv()
        for cp in cps:
            cp.wait_send()

    return _Job(gathered, [jax.ShapeDtypeStruct(g.shape, g.dtype) for g in gathered], {a: a for a in range(n)},
                [pltpu.SemaphoreType.DMA((n, 3)), pltpu.SemaphoreType.DMA((n, 3))], start, finish)


def _sibling_job(grads):
    n = len(grads)

    def start(ins, outs, sems):
        send, recv = sems
        x, y, c = _place()
        cps = [pltpu.make_async_remote_copy(
            src_ref=ins[a].at[:, 1 - c], dst_ref=outs[a], send_sem=send.at[a], recv_sem=recv.at[a],
            device_id=(x, y, 1 - c), device_id_type=MESH) for a in range(n)]
        for cp in cps:
            cp.start()
        return cps

    def finish(cps, ins, outs, sems):
        for cp in cps:
            cp.wait()

    return _Job(grads, [jax.ShapeDtypeStruct(g.shape[:1] + g.shape[2:], F32) for g in grads], {},
                [pltpu.SemaphoreType.DMA((n,)), pltpu.SemaphoreType.DMA((n,))], start, finish)


def _chip_job(psums):
    n = len(psums)

    def copies(ins, outs, sems):
        send, recv, loc = sems
        x, y, c = _place()
        mychip = 2 * x + y
        cps = []
        for a in range(n):
            cps.append(pltpu.make_async_copy(ins[a].at[mychip], outs[a].at[mychip], loc.at[a]))
            for j, chip in enumerate(_chips_of(x, y)):
                cps.append(pltpu.make_async_remote_copy(
                    src_ref=ins[a].at[2 * chip[0] + chip[1]], dst_ref=outs[a].at[mychip],
                    send_sem=send.at[a, j], recv_sem=recv.at[a, j], device_id=(*chip, c), device_id_type=MESH))
        return cps

    def start(ins, outs, sems):
        for cp in copies(ins, outs, sems):
            cp.start()

    def finish(_, ins, outs, sems):
        cps = copies(ins, outs, sems)
        send, recv, loc = sems
        x, y, c = _place()
        mychip = 2 * x + y
        for a in range(n):
            for j, chip in enumerate(_chips_of(x, y)):
                pltpu.make_async_remote_copy(
                    src_ref=ins[a].at[mychip], dst_ref=outs[a].at[2 * chip[0] + chip[1]],
                    send_sem=send.at[a, j], recv_sem=recv.at[a, j], device_id=(x, y, c), device_id_type=MESH).wait_recv()
        for a in range(n):
            cps[4 * a].wait()
            for j in range(3):
                cps[4 * a + 1 + j].wait_send()

    return _Job(psums, [jax.ShapeDtypeStruct(p.shape, BF16) for p in psums], {},
                [pltpu.SemaphoreType.DMA((n, 3)), pltpu.SemaphoreType.DMA((n, 3)), pltpu.SemaphoreType.DMA((n,))],
                start, finish)


def _join_jobs(*jobs):
    jobs = [j for j in jobs if j is not None]
    if len(jobs) <= 1:
        return jobs[0] if jobs else None
    cut = lambda seq, sizes: [seq[sum(sizes[:k]):sum(sizes[:k + 1])] for k in range(len(sizes))]
    n_in = [len(j.ins) for j in jobs]
    n_out = [len(j.out_shapes) for j in jobs]
    n_sem = [len(j.sems) for j in jobs]
    aliases = {}
    for k, j in enumerate(jobs):
        for a, b in j.aliases.items():
            aliases[sum(n_in[:k]) + a] = sum(n_out[:k]) + b

    def start(ins, outs, sems):
        return [j.start(i, o, s) for j, i, o, s in zip(jobs, cut(ins, n_in), cut(outs, n_out), cut(sems, n_sem))]

    def finish(sts, ins, outs, sems):
        for j, st, i, o, s in zip(jobs, sts, cut(ins, n_in), cut(outs, n_out), cut(sems, n_sem)):
            j.finish(st, i, o, s)

    return _Job([t for j in jobs for t in j.ins], [t for j in jobs for t in j.out_shapes], aliases,
                [t for j in jobs for t in j.sems], start, finish)


def _run_job(job, name):
    def body(ins, outs, scr, comm):
        comm[1](comm[0]())

    return _host_call(body, name, [], [], [], [], [], {}, job)[1]


def _allreduce_small(v, job=None):
    def body(ins, outs, scr, comm):
        (v_ref,), (o_ref,), (slots, send_sems, recv_sems) = ins, outs, scr
        st = comm[0]() if comm else None
        x, y, c = _place()
        me = 4 * x + 2 * y + c
        slots[me] = v_ref[...]

        def copy(k):
            peer = (x ^ ((k >> 2) & 1), y ^ ((k >> 1) & 1), c ^ (k & 1))
            return pltpu.make_async_remote_copy(
                src_ref=v_ref, dst_ref=slots.at[me], send_sem=send_sems.at[k - 1], recv_sem=recv_sems.at[k - 1],
                device_id=peer, device_id_type=MESH)

        def arrival(k):
            return pltpu.make_async_remote_copy(
                src_ref=v_ref, dst_ref=slots.at[me ^ k], send_sem=send_sems.at[k - 1], recv_sem=recv_sems.at[k - 1],
                device_id=(x, y, c), device_id_type=MESH)

        sends = [copy(k) for k in range(1, NDEV)]
        for cp in sends:
            cp.start()
        for k in range(1, NDEV):
            arrival(k).wait_recv()
        for cp in sends:
            cp.wait_send()
        acc = slots[0]
        for d in range(1, NDEV):
            acc = acc + slots[d]
        o_ref[...] = acc
        if comm:
            comm[1](st)

    (out,), extra = _host_call(
        body, "allreduce_small", [v], [VMEM_SPEC], [jax.ShapeDtypeStruct(v.shape, F32)], [VMEM_SPEC],
        [pltpu.VMEM((NDEV,) + v.shape, F32), pltpu.SemaphoreType.DMA((NDEV - 1,)),
         pltpu.SemaphoreType.DMA((NDEV - 1,))], {}, job)
    return (out, extra) if job else out


def _pair_sums(views, gots, core):
    n = len(views)

    def body(c_ref, *refs):
        for a in range(n):
            refs[2 * n + a][...] = (refs[a][...] + refs[n + a][...]).astype(BF16)

    def vspec(v):
        return pl.BlockSpec((None, None, v.shape[2] // 2, v.shape[3]), lambda k, h, c: (k, c[0], h, 0))

    def gspec(g):
        return pl.BlockSpec((None, g.shape[1] // 2, g.shape[2]), lambda k, h, c: (k, h, 0))

    return pl.pallas_call(
        body, name="pair_sums",
        grid_spec=pltpu.PrefetchScalarGridSpec(
            num_scalar_prefetch=1, grid=(4, 2),
            in_specs=[vspec(v) for v in views] + [gspec(g) for g in gots],
            out_specs=[gspec(g) for g in gots]),
        out_shape=[jax.ShapeDtypeStruct(g.shape, BF16) for g in gots],
        compiler_params=_cp(("parallel", "parallel")))(core, *views, *gots)


def _chip_sums(parts):
    n = len(parts)

    def body(*refs):
        for a in range(n):
            acc = refs[a][0].astype(F32)
            for k in range(1, 4):
                acc = acc + refs[a][k].astype(F32)
            refs[n + a][...] = acc

    return pl.pallas_call(
        body, name="chip_sums", in_specs=[VMEM_SPEC] * n, out_specs=[VMEM_SPEC] * n,
        out_shape=[jax.ShapeDtypeStruct(p.shape[1:], F32) for p in parts], compiler_params=_cp())(*parts)


def _permute_in(w):
    lead = w.shape[:-1]
    return w.reshape(lead + (3, 3, 2, BQ)).swapaxes(-2, -3).reshape(lead + (QKVW,))


def _unpermute_in(w):
    lead = w.shape[:-1]
    return w.reshape(lead + (3, 2, 3, BQ)).swapaxes(-2, -3).reshape(lead + (QKVW,))


def _row(v):
    v = v.reshape(-1)
    return jnp.pad(v, (0, D - v.shape[0])).reshape(1, D)


def kernel(x, w_in, f_bias, conv_w, w_out, rel_bias, ln1_g, ln1_b, w_gate, w_up, w_down, ln2_g, ln2_b, loss_target, m_w_in, m_f_bias, m_conv_w, m_w_out, m_rel_bias, m_ln1_g, m_ln1_b, m_w_gate, m_w_up, m_w_down, m_ln2_g, m_ln2_b, v_w_in, v_f_bias, v_conv_w, v_w_out, v_rel_bias, v_ln1_g, v_ln1_b, v_w_gate, v_w_up, v_w_down, v_ln2_g, v_ln2_b):
    xi, yi, ci = _place()
    me = 4 * xi + 2 * yi + ci
    core = jnp.reshape(ci, (1,)).astype(jnp.int32)

    win_s = jnp.concatenate([_permute_in(w_in[..., :QKVW]), w_in[..., QKVW:]], axis=-1)
    win_s = jnp.pad(win_s, ((0, 0), (0, 0), (0, NPAD - NPROJ))).astype(BF16)
    per_layer = [win_s, w_out.astype(BF16), jnp.swapaxes(w_gate, 1, 2).astype(BF16),
                 jnp.swapaxes(w_up, 1, 2).astype(BF16), w_down.astype(BF16)]
    sh = [[s[l] for s in per_layer] for l in range(2)]

    def whole(g):
        return g.reshape(NDEV * g.shape[1], g.shape[2])

    cw_rows = lax.dynamic_update_slice(jnp.zeros((2, 3, 256), F32), conv_w, (0, 0, me * 32))
    small = jnp.concatenate([_row(cw_rows[0]), _row(cw_rows[1]), jnp.zeros((SMALL_ROWS - 2, D), F32)], axis=0)
    small, leg_a = _allreduce_small(small, job=_gather_job_a(sh[0][:1]))
    cw_full = small[0:2, :CONVW].reshape(2, 3, 256)
    cw8 = jnp.pad(cw_full, ((0, 0), (0, 5), (0, 0)))
    fb = jnp.pad(f_bias, ((0, 0), (0, GATEW - NH))).reshape(2, 1, GATEW)
    tbl, leg_b = _dil_table(rel_bias, job=_gather_job_b(list(leg_a)))
    W = [{"win": whole(leg_b[0])}, {}]

    def wrow(tn, K, blk=0):
        return pl.BlockSpec((tn, K), lambda i, j: (j, blk))

    def arow(tm, K, blk=0):
        return pl.BlockSpec((tm, K), lambda i, j: (i, blk))

    h = x.reshape(T, D)
    hb = h.astype(BF16)
    saved = []
    for l in range(2):
        Win = W[l]["win"]
        qkv, conv, gate = _proj(hb, Win)
        cum = _fox_prep(gate, fb[l])
        cq = cum[:, :NH].reshape(BL, S, NH).transpose(0, 2, 1).reshape(NSTAT, S)
        ckb = jnp.broadcast_to(cq[:, :, None], (NSTAT, S, BQ))
        if l == 0:
            mixed, amat, a0 = _sb_fwd(qkv, job=_gather_job_a(sh[0][1:]))
            mixed, lse_d, ex = _flash_fwd(qkv, mixed, 1, False, (tbl,),
                                          job=_join_jobs(_gather_job_b(list(a0)), _gather_job_a(sh[1][:2])))
            W[0].update(zip(("wout", "wgT", "wuT", "wd"), [whole(t) for t in ex[:4]]))
            mixed, lse_f, o_fox, ex = _flash_fwd(qkv, mixed, 2, True, (cq, ckb),
                                                 job=_join_jobs(_gather_job_b(list(ex[4:])), _gather_job_a(sh[1][2:])))
            W[1].update(zip(("win", "wout"), [whole(t) for t in ex[:2]]))
            a2 = list(ex[2:])
        else:
            mixed, amat, ex = _sb_fwd(qkv, job=_gather_job_b(a2))
            W[1].update(zip(("wgT", "wuT", "wd"), [whole(t) for t in ex]))
            mixed, lse_d, _ = _flash_fwd(qkv, mixed, 1, False, (tbl,))
            mixed, lse_f, o_fox, _ = _flash_fwd(qkv, mixed, 2, True, (cq, ckb))
        Wout, WgT, WuT, Wd = W[l]["wout"], W[l]["wgT"], W[l]["wuT"], W[l]["wd"]
        mixed = _conv_fwd(conv, cw8[l], mixed)
        x1, xh1, r1, x1b = _mm_ln(mixed, Wout, h, ln1_g[l:l + 1], ln1_b[l:l + 1], "out_proj_ln")
        fs, ft, a, x2, xh2, r2, x2b = _ffn_fwd(x1b, x1, WgT, WuT, Wd, ln2_g[l:l + 1], ln2_b[l:l + 1])
        saved.append(dict(h=hb, qkv=qkv, conv=conv, gate=gate, cq=cq, ckb=ckb, mixed=mixed, amat=amat, lse_d=lse_d,
                          lse_f=lse_f, o_fox=o_fox, x1=x1b, xh1=xh1, r1=r1, fs=fs, ft=ft, a=a, xh2=xh2, r2=r2))
        h, hb = x2, x2b

    dy = h

    def view(gr):
        return gr.reshape(4, 2, gr.shape[0] // NDEV, gr.shape[1])

    G = [None, None]
    small_g = {}
    shard_g = {}
    for l in (1, 0):
        sv = saved[l]
        Win, Wout, WgT, WuT, Wd = W[l]["win"], W[l]["wout"], W[l]["wgT"], W[l]["wuT"], W[l]["wd"]
        res = _ffn_bwd(dy, sv["xh2"], sv["r2"], ln2_g[l:l + 1], sv["fs"], sv["ft"], Wd, WgT, WuT,
                       target=loss_target.reshape(T, D) if l == 1 else None)
        dgt, dut, ds2b, dx1, dg2, db2 = res[:6]
        if l == 1:
            sq = res[6]
        G_d = _mm_tn(sv["a"], ds2b, None, C=D, Ka=DFF, N=D, tm=256, tn=1024, tk=T, ooff=0, name="grad_w_down")
        G_g = _mm_tn(dgt, sv["x1"], None, C=D, Ka=DFF, N=D, tm=256, tn=1024, tk=T, ooff=0, name="grad_w_gate")
        G_u = _mm_tn(dut, sv["x1"], None, C=D, Ka=DFF, N=D, tm=256, tn=1024, tk=T, ooff=0, name="grad_w_up")
        ds1, dg1, db1, ds1b, dmixed = _ln_bwd(dx1, sv["xh1"], sv["r1"], ln1_g[l:l + 1], Wout)
        G_out = _mm_tn(sv["mixed"], ds1b, None, C=D, Ka=D, N=D, tm=256, tn=1024, tk=T, ooff=0, name="grad_w_out")
        early = [view(t) for t in (G_g, G_u, G_d, G_out)] + ([view(G[1]["in"])] if l == 0 else [])
        dqkv, gots = _sb_bwd(sv["qkv"], dmixed, sv["amat"], job=_sibling_job(early))
        ps = _pair_sums(early, list(gots), core)
        dqkv, dtbl, pa = _flash_bwd(sv["qkv"], sv["mixed"], dmixed, sv["lse_d"], dqkv, 1, False, (tbl,),
                                    job=_chip_job(ps[:2]))
        dqkv, dck, pb = _flash_bwd(sv["qkv"], sv["o_fox"], dmixed, sv["lse_f"], dqkv, 2, True,
                                   (sv["cq"], sv["ckb"]), job=_chip_job(ps[2:]))
        sums = _chip_sums(list(pa) + list(pb))
        shard_g[l] = dict(zip(("g", "u", "d", "out"), sums[:4]))
        if l == 0:
            shard_g[1]["in"] = sums[4]
        dconv, dcw = _conv_bwd(sv["conv"], cw8[l], dmixed)
        dcum = jnp.pad(dck.reshape(S, BL, NH).transpose(1, 0, 2).reshape(T, NH), ((0, 0), (0, GATEW - NH)))
        dgate, dfb = _fox_post(dcum, sv["gate"], fb[l])
        drb = _dil_table_bwd(dtbl)
        G_in = _mm_tn(sv["h"], dqkv, None, C=NPAD, Ka=D, N=QKVW, tm=512, tn=768, tk=T, ooff=0, name="grad_w_in_qkv")
        G_in = _mm_tn(sv["h"], dconv, G_in, C=NPAD, Ka=D, N=CONVW, tm=256, tn=768, tk=T, ooff=3,
                      name="grad_w_in_conv")
        G_in = _mm_tn(sv["h"], dgate, G_in, C=NPAD, Ka=D, N=GATEW, tm=1024, tn=128, tk=1024, ooff=24,
                      name="grad_w_in_gate")
        G[l] = {"in": G_in, "out": G_out, "g": G_g, "u": G_u, "d": G_d}
        if l == 0:
            late = [view(G_in)]
            tail = _chip_job(_pair_sums(late, list(_run_job(_sibling_job(late), "sibling_exchange")), core))
            dy, parts = _mm([(dqkv, arow(1024, QKVW), Win, wrow(512, QKVW, 0)),
                             (dconv, arow(1024, CONVW), Win, wrow(512, CONVW, 3)),
                             (dgate, arow(1024, GATEW), Win, wrow(512, GATEW, 24))],
                            nt=True, M=T, N=D, tm=1024, tn=512, out_dtype=F32, name="proj_dx", res=ds1,
                            res_scale=ALPHA, job=tail)
            shard_g[0]["in"] = _chip_sums(list(parts))[0]
        else:
            dy = _proj_bwd(dqkv, dconv, dgate, Win, ds1)
        small_g[l] = dict(ln1_g=dg1, ln1_b=db1, ln2_g=dg2, ln2_b=db2, cw=dcw[0:3].reshape(1, CONVW),
                          fb=dfb[:, :NH], rb=drb[:, :NH])
    grad_x = dy.reshape(BL, S, D)

    rows = []
    for name in ("ln1_g", "ln1_b", "ln2_g", "ln2_b"):
        rows += [small_g[0][name], small_g[1][name]]
    rows += [_row(small_g[0]["cw"]), _row(small_g[1]["cw"]),
             _row(jnp.concatenate([small_g[0]["fb"], small_g[1]["fb"]], axis=0)),
             _row(small_g[0]["rb"] + small_g[1]["rb"]), _row(sq)]
    rows.append(jnp.zeros((SMALL_ROWS - len(rows), D), F32))
    sg = _allreduce_small(jnp.concatenate(rows, axis=0))
    loss = sg[12, 0] * (0.5 / D)
    g_ln1_g, g_ln1_b, g_ln2_g, g_ln2_b = sg[0:2], sg[2:4], sg[4:6], sg[6:8]
    g_conv_full = sg[8:10, :CONVW].reshape(2, 3, 256)
    g_conv = lax.dynamic_slice(g_conv_full, (0, 0, me * 32), (2, 3, 32))
    g_fb = sg[10, :2 * NH].reshape(2, NH)
    g_rb = sg[11, :32 * NH].reshape(32, NH)

    def both(name):
        return jnp.stack([shard_g[0][name], shard_g[1][name]])

    g_in = both("in")
    g_w_in = jnp.concatenate([_unpermute_in(g_in[..., :QKVW]), g_in[..., QKVW:NPROJ]], axis=-1)
    g_w_out = both("out")
    g_w_gate = jnp.swapaxes(both("g"), 1, 2)
    g_w_up = jnp.swapaxes(both("u"), 1, 2)
    g_w_down = both("d")

    up_in = _adamw(w_in, g_w_in, m_w_in, v_w_in, 64)
    up_out = _adamw(w_out, g_w_out, m_w_out, v_w_out, 128)
    up_gate = _adamw(w_gate, g_w_gate, m_w_gate, v_w_gate, 256)
    up_up = _adamw(w_up, g_w_up, m_w_up, v_w_up, 256)
    up_down = _adamw(w_down, g_w_down, m_w_down, v_w_down, 352)

    def pack(fbv, cwv, rbv, l1g, l1b, l2g, l2b):
        r = [l1g, l1b, l2g, l2b, _row(cwv), _row(fbv), _row(rbv)]
        r.append(jnp.zeros((SMALL_ROWS - 11, D), F32))
        return jnp.concatenate(r, axis=0)

    pw = pack(f_bias, conv_w, rel_bias, ln1_g, ln1_b, ln2_g, ln2_b)
    pg = pack(g_fb, g_conv, g_rb, g_ln1_g, g_ln1_b, g_ln2_g, g_ln2_b)
    pm = pack(m_f_bias, m_conv_w, m_rel_bias, m_ln1_g, m_ln1_b, m_ln2_g, m_ln2_b)
    pv = pack(v_f_bias, v_conv_w, v_rel_bias, v_ln1_g, v_ln1_b, v_ln2_g, v_ln2_b)
    ups = [u[0] for u in _adamw(pw[None], pg[None], pm[None], pv[None], SMALL_ROWS)]

    def unpack(p):
        return dict(ln1_g=p[0:2], ln1_b=p[2:4], ln2_g=p[4:6], ln2_b=p[6:8],
                    conv_w=p[8, :192].reshape(2, 3, 32), f_bias=p[9, :2 * NH].reshape(2, NH),
                    rel_bias=p[10, :32 * NH].reshape(32, NH))

    sm = [unpack(p) for p in ups]

    def group(k):
        return (up_in[k], sm[k]["f_bias"], sm[k]["conv_w"], up_out[k], sm[k]["rel_bias"], sm[k]["ln1_g"],
                sm[k]["ln1_b"], up_gate[k], up_up[k], up_down[k], sm[k]["ln2_g"], sm[k]["ln2_b"])

    grads = (g_w_in, g_fb, g_conv, g_w_out, g_rb, g_ln1_g, g_ln1_b, g_w_gate, g_w_up, g_w_down, g_ln2_g, g_ln2_b)
    return (loss, grad_x) + grads + group(0) + group(1) + group(2)
```

```python
import math

import numpy as np
import jax
import jax.numpy as jnp
from jax import lax
from jax.experimental import pallas as pl
from jax.experimental.pallas import tpu as pltpu

F32 = jnp.float32
BF16 = jnp.bfloat16
MESH = pl.DeviceIdType.MESH

D = 1024
S = 2048
BL = 2
T = BL * S
NH = 4
DFF = 2816
NPROJ = 3076
NPAD = 3200
QKVW = 2304
CONVW = 768
GATEW = 128
PAIRW = 384
BQ = 128
HB = 2 * BQ
NB = S // BQ
NDEV = 8
NSTAT = BL * NH
ALPHA = 4.0 ** 0.25
SCALE = 0.125
NEG = -1e30
LN_EPS = 1e-5
ADAM_LR, ADAM_B1, ADAM_B2, ADAM_EPS, ADAM_WD, ADAM_STEP = 0.001, 0.9, 0.999, 1e-08, 0.01, 10
VMEM_LIMIT = 56 * 1024 * 1024
SMALL_ROWS = 16


def _bucket_thresholds():
    d = np.arange(0, S)
    nf = np.maximum(d, 1).astype(np.float32)
    large = 16 + (np.log(nf / np.float32(16)) / np.float32(math.log(128)) * np.float32(16)).astype(np.int32)
    b = np.where(d < 16, d, np.minimum(large, 31))
    return [int(np.argmax(b >= k)) for k in range(32)]


BUCKET_TH = _bucket_thresholds()


def _cp(sem=None, vmem=VMEM_LIMIT):
    return pltpu.CompilerParams(dimension_semantics=sem, vmem_limit_bytes=vmem)


def _dot(a, b):
    return lax.dot_general(a, b, (((1,), (0,)), ((), ())), preferred_element_type=F32)


def _dot_nt(a, b):
    return lax.dot_general(a, b, (((1,), (1,)), ((), ())), preferred_element_type=F32)


def _dot_tn(a, b):
    return lax.dot_general(a, b, (((0,), (0,)), ((), ())), preferred_element_type=F32)


def _split2(x):
    hi = x.astype(BF16)
    mid = (x - hi.astype(F32)).astype(BF16)
    return jnp.concatenate([hi, mid], axis=1)


def _split3(x):
    hi = x.astype(BF16)
    r = x - hi.astype(F32)
    mid = r.astype(BF16)
    lo = (r - mid.astype(F32)).astype(BF16)
    return jnp.concatenate([hi, mid, lo], axis=1)


def _log_sigmoid(u):
    return jnp.minimum(u, 0.0) - jnp.log1p(jnp.exp(-jnp.abs(u)))


def _log_sigmoid_tile(u):
    return jnp.minimum(u, 0.0) - jnp.log(1.0 + jnp.exp(jnp.minimum(u, -u)))


def _iota(shape, dim):
    return lax.broadcasted_iota(jnp.int32, shape, dim)


ANY_SPEC = pl.BlockSpec(memory_space=pl.ANY)
VMEM_SPEC = pl.BlockSpec(memory_space=pltpu.VMEM)


def _mm(pairs, *, nt, M, N, tm, tn, out_dtype, name, res=None, res_scale=1.0, job=None):
    n = len(pairs)
    n_in = 2 * n + (res is not None)
    jins = job.ins if job else []
    jouts = job.out_shapes if job else []
    gi, gj = M // tm, N // tn

    def body(*refs):
        o_ref = refs[n_in + len(jins)]
        if job:
            jrefs = (refs[n_in:n_in + len(jins)], refs[n_in + len(jins) + 1:n_in + len(jins) + 1 + len(jouts)],
                     refs[n_in + len(jins) + 1 + len(jouts):])

            @pl.when((pl.program_id(0) == 0) & (pl.program_id(1) == 0))
            def _():
                job.start(*jrefs)

        acc = None
        for p in range(n):
            a = refs[2 * p][...].astype(BF16)
            b = refs[2 * p + 1][...]
            d = _dot_nt(a, b) if nt else _dot(a, b)
            acc = d if acc is None else acc + d
        if res is not None:
            acc = acc + res_scale * refs[2 * n][...]
        o_ref[...] = acc.astype(out_dtype)
        if job:
            @pl.when((pl.program_id(0) == gi - 1) & (pl.program_id(1) == gj - 1))
            def _():
                job.finish(None, *jrefs)

    ops, specs = [], []
    for a, asp, b, bsp in pairs:
        ops += [a, b]
        specs += [asp, bsp]
    if res is not None:
        ops.append(res)
        specs.append(pl.BlockSpec((tm, tn), lambda i, j: (i, j)))
    out = pl.pallas_call(
        body, name=name, grid=(gi, gj), in_specs=specs + [ANY_SPEC] * len(jins),
        out_specs=[pl.BlockSpec((tm, tn), lambda i, j: (i, j))] + [ANY_SPEC] * len(jouts),
        out_shape=[jax.ShapeDtypeStruct((M, N), out_dtype)] + list(jouts),
        scratch_shapes=list(job.sems) if job else [],
        input_output_aliases={n_in + a: 1 + b for a, b in job.aliases.items()} if job else {},
        compiler_params=_cp(("arbitrary", "arbitrary") if job else ("parallel", "parallel")))(*ops, *jins)
    return (out[0], out[1:]) if job else out[0]


def _mm_tn(a, b, gbuf, *, C, Ka, N, tm, tn, tk, ooff, name):
    def body(*refs):
        a_ref, b_ref, o_ref = refs[0], refs[1], refs[-1]
        k = pl.program_id(2)
        d = _dot_tn(a_ref[...].astype(BF16), b_ref[...].astype(BF16))

        @pl.when(k == 0)
        def _():
            o_ref[...] = d

        @pl.when(k > 0)
        def _():
            o_ref[...] += d

    ops = [a, b] + ([] if gbuf is None else [gbuf])
    return pl.pallas_call(
        body, name=name, grid=(Ka // tm, N // tn, T // tk),
        in_specs=[pl.BlockSpec((tk, tm), lambda i, j, k: (k, i)),
                  pl.BlockSpec((tk, tn), lambda i, j, k: (k, j))] + ([] if gbuf is None else [ANY_SPEC]),
        out_specs=pl.BlockSpec((tm, tn), lambda i, j, k: (i, ooff + j)),
        out_shape=jax.ShapeDtypeStruct((Ka, C), F32),
        input_output_aliases={} if gbuf is None else {2: 0},
        compiler_params=_cp(("parallel", "parallel", "arbitrary")))(*ops)


def _proj(xb, w):
    tm = 512

    def body(x_ref, w_ref, qkv_ref, conv_ref, gate_ref):
        xv = x_ref[...]
        qkv_ref[...] = _dot(xv, w_ref[:, 0:QKVW]).astype(BF16)
        conv_ref[...] = _dot(xv, w_ref[:, QKVW:QKVW + CONVW])
        gate_ref[...] = _dot(xv, w_ref[:, QKVW + CONVW:NPAD])

    def rows(n):
        return pl.BlockSpec((tm, n), lambda i: (i, 0))

    return pl.pallas_call(
        body, name="proj", grid=(T // tm,),
        in_specs=[rows(D), pl.BlockSpec((D, NPAD), lambda i: (0, 0))],
        out_specs=[rows(QKVW), rows(CONVW), rows(GATEW)],
        out_shape=[jax.ShapeDtypeStruct((T, QKVW), BF16), jax.ShapeDtypeStruct((T, CONVW), F32),
                   jax.ShapeDtypeStruct((T, GATEW), F32)],
        compiler_params=_cp(("parallel",)))(xb, w)


def _proj_bwd(dqkv, dconv, dgate, w, res):
    tm = 512

    def body(a_ref, b_ref, c_ref, w_ref, r_ref, o_ref):
        acc = ALPHA * r_ref[...] + _dot_nt(a_ref[...], w_ref[:, 0:QKVW])
        acc = acc + _dot_nt(b_ref[...], w_ref[:, QKVW:QKVW + CONVW])
        o_ref[...] = acc + _dot_nt(c_ref[...].astype(BF16), w_ref[:, QKVW + CONVW:NPAD])

    def rows(n):
        return pl.BlockSpec((tm, n), lambda i: (i, 0))

    return pl.pallas_call(
        body, name="proj_bwd", grid=(T // tm,),
        in_specs=[rows(QKVW), rows(CONVW), rows(GATEW), pl.BlockSpec((D, NPAD), lambda i: (0, 0)), rows(D)],
        out_specs=rows(D), out_shape=jax.ShapeDtypeStruct((T, D), F32),
        compiler_params=_cp(("parallel",)))(dqkv, dconv, dgate, w, res)


def _ffn_fwd(xb, x, wgt, wut, wd, gam, bet):
    tm, ch = 512, 256

    def body(xb_ref, x_ref, g_ref, b_ref, wg_hbm, wu_hbm, wd_hbm,
             go_ref, uo_ref, ao_ref, y_ref, xh_ref, r_ref, yb_ref, wg_v, wu_v, wd_v, sem):
        loads = [pltpu.make_async_copy(s, d, sem.at[k])
                 for k, (s, d) in enumerate(((wg_hbm, wg_v), (wu_hbm, wu_v), (wd_hbm, wd_v)))]

        @pl.when(pl.program_id(0) == 0)
        def _():
            for cp in loads:
                cp.start()
            loads[0].wait()
            loads[1].wait()

        xv = xb_ref[...]
        for c in range(0, DFF, ch):
            gv = _dot_nt(xv, wg_v[c:c + ch, :])
            uv = _dot_nt(xv, wu_v[c:c + ch, :])
            go_ref[:, c:c + ch] = gv.astype(BF16)
            uo_ref[:, c:c + ch] = uv.astype(BF16)
            ao_ref[:, c:c + ch] = (gv * jax.nn.sigmoid(gv) * uv).astype(BF16)
        @pl.when(pl.program_id(0) == 0)
        def _():
            loads[2].wait()

        s = ALPHA * x_ref[...] + _dot(ao_ref[...], wd_v[...])
        mu = jnp.mean(s, axis=-1, keepdims=True)
        xc = s - mu
        var = jnp.mean(xc * xc, axis=-1, keepdims=True)
        r = lax.rsqrt(var + LN_EPS)
        xh = xc * r
        xh_ref[...] = xh.astype(BF16)
        r_ref[...] = r
        y = xh * g_ref[...] + b_ref[...]
        y_ref[...] = y
        yb_ref[...] = y.astype(BF16)

    row = pl.BlockSpec((tm, D), lambda i: (i, 0))
    wide = pl.BlockSpec((tm, DFF), lambda i: (i, 0))
    vec = pl.BlockSpec((1, D), lambda i: (0, 0))
    wsl = pltpu.VMEM((DFF, D), BF16)
    hid = jax.ShapeDtypeStruct((T, DFF), BF16)
    return pl.pallas_call(
        body, name="ffn_fwd", grid=(T // tm,),
        in_specs=[row, row, vec, vec, ANY_SPEC, ANY_SPEC, ANY_SPEC],
        out_specs=[wide, wide, wide, row, row, pl.BlockSpec((tm, 1), lambda i: (i, 0)), row],
        out_shape=[hid, hid, hid, jax.ShapeDtypeStruct((T, D), F32), jax.ShapeDtypeStruct((T, D), BF16),
                   jax.ShapeDtypeStruct((T, 1), F32), jax.ShapeDtypeStruct((T, D), BF16)],
        scratch_shapes=[wsl, wsl, wsl, pltpu.SemaphoreType.DMA((3,))],
        compiler_params=_cp(("arbitrary",)))(xb, x, gam, bet, wgt, wut, wd)


def _ffn_bwd(dy, xh, r, gam, g, u, wd, wgt, wut, target=None):
    tm, ch = 256, 256

    def body(*refs):
        if target is None:
            (dy_ref, xh_ref, r_ref, gam_ref, g_ref, u_ref, wd_hbm, wg_hbm, wu_hbm,
             dg_ref, du_ref, dsb_ref, dx_ref, dgam_ref, dbet_ref, wd_v, wg_v, wu_v, sem) = refs
        else:
            (dy_ref, t_ref, xh_ref, r_ref, gam_ref, g_ref, u_ref, wd_hbm, wg_hbm, wu_hbm,
             dg_ref, du_ref, dsb_ref, dx_ref, dgam_ref, dbet_ref, sq_ref, wd_v, wg_v, wu_v, sem) = refs
        loads = [pltpu.make_async_copy(s, d, sem.at[k])
                 for k, (s, d) in enumerate(((wd_hbm, wd_v), (wg_hbm, wg_v), (wu_hbm, wu_v)))]

        @pl.when(pl.program_id(0) == 0)
        def _():
            for cp in loads:
                cp.start()
            loads[0].wait()

        if target is None:
            dyv = dy_ref[...]
        else:
            e = dy_ref[...] - t_ref[...]
            dyv = e * (1.0 / D)
            p = jnp.sum(jnp.sum(e * e, axis=1, keepdims=True), axis=0, keepdims=True)

            @pl.when(pl.program_id(0) == 0)
            def _():
                sq_ref[...] = p

            @pl.when(pl.program_id(0) > 0)
            def _():
                sq_ref[...] += p

        xhv = xh_ref[...].astype(F32)
        dxh = dyv * gam_ref[...]
        m1 = jnp.mean(dxh, axis=-1, keepdims=True)
        m2 = jnp.mean(dxh * xhv, axis=-1, keepdims=True)
        ds = r_ref[...] * (dxh - m1 - xhv * m2)
        pg = jnp.sum(dyv * xhv, axis=0, keepdims=True)
        pb = jnp.sum(dyv, axis=0, keepdims=True)

        @pl.when(pl.program_id(0) == 0)
        def _():
            dgam_ref[...] = pg
            dbet_ref[...] = pb

        @pl.when(pl.program_id(0) > 0)
        def _():
            dgam_ref[...] += pg
            dbet_ref[...] += pb

        db = ds.astype(BF16)
        dsb_ref[...] = db
        for c in range(0, DFF, ch):
            da = _dot_nt(db, wd_v[c:c + ch, :])
            gv = g_ref[:, c:c + ch].astype(F32)
            sg = jax.nn.sigmoid(gv)
            dg_ref[:, c:c + ch] = (da * u_ref[:, c:c + ch].astype(F32) * (sg * (1.0 + gv * (1.0 - sg)))).astype(BF16)
            du_ref[:, c:c + ch] = (da * (gv * sg)).astype(BF16)
        @pl.when(pl.program_id(0) == 0)
        def _():
            loads[1].wait()
            loads[2].wait()

        dx_ref[...] = ALPHA * ds + _dot(dg_ref[...], wg_v[...]) + _dot(du_ref[...], wu_v[...])

    row = pl.BlockSpec((tm, D), lambda i: (i, 0))
    wide = pl.BlockSpec((tm, DFF), lambda i: (i, 0))
    vec = pl.BlockSpec((1, D), lambda i: (0, 0))
    wsl = pltpu.VMEM((DFF, D), BF16)
    last = target is not None
    return pl.pallas_call(
        body, name="ffn_bwd_loss" if last else "ffn_bwd", grid=(T // tm,),
        in_specs=[row] + ([row] if last else [])
        + [row, pl.BlockSpec((tm, 1), lambda i: (i, 0)), vec, wide, wide, ANY_SPEC, ANY_SPEC, ANY_SPEC],
        out_specs=[wide, wide, row, row, vec, vec] + ([pl.BlockSpec((1, 1), lambda i: (0, 0))] if last else []),
        out_shape=[jax.ShapeDtypeStruct((T, DFF), BF16), jax.ShapeDtypeStruct((T, DFF), BF16),
                   jax.ShapeDtypeStruct((T, D), BF16), jax.ShapeDtypeStruct((T, D), F32),
                   jax.ShapeDtypeStruct((1, D), F32), jax.ShapeDtypeStruct((1, D), F32)]
        + ([jax.ShapeDtypeStruct((1, 1), F32)] if last else []),
        scratch_shapes=[wsl, wsl, wsl, pltpu.SemaphoreType.DMA((3,))],
        compiler_params=_cp(("arbitrary",)))(dy, *([target] if last else []), xh, r, gam, g, u, wd, wgt, wut)


def _mm_ln(a, w, x, gam, bet, name):
    tm = 256
    K = a.shape[1]

    def body(a_ref, w_ref, x_ref, g_ref, b_ref, y_ref, xh_ref, r_ref, yb_ref):
        s = ALPHA * x_ref[...] + _dot(a_ref[...], w_ref[...])
        mu = jnp.mean(s, axis=-1, keepdims=True)
        xc = s - mu
        var = jnp.mean(xc * xc, axis=-1, keepdims=True)
        r = lax.rsqrt(var + LN_EPS)
        xh = xc * r
        xh_ref[...] = xh.astype(BF16)
        r_ref[...] = r
        y = xh * g_ref[...] + b_ref[...]
        y_ref[...] = y
        yb_ref[...] = y.astype(BF16)

    row = pl.BlockSpec((tm, D), lambda i: (i, 0))
    vec = pl.BlockSpec((1, D), lambda i: (0, 0))
    return pl.pallas_call(
        body, name=name, grid=(T // tm,),
        in_specs=[pl.BlockSpec((tm, K), lambda i: (i, 0)), pl.BlockSpec((K, D), lambda i: (0, 0)), row, vec, vec],
        out_specs=[row, row, pl.BlockSpec((tm, 1), lambda i: (i, 0)), row],
        out_shape=[jax.ShapeDtypeStruct((T, D), F32), jax.ShapeDtypeStruct((T, D), BF16),
                   jax.ShapeDtypeStruct((T, 1), F32), jax.ShapeDtypeStruct((T, D), BF16)],
        compiler_params=_cp(("parallel",)))(a, w, x, gam, bet)


def _ln_bwd(dy, xh, r, gam, w):
    tm = 256

    def body(dy_ref, xh_ref, r_ref, g_ref, w_ref, ds_ref, dg_ref, db_ref, dsb_ref, dm_ref):
        i = pl.program_id(0)
        dyv = dy_ref[...]
        xhv = xh_ref[...].astype(F32)
        dxh = dyv * g_ref[...]
        m1 = jnp.mean(dxh, axis=-1, keepdims=True)
        m2 = jnp.mean(dxh * xhv, axis=-1, keepdims=True)
        ds = r_ref[...] * (dxh - m1 - xhv * m2)
        ds_ref[...] = ds
        dsb = ds.astype(BF16)
        dsb_ref[...] = dsb
        dm_ref[...] = _dot_nt(dsb, w_ref[...]).astype(BF16)
        pg = jnp.sum(dyv * xhv, axis=0, keepdims=True)
        pb = jnp.sum(dyv, axis=0, keepdims=True)

        @pl.when(i == 0)
        def _():
            dg_ref[...] = pg
            db_ref[...] = pb

        @pl.when(i > 0)
        def _():
            dg_ref[...] += pg
            db_ref[...] += pb

    row = pl.BlockSpec((tm, D), lambda i: (i, 0))
    vec = pl.BlockSpec((1, D), lambda i: (0, 0))
    return pl.pallas_call(
        body, name="ln_bwd_proj", grid=(T // tm,),
        in_specs=[row, row, pl.BlockSpec((tm, 1), lambda i: (i, 0)), vec, pl.BlockSpec((D, D), lambda i: (0, 0))],
        out_specs=[row, vec, vec, row, row],
        out_shape=[jax.ShapeDtypeStruct((T, D), F32), jax.ShapeDtypeStruct((1, D), F32),
                   jax.ShapeDtypeStruct((1, D), F32), jax.ShapeDtypeStruct((T, D), BF16),
                   jax.ShapeDtypeStruct((T, D), BF16)],
        compiler_params=_cp(("arbitrary",)))(dy, xh, r, gam, w)


def _adamw(w, g, m, v, tr):
    L, R, C = w.shape

    def body(w_ref, g_ref, m_ref, v_ref, d_ref, m2_ref, v2_ref):
        gv = g_ref[...]
        m2 = ADAM_B1 * m_ref[...] + (1.0 - ADAM_B1) * gv
        v2 = ADAM_B2 * v_ref[...] + (1.0 - ADAM_B2) * (gv * gv)
        m_hat = m2 / (1.0 - ADAM_B1 ** ADAM_STEP)
        v_hat = v2 / (1.0 - ADAM_B2 ** ADAM_STEP)
        d_ref[...] = -ADAM_LR * (m_hat / (jnp.sqrt(v_hat) + ADAM_EPS) + ADAM_WD * w_ref[...])
        m2_ref[...] = m2
        v2_ref[...] = v2

    blk = pl.BlockSpec((None, tr, C), lambda l, i: (l, i, 0))
    sh = jax.ShapeDtypeStruct((L, R, C), F32)
    return pl.pallas_call(
        body, name="adamw", grid=(L, R // tr), in_specs=[blk] * 4, out_specs=[blk] * 3,
        out_shape=[sh, sh, sh], compiler_params=_cp(("parallel", "parallel")))(w, g, m, v)


class _Job:
    def __init__(self, ins, out_shapes, aliases, sems, start, finish):
        self.ins, self.out_shapes, self.aliases, self.sems = list(ins), list(out_shapes), dict(aliases), list(sems)
        self.start, self.finish = start, finish


def _host_call(body, name, ins, in_specs, out_shapes, out_specs, scratch, aliases, job):
    n_in, n_out, n_scr = len(ins), len(out_shapes), len(scratch)
    jins = job.ins if job else []
    jouts = job.out_shapes if job else []
    jsems = job.sems if job else []

    def wrapped(*refs):
        a = n_in
        b = a + len(jins)
        c = b + n_out
        d = c + len(jouts)
        e = d + n_scr
        comm = None
        if job:
            jrefs = (refs[a:b], refs[c:d], refs[e:])
            comm = (lambda: job.start(*jrefs), lambda st: job.finish(st, *jrefs))
        body(refs[:a], refs[b:c], refs[d:e], comm)

    al = dict(aliases)
    if job:
        for ji, jo in job.aliases.items():
            al[n_in + ji] = n_out + jo
    res = pl.pallas_call(
        wrapped, name=name, in_specs=list(in_specs) + [ANY_SPEC] * len(jins),
        out_specs=list(out_specs) + [ANY_SPEC] * len(jouts), out_shape=list(out_shapes) + list(jouts),
        scratch_shapes=list(scratch) + list(jsems), input_output_aliases=al,
        compiler_params=_cp())(*ins, *jins)
    return res[:n_out], res[n_out:]


def _copy_in(src, dst, sem):
    cp = pltpu.make_async_copy(src, dst, sem)
    cp.start()
    cp.wait()


CHAINS = [(p, b) for p in range(2) for b in range(BL)]
NC = len(CHAINS)
ROWS_SHAPE = jax.ShapeDtypeStruct((NSTAT, S), F32)
SLAB_QKV = pltpu.VMEM((T, 2 * PAIRW), BF16)
SLAB_OUT = pltpu.VMEM((T, 2 * BQ), BF16)
SLAB_O32 = pltpu.VMEM((T, 2 * BQ), F32)
SLAB_T = pltpu.VMEM((2, BQ, T), BF16)
SLAB_KEYB = pltpu.VMEM((NSTAT, S, BQ), F32)
ACC_KV = pltpu.VMEM((2, T, BQ), F32)
NTRI = NB * (NB + 1) // 2
A_TILES = jax.ShapeDtypeStruct((NTRI, NC, HB, BQ), BF16)
Q_DIAG = pltpu.VMEM((NC // 2, HB, HB), BF16)
PAIR_DIAG = pltpu.VMEM((NC, 2, HB, HB), BF16)
A_SLOTS_OUT, A_SLOTS_IN = 2, 4


def _lane_masks():
    lane = _iota((1, BQ), 1)
    m0 = (lane < 64).astype(BF16)
    return m0, 1.0 - m0


def _merge_heads(x, first):
    return jnp.where(first, x[:BQ], x[BQ:])


def _row_masks():
    r = _iota((BQ, 1), 0)
    m0 = (r < 64).astype(BF16)
    return m0, 1.0 - m0


def _stack(x, m0, m1):
    return jnp.concatenate([x * m0, x * m1], axis=0)


def _stack_t(xt, r0, r1):
    return jnp.concatenate([xt * r0, xt * r1], axis=1)


def _tr(x):
    return x.T


def _rows(b, i):
    return pl.ds(pl.multiple_of(b * S + i * BQ, BQ), BQ)


def _transpose_slab(src, dst, col0):
    def blk(n, _):
        r = pl.ds(pl.multiple_of(n * BQ, BQ), BQ)
        for p in range(2):
            dst[p, :, r] = _tr(src[r, col0(p):col0(p) + BQ])
        return 0

    lax.fori_loop(0, T // BQ, blk, 0)


def _heads(x):
    return x[:BQ], x[BQ:]


def _bcast_heads(r0, r1):
    return jnp.concatenate([jnp.broadcast_to(r0, (BQ, BQ)), jnp.broadcast_to(r1, (BQ, BQ))], axis=0)


def _by_channel(r0, r1):
    return jnp.where(_iota((BQ, BQ), 0) < 64, r0, r1)


def _colsum2(x):
    return jnp.sum(x[:BQ], axis=0, keepdims=True), jnp.sum(x[BQ:], axis=0, keepdims=True)


def _stat_row(ref, p, b, h, i):
    c = b * NH + 2 * p + h
    return ref[c:c + 1, pl.ds(pl.multiple_of(i * BQ, BQ), BQ)]


def _put_row(ref, p, b, h, i, v):
    c = b * NH + 2 * p + h
    ref[c:c + 1, pl.ds(pl.multiple_of(i * BQ, BQ), BQ)] = v


def _valid_t(strict):
    r = _iota((HB, BQ), 0) & (BQ - 1)
    c = _iota((HB, BQ), 1)
    return (r < c) if strict else (r <= c)


def _tri_blockdiag(later):
    r = _iota((HB, HB), 0)
    c = _iota((HB, HB), 1)
    same = (r >= BQ) == (c >= BQ)
    return (same & ((c > r) if later else (c < r))).astype(BF16)


def _cum_mm(tri, x):
    y = _dot(tri, _split2(x))
    return y[:, :BQ] + y[:, BQ:]


def _kv_tiles(qkv_v, p, b, j):
    r = _rows(b, j)
    return qkv_v[r, p * PAIRW + BQ:p * PAIRW + 2 * BQ], qkv_v[r, p * PAIRW + 2 * BQ:p * PAIRW + 3 * BQ]


def _q_tile(qkv_v, p, b, i):
    return qkv_v[_rows(b, i), p * PAIRW:p * PAIRW + BQ] * SCALE


def _put_q_pairs(qkv_v, wq_v, i):
    for c, (p, b) in enumerate(CHAINS):
        h = (c % 2) * BQ
        wq_v[c // 2, h:h + BQ, h:h + BQ] = _tr(_q_tile(qkv_v, p, b, i))


def _lead_pairs(qkv_v, wq_v, j, m0, m1):
    zs = []
    for c in range(0, NC, 2):
        ks = [_stack(_kv_tiles(qkv_v, *CHAINS[c + n], j)[0], m0, m1) for n in range(2)]
        y = _dot(jnp.concatenate(ks, axis=1), wq_v[c // 2])
        zs += [y[:, :BQ], y[:, BQ:]]
    return zs


def _sb_fwd(qkv, job=None):
    def body(ins, outs, scr, comm):
        (qkv_hbm,), (o_hbm, a_hbm), (qkv_v, o_v, sem, vt_v, a_st, a_sems, wq_v) = ins, outs, scr
        wq_v[...] = jnp.zeros_like(wq_v)
        _copy_in(qkv_hbm.at[:, pl.ds(0, 2 * PAIRW)], qkv_v, sem)
        st = comm[0]() if comm else None
        _transpose_slab(qkv_v, vt_v, lambda p: p * PAIRW + 2 * BQ)
        m0, m1 = _lane_masks()
        r0, r1 = _row_masks()
        valid = _valid_t(True)
        later = _tri_blockdiag(True)

        def a_copy(n, t):
            slot = n % A_SLOTS_OUT
            return pltpu.make_async_copy(a_st.at[slot], a_hbm.at[t], a_sems.at[slot])

        def steps(i, jj, cs, diag):
            j = i - jj
            zs = _lead_pairs(qkv_v, wq_v, j, m0, m1)
            lbs, lrs = [], []
            for c in range(NC):
                lb = _log_sigmoid_tile(zs[c])
                lr = lb - zs[c]
                if diag:
                    lr = jnp.where(valid, lr, 0.0)
                lbs.append(lb)
                lrs.append(lr)
            tails = [_cum_mm(later, lrs[c]) for c in range(NC)]
            avs = []
            for c in range(NC):
                a = jnp.exp(lbs[c] + tails[c] + _bcast_heads(*cs[c][0]))
                if diag:
                    a = jnp.where(valid, a, 0.0)
                avs.append(a.astype(BF16))
            n = (i * (i + 1)) // 2 + jj

            @pl.when(n >= A_SLOTS_OUT)
            def _():
                a_copy(n, 0).wait()
            for c in range(NC):
                a_st[n % A_SLOTS_OUT, c] = avs[c]
            a_copy(n, n - jj + j).start()
            out = []
            for c, (p, b) in enumerate(CHAINS):
                vts = _stack_t(vt_v[p, :, _rows(b, j)], r0, r1)
                s0, s1 = _colsum2(lrs[c])
                out.append(((cs[c][0][0] + s0, cs[c][0][1] + s1), cs[c][1] + _dot(vts, avs[c])))
            return tuple(out)

        def qblock(i, _):
            _put_q_pairs(qkv_v, wq_v, i)
            zr = jnp.zeros((1, BQ), F32)
            cs = steps(i, 0, (((zr, zr), jnp.zeros((BQ, BQ), F32)),) * NC, True)
            cs = lax.fori_loop(1, i + 1, lambda jj, cs: steps(i, jj, cs, False), cs)
            for c, (p, b) in enumerate(CHAINS):
                o_v[_rows(b, i), p * BQ:(p + 1) * BQ] = cs[c][1].T.astype(BF16)
            return 0

        lax.fori_loop(0, NB, qblock, 0)
        for n in range(NTRI - A_SLOTS_OUT, NTRI):
            a_copy(n, 0).wait()
        _copy_in(o_v, o_hbm.at[:, pl.ds(0, 2 * BQ)], sem)
        if comm:
            comm[1](st)

    (mixed, amat), extra = _host_call(
        body, "sb_fwd", [qkv], [ANY_SPEC], [jax.ShapeDtypeStruct((T, D), BF16), A_TILES], [ANY_SPEC, ANY_SPEC],
        [SLAB_QKV, SLAB_OUT, pltpu.SemaphoreType.DMA, SLAB_T, pltpu.VMEM((A_SLOTS_OUT, NC, HB, BQ), BF16),
         pltpu.SemaphoreType.DMA((A_SLOTS_OUT,)), Q_DIAG], {}, job)
    return mixed, amat, extra


def _sb_bwd(qkv, dmixed, amat, job=None):
    def body(ins, outs, scr, comm):
        (qkv_hbm, do_hbm, a_hbm), (dqkv_hbm,), (qkv_v, do_v, dq_v, dk_s, dv_s, sems, kt_v, a_st, a_sems, w_v) = ins, outs, scr
        sem = sems.at[0]
        w_v[...] = jnp.zeros_like(w_v)

        def a_copy(t):
            slot = t % A_SLOTS_IN
            return pltpu.make_async_copy(a_hbm.at[t], a_st.at[slot], a_sems.at[slot])

        later = [pltpu.make_async_copy(do_hbm.at[:, pl.ds(0, 2 * BQ)], do_v, sems.at[1])]
        for cp in later:
            cp.start()
        for t in range(A_SLOTS_IN - 1):
            a_copy(t).start()
        _copy_in(qkv_hbm.at[:, pl.ds(0, 2 * PAIRW)], qkv_v, sem)
        st = comm[0]() if comm else None
        _transpose_slab(qkv_v, kt_v, lambda p: p * PAIRW + BQ)
        for cp in later:
            cp.wait()
        m0, m1 = _lane_masks()
        first = _iota((BQ, BQ), 1) < 64
        r0, r1 = _row_masks()
        valid = _valid_t(True)
        earlier = _tri_blockdiag(False)
        dk_s[...] = jnp.zeros_like(dk_s)
        dv_s[...] = jnp.zeros_like(dv_s)

        def steps(i, j, cs, diag):
            t = (i * (i + 1)) // 2 + j
            a_copy(t).wait()

            @pl.when(t + A_SLOTS_IN - 1 < NTRI)
            def _():
                a_copy(t + A_SLOTS_IN - 1).start()
            slot = t % A_SLOTS_IN
            kv = [_kv_tiles(qkv_v, p, b, j) for p, b in CHAINS]
            zd = [_dot(jnp.concatenate([_stack(kv[c][0], m0, m1), _stack(kv[c][1], m0, m1)], axis=1), w_v[c, 0])
                  for c in range(NC)]
            zs = [x[:, :BQ] for x in zd]
            das = [x[:, BQ:] for x in zd]
            avs = [a_st[slot, c] for c in range(NC)]
            gms = [das[c] * avs[c].astype(F32) for c in range(NC)]
            befores = [_dot(earlier, gms[c].astype(BF16)) for c in range(NC)]
            dzbs = []
            for c in range(NC):
                dz = gms[c] - jax.nn.sigmoid(zs[c]) * (gms[c] + befores[c] + _bcast_heads(*cs[c][0]))
                if diag:
                    dz = jnp.where(valid, dz, 0.0)
                dzbs.append(dz.astype(BF16))
            out = []
            for c, (p, b) in enumerate(CHAINS):
                dq = cs[c][1] + _dot(_stack_t(kt_v[p, :, _rows(b, j)], r0, r1), dzbs[c])
                kd = _dot(jnp.concatenate([dzbs[c], avs[c]], axis=1), w_v[c, 1])
                dk_s[p, _rows(b, j), :] += _merge_heads(kd[:, :BQ], first)
                dv_s[p, _rows(b, j), :] += _merge_heads(kd[:, BQ:], first)
                g0, g1 = _colsum2(gms[c])
                out.append(((cs[c][0][0] + g0, cs[c][0][1] + g1), dq))
            return tuple(out)

        def qblock(i, _):
            for c, (p, b) in enumerate(CHAINS):
                qn = _q_tile(qkv_v, p, b, i)
                dn = do_v[_rows(b, i), p * BQ:(p + 1) * BQ]
                for r, (x, y) in enumerate(((_tr(qn), _tr(dn)), (qn, dn))):
                    w_v[c, r, :BQ, :BQ] = x
                    w_v[c, r, BQ:, BQ:] = y
            zr = jnp.zeros((1, BQ), F32)
            cs = (((zr, zr), jnp.zeros((BQ, BQ), F32)),) * NC
            cs = lax.fori_loop(0, i, lambda j, cs: steps(i, j, cs, False), cs)
            cs = steps(i, i, cs, True)
            for c, (p, b) in enumerate(CHAINS):
                dq_v[_rows(b, i), p * PAIRW:p * PAIRW + BQ] = (cs[c][1].T * SCALE).astype(BF16)
            return 0

        lax.fori_loop(0, NB, qblock, 0)
        for p in range(2):
            dq_v[:, p * PAIRW + BQ:p * PAIRW + 2 * BQ] = dk_s[p].astype(BF16)
            dq_v[:, p * PAIRW + 2 * BQ:p * PAIRW + 3 * BQ] = dv_s[p].astype(BF16)
        _copy_in(dq_v, dqkv_hbm.at[:, pl.ds(0, 2 * PAIRW)], sem)
        if comm:
            comm[1](st)

    (dqkv,), extra = _host_call(
        body, "sb_bwd", [qkv, dmixed, amat], [ANY_SPEC, ANY_SPEC, ANY_SPEC],
        [jax.ShapeDtypeStruct((T, QKVW), BF16)], [ANY_SPEC],
        [SLAB_QKV, SLAB_OUT, SLAB_QKV, ACC_KV, ACC_KV, pltpu.SemaphoreType.DMA((4,)), SLAB_T,
         pltpu.VMEM((A_SLOTS_IN, NC, HB, BQ), BF16), pltpu.SemaphoreType.DMA((A_SLOTS_IN,)), PAIR_DIAG], {}, job)
    return dqkv, extra


def _flash_fwd(qkv, mixed, g, fox, bias, job=None):
    def body(ins, outs, scr, comm):
        if fox:
            qkv_hbm, cq_ref, ckb_hbm, _ = ins
            (o_hbm, lse_ref, o32_hbm), (qkv_v, o_v, sem, vt_v, wq_v, o32_v, ckb_v) = outs, scr
        else:
            qkv_hbm, tbl_ref, _ = ins
            (o_hbm, lse_ref), (qkv_v, o_v, sem, vt_v, wq_v) = outs, scr
        wq_v[...] = jnp.zeros_like(wq_v)
        sems = sem
        sem = sems.at[0]
        later = [pltpu.make_async_copy(ckb_hbm, ckb_v, sems.at[1])] if fox else []
        for cp in later:
            cp.start()
        _copy_in(qkv_hbm.at[:, pl.ds(g * 2 * PAIRW, 2 * PAIRW)], qkv_v, sem)
        st = comm[0]() if comm else None
        _transpose_slab(qkv_v, vt_v, lambda p: p * PAIRW + 2 * BQ)
        for cp in later:
            cp.wait()
        m0, m1 = _lane_masks()
        r0, r1 = _row_masks()
        valid = _valid_t(False)

        def steps(cqs, i, j, cs, diag):
            zs = _lead_pairs(qkv_v, wq_v, j, m0, m1)
            prs, alphas, out = [], [], []
            for c, (p, b) in enumerate(CHAINS):
                (ma, mb), (la, lb_), _ = cs[c]
                if fox:
                    kk = pl.ds(pl.multiple_of(j * BQ, BQ), BQ)
                    col = b * NH + 2 * p
                    z = zs[c] + (cqs[c] - jnp.concatenate([ckb_v[col, kk, :], ckb_v[col + 1, kk, :]], axis=0))
                    if diag:
                        z = jnp.where(valid, z, NEG)
                else:
                    z = zs[c] + tbl_ref[p, i - j]
                za, zb = _heads(z)
                na = jnp.maximum(ma, jnp.max(za, axis=0, keepdims=True))
                nb = jnp.maximum(mb, jnp.max(zb, axis=0, keepdims=True))
                aa, ab = jnp.exp(ma - na), jnp.exp(mb - nb)
                pr = jnp.exp(z - _bcast_heads(na, nb))
                sa, sb = _colsum2(pr)
                prs.append(_split2(pr) if fox else pr.astype(BF16))
                alphas.append((aa, ab))
                out.append(((na, nb), (aa * la + sa, ab * lb_ + sb)))
            pvs = []
            for c, (p, b) in enumerate(CHAINS):
                vts = _stack_t(vt_v[p, :, _rows(b, j)], r0, r1)
                if fox:
                    y = _dot(vts, prs[c])
                    pvs.append(y[:, :BQ] + y[:, BQ:])
                else:
                    pvs.append(_dot(vts, prs[c]))
            return tuple((out[c][0], out[c][1], _by_channel(*alphas[c]) * cs[c][2] + pvs[c]) for c in range(NC))

        def qblock(i, _):
            _put_q_pairs(qkv_v, wq_v, i)
            if fox:
                cqs = [_bcast_heads(_stat_row(cq_ref, p, b, 0, i), _stat_row(cq_ref, p, b, 1, i)) for p, b in CHAINS]
            else:
                cqs = [None] * NC
            ng = jnp.full((1, BQ), NEG, F32)
            zr = jnp.zeros((1, BQ), F32)
            cs = steps(cqs, i, i, (((ng, ng), (zr, zr), jnp.zeros((BQ, BQ), F32)),) * NC, True)
            cs = lax.fori_loop(1, i + 1, lambda jj, cs: steps(cqs, i, i - jj, cs, False), cs)
            for c, (p, b) in enumerate(CHAINS):
                (ma, mb), (la, lb_), acc = cs[c]
                o = (acc / _by_channel(la, lb_)).T
                o_v[_rows(b, i), p * BQ:(p + 1) * BQ] = o.astype(BF16)
                if fox:
                    o32_v[_rows(b, i), p * BQ:(p + 1) * BQ] = o
                _put_row(lse_ref, p, b, 0, i, ma + jnp.log(la))
                _put_row(lse_ref, p, b, 1, i, mb + jnp.log(lb_))
            return 0

        lax.fori_loop(0, NB, qblock, 0)
        _copy_in(o_v, o_hbm.at[:, pl.ds(g * 2 * BQ, 2 * BQ)], sem)
        if fox:
            _copy_in(o32_v, o32_hbm, sem)
        if comm:
            comm[1](st)

    bias_specs = [VMEM_SPEC, ANY_SPEC] if fox else [VMEM_SPEC]
    n_in = 2 + len(bias_specs)
    o32 = [jax.ShapeDtypeStruct((T, 2 * BQ), F32)] if fox else []
    res, extra = _host_call(
        body, "fox_fwd" if fox else "dil_fwd", [qkv, *bias, mixed], [ANY_SPEC] + bias_specs + [ANY_SPEC],
        [jax.ShapeDtypeStruct((T, D), BF16), ROWS_SHAPE] + o32, [ANY_SPEC, VMEM_SPEC] + [ANY_SPEC] * len(o32),
        [SLAB_QKV, SLAB_OUT, pltpu.SemaphoreType.DMA((4,)), SLAB_T, Q_DIAG] + ([SLAB_O32, SLAB_KEYB] if fox else []),
        {n_in - 1: 0}, job)
    return (*res, extra)


def _flash_bwd(qkv, o, dmixed, lse, dqkv, g, fox, bias, job=None):
    def body(ins, outs, scr, comm):
        if fox:
            qkv_hbm, o_hbm, do_hbm, lse_ref, cq_ref, ckb_hbm, _ = ins
            (dqkv_hbm, db_ref), (qkv_v, o_v, do_v, dq_v, dk_s, dv_s, sem, kt_v, w_v, ckb_v, dc_s) = outs, scr
        else:
            qkv_hbm, o_hbm, do_hbm, lse_ref, tbl_ref, _ = ins
            (dqkv_hbm, db_ref), (qkv_v, o_v, do_v, dq_v, dk_s, dv_s, sem, kt_v, w_v) = outs, scr
        w_v[...] = jnp.zeros_like(w_v)
        sems = sem
        sem = sems.at[0]
        later = [pltpu.make_async_copy(do_hbm.at[:, pl.ds(g * 2 * BQ, 2 * BQ)], do_v, sems.at[1])]
        if fox:
            later += [pltpu.make_async_copy(o_hbm, o_v, sems.at[2]), pltpu.make_async_copy(ckb_hbm, ckb_v, sems.at[3])]
        else:
            later += [pltpu.make_async_copy(o_hbm.at[:, pl.ds(g * 2 * BQ, 2 * BQ)], o_v, sems.at[2])]
        for cp in later:
            cp.start()
        _copy_in(qkv_hbm.at[:, pl.ds(g * 2 * PAIRW, 2 * PAIRW)], qkv_v, sem)
        st = comm[0]() if comm else None
        _transpose_slab(qkv_v, kt_v, lambda p: p * PAIRW + BQ)
        for cp in later:
            cp.wait()
        m0, m1 = _lane_masks()
        first = _iota((BQ, BQ), 1) < 64
        r0, r1 = _row_masks()
        valid = _valid_t(False)
        dk_s[...] = jnp.zeros_like(dk_s)
        dv_s[...] = jnp.zeros_like(dv_s)
        if fox:
            dc_s[...] = jnp.zeros_like(dc_s)
        else:
            db_ref[...] = jnp.zeros_like(db_ref)

        def steps(cqs, lses, deltas, i, j, dqs, diag):
            kv = [_kv_tiles(qkv_v, p, b, j) for p, b in CHAINS]
            zd = [_dot(jnp.concatenate([_stack(kv[c][0], m0, m1), _stack(kv[c][1], m0, m1)], axis=1), w_v[c, 0])
                  for c in range(NC)]
            zs = [x[:, :BQ] for x in zd]
            dps = [x[:, BQ:] for x in zd]
            prs, dzl = [], []
            for c, (p, b) in enumerate(CHAINS):
                if fox:
                    kk = pl.ds(pl.multiple_of(j * BQ, BQ), BQ)
                    col = b * NH + 2 * p
                    z = zs[c] + (cqs[c] - jnp.concatenate([ckb_v[col, kk, :], ckb_v[col + 1, kk, :]], axis=0))
                    if diag:
                        z = jnp.where(valid, z, NEG)
                else:
                    z = zs[c] + tbl_ref[p, i - j]
                pr = jnp.exp(z - lses[c])
                prs.append(pr.astype(BF16))
                dzl.append(pr * (dps[c] - deltas[c]))
            dzbs = [dz.astype(BF16) for dz in dzl]
            new = []
            for c, (p, b) in enumerate(CHAINS):
                new.append(dqs[c] + _dot(_stack_t(kt_v[p, :, _rows(b, j)], r0, r1), dzbs[c]))
                kd = _dot(jnp.concatenate([dzbs[c], prs[c]], axis=1), w_v[c, 1])
                dk_s[p, _rows(b, j), :] += _merge_heads(kd[:, :BQ], first)
                dv_s[p, _rows(b, j), :] += _merge_heads(kd[:, BQ:], first)
                if fox:
                    dc_s[c, pl.ds(pl.multiple_of(j * HB, HB), HB), :] += dzl[c]
            if not fox:
                for p in range(2):
                    db_ref[p, i - j] = db_ref[p, i - j] + (dzl[2 * p] + dzl[2 * p + 1])
            return tuple(new)

        def qblock(i, _):
            qns = [_q_tile(qkv_v, p, b, i) for p, b in CHAINS]
            dns = [do_v[_rows(b, i), p * BQ:(p + 1) * BQ] for p, b in CHAINS]
            for c in range(NC):
                for r, (x, y) in enumerate(((_tr(qns[c]), _tr(dns[c])), (qns[c], dns[c]))):
                    w_v[c, r, :BQ, :BQ] = x
                    w_v[c, r, BQ:, BQ:] = y
            lses = [_bcast_heads(_stat_row(lse_ref, p, b, 0, i), _stat_row(lse_ref, p, b, 1, i)) for p, b in CHAINS]
            if fox:
                cqs = [_bcast_heads(_stat_row(cq_ref, p, b, 0, i), _stat_row(cq_ref, p, b, 1, i)) for p, b in CHAINS]
            else:
                cqs = [None] * NC
            deltas = []
            for c, (p, b) in enumerate(CHAINS):
                pt = (dns[c].astype(F32) * o_v[_rows(b, i), p * BQ:(p + 1) * BQ].astype(F32)).T
                deltas.append(_bcast_heads(jnp.sum(pt[:64], axis=0, keepdims=True), jnp.sum(pt[64:], axis=0, keepdims=True)))
            dqs = (jnp.zeros((BQ, BQ), F32),) * NC
            dqs = lax.fori_loop(0, i, lambda j, d: steps(cqs, lses, deltas, i, j, d, False), dqs)
            dqs = steps(cqs, lses, deltas, i, i, dqs, True)
            for c, (p, b) in enumerate(CHAINS):
                dq_v[_rows(b, i), p * PAIRW:p * PAIRW + BQ] = (dqs[c].T * SCALE).astype(BF16)
            return 0

        lax.fori_loop(0, NB, qblock, 0)
        for p in range(2):
            dq_v[:, p * PAIRW + BQ:p * PAIRW + 2 * BQ] = dk_s[p].astype(BF16)
            dq_v[:, p * PAIRW + 2 * BQ:p * PAIRW + 3 * BQ] = dv_s[p].astype(BF16)
        _copy_in(dq_v, dqkv_hbm.at[:, pl.ds(g * 2 * PAIRW, 2 * PAIRW)], sem)
        if fox:
            lane = _iota((BQ, NSTAT), 1)

            def fold(n, _):
                t = jnp.zeros((BQ, NSTAT), F32)
                for c, (p, b) in enumerate(CHAINS):
                    s = jnp.sum(dc_s[c, pl.ds(pl.multiple_of(n * HB, HB), HB), :], axis=1, keepdims=True)
                    col = b * NH + 2 * p
                    t = t - jnp.where(lane == col, s[:BQ], 0.0) - jnp.where(lane == col + 1, s[BQ:], 0.0)
                db_ref[pl.ds(pl.multiple_of(n * BQ, BQ), BQ), :] = t
                return 0

            lax.fori_loop(0, NB, fold, 0)
        if comm:
            comm[1](st)

    if fox:
        bias_specs = [VMEM_SPEC, ANY_SPEC]
        db_shape = jax.ShapeDtypeStruct((S, NSTAT), F32)
        more = [SLAB_KEYB, pltpu.VMEM((NC, NB * HB, BQ), F32)]
    else:
        bias_specs = [VMEM_SPEC]
        db_shape = jax.ShapeDtypeStruct((2, NB, HB, BQ), F32)
        more = []
    n_in = 5 + len(bias_specs)
    (dqkv, db), extra = _host_call(
        body, "fox_bwd" if fox else "dil_bwd", [qkv, o, dmixed, lse, *bias, dqkv],
        [ANY_SPEC, ANY_SPEC, ANY_SPEC, VMEM_SPEC] + bias_specs + [ANY_SPEC],
        [jax.ShapeDtypeStruct((T, QKVW), BF16), db_shape], [ANY_SPEC, VMEM_SPEC],
        [SLAB_QKV, SLAB_O32 if fox else SLAB_OUT, SLAB_OUT, SLAB_QKV, ACC_KV, ACC_KV, pltpu.SemaphoreType.DMA((4,)), SLAB_T,
         PAIR_DIAG] + more, {n_in - 1: 0}, job)
    return dqkv, db, extra


def _delta_t(d):
    return d * BQ + _iota((HB, BQ), 1) - (_iota((HB, BQ), 0) & (BQ - 1))


def _buckets_in(d):
    lo, hi = max(d * BQ - (BQ - 1), 0), d * BQ + BQ - 1
    return [b for b in range(32) if BUCKET_TH[b] <= hi and (b == 31 or BUCKET_TH[b + 1] > lo)]


def _in_bucket(delta, b):
    m = delta >= BUCKET_TH[b]
    return m if b == 31 else m & (delta < BUCKET_TH[b + 1])


def _dil_table(rel_bias, job=None):
    def body(ins, outs, scr, comm):
        (rb_ref,), (o_ref,) = ins, outs
        st = comm[0]() if comm else None
        for d in range(NB):
            delta = _delta_t(d)
            pos = delta >= 0
            n = ((pos & (delta <= 128)).astype(jnp.int32)
                 + (pos & (delta <= 512) & ((delta & 3) == 0)).astype(jnp.int32)
                 + (pos & ((delta & 15) == 0)).astype(jnp.int32))
            logn = jnp.where(n == 3, math.log(3.0), jnp.where(n == 2, math.log(2.0), jnp.where(n == 1, 0.0, NEG)))
            head1 = _iota((HB, BQ), 0) >= BQ
            for p in range(2):
                val = jnp.zeros((HB, BQ), F32)
                for b in _buckets_in(d):
                    val = jnp.where(_in_bucket(delta, b), jnp.where(head1, rb_ref[b, 2 * p + 1], rb_ref[b, 2 * p]), val)
                o_ref[p, d] = val + logn
        if comm:
            comm[1](st)

    (tbl,), extra = _host_call(
        body, "dil_table", [rel_bias], [pl.BlockSpec(memory_space=pltpu.SMEM)],
        [jax.ShapeDtypeStruct((2, NB, HB, BQ), F32)], [VMEM_SPEC], [], {}, job)
    return (tbl, extra) if job else tbl


def _dil_table_bwd(dtbl):
    def body(dt_ref, o_ref):
        p = pl.program_id(0)
        rowi = _iota((32, BQ), 0)
        lanei = _iota((32, BQ), 1)

        @pl.when(p == 0)
        def _():
            o_ref[...] = jnp.zeros_like(o_ref)

        out = jnp.zeros((32, BQ), F32)
        for b in range(32):
            acc = None
            for d in range(NB):
                if b in _buckets_in(d):
                    t = jnp.where(_in_bucket(_delta_t(d), b), dt_ref[d], 0.0)
                    acc = t if acc is None else acc + t
            rs = jnp.sum(acc, axis=1, keepdims=True)
            s0 = jnp.sum(rs[:BQ], axis=0, keepdims=True)
            s1 = jnp.sum(rs[BQ:], axis=0, keepdims=True)
            out = (out + jnp.where((rowi == b) & (lanei == 2 * p), s0, 0.0)
                   + jnp.where((rowi == b) & (lanei == 2 * p + 1), s1, 0.0))
        o_ref[...] += out

    return pl.pallas_call(
        body, name="dil_table_bwd", grid=(2,),
        in_specs=[pl.BlockSpec((None, NB, HB, BQ), lambda p: (p, 0, 0, 0))],
        out_specs=pl.BlockSpec((32, BQ), lambda p: (0, 0)),
        out_shape=jax.ShapeDtypeStruct((32, BQ), F32),
        compiler_params=_cp(("arbitrary",)))(dtbl)


def _fox_prep(gate, fb):
    def body(g_ref, fb_ref, c_ref):
        tri = (_iota((BQ, BQ), 0) >= _iota((BQ, BQ), 1)).astype(BF16)

        def blk(i, carry):
            r0 = pl.multiple_of(i * BQ, BQ)
            lf = _log_sigmoid(g_ref[pl.ds(r0, BQ), :] + fb_ref[...])
            c = _dot(tri, _split3(lf))
            c_ref[pl.ds(r0, BQ), :] = c[:, 0:BQ] + c[:, BQ:2 * BQ] + c[:, 2 * BQ:3 * BQ] + carry
            return carry + jnp.sum(lf, axis=0, keepdims=True)

        lax.fori_loop(0, NB, blk, jnp.zeros((1, BQ), F32))

    blk = pl.BlockSpec((S, GATEW), lambda b: (b, 0))
    return pl.pallas_call(
        body, name="fox_prep", grid=(BL,), in_specs=[blk, pl.BlockSpec((1, GATEW), lambda b: (0, 0))],
        out_specs=blk, out_shape=jax.ShapeDtypeStruct((T, GATEW), F32),
        compiler_params=_cp(("parallel",)))(gate, fb)


def _fox_post(dcum, gate, fb):
    def body(dc_ref, g_ref, fb_ref, dg_ref, dfb_ref):
        b = pl.program_id(0)
        tri = (_iota((BQ, BQ), 0) <= _iota((BQ, BQ), 1)).astype(BF16)

        def blk(ii, carry):
            csum, dfb = carry
            r0 = pl.multiple_of((NB - 1 - ii) * BQ, BQ)
            dc = dc_ref[pl.ds(r0, BQ), :]
            c = _dot(tri, _split3(dc))
            dlf = c[:, 0:BQ] + c[:, BQ:2 * BQ] + c[:, 2 * BQ:3 * BQ] + csum
            dg = dlf * jnp.exp(_log_sigmoid(-(g_ref[pl.ds(r0, BQ), :] + fb_ref[...])))
            dg_ref[pl.ds(r0, BQ), :] = dg
            return csum + jnp.sum(dc, axis=0, keepdims=True), dfb + jnp.sum(dg, axis=0, keepdims=True)

        z = jnp.zeros((1, BQ), F32)
        _, dfb = lax.fori_loop(0, NB, blk, (z, z))

        @pl.when(b == 0)
        def _():
            dfb_ref[...] = dfb

        @pl.when(b > 0)
        def _():
            dfb_ref[...] += dfb

    blk = pl.BlockSpec((S, GATEW), lambda b: (b, 0))
    vec = pl.BlockSpec((1, GATEW), lambda b: (0, 0))
    return pl.pallas_call(
        body, name="fox_post", grid=(BL,), in_specs=[blk, blk, vec], out_specs=[blk, vec],
        out_shape=[jax.ShapeDtypeStruct((T, GATEW), F32), jax.ShapeDtypeStruct((1, GATEW), F32)],
        compiler_params=_cp(("arbitrary",)))(dcum, gate, fb)


def _shift_down(x, n):
    return jnp.where(_iota(x.shape, 0) >= n, pltpu.roll(x, n, 0), 0.0)


def _shift_up(x, n):
    return jnp.where(_iota(x.shape, 0) < S - n, pltpu.roll(x, S - n, 0), 0.0)


def _conv_fwd(conv, cw, mixed):
    W = 256

    def body(c_ref, w_ref, _, o_ref):
        u = c_ref[:, W:2 * W] * c_ref[:, 2 * W:3 * W]
        y = w_ref[0:1, :] * _shift_down(u, 2) + w_ref[1:2, :] * _shift_down(u, 1) + w_ref[2:3, :] * u
        o_ref[...] = (c_ref[:, 0:W] * y).astype(BF16)

    return pl.pallas_call(
        body, name="conv_fwd", grid=(BL,),
        in_specs=[pl.BlockSpec((S, CONVW), lambda b: (b, 0)), pl.BlockSpec((8, W), lambda b: (0, 0)), ANY_SPEC],
        out_specs=pl.BlockSpec((S, W), lambda b: (b, 3)),
        out_shape=jax.ShapeDtypeStruct((T, D), BF16), input_output_aliases={2: 0},
        compiler_params=_cp(("parallel",)))(conv, cw, mixed)


def _conv_bwd(conv, cw, dmixed):
    W = 256

    def body(c_ref, w_ref, do_ref, dc_ref, dw_ref):
        b = pl.program_id(0)
        bg = c_ref[:, 0:W]
        cg = c_ref[:, W:2 * W]
        hv = c_ref[:, 2 * W:3 * W]
        do = do_ref[...].astype(F32)
        u = cg * hv
        u1 = _shift_down(u, 1)
        u2 = _shift_down(u, 2)
        y = w_ref[0:1, :] * u2 + w_ref[1:2, :] * u1 + w_ref[2:3, :] * u
        dy = do * bg
        du = w_ref[2:3, :] * dy + w_ref[1:2, :] * _shift_up(dy, 1) + w_ref[0:1, :] * _shift_up(dy, 2)
        dc_ref[:, 0:W] = (do * y).astype(BF16)
        dc_ref[:, W:2 * W] = (du * hv).astype(BF16)
        dc_ref[:, 2 * W:3 * W] = (du * cg).astype(BF16)
        rowi = _iota((8, W), 0)
        dw = (jnp.where(rowi == 0, jnp.sum(dy * u2, axis=0, keepdims=True), 0.0)
              + jnp.where(rowi == 1, jnp.sum(dy * u1, axis=0, keepdims=True), 0.0)
              + jnp.where(rowi == 2, jnp.sum(dy * u, axis=0, keepdims=True), 0.0))

        @pl.when(b == 0)
        def _():
            dw_ref[...] = dw

        @pl.when(b > 0)
        def _():
            dw_ref[...] += dw

    return pl.pallas_call(
        body, name="conv_bwd", grid=(BL,),
        in_specs=[pl.BlockSpec((S, CONVW), lambda b: (b, 0)), pl.BlockSpec((8, W), lambda b: (0, 0)),
                  pl.BlockSpec((S, W), lambda b: (b, 3))],
        out_specs=[pl.BlockSpec((S, CONVW), lambda b: (b, 0)), pl.BlockSpec((8, W), lambda b: (0, 0))],
        out_shape=[jax.ShapeDtypeStruct((T, CONVW), BF16), jax.ShapeDtypeStruct((8, W), F32)],
        compiler_params=_cp(("arbitrary",)))(conv, cw, dmixed)


def _place():
    x, y, c = lax.axis_index("x"), lax.axis_index("y"), lax.axis_index("c")
    return x, y, c


def _chips_of(x, y):
    return [(1 - x, y), (x, 1 - y), (1 - x, 1 - y)]


def _dev(p):
    return 4 * p[0] + 2 * p[1] + p[2]


def _gather_job_a(shards):
    n = len(shards)

    def peers(x, y, c):
        return [(x, y, 1 - c)] + [(*chip, c) for chip in _chips_of(x, y)]

    def start(ins, outs, sems):
        send, recv, loc = sems
        x, y, c = _place()
        me = (x, y, c)
        cps = []
        for a in range(n):
            cps.append(pltpu.make_async_copy(ins[a], outs[a].at[_dev(me)], loc.at[a]))
            for k, peer in enumerate(peers(x, y, c)):
                cps.append(pltpu.make_async_remote_copy(
                    src_ref=ins[a], dst_ref=outs[a].at[_dev(me)], send_sem=send.at[a, k], recv_sem=recv.at[a, k],
                    device_id=peer, device_id_type=MESH))
        for cp in cps:
            cp.start()
        return cps

    def finish(cps, ins, outs, sems):
        send, recv, loc = sems
        x, y, c = _place()
        for a in range(n):
            for k, peer in enumerate(peers(x, y, c)):
                pltpu.make_async_remote_copy(
                    src_ref=ins[a], dst_ref=outs[a].at[_dev(peer)], send_sem=send.at[a, k], recv_sem=recv.at[a, k],
                    device_id=(x, y, c), device_id_type=MESH).wait_recv()
        for a in range(n):
            cps[5 * a].wait()
            for k in range(4):
                cps[5 * a + 1 + k].wait_send()

    return _Job(shards, [jax.ShapeDtypeStruct((NDEV,) + s.shape, s.dtype) for s in shards], {},
                [pltpu.SemaphoreType.DMA((n, 4)), pltpu.SemaphoreType.DMA((n, 4)), pltpu.SemaphoreType.DMA((n,))],
                start, finish)


def _gather_job_b(gathered):
    n = len(gathered)

    def start(ins, outs, sems):
        send, recv = sems
        x, y, c = _place()
        cps = []
        for a in range(n):
            for j, chip in enumerate(_chips_of(x, y)):
                blk = outs[a].at[_dev((*chip, c))]
                cps.append(pltpu.make_async_remote_copy(
                    src_ref=blk, dst_ref=blk, send_sem=send.at[a, j], recv_sem=recv.at[a, j],
                    device_id=(x, y, 1 - c), device_id_type=MESH))
        for cp in cps:
            cp.start()
        return cps

    def finish(cps, ins, outs, sems):
        send, recv = sems
        x, y, c = _place()
        for a in range(n):
            for j, chip in enumerate(_chips_of(x, y)):
                blk = outs[a].at[_dev((*chip, 1 - c))]
                pltpu.make_async_remote_copy(
                    src_ref=blk, dst_ref=blk, send_sem=send.at[a, j], recv_sem=recv.at[a, j],
                    device_id=(x, y, c), device_id_type=MESH).wait_recv()
        for cp in cps:
            cp.wait_send()

    return _Job(gathered, [jax.ShapeDtypeStruct(g.shape, g.dtype) for g in gathered], {a: a for a in range(n)},
                [pltpu.SemaphoreType.DMA((n, 3)), pltpu.SemaphoreType.DMA((n, 3))], start, finish)


def _sibling_job(grads):
    n = len(grads)

    def start(ins, outs, sems):
        send, recv = sems
        x, y, c = _place()
        cps = [pltpu.make_async_remote_copy(
            src_ref=ins[a].at[:, 1 - c], dst_ref=outs[a], send_sem=send.at[a], recv_sem=recv.at[a],
            device_id=(x, y, 1 - c), device_id_type=MESH) for a in range(n)]
        for cp in cps:
            cp.start()
        return cps

    def finish(cps, ins, outs, sems):
        for cp in cps:
            cp.wait()

    return _Job(grads, [jax.ShapeDtypeStruct(g.shape[:1] + g.shape[2:], F32) for g in grads], {},
                [pltpu.SemaphoreType.DMA((n,)), pltpu.SemaphoreType.DMA((n,))], start, finish)


def _chip_job(psums):
    n = len(psums)

    def copies(ins, outs, sems):
        send, recv, loc = sems
        x, y, c = _place()
        mychip = 2 * x + y
        cps = []
        for a in range(n):
            cps.append(pltpu.make_async_copy(ins[a].at[mychip], outs[a].at[mychip], loc.at[a]))
            for j, chip in enumerate(_chips_of(x, y)):
                cps.append(pltpu.make_async_remote_copy(
                    src_ref=ins[a].at[2 * chip[0] + chip[1]], dst_ref=outs[a].at[mychip],
                    send_sem=send.at[a, j], recv_sem=recv.at[a, j], device_id=(*chip, c), device_id_type=MESH))
        return cps

    def start(ins, outs, sems):
        for cp in copies(ins, outs, sems):
            cp.start()

    def finish(_, ins, outs, sems):
        cps = copies(ins, outs, sems)
        send, recv, loc = sems
        x, y, c = _place()
        mychip = 2 * x + y
        for a in range(n):
            for j, chip in enumerate(_chips_of(x, y)):
                pltpu.make_async_remote_copy(
                    src_ref=ins[a].at[mychip], dst_ref=outs[a].at[2 * chip[0] + chip[1]],
                    send_sem=send.at[a, j], recv_sem=recv.at[a, j], device_id=(x, y, c), device_id_type=MESH).wait_recv()
        for a in range(n):
            cps[4 * a].wait()
            for j in range(3):
                cps[4 * a + 1 + j].wait_send()

    return _Job(psums, [jax.ShapeDtypeStruct(p.shape, BF16) for p in psums], {},
                [pltpu.SemaphoreType.DMA((n, 3)), pltpu.SemaphoreType.DMA((n, 3)), pltpu.SemaphoreType.DMA((n,))],
                start, finish)


def _join_jobs(*jobs):
    jobs = [j for j in jobs if j is not None]
    if len(jobs) <= 1:
        return jobs[0] if jobs else None
    cut = lambda seq, sizes: [seq[sum(sizes[:k]):sum(sizes[:k + 1])] for k in range(len(sizes))]
    n_in = [len(j.ins) for j in jobs]
    n_out = [len(j.out_shapes) for j in jobs]
    n_sem = [len(j.sems) for j in jobs]
    aliases = {}
    for k, j in enumerate(jobs):
        for a, b in j.aliases.items():
            aliases[sum(n_in[:k]) + a] = sum(n_out[:k]) + b

    def start(ins, outs, sems):
        return [j.start(i, o, s) for j, i, o, s in zip(jobs, cut(ins, n_in), cut(outs, n_out), cut(sems, n_sem))]

    def finish(sts, ins, outs, sems):
        for j, st, i, o, s in zip(jobs, sts, cut(ins, n_in), cut(outs, n_out), cut(sems, n_sem)):
            j.finish(st, i, o, s)

    return _Job([t for j in jobs for t in j.ins], [t for j in jobs for t in j.out_shapes], aliases,
                [t for j in jobs for t in j.sems], start, finish)


def _run_job(job, name):
    def body(ins, outs, scr, comm):
        comm[1](comm[0]())

    return _host_call(body, name, [], [], [], [], [], {}, job)[1]


def _allreduce_small(v, job=None):
    def body(ins, outs, scr, comm):
        (v_ref,), (o_ref,), (slots, send_sems, recv_sems) = ins, outs, scr
        st = comm[0]() if comm else None
        x, y, c = _place()
        me = 4 * x + 2 * y + c
        slots[me] = v_ref[...]

        def copy(k):
            peer = (x ^ ((k >> 2) & 1), y ^ ((k >> 1) & 1), c ^ (k & 1))
            return pltpu.make_async_remote_copy(
                src_ref=v_ref, dst_ref=slots.at[me], send_sem=send_sems.at[k - 1], recv_sem=recv_sems.at[k - 1],
                device_id=peer, device_id_type=MESH)

        def arrival(k):
            return pltpu.make_async_remote_copy(
                src_ref=v_ref, dst_ref=slots.at[me ^ k], send_sem=send_sems.at[k - 1], recv_sem=recv_sems.at[k - 1],
                device_id=(x, y, c), device_id_type=MESH)

        sends = [copy(k) for k in range(1, NDEV)]
        for cp in sends:
            cp.start()
        for k in range(1, NDEV):
            arrival(k).wait_recv()
        for cp in sends:
            cp.wait_send()
        acc = slots[0]
        for d in range(1, NDEV):
            acc = acc + slots[d]
        o_ref[...] = acc
        if comm:
            comm[1](st)

    (out,), extra = _host_call(
        body, "allreduce_small", [v], [VMEM_SPEC], [jax.ShapeDtypeStruct(v.shape, F32)], [VMEM_SPEC],
        [pltpu.VMEM((NDEV,) + v.shape, F32), pltpu.SemaphoreType.DMA((NDEV - 1,)),
         pltpu.SemaphoreType.DMA((NDEV - 1,))], {}, job)
    return (out, extra) if job else out


def _pair_sums(views, gots, core):
    n = len(views)

    def body(c_ref, *refs):
        for a in range(n):
            refs[2 * n + a][...] = (refs[a][...] + refs[n + a][...]).astype(BF16)

    def vspec(v):
        return pl.BlockSpec((None, None, v.shape[2] // 2, v.shape[3]), lambda k, h, c: (k, c[0], h, 0))

    def gspec(g):
        return pl.BlockSpec((None, g.shape[1] // 2, g.shape[2]), lambda k, h, c: (k, h, 0))

    return pl.pallas_call(
        body, name="pair_sums",
        grid_spec=pltpu.PrefetchScalarGridSpec(
            num_scalar_prefetch=1, grid=(4, 2),
            in_specs=[vspec(v) for v in views] + [gspec(g) for g in gots],
            out_specs=[gspec(g) for g in gots]),
        out_shape=[jax.ShapeDtypeStruct(g.shape, BF16) for g in gots],
        compiler_params=_cp(("parallel", "parallel")))(core, *views, *gots)


def _chip_sums(parts):
    n = len(parts)

    def body(*refs):
        for a in range(n):
            acc = refs[a][0].astype(F32)
            for k in range(1, 4):
                acc = acc + refs[a][k].astype(F32)
            refs[n + a][...] = acc

    return pl.pallas_call(
        body, name="chip_sums", in_specs=[VMEM_SPEC] * n, out_specs=[VMEM_SPEC] * n,
        out_shape=[jax.ShapeDtypeStruct(p.shape[1:], F32) for p in parts], compiler_params=_cp())(*parts)


def _permute_in(w):
    lead = w.shape[:-1]
    return w.reshape(lead + (3, 3, 2, BQ)).swapaxes(-2, -3).reshape(lead + (QKVW,))


def _unpermute_in(w):
    lead = w.shape[:-1]
    return w.reshape(lead + (3, 2, 3, BQ)).swapaxes(-2, -3).reshape(lead + (QKVW,))


def _row(v):
    v = v.reshape(-1)
    return jnp.pad(v, (0, D - v.shape[0])).reshape(1, D)


def kernel(x, w_in, f_bias, conv_w, w_out, rel_bias, ln1_g, ln1_b, w_gate, w_up, w_down, ln2_g, ln2_b, loss_target, m_w_in, m_f_bias, m_conv_w, m_w_out, m_rel_bias, m_ln1_g, m_ln1_b, m_w_gate, m_w_up, m_w_down, m_ln2_g, m_ln2_b, v_w_in, v_f_bias, v_conv_w, v_w_out, v_rel_bias, v_ln1_g, v_ln1_b, v_w_gate, v_w_up, v_w_down, v_ln2_g, v_ln2_b):
    xi, yi, ci = _place()
    me = 4 * xi + 2 * yi + ci
    core = jnp.reshape(ci, (1,)).astype(jnp.int32)

    win_s = jnp.concatenate([_permute_in(w_in[..., :QKVW]), w_in[..., QKVW:]], axis=-1)
    win_s = jnp.pad(win_s, ((0, 0), (0, 0), (0, NPAD - NPROJ))).astype(BF16)
    per_layer = [win_s, w_out.astype(BF16), jnp.swapaxes(w_gate, 1, 2).astype(BF16),
                 jnp.swapaxes(w_up, 1, 2).astype(BF16), w_down.astype(BF16)]
    sh = [[s[l] for s in per_layer] for l in range(2)]

    def whole(g):
        return g.reshape(NDEV * g.shape[1], g.shape[2])

    cw_rows = lax.dynamic_update_slice(jnp.zeros((2, 3, 256), F32), conv_w, (0, 0, me * 32))
    small = jnp.concatenate([_row(cw_rows[0]), _row(cw_rows[1]), jnp.zeros((SMALL_ROWS - 2, D), F32)], axis=0)
    small, leg_a = _allreduce_small(small, job=_gather_job_a(sh[0][:1]))
    cw_full = small[0:2, :CONVW].reshape(2, 3, 256)
    cw8 = jnp.pad(cw_full, ((0, 0), (0, 5), (0, 0)))
    fb = jnp.pad(f_bias, ((0, 0), (0, GATEW - NH))).reshape(2, 1, GATEW)
    tbl, leg_b = _dil_table(rel_bias, job=_gather_job_b(list(leg_a)))
    W = [{"win": whole(leg_b[0])}, {}]

    def wrow(tn, K, blk=0):
        return pl.BlockSpec((tn, K), lambda i, j: (j, blk))

    def arow(tm, K, blk=0):
        return pl.BlockSpec((tm, K), lambda i, j: (i, blk))

    h = x.reshape(T, D)
    hb = h.astype(BF16)
    saved = []
    for l in range(2):
        Win = W[l]["win"]
        qkv, conv, gate = _proj(hb, Win)
        cum = _fox_prep(gate, fb[l])
        cq = cum[:, :NH].reshape(BL, S, NH).transpose(0, 2, 1).reshape(NSTAT, S)
        ckb = jnp.broadcast_to(cq[:, :, None], (NSTAT, S, BQ))
        if l == 0:
            mixed, amat, a0 = _sb_fwd(qkv, job=_gather_job_a(sh[0][1:]))
            mixed, lse_d, ex = _flash_fwd(qkv, mixed, 1, False, (tbl,),
                                          job=_join_jobs(_gather_job_b(list(a0)), _gather_job_a(sh[1][:2])))
            W[0].update(zip(("wout", "wgT", "wuT", "wd"), [whole(t) for t in ex[:4]]))
            mixed, lse_f, o_fox, ex = _flash_fwd(qkv, mixed, 2, True, (cq, ckb),
                                                 job=_join_jobs(_gather_job_b(list(ex[4:])), _gather_job_a(sh[1][2:])))
            W[1].update(zip(("win", "wout"), [whole(t) for t in ex[:2]]))
            a2 = list(ex[2:])
        else:
            mixed, amat, ex = _sb_fwd(qkv, job=_gather_job_b(a2))
            W[1].update(zip(("wgT", "wuT", "wd"), [whole(t) for t in ex]))
            mixed, lse_d, _ = _flash_fwd(qkv, mixed, 1, False, (tbl,))
            mixed, lse_f, o_fox, _ = _flash_fwd(qkv, mixed, 2, True, (cq, ckb))
        Wout, WgT, WuT, Wd = W[l]["wout"], W[l]["wgT"], W[l]["wuT"], W[l]["wd"]
        mixed = _conv_fwd(conv, cw8[l], mixed)
        x1, xh1, r1, x1b = _mm_ln(mixed, Wout, h, ln1_g[l:l + 1], ln1_b[l:l + 1], "out_proj_ln")
        fs, ft, a, x2, xh2, r2, x2b = _ffn_fwd(x1b, x1, WgT, WuT, Wd, ln2_g[l:l + 1], ln2_b[l:l + 1])
        saved.append(dict(h=hb, qkv=qkv, conv=conv, gate=gate, cq=cq, ckb=ckb, mixed=mixed, amat=amat, lse_d=lse_d,
                          lse_f=lse_f, o_fox=o_fox, x1=x1b, xh1=xh1, r1=r1, fs=fs, ft=ft, a=a, xh2=xh2, r2=r2))
        h, hb = x2, x2b

    dy = h

    def view(gr):
        return gr.reshape(4, 2, gr.shape[0] // NDEV, gr.shape[1])

    G = [None, None]
    small_g = {}
    shard_g = {}
    for l in (1, 0):
        sv = saved[l]
        Win, Wout, WgT, WuT, Wd = W[l]["win"], W[l]["wout"], W[l]["wgT"], W[l]["wuT"], W[l]["wd"]
        res = _ffn_bwd(dy, sv["xh2"], sv["r2"], ln2_g[l:l + 1], sv["fs"], sv["ft"], Wd, WgT, WuT,
                       target=loss_target.reshape(T, D) if l == 1 else None)
        dgt, dut, ds2b, dx1, dg2, db2 = res[:6]
        if l == 1:
            sq = res[6]
        G_d = _mm_tn(sv["a"], ds2b, None, C=D, Ka=DFF, N=D, tm=256, tn=1024, tk=T, ooff=0, name="grad_w_down")
        G_g = _mm_tn(dgt, sv["x1"], None, C=D, Ka=DFF, N=D, tm=256, tn=1024, tk=T, ooff=0, name="grad_w_gate")
        G_u = _mm_tn(dut, sv["x1"], None, C=D, Ka=DFF, N=D, tm=256, tn=1024, tk=T, ooff=0, name="grad_w_up")
        ds1, dg1, db1, ds1b, dmixed = _ln_bwd(dx1, sv["xh1"], sv["r1"], ln1_g[l:l + 1], Wout)
        G_out = _mm_tn(sv["mixed"], ds1b, None, C=D, Ka=D, N=D, tm=256, tn=1024, tk=T, ooff=0, name="grad_w_out")
        early = [view(t) for t in (G_g, G_u, G_d, G_out)] + ([view(G[1]["in"])] if l == 0 else [])
        dqkv, gots = _sb_bwd(sv["qkv"], dmixed, sv["amat"], job=_sibling_job(early))
        ps = _pair_sums(early, list(gots), core)
        dqkv, dtbl, pa = _flash_bwd(sv["qkv"], sv["mixed"], dmixed, sv["lse_d"], dqkv, 1, False, (tbl,),
                                    job=_chip_job(ps[:2]))
        dqkv, dck, pb = _flash_bwd(sv["qkv"], sv["o_fox"], dmixed, sv["lse_f"], dqkv, 2, True,
                                   (sv["cq"], sv["ckb"]), job=_chip_job(ps[2:]))
        sums = _chip_sums(list(pa) + list(pb))
        shard_g[l] = dict(zip(("g", "u", "d", "out"), sums[:4]))
        if l == 0:
            shard_g[1]["in"] = sums[4]
        dconv, dcw = _conv_bwd(sv["conv"], cw8[l], dmixed)
        dcum = jnp.pad(dck.reshape(S, BL, NH).transpose(1, 0, 2).reshape(T, NH), ((0, 0), (0, GATEW - NH)))
        dgate, dfb = _fox_post(dcum, sv["gate"], fb[l])
        drb = _dil_table_bwd(dtbl)
        G_in = _mm_tn(sv["h"], dqkv, None, C=NPAD, Ka=D, N=QKVW, tm=512, tn=768, tk=T, ooff=0, name="grad_w_in_qkv")
        G_in = _mm_tn(sv["h"], dconv, G_in, C=NPAD, Ka=D, N=CONVW, tm=256, tn=768, tk=T, ooff=3,
                      name="grad_w_in_conv")
        G_in = _mm_tn(sv["h"], dgate, G_in, C=NPAD, Ka=D, N=GATEW, tm=1024, tn=128, tk=1024, ooff=24,
                      name="grad_w_in_gate")
        G[l] = {"in": G_in, "out": G_out, "g": G_g, "u": G_u, "d": G_d}
        if l == 0:
            late = [view(G_in)]
            tail = _chip_job(_pair_sums(late, list(_run_job(_sibling_job(late), "sibling_exchange")), core))
            dy, parts = _mm([(dqkv, arow(1024, QKVW), Win, wrow(512, QKVW, 0)),
                             (dconv, arow(1024, CONVW), Win, wrow(512, CONVW, 3)),
                             (dgate, arow(1024, GATEW), Win, wrow(512, GATEW, 24))],
                            nt=True, M=T, N=D, tm=1024, tn=512, out_dtype=F32, name="proj_dx", res=ds1,
                            res_scale=ALPHA, job=tail)
            shard_g[0]["in"] = _chip_sums(list(parts))[0]
        else:
            dy = _proj_bwd(dqkv, dconv, dgate, Win, ds1)
        small_g[l] = dict(ln1_g=dg1, ln1_b=db1, ln2_g=dg2, ln2_b=db2, cw=dcw[0:3].reshape(1, CONVW),
                          fb=dfb[:, :NH], rb=drb[:, :NH])
    grad_x = dy.reshape(BL, S, D)

    rows = []
    for name in ("ln1_g", "ln1_b", "ln2_g", "ln2_b"):
        rows += [small_g[0][name], small_g[1][name]]
    rows += [_row(small_g[0]["cw"]), _row(small_g[1]["cw"]),
             _row(jnp.concatenate([small_g[0]["fb"], small_g[1]["fb"]], axis=0)),
             _row(small_g[0]["rb"] + small_g[1]["rb"]), _row(sq)]
    rows.append(jnp.zeros((SMALL_ROWS - len(rows), D), F32))
    sg = _allreduce_small(jnp.concatenate(rows, axis=0))
    loss = sg[12, 0] * (0.5 / D)
    g_ln1_g, g_ln1_b, g_ln2_g, g_ln2_b = sg[0:2], sg[2:4], sg[4:6], sg[6:8]
    g_conv_full = sg[8:10, :CONVW].reshape(2, 3, 256)
    g_conv = lax.dynamic_slice(g_conv_full, (0, 0, me * 32), (2, 3, 32))
    g_fb = sg[10, :2 * NH].reshape(2, NH)
    g_rb = sg[11, :32 * NH].reshape(32, NH)

    def both(name):
        return jnp.stack([shard_g[0][name], shard_g[1][name]])

    g_in = both("in")
    g_w_in = jnp.concatenate([_unpermute_in(g_in[..., :QKVW]), g_in[..., QKVW:NPROJ]], axis=-1)
    g_w_out = both("out")
    g_w_gate = jnp.swapaxes(both("g"), 1, 2)
    g_w_up = jnp.swapaxes(both("u"), 1, 2)
    g_w_down = both("d")

    up_in = _adamw(w_in, g_w_in, m_w_in, v_w_in, 64)
    up_out = _adamw(w_out, g_w_out, m_w_out, v_w_out, 128)
    up_gate = _adamw(w_gate, g_w_gate, m_w_gate, v_w_gate, 256)
    up_up = _adamw(w_up, g_w_up, m_w_up, v_w_up, 256)
    up_down = _adamw(w_down, g_w_down, m_w_down, v_w_down, 352)

    def pack(fbv, cwv, rbv, l1g, l1b, l2g, l2b):
        r = [l1g, l1b, l2g, l2b, _row(cwv), _row(fbv), _row(rbv)]
        r.append(jnp.zeros((SMALL_ROWS - 11, D), F32))
        return jnp.concatenate(r, axis=0)

    pw = pack(f_bias, conv_w, rel_bias, ln1_g, ln1_b, ln2_g, ln2_b)
    pg = pack(g_fb, g_conv, g_rb, g_ln1_g, g_ln1_b, g_ln2_g, g_ln2_b)
    pm = pack(m_f_bias, m_conv_w, m_rel_bias, m_ln1_g, m_ln1_b, m_ln2_g, m_ln2_b)
    pv = pack(v_f_bias, v_conv_w, v_rel_bias, v_ln1_g, v_ln1_b, v_ln2_g, v_ln2_b)
    ups = [u[0] for u in _adamw(pw[None], pg[None], pm[None], pv[None], SMALL_ROWS)]

    def unpack(p):
        return dict(ln1_g=p[0:2], ln1_b=p[2:4], ln2_g=p[4:6], ln2_b=p[6:8],
                    conv_w=p[8, :192].reshape(2, 3, 32), f_bias=p[9, :2 * NH].reshape(2, NH),
                    rel_bias=p[10, :32 * NH].reshape(32, NH))

    sm = [unpack(p) for p in ups]

    def group(k):
        return (up_in[k], sm[k]["f_bias"], sm[k]["conv_w"], up_out[k], sm[k]["rel_bias"], sm[k]["ln1_g"],
                sm[k]["ln1_b"], up_gate[k], up_up[k], up_down[k], sm[k]["ln2_g"], sm[k]["ln2_b"])

    grads = (g_w_in, g_fb, g_conv, g_w_out, g_rb, g_ln1_g, g_ln1_b, g_w_gate, g_w_up, g_w_down, g_ln2_g, g_ln2_b)
    return (loss, grad_x) + grads + group(0) + group(1) + group(2)
```

```python
import math

import numpy as np
import jax
import jax.numpy as jnp
from jax import lax
from jax.experimental import pallas as pl
from jax.experimental.pallas import tpu as pltpu

F32 = jnp.float32
BF16 = jnp.bfloat16
MESH = pl.DeviceIdType.MESH

D = 1024
S = 2048
BL = 2
T = BL * S
NH = 4
DFF = 2816
NPROJ = 3076
NPAD = 3200
QKVW = 2304
CONVW = 768
GATEW = 128
PAIRW = 384
BQ = 128
HB = 2 * BQ
NB = S // BQ
NDEV = 8
NSTAT = BL * NH
ALPHA = 4.0 ** 0.25
SCALE = 0.125
NEG = -1e30
LN_EPS = 1e-5
ADAM_LR, ADAM_B1, ADAM_B2, ADAM_EPS, ADAM_WD, ADAM_STEP = 0.001, 0.9, 0.999, 1e-08, 0.01, 10
VMEM_LIMIT = 56 * 1024 * 1024
SMALL_ROWS = 16


def _bucket_thresholds():
    d = np.arange(0, S)
    nf = np.maximum(d, 1).astype(np.float32)
    large = 16 + (np.log(nf / np.float32(16)) / np.float32(math.log(128)) * np.float32(16)).astype(np.int32)
    b = np.where(d < 16, d, np.minimum(large, 31))
    return [int(np.argmax(b >= k)) for k in range(32)]


BUCKET_TH = _bucket_thresholds()


def _cp(sem=None, vmem=VMEM_LIMIT):
    return pltpu.CompilerParams(dimension_semantics=sem, vmem_limit_bytes=vmem)


def _dot(a, b):
    return lax.dot_general(a, b, (((1,), (0,)), ((), ())), preferred_element_type=F32)


def _dot_nt(a, b):
    return lax.dot_general(a, b, (((1,), (1,)), ((), ())), preferred_element_type=F32)


def _dot_tn(a, b):
    return lax.dot_general(a, b, (((0,), (0,)), ((), ())), preferred_element_type=F32)


def _split2(x):
    hi = x.astype(BF16)
    mid = (x - hi.astype(F32)).astype(BF16)
    return jnp.concatenate([hi, mid], axis=1)


def _split3(x):
    hi = x.astype(BF16)
    r = x - hi.astype(F32)
    mid = r.astype(BF16)
    lo = (r - mid.astype(F32)).astype(BF16)
    return jnp.concatenate([hi, mid, lo], axis=1)


def _log_sigmoid(u):
    return jnp.minimum(u, 0.0) - jnp.log1p(jnp.exp(-jnp.abs(u)))


def _log_sigmoid_tile(u):
    return jnp.minimum(u, 0.0) - jnp.log(1.0 + jnp.exp(jnp.minimum(u, -u)))


def _iota(shape, dim):
    return lax.broadcasted_iota(jnp.int32, shape, dim)


ANY_SPEC = pl.BlockSpec(memory_space=pl.ANY)
VMEM_SPEC = pl.BlockSpec(memory_space=pltpu.VMEM)


def _mm(pairs, *, nt, M, N, tm, tn, out_dtype, name, res=None, res_scale=1.0, job=None):
    n = len(pairs)
    n_in = 2 * n + (res is not None)
    jins = job.ins if job else []
    jouts = job.out_shapes if job else []
    gi, gj = M // tm, N // tn

    def body(*refs):
        o_ref = refs[n_in + len(jins)]
        if job:
            jrefs = (refs[n_in:n_in + len(jins)], refs[n_in + len(jins) + 1:n_in + len(jins) + 1 + len(jouts)],
                     refs[n_in + len(jins) + 1 + len(jouts):])

            @pl.when((pl.program_id(0) == 0) & (pl.program_id(1) == 0))
            def _():
                job.start(*jrefs)

        acc = None
        for p in range(n):
            a = refs[2 * p][...].astype(BF16)
            b = refs[2 * p + 1][...]
            d = _dot_nt(a, b) if nt else _dot(a, b)
            acc = d if acc is None else acc + d
        if res is not None:
            acc = acc + res_scale * refs[2 * n][...]
        o_ref[...] = acc.astype(out_dtype)
        if job:
            @pl.when((pl.program_id(0) == gi - 1) & (pl.program_id(1) == gj - 1))
            def _():
                job.finish(None, *jrefs)

    ops, specs = [], []
    for a, asp, b, bsp in pairs:
        ops += [a, b]
        specs += [asp, bsp]
    if res is not None:
        ops.append(res)
        specs.append(pl.BlockSpec((tm, tn), lambda i, j: (i, j)))
    out = pl.pallas_call(
        body, name=name, grid=(gi, gj), in_specs=specs + [ANY_SPEC] * len(jins),
        out_specs=[pl.BlockSpec((tm, tn), lambda i, j: (i, j))] + [ANY_SPEC] * len(jouts),
        out_shape=[jax.ShapeDtypeStruct((M, N), out_dtype)] + list(jouts),
        scratch_shapes=list(job.sems) if job else [],
        input_output_aliases={n_in + a: 1 + b for a, b in job.aliases.items()} if job else {},
        compiler_params=_cp(("arbitrary", "arbitrary") if job else ("parallel", "parallel")))(*ops, *jins)
    return (out[0], out[1:]) if job else out[0]


def _mm_tn(a, b, gbuf, *, C, Ka, N, tm, tn, tk, ooff, name):
    def body(*refs):
        a_ref, b_ref, o_ref = refs[0], refs[1], refs[-1]
        k = pl.program_id(2)
        d = _dot_tn(a_ref[...].astype(BF16), b_ref[...].astype(BF16))

        @pl.when(k == 0)
        def _():
            o_ref[...] = d

        @pl.when(k > 0)
        def _():
            o_ref[...] += d

    ops = [a, b] + ([] if gbuf is None else [gbuf])
    return pl.pallas_call(
        body, name=name, grid=(Ka // tm, N // tn, T // tk),
        in_specs=[pl.BlockSpec((tk, tm), lambda i, j, k: (k, i)),
                  pl.BlockSpec((tk, tn), lambda i, j, k: (k, j))] + ([] if gbuf is None else [ANY_SPEC]),
        out_specs=pl.BlockSpec((tm, tn), lambda i, j, k: (i, ooff + j)),
        out_shape=jax.ShapeDtypeStruct((Ka, C), F32),
        input_output_aliases={} if gbuf is None else {2: 0},
        compiler_params=_cp(("parallel", "parallel", "arbitrary")))(*ops)


def _proj(xb, w):
    tm = 512

    def body(x_ref, w_ref, qkv_ref, conv_ref, gate_ref):
        xv = x_ref[...]
        qkv_ref[...] = _dot(xv, w_ref[:, 0:QKVW]).astype(BF16)
        conv_ref[...] = _dot(xv, w_ref[:, QKVW:QKVW + CONVW])
        gate_ref[...] = _dot(xv, w_ref[:, QKVW + CONVW:NPAD])

    def rows(n):
        return pl.BlockSpec((tm, n), lambda i: (i, 0))

    return pl.pallas_call(
        body, name="proj", grid=(T // tm,),
        in_specs=[rows(D), pl.BlockSpec((D, NPAD), lambda i: (0, 0))],
        out_specs=[rows(QKVW), rows(CONVW), rows(GATEW)],
        out_shape=[jax.ShapeDtypeStruct((T, QKVW), BF16), jax.ShapeDtypeStruct((T, CONVW), F32),
                   jax.ShapeDtypeStruct((T, GATEW), F32)],
        compiler_params=_cp(("parallel",)))(xb, w)


def _proj_bwd(dqkv, dconv, dgate, w, res):
    tm = 512

    def body(a_ref, b_ref, c_ref, w_ref, r_ref, o_ref):
        acc = ALPHA * r_ref[...] + _dot_nt(a_ref[...], w_ref[:, 0:QKVW])
        acc = acc + _dot_nt(b_ref[...], w_ref[:, QKVW:QKVW + CONVW])
        o_ref[...] = acc + _dot_nt(c_ref[...].astype(BF16), w_ref[:, QKVW + CONVW:NPAD])

    def rows(n):
        return pl.BlockSpec((tm, n), lambda i: (i, 0))

    return pl.pallas_call(
        body, name="proj_bwd", grid=(T // tm,),
        in_specs=[rows(QKVW), rows(CONVW), rows(GATEW), pl.BlockSpec((D, NPAD), lambda i: (0, 0)), rows(D)],
        out_specs=rows(D), out_shape=jax.ShapeDtypeStruct((T, D), F32),
        compiler_params=_cp(("parallel",)))(dqkv, dconv, dgate, w, res)


def _ffn_fwd(xb, x, wgt, wut, wd, gam, bet):
    tm, ch = 512, 256

    def body(xb_ref, x_ref, g_ref, b_ref, wg_hbm, wu_hbm, wd_hbm,
             go_ref, uo_ref, ao_ref, y_ref, xh_ref, r_ref, yb_ref, wg_v, wu_v, wd_v, sem):
        loads = [pltpu.make_async_copy(s, d, sem.at[k])
                 for k, (s, d) in enumerate(((wg_hbm, wg_v), (wu_hbm, wu_v), (wd_hbm, wd_v)))]

        @pl.when(pl.program_id(0) == 0)
        def _():
            for cp in loads:
                cp.start()
            loads[0].wait()
            loads[1].wait()

        xv = xb_ref[...]
        for c in range(0, DFF, ch):
            gv = _dot_nt(xv, wg_v[c:c + ch, :])
            uv = _dot_nt(xv, wu_v[c:c + ch, :])
            go_ref[:, c:c + ch] = gv.astype(BF16)
            uo_ref[:, c:c + ch] = uv.astype(BF16)
            ao_ref[:, c:c + ch] = (gv * jax.nn.sigmoid(gv) * uv).astype(BF16)
        @pl.when(pl.program_id(0) == 0)
        def _():
            loads[2].wait()

        s = ALPHA * x_ref[...] + _dot(ao_ref[...], wd_v[...])
        mu = jnp.mean(s, axis=-1, keepdims=True)
        xc = s - mu
        var = jnp.mean(xc * xc, axis=-1, keepdims=True)
        r = lax.rsqrt(var + LN_EPS)
        xh = xc * r
        xh_ref[...] = xh.astype(BF16)
        r_ref[...] = r
        y = xh * g_ref[...] + b_ref[...]
        y_ref[...] = y
        yb_ref[...] = y.astype(BF16)

    row = pl.BlockSpec((tm, D), lambda i: (i, 0))
    wide = pl.BlockSpec((tm, DFF), lambda i: (i, 0))
    vec = pl.BlockSpec((1, D), lambda i: (0, 0))
    wsl = pltpu.VMEM((DFF, D), BF16)
    hid = jax.ShapeDtypeStruct((T, DFF), BF16)
    return pl.pallas_call(
        body, name="ffn_fwd", grid=(T // tm,),
        in_specs=[row, row, vec, vec, ANY_SPEC, ANY_SPEC, ANY_SPEC],
        out_specs=[wide, wide, wide, row, row, pl.BlockSpec((tm, 1), lambda i: (i, 0)), row],
        out_shape=[hid, hid, hid, jax.ShapeDtypeStruct((T, D), F32), jax.ShapeDtypeStruct((T, D), BF16),
                   jax.ShapeDtypeStruct((T, 1), F32), jax.ShapeDtypeStruct((T, D), BF16)],
        scratch_shapes=[wsl, wsl, wsl, pltpu.SemaphoreType.DMA((3,))],
        compiler_params=_cp(("arbitrary",)))(xb, x, gam, bet, wgt, wut, wd)


def _ffn_bwd(dy, xh, r, gam, g, u, wd, wgt, wut, target=None):
    tm, ch = 256, 256

    def body(*refs):
        if target is None:
            (dy_ref, xh_ref, r_ref, gam_ref, g_ref, u_ref, wd_hbm, wg_hbm, wu_hbm,
             dg_ref, du_ref, dsb_ref, dx_ref, dgam_ref, dbet_ref, wd_v, wg_v, wu_v, sem) = refs
        else:
            (dy_ref, t_ref, xh_ref, r_ref, gam_ref, g_ref, u_ref, wd_hbm, wg_hbm, wu_hbm,
             dg_ref, du_ref, dsb_ref, dx_ref, dgam_ref, dbet_ref, sq_ref, wd_v, wg_v, wu_v, sem) = refs
        loads = [pltpu.make_async_copy(s, d, sem.at[k])
                 for k, (s, d) in enumerate(((wd_hbm, wd_v), (wg_hbm, wg_v), (wu_hbm, wu_v)))]

        @pl.when(pl.program_id(0) == 0)
        def _():
            for cp in loads:
                cp.start()
            loads[0].wait()

        if target is None:
            dyv = dy_ref[...]
        else:
            e = dy_ref[...] - t_ref[...]
            dyv = e * (1.0 / D)
            p = jnp.sum(jnp.sum(e * e, axis=1, keepdims=True), axis=0, keepdims=True)

            @pl.when(pl.program_id(0) == 0)
            def _():
                sq_ref[...] = p

            @pl.when(pl.program_id(0) > 0)
            def _():
                sq_ref[...] += p

        xhv = xh_ref[...].astype(F32)
        dxh = dyv * gam_ref[...]
        m1 = jnp.mean(dxh, axis=-1, keepdims=True)
        m2 = jnp.mean(dxh * xhv, axis=-1, keepdims=True)
        ds = r_ref[...] * (dxh - m1 - xhv * m2)
        pg = jnp.sum(dyv * xhv, axis=0, keepdims=True)
        pb = jnp.sum(dyv, axis=0, keepdims=True)

        @pl.when(pl.program_id(0) == 0)
        def _():
            dgam_ref[...] = pg
            dbet_ref[...] = pb

        @pl.when(pl.program_id(0) > 0)
        def _():
            dgam_ref[...] += pg
            dbet_ref[...] += pb

        db = ds.astype(BF16)
        dsb_ref[...] = db
        for c in range(0, DFF, ch):
            da = _dot_nt(db, wd_v[c:c + ch, :])
            gv = g_ref[:, c:c + ch].astype(F32)
            sg = jax.nn.sigmoid(gv)
            dg_ref[:, c:c + ch] = (da * u_ref[:, c:c + ch].astype(F32) * (sg * (1.0 + gv * (1.0 - sg)))).astype(BF16)
            du_ref[:, c:c + ch] = (da * (gv * sg)).astype(BF16)
        @pl.when(pl.program_id(0) == 0)
        def _():
            loads[1].wait()
            loads[2].wait()

        dx_ref[...] = ALPHA * ds + _dot(dg_ref[...], wg_v[...]) + _dot(du_ref[...], wu_v[...])

    row = pl.BlockSpec((tm, D), lambda i: (i, 0))
    wide = pl.BlockSpec((tm, DFF), lambda i: (i, 0))
    vec = pl.BlockSpec((1, D), lambda i: (0, 0))
    wsl = pltpu.VMEM((DFF, D), BF16)
    last = target is not None
    return pl.pallas_call(
        body, name="ffn_bwd_loss" if last else "ffn_bwd", grid=(T // tm,),
        in_specs=[row] + ([row] if last else [])
        + [row, pl.BlockSpec((tm, 1), lambda i: (i, 0)), vec, wide, wide, ANY_SPEC, ANY_SPEC, ANY_SPEC],
        out_specs=[wide, wide, row, row, vec, vec] + ([pl.BlockSpec((1, 1), lambda i: (0, 0))] if last else []),
        out_shape=[jax.ShapeDtypeStruct((T, DFF), BF16), jax.ShapeDtypeStruct((T, DFF), BF16),
                   jax.ShapeDtypeStruct((T, D), BF16), jax.ShapeDtypeStruct((T, D), F32),
                   jax.ShapeDtypeStruct((1, D), F32), jax.ShapeDtypeStruct((1, D), F32)]
        + ([jax.ShapeDtypeStruct((1, 1), F32)] if last else []),
        scratch_shapes=[wsl, wsl, wsl, pltpu.SemaphoreType.DMA((3,))],
        compiler_params=_cp(("arbitrary",)))(dy, *([target] if last else []), xh, r, gam, g, u, wd, wgt, wut)


def _mm_ln(a, w, x, gam, bet, name):
    tm = 256
    K = a.shape[1]

    def body(a_ref, w_ref, x_ref, g_ref, b_ref, y_ref, xh_ref, r_ref, yb_ref):
        s = ALPHA * x_ref[...] + _dot(a_ref[...], w_ref[...])
        mu = jnp.mean(s, axis=-1, keepdims=True)
        xc = s - mu
        var = jnp.mean(xc * xc, axis=-1, keepdims=True)
        r = lax.rsqrt(var + LN_EPS)
        xh = xc * r
        xh_ref[...] = xh.astype(BF16)
        r_ref[...] = r
        y = xh * g_ref[...] + b_ref[...]
        y_ref[...] = y
        yb_ref[...] = y.astype(BF16)

    row = pl.BlockSpec((tm, D), lambda i: (i, 0))
    vec = pl.BlockSpec((1, D), lambda i: (0, 0))
    return pl.pallas_call(
        body, name=name, grid=(T // tm,),
        in_specs=[pl.BlockSpec((tm, K), lambda i: (i, 0)), pl.BlockSpec((K, D), lambda i: (0, 0)), row, vec, vec],
        out_specs=[row, row, pl.BlockSpec((tm, 1), lambda i: (i, 0)), row],
        out_shape=[jax.ShapeDtypeStruct((T, D), F32), jax.ShapeDtypeStruct((T, D), BF16),
                   jax.ShapeDtypeStruct((T, 1), F32), jax.ShapeDtypeStruct((T, D), BF16)],
        compiler_params=_cp(("parallel",)))(a, w, x, gam, bet)


def _ln_bwd(dy, xh, r, gam, w):
    tm = 256

    def body(dy_ref, xh_ref, r_ref, g_ref, w_ref, ds_ref, dg_ref, db_ref, dsb_ref, dm_ref):
        i = pl.program_id(0)
        dyv = dy_ref[...]
        xhv = xh_ref[...].astype(F32)
        dxh = dyv * g_ref[...]
        m1 = jnp.mean(dxh, axis=-1, keepdims=True)
        m2 = jnp.mean(dxh * xhv, axis=-1, keepdims=True)
        ds = r_ref[...] * (dxh - m1 - xhv * m2)
        ds_ref[...] = ds
        dsb = ds.astype(BF16)
        dsb_ref[...] = dsb
        dm_ref[...] = _dot_nt(dsb, w_ref[...]).astype(BF16)
        pg = jnp.sum(dyv * xhv, axis=0, keepdims=True)
        pb = jnp.sum(dyv, axis=0, keepdims=True)

        @pl.when(i == 0)
        def _():
            dg_ref[...] = pg
            db_ref[...] = pb

        @pl.when(i > 0)
        def _():
            dg_ref[...] += pg
            db_ref[...] += pb

    row = pl.BlockSpec((tm, D), lambda i: (i, 0))
    vec = pl.BlockSpec((1, D), lambda i: (0, 0))
    return pl.pallas_call(
        body, name="ln_bwd_proj", grid=(T // tm,),
        in_specs=[row, row, pl.BlockSpec((tm, 1), lambda i: (i, 0)), vec, pl.BlockSpec((D, D), lambda i: (0, 0))],
        out_specs=[row, vec, vec, row, row],
        out_shape=[jax.ShapeDtypeStruct((T, D), F32), jax.ShapeDtypeStruct((1, D), F32),
                   jax.ShapeDtypeStruct((1, D), F32), jax.ShapeDtypeStruct((T, D), BF16),
                   jax.ShapeDtypeStruct((T, D), BF16)],
        compiler_params=_cp(("arbitrary",)))(dy, xh, r, gam, w)


def _adamw(w, g, m, v, tr):
    L, R, C = w.shape

    def body(w_ref, g_ref, m_ref, v_ref, d_ref, m2_ref, v2_ref):
        gv = g_ref[...]
        m2 = ADAM_B1 * m_ref[...] + (1.0 - ADAM_B1) * gv
        v2 = ADAM_B2 * v_ref[...] + (1.0 - ADAM_B2) * (gv * gv)
        m_hat = m2 / (1.0 - ADAM_B1 ** ADAM_STEP)
        v_hat = v2 / (1.0 - ADAM_B2 ** ADAM_STEP)
        d_ref[...] = -ADAM_LR * (m_hat / (jnp.sqrt(v_hat) + ADAM_EPS) + ADAM_WD * w_ref[...])
        m2_ref[...] = m2
        v2_ref[...] = v2

    blk = pl.BlockSpec((None, tr, C), lambda l, i: (l, i, 0))
    sh = jax.ShapeDtypeStruct((L, R, C), F32)
    return pl.pallas_call(
        body, name="adamw", grid=(L, R // tr), in_specs=[blk] * 4, out_specs=[blk] * 3,
        out_shape=[sh, sh, sh], compiler_params=_cp(("parallel", "parallel")))(w, g, m, v)


class _Job:
    def __init__(self, ins, out_shapes, aliases, sems, start, finish):
        self.ins, self.out_shapes, self.aliases, self.sems = list(ins), list(out_shapes), dict(aliases), list(sems)
        self.start, self.finish = start, finish


def _host_call(body, name, ins, in_specs, out_shapes, out_specs, scratch, aliases, job):
    n_in, n_out, n_scr = len(ins), len(out_shapes), len(scratch)
    jins = job.ins if job else []
    jouts = job.out_shapes if job else []
    jsems = job.sems if job else []

    def wrapped(*refs):
        a = n_in
        b = a + len(jins)
        c = b + n_out
        d = c + len(jouts)
        e = d + n_scr
        comm = None
        if job:
            jrefs = (refs[a:b], refs[c:d], refs[e:])
            comm = (lambda: job.start(*jrefs), lambda st: job.finish(st, *jrefs))
        body(refs[:a], refs[b:c], refs[d:e], comm)

    al = dict(aliases)
    if job:
        for ji, jo in job.aliases.items():
            al[n_in + ji] = n_out + jo
    res = pl.pallas_call(
        wrapped, name=name, in_specs=list(in_specs) + [ANY_SPEC] * len(jins),
        out_specs=list(out_specs) + [ANY_SPEC] * len(jouts), out_shape=list(out_shapes) + list(jouts),
        scratch_shapes=list(scratch) + list(jsems), input_output_aliases=al,
        compiler_params=_cp())(*ins, *jins)
    return res[:n_out], res[n_out:]


def _copy_in(src, dst, sem):
    cp = pltpu.make_async_copy(src, dst, sem)
    cp.start()
    cp.wait()


CHAINS = [(p, b) for p in range(2) for b in range(BL)]
NC = len(CHAINS)
ROWS_SHAPE = jax.ShapeDtypeStruct((NSTAT, S), F32)
SLAB_QKV = pltpu.VMEM((T, 2 * PAIRW), BF16)
SLAB_OUT = pltpu.VMEM((T, 2 * BQ), BF16)
SLAB_O32 = pltpu.VMEM((T, 2 * BQ), F32)
SLAB_T = pltpu.VMEM((2, BQ, T), BF16)
SLAB_KEYB = pltpu.VMEM((NSTAT, S, BQ), F32)
ACC_KV = pltpu.VMEM((2, T, BQ), F32)
NTRI = NB * (NB + 1) // 2
A_TILES = jax.ShapeDtypeStruct((NTRI, NC, HB, BQ), BF16)
PAIR_DIAG = pltpu.VMEM((NC, 2, HB, HB), BF16)
A_SLOTS_OUT, A_SLOTS_IN = 2, 4


def _lane_masks():
    lane = _iota((1, BQ), 1)
    m0 = (lane < 64).astype(BF16)
    return m0, 1.0 - m0


def _merge_heads(x, first):
    return jnp.where(first, x[:BQ], x[BQ:])


def _row_masks():
    r = _iota((BQ, 1), 0)
    m0 = (r < 64).astype(BF16)
    return m0, 1.0 - m0


def _stack(x, m0, m1):
    return jnp.concatenate([x * m0, x * m1], axis=0)


def _stack_t(xt, r0, r1):
    return jnp.concatenate([xt * r0, xt * r1], axis=1)


def _tr(x):
    return x.T


def _rows(b, i):
    return pl.ds(pl.multiple_of(b * S + i * BQ, BQ), BQ)


def _transpose_slab(src, dst, col0):
    def blk(n, _):
        r = pl.ds(pl.multiple_of(n * BQ, BQ), BQ)
        for p in range(2):
            dst[p, :, r] = _tr(src[r, col0(p):col0(p) + BQ])
        return 0

    lax.fori_loop(0, T // BQ, blk, 0)


def _heads(x):
    return x[:BQ], x[BQ:]


def _bcast_heads(r0, r1):
    return jnp.concatenate([jnp.broadcast_to(r0, (BQ, BQ)), jnp.broadcast_to(r1, (BQ, BQ))], axis=0)


def _by_channel(r0, r1):
    return jnp.where(_iota((BQ, BQ), 0) < 64, r0, r1)


def _colsum2(x):
    return jnp.sum(x[:BQ], axis=0, keepdims=True), jnp.sum(x[BQ:], axis=0, keepdims=True)


def _stat_row(ref, p, b, h, i):
    c = b * NH + 2 * p + h
    return ref[c:c + 1, pl.ds(pl.multiple_of(i * BQ, BQ), BQ)]


def _put_row(ref, p, b, h, i, v):
    c = b * NH + 2 * p + h
    ref[c:c + 1, pl.ds(pl.multiple_of(i * BQ, BQ), BQ)] = v


def _valid_t(strict):
    r = _iota((HB, BQ), 0) & (BQ - 1)
    c = _iota((HB, BQ), 1)
    return (r < c) if strict else (r <= c)


def _tri_blockdiag(later):
    r = _iota((HB, HB), 0)
    c = _iota((HB, HB), 1)
    same = (r >= BQ) == (c >= BQ)
    return (same & ((c > r) if later else (c < r))).astype(BF16)


def _cum_mm(tri, x):
    y = _dot(tri, _split2(x))
    return y[:, :BQ] + y[:, BQ:]


def _kv_tiles(qkv_v, p, b, j):
    r = _rows(b, j)
    return qkv_v[r, p * PAIRW + BQ:p * PAIRW + 2 * BQ], qkv_v[r, p * PAIRW + 2 * BQ:p * PAIRW + 3 * BQ]


def _q_tile(qkv_v, p, b, i):
    return qkv_v[_rows(b, i), p * PAIRW:p * PAIRW + BQ] * SCALE


def _sb_fwd(qkv, job=None):
    def body(ins, outs, scr, comm):
        (qkv_hbm,), (o_hbm, a_hbm), (qkv_v, o_v, sem, vt_v, a_st, a_sems) = ins, outs, scr
        _copy_in(qkv_hbm.at[:, pl.ds(0, 2 * PAIRW)], qkv_v, sem)
        st = comm[0]() if comm else None
        _transpose_slab(qkv_v, vt_v, lambda p: p * PAIRW + 2 * BQ)
        m0, m1 = _lane_masks()
        r0, r1 = _row_masks()
        valid = _valid_t(True)
        later = _tri_blockdiag(True)

        def a_copy(n, t):
            slot = n % A_SLOTS_OUT
            return pltpu.make_async_copy(a_st.at[slot], a_hbm.at[t], a_sems.at[slot])

        def steps(qts, i, jj, cs, diag):
            j = i - jj
            ks = [_stack(_kv_tiles(qkv_v, p, b, j)[0], m0, m1) for p, b in CHAINS]
            zs = [_dot(ks[c], qts[c]) for c in range(NC)]
            lbs, lrs = [], []
            for c in range(NC):
                lb = _log_sigmoid_tile(zs[c])
                lr = lb - zs[c]
                if diag:
                    lr = jnp.where(valid, lr, 0.0)
                lbs.append(lb)
                lrs.append(lr)
            tails = [_cum_mm(later, lrs[c]) for c in range(NC)]
            avs = []
            for c in range(NC):
                a = jnp.exp(lbs[c] + tails[c] + _bcast_heads(*cs[c][0]))
                if diag:
                    a = jnp.where(valid, a, 0.0)
                avs.append(a.astype(BF16))
            n = (i * (i + 1)) // 2 + jj

            @pl.when(n >= A_SLOTS_OUT)
            def _():
                a_copy(n, 0).wait()
            for c in range(NC):
                a_st[n % A_SLOTS_OUT, c] = avs[c]
            a_copy(n, n - jj + j).start()
            out = []
            for c, (p, b) in enumerate(CHAINS):
                vts = _stack_t(vt_v[p, :, _rows(b, j)], r0, r1)
                s0, s1 = _colsum2(lrs[c])
                out.append(((cs[c][0][0] + s0, cs[c][0][1] + s1), cs[c][1] + _dot(vts, avs[c])))
            return tuple(out)

        def qblock(i, _):
            qts = [_tr(_q_tile(qkv_v, p, b, i)) for p, b in CHAINS]
            zr = jnp.zeros((1, BQ), F32)
            cs = steps(qts, i, 0, (((zr, zr), jnp.zeros((BQ, BQ), F32)),) * NC, True)
            cs = lax.fori_loop(0, i // 2, lambda n, cs: steps(qts, i, 2 * n + 2, steps(qts, i, 2 * n + 1, cs, False), False), cs)
            cs = lax.fori_loop(i - i % 2 + 1, i + 1, lambda jj, cs: steps(qts, i, jj, cs, False), cs)
            for c, (p, b) in enumerate(CHAINS):
                o_v[_rows(b, i), p * BQ:(p + 1) * BQ] = cs[c][1].T.astype(BF16)
            return 0

        lax.fori_loop(0, NB, qblock, 0)
        for n in range(NTRI - A_SLOTS_OUT, NTRI):
            a_copy(n, 0).wait()
        _copy_in(o_v, o_hbm.at[:, pl.ds(0, 2 * BQ)], sem)
        if comm:
            comm[1](st)

    (mixed, amat), extra = _host_call(
        body, "sb_fwd", [qkv], [ANY_SPEC], [jax.ShapeDtypeStruct((T, D), BF16), A_TILES], [ANY_SPEC, ANY_SPEC],
        [SLAB_QKV, SLAB_OUT, pltpu.SemaphoreType.DMA, SLAB_T, pltpu.VMEM((A_SLOTS_OUT, NC, HB, BQ), BF16),
         pltpu.SemaphoreType.DMA((A_SLOTS_OUT,))], {}, job)
    return mixed, amat, extra


def _sb_bwd(qkv, dmixed, amat, job=None):
    def body(ins, outs, scr, comm):
        (qkv_hbm, do_hbm, a_hbm), (dqkv_hbm,), (qkv_v, do_v, dq_v, dk_s, dv_s, sems, kt_v, a_st, a_sems, w_v) = ins, outs, scr
        sem = sems.at[0]
        w_v[...] = jnp.zeros_like(w_v)

        def a_copy(t):
            slot = t % A_SLOTS_IN
            return pltpu.make_async_copy(a_hbm.at[t], a_st.at[slot], a_sems.at[slot])

        later = [pltpu.make_async_copy(do_hbm.at[:, pl.ds(0, 2 * BQ)], do_v, sems.at[1])]
        for cp in later:
            cp.start()
        for t in range(A_SLOTS_IN - 1):
            a_copy(t).start()
        _copy_in(qkv_hbm.at[:, pl.ds(0, 2 * PAIRW)], qkv_v, sem)
        st = comm[0]() if comm else None
        _transpose_slab(qkv_v, kt_v, lambda p: p * PAIRW + BQ)
        for cp in later:
            cp.wait()
        m0, m1 = _lane_masks()
        first = _iota((BQ, BQ), 1) < 64
        r0, r1 = _row_masks()
        valid = _valid_t(True)
        earlier = _tri_blockdiag(False)
        dk_s[...] = jnp.zeros_like(dk_s)
        dv_s[...] = jnp.zeros_like(dv_s)

        def steps(i, j, cs, diag):
            t = (i * (i + 1)) // 2 + j
            a_copy(t).wait()

            @pl.when(t + A_SLOTS_IN - 1 < NTRI)
            def _():
                a_copy(t + A_SLOTS_IN - 1).start()
            slot = t % A_SLOTS_IN
            kv = [_kv_tiles(qkv_v, p, b, j) for p, b in CHAINS]
            zd = [_dot(jnp.concatenate([_stack(kv[c][0], m0, m1), _stack(kv[c][1], m0, m1)], axis=1), w_v[c, 0])
                  for c in range(NC)]
            zs = [x[:, :BQ] for x in zd]
            das = [x[:, BQ:] for x in zd]
            avs = [a_st[slot, c] for c in range(NC)]
            gms = [das[c] * avs[c].astype(F32) for c in range(NC)]
            befores = [_dot(earlier, gms[c].astype(BF16)) for c in range(NC)]
            dzbs = []
            for c in range(NC):
                dz = gms[c] - jax.nn.sigmoid(zs[c]) * (gms[c] + befores[c] + _bcast_heads(*cs[c][0]))
                if diag:
                    dz = jnp.where(valid, dz, 0.0)
                dzbs.append(dz.astype(BF16))
            out = []
            for c, (p, b) in enumerate(CHAINS):
                dq = cs[c][1] + _dot(_stack_t(kt_v[p, :, _rows(b, j)], r0, r1), dzbs[c])
                kd = _dot(jnp.concatenate([dzbs[c], avs[c]], axis=1), w_v[c, 1])
                dk_s[p, _rows(b, j), :] += _merge_heads(kd[:, :BQ], first)
                dv_s[p, _rows(b, j), :] += _merge_heads(kd[:, BQ:], first)
                g0, g1 = _colsum2(gms[c])
                out.append(((cs[c][0][0] + g0, cs[c][0][1] + g1), dq))
            return tuple(out)

        def qblock(i, _):
            for c, (p, b) in enumerate(CHAINS):
                qn = _q_tile(qkv_v, p, b, i)
                dn = do_v[_rows(b, i), p * BQ:(p + 1) * BQ]
                for r, (x, y) in enumerate(((_tr(qn), _tr(dn)), (qn, dn))):
                    w_v[c, r, :BQ, :BQ] = x
                    w_v[c, r, BQ:, BQ:] = y
            zr = jnp.zeros((1, BQ), F32)
            cs = (((zr, zr), jnp.zeros((BQ, BQ), F32)),) * NC
            cs = lax.fori_loop(0, i // 2, lambda n, cs: steps(i, 2 * n + 1, steps(i, 2 * n, cs, False), False), cs)
            cs = lax.fori_loop(i - i % 2, i, lambda j, cs: steps(i, j, cs, False), cs)
            cs = steps(i, i, cs, True)
            for c, (p, b) in enumerate(CHAINS):
                dq_v[_rows(b, i), p * PAIRW:p * PAIRW + BQ] = (cs[c][1].T * SCALE).astype(BF16)
            return 0

        lax.fori_loop(0, NB, qblock, 0)
        for p in range(2):
            dq_v[:, p * PAIRW + BQ:p * PAIRW + 2 * BQ] = dk_s[p].astype(BF16)
            dq_v[:, p * PAIRW + 2 * BQ:p * PAIRW + 3 * BQ] = dv_s[p].astype(BF16)
        _copy_in(dq_v, dqkv_hbm.at[:, pl.ds(0, 2 * PAIRW)], sem)
        if comm:
            comm[1](st)

    (dqkv,), extra = _host_call(
        body, "sb_bwd", [qkv, dmixed, amat], [ANY_SPEC, ANY_SPEC, ANY_SPEC],
        [jax.ShapeDtypeStruct((T, QKVW), BF16)], [ANY_SPEC],
        [SLAB_QKV, SLAB_OUT, SLAB_QKV, ACC_KV, ACC_KV, pltpu.SemaphoreType.DMA((4,)), SLAB_T,
         pltpu.VMEM((A_SLOTS_IN, NC, HB, BQ), BF16), pltpu.SemaphoreType.DMA((A_SLOTS_IN,)), PAIR_DIAG], {}, job)
    return dqkv, extra


def _flash_fwd(qkv, mixed, g, fox, bias, job=None):
    def body(ins, outs, scr, comm):
        if fox:
            qkv_hbm, cq_ref, ckb_hbm, _ = ins
            (o_hbm, lse_ref, o32_hbm), (qkv_v, o_v, sem, vt_v, o32_v, ckb_v) = outs, scr
        else:
            qkv_hbm, tbl_ref, _ = ins
            (o_hbm, lse_ref), (qkv_v, o_v, sem, vt_v) = outs, scr
        sems = sem
        sem = sems.at[0]
        later = [pltpu.make_async_copy(ckb_hbm, ckb_v, sems.at[1])] if fox else []
        for cp in later:
            cp.start()
        _copy_in(qkv_hbm.at[:, pl.ds(g * 2 * PAIRW, 2 * PAIRW)], qkv_v, sem)
        st = comm[0]() if comm else None
        _transpose_slab(qkv_v, vt_v, lambda p: p * PAIRW + 2 * BQ)
        for cp in later:
            cp.wait()
        m0, m1 = _lane_masks()
        r0, r1 = _row_masks()
        valid = _valid_t(False)

        def steps(qts, cqs, i, j, cs, diag):
            ks = [_stack(_kv_tiles(qkv_v, p, b, j)[0], m0, m1) for p, b in CHAINS]
            zs = [_dot(ks[c], qts[c]) for c in range(NC)]
            prs, alphas, out = [], [], []
            for c, (p, b) in enumerate(CHAINS):
                (ma, mb), (la, lb_), _ = cs[c]
                if fox:
                    kk = pl.ds(pl.multiple_of(j * BQ, BQ), BQ)
                    col = b * NH + 2 * p
                    z = zs[c] + (cqs[c] - jnp.concatenate([ckb_v[col, kk, :], ckb_v[col + 1, kk, :]], axis=0))
                    if diag:
                        z = jnp.where(valid, z, NEG)
                else:
                    z = zs[c] + tbl_ref[p, i - j]
                za, zb = _heads(z)
                na = jnp.maximum(ma, jnp.max(za, axis=0, keepdims=True))
                nb = jnp.maximum(mb, jnp.max(zb, axis=0, keepdims=True))
                aa, ab = jnp.exp(ma - na), jnp.exp(mb - nb)
                pr = jnp.exp(z - _bcast_heads(na, nb))
                sa, sb = _colsum2(pr)
                prs.append(_split2(pr) if fox else pr.astype(BF16))
                alphas.append((aa, ab))
                out.append(((na, nb), (aa * la + sa, ab * lb_ + sb)))
            pvs = []
            for c, (p, b) in enumerate(CHAINS):
                vts = _stack_t(vt_v[p, :, _rows(b, j)], r0, r1)
                if fox:
                    pvs.append(_dot(vts, prs[c][:, :BQ]) + _dot(vts, prs[c][:, BQ:]))
                else:
                    pvs.append(_dot(vts, prs[c]))
            return tuple((out[c][0], out[c][1], _by_channel(*alphas[c]) * cs[c][2] + pvs[c]) for c in range(NC))

        def qblock(i, _):
            qts = [_tr(_q_tile(qkv_v, p, b, i)) for p, b in CHAINS]
            if fox:
                cqs = [_bcast_heads(_stat_row(cq_ref, p, b, 0, i), _stat_row(cq_ref, p, b, 1, i)) for p, b in CHAINS]
            else:
                cqs = [None] * NC
            ng = jnp.full((1, BQ), NEG, F32)
            zr = jnp.zeros((1, BQ), F32)
            cs = steps(qts, cqs, i, i, (((ng, ng), (zr, zr), jnp.zeros((BQ, BQ), F32)),) * NC, True)
            cs = lax.fori_loop(0, i // 2, lambda n, cs: steps(qts, cqs, i, i - 2 * n - 2,
                                                              steps(qts, cqs, i, i - 2 * n - 1, cs, False), False), cs)
            cs = lax.fori_loop(i - i % 2 + 1, i + 1, lambda jj, cs: steps(qts, cqs, i, i - jj, cs, False), cs)
            for c, (p, b) in enumerate(CHAINS):
                (ma, mb), (la, lb_), acc = cs[c]
                o = (acc / _by_channel(la, lb_)).T
                o_v[_rows(b, i), p * BQ:(p + 1) * BQ] = o.astype(BF16)
                if fox:
                    o32_v[_rows(b, i), p * BQ:(p + 1) * BQ] = o
                _put_row(lse_ref, p, b, 0, i, ma + jnp.log(la))
                _put_row(lse_ref, p, b, 1, i, mb + jnp.log(lb_))
            return 0

        lax.fori_loop(0, NB, qblock, 0)
        _copy_in(o_v, o_hbm.at[:, pl.ds(g * 2 * BQ, 2 * BQ)], sem)
        if fox:
            _copy_in(o32_v, o32_hbm, sem)
        if comm:
            comm[1](st)

    bias_specs = [VMEM_SPEC, ANY_SPEC] if fox else [VMEM_SPEC]
    n_in = 2 + len(bias_specs)
    o32 = [jax.ShapeDtypeStruct((T, 2 * BQ), F32)] if fox else []
    res, extra = _host_call(
        body, "fox_fwd" if fox else "dil_fwd", [qkv, *bias, mixed], [ANY_SPEC] + bias_specs + [ANY_SPEC],
        [jax.ShapeDtypeStruct((T, D), BF16), ROWS_SHAPE] + o32, [ANY_SPEC, VMEM_SPEC] + [ANY_SPEC] * len(o32),
        [SLAB_QKV, SLAB_OUT, pltpu.SemaphoreType.DMA((4,)), SLAB_T] + ([SLAB_O32, SLAB_KEYB] if fox else []),
        {n_in - 1: 0}, job)
    return (*res, extra)


def _flash_bwd(qkv, o, dmixed, lse, dqkv, g, fox, bias, job=None):
    def body(ins, outs, scr, comm):
        if fox:
            qkv_hbm, o_hbm, do_hbm, lse_ref, cq_ref, ckb_hbm, _ = ins
            (dqkv_hbm, db_ref), (qkv_v, o_v, do_v, dq_v, dk_s, dv_s, sem, kt_v, w_v, ckb_v, dc_s) = outs, scr
        else:
            qkv_hbm, o_hbm, do_hbm, lse_ref, tbl_ref, _ = ins
            (dqkv_hbm, db_ref), (qkv_v, o_v, do_v, dq_v, dk_s, dv_s, sem, kt_v, w_v) = outs, scr
        w_v[...] = jnp.zeros_like(w_v)
        sems = sem
        sem = sems.at[0]
        later = [pltpu.make_async_copy(do_hbm.at[:, pl.ds(g * 2 * BQ, 2 * BQ)], do_v, sems.at[1])]
        if fox:
            later += [pltpu.make_async_copy(o_hbm, o_v, sems.at[2]), pltpu.make_async_copy(ckb_hbm, ckb_v, sems.at[3])]
        else:
            later += [pltpu.make_async_copy(o_hbm.at[:, pl.ds(g * 2 * BQ, 2 * BQ)], o_v, sems.at[2])]
        for cp in later:
            cp.start()
        _copy_in(qkv_hbm.at[:, pl.ds(g * 2 * PAIRW, 2 * PAIRW)], qkv_v, sem)
        st = comm[0]() if comm else None
        _transpose_slab(qkv_v, kt_v, lambda p: p * PAIRW + BQ)
        for cp in later:
            cp.wait()
        m0, m1 = _lane_masks()
        first = _iota((BQ, BQ), 1) < 64
        r0, r1 = _row_masks()
        valid = _valid_t(False)
        dk_s[...] = jnp.zeros_like(dk_s)
        dv_s[...] = jnp.zeros_like(dv_s)
        if fox:
            dc_s[...] = jnp.zeros_like(dc_s)
        else:
            db_ref[...] = jnp.zeros_like(db_ref)

        def steps(cqs, lses, deltas, i, j, dqs, diag):
            kv = [_kv_tiles(qkv_v, p, b, j) for p, b in CHAINS]
            zd = [_dot(jnp.concatenate([_stack(kv[c][0], m0, m1), _stack(kv[c][1], m0, m1)], axis=1), w_v[c, 0])
                  for c in range(NC)]
            zs = [x[:, :BQ] for x in zd]
            dps = [x[:, BQ:] for x in zd]
            prs, dzl = [], []
            for c, (p, b) in enumerate(CHAINS):
                if fox:
                    kk = pl.ds(pl.multiple_of(j * BQ, BQ), BQ)
                    col = b * NH + 2 * p
                    z = zs[c] + (cqs[c] - jnp.concatenate([ckb_v[col, kk, :], ckb_v[col + 1, kk, :]], axis=0))
                    if diag:
                        z = jnp.where(valid, z, NEG)
                else:
                    z = zs[c] + tbl_ref[p, i - j]
                pr = jnp.exp(z - lses[c])
                prs.append(pr.astype(BF16))
                dzl.append(pr * (dps[c] - deltas[c]))
            dzbs = [dz.astype(BF16) for dz in dzl]
            new = []
            for c, (p, b) in enumerate(CHAINS):
                new.append(dqs[c] + _dot(_stack_t(kt_v[p, :, _rows(b, j)], r0, r1), dzbs[c]))
                kd = _dot(jnp.concatenate([dzbs[c], prs[c]], axis=1), w_v[c, 1])
                dk_s[p, _rows(b, j), :] += _merge_heads(kd[:, :BQ], first)
                dv_s[p, _rows(b, j), :] += _merge_heads(kd[:, BQ:], first)
                if fox:
                    dc_s[c, pl.ds(pl.multiple_of(j * HB, HB), HB), :] += dzl[c]
            if not fox:
                for p in range(2):
                    db_ref[p, i - j] = db_ref[p, i - j] + (dzl[2 * p] + dzl[2 * p + 1])
            return tuple(new)

        def qblock(i, _):
            qns = [_q_tile(qkv_v, p, b, i) for p, b in CHAINS]
            dns = [do_v[_rows(b, i), p * BQ:(p + 1) * BQ] for p, b in CHAINS]
            for c in range(NC):
                for r, (x, y) in enumerate(((_tr(qns[c]), _tr(dns[c])), (qns[c], dns[c]))):
                    w_v[c, r, :BQ, :BQ] = x
                    w_v[c, r, BQ:, BQ:] = y
            lses = [_bcast_heads(_stat_row(lse_ref, p, b, 0, i), _stat_row(lse_ref, p, b, 1, i)) for p, b in CHAINS]
            if fox:
                cqs = [_bcast_heads(_stat_row(cq_ref, p, b, 0, i), _stat_row(cq_ref, p, b, 1, i)) for p, b in CHAINS]
            else:
                cqs = [None] * NC
            deltas = []
            for c, (p, b) in enumerate(CHAINS):
                pt = (dns[c].astype(F32) * o_v[_rows(b, i), p * BQ:(p + 1) * BQ].astype(F32)).T
                deltas.append(_bcast_heads(jnp.sum(pt[:64], axis=0, keepdims=True), jnp.sum(pt[64:], axis=0, keepdims=True)))
            dqs = (jnp.zeros((BQ, BQ), F32),) * NC
            dqs = lax.fori_loop(0, i // 2, lambda n, d: steps(cqs, lses, deltas, i, 2 * n + 1,
                                                              steps(cqs, lses, deltas, i, 2 * n, d, False), False), dqs)
            dqs = lax.fori_loop(i - i % 2, i, lambda j, d: steps(cqs, lses, deltas, i, j, d, False), dqs)
            dqs = steps(cqs, lses, deltas, i, i, dqs, True)
            for c, (p, b) in enumerate(CHAINS):
                dq_v[_rows(b, i), p * PAIRW:p * PAIRW + BQ] = (dqs[c].T * SCALE).astype(BF16)
            return 0

        lax.fori_loop(0, NB, qblock, 0)
        for p in range(2):
            dq_v[:, p * PAIRW + BQ:p * PAIRW + 2 * BQ] = dk_s[p].astype(BF16)
            dq_v[:, p * PAIRW + 2 * BQ:p * PAIRW + 3 * BQ] = dv_s[p].astype(BF16)
        _copy_in(dq_v, dqkv_hbm.at[:, pl.ds(g * 2 * PAIRW, 2 * PAIRW)], sem)
        if fox:
            lane = _iota((BQ, NSTAT), 1)

            def fold(n, _):
                t = jnp.zeros((BQ, NSTAT), F32)
                for c, (p, b) in enumerate(CHAINS):
                    s = jnp.sum(dc_s[c, pl.ds(pl.multiple_of(n * HB, HB), HB), :], axis=1, keepdims=True)
                    col = b * NH + 2 * p
                    t = t - jnp.where(lane == col, s[:BQ], 0.0) - jnp.where(lane == col + 1, s[BQ:], 0.0)
                db_ref[pl.ds(pl.multiple_of(n * BQ, BQ), BQ), :] = t
                return 0

            lax.fori_loop(0, NB, fold, 0)
        if comm:
            comm[1](st)

    if fox:
        bias_specs = [VMEM_SPEC, ANY_SPEC]
        db_shape = jax.ShapeDtypeStruct((S, NSTAT), F32)
        more = [SLAB_KEYB, pltpu.VMEM((NC, NB * HB, BQ), F32)]
    else:
        bias_specs = [VMEM_SPEC]
        db_shape = jax.ShapeDtypeStruct((2, NB, HB, BQ), F32)
        more = []
    n_in = 5 + len(bias_specs)
    (dqkv, db), extra = _host_call(
        body, "fox_bwd" if fox else "dil_bwd", [qkv, o, dmixed, lse, *bias, dqkv],
        [ANY_SPEC, ANY_SPEC, ANY_SPEC, VMEM_SPEC] + bias_specs + [ANY_SPEC],
        [jax.ShapeDtypeStruct((T, QKVW), BF16), db_shape], [ANY_SPEC, VMEM_SPEC],
        [SLAB_QKV, SLAB_O32 if fox else SLAB_OUT, SLAB_OUT, SLAB_QKV, ACC_KV, ACC_KV, pltpu.SemaphoreType.DMA((4,)), SLAB_T,
         PAIR_DIAG] + more, {n_in - 1: 0}, job)
    return dqkv, db, extra


def _delta_t(d):
    return d * BQ + _iota((HB, BQ), 1) - (_iota((HB, BQ), 0) & (BQ - 1))


def _buckets_in(d):
    lo, hi = max(d * BQ - (BQ - 1), 0), d * BQ + BQ - 1
    return [b for b in range(32) if BUCKET_TH[b] <= hi and (b == 31 or BUCKET_TH[b + 1] > lo)]


def _in_bucket(delta, b):
    m = delta >= BUCKET_TH[b]
    return m if b == 31 else m & (delta < BUCKET_TH[b + 1])


def _dil_table(rel_bias, job=None):
    def body(ins, outs, scr, comm):
        (rb_ref,), (o_ref,) = ins, outs
        st = comm[0]() if comm else None
        for d in range(NB):
            delta = _delta_t(d)
            pos = delta >= 0
            n = ((pos & (delta <= 128)).astype(jnp.int32)
                 + (pos & (delta <= 512) & ((delta & 3) == 0)).astype(jnp.int32)
                 + (pos & ((delta & 15) == 0)).astype(jnp.int32))
            logn = jnp.where(n == 3, math.log(3.0), jnp.where(n == 2, math.log(2.0), jnp.where(n == 1, 0.0, NEG)))
            head1 = _iota((HB, BQ), 0) >= BQ
            for p in range(2):
                val = jnp.zeros((HB, BQ), F32)
                for b in _buckets_in(d):
                    val = jnp.where(_in_bucket(delta, b), jnp.where(head1, rb_ref[b, 2 * p + 1], rb_ref[b, 2 * p]), val)
                o_ref[p, d] = val + logn
        if comm:
            comm[1](st)

    (tbl,), extra = _host_call(
        body, "dil_table", [rel_bias], [pl.BlockSpec(memory_space=pltpu.SMEM)],
        [jax.ShapeDtypeStruct((2, NB, HB, BQ), F32)], [VMEM_SPEC], [], {}, job)
    return (tbl, extra) if job else tbl


def _dil_table_bwd(dtbl):
    def body(dt_ref, o_ref):
        p = pl.program_id(0)
        rowi = _iota((32, BQ), 0)
        lanei = _iota((32, BQ), 1)

        @pl.when(p == 0)
        def _():
            o_ref[...] = jnp.zeros_like(o_ref)

        out = jnp.zeros((32, BQ), F32)
        for b in range(32):
            acc = None
            for d in range(NB):
                if b in _buckets_in(d):
                    t = jnp.where(_in_bucket(_delta_t(d), b), dt_ref[d], 0.0)
                    acc = t if acc is None else acc + t
            rs = jnp.sum(acc, axis=1, keepdims=True)
            s0 = jnp.sum(rs[:BQ], axis=0, keepdims=True)
            s1 = jnp.sum(rs[BQ:], axis=0, keepdims=True)
            out = (out + jnp.where((rowi == b) & (lanei == 2 * p), s0, 0.0)
                   + jnp.where((rowi == b) & (lanei == 2 * p + 1), s1, 0.0))
        o_ref[...] += out

    return pl.pallas_call(
        body, name="dil_table_bwd", grid=(2,),
        in_specs=[pl.BlockSpec((None, NB, HB, BQ), lambda p: (p, 0, 0, 0))],
        out_specs=pl.BlockSpec((32, BQ), lambda p: (0, 0)),
        out_shape=jax.ShapeDtypeStruct((32, BQ), F32),
        compiler_params=_cp(("arbitrary",)))(dtbl)


def _fox_prep(gate, fb):
    def body(g_ref, fb_ref, c_ref):
        tri = (_iota((BQ, BQ), 0) >= _iota((BQ, BQ), 1)).astype(BF16)

        def blk(i, carry):
            r0 = pl.multiple_of(i * BQ, BQ)
            lf = _log_sigmoid(g_ref[pl.ds(r0, BQ), :] + fb_ref[...])
            c = _dot(tri, _split3(lf))
            c_ref[pl.ds(r0, BQ), :] = c[:, 0:BQ] + c[:, BQ:2 * BQ] + c[:, 2 * BQ:3 * BQ] + carry
            return carry + jnp.sum(lf, axis=0, keepdims=True)

        lax.fori_loop(0, NB, blk, jnp.zeros((1, BQ), F32))

    blk = pl.BlockSpec((S, GATEW), lambda b: (b, 0))
    return pl.pallas_call(
        body, name="fox_prep", grid=(BL,), in_specs=[blk, pl.BlockSpec((1, GATEW), lambda b: (0, 0))],
        out_specs=blk, out_shape=jax.ShapeDtypeStruct((T, GATEW), F32),
        compiler_params=_cp(("parallel",)))(gate, fb)


def _fox_post(dcum, gate, fb):
    def body(dc_ref, g_ref, fb_ref, dg_ref, dfb_ref):
        b = pl.program_id(0)
        tri = (_iota((BQ, BQ), 0) <= _iota((BQ, BQ), 1)).astype(BF16)

        def blk(ii, carry):
            csum, dfb = carry
            r0 = pl.multiple_of((NB - 1 - ii) * BQ, BQ)
            dc = dc_ref[pl.ds(r0, BQ), :]
            c = _dot(tri, _split3(dc))
            dlf = c[:, 0:BQ] + c[:, BQ:2 * BQ] + c[:, 2 * BQ:3 * BQ] + csum
            dg = dlf * jnp.exp(_log_sigmoid(-(g_ref[pl.ds(r0, BQ), :] + fb_ref[...])))
            dg_ref[pl.ds(r0, BQ), :] = dg
            return csum + jnp.sum(dc, axis=0, keepdims=True), dfb + jnp.sum(dg, axis=0, keepdims=True)

        z = jnp.zeros((1, BQ), F32)
        _, dfb = lax.fori_loop(0, NB, blk, (z, z))

        @pl.when(b == 0)
        def _():
            dfb_ref[...] = dfb

        @pl.when(b > 0)
        def _():
            dfb_ref[...] += dfb

    blk = pl.BlockSpec((S, GATEW), lambda b: (b, 0))
    vec = pl.BlockSpec((1, GATEW), lambda b: (0, 0))
    return pl.pallas_call(
        body, name="fox_post", grid=(BL,), in_specs=[blk, blk, vec], out_specs=[blk, vec],
        out_shape=[jax.ShapeDtypeStruct((T, GATEW), F32), jax.ShapeDtypeStruct((1, GATEW), F32)],
        compiler_params=_cp(("arbitrary",)))(dcum, gate, fb)


def _shift_down(x, n):
    return jnp.where(_iota(x.shape, 0) >= n, pltpu.roll(x, n, 0), 0.0)


def _shift_up(x, n):
    return jnp.where(_iota(x.shape, 0) < S - n, pltpu.roll(x, S - n, 0), 0.0)


def _conv_fwd(conv, cw, mixed):
    W = 256

    def body(c_ref, w_ref, _, o_ref):
        u = c_ref[:, W:2 * W] * c_ref[:, 2 * W:3 * W]
        y = w_ref[0:1, :] * _shift_down(u, 2) + w_ref[1:2, :] * _shift_down(u, 1) + w_ref[2:3, :] * u
        o_ref[...] = (c_ref[:, 0:W] * y).astype(BF16)

    return pl.pallas_call(
        body, name="conv_fwd", grid=(BL,),
        in_specs=[pl.BlockSpec((S, CONVW), lambda b: (b, 0)), pl.BlockSpec((8, W), lambda b: (0, 0)), ANY_SPEC],
        out_specs=pl.BlockSpec((S, W), lambda b: (b, 3)),
        out_shape=jax.ShapeDtypeStruct((T, D), BF16), input_output_aliases={2: 0},
        compiler_params=_cp(("parallel",)))(conv, cw, mixed)


def _conv_bwd(conv, cw, dmixed):
    W = 256

    def body(c_ref, w_ref, do_ref, dc_ref, dw_ref):
        b = pl.program_id(0)
        bg = c_ref[:, 0:W]
        cg = c_ref[:, W:2 * W]
        hv = c_ref[:, 2 * W:3 * W]
        do = do_ref[...].astype(F32)
        u = cg * hv
        u1 = _shift_down(u, 1)
        u2 = _shift_down(u, 2)
        y = w_ref[0:1, :] * u2 + w_ref[1:2, :] * u1 + w_ref[2:3, :] * u
        dy = do * bg
        du = w_ref[2:3, :] * dy + w_ref[1:2, :] * _shift_up(dy, 1) + w_ref[0:1, :] * _shift_up(dy, 2)
        dc_ref[:, 0:W] = (do * y).astype(BF16)
        dc_ref[:, W:2 * W] = (du * hv).astype(BF16)
        dc_ref[:, 2 * W:3 * W] = (du * cg).astype(BF16)
        rowi = _iota((8, W), 0)
        dw = (jnp.where(rowi == 0, jnp.sum(dy * u2, axis=0, keepdims=True), 0.0)
              + jnp.where(rowi == 1, jnp.sum(dy * u1, axis=0, keepdims=True), 0.0)
              + jnp.where(rowi == 2, jnp.sum(dy * u, axis=0, keepdims=True), 0.0))

        @pl.when(b == 0)
        def _():
            dw_ref[...] = dw

        @pl.when(b > 0)
        def _():
            dw_ref[...] += dw

    return pl.pallas_call(
        body, name="conv_bwd", grid=(BL,),
        in_specs=[pl.BlockSpec((S, CONVW), lambda b: (b, 0)), pl.BlockSpec((8, W), lambda b: (0, 0)),
                  pl.BlockSpec((S, W), lambda b: (b, 3))],
        out_specs=[pl.BlockSpec((S, CONVW), lambda b: (b, 0)), pl.BlockSpec((8, W), lambda b: (0, 0))],
        out_shape=[jax.ShapeDtypeStruct((T, CONVW), BF16), jax.ShapeDtypeStruct((8, W), F32)],
        compiler_params=_cp(("arbitrary",)))(conv, cw, dmixed)


def _place():
    x, y, c = lax.axis_index("x"), lax.axis_index("y"), lax.axis_index("c")
    return x, y, c


def _chips_of(x, y):
    return [(1 - x, y), (x, 1 - y), (1 - x, 1 - y)]


def _dev(p):
    return 4 * p[0] + 2 * p[1] + p[2]


def _gather_job_a(shards):
    n = len(shards)

    def peers(x, y, c):
        return [(x, y, 1 - c)] + [(*chip, c) for chip in _chips_of(x, y)]

    def start(ins, outs, sems):
        send, recv, loc = sems
        x, y, c = _place()
        me = (x, y, c)
        cps = []
        for a in range(n):
            cps.append(pltpu.make_async_copy(ins[a], outs[a].at[_dev(me)], loc.at[a]))
            for k, peer in enumerate(peers(x, y, c)):
                cps.append(pltpu.make_async_remote_copy(
                    src_ref=ins[a], dst_ref=outs[a].at[_dev(me)], send_sem=send.at[a, k], recv_sem=recv.at[a, k],
                    device_id=peer, device_id_type=MESH))
        for cp in cps:
            cp.start()
        return cps

    def finish(cps, ins, outs, sems):
        send, recv, loc = sems
        x, y, c = _place()
        for a in range(n):
            for k, peer in enumerate(peers(x, y, c)):
                pltpu.make_async_remote_copy(
                    src_ref=ins[a], dst_ref=outs[a].at[_dev(peer)], send_sem=send.at[a, k], recv_sem=recv.at[a, k],
                    device_id=(x, y, c), device_id_type=MESH).wait_recv()
        for a in range(n):
            cps[5 * a].wait()
            for k in range(4):
                cps[5 * a + 1 + k].wait_send()

    return _Job(shards, [jax.ShapeDtypeStruct((NDEV,) + s.shape, s.dtype) for s in shards], {},
                [pltpu.SemaphoreType.DMA((n, 4)), pltpu.SemaphoreType.DMA((n, 4)), pltpu.SemaphoreType.DMA((n,))],
                start, finish)


def _gather_job_b(gathered):
    n = len(gathered)

    def start(ins, outs, sems):
        send, recv = sems
        x, y, c = _place()
        cps = []
        for a in range(n):
            for j, chip in enumerate(_chips_of(x, y)):
                blk = outs[a].at[_dev((*chip, c))]
                cps.append(pltpu.make_async_remote_copy(
                    src_ref=blk, dst_ref=blk, send_sem=send.at[a, j], recv_sem=recv.at[a, j],
                    device_id=(x, y, 1 - c), device_id_type=MESH))
        for cp in cps:
            cp.start()
        return cps

    def finish(cps, ins, outs, sems):
        send, recv = sems
        x, y, c = _place()
        for a in range(n):
            for j, chip in enumerate(_chips_of(x, y)):
                blk = outs[a].at[_dev((*chip, 1 - c))]
                pltpu.make_async_remote_copy(
                    src_ref=blk, dst_ref=blk, send_sem=send.at[a, j], recv_sem=recv.at[a, j],
                    device_id=(x, y, c), device_id_type=MESH).wait_recv()
        for cp in cps:
            cp.wait_send()

    return _Job(gathered, [jax.ShapeDtypeStruct(g.shape, g.dtype) for g in gathered], {a: a for a in range(n)},
                [pltpu.SemaphoreType.DMA((n, 3)), pltpu.SemaphoreType.DMA((n, 3))], start, finish)


def _sibling_job(grads):
    n = len(grads)

    def start(ins, outs, sems):
        send, recv = sems
        x, y, c = _place()
        cps = [pltpu.make_async_remote_copy(
            src_ref=ins[a].at[:, 1 - c], dst_ref=outs[a], send_sem=send.at[a], recv_sem=recv.at[a],
            device_id=(x, y, 1 - c), device_id_type=MESH) for a in range(n)]
        for cp in cps:
            cp.start()
        return cps

    def finish(cps, ins, outs, sems):
        for cp in cps:
            cp.wait()

    return _Job(grads, [jax.ShapeDtypeStruct(g.shape[:1] + g.shape[2:], F32) for g in grads], {},
                [pltpu.SemaphoreType.DMA((n,)), pltpu.SemaphoreType.DMA((n,))], start, finish)


def _chip_job(psums):
    n = len(psums)

    def copies(ins, outs, sems):
        send, recv, loc = sems
        x, y, c = _place()
        mychip = 2 * x + y
        cps = []
        for a in range(n):
            cps.append(pltpu.make_async_copy(ins[a].at[mychip], outs[a].at[mychip], loc.at[a]))
            for j, chip in enumerate(_chips_of(x, y)):
                cps.append(pltpu.make_async_remote_copy(
                    src_ref=ins[a].at[2 * chip[0] + chip[1]], dst_ref=outs[a].at[mychip],
                    send_sem=send.at[a, j], recv_sem=recv.at[a, j], device_id=(*chip, c), device_id_type=MESH))
        return cps

    def start(ins, outs, sems):
        for cp in copies(ins, outs, sems):
            cp.start()

    def finish(_, ins, outs, sems):
        cps = copies(ins, outs, sems)
        send, recv, loc = sems
        x, y, c = _place()
        mychip = 2 * x + y
        for a in range(n):
            for j, chip in enumerate(_chips_of(x, y)):
                pltpu.make_async_remote_copy(
                    src_ref=ins[a].at[mychip], dst_ref=outs[a].at[2 * chip[0] + chip[1]],
                    send_sem=send.at[a, j], recv_sem=recv.at[a, j], device_id=(x, y, c), device_id_type=MESH).wait_recv()
        for a in range(n):
            cps[4 * a].wait()
            for j in range(3):
                cps[4 * a + 1 + j].wait_send()

    return _Job(psums, [jax.ShapeDtypeStruct(p.shape, BF16) for p in psums], {},
                [pltpu.SemaphoreType.DMA((n, 3)), pltpu.SemaphoreType.DMA((n, 3)), pltpu.SemaphoreType.DMA((n,))],
                start, finish)


def _join_jobs(*jobs):
    jobs = [j for j in jobs if j is not None]
    if len(jobs) <= 1:
        return jobs[0] if jobs else None
    cut = lambda seq, sizes: [seq[sum(sizes[:k]):sum(sizes[:k + 1])] for k in range(len(sizes))]
    n_in = [len(j.ins) for j in jobs]
    n_out = [len(j.out_shapes) for j in jobs]
    n_sem = [len(j.sems) for j in jobs]
    aliases = {}
    for k, j in enumerate(jobs):
        for a, b in j.aliases.items():
            aliases[sum(n_in[:k]) + a] = sum(n_out[:k]) + b

    def start(ins, outs, sems):
        return [j.start(i, o, s) for j, i, o, s in zip(jobs, cut(ins, n_in), cut(outs, n_out), cut(sems, n_sem))]

    def finish(sts, ins, outs, sems):
        for j, st, i, o, s in zip(jobs, sts, cut(ins, n_in), cut(outs, n_out), cut(sems, n_sem)):
            j.finish(st, i, o, s)

    return _Job([t for j in jobs for t in j.ins], [t for j in jobs for t in j.out_shapes], aliases,
                [t for j in jobs for t in j.sems], start, finish)


def _run_job(job, name):
    def body(ins, outs, scr, comm):
        comm[1](comm[0]())

    return _host_call(body, name, [], [], [], [], [], {}, job)[1]


def _allreduce_small(v, job=None):
    def body(ins, outs, scr, comm):
        (v_ref,), (o_ref,), (slots, send_sems, recv_sems) = ins, outs, scr
        st = comm[0]() if comm else None
        x, y, c = _place()
        me = 4 * x + 2 * y + c
        slots[me] = v_ref[...]

        def copy(k):
            peer = (x ^ ((k >> 2) & 1), y ^ ((k >> 1) & 1), c ^ (k & 1))
            return pltpu.make_async_remote_copy(
                src_ref=v_ref, dst_ref=slots.at[me], send_sem=send_sems.at[k - 1], recv_sem=recv_sems.at[k - 1],
                device_id=peer, device_id_type=MESH)

        def arrival(k):
            return pltpu.make_async_remote_copy(
                src_ref=v_ref, dst_ref=slots.at[me ^ k], send_sem=send_sems.at[k - 1], recv_sem=recv_sems.at[k - 1],
                device_id=(x, y, c), device_id_type=MESH)

        sends = [copy(k) for k in range(1, NDEV)]
        for cp in sends:
            cp.start()
        for k in range(1, NDEV):
            arrival(k).wait_recv()
        for cp in sends:
            cp.wait_send()
        acc = slots[0]
        for d in range(1, NDEV):
            acc = acc + slots[d]
        o_ref[...] = acc
        if comm:
            comm[1](st)

    (out,), extra = _host_call(
        body, "allreduce_small", [v], [VMEM_SPEC], [jax.ShapeDtypeStruct(v.shape, F32)], [VMEM_SPEC],
        [pltpu.VMEM((NDEV,) + v.shape, F32), pltpu.SemaphoreType.DMA((NDEV - 1,)),
         pltpu.SemaphoreType.DMA((NDEV - 1,))], {}, job)
    return (out, extra) if job else out


def _pair_sums(views, gots, core):
    n = len(views)

    def body(c_ref, *refs):
        for a in range(n):
            refs[2 * n + a][...] = (refs[a][...] + refs[n + a][...]).astype(BF16)

    def vspec(v):
        return pl.BlockSpec((None, None, v.shape[2] // 2, v.shape[3]), lambda k, h, c: (k, c[0], h, 0))

    def gspec(g):
        return pl.BlockSpec((None, g.shape[1] // 2, g.shape[2]), lambda k, h, c: (k, h, 0))

    return pl.pallas_call(
        body, name="pair_sums",
        grid_spec=pltpu.PrefetchScalarGridSpec(
            num_scalar_prefetch=1, grid=(4, 2),
            in_specs=[vspec(v) for v in views] + [gspec(g) for g in gots],
            out_specs=[gspec(g) for g in gots]),
        out_shape=[jax.ShapeDtypeStruct(g.shape, BF16) for g in gots],
        compiler_params=_cp(("parallel", "parallel")))(core, *views, *gots)


def _chip_sums(parts):
    n = len(parts)

    def body(*refs):
        for a in range(n):
            acc = refs[a][0].astype(F32)
            for k in range(1, 4):
                acc = acc + refs[a][k].astype(F32)
            refs[n + a][...] = acc

    return pl.pallas_call(
        body, name="chip_sums", in_specs=[VMEM_SPEC] * n, out_specs=[VMEM_SPEC] * n,
        out_shape=[jax.ShapeDtypeStruct(p.shape[1:], F32) for p in parts], compiler_params=_cp())(*parts)


def _permute_in(w):
    lead = w.shape[:-1]
    return w.reshape(lead + (3, 3, 2, BQ)).swapaxes(-2, -3).reshape(lead + (QKVW,))


def _unpermute_in(w):
    lead = w.shape[:-1]
    return w.reshape(lead + (3, 2, 3, BQ)).swapaxes(-2, -3).reshape(lead + (QKVW,))


def _row(v):
    v = v.reshape(-1)
    return jnp.pad(v, (0, D - v.shape[0])).reshape(1, D)


def kernel(x, w_in, f_bias, conv_w, w_out, rel_bias, ln1_g, ln1_b, w_gate, w_up, w_down, ln2_g, ln2_b, loss_target, m_w_in, m_f_bias, m_conv_w, m_w_out, m_rel_bias, m_ln1_g, m_ln1_b, m_w_gate, m_w_up, m_w_down, m_ln2_g, m_ln2_b, v_w_in, v_f_bias, v_conv_w, v_w_out, v_rel_bias, v_ln1_g, v_ln1_b, v_w_gate, v_w_up, v_w_down, v_ln2_g, v_ln2_b):
    xi, yi, ci = _place()
    me = 4 * xi + 2 * yi + ci
    core = jnp.reshape(ci, (1,)).astype(jnp.int32)

    win_s = jnp.concatenate([_permute_in(w_in[..., :QKVW]), w_in[..., QKVW:]], axis=-1)
    win_s = jnp.pad(win_s, ((0, 0), (0, 0), (0, NPAD - NPROJ))).astype(BF16)
    per_layer = [win_s, w_out.astype(BF16), jnp.swapaxes(w_gate, 1, 2).astype(BF16),
                 jnp.swapaxes(w_up, 1, 2).astype(BF16), w_down.astype(BF16)]
    sh = [[s[l] for s in per_layer] for l in range(2)]

    def whole(g):
        return g.reshape(NDEV * g.shape[1], g.shape[2])

    cw_rows = lax.dynamic_update_slice(jnp.zeros((2, 3, 256), F32), conv_w, (0, 0, me * 32))
    small = jnp.concatenate([_row(cw_rows[0]), _row(cw_rows[1]), jnp.zeros((SMALL_ROWS - 2, D), F32)], axis=0)
    small, leg_a = _allreduce_small(small, job=_gather_job_a(sh[0][:1]))
    cw_full = small[0:2, :CONVW].reshape(2, 3, 256)
    cw8 = jnp.pad(cw_full, ((0, 0), (0, 5), (0, 0)))
    fb = jnp.pad(f_bias, ((0, 0), (0, GATEW - NH))).reshape(2, 1, GATEW)
    tbl, leg_b = _dil_table(rel_bias, job=_gather_job_b(list(leg_a)))
    W = [{"win": whole(leg_b[0])}, {}]

    def wrow(tn, K, blk=0):
        return pl.BlockSpec((tn, K), lambda i, j: (j, blk))

    def arow(tm, K, blk=0):
        return pl.BlockSpec((tm, K), lambda i, j: (i, blk))

    h = x.reshape(T, D)
    hb = h.astype(BF16)
    saved = []
    for l in range(2):
        Win = W[l]["win"]
        qkv, conv, gate = _proj(hb, Win)
        cum = _fox_prep(gate, fb[l])
        cq = cum[:, :NH].reshape(BL, S, NH).transpose(0, 2, 1).reshape(NSTAT, S)
        ckb = jnp.broadcast_to(cq[:, :, None], (NSTAT, S, BQ))
        if l == 0:
            mixed, amat, a0 = _sb_fwd(qkv, job=_gather_job_a(sh[0][1:]))
            mixed, lse_d, ex = _flash_fwd(qkv, mixed, 1, False, (tbl,),
                                          job=_join_jobs(_gather_job_b(list(a0)), _gather_job_a(sh[1][:2])))
            W[0].update(zip(("wout", "wgT", "wuT", "wd"), [whole(t) for t in ex[:4]]))
            mixed, lse_f, o_fox, ex = _flash_fwd(qkv, mixed, 2, True, (cq, ckb),
                                                 job=_join_jobs(_gather_job_b(list(ex[4:])), _gather_job_a(sh[1][2:])))
            W[1].update(zip(("win", "wout"), [whole(t) for t in ex[:2]]))
            a2 = list(ex[2:])
        else:
            mixed, amat, ex = _sb_fwd(qkv, job=_gather_job_b(a2))
            W[1].update(zip(("wgT", "wuT", "wd"), [whole(t) for t in ex]))
            mixed, lse_d, _ = _flash_fwd(qkv, mixed, 1, False, (tbl,))
            mixed, lse_f, o_fox, _ = _flash_fwd(qkv, mixed, 2, True, (cq, ckb))
        Wout, WgT, WuT, Wd = W[l]["wout"], W[l]["wgT"], W[l]["wuT"], W[l]["wd"]
        mixed = _conv_fwd(conv, cw8[l], mixed)
        x1, xh1, r1, x1b = _mm_ln(mixed, Wout, h, ln1_g[l:l + 1], ln1_b[l:l + 1], "out_proj_ln")
        fs, ft, a, x2, xh2, r2, x2b = _ffn_fwd(x1b, x1, WgT, WuT, Wd, ln2_g[l:l + 1], ln2_b[l:l + 1])
        saved.append(dict(h=hb, qkv=qkv, conv=conv, gate=gate, cq=cq, ckb=ckb, mixed=mixed, amat=amat, lse_d=lse_d,
                          lse_f=lse_f, o_fox=o_fox, x1=x1b, xh1=xh1, r1=r1, fs=fs, ft=ft, a=a, xh2=xh2, r2=r2))
        h, hb = x2, x2b

    dy = h

    def view(gr):
        return gr.reshape(4, 2, gr.shape[0] // NDEV, gr.shape[1])

    G = [None, None]
    small_g = {}
    shard_g = {}
    for l in (1, 0):
        sv = saved[l]
        Win, Wout, WgT, WuT, Wd = W[l]["win"], W[l]["wout"], W[l]["wgT"], W[l]["wuT"], W[l]["wd"]
        res = _ffn_bwd(dy, sv["xh2"], sv["r2"], ln2_g[l:l + 1], sv["fs"], sv["ft"], Wd, WgT, WuT,
                       target=loss_target.reshape(T, D) if l == 1 else None)
        dgt, dut, ds2b, dx1, dg2, db2 = res[:6]
        if l == 1:
            sq = res[6]
        G_d = _mm_tn(sv["a"], ds2b, None, C=D, Ka=DFF, N=D, tm=256, tn=1024, tk=T, ooff=0, name="grad_w_down")
        G_g = _mm_tn(dgt, sv["x1"], None, C=D, Ka=DFF, N=D, tm=256, tn=1024, tk=T, ooff=0, name="grad_w_gate")
        G_u = _mm_tn(dut, sv["x1"], None, C=D, Ka=DFF, N=D, tm=256, tn=1024, tk=T, ooff=0, name="grad_w_up")
        ds1, dg1, db1, ds1b, dmixed = _ln_bwd(dx1, sv["xh1"], sv["r1"], ln1_g[l:l + 1], Wout)
        G_out = _mm_tn(sv["mixed"], ds1b, None, C=D, Ka=D, N=D, tm=256, tn=1024, tk=T, ooff=0, name="grad_w_out")
        early = [view(t) for t in (G_g, G_u, G_d, G_out)] + ([view(G[1]["in"])] if l == 0 else [])
        dqkv, gots = _sb_bwd(sv["qkv"], dmixed, sv["amat"], job=_sibling_job(early))
        ps = _pair_sums(early, list(gots), core)
        dqkv, dtbl, pa = _flash_bwd(sv["qkv"], sv["mixed"], dmixed, sv["lse_d"], dqkv, 1, False, (tbl,),
                                    job=_chip_job(ps[:2]))
        dqkv, dck, pb = _flash_bwd(sv["qkv"], sv["o_fox"], dmixed, sv["lse_f"], dqkv, 2, True,
                                   (sv["cq"], sv["ckb"]), job=_chip_job(ps[2:]))
        sums = _chip_sums(list(pa) + list(pb))
        shard_g[l] = dict(zip(("g", "u", "d", "out"), sums[:4]))
        if l == 0:
            shard_g[1]["in"] = sums[4]
        dconv, dcw = _conv_bwd(sv["conv"], cw8[l], dmixed)
        dcum = jnp.pad(dck.reshape(S, BL, NH).transpose(1, 0, 2).reshape(T, NH), ((0, 0), (0, GATEW - NH)))
        dgate, dfb = _fox_post(dcum, sv["gate"], fb[l])
        drb = _dil_table_bwd(dtbl)
        G_in = _mm_tn(sv["h"], dqkv, None, C=NPAD, Ka=D, N=QKVW, tm=512, tn=768, tk=T, ooff=0, name="grad_w_in_qkv")
        G_in = _mm_tn(sv["h"], dconv, G_in, C=NPAD, Ka=D, N=CONVW, tm=256, tn=768, tk=T, ooff=3,
                      name="grad_w_in_conv")
        G_in = _mm_tn(sv["h"], dgate, G_in, C=NPAD, Ka=D, N=GATEW, tm=1024, tn=128, tk=1024, ooff=24,
                      name="grad_w_in_gate")
        G[l] = {"in": G_in, "out": G_out, "g": G_g, "u": G_u, "d": G_d}
        if l == 0:
            late = [view(G_in)]
            tail = _chip_job(_pair_sums(late, list(_run_job(_sibling_job(late), "sibling_exchange")), core))
            dy, parts = _mm([(dqkv, arow(1024, QKVW), Win, wrow(512, QKVW, 0)),
                             (dconv, arow(1024, CONVW), Win, wrow(512, CONVW, 3)),
                             (dgate, arow(1024, GATEW), Win, wrow(512, GATEW, 24))],
                            nt=True, M=T, N=D, tm=1024, tn=512, out_dtype=F32, name="proj_dx", res=ds1,
                            res_scale=ALPHA, job=tail)
            shard_g[0]["in"] = _chip_sums(list(parts))[0]
        else:
            dy = _proj_bwd(dqkv, dconv, dgate, Win, ds1)
        small_g[l] = dict(ln1_g=dg1, ln1_b=db1, ln2_g=dg2, ln2_b=db2, cw=dcw[0:3].reshape(1, CONVW),
                          fb=dfb[:, :NH], rb=drb[:, :NH])
    grad_x = dy.reshape(BL, S, D)

    rows = []
    for name in ("ln1_g", "ln1_b", "ln2_g", "ln2_b"):
        rows += [small_g[0][name], small_g[1][name]]
    rows += [_row(small_g[0]["cw"]), _row(small_g[1]["cw"]),
             _row(jnp.concatenate([small_g[0]["fb"], small_g[1]["fb"]], axis=0)),
             _row(small_g[0]["rb"] + small_g[1]["rb"]), _row(sq)]
    rows.append(jnp.zeros((SMALL_ROWS - len(rows), D), F32))
    sg = _allreduce_small(jnp.concatenate(rows, axis=0))
    loss = sg[12, 0] * (0.5 / D)
    g_ln1_g, g_ln1_b, g_ln2_g, g_ln2_b = sg[0:2], sg[2:4], sg[4:6], sg[6:8]
    g_conv_full = sg[8:10, :CONVW].reshape(2, 3, 256)
    g_conv = lax.dynamic_slice(g_conv_full, (0, 0, me * 32), (2, 3, 32))
    g_fb = sg[10, :2 * NH].reshape(2, NH)
    g_rb = sg[11, :32 * NH].reshape(32, NH)

    def both(name):
        return jnp.stack([shard_g[0][name], shard_g[1][name]])

    g_in = both("in")
    g_w_in = jnp.concatenate([_unpermute_in(g_in[..., :QKVW]), g_in[..., QKVW:NPROJ]], axis=-1)
    g_w_out = both("out")
    g_w_gate = jnp.swapaxes(both("g"), 1, 2)
    g_w_up = jnp.swapaxes(both("u"), 1, 2)
    g_w_down = both("d")

    up_in = _adamw(w_in, g_w_in, m_w_in, v_w_in, 64)
    up_out = _adamw(w_out, g_w_out, m_w_out, v_w_out, 128)
    up_gate = _adamw(w_gate, g_w_gate, m_w_gate, v_w_gate, 256)
    up_up = _adamw(w_up, g_w_up, m_w_up, v_w_up, 256)
    up_down = _adamw(w_down, g_w_down, m_w_down, v_w_down, 352)

    def pack(fbv, cwv, rbv, l1g, l1b, l2g, l2b):
        r = [l1g, l1b, l2g, l2b, _row(cwv), _row(fbv), _row(rbv)]
        r.append(jnp.zeros((SMALL_ROWS - 11, D), F32))
        return jnp.concatenate(r, axis=0)

    pw = pack(f_bias, conv_w, rel_bias, ln1_g, ln1_b, ln2_g, ln2_b)
    pg = pack(g_fb, g_conv, g_rb, g_ln1_g, g_ln1_b, g_ln2_g, g_ln2_b)
    pm = pack(m_f_bias, m_conv_w, m_rel_bias, m_ln1_g, m_ln1_b, m_ln2_g, m_ln2_b)
    pv = pack(v_f_bias, v_conv_w, v_rel_bias, v_ln1_g, v_ln1_b, v_ln2_g, v_ln2_b)
    ups = [u[0] for u in _adamw(pw[None], pg[None], pm[None], pv[None], SMALL_ROWS)]

    def unpack(p):
        return dict(ln1_g=p[0:2], ln1_b=p[2:4], ln2_g=p[4:6], ln2_b=p[6:8],
                    conv_w=p[8, :192].reshape(2, 3, 32), f_bias=p[9, :2 * NH].reshape(2, NH),
                    rel_bias=p[10, :32 * NH].reshape(32, NH))

    sm = [unpack(p) for p in ups]

    def group(k):
        return (up_in[k], sm[k]["f_bias"], sm[k]["conv_w"], up_out[k], sm[k]["rel_bias"], sm[k]["ln1_g"],
                sm[k]["ln1_b"], up_gate[k], up_up[k], up_down[k], sm[k]["ln2_g"], sm[k]["ln2_b"])

    grads = (g_w_in, g_fb, g_conv, g_w_out, g_rb, g_ln1_g, g_ln1_b, g_w_gate, g_w_up, g_w_down, g_ln2_g, g_ln2_b)
    return (loss, grad_x) + grads + group(0) + group(1) + group(2)
```

```python
import math

import numpy as np
import jax
import jax.numpy as jnp
from jax import lax
from jax.experimental import pallas as pl
from jax.experimental.pallas import tpu as pltpu

F32 = jnp.float32
BF16 = jnp.bfloat16
MESH = pl.DeviceIdType.MESH

D = 1024
S = 2048
BL = 2
T = BL * S
NH = 4
DFF = 2816
NPROJ = 3076
NPAD = 3200
QKVW = 2304
CONVW = 768
GATEW = 128
PAIRW = 384
BQ = 128
HB = 2 * BQ
NB = S // BQ
NDEV = 8
NSTAT = BL * NH
ALPHA = 4.0 ** 0.25
SCALE = 0.125
NEG = -1e30
LN_EPS = 1e-5
ADAM_LR, ADAM_B1, ADAM_B2, ADAM_EPS, ADAM_WD, ADAM_STEP = 0.001, 0.9, 0.999, 1e-08, 0.01, 10
VMEM_LIMIT = 56 * 1024 * 1024
SMALL_ROWS = 16


def _bucket_thresholds():
    d = np.arange(0, S)
    nf = np.maximum(d, 1).astype(np.float32)
    large = 16 + (np.log(nf / np.float32(16)) / np.float32(math.log(128)) * np.float32(16)).astype(np.int32)
    b = np.where(d < 16, d, np.minimum(large, 31))
    return [int(np.argmax(b >= k)) for k in range(32)]


BUCKET_TH = _bucket_thresholds()


def _cp(sem=None, vmem=VMEM_LIMIT):
    return pltpu.CompilerParams(dimension_semantics=sem, vmem_limit_bytes=vmem)


def _dot(a, b):
    return lax.dot_general(a, b, (((1,), (0,)), ((), ())), preferred_element_type=F32)


def _dot_nt(a, b):
    return lax.dot_general(a, b, (((1,), (1,)), ((), ())), preferred_element_type=F32)


def _dot_tn(a, b):
    return lax.dot_general(a, b, (((0,), (0,)), ((), ())), preferred_element_type=F32)


def _split2(x):
    hi = x.astype(BF16)
    mid = (x - hi.astype(F32)).astype(BF16)
    return jnp.concatenate([hi, mid], axis=1)


def _split3(x):
    hi = x.astype(BF16)
    r = x - hi.astype(F32)
    mid = r.astype(BF16)
    lo = (r - mid.astype(F32)).astype(BF16)
    return jnp.concatenate([hi, mid, lo], axis=1)


def _log_sigmoid(u):
    return jnp.minimum(u, 0.0) - jnp.log1p(jnp.exp(-jnp.abs(u)))


def _log_sigmoid_tile(u):
    return jnp.minimum(u, 0.0) - jnp.log(1.0 + jnp.exp(jnp.minimum(u, -u)))


def _iota(shape, dim):
    return lax.broadcasted_iota(jnp.int32, shape, dim)


ANY_SPEC = pl.BlockSpec(memory_space=pl.ANY)
VMEM_SPEC = pl.BlockSpec(memory_space=pltpu.VMEM)


def _mm(pairs, *, nt, M, N, tm, tn, out_dtype, name, res=None, res_scale=1.0, job=None):
    n = len(pairs)
    n_in = 2 * n + (res is not None)
    jins = job.ins if job else []
    jouts = job.out_shapes if job else []
    gi, gj = M // tm, N // tn

    def body(*refs):
        o_ref = refs[n_in + len(jins)]
        if job:
            jrefs = (refs[n_in:n_in + len(jins)], refs[n_in + len(jins) + 1:n_in + len(jins) + 1 + len(jouts)],
                     refs[n_in + len(jins) + 1 + len(jouts):])

            @pl.when((pl.program_id(0) == 0) & (pl.program_id(1) == 0))
            def _():
                job.start(*jrefs)

        acc = None
        for p in range(n):
            a = refs[2 * p][...].astype(BF16)
            b = refs[2 * p + 1][...]
            d = _dot_nt(a, b) if nt else _dot(a, b)
            acc = d if acc is None else acc + d
        if res is not None:
            acc = acc + res_scale * refs[2 * n][...]
        o_ref[...] = acc.astype(out_dtype)
        if job:
            @pl.when((pl.program_id(0) == gi - 1) & (pl.program_id(1) == gj - 1))
            def _():
                job.finish(None, *jrefs)

    ops, specs = [], []
    for a, asp, b, bsp in pairs:
        ops += [a, b]
        specs += [asp, bsp]
    if res is not None:
        ops.append(res)
        specs.append(pl.BlockSpec((tm, tn), lambda i, j: (i, j)))
    out = pl.pallas_call(
        body, name=name, grid=(gi, gj), in_specs=specs + [ANY_SPEC] * len(jins),
        out_specs=[pl.BlockSpec((tm, tn), lambda i, j: (i, j))] + [ANY_SPEC] * len(jouts),
        out_shape=[jax.ShapeDtypeStruct((M, N), out_dtype)] + list(jouts),
        scratch_shapes=list(job.sems) if job else [],
        input_output_aliases={n_in + a: 1 + b for a, b in job.aliases.items()} if job else {},
        compiler_params=_cp(("arbitrary", "arbitrary") if job else ("parallel", "parallel")))(*ops, *jins)
    return (out[0], out[1:]) if job else out[0]


def _mm_tn(a, b, gbuf, *, C, Ka, N, tm, tn, tk, ooff, name):
    def body(*refs):
        a_ref, b_ref, o_ref = refs[0], refs[1], refs[-1]
        k = pl.program_id(2)
        d = _dot_tn(a_ref[...].astype(BF16), b_ref[...].astype(BF16))

        @pl.when(k == 0)
        def _():
            o_ref[...] = d

        @pl.when(k > 0)
        def _():
            o_ref[...] += d

    ops = [a, b] + ([] if gbuf is None else [gbuf])
    return pl.pallas_call(
        body, name=name, grid=(Ka // tm, N // tn, T // tk),
        in_specs=[pl.BlockSpec((tk, tm), lambda i, j, k: (k, i)),
                  pl.BlockSpec((tk, tn), lambda i, j, k: (k, j))] + ([] if gbuf is None else [ANY_SPEC]),
        out_specs=pl.BlockSpec((tm, tn), lambda i, j, k: (i, ooff + j)),
        out_shape=jax.ShapeDtypeStruct((Ka, C), F32),
        input_output_aliases={} if gbuf is None else {2: 0},
        compiler_params=_cp(("parallel", "parallel", "arbitrary")))(*ops)


def _proj(xb, w):
    tm = 512

    def body(x_ref, w_ref, qkv_ref, conv_ref, gate_ref):
        xv = x_ref[...]
        qkv_ref[...] = _dot(xv, w_ref[:, 0:QKVW]).astype(BF16)
        conv_ref[...] = _dot(xv, w_ref[:, QKVW:QKVW + CONVW])
        gate_ref[...] = _dot(xv, w_ref[:, QKVW + CONVW:NPAD])

    def rows(n):
        return pl.BlockSpec((tm, n), lambda i: (i, 0))

    return pl.pallas_call(
        body, name="proj", grid=(T // tm,),
        in_specs=[rows(D), pl.BlockSpec((D, NPAD), lambda i: (0, 0))],
        out_specs=[rows(QKVW), rows(CONVW), rows(GATEW)],
        out_shape=[jax.ShapeDtypeStruct((T, QKVW), BF16), jax.ShapeDtypeStruct((T, CONVW), F32),
                   jax.ShapeDtypeStruct((T, GATEW), F32)],
        compiler_params=_cp(("parallel",)))(xb, w)


def _proj_bwd(dqkv, dconv, dgate, w, res):
    tm = 512

    def body(a_ref, b_ref, c_ref, w_ref, r_ref, o_ref):
        acc = ALPHA * r_ref[...] + _dot_nt(a_ref[...], w_ref[:, 0:QKVW])
        acc = acc + _dot_nt(b_ref[...], w_ref[:, QKVW:QKVW + CONVW])
        o_ref[...] = acc + _dot_nt(c_ref[...].astype(BF16), w_ref[:, QKVW + CONVW:NPAD])

    def rows(n):
        return pl.BlockSpec((tm, n), lambda i: (i, 0))

    return pl.pallas_call(
        body, name="proj_bwd", grid=(T // tm,),
        in_specs=[rows(QKVW), rows(CONVW), rows(GATEW), pl.BlockSpec((D, NPAD), lambda i: (0, 0)), rows(D)],
        out_specs=rows(D), out_shape=jax.ShapeDtypeStruct((T, D), F32),
        compiler_params=_cp(("parallel",)))(dqkv, dconv, dgate, w, res)


def _ffn_fwd(xb, x, wgt, wut, wd, gam, bet):
    tm, ch = 512, 256

    def body(xb_ref, x_ref, g_ref, b_ref, wg_hbm, wu_hbm, wd_hbm,
             go_ref, uo_ref, ao_ref, y_ref, xh_ref, r_ref, yb_ref, wg_v, wu_v, wd_v, sem):
        loads = [pltpu.make_async_copy(s, d, sem.at[k])
                 for k, (s, d) in enumerate(((wg_hbm, wg_v), (wu_hbm, wu_v), (wd_hbm, wd_v)))]

        @pl.when(pl.program_id(0) == 0)
        def _():
            for cp in loads:
                cp.start()
            loads[0].wait()
            loads[1].wait()

        xv = xb_ref[...]
        for c in range(0, DFF, ch):
            gv = _dot_nt(xv, wg_v[c:c + ch, :])
            uv = _dot_nt(xv, wu_v[c:c + ch, :])
            go_ref[:, c:c + ch] = gv.astype(BF16)
            uo_ref[:, c:c + ch] = uv.astype(BF16)
            ao_ref[:, c:c + ch] = (gv * jax.nn.sigmoid(gv) * uv).astype(BF16)
        @pl.when(pl.program_id(0) == 0)
        def _():
            loads[2].wait()

        s = ALPHA * x_ref[...] + _dot(ao_ref[...], wd_v[...])
        mu = jnp.mean(s, axis=-1, keepdims=True)
        xc = s - mu
        var = jnp.mean(xc * xc, axis=-1, keepdims=True)
        r = lax.rsqrt(var + LN_EPS)
        xh = xc * r
        xh_ref[...] = xh.astype(BF16)
        r_ref[...] = r
        y = xh * g_ref[...] + b_ref[...]
        y_ref[...] = y
        yb_ref[...] = y.astype(BF16)

    row = pl.BlockSpec((tm, D), lambda i: (i, 0))
    wide = pl.BlockSpec((tm, DFF), lambda i: (i, 0))
    vec = pl.BlockSpec((1, D), lambda i: (0, 0))
    wsl = pltpu.VMEM((DFF, D), BF16)
    hid = jax.ShapeDtypeStruct((T, DFF), BF16)
    return pl.pallas_call(
        body, name="ffn_fwd", grid=(T // tm,),
        in_specs=[row, row, vec, vec, ANY_SPEC, ANY_SPEC, ANY_SPEC],
        out_specs=[wide, wide, wide, row, row, pl.BlockSpec((tm, 1), lambda i: (i, 0)), row],
        out_shape=[hid, hid, hid, jax.ShapeDtypeStruct((T, D), F32), jax.ShapeDtypeStruct((T, D), BF16),
                   jax.ShapeDtypeStruct((T, 1), F32), jax.ShapeDtypeStruct((T, D), BF16)],
        scratch_shapes=[wsl, wsl, wsl, pltpu.SemaphoreType.DMA((3,))],
        compiler_params=_cp(("arbitrary",)))(xb, x, gam, bet, wgt, wut, wd)


def _ffn_bwd(dy, xh, r, gam, g, u, wd, wgt, wut, target=None):
    tm, ch = 256, 256

    def body(*refs):
        if target is None:
            (dy_ref, xh_ref, r_ref, gam_ref, g_ref, u_ref, wd_hbm, wg_hbm, wu_hbm,
             dg_ref, du_ref, dsb_ref, dx_ref, dgam_ref, dbet_ref, wd_v, wg_v, wu_v, sem) = refs
        else:
            (dy_ref, t_ref, xh_ref, r_ref, gam_ref, g_ref, u_ref, wd_hbm, wg_hbm, wu_hbm,
             dg_ref, du_ref, dsb_ref, dx_ref, dgam_ref, dbet_ref, sq_ref, wd_v, wg_v, wu_v, sem) = refs
        loads = [pltpu.make_async_copy(s, d, sem.at[k])
                 for k, (s, d) in enumerate(((wd_hbm, wd_v), (wg_hbm, wg_v), (wu_hbm, wu_v)))]

        @pl.when(pl.program_id(0) == 0)
        def _():
            for cp in loads:
                cp.start()
            loads[0].wait()

        if target is None:
            dyv = dy_ref[...]
        else:
            e = dy_ref[...] - t_ref[...]
            dyv = e * (1.0 / D)
            p = jnp.sum(jnp.sum(e * e, axis=1, keepdims=True), axis=0, keepdims=True)

            @pl.when(pl.program_id(0) == 0)
            def _():
                sq_ref[...] = p

            @pl.when(pl.program_id(0) > 0)
            def _():
                sq_ref[...] += p

        xhv = xh_ref[...].astype(F32)
        dxh = dyv * gam_ref[...]
        m1 = jnp.mean(dxh, axis=-1, keepdims=True)
        m2 = jnp.mean(dxh * xhv, axis=-1, keepdims=True)
        ds = r_ref[...] * (dxh - m1 - xhv * m2)
        pg = jnp.sum(dyv * xhv, axis=0, keepdims=True)
        pb = jnp.sum(dyv, axis=0, keepdims=True)

        @pl.when(pl.program_id(0) == 0)
        def _():
            dgam_ref[...] = pg
            dbet_ref[...] = pb

        @pl.when(pl.program_id(0) > 0)
        def _():
            dgam_ref[...] += pg
            dbet_ref[...] += pb

        db = ds.astype(BF16)
        dsb_ref[...] = db
        for c in range(0, DFF, ch):
            da = _dot_nt(db, wd_v[c:c + ch, :])
            gv = g_ref[:, c:c + ch].astype(F32)
            sg = jax.nn.sigmoid(gv)
            dg_ref[:, c:c + ch] = (da * u_ref[:, c:c + ch].astype(F32) * (sg * (1.0 + gv * (1.0 - sg)))).astype(BF16)
            du_ref[:, c:c + ch] = (da * (gv * sg)).astype(BF16)
        @pl.when(pl.program_id(0) == 0)
        def _():
            loads[1].wait()
            loads[2].wait()

        dx_ref[...] = ALPHA * ds + _dot(dg_ref[...], wg_v[...]) + _dot(du_ref[...], wu_v[...])

    row = pl.BlockSpec((tm, D), lambda i: (i, 0))
    wide = pl.BlockSpec((tm, DFF), lambda i: (i, 0))
    vec = pl.BlockSpec((1, D), lambda i: (0, 0))
    wsl = pltpu.VMEM((DFF, D), BF16)
    last = target is not None
    return pl.pallas_call(
        body, name="ffn_bwd_loss" if last else "ffn_bwd", grid=(T // tm,),
        in_specs=[row] + ([row] if last else [])
        + [row, pl.BlockSpec((tm, 1), lambda i: (i, 0)), vec, wide, wide, ANY_SPEC, ANY_SPEC, ANY_SPEC],
        out_specs=[wide, wide, row, row, vec, vec] + ([pl.BlockSpec((1, 1), lambda i: (0, 0))] if last else []),
        out_shape=[jax.ShapeDtypeStruct((T, DFF), BF16), jax.ShapeDtypeStruct((T, DFF), BF16),
                   jax.ShapeDtypeStruct((T, D), BF16), jax.ShapeDtypeStruct((T, D), F32),
                   jax.ShapeDtypeStruct((1, D), F32), jax.ShapeDtypeStruct((1, D), F32)]
        + ([jax.ShapeDtypeStruct((1, 1), F32)] if last else []),
        scratch_shapes=[wsl, wsl, wsl, pltpu.SemaphoreType.DMA((3,))],
        compiler_params=_cp(("arbitrary",)))(dy, *([target] if last else []), xh, r, gam, g, u, wd, wgt, wut)


def _mm_ln(a, w, x, gam, bet, name):
    tm = 256
    K = a.shape[1]

    def body(a_ref, w_ref, x_ref, g_ref, b_ref, y_ref, xh_ref, r_ref, yb_ref):
        s = ALPHA * x_ref[...] + _dot(a_ref[...], w_ref[...])
        mu = jnp.mean(s, axis=-1, keepdims=True)
        xc = s - mu
        var = jnp.mean(xc * xc, axis=-1, keepdims=True)
        r = lax.rsqrt(var + LN_EPS)
        xh = xc * r
        xh_ref[...] = xh.astype(BF16)
        r_ref[...] = r
        y = xh * g_ref[...] + b_ref[...]
        y_ref[...] = y
        yb_ref[...] = y.astype(BF16)

    row = pl.BlockSpec((tm, D), lambda i: (i, 0))
    vec = pl.BlockSpec((1, D), lambda i: (0, 0))
    return pl.pallas_call(
        body, name=name, grid=(T // tm,),
        in_specs=[pl.BlockSpec((tm, K), lambda i: (i, 0)), pl.BlockSpec((K, D), lambda i: (0, 0)), row, vec, vec],
        out_specs=[row, row, pl.BlockSpec((tm, 1), lambda i: (i, 0)), row],
        out_shape=[jax.ShapeDtypeStruct((T, D), F32), jax.ShapeDtypeStruct((T, D), BF16),
                   jax.ShapeDtypeStruct((T, 1), F32), jax.ShapeDtypeStruct((T, D), BF16)],
        compiler_params=_cp(("parallel",)))(a, w, x, gam, bet)


def _ln_bwd(dy, xh, r, gam, w):
    tm = 256

    def body(dy_ref, xh_ref, r_ref, g_ref, w_ref, ds_ref, dg_ref, db_ref, dsb_ref, dm_ref):
        i = pl.program_id(0)
        dyv = dy_ref[...]
        xhv = xh_ref[...].astype(F32)
        dxh = dyv * g_ref[...]
        m1 = jnp.mean(dxh, axis=-1, keepdims=True)
        m2 = jnp.mean(dxh * xhv, axis=-1, keepdims=True)
        ds = r_ref[...] * (dxh - m1 - xhv * m2)
        ds_ref[...] = ds
        dsb = ds.astype(BF16)
        dsb_ref[...] = dsb
        dm_ref[...] = _dot_nt(dsb, w_ref[...]).astype(BF16)
        pg = jnp.sum(dyv * xhv, axis=0, keepdims=True)
        pb = jnp.sum(dyv, axis=0, keepdims=True)

        @pl.when(i == 0)
        def _():
            dg_ref[...] = pg
            db_ref[...] = pb

        @pl.when(i > 0)
        def _():
            dg_ref[...] += pg
            db_ref[...] += pb

    row = pl.BlockSpec((tm, D), lambda i: (i, 0))
    vec = pl.BlockSpec((1, D), lambda i: (0, 0))
    return pl.pallas_call(
        body, name="ln_bwd_proj", grid=(T // tm,),
        in_specs=[row, row, pl.BlockSpec((tm, 1), lambda i: (i, 0)), vec, pl.BlockSpec((D, D), lambda i: (0, 0))],
        out_specs=[row, vec, vec, row, row],
        out_shape=[jax.ShapeDtypeStruct((T, D), F32), jax.ShapeDtypeStruct((1, D), F32),
                   jax.ShapeDtypeStruct((1, D), F32), jax.ShapeDtypeStruct((T, D), BF16),
                   jax.ShapeDtypeStruct((T, D), BF16)],
        compiler_params=_cp(("arbitrary",)))(dy, xh, r, gam, w)


def _adamw(w, g, m, v, tr):
    L, R, C = w.shape

    def body(w_ref, g_ref, m_ref, v_ref, d_ref, m2_ref, v2_ref):
        gv = g_ref[...]
        m2 = ADAM_B1 * m_ref[...] + (1.0 - ADAM_B1) * gv
        v2 = ADAM_B2 * v_ref[...] + (1.0 - ADAM_B2) * (gv * gv)
        m_hat = m2 / (1.0 - ADAM_B1 ** ADAM_STEP)
        v_hat = v2 / (1.0 - ADAM_B2 ** ADAM_STEP)
        d_ref[...] = -ADAM_LR * (m_hat / (jnp.sqrt(v_hat) + ADAM_EPS) + ADAM_WD * w_ref[...])
        m2_ref[...] = m2
        v2_ref[...] = v2

    blk = pl.BlockSpec((None, tr, C), lambda l, i: (l, i, 0))
    sh = jax.ShapeDtypeStruct((L, R, C), F32)
    return pl.pallas_call(
        body, name="adamw", grid=(L, R // tr), in_specs=[blk] * 4, out_specs=[blk] * 3,
        out_shape=[sh, sh, sh], compiler_params=_cp(("parallel", "parallel")))(w, g, m, v)


class _Job:
    def __init__(self, ins, out_shapes, aliases, sems, start, finish):
        self.ins, self.out_shapes, self.aliases, self.sems = list(ins), list(out_shapes), dict(aliases), list(sems)
        self.start, self.finish = start, finish


def _host_call(body, name, ins, in_specs, out_shapes, out_specs, scratch, aliases, job):
    n_in, n_out, n_scr = len(ins), len(out_shapes), len(scratch)
    jins = job.ins if job else []
    jouts = job.out_shapes if job else []
    jsems = job.sems if job else []

    def wrapped(*refs):
        a = n_in
        b = a + len(jins)
        c = b + n_out
        d = c + len(jouts)
        e = d + n_scr
        comm = None
        if job:
            jrefs = (refs[a:b], refs[c:d], refs[e:])
            comm = (lambda: job.start(*jrefs), lambda st: job.finish(st, *jrefs))
        body(refs[:a], refs[b:c], refs[d:e], comm)

    al = dict(aliases)
    if job:
        for ji, jo in job.aliases.items():
            al[n_in + ji] = n_out + jo
    res = pl.pallas_call(
        wrapped, name=name, in_specs=list(in_specs) + [ANY_SPEC] * len(jins),
        out_specs=list(out_specs) + [ANY_SPEC] * len(jouts), out_shape=list(out_shapes) + list(jouts),
        scratch_shapes=list(scratch) + list(jsems), input_output_aliases=al,
        compiler_params=_cp())(*ins, *jins)
    return res[:n_out], res[n_out:]


def _copy_in(src, dst, sem):
    cp = pltpu.make_async_copy(src, dst, sem)
    cp.start()
    cp.wait()


CHAINS = [(p, b) for p in range(2) for b in range(BL)]
NC = len(CHAINS)
ROWS_SHAPE = jax.ShapeDtypeStruct((NSTAT, S), F32)
SLAB_QKV = pltpu.VMEM((T, 2 * PAIRW), BF16)
SLAB_OUT = pltpu.VMEM((T, 2 * BQ), BF16)
SLAB_O32 = pltpu.VMEM((T, 2 * BQ), F32)
SLAB_T = pltpu.VMEM((2, BQ, T), BF16)
SLAB_KEYB = pltpu.VMEM((NSTAT, S, BQ), F32)
ACC_KV = pltpu.VMEM((2, T, BQ), F32)
NTRI = NB * (NB + 1) // 2
A_TILES = jax.ShapeDtypeStruct((NTRI, NC, HB, BQ), BF16)
PAIR_DIAG = pltpu.VMEM((NC, 2, HB, HB), BF16)
A_SLOTS_OUT, A_SLOTS_IN = 2, 6
FLASH_PER_TRIP = 4
A_AHEAD = A_SLOTS_IN - 2


def _lane_masks():
    lane = _iota((1, BQ), 1)
    m0 = (lane < 64).astype(BF16)
    return m0, 1.0 - m0


def _merge_heads(x, first):
    return jnp.where(first, x[:BQ], x[BQ:])


def _row_masks():
    r = _iota((BQ, 1), 0)
    m0 = (r < 64).astype(BF16)
    return m0, 1.0 - m0


def _stack(x, m0, m1):
    return jnp.concatenate([x * m0, x * m1], axis=0)


def _stack_t(xt, r0, r1):
    return jnp.concatenate([xt * r0, xt * r1], axis=1)


def _tr(x):
    return x.T


def _rows(b, i):
    return pl.ds(pl.multiple_of(b * S + i * BQ, BQ), BQ)


def _transpose_slab(src, dst, col0):
    def blk(n, _):
        r = pl.ds(pl.multiple_of(n * BQ, BQ), BQ)
        for p in range(2):
            dst[p, :, r] = _tr(src[r, col0(p):col0(p) + BQ])
        return 0

    lax.fori_loop(0, T // BQ, blk, 0)


def _heads(x):
    return x[:BQ], x[BQ:]


def _bcast_heads(r0, r1):
    return jnp.concatenate([jnp.broadcast_to(r0, (BQ, BQ)), jnp.broadcast_to(r1, (BQ, BQ))], axis=0)


def _by_channel(r0, r1):
    return jnp.where(_iota((BQ, BQ), 0) < 64, r0, r1)


def _colsum2(x):
    return jnp.sum(x[:BQ], axis=0, keepdims=True), jnp.sum(x[BQ:], axis=0, keepdims=True)


def _stat_row(ref, p, b, h, i):
    c = b * NH + 2 * p + h
    return ref[c:c + 1, pl.ds(pl.multiple_of(i * BQ, BQ), BQ)]


def _put_row(ref, p, b, h, i, v):
    c = b * NH + 2 * p + h
    ref[c:c + 1, pl.ds(pl.multiple_of(i * BQ, BQ), BQ)] = v


def _valid_t(strict):
    r = _iota((HB, BQ), 0) & (BQ - 1)
    c = _iota((HB, BQ), 1)
    return (r < c) if strict else (r <= c)


def _tri_blockdiag(later):
    r = _iota((HB, HB), 0)
    c = _iota((HB, HB), 1)
    same = (r >= BQ) == (c >= BQ)
    return (same & ((c > r) if later else (c < r))).astype(BF16)


def _cum_mm(tri, x):
    y = _dot(tri, _split2(x))
    return y[:, :BQ] + y[:, BQ:]


def _kv_tiles(qkv_v, p, b, j):
    r = _rows(b, j)
    return qkv_v[r, p * PAIRW + BQ:p * PAIRW + 2 * BQ], qkv_v[r, p * PAIRW + 2 * BQ:p * PAIRW + 3 * BQ]


def _blocks_loop(count, per_trip, step, carry):
    done = 0
    while per_trip >= 1:
        def trip(n, c, per_trip=per_trip, done=done):
            for u in range(per_trip):
                c = step(done + n * per_trip + u, c)
            return c

        trips = (count - done) // per_trip
        carry = lax.fori_loop(0, trips, trip, carry)
        done = done + trips * per_trip
        per_trip //= 2
    return carry


def _q_tile(qkv_v, p, b, i):
    return qkv_v[_rows(b, i), p * PAIRW:p * PAIRW + BQ] * SCALE


def _sb_fwd(qkv, job=None):
    def body(ins, outs, scr, comm):
        (qkv_hbm,), (o_hbm, a_hbm), (qkv_v, o_v, sem, vt_v, a_st, a_sems) = ins, outs, scr
        _copy_in(qkv_hbm.at[:, pl.ds(0, 2 * PAIRW)], qkv_v, sem)
        st = comm[0]() if comm else None
        _transpose_slab(qkv_v, vt_v, lambda p: p * PAIRW + 2 * BQ)
        m0, m1 = _lane_masks()
        r0, r1 = _row_masks()
        valid = _valid_t(True)
        later = _tri_blockdiag(True)

        def a_copy(n, t):
            slot = n % A_SLOTS_OUT
            return pltpu.make_async_copy(a_st.at[slot], a_hbm.at[t], a_sems.at[slot])

        def steps(qts, i, jj, cs, diag):
            j = i - jj
            ks = [_stack(_kv_tiles(qkv_v, p, b, j)[0], m0, m1) for p, b in CHAINS]
            zs = [_dot(ks[c], qts[c]) for c in range(NC)]
            lbs, lrs = [], []
            for c in range(NC):
                lb = _log_sigmoid_tile(zs[c])
                lr = lb - zs[c]
                if diag:
                    lr = jnp.where(valid, lr, 0.0)
                lbs.append(lb)
                lrs.append(lr)
            tails = [_cum_mm(later, lrs[c]) for c in range(NC)]
            avs = []
            for c in range(NC):
                a = jnp.exp(lbs[c] + tails[c] + _bcast_heads(*cs[c][0]))
                if diag:
                    a = jnp.where(valid, a, 0.0)
                avs.append(a.astype(BF16))
            n = (i * (i + 1)) // 2 + jj

            @pl.when(n >= A_SLOTS_OUT)
            def _():
                a_copy(n, 0).wait()
            for c in range(NC):
                a_st[n % A_SLOTS_OUT, c] = avs[c]
            a_copy(n, n - jj + j).start()
            out = []
            for c, (p, b) in enumerate(CHAINS):
                vts = _stack_t(vt_v[p, :, _rows(b, j)], r0, r1)
                s0, s1 = _colsum2(lrs[c])
                out.append(((cs[c][0][0] + s0, cs[c][0][1] + s1), cs[c][1] + _dot(vts, avs[c])))
            return tuple(out)

        def qblock(i, _):
            qts = [_tr(_q_tile(qkv_v, p, b, i)) for p, b in CHAINS]
            zr = jnp.zeros((1, BQ), F32)
            cs = steps(qts, i, 0, (((zr, zr), jnp.zeros((BQ, BQ), F32)),) * NC, True)
            cs = _blocks_loop(i, 2, lambda k, cs: steps(qts, i, k + 1, cs, False), cs)
            for c, (p, b) in enumerate(CHAINS):
                o_v[_rows(b, i), p * BQ:(p + 1) * BQ] = cs[c][1].T.astype(BF16)
            return 0

        lax.fori_loop(0, NB, qblock, 0)
        for n in range(NTRI - A_SLOTS_OUT, NTRI):
            a_copy(n, 0).wait()
        _copy_in(o_v, o_hbm.at[:, pl.ds(0, 2 * BQ)], sem)
        if comm:
            comm[1](st)

    (mixed, amat), extra = _host_call(
        body, "sb_fwd", [qkv], [ANY_SPEC], [jax.ShapeDtypeStruct((T, D), BF16), A_TILES], [ANY_SPEC, ANY_SPEC],
        [SLAB_QKV, SLAB_OUT, pltpu.SemaphoreType.DMA, SLAB_T, pltpu.VMEM((A_SLOTS_OUT, NC, HB, BQ), BF16),
         pltpu.SemaphoreType.DMA((A_SLOTS_OUT,))], {}, job)
    return mixed, amat, extra


def _sb_bwd(qkv, dmixed, amat, job=None):
    def body(ins, outs, scr, comm):
        (qkv_hbm, do_hbm, a_hbm), (dqkv_hbm,), (qkv_v, do_v, dq_v, dk_s, dv_s, sems, kt_v, a_st, a_sems, w_v) = ins, outs, scr
        sem = sems.at[0]
        w_v[...] = jnp.zeros_like(w_v)

        def a_copy(t):
            slot = t % A_SLOTS_IN
            return pltpu.make_async_copy(a_hbm.at[t], a_st.at[slot], a_sems.at[slot])

        later = [pltpu.make_async_copy(do_hbm.at[:, pl.ds(0, 2 * BQ)], do_v, sems.at[1])]
        for cp in later:
            cp.start()
        for t in range(A_AHEAD):
            a_copy(t).start()
        _copy_in(qkv_hbm.at[:, pl.ds(0, 2 * PAIRW)], qkv_v, sem)
        st = comm[0]() if comm else None
        _transpose_slab(qkv_v, kt_v, lambda p: p * PAIRW + BQ)
        for cp in later:
            cp.wait()
        m0, m1 = _lane_masks()
        first = _iota((BQ, BQ), 1) < 64
        r0, r1 = _row_masks()
        valid = _valid_t(True)
        earlier = _tri_blockdiag(False)
        dk_s[...] = jnp.zeros_like(dk_s)
        dv_s[...] = jnp.zeros_like(dv_s)

        def fetch(i, j):
            t = (i * (i + 1)) // 2 + j
            a_copy(t).wait()

            @pl.when(t + A_AHEAD < NTRI)
            def _():
                a_copy(t + A_AHEAD).start()

        def steps(i, j, cs, diag, fetched=False):
            if not fetched:
                fetch(i, j)
            slot = ((i * (i + 1)) // 2 + j) % A_SLOTS_IN
            kv =[_kv_tiles(qkv_v, p, b, j) for p, b in CHAINS]
            zd = [_dot(jnp.concatenate([_stack(kv[c][0], m0, m1), _stack(kv[c][1], m0, m1)], axis=1), w_v[c, 0])
                  for c in range(NC)]
            zs = [x[:, :BQ] for x in zd]
            das = [x[:, BQ:] for x in zd]
            avs = [a_st[slot, c] for c in range(NC)]
            gms = [das[c] * avs[c].astype(F32) for c in range(NC)]
            befores = [_dot(earlier, gms[c].astype(BF16)) for c in range(NC)]
            dzbs = []
            for c in range(NC):
                dz = gms[c] - jax.nn.sigmoid(zs[c]) * (gms[c] + befores[c] + _bcast_heads(*cs[c][0]))
                if diag:
                    dz = jnp.where(valid, dz, 0.0)
                dzbs.append(dz.astype(BF16))
            out = []
            for c, (p, b) in enumerate(CHAINS):
                dq = cs[c][1] + _dot(_stack_t(kt_v[p, :, _rows(b, j)], r0, r1), dzbs[c])
                kd = _dot(jnp.concatenate([dzbs[c], avs[c]], axis=1), w_v[c, 1])
                dk_s[p, _rows(b, j), :] += _merge_heads(kd[:, :BQ], first)
                dv_s[p, _rows(b, j), :] += _merge_heads(kd[:, BQ:], first)
                g0, g1 = _colsum2(gms[c])
                out.append(((cs[c][0][0] + g0, cs[c][0][1] + g1), dq))
            return tuple(out)

        def qblock(i, _):
            for c, (p, b) in enumerate(CHAINS):
                qn = _q_tile(qkv_v, p, b, i)
                dn = do_v[_rows(b, i), p * BQ:(p + 1) * BQ]
                for r, (x, y) in enumerate(((_tr(qn), _tr(dn)), (qn, dn))):
                    w_v[c, r, :BQ, :BQ] = x
                    w_v[c, r, BQ:, BQ:] = y
            zr = jnp.zeros((1, BQ), F32)
            cs = (((zr, zr), jnp.zeros((BQ, BQ), F32)),) * NC
            def two(n, cs):
                fetch(i, 2 * n)
                fetch(i, 2 * n + 1)
                return steps(i, 2 * n + 1, steps(i, 2 * n, cs, False, True), False, True)

            cs = lax.fori_loop(0, i // 2, two, cs)
            cs = lax.fori_loop(i - i % 2, i, lambda j, cs: steps(i, j, cs, False), cs)
            cs = steps(i, i, cs, True)
            for c, (p, b) in enumerate(CHAINS):
                dq_v[_rows(b, i), p * PAIRW:p * PAIRW + BQ] = (cs[c][1].T * SCALE).astype(BF16)
            return 0

        lax.fori_loop(0, NB, qblock, 0)
        for p in range(2):
            dq_v[:, p * PAIRW + BQ:p * PAIRW + 2 * BQ] = dk_s[p].astype(BF16)
            dq_v[:, p * PAIRW + 2 * BQ:p * PAIRW + 3 * BQ] = dv_s[p].astype(BF16)
        _copy_in(dq_v, dqkv_hbm.at[:, pl.ds(0, 2 * PAIRW)], sem)
        if comm:
            comm[1](st)

    (dqkv,), extra = _host_call(
        body, "sb_bwd", [qkv, dmixed, amat], [ANY_SPEC, ANY_SPEC, ANY_SPEC],
        [jax.ShapeDtypeStruct((T, QKVW), BF16)], [ANY_SPEC],
        [SLAB_QKV, SLAB_OUT, SLAB_QKV, ACC_KV, ACC_KV, pltpu.SemaphoreType.DMA((4,)), SLAB_T,
         pltpu.VMEM((A_SLOTS_IN, NC, HB, BQ), BF16), pltpu.SemaphoreType.DMA((A_SLOTS_IN,)), PAIR_DIAG], {}, job)
    return dqkv, extra


def _flash_fwd(qkv, mixed, g, fox, bias, job=None):
    def body(ins, outs, scr, comm):
        if fox:
            qkv_hbm, cq_ref, ckb_hbm, _ = ins
            (o_hbm, lse_ref, o32_hbm), (qkv_v, o_v, sem, vt_v, o32_v, ckb_v) = outs, scr
        else:
            qkv_hbm, tbl_ref, _ = ins
            (o_hbm, lse_ref), (qkv_v, o_v, sem, vt_v) = outs, scr
        sems = sem
        sem = sems.at[0]
        later = [pltpu.make_async_copy(ckb_hbm, ckb_v, sems.at[1])] if fox else []
        for cp in later:
            cp.start()
        _copy_in(qkv_hbm.at[:, pl.ds(g * 2 * PAIRW, 2 * PAIRW)], qkv_v, sem)
        st = comm[0]() if comm else None
        _transpose_slab(qkv_v, vt_v, lambda p: p * PAIRW + 2 * BQ)
        for cp in later:
            cp.wait()
        m0, m1 = _lane_masks()
        r0, r1 = _row_masks()
        valid = _valid_t(False)

        def steps(qts, cqs, i, j, cs, diag):
            ks = [_stack(_kv_tiles(qkv_v, p, b, j)[0], m0, m1) for p, b in CHAINS]
            zs = [_dot(ks[c], qts[c]) for c in range(NC)]
            prs, alphas, out = [], [], []
            for c, (p, b) in enumerate(CHAINS):
                (ma, mb), (la, lb_), _ = cs[c]
                if fox:
                    kk = pl.ds(pl.multiple_of(j * BQ, BQ), BQ)
                    col = b * NH + 2 * p
                    z = zs[c] + (cqs[c] - jnp.concatenate([ckb_v[col, kk, :], ckb_v[col + 1, kk, :]], axis=0))
                    if diag:
                        z = jnp.where(valid, z, NEG)
                else:
                    z = zs[c] + tbl_ref[p, i - j]
                za, zb = _heads(z)
                na = jnp.maximum(ma, jnp.max(za, axis=0, keepdims=True))
                nb = jnp.maximum(mb, jnp.max(zb, axis=0, keepdims=True))
                aa, ab = jnp.exp(ma - na), jnp.exp(mb - nb)
                pr = jnp.exp(z - _bcast_heads(na, nb))
                sa, sb = _colsum2(pr)
                prs.append(_split2(pr) if fox else pr.astype(BF16))
                alphas.append((aa, ab))
                out.append(((na, nb), (aa * la + sa, ab * lb_ + sb)))
            pvs = []
            for c, (p, b) in enumerate(CHAINS):
                vts = _stack_t(vt_v[p, :, _rows(b, j)], r0, r1)
                if fox:
                    pvs.append(_dot(vts, prs[c][:, :BQ]) + _dot(vts, prs[c][:, BQ:]))
                else:
                    pvs.append(_dot(vts, prs[c]))
            return tuple((out[c][0], out[c][1], _by_channel(*alphas[c]) * cs[c][2] + pvs[c]) for c in range(NC))

        def qblock(i, _):
            qts = [_tr(_q_tile(qkv_v, p, b, i)) for p, b in CHAINS]
            if fox:
                cqs = [_bcast_heads(_stat_row(cq_ref, p, b, 0, i), _stat_row(cq_ref, p, b, 1, i)) for p, b in CHAINS]
            else:
                cqs = [None] * NC
            ng = jnp.full((1, BQ), NEG, F32)
            zr = jnp.zeros((1, BQ), F32)
            cs = steps(qts, cqs, i, i, (((ng, ng), (zr, zr), jnp.zeros((BQ, BQ), F32)),) * NC, True)
            cs = _blocks_loop(i, FLASH_PER_TRIP, lambda k, cs: steps(qts, cqs, i, i - 1 - k, cs, False), cs)
            for c, (p, b) in enumerate(CHAINS):
                (ma, mb), (la, lb_), acc = cs[c]
                o = (acc / _by_channel(la, lb_)).T
                o_v[_rows(b, i), p * BQ:(p + 1) * BQ] = o.astype(BF16)
                if fox:
                    o32_v[_rows(b, i), p * BQ:(p + 1) * BQ] = o
                _put_row(lse_ref, p, b, 0, i, ma + jnp.log(la))
                _put_row(lse_ref, p, b, 1, i, mb + jnp.log(lb_))
            return 0

        lax.fori_loop(0, NB, qblock, 0)
        _copy_in(o_v, o_hbm.at[:, pl.ds(g * 2 * BQ, 2 * BQ)], sem)
        if fox:
            _copy_in(o32_v, o32_hbm, sem)
        if comm:
            comm[1](st)

    bias_specs = [VMEM_SPEC, ANY_SPEC] if fox else [VMEM_SPEC]
    n_in = 2 + len(bias_specs)
    o32 = [jax.ShapeDtypeStruct((T, 2 * BQ), F32)] if fox else []
    res, extra = _host_call(
        body, "fox_fwd" if fox else "dil_fwd", [qkv, *bias, mixed], [ANY_SPEC] + bias_specs + [ANY_SPEC],
        [jax.ShapeDtypeStruct((T, D), BF16), ROWS_SHAPE] + o32, [ANY_SPEC, VMEM_SPEC] + [ANY_SPEC] * len(o32),
        [SLAB_QKV, SLAB_OUT, pltpu.SemaphoreType.DMA((4,)), SLAB_T] + ([SLAB_O32, SLAB_KEYB] if fox else []),
        {n_in - 1: 0}, job)
    return (*res, extra)


def _flash_bwd(qkv, o, dmixed, lse, dqkv, g, fox, bias, job=None):
    def body(ins, outs, scr, comm):
        if fox:
            qkv_hbm, o_hbm, do_hbm, lse_ref, cq_ref, ckb_hbm, _ = ins
            (dqkv_hbm, db_ref), (qkv_v, o_v, do_v, dq_v, dk_s, dv_s, sem, kt_v, w_v, ckb_v, dc_s) = outs, scr
        else:
            qkv_hbm, o_hbm, do_hbm, lse_ref, tbl_ref, _ = ins
            (dqkv_hbm, db_ref), (qkv_v, o_v, do_v, dq_v, dk_s, dv_s, sem, kt_v, w_v) = outs, scr
        w_v[...] = jnp.zeros_like(w_v)
        sems = sem
        sem = sems.at[0]
        later = [pltpu.make_async_copy(do_hbm.at[:, pl.ds(g * 2 * BQ, 2 * BQ)], do_v, sems.at[1])]
        if fox:
            later += [pltpu.make_async_copy(o_hbm, o_v, sems.at[2]), pltpu.make_async_copy(ckb_hbm, ckb_v, sems.at[3])]
        else:
            later += [pltpu.make_async_copy(o_hbm.at[:, pl.ds(g * 2 * BQ, 2 * BQ)], o_v, sems.at[2])]
        for cp in later:
            cp.start()
        _copy_in(qkv_hbm.at[:, pl.ds(g * 2 * PAIRW, 2 * PAIRW)], qkv_v, sem)
        st = comm[0]() if comm else None
        _transpose_slab(qkv_v, kt_v, lambda p: p * PAIRW + BQ)
        for cp in later:
            cp.wait()
        m0, m1 = _lane_masks()
        first = _iota((BQ, BQ), 1) < 64
        r0, r1 = _row_masks()
        valid = _valid_t(False)
        dk_s[...] = jnp.zeros_like(dk_s)
        dv_s[...] = jnp.zeros_like(dv_s)
        if fox:
            dc_s[...] = jnp.zeros_like(dc_s)
        else:
            db_ref[...] = jnp.zeros_like(db_ref)

        def steps(cqs, lses, deltas, i, j, dqs, diag):
            kv = [_kv_tiles(qkv_v, p, b, j) for p, b in CHAINS]
            zd = [_dot(jnp.concatenate([_stack(kv[c][0], m0, m1), _stack(kv[c][1], m0, m1)], axis=1), w_v[c, 0])
                  for c in range(NC)]
            zs = [x[:, :BQ] for x in zd]
            dps = [x[:, BQ:] for x in zd]
            prs, dzl = [], []
            for c, (p, b) in enumerate(CHAINS):
                if fox:
                    kk = pl.ds(pl.multiple_of(j * BQ, BQ), BQ)
                    col = b * NH + 2 * p
                    z = zs[c] + (cqs[c] - jnp.concatenate([ckb_v[col, kk, :], ckb_v[col + 1, kk, :]], axis=0))
                    if diag:
                        z = jnp.where(valid, z, NEG)
                else:
                    z = zs[c] + tbl_ref[p, i - j]
                pr = jnp.exp(z - lses[c])
                prs.append(pr.astype(BF16))
                dzl.append(pr * (dps[c] - deltas[c]))
            dzbs = [dz.astype(BF16) for dz in dzl]
            new = []
            for c, (p, b) in enumerate(CHAINS):
                new.append(dqs[c] + _dot(_stack_t(kt_v[p, :, _rows(b, j)], r0, r1), dzbs[c]))
                kd = _dot(jnp.concatenate([dzbs[c], prs[c]], axis=1), w_v[c, 1])
                dk_s[p, _rows(b, j), :] += _merge_heads(kd[:, :BQ], first)
                dv_s[p, _rows(b, j), :] += _merge_heads(kd[:, BQ:], first)
                if fox:
                    dc_s[c, pl.ds(pl.multiple_of(j * HB, HB), HB), :] += dzl[c]
            if not fox:
                for p in range(2):
                    db_ref[p, i - j] = db_ref[p, i - j] + (dzl[2 * p] + dzl[2 * p + 1])
            return tuple(new)

        def qblock(i, _):
            qns = [_q_tile(qkv_v, p, b, i) for p, b in CHAINS]
            dns = [do_v[_rows(b, i), p * BQ:(p + 1) * BQ] for p, b in CHAINS]
            for c in range(NC):
                for r, (x, y) in enumerate(((_tr(qns[c]), _tr(dns[c])), (qns[c], dns[c]))):
                    w_v[c, r, :BQ, :BQ] = x
                    w_v[c, r, BQ:, BQ:] = y
            lses = [_bcast_heads(_stat_row(lse_ref, p, b, 0, i), _stat_row(lse_ref, p, b, 1, i)) for p, b in CHAINS]
            if fox:
                cqs = [_bcast_heads(_stat_row(cq_ref, p, b, 0, i), _stat_row(cq_ref, p, b, 1, i)) for p, b in CHAINS]
            else:
                cqs = [None] * NC
            deltas = []
            for c, (p, b) in enumerate(CHAINS):
                pt = (dns[c].astype(F32) * o_v[_rows(b, i), p * BQ:(p + 1) * BQ].astype(F32)).T
                deltas.append(_bcast_heads(jnp.sum(pt[:64], axis=0, keepdims=True), jnp.sum(pt[64:], axis=0, keepdims=True)))
            dqs = (jnp.zeros((BQ, BQ), F32),) * NC
            dqs = _blocks_loop(i, FLASH_PER_TRIP, lambda j, d: steps(cqs, lses, deltas, i, j, d, False), dqs)
            dqs = steps(cqs, lses, deltas, i, i, dqs, True)
            for c, (p, b) in enumerate(CHAINS):
                dq_v[_rows(b, i), p * PAIRW:p * PAIRW + BQ] = (dqs[c].T * SCALE).astype(BF16)
            return 0

        lax.fori_loop(0, NB, qblock, 0)
        for p in range(2):
            dq_v[:, p * PAIRW + BQ:p * PAIRW + 2 * BQ] = dk_s[p].astype(BF16)
            dq_v[:, p * PAIRW + 2 * BQ:p * PAIRW + 3 * BQ] = dv_s[p].astype(BF16)
        _copy_in(dq_v, dqkv_hbm.at[:, pl.ds(g * 2 * PAIRW, 2 * PAIRW)], sem)
        if fox:
            lane = _iota((BQ, NSTAT), 1)

            def fold(n, _):
                t = jnp.zeros((BQ, NSTAT), F32)
                for c, (p, b) in enumerate(CHAINS):
                    s = jnp.sum(dc_s[c, pl.ds(pl.multiple_of(n * HB, HB), HB), :], axis=1, keepdims=True)
                    col = b * NH + 2 * p
                    t = t - jnp.where(lane == col, s[:BQ], 0.0) - jnp.where(lane == col + 1, s[BQ:], 0.0)
                db_ref[pl.ds(pl.multiple_of(n * BQ, BQ), BQ), :] = t
                return 0

            lax.fori_loop(0, NB, fold, 0)
        if comm:
            comm[1](st)

    if fox:
        bias_specs = [VMEM_SPEC, ANY_SPEC]
        db_shape = jax.ShapeDtypeStruct((S, NSTAT), F32)
        more = [SLAB_KEYB, pltpu.VMEM((NC, NB * HB, BQ), F32)]
    else:
        bias_specs = [VMEM_SPEC]
        db_shape = jax.ShapeDtypeStruct((2, NB, HB, BQ), F32)
        more = []
    n_in = 5 + len(bias_specs)
    (dqkv, db), extra = _host_call(
        body, "fox_bwd" if fox else "dil_bwd", [qkv, o, dmixed, lse, *bias, dqkv],
        [ANY_SPEC, ANY_SPEC, ANY_SPEC, VMEM_SPEC] + bias_specs + [ANY_SPEC],
        [jax.ShapeDtypeStruct((T, QKVW), BF16), db_shape], [ANY_SPEC, VMEM_SPEC],
        [SLAB_QKV, SLAB_O32 if fox else SLAB_OUT, SLAB_OUT, SLAB_QKV, ACC_KV, ACC_KV, pltpu.SemaphoreType.DMA((4,)), SLAB_T,
         PAIR_DIAG] + more, {n_in - 1: 0}, job)
    return dqkv, db, extra


def _delta_t(d):
    return d * BQ + _iota((HB, BQ), 1) - (_iota((HB, BQ), 0) & (BQ - 1))


def _buckets_in(d):
    lo, hi = max(d * BQ - (BQ - 1), 0), d * BQ + BQ - 1
    return [b for b in range(32) if BUCKET_TH[b] <= hi and (b == 31 or BUCKET_TH[b + 1] > lo)]


def _in_bucket(delta, b):
    m = delta >= BUCKET_TH[b]
    return m if b == 31 else m & (delta < BUCKET_TH[b + 1])


def _dil_table(rel_bias, job=None):
    def body(ins, outs, scr, comm):
        (rb_ref,), (o_ref,) = ins, outs
        st = comm[0]() if comm else None
        for d in range(NB):
            delta = _delta_t(d)
            pos = delta >= 0
            n = ((pos & (delta <= 128)).astype(jnp.int32)
                 + (pos & (delta <= 512) & ((delta & 3) == 0)).astype(jnp.int32)
                 + (pos & ((delta & 15) == 0)).astype(jnp.int32))
            logn = jnp.where(n == 3, math.log(3.0), jnp.where(n == 2, math.log(2.0), jnp.where(n == 1, 0.0, NEG)))
            head1 = _iota((HB, BQ), 0) >= BQ
            for p in range(2):
                val = jnp.zeros((HB, BQ), F32)
                for b in _buckets_in(d):
                    val = jnp.where(_in_bucket(delta, b), jnp.where(head1, rb_ref[b, 2 * p + 1], rb_ref[b, 2 * p]), val)
                o_ref[p, d] = val + logn
        if comm:
            comm[1](st)

    (tbl,), extra = _host_call(
        body, "dil_table", [rel_bias], [pl.BlockSpec(memory_space=pltpu.SMEM)],
        [jax.ShapeDtypeStruct((2, NB, HB, BQ), F32)], [VMEM_SPEC], [], {}, job)
    return (tbl, extra) if job else tbl


def _dil_table_bwd(dtbl):
    def body(dt_ref, o_ref):
        p = pl.program_id(0)
        rowi = _iota((32, BQ), 0)
        lanei = _iota((32, BQ), 1)

        @pl.when(p == 0)
        def _():
            o_ref[...] = jnp.zeros_like(o_ref)

        out = jnp.zeros((32, BQ), F32)
        for b in range(32):
            acc = None
            for d in range(NB):
                if b in _buckets_in(d):
                    t = jnp.where(_in_bucket(_delta_t(d), b), dt_ref[d], 0.0)
                    acc = t if acc is None else acc + t
            rs = jnp.sum(acc, axis=1, keepdims=True)
            s0 = jnp.sum(rs[:BQ], axis=0, keepdims=True)
            s1 = jnp.sum(rs[BQ:], axis=0, keepdims=True)
            out = (out + jnp.where((rowi == b) & (lanei == 2 * p), s0, 0.0)
                   + jnp.where((rowi == b) & (lanei == 2 * p + 1), s1, 0.0))
        o_ref[...] += out

    return pl.pallas_call(
        body, name="dil_table_bwd", grid=(2,),
        in_specs=[pl.BlockSpec((None, NB, HB, BQ), lambda p: (p, 0, 0, 0))],
        out_specs=pl.BlockSpec((32, BQ), lambda p: (0, 0)),
        out_shape=jax.ShapeDtypeStruct((32, BQ), F32),
        compiler_params=_cp(("arbitrary",)))(dtbl)


def _fox_prep(gate, fb):
    def body(g_ref, fb_ref, c_ref):
        tri = (_iota((BQ, BQ), 0) >= _iota((BQ, BQ), 1)).astype(BF16)

        def blk(i, carry):
            r0 = pl.multiple_of(i * BQ, BQ)
            lf = _log_sigmoid(g_ref[pl.ds(r0, BQ), :] + fb_ref[...])
            c = _dot(tri, _split3(lf))
            c_ref[pl.ds(r0, BQ), :] = c[:, 0:BQ] + c[:, BQ:2 * BQ] + c[:, 2 * BQ:3 * BQ] + carry
            return carry + jnp.sum(lf, axis=0, keepdims=True)

        lax.fori_loop(0, NB, blk, jnp.zeros((1, BQ), F32))

    blk = pl.BlockSpec((S, GATEW), lambda b: (b, 0))
    return pl.pallas_call(
        body, name="fox_prep", grid=(BL,), in_specs=[blk, pl.BlockSpec((1, GATEW), lambda b: (0, 0))],
        out_specs=blk, out_shape=jax.ShapeDtypeStruct((T, GATEW), F32),
        compiler_params=_cp(("parallel",)))(gate, fb)


def _fox_post(dcum, gate, fb):
    def body(dc_ref, g_ref, fb_ref, dg_ref, dfb_ref):
        b = pl.program_id(0)
        tri = (_iota((BQ, BQ), 0) <= _iota((BQ, BQ), 1)).astype(BF16)

        def blk(ii, carry):
            csum, dfb = carry
            r0 = pl.multiple_of((NB - 1 - ii) * BQ, BQ)
            dc = dc_ref[pl.ds(r0, BQ), :]
            c = _dot(tri, _split3(dc))
            dlf = c[:, 0:BQ] + c[:, BQ:2 * BQ] + c[:, 2 * BQ:3 * BQ] + csum
            dg = dlf * jnp.exp(_log_sigmoid(-(g_ref[pl.ds(r0, BQ), :] + fb_ref[...])))
            dg_ref[pl.ds(r0, BQ), :] = dg
            return csum + jnp.sum(dc, axis=0, keepdims=True), dfb + jnp.sum(dg, axis=0, keepdims=True)

        z = jnp.zeros((1, BQ), F32)
        _, dfb = lax.fori_loop(0, NB, blk, (z, z))

        @pl.when(b == 0)
        def _():
            dfb_ref[...] = dfb

        @pl.when(b > 0)
        def _():
            dfb_ref[...] += dfb

    blk = pl.BlockSpec((S, GATEW), lambda b: (b, 0))
    vec = pl.BlockSpec((1, GATEW), lambda b: (0, 0))
    return pl.pallas_call(
        body, name="fox_post", grid=(BL,), in_specs=[blk, blk, vec], out_specs=[blk, vec],
        out_shape=[jax.ShapeDtypeStruct((T, GATEW), F32), jax.ShapeDtypeStruct((1, GATEW), F32)],
        compiler_params=_cp(("arbitrary",)))(dcum, gate, fb)


def _shift_down(x, n):
    return jnp.where(_iota(x.shape, 0) >= n, pltpu.roll(x, n, 0), 0.0)


def _shift_up(x, n):
    return jnp.where(_iota(x.shape, 0) < S - n, pltpu.roll(x, S - n, 0), 0.0)


def _conv_fwd(conv, cw, mixed):
    W = 256

    def body(c_ref, w_ref, _, o_ref):
        u = c_ref[:, W:2 * W] * c_ref[:, 2 * W:3 * W]
        y = w_ref[0:1, :] * _shift_down(u, 2) + w_ref[1:2, :] * _shift_down(u, 1) + w_ref[2:3, :] * u
        o_ref[...] = (c_ref[:, 0:W] * y).astype(BF16)

    return pl.pallas_call(
        body, name="conv_fwd", grid=(BL,),
        in_specs=[pl.BlockSpec((S, CONVW), lambda b: (b, 0)), pl.BlockSpec((8, W), lambda b: (0, 0)), ANY_SPEC],
        out_specs=pl.BlockSpec((S, W), lambda b: (b, 3)),
        out_shape=jax.ShapeDtypeStruct((T, D), BF16), input_output_aliases={2: 0},
        compiler_params=_cp(("parallel",)))(conv, cw, mixed)


def _conv_bwd(conv, cw, dmixed):
    W = 256

    def body(c_ref, w_ref, do_ref, dc_ref, dw_ref):
        b = pl.program_id(0)
        bg = c_ref[:, 0:W]
        cg = c_ref[:, W:2 * W]
        hv = c_ref[:, 2 * W:3 * W]
        do = do_ref[...].astype(F32)
        u = cg * hv
        u1 = _shift_down(u, 1)
        u2 = _shift_down(u, 2)
        y = w_ref[0:1, :] * u2 + w_ref[1:2, :] * u1 + w_ref[2:3, :] * u
        dy = do * bg
        du = w_ref[2:3, :] * dy + w_ref[1:2, :] * _shift_up(dy, 1) + w_ref[0:1, :] * _shift_up(dy, 2)
        dc_ref[:, 0:W] = (do * y).astype(BF16)
        dc_ref[:, W:2 * W] = (du * hv).astype(BF16)
        dc_ref[:, 2 * W:3 * W] = (du * cg).astype(BF16)
        rowi = _iota((8, W), 0)
        dw = (jnp.where(rowi == 0, jnp.sum(dy * u2, axis=0, keepdims=True), 0.0)
              + jnp.where(rowi == 1, jnp.sum(dy * u1, axis=0, keepdims=True), 0.0)
              + jnp.where(rowi == 2, jnp.sum(dy * u, axis=0, keepdims=True), 0.0))

        @pl.when(b == 0)
        def _():
            dw_ref[...] = dw

        @pl.when(b > 0)
        def _():
            dw_ref[...] += dw

    return pl.pallas_call(
        body, name="conv_bwd", grid=(BL,),
        in_specs=[pl.BlockSpec((S, CONVW), lambda b: (b, 0)), pl.BlockSpec((8, W), lambda b: (0, 0)),
                  pl.BlockSpec((S, W), lambda b: (b, 3))],
        out_specs=[pl.BlockSpec((S, CONVW), lambda b: (b, 0)), pl.BlockSpec((8, W), lambda b: (0, 0))],
        out_shape=[jax.ShapeDtypeStruct((T, CONVW), BF16), jax.ShapeDtypeStruct((8, W), F32)],
        compiler_params=_cp(("arbitrary",)))(conv, cw, dmixed)


def _place():
    x, y, c = lax.axis_index("x"), lax.axis_index("y"), lax.axis_index("c")
    return x, y, c


def _chips_of(x, y):
    return [(1 - x, y), (x, 1 - y), (1 - x, 1 - y)]


def _dev(p):
    return 4 * p[0] + 2 * p[1] + p[2]


def _gather_job_a(shards):
    n = len(shards)

    def peers(x, y, c):
        return [(x, y, 1 - c)] + [(*chip, c) for chip in _chips_of(x, y)]

    def start(ins, outs, sems):
        send, recv, loc = sems
        x, y, c = _place()
        me = (x, y, c)
        cps = []
        for a in range(n):
            cps.append(pltpu.make_async_copy(ins[a], outs[a].at[_dev(me)], loc.at[a]))
            for k, peer in enumerate(peers(x, y, c)):
                cps.append(pltpu.make_async_remote_copy(
                    src_ref=ins[a], dst_ref=outs[a].at[_dev(me)], send_sem=send.at[a, k], recv_sem=recv.at[a, k],
                    device_id=peer, device_id_type=MESH))
        for cp in cps:
            cp.start()
        return cps

    def finish(cps, ins, outs, sems):
        send, recv, loc = sems
        x, y, c = _place()
        for a in range(n):
            for k, peer in enumerate(peers(x, y, c)):
                pltpu.make_async_remote_copy(
                    src_ref=ins[a], dst_ref=outs[a].at[_dev(peer)], send_sem=send.at[a, k], recv_sem=recv.at[a, k],
                    device_id=(x, y, c), device_id_type=MESH).wait_recv()
        for a in range(n):
            cps[5 * a].wait()
            for k in range(4):
                cps[5 * a + 1 + k].wait_send()

    return _Job(shards, [jax.ShapeDtypeStruct((NDEV,) + s.shape, s.dtype) for s in shards], {},
                [pltpu.SemaphoreType.DMA((n, 4)), pltpu.SemaphoreType.DMA((n, 4)), pltpu.SemaphoreType.DMA((n,))],
                start, finish)


def _gather_job_b(gathered):
    n = len(gathered)

    def start(ins, outs, sems):
        send, recv = sems
        x, y, c = _place()
        cps = []
        for a in range(n):
            for j, chip in enumerate(_chips_of(x, y)):
                blk = outs[a].at[_dev((*chip, c))]
                cps.append(pltpu.make_async_remote_copy(
                    src_ref=blk, dst_ref=blk, send_sem=send.at[a, j], recv_sem=recv.at[a, j],
                    device_id=(x, y, 1 - c), device_id_type=MESH))
        for cp in cps:
            cp.start()
        return cps

    def finish(cps, ins, outs, sems):
        send, recv = sems
        x, y, c = _place()
        for a in range(n):
            for j, chip in enumerate(_chips_of(x, y)):
                blk = outs[a].at[_dev((*chip, 1 - c))]
                pltpu.make_async_remote_copy(
                    src_ref=blk, dst_ref=blk, send_sem=send.at[a, j], recv_sem=recv.at[a, j],
                    device_id=(x, y, c), device_id_type=MESH).wait_recv()
        for cp in cps:
            cp.wait_send()

    return _Job(gathered, [jax.ShapeDtypeStruct(g.shape, g.dtype) for g in gathered], {a: a for a in range(n)},
                [pltpu.SemaphoreType.DMA((n, 3)), pltpu.SemaphoreType.DMA((n, 3))], start, finish)


def _sibling_job(grads):
    n = len(grads)

    def start(ins, outs, sems):
        send, recv = sems
        x, y, c = _place()
        cps = [pltpu.make_async_remote_copy(
            src_ref=ins[a].at[:, 1 - c], dst_ref=outs[a], send_sem=send.at[a], recv_sem=recv.at[a],
            device_id=(x, y, 1 - c), device_id_type=MESH) for a in range(n)]
        for cp in cps:
            cp.start()
        return cps

    def finish(cps, ins, outs, sems):
        for cp in cps:
            cp.wait()

    return _Job(grads, [jax.ShapeDtypeStruct(g.shape[:1] + g.shape[2:], F32) for g in grads], {},
                [pltpu.SemaphoreType.DMA((n,)), pltpu.SemaphoreType.DMA((n,))], start, finish)


def _chip_job(psums):
    n = len(psums)

    def copies(ins, outs, sems):
        send, recv, loc = sems
        x, y, c = _place()
        mychip = 2 * x + y
        cps = []
        for a in range(n):
            cps.append(pltpu.make_async_copy(ins[a].at[mychip], outs[a].at[mychip], loc.at[a]))
            for j, chip in enumerate(_chips_of(x, y)):
                cps.append(pltpu.make_async_remote_copy(
                    src_ref=ins[a].at[2 * chip[0] + chip[1]], dst_ref=outs[a].at[mychip],
                    send_sem=send.at[a, j], recv_sem=recv.at[a, j], device_id=(*chip, c), device_id_type=MESH))
        return cps

    def start(ins, outs, sems):
        for cp in copies(ins, outs, sems):
            cp.start()

    def finish(_, ins, outs, sems):
        cps = copies(ins, outs, sems)
        send, recv, loc = sems
        x, y, c = _place()
        mychip = 2 * x + y
        for a in range(n):
            for j, chip in enumerate(_chips_of(x, y)):
                pltpu.make_async_remote_copy(
                    src_ref=ins[a].at[mychip], dst_ref=outs[a].at[2 * chip[0] + chip[1]],
                    send_sem=send.at[a, j], recv_sem=recv.at[a, j], device_id=(x, y, c), device_id_type=MESH).wait_recv()
        for a in range(n):
            cps[4 * a].wait()
            for j in range(3):
                cps[4 * a + 1 + j].wait_send()

    return _Job(psums, [jax.ShapeDtypeStruct(p.shape, BF16) for p in psums], {},
                [pltpu.SemaphoreType.DMA((n, 3)), pltpu.SemaphoreType.DMA((n, 3)), pltpu.SemaphoreType.DMA((n,))],
                start, finish)


def _join_jobs(*jobs):
    jobs = [j for j in jobs if j is not None]
    if len(jobs) <= 1:
        return jobs[0] if jobs else None
    cut = lambda seq, sizes: [seq[sum(sizes[:k]):sum(sizes[:k + 1])] for k in range(len(sizes))]
    n_in = [len(j.ins) for j in jobs]
    n_out = [len(j.out_shapes) for j in jobs]
    n_sem = [len(j.sems) for j in jobs]
    aliases = {}
    for k, j in enumerate(jobs):
        for a, b in j.aliases.items():
            aliases[sum(n_in[:k]) + a] = sum(n_out[:k]) + b

    def start(ins, outs, sems):
        return [j.start(i, o, s) for j, i, o, s in zip(jobs, cut(ins, n_in), cut(outs, n_out), cut(sems, n_sem))]

    def finish(sts, ins, outs, sems):
        for j, st, i, o, s in zip(jobs, sts, cut(ins, n_in), cut(outs, n_out), cut(sems, n_sem)):
            j.finish(st, i, o, s)

    return _Job([t for j in jobs for t in j.ins], [t for j in jobs for t in j.out_shapes], aliases,
                [t for j in jobs for t in j.sems], start, finish)


def _run_job(job, name):
    def body(ins, outs, scr, comm):
        comm[1](comm[0]())

    return _host_call(body, name, [], [], [], [], [], {}, job)[1]


def _allreduce_small(v, job=None):
    def body(ins, outs, scr, comm):
        (v_ref,), (o_ref,), (slots, send_sems, recv_sems) = ins, outs, scr
        st = comm[0]() if comm else None
        x, y, c = _place()
        me = 4 * x + 2 * y + c
        slots[me] = v_ref[...]

        def copy(k):
            peer = (x ^ ((k >> 2) & 1), y ^ ((k >> 1) & 1), c ^ (k & 1))
            return pltpu.make_async_remote_copy(
                src_ref=v_ref, dst_ref=slots.at[me], send_sem=send_sems.at[k - 1], recv_sem=recv_sems.at[k - 1],
                device_id=peer, device_id_type=MESH)

        def arrival(k):
            return pltpu.make_async_remote_copy(
                src_ref=v_ref, dst_ref=slots.at[me ^ k], send_sem=send_sems.at[k - 1], recv_sem=recv_sems.at[k - 1],
                device_id=(x, y, c), device_id_type=MESH)

        sends = [copy(k) for k in range(1, NDEV)]
        for cp in sends:
            cp.start()
        for k in range(1, NDEV):
            arrival(k).wait_recv()
        for cp in sends:
            cp.wait_send()
        acc = slots[0]
        for d in range(1, NDEV):
            acc = acc + slots[d]
        o_ref[...] = acc
        if comm:
            comm[1](st)

    (out,), extra = _host_call(
        body, "allreduce_small", [v], [VMEM_SPEC], [jax.ShapeDtypeStruct(v.shape, F32)], [VMEM_SPEC],
        [pltpu.VMEM((NDEV,) + v.shape, F32), pltpu.SemaphoreType.DMA((NDEV - 1,)),
         pltpu.SemaphoreType.DMA((NDEV - 1,))], {}, job)
    return (out, extra) if job else out


def _pair_sums(views, gots, core):
    n = len(views)

    def body(c_ref, *refs):
        for a in range(n):
            refs[2 * n + a][...] = (refs[a][...] + refs[n + a][...]).astype(BF16)

    def vspec(v):
        return pl.BlockSpec((None, None, v.shape[2] // 2, v.shape[3]), lambda k, h, c: (k, c[0], h, 0))

    def gspec(g):
        return pl.BlockSpec((None, g.shape[1] // 2, g.shape[2]), lambda k, h, c: (k, h, 0))

    return pl.pallas_call(
        body, name="pair_sums",
        grid_spec=pltpu.PrefetchScalarGridSpec(
            num_scalar_prefetch=1, grid=(4, 2),
            in_specs=[vspec(v) for v in views] + [gspec(g) for g in gots],
            out_specs=[gspec(g) for g in gots]),
        out_shape=[jax.ShapeDtypeStruct(g.shape, BF16) for g in gots],
        compiler_params=_cp(("parallel", "parallel")))(core, *views, *gots)


def _chip_sums(parts):
    n = len(parts)

    def body(*refs):
        for a in range(n):
            acc = refs[a][0].astype(F32)
            for k in range(1, 4):
                acc = acc + refs[a][k].astype(F32)
            refs[n + a][...] = acc

    return pl.pallas_call(
        body, name="chip_sums", in_specs=[VMEM_SPEC] * n, out_specs=[VMEM_SPEC] * n,
        out_shape=[jax.ShapeDtypeStruct(p.shape[1:], F32) for p in parts], compiler_params=_cp())(*parts)


def _permute_in(w):
    lead = w.shape[:-1]
    return w.reshape(lead + (3, 3, 2, BQ)).swapaxes(-2, -3).reshape(lead + (QKVW,))


def _unpermute_in(w):
    lead = w.shape[:-1]
    return w.reshape(lead + (3, 2, 3, BQ)).swapaxes(-2, -3).reshape(lead + (QKVW,))


def _row(v):
    v = v.reshape(-1)
    return jnp.pad(v, (0, D - v.shape[0])).reshape(1, D)


def kernel(x, w_in, f_bias, conv_w, w_out, rel_bias, ln1_g, ln1_b, w_gate, w_up, w_down, ln2_g, ln2_b, loss_target, m_w_in, m_f_bias, m_conv_w, m_w_out, m_rel_bias, m_ln1_g, m_ln1_b, m_w_gate, m_w_up, m_w_down, m_ln2_g, m_ln2_b, v_w_in, v_f_bias, v_conv_w, v_w_out, v_rel_bias, v_ln1_g, v_ln1_b, v_w_gate, v_w_up, v_w_down, v_ln2_g, v_ln2_b):
    xi, yi, ci = _place()
    me = 4 * xi + 2 * yi + ci
    core = jnp.reshape(ci, (1,)).astype(jnp.int32)

    win_s = jnp.concatenate([_permute_in(w_in[..., :QKVW]), w_in[..., QKVW:]], axis=-1)
    win_s = jnp.pad(win_s, ((0, 0), (0, 0), (0, NPAD - NPROJ))).astype(BF16)
    per_layer = [win_s, w_out.astype(BF16), jnp.swapaxes(w_gate, 1, 2).astype(BF16),
                 jnp.swapaxes(w_up, 1, 2).astype(BF16), w_down.astype(BF16)]
    sh = [[s[l] for s in per_layer] for l in range(2)]

    def whole(g):
        return g.reshape(NDEV * g.shape[1], g.shape[2])

    cw_rows = lax.dynamic_update_slice(jnp.zeros((2, 3, 256), F32), conv_w, (0, 0, me * 32))
    small = jnp.concatenate([_row(cw_rows[0]), _row(cw_rows[1]), jnp.zeros((SMALL_ROWS - 2, D), F32)], axis=0)
    small, leg_a = _allreduce_small(small, job=_gather_job_a(sh[0][:1]))
    cw_full = small[0:2, :CONVW].reshape(2, 3, 256)
    cw8 = jnp.pad(cw_full, ((0, 0), (0, 5), (0, 0)))
    fb = jnp.pad(f_bias, ((0, 0), (0, GATEW - NH))).reshape(2, 1, GATEW)
    tbl, leg_b = _dil_table(rel_bias, job=_gather_job_b(list(leg_a)))
    W = [{"win": whole(leg_b[0])}, {}]

    def wrow(tn, K, blk=0):
        return pl.BlockSpec((tn, K), lambda i, j: (j, blk))

    def arow(tm, K, blk=0):
        return pl.BlockSpec((tm, K), lambda i, j: (i, blk))

    h = x.reshape(T, D)
    hb = h.astype(BF16)
    saved = []
    for l in range(2):
        Win = W[l]["win"]
        qkv, conv, gate = _proj(hb, Win)
        cum = _fox_prep(gate, fb[l])
        cq = cum[:, :NH].reshape(BL, S, NH).transpose(0, 2, 1).reshape(NSTAT, S)
        ckb = jnp.broadcast_to(cq[:, :, None], (NSTAT, S, BQ))
        if l == 0:
            mixed, amat, a0 = _sb_fwd(qkv, job=_gather_job_a(sh[0][1:]))
            mixed, lse_d, ex = _flash_fwd(qkv, mixed, 1, False, (tbl,),
                                          job=_join_jobs(_gather_job_b(list(a0)), _gather_job_a(sh[1][:2])))
            W[0].update(zip(("wout", "wgT", "wuT", "wd"), [whole(t) for t in ex[:4]]))
            mixed, lse_f, o_fox, ex = _flash_fwd(qkv, mixed, 2, True, (cq, ckb),
                                                 job=_join_jobs(_gather_job_b(list(ex[4:])), _gather_job_a(sh[1][2:])))
            W[1].update(zip(("win", "wout"), [whole(t) for t in ex[:2]]))
            a2 = list(ex[2:])
        else:
            mixed, amat, ex = _sb_fwd(qkv, job=_gather_job_b(a2))
            W[1].update(zip(("wgT", "wuT", "wd"), [whole(t) for t in ex]))
            mixed, lse_d, _ = _flash_fwd(qkv, mixed, 1, False, (tbl,))
            mixed, lse_f, o_fox, _ = _flash_fwd(qkv, mixed, 2, True, (cq, ckb))
        Wout, WgT, WuT, Wd = W[l]["wout"], W[l]["wgT"], W[l]["wuT"], W[l]["wd"]
        mixed = _conv_fwd(conv, cw8[l], mixed)
        x1, xh1, r1, x1b = _mm_ln(mixed, Wout, h, ln1_g[l:l + 1], ln1_b[l:l + 1], "out_proj_ln")
        fs, ft, a, x2, xh2, r2, x2b = _ffn_fwd(x1b, x1, WgT, WuT, Wd, ln2_g[l:l + 1], ln2_b[l:l + 1])
        saved.append(dict(h=hb, qkv=qkv, conv=conv, gate=gate, cq=cq, ckb=ckb, mixed=mixed, amat=amat, lse_d=lse_d,
                          lse_f=lse_f, o_fox=o_fox, x1=x1b, xh1=xh1, r1=r1, fs=fs, ft=ft, a=a, xh2=xh2, r2=r2))
        h, hb = x2, x2b

    dy = h

    def view(gr):
        return gr.reshape(4, 2, gr.shape[0] // NDEV, gr.shape[1])

    G = [None, None]
    small_g = {}
    shard_g = {}
    for l in (1, 0):
        sv = saved[l]
        Win, Wout, WgT, WuT, Wd = W[l]["win"], W[l]["wout"], W[l]["wgT"], W[l]["wuT"], W[l]["wd"]
        res = _ffn_bwd(dy, sv["xh2"], sv["r2"], ln2_g[l:l + 1], sv["fs"], sv["ft"], Wd, WgT, WuT,
                       target=loss_target.reshape(T, D) if l == 1 else None)
        dgt, dut, ds2b, dx1, dg2, db2 = res[:6]
        if l == 1:
            sq = res[6]
        G_d = _mm_tn(sv["a"], ds2b, None, C=D, Ka=DFF, N=D, tm=256, tn=1024, tk=T, ooff=0, name="grad_w_down")
        G_g = _mm_tn(dgt, sv["x1"], None, C=D, Ka=DFF, N=D, tm=256, tn=1024, tk=T, ooff=0, name="grad_w_gate")
        G_u = _mm_tn(dut, sv["x1"], None, C=D, Ka=DFF, N=D, tm=256, tn=1024, tk=T, ooff=0, name="grad_w_up")
        ds1, dg1, db1, ds1b, dmixed = _ln_bwd(dx1, sv["xh1"], sv["r1"], ln1_g[l:l + 1], Wout)
        G_out = _mm_tn(sv["mixed"], ds1b, None, C=D, Ka=D, N=D, tm=256, tn=1024, tk=T, ooff=0, name="grad_w_out")
        early = [view(t) for t in (G_g, G_u, G_d, G_out)] + ([view(G[1]["in"])] if l == 0 else [])
        dqkv, gots = _sb_bwd(sv["qkv"], dmixed, sv["amat"], job=_sibling_job(early))
        ps = _pair_sums(early, list(gots), core)
        dqkv, dtbl, pa = _flash_bwd(sv["qkv"], sv["mixed"], dmixed, sv["lse_d"], dqkv, 1, False, (tbl,),
                                    job=_chip_job(ps[:2]))
        dqkv, dck, pb = _flash_bwd(sv["qkv"], sv["o_fox"], dmixed, sv["lse_f"], dqkv, 2, True,
                                   (sv["cq"], sv["ckb"]), job=_chip_job(ps[2:]))
        sums = _chip_sums(list(pa) + list(pb))
        shard_g[l] = dict(zip(("g", "u", "d", "out"), sums[:4]))
        if l == 0:
            shard_g[1]["in"] = sums[4]
        dconv, dcw = _conv_bwd(sv["conv"], cw8[l], dmixed)
        dcum = jnp.pad(dck.reshape(S, BL, NH).transpose(1, 0, 2).reshape(T, NH), ((0, 0), (0, GATEW - NH)))
        dgate, dfb = _fox_post(dcum, sv["gate"], fb[l])
        drb = _dil_table_bwd(dtbl)
        G_in = _mm_tn(sv["h"], dqkv, None, C=NPAD, Ka=D, N=QKVW, tm=512, tn=768, tk=T, ooff=0, name="grad_w_in_qkv")
        G_in = _mm_tn(sv["h"], dconv, G_in, C=NPAD, Ka=D, N=CONVW, tm=256, tn=768, tk=T, ooff=3,
                      name="grad_w_in_conv")
        G_in = _mm_tn(sv["h"], dgate, G_in, C=NPAD, Ka=D, N=GATEW, tm=1024, tn=128, tk=1024, ooff=24,
                      name="grad_w_in_gate")
        G[l] = {"in": G_in, "out": G_out, "g": G_g, "u": G_u, "d": G_d}
        if l == 0:
            late = [view(G_in)]
            tail = _chip_job(_pair_sums(late, list(_run_job(_sibling_job(late), "sibling_exchange")), core))
            dy, parts = _mm([(dqkv, arow(1024, QKVW), Win, wrow(512, QKVW, 0)),
                             (dconv, arow(1024, CONVW), Win, wrow(512, CONVW, 3)),
                             (dgate, arow(1024, GATEW), Win, wrow(512, GATEW, 24))],
                            nt=True, M=T, N=D, tm=1024, tn=512, out_dtype=F32, name="proj_dx", res=ds1,
                            res_scale=ALPHA, job=tail)
            shard_g[0]["in"] = _chip_sums(list(parts))[0]
        else:
            dy = _proj_bwd(dqkv, dconv, dgate, Win, ds1)
        small_g[l] = dict(ln1_g=dg1, ln1_b=db1, ln2_g=dg2, ln2_b=db2, cw=dcw[0:3].reshape(1, CONVW),
                          fb=dfb[:, :NH], rb=drb[:, :NH])
    grad_x = dy.reshape(BL, S, D)

    rows = []
    for name in ("ln1_g", "ln1_b", "ln2_g", "ln2_b"):
        rows += [small_g[0][name], small_g[1][name]]
    rows += [_row(small_g[0]["cw"]), _row(small_g[1]["cw"]),
             _row(jnp.concatenate([small_g[0]["fb"], small_g[1]["fb"]], axis=0)),
             _row(small_g[0]["rb"] + small_g[1]["rb"]), _row(sq)]
    rows.append(jnp.zeros((SMALL_ROWS - len(rows), D), F32))
    sg = _allreduce_small(jnp.concatenate(rows, axis=0))
    loss = sg[12, 0] * (0.5 / D)
    g_ln1_g, g_ln1_b, g_ln2_g, g_ln2_b = sg[0:2], sg[2:4], sg[4:6], sg[6:8]
    g_conv_full = sg[8:10, :CONVW].reshape(2, 3, 256)
    g_conv = lax.dynamic_slice(g_conv_full, (0, 0, me * 32), (2, 3, 32))
    g_fb = sg[10, :2 * NH].reshape(2, NH)
    g_rb = sg[11, :32 * NH].reshape(32, NH)

    def both(name):
        return jnp.stack([shard_g[0][name], shard_g[1][name]])

    g_in = both("in")
    g_w_in = jnp.concatenate([_unpermute_in(g_in[..., :QKVW]), g_in[..., QKVW:NPROJ]], axis=-1)
    g_w_out = both("out")
    g_w_gate = jnp.swapaxes(both("g"), 1, 2)
    g_w_up = jnp.swapaxes(both("u"), 1, 2)
    g_w_down = both("d")

    up_in = _adamw(w_in, g_w_in, m_w_in, v_w_in, 64)
    up_out = _adamw(w_out, g_w_out, m_w_out, v_w_out, 128)
    up_gate = _adamw(w_gate, g_w_gate, m_w_gate, v_w_gate, 256)
    up_up = _adamw(w_up, g_w_up, m_w_up, v_w_up, 256)
    up_down = _adamw(w_down, g_w_down, m_w_down, v_w_down, 352)

    def pack(fbv, cwv, rbv, l1g, l1b, l2g, l2b):
        r = [l1g, l1b, l2g, l2b, _row(cwv), _row(fbv), _row(rbv)]
        r.append(jnp.zeros((SMALL_ROWS - 11, D), F32))
        return jnp.concatenate(r, axis=0)

    pw = pack(f_bias, conv_w, rel_bias, ln1_g, ln1_b, ln2_g, ln2_b)
    pg = pack(g_fb, g_conv, g_rb, g_ln1_g, g_ln1_b, g_ln2_g, g_ln2_b)
    pm = pack(m_f_bias, m_conv_w, m_rel_bias, m_ln1_g, m_ln1_b, m_ln2_g, m_ln2_b)
    pv = pack(v_f_bias, v_conv_w, v_rel_bias, v_ln1_g, v_ln1_b, v_ln2_g, v_ln2_b)
    ups = [u[0] for u in _adamw(pw[None], pg[None], pm[None], pv[None], SMALL_ROWS)]

    def unpack(p):
        return dict(ln1_g=p[0:2], ln1_b=p[2:4], ln2_g=p[4:6], ln2_b=p[6:8],
                    conv_w=p[8, :192].reshape(2, 3, 32), f_bias=p[9, :2 * NH].reshape(2, NH),
                    rel_bias=p[10, :32 * NH].reshape(32, NH))

    sm = [unpack(p) for p in ups]

    def group(k):
        return (up_in[k], sm[k]["f_bias"], sm[k]["conv_w"], up_out[k], sm[k]["rel_bias"], sm[k]["ln1_g"],
                sm[k]["ln1_b"], up_gate[k], up_up[k], up_down[k], sm[k]["ln2_g"], sm[k]["ln2_b"])

    grads = (g_w_in, g_fb, g_conv, g_w_out, g_rb, g_ln1_g, g_ln1_b, g_w_gate, g_w_up, g_w_down, g_ln2_g, g_ln2_b)
    return (loss, grad_x) + grads + group(0) + group(1) + group(2)
```

```python
import math

import numpy as np
import jax
import jax.numpy as jnp
from jax import lax
from jax.experimental import pallas as pl
from jax.experimental.pallas import tpu as pltpu

F32 = jnp.float32
BF16 = jnp.bfloat16
MESH = pl.DeviceIdType.MESH

D = 1024
S = 2048
BL = 2
T = BL * S
NH = 4
DFF = 2816
NPROJ = 3076
NPAD = 3200
QKVW = 2304
CONVW = 768
GATEW = 128
PAIRW = 384
BQ = 128
HB = 2 * BQ
NB = S // BQ
NDEV = 8
NSTAT = BL * NH
ALPHA = 4.0 ** 0.25
SCALE = 0.125
NEG = -1e30
LN_EPS = 1e-5
ADAM_LR, ADAM_B1, ADAM_B2, ADAM_EPS, ADAM_WD, ADAM_STEP = 0.001, 0.9, 0.999, 1e-08, 0.01, 10
VMEM_LIMIT = 56 * 1024 * 1024
SMALL_ROWS = 16


def _bucket_thresholds():
    d = np.arange(0, S)
    nf = np.maximum(d, 1).astype(np.float32)
    large = 16 + (np.log(nf / np.float32(16)) / np.float32(math.log(128)) * np.float32(16)).astype(np.int32)
    b = np.where(d < 16, d, np.minimum(large, 31))
    return [int(np.argmax(b >= k)) for k in range(32)]


BUCKET_TH = _bucket_thresholds()


def _cp(sem=None, vmem=VMEM_LIMIT):
    return pltpu.CompilerParams(dimension_semantics=sem, vmem_limit_bytes=vmem)


def _dot(a, b):
    return lax.dot_general(a, b, (((1,), (0,)), ((), ())), preferred_element_type=F32)


def _dot_nt(a, b):
    return lax.dot_general(a, b, (((1,), (1,)), ((), ())), preferred_element_type=F32)


def _dot_tn(a, b):
    return lax.dot_general(a, b, (((0,), (0,)), ((), ())), preferred_element_type=F32)


def _split2(x):
    hi = x.astype(BF16)
    mid = (x - hi.astype(F32)).astype(BF16)
    return jnp.concatenate([hi, mid], axis=1)


def _split3(x):
    hi = x.astype(BF16)
    r = x - hi.astype(F32)
    mid = r.astype(BF16)
    lo = (r - mid.astype(F32)).astype(BF16)
    return jnp.concatenate([hi, mid, lo], axis=1)


def _log_sigmoid(u):
    return jnp.minimum(u, 0.0) - jnp.log1p(jnp.exp(-jnp.abs(u)))


def _log_sigmoid_tile(u):
    return jnp.minimum(u, 0.0) - jnp.log(1.0 + jnp.exp(jnp.minimum(u, -u)))


def _iota(shape, dim):
    return lax.broadcasted_iota(jnp.int32, shape, dim)


ANY_SPEC = pl.BlockSpec(memory_space=pl.ANY)
VMEM_SPEC = pl.BlockSpec(memory_space=pltpu.VMEM)


def _mm(pairs, *, nt, M, N, tm, tn, out_dtype, name, res=None, res_scale=1.0, job=None):
    n = len(pairs)
    n_in = 2 * n + (res is not None)
    jins = job.ins if job else []
    jouts = job.out_shapes if job else []
    gi, gj = M // tm, N // tn

    def body(*refs):
        o_ref = refs[n_in + len(jins)]
        if job:
            jrefs = (refs[n_in:n_in + len(jins)], refs[n_in + len(jins) + 1:n_in + len(jins) + 1 + len(jouts)],
                     refs[n_in + len(jins) + 1 + len(jouts):])

            @pl.when((pl.program_id(0) == 0) & (pl.program_id(1) == 0))
            def _():
                job.start(*jrefs)

        acc = None
        for p in range(n):
            a = refs[2 * p][...].astype(BF16)
            b = refs[2 * p + 1][...]
            d = _dot_nt(a, b) if nt else _dot(a, b)
            acc = d if acc is None else acc + d
        if res is not None:
            acc = acc + res_scale * refs[2 * n][...]
        o_ref[...] = acc.astype(out_dtype)
        if job:
            @pl.when((pl.program_id(0) == gi - 1) & (pl.program_id(1) == gj - 1))
            def _():
                job.finish(None, *jrefs)

    ops, specs = [], []
    for a, asp, b, bsp in pairs:
        ops += [a, b]
        specs += [asp, bsp]
    if res is not None:
        ops.append(res)
        specs.append(pl.BlockSpec((tm, tn), lambda i, j: (i, j)))
    out = pl.pallas_call(
        body, name=name, grid=(gi, gj), in_specs=specs + [ANY_SPEC] * len(jins),
        out_specs=[pl.BlockSpec((tm, tn), lambda i, j: (i, j))] + [ANY_SPEC] * len(jouts),
        out_shape=[jax.ShapeDtypeStruct((M, N), out_dtype)] + list(jouts),
        scratch_shapes=list(job.sems) if job else [],
        input_output_aliases={n_in + a: 1 + b for a, b in job.aliases.items()} if job else {},
        compiler_params=_cp(("arbitrary", "arbitrary") if job else ("parallel", "parallel")))(*ops, *jins)
    return (out[0], out[1:]) if job else out[0]


def _mm_tn(a, b, gbuf, *, C, Ka, N, tm, tn, tk, ooff, name):
    def body(*refs):
        a_ref, b_ref, o_ref = refs[0], refs[1], refs[-1]
        k = pl.program_id(2)
        d = _dot_tn(a_ref[...].astype(BF16), b_ref[...].astype(BF16))

        @pl.when(k == 0)
        def _():
            o_ref[...] = d

        @pl.when(k > 0)
        def _():
            o_ref[...] += d

    ops = [a, b] + ([] if gbuf is None else [gbuf])
    return pl.pallas_call(
        body, name=name, grid=(Ka // tm, N // tn, T // tk),
        in_specs=[pl.BlockSpec((tk, tm), lambda i, j, k: (k, i)),
                  pl.BlockSpec((tk, tn), lambda i, j, k: (k, j))] + ([] if gbuf is None else [ANY_SPEC]),
        out_specs=pl.BlockSpec((tm, tn), lambda i, j, k: (i, ooff + j)),
        out_shape=jax.ShapeDtypeStruct((Ka, C), F32),
        input_output_aliases={} if gbuf is None else {2: 0},
        compiler_params=_cp(("parallel", "parallel", "arbitrary")))(*ops)


def _proj(xb, w):
    tm = 512

    def body(x_ref, w_ref, qkv_ref, conv_ref, gate_ref):
        xv = x_ref[...]
        qkv_ref[...] = _dot(xv, w_ref[:, 0:QKVW]).astype(BF16)
        conv_ref[...] = _dot(xv, w_ref[:, QKVW:QKVW + CONVW])
        gate_ref[...] = _dot(xv, w_ref[:, QKVW + CONVW:NPAD])

    def rows(n):
        return pl.BlockSpec((tm, n), lambda i: (i, 0))

    return pl.pallas_call(
        body, name="proj", grid=(T // tm,),
        in_specs=[rows(D), pl.BlockSpec((D, NPAD), lambda i: (0, 0))],
        out_specs=[rows(QKVW), rows(CONVW), rows(GATEW)],
        out_shape=[jax.ShapeDtypeStruct((T, QKVW), BF16), jax.ShapeDtypeStruct((T, CONVW), F32),
                   jax.ShapeDtypeStruct((T, GATEW), F32)],
        compiler_params=_cp(("parallel",)))(xb, w)


def _proj_bwd(dqkv, dconv, dgate, w, res):
    tm = 512

    def body(a_ref, b_ref, c_ref, w_ref, r_ref, o_ref):
        acc = ALPHA * r_ref[...] + _dot_nt(a_ref[...], w_ref[:, 0:QKVW])
        acc = acc + _dot_nt(b_ref[...], w_ref[:, QKVW:QKVW + CONVW])
        o_ref[...] = acc + _dot_nt(c_ref[...].astype(BF16), w_ref[:, QKVW + CONVW:NPAD])

    def rows(n):
        return pl.BlockSpec((tm, n), lambda i: (i, 0))

    return pl.pallas_call(
        body, name="proj_bwd", grid=(T // tm,),
        in_specs=[rows(QKVW), rows(CONVW), rows(GATEW), pl.BlockSpec((D, NPAD), lambda i: (0, 0)), rows(D)],
        out_specs=rows(D), out_shape=jax.ShapeDtypeStruct((T, D), F32),
        compiler_params=_cp(("parallel",)))(dqkv, dconv, dgate, w, res)


def _ffn_fwd(xb, x, wgt, wut, wd, gam, bet):
    tm, ch = 512, 256

    def body(xb_ref, x_ref, g_ref, b_ref, wg_hbm, wu_hbm, wd_hbm,
             go_ref, uo_ref, ao_ref, y_ref, xh_ref, r_ref, yb_ref, wg_v, wu_v, wd_v, sem):
        loads = [pltpu.make_async_copy(s, d, sem.at[k])
                 for k, (s, d) in enumerate(((wg_hbm, wg_v), (wu_hbm, wu_v), (wd_hbm, wd_v)))]

        @pl.when(pl.program_id(0) == 0)
        def _():
            for cp in loads:
                cp.start()
            loads[0].wait()
            loads[1].wait()

        xv = xb_ref[...]
        for c in range(0, DFF, ch):
            gv = _dot_nt(xv, wg_v[c:c + ch, :])
            uv = _dot_nt(xv, wu_v[c:c + ch, :])
            go_ref[:, c:c + ch] = gv.astype(BF16)
            uo_ref[:, c:c + ch] = uv.astype(BF16)
            ao_ref[:, c:c + ch] = (gv * jax.nn.sigmoid(gv) * uv).astype(BF16)
        @pl.when(pl.program_id(0) == 0)
        def _():
            loads[2].wait()

        s = ALPHA * x_ref[...] + _dot(ao_ref[...], wd_v[...])
        mu = jnp.mean(s, axis=-1, keepdims=True)
        xc = s - mu
        var = jnp.mean(xc * xc, axis=-1, keepdims=True)
        r = lax.rsqrt(var + LN_EPS)
        xh = xc * r
        xh_ref[...] = xh.astype(BF16)
        r_ref[...] = r
        y = xh * g_ref[...] + b_ref[...]
        y_ref[...] = y
        yb_ref[...] = y.astype(BF16)

    row = pl.BlockSpec((tm, D), lambda i: (i, 0))
    wide = pl.BlockSpec((tm, DFF), lambda i: (i, 0))
    vec = pl.BlockSpec((1, D), lambda i: (0, 0))
    wsl = pltpu.VMEM((DFF, D), BF16)
    hid = jax.ShapeDtypeStruct((T, DFF), BF16)
    return pl.pallas_call(
        body, name="ffn_fwd", grid=(T // tm,),
        in_specs=[row, row, vec, vec, ANY_SPEC, ANY_SPEC, ANY_SPEC],
        out_specs=[wide, wide, wide, row, row, pl.BlockSpec((tm, 1), lambda i: (i, 0)), row],
        out_shape=[hid, hid, hid, jax.ShapeDtypeStruct((T, D), F32), jax.ShapeDtypeStruct((T, D), BF16),
                   jax.ShapeDtypeStruct((T, 1), F32), jax.ShapeDtypeStruct((T, D), BF16)],
        scratch_shapes=[wsl, wsl, wsl, pltpu.SemaphoreType.DMA((3,))],
        compiler_params=_cp(("arbitrary",)))(xb, x, gam, bet, wgt, wut, wd)


def _ffn_bwd(dy, xh, r, gam, g, u, wd, wgt, wut, target=None):
    tm, ch = 256, 256

    def body(*refs):
        if target is None:
            (dy_ref, xh_ref, r_ref, gam_ref, g_ref, u_ref, wd_hbm, wg_hbm, wu_hbm,
             dg_ref, du_ref, dsb_ref, dx_ref, dgam_ref, dbet_ref, wd_v, wg_v, wu_v, sem) = refs
        else:
            (dy_ref, t_ref, xh_ref, r_ref, gam_ref, g_ref, u_ref, wd_hbm, wg_hbm, wu_hbm,
             dg_ref, du_ref, dsb_ref, dx_ref, dgam_ref, dbet_ref, sq_ref, wd_v, wg_v, wu_v, sem) = refs
        loads = [pltpu.make_async_copy(s, d, sem.at[k])
                 for k, (s, d) in enumerate(((wd_hbm, wd_v), (wg_hbm, wg_v), (wu_hbm, wu_v)))]

        @pl.when(pl.program_id(0) == 0)
        def _():
            for cp in loads:
                cp.start()
            loads[0].wait()

        if target is None:
            dyv = dy_ref[...]
        else:
            e = dy_ref[...] - t_ref[...]
            dyv = e * (1.0 / D)
            p = jnp.sum(jnp.sum(e * e, axis=1, keepdims=True), axis=0, keepdims=True)

            @pl.when(pl.program_id(0) == 0)
            def _():
                sq_ref[...] = p

            @pl.when(pl.program_id(0) > 0)
            def _():
                sq_ref[...] += p

        xhv = xh_ref[...].astype(F32)
        dxh = dyv * gam_ref[...]
        m1 = jnp.mean(dxh, axis=-1, keepdims=True)
        m2 = jnp.mean(dxh * xhv, axis=-1, keepdims=True)
        ds = r_ref[...] * (dxh - m1 - xhv * m2)
        pg = jnp.sum(dyv * xhv, axis=0, keepdims=True)
        pb = jnp.sum(dyv, axis=0, keepdims=True)

        @pl.when(pl.program_id(0) == 0)
        def _():
            dgam_ref[...] = pg
            dbet_ref[...] = pb

        @pl.when(pl.program_id(0) > 0)
        def _():
            dgam_ref[...] += pg
            dbet_ref[...] += pb

        db = ds.astype(BF16)
        dsb_ref[...] = db
        for c in range(0, DFF, ch):
            da = _dot_nt(db, wd_v[c:c + ch, :])
            gv = g_ref[:, c:c + ch].astype(F32)
            sg = jax.nn.sigmoid(gv)
            dg_ref[:, c:c + ch] = (da * u_ref[:, c:c + ch].astype(F32) * (sg * (1.0 + gv * (1.0 - sg)))).astype(BF16)
            du_ref[:, c:c + ch] = (da * (gv * sg)).astype(BF16)
        @pl.when(pl.program_id(0) == 0)
        def _():
            loads[1].wait()
            loads[2].wait()

        dx_ref[...] = ALPHA * ds + _dot(dg_ref[...], wg_v[...]) + _dot(du_ref[...], wu_v[...])

    row = pl.BlockSpec((tm, D), lambda i: (i, 0))
    wide = pl.BlockSpec((tm, DFF), lambda i: (i, 0))
    vec = pl.BlockSpec((1, D), lambda i: (0, 0))
    wsl = pltpu.VMEM((DFF, D), BF16)
    last = target is not None
    return pl.pallas_call(
        body, name="ffn_bwd_loss" if last else "ffn_bwd", grid=(T // tm,),
        in_specs=[row] + ([row] if last else [])
        + [row, pl.BlockSpec((tm, 1), lambda i: (i, 0)), vec, wide, wide, ANY_SPEC, ANY_SPEC, ANY_SPEC],
        out_specs=[wide, wide, row, row, vec, vec] + ([pl.BlockSpec((1, 1), lambda i: (0, 0))] if last else []),
        out_shape=[jax.ShapeDtypeStruct((T, DFF), BF16), jax.ShapeDtypeStruct((T, DFF), BF16),
                   jax.ShapeDtypeStruct((T, D), BF16), jax.ShapeDtypeStruct((T, D), F32),
                   jax.ShapeDtypeStruct((1, D), F32), jax.ShapeDtypeStruct((1, D), F32)]
        + ([jax.ShapeDtypeStruct((1, 1), F32)] if last else []),
        scratch_shapes=[wsl, wsl, wsl, pltpu.SemaphoreType.DMA((3,))],
        compiler_params=_cp(("arbitrary",)))(dy, *([target] if last else []), xh, r, gam, g, u, wd, wgt, wut)


def _mm_ln(a, w, x, gam, bet, name):
    tm = 256
    K = a.shape[1]

    def body(a_ref, w_ref, x_ref, g_ref, b_ref, y_ref, xh_ref, r_ref, yb_ref):
        s = ALPHA * x_ref[...] + _dot(a_ref[...], w_ref[...])
        mu = jnp.mean(s, axis=-1, keepdims=True)
        xc = s - mu
        var = jnp.mean(xc * xc, axis=-1, keepdims=True)
        r = lax.rsqrt(var + LN_EPS)
        xh = xc * r
        xh_ref[...] = xh.astype(BF16)
        r_ref[...] = r
        y = xh * g_ref[...] + b_ref[...]
        y_ref[...] = y
        yb_ref[...] = y.astype(BF16)

    row = pl.BlockSpec((tm, D), lambda i: (i, 0))
    vec = pl.BlockSpec((1, D), lambda i: (0, 0))
    return pl.pallas_call(
        body, name=name, grid=(T // tm,),
        in_specs=[pl.BlockSpec((tm, K), lambda i: (i, 0)), pl.BlockSpec((K, D), lambda i: (0, 0)), row, vec, vec],
        out_specs=[row, row, pl.BlockSpec((tm, 1), lambda i: (i, 0)), row],
        out_shape=[jax.ShapeDtypeStruct((T, D), F32), jax.ShapeDtypeStruct((T, D), BF16),
                   jax.ShapeDtypeStruct((T, 1), F32), jax.ShapeDtypeStruct((T, D), BF16)],
        compiler_params=_cp(("parallel",)))(a, w, x, gam, bet)


def _ln_bwd(dy, xh, r, gam, w):
    tm = 256

    def body(dy_ref, xh_ref, r_ref, g_ref, w_ref, ds_ref, dg_ref, db_ref, dsb_ref, dm_ref):
        i = pl.program_id(0)
        dyv = dy_ref[...]
        xhv = xh_ref[...].astype(F32)
        dxh = dyv * g_ref[...]
        m1 = jnp.mean(dxh, axis=-1, keepdims=True)
        m2 = jnp.mean(dxh * xhv, axis=-1, keepdims=True)
        ds = r_ref[...] * (dxh - m1 - xhv * m2)
        ds_ref[...] = ds
        dsb = ds.astype(BF16)
        dsb_ref[...] = dsb
        dm_ref[...] = _dot_nt(dsb, w_ref[...]).astype(BF16)
        pg = jnp.sum(dyv * xhv, axis=0, keepdims=True)
        pb = jnp.sum(dyv, axis=0, keepdims=True)

        @pl.when(i == 0)
        def _():
            dg_ref[...] = pg
            db_ref[...] = pb

        @pl.when(i > 0)
        def _():
            dg_ref[...] += pg
            db_ref[...] += pb

    row = pl.BlockSpec((tm, D), lambda i: (i, 0))
    vec = pl.BlockSpec((1, D), lambda i: (0, 0))
    return pl.pallas_call(
        body, name="ln_bwd_proj", grid=(T // tm,),
        in_specs=[row, row, pl.BlockSpec((tm, 1), lambda i: (i, 0)), vec, pl.BlockSpec((D, D), lambda i: (0, 0))],
        out_specs=[row, vec, vec, row, row],
        out_shape=[jax.ShapeDtypeStruct((T, D), F32), jax.ShapeDtypeStruct((1, D), F32),
                   jax.ShapeDtypeStruct((1, D), F32), jax.ShapeDtypeStruct((T, D), BF16),
                   jax.ShapeDtypeStruct((T, D), BF16)],
        compiler_params=_cp(("arbitrary",)))(dy, xh, r, gam, w)


def _adamw(w, g, m, v, tr):
    L, R, C = w.shape

    def body(w_ref, g_ref, m_ref, v_ref, d_ref, m2_ref, v2_ref):
        gv = g_ref[...]
        m2 = ADAM_B1 * m_ref[...] + (1.0 - ADAM_B1) * gv
        v2 = ADAM_B2 * v_ref[...] + (1.0 - ADAM_B2) * (gv * gv)
        m_hat = m2 / (1.0 - ADAM_B1 ** ADAM_STEP)
        v_hat = v2 / (1.0 - ADAM_B2 ** ADAM_STEP)
        d_ref[...] = -ADAM_LR * (m_hat / (jnp.sqrt(v_hat) + ADAM_EPS) + ADAM_WD * w_ref[...])
        m2_ref[...] = m2
        v2_ref[...] = v2

    blk = pl.BlockSpec((None, tr, C), lambda l, i: (l, i, 0))
    sh = jax.ShapeDtypeStruct((L, R, C), F32)
    return pl.pallas_call(
        body, name="adamw", grid=(L, R // tr), in_specs=[blk] * 4, out_specs=[blk] * 3,
        out_shape=[sh, sh, sh], compiler_params=_cp(("parallel", "parallel")))(w, g, m, v)


class _Job:
    def __init__(self, ins, out_shapes, aliases, sems, start, finish):
        self.ins, self.out_shapes, self.aliases, self.sems = list(ins), list(out_shapes), dict(aliases), list(sems)
        self.start, self.finish = start, finish


def _host_call(body, name, ins, in_specs, out_shapes, out_specs, scratch, aliases, job):
    n_in, n_out, n_scr = len(ins), len(out_shapes), len(scratch)
    jins = job.ins if job else []
    jouts = job.out_shapes if job else []
    jsems = job.sems if job else []

    def wrapped(*refs):
        a = n_in
        b = a + len(jins)
        c = b + n_out
        d = c + len(jouts)
        e = d + n_scr
        comm = None
        if job:
            jrefs = (refs[a:b], refs[c:d], refs[e:])
            comm = (lambda: job.start(*jrefs), lambda st: job.finish(st, *jrefs))
        body(refs[:a], refs[b:c], refs[d:e], comm)

    al = dict(aliases)
    if job:
        for ji, jo in job.aliases.items():
            al[n_in + ji] = n_out + jo
    res = pl.pallas_call(
        wrapped, name=name, in_specs=list(in_specs) + [ANY_SPEC] * len(jins),
        out_specs=list(out_specs) + [ANY_SPEC] * len(jouts), out_shape=list(out_shapes) + list(jouts),
        scratch_shapes=list(scratch) + list(jsems), input_output_aliases=al,
        compiler_params=_cp())(*ins, *jins)
    return res[:n_out], res[n_out:]


def _copy_in(src, dst, sem):
    cp = pltpu.make_async_copy(src, dst, sem)
    cp.start()
    cp.wait()


CHAINS = [(p, b) for p in range(2) for b in range(BL)]
NC = len(CHAINS)
ROWS_SHAPE = jax.ShapeDtypeStruct((NSTAT, S), F32)
SLAB_QKV = pltpu.VMEM((T, 2 * PAIRW), BF16)
SLAB_OUT = pltpu.VMEM((T, 2 * BQ), BF16)
SLAB_O32 = pltpu.VMEM((T, 2 * BQ), F32)
SLAB_T = pltpu.VMEM((2, BQ, T), BF16)
SLAB_KEYB = pltpu.VMEM((NSTAT, S, BQ), F32)
ACC_KV = pltpu.VMEM((2, T, BQ), F32)
NTRI = NB * (NB + 1) // 2
A_TILES = jax.ShapeDtypeStruct((NTRI, NC, HB, BQ), BF16)
PAIR_DIAG = pltpu.VMEM((NC, 2, HB, HB), BF16)
FLASH_PER_TRIP = SB_PER_TRIP = 4
A_AHEAD = 4
A_SLOTS_IN = A_AHEAD + SB_PER_TRIP
A_SLOTS_OUT = 2 * SB_PER_TRIP


def _lane_masks():
    lane = _iota((1, BQ), 1)
    m0 = (lane < 64).astype(BF16)
    return m0, 1.0 - m0


def _merge_heads(x, first):
    return jnp.where(first, x[:BQ], x[BQ:])


def _row_masks():
    r = _iota((BQ, 1), 0)
    m0 = (r < 64).astype(BF16)
    return m0, 1.0 - m0


def _stack(x, m0, m1):
    return jnp.concatenate([x * m0, x * m1], axis=0)


def _stack_t(xt, r0, r1):
    return jnp.concatenate([xt * r0, xt * r1], axis=1)


def _tr(x):
    return x.T


def _rows(b, i):
    return pl.ds(pl.multiple_of(b * S + i * BQ, BQ), BQ)


def _transpose_slab(src, dst, col0):
    def blk(n, _):
        r = pl.ds(pl.multiple_of(n * BQ, BQ), BQ)
        for p in range(2):
            dst[p, :, r] = _tr(src[r, col0(p):col0(p) + BQ])
        return 0

    lax.fori_loop(0, T // BQ, blk, 0)


def _heads(x):
    return x[:BQ], x[BQ:]


def _bcast_heads(r0, r1):
    return jnp.concatenate([jnp.broadcast_to(r0, (BQ, BQ)), jnp.broadcast_to(r1, (BQ, BQ))], axis=0)


def _by_channel(r0, r1):
    return jnp.where(_iota((BQ, BQ), 0) < 64, r0, r1)


def _colsum2(x):
    return jnp.sum(x[:BQ], axis=0, keepdims=True), jnp.sum(x[BQ:], axis=0, keepdims=True)


def _stat_row(ref, p, b, h, i):
    c = b * NH + 2 * p + h
    return ref[c:c + 1, pl.ds(pl.multiple_of(i * BQ, BQ), BQ)]


def _put_row(ref, p, b, h, i, v):
    c = b * NH + 2 * p + h
    ref[c:c + 1, pl.ds(pl.multiple_of(i * BQ, BQ), BQ)] = v


def _valid_t(strict):
    r = _iota((HB, BQ), 0) & (BQ - 1)
    c = _iota((HB, BQ), 1)
    return (r < c) if strict else (r <= c)


def _tri_blockdiag(later):
    r = _iota((HB, HB), 0)
    c = _iota((HB, HB), 1)
    same = (r >= BQ) == (c >= BQ)
    return (same & ((c > r) if later else (c < r))).astype(BF16)


def _cum_mm(tri, x):
    y = _dot(tri, _split2(x))
    return y[:, :BQ] + y[:, BQ:]


def _kv_tiles(qkv_v, p, b, j):
    r = _rows(b, j)
    return qkv_v[r, p * PAIRW + BQ:p * PAIRW + 2 * BQ], qkv_v[r, p * PAIRW + 2 * BQ:p * PAIRW + 3 * BQ]


def _blocks_loop(count, per_trip, step, carry, before=None):
    done = 0
    while per_trip >= 1:
        def trip(n, c, per_trip=per_trip, done=done):
            for u in range(per_trip if before else 0):
                before(done + n * per_trip + u)
            for u in range(per_trip):
                c = step(done + n * per_trip + u, c)
            return c

        trips = (count - done) // per_trip
        carry = lax.fori_loop(0, trips, trip, carry)
        done = done + trips * per_trip
        per_trip //= 2
    return carry


def _q_tile(qkv_v, p, b, i):
    return qkv_v[_rows(b, i), p * PAIRW:p * PAIRW + BQ] * SCALE


def _sb_fwd(qkv, job=None):
    def body(ins, outs, scr, comm):
        (qkv_hbm,), (o_hbm, a_hbm), (qkv_v, o_v, sem, vt_v, a_st, a_sems) = ins, outs, scr
        _copy_in(qkv_hbm.at[:, pl.ds(0, 2 * PAIRW)], qkv_v, sem)
        st = comm[0]() if comm else None
        _transpose_slab(qkv_v, vt_v, lambda p: p * PAIRW + 2 * BQ)
        m0, m1 = _lane_masks()
        r0, r1 = _row_masks()
        valid = _valid_t(True)
        later = _tri_blockdiag(True)

        def a_copy(n, t):
            slot = n % A_SLOTS_OUT
            return pltpu.make_async_copy(a_st.at[slot], a_hbm.at[t], a_sems.at[slot])

        def free_slot(i, jj):
            n = (i * (i + 1)) // 2 + jj

            @pl.when(n >= A_SLOTS_OUT)
            def _():
                a_copy(n, 0).wait()

        def steps(qts, i, jj, cs, diag):
            j = i - jj
            ks = [_stack(_kv_tiles(qkv_v, p, b, j)[0], m0, m1) for p, b in CHAINS]
            zs = [_dot(ks[c], qts[c]) for c in range(NC)]
            lbs, lrs = [], []
            for c in range(NC):
                lb = _log_sigmoid_tile(zs[c])
                lr = lb - zs[c]
                if diag:
                    lr = jnp.where(valid, lr, 0.0)
                lbs.append(lb)
                lrs.append(lr)
            tails = [_cum_mm(later, lrs[c]) for c in range(NC)]
            avs = []
            for c in range(NC):
                a = jnp.exp(lbs[c] + tails[c] + _bcast_heads(*cs[c][0]))
                if diag:
                    a = jnp.where(valid, a, 0.0)
                avs.append(a.astype(BF16))
            n = (i * (i + 1)) // 2 + jj
            if diag:
                free_slot(i, jj)
            for c in range(NC):
                a_st[n % A_SLOTS_OUT, c] = avs[c]
            a_copy(n, n - jj + j).start()
            out = []
            for c, (p, b) in enumerate(CHAINS):
                vts = _stack_t(vt_v[p, :, _rows(b, j)], r0, r1)
                s0, s1 = _colsum2(lrs[c])
                out.append(((cs[c][0][0] + s0, cs[c][0][1] + s1), cs[c][1] + _dot(vts, avs[c])))
            return tuple(out)

        def qblock(i, _):
            qts = [_tr(_q_tile(qkv_v, p, b, i)) for p, b in CHAINS]
            zr = jnp.zeros((1, BQ), F32)
            cs = steps(qts, i, 0, (((zr, zr), jnp.zeros((BQ, BQ), F32)),) * NC, True)
            cs = _blocks_loop(i, SB_PER_TRIP, lambda k, cs: steps(qts, i, k + 1, cs, False), cs,
                              before=lambda k: free_slot(i, k + 1))
            for c, (p, b) in enumerate(CHAINS):
                o_v[_rows(b, i), p * BQ:(p + 1) * BQ] = cs[c][1].T.astype(BF16)
            return 0

        lax.fori_loop(0, NB, qblock, 0)
        for n in range(NTRI - A_SLOTS_OUT, NTRI):
            a_copy(n, 0).wait()
        _copy_in(o_v, o_hbm.at[:, pl.ds(0, 2 * BQ)], sem)
        if comm:
            comm[1](st)

    (mixed, amat), extra = _host_call(
        body, "sb_fwd", [qkv], [ANY_SPEC], [jax.ShapeDtypeStruct((T, D), BF16), A_TILES], [ANY_SPEC, ANY_SPEC],
        [SLAB_QKV, SLAB_OUT, pltpu.SemaphoreType.DMA, SLAB_T, pltpu.VMEM((A_SLOTS_OUT, NC, HB, BQ), BF16),
         pltpu.SemaphoreType.DMA((A_SLOTS_OUT,))], {}, job)
    return mixed, amat, extra


def _sb_bwd(qkv, dmixed, amat, job=None):
    def body(ins, outs, scr, comm):
        (qkv_hbm, do_hbm, a_hbm), (dqkv_hbm,), (qkv_v, do_v, dq_v, dk_s, dv_s, sems, kt_v, a_st, a_sems, w_v) = ins, outs, scr
        sem = sems.at[0]
        w_v[...] = jnp.zeros_like(w_v)

        def a_copy(t):
            slot = t % A_SLOTS_IN
            return pltpu.make_async_copy(a_hbm.at[t], a_st.at[slot], a_sems.at[slot])

        later = [pltpu.make_async_copy(do_hbm.at[:, pl.ds(0, 2 * BQ)], do_v, sems.at[1])]
        for cp in later:
            cp.start()
        for t in range(A_AHEAD):
            a_copy(t).start()
        _copy_in(qkv_hbm.at[:, pl.ds(0, 2 * PAIRW)], qkv_v, sem)
        st = comm[0]() if comm else None
        _transpose_slab(qkv_v, kt_v, lambda p: p * PAIRW + BQ)
        for cp in later:
            cp.wait()
        m0, m1 = _lane_masks()
        first = _iota((BQ, BQ), 1) < 64
        r0, r1 = _row_masks()
        valid = _valid_t(True)
        earlier = _tri_blockdiag(False)
        dk_s[...] = jnp.zeros_like(dk_s)
        dv_s[...] = jnp.zeros_like(dv_s)

        def fetch(i, j):
            t = (i * (i + 1)) // 2 + j
            a_copy(t).wait()

            @pl.when(t + A_AHEAD < NTRI)
            def _():
                a_copy(t + A_AHEAD).start()

        def steps(i, j, cs, diag, fetched=False):
            if not fetched:
                fetch(i, j)
            slot = ((i * (i + 1)) // 2 + j) % A_SLOTS_IN
            kv =[_kv_tiles(qkv_v, p, b, j) for p, b in CHAINS]
            zd = [_dot(jnp.concatenate([_stack(kv[c][0], m0, m1), _stack(kv[c][1], m0, m1)], axis=1), w_v[c, 0])
                  for c in range(NC)]
            zs = [x[:, :BQ] for x in zd]
            das = [x[:, BQ:] for x in zd]
            avs = [a_st[slot, c] for c in range(NC)]
            gms = [das[c] * avs[c].astype(F32) for c in range(NC)]
            befores = [_dot(earlier, gms[c].astype(BF16)) for c in range(NC)]
            dzbs = []
            for c in range(NC):
                dz = gms[c] - jax.nn.sigmoid(zs[c]) * (gms[c] + befores[c] + _bcast_heads(*cs[c][0]))
                if diag:
                    dz = jnp.where(valid, dz, 0.0)
                dzbs.append(dz.astype(BF16))
            out = []
            for c, (p, b) in enumerate(CHAINS):
                dq = cs[c][1] + _dot(_stack_t(kt_v[p, :, _rows(b, j)], r0, r1), dzbs[c])
                kd = _dot(jnp.concatenate([dzbs[c], avs[c]], axis=1), w_v[c, 1])
                dk_s[p, _rows(b, j), :] += _merge_heads(kd[:, :BQ], first)
                dv_s[p, _rows(b, j), :] += _merge_heads(kd[:, BQ:], first)
                g0, g1 = _colsum2(gms[c])
                out.append(((cs[c][0][0] + g0, cs[c][0][1] + g1), dq))
            return tuple(out)

        def qblock(i, _):
            for c, (p, b) in enumerate(CHAINS):
                qn = _q_tile(qkv_v, p, b, i)
                dn = do_v[_rows(b, i), p * BQ:(p + 1) * BQ]
                for r, (x, y) in enumerate(((_tr(qn), _tr(dn)), (qn, dn))):
                    w_v[c, r, :BQ, :BQ] = x
                    w_v[c, r, BQ:, BQ:] = y
            zr = jnp.zeros((1, BQ), F32)
            cs = (((zr, zr), jnp.zeros((BQ, BQ), F32)),) * NC
            cs = _blocks_loop(i, SB_PER_TRIP, lambda j, cs: steps(i, j, cs, False, True), cs,
                              before=lambda j: fetch(i, j))
            cs = steps(i, i, cs, True)
            for c, (p, b) in enumerate(CHAINS):
                dq_v[_rows(b, i), p * PAIRW:p * PAIRW + BQ] = (cs[c][1].T * SCALE).astype(BF16)
            return 0

        lax.fori_loop(0, NB, qblock, 0)
        for p in range(2):
            dq_v[:, p * PAIRW + BQ:p * PAIRW + 2 * BQ] = dk_s[p].astype(BF16)
            dq_v[:, p * PAIRW + 2 * BQ:p * PAIRW + 3 * BQ] = dv_s[p].astype(BF16)
        _copy_in(dq_v, dqkv_hbm.at[:, pl.ds(0, 2 * PAIRW)], sem)
        if comm:
            comm[1](st)

    (dqkv,), extra = _host_call(
        body, "sb_bwd", [qkv, dmixed, amat], [ANY_SPEC, ANY_SPEC, ANY_SPEC],
        [jax.ShapeDtypeStruct((T, QKVW), BF16)], [ANY_SPEC],
        [SLAB_QKV, SLAB_OUT, SLAB_QKV, ACC_KV, ACC_KV, pltpu.SemaphoreType.DMA((4,)), SLAB_T,
         pltpu.VMEM((A_SLOTS_IN, NC, HB, BQ), BF16), pltpu.SemaphoreType.DMA((A_SLOTS_IN,)), PAIR_DIAG], {}, job)
    return dqkv, extra


def _flash_fwd(qkv, mixed, g, fox, bias, job=None):
    def body(ins, outs, scr, comm):
        if fox:
            qkv_hbm, cq_ref, ckb_hbm, _ = ins
            (o_hbm, lse_ref, o32_hbm), (qkv_v, o_v, sem, vt_v, o32_v, ckb_v) = outs, scr
        else:
            qkv_hbm, tbl_ref, _ = ins
            (o_hbm, lse_ref), (qkv_v, o_v, sem, vt_v) = outs, scr
        sems = sem
        sem = sems.at[0]
        later = [pltpu.make_async_copy(ckb_hbm, ckb_v, sems.at[1])] if fox else []
        for cp in later:
            cp.start()
        _copy_in(qkv_hbm.at[:, pl.ds(g * 2 * PAIRW, 2 * PAIRW)], qkv_v, sem)
        st = comm[0]() if comm else None
        _transpose_slab(qkv_v, vt_v, lambda p: p * PAIRW + 2 * BQ)
        for cp in later:
            cp.wait()
        m0, m1 = _lane_masks()
        r0, r1 = _row_masks()
        valid = _valid_t(False)

        def steps(qts, cqs, i, j, cs, diag):
            ks = [_stack(_kv_tiles(qkv_v, p, b, j)[0], m0, m1) for p, b in CHAINS]
            zs = [_dot(ks[c], qts[c]) for c in range(NC)]
            prs, alphas, out = [], [], []
            for c, (p, b) in enumerate(CHAINS):
                (ma, mb), (la, lb_), _ = cs[c]
                if fox:
                    kk = pl.ds(pl.multiple_of(j * BQ, BQ), BQ)
                    col = b * NH + 2 * p
                    z = zs[c] + (cqs[c] - jnp.concatenate([ckb_v[col, kk, :], ckb_v[col + 1, kk, :]], axis=0))
                    if diag:
                        z = jnp.where(valid, z, NEG)
                else:
                    z = zs[c] + tbl_ref[p, i - j]
                za, zb = _heads(z)
                na = jnp.maximum(ma, jnp.max(za, axis=0, keepdims=True))
                nb = jnp.maximum(mb, jnp.max(zb, axis=0, keepdims=True))
                aa, ab = jnp.exp(ma - na), jnp.exp(mb - nb)
                pr = jnp.exp(z - _bcast_heads(na, nb))
                sa, sb = _colsum2(pr)
                prs.append(_split2(pr) if fox else pr.astype(BF16))
                alphas.append((aa, ab))
                out.append(((na, nb), (aa * la + sa, ab * lb_ + sb)))
            pvs = []
            for c, (p, b) in enumerate(CHAINS):
                vts = _stack_t(vt_v[p, :, _rows(b, j)], r0, r1)
                if fox:
                    pvs.append(_dot(vts, prs[c][:, :BQ]) + _dot(vts, prs[c][:, BQ:]))
                else:
                    pvs.append(_dot(vts, prs[c]))
            return tuple((out[c][0], out[c][1], _by_channel(*alphas[c]) * cs[c][2] + pvs[c]) for c in range(NC))

        def qblock(i, _):
            qts = [_tr(_q_tile(qkv_v, p, b, i)) for p, b in CHAINS]
            if fox:
                cqs = [_bcast_heads(_stat_row(cq_ref, p, b, 0, i), _stat_row(cq_ref, p, b, 1, i)) for p, b in CHAINS]
            else:
                cqs = [None] * NC
            ng = jnp.full((1, BQ), NEG, F32)
            zr = jnp.zeros((1, BQ), F32)
            cs = steps(qts, cqs, i, i, (((ng, ng), (zr, zr), jnp.zeros((BQ, BQ), F32)),) * NC, True)
            cs = _blocks_loop(i, FLASH_PER_TRIP, lambda k, cs: steps(qts, cqs, i, i - 1 - k, cs, False), cs)
            for c, (p, b) in enumerate(CHAINS):
                (ma, mb), (la, lb_), acc = cs[c]
                o = (acc / _by_channel(la, lb_)).T
                o_v[_rows(b, i), p * BQ:(p + 1) * BQ] = o.astype(BF16)
                if fox:
                    o32_v[_rows(b, i), p * BQ:(p + 1) * BQ] = o
                _put_row(lse_ref, p, b, 0, i, ma + jnp.log(la))
                _put_row(lse_ref, p, b, 1, i, mb + jnp.log(lb_))
            return 0

        lax.fori_loop(0, NB, qblock, 0)
        _copy_in(o_v, o_hbm.at[:, pl.ds(g * 2 * BQ, 2 * BQ)], sem)
        if fox:
            _copy_in(o32_v, o32_hbm, sem)
        if comm:
            comm[1](st)

    bias_specs = [VMEM_SPEC, ANY_SPEC] if fox else [VMEM_SPEC]
    n_in = 2 + len(bias_specs)
    o32 = [jax.ShapeDtypeStruct((T, 2 * BQ), F32)] if fox else []
    res, extra = _host_call(
        body, "fox_fwd" if fox else "dil_fwd", [qkv, *bias, mixed], [ANY_SPEC] + bias_specs + [ANY_SPEC],
        [jax.ShapeDtypeStruct((T, D), BF16), ROWS_SHAPE] + o32, [ANY_SPEC, VMEM_SPEC] + [ANY_SPEC] * len(o32),
        [SLAB_QKV, SLAB_OUT, pltpu.SemaphoreType.DMA((4,)), SLAB_T] + ([SLAB_O32, SLAB_KEYB] if fox else []),
        {n_in - 1: 0}, job)
    return (*res, extra)


def _flash_bwd(qkv, o, dmixed, lse, dqkv, g, fox, bias, job=None):
    def body(ins, outs, scr, comm):
        if fox:
            qkv_hbm, o_hbm, do_hbm, lse_ref, cq_ref, ckb_hbm, _ = ins
            (dqkv_hbm, db_ref), (qkv_v, o_v, do_v, dq_v, dk_s, dv_s, sem, kt_v, w_v, ckb_v, dc_s) = outs, scr
        else:
            qkv_hbm, o_hbm, do_hbm, lse_ref, tbl_ref, _ = ins
            (dqkv_hbm, db_ref), (qkv_v, o_v, do_v, dq_v, dk_s, dv_s, sem, kt_v, w_v) = outs, scr
        w_v[...] = jnp.zeros_like(w_v)
        sems = sem
        sem = sems.at[0]
        later = [pltpu.make_async_copy(do_hbm.at[:, pl.ds(g * 2 * BQ, 2 * BQ)], do_v, sems.at[1])]
        if fox:
            later += [pltpu.make_async_copy(o_hbm, o_v, sems.at[2]), pltpu.make_async_copy(ckb_hbm, ckb_v, sems.at[3])]
        else:
            later += [pltpu.make_async_copy(o_hbm.at[:, pl.ds(g * 2 * BQ, 2 * BQ)], o_v, sems.at[2])]
        for cp in later:
            cp.start()
        _copy_in(qkv_hbm.at[:, pl.ds(g * 2 * PAIRW, 2 * PAIRW)], qkv_v, sem)
        st = comm[0]() if comm else None
        _transpose_slab(qkv_v, kt_v, lambda p: p * PAIRW + BQ)
        for cp in later:
            cp.wait()
        m0, m1 = _lane_masks()
        first = _iota((BQ, BQ), 1) < 64
        r0, r1 = _row_masks()
        valid = _valid_t(False)
        dk_s[...] = jnp.zeros_like(dk_s)
        dv_s[...] = jnp.zeros_like(dv_s)
        if fox:
            dc_s[...] = jnp.zeros_like(dc_s)
        else:
            db_ref[...] = jnp.zeros_like(db_ref)

        def steps(cqs, lses, deltas, i, j, dqs, diag):
            kv = [_kv_tiles(qkv_v, p, b, j) for p, b in CHAINS]
            zd = [_dot(jnp.concatenate([_stack(kv[c][0], m0, m1), _stack(kv[c][1], m0, m1)], axis=1), w_v[c, 0])
                  for c in range(NC)]
            zs = [x[:, :BQ] for x in zd]
            dps = [x[:, BQ:] for x in zd]
            prs, dzl = [], []
            for c, (p, b) in enumerate(CHAINS):
                if fox:
                    kk = pl.ds(pl.multiple_of(j * BQ, BQ), BQ)
                    col = b * NH + 2 * p
                    z = zs[c] + (cqs[c] - jnp.concatenate([ckb_v[col, kk, :], ckb_v[col + 1, kk, :]], axis=0))
                    if diag:
                        z = jnp.where(valid, z, NEG)
                else:
                    z = zs[c] + tbl_ref[p, i - j]
                pr = jnp.exp(z - lses[c])
                prs.append(pr.astype(BF16))
                dzl.append(pr * (dps[c] - deltas[c]))
            dzbs = [dz.astype(BF16) for dz in dzl]
            new = []
            for c, (p, b) in enumerate(CHAINS):
                new.append(dqs[c] + _dot(_stack_t(kt_v[p, :, _rows(b, j)], r0, r1), dzbs[c]))
                kd = _dot(jnp.concatenate([dzbs[c], prs[c]], axis=1), w_v[c, 1])
                dk_s[p, _rows(b, j), :] += _merge_heads(kd[:, :BQ], first)
                dv_s[p, _rows(b, j), :] += _merge_heads(kd[:, BQ:], first)
                if fox:
                    dc_s[c, pl.ds(pl.multiple_of(j * HB, HB), HB), :] += dzl[c]
            if not fox:
                for p in range(2):
                    db_ref[p, i - j] = db_ref[p, i - j] + (dzl[2 * p] + dzl[2 * p + 1])
            return tuple(new)

        def qblock(i, _):
            qns = [_q_tile(qkv_v, p, b, i) for p, b in CHAINS]
            dns = [do_v[_rows(b, i), p * BQ:(p + 1) * BQ] for p, b in CHAINS]
            for c in range(NC):
                for r, (x, y) in enumerate(((_tr(qns[c]), _tr(dns[c])), (qns[c], dns[c]))):
                    w_v[c, r, :BQ, :BQ] = x
                    w_v[c, r, BQ:, BQ:] = y
            lses = [_bcast_heads(_stat_row(lse_ref, p, b, 0, i), _stat_row(lse_ref, p, b, 1, i)) for p, b in CHAINS]
            if fox:
                cqs = [_bcast_heads(_stat_row(cq_ref, p, b, 0, i), _stat_row(cq_ref, p, b, 1, i)) for p, b in CHAINS]
            else:
                cqs = [None] * NC
            deltas = []
            for c, (p, b) in enumerate(CHAINS):
                pt = (dns[c].astype(F32) * o_v[_rows(b, i), p * BQ:(p + 1) * BQ].astype(F32)).T
                deltas.append(_bcast_heads(jnp.sum(pt[:64], axis=0, keepdims=True), jnp.sum(pt[64:], axis=0, keepdims=True)))
            dqs = (jnp.zeros((BQ, BQ), F32),) * NC
            dqs = _blocks_loop(i, FLASH_PER_TRIP, lambda j, d: steps(cqs, lses, deltas, i, j, d, False), dqs)
            dqs = steps(cqs, lses, deltas, i, i, dqs, True)
            for c, (p, b) in enumerate(CHAINS):
                dq_v[_rows(b, i), p * PAIRW:p * PAIRW + BQ] = (dqs[c].T * SCALE).astype(BF16)
            return 0

        lax.fori_loop(0, NB, qblock, 0)
        for p in range(2):
            dq_v[:, p * PAIRW + BQ:p * PAIRW + 2 * BQ] = dk_s[p].astype(BF16)
            dq_v[:, p * PAIRW + 2 * BQ:p * PAIRW + 3 * BQ] = dv_s[p].astype(BF16)
        _copy_in(dq_v, dqkv_hbm.at[:, pl.ds(g * 2 * PAIRW, 2 * PAIRW)], sem)
        if fox:
            lane = _iota((BQ, NSTAT), 1)

            def fold(n, _):
                t = jnp.zeros((BQ, NSTAT), F32)
                for c, (p, b) in enumerate(CHAINS):
                    s = jnp.sum(dc_s[c, pl.ds(pl.multiple_of(n * HB, HB), HB), :], axis=1, keepdims=True)
                    col = b * NH + 2 * p
                    t = t - jnp.where(lane == col, s[:BQ], 0.0) - jnp.where(lane == col + 1, s[BQ:], 0.0)
                db_ref[pl.ds(pl.multiple_of(n * BQ, BQ), BQ), :] = t
                return 0

            lax.fori_loop(0, NB, fold, 0)
        if comm:
            comm[1](st)

    if fox:
        bias_specs = [VMEM_SPEC, ANY_SPEC]
        db_shape = jax.ShapeDtypeStruct((S, NSTAT), F32)
        more = [SLAB_KEYB, pltpu.VMEM((NC, NB * HB, BQ), F32)]
    else:
        bias_specs = [VMEM_SPEC]
        db_shape = jax.ShapeDtypeStruct((2, NB, HB, BQ), F32)
        more = []
    n_in = 5 + len(bias_specs)
    (dqkv, db), extra = _host_call(
        body, "fox_bwd" if fox else "dil_bwd", [qkv, o, dmixed, lse, *bias, dqkv],
        [ANY_SPEC, ANY_SPEC, ANY_SPEC, VMEM_SPEC] + bias_specs + [ANY_SPEC],
        [jax.ShapeDtypeStruct((T, QKVW), BF16), db_shape], [ANY_SPEC, VMEM_SPEC],
        [SLAB_QKV, SLAB_O32 if fox else SLAB_OUT, SLAB_OUT, SLAB_QKV, ACC_KV, ACC_KV, pltpu.SemaphoreType.DMA((4,)), SLAB_T,
         PAIR_DIAG] + more, {n_in - 1: 0}, job)
    return dqkv, db, extra


def _delta_t(d):
    return d * BQ + _iota((HB, BQ), 1) - (_iota((HB, BQ), 0) & (BQ - 1))


def _buckets_in(d):
    lo, hi = max(d * BQ - (BQ - 1), 0), d * BQ + BQ - 1
    return [b for b in range(32) if BUCKET_TH[b] <= hi and (b == 31 or BUCKET_TH[b + 1] > lo)]


def _in_bucket(delta, b):
    m = delta >= BUCKET_TH[b]
    return m if b == 31 else m & (delta < BUCKET_TH[b + 1])


def _dil_table(rel_bias, job=None):
    def body(ins, outs, scr, comm):
        (rb_ref,), (o_ref,) = ins, outs
        st = comm[0]() if comm else None
        for d in range(NB):
            delta = _delta_t(d)
            pos = delta >= 0
            n = ((pos & (delta <= 128)).astype(jnp.int32)
                 + (pos & (delta <= 512) & ((delta & 3) == 0)).astype(jnp.int32)
                 + (pos & ((delta & 15) == 0)).astype(jnp.int32))
            logn = jnp.where(n == 3, math.log(3.0), jnp.where(n == 2, math.log(2.0), jnp.where(n == 1, 0.0, NEG)))
            head1 = _iota((HB, BQ), 0) >= BQ
            for p in range(2):
                val = jnp.zeros((HB, BQ), F32)
                for b in _buckets_in(d):
                    val = jnp.where(_in_bucket(delta, b), jnp.where(head1, rb_ref[b, 2 * p + 1], rb_ref[b, 2 * p]), val)
                o_ref[p, d] = val + logn
        if comm:
            comm[1](st)

    (tbl,), extra = _host_call(
        body, "dil_table", [rel_bias], [pl.BlockSpec(memory_space=pltpu.SMEM)],
        [jax.ShapeDtypeStruct((2, NB, HB, BQ), F32)], [VMEM_SPEC], [], {}, job)
    return (tbl, extra) if job else tbl


def _dil_table_bwd(dtbl):
    def body(dt_ref, o_ref):
        p = pl.program_id(0)
        rowi = _iota((32, BQ), 0)
        lanei = _iota((32, BQ), 1)

        @pl.when(p == 0)
        def _():
            o_ref[...] = jnp.zeros_like(o_ref)

        out = jnp.zeros((32, BQ), F32)
        for b in range(32):
            acc = None
            for d in range(NB):
                if b in _buckets_in(d):
                    t = jnp.where(_in_bucket(_delta_t(d), b), dt_ref[d], 0.0)
                    acc = t if acc is None else acc + t
            rs = jnp.sum(acc, axis=1, keepdims=True)
            s0 = jnp.sum(rs[:BQ], axis=0, keepdims=True)
            s1 = jnp.sum(rs[BQ:], axis=0, keepdims=True)
            out = (out + jnp.where((rowi == b) & (lanei == 2 * p), s0, 0.0)
                   + jnp.where((rowi == b) & (lanei == 2 * p + 1), s1, 0.0))
        o_ref[...] += out

    return pl.pallas_call(
        body, name="dil_table_bwd", grid=(2,),
        in_specs=[pl.BlockSpec((None, NB, HB, BQ), lambda p: (p, 0, 0, 0))],
        out_specs=pl.BlockSpec((32, BQ), lambda p: (0, 0)),
        out_shape=jax.ShapeDtypeStruct((32, BQ), F32),
        compiler_params=_cp(("arbitrary",)))(dtbl)


def _fox_prep(gate, fb):
    def body(g_ref, fb_ref, c_ref):
        tri = (_iota((BQ, BQ), 0) >= _iota((BQ, BQ), 1)).astype(BF16)

        def blk(i, carry):
            r0 = pl.multiple_of(i * BQ, BQ)
            lf = _log_sigmoid(g_ref[pl.ds(r0, BQ), :] + fb_ref[...])
            c = _dot(tri, _split3(lf))
            c_ref[pl.ds(r0, BQ), :] = c[:, 0:BQ] + c[:, BQ:2 * BQ] + c[:, 2 * BQ:3 * BQ] + carry
            return carry + jnp.sum(lf, axis=0, keepdims=True)

        lax.fori_loop(0, NB, blk, jnp.zeros((1, BQ), F32))

    blk = pl.BlockSpec((S, GATEW), lambda b: (b, 0))
    return pl.pallas_call(
        body, name="fox_prep", grid=(BL,), in_specs=[blk, pl.BlockSpec((1, GATEW), lambda b: (0, 0))],
        out_specs=blk, out_shape=jax.ShapeDtypeStruct((T, GATEW), F32),
        compiler_params=_cp(("parallel",)))(gate, fb)


def _fox_post(dcum, gate, fb):
    def body(dc_ref, g_ref, fb_ref, dg_ref, dfb_ref):
        b = pl.program_id(0)
        tri = (_iota((BQ, BQ), 0) <= _iota((BQ, BQ), 1)).astype(BF16)

        def blk(ii, carry):
            csum, dfb = carry
            r0 = pl.multiple_of((NB - 1 - ii) * BQ, BQ)
            dc = dc_ref[pl.ds(r0, BQ), :]
            c = _dot(tri, _split3(dc))
            dlf = c[:, 0:BQ] + c[:, BQ:2 * BQ] + c[:, 2 * BQ:3 * BQ] + csum
            dg = dlf * jnp.exp(_log_sigmoid(-(g_ref[pl.ds(r0, BQ), :] + fb_ref[...])))
            dg_ref[pl.ds(r0, BQ), :] = dg
            return csum + jnp.sum(dc, axis=0, keepdims=True), dfb + jnp.sum(dg, axis=0, keepdims=True)

        z = jnp.zeros((1, BQ), F32)
        _, dfb = lax.fori_loop(0, NB, blk, (z, z))

        @pl.when(b == 0)
        def _():
            dfb_ref[...] = dfb

        @pl.when(b > 0)
        def _():
            dfb_ref[...] += dfb

    blk = pl.BlockSpec((S, GATEW), lambda b: (b, 0))
    vec = pl.BlockSpec((1, GATEW), lambda b: (0, 0))
    return pl.pallas_call(
        body, name="fox_post", grid=(BL,), in_specs=[blk, blk, vec], out_specs=[blk, vec],
        out_shape=[jax.ShapeDtypeStruct((T, GATEW), F32), jax.ShapeDtypeStruct((1, GATEW), F32)],
        compiler_params=_cp(("arbitrary",)))(dcum, gate, fb)


def _shift_down(x, n):
    return jnp.where(_iota(x.shape, 0) >= n, pltpu.roll(x, n, 0), 0.0)


def _shift_up(x, n):
    return jnp.where(_iota(x.shape, 0) < S - n, pltpu.roll(x, S - n, 0), 0.0)


def _conv_fwd(conv, cw, mixed):
    W = 256

    def body(c_ref, w_ref, _, o_ref):
        u = c_ref[:, W:2 * W] * c_ref[:, 2 * W:3 * W]
        y = w_ref[0:1, :] * _shift_down(u, 2) + w_ref[1:2, :] * _shift_down(u, 1) + w_ref[2:3, :] * u
        o_ref[...] = (c_ref[:, 0:W] * y).astype(BF16)

    return pl.pallas_call(
        body, name="conv_fwd", grid=(BL,),
        in_specs=[pl.BlockSpec((S, CONVW), lambda b: (b, 0)), pl.BlockSpec((8, W), lambda b: (0, 0)), ANY_SPEC],
        out_specs=pl.BlockSpec((S, W), lambda b: (b, 3)),
        out_shape=jax.ShapeDtypeStruct((T, D), BF16), input_output_aliases={2: 0},
        compiler_params=_cp(("parallel",)))(conv, cw, mixed)


def _conv_bwd(conv, cw, dmixed):
    W = 256

    def body(c_ref, w_ref, do_ref, dc_ref, dw_ref):
        b = pl.program_id(0)
        bg = c_ref[:, 0:W]
        cg = c_ref[:, W:2 * W]
        hv = c_ref[:, 2 * W:3 * W]
        do = do_ref[...].astype(F32)
        u = cg * hv
        u1 = _shift_down(u, 1)
        u2 = _shift_down(u, 2)
        y = w_ref[0:1, :] * u2 + w_ref[1:2, :] * u1 + w_ref[2:3, :] * u
        dy = do * bg
        du = w_ref[2:3, :] * dy + w_ref[1:2, :] * _shift_up(dy, 1) + w_ref[0:1, :] * _shift_up(dy, 2)
        dc_ref[:, 0:W] = (do * y).astype(BF16)
        dc_ref[:, W:2 * W] = (du * hv).astype(BF16)
        dc_ref[:, 2 * W:3 * W] = (du * cg).astype(BF16)
        rowi = _iota((8, W), 0)
        dw = (jnp.where(rowi == 0, jnp.sum(dy * u2, axis=0, keepdims=True), 0.0)
              + jnp.where(rowi == 1, jnp.sum(dy * u1, axis=0, keepdims=True), 0.0)
              + jnp.where(rowi == 2, jnp.sum(dy * u, axis=0, keepdims=True), 0.0))

        @pl.when(b == 0)
        def _():
            dw_ref[...] = dw

        @pl.when(b > 0)
        def _():
            dw_ref[...] += dw

    return pl.pallas_call(
        body, name="conv_bwd", grid=(BL,),
        in_specs=[pl.BlockSpec((S, CONVW), lambda b: (b, 0)), pl.BlockSpec((8, W), lambda b: (0, 0)),
                  pl.BlockSpec((S, W), lambda b: (b, 3))],
        out_specs=[pl.BlockSpec((S, CONVW), lambda b: (b, 0)), pl.BlockSpec((8, W), lambda b: (0, 0))],
        out_shape=[jax.ShapeDtypeStruct((T, CONVW), BF16), jax.ShapeDtypeStruct((8, W), F32)],
        compiler_params=_cp(("arbitrary",)))(conv, cw, dmixed)


def _place():
    x, y, c = lax.axis_index("x"), lax.axis_index("y"), lax.axis_index("c")
    return x, y, c


def _chips_of(x, y):
    return [(1 - x, y), (x, 1 - y), (1 - x, 1 - y)]


def _dev(p):
    return 4 * p[0] + 2 * p[1] + p[2]


def _gather_job_a(shards):
    n = len(shards)

    def peers(x, y, c):
        return [(x, y, 1 - c)] + [(*chip, c) for chip in _chips_of(x, y)]

    def start(ins, outs, sems):
        send, recv, loc = sems
        x, y, c = _place()
        me = (x, y, c)
        cps = []
        for a in range(n):
            cps.append(pltpu.make_async_copy(ins[a], outs[a].at[_dev(me)], loc.at[a]))
            for k, peer in enumerate(peers(x, y, c)):
                cps.append(pltpu.make_async_remote_copy(
                    src_ref=ins[a], dst_ref=outs[a].at[_dev(me)], send_sem=send.at[a, k], recv_sem=recv.at[a, k],
                    device_id=peer, device_id_type=MESH))
        for cp in cps:
            cp.start()
        return cps

    def finish(cps, ins, outs, sems):
        send, recv, loc = sems
        x, y, c = _place()
        for a in range(n):
            for k, peer in enumerate(peers(x, y, c)):
                pltpu.make_async_remote_copy(
                    src_ref=ins[a], dst_ref=outs[a].at[_dev(peer)], send_sem=send.at[a, k], recv_sem=recv.at[a, k],
                    device_id=(x, y, c), device_id_type=MESH).wait_recv()
        for a in range(n):
            cps[5 * a].wait()
            for k in range(4):
                cps[5 * a + 1 + k].wait_send()

    return _Job(shards, [jax.ShapeDtypeStruct((NDEV,) + s.shape, s.dtype) for s in shards], {},
                [pltpu.SemaphoreType.DMA((n, 4)), pltpu.SemaphoreType.DMA((n, 4)), pltpu.SemaphoreType.DMA((n,))],
                start, finish)


def _gather_job_b(gathered):
    n = len(gathered)

    def start(ins, outs, sems):
        send, recv = sems
        x, y, c = _place()
        cps = []
        for a in range(n):
            for j, chip in enumerate(_chips_of(x, y)):
                blk = outs[a].at[_dev((*chip, c))]
                cps.append(pltpu.make_async_remote_copy(
                    src_ref=blk, dst_ref=blk, send_sem=send.at[a, j], recv_sem=recv.at[a, j],
                    device_id=(x, y, 1 - c), device_id_type=MESH))
        for cp in cps:
            cp.start()
        return cps

    def finish(cps, ins, outs, sems):
        send, recv = sems
        x, y, c = _place()
        for a in range(n):
            for j, chip in enumerate(_chips_of(x, y)):
                blk = outs[a].at[_dev((*chip, 1 - c))]
                pltpu.make_async_remote_copy(
                    src_ref=blk, dst_ref=blk, send_sem=send.at[a, j], recv_sem=recv.at[a, j],
                    device_id=(x, y, c), device_id_type=MESH).wait_recv()
        for cp in cps:
            cp.wait_send()

    return _Job(gathered, [jax.ShapeDtypeStruct(g.shape, g.dtype) for g in gathered], {a: a for a in range(n)},
                [pltpu.SemaphoreType.DMA((n, 3)), pltpu.SemaphoreType.DMA((n, 3))], start, finish)


def _sibling_job(grads):
    n = len(grads)

    def start(ins, outs, sems):
        send, recv = sems
        x, y, c = _place()
        cps = [pltpu.make_async_remote_copy(
            src_ref=ins[a].at[:, 1 - c], dst_ref=outs[a], send_sem=send.at[a], recv_sem=recv.at[a],
            device_id=(x, y, 1 - c), device_id_type=MESH) for a in range(n)]
        for cp in cps:
            cp.start()
        return cps

    def finish(cps, ins, outs, sems):
        for cp in cps:
            cp.wait()

    return _Job(grads, [jax.ShapeDtypeStruct(g.shape[:1] + g.shape[2:], F32) for g in grads], {},
                [pltpu.SemaphoreType.DMA((n,)), pltpu.SemaphoreType.DMA((n,))], start, finish)


def _chip_job(psums):
    n = len(psums)

    def copies(ins, outs, sems):
        send, recv, loc = sems
        x, y, c = _place()
        mychip = 2 * x + y
        cps = []
        for a in range(n):
            cps.append(pltpu.make_async_copy(ins[a].at[mychip], outs[a].at[mychip], loc.at[a]))
            for j, chip in enumerate(_chips_of(x, y)):
                cps.append(pltpu.make_async_remote_copy(
                    src_ref=ins[a].at[2 * chip[0] + chip[1]], dst_ref=outs[a].at[mychip],
                    send_sem=send.at[a, j], recv_sem=recv.at[a, j], device_id=(*chip, c), device_id_type=MESH))
        return cps

    def start(ins, outs, sems):
        for cp in copies(ins, outs, sems):
            cp.start()

    def finish(_, ins, outs, sems):
        cps = copies(ins, outs, sems)
        send, recv, loc = sems
        x, y, c = _place()
        mychip = 2 * x + y
        for a in range(n):
            for j, chip in enumerate(_chips_of(x, y)):
                pltpu.make_async_remote_copy(
                    src_ref=ins[a].at[mychip], dst_ref=outs[a].at[2 * chip[0] + chip[1]],
                    send_sem=send.at[a, j], recv_sem=recv.at[a, j], device_id=(x, y, c), device_id_type=MESH).wait_recv()
        for a in range(n):
            cps[4 * a].wait()
            for j in range(3):
                cps[4 * a + 1 + j].wait_send()

    return _Job(psums, [jax.ShapeDtypeStruct(p.shape, BF16) for p in psums], {},
                [pltpu.SemaphoreType.DMA((n, 3)), pltpu.SemaphoreType.DMA((n, 3)), pltpu.SemaphoreType.DMA((n,))],
                start, finish)


def _join_jobs(*jobs):
    jobs = [j for j in jobs if j is not None]
    if len(jobs) <= 1:
        return jobs[0] if jobs else None
    cut = lambda seq, sizes: [seq[sum(sizes[:k]):sum(sizes[:k + 1])] for k in range(len(sizes))]
    n_in = [len(j.ins) for j in jobs]
    n_out = [len(j.out_shapes) for j in jobs]
    n_sem = [len(j.sems) for j in jobs]
    aliases = {}
    for k, j in enumerate(jobs):
        for a, b in j.aliases.items():
            aliases[sum(n_in[:k]) + a] = sum(n_out[:k]) + b

    def start(ins, outs, sems):
        return [j.start(i, o, s) for j, i, o, s in zip(jobs, cut(ins, n_in), cut(outs, n_out), cut(sems, n_sem))]

    def finish(sts, ins, outs, sems):
        for j, st, i, o, s in zip(jobs, sts, cut(ins, n_in), cut(outs, n_out), cut(sems, n_sem)):
            j.finish(st, i, o, s)

    return _Job([t for j in jobs for t in j.ins], [t for j in jobs for t in j.out_shapes], aliases,
                [t for j in jobs for t in j.sems], start, finish)


def _run_job(job, name):
    def body(ins, outs, scr, comm):
        comm[1](comm[0]())

    return _host_call(body, name, [], [], [], [], [], {}, job)[1]


def _allreduce_small(v, job=None):
    def body(ins, outs, scr, comm):
        (v_ref,), (o_ref,), (slots, send_sems, recv_sems) = ins, outs, scr
        st = comm[0]() if comm else None
        x, y, c = _place()
        me = 4 * x + 2 * y + c
        slots[me] = v_ref[...]

        def copy(k):
            peer = (x ^ ((k >> 2) & 1), y ^ ((k >> 1) & 1), c ^ (k & 1))
            return pltpu.make_async_remote_copy(
                src_ref=v_ref, dst_ref=slots.at[me], send_sem=send_sems.at[k - 1], recv_sem=recv_sems.at[k - 1],
                device_id=peer, device_id_type=MESH)

        def arrival(k):
            return pltpu.make_async_remote_copy(
                src_ref=v_ref, dst_ref=slots.at[me ^ k], send_sem=send_sems.at[k - 1], recv_sem=recv_sems.at[k - 1],
                device_id=(x, y, c), device_id_type=MESH)

        sends = [copy(k) for k in range(1, NDEV)]
        for cp in sends:
            cp.start()
        for k in range(1, NDEV):
            arrival(k).wait_recv()
        for cp in sends:
            cp.wait_send()
        acc = slots[0]
        for d in range(1, NDEV):
            acc = acc + slots[d]
        o_ref[...] = acc
        if comm:
            comm[1](st)

    (out,), extra = _host_call(
        body, "allreduce_small", [v], [VMEM_SPEC], [jax.ShapeDtypeStruct(v.shape, F32)], [VMEM_SPEC],
        [pltpu.VMEM((NDEV,) + v.shape, F32), pltpu.SemaphoreType.DMA((NDEV - 1,)),
         pltpu.SemaphoreType.DMA((NDEV - 1,))], {}, job)
    return (out, extra) if job else out


def _pair_sums(views, gots, core):
    n = len(views)

    def body(c_ref, *refs):
        for a in range(n):
            refs[2 * n + a][...] = (refs[a][...] + refs[n + a][...]).astype(BF16)

    def vspec(v):
        return pl.BlockSpec((None, None, v.shape[2] // 2, v.shape[3]), lambda k, h, c: (k, c[0], h, 0))

    def gspec(g):
        return pl.BlockSpec((None, g.shape[1] // 2, g.shape[2]), lambda k, h, c: (k, h, 0))

    return pl.pallas_call(
        body, name="pair_sums",
        grid_spec=pltpu.PrefetchScalarGridSpec(
            num_scalar_prefetch=1, grid=(4, 2),
            in_specs=[vspec(v) for v in views] + [gspec(g) for g in gots],
            out_specs=[gspec(g) for g in gots]),
        out_shape=[jax.ShapeDtypeStruct(g.shape, BF16) for g in gots],
        compiler_params=_cp(("parallel", "parallel")))(core, *views, *gots)


def _chip_sums(parts):
    n = len(parts)

    def body(*refs):
        for a in range(n):
            acc = refs[a][0].astype(F32)
            for k in range(1, 4):
                acc = acc + refs[a][k].astype(F32)
            refs[n + a][...] = acc

    return pl.pallas_call(
        body, name="chip_sums", in_specs=[VMEM_SPEC] * n, out_specs=[VMEM_SPEC] * n,
        out_shape=[jax.ShapeDtypeStruct(p.shape[1:], F32) for p in parts], compiler_params=_cp())(*parts)


def _permute_in(w):
    lead = w.shape[:-1]
    return w.reshape(lead + (3, 3, 2, BQ)).swapaxes(-2, -3).reshape(lead + (QKVW,))


def _unpermute_in(w):
    lead = w.shape[:-1]
    return w.reshape(lead + (3, 2, 3, BQ)).swapaxes(-2, -3).reshape(lead + (QKVW,))


def _row(v):
    v = v.reshape(-1)
    return jnp.pad(v, (0, D - v.shape[0])).reshape(1, D)


def kernel(x, w_in, f_bias, conv_w, w_out, rel_bias, ln1_g, ln1_b, w_gate, w_up, w_down, ln2_g, ln2_b, loss_target, m_w_in, m_f_bias, m_conv_w, m_w_out, m_rel_bias, m_ln1_g, m_ln1_b, m_w_gate, m_w_up, m_w_down, m_ln2_g, m_ln2_b, v_w_in, v_f_bias, v_conv_w, v_w_out, v_rel_bias, v_ln1_g, v_ln1_b, v_w_gate, v_w_up, v_w_down, v_ln2_g, v_ln2_b):
    xi, yi, ci = _place()
    me = 4 * xi + 2 * yi + ci
    core = jnp.reshape(ci, (1,)).astype(jnp.int32)

    win_s = jnp.concatenate([_permute_in(w_in[..., :QKVW]), w_in[..., QKVW:]], axis=-1)
    win_s = jnp.pad(win_s, ((0, 0), (0, 0), (0, NPAD - NPROJ))).astype(BF16)
    per_layer = [win_s, w_out.astype(BF16), jnp.swapaxes(w_gate, 1, 2).astype(BF16),
                 jnp.swapaxes(w_up, 1, 2).astype(BF16), w_down.astype(BF16)]
    sh = [[s[l] for s in per_layer] for l in range(2)]

    def whole(g):
        return g.reshape(NDEV * g.shape[1], g.shape[2])

    cw_rows = lax.dynamic_update_slice(jnp.zeros((2, 3, 256), F32), conv_w, (0, 0, me * 32))
    small = jnp.concatenate([_row(cw_rows[0]), _row(cw_rows[1]), jnp.zeros((SMALL_ROWS - 2, D), F32)], axis=0)
    small, leg_a = _allreduce_small(small, job=_gather_job_a(sh[0][:1]))
    cw_full = small[0:2, :CONVW].reshape(2, 3, 256)
    cw8 = jnp.pad(cw_full, ((0, 0), (0, 5), (0, 0)))
    fb = jnp.pad(f_bias, ((0, 0), (0, GATEW - NH))).reshape(2, 1, GATEW)
    tbl, leg_b = _dil_table(rel_bias, job=_gather_job_b(list(leg_a)))
    W = [{"win": whole(leg_b[0])}, {}]

    def wrow(tn, K, blk=0):
        return pl.BlockSpec((tn, K), lambda i, j: (j, blk))

    def arow(tm, K, blk=0):
        return pl.BlockSpec((tm, K), lambda i, j: (i, blk))

    h = x.reshape(T, D)
    hb = h.astype(BF16)
    saved = []
    for l in range(2):
        Win = W[l]["win"]
        qkv, conv, gate = _proj(hb, Win)
        cum = _fox_prep(gate, fb[l])
        cq = cum[:, :NH].reshape(BL, S, NH).transpose(0, 2, 1).reshape(NSTAT, S)
        ckb = jnp.broadcast_to(cq[:, :, None], (NSTAT, S, BQ))
        if l == 0:
            mixed, amat, a0 = _sb_fwd(qkv, job=_gather_job_a(sh[0][1:]))
            mixed, lse_d, ex = _flash_fwd(qkv, mixed, 1, False, (tbl,),
                                          job=_join_jobs(_gather_job_b(list(a0)), _gather_job_a(sh[1][:2])))
            W[0].update(zip(("wout", "wgT", "wuT", "wd"), [whole(t) for t in ex[:4]]))
            mixed, lse_f, o_fox, ex = _flash_fwd(qkv, mixed, 2, True, (cq, ckb),
                                                 job=_join_jobs(_gather_job_b(list(ex[4:])), _gather_job_a(sh[1][2:])))
            W[1].update(zip(("win", "wout"), [whole(t) for t in ex[:2]]))
            a2 = list(ex[2:])
        else:
            mixed, amat, ex = _sb_fwd(qkv, job=_gather_job_b(a2))
            W[1].update(zip(("wgT", "wuT", "wd"), [whole(t) for t in ex]))
            mixed, lse_d, _ = _flash_fwd(qkv, mixed, 1, False, (tbl,))
            mixed, lse_f, o_fox, _ = _flash_fwd(qkv, mixed, 2, True, (cq, ckb))
        Wout, WgT, WuT, Wd = W[l]["wout"], W[l]["wgT"], W[l]["wuT"], W[l]["wd"]
        mixed = _conv_fwd(conv, cw8[l], mixed)
        x1, xh1, r1, x1b = _mm_ln(mixed, Wout, h, ln1_g[l:l + 1], ln1_b[l:l + 1], "out_proj_ln")
        fs, ft, a, x2, xh2, r2, x2b = _ffn_fwd(x1b, x1, WgT, WuT, Wd, ln2_g[l:l + 1], ln2_b[l:l + 1])
        saved.append(dict(h=hb, qkv=qkv, conv=conv, gate=gate, cq=cq, ckb=ckb, mixed=mixed, amat=amat, lse_d=lse_d,
                          lse_f=lse_f, o_fox=o_fox, x1=x1b, xh1=xh1, r1=r1, fs=fs, ft=ft, a=a, xh2=xh2, r2=r2))
        h, hb = x2, x2b

    dy = h

    def view(gr):
        return gr.reshape(4, 2, gr.shape[0] // NDEV, gr.shape[1])

    G = [None, None]
    small_g = {}
    shard_g = {}
    for l in (1, 0):
        sv = saved[l]
        Win, Wout, WgT, WuT, Wd = W[l]["win"], W[l]["wout"], W[l]["wgT"], W[l]["wuT"], W[l]["wd"]
        res = _ffn_bwd(dy, sv["xh2"], sv["r2"], ln2_g[l:l + 1], sv["fs"], sv["ft"], Wd, WgT, WuT,
                       target=loss_target.reshape(T, D) if l == 1 else None)
        dgt, dut, ds2b, dx1, dg2, db2 = res[:6]
        if l == 1:
            sq = res[6]
        G_d = _mm_tn(sv["a"], ds2b, None, C=D, Ka=DFF, N=D, tm=256, tn=1024, tk=T, ooff=0, name="grad_w_down")
        G_g = _mm_tn(dgt, sv["x1"], None, C=D, Ka=DFF, N=D, tm=256, tn=1024, tk=T, ooff=0, name="grad_w_gate")
        G_u = _mm_tn(dut, sv["x1"], None, C=D, Ka=DFF, N=D, tm=256, tn=1024, tk=T, ooff=0, name="grad_w_up")
        ds1, dg1, db1, ds1b, dmixed = _ln_bwd(dx1, sv["xh1"], sv["r1"], ln1_g[l:l + 1], Wout)
        G_out = _mm_tn(sv["mixed"], ds1b, None, C=D, Ka=D, N=D, tm=256, tn=1024, tk=T, ooff=0, name="grad_w_out")
        early = [view(t) for t in (G_g, G_u, G_d, G_out)] + ([view(G[1]["in"])] if l == 0 else [])
        dqkv, gots = _sb_bwd(sv["qkv"], dmixed, sv["amat"], job=_sibling_job(early))
        ps = _pair_sums(early, list(gots), core)
        dqkv, dtbl, pa = _flash_bwd(sv["qkv"], sv["mixed"], dmixed, sv["lse_d"], dqkv, 1, False, (tbl,),
                                    job=_chip_job(ps[:2]))
        dqkv, dck, pb = _flash_bwd(sv["qkv"], sv["o_fox"], dmixed, sv["lse_f"], dqkv, 2, True,
                                   (sv["cq"], sv["ckb"]), job=_chip_job(ps[2:]))
        sums = _chip_sums(list(pa) + list(pb))
        shard_g[l] = dict(zip(("g", "u", "d", "out"), sums[:4]))
        if l == 0:
            shard_g[1]["in"] = sums[4]
        dconv, dcw = _conv_bwd(sv["conv"], cw8[l], dmixed)
        dcum = jnp.pad(dck.reshape(S, BL, NH).transpose(1, 0, 2).reshape(T, NH), ((0, 0), (0, GATEW - NH)))
        dgate, dfb = _fox_post(dcum, sv["gate"], fb[l])
        drb = _dil_table_bwd(dtbl)
        G_in = _mm_tn(sv["h"], dqkv, None, C=NPAD, Ka=D, N=QKVW, tm=512, tn=768, tk=T, ooff=0, name="grad_w_in_qkv")
        G_in = _mm_tn(sv["h"], dconv, G_in, C=NPAD, Ka=D, N=CONVW, tm=256, tn=768, tk=T, ooff=3,
                      name="grad_w_in_conv")
        G_in = _mm_tn(sv["h"], dgate, G_in, C=NPAD, Ka=D, N=GATEW, tm=1024, tn=128, tk=1024, ooff=24,
                      name="grad_w_in_gate")
        G[l] = {"in": G_in, "out": G_out, "g": G_g, "u": G_u, "d": G_d}
        if l == 0:
            late = [view(G_in)]
            tail = _chip_job(_pair_sums(late, list(_run_job(_sibling_job(late), "sibling_exchange")), core))
            dy, parts = _mm([(dqkv, arow(1024, QKVW), Win, wrow(512, QKVW, 0)),
                             (dconv, arow(1024, CONVW), Win, wrow(512, CONVW, 3)),
                             (dgate, arow(1024, GATEW), Win, wrow(512, GATEW, 24))],
                            nt=True, M=T, N=D, tm=1024, tn=512, out_dtype=F32, name="proj_dx", res=ds1,
                            res_scale=ALPHA, job=tail)
            shard_g[0]["in"] = _chip_sums(list(parts))[0]
        else:
            dy = _proj_bwd(dqkv, dconv, dgate, Win, ds1)
        small_g[l] = dict(ln1_g=dg1, ln1_b=db1, ln2_g=dg2, ln2_b=db2, cw=dcw[0:3].reshape(1, CONVW),
                          fb=dfb[:, :NH], rb=drb[:, :NH])
    grad_x = dy.reshape(BL, S, D)

    rows = []
    for name in ("ln1_g", "ln1_b", "ln2_g", "ln2_b"):
        rows += [small_g[0][name], small_g[1][name]]
    rows += [_row(small_g[0]["cw"]), _row(small_g[1]["cw"]),
             _row(jnp.concatenate([small_g[0]["fb"], small_g[1]["fb"]], axis=0)),
             _row(small_g[0]["rb"] + small_g[1]["rb"]), _row(sq)]
    rows.append(jnp.zeros((SMALL_ROWS - len(rows), D), F32))
    sg = _allreduce_small(jnp.concatenate(rows, axis=0))
    loss = sg[12, 0] * (0.5 / D)
    g_ln1_g, g_ln1_b, g_ln2_g, g_ln2_b = sg[0:2], sg[2:4], sg[4:6], sg[6:8]
    g_conv_full = sg[8:10, :CONVW].reshape(2, 3, 256)
    g_conv = lax.dynamic_slice(g_conv_full, (0, 0, me * 32), (2, 3, 32))
    g_fb = sg[10, :2 * NH].reshape(2, NH)
    g_rb = sg[11, :32 * NH].reshape(32, NH)

    def both(name):
        return jnp.stack([shard_g[0][name], shard_g[1][name]])

    g_in = both("in")
    g_w_in = jnp.concatenate([_unpermute_in(g_in[..., :QKVW]), g_in[..., QKVW:NPROJ]], axis=-1)
    g_w_out = both("out")
    g_w_gate = jnp.swapaxes(both("g"), 1, 2)
    g_w_up = jnp.swapaxes(both("u"), 1, 2)
    g_w_down = both("d")

    up_in = _adamw(w_in, g_w_in, m_w_in, v_w_in, 64)
    up_out = _adamw(w_out, g_w_out, m_w_out, v_w_out, 128)
    up_gate = _adamw(w_gate, g_w_gate, m_w_gate, v_w_gate, 256)
    up_up = _adamw(w_up, g_w_up, m_w_up, v_w_up, 256)
    up_down = _adamw(w_down, g_w_down, m_w_down, v_w_down, 352)

    def pack(fbv, cwv, rbv, l1g, l1b, l2g, l2b):
        r = [l1g, l1b, l2g, l2b, _row(cwv), _row(fbv), _row(rbv)]
        r.append(jnp.zeros((SMALL_ROWS - 11, D), F32))
        return jnp.concatenate(r, axis=0)

    pw = pack(f_bias, conv_w, rel_bias, ln1_g, ln1_b, ln2_g, ln2_b)
    pg = pack(g_fb, g_conv, g_rb, g_ln1_g, g_ln1_b, g_ln2_g, g_ln2_b)
    pm = pack(m_f_bias, m_conv_w, m_rel_bias, m_ln1_g, m_ln1_b, m_ln2_g, m_ln2_b)
    pv = pack(v_f_bias, v_conv_w, v_rel_bias, v_ln1_g, v_ln1_b, v_ln2_g, v_ln2_b)
    ups = [u[0] for u in _adamw(pw[None], pg[None], pm[None], pv[None], SMALL_ROWS)]

    def unpack(p):
        return dict(ln1_g=p[0:2], ln1_b=p[2:4], ln2_g=p[4:6], ln2_b=p[6:8],
                    conv_w=p[8, :192].reshape(2, 3, 32), f_bias=p[9, :2 * NH].reshape(2, NH),
                    rel_bias=p[10, :32 * NH].reshape(32, NH))

    sm = [unpack(p) for p in ups]

    def group(k):
        return (up_in[k], sm[k]["f_bias"], sm[k]["conv_w"], up_out[k], sm[k]["rel_bias"], sm[k]["ln1_g"],
                sm[k]["ln1_b"], up_gate[k], up_up[k], up_down[k], sm[k]["ln2_g"], sm[k]["ln2_b"])

    grads = (g_w_in, g_fb, g_conv, g_w_out, g_rb, g_ln1_g, g_ln1_b, g_w_gate, g_w_up, g_w_down, g_ln2_g, g_ln2_b)
    return (loss, grad_x) + grads + group(0) + group(1) + group(2)
```

```python
import math

import numpy as np
import jax
import jax.numpy as jnp
from jax import lax
from jax.experimental import pallas as pl
from jax.experimental.pallas import tpu as pltpu

F32 = jnp.float32
BF16 = jnp.bfloat16
MESH = pl.DeviceIdType.MESH

D = 1024
S = 2048
BL = 2
T = BL * S
NH = 4
DFF = 2816
NPROJ = 3076
NPAD = 3200
QKVW = 2304
CONVW = 768
GATEW = 128
PAIRW = 384
BQ = 128
HB = 2 * BQ
NB = S // BQ
NDEV = 8
NSTAT = BL * NH
ALPHA = 4.0 ** 0.25
SCALE = 0.125
NEG = -1e30
LN_EPS = 1e-5
ADAM_LR, ADAM_B1, ADAM_B2, ADAM_EPS, ADAM_WD, ADAM_STEP = 0.001, 0.9, 0.999, 1e-08, 0.01, 10
VMEM_LIMIT = 56 * 1024 * 1024
SMALL_ROWS = 16


def _bucket_thresholds():
    d = np.arange(0, S)
    nf = np.maximum(d, 1).astype(np.float32)
    large = 16 + (np.log(nf / np.float32(16)) / np.float32(math.log(128)) * np.float32(16)).astype(np.int32)
    b = np.where(d < 16, d, np.minimum(large, 31))
    return [int(np.argmax(b >= k)) for k in range(32)]


BUCKET_TH = _bucket_thresholds()


def _cp(sem=None, vmem=VMEM_LIMIT):
    return pltpu.CompilerParams(dimension_semantics=sem, vmem_limit_bytes=vmem)


def _dot(a, b):
    return lax.dot_general(a, b, (((1,), (0,)), ((), ())), preferred_element_type=F32)


def _dot_nt(a, b):
    return lax.dot_general(a, b, (((1,), (1,)), ((), ())), preferred_element_type=F32)


def _dot_tn(a, b):
    return lax.dot_general(a, b, (((0,), (0,)), ((), ())), preferred_element_type=F32)


def _split2(x):
    hi = x.astype(BF16)
    mid = (x - hi.astype(F32)).astype(BF16)
    return jnp.concatenate([hi, mid], axis=1)


def _split3(x):
    hi = x.astype(BF16)
    r = x - hi.astype(F32)
    mid = r.astype(BF16)
    lo = (r - mid.astype(F32)).astype(BF16)
    return jnp.concatenate([hi, mid, lo], axis=1)


def _log_sigmoid(u):
    return jnp.minimum(u, 0.0) - jnp.log1p(jnp.exp(-jnp.abs(u)))


def _log_sigmoid_tile(u):
    return jnp.minimum(u, 0.0) - jnp.log(1.0 + jnp.exp(jnp.minimum(u, -u)))


def _iota(shape, dim):
    return lax.broadcasted_iota(jnp.int32, shape, dim)


ANY_SPEC = pl.BlockSpec(memory_space=pl.ANY)
VMEM_SPEC = pl.BlockSpec(memory_space=pltpu.VMEM)


def _mm(pairs, *, nt, M, N, tm, tn, out_dtype, name, res=None, res_scale=1.0, job=None):
    n = len(pairs)
    n_in = 2 * n + (res is not None)
    jins = job.ins if job else []
    jouts = job.out_shapes if job else []
    gi, gj = M // tm, N // tn

    def body(*refs):
        o_ref = refs[n_in + len(jins)]
        if job:
            jrefs = (refs[n_in:n_in + len(jins)], refs[n_in + len(jins) + 1:n_in + len(jins) + 1 + len(jouts)],
                     refs[n_in + len(jins) + 1 + len(jouts):])

            @pl.when((pl.program_id(0) == 0) & (pl.program_id(1) == 0))
            def _():
                job.start(*jrefs)

        acc = None
        for p in range(n):
            a = refs[2 * p][...].astype(BF16)
            b = refs[2 * p + 1][...]
            d = _dot_nt(a, b) if nt else _dot(a, b)
            acc = d if acc is None else acc + d
        if res is not None:
            acc = acc + res_scale * refs[2 * n][...]
        o_ref[...] = acc.astype(out_dtype)
        if job:
            @pl.when((pl.program_id(0) == gi - 1) & (pl.program_id(1) == gj - 1))
            def _():
                job.finish(None, *jrefs)

    ops, specs = [], []
    for a, asp, b, bsp in pairs:
        ops += [a, b]
        specs += [asp, bsp]
    if res is not None:
        ops.append(res)
        specs.append(pl.BlockSpec((tm, tn), lambda i, j: (i, j)))
    out = pl.pallas_call(
        body, name=name, grid=(gi, gj), in_specs=specs + [ANY_SPEC] * len(jins),
        out_specs=[pl.BlockSpec((tm, tn), lambda i, j: (i, j))] + [ANY_SPEC] * len(jouts),
        out_shape=[jax.ShapeDtypeStruct((M, N), out_dtype)] + list(jouts),
        scratch_shapes=list(job.sems) if job else [],
        input_output_aliases={n_in + a: 1 + b for a, b in job.aliases.items()} if job else {},
        compiler_params=_cp(("arbitrary", "arbitrary") if job else ("parallel", "parallel")))(*ops, *jins)
    return (out[0], out[1:]) if job else out[0]


def _mm_tn(a, b, gbuf, *, C, Ka, N, tm, tn, tk, ooff, name):
    def body(*refs):
        a_ref, b_ref, o_ref = refs[0], refs[1], refs[-1]
        k = pl.program_id(2)
        d = _dot_tn(a_ref[...].astype(BF16), b_ref[...].astype(BF16))

        @pl.when(k == 0)
        def _():
            o_ref[...] = d

        @pl.when(k > 0)
        def _():
            o_ref[...] += d

    ops = [a, b] + ([] if gbuf is None else [gbuf])
    return pl.pallas_call(
        body, name=name, grid=(Ka // tm, N // tn, T // tk),
        in_specs=[pl.BlockSpec((tk, tm), lambda i, j, k: (k, i)),
                  pl.BlockSpec((tk, tn), lambda i, j, k: (k, j))] + ([] if gbuf is None else [ANY_SPEC]),
        out_specs=pl.BlockSpec((tm, tn), lambda i, j, k: (i, ooff + j)),
        out_shape=jax.ShapeDtypeStruct((Ka, C), F32),
        input_output_aliases={} if gbuf is None else {2: 0},
        compiler_params=_cp(("parallel", "parallel", "arbitrary")))(*ops)


def _proj(xb, w):
    tm = 512

    def body(x_ref, w_ref, qkv_ref, conv_ref, gate_ref):
        xv = x_ref[...]
        qkv_ref[...] = _dot(xv, w_ref[:, 0:QKVW]).astype(BF16)
        conv_ref[...] = _dot(xv, w_ref[:, QKVW:QKVW + CONVW])
        gate_ref[...] = _dot(xv, w_ref[:, QKVW + CONVW:NPAD])

    def rows(n):
        return pl.BlockSpec((tm, n), lambda i: (i, 0))

    return pl.pallas_call(
        body, name="proj", grid=(T // tm,),
        in_specs=[rows(D), pl.BlockSpec((D, NPAD), lambda i: (0, 0))],
        out_specs=[rows(QKVW), rows(CONVW), rows(GATEW)],
        out_shape=[jax.ShapeDtypeStruct((T, QKVW), BF16), jax.ShapeDtypeStruct((T, CONVW), F32),
                   jax.ShapeDtypeStruct((T, GATEW), F32)],
        compiler_params=_cp(("parallel",)))(xb, w)


def _proj_bwd(dqkv, dconv, dgate, w, res):
    tm = 512

    def body(a_ref, b_ref, c_ref, w_ref, r_ref, o_ref):
        acc = ALPHA * r_ref[...] + _dot_nt(a_ref[...], w_ref[:, 0:QKVW])
        acc = acc + _dot_nt(b_ref[...], w_ref[:, QKVW:QKVW + CONVW])
        o_ref[...] = acc + _dot_nt(c_ref[...].astype(BF16), w_ref[:, QKVW + CONVW:NPAD])

    def rows(n):
        return pl.BlockSpec((tm, n), lambda i: (i, 0))

    return pl.pallas_call(
        body, name="proj_bwd", grid=(T // tm,),
        in_specs=[rows(QKVW), rows(CONVW), rows(GATEW), pl.BlockSpec((D, NPAD), lambda i: (0, 0)), rows(D)],
        out_specs=rows(D), out_shape=jax.ShapeDtypeStruct((T, D), F32),
        compiler_params=_cp(("parallel",)))(dqkv, dconv, dgate, w, res)


def _ffn_fwd(xb, x, wgt, wut, wd, gam, bet):
    tm, ch = 512, 256

    def body(xb_ref, x_ref, g_ref, b_ref, wg_hbm, wu_hbm, wd_hbm,
             go_ref, uo_ref, ao_ref, y_ref, xh_ref, r_ref, yb_ref, wg_v, wu_v, wd_v, sem):
        loads = [pltpu.make_async_copy(s, d, sem.at[k])
                 for k, (s, d) in enumerate(((wg_hbm, wg_v), (wu_hbm, wu_v), (wd_hbm, wd_v)))]

        @pl.when(pl.program_id(0) == 0)
        def _():
            for cp in loads:
                cp.start()
            loads[0].wait()
            loads[1].wait()

        xv = xb_ref[...]
        for c in range(0, DFF, ch):
            gv = _dot_nt(xv, wg_v[c:c + ch, :])
            uv = _dot_nt(xv, wu_v[c:c + ch, :])
            go_ref[:, c:c + ch] = gv.astype(BF16)
            uo_ref[:, c:c + ch] = uv.astype(BF16)
            ao_ref[:, c:c + ch] = (gv * jax.nn.sigmoid(gv) * uv).astype(BF16)
        @pl.when(pl.program_id(0) == 0)
        def _():
            loads[2].wait()

        s = ALPHA * x_ref[...] + _dot(ao_ref[...], wd_v[...])
        mu = jnp.mean(s, axis=-1, keepdims=True)
        xc = s - mu
        var = jnp.mean(xc * xc, axis=-1, keepdims=True)
        r = lax.rsqrt(var + LN_EPS)
        xh = xc * r
        xh_ref[...] = xh.astype(BF16)
        r_ref[...] = r
        y = xh * g_ref[...] + b_ref[...]
        y_ref[...] = y
        yb_ref[...] = y.astype(BF16)

    row = pl.BlockSpec((tm, D), lambda i: (i, 0))
    wide = pl.BlockSpec((tm, DFF), lambda i: (i, 0))
    vec = pl.BlockSpec((1, D), lambda i: (0, 0))
    wsl = pltpu.VMEM((DFF, D), BF16)
    hid = jax.ShapeDtypeStruct((T, DFF), BF16)
    return pl.pallas_call(
        body, name="ffn_fwd", grid=(T // tm,),
        in_specs=[row, row, vec, vec, ANY_SPEC, ANY_SPEC, ANY_SPEC],
        out_specs=[wide, wide, wide, row, row, pl.BlockSpec((tm, 1), lambda i: (i, 0)), row],
        out_shape=[hid, hid, hid, jax.ShapeDtypeStruct((T, D), F32), jax.ShapeDtypeStruct((T, D), BF16),
                   jax.ShapeDtypeStruct((T, 1), F32), jax.ShapeDtypeStruct((T, D), BF16)],
        scratch_shapes=[wsl, wsl, wsl, pltpu.SemaphoreType.DMA((3,))],
        compiler_params=_cp(("arbitrary",)))(xb, x, gam, bet, wgt, wut, wd)


def _ffn_bwd(dy, xh, r, gam, g, u, wd, wgt, wut, target=None):
    tm, ch = 256, 256

    def body(*refs):
        if target is None:
            (dy_ref, xh_ref, r_ref, gam_ref, g_ref, u_ref, wd_hbm, wg_hbm, wu_hbm,
             dg_ref, du_ref, dsb_ref, dx_ref, dgam_ref, dbet_ref, wd_v, wg_v, wu_v, sem) = refs
        else:
            (dy_ref, t_ref, xh_ref, r_ref, gam_ref, g_ref, u_ref, wd_hbm, wg_hbm, wu_hbm,
             dg_ref, du_ref, dsb_ref, dx_ref, dgam_ref, dbet_ref, sq_ref, wd_v, wg_v, wu_v, sem) = refs
        loads = [pltpu.make_async_copy(s, d, sem.at[k])
                 for k, (s, d) in enumerate(((wd_hbm, wd_v), (wg_hbm, wg_v), (wu_hbm, wu_v)))]

        @pl.when(pl.program_id(0) == 0)
        def _():
            for cp in loads:
                cp.start()
            loads[0].wait()

        if target is None:
            dyv = dy_ref[...]
        else:
            e = dy_ref[...] - t_ref[...]
            dyv = e * (1.0 / D)
            p = jnp.sum(jnp.sum(e * e, axis=1, keepdims=True), axis=0, keepdims=True)

            @pl.when(pl.program_id(0) == 0)
            def _():
                sq_ref[...] = p

            @pl.when(pl.program_id(0) > 0)
            def _():
                sq_ref[...] += p

        xhv = xh_ref[...].astype(F32)
        dxh = dyv * gam_ref[...]
        m1 = jnp.mean(dxh, axis=-1, keepdims=True)
        m2 = jnp.mean(dxh * xhv, axis=-1, keepdims=True)
        ds = r_ref[...] * (dxh - m1 - xhv * m2)
        pg = jnp.sum(dyv * xhv, axis=0, keepdims=True)
        pb = jnp.sum(dyv, axis=0, keepdims=True)

        @pl.when(pl.program_id(0) == 0)
        def _():
            dgam_ref[...] = pg
            dbet_ref[...] = pb

        @pl.when(pl.program_id(0) > 0)
        def _():
            dgam_ref[...] += pg
            dbet_ref[...] += pb

        db = ds.astype(BF16)
        dsb_ref[...] = db
        for c in range(0, DFF, ch):
            da = _dot_nt(db, wd_v[c:c + ch, :])
            gv = g_ref[:, c:c + ch].astype(F32)
            sg = jax.nn.sigmoid(gv)
            dg_ref[:, c:c + ch] = (da * u_ref[:, c:c + ch].astype(F32) * (sg * (1.0 + gv * (1.0 - sg)))).astype(BF16)
            du_ref[:, c:c + ch] = (da * (gv * sg)).astype(BF16)
        @pl.when(pl.program_id(0) == 0)
        def _():
            loads[1].wait()
            loads[2].wait()

        dx_ref[...] = ALPHA * ds + _dot(dg_ref[...], wg_v[...]) + _dot(du_ref[...], wu_v[...])

    row = pl.BlockSpec((tm, D), lambda i: (i, 0))
    wide = pl.BlockSpec((tm, DFF), lambda i: (i, 0))
    vec = pl.BlockSpec((1, D), lambda i: (0, 0))
    wsl = pltpu.VMEM((DFF, D), BF16)
    last = target is not None
    return pl.pallas_call(
        body, name="ffn_bwd_loss" if last else "ffn_bwd", grid=(T // tm,),
        in_specs=[row] + ([row] if last else [])
        + [row, pl.BlockSpec((tm, 1), lambda i: (i, 0)), vec, wide, wide, ANY_SPEC, ANY_SPEC, ANY_SPEC],
        out_specs=[wide, wide, row, row, vec, vec] + ([pl.BlockSpec((1, 1), lambda i: (0, 0))] if last else []),
        out_shape=[jax.ShapeDtypeStruct((T, DFF), BF16), jax.ShapeDtypeStruct((T, DFF), BF16),
                   jax.ShapeDtypeStruct((T, D), BF16), jax.ShapeDtypeStruct((T, D), F32),
                   jax.ShapeDtypeStruct((1, D), F32), jax.ShapeDtypeStruct((1, D), F32)]
        + ([jax.ShapeDtypeStruct((1, 1), F32)] if last else []),
        scratch_shapes=[wsl, wsl, wsl, pltpu.SemaphoreType.DMA((3,))],
        compiler_params=_cp(("arbitrary",)))(dy, *([target] if last else []), xh, r, gam, g, u, wd, wgt, wut)


def _mm_ln(a, w, x, gam, bet, name):
    tm = 256
    K = a.shape[1]

    def body(a_ref, w_ref, x_ref, g_ref, b_ref, y_ref, xh_ref, r_ref, yb_ref):
        s = ALPHA * x_ref[...] + _dot(a_ref[...], w_ref[...])
        mu = jnp.mean(s, axis=-1, keepdims=True)
        xc = s - mu
        var = jnp.mean(xc * xc, axis=-1, keepdims=True)
        r = lax.rsqrt(var + LN_EPS)
        xh = xc * r
        xh_ref[...] = xh.astype(BF16)
        r_ref[...] = r
        y = xh * g_ref[...] + b_ref[...]
        y_ref[...] = y
        yb_ref[...] = y.astype(BF16)

    row = pl.BlockSpec((tm, D), lambda i: (i, 0))
    vec = pl.BlockSpec((1, D), lambda i: (0, 0))
    return pl.pallas_call(
        body, name=name, grid=(T // tm,),
        in_specs=[pl.BlockSpec((tm, K), lambda i: (i, 0)), pl.BlockSpec((K, D), lambda i: (0, 0)), row, vec, vec],
        out_specs=[row, row, pl.BlockSpec((tm, 1), lambda i: (i, 0)), row],
        out_shape=[jax.ShapeDtypeStruct((T, D), F32), jax.ShapeDtypeStruct((T, D), BF16),
                   jax.ShapeDtypeStruct((T, 1), F32), jax.ShapeDtypeStruct((T, D), BF16)],
        compiler_params=_cp(("parallel",)))(a, w, x, gam, bet)


def _ln_bwd(dy, xh, r, gam, w):
    tm = 256

    def body(dy_ref, xh_ref, r_ref, g_ref, w_ref, ds_ref, dg_ref, db_ref, dsb_ref, dm_ref):
        i = pl.program_id(0)
        dyv = dy_ref[...]
        xhv = xh_ref[...].astype(F32)
        dxh = dyv * g_ref[...]
        m1 = jnp.mean(dxh, axis=-1, keepdims=True)
        m2 = jnp.mean(dxh * xhv, axis=-1, keepdims=True)
        ds = r_ref[...] * (dxh - m1 - xhv * m2)
        ds_ref[...] = ds
        dsb = ds.astype(BF16)
        dsb_ref[...] = dsb
        dm_ref[...] = _dot_nt(dsb, w_ref[...]).astype(BF16)
        pg = jnp.sum(dyv * xhv, axis=0, keepdims=True)
        pb = jnp.sum(dyv, axis=0, keepdims=True)

        @pl.when(i == 0)
        def _():
            dg_ref[...] = pg
            db_ref[...] = pb

        @pl.when(i > 0)
        def _():
            dg_ref[...] += pg
            db_ref[...] += pb

    row = pl.BlockSpec((tm, D), lambda i: (i, 0))
    vec = pl.BlockSpec((1, D), lambda i: (0, 0))
    return pl.pallas_call(
        body, name="ln_bwd_proj", grid=(T // tm,),
        in_specs=[row, row, pl.BlockSpec((tm, 1), lambda i: (i, 0)), vec, pl.BlockSpec((D, D), lambda i: (0, 0))],
        out_specs=[row, vec, vec, row, row],
        out_shape=[jax.ShapeDtypeStruct((T, D), F32), jax.ShapeDtypeStruct((1, D), F32),
                   jax.ShapeDtypeStruct((1, D), F32), jax.ShapeDtypeStruct((T, D), BF16),
                   jax.ShapeDtypeStruct((T, D), BF16)],
        compiler_params=_cp(("arbitrary",)))(dy, xh, r, gam, w)


def _adamw(w, g, m, v, tr):
    L, R, C = w.shape

    def body(w_ref, g_ref, m_ref, v_ref, d_ref, m2_ref, v2_ref):
        gv = g_ref[...]
        m2 = ADAM_B1 * m_ref[...] + (1.0 - ADAM_B1) * gv
        v2 = ADAM_B2 * v_ref[...] + (1.0 - ADAM_B2) * (gv * gv)
        m_hat = m2 / (1.0 - ADAM_B1 ** ADAM_STEP)
        v_hat = v2 / (1.0 - ADAM_B2 ** ADAM_STEP)
        d_ref[...] = -ADAM_LR * (m_hat / (jnp.sqrt(v_hat) + ADAM_EPS) + ADAM_WD * w_ref[...])
        m2_ref[...] = m2
        v2_ref[...] = v2

    blk = pl.BlockSpec((None, tr, C), lambda l, i: (l, i, 0))
    sh = jax.ShapeDtypeStruct((L, R, C), F32)
    return pl.pallas_call(
        body, name="adamw", grid=(L, R // tr), in_specs=[blk] * 4, out_specs=[blk] * 3,
        out_shape=[sh, sh, sh], compiler_params=_cp(("parallel", "parallel")))(w, g, m, v)


class _Job:
    def __init__(self, ins, out_shapes, aliases, sems, start, finish):
        self.ins, self.out_shapes, self.aliases, self.sems = list(ins), list(out_shapes), dict(aliases), list(sems)
        self.start, self.finish = start, finish


def _host_call(body, name, ins, in_specs, out_shapes, out_specs, scratch, aliases, job):
    n_in, n_out, n_scr = len(ins), len(out_shapes), len(scratch)
    jins = job.ins if job else []
    jouts = job.out_shapes if job else []
    jsems = job.sems if job else []

    def wrapped(*refs):
        a = n_in
        b = a + len(jins)
        c = b + n_out
        d = c + len(jouts)
        e = d + n_scr
        comm = None
        if job:
            jrefs = (refs[a:b], refs[c:d], refs[e:])
            comm = (lambda: job.start(*jrefs), lambda st: job.finish(st, *jrefs))
        body(refs[:a], refs[b:c], refs[d:e], comm)

    al = dict(aliases)
    if job:
        for ji, jo in job.aliases.items():
            al[n_in + ji] = n_out + jo
    res = pl.pallas_call(
        wrapped, name=name, in_specs=list(in_specs) + [ANY_SPEC] * len(jins),
        out_specs=list(out_specs) + [ANY_SPEC] * len(jouts), out_shape=list(out_shapes) + list(jouts),
        scratch_shapes=list(scratch) + list(jsems), input_output_aliases=al,
        compiler_params=_cp())(*ins, *jins)
    return res[:n_out], res[n_out:]


def _copy_in(src, dst, sem):
    cp = pltpu.make_async_copy(src, dst, sem)
    cp.start()
    cp.wait()


CHAINS = [(p, b) for p in range(2) for b in range(BL)]
NC = len(CHAINS)
ROWS_SHAPE = jax.ShapeDtypeStruct((NSTAT, S), F32)
SLAB_QKV = pltpu.VMEM((T, 2 * PAIRW), BF16)
SLAB_OUT = pltpu.VMEM((T, 2 * BQ), BF16)
SLAB_O32 = pltpu.VMEM((T, 2 * BQ), F32)
SLAB_T = pltpu.VMEM((2, BQ, T), BF16)
SLAB_KEYB = pltpu.VMEM((NSTAT, S, BQ), F32)
ACC_KV = pltpu.VMEM((2, T, BQ), F32)
NTRI = NB * (NB + 1) // 2
A_TILES = jax.ShapeDtypeStruct((NTRI, NC, HB, BQ), BF16)
PAIR_DIAG = pltpu.VMEM((NC, 2, HB, HB), BF16)
FLASH_PER_TRIP = SB_PER_TRIP = 4
A_AHEAD = 4
A_SLOTS_IN = A_AHEAD + SB_PER_TRIP
A_SLOTS_OUT = 2 * SB_PER_TRIP


def _lane_masks():
    lane = _iota((1, BQ), 1)
    m0 = (lane < 64).astype(BF16)
    return m0, 1.0 - m0


def _merge_heads(x, first):
    return jnp.where(first, x[:BQ], x[BQ:])


def _row_masks():
    r = _iota((BQ, 1), 0)
    m0 = (r < 64).astype(BF16)
    return m0, 1.0 - m0


def _stack(x, m0, m1):
    return jnp.concatenate([x * m0, x * m1], axis=0)


def _stack_t(xt, r0, r1):
    return jnp.concatenate([xt * r0, xt * r1], axis=1)


def _tr(x):
    return x.T


def _rows(b, i):
    return pl.ds(pl.multiple_of(b * S + i * BQ, BQ), BQ)


def _transpose_slab(src, dst, col0):
    def blk(n, _):
        r = pl.ds(pl.multiple_of(n * BQ, BQ), BQ)
        for p in range(2):
            dst[p, :, r] = _tr(src[r, col0(p):col0(p) + BQ])
        return 0

    lax.fori_loop(0, T // BQ, blk, 0)


def _heads(x):
    return x[:BQ], x[BQ:]


def _bcast_heads(r0, r1):
    return jnp.concatenate([jnp.broadcast_to(r0, (BQ, BQ)), jnp.broadcast_to(r1, (BQ, BQ))], axis=0)


def _by_channel(r0, r1):
    return jnp.where(_iota((BQ, BQ), 0) < 64, r0, r1)


def _colsum2(x):
    return jnp.sum(x[:BQ], axis=0, keepdims=True), jnp.sum(x[BQ:], axis=0, keepdims=True)


def _stat_row(ref, p, b, h, i):
    c = b * NH + 2 * p + h
    return ref[c:c + 1, pl.ds(pl.multiple_of(i * BQ, BQ), BQ)]


def _put_row(ref, p, b, h, i, v):
    c = b * NH + 2 * p + h
    ref[c:c + 1, pl.ds(pl.multiple_of(i * BQ, BQ), BQ)] = v


def _valid_t(strict):
    r = _iota((HB, BQ), 0) & (BQ - 1)
    c = _iota((HB, BQ), 1)
    return (r < c) if strict else (r <= c)


def _tri_blockdiag(later):
    r = _iota((HB, HB), 0)
    c = _iota((HB, HB), 1)
    same = (r >= BQ) == (c >= BQ)
    return (same & ((c > r) if later else (c < r))).astype(BF16)


def _cum_mm(tri, x):
    y = _dot(tri, _split2(x))
    return y[:, :BQ] + y[:, BQ:]


def _kv_tiles(qkv_v, p, b, j):
    r = _rows(b, j)
    return qkv_v[r, p * PAIRW + BQ:p * PAIRW + 2 * BQ], qkv_v[r, p * PAIRW + 2 * BQ:p * PAIRW + 3 * BQ]


def _blocks_loop(count, per_trip, step, carry, before=None):
    done = 0
    while per_trip >= 1:
        def trip(n, c, per_trip=per_trip, done=done):
            for u in range(per_trip if before else 0):
                before(done + n * per_trip + u)
            for u in range(per_trip):
                c = step(done + n * per_trip + u, c)
            return c

        trips = (count - done) // per_trip
        carry = lax.fori_loop(0, trips, trip, carry)
        done = done + trips * per_trip
        per_trip //= 2
    return carry


def _q_tile(qkv_v, p, b, i):
    return qkv_v[_rows(b, i), p * PAIRW:p * PAIRW + BQ] * SCALE


def _sb_fwd(qkv, job=None):
    def body(ins, outs, scr, comm):
        (qkv_hbm,), (o_hbm, a_hbm), (qkv_v, o_v, sem, vt_v, a_st, a_sems) = ins, outs, scr
        _copy_in(qkv_hbm.at[:, pl.ds(0, 2 * PAIRW)], qkv_v, sem)
        st = comm[0]() if comm else None
        _transpose_slab(qkv_v, vt_v, lambda p: p * PAIRW + 2 * BQ)
        m0, m1 = _lane_masks()
        r0, r1 = _row_masks()
        valid = _valid_t(True)
        later = _tri_blockdiag(True)

        def a_copy(n, t):
            slot = n % A_SLOTS_OUT
            return pltpu.make_async_copy(a_st.at[slot], a_hbm.at[t], a_sems.at[slot])

        def free_slot(i, jj):
            n = (i * (i + 1)) // 2 + jj

            @pl.when(n >= A_SLOTS_OUT)
            def _():
                a_copy(n, 0).wait()

        def steps(qts, i, jj, cs, diag):
            j = i - jj
            ks = [_stack(_kv_tiles(qkv_v, p, b, j)[0], m0, m1) for p, b in CHAINS]
            zs = [_dot(ks[c], qts[c]) for c in range(NC)]
            lbs, lrs = [], []
            for c in range(NC):
                lb = _log_sigmoid_tile(zs[c])
                lr = lb - zs[c]
                if diag:
                    lr = jnp.where(valid, lr, 0.0)
                lbs.append(lb)
                lrs.append(lr)
            tails = [_cum_mm(later, lrs[c]) for c in range(NC)]
            avs = []
            for c in range(NC):
                a = jnp.exp(lbs[c] + tails[c] + _bcast_heads(*cs[c][0]))
                if diag:
                    a = jnp.where(valid, a, 0.0)
                avs.append(a.astype(BF16))
            n = (i * (i + 1)) // 2 + jj
            if diag:
                free_slot(i, jj)
            for c in range(NC):
                a_st[n % A_SLOTS_OUT, c] = avs[c]
            a_copy(n, n - jj + j).start()
            out = []
            for c, (p, b) in enumerate(CHAINS):
                vts = _stack_t(vt_v[p, :, _rows(b, j)], r0, r1)
                s0, s1 = _colsum2(lrs[c])
                out.append(((cs[c][0][0] + s0, cs[c][0][1] + s1), cs[c][1] + _dot(vts, avs[c])))
            return tuple(out)

        def qblock(i, _):
            qts = [_tr(_q_tile(qkv_v, p, b, i)) for p, b in CHAINS]
            zr = jnp.zeros((1, BQ), F32)
            cs = steps(qts, i, 0, (((zr, zr), jnp.zeros((BQ, BQ), F32)),) * NC, True)
            cs = _blocks_loop(i, SB_PER_TRIP, lambda k, cs: steps(qts, i, k + 1, cs, False), cs,
                              before=lambda k: free_slot(i, k + 1))
            for c, (p, b) in enumerate(CHAINS):
                o_v[_rows(b, i), p * BQ:(p + 1) * BQ] = cs[c][1].T.astype(BF16)
            return 0

        lax.fori_loop(0, NB, qblock, 0)
        for n in range(NTRI - A_SLOTS_OUT, NTRI):
            a_copy(n, 0).wait()
        _copy_in(o_v, o_hbm.at[:, pl.ds(0, 2 * BQ)], sem)
        if comm:
            comm[1](st)

    (mixed, amat), extra = _host_call(
        body, "sb_fwd", [qkv], [ANY_SPEC], [jax.ShapeDtypeStruct((T, D), BF16), A_TILES], [ANY_SPEC, ANY_SPEC],
        [SLAB_QKV, SLAB_OUT, pltpu.SemaphoreType.DMA, SLAB_T, pltpu.VMEM((A_SLOTS_OUT, NC, HB, BQ), BF16),
         pltpu.SemaphoreType.DMA((A_SLOTS_OUT,))], {}, job)
    return mixed, amat, extra


def _sb_bwd(qkv, dmixed, amat, job=None):
    def body(ins, outs, scr, comm):
        (qkv_hbm, do_hbm, a_hbm), (dqkv_hbm,), (qkv_v, do_v, dq_v, dk_s, dv_s, sems, kt_v, a_st, a_sems, w_v) = ins, outs, scr
        sem = sems.at[0]
        w_v[...] = jnp.zeros_like(w_v)

        def a_copy(t):
            slot = t % A_SLOTS_IN
            return pltpu.make_async_copy(a_hbm.at[t], a_st.at[slot], a_sems.at[slot])

        later = [pltpu.make_async_copy(do_hbm.at[:, pl.ds(0, 2 * BQ)], do_v, sems.at[1])]
        for cp in later:
            cp.start()
        for t in range(A_AHEAD):
            a_copy(t).start()
        _copy_in(qkv_hbm.at[:, pl.ds(0, 2 * PAIRW)], qkv_v, sem)
        st = comm[0]() if comm else None
        _transpose_slab(qkv_v, kt_v, lambda p: p * PAIRW + BQ)
        for cp in later:
            cp.wait()
        m0, m1 = _lane_masks()
        first = _iota((BQ, BQ), 1) < 64
        r0, r1 = _row_masks()
        valid = _valid_t(True)
        earlier = _tri_blockdiag(False)
        dk_s[...] = jnp.zeros_like(dk_s)
        dv_s[...] = jnp.zeros_like(dv_s)

        def fetch(i, j):
            t = (i * (i + 1)) // 2 + j
            a_copy(t).wait()

            @pl.when(t + A_AHEAD < NTRI)
            def _():
                a_copy(t + A_AHEAD).start()

        def steps(i, j, cs, diag, fetched=False):
            if not fetched:
                fetch(i, j)
            slot = ((i * (i + 1)) // 2 + j) % A_SLOTS_IN
            kv =[_kv_tiles(qkv_v, p, b, j) for p, b in CHAINS]
            zd = [_dot(jnp.concatenate([_stack(kv[c][0], m0, m1), _stack(kv[c][1], m0, m1)], axis=1), w_v[c, 0])
                  for c in range(NC)]
            zs = [x[:, :BQ] for x in zd]
            das = [x[:, BQ:] for x in zd]
            avs = [a_st[slot, c] for c in range(NC)]
            gms = [das[c] * avs[c].astype(F32) for c in range(NC)]
            befores = [_dot(earlier, gms[c].astype(BF16)) for c in range(NC)]
            dzbs = []
            for c in range(NC):
                dz = gms[c] - jax.nn.sigmoid(zs[c]) * (gms[c] + befores[c] + _bcast_heads(*cs[c][0]))
                if diag:
                    dz = jnp.where(valid, dz, 0.0)
                dzbs.append(dz.astype(BF16))
            out = []
            for c, (p, b) in enumerate(CHAINS):
                dq = cs[c][1] + _dot(_stack_t(kt_v[p, :, _rows(b, j)], r0, r1), dzbs[c])
                kd = _dot(jnp.concatenate([dzbs[c], avs[c]], axis=1), w_v[c, 1])
                dk_s[p, _rows(b, j), :] += _merge_heads(kd[:, :BQ], first)
                dv_s[p, _rows(b, j), :] += _merge_heads(kd[:, BQ:], first)
                g0, g1 = _colsum2(gms[c])
                out.append(((cs[c][0][0] + g0, cs[c][0][1] + g1), dq))
            return tuple(out)

        def qblock(i, _):
            for c, (p, b) in enumerate(CHAINS):
                qn = _q_tile(qkv_v, p, b, i)
                dn = do_v[_rows(b, i), p * BQ:(p + 1) * BQ]
                for r, (x, y) in enumerate(((_tr(qn), _tr(dn)), (qn, dn))):
                    w_v[c, r, :BQ, :BQ] = x
                    w_v[c, r, BQ:, BQ:] = y
            zr = jnp.zeros((1, BQ), F32)
            cs = (((zr, zr), jnp.zeros((BQ, BQ), F32)),) * NC
            cs = _blocks_loop(i, SB_PER_TRIP, lambda j, cs: steps(i, j, cs, False, True), cs,
                              before=lambda j: fetch(i, j))
            cs = steps(i, i, cs, True)
            for c, (p, b) in enumerate(CHAINS):
                dq_v[_rows(b, i), p * PAIRW:p * PAIRW + BQ] = (cs[c][1].T * SCALE).astype(BF16)
            return 0

        lax.fori_loop(0, NB, qblock, 0)
        for p in range(2):
            dq_v[:, p * PAIRW + BQ:p * PAIRW + 2 * BQ] = dk_s[p].astype(BF16)
            dq_v[:, p * PAIRW + 2 * BQ:p * PAIRW + 3 * BQ] = dv_s[p].astype(BF16)
        _copy_in(dq_v, dqkv_hbm.at[:, pl.ds(0, 2 * PAIRW)], sem)
        if comm:
            comm[1](st)

    (dqkv,), extra = _host_call(
        body, "sb_bwd", [qkv, dmixed, amat], [ANY_SPEC, ANY_SPEC, ANY_SPEC],
        [jax.ShapeDtypeStruct((T, QKVW), BF16)], [ANY_SPEC],
        [SLAB_QKV, SLAB_OUT, SLAB_QKV, ACC_KV, ACC_KV, pltpu.SemaphoreType.DMA((4,)), SLAB_T,
         pltpu.VMEM((A_SLOTS_IN, NC, HB, BQ), BF16), pltpu.SemaphoreType.DMA((A_SLOTS_IN,)), PAIR_DIAG], {}, job)
    return dqkv, extra


def _flash_fwd(qkv, mixed, g, fox, bias, job=None):
    def body(ins, outs, scr, comm):
        if fox:
            qkv_hbm, cq_ref, ckb_hbm, _ = ins
            (o_hbm, lse_ref, o32_hbm), (qkv_v, o_v, sem, vt_v, o32_v, ckb_v) = outs, scr
        else:
            qkv_hbm, tbl_ref, _ = ins
            (o_hbm, lse_ref), (qkv_v, o_v, sem, vt_v) = outs, scr
        sems = sem
        sem = sems.at[0]
        later = [pltpu.make_async_copy(ckb_hbm, ckb_v, sems.at[1])] if fox else []
        for cp in later:
            cp.start()
        _copy_in(qkv_hbm.at[:, pl.ds(g * 2 * PAIRW, 2 * PAIRW)], qkv_v, sem)
        st = comm[0]() if comm else None
        _transpose_slab(qkv_v, vt_v, lambda p: p * PAIRW + 2 * BQ)
        for cp in later:
            cp.wait()
        m0, m1 = _lane_masks()
        r0, r1 = _row_masks()
        valid = _valid_t(False)

        def steps(qts, cqs, i, j, cs, diag):
            ks = [_stack(_kv_tiles(qkv_v, p, b, j)[0], m0, m1) for p, b in CHAINS]
            zs = [_dot(ks[c], qts[c]) for c in range(NC)]
            prs, alphas, out = [], [], []
            for c, (p, b) in enumerate(CHAINS):
                (ma, mb), (la, lb_), _ = cs[c]
                if fox:
                    kk = pl.ds(pl.multiple_of(j * BQ, BQ), BQ)
                    col = b * NH + 2 * p
                    z = zs[c] + (cqs[c] - jnp.concatenate([ckb_v[col, kk, :], ckb_v[col + 1, kk, :]], axis=0))
                    if diag:
                        z = jnp.where(valid, z, NEG)
                else:
                    z = zs[c] + tbl_ref[p, i - j]
                za, zb = _heads(z)
                na = jnp.maximum(ma, jnp.max(za, axis=0, keepdims=True))
                nb = jnp.maximum(mb, jnp.max(zb, axis=0, keepdims=True))
                aa, ab = jnp.exp(ma - na), jnp.exp(mb - nb)
                pr = jnp.exp(z - _bcast_heads(na, nb))
                sa, sb = _colsum2(pr)
                prs.append(_split2(pr) if fox else pr.astype(BF16))
                alphas.append((aa, ab))
                out.append(((na, nb), (aa * la + sa, ab * lb_ + sb)))
            pvs = []
            for c, (p, b) in enumerate(CHAINS):
                vts = _stack_t(vt_v[p, :, _rows(b, j)], r0, r1)
                if fox:
                    pvs.append(_dot(vts, prs[c][:, :BQ]) + _dot(vts, prs[c][:, BQ:]))
                else:
                    pvs.append(_dot(vts, prs[c]))
            return tuple((out[c][0], out[c][1], _by_channel(*alphas[c]) * cs[c][2] + pvs[c]) for c in range(NC))

        def qblock(i, _):
            qts = [_tr(_q_tile(qkv_v, p, b, i)) for p, b in CHAINS]
            if fox:
                cqs = [_bcast_heads(_stat_row(cq_ref, p, b, 0, i), _stat_row(cq_ref, p, b, 1, i)) for p, b in CHAINS]
            else:
                cqs = [None] * NC
            ng = jnp.full((1, BQ), NEG, F32)
            zr = jnp.zeros((1, BQ), F32)
            cs = steps(qts, cqs, i, i, (((ng, ng), (zr, zr), jnp.zeros((BQ, BQ), F32)),) * NC, True)
            cs = _blocks_loop(i, FLASH_PER_TRIP, lambda k, cs: steps(qts, cqs, i, i - 1 - k, cs, False), cs)
            for c, (p, b) in enumerate(CHAINS):
                (ma, mb), (la, lb_), acc = cs[c]
                o = (acc / _by_channel(la, lb_)).T
                o_v[_rows(b, i), p * BQ:(p + 1) * BQ] = o.astype(BF16)
                if fox:
                    o32_v[_rows(b, i), p * BQ:(p + 1) * BQ] = o
                _put_row(lse_ref, p, b, 0, i, ma + jnp.log(la))
                _put_row(lse_ref, p, b, 1, i, mb + jnp.log(lb_))
            return 0

        lax.fori_loop(0, NB, qblock, 0)
        _copy_in(o_v, o_hbm.at[:, pl.ds(g * 2 * BQ, 2 * BQ)], sem)
        if fox:
            _copy_in(o32_v, o32_hbm, sem)
        if comm:
            comm[1](st)

    bias_specs = [VMEM_SPEC, ANY_SPEC] if fox else [VMEM_SPEC]
    n_in = 2 + len(bias_specs)
    o32 = [jax.ShapeDtypeStruct((T, 2 * BQ), F32)] if fox else []
    res, extra = _host_call(
        body, "fox_fwd" if fox else "dil_fwd", [qkv, *bias, mixed], [ANY_SPEC] + bias_specs + [ANY_SPEC],
        [jax.ShapeDtypeStruct((T, D), BF16), ROWS_SHAPE] + o32, [ANY_SPEC, VMEM_SPEC] + [ANY_SPEC] * len(o32),
        [SLAB_QKV, SLAB_OUT, pltpu.SemaphoreType.DMA((4,)), SLAB_T] + ([SLAB_O32, SLAB_KEYB] if fox else []),
        {n_in - 1: 0}, job)
    return (*res, extra)


def _flash_bwd(qkv, o, dmixed, lse, dqkv, g, fox, bias, job=None):
    def body(ins, outs, scr, comm):
        if fox:
            qkv_hbm, o_hbm, do_hbm, lse_ref, cq_ref, ckb_hbm, _ = ins
            (dqkv_hbm, db_ref), (qkv_v, o_v, do_v, dq_v, dk_s, dv_s, sem, kt_v, w_v, ckb_v, dc_s) = outs, scr
        else:
            qkv_hbm, o_hbm, do_hbm, lse_ref, tbl_ref, _ = ins
            (dqkv_hbm, db_ref), (qkv_v, o_v, do_v, dq_v, dk_s, dv_s, sem, kt_v, w_v) = outs, scr
        w_v[...] = jnp.zeros_like(w_v)
        sems = sem
        sem = sems.at[0]
        later = [pltpu.make_async_copy(do_hbm.at[:, pl.ds(g * 2 * BQ, 2 * BQ)], do_v, sems.at[1])]
        if fox:
            later += [pltpu.make_async_copy(o_hbm, o_v, sems.at[2]), pltpu.make_async_copy(ckb_hbm, ckb_v, sems.at[3])]
        else:
            later += [pltpu.make_async_copy(o_hbm.at[:, pl.ds(g * 2 * BQ, 2 * BQ)], o_v, sems.at[2])]
        for cp in later:
            cp.start()
        _copy_in(qkv_hbm.at[:, pl.ds(g * 2 * PAIRW, 2 * PAIRW)], qkv_v, sem)
        st = comm[0]() if comm else None
        _transpose_slab(qkv_v, kt_v, lambda p: p * PAIRW + BQ)
        for cp in later:
            cp.wait()
        m0, m1 = _lane_masks()
        first = _iota((BQ, BQ), 1) < 64
        r0, r1 = _row_masks()
        valid = _valid_t(False)
        dk_s[...] = jnp.zeros_like(dk_s)
        dv_s[...] = jnp.zeros_like(dv_s)
        if fox:
            dc_s[...] = jnp.zeros_like(dc_s)
        else:
            db_ref[...] = jnp.zeros_like(db_ref)

        def steps(cqs, lses, deltas, i, j, dqs, diag):
            kv = [_kv_tiles(qkv_v, p, b, j) for p, b in CHAINS]
            zd = [_dot(jnp.concatenate([_stack(kv[c][0], m0, m1), _stack(kv[c][1], m0, m1)], axis=1), w_v[c, 0])
                  for c in range(NC)]
            zs = [x[:, :BQ] for x in zd]
            dps = [x[:, BQ:] for x in zd]
            prs, dzl = [], []
            for c, (p, b) in enumerate(CHAINS):
                if fox:
                    kk = pl.ds(pl.multiple_of(j * BQ, BQ), BQ)
                    col = b * NH + 2 * p
                    z = zs[c] + (cqs[c] - jnp.concatenate([ckb_v[col, kk, :], ckb_v[col + 1, kk, :]], axis=0))
                    if diag:
                        z = jnp.where(valid, z, NEG)
                else:
                    z = zs[c] + tbl_ref[p, i - j]
                pr = jnp.exp(z - lses[c])
                prs.append(pr.astype(BF16))
                dzl.append(pr * (dps[c] - deltas[c]))
            dzbs = [dz.astype(BF16) for dz in dzl]
            new = []
            for c, (p, b) in enumerate(CHAINS):
                new.append(dqs[c] + _dot(_stack_t(kt_v[p, :, _rows(b, j)], r0, r1), dzbs[c]))
                kd = _dot(jnp.concatenate([dzbs[c], prs[c]], axis=1), w_v[c, 1])
                dk_s[p, _rows(b, j), :] += _merge_heads(kd[:, :BQ], first)
                dv_s[p, _rows(b, j), :] += _merge_heads(kd[:, BQ:], first)
                if fox:
                    dc_s[c, pl.ds(pl.multiple_of(j * HB, HB), HB), :] += dzl[c]
            if not fox:
                for p in range(2):
                    db_ref[p, i - j] = db_ref[p, i - j] + (dzl[2 * p] + dzl[2 * p + 1])
            return tuple(new)

        def qblock(i, _):
            qns = [_q_tile(qkv_v, p, b, i) for p, b in CHAINS]
            dns = [do_v[_rows(b, i), p * BQ:(p + 1) * BQ] for p, b in CHAINS]
            for c in range(NC):
                for r, (x, y) in enumerate(((_tr(qns[c]), _tr(dns[c])), (qns[c], dns[c]))):
                    w_v[c, r, :BQ, :BQ] = x
                    w_v[c, r, BQ:, BQ:] = y
            lses = [_bcast_heads(_stat_row(lse_ref, p, b, 0, i), _stat_row(lse_ref, p, b, 1, i)) for p, b in CHAINS]
            if fox:
                cqs = [_bcast_heads(_stat_row(cq_ref, p, b, 0, i), _stat_row(cq_ref, p, b, 1, i)) for p, b in CHAINS]
            else:
                cqs = [None] * NC
            deltas = []
            for c, (p, b) in enumerate(CHAINS):
                pt = (dns[c].astype(F32) * o_v[_rows(b, i), p * BQ:(p + 1) * BQ].astype(F32)).T
                deltas.append(_bcast_heads(jnp.sum(pt[:64], axis=0, keepdims=True), jnp.sum(pt[64:], axis=0, keepdims=True)))
            dqs = (jnp.zeros((BQ, BQ), F32),) * NC
            dqs = _blocks_loop(i, FLASH_PER_TRIP, lambda j, d: steps(cqs, lses, deltas, i, j, d, False), dqs)
            dqs = steps(cqs, lses, deltas, i, i, dqs, True)
            for c, (p, b) in enumerate(CHAINS):
                dq_v[_rows(b, i), p * PAIRW:p * PAIRW + BQ] = (dqs[c].T * SCALE).astype(BF16)
            return 0

        lax.fori_loop(0, NB, qblock, 0)
        for p in range(2):
            dq_v[:, p * PAIRW + BQ:p * PAIRW + 2 * BQ] = dk_s[p].astype(BF16)
            dq_v[:, p * PAIRW + 2 * BQ:p * PAIRW + 3 * BQ] = dv_s[p].astype(BF16)
        _copy_in(dq_v, dqkv_hbm.at[:, pl.ds(g * 2 * PAIRW, 2 * PAIRW)], sem)
        if fox:
            lane = _iota((BQ, NSTAT), 1)

            def fold(n, _):
                t = jnp.zeros((BQ, NSTAT), F32)
                for c, (p, b) in enumerate(CHAINS):
                    s = jnp.sum(dc_s[c, pl.ds(pl.multiple_of(n * HB, HB), HB), :], axis=1, keepdims=True)
                    col = b * NH + 2 * p
                    t = t - jnp.where(lane == col, s[:BQ], 0.0) - jnp.where(lane == col + 1, s[BQ:], 0.0)
                db_ref[pl.ds(pl.multiple_of(n * BQ, BQ), BQ), :] = t
                return 0

            lax.fori_loop(0, NB, fold, 0)
        if comm:
            comm[1](st)

    if fox:
        bias_specs = [VMEM_SPEC, ANY_SPEC]
        db_shape = jax.ShapeDtypeStruct((S, NSTAT), F32)
        more = [SLAB_KEYB, pltpu.VMEM((NC, NB * HB, BQ), F32)]
    else:
        bias_specs = [VMEM_SPEC]
        db_shape = jax.ShapeDtypeStruct((2, NB, HB, BQ), F32)
        more = []
    n_in = 5 + len(bias_specs)
    (dqkv, db), extra = _host_call(
        body, "fox_bwd" if fox else "dil_bwd", [qkv, o, dmixed, lse, *bias, dqkv],
        [ANY_SPEC, ANY_SPEC, ANY_SPEC, VMEM_SPEC] + bias_specs + [ANY_SPEC],
        [jax.ShapeDtypeStruct((T, QKVW), BF16), db_shape], [ANY_SPEC, VMEM_SPEC],
        [SLAB_QKV, SLAB_O32 if fox else SLAB_OUT, SLAB_OUT, SLAB_QKV, ACC_KV, ACC_KV, pltpu.SemaphoreType.DMA((4,)), SLAB_T,
         PAIR_DIAG] + more, {n_in - 1: 0}, job)
    return dqkv, db, extra


def _delta_t(d):
    return d * BQ + _iota((HB, BQ), 1) - (_iota((HB, BQ), 0) & (BQ - 1))


def _buckets_in(d):
    lo, hi = max(d * BQ - (BQ - 1), 0), d * BQ + BQ - 1
    return [b for b in range(32) if BUCKET_TH[b] <= hi and (b == 31 or BUCKET_TH[b + 1] > lo)]


def _in_bucket(delta, b):
    m = delta >= BUCKET_TH[b]
    return m if b == 31 else m & (delta < BUCKET_TH[b + 1])


def _dil_table(rel_bias, job=None):
    def body(ins, outs, scr, comm):
        (rb_ref,), (o_ref,) = ins, outs
        st = comm[0]() if comm else None
        for d in range(NB):
            delta = _delta_t(d)
            pos = delta >= 0
            n = ((pos & (delta <= 128)).astype(jnp.int32)
                 + (pos & (delta <= 512) & ((delta & 3) == 0)).astype(jnp.int32)
                 + (pos & ((delta & 15) == 0)).astype(jnp.int32))
            logn = jnp.where(n == 3, math.log(3.0), jnp.where(n == 2, math.log(2.0), jnp.where(n == 1, 0.0, NEG)))
            head1 = _iota((HB, BQ), 0) >= BQ
            for p in range(2):
                val = jnp.zeros((HB, BQ), F32)
                for b in _buckets_in(d):
                    val = jnp.where(_in_bucket(delta, b), jnp.where(head1, rb_ref[b, 2 * p + 1], rb_ref[b, 2 * p]), val)
                o_ref[p, d] = val + logn
        if comm:
            comm[1](st)

    (tbl,), extra = _host_call(
        body, "dil_table", [rel_bias], [pl.BlockSpec(memory_space=pltpu.SMEM)],
        [jax.ShapeDtypeStruct((2, NB, HB, BQ), F32)], [VMEM_SPEC], [], {}, job)
    return (tbl, extra) if job else tbl


def _dil_table_bwd(dtbl):
    def body(dt_ref, o_ref):
        p = pl.program_id(0)
        rowi = _iota((32, BQ), 0)
        lanei = _iota((32, BQ), 1)

        @pl.when(p == 0)
        def _():
            o_ref[...] = jnp.zeros_like(o_ref)

        out = jnp.zeros((32, BQ), F32)
        for b in range(32):
            acc = None
            for d in range(NB):
                if b in _buckets_in(d):
                    t = jnp.where(_in_bucket(_delta_t(d), b), dt_ref[d], 0.0)
                    acc = t if acc is None else acc + t
            rs = jnp.sum(acc, axis=1, keepdims=True)
            s0 = jnp.sum(rs[:BQ], axis=0, keepdims=True)
            s1 = jnp.sum(rs[BQ:], axis=0, keepdims=True)
            out = (out + jnp.where((rowi == b) & (lanei == 2 * p), s0, 0.0)
                   + jnp.where((rowi == b) & (lanei == 2 * p + 1), s1, 0.0))
        o_ref[...] += out

    return pl.pallas_call(
        body, name="dil_table_bwd", grid=(2,),
        in_specs=[pl.BlockSpec((None, NB, HB, BQ), lambda p: (p, 0, 0, 0))],
        out_specs=pl.BlockSpec((32, BQ), lambda p: (0, 0)),
        out_shape=jax.ShapeDtypeStruct((32, BQ), F32),
        compiler_params=_cp(("arbitrary",)))(dtbl)


def _fox_prep(gate, fb):
    def body(g_ref, fb_ref, c_ref):
        tri = (_iota((BQ, BQ), 0) >= _iota((BQ, BQ), 1)).astype(BF16)

        def blk(i, carry):
            r0 = pl.multiple_of(i * BQ, BQ)
            lf = _log_sigmoid(g_ref[pl.ds(r0, BQ), :] + fb_ref[...])
            c = _dot(tri, _split3(lf))
            c_ref[pl.ds(r0, BQ), :] = c[:, 0:BQ] + c[:, BQ:2 * BQ] + c[:, 2 * BQ:3 * BQ] + carry
            return carry + jnp.sum(lf, axis=0, keepdims=True)

        lax.fori_loop(0, NB, blk, jnp.zeros((1, BQ), F32))

    blk = pl.BlockSpec((S, GATEW), lambda b: (b, 0))
    return pl.pallas_call(
        body, name="fox_prep", grid=(BL,), in_specs=[blk, pl.BlockSpec((1, GATEW), lambda b: (0, 0))],
        out_specs=blk, out_shape=jax.ShapeDtypeStruct((T, GATEW), F32),
        compiler_params=_cp(("parallel",)))(gate, fb)


def _fox_post(dcum, gate, fb):
    def body(dc_ref, g_ref, fb_ref, dg_ref, dfb_ref):
        b = pl.program_id(0)
        tri = (_iota((BQ, BQ), 0) <= _iota((BQ, BQ), 1)).astype(BF16)

        def blk(ii, carry):
            csum, dfb = carry
            r0 = pl.multiple_of((NB - 1 - ii) * BQ, BQ)
            dc = dc_ref[pl.ds(r0, BQ), :]
            c = _dot(tri, _split3(dc))
            dlf = c[:, 0:BQ] + c[:, BQ:2 * BQ] + c[:, 2 * BQ:3 * BQ] + csum
            dg = dlf * jnp.exp(_log_sigmoid(-(g_ref[pl.ds(r0, BQ), :] + fb_ref[...])))
            dg_ref[pl.ds(r0, BQ), :] = dg
            return csum + jnp.sum(dc, axis=0, keepdims=True), dfb + jnp.sum(dg, axis=0, keepdims=True)

        z = jnp.zeros((1, BQ), F32)
        _, dfb = lax.fori_loop(0, NB, blk, (z, z))

        @pl.when(b == 0)
        def _():
            dfb_ref[...] = dfb

        @pl.when(b > 0)
        def _():
            dfb_ref[...] += dfb

    blk = pl.BlockSpec((S, GATEW), lambda b: (b, 0))
    vec = pl.BlockSpec((1, GATEW), lambda b: (0, 0))
    return pl.pallas_call(
        body, name="fox_post", grid=(BL,), in_specs=[blk, blk, vec], out_specs=[blk, vec],
        out_shape=[jax.ShapeDtypeStruct((T, GATEW), F32), jax.ShapeDtypeStruct((1, GATEW), F32)],
        compiler_params=_cp(("arbitrary",)))(dcum, gate, fb)


def _shift_down(x, n):
    return jnp.where(_iota(x.shape, 0) >= n, pltpu.roll(x, n, 0), 0.0)


def _shift_up(x, n):
    return jnp.where(_iota(x.shape, 0) < S - n, pltpu.roll(x, S - n, 0), 0.0)


def _conv_fwd(conv, cw, mixed):
    W = 256

    def body(c_ref, w_ref, _, o_ref):
        u = c_ref[:, W:2 * W] * c_ref[:, 2 * W:3 * W]
        y = w_ref[0:1, :] * _shift_down(u, 2) + w_ref[1:2, :] * _shift_down(u, 1) + w_ref[2:3, :] * u
        o_ref[...] = (c_ref[:, 0:W] * y).astype(BF16)

    return pl.pallas_call(
        body, name="conv_fwd", grid=(BL,),
        in_specs=[pl.BlockSpec((S, CONVW), lambda b: (b, 0)), pl.BlockSpec((8, W), lambda b: (0, 0)), ANY_SPEC],
        out_specs=pl.BlockSpec((S, W), lambda b: (b, 3)),
        out_shape=jax.ShapeDtypeStruct((T, D), BF16), input_output_aliases={2: 0},
        compiler_params=_cp(("parallel",)))(conv, cw, mixed)


def _conv_bwd(conv, cw, dmixed):
    W = 256

    def body(c_ref, w_ref, do_ref, dc_ref, dw_ref):
        b = pl.program_id(0)
        bg = c_ref[:, 0:W]
        cg = c_ref[:, W:2 * W]
        hv = c_ref[:, 2 * W:3 * W]
        do = do_ref[...].astype(F32)
        u = cg * hv
        u1 = _shift_down(u, 1)
        u2 = _shift_down(u, 2)
        y = w_ref[0:1, :] * u2 + w_ref[1:2, :] * u1 + w_ref[2:3, :] * u
        dy = do * bg
        du = w_ref[2:3, :] * dy + w_ref[1:2, :] * _shift_up(dy, 1) + w_ref[0:1, :] * _shift_up(dy, 2)
        dc_ref[:, 0:W] = (do * y).astype(BF16)
        dc_ref[:, W:2 * W] = (du * hv).astype(BF16)
        dc_ref[:, 2 * W:3 * W] = (du * cg).astype(BF16)
        rowi = _iota((8, W), 0)
        dw = (jnp.where(rowi == 0, jnp.sum(dy * u2, axis=0, keepdims=True), 0.0)
              + jnp.where(rowi == 1, jnp.sum(dy * u1, axis=0, keepdims=True), 0.0)
              + jnp.where(rowi == 2, jnp.sum(dy * u, axis=0, keepdims=True), 0.0))

        @pl.when(b == 0)
        def _():
            dw_ref[...] = dw

        @pl.when(b > 0)
        def _():
            dw_ref[...] += dw

    return pl.pallas_call(
        body, name="conv_bwd", grid=(BL,),
        in_specs=[pl.BlockSpec((S, CONVW), lambda b: (b, 0)), pl.BlockSpec((8, W), lambda b: (0, 0)),
                  pl.BlockSpec((S, W), lambda b: (b, 3))],
        out_specs=[pl.BlockSpec((S, CONVW), lambda b: (b, 0)), pl.BlockSpec((8, W), lambda b: (0, 0))],
        out_shape=[jax.ShapeDtypeStruct((T, CONVW), BF16), jax.ShapeDtypeStruct((8, W), F32)],
        compiler_params=_cp(("arbitrary",)))(conv, cw, dmixed)


def _place():
    x, y, c = lax.axis_index("x"), lax.axis_index("y"), lax.axis_index("c")
    return x, y, c


def _chips_of(x, y):
    return [(1 - x, y), (x, 1 - y), (1 - x, 1 - y)]


def _dev(p):
    return 4 * p[0] + 2 * p[1] + p[2]


def _gather_job_a(shards):
    n = len(shards)

    def peers(x, y, c):
        return [(x, y, 1 - c)] + [(*chip, c) for chip in _chips_of(x, y)]

    def start(ins, outs, sems):
        send, recv, loc = sems
        x, y, c = _place()
        me = (x, y, c)
        cps = []
        for a in range(n):
            cps.append(pltpu.make_async_copy(ins[a], outs[a].at[_dev(me)], loc.at[a]))
            for k, peer in enumerate(peers(x, y, c)):
                cps.append(pltpu.make_async_remote_copy(
                    src_ref=ins[a], dst_ref=outs[a].at[_dev(me)], send_sem=send.at[a, k], recv_sem=recv.at[a, k],
                    device_id=peer, device_id_type=MESH))
        for cp in cps:
            cp.start()
        return cps

    def finish(cps, ins, outs, sems):
        send, recv, loc = sems
        x, y, c = _place()
        for a in range(n):
            for k, peer in enumerate(peers(x, y, c)):
                pltpu.make_async_remote_copy(
                    src_ref=ins[a], dst_ref=outs[a].at[_dev(peer)], send_sem=send.at[a, k], recv_sem=recv.at[a, k],
                    device_id=(x, y, c), device_id_type=MESH).wait_recv()
        for a in range(n):
            cps[5 * a].wait()
            for k in range(4):
                cps[5 * a + 1 + k].wait_send()

    return _Job(shards, [jax.ShapeDtypeStruct((NDEV,) + s.shape, s.dtype) for s in shards], {},
                [pltpu.SemaphoreType.DMA((n, 4)), pltpu.SemaphoreType.DMA((n, 4)), pltpu.SemaphoreType.DMA((n,))],
                start, finish)


def _gather_job_b(gathered):
    n = len(gathered)

    def start(ins, outs, sems):
        send, recv = sems
        x, y, c = _place()
        cps = []
        for a in range(n):
            for j, chip in enumerate(_chips_of(x, y)):
                blk = outs[a].at[_dev((*chip, c))]
                cps.append(pltpu.make_async_remote_copy(
                    src_ref=blk, dst_ref=blk, send_sem=send.at[a, j], recv_sem=recv.at[a, j],
                    device_id=(x, y, 1 - c), device_id_type=MESH))
        for cp in cps:
            cp.start()
        return cps

    def finish(cps, ins, outs, sems):
        send, recv = sems
        x, y, c = _place()
        for a in range(n):
            for j, chip in enumerate(_chips_of(x, y)):
                blk = outs[a].at[_dev((*chip, 1 - c))]
                pltpu.make_async_remote_copy(
                    src_ref=blk, dst_ref=blk, send_sem=send.at[a, j], recv_sem=recv.at[a, j],
                    device_id=(x, y, c), device_id_type=MESH).wait_recv()
        for cp in cps:
            cp.wait_send()

    return _Job(gathered, [jax.ShapeDtypeStruct(g.shape, g.dtype) for g in gathered], {a: a for a in range(n)},
                [pltpu.SemaphoreType.DMA((n, 3)), pltpu.SemaphoreType.DMA((n, 3))], start, finish)


def _sibling_job(grads):
    n = len(grads)

    def start(ins, outs, sems):
        send, recv = sems
        x, y, c = _place()
        cps = [pltpu.make_async_remote_copy(
            src_ref=ins[a].at[:, 1 - c], dst_ref=outs[a], send_sem=send.at[a], recv_sem=recv.at[a],
            device_id=(x, y, 1 - c), device_id_type=MESH) for a in range(n)]
        for cp in cps:
            cp.start()
        return cps

    def finish(cps, ins, outs, sems):
        for cp in cps:
            cp.wait()

    return _Job(grads, [jax.ShapeDtypeStruct(g.shape[:1] + g.shape[2:], F32) for g in grads], {},
                [pltpu.SemaphoreType.DMA((n,)), pltpu.SemaphoreType.DMA((n,))], start, finish)


def _chip_job(psums):
    n = len(psums)

    def copies(ins, outs, sems):
        send, recv, loc = sems
        x, y, c = _place()
        mychip = 2 * x + y
        cps = []
        for a in range(n):
            cps.append(pltpu.make_async_copy(ins[a].at[mychip], outs[a].at[mychip], loc.at[a]))
            for j, chip in enumerate(_chips_of(x, y)):
                cps.append(pltpu.make_async_remote_copy(
                    src_ref=ins[a].at[2 * chip[0] + chip[1]], dst_ref=outs[a].at[mychip],
                    send_sem=send.at[a, j], recv_sem=recv.at[a, j], device_id=(*chip, c), device_id_type=MESH))
        return cps

    def start(ins, outs, sems):
        for cp in copies(ins, outs, sems):
            cp.start()

    def finish(_, ins, outs, sems):
        cps = copies(ins, outs, sems)
        send, recv, loc = sems
        x, y, c = _place()
        mychip = 2 * x + y
        for a in range(n):
            for j, chip in enumerate(_chips_of(x, y)):
                pltpu.make_async_remote_copy(
                    src_ref=ins[a].at[mychip], dst_ref=outs[a].at[2 * chip[0] + chip[1]],
                    send_sem=send.at[a, j], recv_sem=recv.at[a, j], device_id=(x, y, c), device_id_type=MESH).wait_recv()
        for a in range(n):
            cps[4 * a].wait()
            for j in range(3):
                cps[4 * a + 1 + j].wait_send()

    return _Job(psums, [jax.ShapeDtypeStruct(p.shape, BF16) for p in psums], {},
                [pltpu.SemaphoreType.DMA((n, 3)), pltpu.SemaphoreType.DMA((n, 3)), pltpu.SemaphoreType.DMA((n,))],
                start, finish)


def _join_jobs(*jobs):
    jobs = [j for j in jobs if j is not None]
    if len(jobs) <= 1:
        return jobs[0] if jobs else None
    cut = lambda seq, sizes: [seq[sum(sizes[:k]):sum(sizes[:k + 1])] for k in range(len(sizes))]
    n_in = [len(j.ins) for j in jobs]
    n_out = [len(j.out_shapes) for j in jobs]
    n_sem = [len(j.sems) for j in jobs]
    aliases = {}
    for k, j in enumerate(jobs):
        for a, b in j.aliases.items():
            aliases[sum(n_in[:k]) + a] = sum(n_out[:k]) + b

    def start(ins, outs, sems):
        return [j.start(i, o, s) for j, i, o, s in zip(jobs, cut(ins, n_in), cut(outs, n_out), cut(sems, n_sem))]

    def finish(sts, ins, outs, sems):
        for j, st, i, o, s in zip(jobs, sts, cut(ins, n_in), cut(outs, n_out), cut(sems, n_sem)):
            j.finish(st, i, o, s)

    return _Job([t for j in jobs for t in j.ins], [t for j in jobs for t in j.out_shapes], aliases,
                [t for j in jobs for t in j.sems], start, finish)


def _run_job(job, name):
    def body(ins, outs, scr, comm):
        comm[1](comm[0]())

    return _host_call(body, name, [], [], [], [], [], {}, job)[1]


def _allreduce_small(v, job=None):
    def body(ins, outs, scr, comm):
        (v_ref,), (o_ref,), (slots, send_sems, recv_sems) = ins, outs, scr
        st = comm[0]() if comm else None
        x, y, c = _place()
        me = 4 * x + 2 * y + c
        slots[me] = v_ref[...]

        def copy(k):
            peer = (x ^ ((k >> 2) & 1), y ^ ((k >> 1) & 1), c ^ (k & 1))
            return pltpu.make_async_remote_copy(
                src_ref=v_ref, dst_ref=slots.at[me], send_sem=send_sems.at[k - 1], recv_sem=recv_sems.at[k - 1],
                device_id=peer, device_id_type=MESH)

        def arrival(k):
            return pltpu.make_async_remote_copy(
                src_ref=v_ref, dst_ref=slots.at[me ^ k], send_sem=send_sems.at[k - 1], recv_sem=recv_sems.at[k - 1],
                device_id=(x, y, c), device_id_type=MESH)

        sends = [copy(k) for k in range(1, NDEV)]
        for cp in sends:
            cp.start()
        for k in range(1, NDEV):
            arrival(k).wait_recv()
        for cp in sends:
            cp.wait_send()
        acc = slots[0]
        for d in range(1, NDEV):
            acc = acc + slots[d]
        o_ref[...] = acc
        if comm:
            comm[1](st)

    (out,), extra = _host_call(
        body, "allreduce_small", [v], [VMEM_SPEC], [jax.ShapeDtypeStruct(v.shape, F32)], [VMEM_SPEC],
        [pltpu.VMEM((NDEV,) + v.shape, F32), pltpu.SemaphoreType.DMA((NDEV - 1,)),
         pltpu.SemaphoreType.DMA((NDEV - 1,))], {}, job)
    return (out, extra) if job else out


def _pair_sums(views, gots, core):
    n = len(views)

    def body(c_ref, *refs):
        for a in range(n):
            refs[2 * n + a][...] = (refs[a][...] + refs[n + a][...]).astype(BF16)

    def vspec(v):
        return pl.BlockSpec((None, None, v.shape[2] // 2, v.shape[3]), lambda k, h, c: (k, c[0], h, 0))

    def gspec(g):
        return pl.BlockSpec((None, g.shape[1] // 2, g.shape[2]), lambda k, h, c: (k, h, 0))

    return pl.pallas_call(
        body, name="pair_sums",
        grid_spec=pltpu.PrefetchScalarGridSpec(
            num_scalar_prefetch=1, grid=(4, 2),
            in_specs=[vspec(v) for v in views] + [gspec(g) for g in gots],
            out_specs=[gspec(g) for g in gots]),
        out_shape=[jax.ShapeDtypeStruct(g.shape, BF16) for g in gots],
        compiler_params=_cp(("parallel", "parallel")))(core, *views, *gots)


def _chip_sums(parts):
    n = len(parts)

    def body(*refs):
        for a in range(n):
            acc = refs[a][0].astype(F32)
            for k in range(1, 4):
                acc = acc + refs[a][k].astype(F32)
            refs[n + a][...] = acc

    return pl.pallas_call(
        body, name="chip_sums", in_specs=[VMEM_SPEC] * n, out_specs=[VMEM_SPEC] * n,
        out_shape=[jax.ShapeDtypeStruct(p.shape[1:], F32) for p in parts], compiler_params=_cp())(*parts)


def _permute_in(w):
    lead = w.shape[:-1]
    return w.reshape(lead + (3, 3, 2, BQ)).swapaxes(-2, -3).reshape(lead + (QKVW,))


def _unpermute_in(w):
    lead = w.shape[:-1]
    return w.reshape(lead + (3, 2, 3, BQ)).swapaxes(-2, -3).reshape(lead + (QKVW,))


def _row(v):
    v = v.reshape(-1)
    return jnp.pad(v, (0, D - v.shape[0])).reshape(1, D)


def kernel(x, w_in, f_bias, conv_w, w_out, rel_bias, ln1_g, ln1_b, w_gate, w_up, w_down, ln2_g, ln2_b, loss_target, m_w_in, m_f_bias, m_conv_w, m_w_out, m_rel_bias, m_ln1_g, m_ln1_b, m_w_gate, m_w_up, m_w_down, m_ln2_g, m_ln2_b, v_w_in, v_f_bias, v_conv_w, v_w_out, v_rel_bias, v_ln1_g, v_ln1_b, v_w_gate, v_w_up, v_w_down, v_ln2_g, v_ln2_b):
    xi, yi, ci = _place()
    me = 4 * xi + 2 * yi + ci
    core = jnp.reshape(ci, (1,)).astype(jnp.int32)

    win_s = jnp.concatenate([_permute_in(w_in[..., :QKVW]), w_in[..., QKVW:]], axis=-1)
    win_s = jnp.pad(win_s, ((0, 0), (0, 0), (0, NPAD - NPROJ))).astype(BF16)
    per_layer = [win_s, w_out.astype(BF16), jnp.swapaxes(w_gate, 1, 2).astype(BF16),
                 jnp.swapaxes(w_up, 1, 2).astype(BF16), w_down.astype(BF16)]
    sh = [[s[l] for s in per_layer] for l in range(2)]

    def whole(g):
        return g.reshape(NDEV * g.shape[1], g.shape[2])

    cw_rows = lax.dynamic_update_slice(jnp.zeros((2, 3, 256), F32), conv_w, (0, 0, me * 32))
    small = jnp.concatenate([_row(cw_rows[0]), _row(cw_rows[1]), jnp.zeros((SMALL_ROWS - 2, D), F32)], axis=0)
    small, leg_a = _allreduce_small(small, job=_gather_job_a(sh[0][:1]))
    cw_full = small[0:2, :CONVW].reshape(2, 3, 256)
    cw8 = jnp.pad(cw_full, ((0, 0), (0, 5), (0, 0)))
    fb = jnp.pad(f_bias, ((0, 0), (0, GATEW - NH))).reshape(2, 1, GATEW)
    tbl, leg_b = _dil_table(rel_bias, job=_gather_job_b(list(leg_a)))
    W = [{"win": whole(leg_b[0])}, {}]

    def wrow(tn, K, blk=0):
        return pl.BlockSpec((tn, K), lambda i, j: (j, blk))

    def arow(tm, K, blk=0):
        return pl.BlockSpec((tm, K), lambda i, j: (i, blk))

    h = x.reshape(T, D)
    hb = h.astype(BF16)
    saved = []
    for l in range(2):
        Win = W[l]["win"]
        qkv, conv, gate = _proj(hb, Win)
        cum = _fox_prep(gate, fb[l])
        cq = cum[:, :NH].reshape(BL, S, NH).transpose(0, 2, 1).reshape(NSTAT, S)
        ckb = jnp.broadcast_to(cq[:, :, None], (NSTAT, S, BQ))
        if l == 0:
            mixed, amat, a0 = _sb_fwd(qkv, job=_gather_job_a(sh[0][1:]))
            mixed, lse_d, ex = _flash_fwd(qkv, mixed, 1, False, (tbl,),
                                          job=_join_jobs(_gather_job_b(list(a0)), _gather_job_a(sh[1][:2])))
            W[0].update(zip(("wout", "wgT", "wuT", "wd"), [whole(t) for t in ex[:4]]))
            mixed, lse_f, o_fox, ex = _flash_fwd(qkv, mixed, 2, True, (cq, ckb), job=_gather_job_b(list(ex[4:])))
            W[1].update(zip(("win", "wout"), [whole(t) for t in ex[:2]]))
        else:
            mixed, amat, a2 = _sb_fwd(qkv, job=_gather_job_a(sh[1][2:]))
            mixed, lse_d, ex = _flash_fwd(qkv, mixed, 1, False, (tbl,), job=_gather_job_b(list(a2)))
            W[1].update(zip(("wgT", "wuT", "wd"), [whole(t) for t in ex]))
            mixed, lse_f, o_fox, _ = _flash_fwd(qkv, mixed, 2, True, (cq, ckb))
        Wout, WgT, WuT, Wd = W[l]["wout"], W[l]["wgT"], W[l]["wuT"], W[l]["wd"]
        mixed = _conv_fwd(conv, cw8[l], mixed)
        x1, xh1, r1, x1b = _mm_ln(mixed, Wout, h, ln1_g[l:l + 1], ln1_b[l:l + 1], "out_proj_ln")
        fs, ft, a, x2, xh2, r2, x2b = _ffn_fwd(x1b, x1, WgT, WuT, Wd, ln2_g[l:l + 1], ln2_b[l:l + 1])
        saved.append(dict(h=hb, qkv=qkv, conv=conv, gate=gate, cq=cq, ckb=ckb, mixed=mixed, amat=amat, lse_d=lse_d,
                          lse_f=lse_f, o_fox=o_fox, x1=x1b, xh1=xh1, r1=r1, fs=fs, ft=ft, a=a, xh2=xh2, r2=r2))
        h, hb = x2, x2b

    dy = h

    def view(gr):
        return gr.reshape(4, 2, gr.shape[0] // NDEV, gr.shape[1])

    G = [None, None]
    small_g = {}
    shard_g = {}
    for l in (1, 0):
        sv = saved[l]
        Win, Wout, WgT, WuT, Wd = W[l]["win"], W[l]["wout"], W[l]["wgT"], W[l]["wuT"], W[l]["wd"]
        res = _ffn_bwd(dy, sv["xh2"], sv["r2"], ln2_g[l:l + 1], sv["fs"], sv["ft"], Wd, WgT, WuT,
                       target=loss_target.reshape(T, D) if l == 1 else None)
        dgt, dut, ds2b, dx1, dg2, db2 = res[:6]
        if l == 1:
            sq = res[6]
        G_d = _mm_tn(sv["a"], ds2b, None, C=D, Ka=DFF, N=D, tm=256, tn=1024, tk=T, ooff=0, name="grad_w_down")
        G_g = _mm_tn(dgt, sv["x1"], None, C=D, Ka=DFF, N=D, tm=256, tn=1024, tk=T, ooff=0, name="grad_w_gate")
        G_u = _mm_tn(dut, sv["x1"], None, C=D, Ka=DFF, N=D, tm=256, tn=1024, tk=T, ooff=0, name="grad_w_up")
        ds1, dg1, db1, ds1b, dmixed = _ln_bwd(dx1, sv["xh1"], sv["r1"], ln1_g[l:l + 1], Wout)
        G_out = _mm_tn(sv["mixed"], ds1b, None, C=D, Ka=D, N=D, tm=256, tn=1024, tk=T, ooff=0, name="grad_w_out")
        early = [view(t) for t in (G_g, G_u, G_d, G_out)] + ([view(G[1]["in"])] if l == 0 else [])
        dqkv, gots = _sb_bwd(sv["qkv"], dmixed, sv["amat"], job=_sibling_job(early))
        ps = _pair_sums(early, list(gots), core)
        dqkv, dtbl, pa = _flash_bwd(sv["qkv"], sv["mixed"], dmixed, sv["lse_d"], dqkv, 1, False, (tbl,),
                                    job=_chip_job(ps[:2]))
        dqkv, dck, pb = _flash_bwd(sv["qkv"], sv["o_fox"], dmixed, sv["lse_f"], dqkv, 2, True,
                                   (sv["cq"], sv["ckb"]), job=_chip_job(ps[2:]))
        sums = _chip_sums(list(pa) + list(pb))
        shard_g[l] = dict(zip(("g", "u", "d", "out"), sums[:4]))
        if l == 0:
            shard_g[1]["in"] = sums[4]
        dconv, dcw = _conv_bwd(sv["conv"], cw8[l], dmixed)
        dcum = jnp.pad(dck.reshape(S, BL, NH).transpose(1, 0, 2).reshape(T, NH), ((0, 0), (0, GATEW - NH)))
        dgate, dfb = _fox_post(dcum, sv["gate"], fb[l])
        drb = _dil_table_bwd(dtbl)
        G_in = _mm_tn(sv["h"], dqkv, None, C=NPAD, Ka=D, N=QKVW, tm=512, tn=768, tk=T, ooff=0, name="grad_w_in_qkv")
        G_in = _mm_tn(sv["h"], dconv, G_in, C=NPAD, Ka=D, N=CONVW, tm=256, tn=768, tk=T, ooff=3,
                      name="grad_w_in_conv")
        G_in = _mm_tn(sv["h"], dgate, G_in, C=NPAD, Ka=D, N=GATEW, tm=1024, tn=128, tk=1024, ooff=24,
                      name="grad_w_in_gate")
        G[l] = {"in": G_in, "out": G_out, "g": G_g, "u": G_u, "d": G_d}
        if l == 0:
            late = [view(G_in)]
            tail = _chip_job(_pair_sums(late, list(_run_job(_sibling_job(late), "sibling_exchange")), core))
            dy, parts = _mm([(dqkv, arow(1024, QKVW), Win, wrow(512, QKVW, 0)),
                             (dconv, arow(1024, CONVW), Win, wrow(512, CONVW, 3)),
                             (dgate, arow(1024, GATEW), Win, wrow(512, GATEW, 24))],
                            nt=True, M=T, N=D, tm=1024, tn=512, out_dtype=F32, name="proj_dx", res=ds1,
                            res_scale=ALPHA, job=tail)
            shard_g[0]["in"] = _chip_sums(list(parts))[0]
        else:
            dy = _proj_bwd(dqkv, dconv, dgate, Win, ds1)
        small_g[l] = dict(ln1_g=dg1, ln1_b=db1, ln2_g=dg2, ln2_b=db2, cw=dcw[0:3].reshape(1, CONVW),
                          fb=dfb[:, :NH], rb=drb[:, :NH])
    grad_x = dy.reshape(BL, S, D)

    rows = []
    for name in ("ln1_g", "ln1_b", "ln2_g", "ln2_b"):
        rows += [small_g[0][name], small_g[1][name]]
    rows += [_row(small_g[0]["cw"]), _row(small_g[1]["cw"]),
             _row(jnp.concatenate([small_g[0]["fb"], small_g[1]["fb"]], axis=0)),
             _row(small_g[0]["rb"] + small_g[1]["rb"]), _row(sq)]
    rows.append(jnp.zeros((SMALL_ROWS - len(rows), D), F32))
    sg = _allreduce_small(jnp.concatenate(rows, axis=0))
    loss = sg[12, 0] * (0.5 / D)
    g_ln1_g, g_ln1_b, g_ln2_g, g_ln2_b = sg[0:2], sg[2:4], sg[4:6], sg[6:8]
    g_conv_full = sg[8:10, :CONVW].reshape(2, 3, 256)
    g_conv = lax.dynamic_slice(g_conv_full, (0, 0, me * 32), (2, 3, 32))
    g_fb = sg[10, :2 * NH].reshape(2, NH)
    g_rb = sg[11, :32 * NH].reshape(32, NH)

    def both(name):
        return jnp.stack([shard_g[0][name], shard_g[1][name]])

    g_in = both("in")
    g_w_in = jnp.concatenate([_unpermute_in(g_in[..., :QKVW]), g_in[..., QKVW:NPROJ]], axis=-1)
    g_w_out = both("out")
    g_w_gate = jnp.swapaxes(both("g"), 1, 2)
    g_w_up = jnp.swapaxes(both("u"), 1, 2)
    g_w_down = both("d")

    up_in = _adamw(w_in, g_w_in, m_w_in, v_w_in, 64)
    up_out = _adamw(w_out, g_w_out, m_w_out, v_w_out, 128)
    up_gate = _adamw(w_gate, g_w_gate, m_w_gate, v_w_gate, 256)
    up_up = _adamw(w_up, g_w_up, m_w_up, v_w_up, 256)
    up_down = _adamw(w_down, g_w_down, m_w_down, v_w_down, 352)

    def pack(fbv, cwv, rbv, l1g, l1b, l2g, l2b):
        r = [l1g, l1b, l2g, l2b, _row(cwv), _row(fbv), _row(rbv)]
        r.append(jnp.zeros((SMALL_ROWS - 11, D), F32))
        return jnp.concatenate(r, axis=0)

    pw = pack(f_bias, conv_w, rel_bias, ln1_g, ln1_b, ln2_g, ln2_b)
    pg = pack(g_fb, g_conv, g_rb, g_ln1_g, g_ln1_b, g_ln2_g, g_ln2_b)
    pm = pack(m_f_bias, m_conv_w, m_rel_bias, m_ln1_g, m_ln1_b, m_ln2_g, m_ln2_b)
    pv = pack(v_f_bias, v_conv_w, v_rel_bias, v_ln1_g, v_ln1_b, v_ln2_g, v_ln2_b)
    ups = [u[0] for u in _adamw(pw[None], pg[None], pm[None], pv[None], SMALL_ROWS)]

    def unpack(p):
        return dict(ln1_g=p[0:2], ln1_b=p[2:4], ln2_g=p[4:6], ln2_b=p[6:8],
                    conv_w=p[8, :192].reshape(2, 3, 32), f_bias=p[9, :2 * NH].reshape(2, NH),
                    rel_bias=p[10, :32 * NH].reshape(32, NH))

    sm = [unpack(p) for p in ups]

    def group(k):
        return (up_in[k], sm[k]["f_bias"], sm[k]["conv_w"], up_out[k], sm[k]["rel_bias"], sm[k]["ln1_g"],
                sm[k]["ln1_b"], up_gate[k], up_up[k], up_down[k], sm[k]["ln2_g"], sm[k]["ln2_b"])

    grads = (g_w_in, g_fb, g_conv, g_w_out, g_rb, g_ln1_g, g_ln1_b, g_w_gate, g_w_up, g_w_down, g_ln2_g, g_ln2_b)
    return (loss, grad_x) + grads + group(0) + group(1) + group(2)
```

```python
import math

import numpy as np
import jax
import jax.numpy as jnp
from jax import lax
from jax.experimental import pallas as pl
from jax.experimental.pallas import tpu as pltpu

F32 = jnp.float32
BF16 = jnp.bfloat16
MESH = pl.DeviceIdType.MESH

D = 1024
S = 2048
BL = 2
T = BL * S
NH = 4
DFF = 2816
NPROJ = 3076
NPAD = 3200
QKVW = 2304
CONVW = 768
GATEW = 128
PAIRW = 384
BQ = 128
HB = 2 * BQ
NB = S // BQ
NDEV = 8
NSTAT = BL * NH
ALPHA = 4.0 ** 0.25
SCALE = 0.125
NEG = -1e30
LN_EPS = 1e-5
ADAM_LR, ADAM_B1, ADAM_B2, ADAM_EPS, ADAM_WD, ADAM_STEP = 0.001, 0.9, 0.999, 1e-08, 0.01, 10
VMEM_LIMIT = 56 * 1024 * 1024
SMALL_ROWS = 16


def _bucket_thresholds():
    d = np.arange(0, S)
    nf = np.maximum(d, 1).astype(np.float32)
    large = 16 + (np.log(nf / np.float32(16)) / np.float32(math.log(128)) * np.float32(16)).astype(np.int32)
    b = np.where(d < 16, d, np.minimum(large, 31))
    return [int(np.argmax(b >= k)) for k in range(32)]


BUCKET_TH = _bucket_thresholds()


def _cp(sem=None, vmem=VMEM_LIMIT):
    return pltpu.CompilerParams(dimension_semantics=sem, vmem_limit_bytes=vmem)


def _dot(a, b):
    return lax.dot_general(a, b, (((1,), (0,)), ((), ())), preferred_element_type=F32)


def _dot_nt(a, b):
    return lax.dot_general(a, b, (((1,), (1,)), ((), ())), preferred_element_type=F32)


def _dot_tn(a, b):
    return lax.dot_general(a, b, (((0,), (0,)), ((), ())), preferred_element_type=F32)


def _split2(x):
    hi = x.astype(BF16)
    mid = (x - hi.astype(F32)).astype(BF16)
    return jnp.concatenate([hi, mid], axis=1)


def _split3(x):
    hi = x.astype(BF16)
    r = x - hi.astype(F32)
    mid = r.astype(BF16)
    lo = (r - mid.astype(F32)).astype(BF16)
    return jnp.concatenate([hi, mid, lo], axis=1)


def _log_sigmoid(u):
    return jnp.minimum(u, 0.0) - jnp.log1p(jnp.exp(-jnp.abs(u)))


def _log_sigmoid_tile(u):
    return jnp.minimum(u, 0.0) - jnp.log(1.0 + jnp.exp(jnp.minimum(u, -u)))


def _iota(shape, dim):
    return lax.broadcasted_iota(jnp.int32, shape, dim)


ANY_SPEC = pl.BlockSpec(memory_space=pl.ANY)
VMEM_SPEC = pl.BlockSpec(memory_space=pltpu.VMEM)


def _mm(pairs, *, nt, M, N, tm, tn, out_dtype, name, res=None, res_scale=1.0, job=None):
    n = len(pairs)
    n_in = 2 * n + (res is not None)
    jins = job.ins if job else []
    jouts = job.out_shapes if job else []
    gi, gj = M // tm, N // tn

    def body(*refs):
        o_ref = refs[n_in + len(jins)]
        if job:
            jrefs = (refs[n_in:n_in + len(jins)], refs[n_in + len(jins) + 1:n_in + len(jins) + 1 + len(jouts)],
                     refs[n_in + len(jins) + 1 + len(jouts):])

            @pl.when((pl.program_id(0) == 0) & (pl.program_id(1) == 0))
            def _():
                job.start(*jrefs)

        acc = None
        for p in range(n):
            a = refs[2 * p][...].astype(BF16)
            b = refs[2 * p + 1][...]
            d = _dot_nt(a, b) if nt else _dot(a, b)
            acc = d if acc is None else acc + d
        if res is not None:
            acc = acc + res_scale * refs[2 * n][...]
        o_ref[...] = acc.astype(out_dtype)
        if job:
            @pl.when((pl.program_id(0) == gi - 1) & (pl.program_id(1) == gj - 1))
            def _():
                job.finish(None, *jrefs)

    ops, specs = [], []
    for a, asp, b, bsp in pairs:
        ops += [a, b]
        specs += [asp, bsp]
    if res is not None:
        ops.append(res)
        specs.append(pl.BlockSpec((tm, tn), lambda i, j: (i, j)))
    out = pl.pallas_call(
        body, name=name, grid=(gi, gj), in_specs=specs + [ANY_SPEC] * len(jins),
        out_specs=[pl.BlockSpec((tm, tn), lambda i, j: (i, j))] + [ANY_SPEC] * len(jouts),
        out_shape=[jax.ShapeDtypeStruct((M, N), out_dtype)] + list(jouts),
        scratch_shapes=list(job.sems) if job else [],
        input_output_aliases={n_in + a: 1 + b for a, b in job.aliases.items()} if job else {},
        compiler_params=_cp(("arbitrary", "arbitrary") if job else ("parallel", "parallel")))(*ops, *jins)
    return (out[0], out[1:]) if job else out[0]


def _mm_tn(a, b, gbuf, *, C, Ka, N, tm, tn, tk, ooff, name):
    def body(*refs):
        a_ref, b_ref, o_ref = refs[0], refs[1], refs[-1]
        k = pl.program_id(2)
        d = _dot_tn(a_ref[...].astype(BF16), b_ref[...].astype(BF16))

        @pl.when(k == 0)
        def _():
            o_ref[...] = d

        @pl.when(k > 0)
        def _():
            o_ref[...] += d

    ops = [a, b] + ([] if gbuf is None else [gbuf])
    return pl.pallas_call(
        body, name=name, grid=(Ka // tm, N // tn, T // tk),
        in_specs=[pl.BlockSpec((tk, tm), lambda i, j, k: (k, i)),
                  pl.BlockSpec((tk, tn), lambda i, j, k: (k, j))] + ([] if gbuf is None else [ANY_SPEC]),
        out_specs=pl.BlockSpec((tm, tn), lambda i, j, k: (i, ooff + j)),
        out_shape=jax.ShapeDtypeStruct((Ka, C), F32),
        input_output_aliases={} if gbuf is None else {2: 0},
        compiler_params=_cp(("parallel", "parallel", "arbitrary")))(*ops)


def _proj(xb, w):
    tm = 512

    def body(x_ref, w_ref, qkv_ref, conv_ref, gate_ref):
        xv = x_ref[...]
        qkv_ref[...] = _dot(xv, w_ref[:, 0:QKVW]).astype(BF16)
        conv_ref[...] = _dot(xv, w_ref[:, QKVW:QKVW + CONVW])
        gate_ref[...] = _dot(xv, w_ref[:, QKVW + CONVW:NPAD])

    def rows(n):
        return pl.BlockSpec((tm, n), lambda i: (i, 0))

    return pl.pallas_call(
        body, name="proj", grid=(T // tm,),
        in_specs=[rows(D), pl.BlockSpec((D, NPAD), lambda i: (0, 0))],
        out_specs=[rows(QKVW), rows(CONVW), rows(GATEW)],
        out_shape=[jax.ShapeDtypeStruct((T, QKVW), BF16), jax.ShapeDtypeStruct((T, CONVW), F32),
                   jax.ShapeDtypeStruct((T, GATEW), F32)],
        compiler_params=_cp(("parallel",)))(xb, w)


def _proj_bwd(dqkv, dconv, dgate, w, res):
    tm = 512

    def body(a_ref, b_ref, c_ref, w_ref, r_ref, o_ref):
        acc = ALPHA * r_ref[...] + _dot_nt(a_ref[...], w_ref[:, 0:QKVW])
        acc = acc + _dot_nt(b_ref[...], w_ref[:, QKVW:QKVW + CONVW])
        o_ref[...] = acc + _dot_nt(c_ref[...].astype(BF16), w_ref[:, QKVW + CONVW:NPAD])

    def rows(n):
        return pl.BlockSpec((tm, n), lambda i: (i, 0))

    return pl.pallas_call(
        body, name="proj_bwd", grid=(T // tm,),
        in_specs=[rows(QKVW), rows(CONVW), rows(GATEW), pl.BlockSpec((D, NPAD), lambda i: (0, 0)), rows(D)],
        out_specs=rows(D), out_shape=jax.ShapeDtypeStruct((T, D), F32),
        compiler_params=_cp(("parallel",)))(dqkv, dconv, dgate, w, res)


def _ffn_fwd(xb, x, wgt, wut, wd, gam, bet):
    tm, ch = 512, 256

    def body(xb_ref, x_ref, g_ref, b_ref, wg_hbm, wu_hbm, wd_hbm,
             go_ref, uo_ref, ao_ref, y_ref, xh_ref, r_ref, yb_ref, wg_v, wu_v, wd_v, sem):
        loads = [pltpu.make_async_copy(s, d, sem.at[k])
                 for k, (s, d) in enumerate(((wg_hbm, wg_v), (wu_hbm, wu_v), (wd_hbm, wd_v)))]

        @pl.when(pl.program_id(0) == 0)
        def _():
            for cp in loads:
                cp.start()
            loads[0].wait()
            loads[1].wait()

        xv = xb_ref[...]
        for c in range(0, DFF, ch):
            gv = _dot_nt(xv, wg_v[c:c + ch, :])
            uv = _dot_nt(xv, wu_v[c:c + ch, :])
            go_ref[:, c:c + ch] = gv.astype(BF16)
            uo_ref[:, c:c + ch] = uv.astype(BF16)
            ao_ref[:, c:c + ch] = (gv * jax.nn.sigmoid(gv) * uv).astype(BF16)
        @pl.when(pl.program_id(0) == 0)
        def _():
            loads[2].wait()

        s = ALPHA * x_ref[...] + _dot(ao_ref[...], wd_v[...])
        mu = jnp.mean(s, axis=-1, keepdims=True)
        xc = s - mu
        var = jnp.mean(xc * xc, axis=-1, keepdims=True)
        r = lax.rsqrt(var + LN_EPS)
        xh = xc * r
        xh_ref[...] = xh.astype(BF16)
        r_ref[...] = r
        y = xh * g_ref[...] + b_ref[...]
        y_ref[...] = y
        yb_ref[...] = y.astype(BF16)

    row = pl.BlockSpec((tm, D), lambda i: (i, 0))
    wide = pl.BlockSpec((tm, DFF), lambda i: (i, 0))
    vec = pl.BlockSpec((1, D), lambda i: (0, 0))
    wsl = pltpu.VMEM((DFF, D), BF16)
    hid = jax.ShapeDtypeStruct((T, DFF), BF16)
    return pl.pallas_call(
        body, name="ffn_fwd", grid=(T // tm,),
        in_specs=[row, row, vec, vec, ANY_SPEC, ANY_SPEC, ANY_SPEC],
        out_specs=[wide, wide, wide, row, row, pl.BlockSpec((tm, 1), lambda i: (i, 0)), row],
        out_shape=[hid, hid, hid, jax.ShapeDtypeStruct((T, D), F32), jax.ShapeDtypeStruct((T, D), BF16),
                   jax.ShapeDtypeStruct((T, 1), F32), jax.ShapeDtypeStruct((T, D), BF16)],
        scratch_shapes=[wsl, wsl, wsl, pltpu.SemaphoreType.DMA((3,))],
        compiler_params=_cp(("arbitrary",)))(xb, x, gam, bet, wgt, wut, wd)


def _ffn_bwd(dy, xh, r, gam, g, u, wd, wgt, wut, target=None):
    tm, ch = 256, 256

    def body(*refs):
        if target is None:
            (dy_ref, xh_ref, r_ref, gam_ref, g_ref, u_ref, wd_hbm, wg_hbm, wu_hbm,
             dg_ref, du_ref, dsb_ref, dx_ref, dgam_ref, dbet_ref, wd_v, wg_v, wu_v, sem) = refs
        else:
            (dy_ref, t_ref, xh_ref, r_ref, gam_ref, g_ref, u_ref, wd_hbm, wg_hbm, wu_hbm,
             dg_ref, du_ref, dsb_ref, dx_ref, dgam_ref, dbet_ref, sq_ref, wd_v, wg_v, wu_v, sem) = refs
        loads = [pltpu.make_async_copy(s, d, sem.at[k])
                 for k, (s, d) in enumerate(((wd_hbm, wd_v), (wg_hbm, wg_v), (wu_hbm, wu_v)))]

        @pl.when(pl.program_id(0) == 0)
        def _():
            for cp in loads:
                cp.start()
            loads[0].wait()

        if target is None:
            dyv = dy_ref[...]
        else:
            e = dy_ref[...] - t_ref[...]
            dyv = e * (1.0 / D)
            p = jnp.sum(jnp.sum(e * e, axis=1, keepdims=True), axis=0, keepdims=True)

            @pl.when(pl.program_id(0) == 0)
            def _():
                sq_ref[...] = p

            @pl.when(pl.program_id(0) > 0)
            def _():
                sq_ref[...] += p

        xhv = xh_ref[...].astype(F32)
        dxh = dyv * gam_ref[...]
        m1 = jnp.mean(dxh, axis=-1, keepdims=True)
        m2 = jnp.mean(dxh * xhv, axis=-1, keepdims=True)
        ds = r_ref[...] * (dxh - m1 - xhv * m2)
        pg = jnp.sum(dyv * xhv, axis=0, keepdims=True)
        pb = jnp.sum(dyv, axis=0, keepdims=True)

        @pl.when(pl.program_id(0) == 0)
        def _():
            dgam_ref[...] = pg
            dbet_ref[...] = pb

        @pl.when(pl.program_id(0) > 0)
        def _():
            dgam_ref[...] += pg
            dbet_ref[...] += pb

        db = ds.astype(BF16)
        dsb_ref[...] = db
        for c in range(0, DFF, ch):
            da = _dot_nt(db, wd_v[c:c + ch, :])
            gv = g_ref[:, c:c + ch].astype(F32)
            sg = jax.nn.sigmoid(gv)
            dg_ref[:, c:c + ch] = (da * u_ref[:, c:c + ch].astype(F32) * (sg * (1.0 + gv * (1.0 - sg)))).astype(BF16)
            du_ref[:, c:c + ch] = (da * (gv * sg)).astype(BF16)
        @pl.when(pl.program_id(0) == 0)
        def _():
            loads[1].wait()
            loads[2].wait()

        dx_ref[...] = ALPHA * ds + _dot(dg_ref[...], wg_v[...]) + _dot(du_ref[...], wu_v[...])

    row = pl.BlockSpec((tm, D), lambda i: (i, 0))
    wide = pl.BlockSpec((tm, DFF), lambda i: (i, 0))
    vec = pl.BlockSpec((1, D), lambda i: (0, 0))
    wsl = pltpu.VMEM((DFF, D), BF16)
    last = target is not None
    return pl.pallas_call(
        body, name="ffn_bwd_loss" if last else "ffn_bwd", grid=(T // tm,),
        in_specs=[row] + ([row] if last else [])
        + [row, pl.BlockSpec((tm, 1), lambda i: (i, 0)), vec, wide, wide, ANY_SPEC, ANY_SPEC, ANY_SPEC],
        out_specs=[wide, wide, row, row, vec, vec] + ([pl.BlockSpec((1, 1), lambda i: (0, 0))] if last else []),
        out_shape=[jax.ShapeDtypeStruct((T, DFF), BF16), jax.ShapeDtypeStruct((T, DFF), BF16),
                   jax.ShapeDtypeStruct((T, D), BF16), jax.ShapeDtypeStruct((T, D), F32),
                   jax.ShapeDtypeStruct((1, D), F32), jax.ShapeDtypeStruct((1, D), F32)]
        + ([jax.ShapeDtypeStruct((1, 1), F32)] if last else []),
        scratch_shapes=[wsl, wsl, wsl, pltpu.SemaphoreType.DMA((3,))],
        compiler_params=_cp(("arbitrary",)))(dy, *([target] if last else []), xh, r, gam, g, u, wd, wgt, wut)


def _mm_ln(a, w, x, gam, bet, name):
    tm = 256
    K = a.shape[1]

    def body(a_ref, w_ref, x_ref, g_ref, b_ref, y_ref, xh_ref, r_ref, yb_ref):
        s = ALPHA * x_ref[...] + _dot(a_ref[...], w_ref[...])
        mu = jnp.mean(s, axis=-1, keepdims=True)
        xc = s - mu
        var = jnp.mean(xc * xc, axis=-1, keepdims=True)
        r = lax.rsqrt(var + LN_EPS)
        xh = xc * r
        xh_ref[...] = xh.astype(BF16)
        r_ref[...] = r
        y = xh * g_ref[...] + b_ref[...]
        y_ref[...] = y
        yb_ref[...] = y.astype(BF16)

    row = pl.BlockSpec((tm, D), lambda i: (i, 0))
    vec = pl.BlockSpec((1, D), lambda i: (0, 0))
    return pl.pallas_call(
        body, name=name, grid=(T // tm,),
        in_specs=[pl.BlockSpec((tm, K), lambda i: (i, 0)), pl.BlockSpec((K, D), lambda i: (0, 0)), row, vec, vec],
        out_specs=[row, row, pl.BlockSpec((tm, 1), lambda i: (i, 0)), row],
        out_shape=[jax.ShapeDtypeStruct((T, D), F32), jax.ShapeDtypeStruct((T, D), BF16),
                   jax.ShapeDtypeStruct((T, 1), F32), jax.ShapeDtypeStruct((T, D), BF16)],
        compiler_params=_cp(("parallel",)))(a, w, x, gam, bet)


def _ln_bwd(dy, xh, r, gam, w):
    tm = 256

    def body(dy_ref, xh_ref, r_ref, g_ref, w_ref, ds_ref, dg_ref, db_ref, dsb_ref, dm_ref):
        i = pl.program_id(0)
        dyv = dy_ref[...]
        xhv = xh_ref[...].astype(F32)
        dxh = dyv * g_ref[...]
        m1 = jnp.mean(dxh, axis=-1, keepdims=True)
        m2 = jnp.mean(dxh * xhv, axis=-1, keepdims=True)
        ds = r_ref[...] * (dxh - m1 - xhv * m2)
        ds_ref[...] = ds
        dsb = ds.astype(BF16)
        dsb_ref[...] = dsb
        dm_ref[...] = _dot_nt(dsb, w_ref[...]).astype(BF16)
        pg = jnp.sum(dyv * xhv, axis=0, keepdims=True)
        pb = jnp.sum(dyv, axis=0, keepdims=True)

        @pl.when(i == 0)
        def _():
            dg_ref[...] = pg
            db_ref[...] = pb

        @pl.when(i > 0)
        def _():
            dg_ref[...] += pg
            db_ref[...] += pb

    row = pl.BlockSpec((tm, D), lambda i: (i, 0))
    vec = pl.BlockSpec((1, D), lambda i: (0, 0))
    return pl.pallas_call(
        body, name="ln_bwd_proj", grid=(T // tm,),
        in_specs=[row, row, pl.BlockSpec((tm, 1), lambda i: (i, 0)), vec, pl.BlockSpec((D, D), lambda i: (0, 0))],
        out_specs=[row, vec, vec, row, row],
        out_shape=[jax.ShapeDtypeStruct((T, D), F32), jax.ShapeDtypeStruct((1, D), F32),
                   jax.ShapeDtypeStruct((1, D), F32), jax.ShapeDtypeStruct((T, D), BF16),
                   jax.ShapeDtypeStruct((T, D), BF16)],
        compiler_params=_cp(("arbitrary",)))(dy, xh, r, gam, w)


def _adamw(w, g, m, v, tr):
    L, R, C = w.shape

    def body(w_ref, g_ref, m_ref, v_ref, d_ref, m2_ref, v2_ref):
        gv = g_ref[...]
        m2 = ADAM_B1 * m_ref[...] + (1.0 - ADAM_B1) * gv
        v2 = ADAM_B2 * v_ref[...] + (1.0 - ADAM_B2) * (gv * gv)
        m_hat = m2 / (1.0 - ADAM_B1 ** ADAM_STEP)
        v_hat = v2 / (1.0 - ADAM_B2 ** ADAM_STEP)
        d_ref[...] = -ADAM_LR * (m_hat / (jnp.sqrt(v_hat) + ADAM_EPS) + ADAM_WD * w_ref[...])
        m2_ref[...] = m2
        v2_ref[...] = v2

    blk = pl.BlockSpec((None, tr, C), lambda l, i: (l, i, 0))
    sh = jax.ShapeDtypeStruct((L, R, C), F32)
    return pl.pallas_call(
        body, name="adamw", grid=(L, R // tr), in_specs=[blk] * 4, out_specs=[blk] * 3,
        out_shape=[sh, sh, sh], compiler_params=_cp(("parallel", "parallel")))(w, g, m, v)


class _Job:
    def __init__(self, ins, out_shapes, aliases, sems, start, finish):
        self.ins, self.out_shapes, self.aliases, self.sems = list(ins), list(out_shapes), dict(aliases), list(sems)
        self.start, self.finish = start, finish


def _host_call(body, name, ins, in_specs, out_shapes, out_specs, scratch, aliases, job):
    n_in, n_out, n_scr = len(ins), len(out_shapes), len(scratch)
    jins = job.ins if job else []
    jouts = job.out_shapes if job else []
    jsems = job.sems if job else []

    def wrapped(*refs):
        a = n_in
        b = a + len(jins)
        c = b + n_out
        d = c + len(jouts)
        e = d + n_scr
        comm = None
        if job:
            jrefs = (refs[a:b], refs[c:d], refs[e:])
            comm = (lambda: job.start(*jrefs), lambda st: job.finish(st, *jrefs))
        body(refs[:a], refs[b:c], refs[d:e], comm)

    al = dict(aliases)
    if job:
        for ji, jo in job.aliases.items():
            al[n_in + ji] = n_out + jo
    res = pl.pallas_call(
        wrapped, name=name, in_specs=list(in_specs) + [ANY_SPEC] * len(jins),
        out_specs=list(out_specs) + [ANY_SPEC] * len(jouts), out_shape=list(out_shapes) + list(jouts),
        scratch_shapes=list(scratch) + list(jsems), input_output_aliases=al,
        compiler_params=_cp())(*ins, *jins)
    return res[:n_out], res[n_out:]


def _copy_in(src, dst, sem):
    cp = pltpu.make_async_copy(src, dst, sem)
    cp.start()
    cp.wait()


CHAINS = [(p, b) for p in range(2) for b in range(BL)]
NC = len(CHAINS)
ROWS_SHAPE = jax.ShapeDtypeStruct((NSTAT, S), F32)
SLAB_QKV = pltpu.VMEM((T, 2 * PAIRW), BF16)
SLAB_OUT = pltpu.VMEM((T, 2 * BQ), BF16)
SLAB_O32 = pltpu.VMEM((T, 2 * BQ), F32)
SLAB_T = pltpu.VMEM((2, BQ, T), BF16)
SLAB_KEYB = pltpu.VMEM((NSTAT, S, BQ), F32)
ACC_KV = pltpu.VMEM((2, T, BQ), F32)
NTRI = NB * (NB + 1) // 2
A_TILES = jax.ShapeDtypeStruct((NTRI, NC, HB, BQ), BF16)
PAIR_DIAG = pltpu.VMEM((NC, 2, HB, HB), BF16)
FLASH_PER_TRIP = SB_PER_TRIP = 8
A_AHEAD = 4
A_SLOTS_IN = A_AHEAD + SB_PER_TRIP
A_SLOTS_OUT = 2 * SB_PER_TRIP


def _lane_masks():
    lane = _iota((1, BQ), 1)
    m0 = (lane < 64).astype(BF16)
    return m0, 1.0 - m0


def _merge_heads(x, first):
    return jnp.where(first, x[:BQ], x[BQ:])


def _row_masks():
    r = _iota((BQ, 1), 0)
    m0 = (r < 64).astype(BF16)
    return m0, 1.0 - m0


def _stack(x, m0, m1):
    return jnp.concatenate([x * m0, x * m1], axis=0)


def _stack_t(xt, r0, r1):
    return jnp.concatenate([xt * r0, xt * r1], axis=1)


def _tr(x):
    return x.T


def _rows(b, i):
    return pl.ds(pl.multiple_of(b * S + i * BQ, BQ), BQ)


def _transpose_slab(src, dst, col0):
    def blk(n, _):
        r = pl.ds(pl.multiple_of(n * BQ, BQ), BQ)
        for p in range(2):
            dst[p, :, r] = _tr(src[r, col0(p):col0(p) + BQ])
        return 0

    lax.fori_loop(0, T // BQ, blk, 0)


def _heads(x):
    return x[:BQ], x[BQ:]


def _bcast_heads(r0, r1):
    return jnp.concatenate([jnp.broadcast_to(r0, (BQ, BQ)), jnp.broadcast_to(r1, (BQ, BQ))], axis=0)


def _by_channel(r0, r1):
    return jnp.where(_iota((BQ, BQ), 0) < 64, r0, r1)


def _colsum2(x):
    return jnp.sum(x[:BQ], axis=0, keepdims=True), jnp.sum(x[BQ:], axis=0, keepdims=True)


def _stat_row(ref, p, b, h, i):
    c = b * NH + 2 * p + h
    return ref[c:c + 1, pl.ds(pl.multiple_of(i * BQ, BQ), BQ)]


def _put_row(ref, p, b, h, i, v):
    c = b * NH + 2 * p + h
    ref[c:c + 1, pl.ds(pl.multiple_of(i * BQ, BQ), BQ)] = v


def _valid_t(strict):
    r = _iota((HB, BQ), 0) & (BQ - 1)
    c = _iota((HB, BQ), 1)
    return (r < c) if strict else (r <= c)


def _tri_blockdiag(later):
    r = _iota((HB, HB), 0)
    c = _iota((HB, HB), 1)
    same = (r >= BQ) == (c >= BQ)
    return (same & ((c > r) if later else (c < r))).astype(BF16)


def _cum_mm(tri, x):
    y = _dot(tri, _split2(x))
    return y[:, :BQ] + y[:, BQ:]


def _kv_tiles(qkv_v, p, b, j):
    r = _rows(b, j)
    return qkv_v[r, p * PAIRW + BQ:p * PAIRW + 2 * BQ], qkv_v[r, p * PAIRW + 2 * BQ:p * PAIRW + 3 * BQ]


def _blocks_loop(count, per_trip, step, carry, before=None):
    done = 0
    while per_trip >= 1:
        def trip(n, c, per_trip=per_trip, done=done):
            for u in range(per_trip if before else 0):
                before(done + n * per_trip + u)
            for u in range(per_trip):
                c = step(done + n * per_trip + u, c)
            return c

        trips = (count - done) // per_trip
        carry = lax.fori_loop(0, trips, trip, carry)
        done = done + trips * per_trip
        per_trip //= 2
    return carry


def _q_tile(qkv_v, p, b, i):
    return qkv_v[_rows(b, i), p * PAIRW:p * PAIRW + BQ] * SCALE


def _sb_fwd(qkv, job=None):
    def body(ins, outs, scr, comm):
        (qkv_hbm,), (o_hbm, a_hbm), (qkv_v, o_v, sem, vt_v, a_st, a_sems) = ins, outs, scr
        _copy_in(qkv_hbm.at[:, pl.ds(0, 2 * PAIRW)], qkv_v, sem)
        st = comm[0]() if comm else None
        _transpose_slab(qkv_v, vt_v, lambda p: p * PAIRW + 2 * BQ)
        m0, m1 = _lane_masks()
        r0, r1 = _row_masks()
        valid = _valid_t(True)
        later = _tri_blockdiag(True)

        def a_copy(n, t):
            slot = n % A_SLOTS_OUT
            return pltpu.make_async_copy(a_st.at[slot], a_hbm.at[t], a_sems.at[slot])

        def free_slot(i, jj):
            n = (i * (i + 1)) // 2 + jj

            @pl.when(n >= A_SLOTS_OUT)
            def _():
                a_copy(n, 0).wait()

        def steps(qts, i, jj, cs, diag):
            j = i - jj
            ks = [_stack(_kv_tiles(qkv_v, p, b, j)[0], m0, m1) for p, b in CHAINS]
            zs = [_dot(ks[c], qts[c]) for c in range(NC)]
            lbs, lrs = [], []
            for c in range(NC):
                lb = _log_sigmoid_tile(zs[c])
                lr = lb - zs[c]
                if diag:
                    lr = jnp.where(valid, lr, 0.0)
                lbs.append(lb)
                lrs.append(lr)
            tails = [_cum_mm(later, lrs[c]) for c in range(NC)]
            avs = []
            for c in range(NC):
                a = jnp.exp(lbs[c] + tails[c] + _bcast_heads(*cs[c][0]))
                if diag:
                    a = jnp.where(valid, a, 0.0)
                avs.append(a.astype(BF16))
            n = (i * (i + 1)) // 2 + jj
            if diag:
                free_slot(i, jj)
            for c in range(NC):
                a_st[n % A_SLOTS_OUT, c] = avs[c]
            a_copy(n, n - jj + j).start()
            out = []
            for c, (p, b) in enumerate(CHAINS):
                vts = _stack_t(vt_v[p, :, _rows(b, j)], r0, r1)
                s0, s1 = _colsum2(lrs[c])
                out.append(((cs[c][0][0] + s0, cs[c][0][1] + s1), cs[c][1] + _dot(vts, avs[c])))
            return tuple(out)

        def qblock(i, _):
            qts = [_tr(_q_tile(qkv_v, p, b, i)) for p, b in CHAINS]
            zr = jnp.zeros((1, BQ), F32)
            cs = steps(qts, i, 0, (((zr, zr), jnp.zeros((BQ, BQ), F32)),) * NC, True)
            cs = _blocks_loop(i, SB_PER_TRIP, lambda k, cs: steps(qts, i, k + 1, cs, False), cs,
                              before=lambda k: free_slot(i, k + 1))
            for c, (p, b) in enumerate(CHAINS):
                o_v[_rows(b, i), p * BQ:(p + 1) * BQ] = cs[c][1].T.astype(BF16)
            return 0

        lax.fori_loop(0, NB, qblock, 0)
        for n in range(NTRI - A_SLOTS_OUT, NTRI):
            a_copy(n, 0).wait()
        _copy_in(o_v, o_hbm.at[:, pl.ds(0, 2 * BQ)], sem)
        if comm:
            comm[1](st)

    (mixed, amat), extra = _host_call(
        body, "sb_fwd", [qkv], [ANY_SPEC], [jax.ShapeDtypeStruct((T, D), BF16), A_TILES], [ANY_SPEC, ANY_SPEC],
        [SLAB_QKV, SLAB_OUT, pltpu.SemaphoreType.DMA, SLAB_T, pltpu.VMEM((A_SLOTS_OUT, NC, HB, BQ), BF16),
         pltpu.SemaphoreType.DMA((A_SLOTS_OUT,))], {}, job)
    return mixed, amat, extra


def _sb_bwd(qkv, dmixed, amat, job=None):
    def body(ins, outs, scr, comm):
        (qkv_hbm, do_hbm, a_hbm), (dqkv_hbm,), (qkv_v, do_v, dq_v, dk_s, dv_s, sems, kt_v, a_st, a_sems, w_v) = ins, outs, scr
        sem = sems.at[0]
        w_v[...] = jnp.zeros_like(w_v)

        def a_copy(t):
            slot = t % A_SLOTS_IN
            return pltpu.make_async_copy(a_hbm.at[t], a_st.at[slot], a_sems.at[slot])

        later = [pltpu.make_async_copy(do_hbm.at[:, pl.ds(0, 2 * BQ)], do_v, sems.at[1])]
        for cp in later:
            cp.start()
        for t in range(A_AHEAD):
            a_copy(t).start()
        _copy_in(qkv_hbm.at[:, pl.ds(0, 2 * PAIRW)], qkv_v, sem)
        st = comm[0]() if comm else None
        _transpose_slab(qkv_v, kt_v, lambda p: p * PAIRW + BQ)
        for cp in later:
            cp.wait()
        m0, m1 = _lane_masks()
        first = _iota((BQ, BQ), 1) < 64
        r0, r1 = _row_masks()
        valid = _valid_t(True)
        earlier = _tri_blockdiag(False)
        dk_s[...] = jnp.zeros_like(dk_s)
        dv_s[...] = jnp.zeros_like(dv_s)

        def fetch(i, j):
            t = (i * (i + 1)) // 2 + j
            a_copy(t).wait()

            @pl.when(t + A_AHEAD < NTRI)
            def _():
                a_copy(t + A_AHEAD).start()

        def steps(i, j, cs, diag, fetched=False):
            if not fetched:
                fetch(i, j)
            slot = ((i * (i + 1)) // 2 + j) % A_SLOTS_IN
            kv =[_kv_tiles(qkv_v, p, b, j) for p, b in CHAINS]
            zd = [_dot(jnp.concatenate([_stack(kv[c][0], m0, m1), _stack(kv[c][1], m0, m1)], axis=1), w_v[c, 0])
                  for c in range(NC)]
            zs = [x[:, :BQ] for x in zd]
            das = [x[:, BQ:] for x in zd]
            avs = [a_st[slot, c] for c in range(NC)]
            gms = [das[c] * avs[c].astype(F32) for c in range(NC)]
            befores = [_dot(earlier, gms[c].astype(BF16)) for c in range(NC)]
            dzbs = []
            for c in range(NC):
                dz = gms[c] - jax.nn.sigmoid(zs[c]) * (gms[c] + befores[c] + _bcast_heads(*cs[c][0]))
                if diag:
                    dz = jnp.where(valid, dz, 0.0)
                dzbs.append(dz.astype(BF16))
            out = []
            for c, (p, b) in enumerate(CHAINS):
                dq = cs[c][1] + _dot(_stack_t(kt_v[p, :, _rows(b, j)], r0, r1), dzbs[c])
                kd = _dot(jnp.concatenate([dzbs[c], avs[c]], axis=1), w_v[c, 1])
                dk_s[p, _rows(b, j), :] += _merge_heads(kd[:, :BQ], first)
                dv_s[p, _rows(b, j), :] += _merge_heads(kd[:, BQ:], first)
                g0, g1 = _colsum2(gms[c])
                out.append(((cs[c][0][0] + g0, cs[c][0][1] + g1), dq))
            return tuple(out)

        def qblock(i, _):
            for c, (p, b) in enumerate(CHAINS):
                qn = _q_tile(qkv_v, p, b, i)
                dn = do_v[_rows(b, i), p * BQ:(p + 1) * BQ]
                for r, (x, y) in enumerate(((_tr(qn), _tr(dn)), (qn, dn))):
                    w_v[c, r, :BQ, :BQ] = x
                    w_v[c, r, BQ:, BQ:] = y
            zr = jnp.zeros((1, BQ), F32)
            cs = (((zr, zr), jnp.zeros((BQ, BQ), F32)),) * NC
            cs = _blocks_loop(i, SB_PER_TRIP, lambda j, cs: steps(i, j, cs, False, True), cs,
                              before=lambda j: fetch(i, j))
            cs = steps(i, i, cs, True)
            for c, (p, b) in enumerate(CHAINS):
                dq_v[_rows(b, i), p * PAIRW:p * PAIRW + BQ] = (cs[c][1].T * SCALE).astype(BF16)
            return 0

        lax.fori_loop(0, NB, qblock, 0)
        for p in range(2):
            dq_v[:, p * PAIRW + BQ:p * PAIRW + 2 * BQ] = dk_s[p].astype(BF16)
            dq_v[:, p * PAIRW + 2 * BQ:p * PAIRW + 3 * BQ] = dv_s[p].astype(BF16)
        _copy_in(dq_v, dqkv_hbm.at[:, pl.ds(0, 2 * PAIRW)], sem)
        if comm:
            comm[1](st)

    (dqkv,), extra = _host_call(
        body, "sb_bwd", [qkv, dmixed, amat], [ANY_SPEC, ANY_SPEC, ANY_SPEC],
        [jax.ShapeDtypeStruct((T, QKVW), BF16)], [ANY_SPEC],
        [SLAB_QKV, SLAB_OUT, SLAB_QKV, ACC_KV, ACC_KV, pltpu.SemaphoreType.DMA((4,)), SLAB_T,
         pltpu.VMEM((A_SLOTS_IN, NC, HB, BQ), BF16), pltpu.SemaphoreType.DMA((A_SLOTS_IN,)), PAIR_DIAG], {}, job)
    return dqkv, extra


def _flash_fwd(qkv, mixed, g, fox, bias, job=None):
    def body(ins, outs, scr, comm):
        if fox:
            qkv_hbm, cq_ref, ckb_hbm, _ = ins
            (o_hbm, lse_ref, o32_hbm), (qkv_v, o_v, sem, vt_v, o32_v, ckb_v) = outs, scr
        else:
            qkv_hbm, tbl_ref, _ = ins
            (o_hbm, lse_ref), (qkv_v, o_v, sem, vt_v) = outs, scr
        sems = sem
        sem = sems.at[0]
        later = [pltpu.make_async_copy(ckb_hbm, ckb_v, sems.at[1])] if fox else []
        for cp in later:
            cp.start()
        _copy_in(qkv_hbm.at[:, pl.ds(g * 2 * PAIRW, 2 * PAIRW)], qkv_v, sem)
        st = comm[0]() if comm else None
        _transpose_slab(qkv_v, vt_v, lambda p: p * PAIRW + 2 * BQ)
        for cp in later:
            cp.wait()
        m0, m1 = _lane_masks()
        r0, r1 = _row_masks()
        valid = _valid_t(False)

        def steps(qts, cqs, i, j, cs, diag):
            ks = [_stack(_kv_tiles(qkv_v, p, b, j)[0], m0, m1) for p, b in CHAINS]
            zs = [_dot(ks[c], qts[c]) for c in range(NC)]
            prs, alphas, out = [], [], []
            for c, (p, b) in enumerate(CHAINS):
                (ma, mb), (la, lb_), _ = cs[c]
                if fox:
                    kk = pl.ds(pl.multiple_of(j * BQ, BQ), BQ)
                    col = b * NH + 2 * p
                    z = zs[c] + (cqs[c] - jnp.concatenate([ckb_v[col, kk, :], ckb_v[col + 1, kk, :]], axis=0))
                    if diag:
                        z = jnp.where(valid, z, NEG)
                else:
                    z = zs[c] + tbl_ref[p, i - j]
                za, zb = _heads(z)
                na = jnp.maximum(ma, jnp.max(za, axis=0, keepdims=True))
                nb = jnp.maximum(mb, jnp.max(zb, axis=0, keepdims=True))
                aa, ab = jnp.exp(ma - na), jnp.exp(mb - nb)
                pr = jnp.exp(z - _bcast_heads(na, nb))
                sa, sb = _colsum2(pr)
                prs.append(_split2(pr) if fox else pr.astype(BF16))
                alphas.append((aa, ab))
                out.append(((na, nb), (aa * la + sa, ab * lb_ + sb)))
            pvs = []
            for c, (p, b) in enumerate(CHAINS):
                vts = _stack_t(vt_v[p, :, _rows(b, j)], r0, r1)
                if fox:
                    pvs.append(_dot(vts, prs[c][:, :BQ]) + _dot(vts, prs[c][:, BQ:]))
                else:
                    pvs.append(_dot(vts, prs[c]))
            return tuple((out[c][0], out[c][1], _by_channel(*alphas[c]) * cs[c][2] + pvs[c]) for c in range(NC))

        def qblock(i, _):
            qts = [_tr(_q_tile(qkv_v, p, b, i)) for p, b in CHAINS]
            if fox:
                cqs = [_bcast_heads(_stat_row(cq_ref, p, b, 0, i), _stat_row(cq_ref, p, b, 1, i)) for p, b in CHAINS]
            else:
                cqs = [None] * NC
            ng = jnp.full((1, BQ), NEG, F32)
            zr = jnp.zeros((1, BQ), F32)
            cs = steps(qts, cqs, i, i, (((ng, ng), (zr, zr), jnp.zeros((BQ, BQ), F32)),) * NC, True)
            cs = _blocks_loop(i, FLASH_PER_TRIP, lambda k, cs: steps(qts, cqs, i, i - 1 - k, cs, False), cs)
            for c, (p, b) in enumerate(CHAINS):
                (ma, mb), (la, lb_), acc = cs[c]
                o = (acc / _by_channel(la, lb_)).T
                o_v[_rows(b, i), p * BQ:(p + 1) * BQ] = o.astype(BF16)
                if fox:
                    o32_v[_rows(b, i), p * BQ:(p + 1) * BQ] = o
                _put_row(lse_ref, p, b, 0, i, ma + jnp.log(la))
                _put_row(lse_ref, p, b, 1, i, mb + jnp.log(lb_))
            return 0

        lax.fori_loop(0, NB, qblock, 0)
        _copy_in(o_v, o_hbm.at[:, pl.ds(g * 2 * BQ, 2 * BQ)], sem)
        if fox:
            _copy_in(o32_v, o32_hbm, sem)
        if comm:
            comm[1](st)

    bias_specs = [VMEM_SPEC, ANY_SPEC] if fox else [VMEM_SPEC]
    n_in = 2 + len(bias_specs)
    o32 = [jax.ShapeDtypeStruct((T, 2 * BQ), F32)] if fox else []
    res, extra = _host_call(
        body, "fox_fwd" if fox else "dil_fwd", [qkv, *bias, mixed], [ANY_SPEC] + bias_specs + [ANY_SPEC],
        [jax.ShapeDtypeStruct((T, D), BF16), ROWS_SHAPE] + o32, [ANY_SPEC, VMEM_SPEC] + [ANY_SPEC] * len(o32),
        [SLAB_QKV, SLAB_OUT, pltpu.SemaphoreType.DMA((4,)), SLAB_T] + ([SLAB_O32, SLAB_KEYB] if fox else []),
        {n_in - 1: 0}, job)
    return (*res, extra)


def _flash_bwd(qkv, o, dmixed, lse, dqkv, g, fox, bias, job=None):
    def body(ins, outs, scr, comm):
        if fox:
            qkv_hbm, o_hbm, do_hbm, lse_ref, cq_ref, ckb_hbm, _ = ins
            (dqkv_hbm, db_ref), (qkv_v, o_v, do_v, dq_v, dk_s, dv_s, sem, kt_v, w_v, ckb_v, dc_s) = outs, scr
        else:
            qkv_hbm, o_hbm, do_hbm, lse_ref, tbl_ref, _ = ins
            (dqkv_hbm, db_ref), (qkv_v, o_v, do_v, dq_v, dk_s, dv_s, sem, kt_v, w_v) = outs, scr
        w_v[...] = jnp.zeros_like(w_v)
        sems = sem
        sem = sems.at[0]
        later = [pltpu.make_async_copy(do_hbm.at[:, pl.ds(g * 2 * BQ, 2 * BQ)], do_v, sems.at[1])]
        if fox:
            later += [pltpu.make_async_copy(o_hbm, o_v, sems.at[2]), pltpu.make_async_copy(ckb_hbm, ckb_v, sems.at[3])]
        else:
            later += [pltpu.make_async_copy(o_hbm.at[:, pl.ds(g * 2 * BQ, 2 * BQ)], o_v, sems.at[2])]
        for cp in later:
            cp.start()
        _copy_in(qkv_hbm.at[:, pl.ds(g * 2 * PAIRW, 2 * PAIRW)], qkv_v, sem)
        st = comm[0]() if comm else None
        _transpose_slab(qkv_v, kt_v, lambda p: p * PAIRW + BQ)
        for cp in later:
            cp.wait()
        m0, m1 = _lane_masks()
        first = _iota((BQ, BQ), 1) < 64
        r0, r1 = _row_masks()
        valid = _valid_t(False)
        dk_s[...] = jnp.zeros_like(dk_s)
        dv_s[...] = jnp.zeros_like(dv_s)
        if fox:
            dc_s[...] = jnp.zeros_like(dc_s)
        else:
            db_ref[...] = jnp.zeros_like(db_ref)

        def steps(cqs, lses, deltas, i, j, dqs, diag):
            kv = [_kv_tiles(qkv_v, p, b, j) for p, b in CHAINS]
            zd = [_dot(jnp.concatenate([_stack(kv[c][0], m0, m1), _stack(kv[c][1], m0, m1)], axis=1), w_v[c, 0])
                  for c in range(NC)]
            zs = [x[:, :BQ] for x in zd]
            dps = [x[:, BQ:] for x in zd]
            prs, dzl = [], []
            for c, (p, b) in enumerate(CHAINS):
                if fox:
                    kk = pl.ds(pl.multiple_of(j * BQ, BQ), BQ)
                    col = b * NH + 2 * p
                    z = zs[c] + (cqs[c] - jnp.concatenate([ckb_v[col, kk, :], ckb_v[col + 1, kk, :]], axis=0))
                    if diag:
                        z = jnp.where(valid, z, NEG)
                else:
                    z = zs[c] + tbl_ref[p, i - j]
                pr = jnp.exp(z - lses[c])
                prs.append(pr.astype(BF16))
                dzl.append(pr * (dps[c] - deltas[c]))
            dzbs = [dz.astype(BF16) for dz in dzl]
            new = []
            for c, (p, b) in enumerate(CHAINS):
                new.append(dqs[c] + _dot(_stack_t(kt_v[p, :, _rows(b, j)], r0, r1), dzbs[c]))
                kd = _dot(jnp.concatenate([dzbs[c], prs[c]], axis=1), w_v[c, 1])
                dk_s[p, _rows(b, j), :] += _merge_heads(kd[:, :BQ], first)
                dv_s[p, _rows(b, j), :] += _merge_heads(kd[:, BQ:], first)
                if fox:
                    dc_s[c, pl.ds(pl.multiple_of(j * HB, HB), HB), :] += dzl[c]
            if not fox:
                for p in range(2):
                    db_ref[p, i - j] = db_ref[p, i - j] + (dzl[2 * p] + dzl[2 * p + 1])
            return tuple(new)

        def qblock(i, _):
            qns = [_q_tile(qkv_v, p, b, i) for p, b in CHAINS]
            dns = [do_v[_rows(b, i), p * BQ:(p + 1) * BQ] for p, b in CHAINS]
            for c in range(NC):
                for r, (x, y) in enumerate(((_tr(qns[c]), _tr(dns[c])), (qns[c], dns[c]))):
                    w_v[c, r, :BQ, :BQ] = x
                    w_v[c, r, BQ:, BQ:] = y
            lses = [_bcast_heads(_stat_row(lse_ref, p, b, 0, i), _stat_row(lse_ref, p, b, 1, i)) for p, b in CHAINS]
            if fox:
                cqs = [_bcast_heads(_stat_row(cq_ref, p, b, 0, i), _stat_row(cq_ref, p, b, 1, i)) for p, b in CHAINS]
            else:
                cqs = [None] * NC
            deltas = []
            for c, (p, b) in enumerate(CHAINS):
                pt = (dns[c].astype(F32) * o_v[_rows(b, i), p * BQ:(p + 1) * BQ].astype(F32)).T
                deltas.append(_bcast_heads(jnp.sum(pt[:64], axis=0, keepdims=True), jnp.sum(pt[64:], axis=0, keepdims=True)))
            dqs = (jnp.zeros((BQ, BQ), F32),) * NC
            dqs = _blocks_loop(i, FLASH_PER_TRIP, lambda j, d: steps(cqs, lses, deltas, i, j, d, False), dqs)
            dqs = steps(cqs, lses, deltas, i, i, dqs, True)
            for c, (p, b) in enumerate(CHAINS):
                dq_v[_rows(b, i), p * PAIRW:p * PAIRW + BQ] = (dqs[c].T * SCALE).astype(BF16)
            return 0

        lax.fori_loop(0, NB, qblock, 0)
        for p in range(2):
            dq_v[:, p * PAIRW + BQ:p * PAIRW + 2 * BQ] = dk_s[p].astype(BF16)
            dq_v[:, p * PAIRW + 2 * BQ:p * PAIRW + 3 * BQ] = dv_s[p].astype(BF16)
        _copy_in(dq_v, dqkv_hbm.at[:, pl.ds(g * 2 * PAIRW, 2 * PAIRW)], sem)
        if fox:
            lane = _iota((BQ, NSTAT), 1)

            def fold(n, _):
                t = jnp.zeros((BQ, NSTAT), F32)
                for c, (p, b) in enumerate(CHAINS):
                    s = jnp.sum(dc_s[c, pl.ds(pl.multiple_of(n * HB, HB), HB), :], axis=1, keepdims=True)
                    col = b * NH + 2 * p
                    t = t - jnp.where(lane == col, s[:BQ], 0.0) - jnp.where(lane == col + 1, s[BQ:], 0.0)
                db_ref[pl.ds(pl.multiple_of(n * BQ, BQ), BQ), :] = t
                return 0

            lax.fori_loop(0, NB, fold, 0)
        if comm:
            comm[1](st)

    if fox:
        bias_specs = [VMEM_SPEC, ANY_SPEC]
        db_shape = jax.ShapeDtypeStruct((S, NSTAT), F32)
        more = [SLAB_KEYB, pltpu.VMEM((NC, NB * HB, BQ), F32)]
    else:
        bias_specs = [VMEM_SPEC]
        db_shape = jax.ShapeDtypeStruct((2, NB, HB, BQ), F32)
        more = []
    n_in = 5 + len(bias_specs)
    (dqkv, db), extra = _host_call(
        body, "fox_bwd" if fox else "dil_bwd", [qkv, o, dmixed, lse, *bias, dqkv],
        [ANY_SPEC, ANY_SPEC, ANY_SPEC, VMEM_SPEC] + bias_specs + [ANY_SPEC],
        [jax.ShapeDtypeStruct((T, QKVW), BF16), db_shape], [ANY_SPEC, VMEM_SPEC],
        [SLAB_QKV, SLAB_O32 if fox else SLAB_OUT, SLAB_OUT, SLAB_QKV, ACC_KV, ACC_KV, pltpu.SemaphoreType.DMA((4,)), SLAB_T,
         PAIR_DIAG] + more, {n_in - 1: 0}, job)
    return dqkv, db, extra


def _delta_t(d):
    return d * BQ + _iota((HB, BQ), 1) - (_iota((HB, BQ), 0) & (BQ - 1))


def _buckets_in(d):
    lo, hi = max(d * BQ - (BQ - 1), 0), d * BQ + BQ - 1
    return [b for b in range(32) if BUCKET_TH[b] <= hi and (b == 31 or BUCKET_TH[b + 1] > lo)]


def _in_bucket(delta, b):
    m = delta >= BUCKET_TH[b]
    return m if b == 31 else m & (delta < BUCKET_TH[b + 1])


def _dil_table(rel_bias, job=None):
    def body(ins, outs, scr, comm):
        (rb_ref,), (o_ref,) = ins, outs
        st = comm[0]() if comm else None
        for d in range(NB):
            delta = _delta_t(d)
            pos = delta >= 0
            n = ((pos & (delta <= 128)).astype(jnp.int32)
                 + (pos & (delta <= 512) & ((delta & 3) == 0)).astype(jnp.int32)
                 + (pos & ((delta & 15) == 0)).astype(jnp.int32))
            logn = jnp.where(n == 3, math.log(3.0), jnp.where(n == 2, math.log(2.0), jnp.where(n == 1, 0.0, NEG)))
            head1 = _iota((HB, BQ), 0) >= BQ
            for p in range(2):
                val = jnp.zeros((HB, BQ), F32)
                for b in _buckets_in(d):
                    val = jnp.where(_in_bucket(delta, b), jnp.where(head1, rb_ref[b, 2 * p + 1], rb_ref[b, 2 * p]), val)
                o_ref[p, d] = val + logn
        if comm:
            comm[1](st)

    (tbl,), extra = _host_call(
        body, "dil_table", [rel_bias], [pl.BlockSpec(memory_space=pltpu.SMEM)],
        [jax.ShapeDtypeStruct((2, NB, HB, BQ), F32)], [VMEM_SPEC], [], {}, job)
    return (tbl, extra) if job else tbl


def _dil_table_bwd(dtbl):
    def body(dt_ref, o_ref):
        p = pl.program_id(0)
        rowi = _iota((32, BQ), 0)
        lanei = _iota((32, BQ), 1)

        @pl.when(p == 0)
        def _():
            o_ref[...] = jnp.zeros_like(o_ref)

        out = jnp.zeros((32, BQ), F32)
        for b in range(32):
            acc = None
            for d in range(NB):
                if b in _buckets_in(d):
                    t = jnp.where(_in_bucket(_delta_t(d), b), dt_ref[d], 0.0)
                    acc = t if acc is None else acc + t
            rs = jnp.sum(acc, axis=1, keepdims=True)
            s0 = jnp.sum(rs[:BQ], axis=0, keepdims=True)
            s1 = jnp.sum(rs[BQ:], axis=0, keepdims=True)
            out = (out + jnp.where((rowi == b) & (lanei == 2 * p), s0, 0.0)
                   + jnp.where((rowi == b) & (lanei == 2 * p + 1), s1, 0.0))
        o_ref[...] += out

    return pl.pallas_call(
        body, name="dil_table_bwd", grid=(2,),
        in_specs=[pl.BlockSpec((None, NB, HB, BQ), lambda p: (p, 0, 0, 0))],
        out_specs=pl.BlockSpec((32, BQ), lambda p: (0, 0)),
        out_shape=jax.ShapeDtypeStruct((32, BQ), F32),
        compiler_params=_cp(("arbitrary",)))(dtbl)


def _fox_prep(gate, fb):
    def body(g_ref, fb_ref, c_ref):
        tri = (_iota((BQ, BQ), 0) >= _iota((BQ, BQ), 1)).astype(BF16)

        def blk(i, carry):
            r0 = pl.multiple_of(i * BQ, BQ)
            lf = _log_sigmoid(g_ref[pl.ds(r0, BQ), :] + fb_ref[...])
            c = _dot(tri, _split3(lf))
            c_ref[pl.ds(r0, BQ), :] = c[:, 0:BQ] + c[:, BQ:2 * BQ] + c[:, 2 * BQ:3 * BQ] + carry
            return carry + jnp.sum(lf, axis=0, keepdims=True)

        lax.fori_loop(0, NB, blk, jnp.zeros((1, BQ), F32))

    blk = pl.BlockSpec((S, GATEW), lambda b: (b, 0))
    return pl.pallas_call(
        body, name="fox_prep", grid=(BL,), in_specs=[blk, pl.BlockSpec((1, GATEW), lambda b: (0, 0))],
        out_specs=blk, out_shape=jax.ShapeDtypeStruct((T, GATEW), F32),
        compiler_params=_cp(("parallel",)))(gate, fb)


def _fox_post(dcum, gate, fb):
    def body(dc_ref, g_ref, fb_ref, dg_ref, dfb_ref):
        b = pl.program_id(0)
        tri = (_iota((BQ, BQ), 0) <= _iota((BQ, BQ), 1)).astype(BF16)

        def blk(ii, carry):
            csum, dfb = carry
            r0 = pl.multiple_of((NB - 1 - ii) * BQ, BQ)
            dc = dc_ref[pl.ds(r0, BQ), :]
            c = _dot(tri, _split3(dc))
            dlf = c[:, 0:BQ] + c[:, BQ:2 * BQ] + c[:, 2 * BQ:3 * BQ] + csum
            dg = dlf * jnp.exp(_log_sigmoid(-(g_ref[pl.ds(r0, BQ), :] + fb_ref[...])))
            dg_ref[pl.ds(r0, BQ), :] = dg
            return csum + jnp.sum(dc, axis=0, keepdims=True), dfb + jnp.sum(dg, axis=0, keepdims=True)

        z = jnp.zeros((1, BQ), F32)
        _, dfb = lax.fori_loop(0, NB, blk, (z, z))

        @pl.when(b == 0)
        def _():
            dfb_ref[...] = dfb

        @pl.when(b > 0)
        def _():
            dfb_ref[...] += dfb

    blk = pl.BlockSpec((S, GATEW), lambda b: (b, 0))
    vec = pl.BlockSpec((1, GATEW), lambda b: (0, 0))
    return pl.pallas_call(
        body, name="fox_post", grid=(BL,), in_specs=[blk, blk, vec], out_specs=[blk, vec],
        out_shape=[jax.ShapeDtypeStruct((T, GATEW), F32), jax.ShapeDtypeStruct((1, GATEW), F32)],
        compiler_params=_cp(("arbitrary",)))(dcum, gate, fb)


def _shift_down(x, n):
    return jnp.where(_iota(x.shape, 0) >= n, pltpu.roll(x, n, 0), 0.0)


def _shift_up(x, n):
    return jnp.where(_iota(x.shape, 0) < S - n, pltpu.roll(x, S - n, 0), 0.0)


def _conv_fwd(conv, cw, mixed):
    W = 256

    def body(c_ref, w_ref, _, o_ref):
        u = c_ref[:, W:2 * W] * c_ref[:, 2 * W:3 * W]
        y = w_ref[0:1, :] * _shift_down(u, 2) + w_ref[1:2, :] * _shift_down(u, 1) + w_ref[2:3, :] * u
        o_ref[...] = (c_ref[:, 0:W] * y).astype(BF16)

    return pl.pallas_call(
        body, name="conv_fwd", grid=(BL,),
        in_specs=[pl.BlockSpec((S, CONVW), lambda b: (b, 0)), pl.BlockSpec((8, W), lambda b: (0, 0)), ANY_SPEC],
        out_specs=pl.BlockSpec((S, W), lambda b: (b, 3)),
        out_shape=jax.ShapeDtypeStruct((T, D), BF16), input_output_aliases={2: 0},
        compiler_params=_cp(("parallel",)))(conv, cw, mixed)


def _conv_bwd(conv, cw, dmixed):
    W = 256

    def body(c_ref, w_ref, do_ref, dc_ref, dw_ref):
        b = pl.program_id(0)
        bg = c_ref[:, 0:W]
        cg = c_ref[:, W:2 * W]
        hv = c_ref[:, 2 * W:3 * W]
        do = do_ref[...].astype(F32)
        u = cg * hv
        u1 = _shift_down(u, 1)
        u2 = _shift_down(u, 2)
        y = w_ref[0:1, :] * u2 + w_ref[1:2, :] * u1 + w_ref[2:3, :] * u
        dy = do * bg
        du = w_ref[2:3, :] * dy + w_ref[1:2, :] * _shift_up(dy, 1) + w_ref[0:1, :] * _shift_up(dy, 2)
        dc_ref[:, 0:W] = (do * y).astype(BF16)
        dc_ref[:, W:2 * W] = (du * hv).astype(BF16)
        dc_ref[:, 2 * W:3 * W] = (du * cg).astype(BF16)
        rowi = _iota((8, W), 0)
        dw = (jnp.where(rowi == 0, jnp.sum(dy * u2, axis=0, keepdims=True), 0.0)
              + jnp.where(rowi == 1, jnp.sum(dy * u1, axis=0, keepdims=True), 0.0)
              + jnp.where(rowi == 2, jnp.sum(dy * u, axis=0, keepdims=True), 0.0))

        @pl.when(b == 0)
        def _():
            dw_ref[...] = dw

        @pl.when(b > 0)
        def _():
            dw_ref[...] += dw

    return pl.pallas_call(
        body, name="conv_bwd", grid=(BL,),
        in_specs=[pl.BlockSpec((S, CONVW), lambda b: (b, 0)), pl.BlockSpec((8, W), lambda b: (0, 0)),
                  pl.BlockSpec((S, W), lambda b: (b, 3))],
        out_specs=[pl.BlockSpec((S, CONVW), lambda b: (b, 0)), pl.BlockSpec((8, W), lambda b: (0, 0))],
        out_shape=[jax.ShapeDtypeStruct((T, CONVW), BF16), jax.ShapeDtypeStruct((8, W), F32)],
        compiler_params=_cp(("arbitrary",)))(conv, cw, dmixed)


def _place():
    x, y, c = lax.axis_index("x"), lax.axis_index("y"), lax.axis_index("c")
    return x, y, c


def _chips_of(x, y):
    return [(1 - x, y), (x, 1 - y), (1 - x, 1 - y)]


def _dev(p):
    return 4 * p[0] + 2 * p[1] + p[2]


def _gather_job_a(shards):
    n = len(shards)

    def peers(x, y, c):
        return [(x, y, 1 - c)] + [(*chip, c) for chip in _chips_of(x, y)]

    def start(ins, outs, sems):
        send, recv, loc = sems
        x, y, c = _place()
        me = (x, y, c)
        cps = []
        for a in range(n):
            cps.append(pltpu.make_async_copy(ins[a], outs[a].at[_dev(me)], loc.at[a]))
            for k, peer in enumerate(peers(x, y, c)):
                cps.append(pltpu.make_async_remote_copy(
                    src_ref=ins[a], dst_ref=outs[a].at[_dev(me)], send_sem=send.at[a, k], recv_sem=recv.at[a, k],
                    device_id=peer, device_id_type=MESH))
        for cp in cps:
            cp.start()
        return cps

    def finish(cps, ins, outs, sems):
        send, recv, loc = sems
        x, y, c = _place()
        for a in range(n):
            for k, peer in enumerate(peers(x, y, c)):
                pltpu.make_async_remote_copy(
                    src_ref=ins[a], dst_ref=outs[a].at[_dev(peer)], send_sem=send.at[a, k], recv_sem=recv.at[a, k],
                    device_id=(x, y, c), device_id_type=MESH).wait_recv()
        for a in range(n):
            cps[5 * a].wait()
            for k in range(4):
                cps[5 * a + 1 + k].wait_send()

    return _Job(shards, [jax.ShapeDtypeStruct((NDEV,) + s.shape, s.dtype) for s in shards], {},
                [pltpu.SemaphoreType.DMA((n, 4)), pltpu.SemaphoreType.DMA((n, 4)), pltpu.SemaphoreType.DMA((n,))],
                start, finish)


def _gather_job_b(gathered):
    n = len(gathered)

    def start(ins, outs, sems):
        send, recv = sems
        x, y, c = _place()
        cps = []
        for a in range(n):
            for j, chip in enumerate(_chips_of(x, y)):
                blk = outs[a].at[_dev((*chip, c))]
                cps.append(pltpu.make_async_remote_copy(
                    src_ref=blk, dst_ref=blk, send_sem=send.at[a, j], recv_sem=recv.at[a, j],
                    device_id=(x, y, 1 - c), device_id_type=MESH))
        for cp in cps:
            cp.start()
        return cps

    def finish(cps, ins, outs, sems):
        send, recv = sems
        x, y, c = _place()
        for a in range(n):
            for j, chip in enumerate(_chips_of(x, y)):
                blk = outs[a].at[_dev((*chip, 1 - c))]
                pltpu.make_async_remote_copy(
                    src_ref=blk, dst_ref=blk, send_sem=send.at[a, j], recv_sem=recv.at[a, j],
                    device_id=(x, y, c), device_id_type=MESH).wait_recv()
        for cp in cps:
            cp.wait_send()

    return _Job(gathered, [jax.ShapeDtypeStruct(g.shape, g.dtype) for g in gathered], {a: a for a in range(n)},
                [pltpu.SemaphoreType.DMA((n, 3)), pltpu.SemaphoreType.DMA((n, 3))], start, finish)


def _sibling_job(grads):
    n = len(grads)

    def start(ins, outs, sems):
        send, recv = sems
        x, y, c = _place()
        cps = [pltpu.make_async_remote_copy(
            src_ref=ins[a].at[:, 1 - c], dst_ref=outs[a], send_sem=send.at[a], recv_sem=recv.at[a],
            device_id=(x, y, 1 - c), device_id_type=MESH) for a in range(n)]
        for cp in cps:
            cp.start()
        return cps

    def finish(cps, ins, outs, sems):
        for cp in cps:
            cp.wait()

    return _Job(grads, [jax.ShapeDtypeStruct(g.shape[:1] + g.shape[2:], F32) for g in grads], {},
                [pltpu.SemaphoreType.DMA((n,)), pltpu.SemaphoreType.DMA((n,))], start, finish)


def _chip_job(psums):
    n = len(psums)

    def copies(ins, outs, sems):
        send, recv, loc = sems
        x, y, c = _place()
        mychip = 2 * x + y
        cps = []
        for a in range(n):
            cps.append(pltpu.make_async_copy(ins[a].at[mychip], outs[a].at[mychip], loc.at[a]))
            for j, chip in enumerate(_chips_of(x, y)):
                cps.append(pltpu.make_async_remote_copy(
                    src_ref=ins[a].at[2 * chip[0] + chip[1]], dst_ref=outs[a].at[mychip],
                    send_sem=send.at[a, j], recv_sem=recv.at[a, j], device_id=(*chip, c), device_id_type=MESH))
        return cps

    def start(ins, outs, sems):
        for cp in copies(ins, outs, sems):
            cp.start()

    def finish(_, ins, outs, sems):
        cps = copies(ins, outs, sems)
        send, recv, loc = sems
        x, y, c = _place()
        mychip = 2 * x + y
        for a in range(n):
            for j, chip in enumerate(_chips_of(x, y)):
                pltpu.make_async_remote_copy(
                    src_ref=ins[a].at[mychip], dst_ref=outs[a].at[2 * chip[0] + chip[1]],
                    send_sem=send.at[a, j], recv_sem=recv.at[a, j], device_id=(x, y, c), device_id_type=MESH).wait_recv()
        for a in range(n):
            cps[4 * a].wait()
            for j in range(3):
                cps[4 * a + 1 + j].wait_send()

    return _Job(psums, [jax.ShapeDtypeStruct(p.shape, BF16) for p in psums], {},
                [pltpu.SemaphoreType.DMA((n, 3)), pltpu.SemaphoreType.DMA((n, 3)), pltpu.SemaphoreType.DMA((n,))],
                start, finish)


def _join_jobs(*jobs):
    jobs = [j for j in jobs if j is not None]
    if len(jobs) <= 1:
        return jobs[0] if jobs else None
    cut = lambda seq, sizes: [seq[sum(sizes[:k]):sum(sizes[:k + 1])] for k in range(len(sizes))]
    n_in = [len(j.ins) for j in jobs]
    n_out = [len(j.out_shapes) for j in jobs]
    n_sem = [len(j.sems) for j in jobs]
    aliases = {}
    for k, j in enumerate(jobs):
        for a, b in j.aliases.items():
            aliases[sum(n_in[:k]) + a] = sum(n_out[:k]) + b

    def start(ins, outs, sems):
        return [j.start(i, o, s) for j, i, o, s in zip(jobs, cut(ins, n_in), cut(outs, n_out), cut(sems, n_sem))]

    def finish(sts, ins, outs, sems):
        for j, st, i, o, s in zip(jobs, sts, cut(ins, n_in), cut(outs, n_out), cut(sems, n_sem)):
            j.finish(st, i, o, s)

    return _Job([t for j in jobs for t in j.ins], [t for j in jobs for t in j.out_shapes], aliases,
                [t for j in jobs for t in j.sems], start, finish)


def _run_job(job, name):
    def body(ins, outs, scr, comm):
        comm[1](comm[0]())

    return _host_call(body, name, [], [], [], [], [], {}, job)[1]


def _allreduce_small(v, job=None):
    def body(ins, outs, scr, comm):
        (v_ref,), (o_ref,), (slots, send_sems, recv_sems) = ins, outs, scr
        st = comm[0]() if comm else None
        x, y, c = _place()
        me = 4 * x + 2 * y + c
        slots[me] = v_ref[...]

        def copy(k):
            peer = (x ^ ((k >> 2) & 1), y ^ ((k >> 1) & 1), c ^ (k & 1))
            return pltpu.make_async_remote_copy(
                src_ref=v_ref, dst_ref=slots.at[me], send_sem=send_sems.at[k - 1], recv_sem=recv_sems.at[k - 1],
                device_id=peer, device_id_type=MESH)

        def arrival(k):
            return pltpu.make_async_remote_copy(
                src_ref=v_ref, dst_ref=slots.at[me ^ k], send_sem=send_sems.at[k - 1], recv_sem=recv_sems.at[k - 1],
                device_id=(x, y, c), device_id_type=MESH)

        sends = [copy(k) for k in range(1, NDEV)]
        for cp in sends:
            cp.start()
        for k in range(1, NDEV):
            arrival(k).wait_recv()
        for cp in sends:
            cp.wait_send()
        acc = slots[0]
        for d in range(1, NDEV):
            acc = acc + slots[d]
        o_ref[...] = acc
        if comm:
            comm[1](st)

    (out,), extra = _host_call(
        body, "allreduce_small", [v], [VMEM_SPEC], [jax.ShapeDtypeStruct(v.shape, F32)], [VMEM_SPEC],
        [pltpu.VMEM((NDEV,) + v.shape, F32), pltpu.SemaphoreType.DMA((NDEV - 1,)),
         pltpu.SemaphoreType.DMA((NDEV - 1,))], {}, job)
    return (out, extra) if job else out


def _pair_sums(views, gots, core):
    n = len(views)

    def body(c_ref, *refs):
        for a in range(n):
            refs[2 * n + a][...] = (refs[a][...] + refs[n + a][...]).astype(BF16)

    def vspec(v):
        return pl.BlockSpec((None, None, v.shape[2] // 2, v.shape[3]), lambda k, h, c: (k, c[0], h, 0))

    def gspec(g):
        return pl.BlockSpec((None, g.shape[1] // 2, g.shape[2]), lambda k, h, c: (k, h, 0))

    return pl.pallas_call(
        body, name="pair_sums",
        grid_spec=pltpu.PrefetchScalarGridSpec(
            num_scalar_prefetch=1, grid=(4, 2),
            in_specs=[vspec(v) for v in views] + [gspec(g) for g in gots],
            out_specs=[gspec(g) for g in gots]),
        out_shape=[jax.ShapeDtypeStruct(g.shape, BF16) for g in gots],
        compiler_params=_cp(("parallel", "parallel")))(core, *views, *gots)


def _chip_sums(parts):
    n = len(parts)

    def body(*refs):
        for a in range(n):
            acc = refs[a][0].astype(F32)
            for k in range(1, 4):
                acc = acc + refs[a][k].astype(F32)
            refs[n + a][...] = acc

    return pl.pallas_call(
        body, name="chip_sums", in_specs=[VMEM_SPEC] * n, out_specs=[VMEM_SPEC] * n,
        out_shape=[jax.ShapeDtypeStruct(p.shape[1:], F32) for p in parts], compiler_params=_cp())(*parts)


def _permute_in(w):
    lead = w.shape[:-1]
    return w.reshape(lead + (3, 3, 2, BQ)).swapaxes(-2, -3).reshape(lead + (QKVW,))


def _unpermute_in(w):
    lead = w.shape[:-1]
    return w.reshape(lead + (3, 2, 3, BQ)).swapaxes(-2, -3).reshape(lead + (QKVW,))


def _row(v):
    v = v.reshape(-1)
    return jnp.pad(v, (0, D - v.shape[0])).reshape(1, D)


def kernel(x, w_in, f_bias, conv_w, w_out, rel_bias, ln1_g, ln1_b, w_gate, w_up, w_down, ln2_g, ln2_b, loss_target, m_w_in, m_f_bias, m_conv_w, m_w_out, m_rel_bias, m_ln1_g, m_ln1_b, m_w_gate, m_w_up, m_w_down, m_ln2_g, m_ln2_b, v_w_in, v_f_bias, v_conv_w, v_w_out, v_rel_bias, v_ln1_g, v_ln1_b, v_w_gate, v_w_up, v_w_down, v_ln2_g, v_ln2_b):
    xi, yi, ci = _place()
    me = 4 * xi + 2 * yi + ci
    core = jnp.reshape(ci, (1,)).astype(jnp.int32)

    win_s = jnp.concatenate([_permute_in(w_in[..., :QKVW]), w_in[..., QKVW:]], axis=-1)
    win_s = jnp.pad(win_s, ((0, 0), (0, 0), (0, NPAD - NPROJ))).astype(BF16)
    per_layer = [win_s, w_out.astype(BF16), jnp.swapaxes(w_gate, 1, 2).astype(BF16),
                 jnp.swapaxes(w_up, 1, 2).astype(BF16), w_down.astype(BF16)]
    sh = [[s[l] for s in per_layer] for l in range(2)]

    def whole(g):
        return g.reshape(NDEV * g.shape[1], g.shape[2])

    cw_rows = lax.dynamic_update_slice(jnp.zeros((2, 3, 256), F32), conv_w, (0, 0, me * 32))
    small = jnp.concatenate([_row(cw_rows[0]), _row(cw_rows[1]), jnp.zeros((SMALL_ROWS - 2, D), F32)], axis=0)
    small, leg_a = _allreduce_small(small, job=_gather_job_a(sh[0][:1]))
    cw_full = small[0:2, :CONVW].reshape(2, 3, 256)
    cw8 = jnp.pad(cw_full, ((0, 0), (0, 5), (0, 0)))
    fb = jnp.pad(f_bias, ((0, 0), (0, GATEW - NH))).reshape(2, 1, GATEW)
    tbl, leg_b = _dil_table(rel_bias, job=_gather_job_b(list(leg_a)))
    W = [{"win": whole(leg_b[0])}, {}]

    def wrow(tn, K, blk=0):
        return pl.BlockSpec((tn, K), lambda i, j: (j, blk))

    def arow(tm, K, blk=0):
        return pl.BlockSpec((tm, K), lambda i, j: (i, blk))

    h = x.reshape(T, D)
    hb = h.astype(BF16)
    saved = []
    for l in range(2):
        Win = W[l]["win"]
        qkv, conv, gate = _proj(hb, Win)
        cum = _fox_prep(gate, fb[l])
        cq = cum[:, :NH].reshape(BL, S, NH).transpose(0, 2, 1).reshape(NSTAT, S)
        ckb = jnp.broadcast_to(cq[:, :, None], (NSTAT, S, BQ))
        if l == 0:
            mixed, amat, a0 = _sb_fwd(qkv, job=_gather_job_a(sh[0][1:]))
            mixed, lse_d, ex = _flash_fwd(qkv, mixed, 1, False, (tbl,),
                                          job=_join_jobs(_gather_job_b(list(a0)), _gather_job_a(sh[1][:2])))
            W[0].update(zip(("wout", "wgT", "wuT", "wd"), [whole(t) for t in ex[:4]]))
            mixed, lse_f, o_fox, ex = _flash_fwd(qkv, mixed, 2, True, (cq, ckb), job=_gather_job_b(list(ex[4:])))
            W[1].update(zip(("win", "wout"), [whole(t) for t in ex[:2]]))
        else:
            mixed, amat, a2 = _sb_fwd(qkv, job=_gather_job_a(sh[1][2:]))
            mixed, lse_d, ex = _flash_fwd(qkv, mixed, 1, False, (tbl,), job=_gather_job_b(list(a2)))
            W[1].update(zip(("wgT", "wuT", "wd"), [whole(t) for t in ex]))
            mixed, lse_f, o_fox, _ = _flash_fwd(qkv, mixed, 2, True, (cq, ckb))
        Wout, WgT, WuT, Wd = W[l]["wout"], W[l]["wgT"], W[l]["wuT"], W[l]["wd"]
        mixed = _conv_fwd(conv, cw8[l], mixed)
        x1, xh1, r1, x1b = _mm_ln(mixed, Wout, h, ln1_g[l:l + 1], ln1_b[l:l + 1], "out_proj_ln")
        fs, ft, a, x2, xh2, r2, x2b = _ffn_fwd(x1b, x1, WgT, WuT, Wd, ln2_g[l:l + 1], ln2_b[l:l + 1])
        saved.append(dict(h=hb, qkv=qkv, conv=conv, gate=gate, cq=cq, ckb=ckb, mixed=mixed, amat=amat, lse_d=lse_d,
                          lse_f=lse_f, o_fox=o_fox, x1=x1b, xh1=xh1, r1=r1, fs=fs, ft=ft, a=a, xh2=xh2, r2=r2))
        h, hb = x2, x2b

    dy = h

    def view(gr):
        return gr.reshape(4, 2, gr.shape[0] // NDEV, gr.shape[1])

    G = [None, None]
    small_g = {}
    shard_g = {}
    for l in (1, 0):
        sv = saved[l]
        Win, Wout, WgT, WuT, Wd = W[l]["win"], W[l]["wout"], W[l]["wgT"], W[l]["wuT"], W[l]["wd"]
        res = _ffn_bwd(dy, sv["xh2"], sv["r2"], ln2_g[l:l + 1], sv["fs"], sv["ft"], Wd, WgT, WuT,
                       target=loss_target.reshape(T, D) if l == 1 else None)
        dgt, dut, ds2b, dx1, dg2, db2 = res[:6]
        if l == 1:
            sq = res[6]
        G_d = _mm_tn(sv["a"], ds2b, None, C=D, Ka=DFF, N=D, tm=256, tn=1024, tk=T, ooff=0, name="grad_w_down")
        G_g = _mm_tn(dgt, sv["x1"], None, C=D, Ka=DFF, N=D, tm=256, tn=1024, tk=T, ooff=0, name="grad_w_gate")
        G_u = _mm_tn(dut, sv["x1"], None, C=D, Ka=DFF, N=D, tm=256, tn=1024, tk=T, ooff=0, name="grad_w_up")
        ds1, dg1, db1, ds1b, dmixed = _ln_bwd(dx1, sv["xh1"], sv["r1"], ln1_g[l:l + 1], Wout)
        G_out = _mm_tn(sv["mixed"], ds1b, None, C=D, Ka=D, N=D, tm=256, tn=1024, tk=T, ooff=0, name="grad_w_out")
        early = [view(t) for t in (G_g, G_u, G_d, G_out)] + ([view(G[1]["in"])] if l == 0 else [])
        dqkv, gots = _sb_bwd(sv["qkv"], dmixed, sv["amat"], job=_sibling_job(early))
        ps = _pair_sums(early, list(gots), core)
        dqkv, dtbl, pa = _flash_bwd(sv["qkv"], sv["mixed"], dmixed, sv["lse_d"], dqkv, 1, False, (tbl,),
                                    job=_chip_job(ps[:2]))
        dqkv, dck, pb = _flash_bwd(sv["qkv"], sv["o_fox"], dmixed, sv["lse_f"], dqkv, 2, True,
                                   (sv["cq"], sv["ckb"]), job=_chip_job(ps[2:]))
        sums = _chip_sums(list(pa) + list(pb))
        shard_g[l] = dict(zip(("g", "u", "d", "out"), sums[:4]))
        if l == 0:
            shard_g[1]["in"] = sums[4]
        dconv, dcw = _conv_bwd(sv["conv"], cw8[l], dmixed)
        dcum = jnp.pad(dck.reshape(S, BL, NH).transpose(1, 0, 2).reshape(T, NH), ((0, 0), (0, GATEW - NH)))
        dgate, dfb = _fox_post(dcum, sv["gate"], fb[l])
        drb = _dil_table_bwd(dtbl)
        G_in = _mm_tn(sv["h"], dqkv, None, C=NPAD, Ka=D, N=QKVW, tm=512, tn=768, tk=T, ooff=0, name="grad_w_in_qkv")
        G_in = _mm_tn(sv["h"], dconv, G_in, C=NPAD, Ka=D, N=CONVW, tm=256, tn=768, tk=T, ooff=3,
                      name="grad_w_in_conv")
        G_in = _mm_tn(sv["h"], dgate, G_in, C=NPAD, Ka=D, N=GATEW, tm=1024, tn=128, tk=1024, ooff=24,
                      name="grad_w_in_gate")
        G[l] = {"in": G_in, "out": G_out, "g": G_g, "u": G_u, "d": G_d}
        if l == 0:
            late = [view(G_in)]
            tail = _chip_job(_pair_sums(late, list(_run_job(_sibling_job(late), "sibling_exchange")), core))
            dy, parts = _mm([(dqkv, arow(1024, QKVW), Win, wrow(512, QKVW, 0)),
                             (dconv, arow(1024, CONVW), Win, wrow(512, CONVW, 3)),
                             (dgate, arow(1024, GATEW), Win, wrow(512, GATEW, 24))],
                            nt=True, M=T, N=D, tm=1024, tn=512, out_dtype=F32, name="proj_dx", res=ds1,
                            res_scale=ALPHA, job=tail)
            shard_g[0]["in"] = _chip_sums(list(parts))[0]
        else:
            dy = _proj_bwd(dqkv, dconv, dgate, Win, ds1)
        small_g[l] = dict(ln1_g=dg1, ln1_b=db1, ln2_g=dg2, ln2_b=db2, cw=dcw[0:3].reshape(1, CONVW),
                          fb=dfb[:, :NH], rb=drb[:, :NH])
    grad_x = dy.reshape(BL, S, D)

    rows = []
    for name in ("ln1_g", "ln1_b", "ln2_g", "ln2_b"):
        rows += [small_g[0][name], small_g[1][name]]
    rows += [_row(small_g[0]["cw"]), _row(small_g[1]["cw"]),
             _row(jnp.concatenate([small_g[0]["fb"], small_g[1]["fb"]], axis=0)),
             _row(small_g[0]["rb"] + small_g[1]["rb"]), _row(sq)]
    rows.append(jnp.zeros((SMALL_ROWS - len(rows), D), F32))
    sg = _allreduce_small(jnp.concatenate(rows, axis=0))
    loss = sg[12, 0] * (0.5 / D)
    g_ln1_g, g_ln1_b, g_ln2_g, g_ln2_b = sg[0:2], sg[2:4], sg[4:6], sg[6:8]
    g_conv_full = sg[8:10, :CONVW].reshape(2, 3, 256)
    g_conv = lax.dynamic_slice(g_conv_full, (0, 0, me * 32), (2, 3, 32))
    g_fb = sg[10, :2 * NH].reshape(2, NH)
    g_rb = sg[11, :32 * NH].reshape(32, NH)

    def both(name):
        return jnp.stack([shard_g[0][name], shard_g[1][name]])

    g_in = both("in")
    g_w_in = jnp.concatenate([_unpermute_in(g_in[..., :QKVW]), g_in[..., QKVW:NPROJ]], axis=-1)
    g_w_out = both("out")
    g_w_gate = jnp.swapaxes(both("g"), 1, 2)
    g_w_up = jnp.swapaxes(both("u"), 1, 2)
    g_w_down = both("d")

    up_in = _adamw(w_in, g_w_in, m_w_in, v_w_in, 64)
    up_out = _adamw(w_out, g_w_out, m_w_out, v_w_out, 128)
    up_gate = _adamw(w_gate, g_w_gate, m_w_gate, v_w_gate, 256)
    up_up = _adamw(w_up, g_w_up, m_w_up, v_w_up, 256)
    up_down = _adamw(w_down, g_w_down, m_w_down, v_w_down, 352)

    def pack(fbv, cwv, rbv, l1g, l1b, l2g, l2b):
        r = [l1g, l1b, l2g, l2b, _row(cwv), _row(fbv), _row(rbv)]
        r.append(jnp.zeros((SMALL_ROWS - 11, D), F32))
        return jnp.concatenate(r, axis=0)

    pw = pack(f_bias, conv_w, rel_bias, ln1_g, ln1_b, ln2_g, ln2_b)
    pg = pack(g_fb, g_conv, g_rb, g_ln1_g, g_ln1_b, g_ln2_g, g_ln2_b)
    pm = pack(m_f_bias, m_conv_w, m_rel_bias, m_ln1_g, m_ln1_b, m_ln2_g, m_ln2_b)
    pv = pack(v_f_bias, v_conv_w, v_rel_bias, v_ln1_g, v_ln1_b, v_ln2_g, v_ln2_b)
    ups = [u[0] for u in _adamw(pw[None], pg[None], pm[None], pv[None], SMALL_ROWS)]

    def unpack(p):
        return dict(ln1_g=p[0:2], ln1_b=p[2:4], ln2_g=p[4:6], ln2_b=p[6:8],
                    conv_w=p[8, :192].reshape(2, 3, 32), f_bias=p[9, :2 * NH].reshape(2, NH),
                    rel_bias=p[10, :32 * NH].reshape(32, NH))

    sm = [unpack(p) for p in ups]

    def group(k):
        return (up_in[k], sm[k]["f_bias"], sm[k]["conv_w"], up_out[k], sm[k]["rel_bias"], sm[k]["ln1_g"],
                sm[k]["ln1_b"], up_gate[k], up_up[k], up_down[k], sm[k]["ln2_g"], sm[k]["ln2_b"])

    grads = (g_w_in, g_fb, g_conv, g_w_out, g_rb, g_ln1_g, g_ln1_b, g_w_gate, g_w_up, g_w_down, g_ln2_g, g_ln2_b)
    return (loss, grad_x) + grads + group(0) + group(1) + group(2)
```
